```python
import math
import jax, jax.numpy as jnp
from jax import lax
import numpy as np

D_MODEL = 1024
BATCH = 16
SEQ = 2048
DEPTH = 1

D_MIX = D_MODEL
D_SSM = D_MIX // 2
D_ATTN = D_MIX // 2
SSM_GROUP_CH = 16
SSM_GROUPS = D_SSM // SSM_GROUP_CH
SSM_STATE = 64
HEAD_DIM = 64
N_HEADS = D_ATTN // HEAD_DIM
KV_HEADS = 2
Q_PER_KV = N_HEADS // KV_HEADS
WINDOW = 128
BLOCK = 128
D_PLE = 256
EPS = 1e-6

SPLIT_SIZES = (D_SSM, D_SSM, N_HEADS * HEAD_DIM, KV_HEADS * HEAD_DIM, KV_HEADS * HEAD_DIM, D_ATTN)
SPLIT_IDX = tuple(int(s) for s in np.cumsum(SPLIT_SIZES)[:-1])
D_IN = sum(SPLIT_SIZES)

kernel_name = "hymba_s5_swa_sink_alibi_layer"


def rms_norm(x, g):
    xf = x.astype(jnp.float32)
    y = xf * lax.rsqrt(jnp.mean(xf * xf, axis=-1, keepdims=True) + EPS)
    return (y * g.astype(jnp.float32)).astype(x.dtype)


def alibi_slopes(n_heads):
    return jnp.exp2(-8.0 * (jnp.arange(n_heads, dtype=jnp.float32) + 1.0) / n_heads)


def s5_branch(u, lam_re, lam_im, log_step, b_re, b_im, c_re, c_im, d, w_glu, b_glu):
    f32 = jnp.float32
    bsz, seq, _ = u.shape
    uf = u.astype(f32).reshape(bsz, seq, SSM_GROUPS, SSM_GROUP_CH)
    lam = lax.complex(lam_re.astype(f32), lam_im.astype(f32))
    step = jnp.exp(log_step.astype(f32))[:, None]
    lam_bar = jnp.exp(lam * step)
    b = lax.complex(b_re.astype(f32), b_im.astype(f32))
    b_bar = ((lam_bar - 1.0) / lam)[..., None] * b
    bu = jnp.einsum('blgp,gnp->blgn', uf.astype(jnp.complex64), b_bar)
    a = jnp.broadcast_to(lam_bar[None, None], (1, seq, SSM_GROUPS, SSM_STATE))

    def combine(left, right):
        a_l, b_l = left
        a_r, b_r = right
        return a_r * a_l, a_r * b_l + b_r

    _, states = lax.associative_scan(combine, (a, bu), axis=1)
    c = lax.complex(c_re.astype(f32), c_im.astype(f32))
    y = jnp.einsum('blgn,gpn->blgp', states, c).real \
        + d.astype(f32).reshape(SSM_GROUPS, SSM_GROUP_CH) * uf
    y = y.reshape(bsz, seq, D_SSM)
    g = jax.nn.gelu(y)
    out = g * jax.nn.sigmoid(g @ w_glu.astype(f32) + b_glu.astype(f32))
    return out.astype(u.dtype)


def swa_branch(q, k, v, sinks):
    f32 = jnp.float32
    bsz, seq = q.shape[:2]
    nb = seq // BLOCK
    qb = q.reshape(bsz, nb, BLOCK, KV_HEADS, Q_PER_KV, HEAD_DIM)

    def band(t):
        tb = t.reshape(bsz, nb, BLOCK, KV_HEADS, HEAD_DIM)
        prev = jnp.concatenate([jnp.zeros_like(tb[:, :1]), tb[:, :-1]], axis=1)
        return jnp.concatenate([prev, tb], axis=2)

    kb, vb = band(k), band(v)
    scale = 1.0 / math.sqrt(HEAD_DIM)
    scores = jnp.einsum('bnqkgd,bnskd->bnkgqs', qb, kb, preferred_element_type=f32) * scale
    q_idx = jnp.arange(BLOCK)[:, None]
    s_idx = jnp.arange(2 * BLOCK)[None, :]
    dist = q_idx + BLOCK - s_idx
    valid = (dist >= 0) & (dist < WINDOW)
    block_ok = (jnp.arange(nb)[:, None] > 0) | (jnp.arange(2 * BLOCK)[None, :] >= BLOCK)
    mask = valid[None, :, :] & block_ok[:, None, :]
    slopes = alibi_slopes(N_HEADS).reshape(KV_HEADS, Q_PER_KV)
    bias = -slopes[:, :, None, None] * dist.astype(f32)[None, None]
    scores = jnp.where(mask[None, :, None, None], scores + bias[None, None], -jnp.inf)
    sink = sinks.astype(f32).reshape(KV_HEADS, Q_PER_KV)[None, None, :, :, None, None]
    m = jnp.maximum(jnp.max(scores, axis=-1, keepdims=True), sink)
    e = jnp.exp(scores - m)
    probs = e / (jnp.sum(e, axis=-1, keepdims=True) + jnp.exp(sink - m))
    out = jnp.einsum('bnkgqs,bnskd->bnqkgd', probs.astype(v.dtype), vb)
    return out.reshape(bsz, seq, N_HEADS * HEAD_DIM)


def _fwd_setup_inputs(seed: int = 0) -> dict:
    key = jax.random.key(seed)
    ks = jax.random.split(key, 24)
    f32 = jnp.float32
    nrm = lambda k, shape, s: jax.random.normal(k, shape, f32) * s
    x = jax.random.normal(ks[0], (BATCH, SEQ, D_MODEL), f32)
    p = jax.random.normal(ks[1], (DEPTH, BATCH, SEQ, D_PLE), f32)
    pre_norm_g = 1.0 + nrm(ks[2], (DEPTH, D_MODEL), 0.02)
    w_in = nrm(ks[3], (DEPTH, D_MODEL, D_IN), D_MODEL ** -0.5)
    n = jnp.arange(SSM_STATE, dtype=f32)
    ssm_lam_re = -0.5 * jnp.exp(nrm(ks[4], (DEPTH, SSM_GROUPS, SSM_STATE), 0.05))
    ssm_lam_im = jnp.pi * n[None, None, :] + nrm(ks[5], (DEPTH, SSM_GROUPS, SSM_STATE), 0.01)
    ssm_log_step = jax.random.uniform(ks[6], (DEPTH, SSM_GROUPS), f32, math.log(1e-3), math.log(1e-1))
    bs = (2.0 * SSM_GROUP_CH) ** -0.5
    ssm_b_re = nrm(ks[7], (DEPTH, SSM_GROUPS, SSM_STATE, SSM_GROUP_CH), bs)
    ssm_b_im = nrm(ks[8], (DEPTH, SSM_GROUPS, SSM_STATE, SSM_GROUP_CH), bs)
    cs = (2.0 * SSM_STATE) ** -0.5
    ssm_c_re = nrm(ks[9], (DEPTH, SSM_GROUPS, SSM_GROUP_CH, SSM_STATE), cs)
    ssm_c_im = nrm(ks[10], (DEPTH, SSM_GROUPS, SSM_GROUP_CH, SSM_STATE), cs)
    ssm_d = nrm(ks[11], (DEPTH, D_SSM), 1.0)
    ssm_w_glu = nrm(ks[12], (DEPTH, D_SSM, D_SSM), D_SSM ** -0.5)
    ssm_b_glu = nrm(ks[13], (DEPTH, D_SSM), 0.01)
    attn_sinks = nrm(ks[14], (DEPTH, N_HEADS), 1.0)
    w_out = nrm(ks[15], (DEPTH, D_MIX, D_MODEL), D_MIX ** -0.5)
    post_norm_g = 1.0 + nrm(ks[16], (DEPTH, D_MODEL), 0.02)
    pl_w_proj = nrm(ks[17], (DEPTH, D_PLE, D_MODEL), D_PLE ** -0.5)
    pl_w_gate = nrm(ks[18], (DEPTH, D_MODEL, D_MODEL), D_MODEL ** -0.5)
    pl_b_gate = nrm(ks[19], (DEPTH, D_MODEL), 0.01)
    return {"x": x, "p": p, "pre_norm_g": pre_norm_g, "w_in": w_in,
            "ssm_lam_re": ssm_lam_re, "ssm_lam_im": ssm_lam_im, "ssm_log_step": ssm_log_step,
            "ssm_b_re": ssm_b_re, "ssm_b_im": ssm_b_im, "ssm_c_re": ssm_c_re, "ssm_c_im": ssm_c_im,
            "ssm_d": ssm_d, "ssm_w_glu": ssm_w_glu, "ssm_b_glu": ssm_b_glu,
            "attn_sinks": attn_sinks, "w_out": w_out, "post_norm_g": post_norm_g,
            "pl_w_proj": pl_w_proj, "pl_w_gate": pl_w_gate, "pl_b_gate": pl_b_gate}


def _fwd_reference(x, p, pre_norm_g, w_in, ssm_lam_re, ssm_lam_im, ssm_log_step, ssm_b_re, ssm_b_im,
              ssm_c_re, ssm_c_im, ssm_d, ssm_w_glu, ssm_b_glu, attn_sinks, w_out, post_norm_g,
              pl_w_proj, pl_w_gate, pl_b_gate):
    bsz, seq, _ = x.shape
    h = x
    for i in range(DEPTH):
        hn = rms_norm(h, pre_norm_g[i])
        proj = hn @ w_in[i]
        u_ssm, z_ssm, q, k, v, z_attn = jnp.split(proj, SPLIT_IDX, axis=-1)
        ssm_out = s5_branch(u_ssm, ssm_lam_re[i], ssm_lam_im[i], ssm_log_step[i],
                            ssm_b_re[i], ssm_b_im[i], ssm_c_re[i], ssm_c_im[i], ssm_d[i],
                            ssm_w_glu[i], ssm_b_glu[i]) * jax.nn.silu(z_ssm)
        attn_out = swa_branch(q.reshape(bsz, seq, N_HEADS, HEAD_DIM),
                              k.reshape(bsz, seq, KV_HEADS, HEAD_DIM),
                              v.reshape(bsz, seq, KV_HEADS, HEAD_DIM),
                              attn_sinks[i]) * jax.nn.silu(z_attn)
        mixed = jnp.concatenate([ssm_out, attn_out], axis=-1) @ w_out[i]
        h = h + rms_norm(mixed, post_norm_g[i])
        gate = jax.nn.sigmoid(h @ pl_w_gate[i] + pl_b_gate[i])
        h = h + gate * (p[i] @ pl_w_proj[i])
    return h


import jax as _jax
import jax.numpy as _jnp

TWIN_FORMAT = 'train_step'
FWD_PARAMS = ['x', 'p', 'pre_norm_g', 'w_in', 'ssm_lam_re', 'ssm_lam_im', 'ssm_log_step', 'ssm_b_re', 'ssm_b_im', 'ssm_c_re', 'ssm_c_im', 'ssm_d', 'ssm_w_glu', 'ssm_b_glu', 'attn_sinks', 'w_out', 'post_norm_g', 'pl_w_proj', 'pl_w_gate', 'pl_b_gate']
TWIN_WEIGHTS = ['pre_norm_g', 'w_in', 'ssm_lam_re', 'ssm_lam_im', 'ssm_log_step', 'ssm_b_re', 'ssm_b_im', 'ssm_c_re', 'ssm_c_im', 'ssm_d', 'ssm_w_glu', 'ssm_b_glu', 'attn_sinks', 'w_out', 'post_norm_g', 'pl_w_proj', 'pl_w_gate', 'pl_b_gate']
TWIN_DIFF_INPUT = 'x'
TWIN_INPUTS = ['x', 'p', 'pre_norm_g', 'w_in', 'ssm_lam_re', 'ssm_lam_im', 'ssm_log_step', 'ssm_b_re', 'ssm_b_im', 'ssm_c_re', 'ssm_c_im', 'ssm_d', 'ssm_w_glu', 'ssm_b_glu', 'attn_sinks', 'w_out', 'post_norm_g', 'pl_w_proj', 'pl_w_gate', 'pl_b_gate', 'loss_target', 'm_pre_norm_g', 'm_w_in', 'm_ssm_lam_re', 'm_ssm_lam_im', 'm_ssm_log_step', 'm_ssm_b_re', 'm_ssm_b_im', 'm_ssm_c_re', 'm_ssm_c_im', 'm_ssm_d', 'm_ssm_w_glu', 'm_ssm_b_glu', 'm_attn_sinks', 'm_w_out', 'm_post_norm_g', 'm_pl_w_proj', 'm_pl_w_gate', 'm_pl_b_gate', 'v_pre_norm_g', 'v_w_in', 'v_ssm_lam_re', 'v_ssm_lam_im', 'v_ssm_log_step', 'v_ssm_b_re', 'v_ssm_b_im', 'v_ssm_c_re', 'v_ssm_c_im', 'v_ssm_d', 'v_ssm_w_glu', 'v_ssm_b_glu', 'v_attn_sinks', 'v_w_out', 'v_post_norm_g', 'v_pl_w_proj', 'v_pl_w_gate', 'v_pl_b_gate']
TWIN_OUTPUTS = ['loss', 'grad_x', 'grad_pre_norm_g', 'grad_w_in', 'grad_ssm_lam_re', 'grad_ssm_lam_im', 'grad_ssm_log_step', 'grad_ssm_b_re', 'grad_ssm_b_im', 'grad_ssm_c_re', 'grad_ssm_c_im', 'grad_ssm_d', 'grad_ssm_w_glu', 'grad_ssm_b_glu', 'grad_attn_sinks', 'grad_w_out', 'grad_post_norm_g', 'grad_pl_w_proj', 'grad_pl_w_gate', 'grad_pl_b_gate', 'delta_pre_norm_g', 'delta_w_in', 'delta_ssm_lam_re', 'delta_ssm_lam_im', 'delta_ssm_log_step', 'delta_ssm_b_re', 'delta_ssm_b_im', 'delta_ssm_c_re', 'delta_ssm_c_im', 'delta_ssm_d', 'delta_ssm_w_glu', 'delta_ssm_b_glu', 'delta_attn_sinks', 'delta_w_out', 'delta_post_norm_g', 'delta_pl_w_proj', 'delta_pl_w_gate', 'delta_pl_b_gate', 'new_m_pre_norm_g', 'new_m_w_in', 'new_m_ssm_lam_re', 'new_m_ssm_lam_im', 'new_m_ssm_log_step', 'new_m_ssm_b_re', 'new_m_ssm_b_im', 'new_m_ssm_c_re', 'new_m_ssm_c_im', 'new_m_ssm_d', 'new_m_ssm_w_glu', 'new_m_ssm_b_glu', 'new_m_attn_sinks', 'new_m_w_out', 'new_m_post_norm_g', 'new_m_pl_w_proj', 'new_m_pl_w_gate', 'new_m_pl_b_gate', 'new_v_pre_norm_g', 'new_v_w_in', 'new_v_ssm_lam_re', 'new_v_ssm_lam_im', 'new_v_ssm_log_step', 'new_v_ssm_b_re', 'new_v_ssm_b_im', 'new_v_ssm_c_re', 'new_v_ssm_c_im', 'new_v_ssm_d', 'new_v_ssm_w_glu', 'new_v_ssm_b_glu', 'new_v_attn_sinks', 'new_v_w_out', 'new_v_post_norm_g', 'new_v_pl_w_proj', 'new_v_pl_w_gate', 'new_v_pl_b_gate']
TWIN_LEAF_KINDS = {'loss': 'loss', 'grad_x': 'grad_x', 'grad_pre_norm_g': 'grad_w', 'grad_w_in': 'grad_w', 'grad_ssm_lam_re': 'grad_w', 'grad_ssm_lam_im': 'grad_w', 'grad_ssm_log_step': 'grad_w', 'grad_ssm_b_re': 'grad_w', 'grad_ssm_b_im': 'grad_w', 'grad_ssm_c_re': 'grad_w', 'grad_ssm_c_im': 'grad_w', 'grad_ssm_d': 'grad_w', 'grad_ssm_w_glu': 'grad_w', 'grad_ssm_b_glu': 'grad_w', 'grad_attn_sinks': 'grad_w', 'grad_w_out': 'grad_w', 'grad_post_norm_g': 'grad_w', 'grad_pl_w_proj': 'grad_w', 'grad_pl_w_gate': 'grad_w', 'grad_pl_b_gate': 'grad_w', 'delta_pre_norm_g': 'delta_w', 'delta_w_in': 'delta_w', 'delta_ssm_lam_re': 'delta_w', 'delta_ssm_lam_im': 'delta_w', 'delta_ssm_log_step': 'delta_w', 'delta_ssm_b_re': 'delta_w', 'delta_ssm_b_im': 'delta_w', 'delta_ssm_c_re': 'delta_w', 'delta_ssm_c_im': 'delta_w', 'delta_ssm_d': 'delta_w', 'delta_ssm_w_glu': 'delta_w', 'delta_ssm_b_glu': 'delta_w', 'delta_attn_sinks': 'delta_w', 'delta_w_out': 'delta_w', 'delta_post_norm_g': 'delta_w', 'delta_pl_w_proj': 'delta_w', 'delta_pl_w_gate': 'delta_w', 'delta_pl_b_gate': 'delta_w', 'new_m_pre_norm_g': 'new_m', 'new_m_w_in': 'new_m', 'new_m_ssm_lam_re': 'new_m', 'new_m_ssm_lam_im': 'new_m', 'new_m_ssm_log_step': 'new_m', 'new_m_ssm_b_re': 'new_m', 'new_m_ssm_b_im': 'new_m', 'new_m_ssm_c_re': 'new_m', 'new_m_ssm_c_im': 'new_m', 'new_m_ssm_d': 'new_m', 'new_m_ssm_w_glu': 'new_m', 'new_m_ssm_b_glu': 'new_m', 'new_m_attn_sinks': 'new_m', 'new_m_w_out': 'new_m', 'new_m_post_norm_g': 'new_m', 'new_m_pl_w_proj': 'new_m', 'new_m_pl_w_gate': 'new_m', 'new_m_pl_b_gate': 'new_m', 'new_v_pre_norm_g': 'new_v', 'new_v_w_in': 'new_v', 'new_v_ssm_lam_re': 'new_v', 'new_v_ssm_lam_im': 'new_v', 'new_v_ssm_log_step': 'new_v', 'new_v_ssm_b_re': 'new_v', 'new_v_ssm_b_im': 'new_v', 'new_v_ssm_c_re': 'new_v', 'new_v_ssm_c_im': 'new_v', 'new_v_ssm_d': 'new_v', 'new_v_ssm_w_glu': 'new_v', 'new_v_ssm_b_glu': 'new_v', 'new_v_attn_sinks': 'new_v', 'new_v_w_out': 'new_v', 'new_v_post_norm_g': 'new_v', 'new_v_pl_w_proj': 'new_v', 'new_v_pl_w_gate': 'new_v', 'new_v_pl_b_gate': 'new_v'}


def _forward(args):
    return _fwd_reference(*[args[k] for k in FWD_PARAMS])


def _output_shape():
    out = _jax.eval_shape(lambda: _forward(_fwd_setup_inputs(0)))
    return out.shape, out.dtype

N_MICROBATCH = 1
ADAM_LR = 0.001
ADAM_B1 = 0.9
ADAM_B2 = 0.999
ADAM_EPS = 1e-08
ADAM_WD = 0.01
ADAM_STEP = 10
PER_EXAMPLE_BATCH_AXIS = {'x': 0, 'p': 1, 'loss_target': 0}
SHARED_INPUTS = []
_WEIGHT_DTYPES = {'pre_norm_g': _jnp.float32, 'w_in': _jnp.float32, 'ssm_lam_re': _jnp.float32, 'ssm_lam_im': _jnp.float32, 'ssm_log_step': _jnp.float32, 'ssm_b_re': _jnp.float32, 'ssm_b_im': _jnp.float32, 'ssm_c_re': _jnp.float32, 'ssm_c_im': _jnp.float32, 'ssm_d': _jnp.float32, 'ssm_w_glu': _jnp.float32, 'ssm_b_glu': _jnp.float32, 'attn_sinks': _jnp.float32, 'w_out': _jnp.float32, 'post_norm_g': _jnp.float32, 'pl_w_proj': _jnp.float32, 'pl_w_gate': _jnp.float32, 'pl_b_gate': _jnp.float32}
MOMENT_SCALE = {'pre_norm_g': 5.123730e-01, 'w_in': 3.338665e-01, 'ssm_lam_re': 1.935828e-02, 'ssm_lam_im': 1.839759e-02, 'ssm_log_step': 7.020334e+00, 'ssm_b_re': 1.150667e-02, 'ssm_b_im': 1.106353e-02, 'ssm_c_re': 2.374008e-02, 'ssm_c_im': 2.267345e-02, 'ssm_d': 4.823042e-01, 'ssm_w_glu': 1.017194e-01, 'ssm_b_glu': 2.215555e-01, 'attn_sinks': 2.034784e-01, 'w_out': 4.031186e-01, 'post_norm_g': 3.276898e+01, 'pl_w_proj': 4.571320e-01, 'pl_w_gate': 2.568819e-01, 'pl_b_gate': 2.907205e+00}


def _to_microbatches(a, axis):
    t = _jnp.moveaxis(a, axis, 0)
    t = t.reshape((N_MICROBATCH, t.shape[0] // N_MICROBATCH) + t.shape[1:])
    return _jnp.moveaxis(t, 1, axis + 1)


def setup_inputs(seed: int = 0) -> dict:
    inp = _fwd_setup_inputs(seed)
    key = _jax.random.fold_in(_jax.random.key(seed), 7919)
    shape, _ = _output_shape()
    out = dict(inp)
    out["loss_target"] = _jax.random.normal(_jax.random.fold_in(key, 0), shape, _jnp.float32)
    for i, name in enumerate(TWIN_WEIGHTS):
        w = inp[name].astype(_jnp.float32)
        if MOMENT_SCALE is None:
            s = _jnp.sqrt(_jnp.mean(_jnp.square(w)) + 1e-30)
        else:
            s = MOMENT_SCALE[name]
        km, kv = _jax.random.split(_jax.random.fold_in(key, i + 1))
        out[name] = w
        out["m_" + name] = s * _jax.random.normal(km, w.shape, _jnp.float32)
        out["v_" + name] = (s * s) * _jax.random.uniform(kv, w.shape, _jnp.float32, 0.5, 1.5)
    if N_MICROBATCH > 1:
        for name, axis in PER_EXAMPLE_BATCH_AXIS.items():
            out[name] = _to_microbatches(out[name], axis)
    return {'x': out['x'], 'p': out['p'], 'pre_norm_g': out['pre_norm_g'], 'w_in': out['w_in'], 'ssm_lam_re': out['ssm_lam_re'], 'ssm_lam_im': out['ssm_lam_im'], 'ssm_log_step': out['ssm_log_step'], 'ssm_b_re': out['ssm_b_re'], 'ssm_b_im': out['ssm_b_im'], 'ssm_c_re': out['ssm_c_re'], 'ssm_c_im': out['ssm_c_im'], 'ssm_d': out['ssm_d'], 'ssm_w_glu': out['ssm_w_glu'], 'ssm_b_glu': out['ssm_b_glu'], 'attn_sinks': out['attn_sinks'], 'w_out': out['w_out'], 'post_norm_g': out['post_norm_g'], 'pl_w_proj': out['pl_w_proj'], 'pl_w_gate': out['pl_w_gate'], 'pl_b_gate': out['pl_b_gate'], 'loss_target': out['loss_target'], 'm_pre_norm_g': out['m_pre_norm_g'], 'm_w_in': out['m_w_in'], 'm_ssm_lam_re': out['m_ssm_lam_re'], 'm_ssm_lam_im': out['m_ssm_lam_im'], 'm_ssm_log_step': out['m_ssm_log_step'], 'm_ssm_b_re': out['m_ssm_b_re'], 'm_ssm_b_im': out['m_ssm_b_im'], 'm_ssm_c_re': out['m_ssm_c_re'], 'm_ssm_c_im': out['m_ssm_c_im'], 'm_ssm_d': out['m_ssm_d'], 'm_ssm_w_glu': out['m_ssm_w_glu'], 'm_ssm_b_glu': out['m_ssm_b_glu'], 'm_attn_sinks': out['m_attn_sinks'], 'm_w_out': out['m_w_out'], 'm_post_norm_g': out['m_post_norm_g'], 'm_pl_w_proj': out['m_pl_w_proj'], 'm_pl_w_gate': out['m_pl_w_gate'], 'm_pl_b_gate': out['m_pl_b_gate'], 'v_pre_norm_g': out['v_pre_norm_g'], 'v_w_in': out['v_w_in'], 'v_ssm_lam_re': out['v_ssm_lam_re'], 'v_ssm_lam_im': out['v_ssm_lam_im'], 'v_ssm_log_step': out['v_ssm_log_step'], 'v_ssm_b_re': out['v_ssm_b_re'], 'v_ssm_b_im': out['v_ssm_b_im'], 'v_ssm_c_re': out['v_ssm_c_re'], 'v_ssm_c_im': out['v_ssm_c_im'], 'v_ssm_d': out['v_ssm_d'], 'v_ssm_w_glu': out['v_ssm_w_glu'], 'v_ssm_b_glu': out['v_ssm_b_glu'], 'v_attn_sinks': out['v_attn_sinks'], 'v_w_out': out['v_w_out'], 'v_post_norm_g': out['v_post_norm_g'], 'v_pl_w_proj': out['v_pl_w_proj'], 'v_pl_w_gate': out['v_pl_w_gate'], 'v_pl_b_gate': out['v_pl_b_gate']}


def _loss(weights, diff, rest, loss_target):
    with _jax.named_scope("forward"):
        args = {**rest, TWIN_DIFF_INPUT: diff, **{k: w.astype(_WEIGHT_DTYPES[k]) for k, w in weights.items()}}
        y = _forward(args)
    with _jax.named_scope("loss_head"):
        err = _jnp.square(y.astype(_jnp.float32) - loss_target)
        return 0.5 * _jnp.sum(_jnp.mean(err, axis=-1)) if err.ndim else 0.5 * err


def _adamw(w, g, m, v):
    m = ADAM_B1 * m + (1.0 - ADAM_B1) * g
    v = ADAM_B2 * v + (1.0 - ADAM_B2) * _jnp.square(g)
    m_hat = m / (1.0 - ADAM_B1 ** ADAM_STEP)
    v_hat = v / (1.0 - ADAM_B2 ** ADAM_STEP)
    delta = -ADAM_LR * (m_hat / (_jnp.sqrt(v_hat) + ADAM_EPS) + ADAM_WD * w)
    return delta, m, v


def reference(x, p, pre_norm_g, w_in, ssm_lam_re, ssm_lam_im, ssm_log_step, ssm_b_re, ssm_b_im, ssm_c_re, ssm_c_im, ssm_d, ssm_w_glu, ssm_b_glu, attn_sinks, w_out, post_norm_g, pl_w_proj, pl_w_gate, pl_b_gate, loss_target, m_pre_norm_g, m_w_in, m_ssm_lam_re, m_ssm_lam_im, m_ssm_log_step, m_ssm_b_re, m_ssm_b_im, m_ssm_c_re, m_ssm_c_im, m_ssm_d, m_ssm_w_glu, m_ssm_b_glu, m_attn_sinks, m_w_out, m_post_norm_g, m_pl_w_proj, m_pl_w_gate, m_pl_b_gate, v_pre_norm_g, v_w_in, v_ssm_lam_re, v_ssm_lam_im, v_ssm_log_step, v_ssm_b_re, v_ssm_b_im, v_ssm_c_re, v_ssm_c_im, v_ssm_d, v_ssm_w_glu, v_ssm_b_glu, v_attn_sinks, v_w_out, v_post_norm_g, v_pl_w_proj, v_pl_w_gate, v_pl_b_gate):
    given = dict(x=x, p=p, pre_norm_g=pre_norm_g, w_in=w_in, ssm_lam_re=ssm_lam_re, ssm_lam_im=ssm_lam_im, ssm_log_step=ssm_log_step, ssm_b_re=ssm_b_re, ssm_b_im=ssm_b_im, ssm_c_re=ssm_c_re, ssm_c_im=ssm_c_im, ssm_d=ssm_d, ssm_w_glu=ssm_w_glu, ssm_b_glu=ssm_b_glu, attn_sinks=attn_sinks, w_out=w_out, post_norm_g=post_norm_g, pl_w_proj=pl_w_proj, pl_w_gate=pl_w_gate, pl_b_gate=pl_b_gate, loss_target=loss_target, m_pre_norm_g=m_pre_norm_g, m_w_in=m_w_in, m_ssm_lam_re=m_ssm_lam_re, m_ssm_lam_im=m_ssm_lam_im, m_ssm_log_step=m_ssm_log_step, m_ssm_b_re=m_ssm_b_re, m_ssm_b_im=m_ssm_b_im, m_ssm_c_re=m_ssm_c_re, m_ssm_c_im=m_ssm_c_im, m_ssm_d=m_ssm_d, m_ssm_w_glu=m_ssm_w_glu, m_ssm_b_glu=m_ssm_b_glu, m_attn_sinks=m_attn_sinks, m_w_out=m_w_out, m_post_norm_g=m_post_norm_g, m_pl_w_proj=m_pl_w_proj, m_pl_w_gate=m_pl_w_gate, m_pl_b_gate=m_pl_b_gate, v_pre_norm_g=v_pre_norm_g, v_w_in=v_w_in, v_ssm_lam_re=v_ssm_lam_re, v_ssm_lam_im=v_ssm_lam_im, v_ssm_log_step=v_ssm_log_step, v_ssm_b_re=v_ssm_b_re, v_ssm_b_im=v_ssm_b_im, v_ssm_c_re=v_ssm_c_re, v_ssm_c_im=v_ssm_c_im, v_ssm_d=v_ssm_d, v_ssm_w_glu=v_ssm_w_glu, v_ssm_b_glu=v_ssm_b_glu, v_attn_sinks=v_attn_sinks, v_w_out=v_w_out, v_post_norm_g=v_post_norm_g, v_pl_w_proj=v_pl_w_proj, v_pl_w_gate=v_pl_w_gate, v_pl_b_gate=v_pl_b_gate)
    weights = {n: given[n] for n in TWIN_WEIGHTS}
    shared = {n: given[n] for n in SHARED_INPUTS}
    per_example = {n: given[n] for n in ['x', 'p']}
    grad_fn = _jax.value_and_grad(_loss, argnums=(0, 1))

    def one_microbatch(ex, loss_target):
        ex = dict(ex)
        diff = ex.pop(TWIN_DIFF_INPUT)
        return grad_fn(weights, diff, {**shared, **ex}, loss_target)

    if N_MICROBATCH == 1:
        loss, (grad_w, grad_x) = one_microbatch(per_example, given["loss_target"])
    else:
        def body(carry, xs):
            loss_sum, grad_sum = carry
            l_k, (gw_k, gx_k) = one_microbatch(xs[0], xs[1])
            with _jax.named_scope("update"):
                return (loss_sum + l_k, _jax.tree.map(_jnp.add, grad_sum, gw_k)), gx_k

        init = (_jnp.zeros((), _jnp.float32), _jax.tree.map(_jnp.zeros_like, weights))
        (loss, grad_w), grad_x = _jax.lax.scan(body, init, (per_example, given["loss_target"]))
    with _jax.named_scope("update"):
        delta_w, new_m, new_v = {}, {}, {}
        for n in TWIN_WEIGHTS:
            delta_w[n], new_m[n], new_v[n] = _adamw(weights[n], grad_w[n], given["m_" + n], given["v_" + n])
    return (loss, grad_x, *[grad_w[n] for n in TWIN_WEIGHTS], *[delta_w[n] for n in TWIN_WEIGHTS],
            *[new_m[n] for n in TWIN_WEIGHTS], *[new_v[n] for n in TWIN_WEIGHTS])
```

```python
import functools
import math

import jax
import jax.numpy as jnp
from jax import lax
from jax.experimental import pallas as pl
from jax.experimental.pallas import tpu as pltpu

F32 = jnp.float32
BF16 = jnp.bfloat16

D_MODEL = 1024
D_SSM = 512
D_ATTN = 512
SSM_P = 16
SSM_G = 32
SSM_N = 64
N_HEADS = 8
KV_HEADS = 2
Q_PER_KV = 4
HEAD_DIM = 64
ATT_BLOCK = 128
D_PLE = 256
D_IN = 2304
EPS = 1e-6
N_DEV = 8
N_SEG = 8
G_TILE = 8
N_GT = SSM_G // G_TILE
CH_T = G_TILE * SSM_P
ST_T = G_TILE * SSM_N
N_STATE = SSM_G * SSM_N
LANES = 128
VMEM_LIMIT = 60 * 1024 * 1024

ADAM_LR = 0.001
ADAM_B1 = 0.9
ADAM_B2 = 0.999
ADAM_EPS = 1e-08
ADAM_WD = 0.01
ADAM_STEP = 10

GELU_C = math.sqrt(2.0 / math.pi)
GELU_K = 0.044715
ATT_SCALE = 1.0 / math.sqrt(HEAD_DIM)
NEG_BIG = -1e30

BIG_SIZES = (D_MODEL * (D_IN // N_DEV), (D_MODEL // N_DEV) * D_MODEL, (D_MODEL // N_DEV) * D_MODEL,
             D_PLE * (D_MODEL // N_DEV), (D_SSM // N_DEV) * D_SSM)
BIG_ROWS = sum(BIG_SIZES) // LANES
SMALL_NAMES = ("pre_norm_g", "ssm_lam_re", "ssm_lam_im", "ssm_log_step", "ssm_b_re", "ssm_b_im", "ssm_c_re",
               "ssm_c_im", "ssm_d", "ssm_b_glu", "attn_sinks", "post_norm_g", "pl_b_gate")
SMALL_SIZES = (1024, 2048, 2048, 32, 32768, 32768, 32768, 32768, 512, 512, 8, 1024, 1024)
SMALL_ROWS = 144


def _mm(a, b):
    return jnp.dot(a.astype(BF16), b.astype(BF16), preferred_element_type=F32)


def _mm_nt(a, b):
    return lax.dot_general(a.astype(BF16), b.astype(BF16), (((1,), (1,)), ((), ())), preferred_element_type=F32)


def _mm_tn(a, b):
    return lax.dot_general(a.astype(BF16), b.astype(BF16), (((0,), (0,)), ((), ())), preferred_element_type=F32)


def _sigmoid(x):
    return 1.0 / (1.0 + jnp.exp(-x))


def _tc_params(sem):
    return pltpu.CompilerParams(dimension_semantics=sem, vmem_limit_bytes=VMEM_LIMIT)


def _const_spec(shape):
    nd = len(shape)
    return pl.BlockSpec(shape, lambda *_: (0,) * nd)


def _mesh_pos():
    return lax.axis_index("x"), lax.axis_index("y"), lax.axis_index("c")


ROW_CHUNK = 64


def _row_chunks(nrows, fn):
    def step(i, carry):
        fn(pl.ds(pl.multiple_of(i * ROW_CHUNK, ROW_CHUNK), ROW_CHUNK))
        return carry

    lax.fori_loop(0, nrows // ROW_CHUNK, step, 0)


def _allgather_weights(packed):
    rows = packed.shape[0]

    def body(src_ref, out_ref, send_sems, recv_sems):
        x, y, c = _mesh_pos()
        me, sibling = (x, y, c), (x, y, 1 - c)
        chips = [(1 - x, y), (x, 1 - y), (1 - x, 1 - y)]

        def blk(px, py, pc):
            return out_ref.at[pl.ds(pl.multiple_of((4 * px + 2 * py + pc) * rows, 16), rows), :]

        def copy(k, block, to):
            return pltpu.make_async_remote_copy(
                src_ref=blk(*block), dst_ref=blk(*block), send_sem=send_sems.at[k], recv_sem=recv_sems.at[k],
                device_id=to, device_id_type=pl.DeviceIdType.MESH)

        mine = blk(*me)

        def cast(r):
            mine[r, :] = src_ref[r, :].astype(BF16)

        _row_chunks(rows, cast)
        first = [copy(0, me, sibling)] + [copy(1 + j, me, (*chip, c)) for j, chip in enumerate(chips)]
        for cp in first:
            cp.start()
        passed = [copy(4 + j, (*chip, c), sibling) for j, chip in enumerate(chips)]
        for j, chip in enumerate(chips):
            copy(1 + j, (*chip, c), me).wait_recv()
            passed[j].start()
        copy(0, sibling, me).wait_recv()
        for j, chip in enumerate(chips):
            copy(4 + j, (*chip, 1 - c), me).wait_recv()
        for cp in first + passed:
            cp.wait_send()

    return pl.pallas_call(
        body, name="allgather_weights",
        out_shape=jax.ShapeDtypeStruct((N_DEV * rows, LANES), BF16),
        in_specs=[pl.BlockSpec(memory_space=pltpu.VMEM)],
        out_specs=pl.BlockSpec(memory_space=pltpu.VMEM),
        scratch_shapes=[pltpu.SemaphoreType.DMA((7,)), pltpu.SemaphoreType.DMA((7,))],
        compiler_params=pltpu.CompilerParams(vmem_limit_bytes=VMEM_LIMIT),
    )(packed)


def _adamw(w, g, m, v):
    m = ADAM_B1 * m + (1.0 - ADAM_B1) * g
    v = ADAM_B2 * v + (1.0 - ADAM_B2) * (g * g)
    m_hat = m / (1.0 - ADAM_B1 ** ADAM_STEP)
    v_hat = v / (1.0 - ADAM_B2 ** ADAM_STEP)
    delta = -ADAM_LR * (m_hat / (jnp.sqrt(v_hat) + ADAM_EPS) + ADAM_WD * w)
    return delta, m, v


def _reduce_update(gbig16, gown, gsmall, wbig, mbig, vbig, wsmall, msmall, vsmall):
    rb, rs = gown.shape[0], gsmall.shape[1]

    def body(gb_ref, go_ref, gs_ref, wb_ref, mb_ref, vb_ref, ws_ref, ms_ref, vs_ref,
             ob_ref, os_ref,
             sendb, recv1b, recv2b, recv1s, recv2s, part_s,
             s_send, s_recv):
        x, y, c = _mesh_pos()
        sibling = (x, y, 1 - c)
        chips = [(1 - x, y), (x, 1 - y), (1 - x, 1 - y)]
        all_chips = [(x, y)] + chips

        def slot(px, py, pc):
            return 4 * px + 2 * py + pc

        def rcopy(k, src, dst, to):
            return pltpu.make_async_remote_copy(src_ref=src, dst_ref=dst, send_sem=s_send.at[k], recv_sem=s_recv.at[k],
                                                device_id=to, device_id_type=pl.DeviceIdType.MESH)

        lvl1 = []
        for j, chip in enumerate(all_chips):
            sl = slot(*chip, 1 - c)
            lvl1.append(rcopy(j, gb_ref.at[sl], recv1b.at[j], sibling))
            lvl1.append(rcopy(4 + j, gs_ref.at[sl], recv1s.at[j], sibling))
        for cp in lvl1:
            cp.start()
        for cp in lvl1:
            cp.wait_recv()
        def partials(rows):
            ob_ref[0, rows, :] = go_ref[rows, :] + recv1b[0, rows, :].astype(F32)
            for j, chip in enumerate(chips):
                mine16 = gb_ref[slot(*chip, c), rows, :].astype(F32)
                sendb[j, rows, :] = (mine16 + recv1b[1 + j, rows, :].astype(F32)).astype(BF16)

        _row_chunks(rb, partials)
        lvl2 = []
        for j, chip in enumerate(all_chips):
            part_s[j] = gs_ref[slot(*chip, c)] + recv1s[j]
            if j > 0:
                lvl2.append(rcopy(8 + 2 * (j - 1), sendb.at[j - 1], recv2b.at[j - 1], (*chip, c)))
                lvl2.append(rcopy(9 + 2 * (j - 1), part_s.at[j], recv2s.at[j - 1], (*chip, c)))
        for cp in lvl2:
            cp.start()
        for cp in lvl2:
            cp.wait_recv()

        def update(rows):
            gb = ob_ref[0, rows, :]
            for j in range(3):
                gb = gb + recv2b[j, rows, :].astype(F32)
            db, nmb, nvb = _adamw(wb_ref[rows, :], gb, mb_ref[rows, :], vb_ref[rows, :])
            ob_ref[0, rows, :] = gb
            ob_ref[1, rows, :] = db
            ob_ref[2, rows, :] = nmb
            ob_ref[3, rows, :] = nvb

        _row_chunks(rb, update)
        gs = part_s[0]
        for j in range(3):
            gs = gs + recv2s[j]
        ds_, nms, nvs = _adamw(ws_ref[...], gs, ms_ref[...], vs_ref[...])
        mine = os_ref.at[slot(x, y, c)]
        mine[0] = gs
        mine[1] = ds_
        mine[2] = nms
        mine[3] = nvs
        peers = [sibling] + [(*chip, c) for chip in chips] + [(*chip, 1 - c) for chip in chips]
        lvl3 = [rcopy(14 + j, mine, mine, peer) for j, peer in enumerate(peers)]
        for cp in lvl3:
            cp.start()
        for cp in lvl3:
            cp.wait_recv()
        for cp in lvl1 + lvl2 + lvl3:
            cp.wait_send()

    vm = pl.BlockSpec(memory_space=pltpu.VMEM)
    return pl.pallas_call(
        body, name="reduce_update",
        out_shape=(jax.ShapeDtypeStruct((4, rb, LANES), F32), jax.ShapeDtypeStruct((N_DEV, 4, rs, LANES), F32)),
        in_specs=[vm] * 9, out_specs=(vm, vm),
        scratch_shapes=[pltpu.VMEM((3, rb, LANES), BF16), pltpu.VMEM((4, rb, LANES), BF16),
                        pltpu.VMEM((3, rb, LANES), BF16), pltpu.VMEM((4, rs, LANES), F32),
                        pltpu.VMEM((3, rs, LANES), F32), pltpu.VMEM((4, rs, LANES), F32),
                        pltpu.SemaphoreType.DMA((21,)), pltpu.SemaphoreType.DMA((21,))],
        compiler_params=pltpu.CompilerParams(vmem_limit_bytes=VMEM_LIMIT),
    )(gbig16, gown, gsmall, wbig, mbig, vbig, wsmall, msmall, vsmall)


def _in_proj(x2, g_pre, w_in, bl, seg):
    t = x2.shape[0]
    tm = seg

    def body(x_ref, g_ref, w_ref, u_ref, zs_ref, q_ref, k_ref, v_ref, za_ref):
        xv = x_ref[...]
        r = lax.rsqrt(jnp.mean(xv * xv, axis=-1, keepdims=True) + EPS)
        hn = xv * r * g_ref[...]
        proj = _mm(hn, w_ref[...])
        u_ref[0] = proj[:, 0:512]
        zs_ref[...] = proj[:, 512:1024]
        q_ref[...] = proj[:, 1024:1536].astype(BF16)
        k_ref[...] = proj[:, 1536:1664].astype(BF16)
        v_ref[...] = proj[:, 1664:1792].astype(BF16)
        za_ref[...] = proj[:, 1792:2304]

    row = lambda w: pl.BlockSpec((tm, w), lambda i: (i, 0))
    return pl.pallas_call(
        body, name="in_proj", grid=(t // tm,),
        in_specs=[row(D_MODEL), _const_spec((1, D_MODEL)), _const_spec((D_MODEL, D_IN))],
        out_specs=(pl.BlockSpec((1, tm, D_SSM), lambda i: (i // N_SEG, 0, i % N_SEG)),
                   row(512), row(512), row(128), row(128), row(512)),
        out_shape=(jax.ShapeDtypeStruct((bl, seg, N_SEG * D_SSM), F32),
                   jax.ShapeDtypeStruct((t, 512), F32), jax.ShapeDtypeStruct((t, 512), BF16),
                   jax.ShapeDtypeStruct((t, 128), BF16), jax.ShapeDtypeStruct((t, 128), BF16),
                   jax.ShapeDtypeStruct((t, 512), F32)),
        compiler_params=_tc_params(("arbitrary",)),
    )(x2, g_pre, w_in)


def _ssm_prep(lam_re, lam_im, log_step, b_re, b_im, seg):
    def body(lr_ref, li_ref, ls_ref, br_ref, bi_ref, lrr_ref, lir_ref, lsr_ref,
             ar_ref, ai_ref, bbr_ref, bbi_ref, pr_ref, pi_ref):
        lr, li = lr_ref[...], li_ref[...]
        step = jnp.exp(ls_ref[...])
        mag = jnp.exp(lr * step)
        ar = mag * jnp.cos(li * step)
        ai = mag * jnp.sin(li * step)
        ar_ref[...] = ar
        ai_ref[...] = ai
        den = lr * lr + li * li
        cr = ((ar - 1.0) * lr + ai * li) / den
        ci = (ai * lr - (ar - 1.0) * li) / den
        br, bi = br_ref[...], bi_ref[...]
        bbr_ref[...] = cr * br - ci * bi
        bbi_ref[...] = cr * bi + ci * br
        stepr = jnp.exp(lsr_ref[...])
        k = (lax.broadcasted_iota(jnp.int32, (8, N_STATE), 0) + 1).astype(F32)
        magk = jnp.exp(k * (lrr_ref[...] * stepr))
        ang = k * (lir_ref[...] * stepr)
        pr_ref[0:8, :] = magk * jnp.cos(ang)
        pi_ref[0:8, :] = magk * jnp.sin(ang)
        n = 8
        while n < seg:
            tr, ti = pr_ref[n - 1:n, :], pi_ref[n - 1:n, :]
            xr, xi = pr_ref[0:n, :], pi_ref[0:n, :]
            pr_ref[n:2 * n, :] = xr * tr - xi * ti
            pi_ref[n:2 * n, :] = xr * ti + xi * tr
            n *= 2

    col = jax.ShapeDtypeStruct((N_STATE, 1), F32)
    mat = jax.ShapeDtypeStruct((N_STATE, SSM_P), F32)
    pw = jax.ShapeDtypeStruct((seg, N_STATE), F32)
    vm = pl.BlockSpec(memory_space=pltpu.VMEM)
    return pl.pallas_call(
        body, name="ssm_prep", out_shape=(col, col, mat, mat, pw, pw),
        in_specs=[vm] * 8, out_specs=(vm,) * 6,
    )(lam_re, lam_im, log_step, b_re, b_im, lam_re.reshape(1, N_STATE), lam_im.reshape(1, N_STATE),
      log_step.reshape(1, N_STATE))


def _scan_forward(xs, a_re, a_im, pw_re, pw_im, cs, seg):
    are = jnp.broadcast_to(a_re, (N_SEG, ST_T))
    aim = jnp.broadcast_to(a_im, (N_SEG, ST_T))

    def step(t, carry):
        xr, xi = carry
        r = pl.multiple_of(t * N_SEG, N_SEG)
        nr = are * xr - aim * xi + xs[pl.ds(r, N_SEG), 0:ST_T]
        ni = are * xi + aim * xr + xs[pl.ds(r, N_SEG), ST_T:2 * ST_T]
        xs[pl.ds(r, N_SEG), 0:ST_T] = nr
        xs[pl.ds(r, N_SEG), ST_T:2 * ST_T] = ni
        return nr, ni

    zero = jnp.zeros((N_SEG, ST_T), F32)
    fr, fi = lax.fori_loop(0, seg, step, (zero, zero), unroll=2)
    sr, si = pw_re[seg - 1:seg, :], pw_im[seg - 1:seg, :]
    cr = jnp.zeros((1, ST_T), F32)
    ci = jnp.zeros((1, ST_T), F32)
    cs[0:1, :] = cr
    cs[8:9, :] = ci
    for s in range(1, N_SEG):
        ncr = sr * cr - si * ci + fr[s - 1:s, :]
        nci = sr * ci + si * cr + fi[s - 1:s, :]
        cr, ci = ncr, nci
        cs[s:s + 1, :] = cr
        cs[8 + s:9 + s, :] = ci
    car, cai = cs[0:8, :], cs[8:16, :]

    def fix(t, _):
        r = pl.multiple_of(t * N_SEG, N_SEG)
        pr, pi = pw_re[pl.ds(t, 1), :], pw_im[pl.ds(t, 1), :]
        xs[pl.ds(r, N_SEG), 0:ST_T] = xs[pl.ds(r, N_SEG), 0:ST_T] + (pr * car - pi * cai)
        xs[pl.ds(r, N_SEG), ST_T:2 * ST_T] = xs[pl.ds(r, N_SEG), ST_T:2 * ST_T] + (pr * cai + pi * car)
        return 0

    lax.fori_loop(0, seg, fix, 0, unroll=2)


def _ssm_forward(u_perm, bcat, ccat, a_re, a_im, pw_re, pw_im, d_row, seg):
    bl, rows, _ = u_perm.shape

    def body(u_ref, b_ref, c_ref, ar_ref, ai_ref, pr_ref, pi_ref, d_ref, y_ref, xs, cs):
        u = u_ref[0]
        xs[...] = _mm(u, b_ref[0])
        _scan_forward(xs, ar_ref[...], ai_ref[...], pr_ref, pi_ref, cs, seg)
        y_ref[0] = _mm(xs[...], c_ref[0]) + d_ref[...] * u

    return pl.pallas_call(
        body, name="ssm_forward", grid=(bl, N_GT),
        in_specs=[pl.BlockSpec((1, rows, CH_T), lambda b, j: (b, 0, j)),
                  pl.BlockSpec((1, CH_T, 2 * ST_T), lambda b, j: (j, 0, 0)),
                  pl.BlockSpec((1, 2 * ST_T, CH_T), lambda b, j: (j, 0, 0)),
                  pl.BlockSpec((1, ST_T), lambda b, j: (0, j)), pl.BlockSpec((1, ST_T), lambda b, j: (0, j)),
                  pl.BlockSpec((seg, ST_T), lambda b, j: (0, j)), pl.BlockSpec((seg, ST_T), lambda b, j: (0, j)),
                  pl.BlockSpec((1, CH_T), lambda b, j: (0, j))],
        out_specs=pl.BlockSpec((1, rows, CH_T), lambda b, j: (b, 0, j)),
        out_shape=jax.ShapeDtypeStruct((bl, rows, D_SSM), F32),
        scratch_shapes=[pltpu.VMEM((rows, 2 * ST_T), F32), pltpu.VMEM((16, ST_T), F32)],
        compiler_params=_tc_params(("arbitrary", "arbitrary")),
    )(u_perm, bcat, ccat, a_re, a_im, pw_re, pw_im, d_row)


def _ssm_backward(u_perm, dy_perm, bcat, bcat_t, ccat_t, a_re, a_im, pw_re, pw_im, d_row, seg):
    bl, rows, _ = u_perm.shape

    def body(u_ref, dy_ref, b_ref, bt_ref, ct_ref, ar_ref, ai_ref, pr_ref, pi_ref, d_ref,
             du_ref, db_ref, dc_ref, dar_ref, dai_ref, dd_ref, xs, ls, cs, cl):
        b = pl.program_id(1)
        u = u_ref[0]
        dy = dy_ref[0]
        xs[...] = _mm(u, b_ref[0])
        _scan_forward(xs, ar_ref[...], ai_ref[...], pr_ref, pi_ref, cs, seg)
        ls[...] = _mm(dy, ct_ref[0])
        are = jnp.broadcast_to(ar_ref[...], (N_SEG, ST_T))
        aim = jnp.broadcast_to(ai_ref[...], (N_SEG, ST_T))

        def step(i, carry):
            lr, li = carry
            r = pl.multiple_of((seg - 1 - i) * N_SEG, N_SEG)
            nr = are * lr + aim * li + ls[pl.ds(r, N_SEG), 0:ST_T]
            ni = are * li - aim * lr + ls[pl.ds(r, N_SEG), ST_T:2 * ST_T]
            ls[pl.ds(r, N_SEG), 0:ST_T] = nr
            ls[pl.ds(r, N_SEG), ST_T:2 * ST_T] = ni
            return nr, ni

        zero = jnp.zeros((N_SEG, ST_T), F32)
        fr, fi = lax.fori_loop(0, seg, step, (zero, zero), unroll=2)
        sr, si = pr_ref[seg - 1:seg, :], pi_ref[seg - 1:seg, :]
        cr = jnp.zeros((1, ST_T), F32)
        ci = jnp.zeros((1, ST_T), F32)
        cl[7:8, :] = cr
        cl[15:16, :] = ci
        for s in range(N_SEG - 2, -1, -1):
            ncr = sr * cr + si * ci + fr[s + 1:s + 2, :]
            nci = sr * ci - si * cr + fi[s + 1:s + 2, :]
            cr, ci = ncr, nci
            cl[s:s + 1, :] = cr
            cl[8 + s:9 + s, :] = ci
        clr, cli = cl[0:8, :], cl[8:16, :]

        def fix(t, acc):
            dr, di = acc
            r = pl.multiple_of(t * N_SEG, N_SEG)
            pr, pi = pr_ref[pl.ds(seg - 1 - t, 1), :], pi_ref[pl.ds(seg - 1 - t, 1), :]
            lr = ls[pl.ds(r, N_SEG), 0:ST_T] + (pr * clr + pi * cli)
            li = ls[pl.ds(r, N_SEG), ST_T:2 * ST_T] + (pr * cli - pi * clr)
            ls[pl.ds(r, N_SEG), 0:ST_T] = lr
            ls[pl.ds(r, N_SEG), ST_T:2 * ST_T] = li
            rp = pl.multiple_of(jnp.maximum(t - 1, 0) * N_SEG, N_SEG)
            first = t == 0
            xpr = jnp.where(first, cs[0:8, :], xs[pl.ds(rp, N_SEG), 0:ST_T])
            xpi = jnp.where(first, cs[8:16, :], xs[pl.ds(rp, N_SEG), ST_T:2 * ST_T])
            return dr + (lr * xpr + li * xpi), di + (li * xpr - lr * xpi)

        dr, di = lax.fori_loop(0, seg, fix, (zero, zero), unroll=2)
        dar = jnp.sum(dr, axis=0, keepdims=True)
        dai = jnp.sum(di, axis=0, keepdims=True)
        lall = ls[...]
        du_ref[0] = _mm(lall, bt_ref[0]) + d_ref[...] * dy
        dbp = _mm_tn(u, lall)
        dcp = _mm_tn(dy, xs[...])
        ddp = jnp.sum(dy * u, axis=0, keepdims=True)

        @pl.when(b == 0)
        def _():
            db_ref[0] = dbp
            dc_ref[0] = dcp
            dar_ref[...] = dar
            dai_ref[...] = dai
            dd_ref[...] = ddp

        @pl.when(b != 0)
        def _():
            db_ref[0] += dbp
            dc_ref[0] += dcp
            dar_ref[...] += dar
            dai_ref[...] += dai
            dd_ref[...] += ddp

    tile3 = lambda r, c: pl.BlockSpec((1, r, c), lambda j, b: (j, 0, 0))
    lane = lambda r, c: pl.BlockSpec((r, c), lambda j, b: (0, j))
    act = pl.BlockSpec((1, rows, CH_T), lambda j, b: (b, 0, j))
    return pl.pallas_call(
        body, name="ssm_backward", grid=(N_GT, bl),
        in_specs=[act, act, tile3(CH_T, 2 * ST_T), tile3(2 * ST_T, CH_T), tile3(CH_T, 2 * ST_T),
                  lane(1, ST_T), lane(1, ST_T), lane(seg, ST_T), lane(seg, ST_T), lane(1, CH_T)],
        out_specs=(act, tile3(CH_T, 2 * ST_T), tile3(CH_T, 2 * ST_T), lane(1, ST_T), lane(1, ST_T), lane(1, CH_T)),
        out_shape=(jax.ShapeDtypeStruct((bl, rows, D_SSM), F32),
                   jax.ShapeDtypeStruct((N_GT, CH_T, 2 * ST_T), F32), jax.ShapeDtypeStruct((N_GT, CH_T, 2 * ST_T), F32),
                   jax.ShapeDtypeStruct((1, N_STATE), F32), jax.ShapeDtypeStruct((1, N_STATE), F32),
                   jax.ShapeDtypeStruct((1, D_SSM), F32)),
        scratch_shapes=[pltpu.VMEM((rows, 2 * ST_T), F32), pltpu.VMEM((rows, 2 * ST_T), F32),
                        pltpu.VMEM((16, ST_T), F32), pltpu.VMEM((16, ST_T), F32)],
        compiler_params=_tc_params(("arbitrary", "arbitrary")),
    )(u_perm, dy_perm, bcat, bcat_t, ccat_t, a_re, a_im, pw_re, pw_im, d_row)


def _ssm_param_grads(lam_re, lam_im, log_step, b_re, b_im, a_re, a_im, da_re, da_im, dbb_re, dbb_im):
    def body(lr_ref, li_ref, ls_ref, br_ref, bi_ref, ar_ref, ai_ref, gar_ref, gai_ref, gbr_ref, gbi_ref,
             dlr_ref, dli_ref, dls_ref, dbr_ref, dbi_ref):
        lr, li = lr_ref[...], li_ref[...]
        step = jnp.exp(ls_ref[...])
        ar, ai = ar_ref[...], ai_ref[...]
        den = lr * lr + li * li
        cr = ((ar - 1.0) * lr + ai * li) / den
        ci = (ai * lr - (ar - 1.0) * li) / den
        br, bi = br_ref[...], bi_ref[...]
        gbr, gbi = gbr_ref[...], gbi_ref[...]
        dbr_ref[...] = cr * gbr + ci * gbi
        dbi_ref[...] = cr * gbi - ci * gbr
        gcr = jnp.sum(br * gbr + bi * gbi, axis=-1, keepdims=True)
        gci = jnp.sum(br * gbi - bi * gbr, axis=-1, keepdims=True)
        ilr, ili = lr / den, -li / den
        gar = gar_ref[...] + (ilr * gcr + ili * gci)
        gai = gai_ref[...] + (ilr * gci - ili * gcr)
        qr, qi = cr * ilr - ci * ili, cr * ili + ci * ilr
        glr = -(qr * gcr + qi * gci)
        gli = -(qr * gci - qi * gcr)
        gwr = ar * gar + ai * gai
        gwi = ar * gai - ai * gar
        dlr_ref[...] = glr + step * gwr
        dli_ref[...] = gli + step * gwi
        dls_ref[...] = (lr * gwr + li * gwi) * step

    col = jax.ShapeDtypeStruct((N_STATE, 1), F32)
    mat = jax.ShapeDtypeStruct((N_STATE, SSM_P), F32)
    vm = pl.BlockSpec(memory_space=pltpu.VMEM)
    return pl.pallas_call(
        body, name="ssm_param_grads", out_shape=(col, col, col, mat, mat),
        in_specs=[vm] * 11, out_specs=(vm,) * 5,
    )(lam_re, lam_im, log_step, b_re, b_im, a_re, a_im, da_re, da_im, dbb_re, dbb_im)


def _att_bias_mask(first_block):
    qi = lax.broadcasted_iota(jnp.int32, (ATT_BLOCK, 2 * ATT_BLOCK), 0)
    si = lax.broadcasted_iota(jnp.int32, (ATT_BLOCK, 2 * ATT_BLOCK), 1)
    dist = qi + ATT_BLOCK - si
    valid = (dist >= 0) & (dist < ATT_BLOCK) & ((si >= ATT_BLOCK) | jnp.logical_not(first_block))
    return dist.astype(F32), valid


def _attn_forward(q, k, v, sinks, bl, nb):
    t = q.shape[0]

    def body(sink_ref, q_ref, kp_ref, kc_ref, vp_ref, vc_ref, o_ref, lse_ref):
        i = pl.program_id(1)
        dist, valid = _att_bias_mask(i == 0)
        kk = jnp.concatenate([kp_ref[...], kc_ref[...]], axis=0)
        vv = jnp.concatenate([vp_ref[...], vc_ref[...]], axis=0)
        qv = q_ref[...]
        for h in range(N_HEADS):
            kv = h // Q_PER_KV
            slope = 2.0 ** (-(h + 1))
            qh = qv[:, h * HEAD_DIM:(h + 1) * HEAD_DIM]
            kh = kk[:, kv * HEAD_DIM:(kv + 1) * HEAD_DIM]
            vh = vv[:, kv * HEAD_DIM:(kv + 1) * HEAD_DIM]
            s = _mm_nt(qh, kh) * ATT_SCALE - slope * dist
            s = jnp.where(valid, s, NEG_BIG)
            sink = sink_ref[h]
            m = jnp.maximum(jnp.max(s, axis=-1, keepdims=True), sink)
            e = jnp.exp(s - m)
            den = jnp.sum(e, axis=-1, keepdims=True) + jnp.exp(sink - m)
            p = e / den
            o_ref[:, h * HEAD_DIM:(h + 1) * HEAD_DIM] = _mm(p, vh)
            lse_ref[:, h:h + 1] = m + jnp.log(den)

    cur = lambda w: pl.BlockSpec((ATT_BLOCK, w), lambda b, i: (b * nb + i, 0))
    prev = lambda w: pl.BlockSpec((ATT_BLOCK, w), lambda b, i: (b * nb + jnp.maximum(i - 1, 0), 0))
    return pl.pallas_call(
        body, name="attn_forward", grid=(bl, nb),
        in_specs=[pl.BlockSpec(memory_space=pltpu.SMEM), cur(512), prev(128), cur(128), prev(128), cur(128)],
        out_specs=(cur(512), cur(N_HEADS)),
        out_shape=(jax.ShapeDtypeStruct((t, D_ATTN), F32), jax.ShapeDtypeStruct((t, N_HEADS), F32)),
        compiler_params=_tc_params(("arbitrary", "arbitrary")),
    )(sinks, q, k, k, v, v)


def _attn_backward(q, k, v, o, do, lse, sinks, bl, nb):
    t = q.shape[0]

    def body(sink_ref, qc_ref, qn_ref, kp_ref, kc_ref, vp_ref, vc_ref, oc_ref, on_ref, doc_ref, don_ref,
             lc_ref, ln_ref, dq_ref, dk_ref, dv_ref, ds_ref):
        b, i = pl.program_id(0), pl.program_id(1)
        dist, valid = _att_bias_mask(i == 0)
        has_next = i + 1 < nb
        dist_n = dist[:, 0:ATT_BLOCK]
        valid_n = (dist_n < ATT_BLOCK) & has_next
        kk = jnp.concatenate([kp_ref[...], kc_ref[...]], axis=0)
        vv = jnp.concatenate([vp_ref[...], vc_ref[...]], axis=0)
        qc, qn = qc_ref[...], qn_ref[...]
        oc, on = oc_ref[...], on_ref[...]
        doc, don = doc_ref[...], don_ref[...]
        lc, ln = lc_ref[...], ln_ref[...]
        dsink_cols = []
        for kv in range(KV_HEADS):
            kh = kk[:, kv * HEAD_DIM:(kv + 1) * HEAD_DIM]
            vh = vv[:, kv * HEAD_DIM:(kv + 1) * HEAD_DIM]
            khc, vhc = kh[ATT_BLOCK:, :], vh[ATT_BLOCK:, :]
            dk_acc = jnp.zeros((ATT_BLOCK, HEAD_DIM), F32)
            dv_acc = jnp.zeros((ATT_BLOCK, HEAD_DIM), F32)
            for g in range(Q_PER_KV):
                h = kv * Q_PER_KV + g
                hs = slice(h * HEAD_DIM, (h + 1) * HEAD_DIM)
                slope = 2.0 ** (-(h + 1))
                qh, doh = qc[:, hs], doc[:, hs]
                delta = jnp.sum(doh * oc[:, hs], axis=-1, keepdims=True)
                lse_h = lc[:, h:h + 1]
                s = _mm_nt(qh, kh) * ATT_SCALE - slope * dist
                p = jnp.where(valid, jnp.exp(s - lse_h), 0.0)
                dp = _mm_nt(doh, vh)
                dsc = p * (dp - delta)
                dq_ref[:, hs] = _mm(dsc, kh) * ATT_SCALE
                dk_acc += _mm_tn(dsc[:, ATT_BLOCK:], qh)
                dv_acc += _mm_tn(p[:, ATT_BLOCK:], doh)
                dsink_cols.append(-jnp.sum(jnp.exp(sink_ref[h] - lse_h) * delta, axis=0, keepdims=True))
                qh2, doh2 = qn[:, hs], don[:, hs]
                delta2 = jnp.sum(doh2 * on[:, hs], axis=-1, keepdims=True)
                s2 = _mm_nt(qh2, khc) * ATT_SCALE - slope * dist_n
                p2 = jnp.where(valid_n, jnp.exp(s2 - ln[:, h:h + 1]), 0.0)
                dp2 = _mm_nt(doh2, vhc)
                ds2 = p2 * (dp2 - delta2)
                dk_acc += _mm_tn(ds2, qh2)
                dv_acc += _mm_tn(p2, doh2)
            dk_ref[:, kv * HEAD_DIM:(kv + 1) * HEAD_DIM] = dk_acc * ATT_SCALE
            dv_ref[:, kv * HEAD_DIM:(kv + 1) * HEAD_DIM] = dv_acc
        dsink = jnp.concatenate(dsink_cols, axis=1)

        @pl.when((b == 0) & (i == 0))
        def _():
            ds_ref[...] = dsink

        @pl.when((b != 0) | (i != 0))
        def _():
            ds_ref[...] += dsink

    cur = lambda w: pl.BlockSpec((ATT_BLOCK, w), lambda b, i: (b * nb + i, 0))
    prev = lambda w: pl.BlockSpec((ATT_BLOCK, w), lambda b, i: (b * nb + jnp.maximum(i - 1, 0), 0))
    nxt = lambda w: pl.BlockSpec((ATT_BLOCK, w), lambda b, i: (b * nb + jnp.minimum(i + 1, nb - 1), 0))
    return pl.pallas_call(
        body, name="attn_backward", grid=(bl, nb),
        in_specs=[pl.BlockSpec(memory_space=pltpu.SMEM), cur(512), nxt(512), prev(128), cur(128), prev(128), cur(128),
                  cur(512), nxt(512), cur(512), nxt(512), cur(N_HEADS), nxt(N_HEADS)],
        out_specs=(cur(512), cur(128), cur(128), pl.BlockSpec((1, N_HEADS), lambda b, i: (0, 0))),
        out_shape=(jax.ShapeDtypeStruct((t, D_ATTN), F32), jax.ShapeDtypeStruct((t, 128), F32),
                   jax.ShapeDtypeStruct((t, 128), F32), jax.ShapeDtypeStruct((1, N_HEADS), F32)),
        compiler_params=_tc_params(("arbitrary", "arbitrary")),
    )(sinks, q, q, k, k, v, v, o, o, do, do, lse, lse)


def _mix_forward_backward(x2, y_perm, z_ssm, attn, z_attn, p2, target2, w_glu, b_glu, w_out, g_post, w_gate, b_gate,
                          w_proj, bl, seg):
    t = x2.shape[0]
    tm = seg

    def body(x_ref, y_ref, zs_ref, at_ref, za_ref, p_ref, tg_ref,
             wglu_ref, bglu_ref, wout_ref, gpost_ref, wgate_ref, bgate_ref, wproj_ref,
             loss_ref, dh1_ref, dy_ref, dzs_ref, dat_ref, dza_ref,
             dwglu_ref, dbglu_ref, dwout_ref, dgpost_ref, dwgate_ref, dbgate_ref, dwproj_ref):
        i = pl.program_id(0)
        y = y_ref[0]
        u3 = GELU_C * (y + GELU_K * y * y * y)
        th = jnp.tanh(u3)
        gl = 0.5 * y * (1.0 + th)
        a = _mm(gl, wglu_ref[...]) + bglu_ref[...]
        sa = _sigmoid(a)
        glu = gl * sa
        zs = zs_ref[...]
        sgs = _sigmoid(zs)
        ssm_out = glu * (zs * sgs)
        za = za_ref[...]
        sga = _sigmoid(za)
        at = at_ref[...]
        attn_out = at * (za * sga)
        cat = jnp.concatenate([ssm_out, attn_out], axis=-1).astype(BF16)
        mixed = _mm(cat, wout_ref[...])
        r2 = lax.rsqrt(jnp.mean(mixed * mixed, axis=-1, keepdims=True) + EPS)
        nhat = mixed * r2
        gpost = gpost_ref[...]
        h1 = x_ref[...] + nhat * gpost
        gate = _sigmoid(_mm(h1, wgate_ref[...]) + bgate_ref[...])
        pv = p_ref[...]
        pp = _mm(pv, wproj_ref[...])
        h2 = h1 + gate * pp
        err = h2 - tg_ref[...]
        loss_part = jnp.sum(jnp.sum(err * err, axis=-1, keepdims=True), axis=0, keepdims=True) * (0.5 / D_MODEL)
        dh2 = err * (1.0 / D_MODEL)
        dgp = dh2 * pp * gate * (1.0 - gate)
        dpp = dh2 * gate
        dh1 = dh2 + _mm_nt(dgp, wgate_ref[...])
        dh1_ref[...] = dh1
        dnhat = dh1 * gpost
        dmixed = r2 * (dnhat - nhat * jnp.mean(dnhat * nhat, axis=-1, keepdims=True))
        dcat = _mm_nt(dmixed, wout_ref[...])
        dso, dao = dcat[:, 0:D_SSM], dcat[:, D_SSM:]
        dat_ref[...] = dao * (za * sga)
        dza_ref[...] = dao * at * (sga * (1.0 + za * (1.0 - sga)))
        dzs_ref[...] = dso * glu * (sgs * (1.0 + zs * (1.0 - sgs)))
        dglu = dso * (zs * sgs)
        da = dglu * gl * sa * (1.0 - sa)
        dgl = dglu * sa + _mm_nt(da, wglu_ref[...])
        dgelu = 0.5 * (1.0 + th) + 0.5 * y * (1.0 - th * th) * (GELU_C * (1.0 + 3.0 * GELU_K * y * y))
        dy_ref[0] = dgl * dgelu
        parts = (
            (dwglu_ref, _mm_tn(gl, da)), (dbglu_ref, jnp.sum(da, axis=0, keepdims=True)),
            (dwout_ref, _mm_tn(cat, dmixed)), (dgpost_ref, jnp.sum(dh1 * nhat, axis=0, keepdims=True)),
            (dwgate_ref, _mm_tn(h1, dgp)), (dbgate_ref, jnp.sum(dgp, axis=0, keepdims=True)),
            (dwproj_ref, _mm_tn(pv, dpp)), (loss_ref, loss_part),
        )

        @pl.when(i == 0)
        def _():
            for ref, val in parts:
                ref[...] = val

        @pl.when(i != 0)
        def _():
            for ref, val in parts:
                ref[...] += val

    row = lambda w: pl.BlockSpec((tm, w), lambda i: (i, 0))
    perm = pl.BlockSpec((1, tm, D_SSM), lambda i: (i // N_SEG, 0, i % N_SEG))
    perm_shape = jax.ShapeDtypeStruct((bl, seg, N_SEG * D_SSM), F32)
    acc = lambda r, c: (_const_spec((r, c)), jax.ShapeDtypeStruct((r, c), F32))
    accs = [acc(D_SSM, D_SSM), acc(1, D_SSM), acc(D_MODEL, D_MODEL), acc(1, D_MODEL), acc(D_MODEL, D_MODEL),
            acc(1, D_MODEL), acc(D_PLE, D_MODEL)]
    return pl.pallas_call(
        body, name="mix_forward_backward", grid=(t // tm,),
        in_specs=[row(D_MODEL), perm, row(512), row(512), row(512), row(D_PLE), row(D_MODEL),
                  _const_spec((D_SSM, D_SSM)), _const_spec((1, D_SSM)), _const_spec((D_MODEL, D_MODEL)),
                  _const_spec((1, D_MODEL)), _const_spec((D_MODEL, D_MODEL)), _const_spec((1, D_MODEL)),
                  _const_spec((D_PLE, D_MODEL))],
        out_specs=(_const_spec((1, 1)), row(D_MODEL), perm, row(512), row(512), row(512)) + tuple(a[0] for a in accs),
        out_shape=(jax.ShapeDtypeStruct((1, 1), F32), jax.ShapeDtypeStruct((t, D_MODEL), F32), perm_shape,
                   jax.ShapeDtypeStruct((t, 512), F32), jax.ShapeDtypeStruct((t, 512), F32),
                   jax.ShapeDtypeStruct((t, 512), F32)) + tuple(a[1] for a in accs),
        compiler_params=_tc_params(("arbitrary",)),
    )(x2, y_perm, z_ssm, attn, z_attn, p2, target2, w_glu, b_glu, w_out, g_post, w_gate, b_gate, w_proj)


def _in_backward(x2, dh1, du_perm, dz_ssm, dq, dk, dv, dz_attn, g_pre, w_in, bl, seg):
    t = x2.shape[0]
    tm = seg

    def body(x_ref, dh1_ref, du_ref, dzs_ref, dq_ref, dk_ref, dv_ref, dza_ref, g_ref, w_ref,
             gx_ref, dw_ref, dg_ref):
        i = pl.program_id(0)
        xv = x_ref[...]
        r = lax.rsqrt(jnp.mean(xv * xv, axis=-1, keepdims=True) + EPS)
        xhat = xv * r
        g = g_ref[...]
        hn = (xhat * g).astype(BF16)
        dproj = jnp.concatenate([du_ref[0].astype(BF16), dzs_ref[...].astype(BF16), dq_ref[...].astype(BF16),
                                 dk_ref[...].astype(BF16), dv_ref[...].astype(BF16), dza_ref[...].astype(BF16)],
                                axis=-1)
        dhn = _mm_nt(dproj, w_ref[...])
        dxhat = dhn * g
        gx_ref[...] = dh1_ref[...] + r * (dxhat - xhat * jnp.mean(dxhat * xhat, axis=-1, keepdims=True))
        dwp = _mm_tn(hn, dproj)
        dgp = jnp.sum(dhn * xhat, axis=0, keepdims=True)

        @pl.when(i == 0)
        def _():
            dw_ref[...] = dwp
            dg_ref[...] = dgp

        @pl.when(i != 0)
        def _():
            dw_ref[...] += dwp
            dg_ref[...] += dgp

    row = lambda w: pl.BlockSpec((tm, w), lambda i: (i, 0))
    perm = pl.BlockSpec((1, tm, D_SSM), lambda i: (i // N_SEG, 0, i % N_SEG))
    return pl.pallas_call(
        body, name="in_backward", grid=(t // tm,),
        in_specs=[row(D_MODEL), row(D_MODEL), perm, row(512), row(512), row(128), row(128), row(512),
                  _const_spec((1, D_MODEL)), _const_spec((D_MODEL, D_IN))],
        out_specs=(row(D_MODEL), _const_spec((D_MODEL, D_IN)), _const_spec((1, D_MODEL))),
        out_shape=(jax.ShapeDtypeStruct((t, D_MODEL), F32), jax.ShapeDtypeStruct((D_MODEL, D_IN), F32),
                   jax.ShapeDtypeStruct((1, D_MODEL), F32)),
        compiler_params=_tc_params(("arbitrary",)),
    )(x2, dh1, du_perm, dz_ssm, dq, dk, dv, dz_attn, g_pre, w_in)


def _block_diag(t):
    a, b = t.shape[1], t.shape[2]
    eye = jnp.eye(G_TILE, dtype=t.dtype)
    t = t.reshape(N_GT, G_TILE, a, 1, b) * eye[None, :, None, :, None]
    return t.reshape(N_GT, G_TILE * a, G_TILE * b)


def _diag_blocks(m, a, b):
    m = m.reshape(N_GT, G_TILE, a, G_TILE, b)
    return jnp.einsum("tgagb->tgab", m).reshape(SSM_G, a, b)


def _local_step(x, p, target, pre_norm_g, w_in, ssm_lam_re, ssm_lam_im, ssm_log_step, ssm_b_re, ssm_b_im, ssm_c_re,
                ssm_c_im, ssm_d, w_glu, ssm_b_glu, attn_sinks, w_out, post_norm_g, w_proj, w_gate, pl_b_gate):
    bl, seq, _ = x.shape
    seg = seq // N_SEG
    nb = seq // ATT_BLOCK
    t = bl * seq
    x2 = x.reshape(t, D_MODEL)
    p2 = p.reshape(t, D_PLE)
    tg2 = target.reshape(t, D_MODEL)

    lam_re = ssm_lam_re.reshape(N_STATE, 1)
    lam_im = ssm_lam_im.reshape(N_STATE, 1)
    log_step = jnp.broadcast_to(ssm_log_step.reshape(SSM_G, 1), (SSM_G, SSM_N)).reshape(N_STATE, 1)
    b_re = ssm_b_re.reshape(N_STATE, SSM_P)
    b_im = ssm_b_im.reshape(N_STATE, SSM_P)
    a_re, a_im, bb_re, bb_im, pw_re, pw_im = _ssm_prep(lam_re, lam_im, log_step, b_re, b_im, seg)
    bb_re_t = jnp.swapaxes(bb_re.reshape(SSM_G, SSM_N, SSM_P), 1, 2)
    bb_im_t = jnp.swapaxes(bb_im.reshape(SSM_G, SSM_N, SSM_P), 1, 2)
    bcat = jnp.concatenate([_block_diag(bb_re_t), _block_diag(bb_im_t)], axis=-1).astype(BF16)
    ccat_t = jnp.concatenate([_block_diag(ssm_c_re.reshape(SSM_G, SSM_P, SSM_N)),
                              -_block_diag(ssm_c_im.reshape(SSM_G, SSM_P, SSM_N))], axis=-1).astype(BF16)
    bcat_t = jnp.swapaxes(bcat, 1, 2)
    ccat = jnp.swapaxes(ccat_t, 1, 2)
    a_re_row, a_im_row = a_re.reshape(1, N_STATE), a_im.reshape(1, N_STATE)
    d_row = ssm_d.reshape(1, D_SSM)

    u_perm, z_ssm, q, k, v, z_attn = _in_proj(x2, pre_norm_g.reshape(1, D_MODEL), w_in, bl, seg)
    u_perm = u_perm.reshape(bl, seq, D_SSM)
    y_perm = _ssm_forward(u_perm, bcat, ccat, a_re_row, a_im_row, pw_re, pw_im, d_row, seg)
    sinks = attn_sinks.reshape(N_HEADS)
    attn, lse = _attn_forward(q, k, v, sinks, bl, nb)
    (loss, dh1, dy_perm, dz_ssm, dattn, dz_attn, d_w_glu, d_b_glu, d_w_out, d_g_post, d_w_gate, d_b_gate,
     d_w_proj) = _mix_forward_backward(
        x2, y_perm.reshape(bl, seg, N_SEG * D_SSM), z_ssm, attn, z_attn, p2, tg2, w_glu,
        ssm_b_glu.reshape(1, D_SSM), w_out, post_norm_g.reshape(1, D_MODEL), w_gate, pl_b_gate.reshape(1, D_MODEL),
        w_proj, bl, seg)
    dq, dk, dv, d_sinks = _attn_backward(q, k, v, attn, dattn, lse, sinks, bl, nb)
    du_perm, d_bcat, d_ccat_t, da_re, da_im, d_d = _ssm_backward(
        u_perm, dy_perm.reshape(bl, seq, D_SSM), bcat, bcat_t, ccat_t, a_re_row, a_im_row, pw_re, pw_im, d_row, seg)
    grad_x, d_w_in, d_g_pre = _in_backward(
        x2, dh1, du_perm.reshape(bl, seg, N_SEG * D_SSM), dz_ssm, dq, dk, dv, dz_attn,
        pre_norm_g.reshape(1, D_MODEL), w_in, bl, seg)
    dbb_re = jnp.swapaxes(_diag_blocks(d_bcat[:, :, 0:ST_T], SSM_P, SSM_N), 1, 2).reshape(N_STATE, SSM_P)
    dbb_im = jnp.swapaxes(_diag_blocks(d_bcat[:, :, ST_T:], SSM_P, SSM_N), 1, 2).reshape(N_STATE, SSM_P)
    d_lam_re, d_lam_im, d_ls, d_b_re, d_b_im = _ssm_param_grads(
        lam_re, lam_im, log_step, b_re, b_im, a_re, a_im, da_re.reshape(N_STATE, 1), da_im.reshape(N_STATE, 1),
        dbb_re, dbb_im)
    grads = {
        "pre_norm_g": d_g_pre, "w_in": d_w_in, "ssm_lam_re": d_lam_re, "ssm_lam_im": d_lam_im,
        "ssm_log_step": jnp.sum(d_ls.reshape(SSM_G, SSM_N), axis=-1), "ssm_b_re": d_b_re, "ssm_b_im": d_b_im,
        "ssm_c_re": _diag_blocks(d_ccat_t[:, :, 0:ST_T], SSM_P, SSM_N),
        "ssm_c_im": -_diag_blocks(d_ccat_t[:, :, ST_T:], SSM_P, SSM_N),
        "ssm_d": d_d, "ssm_w_glu": d_w_glu, "ssm_b_glu": d_b_glu, "attn_sinks": d_sinks, "w_out": d_w_out,
        "post_norm_g": d_g_post, "pl_w_proj": d_w_proj, "pl_w_gate": d_w_gate, "pl_b_gate": d_b_gate,
    }
    return loss.reshape(()), grad_x.reshape(bl, seq, D_MODEL), grads


BIG_NAMES = ("w_in", "w_out", "pl_w_gate", "pl_w_proj", "ssm_w_glu")
COL_SHARDED = {"w_in": D_IN // N_DEV, "pl_w_proj": D_MODEL // N_DEV}
WEIGHT_NAMES = ("pre_norm_g", "w_in", "ssm_lam_re", "ssm_lam_im", "ssm_log_step", "ssm_b_re", "ssm_b_im", "ssm_c_re",
                "ssm_c_im", "ssm_d", "ssm_w_glu", "ssm_b_glu", "attn_sinks", "w_out", "post_norm_g", "pl_w_proj",
                "pl_w_gate", "pl_b_gate")


def _pack_big_shards(shards):
    return jnp.concatenate([shards[n].reshape(-1, LANES) for n in BIG_NAMES], axis=0)


def _unpack_big_shards(packed, like):
    out, r = {}, 0
    for n, size in zip(BIG_NAMES, BIG_SIZES):
        out[n] = packed[r:r + size // LANES].reshape(like[n].shape)
        r += size // LANES
    return out


def _unpack_gathered(gathered, shard_shapes):
    g = gathered.reshape(N_DEV, BIG_ROWS, LANES)
    out, r = {}, 0
    for n, size in zip(BIG_NAMES, BIG_SIZES):
        rows, cols = shard_shapes[n]
        part = g[:, r:r + size // LANES].reshape(N_DEV, rows, cols)
        if n in COL_SHARDED:
            out[n] = jnp.swapaxes(part, 0, 1).reshape(rows, N_DEV * cols)
        else:
            out[n] = part.reshape(N_DEV * rows, cols)
        r += size // LANES
    return out


def _pack_big_full(full):
    parts = []
    for n in BIG_NAMES:
        m = full[n]
        if n in COL_SHARDED:
            m = jnp.swapaxes(m.reshape(m.shape[0], N_DEV, COL_SHARDED[n]), 0, 1)
        parts.append(m.reshape(N_DEV, -1, LANES))
    return jnp.concatenate(parts, axis=1)


def _pack_small(vals):
    flat = jnp.concatenate([vals[n].reshape(-1) for n in SMALL_NAMES])
    flat = jnp.pad(flat, (0, N_DEV * SMALL_ROWS * LANES - flat.shape[0]))
    return flat.reshape(N_DEV, SMALL_ROWS, LANES)


def _unpack_small(packed, like):
    flat, out, r = packed.reshape(-1), {}, 0
    for n, size in zip(SMALL_NAMES, SMALL_SIZES):
        out[n] = flat[r:r + size].reshape(like[n].shape)
        r += size
    return out


def kernel(x, p, pre_norm_g, w_in, ssm_lam_re, ssm_lam_im, ssm_log_step, ssm_b_re, ssm_b_im, ssm_c_re, ssm_c_im, ssm_d, ssm_w_glu, ssm_b_glu, attn_sinks, w_out, post_norm_g, pl_w_proj, pl_w_gate, pl_b_gate, loss_target, m_pre_norm_g, m_w_in, m_ssm_lam_re, m_ssm_lam_im, m_ssm_log_step, m_ssm_b_re, m_ssm_b_im, m_ssm_c_re, m_ssm_c_im, m_ssm_d, m_ssm_w_glu, m_ssm_b_glu, m_attn_sinks, m_w_out, m_post_norm_g, m_pl_w_proj, m_pl_w_gate, m_pl_b_gate, v_pre_norm_g, v_w_in, v_ssm_lam_re, v_ssm_lam_im, v_ssm_log_step, v_ssm_b_re, v_ssm_b_im, v_ssm_c_re, v_ssm_c_im, v_ssm_d, v_ssm_w_glu, v_ssm_b_glu, v_attn_sinks, v_w_out, v_post_norm_g, v_pl_w_proj, v_pl_w_gate, v_pl_b_gate):
    w = dict(pre_norm_g=pre_norm_g, w_in=w_in, ssm_lam_re=ssm_lam_re, ssm_lam_im=ssm_lam_im, ssm_log_step=ssm_log_step,
             ssm_b_re=ssm_b_re, ssm_b_im=ssm_b_im, ssm_c_re=ssm_c_re, ssm_c_im=ssm_c_im, ssm_d=ssm_d, ssm_w_glu=ssm_w_glu,
             ssm_b_glu=ssm_b_glu, attn_sinks=attn_sinks, w_out=w_out, post_norm_g=post_norm_g, pl_w_proj=pl_w_proj,
             pl_w_gate=pl_w_gate, pl_b_gate=pl_b_gate)
    m = dict(pre_norm_g=m_pre_norm_g, w_in=m_w_in, ssm_lam_re=m_ssm_lam_re, ssm_lam_im=m_ssm_lam_im,
             ssm_log_step=m_ssm_log_step, ssm_b_re=m_ssm_b_re, ssm_b_im=m_ssm_b_im, ssm_c_re=m_ssm_c_re,
             ssm_c_im=m_ssm_c_im, ssm_d=m_ssm_d, ssm_w_glu=m_ssm_w_glu, ssm_b_glu=m_ssm_b_glu, attn_sinks=m_attn_sinks,
             w_out=m_w_out, post_norm_g=m_post_norm_g, pl_w_proj=m_pl_w_proj, pl_w_gate=m_pl_w_gate,
             pl_b_gate=m_pl_b_gate)
    v = dict(pre_norm_g=v_pre_norm_g, w_in=v_w_in, ssm_lam_re=v_ssm_lam_re, ssm_lam_im=v_ssm_lam_im,
             ssm_log_step=v_ssm_log_step, ssm_b_re=v_ssm_b_re, ssm_b_im=v_ssm_b_im, ssm_c_re=v_ssm_c_re,
             ssm_c_im=v_ssm_c_im, ssm_d=v_ssm_d, ssm_w_glu=v_ssm_w_glu, ssm_b_glu=v_ssm_b_glu, attn_sinks=v_attn_sinks,
             w_out=v_w_out, post_norm_g=v_post_norm_g, pl_w_proj=v_pl_w_proj, pl_w_gate=v_pl_w_gate,
             pl_b_gate=v_pl_b_gate)
    me = 4 * lax.axis_index("x") + 2 * lax.axis_index("y") + lax.axis_index("c")

    w_packed = _pack_big_shards(w)
    shard_shapes = {n: w[n].shape[1:] for n in BIG_NAMES}
    full = _unpack_gathered(_allgather_weights(w_packed), shard_shapes)

    loss, grad_x, grads = _local_step(
        x, p[0], loss_target, pre_norm_g[0], full["w_in"], ssm_lam_re[0], ssm_lam_im[0], ssm_log_step[0], ssm_b_re[0],
        ssm_b_im[0], ssm_c_re[0], ssm_c_im[0], ssm_d[0], full["ssm_w_glu"], ssm_b_glu[0], attn_sinks[0],
        full["w_out"], post_norm_g[0], full["pl_w_proj"], full["pl_w_gate"], pl_b_gate[0])
    loss = lax.psum(loss, ("x", "y", "c"))

    gbig = _pack_big_full(grads)
    gown = lax.dynamic_index_in_dim(gbig, me, axis=0, keepdims=False)
    own_small = lambda d: lax.dynamic_index_in_dim(_pack_small(d), me, axis=0, keepdims=False)
    big_out, small_out = _reduce_update(
        gbig.astype(BF16), gown, _pack_small(grads), w_packed, _pack_big_shards(m), _pack_big_shards(v),
        own_small(w), own_small(m), own_small(v))
    small_out = jnp.swapaxes(small_out, 0, 1)
    results = []
    for kind in range(4):
        vals = {**_unpack_big_shards(big_out[kind], w), **_unpack_small(small_out[kind], w)}
        results.append([vals[n] for n in WEIGHT_NAMES])
    return (loss, grad_x, *results[0], *results[1], *results[2], *results[3])
```

```python
import functools
import math

import jax
import jax.numpy as jnp
from jax import lax
from jax.experimental import pallas as pl
from jax.experimental.pallas import tpu as pltpu

F32 = jnp.float32
BF16 = jnp.bfloat16

D_MODEL = 1024
D_SSM = 512
D_ATTN = 512
SSM_P = 16
SSM_G = 32
SSM_N = 64
N_HEADS = 8
KV_HEADS = 2
Q_PER_KV = 4
HEAD_DIM = 64
ATT_BLOCK = 128
D_PLE = 256
D_IN = 2304
EPS = 1e-6
N_DEV = 8
N_SEG = 8
G_TILE = 8
N_GT = SSM_G // G_TILE
CH_T = G_TILE * SSM_P
ST_T = G_TILE * SSM_N
N_STATE = SSM_G * SSM_N
LANES = 128
VMEM_LIMIT = 60 * 1024 * 1024

ADAM_LR = 0.001
ADAM_B1 = 0.9
ADAM_B2 = 0.999
ADAM_EPS = 1e-08
ADAM_WD = 0.01
ADAM_STEP = 10

GELU_C = math.sqrt(2.0 / math.pi)
GELU_K = 0.044715
ATT_SCALE = 1.0 / math.sqrt(HEAD_DIM)
NEG_BIG = -1e30


def _mm(a, b):
    return jnp.dot(a.astype(BF16), b.astype(BF16), preferred_element_type=F32)


def _mm_nt(a, b):
    return lax.dot_general(a.astype(BF16), b.astype(BF16), (((1,), (1,)), ((), ())), preferred_element_type=F32)


def _mm_tn(a, b):
    return lax.dot_general(a.astype(BF16), b.astype(BF16), (((0,), (0,)), ((), ())), preferred_element_type=F32)


def _sigmoid(x):
    return 1.0 / (1.0 + jnp.exp(-x))


def _tc_params(sem):
    return pltpu.CompilerParams(dimension_semantics=sem, vmem_limit_bytes=VMEM_LIMIT)


def _const_spec(shape):
    nd = len(shape)
    return pl.BlockSpec(shape, lambda *_: (0,) * nd)


def _mesh_pos():
    return lax.axis_index("x"), lax.axis_index("y"), lax.axis_index("c")


ROW_CHUNK = 64


def _row_chunks(nrows, fn):
    def step(i, carry):
        fn(pl.ds(pl.multiple_of(i * ROW_CHUNK, ROW_CHUNK), ROW_CHUNK))
        return carry

    lax.fori_loop(0, nrows // ROW_CHUNK, step, 0)


def _slot(px, py, pc):
    return 4 * px + 2 * py + pc


def _allgather_weights(shards):
    n = len(shards)

    def body(*refs):
        srcs, outs, (send_sems, recv_sems) = refs[:n], refs[n:2 * n], refs[2 * n:]
        x, y, c = _mesh_pos()
        me, sibling = (x, y, c), (x, y, 1 - c)
        chips = [(1 - x, y), (x, 1 - y), (1 - x, 1 - y)]

        def copy(a, k, block, to):
            blk = outs[a].at[_slot(*block)]
            return pltpu.make_async_remote_copy(
                src_ref=blk, dst_ref=blk, send_sem=send_sems.at[7 * a + k], recv_sem=recv_sems.at[7 * a + k],
                device_id=to, device_id_type=pl.DeviceIdType.MESH)

        sends = []
        for a in range(n):
            mine = outs[a].at[_slot(*me)]

            def cast(r, mine=mine, src=srcs[a]):
                mine[r, :] = src[r, :].astype(BF16)

            _row_chunks(srcs[a].shape[0], cast)
            first = [copy(a, 0, me, sibling)] + [copy(a, 1 + j, me, (*chip, c)) for j, chip in enumerate(chips)]
            for cp in first:
                cp.start()
            sends += first
        for a in range(n):
            for j, chip in enumerate(chips):
                copy(a, 1 + j, (*chip, c), me).wait_recv()
                fwd = copy(a, 4 + j, (*chip, c), sibling)
                fwd.start()
                sends.append(fwd)
        for a in range(n):
            copy(a, 0, sibling, me).wait_recv()
            for j, chip in enumerate(chips):
                copy(a, 4 + j, (*chip, 1 - c), me).wait_recv()
        for cp in sends:
            cp.wait_send()

    vm = pl.BlockSpec(memory_space=pltpu.VMEM)
    return pl.pallas_call(
        body, name="allgather_weights",
        out_shape=tuple(jax.ShapeDtypeStruct((N_DEV,) + s.shape, BF16) for s in shards),
        in_specs=[vm] * n, out_specs=(vm,) * n,
        scratch_shapes=[pltpu.SemaphoreType.DMA((7 * n,)), pltpu.SemaphoreType.DMA((7 * n,))],
        compiler_params=pltpu.CompilerParams(vmem_limit_bytes=VMEM_LIMIT),
    )(*shards)


def _adamw(w, g, m, v):
    m = ADAM_B1 * m + (1.0 - ADAM_B1) * g
    v = ADAM_B2 * v + (1.0 - ADAM_B2) * (g * g)
    m_hat = m / (1.0 - ADAM_B1 ** ADAM_STEP)
    v_hat = v / (1.0 - ADAM_B2 ** ADAM_STEP)
    delta = -ADAM_LR * (m_hat / (jnp.sqrt(v_hat) + ADAM_EPS) + ADAM_WD * w)
    return delta, m, v


def _remote(src, dst, send_sems, recv_sems, k, to):
    return pltpu.make_async_remote_copy(src_ref=src, dst_ref=dst, send_sem=send_sems.at[k], recv_sem=recv_sems.at[k],
                                        device_id=to, device_id_type=pl.DeviceIdType.MESH)


def _reduce_update_big(g16, gown, w, m, v):
    n = len(g16)

    def body(*refs):
        g16_r, go_r, w_r, m_r, v_r = (refs[i * n:(i + 1) * n] for i in range(5))
        outs = refs[5 * n:9 * n]
        send2, recv1, recv2 = (refs[9 * n + i * n:9 * n + (i + 1) * n] for i in range(3))
        s_send, s_recv = refs[12 * n:]
        x, y, c = _mesh_pos()
        sibling = (x, y, 1 - c)
        chips = [(1 - x, y), (x, 1 - y), (1 - x, 1 - y)]
        all_chips = [(x, y)] + chips
        sends = []
        lvl1 = []
        for a in range(n):
            cps = [_remote(g16_r[a].at[_slot(*chip, 1 - c)], recv1[a].at[j], s_send, s_recv, 7 * a + j, sibling)
                   for j, chip in enumerate(all_chips)]
            for cp in cps:
                cp.start()
            lvl1.append(cps)
        lvl2 = []
        for a in range(n):
            for cp in lvl1[a]:
                cp.wait_recv()
            og = outs[4 * a]

            def partials(r, a=a, og=og):
                og[r, :] = go_r[a][r, :] + recv1[a][0, r, :].astype(F32)
                for j, chip in enumerate(chips):
                    mine16 = g16_r[a][_slot(*chip, c), r, :].astype(F32)
                    send2[a][j, r, :] = (mine16 + recv1[a][1 + j, r, :].astype(F32)).astype(BF16)

            _row_chunks(go_r[a].shape[0], partials)
            cps = [_remote(send2[a].at[j], recv2[a].at[j], s_send, s_recv, 7 * a + 4 + j, (*chip, c))
                   for j, chip in enumerate(chips)]
            for cp in cps:
                cp.start()
            lvl2.append(cps)
        for a in range(n):
            for cp in lvl2[a]:
                cp.wait_recv()
            og, od, om, ov = outs[4 * a:4 * a + 4]

            def update(r, a=a, og=og, od=od, om=om, ov=ov):
                g = og[r, :]
                for j in range(3):
                    g = g + recv2[a][j, r, :].astype(F32)
                d, nm, nv = _adamw(w_r[a][r, :], g, m_r[a][r, :], v_r[a][r, :])
                og[r, :] = g
                od[r, :] = d
                om[r, :] = nm
                ov[r, :] = nv

            _row_chunks(go_r[a].shape[0], update)
        for cps in lvl1 + lvl2:
            for cp in cps:
                cp.wait_send()

    vm = pl.BlockSpec(memory_space=pltpu.VMEM)
    shard = [jax.ShapeDtypeStruct(t.shape, F32) for t in gown]
    scratch = ([pltpu.VMEM((3,) + t.shape, BF16) for t in gown] + [pltpu.VMEM((4,) + t.shape, BF16) for t in gown]
               + [pltpu.VMEM((3,) + t.shape, BF16) for t in gown]
               + [pltpu.SemaphoreType.DMA((7 * n,)), pltpu.SemaphoreType.DMA((7 * n,))])
    res = pl.pallas_call(
        body, name="reduce_update_big",
        out_shape=tuple(s for s in shard for _ in range(4)),
        in_specs=[vm] * (5 * n), out_specs=(vm,) * (4 * n), scratch_shapes=scratch,
        compiler_params=pltpu.CompilerParams(vmem_limit_bytes=VMEM_LIMIT),
    )(*g16, *gown, *w, *m, *v)
    return [res[4 * a:4 * a + 4] for a in range(n)]


TINY = (("pre_norm_g", 1, 1024), ("post_norm_g", 1, 1024), ("pl_b_gate", 1, 1024), ("ssm_d", 1, 512),
        ("ssm_b_glu", 1, 512), ("ssm_log_step", 1, 32), ("attn_sinks", 1, 8), ("ssm_lam_re", 32, 64),
        ("ssm_lam_im", 32, 64))
MEDIUM = (("ssm_b_re", N_STATE, SSM_P), ("ssm_b_im", N_STATE, SSM_P), ("ssm_c_re", SSM_G * SSM_P, SSM_N),
          ("ssm_c_im", SSM_G * SSM_P, SSM_N))


def _stage_rows():
    offs, r = {}, 0
    for name, rows, cols in TINY + (("loss", 1, 1),):
        if rows > 1:
            r = -(-r // 8) * 8
        offs[name] = r
        r += rows if rows > 1 else max(cols // LANES, 1)
    return offs, -(-r // 8) * 8


def _reduce_update_small(loss, g_tiny, w_tiny, m_tiny, v_tiny, g_med, w_med, m_med, v_med):
    nt, nm_ = len(TINY), len(MEDIUM)
    offs, stage_rows = _stage_rows()

    def body(*refs):
        loss_r = refs[0]
        gt, wt, mt, vt = (refs[1 + i * nt:1 + (i + 1) * nt] for i in range(4))
        base = 1 + 4 * nt
        gm, wm, mm, vm_ = (refs[base + i * nm_:base + (i + 1) * nm_] for i in range(4))
        base += 4 * nm_
        loss_o = refs[base]
        out_t = refs[base + 1:base + 1 + 4 * nt]
        base += 1 + 4 * nt
        out_m = refs[base:base + 4 * nm_]
        base += 4 * nm_
        stage = refs[base]
        recv1, part, recv2 = (refs[base + 1 + i * nm_:base + 1 + (i + 1) * nm_] for i in range(3))
        s_send, s_recv = refs[base + 1 + 3 * nm_:]
        x, y, c = _mesh_pos()
        me = _slot(x, y, c)
        sibling = (x, y, 1 - c)
        chips = [(1 - x, y), (x, 1 - y), (1 - x, 1 - y)]
        all_chips = [(x, y)] + chips
        peers = [sibling] + [(*chip, c) for chip in chips] + [(*chip, 1 - c) for chip in chips]
        sem = iter(range(7 + 14 * nm_))
        lvl1 = []
        for a in range(nm_):
            cps = [_remote(gm[a].at[_slot(*chip, 1 - c)], recv1[a].at[j], s_send, s_recv, next(sem), sibling)
                   for j, chip in enumerate(all_chips)]
            for cp in cps:
                cp.start()
            lvl1.append(cps)
        mine = stage.at[me]
        mine[...] = jnp.zeros((stage_rows, LANES), F32)
        for (name, rows, cols), ref in zip(TINY + (("loss", 1, 1),), gt + (loss_r,)):
            r0 = offs[name]
            if rows > 1:
                mine[r0:r0 + rows, 0:cols] = ref[...]
            elif cols >= LANES:
                for i in range(cols // LANES):
                    mine[r0 + i:r0 + i + 1, :] = ref[:, i * LANES:(i + 1) * LANES]
            else:
                mine[r0:r0 + 1, 0:cols] = ref[...]
        tiny_cps = [_remote(mine, mine, s_send, s_recv, next(sem), peer) for peer in peers]
        for cp in tiny_cps:
            cp.start()
        lvl2 = []
        for a in range(nm_):
            for cp in lvl1[a]:
                cp.wait_recv()
            for j, chip in enumerate(all_chips):
                part[a][j] = gm[a][_slot(*chip, c)] + recv1[a][j]
            cps = [_remote(part[a].at[1 + j], recv2[a].at[j], s_send, s_recv, next(sem), (*chip, c))
                   for j, chip in enumerate(chips)]
            for cp in cps:
                cp.start()
            lvl2.append(cps)
        lvl3 = []
        for a in range(nm_):
            for cp in lvl2[a]:
                cp.wait_recv()
            blk = out_m[4 * a].at[me]
            blk[...] = ((part[a][0] + recv2[a][0]) + recv2[a][1]) + recv2[a][2]
            cps = [_remote(blk, blk, s_send, s_recv, next(sem), peer) for peer in peers]
            for cp in cps:
                cp.start()
            lvl3.append(cps)
        for cp in tiny_cps:
            cp.wait_recv()
        tot = stage[0]
        for d in range(1, N_DEV):
            tot = tot + stage[d]
        loss_o[...] = tot[offs["loss"]:offs["loss"] + 1, 0:1]
        for k, (name, rows, cols) in enumerate(TINY):
            r0 = offs[name]
            if rows > 1:
                g = tot[r0:r0 + rows, 0:cols]
            elif cols >= LANES:
                g = jnp.concatenate([tot[r0 + i:r0 + i + 1, :] for i in range(cols // LANES)], axis=1)
            else:
                g = tot[r0:r0 + 1, 0:cols]
            d, nm, nv = _adamw(wt[k][...], g, mt[k][...], vt[k][...])
            for ref, val in zip(out_t[4 * k:4 * k + 4], (g, d, nm, nv)):
                ref[...] = val
        for a in range(nm_):
            for cp in lvl3[a]:
                cp.wait_recv()
            og, od, om, ov = out_m[4 * a:4 * a + 4]
            for b in range(N_DEV):
                d, nm, nv = _adamw(wm[a][b], og[b], mm[a][b], vm_[a][b])
                od[b] = d
                om[b] = nm
                ov[b] = nv
        for cps in lvl1 + lvl2 + lvl3 + [tiny_cps]:
            for cp in cps:
                cp.wait_send()

    vmem = pl.BlockSpec(memory_space=pltpu.VMEM)
    t_shapes = [jax.ShapeDtypeStruct((rows, cols), F32) for _, rows, cols in TINY]
    m_shapes = [jax.ShapeDtypeStruct((N_DEV, rows // N_DEV, cols), F32) for _, rows, cols in MEDIUM]
    blk = [(rows // N_DEV, cols) for _, rows, cols in MEDIUM]
    scratch = ([pltpu.VMEM((N_DEV, stage_rows, LANES), F32)]
               + [pltpu.VMEM((4,) + b, F32) for b in blk] + [pltpu.VMEM((4,) + b, F32) for b in blk]
               + [pltpu.VMEM((3,) + b, F32) for b in blk]
               + [pltpu.SemaphoreType.DMA((7 + 14 * nm_,)), pltpu.SemaphoreType.DMA((7 + 14 * nm_,))])
    res = pl.pallas_call(
        body, name="reduce_update_small",
        out_shape=(jax.ShapeDtypeStruct((1, 1), F32),) + tuple(s for s in t_shapes for _ in range(4))
        + tuple(s for s in m_shapes for _ in range(4)),
        in_specs=[vmem] * (1 + 4 * nt + 4 * nm_), out_specs=(vmem,) * (1 + 4 * nt + 4 * nm_), scratch_shapes=scratch,
        compiler_params=pltpu.CompilerParams(vmem_limit_bytes=VMEM_LIMIT),
    )(loss, *g_tiny, *w_tiny, *m_tiny, *v_tiny, *g_med, *w_med, *m_med, *v_med)
    tiny_out = [res[1 + 4 * k:5 + 4 * k] for k in range(nt)]
    med_out = [res[1 + 4 * nt + 4 * a:5 + 4 * nt + 4 * a] for a in range(nm_)]
    return res[0], tiny_out, med_out


def _in_proj(x2, g_pre, w_in, bl, seg):
    t = x2.shape[0]
    tm = seg

    def body(x_ref, g_ref, w_ref, u_ref, zs_ref, q_ref, k_ref, v_ref, za_ref):
        xv = x_ref[...]
        r = lax.rsqrt(jnp.mean(xv * xv, axis=-1, keepdims=True) + EPS)
        hn = xv * r * g_ref[...]
        proj = _mm(hn, w_ref[...])
        u_ref[0] = proj[:, 0:512]
        zs_ref[...] = proj[:, 512:1024]
        q_ref[...] = proj[:, 1024:1536].astype(BF16)
        k_ref[...] = proj[:, 1536:1664].astype(BF16)
        v_ref[...] = proj[:, 1664:1792].astype(BF16)
        za_ref[...] = proj[:, 1792:2304]

    row = lambda w: pl.BlockSpec((tm, w), lambda i: (i, 0))
    return pl.pallas_call(
        body, name="in_proj", grid=(t // tm,),
        in_specs=[row(D_MODEL), _const_spec((1, D_MODEL)), _const_spec((D_MODEL, D_IN))],
        out_specs=(pl.BlockSpec((1, tm, D_SSM), lambda i: (i // N_SEG, 0, i % N_SEG)),
                   row(512), row(512), row(128), row(128), row(512)),
        out_shape=(jax.ShapeDtypeStruct((bl, seg, N_SEG * D_SSM), F32),
                   jax.ShapeDtypeStruct((t, 512), F32), jax.ShapeDtypeStruct((t, 512), BF16),
                   jax.ShapeDtypeStruct((t, 128), BF16), jax.ShapeDtypeStruct((t, 128), BF16),
                   jax.ShapeDtypeStruct((t, 512), F32)),
        compiler_params=_tc_params(("arbitrary",)),
    )(x2, g_pre, w_in)


def _ssm_prep(lam_re, lam_im, log_step, b_re, b_im, seg):
    def body(lr_ref, li_ref, ls_ref, br_ref, bi_ref, lrr_ref, lir_ref, lsr_ref,
             ar_ref, ai_ref, bbr_ref, bbi_ref, pr_ref, pi_ref):
        lr, li = lr_ref[...], li_ref[...]
        step = jnp.exp(ls_ref[...])
        mag = jnp.exp(lr * step)
        ar = mag * jnp.cos(li * step)
        ai = mag * jnp.sin(li * step)
        ar_ref[...] = ar
        ai_ref[...] = ai
        den = lr * lr + li * li
        cr = ((ar - 1.0) * lr + ai * li) / den
        ci = (ai * lr - (ar - 1.0) * li) / den
        br, bi = br_ref[...], bi_ref[...]
        bbr_ref[...] = cr * br - ci * bi
        bbi_ref[...] = cr * bi + ci * br
        stepr = jnp.exp(lsr_ref[...])
        k = (lax.broadcasted_iota(jnp.int32, (8, N_STATE), 0) + 1).astype(F32)
        magk = jnp.exp(k * (lrr_ref[...] * stepr))
        ang = k * (lir_ref[...] * stepr)
        pr_ref[0:8, :] = magk * jnp.cos(ang)
        pi_ref[0:8, :] = magk * jnp.sin(ang)
        n = 8
        while n < seg:
            tr, ti = pr_ref[n - 1:n, :], pi_ref[n - 1:n, :]
            xr, xi = pr_ref[0:n, :], pi_ref[0:n, :]
            pr_ref[n:2 * n, :] = xr * tr - xi * ti
            pi_ref[n:2 * n, :] = xr * ti + xi * tr
            n *= 2

    col = jax.ShapeDtypeStruct((N_STATE, 1), F32)
    mat = jax.ShapeDtypeStruct((N_STATE, SSM_P), F32)
    pw = jax.ShapeDtypeStruct((seg, N_STATE), F32)
    vm = pl.BlockSpec(memory_space=pltpu.VMEM)
    return pl.pallas_call(
        body, name="ssm_prep", out_shape=(col, col, mat, mat, pw, pw),
        in_specs=[vm] * 8, out_specs=(vm,) * 6,
    )(lam_re, lam_im, log_step, b_re, b_im, lam_re.reshape(1, N_STATE), lam_im.reshape(1, N_STATE),
      log_step.reshape(1, N_STATE))


def _scan_forward(xs, a_re, a_im, pw_re, pw_im, cs, seg):
    are = jnp.broadcast_to(a_re, (N_SEG, ST_T))
    aim = jnp.broadcast_to(a_im, (N_SEG, ST_T))

    def step(t, carry):
        xr, xi = carry
        r = pl.multiple_of(t * N_SEG, N_SEG)
        nr = are * xr - aim * xi + xs[pl.ds(r, N_SEG), 0:ST_T]
        ni = are * xi + aim * xr + xs[pl.ds(r, N_SEG), ST_T:2 * ST_T]
        xs[pl.ds(r, N_SEG), 0:ST_T] = nr
        xs[pl.ds(r, N_SEG), ST_T:2 * ST_T] = ni
        return nr, ni

    zero = jnp.zeros((N_SEG, ST_T), F32)
    fr, fi = lax.fori_loop(0, seg, step, (zero, zero), unroll=2)
    sr, si = pw_re[seg - 1:seg, :], pw_im[seg - 1:seg, :]
    cr = jnp.zeros((1, ST_T), F32)
    ci = jnp.zeros((1, ST_T), F32)
    cs[0:1, :] = cr
    cs[8:9, :] = ci
    for s in range(1, N_SEG):
        ncr = sr * cr - si * ci + fr[s - 1:s, :]
        nci = sr * ci + si * cr + fi[s - 1:s, :]
        cr, ci = ncr, nci
        cs[s:s + 1, :] = cr
        cs[8 + s:9 + s, :] = ci
    car, cai = cs[0:8, :], cs[8:16, :]

    def fix(t, _):
        r = pl.multiple_of(t * N_SEG, N_SEG)
        pr, pi = pw_re[pl.ds(t, 1), :], pw_im[pl.ds(t, 1), :]
        xs[pl.ds(r, N_SEG), 0:ST_T] = xs[pl.ds(r, N_SEG), 0:ST_T] + (pr * car - pi * cai)
        xs[pl.ds(r, N_SEG), ST_T:2 * ST_T] = xs[pl.ds(r, N_SEG), ST_T:2 * ST_T] + (pr * cai + pi * car)
        return 0

    lax.fori_loop(0, seg, fix, 0, unroll=2)


def _ssm_forward(u_perm, bcat, ccat, a_re, a_im, pw_re, pw_im, d_row, seg):
    bl, rows, _ = u_perm.shape

    def body(u_ref, b_ref, c_ref, ar_ref, ai_ref, pr_ref, pi_ref, d_ref, y_ref, xs, cs):
        u = u_ref[0]
        xs[...] = _mm(u, b_ref[0])
        _scan_forward(xs, ar_ref[...], ai_ref[...], pr_ref, pi_ref, cs, seg)
        y_ref[0] = _mm(xs[...], c_ref[0]) + d_ref[...] * u

    return pl.pallas_call(
        body, name="ssm_forward", grid=(bl, N_GT),
        in_specs=[pl.BlockSpec((1, rows, CH_T), lambda b, j: (b, 0, j)),
                  pl.BlockSpec((1, CH_T, 2 * ST_T), lambda b, j: (j, 0, 0)),
                  pl.BlockSpec((1, 2 * ST_T, CH_T), lambda b, j: (j, 0, 0)),
                  pl.BlockSpec((1, ST_T), lambda b, j: (0, j)), pl.BlockSpec((1, ST_T), lambda b, j: (0, j)),
                  pl.BlockSpec((seg, ST_T), lambda b, j: (0, j)), pl.BlockSpec((seg, ST_T), lambda b, j: (0, j)),
                  pl.BlockSpec((1, CH_T), lambda b, j: (0, j))],
        out_specs=pl.BlockSpec((1, rows, CH_T), lambda b, j: (b, 0, j)),
        out_shape=jax.ShapeDtypeStruct((bl, rows, D_SSM), F32),
        scratch_shapes=[pltpu.VMEM((rows, 2 * ST_T), F32), pltpu.VMEM((16, ST_T), F32)],
        compiler_params=_tc_params(("arbitrary", "arbitrary")),
    )(u_perm, bcat, ccat, a_re, a_im, pw_re, pw_im, d_row)


def _ssm_backward(u_perm, dy_perm, bcat, bcat_t, ccat_t, a_re, a_im, pw_re, pw_im, d_row, seg):
    bl, rows, _ = u_perm.shape

    def body(u_ref, dy_ref, b_ref, bt_ref, ct_ref, ar_ref, ai_ref, pr_ref, pi_ref, d_ref,
             du_ref, db_ref, dc_ref, dar_ref, dai_ref, dd_ref, xs, ls, cs, cl):
        b = pl.program_id(1)
        u = u_ref[0]
        dy = dy_ref[0]
        xs[...] = _mm(u, b_ref[0])
        _scan_forward(xs, ar_ref[...], ai_ref[...], pr_ref, pi_ref, cs, seg)
        ls[...] = _mm(dy, ct_ref[0])
        are = jnp.broadcast_to(ar_ref[...], (N_SEG, ST_T))
        aim = jnp.broadcast_to(ai_ref[...], (N_SEG, ST_T))

        def step(i, carry):
            lr, li = carry
            r = pl.multiple_of((seg - 1 - i) * N_SEG, N_SEG)
            nr = are * lr + aim * li + ls[pl.ds(r, N_SEG), 0:ST_T]
            ni = are * li - aim * lr + ls[pl.ds(r, N_SEG), ST_T:2 * ST_T]
            ls[pl.ds(r, N_SEG), 0:ST_T] = nr
            ls[pl.ds(r, N_SEG), ST_T:2 * ST_T] = ni
            return nr, ni

        zero = jnp.zeros((N_SEG, ST_T), F32)
        fr, fi = lax.fori_loop(0, seg, step, (zero, zero), unroll=2)
        sr, si = pr_ref[seg - 1:seg, :], pi_ref[seg - 1:seg, :]
        cr = jnp.zeros((1, ST_T), F32)
        ci = jnp.zeros((1, ST_T), F32)
        cl[7:8, :] = cr
        cl[15:16, :] = ci
        for s in range(N_SEG - 2, -1, -1):
            ncr = sr * cr + si * ci + fr[s + 1:s + 2, :]
            nci = sr * ci - si * cr + fi[s + 1:s + 2, :]
            cr, ci = ncr, nci
            cl[s:s + 1, :] = cr
            cl[8 + s:9 + s, :] = ci
        clr, cli = cl[0:8, :], cl[8:16, :]

        def fix(t, acc):
            dr, di = acc
            r = pl.multiple_of(t * N_SEG, N_SEG)
            pr, pi = pr_ref[pl.ds(seg - 1 - t, 1), :], pi_ref[pl.ds(seg - 1 - t, 1), :]
            lr = ls[pl.ds(r, N_SEG), 0:ST_T] + (pr * clr + pi * cli)
            li = ls[pl.ds(r, N_SEG), ST_T:2 * ST_T] + (pr * cli - pi * clr)
            ls[pl.ds(r, N_SEG), 0:ST_T] = lr
            ls[pl.ds(r, N_SEG), ST_T:2 * ST_T] = li
            rp = pl.multiple_of(jnp.maximum(t - 1, 0) * N_SEG, N_SEG)
            first = t == 0
            xpr = jnp.where(first, cs[0:8, :], xs[pl.ds(rp, N_SEG), 0:ST_T])
            xpi = jnp.where(first, cs[8:16, :], xs[pl.ds(rp, N_SEG), ST_T:2 * ST_T])
            return dr + (lr * xpr + li * xpi), di + (li * xpr - lr * xpi)

        dr, di = lax.fori_loop(0, seg, fix, (zero, zero), unroll=2)
        dar = jnp.sum(dr, axis=0, keepdims=True)
        dai = jnp.sum(di, axis=0, keepdims=True)
        lall = ls[...]
        du_ref[0] = _mm(lall, bt_ref[0]) + d_ref[...] * dy
        dbp = _mm_tn(u, lall)
        dcp = _mm_tn(dy, xs[...])
        ddp = jnp.sum(dy * u, axis=0, keepdims=True)

        @pl.when(b == 0)
        def _():
            db_ref[0] = dbp
            dc_ref[0] = dcp
            dar_ref[...] = dar
            dai_ref[...] = dai
            dd_ref[...] = ddp

        @pl.when(b != 0)
        def _():
            db_ref[0] += dbp
            dc_ref[0] += dcp
            dar_ref[...] += dar
            dai_ref[...] += dai
            dd_ref[...] += ddp

    tile3 = lambda r, c: pl.BlockSpec((1, r, c), lambda j, b: (j, 0, 0))
    lane = lambda r, c: pl.BlockSpec((r, c), lambda j, b: (0, j))
    act = pl.BlockSpec((1, rows, CH_T), lambda j, b: (b, 0, j))
    return pl.pallas_call(
        body, name="ssm_backward", grid=(N_GT, bl),
        in_specs=[act, act, tile3(CH_T, 2 * ST_T), tile3(2 * ST_T, CH_T), tile3(CH_T, 2 * ST_T),
                  lane(1, ST_T), lane(1, ST_T), lane(seg, ST_T), lane(seg, ST_T), lane(1, CH_T)],
        out_specs=(act, tile3(CH_T, 2 * ST_T), tile3(CH_T, 2 * ST_T), lane(1, ST_T), lane(1, ST_T), lane(1, CH_T)),
        out_shape=(jax.ShapeDtypeStruct((bl, rows, D_SSM), F32),
                   jax.ShapeDtypeStruct((N_GT, CH_T, 2 * ST_T), F32), jax.ShapeDtypeStruct((N_GT, CH_T, 2 * ST_T), F32),
                   jax.ShapeDtypeStruct((1, N_STATE), F32), jax.ShapeDtypeStruct((1, N_STATE), F32),
                   jax.ShapeDtypeStruct((1, D_SSM), F32)),
        scratch_shapes=[pltpu.VMEM((rows, 2 * ST_T), F32), pltpu.VMEM((rows, 2 * ST_T), F32),
                        pltpu.VMEM((16, ST_T), F32), pltpu.VMEM((16, ST_T), F32)],
        compiler_params=_tc_params(("arbitrary", "arbitrary")),
    )(u_perm, dy_perm, bcat, bcat_t, ccat_t, a_re, a_im, pw_re, pw_im, d_row)


def _ssm_param_grads(lam_re, lam_im, log_step, b_re, b_im, a_re, a_im, da_re, da_im, dbb_re, dbb_im):
    def body(lr_ref, li_ref, ls_ref, br_ref, bi_ref, ar_ref, ai_ref, gar_ref, gai_ref, gbr_ref, gbi_ref,
             dlr_ref, dli_ref, dls_ref, dbr_ref, dbi_ref):
        lr, li = lr_ref[...], li_ref[...]
        step = jnp.exp(ls_ref[...])
        ar, ai = ar_ref[...], ai_ref[...]
        den = lr * lr + li * li
        cr = ((ar - 1.0) * lr + ai * li) / den
        ci = (ai * lr - (ar - 1.0) * li) / den
        br, bi = br_ref[...], bi_ref[...]
        gbr, gbi = gbr_ref[...], gbi_ref[...]
        dbr_ref[...] = cr * gbr + ci * gbi
        dbi_ref[...] = cr * gbi - ci * gbr
        gcr = jnp.sum(br * gbr + bi * gbi, axis=-1, keepdims=True)
        gci = jnp.sum(br * gbi - bi * gbr, axis=-1, keepdims=True)
        ilr, ili = lr / den, -li / den
        gar = gar_ref[...] + (ilr * gcr + ili * gci)
        gai = gai_ref[...] + (ilr * gci - ili * gcr)
        qr, qi = cr * ilr - ci * ili, cr * ili + ci * ilr
        glr = -(qr * gcr + qi * gci)
        gli = -(qr * gci - qi * gcr)
        gwr = ar * gar + ai * gai
        gwi = ar * gai - ai * gar
        dlr_ref[...] = glr + step * gwr
        dli_ref[...] = gli + step * gwi
        dls_ref[...] = (lr * gwr + li * gwi) * step

    col = jax.ShapeDtypeStruct((N_STATE, 1), F32)
    mat = jax.ShapeDtypeStruct((N_STATE, SSM_P), F32)
    vm = pl.BlockSpec(memory_space=pltpu.VMEM)
    return pl.pallas_call(
        body, name="ssm_param_grads", out_shape=(col, col, col, mat, mat),
        in_specs=[vm] * 11, out_specs=(vm,) * 5,
    )(lam_re, lam_im, log_step, b_re, b_im, a_re, a_im, da_re, da_im, dbb_re, dbb_im)


def _att_bias_mask(first_block):
    qi = lax.broadcasted_iota(jnp.int32, (ATT_BLOCK, 2 * ATT_BLOCK), 0)
    si = lax.broadcasted_iota(jnp.int32, (ATT_BLOCK, 2 * ATT_BLOCK), 1)
    dist = qi + ATT_BLOCK - si
    valid = (dist >= 0) & (dist < ATT_BLOCK) & ((si >= ATT_BLOCK) | jnp.logical_not(first_block))
    return dist.astype(F32), valid


def _attn_forward(q, k, v, sinks, bl, nb):
    t = q.shape[0]

    def body(sink_ref, q_ref, kp_ref, kc_ref, vp_ref, vc_ref, o_ref, lse_ref):
        i = pl.program_id(1)
        dist, valid = _att_bias_mask(i == 0)
        kk = jnp.concatenate([kp_ref[...], kc_ref[...]], axis=0)
        vv = jnp.concatenate([vp_ref[...], vc_ref[...]], axis=0)
        qv = q_ref[...]
        for h in range(N_HEADS):
            kv = h // Q_PER_KV
            slope = 2.0 ** (-(h + 1))
            qh = qv[:, h * HEAD_DIM:(h + 1) * HEAD_DIM]
            kh = kk[:, kv * HEAD_DIM:(kv + 1) * HEAD_DIM]
            vh = vv[:, kv * HEAD_DIM:(kv + 1) * HEAD_DIM]
            s = _mm_nt(qh, kh) * ATT_SCALE - slope * dist
            s = jnp.where(valid, s, NEG_BIG)
            sink = sink_ref[h]
            m = jnp.maximum(jnp.max(s, axis=-1, keepdims=True), sink)
            e = jnp.exp(s - m)
            den = jnp.sum(e, axis=-1, keepdims=True) + jnp.exp(sink - m)
            p = e / den
            o_ref[:, h * HEAD_DIM:(h + 1) * HEAD_DIM] = _mm(p, vh)
            lse_ref[:, h:h + 1] = m + jnp.log(den)

    cur = lambda w: pl.BlockSpec((ATT_BLOCK, w), lambda b, i: (b * nb + i, 0))
    prev = lambda w: pl.BlockSpec((ATT_BLOCK, w), lambda b, i: (b * nb + jnp.maximum(i - 1, 0), 0))
    return pl.pallas_call(
        body, name="attn_forward", grid=(bl, nb),
        in_specs=[pl.BlockSpec(memory_space=pltpu.SMEM), cur(512), prev(128), cur(128), prev(128), cur(128)],
        out_specs=(cur(512), cur(N_HEADS)),
        out_shape=(jax.ShapeDtypeStruct((t, D_ATTN), F32), jax.ShapeDtypeStruct((t, N_HEADS), F32)),
        compiler_params=_tc_params(("arbitrary", "arbitrary")),
    )(sinks, q, k, k, v, v)


def _attn_backward(q, k, v, o, do, lse, sinks, bl, nb):
    t = q.shape[0]

    def body(sink_ref, qc_ref, qn_ref, kp_ref, kc_ref, vp_ref, vc_ref, oc_ref, on_ref, doc_ref, don_ref,
             lc_ref, ln_ref, dq_ref, dk_ref, dv_ref, ds_ref):
        b, i = pl.program_id(0), pl.program_id(1)
        dist, valid = _att_bias_mask(i == 0)
        has_next = i + 1 < nb
        dist_n = dist[:, 0:ATT_BLOCK]
        valid_n = (dist_n < ATT_BLOCK) & has_next
        kk = jnp.concatenate([kp_ref[...], kc_ref[...]], axis=0)
        vv = jnp.concatenate([vp_ref[...], vc_ref[...]], axis=0)
        qc, qn = qc_ref[...], qn_ref[...]
        oc, on = oc_ref[...], on_ref[...]
        doc, don = doc_ref[...], don_ref[...]
        lc, ln = lc_ref[...], ln_ref[...]
        dsink_cols = []
        for kv in range(KV_HEADS):
            kh = kk[:, kv * HEAD_DIM:(kv + 1) * HEAD_DIM]
            vh = vv[:, kv * HEAD_DIM:(kv + 1) * HEAD_DIM]
            khc, vhc = kh[ATT_BLOCK:, :], vh[ATT_BLOCK:, :]
            dk_acc = jnp.zeros((ATT_BLOCK, HEAD_DIM), F32)
            dv_acc = jnp.zeros((ATT_BLOCK, HEAD_DIM), F32)
            for g in range(Q_PER_KV):
                h = kv * Q_PER_KV + g
                hs = slice(h * HEAD_DIM, (h + 1) * HEAD_DIM)
                slope = 2.0 ** (-(h + 1))
                qh, doh = qc[:, hs], doc[:, hs]
                delta = jnp.sum(doh * oc[:, hs], axis=-1, keepdims=True)
                lse_h = lc[:, h:h + 1]
                s = _mm_nt(qh, kh) * ATT_SCALE - slope * dist
                p = jnp.where(valid, jnp.exp(s - lse_h), 0.0)
                dp = _mm_nt(doh, vh)
                dsc = p * (dp - delta)
                dq_ref[:, hs] = _mm(dsc, kh) * ATT_SCALE
                dk_acc += _mm_tn(dsc[:, ATT_BLOCK:], qh)
                dv_acc += _mm_tn(p[:, ATT_BLOCK:], doh)
                dsink_cols.append(-jnp.sum(jnp.exp(sink_ref[h] - lse_h) * delta, axis=0, keepdims=True))
                qh2, doh2 = qn[:, hs], don[:, hs]
                delta2 = jnp.sum(doh2 * on[:, hs], axis=-1, keepdims=True)
                s2 = _mm_nt(qh2, khc) * ATT_SCALE - slope * dist_n
                p2 = jnp.where(valid_n, jnp.exp(s2 - ln[:, h:h + 1]), 0.0)
                dp2 = _mm_nt(doh2, vhc)
                ds2 = p2 * (dp2 - delta2)
                dk_acc += _mm_tn(ds2, qh2)
                dv_acc += _mm_tn(p2, doh2)
            dk_ref[:, kv * HEAD_DIM:(kv + 1) * HEAD_DIM] = dk_acc * ATT_SCALE
            dv_ref[:, kv * HEAD_DIM:(kv + 1) * HEAD_DIM] = dv_acc
        dsink = jnp.concatenate(dsink_cols, axis=1)

        @pl.when((b == 0) & (i == 0))
        def _():
            ds_ref[...] = dsink

        @pl.when((b != 0) | (i != 0))
        def _():
            ds_ref[...] += dsink

    cur = lambda w: pl.BlockSpec((ATT_BLOCK, w), lambda b, i: (b * nb + i, 0))
    prev = lambda w: pl.BlockSpec((ATT_BLOCK, w), lambda b, i: (b * nb + jnp.maximum(i - 1, 0), 0))
    nxt = lambda w: pl.BlockSpec((ATT_BLOCK, w), lambda b, i: (b * nb + jnp.minimum(i + 1, nb - 1), 0))
    return pl.pallas_call(
        body, name="attn_backward", grid=(bl, nb),
        in_specs=[pl.BlockSpec(memory_space=pltpu.SMEM), cur(512), nxt(512), prev(128), cur(128), prev(128), cur(128),
                  cur(512), nxt(512), cur(512), nxt(512), cur(N_HEADS), nxt(N_HEADS)],
        out_specs=(cur(512), cur(128), cur(128), pl.BlockSpec((1, N_HEADS), lambda b, i: (0, 0))),
        out_shape=(jax.ShapeDtypeStruct((t, D_ATTN), F32), jax.ShapeDtypeStruct((t, 128), F32),
                   jax.ShapeDtypeStruct((t, 128), F32), jax.ShapeDtypeStruct((1, N_HEADS), F32)),
        compiler_params=_tc_params(("arbitrary", "arbitrary")),
    )(sinks, q, q, k, k, v, v, o, o, do, do, lse, lse)


def _mix_forward_backward(x2, y_perm, z_ssm, attn, z_attn, p2, target2, w_glu, b_glu, w_out, g_post, w_gate, b_gate,
                          w_proj, bl, seg):
    t = x2.shape[0]
    tm = seg

    def body(x_ref, y_ref, zs_ref, at_ref, za_ref, p_ref, tg_ref,
             wglu_ref, bglu_ref, wout_ref, gpost_ref, wgate_ref, bgate_ref, wproj_ref,
             loss_ref, dh1_ref, dy_ref, dzs_ref, dat_ref, dza_ref,
             dwglu_ref, dbglu_ref, dwout_ref, dgpost_ref, dwgate_ref, dbgate_ref, dwproj_ref):
        i = pl.program_id(0)
        y = y_ref[0]
        u3 = GELU_C * (y + GELU_K * y * y * y)
        th = jnp.tanh(u3)
        gl = 0.5 * y * (1.0 + th)
        a = _mm(gl, wglu_ref[...]) + bglu_ref[...]
        sa = _sigmoid(a)
        glu = gl * sa
        zs = zs_ref[...]
        sgs = _sigmoid(zs)
        ssm_out = glu * (zs * sgs)
        za = za_ref[...]
        sga = _sigmoid(za)
        at = at_ref[...]
        attn_out = at * (za * sga)
        cat = jnp.concatenate([ssm_out, attn_out], axis=-1).astype(BF16)
        mixed = _mm(cat, wout_ref[...])
        r2 = lax.rsqrt(jnp.mean(mixed * mixed, axis=-1, keepdims=True) + EPS)
        nhat = mixed * r2
        gpost = gpost_ref[...]
        h1 = x_ref[...] + nhat * gpost
        gate = _sigmoid(_mm(h1, wgate_ref[...]) + bgate_ref[...])
        pv = p_ref[...]
        pp = _mm(pv, wproj_ref[...])
        h2 = h1 + gate * pp
        err = h2 - tg_ref[...]
        loss_part = jnp.sum(jnp.sum(err * err, axis=-1, keepdims=True), axis=0, keepdims=True) * (0.5 / D_MODEL)
        dh2 = err * (1.0 / D_MODEL)
        dgp = dh2 * pp * gate * (1.0 - gate)
        dpp = dh2 * gate
        dh1 = dh2 + _mm_nt(dgp, wgate_ref[...])
        dh1_ref[...] = dh1
        dnhat = dh1 * gpost
        dmixed = r2 * (dnhat - nhat * jnp.mean(dnhat * nhat, axis=-1, keepdims=True))
        dcat = _mm_nt(dmixed, wout_ref[...])
        dso, dao = dcat[:, 0:D_SSM], dcat[:, D_SSM:]
        dat_ref[...] = dao * (za * sga)
        dza_ref[...] = dao * at * (sga * (1.0 + za * (1.0 - sga)))
        dzs_ref[...] = dso * glu * (sgs * (1.0 + zs * (1.0 - sgs)))
        dglu = dso * (zs * sgs)
        da = dglu * gl * sa * (1.0 - sa)
        dgl = dglu * sa + _mm_nt(da, wglu_ref[...])
        dgelu = 0.5 * (1.0 + th) + 0.5 * y * (1.0 - th * th) * (GELU_C * (1.0 + 3.0 * GELU_K * y * y))
        dy_ref[0] = dgl * dgelu
        parts = (
            (dwglu_ref, _mm_tn(gl, da)), (dbglu_ref, jnp.sum(da, axis=0, keepdims=True)),
            (dwout_ref, _mm_tn(cat, dmixed)), (dgpost_ref, jnp.sum(dh1 * nhat, axis=0, keepdims=True)),
            (dwgate_ref, _mm_tn(h1, dgp)), (dbgate_ref, jnp.sum(dgp, axis=0, keepdims=True)),
            (dwproj_ref, _mm_tn(pv, dpp)), (loss_ref, loss_part),
        )

        @pl.when(i == 0)
        def _():
            for ref, val in parts:
                ref[...] = val

        @pl.when(i != 0)
        def _():
            for ref, val in parts:
                ref[...] += val

    row = lambda w: pl.BlockSpec((tm, w), lambda i: (i, 0))
    perm = pl.BlockSpec((1, tm, D_SSM), lambda i: (i // N_SEG, 0, i % N_SEG))
    perm_shape = jax.ShapeDtypeStruct((bl, seg, N_SEG * D_SSM), F32)
    acc = lambda r, c: (_const_spec((r, c)), jax.ShapeDtypeStruct((r, c), F32))
    accs = [acc(D_SSM, D_SSM), acc(1, D_SSM), acc(D_MODEL, D_MODEL), acc(1, D_MODEL), acc(D_MODEL, D_MODEL),
            acc(1, D_MODEL), acc(D_PLE, D_MODEL)]
    return pl.pallas_call(
        body, name="mix_forward_backward", grid=(t // tm,),
        in_specs=[row(D_MODEL), perm, row(512), row(512), row(512), row(D_PLE), row(D_MODEL),
                  _const_spec((D_SSM, D_SSM)), _const_spec((1, D_SSM)), _const_spec((D_MODEL, D_MODEL)),
                  _const_spec((1, D_MODEL)), _const_spec((D_MODEL, D_MODEL)), _const_spec((1, D_MODEL)),
                  _const_spec((D_PLE, D_MODEL))],
        out_specs=(_const_spec((1, 1)), row(D_MODEL), perm, row(512), row(512), row(512)) + tuple(a[0] for a in accs),
        out_shape=(jax.ShapeDtypeStruct((1, 1), F32), jax.ShapeDtypeStruct((t, D_MODEL), F32), perm_shape,
                   jax.ShapeDtypeStruct((t, 512), F32), jax.ShapeDtypeStruct((t, 512), F32),
                   jax.ShapeDtypeStruct((t, 512), F32)) + tuple(a[1] for a in accs),
        compiler_params=_tc_params(("arbitrary",)),
    )(x2, y_perm, z_ssm, attn, z_attn, p2, target2, w_glu, b_glu, w_out, g_post, w_gate, b_gate, w_proj)


def _in_backward(x2, dh1, du_perm, dz_ssm, dq, dk, dv, dz_attn, g_pre, w_in, bl, seg):
    t = x2.shape[0]
    tm = seg

    def body(x_ref, dh1_ref, du_ref, dzs_ref, dq_ref, dk_ref, dv_ref, dza_ref, g_ref, w_ref,
             gx_ref, dw_ref, dg_ref):
        i = pl.program_id(0)
        xv = x_ref[...]
        r = lax.rsqrt(jnp.mean(xv * xv, axis=-1, keepdims=True) + EPS)
        xhat = xv * r
        g = g_ref[...]
        hn = (xhat * g).astype(BF16)
        dproj = jnp.concatenate([du_ref[0].astype(BF16), dzs_ref[...].astype(BF16), dq_ref[...].astype(BF16),
                                 dk_ref[...].astype(BF16), dv_ref[...].astype(BF16), dza_ref[...].astype(BF16)],
                                axis=-1)
        dhn = _mm_nt(dproj, w_ref[...])
        dxhat = dhn * g
        gx_ref[...] = dh1_ref[...] + r * (dxhat - xhat * jnp.mean(dxhat * xhat, axis=-1, keepdims=True))
        dwp = _mm_tn(hn, dproj)
        dgp = jnp.sum(dhn * xhat, axis=0, keepdims=True)

        @pl.when(i == 0)
        def _():
            dw_ref[...] = dwp
            dg_ref[...] = dgp

        @pl.when(i != 0)
        def _():
            dw_ref[...] += dwp
            dg_ref[...] += dgp

    row = lambda w: pl.BlockSpec((tm, w), lambda i: (i, 0))
    perm = pl.BlockSpec((1, tm, D_SSM), lambda i: (i // N_SEG, 0, i % N_SEG))
    return pl.pallas_call(
        body, name="in_backward", grid=(t // tm,),
        in_specs=[row(D_MODEL), row(D_MODEL), perm, row(512), row(512), row(128), row(128), row(512),
                  _const_spec((1, D_MODEL)), _const_spec((D_MODEL, D_IN))],
        out_specs=(row(D_MODEL), _const_spec((D_MODEL, D_IN)), _const_spec((1, D_MODEL))),
        out_shape=(jax.ShapeDtypeStruct((t, D_MODEL), F32), jax.ShapeDtypeStruct((D_MODEL, D_IN), F32),
                   jax.ShapeDtypeStruct((1, D_MODEL), F32)),
        compiler_params=_tc_params(("arbitrary",)),
    )(x2, dh1, du_perm, dz_ssm, dq, dk, dv, dz_attn, g_pre, w_in)


def _block_diag(t):
    a, b = t.shape[1], t.shape[2]
    eye = jnp.eye(G_TILE, dtype=t.dtype)
    t = t.reshape(N_GT, G_TILE, a, 1, b) * eye[None, :, None, :, None]
    return t.reshape(N_GT, G_TILE * a, G_TILE * b)


def _diag_blocks(m, a, b):
    m = m.reshape(N_GT, G_TILE, a, G_TILE, b)
    return jnp.einsum("tgagb->tgab", m).reshape(SSM_G, a, b)


def _local_step(x, p, target, pre_norm_g, w_in, ssm_lam_re, ssm_lam_im, ssm_log_step, ssm_b_re, ssm_b_im, ssm_c_re,
                ssm_c_im, ssm_d, w_glu, ssm_b_glu, attn_sinks, w_out, post_norm_g, w_proj, w_gate, pl_b_gate):
    bl, seq, _ = x.shape
    seg = seq // N_SEG
    nb = seq // ATT_BLOCK
    t = bl * seq
    x2 = x.reshape(t, D_MODEL)
    p2 = p.reshape(t, D_PLE)
    tg2 = target.reshape(t, D_MODEL)

    lam_re = ssm_lam_re.reshape(N_STATE, 1)
    lam_im = ssm_lam_im.reshape(N_STATE, 1)
    log_step = jnp.broadcast_to(ssm_log_step.reshape(SSM_G, 1), (SSM_G, SSM_N)).reshape(N_STATE, 1)
    b_re = ssm_b_re.reshape(N_STATE, SSM_P)
    b_im = ssm_b_im.reshape(N_STATE, SSM_P)
    a_re, a_im, bb_re, bb_im, pw_re, pw_im = _ssm_prep(lam_re, lam_im, log_step, b_re, b_im, seg)
    bb_re_t = jnp.swapaxes(bb_re.reshape(SSM_G, SSM_N, SSM_P), 1, 2)
    bb_im_t = jnp.swapaxes(bb_im.reshape(SSM_G, SSM_N, SSM_P), 1, 2)
    bcat = jnp.concatenate([_block_diag(bb_re_t), _block_diag(bb_im_t)], axis=-1).astype(BF16)
    ccat_t = jnp.concatenate([_block_diag(ssm_c_re.reshape(SSM_G, SSM_P, SSM_N)),
                              -_block_diag(ssm_c_im.reshape(SSM_G, SSM_P, SSM_N))], axis=-1).astype(BF16)
    bcat_t = jnp.swapaxes(bcat, 1, 2)
    ccat = jnp.swapaxes(ccat_t, 1, 2)
    a_re_row, a_im_row = a_re.reshape(1, N_STATE), a_im.reshape(1, N_STATE)
    d_row = ssm_d.reshape(1, D_SSM)

    u_perm, z_ssm, q, k, v, z_attn = _in_proj(x2, pre_norm_g.reshape(1, D_MODEL), w_in, bl, seg)
    u_perm = u_perm.reshape(bl, seq, D_SSM)
    y_perm = _ssm_forward(u_perm, bcat, ccat, a_re_row, a_im_row, pw_re, pw_im, d_row, seg)
    sinks = attn_sinks.reshape(N_HEADS)
    attn, lse = _attn_forward(q, k, v, sinks, bl, nb)
    (loss, dh1, dy_perm, dz_ssm, dattn, dz_attn, d_w_glu, d_b_glu, d_w_out, d_g_post, d_w_gate, d_b_gate,
     d_w_proj) = _mix_forward_backward(
        x2, y_perm.reshape(bl, seg, N_SEG * D_SSM), z_ssm, attn, z_attn, p2, tg2, w_glu,
        ssm_b_glu.reshape(1, D_SSM), w_out, post_norm_g.reshape(1, D_MODEL), w_gate, pl_b_gate.reshape(1, D_MODEL),
        w_proj, bl, seg)
    dq, dk, dv, d_sinks = _attn_backward(q, k, v, attn, dattn, lse, sinks, bl, nb)
    du_perm, d_bcat, d_ccat_t, da_re, da_im, d_d = _ssm_backward(
        u_perm, dy_perm.reshape(bl, seq, D_SSM), bcat, bcat_t, ccat_t, a_re_row, a_im_row, pw_re, pw_im, d_row, seg)
    grad_x, d_w_in, d_g_pre = _in_backward(
        x2, dh1, du_perm.reshape(bl, seg, N_SEG * D_SSM), dz_ssm, dq, dk, dv, dz_attn,
        pre_norm_g.reshape(1, D_MODEL), w_in, bl, seg)
    dbb_re = jnp.swapaxes(_diag_blocks(d_bcat[:, :, 0:ST_T], SSM_P, SSM_N), 1, 2).reshape(N_STATE, SSM_P)
    dbb_im = jnp.swapaxes(_diag_blocks(d_bcat[:, :, ST_T:], SSM_P, SSM_N), 1, 2).reshape(N_STATE, SSM_P)
    d_lam_re, d_lam_im, d_ls, d_b_re, d_b_im = _ssm_param_grads(
        lam_re, lam_im, log_step, b_re, b_im, a_re, a_im, da_re.reshape(N_STATE, 1), da_im.reshape(N_STATE, 1),
        dbb_re, dbb_im)
    grads = {
        "pre_norm_g": d_g_pre, "w_in": d_w_in, "ssm_lam_re": d_lam_re, "ssm_lam_im": d_lam_im,
        "ssm_log_step": jnp.sum(d_ls.reshape(SSM_G, SSM_N), axis=-1), "ssm_b_re": d_b_re, "ssm_b_im": d_b_im,
        "ssm_c_re": _diag_blocks(d_ccat_t[:, :, 0:ST_T], SSM_P, SSM_N),
        "ssm_c_im": -_diag_blocks(d_ccat_t[:, :, ST_T:], SSM_P, SSM_N),
        "ssm_d": d_d, "ssm_w_glu": d_w_glu, "ssm_b_glu": d_b_glu, "attn_sinks": d_sinks, "w_out": d_w_out,
        "post_norm_g": d_g_post, "pl_w_proj": d_w_proj, "pl_w_gate": d_w_gate, "pl_b_gate": d_b_gate,
    }
    return loss, grad_x.reshape(bl, seq, D_MODEL), grads


BIG_NAMES = ("w_in", "w_out", "pl_w_gate", "pl_w_proj", "ssm_w_glu")
COL_SHARDED = {"w_in": D_IN // N_DEV, "pl_w_proj": D_MODEL // N_DEV}
WEIGHT_NAMES = ("pre_norm_g", "w_in", "ssm_lam_re", "ssm_lam_im", "ssm_log_step", "ssm_b_re", "ssm_b_im", "ssm_c_re",
                "ssm_c_im", "ssm_d", "ssm_w_glu", "ssm_b_glu", "attn_sinks", "w_out", "post_norm_g", "pl_w_proj",
                "pl_w_gate", "pl_b_gate")


def _gathered_to_full(name, g):
    _, rows, cols = g.shape
    if name in COL_SHARDED:
        return jnp.swapaxes(g, 0, 1).reshape(rows, N_DEV * cols)
    return g.reshape(N_DEV * rows, cols)


def _full_to_owned(name, full):
    if name in COL_SHARDED:
        return jnp.swapaxes(full.reshape(full.shape[0], N_DEV, COL_SHARDED[name]), 0, 1)
    return full.reshape(N_DEV, full.shape[0] // N_DEV, full.shape[1])


def kernel(x, p, pre_norm_g, w_in, ssm_lam_re, ssm_lam_im, ssm_log_step, ssm_b_re, ssm_b_im, ssm_c_re, ssm_c_im, ssm_d, ssm_w_glu, ssm_b_glu, attn_sinks, w_out, post_norm_g, pl_w_proj, pl_w_gate, pl_b_gate, loss_target, m_pre_norm_g, m_w_in, m_ssm_lam_re, m_ssm_lam_im, m_ssm_log_step, m_ssm_b_re, m_ssm_b_im, m_ssm_c_re, m_ssm_c_im, m_ssm_d, m_ssm_w_glu, m_ssm_b_glu, m_attn_sinks, m_w_out, m_post_norm_g, m_pl_w_proj, m_pl_w_gate, m_pl_b_gate, v_pre_norm_g, v_w_in, v_ssm_lam_re, v_ssm_lam_im, v_ssm_log_step, v_ssm_b_re, v_ssm_b_im, v_ssm_c_re, v_ssm_c_im, v_ssm_d, v_ssm_w_glu, v_ssm_b_glu, v_attn_sinks, v_w_out, v_post_norm_g, v_pl_w_proj, v_pl_w_gate, v_pl_b_gate):
    w = dict(pre_norm_g=pre_norm_g, w_in=w_in, ssm_lam_re=ssm_lam_re, ssm_lam_im=ssm_lam_im, ssm_log_step=ssm_log_step,
             ssm_b_re=ssm_b_re, ssm_b_im=ssm_b_im, ssm_c_re=ssm_c_re, ssm_c_im=ssm_c_im, ssm_d=ssm_d, ssm_w_glu=ssm_w_glu,
             ssm_b_glu=ssm_b_glu, attn_sinks=attn_sinks, w_out=w_out, post_norm_g=post_norm_g, pl_w_proj=pl_w_proj,
             pl_w_gate=pl_w_gate, pl_b_gate=pl_b_gate)
    m = dict(pre_norm_g=m_pre_norm_g, w_in=m_w_in, ssm_lam_re=m_ssm_lam_re, ssm_lam_im=m_ssm_lam_im,
             ssm_log_step=m_ssm_log_step, ssm_b_re=m_ssm_b_re, ssm_b_im=m_ssm_b_im, ssm_c_re=m_ssm_c_re,
             ssm_c_im=m_ssm_c_im, ssm_d=m_ssm_d, ssm_w_glu=m_ssm_w_glu, ssm_b_glu=m_ssm_b_glu, attn_sinks=m_attn_sinks,
             w_out=m_w_out, post_norm_g=m_post_norm_g, pl_w_proj=m_pl_w_proj, pl_w_gate=m_pl_w_gate,
             pl_b_gate=m_pl_b_gate)
    v = dict(pre_norm_g=v_pre_norm_g, w_in=v_w_in, ssm_lam_re=v_ssm_lam_re, ssm_lam_im=v_ssm_lam_im,
             ssm_log_step=v_ssm_log_step, ssm_b_re=v_ssm_b_re, ssm_b_im=v_ssm_b_im, ssm_c_re=v_ssm_c_re,
             ssm_c_im=v_ssm_c_im, ssm_d=v_ssm_d, ssm_w_glu=v_ssm_w_glu, ssm_b_glu=v_ssm_b_glu, attn_sinks=v_attn_sinks,
             w_out=v_w_out, post_norm_g=v_post_norm_g, pl_w_proj=v_pl_w_proj, pl_w_gate=v_pl_w_gate,
             pl_b_gate=v_pl_b_gate)
    me = _slot(lax.axis_index("x"), lax.axis_index("y"), lax.axis_index("c"))

    gathered = _allgather_weights([w[n][0] for n in BIG_NAMES])
    full = {n: _gathered_to_full(n, g) for n, g in zip(BIG_NAMES, gathered)}

    loss, grad_x, grads = _local_step(
        x, p[0], loss_target, pre_norm_g[0], full["w_in"], ssm_lam_re[0], ssm_lam_im[0], ssm_log_step[0], ssm_b_re[0],
        ssm_b_im[0], ssm_c_re[0], ssm_c_im[0], ssm_d[0], full["ssm_w_glu"], ssm_b_glu[0], attn_sinks[0],
        full["w_out"], post_norm_g[0], full["pl_w_proj"], full["pl_w_gate"], pl_b_gate[0])

    owned = [_full_to_owned(n, grads[n]) for n in BIG_NAMES]
    big = _reduce_update_big(
        [o.astype(BF16) for o in owned], [lax.dynamic_index_in_dim(o, me, axis=0, keepdims=False) for o in owned],
        [w[n][0] for n in BIG_NAMES], [m[n][0] for n in BIG_NAMES], [v[n][0] for n in BIG_NAMES])
    tiny_form = lambda d: [d[n].reshape(rows, cols) for n, rows, cols in TINY]
    med_form = lambda d: [d[n].reshape(N_DEV, rows // N_DEV, cols) for n, rows, cols in MEDIUM]
    loss, tiny, med = _reduce_update_small(loss, tiny_form(grads), tiny_form(w), tiny_form(m), tiny_form(v),
                                           med_form(grads), med_form(w), med_form(m), med_form(v))
    vals = {n: r for n, r in zip(BIG_NAMES, big)}
    vals.update({n: r for (n, _, _), r in zip(TINY, tiny)})
    vals.update({n: r for (n, _, _), r in zip(MEDIUM, med)})
    results = [[vals[n][kind].reshape(w[n].shape) for n in WEIGHT_NAMES] for kind in range(4)]
    return (loss.reshape(()), grad_x, *results[0], *results[1], *results[2], *results[3])
```

```python
import functools
import math

import jax
import jax.numpy as jnp
from jax import lax
from jax.experimental import pallas as pl
from jax.experimental.pallas import tpu as pltpu

F32 = jnp.float32
BF16 = jnp.bfloat16

D_MODEL = 1024
D_SSM = 512
D_ATTN = 512
SSM_P = 16
SSM_G = 32
SSM_N = 64
N_HEADS = 8
KV_HEADS = 2
Q_PER_KV = 4
HEAD_DIM = 64
ATT_BLOCK = 128
D_PLE = 256
D_IN = 2304
EPS = 1e-6
N_DEV = 8
N_SEG = 8
G_TILE = 8
N_GT = SSM_G // G_TILE
CH_T = G_TILE * SSM_P
ST_T = G_TILE * SSM_N
N_STATE = SSM_G * SSM_N
LANES = 128
VMEM_LIMIT = 60 * 1024 * 1024

ADAM_LR = 0.001
ADAM_B1 = 0.9
ADAM_B2 = 0.999
ADAM_EPS = 1e-08
ADAM_WD = 0.01
ADAM_STEP = 10

GELU_C = math.sqrt(2.0 / math.pi)
GELU_K = 0.044715
ATT_SCALE = 1.0 / math.sqrt(HEAD_DIM)
NEG_BIG = -1e30


def _mm(a, b):
    return jnp.dot(a.astype(BF16), b.astype(BF16), preferred_element_type=F32)


def _mm_nt(a, b):
    return lax.dot_general(a.astype(BF16), b.astype(BF16), (((1,), (1,)), ((), ())), preferred_element_type=F32)


def _mm_tn(a, b):
    return lax.dot_general(a.astype(BF16), b.astype(BF16), (((0,), (0,)), ((), ())), preferred_element_type=F32)


def _sigmoid(x):
    return 1.0 / (1.0 + jnp.exp(-x))


def _tc_params(sem):
    return pltpu.CompilerParams(dimension_semantics=sem, vmem_limit_bytes=VMEM_LIMIT)


def _const_spec(shape):
    nd = len(shape)
    return pl.BlockSpec(shape, lambda *_: (0,) * nd)


def _mesh_pos():
    return lax.axis_index("x"), lax.axis_index("y"), lax.axis_index("c")


ROW_CHUNK = 64


def _row_chunks(nrows, fn):
    def step(i, carry):
        fn(pl.ds(pl.multiple_of(i * ROW_CHUNK, ROW_CHUNK), ROW_CHUNK))
        return carry

    lax.fori_loop(0, nrows // ROW_CHUNK, step, 0)


def _slot(px, py, pc):
    return 4 * px + 2 * py + pc


def _allgather_weights(shards):
    n = len(shards)

    def body(*refs):
        srcs, outs, (send_sems, recv_sems) = refs[:n], refs[n:2 * n], refs[2 * n:]
        x, y, c = _mesh_pos()
        me, sibling = (x, y, c), (x, y, 1 - c)
        chips = [(1 - x, y), (x, 1 - y), (1 - x, 1 - y)]

        def copy(a, k, block, to):
            blk = outs[a].at[_slot(*block)]
            return pltpu.make_async_remote_copy(
                src_ref=blk, dst_ref=blk, send_sem=send_sems.at[7 * a + k], recv_sem=recv_sems.at[7 * a + k],
                device_id=to, device_id_type=pl.DeviceIdType.MESH)

        sends = []
        for a in range(n):
            mine = outs[a].at[_slot(*me)]

            def cast(r, mine=mine, src=srcs[a]):
                mine[r, :] = src[r, :].astype(BF16)

            _row_chunks(srcs[a].shape[0], cast)
            first = [copy(a, 0, me, sibling)] + [copy(a, 1 + j, me, (*chip, c)) for j, chip in enumerate(chips)]
            for cp in first:
                cp.start()
            sends += first
        for a in range(n):
            for j, chip in enumerate(chips):
                copy(a, 1 + j, (*chip, c), me).wait_recv()
                fwd = copy(a, 4 + j, (*chip, c), sibling)
                fwd.start()
                sends.append(fwd)
        for a in range(n):
            copy(a, 0, sibling, me).wait_recv()
            for j, chip in enumerate(chips):
                copy(a, 4 + j, (*chip, 1 - c), me).wait_recv()
        for cp in sends:
            cp.wait_send()

    vm = pl.BlockSpec(memory_space=pltpu.VMEM)
    return pl.pallas_call(
        body, name="allgather_weights",
        out_shape=tuple(jax.ShapeDtypeStruct((N_DEV,) + s.shape, BF16) for s in shards),
        in_specs=[vm] * n, out_specs=(vm,) * n,
        scratch_shapes=[pltpu.SemaphoreType.DMA((7 * n,)), pltpu.SemaphoreType.DMA((7 * n,))],
        compiler_params=pltpu.CompilerParams(vmem_limit_bytes=VMEM_LIMIT),
    )(*shards)


def _adamw(w, g, m, v):
    m = ADAM_B1 * m + (1.0 - ADAM_B1) * g
    v = ADAM_B2 * v + (1.0 - ADAM_B2) * (g * g)
    m_hat = m / (1.0 - ADAM_B1 ** ADAM_STEP)
    v_hat = v / (1.0 - ADAM_B2 ** ADAM_STEP)
    delta = -ADAM_LR * (m_hat / (jnp.sqrt(v_hat) + ADAM_EPS) + ADAM_WD * w)
    return delta, m, v


def _remote(src, dst, send_sems, recv_sems, k, to):
    return pltpu.make_async_remote_copy(src_ref=src, dst_ref=dst, send_sem=send_sems.at[k], recv_sem=recv_sems.at[k],
                                        device_id=to, device_id_type=pl.DeviceIdType.MESH)


def _reduce_big(g16, gown):
    n = len(g16)

    def body(*refs):
        g16_r, go_r, outs = (refs[i * n:(i + 1) * n] for i in range(3))
        send2, recv1, recv2 = (refs[3 * n + i * n:3 * n + (i + 1) * n] for i in range(3))
        s_send, s_recv = refs[6 * n:]
        x, y, c = _mesh_pos()
        sibling = (x, y, 1 - c)
        chips = [(1 - x, y), (x, 1 - y), (1 - x, 1 - y)]
        all_chips = [(x, y)] + chips
        lvl1 = []
        for a in range(n):
            cps = [_remote(g16_r[a].at[_slot(*chip, 1 - c)], recv1[a].at[j], s_send, s_recv, 7 * a + j, sibling)
                   for j, chip in enumerate(all_chips)]
            for cp in cps:
                cp.start()
            lvl1.append(cps)
        lvl2 = []
        for a in range(n):
            for cp in lvl1[a]:
                cp.wait_recv()
            og = outs[a]

            def partials(r, a=a, og=og):
                og[r, :] = go_r[a][r, :] + recv1[a][0, r, :].astype(F32)
                for j, chip in enumerate(chips):
                    mine16 = g16_r[a][_slot(*chip, c), r, :].astype(F32)
                    send2[a][j, r, :] = (mine16 + recv1[a][1 + j, r, :].astype(F32)).astype(BF16)

            _row_chunks(go_r[a].shape[0], partials)
            cps = [_remote(send2[a].at[j], recv2[a].at[j], s_send, s_recv, 7 * a + 4 + j, (*chip, c))
                   for j, chip in enumerate(chips)]
            for cp in cps:
                cp.start()
            lvl2.append(cps)
        for a in range(n):
            for cp in lvl2[a]:
                cp.wait_recv()
            og = outs[a]

            def total(r, a=a, og=og):
                g = og[r, :]
                for j in range(3):
                    g = g + recv2[a][j, r, :].astype(F32)
                og[r, :] = g

            _row_chunks(go_r[a].shape[0], total)
        for cps in lvl1 + lvl2:
            for cp in cps:
                cp.wait_send()

    vm = pl.BlockSpec(memory_space=pltpu.VMEM)
    scratch = ([pltpu.VMEM((3,) + t.shape, BF16) for t in gown] + [pltpu.VMEM((4,) + t.shape, BF16) for t in gown]
               + [pltpu.VMEM((3,) + t.shape, BF16) for t in gown]
               + [pltpu.SemaphoreType.DMA((7 * n,)), pltpu.SemaphoreType.DMA((7 * n,))])
    return pl.pallas_call(
        body, name="reduce_big",
        out_shape=tuple(jax.ShapeDtypeStruct(t.shape, F32) for t in gown),
        in_specs=[vm] * (2 * n), out_specs=(vm,) * n, scratch_shapes=scratch,
        compiler_params=pltpu.CompilerParams(vmem_limit_bytes=VMEM_LIMIT),
    )(*g16, *gown)


def _adamw_update(g, w, m, v):
    n = len(g)

    def body(*refs):
        g_r, w_r, m_r, v_r = (refs[i * n:(i + 1) * n] for i in range(4))
        outs = refs[4 * n:]
        for a in range(n):
            og, od, om, ov = outs[4 * a:4 * a + 4]

            def update(idx, a=a, og=og, od=od, om=om, ov=ov):
                gv = g_r[a][idx]
                d, nm, nv = _adamw(w_r[a][idx], gv, m_r[a][idx], v_r[a][idx])
                og[idx] = gv
                od[idx] = d
                om[idx] = nm
                ov[idx] = nv

            shape = g_r[a].shape
            if len(shape) == 3:
                for b in range(shape[0]):
                    update(b)
            elif shape[0] % ROW_CHUNK == 0:
                _row_chunks(shape[0], update)
            else:
                update(Ellipsis)

    vm = pl.BlockSpec(memory_space=pltpu.VMEM)
    res = pl.pallas_call(
        body, name="adamw_update",
        out_shape=tuple(jax.ShapeDtypeStruct(t.shape, F32) for t in g for _ in range(4)),
        in_specs=[vm] * (4 * n), out_specs=(vm,) * (4 * n),
        compiler_params=pltpu.CompilerParams(vmem_limit_bytes=VMEM_LIMIT),
    )(*g, *w, *m, *v)
    return [res[4 * a:4 * a + 4] for a in range(n)]


TINY = (("pre_norm_g", 1, 1024), ("post_norm_g", 1, 1024), ("pl_b_gate", 1, 1024), ("ssm_d", 1, 512),
        ("ssm_b_glu", 1, 512), ("ssm_log_step", 1, 32), ("attn_sinks", 1, 8), ("ssm_lam_re", 32, 64),
        ("ssm_lam_im", 32, 64))
MEDIUM = (("ssm_b_re", N_STATE, SSM_P), ("ssm_b_im", N_STATE, SSM_P), ("ssm_c_re", SSM_G * SSM_P, SSM_N),
          ("ssm_c_im", SSM_G * SSM_P, SSM_N))


def _stage_rows():
    offs, r = {}, 0
    for name, rows, cols in TINY + (("loss", 1, 1),):
        if rows > 1:
            r = -(-r // 8) * 8
        offs[name] = r
        r += rows if rows > 1 else max(cols // LANES, 1)
    return offs, -(-r // 8) * 8


def _reduce_small(loss, g_tiny, g_med):
    nt, nm_ = len(TINY), len(MEDIUM)
    offs, stage_rows = _stage_rows()

    def body(*refs):
        loss_r, gt, gm = refs[0], refs[1:1 + nt], refs[1 + nt:1 + nt + nm_]
        base = 1 + nt + nm_
        loss_o, out_t, out_m = refs[base], refs[base + 1:base + 1 + nt], refs[base + 1 + nt:base + 1 + nt + nm_]
        base += 1 + nt + nm_
        stage = refs[base]
        recv1, part, recv2 = (refs[base + 1 + i * nm_:base + 1 + (i + 1) * nm_] for i in range(3))
        s_send, s_recv = refs[base + 1 + 3 * nm_:]
        x, y, c = _mesh_pos()
        me = _slot(x, y, c)
        sibling = (x, y, 1 - c)
        chips = [(1 - x, y), (x, 1 - y), (1 - x, 1 - y)]
        all_chips = [(x, y)] + chips
        peers = [sibling] + [(*chip, c) for chip in chips] + [(*chip, 1 - c) for chip in chips]
        sem = iter(range(7 + 14 * nm_))
        lvl1 = []
        for a in range(nm_):
            cps = [_remote(gm[a].at[_slot(*chip, 1 - c)], recv1[a].at[j], s_send, s_recv, next(sem), sibling)
                   for j, chip in enumerate(all_chips)]
            for cp in cps:
                cp.start()
            lvl1.append(cps)
        mine = stage.at[me]
        mine[...] = jnp.zeros((stage_rows, LANES), F32)
        for (name, rows, cols), ref in zip(TINY + (("loss", 1, 1),), gt + (loss_r,)):
            r0 = offs[name]
            if rows > 1:
                mine[r0:r0 + rows, 0:cols] = ref[...]
            elif cols >= LANES:
                for i in range(cols // LANES):
                    mine[r0 + i:r0 + i + 1, :] = ref[:, i * LANES:(i + 1) * LANES]
            else:
                mine[r0:r0 + 1, 0:cols] = ref[...]
        tiny_cps = [_remote(mine, mine, s_send, s_recv, next(sem), peer) for peer in peers]
        for cp in tiny_cps:
            cp.start()
        lvl2 = []
        for a in range(nm_):
            for cp in lvl1[a]:
                cp.wait_recv()
            for j, chip in enumerate(all_chips):
                part[a][j] = gm[a][_slot(*chip, c)] + recv1[a][j]
            cps = [_remote(part[a].at[1 + j], recv2[a].at[j], s_send, s_recv, next(sem), (*chip, c))
                   for j, chip in enumerate(chips)]
            for cp in cps:
                cp.start()
            lvl2.append(cps)
        lvl3 = []
        for a in range(nm_):
            for cp in lvl2[a]:
                cp.wait_recv()
            blk = out_m[a].at[me]
            blk[...] = ((part[a][0] + recv2[a][0]) + recv2[a][1]) + recv2[a][2]
            cps = [_remote(blk, blk, s_send, s_recv, next(sem), peer) for peer in peers]
            for cp in cps:
                cp.start()
            lvl3.append(cps)
        for cp in tiny_cps:
            cp.wait_recv()
        tot = stage[0]
        for d in range(1, N_DEV):
            tot = tot + stage[d]
        loss_o[...] = tot[offs["loss"]:offs["loss"] + 1, 0:1]
        for k, (name, rows, cols) in enumerate(TINY):
            r0 = offs[name]
            if rows > 1:
                out_t[k][...] = tot[r0:r0 + rows, 0:cols]
            elif cols >= LANES:
                for i in range(cols // LANES):
                    out_t[k][:, i * LANES:(i + 1) * LANES] = tot[r0 + i:r0 + i + 1, :]
            else:
                out_t[k][...] = tot[r0:r0 + 1, 0:cols]
        for cps in lvl3:
            for cp in cps:
                cp.wait_recv()
        for cps in lvl1 + lvl2 + lvl3 + [tiny_cps]:
            for cp in cps:
                cp.wait_send()

    vmem = pl.BlockSpec(memory_space=pltpu.VMEM)
    t_shapes = [jax.ShapeDtypeStruct((rows, cols), F32) for _, rows, cols in TINY]
    m_shapes = [jax.ShapeDtypeStruct((N_DEV, rows // N_DEV, cols), F32) for _, rows, cols in MEDIUM]
    blk = [(rows // N_DEV, cols) for _, rows, cols in MEDIUM]
    scratch = ([pltpu.VMEM((N_DEV, stage_rows, LANES), F32)]
               + [pltpu.VMEM((4,) + b, F32) for b in blk] + [pltpu.VMEM((4,) + b, F32) for b in blk]
               + [pltpu.VMEM((3,) + b, F32) for b in blk]
               + [pltpu.SemaphoreType.DMA((7 + 14 * nm_,)), pltpu.SemaphoreType.DMA((7 + 14 * nm_,))])
    res = pl.pallas_call(
        body, name="reduce_small",
        out_shape=(jax.ShapeDtypeStruct((1, 1), F32),) + tuple(t_shapes) + tuple(m_shapes),
        in_specs=[vmem] * (1 + nt + nm_), out_specs=(vmem,) * (1 + nt + nm_), scratch_shapes=scratch,
        compiler_params=pltpu.CompilerParams(vmem_limit_bytes=VMEM_LIMIT),
    )(loss, *g_tiny, *g_med)
    return res[0], list(res[1:1 + nt]), list(res[1 + nt:])


def _in_proj(x2, g_pre, w_in, bl, seg):
    t = x2.shape[0]
    tm = seg

    def body(x_ref, g_ref, w_ref, u_ref, zs_ref, q_ref, k_ref, v_ref, za_ref):
        xv = x_ref[...]
        r = lax.rsqrt(jnp.mean(xv * xv, axis=-1, keepdims=True) + EPS)
        hn = xv * r * g_ref[...]
        proj = _mm(hn, w_ref[...])
        u_ref[0] = proj[:, 0:512]
        zs_ref[...] = proj[:, 512:1024]
        q_ref[...] = proj[:, 1024:1536].astype(BF16)
        k_ref[...] = proj[:, 1536:1664].astype(BF16)
        v_ref[...] = proj[:, 1664:1792].astype(BF16)
        za_ref[...] = proj[:, 1792:2304]

    row = lambda w: pl.BlockSpec((tm, w), lambda i: (i, 0))
    return pl.pallas_call(
        body, name="in_proj", grid=(t // tm,),
        in_specs=[row(D_MODEL), _const_spec((1, D_MODEL)), _const_spec((D_MODEL, D_IN))],
        out_specs=(pl.BlockSpec((1, tm, D_SSM), lambda i: (i // N_SEG, 0, i % N_SEG)),
                   row(512), row(512), row(128), row(128), row(512)),
        out_shape=(jax.ShapeDtypeStruct((bl, seg, N_SEG * D_SSM), F32),
                   jax.ShapeDtypeStruct((t, 512), F32), jax.ShapeDtypeStruct((t, 512), BF16),
                   jax.ShapeDtypeStruct((t, 128), BF16), jax.ShapeDtypeStruct((t, 128), BF16),
                   jax.ShapeDtypeStruct((t, 512), F32)),
        compiler_params=_tc_params(("arbitrary",)),
    )(x2, g_pre, w_in)


def _ssm_prep(lam_re, lam_im, log_step, b_re, b_im, seg):
    def body(lr_ref, li_ref, ls_ref, br_ref, bi_ref, lrr_ref, lir_ref, lsr_ref,
             ar_ref, ai_ref, bbr_ref, bbi_ref, pr_ref, pi_ref):
        lr, li = lr_ref[...], li_ref[...]
        step = jnp.exp(ls_ref[...])
        mag = jnp.exp(lr * step)
        ar = mag * jnp.cos(li * step)
        ai = mag * jnp.sin(li * step)
        ar_ref[...] = ar
        ai_ref[...] = ai
        den = lr * lr + li * li
        cr = ((ar - 1.0) * lr + ai * li) / den
        ci = (ai * lr - (ar - 1.0) * li) / den
        br, bi = br_ref[...], bi_ref[...]
        bbr_ref[...] = cr * br - ci * bi
        bbi_ref[...] = cr * bi + ci * br
        stepr = jnp.exp(lsr_ref[...])
        k = (lax.broadcasted_iota(jnp.int32, (8, N_STATE), 0) + 1).astype(F32)
        magk = jnp.exp(k * (lrr_ref[...] * stepr))
        ang = k * (lir_ref[...] * stepr)
        pr_ref[0:8, :] = magk * jnp.cos(ang)
        pi_ref[0:8, :] = magk * jnp.sin(ang)
        n = 8
        while n < seg:
            tr, ti = pr_ref[n - 1:n, :], pi_ref[n - 1:n, :]
            xr, xi = pr_ref[0:n, :], pi_ref[0:n, :]
            pr_ref[n:2 * n, :] = xr * tr - xi * ti
            pi_ref[n:2 * n, :] = xr * ti + xi * tr
            n *= 2

    col = jax.ShapeDtypeStruct((N_STATE, 1), F32)
    mat = jax.ShapeDtypeStruct((N_STATE, SSM_P), F32)
    pw = jax.ShapeDtypeStruct((seg, N_STATE), F32)
    vm = pl.BlockSpec(memory_space=pltpu.VMEM)
    return pl.pallas_call(
        body, name="ssm_prep", out_shape=(col, col, mat, mat, pw, pw),
        in_specs=[vm] * 8, out_specs=(vm,) * 6,
    )(lam_re, lam_im, log_step, b_re, b_im, lam_re.reshape(1, N_STATE), lam_im.reshape(1, N_STATE),
      log_step.reshape(1, N_STATE))


def _scan_forward(xs, a_re, a_im, pw_re, pw_im, cs, seg):
    are = jnp.broadcast_to(a_re, (N_SEG, ST_T))
    aim = jnp.broadcast_to(a_im, (N_SEG, ST_T))

    def step(t, carry):
        xr, xi = carry
        r = pl.multiple_of(t * N_SEG, N_SEG)
        nr = are * xr - aim * xi + xs[pl.ds(r, N_SEG), 0:ST_T]
        ni = are * xi + aim * xr + xs[pl.ds(r, N_SEG), ST_T:2 * ST_T]
        xs[pl.ds(r, N_SEG), 0:ST_T] = nr
        xs[pl.ds(r, N_SEG), ST_T:2 * ST_T] = ni
        return nr, ni

    zero = jnp.zeros((N_SEG, ST_T), F32)
    fr, fi = lax.fori_loop(0, seg, step, (zero, zero), unroll=2)
    sr, si = pw_re[seg - 1:seg, :], pw_im[seg - 1:seg, :]
    cr = jnp.zeros((1, ST_T), F32)
    ci = jnp.zeros((1, ST_T), F32)
    cs[0:1, :] = cr
    cs[8:9, :] = ci
    for s in range(1, N_SEG):
        ncr = sr * cr - si * ci + fr[s - 1:s, :]
        nci = sr * ci + si * cr + fi[s - 1:s, :]
        cr, ci = ncr, nci
        cs[s:s + 1, :] = cr
        cs[8 + s:9 + s, :] = ci
    car, cai = cs[0:8, :], cs[8:16, :]

    def fix(t, _):
        r = pl.multiple_of(t * N_SEG, N_SEG)
        pr, pi = pw_re[pl.ds(t, 1), :], pw_im[pl.ds(t, 1), :]
        xs[pl.ds(r, N_SEG), 0:ST_T] = xs[pl.ds(r, N_SEG), 0:ST_T] + (pr * car - pi * cai)
        xs[pl.ds(r, N_SEG), ST_T:2 * ST_T] = xs[pl.ds(r, N_SEG), ST_T:2 * ST_T] + (pr * cai + pi * car)
        return 0

    lax.fori_loop(0, seg, fix, 0, unroll=2)


def _ssm_forward(u_perm, bcat, ccat, a_re, a_im, pw_re, pw_im, d_row, seg):
    bl, rows, _ = u_perm.shape

    def body(u_ref, b_ref, c_ref, ar_ref, ai_ref, pr_ref, pi_ref, d_ref, y_ref, xs, cs):
        u = u_ref[0]
        xs[...] = _mm(u, b_ref[0])
        _scan_forward(xs, ar_ref[...], ai_ref[...], pr_ref, pi_ref, cs, seg)
        y_ref[0] = _mm(xs[...], c_ref[0]) + d_ref[...] * u

    return pl.pallas_call(
        body, name="ssm_forward", grid=(bl, N_GT),
        in_specs=[pl.BlockSpec((1, rows, CH_T), lambda b, j: (b, 0, j)),
                  pl.BlockSpec((1, CH_T, 2 * ST_T), lambda b, j: (j, 0, 0)),
                  pl.BlockSpec((1, 2 * ST_T, CH_T), lambda b, j: (j, 0, 0)),
                  pl.BlockSpec((1, ST_T), lambda b, j: (0, j)), pl.BlockSpec((1, ST_T), lambda b, j: (0, j)),
                  pl.BlockSpec((seg, ST_T), lambda b, j: (0, j)), pl.BlockSpec((seg, ST_T), lambda b, j: (0, j)),
                  pl.BlockSpec((1, CH_T), lambda b, j: (0, j))],
        out_specs=pl.BlockSpec((1, rows, CH_T), lambda b, j: (b, 0, j)),
        out_shape=jax.ShapeDtypeStruct((bl, rows, D_SSM), F32),
        scratch_shapes=[pltpu.VMEM((rows, 2 * ST_T), F32), pltpu.VMEM((16, ST_T), F32)],
        compiler_params=_tc_params(("arbitrary", "arbitrary")),
    )(u_perm, bcat, ccat, a_re, a_im, pw_re, pw_im, d_row)


def _ssm_backward(u_perm, dy_perm, bcat, bcat_t, ccat_t, a_re, a_im, pw_re, pw_im, d_row, seg):
    bl, rows, _ = u_perm.shape

    def body(u_ref, dy_ref, b_ref, bt_ref, ct_ref, ar_ref, ai_ref, pr_ref, pi_ref, d_ref,
             du_ref, db_ref, dc_ref, dar_ref, dai_ref, dd_ref, xs, ls, cs, cl):
        b = pl.program_id(1)
        u = u_ref[0]
        dy = dy_ref[0]
        xs[...] = _mm(u, b_ref[0])
        _scan_forward(xs, ar_ref[...], ai_ref[...], pr_ref, pi_ref, cs, seg)
        ls[...] = _mm(dy, ct_ref[0])
        are = jnp.broadcast_to(ar_ref[...], (N_SEG, ST_T))
        aim = jnp.broadcast_to(ai_ref[...], (N_SEG, ST_T))

        def step(i, carry):
            lr, li = carry
            r = pl.multiple_of((seg - 1 - i) * N_SEG, N_SEG)
            nr = are * lr + aim * li + ls[pl.ds(r, N_SEG), 0:ST_T]
            ni = are * li - aim * lr + ls[pl.ds(r, N_SEG), ST_T:2 * ST_T]
            ls[pl.ds(r, N_SEG), 0:ST_T] = nr
            ls[pl.ds(r, N_SEG), ST_T:2 * ST_T] = ni
            return nr, ni

        zero = jnp.zeros((N_SEG, ST_T), F32)
        fr, fi = lax.fori_loop(0, seg, step, (zero, zero), unroll=2)
        sr, si = pr_ref[seg - 1:seg, :], pi_ref[seg - 1:seg, :]
        cr = jnp.zeros((1, ST_T), F32)
        ci = jnp.zeros((1, ST_T), F32)
        cl[7:8, :] = cr
        cl[15:16, :] = ci
        for s in range(N_SEG - 2, -1, -1):
            ncr = sr * cr + si * ci + fr[s + 1:s + 2, :]
            nci = sr * ci - si * cr + fi[s + 1:s + 2, :]
            cr, ci = ncr, nci
            cl[s:s + 1, :] = cr
            cl[8 + s:9 + s, :] = ci
        clr, cli = cl[0:8, :], cl[8:16, :]

        def fix(t, acc):
            dr, di = acc
            r = pl.multiple_of(t * N_SEG, N_SEG)
            pr, pi = pr_ref[pl.ds(seg - 1 - t, 1), :], pi_ref[pl.ds(seg - 1 - t, 1), :]
            lr = ls[pl.ds(r, N_SEG), 0:ST_T] + (pr * clr + pi * cli)
            li = ls[pl.ds(r, N_SEG), ST_T:2 * ST_T] + (pr * cli - pi * clr)
            ls[pl.ds(r, N_SEG), 0:ST_T] = lr
            ls[pl.ds(r, N_SEG), ST_T:2 * ST_T] = li
            rp = pl.multiple_of(jnp.maximum(t - 1, 0) * N_SEG, N_SEG)
            first = t == 0
            xpr = jnp.where(first, cs[0:8, :], xs[pl.ds(rp, N_SEG), 0:ST_T])
            xpi = jnp.where(first, cs[8:16, :], xs[pl.ds(rp, N_SEG), ST_T:2 * ST_T])
            return dr + (lr * xpr + li * xpi), di + (li * xpr - lr * xpi)

        dr, di = lax.fori_loop(0, seg, fix, (zero, zero), unroll=2)
        dar = jnp.sum(dr, axis=0, keepdims=True)
        dai = jnp.sum(di, axis=0, keepdims=True)
        lall = ls[...]
        du_ref[0] = _mm(lall, bt_ref[0]) + d_ref[...] * dy
        dbp = _mm_tn(u, lall)
        dcp = _mm_tn(dy, xs[...])
        ddp = jnp.sum(dy * u, axis=0, keepdims=True)

        @pl.when(b == 0)
        def _():
            db_ref[0] = dbp
            dc_ref[0] = dcp
            dar_ref[...] = dar
            dai_ref[...] = dai
            dd_ref[...] = ddp

        @pl.when(b != 0)
        def _():
            db_ref[0] += dbp
            dc_ref[0] += dcp
            dar_ref[...] += dar
            dai_ref[...] += dai
            dd_ref[...] += ddp

    tile3 = lambda r, c: pl.BlockSpec((1, r, c), lambda j, b: (j, 0, 0))
    lane = lambda r, c: pl.BlockSpec((r, c), lambda j, b: (0, j))
    act = pl.BlockSpec((1, rows, CH_T), lambda j, b: (b, 0, j))
    return pl.pallas_call(
        body, name="ssm_backward", grid=(N_GT, bl),
        in_specs=[act, act, tile3(CH_T, 2 * ST_T), tile3(2 * ST_T, CH_T), tile3(CH_T, 2 * ST_T),
                  lane(1, ST_T), lane(1, ST_T), lane(seg, ST_T), lane(seg, ST_T), lane(1, CH_T)],
        out_specs=(act, tile3(CH_T, 2 * ST_T), tile3(CH_T, 2 * ST_T), lane(1, ST_T), lane(1, ST_T), lane(1, CH_T)),
        out_shape=(jax.ShapeDtypeStruct((bl, rows, D_SSM), F32),
                   jax.ShapeDtypeStruct((N_GT, CH_T, 2 * ST_T), F32), jax.ShapeDtypeStruct((N_GT, CH_T, 2 * ST_T), F32),
                   jax.ShapeDtypeStruct((1, N_STATE), F32), jax.ShapeDtypeStruct((1, N_STATE), F32),
                   jax.ShapeDtypeStruct((1, D_SSM), F32)),
        scratch_shapes=[pltpu.VMEM((rows, 2 * ST_T), F32), pltpu.VMEM((rows, 2 * ST_T), F32),
                        pltpu.VMEM((16, ST_T), F32), pltpu.VMEM((16, ST_T), F32)],
        compiler_params=_tc_params(("arbitrary", "arbitrary")),
    )(u_perm, dy_perm, bcat, bcat_t, ccat_t, a_re, a_im, pw_re, pw_im, d_row)


def _ssm_param_grads(lam_re, lam_im, log_step, b_re, b_im, a_re, a_im, da_re, da_im, dbb_re, dbb_im):
    def body(lr_ref, li_ref, ls_ref, br_ref, bi_ref, ar_ref, ai_ref, gar_ref, gai_ref, gbr_ref, gbi_ref,
             dlr_ref, dli_ref, dls_ref, dbr_ref, dbi_ref):
        lr, li = lr_ref[...], li_ref[...]
        step = jnp.exp(ls_ref[...])
        ar, ai = ar_ref[...], ai_ref[...]
        den = lr * lr + li * li
        cr = ((ar - 1.0) * lr + ai * li) / den
        ci = (ai * lr - (ar - 1.0) * li) / den
        br, bi = br_ref[...], bi_ref[...]
        gbr, gbi = gbr_ref[...], gbi_ref[...]
        dbr_ref[...] = cr * gbr + ci * gbi
        dbi_ref[...] = cr * gbi - ci * gbr
        gcr = jnp.sum(br * gbr + bi * gbi, axis=-1, keepdims=True)
        gci = jnp.sum(br * gbi - bi * gbr, axis=-1, keepdims=True)
        ilr, ili = lr / den, -li / den
        gar = gar_ref[...] + (ilr * gcr + ili * gci)
        gai = gai_ref[...] + (ilr * gci - ili * gcr)
        qr, qi = cr * ilr - ci * ili, cr * ili + ci * ilr
        glr = -(qr * gcr + qi * gci)
        gli = -(qr * gci - qi * gcr)
        gwr = ar * gar + ai * gai
        gwi = ar * gai - ai * gar
        dlr_ref[...] = glr + step * gwr
        dli_ref[...] = gli + step * gwi
        dls_ref[...] = (lr * gwr + li * gwi) * step

    col = jax.ShapeDtypeStruct((N_STATE, 1), F32)
    mat = jax.ShapeDtypeStruct((N_STATE, SSM_P), F32)
    vm = pl.BlockSpec(memory_space=pltpu.VMEM)
    return pl.pallas_call(
        body, name="ssm_param_grads", out_shape=(col, col, col, mat, mat),
        in_specs=[vm] * 11, out_specs=(vm,) * 5,
    )(lam_re, lam_im, log_step, b_re, b_im, a_re, a_im, da_re, da_im, dbb_re, dbb_im)


def _att_bias_mask(first_block):
    qi = lax.broadcasted_iota(jnp.int32, (ATT_BLOCK, 2 * ATT_BLOCK), 0)
    si = lax.broadcasted_iota(jnp.int32, (ATT_BLOCK, 2 * ATT_BLOCK), 1)
    dist = qi + ATT_BLOCK - si
    valid = (dist >= 0) & (dist < ATT_BLOCK) & ((si >= ATT_BLOCK) | jnp.logical_not(first_block))
    return dist.astype(F32), valid


def _attn_forward(q, k, v, sinks, bl, nb):
    t = q.shape[0]

    def body(sink_ref, q_ref, kp_ref, kc_ref, vp_ref, vc_ref, o_ref, lse_ref):
        i = pl.program_id(1)
        dist, valid = _att_bias_mask(i == 0)
        kk = jnp.concatenate([kp_ref[...], kc_ref[...]], axis=0)
        vv = jnp.concatenate([vp_ref[...], vc_ref[...]], axis=0)
        qv = q_ref[...]
        for h in range(N_HEADS):
            kv = h // Q_PER_KV
            slope = 2.0 ** (-(h + 1))
            qh = qv[:, h * HEAD_DIM:(h + 1) * HEAD_DIM]
            kh = kk[:, kv * HEAD_DIM:(kv + 1) * HEAD_DIM]
            vh = vv[:, kv * HEAD_DIM:(kv + 1) * HEAD_DIM]
            s = _mm_nt(qh, kh) * ATT_SCALE - slope * dist
            s = jnp.where(valid, s, NEG_BIG)
            sink = sink_ref[h]
            m = jnp.maximum(jnp.max(s, axis=-1, keepdims=True), sink)
            e = jnp.exp(s - m)
            den = jnp.sum(e, axis=-1, keepdims=True) + jnp.exp(sink - m)
            p = e / den
            o_ref[:, h * HEAD_DIM:(h + 1) * HEAD_DIM] = _mm(p, vh)
            lse_ref[:, h:h + 1] = m + jnp.log(den)

    cur = lambda w: pl.BlockSpec((ATT_BLOCK, w), lambda b, i: (b * nb + i, 0))
    prev = lambda w: pl.BlockSpec((ATT_BLOCK, w), lambda b, i: (b * nb + jnp.maximum(i - 1, 0), 0))
    return pl.pallas_call(
        body, name="attn_forward", grid=(bl, nb),
        in_specs=[pl.BlockSpec(memory_space=pltpu.SMEM), cur(512), prev(128), cur(128), prev(128), cur(128)],
        out_specs=(cur(512), cur(N_HEADS)),
        out_shape=(jax.ShapeDtypeStruct((t, D_ATTN), F32), jax.ShapeDtypeStruct((t, N_HEADS), F32)),
        compiler_params=_tc_params(("arbitrary", "arbitrary")),
    )(sinks, q, k, k, v, v)


def _attn_backward(q, k, v, o, do, lse, sinks, bl, nb):
    t = q.shape[0]

    def body(sink_ref, qc_ref, qn_ref, kp_ref, kc_ref, vp_ref, vc_ref, oc_ref, on_ref, doc_ref, don_ref,
             lc_ref, ln_ref, dq_ref, dk_ref, dv_ref, ds_ref):
        b, i = pl.program_id(0), pl.program_id(1)
        dist, valid = _att_bias_mask(i == 0)
        has_next = i + 1 < nb
        dist_n = dist[:, 0:ATT_BLOCK]
        valid_n = (dist_n < ATT_BLOCK) & has_next
        kk = jnp.concatenate([kp_ref[...], kc_ref[...]], axis=0)
        vv = jnp.concatenate([vp_ref[...], vc_ref[...]], axis=0)
        qc, qn = qc_ref[...], qn_ref[...]
        oc, on = oc_ref[...], on_ref[...]
        doc, don = doc_ref[...], don_ref[...]
        lc, ln = lc_ref[...], ln_ref[...]
        dsink_cols = []
        for kv in range(KV_HEADS):
            kh = kk[:, kv * HEAD_DIM:(kv + 1) * HEAD_DIM]
            vh = vv[:, kv * HEAD_DIM:(kv + 1) * HEAD_DIM]
            khc, vhc = kh[ATT_BLOCK:, :], vh[ATT_BLOCK:, :]
            dk_acc = jnp.zeros((ATT_BLOCK, HEAD_DIM), F32)
            dv_acc = jnp.zeros((ATT_BLOCK, HEAD_DIM), F32)
            for g in range(Q_PER_KV):
                h = kv * Q_PER_KV + g
                hs = slice(h * HEAD_DIM, (h + 1) * HEAD_DIM)
                slope = 2.0 ** (-(h + 1))
                qh, doh = qc[:, hs], doc[:, hs]
                delta = jnp.sum(doh * oc[:, hs], axis=-1, keepdims=True)
                lse_h = lc[:, h:h + 1]
                s = _mm_nt(qh, kh) * ATT_SCALE - slope * dist
                p = jnp.where(valid, jnp.exp(s - lse_h), 0.0)
                dp = _mm_nt(doh, vh)
                dsc = p * (dp - delta)
                dq_ref[:, hs] = _mm(dsc, kh) * ATT_SCALE
                dk_acc += _mm_tn(dsc[:, ATT_BLOCK:], qh)
                dv_acc += _mm_tn(p[:, ATT_BLOCK:], doh)
                dsink_cols.append(-jnp.sum(jnp.exp(sink_ref[h] - lse_h) * delta, axis=0, keepdims=True))
                qh2, doh2 = qn[:, hs], don[:, hs]
                delta2 = jnp.sum(doh2 * on[:, hs], axis=-1, keepdims=True)
                s2 = _mm_nt(qh2, khc) * ATT_SCALE - slope * dist_n
                p2 = jnp.where(valid_n, jnp.exp(s2 - ln[:, h:h + 1]), 0.0)
                dp2 = _mm_nt(doh2, vhc)
                ds2 = p2 * (dp2 - delta2)
                dk_acc += _mm_tn(ds2, qh2)
                dv_acc += _mm_tn(p2, doh2)
            dk_ref[:, kv * HEAD_DIM:(kv + 1) * HEAD_DIM] = dk_acc * ATT_SCALE
            dv_ref[:, kv * HEAD_DIM:(kv + 1) * HEAD_DIM] = dv_acc
        dsink = jnp.concatenate(dsink_cols, axis=1)

        @pl.when((b == 0) & (i == 0))
        def _():
            ds_ref[...] = dsink

        @pl.when((b != 0) | (i != 0))
        def _():
            ds_ref[...] += dsink

    cur = lambda w: pl.BlockSpec((ATT_BLOCK, w), lambda b, i: (b * nb + i, 0))
    prev = lambda w: pl.BlockSpec((ATT_BLOCK, w), lambda b, i: (b * nb + jnp.maximum(i - 1, 0), 0))
    nxt = lambda w: pl.BlockSpec((ATT_BLOCK, w), lambda b, i: (b * nb + jnp.minimum(i + 1, nb - 1), 0))
    return pl.pallas_call(
        body, name="attn_backward", grid=(bl, nb),
        in_specs=[pl.BlockSpec(memory_space=pltpu.SMEM), cur(512), nxt(512), prev(128), cur(128), prev(128), cur(128),
                  cur(512), nxt(512), cur(512), nxt(512), cur(N_HEADS), nxt(N_HEADS)],
        out_specs=(cur(512), cur(128), cur(128), pl.BlockSpec((1, N_HEADS), lambda b, i: (0, 0))),
        out_shape=(jax.ShapeDtypeStruct((t, D_ATTN), F32), jax.ShapeDtypeStruct((t, 128), F32),
                   jax.ShapeDtypeStruct((t, 128), F32), jax.ShapeDtypeStruct((1, N_HEADS), F32)),
        compiler_params=_tc_params(("arbitrary", "arbitrary")),
    )(sinks, q, q, k, k, v, v, o, o, do, do, lse, lse)


def _mix_forward_backward(x2, y_perm, z_ssm, attn, z_attn, p2, target2, w_glu, b_glu, w_out, g_post, w_gate, b_gate,
                          w_proj, bl, seg):
    t = x2.shape[0]
    tm = seg

    def body(x_ref, y_ref, zs_ref, at_ref, za_ref, p_ref, tg_ref,
             wglu_ref, bglu_ref, wout_ref, gpost_ref, wgate_ref, bgate_ref, wproj_ref,
             loss_ref, dh1_ref, dy_ref, dzs_ref, dat_ref, dza_ref,
             dwglu_ref, dbglu_ref, dwout_ref, dgpost_ref, dwgate_ref, dbgate_ref, dwproj_ref):
        i = pl.program_id(0)
        y = y_ref[0]
        u3 = GELU_C * (y + GELU_K * y * y * y)
        th = jnp.tanh(u3)
        gl = 0.5 * y * (1.0 + th)
        a = _mm(gl, wglu_ref[...]) + bglu_ref[...]
        sa = _sigmoid(a)
        glu = gl * sa
        zs = zs_ref[...]
        sgs = _sigmoid(zs)
        ssm_out = glu * (zs * sgs)
        za = za_ref[...]
        sga = _sigmoid(za)
        at = at_ref[...]
        attn_out = at * (za * sga)
        cat = jnp.concatenate([ssm_out, attn_out], axis=-1).astype(BF16)
        mixed = _mm(cat, wout_ref[...])
        r2 = lax.rsqrt(jnp.mean(mixed * mixed, axis=-1, keepdims=True) + EPS)
        nhat = mixed * r2
        gpost = gpost_ref[...]
        h1 = x_ref[...] + nhat * gpost
        gate = _sigmoid(_mm(h1, wgate_ref[...]) + bgate_ref[...])
        pv = p_ref[...]
        pp = _mm(pv, wproj_ref[...])
        h2 = h1 + gate * pp
        err = h2 - tg_ref[...]
        loss_part = jnp.sum(jnp.sum(err * err, axis=-1, keepdims=True), axis=0, keepdims=True) * (0.5 / D_MODEL)
        dh2 = err * (1.0 / D_MODEL)
        dgp = dh2 * pp * gate * (1.0 - gate)
        dpp = dh2 * gate
        dh1 = dh2 + _mm_nt(dgp, wgate_ref[...])
        dh1_ref[...] = dh1
        dnhat = dh1 * gpost
        dmixed = r2 * (dnhat - nhat * jnp.mean(dnhat * nhat, axis=-1, keepdims=True))
        dcat = _mm_nt(dmixed, wout_ref[...])
        dso, dao = dcat[:, 0:D_SSM], dcat[:, D_SSM:]
        dat_ref[...] = dao * (za * sga)
        dza_ref[...] = dao * at * (sga * (1.0 + za * (1.0 - sga)))
        dzs_ref[...] = dso * glu * (sgs * (1.0 + zs * (1.0 - sgs)))
        dglu = dso * (zs * sgs)
        da = dglu * gl * sa * (1.0 - sa)
        dgl = dglu * sa + _mm_nt(da, wglu_ref[...])
        dgelu = 0.5 * (1.0 + th) + 0.5 * y * (1.0 - th * th) * (GELU_C * (1.0 + 3.0 * GELU_K * y * y))
        dy_ref[0] = dgl * dgelu
        parts = (
            (dwglu_ref, _mm_tn(gl, da)), (dbglu_ref, jnp.sum(da, axis=0, keepdims=True)),
            (dwout_ref, _mm_tn(cat, dmixed)), (dgpost_ref, jnp.sum(dh1 * nhat, axis=0, keepdims=True)),
            (dwgate_ref, _mm_tn(h1, dgp)), (dbgate_ref, jnp.sum(dgp, axis=0, keepdims=True)),
            (dwproj_ref, _mm_tn(pv, dpp)), (loss_ref, loss_part),
        )

        @pl.when(i == 0)
        def _():
            for ref, val in parts:
                ref[...] = val

        @pl.when(i != 0)
        def _():
            for ref, val in parts:
                ref[...] += val

    row = lambda w: pl.BlockSpec((tm, w), lambda i: (i, 0))
    perm = pl.BlockSpec((1, tm, D_SSM), lambda i: (i // N_SEG, 0, i % N_SEG))
    perm_shape = jax.ShapeDtypeStruct((bl, seg, N_SEG * D_SSM), F32)
    acc = lambda r, c: (_const_spec((r, c)), jax.ShapeDtypeStruct((r, c), F32))
    accs = [acc(D_SSM, D_SSM), acc(1, D_SSM), acc(D_MODEL, D_MODEL), acc(1, D_MODEL), acc(D_MODEL, D_MODEL),
            acc(1, D_MODEL), acc(D_PLE, D_MODEL)]
    return pl.pallas_call(
        body, name="mix_forward_backward", grid=(t // tm,),
        in_specs=[row(D_MODEL), perm, row(512), row(512), row(512), row(D_PLE), row(D_MODEL),
                  _const_spec((D_SSM, D_SSM)), _const_spec((1, D_SSM)), _const_spec((D_MODEL, D_MODEL)),
                  _const_spec((1, D_MODEL)), _const_spec((D_MODEL, D_MODEL)), _const_spec((1, D_MODEL)),
                  _const_spec((D_PLE, D_MODEL))],
        out_specs=(_const_spec((1, 1)), row(D_MODEL), perm, row(512), row(512), row(512)) + tuple(a[0] for a in accs),
        out_shape=(jax.ShapeDtypeStruct((1, 1), F32), jax.ShapeDtypeStruct((t, D_MODEL), F32), perm_shape,
                   jax.ShapeDtypeStruct((t, 512), F32), jax.ShapeDtypeStruct((t, 512), F32),
                   jax.ShapeDtypeStruct((t, 512), F32)) + tuple(a[1] for a in accs),
        compiler_params=_tc_params(("arbitrary",)),
    )(x2, y_perm, z_ssm, attn, z_attn, p2, target2, w_glu, b_glu, w_out, g_post, w_gate, b_gate, w_proj)


def _in_backward(x2, dh1, du_perm, dz_ssm, dq, dk, dv, dz_attn, g_pre, w_in, bl, seg):
    t = x2.shape[0]
    tm = seg

    def body(x_ref, dh1_ref, du_ref, dzs_ref, dq_ref, dk_ref, dv_ref, dza_ref, g_ref, w_ref,
             gx_ref, dw_ref, dg_ref):
        i = pl.program_id(0)
        xv = x_ref[...]
        r = lax.rsqrt(jnp.mean(xv * xv, axis=-1, keepdims=True) + EPS)
        xhat = xv * r
        g = g_ref[...]
        hn = (xhat * g).astype(BF16)
        dproj = jnp.concatenate([du_ref[0].astype(BF16), dzs_ref[...].astype(BF16), dq_ref[...].astype(BF16),
                                 dk_ref[...].astype(BF16), dv_ref[...].astype(BF16), dza_ref[...].astype(BF16)],
                                axis=-1)
        dhn = _mm_nt(dproj, w_ref[...])
        dxhat = dhn * g
        gx_ref[...] = dh1_ref[...] + r * (dxhat - xhat * jnp.mean(dxhat * xhat, axis=-1, keepdims=True))
        dwp = _mm_tn(hn, dproj)
        dgp = jnp.sum(dhn * xhat, axis=0, keepdims=True)

        @pl.when(i == 0)
        def _():
            dw_ref[...] = dwp
            dg_ref[...] = dgp

        @pl.when(i != 0)
        def _():
            dw_ref[...] += dwp
            dg_ref[...] += dgp

    row = lambda w: pl.BlockSpec((tm, w), lambda i: (i, 0))
    perm = pl.BlockSpec((1, tm, D_SSM), lambda i: (i // N_SEG, 0, i % N_SEG))
    return pl.pallas_call(
        body, name="in_backward", grid=(t // tm,),
        in_specs=[row(D_MODEL), row(D_MODEL), perm, row(512), row(512), row(128), row(128), row(512),
                  _const_spec((1, D_MODEL)), _const_spec((D_MODEL, D_IN))],
        out_specs=(row(D_MODEL), _const_spec((D_MODEL, D_IN)), _const_spec((1, D_MODEL))),
        out_shape=(jax.ShapeDtypeStruct((t, D_MODEL), F32), jax.ShapeDtypeStruct((D_MODEL, D_IN), F32),
                   jax.ShapeDtypeStruct((1, D_MODEL), F32)),
        compiler_params=_tc_params(("arbitrary",)),
    )(x2, dh1, du_perm, dz_ssm, dq, dk, dv, dz_attn, g_pre, w_in)


def _block_diag(t):
    a, b = t.shape[1], t.shape[2]
    eye = jnp.eye(G_TILE, dtype=t.dtype)
    t = t.reshape(N_GT, G_TILE, a, 1, b) * eye[None, :, None, :, None]
    return t.reshape(N_GT, G_TILE * a, G_TILE * b)


def _diag_blocks(m, a, b):
    m = m.reshape(N_GT, G_TILE, a, G_TILE, b)
    return jnp.einsum("tgagb->tgab", m).reshape(SSM_G, a, b)


def _local_step(x, p, target, pre_norm_g, w_in, ssm_lam_re, ssm_lam_im, ssm_log_step, ssm_b_re, ssm_b_im, ssm_c_re,
                ssm_c_im, ssm_d, w_glu, ssm_b_glu, attn_sinks, w_out, post_norm_g, w_proj, w_gate, pl_b_gate):
    bl, seq, _ = x.shape
    seg = seq // N_SEG
    nb = seq // ATT_BLOCK
    t = bl * seq
    x2 = x.reshape(t, D_MODEL)
    p2 = p.reshape(t, D_PLE)
    tg2 = target.reshape(t, D_MODEL)

    lam_re = ssm_lam_re.reshape(N_STATE, 1)
    lam_im = ssm_lam_im.reshape(N_STATE, 1)
    log_step = jnp.broadcast_to(ssm_log_step.reshape(SSM_G, 1), (SSM_G, SSM_N)).reshape(N_STATE, 1)
    b_re = ssm_b_re.reshape(N_STATE, SSM_P)
    b_im = ssm_b_im.reshape(N_STATE, SSM_P)
    a_re, a_im, bb_re, bb_im, pw_re, pw_im = _ssm_prep(lam_re, lam_im, log_step, b_re, b_im, seg)
    bb_re_t = jnp.swapaxes(bb_re.reshape(SSM_G, SSM_N, SSM_P), 1, 2)
    bb_im_t = jnp.swapaxes(bb_im.reshape(SSM_G, SSM_N, SSM_P), 1, 2)
    bcat = jnp.concatenate([_block_diag(bb_re_t), _block_diag(bb_im_t)], axis=-1).astype(BF16)
    ccat_t = jnp.concatenate([_block_diag(ssm_c_re.reshape(SSM_G, SSM_P, SSM_N)),
                              -_block_diag(ssm_c_im.reshape(SSM_G, SSM_P, SSM_N))], axis=-1).astype(BF16)
    bcat_t = jnp.swapaxes(bcat, 1, 2)
    ccat = jnp.swapaxes(ccat_t, 1, 2)
    a_re_row, a_im_row = a_re.reshape(1, N_STATE), a_im.reshape(1, N_STATE)
    d_row = ssm_d.reshape(1, D_SSM)

    u_perm, z_ssm, q, k, v, z_attn = _in_proj(x2, pre_norm_g.reshape(1, D_MODEL), w_in, bl, seg)
    u_perm = u_perm.reshape(bl, seq, D_SSM)
    y_perm = _ssm_forward(u_perm, bcat, ccat, a_re_row, a_im_row, pw_re, pw_im, d_row, seg)
    sinks = attn_sinks.reshape(N_HEADS)
    attn, lse = _attn_forward(q, k, v, sinks, bl, nb)
    (loss, dh1, dy_perm, dz_ssm, dattn, dz_attn, d_w_glu, d_b_glu, d_w_out, d_g_post, d_w_gate, d_b_gate,
     d_w_proj) = _mix_forward_backward(
        x2, y_perm.reshape(bl, seg, N_SEG * D_SSM), z_ssm, attn, z_attn, p2, tg2, w_glu,
        ssm_b_glu.reshape(1, D_SSM), w_out, post_norm_g.reshape(1, D_MODEL), w_gate, pl_b_gate.reshape(1, D_MODEL),
        w_proj, bl, seg)
    dq, dk, dv, d_sinks = _attn_backward(q, k, v, attn, dattn, lse, sinks, bl, nb)
    du_perm, d_bcat, d_ccat_t, da_re, da_im, d_d = _ssm_backward(
        u_perm, dy_perm.reshape(bl, seq, D_SSM), bcat, bcat_t, ccat_t, a_re_row, a_im_row, pw_re, pw_im, d_row, seg)
    grad_x, d_w_in, d_g_pre = _in_backward(
        x2, dh1, du_perm.reshape(bl, seg, N_SEG * D_SSM), dz_ssm, dq, dk, dv, dz_attn,
        pre_norm_g.reshape(1, D_MODEL), w_in, bl, seg)
    dbb_re = jnp.swapaxes(_diag_blocks(d_bcat[:, :, 0:ST_T], SSM_P, SSM_N), 1, 2).reshape(N_STATE, SSM_P)
    dbb_im = jnp.swapaxes(_diag_blocks(d_bcat[:, :, ST_T:], SSM_P, SSM_N), 1, 2).reshape(N_STATE, SSM_P)
    d_lam_re, d_lam_im, d_ls, d_b_re, d_b_im = _ssm_param_grads(
        lam_re, lam_im, log_step, b_re, b_im, a_re, a_im, da_re.reshape(N_STATE, 1), da_im.reshape(N_STATE, 1),
        dbb_re, dbb_im)
    grads = {
        "pre_norm_g": d_g_pre, "w_in": d_w_in, "ssm_lam_re": d_lam_re, "ssm_lam_im": d_lam_im,
        "ssm_log_step": jnp.sum(d_ls.reshape(SSM_G, SSM_N), axis=-1), "ssm_b_re": d_b_re, "ssm_b_im": d_b_im,
        "ssm_c_re": _diag_blocks(d_ccat_t[:, :, 0:ST_T], SSM_P, SSM_N),
        "ssm_c_im": -_diag_blocks(d_ccat_t[:, :, ST_T:], SSM_P, SSM_N),
        "ssm_d": d_d, "ssm_w_glu": d_w_glu, "ssm_b_glu": d_b_glu, "attn_sinks": d_sinks, "w_out": d_w_out,
        "post_norm_g": d_g_post, "pl_w_proj": d_w_proj, "pl_w_gate": d_w_gate, "pl_b_gate": d_b_gate,
    }
    return loss, grad_x.reshape(bl, seq, D_MODEL), grads


BIG_NAMES = ("w_in", "w_out", "pl_w_gate", "pl_w_proj", "ssm_w_glu")
COL_SHARDED = {"w_in": D_IN // N_DEV, "pl_w_proj": D_MODEL // N_DEV}
WEIGHT_NAMES = ("pre_norm_g", "w_in", "ssm_lam_re", "ssm_lam_im", "ssm_log_step", "ssm_b_re", "ssm_b_im", "ssm_c_re",
                "ssm_c_im", "ssm_d", "ssm_w_glu", "ssm_b_glu", "attn_sinks", "w_out", "post_norm_g", "pl_w_proj",
                "pl_w_gate", "pl_b_gate")


def _gathered_to_full(name, g):
    _, rows, cols = g.shape
    if name in COL_SHARDED:
        return jnp.swapaxes(g, 0, 1).reshape(rows, N_DEV * cols)
    return g.reshape(N_DEV * rows, cols)


def _full_to_owned(name, full):
    if name in COL_SHARDED:
        return jnp.swapaxes(full.reshape(full.shape[0], N_DEV, COL_SHARDED[name]), 0, 1)
    return full.reshape(N_DEV, full.shape[0] // N_DEV, full.shape[1])


def kernel(x, p, pre_norm_g, w_in, ssm_lam_re, ssm_lam_im, ssm_log_step, ssm_b_re, ssm_b_im, ssm_c_re, ssm_c_im, ssm_d, ssm_w_glu, ssm_b_glu, attn_sinks, w_out, post_norm_g, pl_w_proj, pl_w_gate, pl_b_gate, loss_target, m_pre_norm_g, m_w_in, m_ssm_lam_re, m_ssm_lam_im, m_ssm_log_step, m_ssm_b_re, m_ssm_b_im, m_ssm_c_re, m_ssm_c_im, m_ssm_d, m_ssm_w_glu, m_ssm_b_glu, m_attn_sinks, m_w_out, m_post_norm_g, m_pl_w_proj, m_pl_w_gate, m_pl_b_gate, v_pre_norm_g, v_w_in, v_ssm_lam_re, v_ssm_lam_im, v_ssm_log_step, v_ssm_b_re, v_ssm_b_im, v_ssm_c_re, v_ssm_c_im, v_ssm_d, v_ssm_w_glu, v_ssm_b_glu, v_attn_sinks, v_w_out, v_post_norm_g, v_pl_w_proj, v_pl_w_gate, v_pl_b_gate):
    w = dict(pre_norm_g=pre_norm_g, w_in=w_in, ssm_lam_re=ssm_lam_re, ssm_lam_im=ssm_lam_im, ssm_log_step=ssm_log_step,
             ssm_b_re=ssm_b_re, ssm_b_im=ssm_b_im, ssm_c_re=ssm_c_re, ssm_c_im=ssm_c_im, ssm_d=ssm_d, ssm_w_glu=ssm_w_glu,
             ssm_b_glu=ssm_b_glu, attn_sinks=attn_sinks, w_out=w_out, post_norm_g=post_norm_g, pl_w_proj=pl_w_proj,
             pl_w_gate=pl_w_gate, pl_b_gate=pl_b_gate)
    m = dict(pre_norm_g=m_pre_norm_g, w_in=m_w_in, ssm_lam_re=m_ssm_lam_re, ssm_lam_im=m_ssm_lam_im,
             ssm_log_step=m_ssm_log_step, ssm_b_re=m_ssm_b_re, ssm_b_im=m_ssm_b_im, ssm_c_re=m_ssm_c_re,
             ssm_c_im=m_ssm_c_im, ssm_d=m_ssm_d, ssm_w_glu=m_ssm_w_glu, ssm_b_glu=m_ssm_b_glu, attn_sinks=m_attn_sinks,
             w_out=m_w_out, post_norm_g=m_post_norm_g, pl_w_proj=m_pl_w_proj, pl_w_gate=m_pl_w_gate,
             pl_b_gate=m_pl_b_gate)
    v = dict(pre_norm_g=v_pre_norm_g, w_in=v_w_in, ssm_lam_re=v_ssm_lam_re, ssm_lam_im=v_ssm_lam_im,
             ssm_log_step=v_ssm_log_step, ssm_b_re=v_ssm_b_re, ssm_b_im=v_ssm_b_im, ssm_c_re=v_ssm_c_re,
             ssm_c_im=v_ssm_c_im, ssm_d=v_ssm_d, ssm_w_glu=v_ssm_w_glu, ssm_b_glu=v_ssm_b_glu, attn_sinks=v_attn_sinks,
             w_out=v_w_out, post_norm_g=v_post_norm_g, pl_w_proj=v_pl_w_proj, pl_w_gate=v_pl_w_gate,
             pl_b_gate=v_pl_b_gate)
    me = _slot(lax.axis_index("x"), lax.axis_index("y"), lax.axis_index("c"))

    gathered = _allgather_weights([w[n][0] for n in BIG_NAMES])
    full = {n: _gathered_to_full(n, g) for n, g in zip(BIG_NAMES, gathered)}

    loss, grad_x, grads = _local_step(
        x, p[0], loss_target, pre_norm_g[0], full["w_in"], ssm_lam_re[0], ssm_lam_im[0], ssm_log_step[0], ssm_b_re[0],
        ssm_b_im[0], ssm_c_re[0], ssm_c_im[0], ssm_d[0], full["ssm_w_glu"], ssm_b_glu[0], attn_sinks[0],
        full["w_out"], post_norm_g[0], full["pl_w_proj"], full["pl_w_gate"], pl_b_gate[0])

    owned = [_full_to_owned(n, grads[n]) for n in BIG_NAMES]
    g_big = _reduce_big([o.astype(BF16) for o in owned],
                        [lax.dynamic_index_in_dim(o, me, axis=0, keepdims=False) for o in owned])
    tiny_form = lambda d: [d[n].reshape(rows, cols) for n, rows, cols in TINY]
    med_form = lambda d: [d[n].reshape(N_DEV, rows // N_DEV, cols) for n, rows, cols in MEDIUM]
    loss, g_tiny, g_med = _reduce_small(loss, tiny_form(grads), med_form(grads))
    names = BIG_NAMES + tuple(n for n, _, _ in TINY + MEDIUM)
    form = lambda d: [d[n][0] for n in BIG_NAMES] + tiny_form(d) + med_form(d)
    updated = _adamw_update(list(g_big) + g_tiny + g_med, form(w), form(m), form(v))
    vals = dict(zip(names, updated))
    results = [[vals[n][kind].reshape(w[n].shape) for n in WEIGHT_NAMES] for kind in range(4)]
    return (loss.reshape(()), grad_x, *results[0], *results[1], *results[2], *results[3])
```

```python
import functools
import math

import jax
import jax.numpy as jnp
from jax import lax
from jax.experimental import pallas as pl
from jax.experimental.pallas import tpu as pltpu

F32 = jnp.float32
BF16 = jnp.bfloat16

D_MODEL = 1024
D_SSM = 512
D_ATTN = 512
SSM_P = 16
SSM_G = 32
SSM_N = 64
N_HEADS = 8
KV_HEADS = 2
Q_PER_KV = 4
HEAD_DIM = 64
ATT_BLOCK = 128
D_PLE = 256
D_IN = 2304
EPS = 1e-6
N_DEV = 8
N_SEG = 8
G_TILE = 8
N_GT = SSM_G // G_TILE
CH_T = G_TILE * SSM_P
ST_T = G_TILE * SSM_N
N_STATE = SSM_G * SSM_N
SCAN_UNROLL = 4
LANES = 128
VMEM_LIMIT = 60 * 1024 * 1024

ADAM_LR = 0.001
ADAM_B1 = 0.9
ADAM_B2 = 0.999
ADAM_EPS = 1e-08
ADAM_WD = 0.01
ADAM_STEP = 10

GELU_C = math.sqrt(2.0 / math.pi)
GELU_K = 0.044715
ATT_SCALE = 1.0 / math.sqrt(HEAD_DIM)
NEG_BIG = -1e30


def _mm(a, b):
    return jnp.dot(a.astype(BF16), b.astype(BF16), preferred_element_type=F32)


def _mm_nt(a, b):
    return lax.dot_general(a.astype(BF16), b.astype(BF16), (((1,), (1,)), ((), ())), preferred_element_type=F32)


def _mm_tn(a, b):
    return lax.dot_general(a.astype(BF16), b.astype(BF16), (((0,), (0,)), ((), ())), preferred_element_type=F32)


def _sigmoid(x):
    return 1.0 / (1.0 + jnp.exp(-x))


def _tc_params(sem):
    return pltpu.CompilerParams(dimension_semantics=sem, vmem_limit_bytes=VMEM_LIMIT)


def _const_spec(shape):
    nd = len(shape)
    return pl.BlockSpec(shape, lambda *_: (0,) * nd)


def _mesh_pos():
    return lax.axis_index("x"), lax.axis_index("y"), lax.axis_index("c")


ROW_CHUNK = 64


def _row_chunks(nrows, fn):
    def step(i, carry):
        fn(pl.ds(pl.multiple_of(i * ROW_CHUNK, ROW_CHUNK), ROW_CHUNK))
        return carry

    lax.fori_loop(0, nrows // ROW_CHUNK, step, 0)


def _slot(px, py, pc):
    return 4 * px + 2 * py + pc


def _allgather_weights(shards):
    n = len(shards)

    def body(*refs):
        srcs, outs, (send_sems, recv_sems) = refs[:n], refs[n:2 * n], refs[2 * n:]
        x, y, c = _mesh_pos()
        me, sibling = (x, y, c), (x, y, 1 - c)
        chips = [(1 - x, y), (x, 1 - y), (1 - x, 1 - y)]

        def copy(a, k, block, to):
            blk = outs[a].at[_slot(*block)]
            return pltpu.make_async_remote_copy(
                src_ref=blk, dst_ref=blk, send_sem=send_sems.at[7 * a + k], recv_sem=recv_sems.at[7 * a + k],
                device_id=to, device_id_type=pl.DeviceIdType.MESH)

        sends = []
        for a in range(n):
            mine = outs[a].at[_slot(*me)]

            def cast(r, mine=mine, src=srcs[a]):
                mine[r, :] = src[r, :].astype(BF16)

            _row_chunks(srcs[a].shape[0], cast)
            first = [copy(a, 0, me, sibling)] + [copy(a, 1 + j, me, (*chip, c)) for j, chip in enumerate(chips)]
            for cp in first:
                cp.start()
            sends += first
        for a in range(n):
            for j, chip in enumerate(chips):
                copy(a, 1 + j, (*chip, c), me).wait_recv()
                fwd = copy(a, 4 + j, (*chip, c), sibling)
                fwd.start()
                sends.append(fwd)
        for a in range(n):
            copy(a, 0, sibling, me).wait_recv()
            for j, chip in enumerate(chips):
                copy(a, 4 + j, (*chip, 1 - c), me).wait_recv()
        for cp in sends:
            cp.wait_send()

    vm = pl.BlockSpec(memory_space=pltpu.VMEM)
    return pl.pallas_call(
        body, name="allgather_weights",
        out_shape=tuple(jax.ShapeDtypeStruct((N_DEV,) + s.shape, BF16) for s in shards),
        in_specs=[vm] * n, out_specs=(vm,) * n,
        scratch_shapes=[pltpu.SemaphoreType.DMA((7 * n,)), pltpu.SemaphoreType.DMA((7 * n,))],
        compiler_params=pltpu.CompilerParams(vmem_limit_bytes=VMEM_LIMIT),
    )(*shards)


def _adamw(w, g, m, v):
    m = ADAM_B1 * m + (1.0 - ADAM_B1) * g
    v = ADAM_B2 * v + (1.0 - ADAM_B2) * (g * g)
    m_hat = m / (1.0 - ADAM_B1 ** ADAM_STEP)
    v_hat = v / (1.0 - ADAM_B2 ** ADAM_STEP)
    delta = -ADAM_LR * (m_hat / (jnp.sqrt(v_hat) + ADAM_EPS) + ADAM_WD * w)
    return delta, m, v


def _remote(src, dst, send_sems, recv_sems, k, to):
    return pltpu.make_async_remote_copy(src_ref=src, dst_ref=dst, send_sem=send_sems.at[k], recv_sem=recv_sems.at[k],
                                        device_id=to, device_id_type=pl.DeviceIdType.MESH)


def _reduce_big(g16, gown):
    n = len(g16)

    def body(*refs):
        g16_r, go_r, outs = (refs[i * n:(i + 1) * n] for i in range(3))
        send2, recv1, recv2 = (refs[3 * n + i * n:3 * n + (i + 1) * n] for i in range(3))
        s_send, s_recv = refs[6 * n:]
        x, y, c = _mesh_pos()
        sibling = (x, y, 1 - c)
        chips = [(1 - x, y), (x, 1 - y), (1 - x, 1 - y)]
        all_chips = [(x, y)] + chips
        lvl1 = []
        for a in range(n):
            cps = [_remote(g16_r[a].at[_slot(*chip, 1 - c)], recv1[a].at[j], s_send, s_recv, 7 * a + j, sibling)
                   for j, chip in enumerate(all_chips)]
            for cp in cps:
                cp.start()
            lvl1.append(cps)
        lvl2 = []
        for a in range(n):
            for cp in lvl1[a]:
                cp.wait_recv()
            og = outs[a]

            def partials(r, a=a, og=og):
                og[r, :] = go_r[a][r, :] + recv1[a][0, r, :].astype(F32)
                for j, chip in enumerate(chips):
                    mine16 = g16_r[a][_slot(*chip, c), r, :].astype(F32)
                    send2[a][j, r, :] = (mine16 + recv1[a][1 + j, r, :].astype(F32)).astype(BF16)

            _row_chunks(go_r[a].shape[0], partials)
            cps = [_remote(send2[a].at[j], recv2[a].at[j], s_send, s_recv, 7 * a + 4 + j, (*chip, c))
                   for j, chip in enumerate(chips)]
            for cp in cps:
                cp.start()
            lvl2.append(cps)
        for a in range(n):
            for cp in lvl2[a]:
                cp.wait_recv()
            og = outs[a]

            def total(r, a=a, og=og):
                g = og[r, :]
                for j in range(3):
                    g = g + recv2[a][j, r, :].astype(F32)
                og[r, :] = g

            _row_chunks(go_r[a].shape[0], total)
        for cps in lvl1 + lvl2:
            for cp in cps:
                cp.wait_send()

    vm = pl.BlockSpec(memory_space=pltpu.VMEM)
    scratch = ([pltpu.VMEM((3,) + t.shape, BF16) for t in gown] + [pltpu.VMEM((4,) + t.shape, BF16) for t in gown]
               + [pltpu.VMEM((3,) + t.shape, BF16) for t in gown]
               + [pltpu.SemaphoreType.DMA((7 * n,)), pltpu.SemaphoreType.DMA((7 * n,))])
    return pl.pallas_call(
        body, name="reduce_big",
        out_shape=tuple(jax.ShapeDtypeStruct(t.shape, F32) for t in gown),
        in_specs=[vm] * (2 * n), out_specs=(vm,) * n, scratch_shapes=scratch,
        compiler_params=pltpu.CompilerParams(vmem_limit_bytes=VMEM_LIMIT),
    )(*g16, *gown)


def _adamw_update(g, w, m, v):
    n = len(g)

    def body(*refs):
        g_r, w_r, m_r, v_r = (refs[i * n:(i + 1) * n] for i in range(4))
        outs = refs[4 * n:]
        for a in range(n):
            og, od, om, ov = outs[4 * a:4 * a + 4]

            def update(idx, a=a, og=og, od=od, om=om, ov=ov):
                gv = g_r[a][idx]
                d, nm, nv = _adamw(w_r[a][idx], gv, m_r[a][idx], v_r[a][idx])
                og[idx] = gv
                od[idx] = d
                om[idx] = nm
                ov[idx] = nv

            shape = g_r[a].shape
            if len(shape) == 3:
                for b in range(shape[0]):
                    update(b)
            elif shape[0] % ROW_CHUNK == 0:
                _row_chunks(shape[0], update)
            else:
                update(Ellipsis)

    vm = pl.BlockSpec(memory_space=pltpu.VMEM)
    res = pl.pallas_call(
        body, name="adamw_update",
        out_shape=tuple(jax.ShapeDtypeStruct(t.shape, F32) for t in g for _ in range(4)),
        in_specs=[vm] * (4 * n), out_specs=(vm,) * (4 * n),
        compiler_params=pltpu.CompilerParams(vmem_limit_bytes=VMEM_LIMIT),
    )(*g, *w, *m, *v)
    return [res[4 * a:4 * a + 4] for a in range(n)]


TINY = (("pre_norm_g", 1, 1024), ("post_norm_g", 1, 1024), ("pl_b_gate", 1, 1024), ("ssm_d", 1, 512),
        ("ssm_b_glu", 1, 512), ("ssm_log_step", 1, 32), ("attn_sinks", 1, 8), ("ssm_lam_re", 32, 64),
        ("ssm_lam_im", 32, 64))
MEDIUM = (("ssm_b_re", N_STATE, SSM_P), ("ssm_b_im", N_STATE, SSM_P), ("ssm_c_re", SSM_G * SSM_P, SSM_N),
          ("ssm_c_im", SSM_G * SSM_P, SSM_N))


def _stage_rows():
    offs, r = {}, 0
    for name, rows, cols in TINY + (("loss", 1, 1),):
        if rows > 1:
            r = -(-r // 8) * 8
        offs[name] = r
        r += rows if rows > 1 else max(cols // LANES, 1)
    return offs, -(-r // 8) * 8


def _reduce_small(loss, g_tiny, g_med):
    nt, nm_ = len(TINY), len(MEDIUM)
    offs, stage_rows = _stage_rows()

    def body(*refs):
        loss_r, gt, gm = refs[0], refs[1:1 + nt], refs[1 + nt:1 + nt + nm_]
        base = 1 + nt + nm_
        loss_o, out_t, out_m = refs[base], refs[base + 1:base + 1 + nt], refs[base + 1 + nt:base + 1 + nt + nm_]
        base += 1 + nt + nm_
        stage = refs[base]
        recv1, part, recv2 = (refs[base + 1 + i * nm_:base + 1 + (i + 1) * nm_] for i in range(3))
        s_send, s_recv = refs[base + 1 + 3 * nm_:]
        x, y, c = _mesh_pos()
        me = _slot(x, y, c)
        sibling = (x, y, 1 - c)
        chips = [(1 - x, y), (x, 1 - y), (1 - x, 1 - y)]
        all_chips = [(x, y)] + chips
        peers = [sibling] + [(*chip, c) for chip in chips] + [(*chip, 1 - c) for chip in chips]
        sem = iter(range(7 + 14 * nm_))
        lvl1 = []
        for a in range(nm_):
            cps = [_remote(gm[a].at[_slot(*chip, 1 - c)], recv1[a].at[j], s_send, s_recv, next(sem), sibling)
                   for j, chip in enumerate(all_chips)]
            for cp in cps:
                cp.start()
            lvl1.append(cps)
        mine = stage.at[me]
        mine[...] = jnp.zeros((stage_rows, LANES), F32)
        for (name, rows, cols), ref in zip(TINY + (("loss", 1, 1),), gt + (loss_r,)):
            r0 = offs[name]
            if rows > 1:
                mine[r0:r0 + rows, 0:cols] = ref[...]
            elif cols >= LANES:
                for i in range(cols // LANES):
                    mine[r0 + i:r0 + i + 1, :] = ref[:, i * LANES:(i + 1) * LANES]
            else:
                mine[r0:r0 + 1, 0:cols] = ref[...]
        tiny_cps = [_remote(mine, mine, s_send, s_recv, next(sem), peer) for peer in peers]
        for cp in tiny_cps:
            cp.start()
        lvl2 = []
        for a in range(nm_):
            for cp in lvl1[a]:
                cp.wait_recv()
            for j, chip in enumerate(all_chips):
                part[a][j] = gm[a][_slot(*chip, c)] + recv1[a][j]
            cps = [_remote(part[a].at[1 + j], recv2[a].at[j], s_send, s_recv, next(sem), (*chip, c))
                   for j, chip in enumerate(chips)]
            for cp in cps:
                cp.start()
            lvl2.append(cps)
        lvl3 = []
        for a in range(nm_):
            for cp in lvl2[a]:
                cp.wait_recv()
            blk = out_m[a].at[me]
            blk[...] = ((part[a][0] + recv2[a][0]) + recv2[a][1]) + recv2[a][2]
            cps = [_remote(blk, blk, s_send, s_recv, next(sem), peer) for peer in peers]
            for cp in cps:
                cp.start()
            lvl3.append(cps)
        for cp in tiny_cps:
            cp.wait_recv()
        tot = stage[0]
        for d in range(1, N_DEV):
            tot = tot + stage[d]
        loss_o[...] = tot[offs["loss"]:offs["loss"] + 1, 0:1]
        for k, (name, rows, cols) in enumerate(TINY):
            r0 = offs[name]
            if rows > 1:
                out_t[k][...] = tot[r0:r0 + rows, 0:cols]
            elif cols >= LANES:
                for i in range(cols // LANES):
                    out_t[k][:, i * LANES:(i + 1) * LANES] = tot[r0 + i:r0 + i + 1, :]
            else:
                out_t[k][...] = tot[r0:r0 + 1, 0:cols]
        for cps in lvl3:
            for cp in cps:
                cp.wait_recv()
        for cps in lvl1 + lvl2 + lvl3 + [tiny_cps]:
            for cp in cps:
                cp.wait_send()

    vmem = pl.BlockSpec(memory_space=pltpu.VMEM)
    t_shapes = [jax.ShapeDtypeStruct((rows, cols), F32) for _, rows, cols in TINY]
    m_shapes = [jax.ShapeDtypeStruct((N_DEV, rows // N_DEV, cols), F32) for _, rows, cols in MEDIUM]
    blk = [(rows // N_DEV, cols) for _, rows, cols in MEDIUM]
    scratch = ([pltpu.VMEM((N_DEV, stage_rows, LANES), F32)]
               + [pltpu.VMEM((4,) + b, F32) for b in blk] + [pltpu.VMEM((4,) + b, F32) for b in blk]
               + [pltpu.VMEM((3,) + b, F32) for b in blk]
               + [pltpu.SemaphoreType.DMA((7 + 14 * nm_,)), pltpu.SemaphoreType.DMA((7 + 14 * nm_,))])
    res = pl.pallas_call(
        body, name="reduce_small",
        out_shape=(jax.ShapeDtypeStruct((1, 1), F32),) + tuple(t_shapes) + tuple(m_shapes),
        in_specs=[vmem] * (1 + nt + nm_), out_specs=(vmem,) * (1 + nt + nm_), scratch_shapes=scratch,
        compiler_params=pltpu.CompilerParams(vmem_limit_bytes=VMEM_LIMIT),
    )(loss, *g_tiny, *g_med)
    return res[0], list(res[1:1 + nt]), list(res[1 + nt:])


def _in_proj(x2, g_pre, w_in, bl, seg):
    t = x2.shape[0]
    tm = seg

    def body(x_ref, g_ref, w_ref, u_ref, zs_ref, q_ref, k_ref, v_ref, za_ref):
        xv = x_ref[...]
        r = lax.rsqrt(jnp.mean(xv * xv, axis=-1, keepdims=True) + EPS)
        hn = xv * r * g_ref[...]
        proj = _mm(hn, w_ref[...])
        u_ref[0] = proj[:, 0:512]
        zs_ref[...] = proj[:, 512:1024]
        q_ref[...] = proj[:, 1024:1536].astype(BF16)
        k_ref[...] = proj[:, 1536:1664].astype(BF16)
        v_ref[...] = proj[:, 1664:1792].astype(BF16)
        za_ref[...] = proj[:, 1792:2304]

    row = lambda w: pl.BlockSpec((tm, w), lambda i: (i, 0))
    return pl.pallas_call(
        body, name="in_proj", grid=(t // tm,),
        in_specs=[row(D_MODEL), _const_spec((1, D_MODEL)), _const_spec((D_MODEL, D_IN))],
        out_specs=(pl.BlockSpec((1, tm, D_SSM), lambda i: (i // N_SEG, 0, i % N_SEG)),
                   row(512), row(512), row(128), row(128), row(512)),
        out_shape=(jax.ShapeDtypeStruct((bl, seg, N_SEG * D_SSM), F32),
                   jax.ShapeDtypeStruct((t, 512), F32), jax.ShapeDtypeStruct((t, 512), BF16),
                   jax.ShapeDtypeStruct((t, 128), BF16), jax.ShapeDtypeStruct((t, 128), BF16),
                   jax.ShapeDtypeStruct((t, 512), F32)),
        compiler_params=_tc_params(("arbitrary",)),
    )(x2, g_pre, w_in)


def _ssm_prep(lam_re, lam_im, log_step, b_re, b_im, seg):
    def body(lr_ref, li_ref, ls_ref, br_ref, bi_ref, lrr_ref, lir_ref, lsr_ref,
             ar_ref, ai_ref, bbr_ref, bbi_ref, pr_ref, pi_ref):
        lr, li = lr_ref[...], li_ref[...]
        step = jnp.exp(ls_ref[...])
        mag = jnp.exp(lr * step)
        ar = mag * jnp.cos(li * step)
        ai = mag * jnp.sin(li * step)
        ar_ref[...] = ar
        ai_ref[...] = ai
        den = lr * lr + li * li
        cr = ((ar - 1.0) * lr + ai * li) / den
        ci = (ai * lr - (ar - 1.0) * li) / den
        br, bi = br_ref[...], bi_ref[...]
        bbr_ref[...] = cr * br - ci * bi
        bbi_ref[...] = cr * bi + ci * br
        stepr = jnp.exp(lsr_ref[...])
        k = (lax.broadcasted_iota(jnp.int32, (8, N_STATE), 0) + 1).astype(F32)
        magk = jnp.exp(k * (lrr_ref[...] * stepr))
        ang = k * (lir_ref[...] * stepr)
        pr_ref[0:8, :] = magk * jnp.cos(ang)
        pi_ref[0:8, :] = magk * jnp.sin(ang)
        n = 8
        while n < seg:
            tr, ti = pr_ref[n - 1:n, :], pi_ref[n - 1:n, :]
            xr, xi = pr_ref[0:n, :], pi_ref[0:n, :]
            pr_ref[n:2 * n, :] = xr * tr - xi * ti
            pi_ref[n:2 * n, :] = xr * ti + xi * tr
            n *= 2

    col = jax.ShapeDtypeStruct((N_STATE, 1), F32)
    mat = jax.ShapeDtypeStruct((N_STATE, SSM_P), F32)
    pw = jax.ShapeDtypeStruct((seg, N_STATE), F32)
    vm = pl.BlockSpec(memory_space=pltpu.VMEM)
    return pl.pallas_call(
        body, name="ssm_prep", out_shape=(col, col, mat, mat, pw, pw),
        in_specs=[vm] * 8, out_specs=(vm,) * 6,
    )(lam_re, lam_im, log_step, b_re, b_im, lam_re.reshape(1, N_STATE), lam_im.reshape(1, N_STATE),
      log_step.reshape(1, N_STATE))


def _seg_rows(t):
    if isinstance(t, int):
        return pl.ds(t * N_SEG, N_SEG)
    return pl.ds(pl.multiple_of(t * N_SEG, N_SEG), N_SEG)


def _scan_forward(xs, a_re, a_im, pw_re, pw_im, cs, seg):
    are = jnp.broadcast_to(a_re, (N_SEG, ST_T))
    aim = jnp.broadcast_to(a_im, (N_SEG, ST_T))

    def steps(k, carry):
        xr, xi = carry
        for j in range(SCAN_UNROLL):
            r = pl.multiple_of((k * SCAN_UNROLL + j) * N_SEG, N_SEG)
            nr = are * xr - aim * xi + xs[pl.ds(r, N_SEG), 0:ST_T]
            ni = are * xi + aim * xr + xs[pl.ds(r, N_SEG), ST_T:2 * ST_T]
            xs[pl.ds(r, N_SEG), 0:ST_T] = nr
            xs[pl.ds(r, N_SEG), ST_T:2 * ST_T] = ni
            xr, xi = nr, ni
        return xr, xi

    zero = jnp.zeros((N_SEG, ST_T), F32)
    fr, fi = lax.fori_loop(0, seg // SCAN_UNROLL, steps, (zero, zero))
    sr, si = pw_re[seg - 1:seg, :], pw_im[seg - 1:seg, :]
    cr = jnp.zeros((1, ST_T), F32)
    ci = jnp.zeros((1, ST_T), F32)
    cs[0:1, :] = cr
    cs[8:9, :] = ci
    for s in range(1, N_SEG):
        ncr = sr * cr - si * ci + fr[s - 1:s, :]
        nci = sr * ci + si * cr + fi[s - 1:s, :]
        cr, ci = ncr, nci
        cs[s:s + 1, :] = cr
        cs[8 + s:9 + s, :] = ci
    car, cai = cs[0:8, :], cs[8:16, :]

    def fix(t, _):
        r = pl.multiple_of(t * N_SEG, N_SEG)
        pr, pi = pw_re[pl.ds(t, 1), :], pw_im[pl.ds(t, 1), :]
        xs[pl.ds(r, N_SEG), 0:ST_T] = xs[pl.ds(r, N_SEG), 0:ST_T] + (pr * car - pi * cai)
        xs[pl.ds(r, N_SEG), ST_T:2 * ST_T] = xs[pl.ds(r, N_SEG), ST_T:2 * ST_T] + (pr * cai + pi * car)
        return 0

    lax.fori_loop(0, seg, fix, 0, unroll=SCAN_UNROLL)


def _ssm_forward(u_perm, bcat, ccat, a_re, a_im, pw_re, pw_im, d_row, seg):
    bl, rows, _ = u_perm.shape

    def body(u_ref, b_ref, c_ref, ar_ref, ai_ref, pr_ref, pi_ref, d_ref, y_ref, xs, cs):
        u = u_ref[0]
        xs[...] = _mm(u, b_ref[0])
        _scan_forward(xs, ar_ref[...], ai_ref[...], pr_ref, pi_ref, cs, seg)
        y_ref[0] = _mm(xs[...], c_ref[0]) + d_ref[...] * u

    return pl.pallas_call(
        body, name="ssm_forward", grid=(bl, N_GT),
        in_specs=[pl.BlockSpec((1, rows, CH_T), lambda b, j: (b, 0, j)),
                  pl.BlockSpec((1, CH_T, 2 * ST_T), lambda b, j: (j, 0, 0)),
                  pl.BlockSpec((1, 2 * ST_T, CH_T), lambda b, j: (j, 0, 0)),
                  pl.BlockSpec((1, ST_T), lambda b, j: (0, j)), pl.BlockSpec((1, ST_T), lambda b, j: (0, j)),
                  pl.BlockSpec((seg, ST_T), lambda b, j: (0, j)), pl.BlockSpec((seg, ST_T), lambda b, j: (0, j)),
                  pl.BlockSpec((1, CH_T), lambda b, j: (0, j))],
        out_specs=pl.BlockSpec((1, rows, CH_T), lambda b, j: (b, 0, j)),
        out_shape=jax.ShapeDtypeStruct((bl, rows, D_SSM), F32),
        scratch_shapes=[pltpu.VMEM((rows, 2 * ST_T), F32), pltpu.VMEM((16, ST_T), F32)],
        compiler_params=_tc_params(("arbitrary", "arbitrary")),
    )(u_perm, bcat, ccat, a_re, a_im, pw_re, pw_im, d_row)


def _ssm_backward(u_perm, dy_perm, bcat, bcat_t, ccat_t, a_re, a_im, pw_re, pw_im, d_row, seg):
    bl, rows, _ = u_perm.shape

    def body(u_ref, dy_ref, b_ref, bt_ref, ct_ref, ar_ref, ai_ref, pr_ref, pi_ref, d_ref,
             du_ref, db_ref, dc_ref, dar_ref, dai_ref, dd_ref, xs, ls, cs, cl):
        b = pl.program_id(1)
        u = u_ref[0]
        dy = dy_ref[0]
        xs[...] = _mm(u, b_ref[0])
        _scan_forward(xs, ar_ref[...], ai_ref[...], pr_ref, pi_ref, cs, seg)
        ls[...] = _mm(dy, ct_ref[0])
        are = jnp.broadcast_to(ar_ref[...], (N_SEG, ST_T))
        aim = jnp.broadcast_to(ai_ref[...], (N_SEG, ST_T))

        def steps(k, carry):
            lr, li = carry
            for j in range(SCAN_UNROLL):
                r = pl.multiple_of((seg - 1 - (k * SCAN_UNROLL + j)) * N_SEG, N_SEG)
                nr = are * lr + aim * li + ls[pl.ds(r, N_SEG), 0:ST_T]
                ni = are * li - aim * lr + ls[pl.ds(r, N_SEG), ST_T:2 * ST_T]
                ls[pl.ds(r, N_SEG), 0:ST_T] = nr
                ls[pl.ds(r, N_SEG), ST_T:2 * ST_T] = ni
                lr, li = nr, ni
            return lr, li

        zero = jnp.zeros((N_SEG, ST_T), F32)
        fr, fi = lax.fori_loop(0, seg // SCAN_UNROLL, steps, (zero, zero))
        sr, si = pr_ref[seg - 1:seg, :], pi_ref[seg - 1:seg, :]
        cr = jnp.zeros((1, ST_T), F32)
        ci = jnp.zeros((1, ST_T), F32)
        cl[7:8, :] = cr
        cl[15:16, :] = ci
        for s in range(N_SEG - 2, -1, -1):
            ncr = sr * cr + si * ci + fr[s + 1:s + 2, :]
            nci = sr * ci - si * cr + fi[s + 1:s + 2, :]
            cr, ci = ncr, nci
            cl[s:s + 1, :] = cr
            cl[8 + s:9 + s, :] = ci
        clr, cli = cl[0:8, :], cl[8:16, :]

        def fix_rows(rows, t, xpr, xpi, acc):
            dr, di = acc
            pr, pi = pr_ref[pl.ds(seg - 1 - t, 1), :], pi_ref[pl.ds(seg - 1 - t, 1), :]
            lr = ls[rows, 0:ST_T] + (pr * clr + pi * cli)
            li = ls[rows, ST_T:2 * ST_T] + (pr * cli - pi * clr)
            ls[rows, 0:ST_T] = lr
            ls[rows, ST_T:2 * ST_T] = li
            return dr + (lr * xpr + li * xpi), di + (li * xpr - lr * xpi)

        def fix_at(t, acc):
            prev = _seg_rows(t - 1)
            return fix_rows(_seg_rows(t), t, xs[prev, 0:ST_T], xs[prev, ST_T:2 * ST_T], acc)

        def fix(k, acc):
            for j in range(SCAN_UNROLL):
                acc = fix_at(k * SCAN_UNROLL + j, acc)
            return acc

        acc = fix_rows(pl.ds(0, N_SEG), 0, cs[0:8, :], cs[8:16, :], (zero, zero))
        for t in range(1, SCAN_UNROLL):
            acc = fix_at(t, acc)
        dr, di = lax.fori_loop(1, seg // SCAN_UNROLL, fix, acc)
        dar = jnp.sum(dr, axis=0, keepdims=True)
        dai = jnp.sum(di, axis=0, keepdims=True)
        lall = ls[...]
        du_ref[0] = _mm(lall, bt_ref[0]) + d_ref[...] * dy
        dbp = _mm_tn(u, lall)
        dcp = _mm_tn(dy, xs[...])
        ddp = jnp.sum(dy * u, axis=0, keepdims=True)

        @pl.when(b == 0)
        def _():
            db_ref[0] = dbp
            dc_ref[0] = dcp
            dar_ref[...] = dar
            dai_ref[...] = dai
            dd_ref[...] = ddp

        @pl.when(b != 0)
        def _():
            db_ref[0] += dbp
            dc_ref[0] += dcp
            dar_ref[...] += dar
            dai_ref[...] += dai
            dd_ref[...] += ddp

    tile3 = lambda r, c: pl.BlockSpec((1, r, c), lambda j, b: (j, 0, 0))
    lane = lambda r, c: pl.BlockSpec((r, c), lambda j, b: (0, j))
    act = pl.BlockSpec((1, rows, CH_T), lambda j, b: (b, 0, j))
    return pl.pallas_call(
        body, name="ssm_backward", grid=(N_GT, bl),
        in_specs=[act, act, tile3(CH_T, 2 * ST_T), tile3(2 * ST_T, CH_T), tile3(CH_T, 2 * ST_T),
                  lane(1, ST_T), lane(1, ST_T), lane(seg, ST_T), lane(seg, ST_T), lane(1, CH_T)],
        out_specs=(act, tile3(CH_T, 2 * ST_T), tile3(CH_T, 2 * ST_T), lane(1, ST_T), lane(1, ST_T), lane(1, CH_T)),
        out_shape=(jax.ShapeDtypeStruct((bl, rows, D_SSM), F32),
                   jax.ShapeDtypeStruct((N_GT, CH_T, 2 * ST_T), F32), jax.ShapeDtypeStruct((N_GT, CH_T, 2 * ST_T), F32),
                   jax.ShapeDtypeStruct((1, N_STATE), F32), jax.ShapeDtypeStruct((1, N_STATE), F32),
                   jax.ShapeDtypeStruct((1, D_SSM), F32)),
        scratch_shapes=[pltpu.VMEM((rows, 2 * ST_T), F32), pltpu.VMEM((rows, 2 * ST_T), F32),
                        pltpu.VMEM((16, ST_T), F32), pltpu.VMEM((16, ST_T), F32)],
        compiler_params=_tc_params(("arbitrary", "arbitrary")),
    )(u_perm, dy_perm, bcat, bcat_t, ccat_t, a_re, a_im, pw_re, pw_im, d_row)


def _ssm_param_grads(lam_re, lam_im, log_step, b_re, b_im, a_re, a_im, da_re, da_im, dbb_re, dbb_im):
    def body(lr_ref, li_ref, ls_ref, br_ref, bi_ref, ar_ref, ai_ref, gar_ref, gai_ref, gbr_ref, gbi_ref,
             dlr_ref, dli_ref, dls_ref, dbr_ref, dbi_ref):
        lr, li = lr_ref[...], li_ref[...]
        step = jnp.exp(ls_ref[...])
        ar, ai = ar_ref[...], ai_ref[...]
        den = lr * lr + li * li
        cr = ((ar - 1.0) * lr + ai * li) / den
        ci = (ai * lr - (ar - 1.0) * li) / den
        br, bi = br_ref[...], bi_ref[...]
        gbr, gbi = gbr_ref[...], gbi_ref[...]
        dbr_ref[...] = cr * gbr + ci * gbi
        dbi_ref[...] = cr * gbi - ci * gbr
        gcr = jnp.sum(br * gbr + bi * gbi, axis=-1, keepdims=True)
        gci = jnp.sum(br * gbi - bi * gbr, axis=-1, keepdims=True)
        ilr, ili = lr / den, -li / den
        gar = gar_ref[...] + (ilr * gcr + ili * gci)
        gai = gai_ref[...] + (ilr * gci - ili * gcr)
        qr, qi = cr * ilr - ci * ili, cr * ili + ci * ilr
        glr = -(qr * gcr + qi * gci)
        gli = -(qr * gci - qi * gcr)
        gwr = ar * gar + ai * gai
        gwi = ar * gai - ai * gar
        dlr_ref[...] = glr + step * gwr
        dli_ref[...] = gli + step * gwi
        dls_ref[...] = (lr * gwr + li * gwi) * step

    col = jax.ShapeDtypeStruct((N_STATE, 1), F32)
    mat = jax.ShapeDtypeStruct((N_STATE, SSM_P), F32)
    vm = pl.BlockSpec(memory_space=pltpu.VMEM)
    return pl.pallas_call(
        body, name="ssm_param_grads", out_shape=(col, col, col, mat, mat),
        in_specs=[vm] * 11, out_specs=(vm,) * 5,
    )(lam_re, lam_im, log_step, b_re, b_im, a_re, a_im, da_re, da_im, dbb_re, dbb_im)


ROWS4 = Q_PER_KV * ATT_BLOCK


def _att_dist_mask(first_block):
    qi = lax.broadcasted_iota(jnp.int32, (ROWS4, 2 * ATT_BLOCK), 0) & (ATT_BLOCK - 1)
    si = lax.broadcasted_iota(jnp.int32, (ROWS4, 2 * ATT_BLOCK), 1)
    dist = qi + ATT_BLOCK - si
    valid = (dist >= 0) & (dist < ATT_BLOCK) & ((si >= ATT_BLOCK) | jnp.logical_not(first_block))
    return dist.astype(F32), valid


def _stack_heads(x, kv):
    return jnp.concatenate([x[:, (kv * Q_PER_KV + g) * HEAD_DIM:(kv * Q_PER_KV + g + 1) * HEAD_DIM]
                            for g in range(Q_PER_KV)], axis=0)


def _stack_cols(x, kv):
    return jnp.concatenate([x[:, kv * Q_PER_KV + g:kv * Q_PER_KV + g + 1] for g in range(Q_PER_KV)], axis=0)


def _per_head_col(vals):
    return jnp.concatenate([jnp.full((ATT_BLOCK, 1), v, F32) for v in vals], axis=0)


def _attn_forward(q, k, v, sinks, bl, nb):
    t = q.shape[0]

    def body(sink_ref, q_ref, kp_ref, kc_ref, vp_ref, vc_ref, o_ref, lse_ref):
        i = pl.program_id(1)
        dist4, valid4 = _att_dist_mask(i == 0)
        dist, valid = dist4[0:ATT_BLOCK, :], valid4[0:ATT_BLOCK, :]
        kk = jnp.concatenate([kp_ref[...], kc_ref[...]], axis=0)
        vv = jnp.concatenate([vp_ref[...], vc_ref[...]], axis=0)
        qv = q_ref[...]
        for h in range(N_HEADS):
            kv = h // Q_PER_KV
            slope = 2.0 ** (-(h + 1))
            qh = qv[:, h * HEAD_DIM:(h + 1) * HEAD_DIM]
            kh = kk[:, kv * HEAD_DIM:(kv + 1) * HEAD_DIM]
            vh = vv[:, kv * HEAD_DIM:(kv + 1) * HEAD_DIM]
            s = _mm_nt(qh, kh) * ATT_SCALE - slope * dist
            s = jnp.where(valid, s, NEG_BIG)
            sink = sink_ref[h]
            m = jnp.maximum(jnp.max(s, axis=-1, keepdims=True), sink)
            e = jnp.exp(s - m)
            den = jnp.sum(e, axis=-1, keepdims=True) + jnp.exp(sink - m)
            o_ref[:, h * HEAD_DIM:(h + 1) * HEAD_DIM] = _mm(e, vh) * (1.0 / den)
            lse_ref[:, h:h + 1] = m + jnp.log(den)

    cur = lambda w: pl.BlockSpec((ATT_BLOCK, w), lambda b, i: (b * nb + i, 0))
    prev = lambda w: pl.BlockSpec((ATT_BLOCK, w), lambda b, i: (b * nb + jnp.maximum(i - 1, 0), 0))
    return pl.pallas_call(
        body, name="attn_forward", grid=(bl, nb),
        in_specs=[pl.BlockSpec(memory_space=pltpu.SMEM), cur(512), prev(128), cur(128), prev(128), cur(128)],
        out_specs=(cur(512), cur(N_HEADS)),
        out_shape=(jax.ShapeDtypeStruct((t, D_ATTN), F32), jax.ShapeDtypeStruct((t, N_HEADS), F32)),
        compiler_params=_tc_params(("arbitrary", "arbitrary")),
    )(sinks, q, k, k, v, v)


def _attn_backward(q, k, v, o, do, lse, sinks, bl, nb):
    t = q.shape[0]

    def body(sink_ref, qc_ref, qn_ref, kp_ref, kc_ref, vp_ref, vc_ref, oc_ref, on_ref, doc_ref, don_ref,
             lc_ref, ln_ref, dq_ref, dk_ref, dv_ref, ds_ref):
        b, i = pl.program_id(0), pl.program_id(1)
        dist, valid = _att_dist_mask(i == 0)
        has_next = i + 1 < nb
        dist_n = dist[:, 0:ATT_BLOCK]
        valid_n = (dist_n < ATT_BLOCK) & has_next
        kk = jnp.concatenate([kp_ref[...], kc_ref[...]], axis=0)
        vv = jnp.concatenate([vp_ref[...], vc_ref[...]], axis=0)
        qc, qn = qc_ref[...], qn_ref[...]
        oc, on = oc_ref[...], on_ref[...]
        doc, don = doc_ref[...], don_ref[...]
        lc, ln = lc_ref[...], ln_ref[...]
        dsink_cols = []
        for kv in range(KV_HEADS):
            heads = range(kv * Q_PER_KV, (kv + 1) * Q_PER_KV)
            kh = kk[:, kv * HEAD_DIM:(kv + 1) * HEAD_DIM]
            vh = vv[:, kv * HEAD_DIM:(kv + 1) * HEAD_DIM]
            khc, vhc = kh[ATT_BLOCK:, :], vh[ATT_BLOCK:, :]
            slope = _per_head_col([2.0 ** (-(h + 1)) for h in heads])
            sink = _per_head_col([sink_ref[h] for h in heads])
            q4, do4 = _stack_heads(qc, kv), _stack_heads(doc, kv)
            delta = jnp.sum(do4 * _stack_heads(oc, kv), axis=-1, keepdims=True)
            lse4 = _stack_cols(lc, kv)
            s = _mm_nt(q4, kh) * ATT_SCALE - slope * dist
            p = jnp.where(valid, jnp.exp(s - lse4), 0.0)
            dsc = p * (_mm_nt(do4, vh) - delta)
            dq4 = _mm(dsc, kh) * ATT_SCALE
            dk_acc = _mm_tn(dsc[:, ATT_BLOCK:], q4)
            dv_acc = _mm_tn(p[:, ATT_BLOCK:], do4)
            dsink4 = jnp.exp(sink - lse4) * delta
            q4n, do4n = _stack_heads(qn, kv), _stack_heads(don, kv)
            delta_n = jnp.sum(do4n * _stack_heads(on, kv), axis=-1, keepdims=True)
            s2 = _mm_nt(q4n, khc) * ATT_SCALE - slope * dist_n
            p2 = jnp.where(valid_n, jnp.exp(s2 - _stack_cols(ln, kv)), 0.0)
            ds2 = p2 * (_mm_nt(do4n, vhc) - delta_n)
            dk_acc += _mm_tn(ds2, q4n)
            dv_acc += _mm_tn(p2, do4n)
            dk_ref[:, kv * HEAD_DIM:(kv + 1) * HEAD_DIM] = dk_acc * ATT_SCALE
            dv_ref[:, kv * HEAD_DIM:(kv + 1) * HEAD_DIM] = dv_acc
            for g, h in enumerate(heads):
                rows = slice(g * ATT_BLOCK, (g + 1) * ATT_BLOCK)
                dq_ref[:, h * HEAD_DIM:(h + 1) * HEAD_DIM] = dq4[rows, :]
                dsink_cols.append(-jnp.sum(dsink4[rows, :], axis=0, keepdims=True))
        dsink = jnp.concatenate(dsink_cols, axis=1)

        @pl.when((b == 0) & (i == 0))
        def _():
            ds_ref[...] = dsink

        @pl.when((b != 0) | (i != 0))
        def _():
            ds_ref[...] += dsink

    cur = lambda w: pl.BlockSpec((ATT_BLOCK, w), lambda b, i: (b * nb + i, 0))
    prev = lambda w: pl.BlockSpec((ATT_BLOCK, w), lambda b, i: (b * nb + jnp.maximum(i - 1, 0), 0))
    nxt = lambda w: pl.BlockSpec((ATT_BLOCK, w), lambda b, i: (b * nb + jnp.minimum(i + 1, nb - 1), 0))
    return pl.pallas_call(
        body, name="attn_backward", grid=(bl, nb),
        in_specs=[pl.BlockSpec(memory_space=pltpu.SMEM), cur(512), nxt(512), prev(128), cur(128), prev(128), cur(128),
                  cur(512), nxt(512), cur(512), nxt(512), cur(N_HEADS), nxt(N_HEADS)],
        out_specs=(cur(512), cur(128), cur(128), pl.BlockSpec((1, N_HEADS), lambda b, i: (0, 0))),
        out_shape=(jax.ShapeDtypeStruct((t, D_ATTN), F32), jax.ShapeDtypeStruct((t, 128), F32),
                   jax.ShapeDtypeStruct((t, 128), F32), jax.ShapeDtypeStruct((1, N_HEADS), F32)),
        compiler_params=_tc_params(("arbitrary", "arbitrary")),
    )(sinks, q, q, k, k, v, v, o, o, do, do, lse, lse)


def _mix_forward_backward(x2, y_perm, z_ssm, attn, z_attn, p2, target2, w_glu, b_glu, w_out, g_post, w_gate, b_gate,
                          w_proj, bl, seg):
    t = x2.shape[0]
    tm = seg

    def body(x_ref, y_ref, zs_ref, at_ref, za_ref, p_ref, tg_ref,
             wglu_ref, bglu_ref, wout_ref, gpost_ref, wgate_ref, bgate_ref, wproj_ref,
             loss_ref, dh1_ref, dy_ref, dzs_ref, dat_ref, dza_ref,
             dwglu_ref, dbglu_ref, dwout_ref, dgpost_ref, dwgate_ref, dbgate_ref, dwproj_ref):
        i = pl.program_id(0)
        y = y_ref[0]
        u3 = GELU_C * (y + GELU_K * y * y * y)
        th = jnp.tanh(u3)
        gl = 0.5 * y * (1.0 + th)
        a = _mm(gl, wglu_ref[...]) + bglu_ref[...]
        sa = _sigmoid(a)
        glu = gl * sa
        zs = zs_ref[...]
        sgs = _sigmoid(zs)
        ssm_out = glu * (zs * sgs)
        za = za_ref[...]
        sga = _sigmoid(za)
        at = at_ref[...]
        attn_out = at * (za * sga)
        cat = jnp.concatenate([ssm_out, attn_out], axis=-1).astype(BF16)
        mixed = _mm(cat, wout_ref[...])
        r2 = lax.rsqrt(jnp.mean(mixed * mixed, axis=-1, keepdims=True) + EPS)
        nhat = mixed * r2
        gpost = gpost_ref[...]
        h1 = x_ref[...] + nhat * gpost
        gate = _sigmoid(_mm(h1, wgate_ref[...]) + bgate_ref[...])
        pv = p_ref[...]
        pp = _mm(pv, wproj_ref[...])
        h2 = h1 + gate * pp
        err = h2 - tg_ref[...]
        loss_part = jnp.sum(jnp.sum(err * err, axis=-1, keepdims=True), axis=0, keepdims=True) * (0.5 / D_MODEL)
        dh2 = err * (1.0 / D_MODEL)
        dgp = dh2 * pp * gate * (1.0 - gate)
        dpp = dh2 * gate
        dh1 = dh2 + _mm_nt(dgp, wgate_ref[...])
        dh1_ref[...] = dh1
        dnhat = dh1 * gpost
        dmixed = r2 * (dnhat - nhat * jnp.mean(dnhat * nhat, axis=-1, keepdims=True))
        dcat = _mm_nt(dmixed, wout_ref[...])
        dso, dao = dcat[:, 0:D_SSM], dcat[:, D_SSM:]
        dat_ref[...] = dao * (za * sga)
        dza_ref[...] = dao * at * (sga * (1.0 + za * (1.0 - sga)))
        dzs_ref[...] = dso * glu * (sgs * (1.0 + zs * (1.0 - sgs)))
        dglu = dso * (zs * sgs)
        da = dglu * gl * sa * (1.0 - sa)
        dgl = dglu * sa + _mm_nt(da, wglu_ref[...])
        dgelu = 0.5 * (1.0 + th) + 0.5 * y * (1.0 - th * th) * (GELU_C * (1.0 + 3.0 * GELU_K * y * y))
        dy_ref[0] = dgl * dgelu
        parts = (
            (dwglu_ref, _mm_tn(gl, da)), (dbglu_ref, jnp.sum(da, axis=0, keepdims=True)),
            (dwout_ref, _mm_tn(cat, dmixed)), (dgpost_ref, jnp.sum(dh1 * nhat, axis=0, keepdims=True)),
            (dwgate_ref, _mm_tn(h1, dgp)), (dbgate_ref, jnp.sum(dgp, axis=0, keepdims=True)),
            (dwproj_ref, _mm_tn(pv, dpp)), (loss_ref, loss_part),
        )

        @pl.when(i == 0)
        def _():
            for ref, val in parts:
                ref[...] = val

        @pl.when(i != 0)
        def _():
            for ref, val in parts:
                ref[...] += val

    row = lambda w: pl.BlockSpec((tm, w), lambda i: (i, 0))
    perm = pl.BlockSpec((1, tm, D_SSM), lambda i: (i // N_SEG, 0, i % N_SEG))
    perm_shape = jax.ShapeDtypeStruct((bl, seg, N_SEG * D_SSM), F32)
    acc = lambda r, c: (_const_spec((r, c)), jax.ShapeDtypeStruct((r, c), F32))
    accs = [acc(D_SSM, D_SSM), acc(1, D_SSM), acc(D_MODEL, D_MODEL), acc(1, D_MODEL), acc(D_MODEL, D_MODEL),
            acc(1, D_MODEL), acc(D_PLE, D_MODEL)]
    return pl.pallas_call(
        body, name="mix_forward_backward", grid=(t // tm,),
        in_specs=[row(D_MODEL), perm, row(512), row(512), row(512), row(D_PLE), row(D_MODEL),
                  _const_spec((D_SSM, D_SSM)), _const_spec((1, D_SSM)), _const_spec((D_MODEL, D_MODEL)),
                  _const_spec((1, D_MODEL)), _const_spec((D_MODEL, D_MODEL)), _const_spec((1, D_MODEL)),
                  _const_spec((D_PLE, D_MODEL))],
        out_specs=(_const_spec((1, 1)), row(D_MODEL), perm, row(512), row(512), row(512)) + tuple(a[0] for a in accs),
        out_shape=(jax.ShapeDtypeStruct((1, 1), F32), jax.ShapeDtypeStruct((t, D_MODEL), F32), perm_shape,
                   jax.ShapeDtypeStruct((t, 512), F32), jax.ShapeDtypeStruct((t, 512), F32),
                   jax.ShapeDtypeStruct((t, 512), F32)) + tuple(a[1] for a in accs),
        compiler_params=_tc_params(("arbitrary",)),
    )(x2, y_perm, z_ssm, attn, z_attn, p2, target2, w_glu, b_glu, w_out, g_post, w_gate, b_gate, w_proj)


def _in_backward(x2, dh1, du_perm, dz_ssm, dq, dk, dv, dz_attn, g_pre, w_in, bl, seg):
    t = x2.shape[0]
    tm = seg

    def body(x_ref, dh1_ref, du_ref, dzs_ref, dq_ref, dk_ref, dv_ref, dza_ref, g_ref, w_ref,
             gx_ref, dw_ref, dg_ref):
        i = pl.program_id(0)
        xv = x_ref[...]
        r = lax.rsqrt(jnp.mean(xv * xv, axis=-1, keepdims=True) + EPS)
        xhat = xv * r
        g = g_ref[...]
        hn = (xhat * g).astype(BF16)
        dproj = jnp.concatenate([du_ref[0].astype(BF16), dzs_ref[...].astype(BF16), dq_ref[...].astype(BF16),
                                 dk_ref[...].astype(BF16), dv_ref[...].astype(BF16), dza_ref[...].astype(BF16)],
                                axis=-1)
        dhn = _mm_nt(dproj, w_ref[...])
        dxhat = dhn * g
        gx_ref[...] = dh1_ref[...] + r * (dxhat - xhat * jnp.mean(dxhat * xhat, axis=-1, keepdims=True))
        dwp = _mm_tn(hn, dproj)
        dgp = jnp.sum(dhn * xhat, axis=0, keepdims=True)

        @pl.when(i == 0)
        def _():
            dw_ref[...] = dwp
            dg_ref[...] = dgp

        @pl.when(i != 0)
        def _():
            dw_ref[...] += dwp
            dg_ref[...] += dgp

    row = lambda w: pl.BlockSpec((tm, w), lambda i: (i, 0))
    perm = pl.BlockSpec((1, tm, D_SSM), lambda i: (i // N_SEG, 0, i % N_SEG))
    return pl.pallas_call(
        body, name="in_backward", grid=(t // tm,),
        in_specs=[row(D_MODEL), row(D_MODEL), perm, row(512), row(512), row(128), row(128), row(512),
                  _const_spec((1, D_MODEL)), _const_spec((D_MODEL, D_IN))],
        out_specs=(row(D_MODEL), _const_spec((D_MODEL, D_IN)), _const_spec((1, D_MODEL))),
        out_shape=(jax.ShapeDtypeStruct((t, D_MODEL), F32), jax.ShapeDtypeStruct((D_MODEL, D_IN), F32),
                   jax.ShapeDtypeStruct((1, D_MODEL), F32)),
        compiler_params=_tc_params(("arbitrary",)),
    )(x2, dh1, du_perm, dz_ssm, dq, dk, dv, dz_attn, g_pre, w_in)


def _block_diag(t):
    a, b = t.shape[1], t.shape[2]
    eye = jnp.eye(G_TILE, dtype=t.dtype)
    t = t.reshape(N_GT, G_TILE, a, 1, b) * eye[None, :, None, :, None]
    return t.reshape(N_GT, G_TILE * a, G_TILE * b)


def _diag_blocks(m, a, b):
    m = m.reshape(N_GT, G_TILE, a, G_TILE, b)
    return jnp.einsum("tgagb->tgab", m).reshape(SSM_G, a, b)


def _local_step(x, p, target, pre_norm_g, w_in, ssm_lam_re, ssm_lam_im, ssm_log_step, ssm_b_re, ssm_b_im, ssm_c_re,
                ssm_c_im, ssm_d, w_glu, ssm_b_glu, attn_sinks, w_out, post_norm_g, w_proj, w_gate, pl_b_gate):
    bl, seq, _ = x.shape
    seg = seq // N_SEG
    nb = seq // ATT_BLOCK
    t = bl * seq
    x2 = x.reshape(t, D_MODEL)
    p2 = p.reshape(t, D_PLE)
    tg2 = target.reshape(t, D_MODEL)

    lam_re = ssm_lam_re.reshape(N_STATE, 1)
    lam_im = ssm_lam_im.reshape(N_STATE, 1)
    log_step = jnp.broadcast_to(ssm_log_step.reshape(SSM_G, 1), (SSM_G, SSM_N)).reshape(N_STATE, 1)
    b_re = ssm_b_re.reshape(N_STATE, SSM_P)
    b_im = ssm_b_im.reshape(N_STATE, SSM_P)
    a_re, a_im, bb_re, bb_im, pw_re, pw_im = _ssm_prep(lam_re, lam_im, log_step, b_re, b_im, seg)
    bb_re_t = jnp.swapaxes(bb_re.reshape(SSM_G, SSM_N, SSM_P), 1, 2)
    bb_im_t = jnp.swapaxes(bb_im.reshape(SSM_G, SSM_N, SSM_P), 1, 2)
    bcat = jnp.concatenate([_block_diag(bb_re_t), _block_diag(bb_im_t)], axis=-1).astype(BF16)
    ccat_t = jnp.concatenate([_block_diag(ssm_c_re.reshape(SSM_G, SSM_P, SSM_N)),
                              -_block_diag(ssm_c_im.reshape(SSM_G, SSM_P, SSM_N))], axis=-1).astype(BF16)
    bcat_t = jnp.swapaxes(bcat, 1, 2)
    ccat = jnp.swapaxes(ccat_t, 1, 2)
    a_re_row, a_im_row = a_re.reshape(1, N_STATE), a_im.reshape(1, N_STATE)
    d_row = ssm_d.reshape(1, D_SSM)

    u_perm, z_ssm, q, k, v, z_attn = _in_proj(x2, pre_norm_g.reshape(1, D_MODEL), w_in, bl, seg)
    u_perm = u_perm.reshape(bl, seq, D_SSM)
    y_perm = _ssm_forward(u_perm, bcat, ccat, a_re_row, a_im_row, pw_re, pw_im, d_row, seg)
    sinks = attn_sinks.reshape(N_HEADS)
    attn, lse = _attn_forward(q, k, v, sinks, bl, nb)
    (loss, dh1, dy_perm, dz_ssm, dattn, dz_attn, d_w_glu, d_b_glu, d_w_out, d_g_post, d_w_gate, d_b_gate,
     d_w_proj) = _mix_forward_backward(
        x2, y_perm.reshape(bl, seg, N_SEG * D_SSM), z_ssm, attn, z_attn, p2, tg2, w_glu,
        ssm_b_glu.reshape(1, D_SSM), w_out, post_norm_g.reshape(1, D_MODEL), w_gate, pl_b_gate.reshape(1, D_MODEL),
        w_proj, bl, seg)
    dq, dk, dv, d_sinks = _attn_backward(q, k, v, attn, dattn, lse, sinks, bl, nb)
    du_perm, d_bcat, d_ccat_t, da_re, da_im, d_d = _ssm_backward(
        u_perm, dy_perm.reshape(bl, seq, D_SSM), bcat, bcat_t, ccat_t, a_re_row, a_im_row, pw_re, pw_im, d_row, seg)
    grad_x, d_w_in, d_g_pre = _in_backward(
        x2, dh1, du_perm.reshape(bl, seg, N_SEG * D_SSM), dz_ssm, dq, dk, dv, dz_attn,
        pre_norm_g.reshape(1, D_MODEL), w_in, bl, seg)
    dbb_re = jnp.swapaxes(_diag_blocks(d_bcat[:, :, 0:ST_T], SSM_P, SSM_N), 1, 2).reshape(N_STATE, SSM_P)
    dbb_im = jnp.swapaxes(_diag_blocks(d_bcat[:, :, ST_T:], SSM_P, SSM_N), 1, 2).reshape(N_STATE, SSM_P)
    d_lam_re, d_lam_im, d_ls, d_b_re, d_b_im = _ssm_param_grads(
        lam_re, lam_im, log_step, b_re, b_im, a_re, a_im, da_re.reshape(N_STATE, 1), da_im.reshape(N_STATE, 1),
        dbb_re, dbb_im)
    grads = {
        "pre_norm_g": d_g_pre, "w_in": d_w_in, "ssm_lam_re": d_lam_re, "ssm_lam_im": d_lam_im,
        "ssm_log_step": jnp.sum(d_ls.reshape(SSM_G, SSM_N), axis=-1), "ssm_b_re": d_b_re, "ssm_b_im": d_b_im,
        "ssm_c_re": _diag_blocks(d_ccat_t[:, :, 0:ST_T], SSM_P, SSM_N),
        "ssm_c_im": -_diag_blocks(d_ccat_t[:, :, ST_T:], SSM_P, SSM_N),
        "ssm_d": d_d, "ssm_w_glu": d_w_glu, "ssm_b_glu": d_b_glu, "attn_sinks": d_sinks, "w_out": d_w_out,
        "post_norm_g": d_g_post, "pl_w_proj": d_w_proj, "pl_w_gate": d_w_gate, "pl_b_gate": d_b_gate,
    }
    return loss, grad_x.reshape(bl, seq, D_MODEL), grads


BIG_NAMES = ("w_in", "w_out", "pl_w_gate", "pl_w_proj", "ssm_w_glu")
COL_SHARDED = {"w_in": D_IN // N_DEV, "pl_w_proj": D_MODEL // N_DEV}
WEIGHT_NAMES = ("pre_norm_g", "w_in", "ssm_lam_re", "ssm_lam_im", "ssm_log_step", "ssm_b_re", "ssm_b_im", "ssm_c_re",
                "ssm_c_im", "ssm_d", "ssm_w_glu", "ssm_b_glu", "attn_sinks", "w_out", "post_norm_g", "pl_w_proj",
                "pl_w_gate", "pl_b_gate")


def _gathered_to_full(name, g):
    _, rows, cols = g.shape
    if name in COL_SHARDED:
        return jnp.swapaxes(g, 0, 1).reshape(rows, N_DEV * cols)
    return g.reshape(N_DEV * rows, cols)


def _full_to_owned(name, full):
    if name in COL_SHARDED:
        return jnp.swapaxes(full.reshape(full.shape[0], N_DEV, COL_SHARDED[name]), 0, 1)
    return full.reshape(N_DEV, full.shape[0] // N_DEV, full.shape[1])


def kernel(x, p, pre_norm_g, w_in, ssm_lam_re, ssm_lam_im, ssm_log_step, ssm_b_re, ssm_b_im, ssm_c_re, ssm_c_im, ssm_d, ssm_w_glu, ssm_b_glu, attn_sinks, w_out, post_norm_g, pl_w_proj, pl_w_gate, pl_b_gate, loss_target, m_pre_norm_g, m_w_in, m_ssm_lam_re, m_ssm_lam_im, m_ssm_log_step, m_ssm_b_re, m_ssm_b_im, m_ssm_c_re, m_ssm_c_im, m_ssm_d, m_ssm_w_glu, m_ssm_b_glu, m_attn_sinks, m_w_out, m_post_norm_g, m_pl_w_proj, m_pl_w_gate, m_pl_b_gate, v_pre_norm_g, v_w_in, v_ssm_lam_re, v_ssm_lam_im, v_ssm_log_step, v_ssm_b_re, v_ssm_b_im, v_ssm_c_re, v_ssm_c_im, v_ssm_d, v_ssm_w_glu, v_ssm_b_glu, v_attn_sinks, v_w_out, v_post_norm_g, v_pl_w_proj, v_pl_w_gate, v_pl_b_gate):
    w = dict(pre_norm_g=pre_norm_g, w_in=w_in, ssm_lam_re=ssm_lam_re, ssm_lam_im=ssm_lam_im, ssm_log_step=ssm_log_step,
             ssm_b_re=ssm_b_re, ssm_b_im=ssm_b_im, ssm_c_re=ssm_c_re, ssm_c_im=ssm_c_im, ssm_d=ssm_d, ssm_w_glu=ssm_w_glu,
             ssm_b_glu=ssm_b_glu, attn_sinks=attn_sinks, w_out=w_out, post_norm_g=post_norm_g, pl_w_proj=pl_w_proj,
             pl_w_gate=pl_w_gate, pl_b_gate=pl_b_gate)
    m = dict(pre_norm_g=m_pre_norm_g, w_in=m_w_in, ssm_lam_re=m_ssm_lam_re, ssm_lam_im=m_ssm_lam_im,
             ssm_log_step=m_ssm_log_step, ssm_b_re=m_ssm_b_re, ssm_b_im=m_ssm_b_im, ssm_c_re=m_ssm_c_re,
             ssm_c_im=m_ssm_c_im, ssm_d=m_ssm_d, ssm_w_glu=m_ssm_w_glu, ssm_b_glu=m_ssm_b_glu, attn_sinks=m_attn_sinks,
             w_out=m_w_out, post_norm_g=m_post_norm_g, pl_w_proj=m_pl_w_proj, pl_w_gate=m_pl_w_gate,
             pl_b_gate=m_pl_b_gate)
    v = dict(pre_norm_g=v_pre_norm_g, w_in=v_w_in, ssm_lam_re=v_ssm_lam_re, ssm_lam_im=v_ssm_lam_im,
             ssm_log_step=v_ssm_log_step, ssm_b_re=v_ssm_b_re, ssm_b_im=v_ssm_b_im, ssm_c_re=v_ssm_c_re,
             ssm_c_im=v_ssm_c_im, ssm_d=v_ssm_d, ssm_w_glu=v_ssm_w_glu, ssm_b_glu=v_ssm_b_glu, attn_sinks=v_attn_sinks,
             w_out=v_w_out, post_norm_g=v_post_norm_g, pl_w_proj=v_pl_w_proj, pl_w_gate=v_pl_w_gate,
             pl_b_gate=v_pl_b_gate)
    me = _slot(lax.axis_index("x"), lax.axis_index("y"), lax.axis_index("c"))

    gathered = _allgather_weights([w[n][0] for n in BIG_NAMES])
    full = {n: _gathered_to_full(n, g) for n, g in zip(BIG_NAMES, gathered)}

    loss, grad_x, grads = _local_step(
        x, p[0], loss_target, pre_norm_g[0], full["w_in"], ssm_lam_re[0], ssm_lam_im[0], ssm_log_step[0], ssm_b_re[0],
        ssm_b_im[0], ssm_c_re[0], ssm_c_im[0], ssm_d[0], full["ssm_w_glu"], ssm_b_glu[0], attn_sinks[0],
        full["w_out"], post_norm_g[0], full["pl_w_proj"], full["pl_w_gate"], pl_b_gate[0])

    owned = [_full_to_owned(n, grads[n]) for n in BIG_NAMES]
    g_big = _reduce_big([o.astype(BF16) for o in owned],
                        [lax.dynamic_index_in_dim(o, me, axis=0, keepdims=False) for o in owned])
    tiny_form = lambda d: [d[n].reshape(rows, cols) for n, rows, cols in TINY]
    med_form = lambda d: [d[n].reshape(N_DEV, rows // N_DEV, cols) for n, rows, cols in MEDIUM]
    loss, g_tiny, g_med = _reduce_small(loss, tiny_form(grads), med_form(grads))
    names = BIG_NAMES + tuple(n for n, _, _ in TINY + MEDIUM)
    form = lambda d: [d[n][0] for n in BIG_NAMES] + tiny_form(d) + med_form(d)
    updated = _adamw_update(list(g_big) + g_tiny + g_med, form(w), form(m), form(v))
    vals = dict(zip(names, updated))
    results = [[vals[n][kind].reshape(w[n].shape) for n in WEIGHT_NAMES] for kind in range(4)]
    return (loss.reshape(()), grad_x, *results[0], *results[1], *results[2], *results[3])
```

```python
import functools
import math

import jax
import jax.numpy as jnp
from jax import lax
from jax.experimental import pallas as pl
from jax.experimental.pallas import tpu as pltpu

F32 = jnp.float32
BF16 = jnp.bfloat16

D_MODEL = 1024
D_SSM = 512
D_ATTN = 512
SSM_P = 16
SSM_G = 32
SSM_N = 64
N_HEADS = 8
KV_HEADS = 2
Q_PER_KV = 4
HEAD_DIM = 64
ATT_BLOCK = 128
D_PLE = 256
D_IN = 2304
EPS = 1e-6
N_DEV = 8
N_SEG = 8
G_TILE = 8
N_GT = SSM_G // G_TILE
CH_T = G_TILE * SSM_P
ST_T = G_TILE * SSM_N
N_STATE = SSM_G * SSM_N
SCAN_UNROLL = 4
LANES = 128
VMEM_LIMIT = 60 * 1024 * 1024

ADAM_LR = 0.001
ADAM_B1 = 0.9
ADAM_B2 = 0.999
ADAM_EPS = 1e-08
ADAM_WD = 0.01
ADAM_STEP = 10

GELU_C = math.sqrt(2.0 / math.pi)
GELU_K = 0.044715
ATT_SCALE = 1.0 / math.sqrt(HEAD_DIM)
NEG_BIG = -1e30


def _mm(a, b):
    return jnp.dot(a.astype(BF16), b.astype(BF16), preferred_element_type=F32)


def _mm_nt(a, b):
    return lax.dot_general(a.astype(BF16), b.astype(BF16), (((1,), (1,)), ((), ())), preferred_element_type=F32)


def _mm_tn(a, b):
    return lax.dot_general(a.astype(BF16), b.astype(BF16), (((0,), (0,)), ((), ())), preferred_element_type=F32)


def _sigmoid(x):
    return 1.0 / (1.0 + jnp.exp(-x))


def _tc_params(sem):
    return pltpu.CompilerParams(dimension_semantics=sem, vmem_limit_bytes=VMEM_LIMIT)


def _const_spec(shape):
    nd = len(shape)
    return pl.BlockSpec(shape, lambda *_: (0,) * nd)


def _mesh_pos():
    return lax.axis_index("x"), lax.axis_index("y"), lax.axis_index("c")


ROW_CHUNK = 64


def _row_chunks(nrows, fn):
    def step(i, carry):
        fn(pl.ds(pl.multiple_of(i * ROW_CHUNK, ROW_CHUNK), ROW_CHUNK))
        return carry

    lax.fori_loop(0, nrows // ROW_CHUNK, step, 0)


def _slot(px, py, pc):
    return 4 * px + 2 * py + pc


def _allgather_weights(shards):
    n = len(shards)

    def body(*refs):
        srcs, outs, (send_sems, recv_sems) = refs[:n], refs[n:2 * n], refs[2 * n:]
        x, y, c = _mesh_pos()
        me, sibling = (x, y, c), (x, y, 1 - c)
        chips = [(1 - x, y), (x, 1 - y), (1 - x, 1 - y)]

        def copy(a, k, block, to):
            blk = outs[a].at[_slot(*block)]
            return pltpu.make_async_remote_copy(
                src_ref=blk, dst_ref=blk, send_sem=send_sems.at[7 * a + k], recv_sem=recv_sems.at[7 * a + k],
                device_id=to, device_id_type=pl.DeviceIdType.MESH)

        sends = []
        for a in range(n):
            mine = outs[a].at[_slot(*me)]

            def cast(r, mine=mine, src=srcs[a]):
                mine[r, :] = src[r, :].astype(BF16)

            _row_chunks(srcs[a].shape[0], cast)
            first = [copy(a, 0, me, sibling)] + [copy(a, 1 + j, me, (*chip, c)) for j, chip in enumerate(chips)]
            for cp in first:
                cp.start()
            sends += first
        for a in range(n):
            for j, chip in enumerate(chips):
                copy(a, 1 + j, (*chip, c), me).wait_recv()
                fwd = copy(a, 4 + j, (*chip, c), sibling)
                fwd.start()
                sends.append(fwd)
        for a in range(n):
            copy(a, 0, sibling, me).wait_recv()
            for j, chip in enumerate(chips):
                copy(a, 4 + j, (*chip, 1 - c), me).wait_recv()
        for cp in sends:
            cp.wait_send()

    vm = pl.BlockSpec(memory_space=pltpu.VMEM)
    return pl.pallas_call(
        body, name="allgather_weights",
        out_shape=tuple(jax.ShapeDtypeStruct((N_DEV,) + s.shape, BF16) for s in shards),
        in_specs=[vm] * n, out_specs=(vm,) * n,
        scratch_shapes=[pltpu.SemaphoreType.DMA((7 * n,)), pltpu.SemaphoreType.DMA((7 * n,))],
        compiler_params=pltpu.CompilerParams(vmem_limit_bytes=VMEM_LIMIT),
    )(*shards)


def _adamw(w, g, m, v):
    m = ADAM_B1 * m + (1.0 - ADAM_B1) * g
    v = ADAM_B2 * v + (1.0 - ADAM_B2) * (g * g)
    m_hat = m / (1.0 - ADAM_B1 ** ADAM_STEP)
    v_hat = v / (1.0 - ADAM_B2 ** ADAM_STEP)
    delta = -ADAM_LR * (m_hat / (jnp.sqrt(v_hat) + ADAM_EPS) + ADAM_WD * w)
    return delta, m, v


def _remote(src, dst, send_sems, recv_sems, k, to):
    return pltpu.make_async_remote_copy(src_ref=src, dst_ref=dst, send_sem=send_sems.at[k], recv_sem=recv_sems.at[k],
                                        device_id=to, device_id_type=pl.DeviceIdType.MESH)


def _big_reduce_phases(g16_r, go_r, outs, send2, recv1, recv2, s_send, s_recv):
    n = len(g16_r)
    x, y, c = _mesh_pos()
    sibling = (x, y, 1 - c)
    chips = [(1 - x, y), (x, 1 - y), (1 - x, 1 - y)]
    all_chips = [(x, y)] + chips
    lvl1 = []
    for a in range(n):
        cps = [_remote(g16_r[a].at[_slot(*chip, 1 - c)], recv1[a].at[j], s_send, s_recv, 7 * a + j, sibling)
               for j, chip in enumerate(all_chips)]
        for cp in cps:
            cp.start()
        lvl1.append(cps)
    yield
    lvl2 = []
    for a in range(n):
        for cp in lvl1[a]:
            cp.wait_recv()
        og = outs[a]

        def partials(r, a=a, og=og):
            og[r, :] = go_r[a][r, :] + recv1[a][0, r, :].astype(F32)
            for j, chip in enumerate(chips):
                mine16 = g16_r[a][_slot(*chip, c), r, :].astype(F32)
                send2[a][j, r, :] = (mine16 + recv1[a][1 + j, r, :].astype(F32)).astype(BF16)

        _row_chunks(go_r[a].shape[0], partials)
        cps = [_remote(send2[a].at[j], recv2[a].at[j], s_send, s_recv, 7 * a + 4 + j, (*chip, c))
               for j, chip in enumerate(chips)]
        for cp in cps:
            cp.start()
        lvl2.append(cps)
    yield
    for a in range(n):
        for cp in lvl2[a]:
            cp.wait_recv()
        og = outs[a]

        def total(r, a=a, og=og):
            g = og[r, :]
            for j in range(3):
                g = g + recv2[a][j, r, :].astype(F32)
            og[r, :] = g

        _row_chunks(go_r[a].shape[0], total)
    yield
    for cps in lvl1 + lvl2:
        for cp in cps:
            cp.wait_send()


def _interleave(*phases):
    active = list(phases)
    while active:
        for g in list(active):
            try:
                next(g)
            except StopIteration:
                active.remove(g)


def _adamw_update(g, w, m, v):
    n = len(g)

    def body(*refs):
        g_r, w_r, m_r, v_r = (refs[i * n:(i + 1) * n] for i in range(4))
        outs = refs[4 * n:]
        for a in range(n):
            og, od, om, ov = outs[4 * a:4 * a + 4]

            def update(idx, a=a, og=og, od=od, om=om, ov=ov):
                gv = g_r[a][idx]
                d, nm, nv = _adamw(w_r[a][idx], gv, m_r[a][idx], v_r[a][idx])
                og[idx] = gv
                od[idx] = d
                om[idx] = nm
                ov[idx] = nv

            shape = g_r[a].shape
            if len(shape) == 3:
                for b in range(shape[0]):
                    update(b)
            elif shape[0] % ROW_CHUNK == 0:
                _row_chunks(shape[0], update)
            else:
                update(Ellipsis)

    vm = pl.BlockSpec(memory_space=pltpu.VMEM)
    res = pl.pallas_call(
        body, name="adamw_update",
        out_shape=tuple(jax.ShapeDtypeStruct(t.shape, F32) for t in g for _ in range(4)),
        in_specs=[vm] * (4 * n), out_specs=(vm,) * (4 * n),
        compiler_params=pltpu.CompilerParams(vmem_limit_bytes=VMEM_LIMIT),
    )(*g, *w, *m, *v)
    return [res[4 * a:4 * a + 4] for a in range(n)]


TINY = (("pre_norm_g", 1, 1024), ("post_norm_g", 1, 1024), ("pl_b_gate", 1, 1024), ("ssm_d", 1, 512),
        ("ssm_b_glu", 1, 512), ("ssm_log_step", 1, 32), ("attn_sinks", 1, 8), ("ssm_lam_re", 32, 64),
        ("ssm_lam_im", 32, 64))
MEDIUM = (("ssm_b_re", N_STATE, SSM_P), ("ssm_b_im", N_STATE, SSM_P), ("ssm_c_re", SSM_G * SSM_P, SSM_N),
          ("ssm_c_im", SSM_G * SSM_P, SSM_N))


def _stage_rows():
    offs, r = {}, 0
    for name, rows, cols in TINY + (("loss", 1, 1),):
        if rows > 1:
            r = -(-r // 8) * 8
        offs[name] = r
        r += rows if rows > 1 else max(cols // LANES, 1)
    return offs, -(-r // 8) * 8


def _reduce_final(g16, gown, loss, g_tiny, g_med):
    nb_, nt, nm_ = len(g16), len(TINY), len(MEDIUM)
    offs, stage_rows = _stage_rows()

    def body(*refs):
        g16_r, go_r = refs[:nb_], refs[nb_:2 * nb_]
        base = 2 * nb_
        loss_r, gt, gm = refs[base], refs[base + 1:base + 1 + nt], refs[base + 1 + nt:base + 1 + nt + nm_]
        base += 1 + nt + nm_
        out_b = refs[base:base + nb_]
        base += nb_
        loss_o, out_t, out_m = refs[base], refs[base + 1:base + 1 + nt], refs[base + 1 + nt:base + 1 + nt + nm_]
        base += 1 + nt + nm_
        send2_b, recv1_b, recv2_b = (refs[base + i * nb_:base + (i + 1) * nb_] for i in range(3))
        base += 3 * nb_
        stage = refs[base]
        recv1, part, recv2 = (refs[base + 1 + i * nm_:base + 1 + (i + 1) * nm_] for i in range(3))
        bs_send, bs_recv, s_send, s_recv = refs[base + 1 + 3 * nm_:]
        _interleave(_big_reduce_phases(g16_r, go_r, out_b, send2_b, recv1_b, recv2_b, bs_send, bs_recv),
                    small_phases(loss_r, gt, gm, loss_o, out_t, out_m, stage, recv1, part, recv2, s_send, s_recv))

    def small_phases(loss_r, gt, gm, loss_o, out_t, out_m, stage, recv1, part, recv2, s_send, s_recv):
        x, y, c = _mesh_pos()
        me = _slot(x, y, c)
        sibling = (x, y, 1 - c)
        chips = [(1 - x, y), (x, 1 - y), (1 - x, 1 - y)]
        all_chips = [(x, y)] + chips
        peers = [sibling] + [(*chip, c) for chip in chips] + [(*chip, 1 - c) for chip in chips]
        sem = iter(range(7 + 14 * nm_))
        lvl1 = []
        for a in range(nm_):
            cps = [_remote(gm[a].at[_slot(*chip, 1 - c)], recv1[a].at[j], s_send, s_recv, next(sem), sibling)
                   for j, chip in enumerate(all_chips)]
            for cp in cps:
                cp.start()
            lvl1.append(cps)
        mine = stage.at[me]
        mine[...] = jnp.zeros((stage_rows, LANES), F32)
        for (name, rows, cols), ref in zip(TINY + (("loss", 1, 1),), gt + (loss_r,)):
            r0 = offs[name]
            if rows > 1:
                mine[r0:r0 + rows, 0:cols] = ref[...]
            elif cols >= LANES:
                for i in range(cols // LANES):
                    mine[r0 + i:r0 + i + 1, :] = ref[:, i * LANES:(i + 1) * LANES]
            else:
                mine[r0:r0 + 1, 0:cols] = ref[...]
        tiny_cps = [_remote(mine, mine, s_send, s_recv, next(sem), peer) for peer in peers]
        for cp in tiny_cps:
            cp.start()
        yield
        lvl2 = []
        for a in range(nm_):
            for cp in lvl1[a]:
                cp.wait_recv()
            for j, chip in enumerate(all_chips):
                part[a][j] = gm[a][_slot(*chip, c)] + recv1[a][j]
            cps = [_remote(part[a].at[1 + j], recv2[a].at[j], s_send, s_recv, next(sem), (*chip, c))
                   for j, chip in enumerate(chips)]
            for cp in cps:
                cp.start()
            lvl2.append(cps)
        yield
        lvl3 = []
        for a in range(nm_):
            for cp in lvl2[a]:
                cp.wait_recv()
            blk = out_m[a].at[me]
            blk[...] = ((part[a][0] + recv2[a][0]) + recv2[a][1]) + recv2[a][2]
            cps = [_remote(blk, blk, s_send, s_recv, next(sem), peer) for peer in peers]
            for cp in cps:
                cp.start()
            lvl3.append(cps)
        yield
        for cp in tiny_cps:
            cp.wait_recv()
        tot = stage[0]
        for d in range(1, N_DEV):
            tot = tot + stage[d]
        loss_o[...] = tot[offs["loss"]:offs["loss"] + 1, 0:1]
        for k, (name, rows, cols) in enumerate(TINY):
            r0 = offs[name]
            if rows > 1:
                out_t[k][...] = tot[r0:r0 + rows, 0:cols]
            elif cols >= LANES:
                for i in range(cols // LANES):
                    out_t[k][:, i * LANES:(i + 1) * LANES] = tot[r0 + i:r0 + i + 1, :]
            else:
                out_t[k][...] = tot[r0:r0 + 1, 0:cols]
        for cps in lvl3:
            for cp in cps:
                cp.wait_recv()
        for cps in lvl1 + lvl2 + lvl3 + [tiny_cps]:
            for cp in cps:
                cp.wait_send()

    vmem = pl.BlockSpec(memory_space=pltpu.VMEM)
    t_shapes = [jax.ShapeDtypeStruct((rows, cols), F32) for _, rows, cols in TINY]
    m_shapes = [jax.ShapeDtypeStruct((N_DEV, rows // N_DEV, cols), F32) for _, rows, cols in MEDIUM]
    blk = [(rows // N_DEV, cols) for _, rows, cols in MEDIUM]
    scratch = ([pltpu.VMEM((3,) + t.shape, BF16) for t in gown] + [pltpu.VMEM((4,) + t.shape, BF16) for t in gown]
               + [pltpu.VMEM((3,) + t.shape, BF16) for t in gown]
               + [pltpu.VMEM((N_DEV, stage_rows, LANES), F32)]
               + [pltpu.VMEM((4,) + b, F32) for b in blk] + [pltpu.VMEM((4,) + b, F32) for b in blk]
               + [pltpu.VMEM((3,) + b, F32) for b in blk]
               + [pltpu.SemaphoreType.DMA((7 * nb_,)), pltpu.SemaphoreType.DMA((7 * nb_,)),
                  pltpu.SemaphoreType.DMA((7 + 14 * nm_,)), pltpu.SemaphoreType.DMA((7 + 14 * nm_,))])
    n_in, n_out = 2 * nb_ + 1 + nt + nm_, nb_ + 1 + nt + nm_
    res = pl.pallas_call(
        body, name="reduce_final",
        out_shape=tuple(jax.ShapeDtypeStruct(t.shape, F32) for t in gown) + (jax.ShapeDtypeStruct((1, 1), F32),)
        + tuple(t_shapes) + tuple(m_shapes),
        in_specs=[vmem] * n_in, out_specs=(vmem,) * n_out, scratch_shapes=scratch,
        compiler_params=pltpu.CompilerParams(vmem_limit_bytes=VMEM_LIMIT),
    )(*g16, *gown, loss, *g_tiny, *g_med)
    return list(res[:nb_]), res[nb_], list(res[nb_ + 1:nb_ + 1 + nt]), list(res[nb_ + 1 + nt:])


def _in_proj(x2, g_pre, w_in, late, bl, seg):
    t = x2.shape[0]
    tm = seg
    steps = t // tm
    n = len(late)
    forward_step, last_step = (steps * 5) // 8, steps - 1

    def body(*refs):
        x_ref, g_ref, w_ref = refs[:3]
        late_r = refs[3:3 + n]
        u_ref, zs_ref, q_ref, k_ref, v_ref, za_ref = refs[3 + n:9 + n]
        gath = refs[9 + n:9 + 2 * n]
        cast = refs[9 + 2 * n:9 + 3 * n]
        send_sems, recv_sems, local_sems = refs[9 + 3 * n:]
        i = pl.program_id(0)
        x, y, c = _mesh_pos()
        me, sibling = (x, y, c), (x, y, 1 - c)
        chips = [(1 - x, y), (x, 1 - y), (1 - x, 1 - y)]

        def own(a, k, to):
            return _remote(cast[a], gath[a].at[_slot(*me)], send_sems, recv_sems, 7 * a + k, to)

        def relay(a, k, block, to):
            blk = gath[a].at[_slot(*block)]
            return _remote(blk, blk, send_sems, recv_sems, 7 * a + k, to)

        def keep(a):
            return pltpu.make_async_copy(cast[a], gath[a].at[_slot(*me)], local_sems.at[a])

        @pl.when(i == 0)
        def _():
            for a in range(n):
                def to16(r, a=a):
                    cast[a][r, :] = late_r[a][r, :].astype(BF16)

                _row_chunks(late_r[a].shape[0], to16)
                keep(a).start()
                own(a, 0, sibling).start()
                for j, chip in enumerate(chips):
                    own(a, 1 + j, (*chip, c)).start()

        xv = x_ref[...]
        r = lax.rsqrt(jnp.mean(xv * xv, axis=-1, keepdims=True) + EPS)
        hn = xv * r * g_ref[...]
        proj = _mm(hn, w_ref[...])
        u_ref[0] = proj[:, 0:512]
        zs_ref[...] = proj[:, 512:1024]
        q_ref[...] = proj[:, 1024:1536].astype(BF16)
        k_ref[...] = proj[:, 1536:1664].astype(BF16)
        v_ref[...] = proj[:, 1664:1792].astype(BF16)
        za_ref[...] = proj[:, 1792:2304]

        @pl.when(i == forward_step)
        def _():
            for a in range(n):
                for j, chip in enumerate(chips):
                    relay(a, 1 + j, (*chip, c), me).wait_recv()
                    relay(a, 4 + j, (*chip, c), sibling).start()

        @pl.when(i == last_step)
        def _():
            for a in range(n):
                relay(a, 0, sibling, me).wait_recv()
                for j, chip in enumerate(chips):
                    relay(a, 4 + j, (*chip, 1 - c), me).wait_recv()
                own(a, 0, sibling).wait_send()
                for j, chip in enumerate(chips):
                    own(a, 1 + j, (*chip, c)).wait_send()
                    relay(a, 4 + j, (*chip, c), sibling).wait_send()
                keep(a).wait()

    row = lambda w: pl.BlockSpec((tm, w), lambda i: (i, 0))
    hbm = pl.BlockSpec(memory_space=pl.ANY)
    res = pl.pallas_call(
        body, name="in_proj", grid=(steps,),
        in_specs=[row(D_MODEL), _const_spec((1, D_MODEL)), _const_spec((D_MODEL, D_IN))]
        + [_const_spec(s.shape) for s in late],
        out_specs=(pl.BlockSpec((1, tm, D_SSM), lambda i: (i // N_SEG, 0, i % N_SEG)),
                   row(512), row(512), row(128), row(128), row(512)) + (hbm,) * n,
        out_shape=(jax.ShapeDtypeStruct((bl, seg, N_SEG * D_SSM), F32),
                   jax.ShapeDtypeStruct((t, 512), F32), jax.ShapeDtypeStruct((t, 512), BF16),
                   jax.ShapeDtypeStruct((t, 128), BF16), jax.ShapeDtypeStruct((t, 128), BF16),
                   jax.ShapeDtypeStruct((t, 512), F32))
        + tuple(jax.ShapeDtypeStruct((N_DEV,) + s.shape, BF16) for s in late),
        scratch_shapes=[pltpu.VMEM(s.shape, BF16) for s in late]
        + [pltpu.SemaphoreType.DMA((7 * n,)), pltpu.SemaphoreType.DMA((7 * n,)), pltpu.SemaphoreType.DMA((n,))],
        compiler_params=_tc_params(("arbitrary",)),
    )(x2, g_pre, w_in, *late)
    return res[:6], list(res[6:])


def _ssm_prep(lam_re, lam_im, log_step, b_re, b_im, seg):
    def body(lr_ref, li_ref, ls_ref, br_ref, bi_ref, lrr_ref, lir_ref, lsr_ref,
             ar_ref, ai_ref, bbr_ref, bbi_ref, pr_ref, pi_ref):
        lr, li = lr_ref[...], li_ref[...]
        step = jnp.exp(ls_ref[...])
        mag = jnp.exp(lr * step)
        ar = mag * jnp.cos(li * step)
        ai = mag * jnp.sin(li * step)
        ar_ref[...] = ar
        ai_ref[...] = ai
        den = lr * lr + li * li
        cr = ((ar - 1.0) * lr + ai * li) / den
        ci = (ai * lr - (ar - 1.0) * li) / den
        br, bi = br_ref[...], bi_ref[...]
        bbr_ref[...] = cr * br - ci * bi
        bbi_ref[...] = cr * bi + ci * br
        stepr = jnp.exp(lsr_ref[...])
        k = (lax.broadcasted_iota(jnp.int32, (8, N_STATE), 0) + 1).astype(F32)
        magk = jnp.exp(k * (lrr_ref[...] * stepr))
        ang = k * (lir_ref[...] * stepr)
        pr_ref[0:8, :] = magk * jnp.cos(ang)
        pi_ref[0:8, :] = magk * jnp.sin(ang)
        n = 8
        while n < seg:
            tr, ti = pr_ref[n - 1:n, :], pi_ref[n - 1:n, :]
            xr, xi = pr_ref[0:n, :], pi_ref[0:n, :]
            pr_ref[n:2 * n, :] = xr * tr - xi * ti
            pi_ref[n:2 * n, :] = xr * ti + xi * tr
            n *= 2

    col = jax.ShapeDtypeStruct((N_STATE, 1), F32)
    mat = jax.ShapeDtypeStruct((N_STATE, SSM_P), F32)
    pw = jax.ShapeDtypeStruct((seg, N_STATE), F32)
    vm = pl.BlockSpec(memory_space=pltpu.VMEM)
    return pl.pallas_call(
        body, name="ssm_prep", out_shape=(col, col, mat, mat, pw, pw),
        in_specs=[vm] * 8, out_specs=(vm,) * 6,
    )(lam_re, lam_im, log_step, b_re, b_im, lam_re.reshape(1, N_STATE), lam_im.reshape(1, N_STATE),
      log_step.reshape(1, N_STATE))


def _seg_rows(t):
    if isinstance(t, int):
        return pl.ds(t * N_SEG, N_SEG)
    return pl.ds(pl.multiple_of(t * N_SEG, N_SEG), N_SEG)


def _scan_forward(xs, a_re, a_im, pw_re, pw_im, cs, seg):
    are = jnp.broadcast_to(a_re, (N_SEG, ST_T))
    aim = jnp.broadcast_to(a_im, (N_SEG, ST_T))

    def steps(k, carry):
        xr, xi = carry
        for j in range(SCAN_UNROLL):
            r = pl.multiple_of((k * SCAN_UNROLL + j) * N_SEG, N_SEG)
            nr = are * xr - aim * xi + xs[pl.ds(r, N_SEG), 0:ST_T]
            ni = are * xi + aim * xr + xs[pl.ds(r, N_SEG), ST_T:2 * ST_T]
            xs[pl.ds(r, N_SEG), 0:ST_T] = nr
            xs[pl.ds(r, N_SEG), ST_T:2 * ST_T] = ni
            xr, xi = nr, ni
        return xr, xi

    zero = jnp.zeros((N_SEG, ST_T), F32)
    fr, fi = lax.fori_loop(0, seg // SCAN_UNROLL, steps, (zero, zero))
    sr, si = pw_re[seg - 1:seg, :], pw_im[seg - 1:seg, :]
    cr = jnp.zeros((1, ST_T), F32)
    ci = jnp.zeros((1, ST_T), F32)
    cs[0:1, :] = cr
    cs[8:9, :] = ci
    for s in range(1, N_SEG):
        ncr = sr * cr - si * ci + fr[s - 1:s, :]
        nci = sr * ci + si * cr + fi[s - 1:s, :]
        cr, ci = ncr, nci
        cs[s:s + 1, :] = cr
        cs[8 + s:9 + s, :] = ci
    car, cai = cs[0:8, :], cs[8:16, :]

    def fix(t, _):
        r = pl.multiple_of(t * N_SEG, N_SEG)
        pr, pi = pw_re[pl.ds(t, 1), :], pw_im[pl.ds(t, 1), :]
        xs[pl.ds(r, N_SEG), 0:ST_T] = xs[pl.ds(r, N_SEG), 0:ST_T] + (pr * car - pi * cai)
        xs[pl.ds(r, N_SEG), ST_T:2 * ST_T] = xs[pl.ds(r, N_SEG), ST_T:2 * ST_T] + (pr * cai + pi * car)
        return 0

    lax.fori_loop(0, seg, fix, 0, unroll=SCAN_UNROLL)


def _ssm_forward(u_perm, bcat, ccat, a_re, a_im, pw_re, pw_im, d_row, seg):
    bl, rows, _ = u_perm.shape

    def body(u_ref, b_ref, c_ref, ar_ref, ai_ref, pr_ref, pi_ref, d_ref, y_ref, xs, cs):
        u = u_ref[0]
        xs[...] = _mm(u, b_ref[0])
        _scan_forward(xs, ar_ref[...], ai_ref[...], pr_ref, pi_ref, cs, seg)
        y_ref[0] = _mm(xs[...], c_ref[0]) + d_ref[...] * u

    return pl.pallas_call(
        body, name="ssm_forward", grid=(bl, N_GT),
        in_specs=[pl.BlockSpec((1, rows, CH_T), lambda b, j: (b, 0, j)),
                  pl.BlockSpec((1, CH_T, 2 * ST_T), lambda b, j: (j, 0, 0)),
                  pl.BlockSpec((1, 2 * ST_T, CH_T), lambda b, j: (j, 0, 0)),
                  pl.BlockSpec((1, ST_T), lambda b, j: (0, j)), pl.BlockSpec((1, ST_T), lambda b, j: (0, j)),
                  pl.BlockSpec((seg, ST_T), lambda b, j: (0, j)), pl.BlockSpec((seg, ST_T), lambda b, j: (0, j)),
                  pl.BlockSpec((1, CH_T), lambda b, j: (0, j))],
        out_specs=pl.BlockSpec((1, rows, CH_T), lambda b, j: (b, 0, j)),
        out_shape=jax.ShapeDtypeStruct((bl, rows, D_SSM), F32),
        scratch_shapes=[pltpu.VMEM((rows, 2 * ST_T), F32), pltpu.VMEM((16, ST_T), F32)],
        compiler_params=_tc_params(("arbitrary", "arbitrary")),
    )(u_perm, bcat, ccat, a_re, a_im, pw_re, pw_im, d_row)


def _ssm_backward(u_perm, dy_perm, bcat, bcat_t, ccat_t, a_re, a_im, pw_re, pw_im, d_row, seg):
    bl, rows, _ = u_perm.shape

    def body(u_ref, dy_ref, b_ref, bt_ref, ct_ref, ar_ref, ai_ref, pr_ref, pi_ref, d_ref,
             du_ref, db_ref, dc_ref, dar_ref, dai_ref, dd_ref, xs, ls, cs, cl):
        b = pl.program_id(1)
        u = u_ref[0]
        dy = dy_ref[0]
        xs[...] = _mm(u, b_ref[0])
        _scan_forward(xs, ar_ref[...], ai_ref[...], pr_ref, pi_ref, cs, seg)
        ls[...] = _mm(dy, ct_ref[0])
        are = jnp.broadcast_to(ar_ref[...], (N_SEG, ST_T))
        aim = jnp.broadcast_to(ai_ref[...], (N_SEG, ST_T))

        def steps(k, carry):
            lr, li = carry
            for j in range(SCAN_UNROLL):
                r = pl.multiple_of((seg - 1 - (k * SCAN_UNROLL + j)) * N_SEG, N_SEG)
                nr = are * lr + aim * li + ls[pl.ds(r, N_SEG), 0:ST_T]
                ni = are * li - aim * lr + ls[pl.ds(r, N_SEG), ST_T:2 * ST_T]
                ls[pl.ds(r, N_SEG), 0:ST_T] = nr
                ls[pl.ds(r, N_SEG), ST_T:2 * ST_T] = ni
                lr, li = nr, ni
            return lr, li

        zero = jnp.zeros((N_SEG, ST_T), F32)
        fr, fi = lax.fori_loop(0, seg // SCAN_UNROLL, steps, (zero, zero))
        sr, si = pr_ref[seg - 1:seg, :], pi_ref[seg - 1:seg, :]
        cr = jnp.zeros((1, ST_T), F32)
        ci = jnp.zeros((1, ST_T), F32)
        cl[7:8, :] = cr
        cl[15:16, :] = ci
        for s in range(N_SEG - 2, -1, -1):
            ncr = sr * cr + si * ci + fr[s + 1:s + 2, :]
            nci = sr * ci - si * cr + fi[s + 1:s + 2, :]
            cr, ci = ncr, nci
            cl[s:s + 1, :] = cr
            cl[8 + s:9 + s, :] = ci
        clr, cli = cl[0:8, :], cl[8:16, :]

        def fix_rows(rows, t, xpr, xpi, acc):
            dr, di = acc
            pr, pi = pr_ref[pl.ds(seg - 1 - t, 1), :], pi_ref[pl.ds(seg - 1 - t, 1), :]
            lr = ls[rows, 0:ST_T] + (pr * clr + pi * cli)
            li = ls[rows, ST_T:2 * ST_T] + (pr * cli - pi * clr)
            ls[rows, 0:ST_T] = lr
            ls[rows, ST_T:2 * ST_T] = li
            return dr + (lr * xpr + li * xpi), di + (li * xpr - lr * xpi)

        def fix_at(t, acc):
            prev = _seg_rows(t - 1)
            return fix_rows(_seg_rows(t), t, xs[prev, 0:ST_T], xs[prev, ST_T:2 * ST_T], acc)

        def fix(k, acc):
            for j in range(SCAN_UNROLL):
                acc = fix_at(k * SCAN_UNROLL + j, acc)
            return acc

        acc = fix_rows(pl.ds(0, N_SEG), 0, cs[0:8, :], cs[8:16, :], (zero, zero))
        for t in range(1, SCAN_UNROLL):
            acc = fix_at(t, acc)
        dr, di = lax.fori_loop(1, seg // SCAN_UNROLL, fix, acc)
        dar = jnp.sum(dr, axis=0, keepdims=True)
        dai = jnp.sum(di, axis=0, keepdims=True)
        lall = ls[...]
        du_ref[0] = _mm(lall, bt_ref[0]) + d_ref[...] * dy
        dbp = _mm_tn(u, lall)
        dcp = _mm_tn(dy, xs[...])
        ddp = jnp.sum(dy * u, axis=0, keepdims=True)

        @pl.when(b == 0)
        def _():
            db_ref[0] = dbp
            dc_ref[0] = dcp
            dar_ref[...] = dar
            dai_ref[...] = dai
            dd_ref[...] = ddp

        @pl.when(b != 0)
        def _():
            db_ref[0] += dbp
            dc_ref[0] += dcp
            dar_ref[...] += dar
            dai_ref[...] += dai
            dd_ref[...] += ddp

    tile3 = lambda r, c: pl.BlockSpec((1, r, c), lambda j, b: (j, 0, 0))
    lane = lambda r, c: pl.BlockSpec((r, c), lambda j, b: (0, j))
    act = pl.BlockSpec((1, rows, CH_T), lambda j, b: (b, 0, j))
    return pl.pallas_call(
        body, name="ssm_backward", grid=(N_GT, bl),
        in_specs=[act, act, tile3(CH_T, 2 * ST_T), tile3(2 * ST_T, CH_T), tile3(CH_T, 2 * ST_T),
                  lane(1, ST_T), lane(1, ST_T), lane(seg, ST_T), lane(seg, ST_T), lane(1, CH_T)],
        out_specs=(act, tile3(CH_T, 2 * ST_T), tile3(CH_T, 2 * ST_T), lane(1, ST_T), lane(1, ST_T), lane(1, CH_T)),
        out_shape=(jax.ShapeDtypeStruct((bl, rows, D_SSM), F32),
                   jax.ShapeDtypeStruct((N_GT, CH_T, 2 * ST_T), F32), jax.ShapeDtypeStruct((N_GT, CH_T, 2 * ST_T), F32),
                   jax.ShapeDtypeStruct((1, N_STATE), F32), jax.ShapeDtypeStruct((1, N_STATE), F32),
                   jax.ShapeDtypeStruct((1, D_SSM), F32)),
        scratch_shapes=[pltpu.VMEM((rows, 2 * ST_T), F32), pltpu.VMEM((rows, 2 * ST_T), F32),
                        pltpu.VMEM((16, ST_T), F32), pltpu.VMEM((16, ST_T), F32)],
        compiler_params=_tc_params(("arbitrary", "arbitrary")),
    )(u_perm, dy_perm, bcat, bcat_t, ccat_t, a_re, a_im, pw_re, pw_im, d_row)


def _ssm_param_grads(lam_re, lam_im, log_step, b_re, b_im, a_re, a_im, da_re, da_im, dbb_re, dbb_im):
    def body(lr_ref, li_ref, ls_ref, br_ref, bi_ref, ar_ref, ai_ref, gar_ref, gai_ref, gbr_ref, gbi_ref,
             dlr_ref, dli_ref, dls_ref, dbr_ref, dbi_ref):
        lr, li = lr_ref[...], li_ref[...]
        step = jnp.exp(ls_ref[...])
        ar, ai = ar_ref[...], ai_ref[...]
        den = lr * lr + li * li
        cr = ((ar - 1.0) * lr + ai * li) / den
        ci = (ai * lr - (ar - 1.0) * li) / den
        br, bi = br_ref[...], bi_ref[...]
        gbr, gbi = gbr_ref[...], gbi_ref[...]
        dbr_ref[...] = cr * gbr + ci * gbi
        dbi_ref[...] = cr * gbi - ci * gbr
        gcr = jnp.sum(br * gbr + bi * gbi, axis=-1, keepdims=True)
        gci = jnp.sum(br * gbi - bi * gbr, axis=-1, keepdims=True)
        ilr, ili = lr / den, -li / den
        gar = gar_ref[...] + (ilr * gcr + ili * gci)
        gai = gai_ref[...] + (ilr * gci - ili * gcr)
        qr, qi = cr * ilr - ci * ili, cr * ili + ci * ilr
        glr = -(qr * gcr + qi * gci)
        gli = -(qr * gci - qi * gcr)
        gwr = ar * gar + ai * gai
        gwi = ar * gai - ai * gar
        dlr_ref[...] = glr + step * gwr
        dli_ref[...] = gli + step * gwi
        dls_ref[...] = (lr * gwr + li * gwi) * step

    col = jax.ShapeDtypeStruct((N_STATE, 1), F32)
    mat = jax.ShapeDtypeStruct((N_STATE, SSM_P), F32)
    vm = pl.BlockSpec(memory_space=pltpu.VMEM)
    return pl.pallas_call(
        body, name="ssm_param_grads", out_shape=(col, col, col, mat, mat),
        in_specs=[vm] * 11, out_specs=(vm,) * 5,
    )(lam_re, lam_im, log_step, b_re, b_im, a_re, a_im, da_re, da_im, dbb_re, dbb_im)


ROWS4 = Q_PER_KV * ATT_BLOCK


def _att_dist_mask(first_block):
    qi = lax.broadcasted_iota(jnp.int32, (ROWS4, 2 * ATT_BLOCK), 0) & (ATT_BLOCK - 1)
    si = lax.broadcasted_iota(jnp.int32, (ROWS4, 2 * ATT_BLOCK), 1)
    dist = qi + ATT_BLOCK - si
    valid = (dist >= 0) & (dist < ATT_BLOCK) & ((si >= ATT_BLOCK) | jnp.logical_not(first_block))
    return dist.astype(F32), valid


def _stack_heads(x, kv):
    return jnp.concatenate([x[:, (kv * Q_PER_KV + g) * HEAD_DIM:(kv * Q_PER_KV + g + 1) * HEAD_DIM]
                            for g in range(Q_PER_KV)], axis=0)


def _stack_cols(x, kv):
    return jnp.concatenate([x[:, kv * Q_PER_KV + g:kv * Q_PER_KV + g + 1] for g in range(Q_PER_KV)], axis=0)


def _per_head_col(vals):
    return jnp.concatenate([jnp.full((ATT_BLOCK, 1), v, F32) for v in vals], axis=0)


def _attn_forward(q, k, v, sinks, bl, nb):
    t = q.shape[0]

    def body(sink_ref, q_ref, kp_ref, kc_ref, vp_ref, vc_ref, o_ref, lse_ref):
        i = pl.program_id(1)
        dist4, valid4 = _att_dist_mask(i == 0)
        dist, valid = dist4[0:ATT_BLOCK, :], valid4[0:ATT_BLOCK, :]
        kk = jnp.concatenate([kp_ref[...], kc_ref[...]], axis=0)
        vv = jnp.concatenate([vp_ref[...], vc_ref[...]], axis=0)
        qv = q_ref[...]
        for h in range(N_HEADS):
            kv = h // Q_PER_KV
            slope = 2.0 ** (-(h + 1))
            qh = qv[:, h * HEAD_DIM:(h + 1) * HEAD_DIM]
            kh = kk[:, kv * HEAD_DIM:(kv + 1) * HEAD_DIM]
            vh = vv[:, kv * HEAD_DIM:(kv + 1) * HEAD_DIM]
            s = _mm_nt(qh, kh) * ATT_SCALE - slope * dist
            s = jnp.where(valid, s, NEG_BIG)
            sink = sink_ref[h]
            m = jnp.maximum(jnp.max(s, axis=-1, keepdims=True), sink)
            e = jnp.exp(s - m)
            den = jnp.sum(e, axis=-1, keepdims=True) + jnp.exp(sink - m)
            o_ref[:, h * HEAD_DIM:(h + 1) * HEAD_DIM] = _mm(e, vh) * (1.0 / den)
            lse_ref[:, h:h + 1] = m + jnp.log(den)

    cur = lambda w: pl.BlockSpec((ATT_BLOCK, w), lambda b, i: (b * nb + i, 0))
    prev = lambda w: pl.BlockSpec((ATT_BLOCK, w), lambda b, i: (b * nb + jnp.maximum(i - 1, 0), 0))
    return pl.pallas_call(
        body, name="attn_forward", grid=(bl, nb),
        in_specs=[pl.BlockSpec(memory_space=pltpu.SMEM), cur(512), prev(128), cur(128), prev(128), cur(128)],
        out_specs=(cur(512), cur(N_HEADS)),
        out_shape=(jax.ShapeDtypeStruct((t, D_ATTN), F32), jax.ShapeDtypeStruct((t, N_HEADS), F32)),
        compiler_params=_tc_params(("arbitrary", "arbitrary")),
    )(sinks, q, k, k, v, v)


def _attn_backward(q, k, v, o, do, lse, sinks, late16, late_own, bl, nb):
    t = q.shape[0]
    n = len(late16)
    steps = bl * nb
    mid1, mid2, last = steps // 4, (steps * 3) // 4, steps - 1

    def body(*refs):
        (sink_ref, qc_ref, qn_ref, kp_ref, kc_ref, vp_ref, vc_ref, oc_ref, on_ref, doc_ref, don_ref,
         lc_ref, ln_ref) = refs[:13]
        g16_r, go_r = refs[13:13 + n], refs[13 + n:13 + 2 * n]
        dq_ref, dk_ref, dv_ref, ds_ref = refs[13 + 2 * n:17 + 2 * n]
        red = refs[17 + 2 * n:17 + 3 * n]
        own16, recv1, send2, recv2 = (refs[17 + 3 * n + k * n:17 + 3 * n + (k + 1) * n] for k in range(4))
        s_send, s_recv, s_local = refs[17 + 7 * n:]
        b, i = pl.program_id(0), pl.program_id(1)
        step = b * nb + i
        x, y, c = _mesh_pos()
        sibling = (x, y, 1 - c)
        chips = [(1 - x, y), (x, 1 - y), (1 - x, 1 - y)]
        all_chips = [(x, y)] + chips

        def lvl1(a, j):
            return _remote(g16_r[a].at[_slot(*all_chips[j], 1 - c)], recv1[a].at[j], s_send, s_recv, 7 * a + j, sibling)

        def lvl2(a, j):
            return _remote(send2[a].at[j], recv2[a].at[j], s_send, s_recv, 7 * a + 4 + j, (*chips[j], c))

        def mine(a, j):
            return pltpu.make_async_copy(g16_r[a].at[_slot(*chips[j], c)], own16[a].at[j], s_local.at[3 * a + j])

        @pl.when(step == 0)
        def _():
            for a in range(n):
                for j in range(3):
                    mine(a, j).start()
                for j in range(4):
                    lvl1(a, j).start()

        @pl.when(step == mid1)
        def _():
            for a in range(n):
                for j in range(3):
                    mine(a, j).wait()
                for j in range(4):
                    lvl1(a, j).wait_recv()

                def partials(r, a=a):
                    red[a][r, :] = go_r[a][r, :] + recv1[a][0, r, :].astype(F32)
                    for j in range(3):
                        send2[a][j, r, :] = (own16[a][j, r, :].astype(F32)
                                             + recv1[a][1 + j, r, :].astype(F32)).astype(BF16)

                _row_chunks(go_r[a].shape[0], partials)
                for j in range(3):
                    lvl2(a, j).start()

        @pl.when(step == mid2)
        def _():
            for a in range(n):
                for j in range(3):
                    lvl2(a, j).wait_recv()

                def total(r, a=a):
                    g = red[a][r, :]
                    for j in range(3):
                        g = g + recv2[a][j, r, :].astype(F32)
                    red[a][r, :] = g

                _row_chunks(go_r[a].shape[0], total)

        @pl.when(step == last)
        def _():
            for a in range(n):
                for j in range(4):
                    lvl1(a, j).wait_send()
                for j in range(3):
                    lvl2(a, j).wait_send()

        dist, valid = _att_dist_mask(i == 0)
        has_next = i + 1 < nb
        dist_n = dist[:, 0:ATT_BLOCK]
        valid_n = (dist_n < ATT_BLOCK) & has_next
        kk = jnp.concatenate([kp_ref[...], kc_ref[...]], axis=0)
        vv = jnp.concatenate([vp_ref[...], vc_ref[...]], axis=0)
        qc, qn = qc_ref[...], qn_ref[...]
        oc, on = oc_ref[...], on_ref[...]
        doc, don = doc_ref[...], don_ref[...]
        lc, ln = lc_ref[...], ln_ref[...]
        dsink_cols = []
        for kv in range(KV_HEADS):
            heads = range(kv * Q_PER_KV, (kv + 1) * Q_PER_KV)
            kh = kk[:, kv * HEAD_DIM:(kv + 1) * HEAD_DIM]
            vh = vv[:, kv * HEAD_DIM:(kv + 1) * HEAD_DIM]
            khc, vhc = kh[ATT_BLOCK:, :], vh[ATT_BLOCK:, :]
            slope = _per_head_col([2.0 ** (-(h + 1)) for h in heads])
            sink = _per_head_col([sink_ref[h] for h in heads])
            q4, do4 = _stack_heads(qc, kv), _stack_heads(doc, kv)
            delta = jnp.sum(do4 * _stack_heads(oc, kv), axis=-1, keepdims=True)
            lse4 = _stack_cols(lc, kv)
            s = _mm_nt(q4, kh) * ATT_SCALE - slope * dist
            p = jnp.where(valid, jnp.exp(s - lse4), 0.0)
            dsc = p * (_mm_nt(do4, vh) - delta)
            dq4 = _mm(dsc, kh) * ATT_SCALE
            dk_acc = _mm_tn(dsc[:, ATT_BLOCK:], q4)
            dv_acc = _mm_tn(p[:, ATT_BLOCK:], do4)
            dsink4 = jnp.exp(sink - lse4) * delta
            q4n, do4n = _stack_heads(qn, kv), _stack_heads(don, kv)
            delta_n = jnp.sum(do4n * _stack_heads(on, kv), axis=-1, keepdims=True)
            s2 = _mm_nt(q4n, khc) * ATT_SCALE - slope * dist_n
            p2 = jnp.where(valid_n, jnp.exp(s2 - _stack_cols(ln, kv)), 0.0)
            ds2 = p2 * (_mm_nt(do4n, vhc) - delta_n)
            dk_acc += _mm_tn(ds2, q4n)
            dv_acc += _mm_tn(p2, do4n)
            dk_ref[:, kv * HEAD_DIM:(kv + 1) * HEAD_DIM] = dk_acc * ATT_SCALE
            dv_ref[:, kv * HEAD_DIM:(kv + 1) * HEAD_DIM] = dv_acc
            for g, h in enumerate(heads):
                rows = slice(g * ATT_BLOCK, (g + 1) * ATT_BLOCK)
                dq_ref[:, h * HEAD_DIM:(h + 1) * HEAD_DIM] = dq4[rows, :]
                dsink_cols.append(-jnp.sum(dsink4[rows, :], axis=0, keepdims=True))
        dsink = jnp.concatenate(dsink_cols, axis=1)

        @pl.when((b == 0) & (i == 0))
        def _():
            ds_ref[...] = dsink

        @pl.when((b != 0) | (i != 0))
        def _():
            ds_ref[...] += dsink

    cur = lambda w: pl.BlockSpec((ATT_BLOCK, w), lambda b, i: (b * nb + i, 0))
    prev = lambda w: pl.BlockSpec((ATT_BLOCK, w), lambda b, i: (b * nb + jnp.maximum(i - 1, 0), 0))
    nxt = lambda w: pl.BlockSpec((ATT_BLOCK, w), lambda b, i: (b * nb + jnp.minimum(i + 1, nb - 1), 0))
    const2 = lambda s: pl.BlockSpec(s, lambda b, i: (0, 0))
    shard = [s.shape for s in late_own]
    res = pl.pallas_call(
        body, name="attn_backward", grid=(bl, nb),
        in_specs=[pl.BlockSpec(memory_space=pltpu.SMEM), cur(512), nxt(512), prev(128), cur(128), prev(128), cur(128),
                  cur(512), nxt(512), cur(512), nxt(512), cur(N_HEADS), nxt(N_HEADS)]
        + [pl.BlockSpec(memory_space=pl.ANY)] * n + [const2(s) for s in shard],
        out_specs=(cur(512), cur(128), cur(128), const2((1, N_HEADS))) + tuple(const2(s) for s in shard),
        out_shape=(jax.ShapeDtypeStruct((t, D_ATTN), F32), jax.ShapeDtypeStruct((t, 128), F32),
                   jax.ShapeDtypeStruct((t, 128), F32), jax.ShapeDtypeStruct((1, N_HEADS), F32))
        + tuple(jax.ShapeDtypeStruct(s, F32) for s in shard),
        scratch_shapes=[pltpu.VMEM((3,) + s, BF16) for s in shard] + [pltpu.VMEM((4,) + s, BF16) for s in shard]
        + [pltpu.VMEM((3,) + s, BF16) for s in shard] + [pltpu.VMEM((3,) + s, BF16) for s in shard]
        + [pltpu.SemaphoreType.DMA((7 * n,)), pltpu.SemaphoreType.DMA((7 * n,)), pltpu.SemaphoreType.DMA((3 * n,))],
        compiler_params=_tc_params(("arbitrary", "arbitrary")),
    )(sinks, q, q, k, k, v, v, o, o, do, do, lse, lse, *late16, *late_own)
    return res[:4], list(res[4:])


def _mix_forward_backward(x2, y_perm, z_ssm, attn, z_attn, p2, target2, w_glu, b_glu, w_out, g_post, w_gate, b_gate,
                          w_proj, bl, seg):
    t = x2.shape[0]
    tm = seg

    def body(x_ref, y_ref, zs_ref, at_ref, za_ref, p_ref, tg_ref,
             wglu_ref, bglu_ref, wout_ref, gpost_ref, wgate_ref, bgate_ref, wproj_ref,
             loss_ref, dh1_ref, dy_ref, dzs_ref, dat_ref, dza_ref,
             dwglu_ref, dbglu_ref, dwout_ref, dgpost_ref, dwgate_ref, dbgate_ref, dwproj_ref):
        i = pl.program_id(0)
        y = y_ref[0]
        u3 = GELU_C * (y + GELU_K * y * y * y)
        th = jnp.tanh(u3)
        gl = 0.5 * y * (1.0 + th)
        a = _mm(gl, wglu_ref[...]) + bglu_ref[...]
        sa = _sigmoid(a)
        glu = gl * sa
        zs = zs_ref[...]
        sgs = _sigmoid(zs)
        ssm_out = glu * (zs * sgs)
        za = za_ref[...]
        sga = _sigmoid(za)
        at = at_ref[...]
        attn_out = at * (za * sga)
        cat = jnp.concatenate([ssm_out, attn_out], axis=-1).astype(BF16)
        mixed = _mm(cat, wout_ref[...])
        r2 = lax.rsqrt(jnp.mean(mixed * mixed, axis=-1, keepdims=True) + EPS)
        nhat = mixed * r2
        gpost = gpost_ref[...]
        h1 = x_ref[...] + nhat * gpost
        gate = _sigmoid(_mm(h1, wgate_ref[...]) + bgate_ref[...])
        pv = p_ref[...]
        pp = _mm(pv, wproj_ref[...])
        h2 = h1 + gate * pp
        err = h2 - tg_ref[...]
        loss_part = jnp.sum(jnp.sum(err * err, axis=-1, keepdims=True), axis=0, keepdims=True) * (0.5 / D_MODEL)
        dh2 = err * (1.0 / D_MODEL)
        dgp = dh2 * pp * gate * (1.0 - gate)
        dpp = dh2 * gate
        dh1 = dh2 + _mm_nt(dgp, wgate_ref[...])
        dh1_ref[...] = dh1
        dnhat = dh1 * gpost
        dmixed = r2 * (dnhat - nhat * jnp.mean(dnhat * nhat, axis=-1, keepdims=True))
        dcat = _mm_nt(dmixed, wout_ref[...])
        dso, dao = dcat[:, 0:D_SSM], dcat[:, D_SSM:]
        dat_ref[...] = dao * (za * sga)
        dza_ref[...] = dao * at * (sga * (1.0 + za * (1.0 - sga)))
        dzs_ref[...] = dso * glu * (sgs * (1.0 + zs * (1.0 - sgs)))
        dglu = dso * (zs * sgs)
        da = dglu * gl * sa * (1.0 - sa)
        dgl = dglu * sa + _mm_nt(da, wglu_ref[...])
        dgelu = 0.5 * (1.0 + th) + 0.5 * y * (1.0 - th * th) * (GELU_C * (1.0 + 3.0 * GELU_K * y * y))
        dy_ref[0] = dgl * dgelu
        parts = (
            (dwglu_ref, _mm_tn(gl, da)), (dbglu_ref, jnp.sum(da, axis=0, keepdims=True)),
            (dwout_ref, _mm_tn(cat, dmixed)), (dgpost_ref, jnp.sum(dh1 * nhat, axis=0, keepdims=True)),
            (dwgate_ref, _mm_tn(h1, dgp)), (dbgate_ref, jnp.sum(dgp, axis=0, keepdims=True)),
            (dwproj_ref, _mm_tn(pv, dpp)), (loss_ref, loss_part),
        )

        @pl.when(i == 0)
        def _():
            for ref, val in parts:
                ref[...] = val

        @pl.when(i != 0)
        def _():
            for ref, val in parts:
                ref[...] += val

    row = lambda w: pl.BlockSpec((tm, w), lambda i: (i, 0))
    perm = pl.BlockSpec((1, tm, D_SSM), lambda i: (i // N_SEG, 0, i % N_SEG))
    perm_shape = jax.ShapeDtypeStruct((bl, seg, N_SEG * D_SSM), F32)
    acc = lambda r, c: (_const_spec((r, c)), jax.ShapeDtypeStruct((r, c), F32))
    accs = [acc(D_SSM, D_SSM), acc(1, D_SSM), acc(D_MODEL, D_MODEL), acc(1, D_MODEL), acc(D_MODEL, D_MODEL),
            acc(1, D_MODEL), acc(D_PLE, D_MODEL)]
    return pl.pallas_call(
        body, name="mix_forward_backward", grid=(t // tm,),
        in_specs=[row(D_MODEL), perm, row(512), row(512), row(512), row(D_PLE), row(D_MODEL),
                  _const_spec((D_SSM, D_SSM)), _const_spec((1, D_SSM)), _const_spec((D_MODEL, D_MODEL)),
                  _const_spec((1, D_MODEL)), _const_spec((D_MODEL, D_MODEL)), _const_spec((1, D_MODEL)),
                  _const_spec((D_PLE, D_MODEL))],
        out_specs=(_const_spec((1, 1)), row(D_MODEL), perm, row(512), row(512), row(512)) + tuple(a[0] for a in accs),
        out_shape=(jax.ShapeDtypeStruct((1, 1), F32), jax.ShapeDtypeStruct((t, D_MODEL), F32), perm_shape,
                   jax.ShapeDtypeStruct((t, 512), F32), jax.ShapeDtypeStruct((t, 512), F32),
                   jax.ShapeDtypeStruct((t, 512), F32)) + tuple(a[1] for a in accs),
        compiler_params=_tc_params(("arbitrary",)),
    )(x2, y_perm, z_ssm, attn, z_attn, p2, target2, w_glu, b_glu, w_out, g_post, w_gate, b_gate, w_proj)


def _in_backward(x2, dh1, du_perm, dz_ssm, dq, dk, dv, dz_attn, g_pre, w_in, bl, seg):
    t = x2.shape[0]
    tm = seg

    def body(x_ref, dh1_ref, du_ref, dzs_ref, dq_ref, dk_ref, dv_ref, dza_ref, g_ref, w_ref,
             gx_ref, dw_ref, dg_ref):
        i = pl.program_id(0)
        xv = x_ref[...]
        r = lax.rsqrt(jnp.mean(xv * xv, axis=-1, keepdims=True) + EPS)
        xhat = xv * r
        g = g_ref[...]
        hn = (xhat * g).astype(BF16)
        dproj = jnp.concatenate([du_ref[0].astype(BF16), dzs_ref[...].astype(BF16), dq_ref[...].astype(BF16),
                                 dk_ref[...].astype(BF16), dv_ref[...].astype(BF16), dza_ref[...].astype(BF16)],
                                axis=-1)
        dhn = _mm_nt(dproj, w_ref[...])
        dxhat = dhn * g
        gx_ref[...] = dh1_ref[...] + r * (dxhat - xhat * jnp.mean(dxhat * xhat, axis=-1, keepdims=True))
        dwp = _mm_tn(hn, dproj)
        dgp = jnp.sum(dhn * xhat, axis=0, keepdims=True)

        @pl.when(i == 0)
        def _():
            dw_ref[...] = dwp
            dg_ref[...] = dgp

        @pl.when(i != 0)
        def _():
            dw_ref[...] += dwp
            dg_ref[...] += dgp

    row = lambda w: pl.BlockSpec((tm, w), lambda i: (i, 0))
    perm = pl.BlockSpec((1, tm, D_SSM), lambda i: (i // N_SEG, 0, i % N_SEG))
    return pl.pallas_call(
        body, name="in_backward", grid=(t // tm,),
        in_specs=[row(D_MODEL), row(D_MODEL), perm, row(512), row(512), row(128), row(128), row(512),
                  _const_spec((1, D_MODEL)), _const_spec((D_MODEL, D_IN))],
        out_specs=(row(D_MODEL), _const_spec((D_MODEL, D_IN)), _const_spec((1, D_MODEL))),
        out_shape=(jax.ShapeDtypeStruct((t, D_MODEL), F32), jax.ShapeDtypeStruct((D_MODEL, D_IN), F32),
                   jax.ShapeDtypeStruct((1, D_MODEL), F32)),
        compiler_params=_tc_params(("arbitrary",)),
    )(x2, dh1, du_perm, dz_ssm, dq, dk, dv, dz_attn, g_pre, w_in)


def _block_diag(t):
    a, b = t.shape[1], t.shape[2]
    eye = jnp.eye(G_TILE, dtype=t.dtype)
    t = t.reshape(N_GT, G_TILE, a, 1, b) * eye[None, :, None, :, None]
    return t.reshape(N_GT, G_TILE * a, G_TILE * b)


def _diag_blocks(m, a, b):
    m = m.reshape(N_GT, G_TILE, a, G_TILE, b)
    return jnp.einsum("tgagb->tgab", m).reshape(SSM_G, a, b)


def _local_step(x, p, target, pre_norm_g, w_in, ssm_lam_re, ssm_lam_im, ssm_log_step, ssm_b_re, ssm_b_im, ssm_c_re,
                ssm_c_im, ssm_d, ssm_b_glu, attn_sinks, post_norm_g, pl_b_gate, late, me):
    bl, seq, _ = x.shape
    seg = seq // N_SEG
    nb = seq // ATT_BLOCK
    t = bl * seq
    x2 = x.reshape(t, D_MODEL)
    p2 = p.reshape(t, D_PLE)
    tg2 = target.reshape(t, D_MODEL)

    lam_re = ssm_lam_re.reshape(N_STATE, 1)
    lam_im = ssm_lam_im.reshape(N_STATE, 1)
    log_step = jnp.broadcast_to(ssm_log_step.reshape(SSM_G, 1), (SSM_G, SSM_N)).reshape(N_STATE, 1)
    b_re = ssm_b_re.reshape(N_STATE, SSM_P)
    b_im = ssm_b_im.reshape(N_STATE, SSM_P)
    a_re, a_im, bb_re, bb_im, pw_re, pw_im = _ssm_prep(lam_re, lam_im, log_step, b_re, b_im, seg)
    bb_re_t = jnp.swapaxes(bb_re.reshape(SSM_G, SSM_N, SSM_P), 1, 2)
    bb_im_t = jnp.swapaxes(bb_im.reshape(SSM_G, SSM_N, SSM_P), 1, 2)
    bcat = jnp.concatenate([_block_diag(bb_re_t), _block_diag(bb_im_t)], axis=-1).astype(BF16)
    ccat_t = jnp.concatenate([_block_diag(ssm_c_re.reshape(SSM_G, SSM_P, SSM_N)),
                              -_block_diag(ssm_c_im.reshape(SSM_G, SSM_P, SSM_N))], axis=-1).astype(BF16)
    bcat_t = jnp.swapaxes(bcat, 1, 2)
    ccat = jnp.swapaxes(ccat_t, 1, 2)
    a_re_row, a_im_row = a_re.reshape(1, N_STATE), a_im.reshape(1, N_STATE)
    d_row = ssm_d.reshape(1, D_SSM)

    (u_perm, z_ssm, q, k, v, z_attn), gathered = _in_proj(x2, pre_norm_g.reshape(1, D_MODEL), w_in, late, bl, seg)
    w_out, w_gate, w_proj, w_glu = (_gathered_to_full(n, g) for n, g in zip(LATE_NAMES, gathered))
    u_perm = u_perm.reshape(bl, seq, D_SSM)
    y_perm = _ssm_forward(u_perm, bcat, ccat, a_re_row, a_im_row, pw_re, pw_im, d_row, seg)
    sinks = attn_sinks.reshape(N_HEADS)
    attn, lse = _attn_forward(q, k, v, sinks, bl, nb)
    (loss, dh1, dy_perm, dz_ssm, dattn, dz_attn, d_w_glu, d_b_glu, d_w_out, d_g_post, d_w_gate, d_b_gate,
     d_w_proj) = _mix_forward_backward(
        x2, y_perm.reshape(bl, seg, N_SEG * D_SSM), z_ssm, attn, z_attn, p2, tg2, w_glu,
        ssm_b_glu.reshape(1, D_SSM), w_out, post_norm_g.reshape(1, D_MODEL), w_gate, pl_b_gate.reshape(1, D_MODEL),
        w_proj, bl, seg)
    owned = [_full_to_owned(n, d) for n, d in zip(LATE_NAMES, (d_w_out, d_w_gate, d_w_proj, d_w_glu))]
    (dq, dk, dv, d_sinks), late_grads = _attn_backward(
        q, k, v, attn, dattn, lse, sinks, [o.astype(BF16) for o in owned],
        [lax.dynamic_index_in_dim(o, me, axis=0, keepdims=False) for o in owned], bl, nb)
    du_perm, d_bcat, d_ccat_t, da_re, da_im, d_d = _ssm_backward(
        u_perm, dy_perm.reshape(bl, seq, D_SSM), bcat, bcat_t, ccat_t, a_re_row, a_im_row, pw_re, pw_im, d_row, seg)
    grad_x, d_w_in, d_g_pre = _in_backward(
        x2, dh1, du_perm.reshape(bl, seg, N_SEG * D_SSM), dz_ssm, dq, dk, dv, dz_attn,
        pre_norm_g.reshape(1, D_MODEL), w_in, bl, seg)
    dbb_re = jnp.swapaxes(_diag_blocks(d_bcat[:, :, 0:ST_T], SSM_P, SSM_N), 1, 2).reshape(N_STATE, SSM_P)
    dbb_im = jnp.swapaxes(_diag_blocks(d_bcat[:, :, ST_T:], SSM_P, SSM_N), 1, 2).reshape(N_STATE, SSM_P)
    d_lam_re, d_lam_im, d_ls, d_b_re, d_b_im = _ssm_param_grads(
        lam_re, lam_im, log_step, b_re, b_im, a_re, a_im, da_re.reshape(N_STATE, 1), da_im.reshape(N_STATE, 1),
        dbb_re, dbb_im)
    grads = {
        "pre_norm_g": d_g_pre, "w_in": d_w_in, "ssm_lam_re": d_lam_re, "ssm_lam_im": d_lam_im,
        "ssm_log_step": jnp.sum(d_ls.reshape(SSM_G, SSM_N), axis=-1), "ssm_b_re": d_b_re, "ssm_b_im": d_b_im,
        "ssm_c_re": _diag_blocks(d_ccat_t[:, :, 0:ST_T], SSM_P, SSM_N),
        "ssm_c_im": -_diag_blocks(d_ccat_t[:, :, ST_T:], SSM_P, SSM_N),
        "ssm_d": d_d, "ssm_b_glu": d_b_glu, "attn_sinks": d_sinks, "post_norm_g": d_g_post, "pl_b_gate": d_b_gate,
    }
    return loss, grad_x.reshape(bl, seq, D_MODEL), grads, late_grads


LATE_NAMES = ("w_out", "pl_w_gate", "pl_w_proj", "ssm_w_glu")
BIG_NAMES = ("w_in",) + LATE_NAMES
COL_SHARDED = {"w_in": D_IN // N_DEV, "pl_w_proj": D_MODEL // N_DEV}
WEIGHT_NAMES = ("pre_norm_g", "w_in", "ssm_lam_re", "ssm_lam_im", "ssm_log_step", "ssm_b_re", "ssm_b_im", "ssm_c_re",
                "ssm_c_im", "ssm_d", "ssm_w_glu", "ssm_b_glu", "attn_sinks", "w_out", "post_norm_g", "pl_w_proj",
                "pl_w_gate", "pl_b_gate")


def _gathered_to_full(name, g):
    _, rows, cols = g.shape
    if name in COL_SHARDED:
        return jnp.swapaxes(g, 0, 1).reshape(rows, N_DEV * cols)
    return g.reshape(N_DEV * rows, cols)


def _full_to_owned(name, full):
    if name in COL_SHARDED:
        return jnp.swapaxes(full.reshape(full.shape[0], N_DEV, COL_SHARDED[name]), 0, 1)
    return full.reshape(N_DEV, full.shape[0] // N_DEV, full.shape[1])


def kernel(x, p, pre_norm_g, w_in, ssm_lam_re, ssm_lam_im, ssm_log_step, ssm_b_re, ssm_b_im, ssm_c_re, ssm_c_im, ssm_d, ssm_w_glu, ssm_b_glu, attn_sinks, w_out, post_norm_g, pl_w_proj, pl_w_gate, pl_b_gate, loss_target, m_pre_norm_g, m_w_in, m_ssm_lam_re, m_ssm_lam_im, m_ssm_log_step, m_ssm_b_re, m_ssm_b_im, m_ssm_c_re, m_ssm_c_im, m_ssm_d, m_ssm_w_glu, m_ssm_b_glu, m_attn_sinks, m_w_out, m_post_norm_g, m_pl_w_proj, m_pl_w_gate, m_pl_b_gate, v_pre_norm_g, v_w_in, v_ssm_lam_re, v_ssm_lam_im, v_ssm_log_step, v_ssm_b_re, v_ssm_b_im, v_ssm_c_re, v_ssm_c_im, v_ssm_d, v_ssm_w_glu, v_ssm_b_glu, v_attn_sinks, v_w_out, v_post_norm_g, v_pl_w_proj, v_pl_w_gate, v_pl_b_gate):
    w = dict(pre_norm_g=pre_norm_g, w_in=w_in, ssm_lam_re=ssm_lam_re, ssm_lam_im=ssm_lam_im, ssm_log_step=ssm_log_step,
             ssm_b_re=ssm_b_re, ssm_b_im=ssm_b_im, ssm_c_re=ssm_c_re, ssm_c_im=ssm_c_im, ssm_d=ssm_d, ssm_w_glu=ssm_w_glu,
             ssm_b_glu=ssm_b_glu, attn_sinks=attn_sinks, w_out=w_out, post_norm_g=post_norm_g, pl_w_proj=pl_w_proj,
             pl_w_gate=pl_w_gate, pl_b_gate=pl_b_gate)
    m = dict(pre_norm_g=m_pre_norm_g, w_in=m_w_in, ssm_lam_re=m_ssm_lam_re, ssm_lam_im=m_ssm_lam_im,
             ssm_log_step=m_ssm_log_step, ssm_b_re=m_ssm_b_re, ssm_b_im=m_ssm_b_im, ssm_c_re=m_ssm_c_re,
             ssm_c_im=m_ssm_c_im, ssm_d=m_ssm_d, ssm_w_glu=m_ssm_w_glu, ssm_b_glu=m_ssm_b_glu, attn_sinks=m_attn_sinks,
             w_out=m_w_out, post_norm_g=m_post_norm_g, pl_w_proj=m_pl_w_proj, pl_w_gate=m_pl_w_gate,
             pl_b_gate=m_pl_b_gate)
    v = dict(pre_norm_g=v_pre_norm_g, w_in=v_w_in, ssm_lam_re=v_ssm_lam_re, ssm_lam_im=v_ssm_lam_im,
             ssm_log_step=v_ssm_log_step, ssm_b_re=v_ssm_b_re, ssm_b_im=v_ssm_b_im, ssm_c_re=v_ssm_c_re,
             ssm_c_im=v_ssm_c_im, ssm_d=v_ssm_d, ssm_w_glu=v_ssm_w_glu, ssm_b_glu=v_ssm_b_glu, attn_sinks=v_attn_sinks,
             w_out=v_w_out, post_norm_g=v_post_norm_g, pl_w_proj=v_pl_w_proj, pl_w_gate=v_pl_w_gate,
             pl_b_gate=v_pl_b_gate)
    me = _slot(lax.axis_index("x"), lax.axis_index("y"), lax.axis_index("c"))

    (gathered,) = _allgather_weights([w["w_in"][0]])
    loss, grad_x, grads, g_late = _local_step(
        x, p[0], loss_target, pre_norm_g[0], _gathered_to_full("w_in", gathered), ssm_lam_re[0], ssm_lam_im[0],
        ssm_log_step[0], ssm_b_re[0], ssm_b_im[0], ssm_c_re[0], ssm_c_im[0], ssm_d[0], ssm_b_glu[0], attn_sinks[0],
        post_norm_g[0], pl_b_gate[0], [w[n][0] for n in LATE_NAMES], me)

    owned = _full_to_owned("w_in", grads["w_in"])
    tiny_form = lambda d: [d[n].reshape(rows, cols) for n, rows, cols in TINY]
    med_form = lambda d: [d[n].reshape(N_DEV, rows // N_DEV, cols) for n, rows, cols in MEDIUM]
    g_big, loss, g_tiny, g_med = _reduce_final(
        [owned.astype(BF16)], [lax.dynamic_index_in_dim(owned, me, axis=0, keepdims=False)],
        loss, tiny_form(grads), med_form(grads))
    names = BIG_NAMES + tuple(n for n, _, _ in TINY + MEDIUM)
    form = lambda d: [d[n][0] for n in BIG_NAMES] + tiny_form(d) + med_form(d)
    updated = _adamw_update(g_big + g_late + g_tiny + g_med, form(w), form(m), form(v))
    vals = dict(zip(names, updated))
    results = [[vals[n][kind].reshape(w[n].shape) for n in WEIGHT_NAMES] for kind in range(4)]
    return (loss.reshape(()), grad_x, *results[0], *results[1], *results[2], *results[3])
```

```python
import functools
import math

import jax
import jax.numpy as jnp
from jax import lax
from jax.experimental import pallas as pl
from jax.experimental.pallas import tpu as pltpu

F32 = jnp.float32
BF16 = jnp.bfloat16

D_MODEL = 1024
D_SSM = 512
D_ATTN = 512
SSM_P = 16
SSM_G = 32
SSM_N = 64
N_HEADS = 8
KV_HEADS = 2
Q_PER_KV = 4
HEAD_DIM = 64
ATT_BLOCK = 128
D_PLE = 256
D_IN = 2304
EPS = 1e-6
N_DEV = 8
N_SEG = 8
G_TILE = 8
N_GT = SSM_G // G_TILE
CH_T = G_TILE * SSM_P
ST_T = G_TILE * SSM_N
N_STATE = SSM_G * SSM_N
SCAN_UNROLL = 4
LANES = 128
VMEM_LIMIT = 60 * 1024 * 1024

ADAM_LR = 0.001
ADAM_B1 = 0.9
ADAM_B2 = 0.999
ADAM_EPS = 1e-08
ADAM_WD = 0.01
ADAM_STEP = 10

GELU_C = math.sqrt(2.0 / math.pi)
GELU_K = 0.044715
ATT_SCALE = 1.0 / math.sqrt(HEAD_DIM)
NEG_BIG = -1e30


def _mm(a, b):
    return jnp.dot(a.astype(BF16), b.astype(BF16), preferred_element_type=F32)


def _mm_nt(a, b):
    return lax.dot_general(a.astype(BF16), b.astype(BF16), (((1,), (1,)), ((), ())), preferred_element_type=F32)


def _mm_tn(a, b):
    return lax.dot_general(a.astype(BF16), b.astype(BF16), (((0,), (0,)), ((), ())), preferred_element_type=F32)


def _sigmoid(x):
    return 1.0 / (1.0 + jnp.exp(-x))


def _tc_params(sem):
    return pltpu.CompilerParams(dimension_semantics=sem, vmem_limit_bytes=VMEM_LIMIT)


def _const_spec(shape):
    nd = len(shape)
    return pl.BlockSpec(shape, lambda *_: (0,) * nd)


def _mesh_pos():
    return lax.axis_index("x"), lax.axis_index("y"), lax.axis_index("c")


ROW_CHUNKS = (64, 32, 16)


def _row_chunk(nrows):
    return next((c for c in ROW_CHUNKS if nrows % c == 0), None)


def _row_chunks(nrows, fn):
    chunk = _row_chunk(nrows)

    def step(i, carry):
        fn(pl.ds(pl.multiple_of(i * chunk, chunk), chunk))
        return carry

    lax.fori_loop(0, nrows // chunk, step, 0)


def _slot(px, py, pc):
    return 4 * px + 2 * py + pc


def _allgather_weights(shards):
    n = len(shards)

    def body(*refs):
        srcs, outs, (send_sems, recv_sems) = refs[:n], refs[n:2 * n], refs[2 * n:]
        x, y, c = _mesh_pos()
        me, sibling = (x, y, c), (x, y, 1 - c)
        chips = [(1 - x, y), (x, 1 - y), (1 - x, 1 - y)]

        def copy(a, k, block, to):
            blk = outs[a].at[_slot(*block)]
            return pltpu.make_async_remote_copy(
                src_ref=blk, dst_ref=blk, send_sem=send_sems.at[7 * a + k], recv_sem=recv_sems.at[7 * a + k],
                device_id=to, device_id_type=pl.DeviceIdType.MESH)

        sends = []
        for a in range(n):
            mine = outs[a].at[_slot(*me)]

            def cast(r, mine=mine, src=srcs[a]):
                mine[r, :] = src[r, :].astype(BF16)

            _row_chunks(srcs[a].shape[0], cast)
            first = [copy(a, 0, me, sibling)] + [copy(a, 1 + j, me, (*chip, c)) for j, chip in enumerate(chips)]
            for cp in first:
                cp.start()
            sends += first
        for a in range(n):
            for j, chip in enumerate(chips):
                copy(a, 1 + j, (*chip, c), me).wait_recv()
                fwd = copy(a, 4 + j, (*chip, c), sibling)
                fwd.start()
                sends.append(fwd)
        for a in range(n):
            copy(a, 0, sibling, me).wait_recv()
            for j, chip in enumerate(chips):
                copy(a, 4 + j, (*chip, 1 - c), me).wait_recv()
        for cp in sends:
            cp.wait_send()

    vm = pl.BlockSpec(memory_space=pltpu.VMEM)
    return pl.pallas_call(
        body, name="allgather_weights",
        out_shape=tuple(jax.ShapeDtypeStruct((N_DEV,) + s.shape, BF16) for s in shards),
        in_specs=[vm] * n, out_specs=(vm,) * n,
        scratch_shapes=[pltpu.SemaphoreType.DMA((7 * n,)), pltpu.SemaphoreType.DMA((7 * n,))],
        compiler_params=pltpu.CompilerParams(vmem_limit_bytes=VMEM_LIMIT),
    )(*shards)


def _adamw(w, g, m, v):
    m = ADAM_B1 * m + (1.0 - ADAM_B1) * g
    v = ADAM_B2 * v + (1.0 - ADAM_B2) * (g * g)
    m_hat = m / (1.0 - ADAM_B1 ** ADAM_STEP)
    v_hat = v / (1.0 - ADAM_B2 ** ADAM_STEP)
    delta = -ADAM_LR * (m_hat / (jnp.sqrt(v_hat) + ADAM_EPS) + ADAM_WD * w)
    return delta, m, v


def _remote(src, dst, send_sems, recv_sems, k, to):
    return pltpu.make_async_remote_copy(src_ref=src, dst_ref=dst, send_sem=send_sems.at[k], recv_sem=recv_sems.at[k],
                                        device_id=to, device_id_type=pl.DeviceIdType.MESH)


def _big_reduce_phases(g16_r, go_r, outs, send2, recv1, recv2, s_send, s_recv):
    n = len(g16_r)
    x, y, c = _mesh_pos()
    sibling = (x, y, 1 - c)
    chips = [(1 - x, y), (x, 1 - y), (1 - x, 1 - y)]
    all_chips = [(x, y)] + chips
    lvl1 = []
    for a in range(n):
        cps = [_remote(g16_r[a].at[_slot(*chip, 1 - c)], recv1[a].at[j], s_send, s_recv, 7 * a + j, sibling)
               for j, chip in enumerate(all_chips)]
        for cp in cps:
            cp.start()
        lvl1.append(cps)
    yield
    lvl2 = []
    for a in range(n):
        for cp in lvl1[a]:
            cp.wait_recv()
        og = outs[a]

        def partials(r, a=a, og=og):
            og[r, :] = go_r[a][r, :] + recv1[a][0, r, :].astype(F32)
            for j, chip in enumerate(chips):
                mine16 = g16_r[a][_slot(*chip, c), r, :].astype(F32)
                send2[a][j, r, :] = (mine16 + recv1[a][1 + j, r, :].astype(F32)).astype(BF16)

        _row_chunks(go_r[a].shape[0], partials)
        cps = [_remote(send2[a].at[j], recv2[a].at[j], s_send, s_recv, 7 * a + 4 + j, (*chip, c))
               for j, chip in enumerate(chips)]
        for cp in cps:
            cp.start()
        lvl2.append(cps)
    yield
    for a in range(n):
        for cp in lvl2[a]:
            cp.wait_recv()
        og = outs[a]

        def total(r, a=a, og=og):
            g = og[r, :]
            for j in range(3):
                g = g + recv2[a][j, r, :].astype(F32)
            og[r, :] = g

        _row_chunks(go_r[a].shape[0], total)
    yield
    for cps in lvl1 + lvl2:
        for cp in cps:
            cp.wait_send()


def _interleave(*phases):
    active = list(phases)
    while active:
        for g in list(active):
            try:
                next(g)
            except StopIteration:
                active.remove(g)


def _adamw_update(g, w, m, v):
    n = len(g)

    def body(*refs):
        g_r, w_r, m_r, v_r = (refs[i * n:(i + 1) * n] for i in range(4))
        outs = refs[4 * n:]
        for a in range(n):
            og, od, om, ov = outs[4 * a:4 * a + 4]

            def update(idx, a=a, og=og, od=od, om=om, ov=ov):
                gv = g_r[a][idx]
                d, nm, nv = _adamw(w_r[a][idx], gv, m_r[a][idx], v_r[a][idx])
                og[idx] = gv
                od[idx] = d
                om[idx] = nm
                ov[idx] = nv

            shape = g_r[a].shape
            if len(shape) == 3:
                for b in range(shape[0]):
                    update(b)
            elif _row_chunk(shape[0]) is not None:
                _row_chunks(shape[0], update)
            else:
                update(Ellipsis)

    vm = pl.BlockSpec(memory_space=pltpu.VMEM)
    res = pl.pallas_call(
        body, name="adamw_update",
        out_shape=tuple(jax.ShapeDtypeStruct(t.shape, F32) for t in g for _ in range(4)),
        in_specs=[vm] * (4 * n), out_specs=(vm,) * (4 * n),
        compiler_params=pltpu.CompilerParams(vmem_limit_bytes=VMEM_LIMIT),
    )(*g, *w, *m, *v)
    return [res[4 * a:4 * a + 4] for a in range(n)]


TINY = (("pre_norm_g", 1, 1024), ("post_norm_g", 1, 1024), ("pl_b_gate", 1, 1024), ("ssm_d", 1, 512),
        ("ssm_b_glu", 1, 512), ("ssm_log_step", 1, 32), ("attn_sinks", 1, 8), ("ssm_lam_re", 32, 64),
        ("ssm_lam_im", 32, 64))
MEDIUM = (("ssm_b_re", SSM_G * SSM_P, SSM_N), ("ssm_b_im", SSM_G * SSM_P, SSM_N), ("ssm_c_re", SSM_G * SSM_P, SSM_N),
          ("ssm_c_im", SSM_G * SSM_P, SSM_N))


def _stage_rows():
    offs, r = {}, 0
    for name, rows, cols in TINY + (("loss", 1, 1),):
        if rows > 1:
            r = -(-r // 8) * 8
        offs[name] = r
        r += rows if rows > 1 else max(cols // LANES, 1)
    return offs, -(-r // 8) * 8


def _reduce_final(g16, gown, loss, g_tiny, g_med):
    nb_, nt, nm_ = len(g16), len(TINY), len(MEDIUM)
    offs, stage_rows = _stage_rows()

    def body(*refs):
        g16_r, go_r = refs[:nb_], refs[nb_:2 * nb_]
        base = 2 * nb_
        loss_r, gt, gm = refs[base], refs[base + 1:base + 1 + nt], refs[base + 1 + nt:base + 1 + nt + nm_]
        base += 1 + nt + nm_
        out_b = refs[base:base + nb_]
        base += nb_
        loss_o, out_t, out_m = refs[base], refs[base + 1:base + 1 + nt], refs[base + 1 + nt:base + 1 + nt + nm_]
        base += 1 + nt + nm_
        send2_b, recv1_b, recv2_b = (refs[base + i * nb_:base + (i + 1) * nb_] for i in range(3))
        base += 3 * nb_
        stage = refs[base]
        recv1, part, recv2 = (refs[base + 1 + i * nm_:base + 1 + (i + 1) * nm_] for i in range(3))
        bs_send, bs_recv, s_send, s_recv = refs[base + 1 + 3 * nm_:]
        _interleave(_big_reduce_phases(g16_r, go_r, out_b, send2_b, recv1_b, recv2_b, bs_send, bs_recv),
                    small_phases(loss_r, gt, gm, loss_o, out_t, out_m, stage, recv1, part, recv2, s_send, s_recv))

    def small_phases(loss_r, gt, gm, loss_o, out_t, out_m, stage, recv1, part, recv2, s_send, s_recv):
        x, y, c = _mesh_pos()
        me = _slot(x, y, c)
        sibling = (x, y, 1 - c)
        chips = [(1 - x, y), (x, 1 - y), (1 - x, 1 - y)]
        all_chips = [(x, y)] + chips
        peers = [sibling] + [(*chip, c) for chip in chips] + [(*chip, 1 - c) for chip in chips]
        sem = iter(range(7 + 14 * nm_))
        lvl1 = []
        for a in range(nm_):
            cps = [_remote(gm[a].at[_slot(*chip, 1 - c)], recv1[a].at[j], s_send, s_recv, next(sem), sibling)
                   for j, chip in enumerate(all_chips)]
            for cp in cps:
                cp.start()
            lvl1.append(cps)
        mine = stage.at[me]
        mine[...] = jnp.zeros((stage_rows, LANES), F32)
        for (name, rows, cols), ref in zip(TINY + (("loss", 1, 1),), gt + (loss_r,)):
            r0 = offs[name]
            if rows > 1:
                mine[r0:r0 + rows, 0:cols] = ref[...]
            elif cols >= LANES:
                for i in range(cols // LANES):
                    mine[r0 + i:r0 + i + 1, :] = ref[:, i * LANES:(i + 1) * LANES]
            else:
                mine[r0:r0 + 1, 0:cols] = ref[...]
        tiny_cps = [_remote(mine, mine, s_send, s_recv, next(sem), peer) for peer in peers]
        for cp in tiny_cps:
            cp.start()
        yield
        lvl2 = []
        for a in range(nm_):
            for cp in lvl1[a]:
                cp.wait_recv()
            for j, chip in enumerate(all_chips):
                part[a][j] = gm[a][_slot(*chip, c)] + recv1[a][j]
            cps = [_remote(part[a].at[1 + j], recv2[a].at[j], s_send, s_recv, next(sem), (*chip, c))
                   for j, chip in enumerate(chips)]
            for cp in cps:
                cp.start()
            lvl2.append(cps)
        yield
        lvl3 = []
        for a in range(nm_):
            for cp in lvl2[a]:
                cp.wait_recv()
            blk = out_m[a].at[me]
            blk[...] = ((part[a][0] + recv2[a][0]) + recv2[a][1]) + recv2[a][2]
            cps = [_remote(blk, blk, s_send, s_recv, next(sem), peer) for peer in peers]
            for cp in cps:
                cp.start()
            lvl3.append(cps)
        yield
        for cp in tiny_cps:
            cp.wait_recv()
        tot = stage[0]
        for d in range(1, N_DEV):
            tot = tot + stage[d]
        loss_o[...] = tot[offs["loss"]:offs["loss"] + 1, 0:1]
        for k, (name, rows, cols) in enumerate(TINY):
            r0 = offs[name]
            if rows > 1:
                out_t[k][...] = tot[r0:r0 + rows, 0:cols]
            elif cols >= LANES:
                for i in range(cols // LANES):
                    out_t[k][:, i * LANES:(i + 1) * LANES] = tot[r0 + i:r0 + i + 1, :]
            else:
                out_t[k][...] = tot[r0:r0 + 1, 0:cols]
        for cps in lvl3:
            for cp in cps:
                cp.wait_recv()
        for cps in lvl1 + lvl2 + lvl3 + [tiny_cps]:
            for cp in cps:
                cp.wait_send()

    vmem = pl.BlockSpec(memory_space=pltpu.VMEM)
    t_shapes = [jax.ShapeDtypeStruct((rows, cols), F32) for _, rows, cols in TINY]
    m_shapes = [jax.ShapeDtypeStruct((N_DEV, rows // N_DEV, cols), F32) for _, rows, cols in MEDIUM]
    blk = [(rows // N_DEV, cols) for _, rows, cols in MEDIUM]
    scratch = ([pltpu.VMEM((3,) + t.shape, BF16) for t in gown] + [pltpu.VMEM((4,) + t.shape, BF16) for t in gown]
               + [pltpu.VMEM((3,) + t.shape, BF16) for t in gown]
               + [pltpu.VMEM((N_DEV, stage_rows, LANES), F32)]
               + [pltpu.VMEM((4,) + b, F32) for b in blk] + [pltpu.VMEM((4,) + b, F32) for b in blk]
               + [pltpu.VMEM((3,) + b, F32) for b in blk]
               + [pltpu.SemaphoreType.DMA((7 * nb_,)), pltpu.SemaphoreType.DMA((7 * nb_,)),
                  pltpu.SemaphoreType.DMA((7 + 14 * nm_,)), pltpu.SemaphoreType.DMA((7 + 14 * nm_,))])
    n_in, n_out = 2 * nb_ + 1 + nt + nm_, nb_ + 1 + nt + nm_
    res = pl.pallas_call(
        body, name="reduce_final",
        out_shape=tuple(jax.ShapeDtypeStruct(t.shape, F32) for t in gown) + (jax.ShapeDtypeStruct((1, 1), F32),)
        + tuple(t_shapes) + tuple(m_shapes),
        in_specs=[vmem] * n_in, out_specs=(vmem,) * n_out, scratch_shapes=scratch,
        compiler_params=pltpu.CompilerParams(vmem_limit_bytes=VMEM_LIMIT),
    )(*g16, *gown, loss, *g_tiny, *g_med)
    return list(res[:nb_]), res[nb_], list(res[nb_ + 1:nb_ + 1 + nt]), list(res[nb_ + 1 + nt:])


def _in_proj(x2, g_pre, w_in, late, bl, seg):
    t = x2.shape[0]
    tm = seg
    steps = t // tm
    n = len(late)
    forward_step, last_step = (steps * 5) // 8, steps - 1

    def body(*refs):
        x_ref, g_ref, w_ref = refs[:3]
        late_r = refs[3:3 + n]
        u_ref, zs_ref, q_ref, k_ref, v_ref, za_ref = refs[3 + n:9 + n]
        gath = refs[9 + n:9 + 2 * n]
        cast = refs[9 + 2 * n:9 + 3 * n]
        send_sems, recv_sems, local_sems = refs[9 + 3 * n:]
        i = pl.program_id(0)
        x, y, c = _mesh_pos()
        me, sibling = (x, y, c), (x, y, 1 - c)
        chips = [(1 - x, y), (x, 1 - y), (1 - x, 1 - y)]

        def own(a, k, to):
            return _remote(cast[a], gath[a].at[_slot(*me)], send_sems, recv_sems, 7 * a + k, to)

        def relay(a, k, block, to):
            blk = gath[a].at[_slot(*block)]
            return _remote(blk, blk, send_sems, recv_sems, 7 * a + k, to)

        def keep(a):
            return pltpu.make_async_copy(cast[a], gath[a].at[_slot(*me)], local_sems.at[a])

        @pl.when(i == 0)
        def _():
            for a in range(n):
                def to16(r, a=a):
                    cast[a][r, :] = late_r[a][r, :].astype(BF16)

                _row_chunks(late_r[a].shape[0], to16)
                keep(a).start()
                own(a, 0, sibling).start()
                for j, chip in enumerate(chips):
                    own(a, 1 + j, (*chip, c)).start()

        xv = x_ref[...]
        r = lax.rsqrt(jnp.mean(xv * xv, axis=-1, keepdims=True) + EPS)
        hn = xv * r * g_ref[...]
        proj = _mm_nt(hn, w_ref[...])
        u_ref[0] = proj[:, 0:512]
        zs_ref[...] = proj[:, 512:1024]
        q_ref[...] = proj[:, 1024:1536].astype(BF16)
        k_ref[...] = proj[:, 1536:1664].astype(BF16)
        v_ref[...] = proj[:, 1664:1792].astype(BF16)
        za_ref[...] = proj[:, 1792:2304]

        @pl.when(i == forward_step)
        def _():
            for a in range(n):
                for j, chip in enumerate(chips):
                    relay(a, 1 + j, (*chip, c), me).wait_recv()
                    relay(a, 4 + j, (*chip, c), sibling).start()

        @pl.when(i == last_step)
        def _():
            for a in range(n):
                relay(a, 0, sibling, me).wait_recv()
                for j, chip in enumerate(chips):
                    relay(a, 4 + j, (*chip, 1 - c), me).wait_recv()
                own(a, 0, sibling).wait_send()
                for j, chip in enumerate(chips):
                    own(a, 1 + j, (*chip, c)).wait_send()
                    relay(a, 4 + j, (*chip, c), sibling).wait_send()
                keep(a).wait()

    row = lambda w: pl.BlockSpec((tm, w), lambda i: (i, 0))
    hbm = pl.BlockSpec(memory_space=pl.ANY)
    res = pl.pallas_call(
        body, name="in_proj", grid=(steps,),
        in_specs=[row(D_MODEL), _const_spec((1, D_MODEL)), _const_spec((D_IN, D_MODEL))]
        + [_const_spec(s.shape) for s in late],
        out_specs=(pl.BlockSpec((1, tm, D_SSM), lambda i: (i // N_SEG, 0, i % N_SEG)),
                   row(512), row(512), row(128), row(128), row(512)) + (hbm,) * n,
        out_shape=(jax.ShapeDtypeStruct((bl, seg, N_SEG * D_SSM), F32),
                   jax.ShapeDtypeStruct((t, 512), F32), jax.ShapeDtypeStruct((t, 512), BF16),
                   jax.ShapeDtypeStruct((t, 128), BF16), jax.ShapeDtypeStruct((t, 128), BF16),
                   jax.ShapeDtypeStruct((t, 512), F32))
        + tuple(jax.ShapeDtypeStruct((N_DEV,) + s.shape, BF16) for s in late),
        scratch_shapes=[pltpu.VMEM(s.shape, BF16) for s in late]
        + [pltpu.SemaphoreType.DMA((7 * n,)), pltpu.SemaphoreType.DMA((7 * n,)), pltpu.SemaphoreType.DMA((n,))],
        compiler_params=_tc_params(("arbitrary",)),
    )(x2, g_pre, w_in, *late)
    return res[:6], list(res[6:])


def _discretise(lr, li, ls):
    step = jnp.exp(ls)
    mag = jnp.exp(lr * step)
    ar = mag * jnp.cos(li * step)
    ai = mag * jnp.sin(li * step)
    den = lr * lr + li * li
    cr = ((ar - 1.0) * lr + ai * li) / den
    ci = (ai * lr - (ar - 1.0) * li) / den
    return step, ar, ai, den, cr, ci


def _per_channel(v):
    return jnp.broadcast_to(v[:, None, :], (SSM_G, SSM_P, SSM_N)).reshape(SSM_G * SSM_P, SSM_N)


def _ssm_prep(lam_re, lam_im, log_step, b_re, b_im, seg):
    def body(lr_ref, li_ref, ls_ref, br_ref, bi_ref, lrr_ref, lir_ref, lsr_ref,
             ar_ref, ai_ref, bbr_ref, bbi_ref, pr_ref, pi_ref):
        _, _, _, _, cr, ci = _discretise(lr_ref[...], li_ref[...], ls_ref[...])
        cr, ci = _per_channel(cr), _per_channel(ci)
        br, bi = br_ref[...], bi_ref[...]
        bbr_ref[...] = cr * br - ci * bi
        bbi_ref[...] = cr * bi + ci * br
        stepr = jnp.exp(lsr_ref[...])
        k = (lax.broadcasted_iota(jnp.int32, (8, N_STATE), 0) + 1).astype(F32)
        magk = jnp.exp(k * (lrr_ref[...] * stepr))
        ang = k * (lir_ref[...] * stepr)
        pr_ref[0:8, :] = magk * jnp.cos(ang)
        pi_ref[0:8, :] = magk * jnp.sin(ang)
        n = 8
        while n < seg:
            tr, ti = pr_ref[n - 1:n, :], pi_ref[n - 1:n, :]
            xr, xi = pr_ref[0:n, :], pi_ref[0:n, :]
            pr_ref[n:2 * n, :] = xr * tr - xi * ti
            pi_ref[n:2 * n, :] = xr * ti + xi * tr
            n *= 2
        ar_ref[...] = pr_ref[0:1, :]
        ai_ref[...] = pi_ref[0:1, :]

    row = jax.ShapeDtypeStruct((1, N_STATE), F32)
    mat = jax.ShapeDtypeStruct((SSM_G * SSM_P, SSM_N), F32)
    pw = jax.ShapeDtypeStruct((seg, N_STATE), F32)
    vm = pl.BlockSpec(memory_space=pltpu.VMEM)
    step_row = jnp.broadcast_to(log_step, (SSM_G, SSM_N)).reshape(1, N_STATE)
    return pl.pallas_call(
        body, name="ssm_prep", out_shape=(row, row, mat, mat, pw, pw),
        in_specs=[vm] * 8, out_specs=(vm,) * 6,
    )(lam_re, lam_im, log_step, b_re, b_im, lam_re.reshape(1, N_STATE), lam_im.reshape(1, N_STATE), step_row)


def _seg_rows(t):
    if isinstance(t, int):
        return pl.ds(t * N_SEG, N_SEG)
    return pl.ds(pl.multiple_of(t * N_SEG, N_SEG), N_SEG)


def _scan_forward(xs, a_re, a_im, pw_re, pw_im, cs, seg):
    are = jnp.broadcast_to(a_re, (N_SEG, ST_T))
    aim = jnp.broadcast_to(a_im, (N_SEG, ST_T))

    def steps(k, carry):
        xr, xi = carry
        for j in range(SCAN_UNROLL):
            r = pl.multiple_of((k * SCAN_UNROLL + j) * N_SEG, N_SEG)
            nr = are * xr - aim * xi + xs[pl.ds(r, N_SEG), 0:ST_T]
            ni = are * xi + aim * xr + xs[pl.ds(r, N_SEG), ST_T:2 * ST_T]
            xs[pl.ds(r, N_SEG), 0:ST_T] = nr
            xs[pl.ds(r, N_SEG), ST_T:2 * ST_T] = ni
            xr, xi = nr, ni
        return xr, xi

    zero = jnp.zeros((N_SEG, ST_T), F32)
    fr, fi = lax.fori_loop(0, seg // SCAN_UNROLL, steps, (zero, zero))
    sr, si = pw_re[seg - 1:seg, :], pw_im[seg - 1:seg, :]
    cr = jnp.zeros((1, ST_T), F32)
    ci = jnp.zeros((1, ST_T), F32)
    cs[0:1, :] = cr
    cs[8:9, :] = ci
    for s in range(1, N_SEG):
        ncr = sr * cr - si * ci + fr[s - 1:s, :]
        nci = sr * ci + si * cr + fi[s - 1:s, :]
        cr, ci = ncr, nci
        cs[s:s + 1, :] = cr
        cs[8 + s:9 + s, :] = ci
    car, cai = cs[0:8, :], cs[8:16, :]

    def fix(t, _):
        r = pl.multiple_of(t * N_SEG, N_SEG)
        pr, pi = pw_re[pl.ds(t, 1), :], pw_im[pl.ds(t, 1), :]
        xs[pl.ds(r, N_SEG), 0:ST_T] = xs[pl.ds(r, N_SEG), 0:ST_T] + (pr * car - pi * cai)
        xs[pl.ds(r, N_SEG), ST_T:2 * ST_T] = xs[pl.ds(r, N_SEG), ST_T:2 * ST_T] + (pr * cai + pi * car)
        return 0

    lax.fori_loop(0, seg, fix, 0, unroll=SCAN_UNROLL)


def _ssm_forward(u_perm, bcat, ccat, a_re, a_im, pw_re, pw_im, d_row, seg):
    bl, rows, _ = u_perm.shape

    def body(u_ref, b_ref, c_ref, ar_ref, ai_ref, pr_ref, pi_ref, d_ref, y_ref, xs, cs):
        u = u_ref[0]
        xs[...] = _mm(u, b_ref[0])
        _scan_forward(xs, ar_ref[...], ai_ref[...], pr_ref, pi_ref, cs, seg)
        y_ref[0] = _mm(xs[...], c_ref[0]) + d_ref[...] * u

    return pl.pallas_call(
        body, name="ssm_forward", grid=(bl, N_GT),
        in_specs=[pl.BlockSpec((1, rows, CH_T), lambda b, j: (b, 0, j)),
                  pl.BlockSpec((1, CH_T, 2 * ST_T), lambda b, j: (j, 0, 0)),
                  pl.BlockSpec((1, 2 * ST_T, CH_T), lambda b, j: (j, 0, 0)),
                  pl.BlockSpec((1, ST_T), lambda b, j: (0, j)), pl.BlockSpec((1, ST_T), lambda b, j: (0, j)),
                  pl.BlockSpec((seg, ST_T), lambda b, j: (0, j)), pl.BlockSpec((seg, ST_T), lambda b, j: (0, j)),
                  pl.BlockSpec((1, CH_T), lambda b, j: (0, j))],
        out_specs=pl.BlockSpec((1, rows, CH_T), lambda b, j: (b, 0, j)),
        out_shape=jax.ShapeDtypeStruct((bl, rows, D_SSM), F32),
        scratch_shapes=[pltpu.VMEM((rows, 2 * ST_T), F32), pltpu.VMEM((16, ST_T), F32)],
        compiler_params=_tc_params(("arbitrary", "arbitrary")),
    )(u_perm, bcat, ccat, a_re, a_im, pw_re, pw_im, d_row)


def _ssm_backward(u_perm, dy_perm, bcat, bcat_t, ccat_t, a_re, a_im, pw_re, pw_im, d_row, seg):
    bl, rows, _ = u_perm.shape

    def body(u_ref, dy_ref, b_ref, bt_ref, ct_ref, ar_ref, ai_ref, pr_ref, pi_ref, d_ref,
             du_ref, db_ref, dc_ref, dar_ref, dai_ref, dd_ref, xs, ls, cs, cl):
        b = pl.program_id(1)
        u = u_ref[0]
        dy = dy_ref[0]
        xs[...] = _mm(u, b_ref[0])
        _scan_forward(xs, ar_ref[...], ai_ref[...], pr_ref, pi_ref, cs, seg)
        ls[...] = _mm(dy, ct_ref[0])
        are = jnp.broadcast_to(ar_ref[...], (N_SEG, ST_T))
        aim = jnp.broadcast_to(ai_ref[...], (N_SEG, ST_T))

        def steps(k, carry):
            lr, li = carry
            for j in range(SCAN_UNROLL):
                r = pl.multiple_of((seg - 1 - (k * SCAN_UNROLL + j)) * N_SEG, N_SEG)
                nr = are * lr + aim * li + ls[pl.ds(r, N_SEG), 0:ST_T]
                ni = are * li - aim * lr + ls[pl.ds(r, N_SEG), ST_T:2 * ST_T]
                ls[pl.ds(r, N_SEG), 0:ST_T] = nr
                ls[pl.ds(r, N_SEG), ST_T:2 * ST_T] = ni
                lr, li = nr, ni
            return lr, li

        zero = jnp.zeros((N_SEG, ST_T), F32)
        fr, fi = lax.fori_loop(0, seg // SCAN_UNROLL, steps, (zero, zero))
        sr, si = pr_ref[seg - 1:seg, :], pi_ref[seg - 1:seg, :]
        cr = jnp.zeros((1, ST_T), F32)
        ci = jnp.zeros((1, ST_T), F32)
        cl[7:8, :] = cr
        cl[15:16, :] = ci
        for s in range(N_SEG - 2, -1, -1):
            ncr = sr * cr + si * ci + fr[s + 1:s + 2, :]
            nci = sr * ci - si * cr + fi[s + 1:s + 2, :]
            cr, ci = ncr, nci
            cl[s:s + 1, :] = cr
            cl[8 + s:9 + s, :] = ci
        clr, cli = cl[0:8, :], cl[8:16, :]

        def fix_rows(rows, t, xpr, xpi, acc):
            dr, di = acc
            pr, pi = pr_ref[pl.ds(seg - 1 - t, 1), :], pi_ref[pl.ds(seg - 1 - t, 1), :]
            lr = ls[rows, 0:ST_T] + (pr * clr + pi * cli)
            li = ls[rows, ST_T:2 * ST_T] + (pr * cli - pi * clr)
            ls[rows, 0:ST_T] = lr
            ls[rows, ST_T:2 * ST_T] = li
            return dr + (lr * xpr + li * xpi), di + (li * xpr - lr * xpi)

        def fix_at(t, acc):
            prev = _seg_rows(t - 1)
            return fix_rows(_seg_rows(t), t, xs[prev, 0:ST_T], xs[prev, ST_T:2 * ST_T], acc)

        def fix(k, acc):
            for j in range(SCAN_UNROLL):
                acc = fix_at(k * SCAN_UNROLL + j, acc)
            return acc

        acc = fix_rows(pl.ds(0, N_SEG), 0, cs[0:8, :], cs[8:16, :], (zero, zero))
        for t in range(1, SCAN_UNROLL):
            acc = fix_at(t, acc)
        dr, di = lax.fori_loop(1, seg // SCAN_UNROLL, fix, acc)
        dar = jnp.sum(dr, axis=0, keepdims=True)
        dai = jnp.sum(di, axis=0, keepdims=True)
        lall = ls[...]
        du_ref[0] = _mm(lall, bt_ref[0]) + d_ref[...] * dy
        dbp = _mm_tn(u, lall)
        dcp = _mm_tn(dy, xs[...])
        ddp = jnp.sum(dy * u, axis=0, keepdims=True)

        @pl.when(b == 0)
        def _():
            db_ref[0] = dbp
            dc_ref[0] = dcp
            dar_ref[...] = dar
            dai_ref[...] = dai
            dd_ref[...] = ddp

        @pl.when(b != 0)
        def _():
            db_ref[0] += dbp
            dc_ref[0] += dcp
            dar_ref[...] += dar
            dai_ref[...] += dai
            dd_ref[...] += ddp

    tile3 = lambda r, c: pl.BlockSpec((1, r, c), lambda j, b: (j, 0, 0))
    lane = lambda r, c: pl.BlockSpec((r, c), lambda j, b: (0, j))
    act = pl.BlockSpec((1, rows, CH_T), lambda j, b: (b, 0, j))
    return pl.pallas_call(
        body, name="ssm_backward", grid=(N_GT, bl),
        in_specs=[act, act, tile3(CH_T, 2 * ST_T), tile3(2 * ST_T, CH_T), tile3(CH_T, 2 * ST_T),
                  lane(1, ST_T), lane(1, ST_T), lane(seg, ST_T), lane(seg, ST_T), lane(1, CH_T)],
        out_specs=(act, tile3(CH_T, 2 * ST_T), tile3(CH_T, 2 * ST_T), lane(1, ST_T), lane(1, ST_T), lane(1, CH_T)),
        out_shape=(jax.ShapeDtypeStruct((bl, rows, D_SSM), F32),
                   jax.ShapeDtypeStruct((N_GT, CH_T, 2 * ST_T), F32), jax.ShapeDtypeStruct((N_GT, CH_T, 2 * ST_T), F32),
                   jax.ShapeDtypeStruct((1, N_STATE), F32), jax.ShapeDtypeStruct((1, N_STATE), F32),
                   jax.ShapeDtypeStruct((1, D_SSM), F32)),
        scratch_shapes=[pltpu.VMEM((rows, 2 * ST_T), F32), pltpu.VMEM((rows, 2 * ST_T), F32),
                        pltpu.VMEM((16, ST_T), F32), pltpu.VMEM((16, ST_T), F32)],
        compiler_params=_tc_params(("arbitrary", "arbitrary")),
    )(u_perm, dy_perm, bcat, bcat_t, ccat_t, a_re, a_im, pw_re, pw_im, d_row)


def _ssm_param_grads(lam_re, lam_im, log_step, b_re, b_im, da_re, da_im, dbb_re, dbb_im):
    def body(lr_ref, li_ref, ls_ref, br_ref, bi_ref, gar_ref, gai_ref, gbr_ref, gbi_ref,
             dlr_ref, dli_ref, dls_ref, dbr_ref, dbi_ref):
        lr, li = lr_ref[...], li_ref[...]
        step, ar, ai, den, cr, ci = _discretise(lr, li, ls_ref[...])
        crb, cib = _per_channel(cr), _per_channel(ci)
        br, bi = br_ref[...], bi_ref[...]
        gbr, gbi = gbr_ref[...], gbi_ref[...]
        dbr_ref[...] = crb * gbr + cib * gbi
        dbi_ref[...] = crb * gbi - cib * gbr
        over_channels = lambda t: jnp.sum(t.reshape(SSM_G, SSM_P, SSM_N), axis=1)
        gcr = over_channels(br * gbr + bi * gbi)
        gci = over_channels(br * gbi - bi * gbr)
        ilr, ili = lr / den, -li / den
        gar = gar_ref[...] + (ilr * gcr + ili * gci)
        gai = gai_ref[...] + (ilr * gci - ili * gcr)
        qr, qi = cr * ilr - ci * ili, cr * ili + ci * ilr
        glr = -(qr * gcr + qi * gci)
        gli = -(qr * gci - qi * gcr)
        gwr = ar * gar + ai * gai
        gwi = ar * gai - ai * gar
        dlr_ref[...] = glr + step * gwr
        dli_ref[...] = gli + step * gwi
        dls_ref[...] = jnp.sum(lr * gwr + li * gwi, axis=-1, keepdims=True) * step

    lam = jax.ShapeDtypeStruct((SSM_G, SSM_N), F32)
    mat = jax.ShapeDtypeStruct((SSM_G * SSM_P, SSM_N), F32)
    vm = pl.BlockSpec(memory_space=pltpu.VMEM)
    return pl.pallas_call(
        body, name="ssm_param_grads", out_shape=(lam, lam, jax.ShapeDtypeStruct((SSM_G, 1), F32), mat, mat),
        in_specs=[vm] * 9, out_specs=(vm,) * 5,
    )(lam_re, lam_im, log_step, b_re, b_im, da_re, da_im, dbb_re, dbb_im)


ROWS4 = Q_PER_KV * ATT_BLOCK


def _att_dist_mask(first_block):
    qi = lax.broadcasted_iota(jnp.int32, (ROWS4, 2 * ATT_BLOCK), 0) & (ATT_BLOCK - 1)
    si = lax.broadcasted_iota(jnp.int32, (ROWS4, 2 * ATT_BLOCK), 1)
    dist = qi + ATT_BLOCK - si
    valid = (dist >= 0) & (dist < ATT_BLOCK) & ((si >= ATT_BLOCK) | jnp.logical_not(first_block))
    return dist.astype(F32), valid


def _stack_heads(x, kv):
    return jnp.concatenate([x[:, (kv * Q_PER_KV + g) * HEAD_DIM:(kv * Q_PER_KV + g + 1) * HEAD_DIM]
                            for g in range(Q_PER_KV)], axis=0)


def _stack_cols(x, kv):
    return jnp.concatenate([x[:, kv * Q_PER_KV + g:kv * Q_PER_KV + g + 1] for g in range(Q_PER_KV)], axis=0)


def _per_head_col(vals):
    return jnp.concatenate([jnp.full((ATT_BLOCK, 1), v, F32) for v in vals], axis=0)


def _attn_forward(q, k, v, sinks, bl, nb):
    t = q.shape[0]

    def body(sink_ref, q_ref, kp_ref, kc_ref, vp_ref, vc_ref, o_ref, lse_ref):
        i = pl.program_id(1)
        dist4, valid4 = _att_dist_mask(i == 0)
        dist, valid = dist4[0:ATT_BLOCK, :], valid4[0:ATT_BLOCK, :]
        kk = jnp.concatenate([kp_ref[...], kc_ref[...]], axis=0)
        vv = jnp.concatenate([vp_ref[...], vc_ref[...]], axis=0)
        qv = q_ref[...]
        for h in range(N_HEADS):
            kv = h // Q_PER_KV
            slope = 2.0 ** (-(h + 1))
            qh = qv[:, h * HEAD_DIM:(h + 1) * HEAD_DIM]
            kh = kk[:, kv * HEAD_DIM:(kv + 1) * HEAD_DIM]
            vh = vv[:, kv * HEAD_DIM:(kv + 1) * HEAD_DIM]
            s = _mm_nt(qh, kh) * ATT_SCALE - slope * dist
            s = jnp.where(valid, s, NEG_BIG)
            sink = sink_ref[h]
            m = jnp.maximum(jnp.max(s, axis=-1, keepdims=True), sink)
            e = jnp.exp(s - m)
            den = jnp.sum(e, axis=-1, keepdims=True) + jnp.exp(sink - m)
            o_ref[:, h * HEAD_DIM:(h + 1) * HEAD_DIM] = _mm(e, vh) * (1.0 / den)
            lse_ref[:, h:h + 1] = m + jnp.log(den)

    cur = lambda w: pl.BlockSpec((ATT_BLOCK, w), lambda b, i: (b * nb + i, 0))
    prev = lambda w: pl.BlockSpec((ATT_BLOCK, w), lambda b, i: (b * nb + jnp.maximum(i - 1, 0), 0))
    return pl.pallas_call(
        body, name="attn_forward", grid=(bl, nb),
        in_specs=[pl.BlockSpec(memory_space=pltpu.SMEM), cur(512), prev(128), cur(128), prev(128), cur(128)],
        out_specs=(cur(512), cur(N_HEADS)),
        out_shape=(jax.ShapeDtypeStruct((t, D_ATTN), F32), jax.ShapeDtypeStruct((t, N_HEADS), F32)),
        compiler_params=_tc_params(("arbitrary", "arbitrary")),
    )(sinks, q, k, k, v, v)


def _attn_backward(q, k, v, o, do, lse, sinks, late16, late_own, bl, nb):
    t = q.shape[0]
    n = len(late16)
    steps = bl * nb
    mid1, mid2, last = steps // 4, (steps * 3) // 4, steps - 1

    def body(*refs):
        (sink_ref, qc_ref, qn_ref, kp_ref, kc_ref, vp_ref, vc_ref, oc_ref, on_ref, doc_ref, don_ref,
         lc_ref, ln_ref) = refs[:13]
        g16_r, go_r = refs[13:13 + n], refs[13 + n:13 + 2 * n]
        dq_ref, dk_ref, dv_ref, ds_ref = refs[13 + 2 * n:17 + 2 * n]
        red = refs[17 + 2 * n:17 + 3 * n]
        own16, recv1, send2, recv2 = (refs[17 + 3 * n + k * n:17 + 3 * n + (k + 1) * n] for k in range(4))
        s_send, s_recv, s_local = refs[17 + 7 * n:]
        b, i = pl.program_id(0), pl.program_id(1)
        step = b * nb + i
        x, y, c = _mesh_pos()
        sibling = (x, y, 1 - c)
        chips = [(1 - x, y), (x, 1 - y), (1 - x, 1 - y)]
        all_chips = [(x, y)] + chips

        def lvl1(a, j):
            return _remote(g16_r[a].at[_slot(*all_chips[j], 1 - c)], recv1[a].at[j], s_send, s_recv, 7 * a + j, sibling)

        def lvl2(a, j):
            return _remote(send2[a].at[j], recv2[a].at[j], s_send, s_recv, 7 * a + 4 + j, (*chips[j], c))

        def mine(a, j):
            return pltpu.make_async_copy(g16_r[a].at[_slot(*chips[j], c)], own16[a].at[j], s_local.at[3 * a + j])

        @pl.when(step == 0)
        def _():
            for a in range(n):
                for j in range(3):
                    mine(a, j).start()
                for j in range(4):
                    lvl1(a, j).start()

        @pl.when(step == mid1)
        def _():
            for a in range(n):
                for j in range(3):
                    mine(a, j).wait()
                for j in range(4):
                    lvl1(a, j).wait_recv()

                def partials(r, a=a):
                    red[a][r, :] = go_r[a][r, :] + recv1[a][0, r, :].astype(F32)
                    for j in range(3):
                        send2[a][j, r, :] = (own16[a][j, r, :].astype(F32)
                                             + recv1[a][1 + j, r, :].astype(F32)).astype(BF16)

                _row_chunks(go_r[a].shape[0], partials)
                for j in range(3):
                    lvl2(a, j).start()

        @pl.when(step == mid2)
        def _():
            for a in range(n):
                for j in range(3):
                    lvl2(a, j).wait_recv()

                def total(r, a=a):
                    g = red[a][r, :]
                    for j in range(3):
                        g = g + recv2[a][j, r, :].astype(F32)
                    red[a][r, :] = g

                _row_chunks(go_r[a].shape[0], total)

        @pl.when(step == last)
        def _():
            for a in range(n):
                for j in range(4):
                    lvl1(a, j).wait_send()
                for j in range(3):
                    lvl2(a, j).wait_send()

        dist, valid = _att_dist_mask(i == 0)
        has_next = i + 1 < nb
        dist_n = dist[:, 0:ATT_BLOCK]
        valid_n = (dist_n < ATT_BLOCK) & has_next
        kk = jnp.concatenate([kp_ref[...], kc_ref[...]], axis=0)
        vv = jnp.concatenate([vp_ref[...], vc_ref[...]], axis=0)
        qc, qn = qc_ref[...], qn_ref[...]
        oc, on = oc_ref[...], on_ref[...]
        doc, don = doc_ref[...], don_ref[...]
        lc, ln = lc_ref[...], ln_ref[...]
        dsink_cols = []
        for kv in range(KV_HEADS):
            heads = range(kv * Q_PER_KV, (kv + 1) * Q_PER_KV)
            kh = kk[:, kv * HEAD_DIM:(kv + 1) * HEAD_DIM]
            vh = vv[:, kv * HEAD_DIM:(kv + 1) * HEAD_DIM]
            khc, vhc = kh[ATT_BLOCK:, :], vh[ATT_BLOCK:, :]
            slope = _per_head_col([2.0 ** (-(h + 1)) for h in heads])
            sink = _per_head_col([sink_ref[h] for h in heads])
            q4, do4 = _stack_heads(qc, kv), _stack_heads(doc, kv)
            delta = jnp.sum(do4 * _stack_heads(oc, kv), axis=-1, keepdims=True)
            lse4 = _stack_cols(lc, kv)
            s = _mm_nt(q4, kh) * ATT_SCALE - slope * dist
            p = jnp.where(valid, jnp.exp(s - lse4), 0.0)
            dsc = p * (_mm_nt(do4, vh) - delta)
            dq4 = _mm(dsc, kh) * ATT_SCALE
            dk_acc = _mm_tn(dsc[:, ATT_BLOCK:], q4)
            dv_acc = _mm_tn(p[:, ATT_BLOCK:], do4)
            dsink4 = jnp.exp(sink - lse4) * delta
            q4n, do4n = _stack_heads(qn, kv), _stack_heads(don, kv)
            delta_n = jnp.sum(do4n * _stack_heads(on, kv), axis=-1, keepdims=True)
            s2 = _mm_nt(q4n, khc) * ATT_SCALE - slope * dist_n
            p2 = jnp.where(valid_n, jnp.exp(s2 - _stack_cols(ln, kv)), 0.0)
            ds2 = p2 * (_mm_nt(do4n, vhc) - delta_n)
            dk_acc += _mm_tn(ds2, q4n)
            dv_acc += _mm_tn(p2, do4n)
            dk_ref[:, kv * HEAD_DIM:(kv + 1) * HEAD_DIM] = dk_acc * ATT_SCALE
            dv_ref[:, kv * HEAD_DIM:(kv + 1) * HEAD_DIM] = dv_acc
            for g, h in enumerate(heads):
                rows = slice(g * ATT_BLOCK, (g + 1) * ATT_BLOCK)
                dq_ref[:, h * HEAD_DIM:(h + 1) * HEAD_DIM] = dq4[rows, :]
                dsink_cols.append(-jnp.sum(dsink4[rows, :], axis=0, keepdims=True))
        dsink = jnp.concatenate(dsink_cols, axis=1)

        @pl.when((b == 0) & (i == 0))
        def _():
            ds_ref[...] = dsink

        @pl.when((b != 0) | (i != 0))
        def _():
            ds_ref[...] += dsink

    cur = lambda w: pl.BlockSpec((ATT_BLOCK, w), lambda b, i: (b * nb + i, 0))
    prev = lambda w: pl.BlockSpec((ATT_BLOCK, w), lambda b, i: (b * nb + jnp.maximum(i - 1, 0), 0))
    nxt = lambda w: pl.BlockSpec((ATT_BLOCK, w), lambda b, i: (b * nb + jnp.minimum(i + 1, nb - 1), 0))
    const2 = lambda s: pl.BlockSpec(s, lambda b, i: (0, 0))
    shard = [s.shape for s in late_own]
    res = pl.pallas_call(
        body, name="attn_backward", grid=(bl, nb),
        in_specs=[pl.BlockSpec(memory_space=pltpu.SMEM), cur(512), nxt(512), prev(128), cur(128), prev(128), cur(128),
                  cur(512), nxt(512), cur(512), nxt(512), cur(N_HEADS), nxt(N_HEADS)]
        + [pl.BlockSpec(memory_space=pl.ANY)] * n + [const2(s) for s in shard],
        out_specs=(cur(512), cur(128), cur(128), const2((1, N_HEADS))) + tuple(const2(s) for s in shard),
        out_shape=(jax.ShapeDtypeStruct((t, D_ATTN), F32), jax.ShapeDtypeStruct((t, 128), F32),
                   jax.ShapeDtypeStruct((t, 128), F32), jax.ShapeDtypeStruct((1, N_HEADS), F32))
        + tuple(jax.ShapeDtypeStruct(s, F32) for s in shard),
        scratch_shapes=[pltpu.VMEM((3,) + s, BF16) for s in shard] + [pltpu.VMEM((4,) + s, BF16) for s in shard]
        + [pltpu.VMEM((3,) + s, BF16) for s in shard] + [pltpu.VMEM((3,) + s, BF16) for s in shard]
        + [pltpu.SemaphoreType.DMA((7 * n,)), pltpu.SemaphoreType.DMA((7 * n,)), pltpu.SemaphoreType.DMA((3 * n,))],
        compiler_params=_tc_params(("arbitrary", "arbitrary")),
    )(sinks, q, q, k, k, v, v, o, o, do, do, lse, lse, *late16, *late_own)
    return res[:4], list(res[4:])


def _mix_forward_backward(x2, y_perm, z_ssm, attn, z_attn, p2, target2, w_glu, b_glu, w_out, g_post, w_gate, b_gate,
                          w_proj, bl, seg):
    t = x2.shape[0]
    tm = seg

    def body(x_ref, y_ref, zs_ref, at_ref, za_ref, p_ref, tg_ref,
             wglu_ref, bglu_ref, wout_ref, gpost_ref, wgate_ref, bgate_ref, wproj_ref,
             loss_ref, dh1_ref, dy_ref, dzs_ref, dat_ref, dza_ref,
             dwglu_ref, dbglu_ref, dwout_ref, dgpost_ref, dwgate_ref, dbgate_ref, dwproj_ref):
        i = pl.program_id(0)
        y = y_ref[0]
        u3 = GELU_C * (y + GELU_K * y * y * y)
        th = jnp.tanh(u3)
        gl = 0.5 * y * (1.0 + th)
        a = _mm(gl, wglu_ref[...]) + bglu_ref[...]
        sa = _sigmoid(a)
        glu = gl * sa
        zs = zs_ref[...]
        sgs = _sigmoid(zs)
        ssm_out = glu * (zs * sgs)
        za = za_ref[...]
        sga = _sigmoid(za)
        at = at_ref[...]
        attn_out = at * (za * sga)
        cat = jnp.concatenate([ssm_out, attn_out], axis=-1).astype(BF16)
        mixed = _mm(cat, wout_ref[...])
        r2 = lax.rsqrt(jnp.mean(mixed * mixed, axis=-1, keepdims=True) + EPS)
        nhat = mixed * r2
        gpost = gpost_ref[...]
        h1 = x_ref[...] + nhat * gpost
        gate = _sigmoid(_mm(h1, wgate_ref[...]) + bgate_ref[...])
        pv = p_ref[...]
        pp = _mm(pv, wproj_ref[...])
        h2 = h1 + gate * pp
        err = h2 - tg_ref[...]
        loss_part = jnp.sum(jnp.sum(err * err, axis=-1, keepdims=True), axis=0, keepdims=True) * (0.5 / D_MODEL)
        dh2 = err * (1.0 / D_MODEL)
        dgp = dh2 * pp * gate * (1.0 - gate)
        dpp = dh2 * gate
        dh1 = dh2 + _mm_nt(dgp, wgate_ref[...])
        dh1_ref[...] = dh1
        dnhat = dh1 * gpost
        dmixed = r2 * (dnhat - nhat * jnp.mean(dnhat * nhat, axis=-1, keepdims=True))
        dcat = _mm_nt(dmixed, wout_ref[...])
        dso, dao = dcat[:, 0:D_SSM], dcat[:, D_SSM:]
        dat_ref[...] = dao * (za * sga)
        dza_ref[...] = dao * at * (sga * (1.0 + za * (1.0 - sga)))
        dzs_ref[...] = dso * glu * (sgs * (1.0 + zs * (1.0 - sgs)))
        dglu = dso * (zs * sgs)
        da = dglu * gl * sa * (1.0 - sa)
        dgl = dglu * sa + _mm_nt(da, wglu_ref[...])
        dgelu = 0.5 * (1.0 + th) + 0.5 * y * (1.0 - th * th) * (GELU_C * (1.0 + 3.0 * GELU_K * y * y))
        dy_ref[0] = dgl * dgelu
        parts = (
            (dwglu_ref, _mm_tn(gl, da)), (dbglu_ref, jnp.sum(da, axis=0, keepdims=True)),
            (dwout_ref, _mm_tn(cat, dmixed)), (dgpost_ref, jnp.sum(dh1 * nhat, axis=0, keepdims=True)),
            (dwgate_ref, _mm_tn(h1, dgp)), (dbgate_ref, jnp.sum(dgp, axis=0, keepdims=True)),
            (dwproj_ref, _mm_tn(pv, dpp)), (loss_ref, loss_part),
        )

        @pl.when(i == 0)
        def _():
            for ref, val in parts:
                ref[...] = val

        @pl.when(i != 0)
        def _():
            for ref, val in parts:
                ref[...] += val

    row = lambda w: pl.BlockSpec((tm, w), lambda i: (i, 0))
    perm = pl.BlockSpec((1, tm, D_SSM), lambda i: (i // N_SEG, 0, i % N_SEG))
    perm_shape = jax.ShapeDtypeStruct((bl, seg, N_SEG * D_SSM), F32)
    acc = lambda r, c: (_const_spec((r, c)), jax.ShapeDtypeStruct((r, c), F32))
    accs = [acc(D_SSM, D_SSM), acc(1, D_SSM), acc(D_MODEL, D_MODEL), acc(1, D_MODEL), acc(D_MODEL, D_MODEL),
            acc(1, D_MODEL), acc(D_PLE, D_MODEL)]
    return pl.pallas_call(
        body, name="mix_forward_backward", grid=(t // tm,),
        in_specs=[row(D_MODEL), perm, row(512), row(512), row(512), row(D_PLE), row(D_MODEL),
                  _const_spec((D_SSM, D_SSM)), _const_spec((1, D_SSM)), _const_spec((D_MODEL, D_MODEL)),
                  _const_spec((1, D_MODEL)), _const_spec((D_MODEL, D_MODEL)), _const_spec((1, D_MODEL)),
                  _const_spec((D_PLE, D_MODEL))],
        out_specs=(_const_spec((1, 1)), row(D_MODEL), perm, row(512), row(512), row(512)) + tuple(a[0] for a in accs),
        out_shape=(jax.ShapeDtypeStruct((1, 1), F32), jax.ShapeDtypeStruct((t, D_MODEL), F32), perm_shape,
                   jax.ShapeDtypeStruct((t, 512), F32), jax.ShapeDtypeStruct((t, 512), F32),
                   jax.ShapeDtypeStruct((t, 512), F32)) + tuple(a[1] for a in accs),
        compiler_params=_tc_params(("arbitrary",)),
    )(x2, y_perm, z_ssm, attn, z_attn, p2, target2, w_glu, b_glu, w_out, g_post, w_gate, b_gate, w_proj)


def _in_backward(x2, dh1, du_perm, dz_ssm, dq, dk, dv, dz_attn, g_pre, w_in, bl, seg):
    t = x2.shape[0]
    tm = seg

    def body(x_ref, dh1_ref, du_ref, dzs_ref, dq_ref, dk_ref, dv_ref, dza_ref, g_ref, w_ref,
             gx_ref, dw_ref, dg_ref):
        i = pl.program_id(0)
        xv = x_ref[...]
        r = lax.rsqrt(jnp.mean(xv * xv, axis=-1, keepdims=True) + EPS)
        xhat = xv * r
        g = g_ref[...]
        hn = (xhat * g).astype(BF16)
        dproj = jnp.concatenate([du_ref[0].astype(BF16), dzs_ref[...].astype(BF16), dq_ref[...].astype(BF16),
                                 dk_ref[...].astype(BF16), dv_ref[...].astype(BF16), dza_ref[...].astype(BF16)],
                                axis=-1)
        dhn = _mm(dproj, w_ref[...])
        dxhat = dhn * g
        gx_ref[...] = dh1_ref[...] + r * (dxhat - xhat * jnp.mean(dxhat * xhat, axis=-1, keepdims=True))
        dwp = _mm_tn(dproj, hn)
        dgp = jnp.sum(dhn * xhat, axis=0, keepdims=True)

        @pl.when(i == 0)
        def _():
            dw_ref[...] = dwp
            dg_ref[...] = dgp

        @pl.when(i != 0)
        def _():
            dw_ref[...] += dwp
            dg_ref[...] += dgp

    row = lambda w: pl.BlockSpec((tm, w), lambda i: (i, 0))
    perm = pl.BlockSpec((1, tm, D_SSM), lambda i: (i // N_SEG, 0, i % N_SEG))
    return pl.pallas_call(
        body, name="in_backward", grid=(t // tm,),
        in_specs=[row(D_MODEL), row(D_MODEL), perm, row(512), row(512), row(128), row(128), row(512),
                  _const_spec((1, D_MODEL)), _const_spec((D_IN, D_MODEL))],
        out_specs=(row(D_MODEL), _const_spec((D_IN, D_MODEL)), _const_spec((1, D_MODEL))),
        out_shape=(jax.ShapeDtypeStruct((t, D_MODEL), F32), jax.ShapeDtypeStruct((D_IN, D_MODEL), F32),
                   jax.ShapeDtypeStruct((1, D_MODEL), F32)),
        compiler_params=_tc_params(("arbitrary",)),
    )(x2, dh1, du_perm, dz_ssm, dq, dk, dv, dz_attn, g_pre, w_in)


def _block_diag(t):
    a, b = t.shape[1], t.shape[2]
    eye = jnp.eye(G_TILE, dtype=t.dtype)
    t = t.reshape(N_GT, G_TILE, a, 1, b) * eye[None, :, None, :, None]
    return t.reshape(N_GT, G_TILE * a, G_TILE * b)


def _diag_blocks(m, a, b):
    m = m.reshape(N_GT, G_TILE, a, G_TILE, b)
    return jnp.einsum("tgagb->tgab", m).reshape(SSM_G, a, b)


def _local_step(x, p, target, pre_norm_g, w_in, ssm_lam_re, ssm_lam_im, ssm_log_step, ssm_b_re, ssm_b_im, ssm_c_re,
                ssm_c_im, ssm_d, ssm_b_glu, attn_sinks, post_norm_g, pl_b_gate, late, me):
    bl, seq, _ = x.shape
    seg = seq // N_SEG
    nb = seq // ATT_BLOCK
    t = bl * seq
    x2 = x.reshape(t, D_MODEL)
    p2 = p.reshape(t, D_PLE)
    tg2 = target.reshape(t, D_MODEL)

    lam_re, lam_im = ssm_lam_re, ssm_lam_im
    log_step = ssm_log_step.reshape(SSM_G, 1)
    a_re_row, a_im_row, bb_re, bb_im, pw_re, pw_im = _ssm_prep(lam_re, lam_im, log_step, ssm_b_re, ssm_b_im, seg)
    by_group = lambda t: t.reshape(SSM_G, SSM_P, SSM_N)
    bcat = jnp.concatenate([_block_diag(by_group(bb_re)), _block_diag(by_group(bb_im))], axis=-1).astype(BF16)
    ccat_t = jnp.concatenate([_block_diag(by_group(ssm_c_re)), -_block_diag(by_group(ssm_c_im))],
                             axis=-1).astype(BF16)
    bcat_t = jnp.swapaxes(bcat, 1, 2)
    ccat = jnp.swapaxes(ccat_t, 1, 2)
    d_row = ssm_d.reshape(1, D_SSM)

    (u_perm, z_ssm, q, k, v, z_attn), gathered = _in_proj(x2, pre_norm_g.reshape(1, D_MODEL), w_in, late, bl, seg)
    w_out, w_gate, w_proj, w_glu = (_gathered_to_full(n, g) for n, g in zip(LATE_NAMES, gathered))
    u_perm = u_perm.reshape(bl, seq, D_SSM)
    y_perm = _ssm_forward(u_perm, bcat, ccat, a_re_row, a_im_row, pw_re, pw_im, d_row, seg)
    sinks = attn_sinks.reshape(N_HEADS)
    attn, lse = _attn_forward(q, k, v, sinks, bl, nb)
    (loss, dh1, dy_perm, dz_ssm, dattn, dz_attn, d_w_glu, d_b_glu, d_w_out, d_g_post, d_w_gate, d_b_gate,
     d_w_proj) = _mix_forward_backward(
        x2, y_perm.reshape(bl, seg, N_SEG * D_SSM), z_ssm, attn, z_attn, p2, tg2, w_glu,
        ssm_b_glu.reshape(1, D_SSM), w_out, post_norm_g.reshape(1, D_MODEL), w_gate, pl_b_gate.reshape(1, D_MODEL),
        w_proj, bl, seg)
    owned = [_full_to_owned(n, d) for n, d in zip(LATE_NAMES, (d_w_out, d_w_gate, d_w_proj, d_w_glu))]
    (dq, dk, dv, d_sinks), late_grads = _attn_backward(
        q, k, v, attn, dattn, lse, sinks, [o.astype(BF16) for o in owned],
        [lax.dynamic_index_in_dim(o, me, axis=0, keepdims=False) for o in owned], bl, nb)
    du_perm, d_bcat, d_ccat_t, da_re, da_im, d_d = _ssm_backward(
        u_perm, dy_perm.reshape(bl, seq, D_SSM), bcat, bcat_t, ccat_t, a_re_row, a_im_row, pw_re, pw_im, d_row, seg)
    grad_x, d_w_in, d_g_pre = _in_backward(
        x2, dh1, du_perm.reshape(bl, seg, N_SEG * D_SSM), dz_ssm, dq, dk, dv, dz_attn,
        pre_norm_g.reshape(1, D_MODEL), w_in, bl, seg)
    flat = lambda t: t.reshape(SSM_G * SSM_P, SSM_N)
    d_lam_re, d_lam_im, d_ls, d_b_re, d_b_im = _ssm_param_grads(
        lam_re, lam_im, log_step, ssm_b_re, ssm_b_im, da_re.reshape(SSM_G, SSM_N), da_im.reshape(SSM_G, SSM_N),
        flat(_diag_blocks(d_bcat[:, :, 0:ST_T], SSM_P, SSM_N)), flat(_diag_blocks(d_bcat[:, :, ST_T:], SSM_P, SSM_N)))
    grads = {
        "pre_norm_g": d_g_pre, "w_in": d_w_in, "ssm_lam_re": d_lam_re, "ssm_lam_im": d_lam_im,
        "ssm_log_step": d_ls, "ssm_b_re": d_b_re, "ssm_b_im": d_b_im,
        "ssm_c_re": _diag_blocks(d_ccat_t[:, :, 0:ST_T], SSM_P, SSM_N),
        "ssm_c_im": -_diag_blocks(d_ccat_t[:, :, ST_T:], SSM_P, SSM_N),
        "ssm_d": d_d, "ssm_b_glu": d_b_glu, "attn_sinks": d_sinks, "post_norm_g": d_g_post, "pl_b_gate": d_b_gate,
    }
    return loss, grad_x.reshape(bl, seq, D_MODEL), grads, late_grads


LATE_NAMES = ("w_out", "pl_w_gate", "pl_w_proj", "ssm_w_glu")
BIG_NAMES = ("w_in",) + LATE_NAMES
COL_SHARDED = {"w_in": D_IN // N_DEV, "pl_w_proj": D_MODEL // N_DEV}
WEIGHT_NAMES = ("pre_norm_g", "w_in", "ssm_lam_re", "ssm_lam_im", "ssm_log_step", "ssm_b_re", "ssm_b_im", "ssm_c_re",
                "ssm_c_im", "ssm_d", "ssm_w_glu", "ssm_b_glu", "attn_sinks", "w_out", "post_norm_g", "pl_w_proj",
                "pl_w_gate", "pl_b_gate")


TRANSPOSED = {"w_in": (0, 1), "ssm_b_re": (1, 2), "ssm_b_im": (1, 2)}


def _kernel_form(name, a):
    a = a[0]
    if name in TRANSPOSED:
        a = jnp.swapaxes(a, *TRANSPOSED[name])
    if name in ("ssm_b_re", "ssm_b_im", "ssm_c_re", "ssm_c_im"):
        a = a.reshape(SSM_G * SSM_P, SSM_N)
    return a


def _given_form(name, a, shape):
    if name in TRANSPOSED:
        i, j = TRANSPOSED[name]
        swapped = list(shape[1:])
        swapped[i], swapped[j] = swapped[j], swapped[i]
        return jnp.swapaxes(a.reshape(swapped), i, j).reshape(shape)
    return a.reshape(shape)


def _gathered_to_full(name, g):
    _, rows, cols = g.shape
    if name in COL_SHARDED:
        return jnp.swapaxes(g, 0, 1).reshape(rows, N_DEV * cols)
    return g.reshape(N_DEV * rows, cols)


def _full_to_owned(name, full):
    if name in COL_SHARDED:
        return jnp.swapaxes(full.reshape(full.shape[0], N_DEV, COL_SHARDED[name]), 0, 1)
    return full.reshape(N_DEV, full.shape[0] // N_DEV, full.shape[1])


def kernel(x, p, pre_norm_g, w_in, ssm_lam_re, ssm_lam_im, ssm_log_step, ssm_b_re, ssm_b_im, ssm_c_re, ssm_c_im, ssm_d, ssm_w_glu, ssm_b_glu, attn_sinks, w_out, post_norm_g, pl_w_proj, pl_w_gate, pl_b_gate, loss_target, m_pre_norm_g, m_w_in, m_ssm_lam_re, m_ssm_lam_im, m_ssm_log_step, m_ssm_b_re, m_ssm_b_im, m_ssm_c_re, m_ssm_c_im, m_ssm_d, m_ssm_w_glu, m_ssm_b_glu, m_attn_sinks, m_w_out, m_post_norm_g, m_pl_w_proj, m_pl_w_gate, m_pl_b_gate, v_pre_norm_g, v_w_in, v_ssm_lam_re, v_ssm_lam_im, v_ssm_log_step, v_ssm_b_re, v_ssm_b_im, v_ssm_c_re, v_ssm_c_im, v_ssm_d, v_ssm_w_glu, v_ssm_b_glu, v_attn_sinks, v_w_out, v_post_norm_g, v_pl_w_proj, v_pl_w_gate, v_pl_b_gate):
    w = dict(pre_norm_g=pre_norm_g, w_in=w_in, ssm_lam_re=ssm_lam_re, ssm_lam_im=ssm_lam_im, ssm_log_step=ssm_log_step,
             ssm_b_re=ssm_b_re, ssm_b_im=ssm_b_im, ssm_c_re=ssm_c_re, ssm_c_im=ssm_c_im, ssm_d=ssm_d, ssm_w_glu=ssm_w_glu,
             ssm_b_glu=ssm_b_glu, attn_sinks=attn_sinks, w_out=w_out, post_norm_g=post_norm_g, pl_w_proj=pl_w_proj,
             pl_w_gate=pl_w_gate, pl_b_gate=pl_b_gate)
    m = dict(pre_norm_g=m_pre_norm_g, w_in=m_w_in, ssm_lam_re=m_ssm_lam_re, ssm_lam_im=m_ssm_lam_im,
             ssm_log_step=m_ssm_log_step, ssm_b_re=m_ssm_b_re, ssm_b_im=m_ssm_b_im, ssm_c_re=m_ssm_c_re,
             ssm_c_im=m_ssm_c_im, ssm_d=m_ssm_d, ssm_w_glu=m_ssm_w_glu, ssm_b_glu=m_ssm_b_glu, attn_sinks=m_attn_sinks,
             w_out=m_w_out, post_norm_g=m_post_norm_g, pl_w_proj=m_pl_w_proj, pl_w_gate=m_pl_w_gate,
             pl_b_gate=m_pl_b_gate)
    v = dict(pre_norm_g=v_pre_norm_g, w_in=v_w_in, ssm_lam_re=v_ssm_lam_re, ssm_lam_im=v_ssm_lam_im,
             ssm_log_step=v_ssm_log_step, ssm_b_re=v_ssm_b_re, ssm_b_im=v_ssm_b_im, ssm_c_re=v_ssm_c_re,
             ssm_c_im=v_ssm_c_im, ssm_d=v_ssm_d, ssm_w_glu=v_ssm_w_glu, ssm_b_glu=v_ssm_b_glu, attn_sinks=v_attn_sinks,
             w_out=v_w_out, post_norm_g=v_post_norm_g, pl_w_proj=v_pl_w_proj, pl_w_gate=v_pl_w_gate,
             pl_b_gate=v_pl_b_gate)
    me = _slot(lax.axis_index("x"), lax.axis_index("y"), lax.axis_index("c"))
    kf = lambda d: {n: _kernel_form(n, a) for n, a in d.items()}
    wk, mk, vk = kf(w), kf(m), kf(v)

    (gathered,) = _allgather_weights([wk["w_in"]])
    loss, grad_x, grads, g_late = _local_step(
        x, p[0], loss_target, wk["pre_norm_g"], gathered.reshape(D_IN, D_MODEL), wk["ssm_lam_re"], wk["ssm_lam_im"],
        wk["ssm_log_step"], wk["ssm_b_re"], wk["ssm_b_im"], wk["ssm_c_re"], wk["ssm_c_im"], wk["ssm_d"],
        wk["ssm_b_glu"], wk["attn_sinks"], wk["post_norm_g"], wk["pl_b_gate"], [wk[n] for n in LATE_NAMES], me)

    owned = grads["w_in"].reshape(N_DEV, D_IN // N_DEV, D_MODEL)
    tiny_form = lambda d: [d[n].reshape(rows, cols) for n, rows, cols in TINY]
    med_form = lambda d: [d[n].reshape(N_DEV, rows // N_DEV, cols) for n, rows, cols in MEDIUM]
    g_big, loss, g_tiny, g_med = _reduce_final(
        [owned.astype(BF16)], [lax.dynamic_index_in_dim(owned, me, axis=0, keepdims=False)],
        loss, tiny_form(grads), med_form(grads))
    names = BIG_NAMES + tuple(n for n, _, _ in TINY + MEDIUM)
    form = lambda d: [d[n] for n in BIG_NAMES] + tiny_form(d) + med_form(d)
    updated = _adamw_update(g_big + g_late + g_tiny + g_med, form(wk), form(mk), form(vk))
    vals = dict(zip(names, updated))
    results = [[_given_form(n, vals[n][kind], w[n].shape) for n in WEIGHT_NAMES] for kind in range(4)]
    return (loss.reshape(()), grad_x, *results[0], *results[1], *results[2], *results[3])
```

```python
import functools
import math

import jax
import jax.numpy as jnp
from jax import lax
from jax.experimental import pallas as pl
from jax.experimental.pallas import tpu as pltpu

F32 = jnp.float32
BF16 = jnp.bfloat16

D_MODEL = 1024
D_SSM = 512
D_ATTN = 512
SSM_P = 16
SSM_G = 32
SSM_N = 64
N_HEADS = 8
KV_HEADS = 2
Q_PER_KV = 4
HEAD_DIM = 64
ATT_BLOCK = 128
D_PLE = 256
D_IN = 2304
EPS = 1e-6
N_DEV = 8
N_SEG = 8
G_TILE = 8
N_GT = SSM_G // G_TILE
CH_T = G_TILE * SSM_P
ST_T = G_TILE * SSM_N
N_STATE = SSM_G * SSM_N
SCAN_UNROLL = 4
LANES = 128
VMEM_LIMIT = 60 * 1024 * 1024

ADAM_LR = 0.001
ADAM_B1 = 0.9
ADAM_B2 = 0.999
ADAM_EPS = 1e-08
ADAM_WD = 0.01
ADAM_STEP = 10

GELU_C = math.sqrt(2.0 / math.pi)
GELU_K = 0.044715
ATT_SCALE = 1.0 / math.sqrt(HEAD_DIM)
NEG_BIG = -1e30


def _mm(a, b):
    return jnp.dot(a.astype(BF16), b.astype(BF16), preferred_element_type=F32)


def _mm_nt(a, b):
    return lax.dot_general(a.astype(BF16), b.astype(BF16), (((1,), (1,)), ((), ())), preferred_element_type=F32)


def _mm_tn(a, b):
    return lax.dot_general(a.astype(BF16), b.astype(BF16), (((0,), (0,)), ((), ())), preferred_element_type=F32)


def _sigmoid(x):
    return 1.0 / (1.0 + jnp.exp(-x))


def _tc_params(sem):
    return pltpu.CompilerParams(dimension_semantics=sem, vmem_limit_bytes=VMEM_LIMIT)


def _const_spec(shape):
    nd = len(shape)
    return pl.BlockSpec(shape, lambda *_: (0,) * nd)


def _mesh_pos():
    return lax.axis_index("x"), lax.axis_index("y"), lax.axis_index("c")


ROW_CHUNKS = (64, 32, 16)


def _row_chunk(nrows):
    return next((c for c in ROW_CHUNKS if nrows % c == 0), None)


def _row_chunks(nrows, fn):
    chunk = _row_chunk(nrows)

    def step(i, carry):
        fn(pl.ds(pl.multiple_of(i * chunk, chunk), chunk))
        return carry

    lax.fori_loop(0, nrows // chunk, step, 0)


def _slot(px, py, pc):
    return 4 * px + 2 * py + pc


def _allgather_weights(shards):
    n = len(shards)

    def body(*refs):
        srcs, outs, (send_sems, recv_sems) = refs[:n], refs[n:2 * n], refs[2 * n:]
        x, y, c = _mesh_pos()
        me, sibling = (x, y, c), (x, y, 1 - c)
        chips = [(1 - x, y), (x, 1 - y), (1 - x, 1 - y)]

        def copy(a, k, block, to):
            blk = outs[a].at[_slot(*block)]
            return pltpu.make_async_remote_copy(
                src_ref=blk, dst_ref=blk, send_sem=send_sems.at[7 * a + k], recv_sem=recv_sems.at[7 * a + k],
                device_id=to, device_id_type=pl.DeviceIdType.MESH)

        sends = []
        for a in range(n):
            mine = outs[a].at[_slot(*me)]

            def cast(r, mine=mine, src=srcs[a]):
                mine[r, :] = src[r, :].astype(BF16)

            _row_chunks(srcs[a].shape[0], cast)
            first = [copy(a, 0, me, sibling)] + [copy(a, 1 + j, me, (*chip, c)) for j, chip in enumerate(chips)]
            for cp in first:
                cp.start()
            sends += first
        for a in range(n):
            for j, chip in enumerate(chips):
                copy(a, 1 + j, (*chip, c), me).wait_recv()
                fwd = copy(a, 4 + j, (*chip, c), sibling)
                fwd.start()
                sends.append(fwd)
        for a in range(n):
            copy(a, 0, sibling, me).wait_recv()
            for j, chip in enumerate(chips):
                copy(a, 4 + j, (*chip, 1 - c), me).wait_recv()
        for cp in sends:
            cp.wait_send()

    vm = pl.BlockSpec(memory_space=pltpu.VMEM)
    return pl.pallas_call(
        body, name="allgather_weights",
        out_shape=tuple(jax.ShapeDtypeStruct((N_DEV,) + s.shape, BF16) for s in shards),
        in_specs=[vm] * n, out_specs=(vm,) * n,
        scratch_shapes=[pltpu.SemaphoreType.DMA((7 * n,)), pltpu.SemaphoreType.DMA((7 * n,))],
        compiler_params=pltpu.CompilerParams(vmem_limit_bytes=VMEM_LIMIT),
    )(*shards)


def _adamw(w, g, m, v):
    m = ADAM_B1 * m + (1.0 - ADAM_B1) * g
    v = ADAM_B2 * v + (1.0 - ADAM_B2) * (g * g)
    m_hat = m / (1.0 - ADAM_B1 ** ADAM_STEP)
    v_hat = v / (1.0 - ADAM_B2 ** ADAM_STEP)
    delta = -ADAM_LR * (m_hat / (jnp.sqrt(v_hat) + ADAM_EPS) + ADAM_WD * w)
    return delta, m, v


def _remote(src, dst, send_sems, recv_sems, k, to):
    return pltpu.make_async_remote_copy(src_ref=src, dst_ref=dst, send_sem=send_sems.at[k], recv_sem=recv_sems.at[k],
                                        device_id=to, device_id_type=pl.DeviceIdType.MESH)


def _big_reduce_phases(g16_r, go_r, outs, send2, recv1, recv2, s_send, s_recv):
    n = len(g16_r)
    x, y, c = _mesh_pos()
    sibling = (x, y, 1 - c)
    chips = [(1 - x, y), (x, 1 - y), (1 - x, 1 - y)]
    all_chips = [(x, y)] + chips
    lvl1 = []
    for a in range(n):
        cps = [_remote(g16_r[a].at[_slot(*chip, 1 - c)], recv1[a].at[j], s_send, s_recv, 7 * a + j, sibling)
               for j, chip in enumerate(all_chips)]
        for cp in cps:
            cp.start()
        lvl1.append(cps)
    yield
    lvl2 = []
    for a in range(n):
        for cp in lvl1[a]:
            cp.wait_recv()
        og = outs[a]

        def partials(r, a=a, og=og):
            og[r, :] = go_r[a][r, :] + recv1[a][0, r, :].astype(F32)
            for j, chip in enumerate(chips):
                mine16 = g16_r[a][_slot(*chip, c), r, :].astype(F32)
                send2[a][j, r, :] = (mine16 + recv1[a][1 + j, r, :].astype(F32)).astype(BF16)

        _row_chunks(go_r[a].shape[0], partials)
        cps = [_remote(send2[a].at[j], recv2[a].at[j], s_send, s_recv, 7 * a + 4 + j, (*chip, c))
               for j, chip in enumerate(chips)]
        for cp in cps:
            cp.start()
        lvl2.append(cps)
    yield
    for a in range(n):
        for cp in lvl2[a]:
            cp.wait_recv()
        og = outs[a]

        def total(r, a=a, og=og):
            g = og[r, :]
            for j in range(3):
                g = g + recv2[a][j, r, :].astype(F32)
            og[r, :] = g

        _row_chunks(go_r[a].shape[0], total)
    yield
    for cps in lvl1 + lvl2:
        for cp in cps:
            cp.wait_send()


def _interleave(*phases):
    active = list(phases)
    while active:
        for g in list(active):
            try:
                next(g)
            except StopIteration:
                active.remove(g)


def _adamw_update(g, w, m, v):
    n = len(g)

    def body(*refs):
        g_r, w_r, m_r, v_r = (refs[i * n:(i + 1) * n] for i in range(4))
        outs = refs[4 * n:]
        for a in range(n):
            og, od, om, ov = outs[4 * a:4 * a + 4]

            def update(idx, a=a, og=og, od=od, om=om, ov=ov):
                gv = g_r[a][idx]
                d, nm, nv = _adamw(w_r[a][idx], gv, m_r[a][idx], v_r[a][idx])
                og[idx] = gv
                od[idx] = d
                om[idx] = nm
                ov[idx] = nv

            shape = g_r[a].shape
            if len(shape) == 3:
                for b in range(shape[0]):
                    update(b)
            elif _row_chunk(shape[0]) is not None:
                _row_chunks(shape[0], update)
            else:
                update(Ellipsis)

    vm = pl.BlockSpec(memory_space=pltpu.VMEM)
    res = pl.pallas_call(
        body, name="adamw_update",
        out_shape=tuple(jax.ShapeDtypeStruct(t.shape, F32) for t in g for _ in range(4)),
        in_specs=[vm] * (4 * n), out_specs=(vm,) * (4 * n),
        compiler_params=pltpu.CompilerParams(vmem_limit_bytes=VMEM_LIMIT),
    )(*g, *w, *m, *v)
    return [res[4 * a:4 * a + 4] for a in range(n)]


TINY = (("pre_norm_g", 1, 1024), ("post_norm_g", 1, 1024), ("pl_b_gate", 1, 1024), ("ssm_d", 1, 512),
        ("ssm_b_glu", 1, 512), ("ssm_log_step", 1, 32), ("attn_sinks", 1, 8), ("ssm_lam_re", 32, 64),
        ("ssm_lam_im", 32, 64))
MEDIUM = (("ssm_b_re", SSM_G * SSM_P, SSM_N), ("ssm_b_im", SSM_G * SSM_P, SSM_N), ("ssm_c_re", SSM_G * SSM_P, SSM_N),
          ("ssm_c_im", SSM_G * SSM_P, SSM_N))


def _stage_rows():
    offs, r = {}, 0
    for name, rows, cols in TINY + (("loss", 1, 1),):
        if rows > 1:
            r = -(-r // 8) * 8
        offs[name] = r
        r += rows if rows > 1 else max(cols // LANES, 1)
    return offs, -(-r // 8) * 8


def _reduce_final(g16, gown, loss, g_tiny, g_med):
    nb_, nt, nm_ = len(g16), len(TINY), len(MEDIUM)
    offs, stage_rows = _stage_rows()

    def body(*refs):
        g16_r, go_r = refs[:nb_], refs[nb_:2 * nb_]
        base = 2 * nb_
        loss_r, gt, gm = refs[base], refs[base + 1:base + 1 + nt], refs[base + 1 + nt:base + 1 + nt + nm_]
        base += 1 + nt + nm_
        out_b = refs[base:base + nb_]
        base += nb_
        loss_o, out_t, out_m = refs[base], refs[base + 1:base + 1 + nt], refs[base + 1 + nt:base + 1 + nt + nm_]
        base += 1 + nt + nm_
        send2_b, recv1_b, recv2_b = (refs[base + i * nb_:base + (i + 1) * nb_] for i in range(3))
        base += 3 * nb_
        stage = refs[base]
        recv1, part, recv2 = (refs[base + 1 + i * nm_:base + 1 + (i + 1) * nm_] for i in range(3))
        bs_send, bs_recv, s_send, s_recv = refs[base + 1 + 3 * nm_:]
        _interleave(_big_reduce_phases(g16_r, go_r, out_b, send2_b, recv1_b, recv2_b, bs_send, bs_recv),
                    small_phases(loss_r, gt, gm, loss_o, out_t, out_m, stage, recv1, part, recv2, s_send, s_recv))

    def small_phases(loss_r, gt, gm, loss_o, out_t, out_m, stage, recv1, part, recv2, s_send, s_recv):
        x, y, c = _mesh_pos()
        me = _slot(x, y, c)
        sibling = (x, y, 1 - c)
        chips = [(1 - x, y), (x, 1 - y), (1 - x, 1 - y)]
        all_chips = [(x, y)] + chips
        peers = [sibling] + [(*chip, c) for chip in chips] + [(*chip, 1 - c) for chip in chips]
        sem = iter(range(7 + 14 * nm_))
        lvl1 = []
        for a in range(nm_):
            cps = [_remote(gm[a].at[_slot(*chip, 1 - c)], recv1[a].at[j], s_send, s_recv, next(sem), sibling)
                   for j, chip in enumerate(all_chips)]
            for cp in cps:
                cp.start()
            lvl1.append(cps)
        mine = stage.at[me]
        mine[...] = jnp.zeros((stage_rows, LANES), F32)
        for (name, rows, cols), ref in zip(TINY + (("loss", 1, 1),), gt + (loss_r,)):
            r0 = offs[name]
            if rows > 1:
                mine[r0:r0 + rows, 0:cols] = ref[...]
            elif cols >= LANES:
                for i in range(cols // LANES):
                    mine[r0 + i:r0 + i + 1, :] = ref[:, i * LANES:(i + 1) * LANES]
            else:
                mine[r0:r0 + 1, 0:cols] = ref[...]
        tiny_cps = [_remote(mine, mine, s_send, s_recv, next(sem), peer) for peer in peers]
        for cp in tiny_cps:
            cp.start()
        yield
        lvl2 = []
        for a in range(nm_):
            for cp in lvl1[a]:
                cp.wait_recv()
            for j, chip in enumerate(all_chips):
                part[a][j] = gm[a][_slot(*chip, c)] + recv1[a][j]
            cps = [_remote(part[a].at[1 + j], recv2[a].at[j], s_send, s_recv, next(sem), (*chip, c))
                   for j, chip in enumerate(chips)]
            for cp in cps:
                cp.start()
            lvl2.append(cps)
        yield
        lvl3 = []
        for a in range(nm_):
            for cp in lvl2[a]:
                cp.wait_recv()
            blk = out_m[a].at[me]
            blk[...] = ((part[a][0] + recv2[a][0]) + recv2[a][1]) + recv2[a][2]
            cps = [_remote(blk, blk, s_send, s_recv, next(sem), peer) for peer in peers]
            for cp in cps:
                cp.start()
            lvl3.append(cps)
        yield
        for cp in tiny_cps:
            cp.wait_recv()
        tot = stage[0]
        for d in range(1, N_DEV):
            tot = tot + stage[d]
        loss_o[...] = tot[offs["loss"]:offs["loss"] + 1, 0:1]
        for k, (name, rows, cols) in enumerate(TINY):
            r0 = offs[name]
            if rows > 1:
                out_t[k][...] = tot[r0:r0 + rows, 0:cols]
            elif cols >= LANES:
                for i in range(cols // LANES):
                    out_t[k][:, i * LANES:(i + 1) * LANES] = tot[r0 + i:r0 + i + 1, :]
            else:
                out_t[k][...] = tot[r0:r0 + 1, 0:cols]
        for cps in lvl3:
            for cp in cps:
                cp.wait_recv()
        for cps in lvl1 + lvl2 + lvl3 + [tiny_cps]:
            for cp in cps:
                cp.wait_send()

    vmem = pl.BlockSpec(memory_space=pltpu.VMEM)
    t_shapes = [jax.ShapeDtypeStruct((rows, cols), F32) for _, rows, cols in TINY]
    m_shapes = [jax.ShapeDtypeStruct((N_DEV, rows // N_DEV, cols), F32) for _, rows, cols in MEDIUM]
    blk = [(rows // N_DEV, cols) for _, rows, cols in MEDIUM]
    scratch = ([pltpu.VMEM((3,) + t.shape, BF16) for t in gown] + [pltpu.VMEM((4,) + t.shape, BF16) for t in gown]
               + [pltpu.VMEM((3,) + t.shape, BF16) for t in gown]
               + [pltpu.VMEM((N_DEV, stage_rows, LANES), F32)]
               + [pltpu.VMEM((4,) + b, F32) for b in blk] + [pltpu.VMEM((4,) + b, F32) for b in blk]
               + [pltpu.VMEM((3,) + b, F32) for b in blk]
               + [pltpu.SemaphoreType.DMA((7 * nb_,)), pltpu.SemaphoreType.DMA((7 * nb_,)),
                  pltpu.SemaphoreType.DMA((7 + 14 * nm_,)), pltpu.SemaphoreType.DMA((7 + 14 * nm_,))])
    n_in, n_out = 2 * nb_ + 1 + nt + nm_, nb_ + 1 + nt + nm_
    res = pl.pallas_call(
        body, name="reduce_final",
        out_shape=tuple(jax.ShapeDtypeStruct(t.shape, F32) for t in gown) + (jax.ShapeDtypeStruct((1, 1), F32),)
        + tuple(t_shapes) + tuple(m_shapes),
        in_specs=[vmem] * n_in, out_specs=(vmem,) * n_out, scratch_shapes=scratch,
        compiler_params=pltpu.CompilerParams(vmem_limit_bytes=VMEM_LIMIT),
    )(*g16, *gown, loss, *g_tiny, *g_med)
    return list(res[:nb_]), res[nb_], list(res[nb_ + 1:nb_ + 1 + nt]), list(res[nb_ + 1 + nt:])


def _in_proj(x2, g_pre, w_in, late, bl, seg):
    t = x2.shape[0]
    tm = seg
    steps = t // tm
    n = len(late)
    forward_step, last_step = (steps * 5) // 8, steps - 1

    def body(*refs):
        x_ref, g_ref, w_ref = refs[:3]
        late_r = refs[3:3 + n]
        u_ref, zs_ref, q_ref, k_ref, v_ref, za_ref = refs[3 + n:9 + n]
        gath = refs[9 + n:9 + 2 * n]
        cast = refs[9 + 2 * n:9 + 3 * n]
        send_sems, recv_sems, local_sems = refs[9 + 3 * n:]
        i = pl.program_id(0)
        x, y, c = _mesh_pos()
        me, sibling = (x, y, c), (x, y, 1 - c)
        chips = [(1 - x, y), (x, 1 - y), (1 - x, 1 - y)]

        def own(a, k, to):
            return _remote(cast[a], gath[a].at[_slot(*me)], send_sems, recv_sems, 7 * a + k, to)

        def relay(a, k, block, to):
            blk = gath[a].at[_slot(*block)]
            return _remote(blk, blk, send_sems, recv_sems, 7 * a + k, to)

        def keep(a):
            return pltpu.make_async_copy(cast[a], gath[a].at[_slot(*me)], local_sems.at[a])

        @pl.when(i == 0)
        def _():
            for a in range(n):
                def to16(r, a=a):
                    cast[a][r, :] = late_r[a][r, :].astype(BF16)

                _row_chunks(late_r[a].shape[0], to16)
                keep(a).start()
                own(a, 0, sibling).start()
                for j, chip in enumerate(chips):
                    own(a, 1 + j, (*chip, c)).start()

        xv = x_ref[...]
        r = lax.rsqrt(jnp.mean(xv * xv, axis=-1, keepdims=True) + EPS)
        hn = xv * r * g_ref[...]
        proj = _mm_nt(hn, w_ref[...])
        u_ref[0] = proj[:, 0:512]
        zs_ref[...] = proj[:, 512:1024]
        q_ref[...] = proj[:, 1024:1536].astype(BF16)
        k_ref[...] = proj[:, 1536:1664].astype(BF16)
        v_ref[...] = proj[:, 1664:1792].astype(BF16)
        za_ref[...] = proj[:, 1792:2304]

        @pl.when(i == forward_step)
        def _():
            for a in range(n):
                for j, chip in enumerate(chips):
                    relay(a, 1 + j, (*chip, c), me).wait_recv()
                    relay(a, 4 + j, (*chip, c), sibling).start()

        @pl.when(i == last_step)
        def _():
            for a in range(n):
                relay(a, 0, sibling, me).wait_recv()
                for j, chip in enumerate(chips):
                    relay(a, 4 + j, (*chip, 1 - c), me).wait_recv()
                own(a, 0, sibling).wait_send()
                for j, chip in enumerate(chips):
                    own(a, 1 + j, (*chip, c)).wait_send()
                    relay(a, 4 + j, (*chip, c), sibling).wait_send()
                keep(a).wait()

    row = lambda w: pl.BlockSpec((tm, w), lambda i: (i, 0))
    hbm = pl.BlockSpec(memory_space=pl.ANY)
    res = pl.pallas_call(
        body, name="in_proj", grid=(steps,),
        in_specs=[row(D_MODEL), _const_spec((1, D_MODEL)), _const_spec((D_IN, D_MODEL))]
        + [_const_spec(s.shape) for s in late],
        out_specs=(pl.BlockSpec((1, tm, D_SSM), lambda i: (i // N_SEG, 0, i % N_SEG)),
                   row(512), row(512), row(128), row(128), row(512)) + (hbm,) * n,
        out_shape=(jax.ShapeDtypeStruct((bl, seg, N_SEG * D_SSM), F32),
                   jax.ShapeDtypeStruct((t, 512), F32), jax.ShapeDtypeStruct((t, 512), BF16),
                   jax.ShapeDtypeStruct((t, 128), BF16), jax.ShapeDtypeStruct((t, 128), BF16),
                   jax.ShapeDtypeStruct((t, 512), F32))
        + tuple(jax.ShapeDtypeStruct((N_DEV,) + s.shape, BF16) for s in late),
        scratch_shapes=[pltpu.VMEM(s.shape, BF16) for s in late]
        + [pltpu.SemaphoreType.DMA((7 * n,)), pltpu.SemaphoreType.DMA((7 * n,)), pltpu.SemaphoreType.DMA((n,))],
        compiler_params=_tc_params(("arbitrary",)),
    )(x2, g_pre, w_in, *late)
    return res[:6], list(res[6:])


def _discretise(lr, li, ls):
    step = jnp.exp(ls)
    mag = jnp.exp(lr * step)
    ar = mag * jnp.cos(li * step)
    ai = mag * jnp.sin(li * step)
    den = lr * lr + li * li
    cr = ((ar - 1.0) * lr + ai * li) / den
    ci = (ai * lr - (ar - 1.0) * li) / den
    return step, ar, ai, den, cr, ci


def _per_channel(v):
    return jnp.broadcast_to(v[:, None, :], (SSM_G, SSM_P, SSM_N)).reshape(SSM_G * SSM_P, SSM_N)


def _ssm_prep(lam_re, lam_im, log_step, b_re, b_im, seg):
    def body(lr_ref, li_ref, ls_ref, br_ref, bi_ref, lrr_ref, lir_ref, lsr_ref,
             ar_ref, ai_ref, bbr_ref, bbi_ref, pr_ref, pi_ref):
        _, _, _, _, cr, ci = _discretise(lr_ref[...], li_ref[...], ls_ref[...])
        cr, ci = _per_channel(cr), _per_channel(ci)
        br, bi = br_ref[...], bi_ref[...]
        bbr_ref[...] = cr * br - ci * bi
        bbi_ref[...] = cr * bi + ci * br
        stepr = jnp.exp(lsr_ref[...])
        k = (lax.broadcasted_iota(jnp.int32, (8, N_STATE), 0) + 1).astype(F32)
        magk = jnp.exp(k * (lrr_ref[...] * stepr))
        ang = k * (lir_ref[...] * stepr)
        pr_ref[0:8, :] = magk * jnp.cos(ang)
        pi_ref[0:8, :] = magk * jnp.sin(ang)
        n = 8
        while n < seg:
            tr, ti = pr_ref[n - 1:n, :], pi_ref[n - 1:n, :]
            xr, xi = pr_ref[0:n, :], pi_ref[0:n, :]
            pr_ref[n:2 * n, :] = xr * tr - xi * ti
            pi_ref[n:2 * n, :] = xr * ti + xi * tr
            n *= 2
        ar_ref[...] = pr_ref[0:1, :]
        ai_ref[...] = pi_ref[0:1, :]

    row = jax.ShapeDtypeStruct((1, N_STATE), F32)
    mat = jax.ShapeDtypeStruct((SSM_G * SSM_P, SSM_N), F32)
    pw = jax.ShapeDtypeStruct((seg, N_STATE), F32)
    vm = pl.BlockSpec(memory_space=pltpu.VMEM)
    step_row = jnp.broadcast_to(log_step, (SSM_G, SSM_N)).reshape(1, N_STATE)
    return pl.pallas_call(
        body, name="ssm_prep", out_shape=(row, row, mat, mat, pw, pw),
        in_specs=[vm] * 8, out_specs=(vm,) * 6,
    )(lam_re, lam_im, log_step, b_re, b_im, lam_re.reshape(1, N_STATE), lam_im.reshape(1, N_STATE), step_row)


def _seg_rows(t):
    if isinstance(t, int):
        return pl.ds(t * N_SEG, N_SEG)
    return pl.ds(pl.multiple_of(t * N_SEG, N_SEG), N_SEG)


def _scan_forward(xs, a_re, a_im, pw_re, pw_im, cs, seg):
    are = jnp.broadcast_to(a_re, (N_SEG, ST_T))
    aim = jnp.broadcast_to(a_im, (N_SEG, ST_T))

    def steps(k, carry):
        xr, xi = carry
        for j in range(SCAN_UNROLL):
            r = pl.multiple_of((k * SCAN_UNROLL + j) * N_SEG, N_SEG)
            nr = are * xr - aim * xi + xs[pl.ds(r, N_SEG), 0:ST_T]
            ni = are * xi + aim * xr + xs[pl.ds(r, N_SEG), ST_T:2 * ST_T]
            xs[pl.ds(r, N_SEG), 0:ST_T] = nr
            xs[pl.ds(r, N_SEG), ST_T:2 * ST_T] = ni
            xr, xi = nr, ni
        return xr, xi

    zero = jnp.zeros((N_SEG, ST_T), F32)
    fr, fi = lax.fori_loop(0, seg // SCAN_UNROLL, steps, (zero, zero))
    sr, si = pw_re[seg - 1:seg, :], pw_im[seg - 1:seg, :]
    cr = jnp.zeros((1, ST_T), F32)
    ci = jnp.zeros((1, ST_T), F32)
    cs[0:1, :] = cr
    cs[8:9, :] = ci
    for s in range(1, N_SEG):
        ncr = sr * cr - si * ci + fr[s - 1:s, :]
        nci = sr * ci + si * cr + fi[s - 1:s, :]
        cr, ci = ncr, nci
        cs[s:s + 1, :] = cr
        cs[8 + s:9 + s, :] = ci
    car, cai = cs[0:8, :], cs[8:16, :]

    def fix(t, _):
        r = pl.multiple_of(t * N_SEG, N_SEG)
        pr, pi = pw_re[pl.ds(t, 1), :], pw_im[pl.ds(t, 1), :]
        xs[pl.ds(r, N_SEG), 0:ST_T] = xs[pl.ds(r, N_SEG), 0:ST_T] + (pr * car - pi * cai)
        xs[pl.ds(r, N_SEG), ST_T:2 * ST_T] = xs[pl.ds(r, N_SEG), ST_T:2 * ST_T] + (pr * cai + pi * car)
        return 0

    lax.fori_loop(0, seg, fix, 0, unroll=SCAN_UNROLL)


def _ssm_forward(u_perm, bcat, ccat, a_re, a_im, pw_re, pw_im, d_row, seg):
    bl, rows, _ = u_perm.shape

    def body(u_ref, b_ref, c_ref, ar_ref, ai_ref, pr_ref, pi_ref, d_ref, y_ref, xs_ref, cs_ref):
        u = u_ref[0]
        xs, cs = xs_ref.at[0, 0], cs_ref.at[0, 0]
        xs[...] = _mm(u, b_ref[0])
        _scan_forward(xs, ar_ref[...], ai_ref[...], pr_ref, pi_ref, cs, seg)
        y_ref[0] = _mm(xs[...], c_ref[0]) + d_ref[...] * u

    state = lambda r, c: pl.BlockSpec((1, 1, r, c), lambda b, j: (b, j, 0, 0))
    return pl.pallas_call(
        body, name="ssm_forward", grid=(bl, N_GT),
        in_specs=[pl.BlockSpec((1, rows, CH_T), lambda b, j: (b, 0, j)),
                  pl.BlockSpec((1, CH_T, 2 * ST_T), lambda b, j: (j, 0, 0)),
                  pl.BlockSpec((1, 2 * ST_T, CH_T), lambda b, j: (j, 0, 0)),
                  pl.BlockSpec((1, ST_T), lambda b, j: (0, j)), pl.BlockSpec((1, ST_T), lambda b, j: (0, j)),
                  pl.BlockSpec((seg, ST_T), lambda b, j: (0, j)), pl.BlockSpec((seg, ST_T), lambda b, j: (0, j)),
                  pl.BlockSpec((1, CH_T), lambda b, j: (0, j))],
        out_specs=(pl.BlockSpec((1, rows, CH_T), lambda b, j: (b, 0, j)), state(rows, 2 * ST_T), state(16, ST_T)),
        out_shape=(jax.ShapeDtypeStruct((bl, rows, D_SSM), F32),
                   jax.ShapeDtypeStruct((bl, N_GT, rows, 2 * ST_T), F32),
                   jax.ShapeDtypeStruct((bl, N_GT, 16, ST_T), F32)),
        compiler_params=_tc_params(("arbitrary", "arbitrary")),
    )(u_perm, bcat, ccat, a_re, a_im, pw_re, pw_im, d_row)


def _ssm_backward(u_perm, dy_perm, states, carries, bcat_t, ccat_t, a_re, a_im, pw_re, pw_im, d_row, seg):
    bl, rows, _ = u_perm.shape

    def body(u_ref, dy_ref, xs_ref, cs_ref, bt_ref, ct_ref, ar_ref, ai_ref, pr_ref, pi_ref, d_ref,
             du_ref, db_ref, dc_ref, dar_ref, dai_ref, dd_ref, ls, cl):
        b = pl.program_id(1)
        u = u_ref[0]
        dy = dy_ref[0]
        xs, cs = xs_ref.at[0, 0], cs_ref.at[0, 0]
        ls[...] = _mm(dy, ct_ref[0])
        are = jnp.broadcast_to(ar_ref[...], (N_SEG, ST_T))
        aim = jnp.broadcast_to(ai_ref[...], (N_SEG, ST_T))

        def steps(k, carry):
            lr, li = carry
            for j in range(SCAN_UNROLL):
                r = pl.multiple_of((seg - 1 - (k * SCAN_UNROLL + j)) * N_SEG, N_SEG)
                nr = are * lr + aim * li + ls[pl.ds(r, N_SEG), 0:ST_T]
                ni = are * li - aim * lr + ls[pl.ds(r, N_SEG), ST_T:2 * ST_T]
                ls[pl.ds(r, N_SEG), 0:ST_T] = nr
                ls[pl.ds(r, N_SEG), ST_T:2 * ST_T] = ni
                lr, li = nr, ni
            return lr, li

        zero = jnp.zeros((N_SEG, ST_T), F32)
        fr, fi = lax.fori_loop(0, seg // SCAN_UNROLL, steps, (zero, zero))
        sr, si = pr_ref[seg - 1:seg, :], pi_ref[seg - 1:seg, :]
        cr = jnp.zeros((1, ST_T), F32)
        ci = jnp.zeros((1, ST_T), F32)
        cl[7:8, :] = cr
        cl[15:16, :] = ci
        for s in range(N_SEG - 2, -1, -1):
            ncr = sr * cr + si * ci + fr[s + 1:s + 2, :]
            nci = sr * ci - si * cr + fi[s + 1:s + 2, :]
            cr, ci = ncr, nci
            cl[s:s + 1, :] = cr
            cl[8 + s:9 + s, :] = ci
        clr, cli = cl[0:8, :], cl[8:16, :]

        def fix_rows(rows, t, xpr, xpi, acc):
            dr, di = acc
            pr, pi = pr_ref[pl.ds(seg - 1 - t, 1), :], pi_ref[pl.ds(seg - 1 - t, 1), :]
            lr = ls[rows, 0:ST_T] + (pr * clr + pi * cli)
            li = ls[rows, ST_T:2 * ST_T] + (pr * cli - pi * clr)
            ls[rows, 0:ST_T] = lr
            ls[rows, ST_T:2 * ST_T] = li
            return dr + (lr * xpr + li * xpi), di + (li * xpr - lr * xpi)

        def fix_at(t, acc):
            prev = _seg_rows(t - 1)
            return fix_rows(_seg_rows(t), t, xs[prev, 0:ST_T], xs[prev, ST_T:2 * ST_T], acc)

        def fix(k, acc):
            for j in range(SCAN_UNROLL):
                acc = fix_at(k * SCAN_UNROLL + j, acc)
            return acc

        acc = fix_rows(pl.ds(0, N_SEG), 0, cs[0:8, :], cs[8:16, :], (zero, zero))
        for t in range(1, SCAN_UNROLL):
            acc = fix_at(t, acc)
        dr, di = lax.fori_loop(1, seg // SCAN_UNROLL, fix, acc)
        dar = jnp.sum(dr, axis=0, keepdims=True)
        dai = jnp.sum(di, axis=0, keepdims=True)
        lall = ls[...]
        du_ref[0] = _mm(lall, bt_ref[0]) + d_ref[...] * dy
        dbp = _mm_tn(u, lall)
        dcp = _mm_tn(dy, xs[...])
        ddp = jnp.sum(dy * u, axis=0, keepdims=True)

        @pl.when(b == 0)
        def _():
            db_ref[0] = dbp
            dc_ref[0] = dcp
            dar_ref[...] = dar
            dai_ref[...] = dai
            dd_ref[...] = ddp

        @pl.when(b != 0)
        def _():
            db_ref[0] += dbp
            dc_ref[0] += dcp
            dar_ref[...] += dar
            dai_ref[...] += dai
            dd_ref[...] += ddp

    tile3 = lambda r, c: pl.BlockSpec((1, r, c), lambda j, b: (j, 0, 0))
    lane = lambda r, c: pl.BlockSpec((r, c), lambda j, b: (0, j))
    act = pl.BlockSpec((1, rows, CH_T), lambda j, b: (b, 0, j))
    state = lambda r, c: pl.BlockSpec((1, 1, r, c), lambda j, b: (b, j, 0, 0))
    return pl.pallas_call(
        body, name="ssm_backward", grid=(N_GT, bl),
        in_specs=[act, act, state(rows, 2 * ST_T), state(16, ST_T), tile3(2 * ST_T, CH_T), tile3(CH_T, 2 * ST_T),
                  lane(1, ST_T), lane(1, ST_T), lane(seg, ST_T), lane(seg, ST_T), lane(1, CH_T)],
        out_specs=(act, tile3(CH_T, 2 * ST_T), tile3(CH_T, 2 * ST_T), lane(1, ST_T), lane(1, ST_T), lane(1, CH_T)),
        out_shape=(jax.ShapeDtypeStruct((bl, rows, D_SSM), F32),
                   jax.ShapeDtypeStruct((N_GT, CH_T, 2 * ST_T), F32), jax.ShapeDtypeStruct((N_GT, CH_T, 2 * ST_T), F32),
                   jax.ShapeDtypeStruct((1, N_STATE), F32), jax.ShapeDtypeStruct((1, N_STATE), F32),
                   jax.ShapeDtypeStruct((1, D_SSM), F32)),
        scratch_shapes=[pltpu.VMEM((rows, 2 * ST_T), F32), pltpu.VMEM((16, ST_T), F32)],
        compiler_params=_tc_params(("arbitrary", "arbitrary")),
    )(u_perm, dy_perm, states, carries, bcat_t, ccat_t, a_re, a_im, pw_re, pw_im, d_row)


def _ssm_param_grads(lam_re, lam_im, log_step, b_re, b_im, da_re, da_im, dbb_re, dbb_im):
    def body(lr_ref, li_ref, ls_ref, br_ref, bi_ref, gar_ref, gai_ref, gbr_ref, gbi_ref,
             dlr_ref, dli_ref, dls_ref, dbr_ref, dbi_ref):
        lr, li = lr_ref[...], li_ref[...]
        step, ar, ai, den, cr, ci = _discretise(lr, li, ls_ref[...])
        crb, cib = _per_channel(cr), _per_channel(ci)
        br, bi = br_ref[...], bi_ref[...]
        gbr, gbi = gbr_ref[...], gbi_ref[...]
        dbr_ref[...] = crb * gbr + cib * gbi
        dbi_ref[...] = crb * gbi - cib * gbr
        over_channels = lambda t: jnp.sum(t.reshape(SSM_G, SSM_P, SSM_N), axis=1)
        gcr = over_channels(br * gbr + bi * gbi)
        gci = over_channels(br * gbi - bi * gbr)
        ilr, ili = lr / den, -li / den
        gar = gar_ref[...] + (ilr * gcr + ili * gci)
        gai = gai_ref[...] + (ilr * gci - ili * gcr)
        qr, qi = cr * ilr - ci * ili, cr * ili + ci * ilr
        glr = -(qr * gcr + qi * gci)
        gli = -(qr * gci - qi * gcr)
        gwr = ar * gar + ai * gai
        gwi = ar * gai - ai * gar
        dlr_ref[...] = glr + step * gwr
        dli_ref[...] = gli + step * gwi
        dls_ref[...] = jnp.sum(lr * gwr + li * gwi, axis=-1, keepdims=True) * step

    lam = jax.ShapeDtypeStruct((SSM_G, SSM_N), F32)
    mat = jax.ShapeDtypeStruct((SSM_G * SSM_P, SSM_N), F32)
    vm = pl.BlockSpec(memory_space=pltpu.VMEM)
    return pl.pallas_call(
        body, name="ssm_param_grads", out_shape=(lam, lam, jax.ShapeDtypeStruct((SSM_G, 1), F32), mat, mat),
        in_specs=[vm] * 9, out_specs=(vm,) * 5,
    )(lam_re, lam_im, log_step, b_re, b_im, da_re, da_im, dbb_re, dbb_im)


ROWS4 = Q_PER_KV * ATT_BLOCK


def _att_dist_mask(first_block):
    qi = lax.broadcasted_iota(jnp.int32, (ROWS4, 2 * ATT_BLOCK), 0) & (ATT_BLOCK - 1)
    si = lax.broadcasted_iota(jnp.int32, (ROWS4, 2 * ATT_BLOCK), 1)
    dist = qi + ATT_BLOCK - si
    valid = (dist >= 0) & (dist < ATT_BLOCK) & ((si >= ATT_BLOCK) | jnp.logical_not(first_block))
    return dist.astype(F32), valid


def _stack_heads(x, kv):
    return jnp.concatenate([x[:, (kv * Q_PER_KV + g) * HEAD_DIM:(kv * Q_PER_KV + g + 1) * HEAD_DIM]
                            for g in range(Q_PER_KV)], axis=0)


def _stack_cols(x, kv):
    return jnp.concatenate([x[:, kv * Q_PER_KV + g:kv * Q_PER_KV + g + 1] for g in range(Q_PER_KV)], axis=0)


def _per_head_col(vals):
    return jnp.concatenate([jnp.full((ATT_BLOCK, 1), v, F32) for v in vals], axis=0)


def _attn_forward(q, k, v, sinks, bl, nb):
    t = q.shape[0]

    def body(sink_ref, q_ref, kp_ref, kc_ref, vp_ref, vc_ref, o_ref, lse_ref):
        i = pl.program_id(1)
        dist4, valid4 = _att_dist_mask(i == 0)
        dist, valid = dist4[0:ATT_BLOCK, :], valid4[0:ATT_BLOCK, :]
        kk = jnp.concatenate([kp_ref[...], kc_ref[...]], axis=0)
        vv = jnp.concatenate([vp_ref[...], vc_ref[...]], axis=0)
        qv = q_ref[...]
        for h in range(N_HEADS):
            kv = h // Q_PER_KV
            slope = 2.0 ** (-(h + 1))
            qh = qv[:, h * HEAD_DIM:(h + 1) * HEAD_DIM]
            kh = kk[:, kv * HEAD_DIM:(kv + 1) * HEAD_DIM]
            vh = vv[:, kv * HEAD_DIM:(kv + 1) * HEAD_DIM]
            s = _mm_nt(qh, kh) * ATT_SCALE - slope * dist
            s = jnp.where(valid, s, NEG_BIG)
            sink = sink_ref[h]
            m = jnp.maximum(jnp.max(s, axis=-1, keepdims=True), sink)
            e = jnp.exp(s - m)
            den = jnp.sum(e, axis=-1, keepdims=True) + jnp.exp(sink - m)
            o_ref[:, h * HEAD_DIM:(h + 1) * HEAD_DIM] = _mm(e, vh) * (1.0 / den)
            lse_ref[:, h:h + 1] = m + jnp.log(den)

    cur = lambda w: pl.BlockSpec((ATT_BLOCK, w), lambda b, i: (b * nb + i, 0))
    prev = lambda w: pl.BlockSpec((ATT_BLOCK, w), lambda b, i: (b * nb + jnp.maximum(i - 1, 0), 0))
    return pl.pallas_call(
        body, name="attn_forward", grid=(bl, nb),
        in_specs=[pl.BlockSpec(memory_space=pltpu.SMEM), cur(512), prev(128), cur(128), prev(128), cur(128)],
        out_specs=(cur(512), cur(N_HEADS)),
        out_shape=(jax.ShapeDtypeStruct((t, D_ATTN), F32), jax.ShapeDtypeStruct((t, N_HEADS), F32)),
        compiler_params=_tc_params(("arbitrary", "arbitrary")),
    )(sinks, q, k, k, v, v)


def _attn_backward(q, k, v, o, do, lse, sinks, late16, late_own, bl, nb):
    t = q.shape[0]
    n = len(late16)
    steps = bl * nb
    mid1, mid2, last = steps // 4, (steps * 3) // 4, steps - 1

    def body(*refs):
        (sink_ref, qc_ref, qn_ref, kp_ref, kc_ref, vp_ref, vc_ref, oc_ref, on_ref, doc_ref, don_ref,
         lc_ref, ln_ref) = refs[:13]
        g16_r, go_r = refs[13:13 + n], refs[13 + n:13 + 2 * n]
        dq_ref, dk_ref, dv_ref, ds_ref = refs[13 + 2 * n:17 + 2 * n]
        red = refs[17 + 2 * n:17 + 3 * n]
        own16, recv1, send2, recv2 = (refs[17 + 3 * n + k * n:17 + 3 * n + (k + 1) * n] for k in range(4))
        s_send, s_recv, s_local = refs[17 + 7 * n:]
        b, i = pl.program_id(0), pl.program_id(1)
        step = b * nb + i
        x, y, c = _mesh_pos()
        sibling = (x, y, 1 - c)
        chips = [(1 - x, y), (x, 1 - y), (1 - x, 1 - y)]
        all_chips = [(x, y)] + chips

        def lvl1(a, j):
            return _remote(g16_r[a].at[_slot(*all_chips[j], 1 - c)], recv1[a].at[j], s_send, s_recv, 7 * a + j, sibling)

        def lvl2(a, j):
            return _remote(send2[a].at[j], recv2[a].at[j], s_send, s_recv, 7 * a + 4 + j, (*chips[j], c))

        def mine(a, j):
            return pltpu.make_async_copy(g16_r[a].at[_slot(*chips[j], c)], own16[a].at[j], s_local.at[3 * a + j])

        @pl.when(step == 0)
        def _():
            for a in range(n):
                for j in range(3):
                    mine(a, j).start()
                for j in range(4):
                    lvl1(a, j).start()

        @pl.when(step == mid1)
        def _():
            for a in range(n):
                for j in range(3):
                    mine(a, j).wait()
                for j in range(4):
                    lvl1(a, j).wait_recv()

                def partials(r, a=a):
                    red[a][r, :] = go_r[a][r, :] + recv1[a][0, r, :].astype(F32)
                    for j in range(3):
                        send2[a][j, r, :] = (own16[a][j, r, :].astype(F32)
                                             + recv1[a][1 + j, r, :].astype(F32)).astype(BF16)

                _row_chunks(go_r[a].shape[0], partials)
                for j in range(3):
                    lvl2(a, j).start()

        @pl.when(step == mid2)
        def _():
            for a in range(n):
                for j in range(3):
                    lvl2(a, j).wait_recv()

                def total(r, a=a):
                    g = red[a][r, :]
                    for j in range(3):
                        g = g + recv2[a][j, r, :].astype(F32)
                    red[a][r, :] = g

                _row_chunks(go_r[a].shape[0], total)

        @pl.when(step == last)
        def _():
            for a in range(n):
                for j in range(4):
                    lvl1(a, j).wait_send()
                for j in range(3):
                    lvl2(a, j).wait_send()

        dist, valid = _att_dist_mask(i == 0)
        has_next = i + 1 < nb
        dist_n = dist[:, 0:ATT_BLOCK]
        valid_n = (dist_n < ATT_BLOCK) & has_next
        kk = jnp.concatenate([kp_ref[...], kc_ref[...]], axis=0)
        vv = jnp.concatenate([vp_ref[...], vc_ref[...]], axis=0)
        qc, qn = qc_ref[...], qn_ref[...]
        oc, on = oc_ref[...], on_ref[...]
        doc, don = doc_ref[...], don_ref[...]
        lc, ln = lc_ref[...], ln_ref[...]
        dsink_cols = []
        for kv in range(KV_HEADS):
            heads = range(kv * Q_PER_KV, (kv + 1) * Q_PER_KV)
            kh = kk[:, kv * HEAD_DIM:(kv + 1) * HEAD_DIM]
            vh = vv[:, kv * HEAD_DIM:(kv + 1) * HEAD_DIM]
            khc, vhc = kh[ATT_BLOCK:, :], vh[ATT_BLOCK:, :]
            slope = _per_head_col([2.0 ** (-(h + 1)) for h in heads])
            sink = _per_head_col([sink_ref[h] for h in heads])
            q4, do4 = _stack_heads(qc, kv), _stack_heads(doc, kv)
            delta = jnp.sum(do4 * _stack_heads(oc, kv), axis=-1, keepdims=True)
            lse4 = _stack_cols(lc, kv)
            s = _mm_nt(q4, kh) * ATT_SCALE - slope * dist
            p = jnp.where(valid, jnp.exp(s - lse4), 0.0)
            dsc = p * (_mm_nt(do4, vh) - delta)
            dq4 = _mm(dsc, kh) * ATT_SCALE
            dk_acc = _mm_tn(dsc[:, ATT_BLOCK:], q4)
            dv_acc = _mm_tn(p[:, ATT_BLOCK:], do4)
            dsink4 = jnp.exp(sink - lse4) * delta
            q4n, do4n = _stack_heads(qn, kv), _stack_heads(don, kv)
            delta_n = jnp.sum(do4n * _stack_heads(on, kv), axis=-1, keepdims=True)
            s2 = _mm_nt(q4n, khc) * ATT_SCALE - slope * dist_n
            p2 = jnp.where(valid_n, jnp.exp(s2 - _stack_cols(ln, kv)), 0.0)
            ds2 = p2 * (_mm_nt(do4n, vhc) - delta_n)
            dk_acc += _mm_tn(ds2, q4n)
            dv_acc += _mm_tn(p2, do4n)
            dk_ref[:, kv * HEAD_DIM:(kv + 1) * HEAD_DIM] = dk_acc * ATT_SCALE
            dv_ref[:, kv * HEAD_DIM:(kv + 1) * HEAD_DIM] = dv_acc
            for g, h in enumerate(heads):
                rows = slice(g * ATT_BLOCK, (g + 1) * ATT_BLOCK)
                dq_ref[:, h * HEAD_DIM:(h + 1) * HEAD_DIM] = dq4[rows, :]
                dsink_cols.append(-jnp.sum(dsink4[rows, :], axis=0, keepdims=True))
        dsink = jnp.concatenate(dsink_cols, axis=1)

        @pl.when((b == 0) & (i == 0))
        def _():
            ds_ref[...] = dsink

        @pl.when((b != 0) | (i != 0))
        def _():
            ds_ref[...] += dsink

    cur = lambda w: pl.BlockSpec((ATT_BLOCK, w), lambda b, i: (b * nb + i, 0))
    prev = lambda w: pl.BlockSpec((ATT_BLOCK, w), lambda b, i: (b * nb + jnp.maximum(i - 1, 0), 0))
    nxt = lambda w: pl.BlockSpec((ATT_BLOCK, w), lambda b, i: (b * nb + jnp.minimum(i + 1, nb - 1), 0))
    const2 = lambda s: pl.BlockSpec(s, lambda b, i: (0, 0))
    shard = [s.shape for s in late_own]
    res = pl.pallas_call(
        body, name="attn_backward", grid=(bl, nb),
        in_specs=[pl.BlockSpec(memory_space=pltpu.SMEM), cur(512), nxt(512), prev(128), cur(128), prev(128), cur(128),
                  cur(512), nxt(512), cur(512), nxt(512), cur(N_HEADS), nxt(N_HEADS)]
        + [pl.BlockSpec(memory_space=pl.ANY)] * n + [const2(s) for s in shard],
        out_specs=(cur(512), cur(128), cur(128), const2((1, N_HEADS))) + tuple(const2(s) for s in shard),
        out_shape=(jax.ShapeDtypeStruct((t, D_ATTN), F32), jax.ShapeDtypeStruct((t, 128), F32),
                   jax.ShapeDtypeStruct((t, 128), F32), jax.ShapeDtypeStruct((1, N_HEADS), F32))
        + tuple(jax.ShapeDtypeStruct(s, F32) for s in shard),
        scratch_shapes=[pltpu.VMEM((3,) + s, BF16) for s in shard] + [pltpu.VMEM((4,) + s, BF16) for s in shard]
        + [pltpu.VMEM((3,) + s, BF16) for s in shard] + [pltpu.VMEM((3,) + s, BF16) for s in shard]
        + [pltpu.SemaphoreType.DMA((7 * n,)), pltpu.SemaphoreType.DMA((7 * n,)), pltpu.SemaphoreType.DMA((3 * n,))],
        compiler_params=_tc_params(("arbitrary", "arbitrary")),
    )(sinks, q, q, k, k, v, v, o, o, do, do, lse, lse, *late16, *late_own)
    return res[:4], list(res[4:])


def _mix_forward_backward(x2, y_perm, z_ssm, attn, z_attn, p2, target2, w_glu, b_glu, w_out, g_post, w_gate, b_gate,
                          w_proj, bl, seg):
    t = x2.shape[0]
    tm = seg

    def body(x_ref, y_ref, zs_ref, at_ref, za_ref, p_ref, tg_ref,
             wglu_ref, bglu_ref, wout_ref, gpost_ref, wgate_ref, bgate_ref, wproj_ref,
             loss_ref, dh1_ref, dy_ref, dzs_ref, dat_ref, dza_ref,
             dwglu_ref, dbglu_ref, dwout_ref, dgpost_ref, dwgate_ref, dbgate_ref, dwproj_ref):
        i = pl.program_id(0)
        y = y_ref[0]
        u3 = GELU_C * (y + GELU_K * y * y * y)
        th = jnp.tanh(u3)
        gl = 0.5 * y * (1.0 + th)
        a = _mm(gl, wglu_ref[...]) + bglu_ref[...]
        sa = _sigmoid(a)
        glu = gl * sa
        zs = zs_ref[...]
        sgs = _sigmoid(zs)
        ssm_out = glu * (zs * sgs)
        za = za_ref[...]
        sga = _sigmoid(za)
        at = at_ref[...]
        attn_out = at * (za * sga)
        cat = jnp.concatenate([ssm_out, attn_out], axis=-1).astype(BF16)
        mixed = _mm(cat, wout_ref[...])
        r2 = lax.rsqrt(jnp.mean(mixed * mixed, axis=-1, keepdims=True) + EPS)
        nhat = mixed * r2
        gpost = gpost_ref[...]
        h1 = x_ref[...] + nhat * gpost
        gate = _sigmoid(_mm(h1, wgate_ref[...]) + bgate_ref[...])
        pv = p_ref[...]
        pp = _mm(pv, wproj_ref[...])
        h2 = h1 + gate * pp
        err = h2 - tg_ref[...]
        loss_part = jnp.sum(jnp.sum(err * err, axis=-1, keepdims=True), axis=0, keepdims=True) * (0.5 / D_MODEL)
        dh2 = err * (1.0 / D_MODEL)
        dgp = dh2 * pp * gate * (1.0 - gate)
        dpp = dh2 * gate
        dh1 = dh2 + _mm_nt(dgp, wgate_ref[...])
        dh1_ref[...] = dh1
        dnhat = dh1 * gpost
        dmixed = r2 * (dnhat - nhat * jnp.mean(dnhat * nhat, axis=-1, keepdims=True))
        dcat = _mm_nt(dmixed, wout_ref[...])
        dso, dao = dcat[:, 0:D_SSM], dcat[:, D_SSM:]
        dat_ref[...] = dao * (za * sga)
        dza_ref[...] = dao * at * (sga * (1.0 + za * (1.0 - sga)))
        dzs_ref[...] = dso * glu * (sgs * (1.0 + zs * (1.0 - sgs)))
        dglu = dso * (zs * sgs)
        da = dglu * gl * sa * (1.0 - sa)
        dgl = dglu * sa + _mm_nt(da, wglu_ref[...])
        dgelu = 0.5 * (1.0 + th) + 0.5 * y * (1.0 - th * th) * (GELU_C * (1.0 + 3.0 * GELU_K * y * y))
        dy_ref[0] = dgl * dgelu
        parts = (
            (dwglu_ref, _mm_tn(gl, da)), (dbglu_ref, jnp.sum(da, axis=0, keepdims=True)),
            (dwout_ref, _mm_tn(cat, dmixed)), (dgpost_ref, jnp.sum(dh1 * nhat, axis=0, keepdims=True)),
            (dwgate_ref, _mm_tn(h1, dgp)), (dbgate_ref, jnp.sum(dgp, axis=0, keepdims=True)),
            (dwproj_ref, _mm_tn(pv, dpp)), (loss_ref, loss_part),
        )

        @pl.when(i == 0)
        def _():
            for ref, val in parts:
                ref[...] = val

        @pl.when(i != 0)
        def _():
            for ref, val in parts:
                ref[...] += val

    row = lambda w: pl.BlockSpec((tm, w), lambda i: (i, 0))
    perm = pl.BlockSpec((1, tm, D_SSM), lambda i: (i // N_SEG, 0, i % N_SEG))
    perm_shape = jax.ShapeDtypeStruct((bl, seg, N_SEG * D_SSM), F32)
    acc = lambda r, c: (_const_spec((r, c)), jax.ShapeDtypeStruct((r, c), F32))
    accs = [acc(D_SSM, D_SSM), acc(1, D_SSM), acc(D_MODEL, D_MODEL), acc(1, D_MODEL), acc(D_MODEL, D_MODEL),
            acc(1, D_MODEL), acc(D_PLE, D_MODEL)]
    return pl.pallas_call(
        body, name="mix_forward_backward", grid=(t // tm,),
        in_specs=[row(D_MODEL), perm, row(512), row(512), row(512), row(D_PLE), row(D_MODEL),
                  _const_spec((D_SSM, D_SSM)), _const_spec((1, D_SSM)), _const_spec((D_MODEL, D_MODEL)),
                  _const_spec((1, D_MODEL)), _const_spec((D_MODEL, D_MODEL)), _const_spec((1, D_MODEL)),
                  _const_spec((D_PLE, D_MODEL))],
        out_specs=(_const_spec((1, 1)), row(D_MODEL), perm, row(512), row(512), row(512)) + tuple(a[0] for a in accs),
        out_shape=(jax.ShapeDtypeStruct((1, 1), F32), jax.ShapeDtypeStruct((t, D_MODEL), F32), perm_shape,
                   jax.ShapeDtypeStruct((t, 512), F32), jax.ShapeDtypeStruct((t, 512), F32),
                   jax.ShapeDtypeStruct((t, 512), F32)) + tuple(a[1] for a in accs),
        compiler_params=_tc_params(("arbitrary",)),
    )(x2, y_perm, z_ssm, attn, z_attn, p2, target2, w_glu, b_glu, w_out, g_post, w_gate, b_gate, w_proj)


def _in_backward(x2, dh1, du_perm, dz_ssm, dq, dk, dv, dz_attn, g_pre, w_in, bl, seg):
    t = x2.shape[0]
    tm = seg

    def body(x_ref, dh1_ref, du_ref, dzs_ref, dq_ref, dk_ref, dv_ref, dza_ref, g_ref, w_ref,
             gx_ref, dw_ref, dg_ref):
        i = pl.program_id(0)
        xv = x_ref[...]
        r = lax.rsqrt(jnp.mean(xv * xv, axis=-1, keepdims=True) + EPS)
        xhat = xv * r
        g = g_ref[...]
        hn = (xhat * g).astype(BF16)
        dproj = jnp.concatenate([du_ref[0].astype(BF16), dzs_ref[...].astype(BF16), dq_ref[...].astype(BF16),
                                 dk_ref[...].astype(BF16), dv_ref[...].astype(BF16), dza_ref[...].astype(BF16)],
                                axis=-1)
        dhn = _mm(dproj, w_ref[...])
        dxhat = dhn * g
        gx_ref[...] = dh1_ref[...] + r * (dxhat - xhat * jnp.mean(dxhat * xhat, axis=-1, keepdims=True))
        dwp = _mm_tn(dproj, hn)
        dgp = jnp.sum(dhn * xhat, axis=0, keepdims=True)

        @pl.when(i == 0)
        def _():
            dw_ref[...] = dwp
            dg_ref[...] = dgp

        @pl.when(i != 0)
        def _():
            dw_ref[...] += dwp
            dg_ref[...] += dgp

    row = lambda w: pl.BlockSpec((tm, w), lambda i: (i, 0))
    perm = pl.BlockSpec((1, tm, D_SSM), lambda i: (i // N_SEG, 0, i % N_SEG))
    return pl.pallas_call(
        body, name="in_backward", grid=(t // tm,),
        in_specs=[row(D_MODEL), row(D_MODEL), perm, row(512), row(512), row(128), row(128), row(512),
                  _const_spec((1, D_MODEL)), _const_spec((D_IN, D_MODEL))],
        out_specs=(row(D_MODEL), _const_spec((D_IN, D_MODEL)), _const_spec((1, D_MODEL))),
        out_shape=(jax.ShapeDtypeStruct((t, D_MODEL), F32), jax.ShapeDtypeStruct((D_IN, D_MODEL), F32),
                   jax.ShapeDtypeStruct((1, D_MODEL), F32)),
        compiler_params=_tc_params(("arbitrary",)),
    )(x2, dh1, du_perm, dz_ssm, dq, dk, dv, dz_attn, g_pre, w_in)


def _block_diag(t):
    a, b = t.shape[1], t.shape[2]
    eye = jnp.eye(G_TILE, dtype=t.dtype)
    t = t.reshape(N_GT, G_TILE, a, 1, b) * eye[None, :, None, :, None]
    return t.reshape(N_GT, G_TILE * a, G_TILE * b)


def _diag_blocks(m, a, b):
    m = m.reshape(N_GT, G_TILE, a, G_TILE, b)
    return jnp.einsum("tgagb->tgab", m).reshape(SSM_G, a, b)


def _local_step(x, p, target, pre_norm_g, w_in, ssm_lam_re, ssm_lam_im, ssm_log_step, ssm_b_re, ssm_b_im, ssm_c_re,
                ssm_c_im, ssm_d, ssm_b_glu, attn_sinks, post_norm_g, pl_b_gate, late, me):
    bl, seq, _ = x.shape
    seg = seq // N_SEG
    nb = seq // ATT_BLOCK
    t = bl * seq
    x2 = x.reshape(t, D_MODEL)
    p2 = p.reshape(t, D_PLE)
    tg2 = target.reshape(t, D_MODEL)

    lam_re, lam_im = ssm_lam_re, ssm_lam_im
    log_step = ssm_log_step.reshape(SSM_G, 1)
    a_re_row, a_im_row, bb_re, bb_im, pw_re, pw_im = _ssm_prep(lam_re, lam_im, log_step, ssm_b_re, ssm_b_im, seg)
    by_group = lambda t: t.reshape(SSM_G, SSM_P, SSM_N)
    bcat = jnp.concatenate([_block_diag(by_group(bb_re)), _block_diag(by_group(bb_im))], axis=-1).astype(BF16)
    ccat_t = jnp.concatenate([_block_diag(by_group(ssm_c_re)), -_block_diag(by_group(ssm_c_im))],
                             axis=-1).astype(BF16)
    bcat_t = jnp.swapaxes(bcat, 1, 2)
    ccat = jnp.swapaxes(ccat_t, 1, 2)
    d_row = ssm_d.reshape(1, D_SSM)

    (u_perm, z_ssm, q, k, v, z_attn), gathered = _in_proj(x2, pre_norm_g.reshape(1, D_MODEL), w_in, late, bl, seg)
    w_out, w_gate, w_proj, w_glu = (_gathered_to_full(n, g) for n, g in zip(LATE_NAMES, gathered))
    u_perm = u_perm.reshape(bl, seq, D_SSM)
    y_perm, states, carries = _ssm_forward(u_perm, bcat, ccat, a_re_row, a_im_row, pw_re, pw_im, d_row, seg)
    sinks = attn_sinks.reshape(N_HEADS)
    attn, lse = _attn_forward(q, k, v, sinks, bl, nb)
    (loss, dh1, dy_perm, dz_ssm, dattn, dz_attn, d_w_glu, d_b_glu, d_w_out, d_g_post, d_w_gate, d_b_gate,
     d_w_proj) = _mix_forward_backward(
        x2, y_perm.reshape(bl, seg, N_SEG * D_SSM), z_ssm, attn, z_attn, p2, tg2, w_glu,
        ssm_b_glu.reshape(1, D_SSM), w_out, post_norm_g.reshape(1, D_MODEL), w_gate, pl_b_gate.reshape(1, D_MODEL),
        w_proj, bl, seg)
    owned = [_full_to_owned(n, d) for n, d in zip(LATE_NAMES, (d_w_out, d_w_gate, d_w_proj, d_w_glu))]
    (dq, dk, dv, d_sinks), late_grads = _attn_backward(
        q, k, v, attn, dattn, lse, sinks, [o.astype(BF16) for o in owned],
        [lax.dynamic_index_in_dim(o, me, axis=0, keepdims=False) for o in owned], bl, nb)
    du_perm, d_bcat, d_ccat_t, da_re, da_im, d_d = _ssm_backward(
        u_perm, dy_perm.reshape(bl, seq, D_SSM), states, carries, bcat_t, ccat_t, a_re_row, a_im_row, pw_re, pw_im,
        d_row, seg)
    grad_x, d_w_in, d_g_pre = _in_backward(
        x2, dh1, du_perm.reshape(bl, seg, N_SEG * D_SSM), dz_ssm, dq, dk, dv, dz_attn,
        pre_norm_g.reshape(1, D_MODEL), w_in, bl, seg)
    flat = lambda t: t.reshape(SSM_G * SSM_P, SSM_N)
    d_lam_re, d_lam_im, d_ls, d_b_re, d_b_im = _ssm_param_grads(
        lam_re, lam_im, log_step, ssm_b_re, ssm_b_im, da_re.reshape(SSM_G, SSM_N), da_im.reshape(SSM_G, SSM_N),
        flat(_diag_blocks(d_bcat[:, :, 0:ST_T], SSM_P, SSM_N)), flat(_diag_blocks(d_bcat[:, :, ST_T:], SSM_P, SSM_N)))
    grads = {
        "pre_norm_g": d_g_pre, "w_in": d_w_in, "ssm_lam_re": d_lam_re, "ssm_lam_im": d_lam_im,
        "ssm_log_step": d_ls, "ssm_b_re": d_b_re, "ssm_b_im": d_b_im,
        "ssm_c_re": _diag_blocks(d_ccat_t[:, :, 0:ST_T], SSM_P, SSM_N),
        "ssm_c_im": -_diag_blocks(d_ccat_t[:, :, ST_T:], SSM_P, SSM_N),
        "ssm_d": d_d, "ssm_b_glu": d_b_glu, "attn_sinks": d_sinks, "post_norm_g": d_g_post, "pl_b_gate": d_b_gate,
    }
    return loss, grad_x.reshape(bl, seq, D_MODEL), grads, late_grads


LATE_NAMES = ("w_out", "pl_w_gate", "pl_w_proj", "ssm_w_glu")
BIG_NAMES = ("w_in",) + LATE_NAMES
COL_SHARDED = {"w_in": D_IN // N_DEV, "pl_w_proj": D_MODEL // N_DEV}
WEIGHT_NAMES = ("pre_norm_g", "w_in", "ssm_lam_re", "ssm_lam_im", "ssm_log_step", "ssm_b_re", "ssm_b_im", "ssm_c_re",
                "ssm_c_im", "ssm_d", "ssm_w_glu", "ssm_b_glu", "attn_sinks", "w_out", "post_norm_g", "pl_w_proj",
                "pl_w_gate", "pl_b_gate")


TRANSPOSED = {"w_in": (0, 1), "ssm_b_re": (1, 2), "ssm_b_im": (1, 2)}


def _kernel_form(name, a):
    a = a[0]
    if name in TRANSPOSED:
        a = jnp.swapaxes(a, *TRANSPOSED[name])
    if name in ("ssm_b_re", "ssm_b_im", "ssm_c_re", "ssm_c_im"):
        a = a.reshape(SSM_G * SSM_P, SSM_N)
    return a


def _given_form(name, a, shape):
    if name in TRANSPOSED:
        i, j = TRANSPOSED[name]
        swapped = list(shape[1:])
        swapped[i], swapped[j] = swapped[j], swapped[i]
        return jnp.swapaxes(a.reshape(swapped), i, j).reshape(shape)
    return a.reshape(shape)


def _gathered_to_full(name, g):
    _, rows, cols = g.shape
    if name in COL_SHARDED:
        return jnp.swapaxes(g, 0, 1).reshape(rows, N_DEV * cols)
    return g.reshape(N_DEV * rows, cols)


def _full_to_owned(name, full):
    if name in COL_SHARDED:
        return jnp.swapaxes(full.reshape(full.shape[0], N_DEV, COL_SHARDED[name]), 0, 1)
    return full.reshape(N_DEV, full.shape[0] // N_DEV, full.shape[1])


def kernel(x, p, pre_norm_g, w_in, ssm_lam_re, ssm_lam_im, ssm_log_step, ssm_b_re, ssm_b_im, ssm_c_re, ssm_c_im, ssm_d, ssm_w_glu, ssm_b_glu, attn_sinks, w_out, post_norm_g, pl_w_proj, pl_w_gate, pl_b_gate, loss_target, m_pre_norm_g, m_w_in, m_ssm_lam_re, m_ssm_lam_im, m_ssm_log_step, m_ssm_b_re, m_ssm_b_im, m_ssm_c_re, m_ssm_c_im, m_ssm_d, m_ssm_w_glu, m_ssm_b_glu, m_attn_sinks, m_w_out, m_post_norm_g, m_pl_w_proj, m_pl_w_gate, m_pl_b_gate, v_pre_norm_g, v_w_in, v_ssm_lam_re, v_ssm_lam_im, v_ssm_log_step, v_ssm_b_re, v_ssm_b_im, v_ssm_c_re, v_ssm_c_im, v_ssm_d, v_ssm_w_glu, v_ssm_b_glu, v_attn_sinks, v_w_out, v_post_norm_g, v_pl_w_proj, v_pl_w_gate, v_pl_b_gate):
    w = dict(pre_norm_g=pre_norm_g, w_in=w_in, ssm_lam_re=ssm_lam_re, ssm_lam_im=ssm_lam_im, ssm_log_step=ssm_log_step,
             ssm_b_re=ssm_b_re, ssm_b_im=ssm_b_im, ssm_c_re=ssm_c_re, ssm_c_im=ssm_c_im, ssm_d=ssm_d, ssm_w_glu=ssm_w_glu,
             ssm_b_glu=ssm_b_glu, attn_sinks=attn_sinks, w_out=w_out, post_norm_g=post_norm_g, pl_w_proj=pl_w_proj,
             pl_w_gate=pl_w_gate, pl_b_gate=pl_b_gate)
    m = dict(pre_norm_g=m_pre_norm_g, w_in=m_w_in, ssm_lam_re=m_ssm_lam_re, ssm_lam_im=m_ssm_lam_im,
             ssm_log_step=m_ssm_log_step, ssm_b_re=m_ssm_b_re, ssm_b_im=m_ssm_b_im, ssm_c_re=m_ssm_c_re,
             ssm_c_im=m_ssm_c_im, ssm_d=m_ssm_d, ssm_w_glu=m_ssm_w_glu, ssm_b_glu=m_ssm_b_glu, attn_sinks=m_attn_sinks,
             w_out=m_w_out, post_norm_g=m_post_norm_g, pl_w_proj=m_pl_w_proj, pl_w_gate=m_pl_w_gate,
             pl_b_gate=m_pl_b_gate)
    v = dict(pre_norm_g=v_pre_norm_g, w_in=v_w_in, ssm_lam_re=v_ssm_lam_re, ssm_lam_im=v_ssm_lam_im,
             ssm_log_step=v_ssm_log_step, ssm_b_re=v_ssm_b_re, ssm_b_im=v_ssm_b_im, ssm_c_re=v_ssm_c_re,
             ssm_c_im=v_ssm_c_im, ssm_d=v_ssm_d, ssm_w_glu=v_ssm_w_glu, ssm_b_glu=v_ssm_b_glu, attn_sinks=v_attn_sinks,
             w_out=v_w_out, post_norm_g=v_post_norm_g, pl_w_proj=v_pl_w_proj, pl_w_gate=v_pl_w_gate,
             pl_b_gate=v_pl_b_gate)
    me = _slot(lax.axis_index("x"), lax.axis_index("y"), lax.axis_index("c"))
    kf = lambda d: {n: _kernel_form(n, a) for n, a in d.items()}
    wk, mk, vk = kf(w), kf(m), kf(v)

    (gathered,) = _allgather_weights([wk["w_in"]])
    loss, grad_x, grads, g_late = _local_step(
        x, p[0], loss_target, wk["pre_norm_g"], gathered.reshape(D_IN, D_MODEL), wk["ssm_lam_re"], wk["ssm_lam_im"],
        wk["ssm_log_step"], wk["ssm_b_re"], wk["ssm_b_im"], wk["ssm_c_re"], wk["ssm_c_im"], wk["ssm_d"],
        wk["ssm_b_glu"], wk["attn_sinks"], wk["post_norm_g"], wk["pl_b_gate"], [wk[n] for n in LATE_NAMES], me)

    owned = grads["w_in"].reshape(N_DEV, D_IN // N_DEV, D_MODEL)
    tiny_form = lambda d: [d[n].reshape(rows, cols) for n, rows, cols in TINY]
    med_form = lambda d: [d[n].reshape(N_DEV, rows // N_DEV, cols) for n, rows, cols in MEDIUM]
    g_big, loss, g_tiny, g_med = _reduce_final(
        [owned.astype(BF16)], [lax.dynamic_index_in_dim(owned, me, axis=0, keepdims=False)],
        loss, tiny_form(grads), med_form(grads))
    names = BIG_NAMES + tuple(n for n, _, _ in TINY + MEDIUM)
    form = lambda d: [d[n] for n in BIG_NAMES] + tiny_form(d) + med_form(d)
    updated = _adamw_update(g_big + g_late + g_tiny + g_med, form(wk), form(mk), form(vk))
    vals = dict(zip(names, updated))
    results = [[_given_form(n, vals[n][kind], w[n].shape) for n in WEIGHT_NAMES] for kind in range(4)]
    return (loss.reshape(()), grad_x, *results[0], *results[1], *results[2], *results[3])
```

```python
import functools
import math

import jax
import jax.numpy as jnp
from jax import lax
from jax.experimental import pallas as pl
from jax.experimental.pallas import tpu as pltpu

F32 = jnp.float32
BF16 = jnp.bfloat16

D_MODEL = 1024
D_SSM = 512
D_ATTN = 512
SSM_P = 16
SSM_G = 32
SSM_N = 64
N_HEADS = 8
KV_HEADS = 2
Q_PER_KV = 4
HEAD_DIM = 64
ATT_BLOCK = 128
D_PLE = 256
D_IN = 2304
EPS = 1e-6
N_DEV = 8
N_SEG = 8
G_TILE = 8
N_GT = SSM_G // G_TILE
CH_T = G_TILE * SSM_P
ST_T = G_TILE * SSM_N
N_STATE = SSM_G * SSM_N
SCAN_UNROLL = 4
LANES = 128
VMEM_LIMIT = 60 * 1024 * 1024

ADAM_LR = 0.001
ADAM_B1 = 0.9
ADAM_B2 = 0.999
ADAM_EPS = 1e-08
ADAM_WD = 0.01
ADAM_STEP = 10

GELU_C = math.sqrt(2.0 / math.pi)
GELU_K = 0.044715
ATT_SCALE = 1.0 / math.sqrt(HEAD_DIM)
NEG_BIG = -1e30


def _mm(a, b):
    return jnp.dot(a.astype(BF16), b.astype(BF16), preferred_element_type=F32)


def _mm_nt(a, b):
    return lax.dot_general(a.astype(BF16), b.astype(BF16), (((1,), (1,)), ((), ())), preferred_element_type=F32)


def _mm_tn(a, b):
    return lax.dot_general(a.astype(BF16), b.astype(BF16), (((0,), (0,)), ((), ())), preferred_element_type=F32)


def _sigmoid(x):
    return 1.0 / (1.0 + jnp.exp(-x))


def _tc_params(sem):
    return pltpu.CompilerParams(dimension_semantics=sem, vmem_limit_bytes=VMEM_LIMIT)


def _const_spec(shape):
    nd = len(shape)
    return pl.BlockSpec(shape, lambda *_: (0,) * nd)


def _mesh_pos():
    return lax.axis_index("x"), lax.axis_index("y"), lax.axis_index("c")


ROW_CHUNKS = (64, 32, 16)


def _row_chunk(nrows):
    return next((c for c in ROW_CHUNKS if nrows % c == 0), None)


def _row_chunks(nrows, fn, chunk=None, init=None):
    chunk = chunk or _row_chunk(nrows)

    def step(i, carry):
        rows = pl.ds(pl.multiple_of(i * chunk, chunk), chunk)
        if init is None:
            fn(rows)
            return carry
        return fn(rows, carry)

    return lax.fori_loop(0, nrows // chunk, step, 0 if init is None else init)


def _slot(px, py, pc):
    return 4 * px + 2 * py + pc


def _allgather_weights(shards):
    n = len(shards)

    def body(*refs):
        srcs, outs, (send_sems, recv_sems) = refs[:n], refs[n:2 * n], refs[2 * n:]
        x, y, c = _mesh_pos()
        me, sibling = (x, y, c), (x, y, 1 - c)
        chips = [(1 - x, y), (x, 1 - y), (1 - x, 1 - y)]

        def copy(a, k, block, to):
            blk = outs[a].at[_slot(*block)]
            return pltpu.make_async_remote_copy(
                src_ref=blk, dst_ref=blk, send_sem=send_sems.at[7 * a + k], recv_sem=recv_sems.at[7 * a + k],
                device_id=to, device_id_type=pl.DeviceIdType.MESH)

        sends = []
        for a in range(n):
            mine = outs[a].at[_slot(*me)]

            def cast(r, mine=mine, src=srcs[a]):
                mine[r, :] = src[r, :].astype(BF16)

            _row_chunks(srcs[a].shape[0], cast)
            first = [copy(a, 0, me, sibling)] + [copy(a, 1 + j, me, (*chip, c)) for j, chip in enumerate(chips)]
            for cp in first:
                cp.start()
            sends += first
        for a in range(n):
            for j, chip in enumerate(chips):
                copy(a, 1 + j, (*chip, c), me).wait_recv()
                fwd = copy(a, 4 + j, (*chip, c), sibling)
                fwd.start()
                sends.append(fwd)
        for a in range(n):
            copy(a, 0, sibling, me).wait_recv()
            for j, chip in enumerate(chips):
                copy(a, 4 + j, (*chip, 1 - c), me).wait_recv()
        for cp in sends:
            cp.wait_send()

    vm = pl.BlockSpec(memory_space=pltpu.VMEM)
    return pl.pallas_call(
        body, name="allgather_weights",
        out_shape=tuple(jax.ShapeDtypeStruct((N_DEV,) + s.shape, BF16) for s in shards),
        in_specs=[vm] * n, out_specs=(vm,) * n,
        scratch_shapes=[pltpu.SemaphoreType.DMA((7 * n,)), pltpu.SemaphoreType.DMA((7 * n,))],
        compiler_params=pltpu.CompilerParams(vmem_limit_bytes=VMEM_LIMIT),
    )(*shards)


def _adamw(w, g, m, v):
    m = ADAM_B1 * m + (1.0 - ADAM_B1) * g
    v = ADAM_B2 * v + (1.0 - ADAM_B2) * (g * g)
    m_hat = m / (1.0 - ADAM_B1 ** ADAM_STEP)
    v_hat = v / (1.0 - ADAM_B2 ** ADAM_STEP)
    delta = -ADAM_LR * (m_hat / (jnp.sqrt(v_hat) + ADAM_EPS) + ADAM_WD * w)
    return delta, m, v


def _remote(src, dst, send_sems, recv_sems, k, to):
    return pltpu.make_async_remote_copy(src_ref=src, dst_ref=dst, send_sem=send_sems.at[k], recv_sem=recv_sems.at[k],
                                        device_id=to, device_id_type=pl.DeviceIdType.MESH)


def _big_reduce_phases(g16_r, go_r, outs, send2, recv1, recv2, s_send, s_recv):
    n = len(g16_r)
    x, y, c = _mesh_pos()
    sibling = (x, y, 1 - c)
    chips = [(1 - x, y), (x, 1 - y), (1 - x, 1 - y)]
    all_chips = [(x, y)] + chips
    lvl1 = []
    for a in range(n):
        cps = [_remote(g16_r[a].at[_slot(*chip, 1 - c)], recv1[a].at[j], s_send, s_recv, 7 * a + j, sibling)
               for j, chip in enumerate(all_chips)]
        for cp in cps:
            cp.start()
        lvl1.append(cps)
    yield
    lvl2 = []
    for a in range(n):
        for cp in lvl1[a]:
            cp.wait_recv()
        og = outs[a]

        def partials(r, a=a, og=og):
            og[r, :] = go_r[a][r, :] + recv1[a][0, r, :].astype(F32)
            for j, chip in enumerate(chips):
                mine16 = g16_r[a][_slot(*chip, c), r, :].astype(F32)
                send2[a][j, r, :] = (mine16 + recv1[a][1 + j, r, :].astype(F32)).astype(BF16)

        _row_chunks(go_r[a].shape[0], partials)
        cps = [_remote(send2[a].at[j], recv2[a].at[j], s_send, s_recv, 7 * a + 4 + j, (*chip, c))
               for j, chip in enumerate(chips)]
        for cp in cps:
            cp.start()
        lvl2.append(cps)
    yield
    for a in range(n):
        for cp in lvl2[a]:
            cp.wait_recv()
        og = outs[a]

        def total(r, a=a, og=og):
            g = og[r, :]
            for j in range(3):
                g = g + recv2[a][j, r, :].astype(F32)
            og[r, :] = g

        _row_chunks(go_r[a].shape[0], total)
    yield
    for cps in lvl1 + lvl2:
        for cp in cps:
            cp.wait_send()


def _interleave(*phases):
    active = list(phases)
    while active:
        for g in list(active):
            try:
                next(g)
            except StopIteration:
                active.remove(g)


def _adamw_update(g, w, m, v):
    n = len(g)

    def body(*refs):
        g_r, w_r, m_r, v_r = (refs[i * n:(i + 1) * n] for i in range(4))
        outs = refs[4 * n:]
        for a in range(n):
            og, od, om, ov = outs[4 * a:4 * a + 4]

            def update(idx, a=a, og=og, od=od, om=om, ov=ov):
                gv = g_r[a][idx]
                d, nm, nv = _adamw(w_r[a][idx], gv, m_r[a][idx], v_r[a][idx])
                og[idx] = gv
                od[idx] = d
                om[idx] = nm
                ov[idx] = nv

            shape = g_r[a].shape
            if len(shape) == 3:
                for b in range(shape[0]):
                    update(b)
            elif _row_chunk(shape[0]) is not None:
                _row_chunks(shape[0], update)
            else:
                update(Ellipsis)

    vm = pl.BlockSpec(memory_space=pltpu.VMEM)
    res = pl.pallas_call(
        body, name="adamw_update",
        out_shape=tuple(jax.ShapeDtypeStruct(t.shape, F32) for t in g for _ in range(4)),
        in_specs=[vm] * (4 * n), out_specs=(vm,) * (4 * n),
        compiler_params=pltpu.CompilerParams(vmem_limit_bytes=VMEM_LIMIT),
    )(*g, *w, *m, *v)
    return [res[4 * a:4 * a + 4] for a in range(n)]


TINY = (("pre_norm_g", 1, 1024), ("post_norm_g", 1, 1024), ("pl_b_gate", 1, 1024), ("ssm_d", 1, 512),
        ("ssm_b_glu", 1, 512), ("ssm_log_step", 1, 32), ("attn_sinks", 1, 8), ("ssm_lam_re", 32, 64),
        ("ssm_lam_im", 32, 64))
MEDIUM = (("ssm_b_re", SSM_G * SSM_P, SSM_N), ("ssm_b_im", SSM_G * SSM_P, SSM_N), ("ssm_c_re", SSM_G * SSM_P, SSM_N),
          ("ssm_c_im", SSM_G * SSM_P, SSM_N))


def _stage_rows():
    offs, r = {}, 0
    for name, rows, cols in TINY + (("loss", 1, 1),):
        if rows > 1:
            r = -(-r // 8) * 8
        offs[name] = r
        r += rows if rows > 1 else max(cols // LANES, 1)
    return offs, -(-r // 8) * 8


def _reduce_final(g16, gown, loss, g_tiny, g_med):
    nb_, nt, nm_ = len(g16), len(TINY), len(MEDIUM)
    offs, stage_rows = _stage_rows()

    def body(*refs):
        g16_r, go_r = refs[:nb_], refs[nb_:2 * nb_]
        base = 2 * nb_
        loss_r, gt, gm = refs[base], refs[base + 1:base + 1 + nt], refs[base + 1 + nt:base + 1 + nt + nm_]
        base += 1 + nt + nm_
        out_b = refs[base:base + nb_]
        base += nb_
        loss_o, out_t, out_m = refs[base], refs[base + 1:base + 1 + nt], refs[base + 1 + nt:base + 1 + nt + nm_]
        base += 1 + nt + nm_
        send2_b, recv1_b, recv2_b = (refs[base + i * nb_:base + (i + 1) * nb_] for i in range(3))
        base += 3 * nb_
        stage = refs[base]
        recv1, part, recv2 = (refs[base + 1 + i * nm_:base + 1 + (i + 1) * nm_] for i in range(3))
        bs_send, bs_recv, s_send, s_recv = refs[base + 1 + 3 * nm_:]
        _interleave(_big_reduce_phases(g16_r, go_r, out_b, send2_b, recv1_b, recv2_b, bs_send, bs_recv),
                    small_phases(loss_r, gt, gm, loss_o, out_t, out_m, stage, recv1, part, recv2, s_send, s_recv))

    def small_phases(loss_r, gt, gm, loss_o, out_t, out_m, stage, recv1, part, recv2, s_send, s_recv):
        x, y, c = _mesh_pos()
        me = _slot(x, y, c)
        sibling = (x, y, 1 - c)
        chips = [(1 - x, y), (x, 1 - y), (1 - x, 1 - y)]
        all_chips = [(x, y)] + chips
        peers = [sibling] + [(*chip, c) for chip in chips] + [(*chip, 1 - c) for chip in chips]
        sem = iter(range(7 + 14 * nm_))
        lvl1 = []
        for a in range(nm_):
            cps = [_remote(gm[a].at[_slot(*chip, 1 - c)], recv1[a].at[j], s_send, s_recv, next(sem), sibling)
                   for j, chip in enumerate(all_chips)]
            for cp in cps:
                cp.start()
            lvl1.append(cps)
        mine = stage.at[me]
        mine[...] = jnp.zeros((stage_rows, LANES), F32)
        for (name, rows, cols), ref in zip(TINY + (("loss", 1, 1),), gt + (loss_r,)):
            r0 = offs[name]
            if rows > 1:
                mine[r0:r0 + rows, 0:cols] = ref[...]
            elif cols >= LANES:
                for i in range(cols // LANES):
                    mine[r0 + i:r0 + i + 1, :] = ref[:, i * LANES:(i + 1) * LANES]
            else:
                mine[r0:r0 + 1, 0:cols] = ref[...]
        tiny_cps = [_remote(mine, mine, s_send, s_recv, next(sem), peer) for peer in peers]
        for cp in tiny_cps:
            cp.start()
        yield
        lvl2 = []
        for a in range(nm_):
            for cp in lvl1[a]:
                cp.wait_recv()
            for j, chip in enumerate(all_chips):
                part[a][j] = gm[a][_slot(*chip, c)] + recv1[a][j]
            cps = [_remote(part[a].at[1 + j], recv2[a].at[j], s_send, s_recv, next(sem), (*chip, c))
                   for j, chip in enumerate(chips)]
            for cp in cps:
                cp.start()
            lvl2.append(cps)
        yield
        lvl3 = []
        for a in range(nm_):
            for cp in lvl2[a]:
                cp.wait_recv()
            blk = out_m[a].at[me]
            blk[...] = ((part[a][0] + recv2[a][0]) + recv2[a][1]) + recv2[a][2]
            cps = [_remote(blk, blk, s_send, s_recv, next(sem), peer) for peer in peers]
            for cp in cps:
                cp.start()
            lvl3.append(cps)
        yield
        for cp in tiny_cps:
            cp.wait_recv()
        tot = stage[0]
        for d in range(1, N_DEV):
            tot = tot + stage[d]
        loss_o[...] = tot[offs["loss"]:offs["loss"] + 1, 0:1]
        for k, (name, rows, cols) in enumerate(TINY):
            r0 = offs[name]
            if rows > 1:
                out_t[k][...] = tot[r0:r0 + rows, 0:cols]
            elif cols >= LANES:
                for i in range(cols // LANES):
                    out_t[k][:, i * LANES:(i + 1) * LANES] = tot[r0 + i:r0 + i + 1, :]
            else:
                out_t[k][...] = tot[r0:r0 + 1, 0:cols]
        for cps in lvl3:
            for cp in cps:
                cp.wait_recv()
        for cps in lvl1 + lvl2 + lvl3 + [tiny_cps]:
            for cp in cps:
                cp.wait_send()

    vmem = pl.BlockSpec(memory_space=pltpu.VMEM)
    t_shapes = [jax.ShapeDtypeStruct((rows, cols), F32) for _, rows, cols in TINY]
    m_shapes = [jax.ShapeDtypeStruct((N_DEV, rows // N_DEV, cols), F32) for _, rows, cols in MEDIUM]
    blk = [(rows // N_DEV, cols) for _, rows, cols in MEDIUM]
    scratch = ([pltpu.VMEM((3,) + t.shape, BF16) for t in gown] + [pltpu.VMEM((4,) + t.shape, BF16) for t in gown]
               + [pltpu.VMEM((3,) + t.shape, BF16) for t in gown]
               + [pltpu.VMEM((N_DEV, stage_rows, LANES), F32)]
               + [pltpu.VMEM((4,) + b, F32) for b in blk] + [pltpu.VMEM((4,) + b, F32) for b in blk]
               + [pltpu.VMEM((3,) + b, F32) for b in blk]
               + [pltpu.SemaphoreType.DMA((7 * nb_,)), pltpu.SemaphoreType.DMA((7 * nb_,)),
                  pltpu.SemaphoreType.DMA((7 + 14 * nm_,)), pltpu.SemaphoreType.DMA((7 + 14 * nm_,))])
    n_in, n_out = 2 * nb_ + 1 + nt + nm_, nb_ + 1 + nt + nm_
    res = pl.pallas_call(
        body, name="reduce_final",
        out_shape=tuple(jax.ShapeDtypeStruct(t.shape, F32) for t in gown) + (jax.ShapeDtypeStruct((1, 1), F32),)
        + tuple(t_shapes) + tuple(m_shapes),
        in_specs=[vmem] * n_in, out_specs=(vmem,) * n_out, scratch_shapes=scratch,
        compiler_params=pltpu.CompilerParams(vmem_limit_bytes=VMEM_LIMIT),
    )(*g16, *gown, loss, *g_tiny, *g_med)
    return list(res[:nb_]), res[nb_], list(res[nb_ + 1:nb_ + 1 + nt]), list(res[nb_ + 1 + nt:])


def _gather_phases(shard_r, gath, cast, send_sems, recv_sems, local_sems):
    n = len(shard_r)
    x, y, c = _mesh_pos()
    me, sibling = (x, y, c), (x, y, 1 - c)
    chips = [(1 - x, y), (x, 1 - y), (1 - x, 1 - y)]

    def own(a, k, to):
        return _remote(cast[a], gath[a].at[_slot(*me)], send_sems, recv_sems, 7 * a + k, to)

    def passed(a, k, block, to):
        blk = gath[a].at[_slot(*block)]
        return _remote(blk, blk, send_sems, recv_sems, 7 * a + k, to)

    def keep(a):
        return pltpu.make_async_copy(cast[a], gath[a].at[_slot(*me)], local_sems.at[a])

    def start():
        for a in range(n):
            def to16(r, a=a):
                cast[a][r, :] = shard_r[a][r, :].astype(BF16)

            _row_chunks(shard_r[a].shape[0], to16)
            keep(a).start()
            own(a, 0, sibling).start()
            for j, chip in enumerate(chips):
                own(a, 1 + j, (*chip, c)).start()

    def relay():
        for a in range(n):
            for j, chip in enumerate(chips):
                passed(a, 1 + j, (*chip, c), me).wait_recv()
                passed(a, 4 + j, (*chip, c), sibling).start()

    def finish():
        for a in range(n):
            passed(a, 0, sibling, me).wait_recv()
            for j, chip in enumerate(chips):
                passed(a, 4 + j, (*chip, 1 - c), me).wait_recv()
            own(a, 0, sibling).wait_send()
            for j, chip in enumerate(chips):
                own(a, 1 + j, (*chip, c)).wait_send()
                passed(a, 4 + j, (*chip, c), sibling).wait_send()
            keep(a).wait()

    return start, relay, finish


def _gather_operands(shards):
    n = len(shards)
    return ((pl.BlockSpec(memory_space=pl.ANY),) * n,
            tuple(jax.ShapeDtypeStruct((N_DEV,) + s.shape, BF16) for s in shards),
            [pltpu.VMEM(s.shape, BF16) for s in shards]
            + [pltpu.SemaphoreType.DMA((7 * n,)), pltpu.SemaphoreType.DMA((7 * n,)), pltpu.SemaphoreType.DMA((n,))])


def _in_proj(x2, g_pre, w_in, late, bl, seg):
    t = x2.shape[0]
    tm = seg
    steps = t // tm
    n = len(late)
    forward_step, last_step = (steps * 5) // 8, steps - 1

    def body(*refs):
        x_ref, g_ref, w_ref = refs[:3]
        late_r = refs[3:3 + n]
        u_ref, zs_ref, q_ref, k_ref, v_ref, za_ref = refs[3 + n:9 + n]
        gath = refs[9 + n:9 + 2 * n]
        cast = refs[9 + 2 * n:9 + 3 * n]
        send_sems, recv_sems, local_sems = refs[9 + 3 * n:]
        i = pl.program_id(0)
        start, relay, finish = _gather_phases(late_r, gath, cast, send_sems, recv_sems, local_sems)
        pl.when(i == 0)(start)
        xv = x_ref[...]
        r = lax.rsqrt(jnp.mean(xv * xv, axis=-1, keepdims=True) + EPS)
        hn = xv * r * g_ref[...]
        proj = _mm_nt(hn, w_ref[...])
        u_ref[0] = proj[:, 0:512]
        zs_ref[...] = proj[:, 512:1024]
        q_ref[...] = proj[:, 1024:1536].astype(BF16)
        k_ref[...] = proj[:, 1536:1664].astype(BF16)
        v_ref[...] = proj[:, 1664:1792].astype(BF16)
        za_ref[...] = proj[:, 1792:2304]
        pl.when(i == forward_step)(relay)
        pl.when(i == last_step)(finish)

    row = lambda w: pl.BlockSpec((tm, w), lambda i: (i, 0))
    g_specs, g_shapes, g_scratch = _gather_operands(late)
    res = pl.pallas_call(
        body, name="in_proj", grid=(steps,),
        in_specs=[row(D_MODEL), _const_spec((1, D_MODEL)), _const_spec((D_IN, D_MODEL))]
        + [_const_spec(s.shape) for s in late],
        out_specs=(pl.BlockSpec((1, tm, D_SSM), lambda i: (i // N_SEG, 0, i % N_SEG)),
                   row(512), row(512), row(128), row(128), row(512)) + g_specs,
        out_shape=(jax.ShapeDtypeStruct((bl, seg, N_SEG * D_SSM), F32),
                   jax.ShapeDtypeStruct((t, 512), F32), jax.ShapeDtypeStruct((t, 512), BF16),
                   jax.ShapeDtypeStruct((t, 128), BF16), jax.ShapeDtypeStruct((t, 128), BF16),
                   jax.ShapeDtypeStruct((t, 512), F32)) + g_shapes,
        scratch_shapes=g_scratch,
        compiler_params=_tc_params(("arbitrary",)),
    )(x2, g_pre, w_in, *late)
    return res[:6], list(res[6:])


def _discretise(lr, li, ls):
    step = jnp.exp(ls)
    mag = jnp.exp(lr * step)
    ar = mag * jnp.cos(li * step)
    ai = mag * jnp.sin(li * step)
    den = lr * lr + li * li
    cr = ((ar - 1.0) * lr + ai * li) / den
    ci = (ai * lr - (ar - 1.0) * li) / den
    return step, ar, ai, den, cr, ci


def _per_channel(v):
    return jnp.broadcast_to(v[:, None, :], (SSM_G, SSM_P, SSM_N)).reshape(SSM_G * SSM_P, SSM_N)


def _ssm_prep(lam_re, lam_im, log_step, b_re, b_im, seg):
    def body(lr_ref, li_ref, ls_ref, br_ref, bi_ref, lrr_ref, lir_ref, lsr_ref,
             ar_ref, ai_ref, bbr_ref, bbi_ref, pr_ref, pi_ref):
        _, _, _, _, cr, ci = _discretise(lr_ref[...], li_ref[...], ls_ref[...])
        cr, ci = _per_channel(cr), _per_channel(ci)
        br, bi = br_ref[...], bi_ref[...]
        bbr_ref[...] = cr * br - ci * bi
        bbi_ref[...] = cr * bi + ci * br
        stepr = jnp.exp(lsr_ref[...])
        k = (lax.broadcasted_iota(jnp.int32, (8, N_STATE), 0) + 1).astype(F32)
        magk = jnp.exp(k * (lrr_ref[...] * stepr))
        ang = k * (lir_ref[...] * stepr)
        pr_ref[0:8, :] = magk * jnp.cos(ang)
        pi_ref[0:8, :] = magk * jnp.sin(ang)
        n = 8
        while n < seg:
            tr, ti = pr_ref[n - 1:n, :], pi_ref[n - 1:n, :]
            xr, xi = pr_ref[0:n, :], pi_ref[0:n, :]
            pr_ref[n:2 * n, :] = xr * tr - xi * ti
            pi_ref[n:2 * n, :] = xr * ti + xi * tr
            n *= 2
        ar_ref[...] = pr_ref[0:1, :]
        ai_ref[...] = pi_ref[0:1, :]

    row = jax.ShapeDtypeStruct((1, N_STATE), F32)
    mat = jax.ShapeDtypeStruct((SSM_G * SSM_P, SSM_N), F32)
    pw = jax.ShapeDtypeStruct((seg, N_STATE), F32)
    vm = pl.BlockSpec(memory_space=pltpu.VMEM)
    step_row = jnp.broadcast_to(log_step, (SSM_G, SSM_N)).reshape(1, N_STATE)
    return pl.pallas_call(
        body, name="ssm_prep", out_shape=(row, row, mat, mat, pw, pw),
        in_specs=[vm] * 8, out_specs=(vm,) * 6,
    )(lam_re, lam_im, log_step, b_re, b_im, lam_re.reshape(1, N_STATE), lam_im.reshape(1, N_STATE), step_row)


def _seg_rows(t):
    if isinstance(t, int):
        return pl.ds(t * N_SEG, N_SEG)
    return pl.ds(pl.multiple_of(t * N_SEG, N_SEG), N_SEG)


def _scan_forward(xs, a_re, a_im, pw_re, pw_im, cs, seg):
    are = jnp.broadcast_to(a_re, (N_SEG, ST_T))
    aim = jnp.broadcast_to(a_im, (N_SEG, ST_T))

    def steps(k, carry):
        xr, xi = carry
        for j in range(SCAN_UNROLL):
            r = pl.multiple_of((k * SCAN_UNROLL + j) * N_SEG, N_SEG)
            nr = are * xr - aim * xi + xs[pl.ds(r, N_SEG), 0:ST_T]
            ni = are * xi + aim * xr + xs[pl.ds(r, N_SEG), ST_T:2 * ST_T]
            xs[pl.ds(r, N_SEG), 0:ST_T] = nr
            xs[pl.ds(r, N_SEG), ST_T:2 * ST_T] = ni
            xr, xi = nr, ni
        return xr, xi

    zero = jnp.zeros((N_SEG, ST_T), F32)
    fr, fi = lax.fori_loop(0, seg // SCAN_UNROLL, steps, (zero, zero))
    sr, si = pw_re[seg - 1:seg, :], pw_im[seg - 1:seg, :]
    cr = jnp.zeros((1, ST_T), F32)
    ci = jnp.zeros((1, ST_T), F32)
    cs[0:1, :] = cr
    cs[8:9, :] = ci
    for s in range(1, N_SEG):
        ncr = sr * cr - si * ci + fr[s - 1:s, :]
        nci = sr * ci + si * cr + fi[s - 1:s, :]
        cr, ci = ncr, nci
        cs[s:s + 1, :] = cr
        cs[8 + s:9 + s, :] = ci
    car, cai = cs[0:8, :], cs[8:16, :]

    def fix(t, _):
        r = pl.multiple_of(t * N_SEG, N_SEG)
        pr, pi = pw_re[pl.ds(t, 1), :], pw_im[pl.ds(t, 1), :]
        xs[pl.ds(r, N_SEG), 0:ST_T] = xs[pl.ds(r, N_SEG), 0:ST_T] + (pr * car - pi * cai)
        xs[pl.ds(r, N_SEG), ST_T:2 * ST_T] = xs[pl.ds(r, N_SEG), ST_T:2 * ST_T] + (pr * cai + pi * car)
        return 0

    lax.fori_loop(0, seg, fix, 0, unroll=SCAN_UNROLL)


def _ssm_forward(u_perm, bcat, ccat, a_re, a_im, pw_re, pw_im, d_row, seg):
    bl, rows, _ = u_perm.shape

    def body(u_ref, b_ref, c_ref, ar_ref, ai_ref, pr_ref, pi_ref, d_ref, y_ref, xs_ref, cs_ref):
        u = u_ref[0]
        xs, cs = xs_ref.at[0, 0], cs_ref.at[0, 0]
        xs[...] = _mm(u, b_ref[0])
        _scan_forward(xs, ar_ref[...], ai_ref[...], pr_ref, pi_ref, cs, seg)
        y_ref[0] = _mm(xs[...], c_ref[0]) + d_ref[...] * u

    state = lambda r, c: pl.BlockSpec((1, 1, r, c), lambda b, j: (b, j, 0, 0))
    return pl.pallas_call(
        body, name="ssm_forward", grid=(bl, N_GT),
        in_specs=[pl.BlockSpec((1, rows, CH_T), lambda b, j: (b, 0, j)),
                  pl.BlockSpec((1, CH_T, 2 * ST_T), lambda b, j: (j, 0, 0)),
                  pl.BlockSpec((1, 2 * ST_T, CH_T), lambda b, j: (j, 0, 0)),
                  pl.BlockSpec((1, ST_T), lambda b, j: (0, j)), pl.BlockSpec((1, ST_T), lambda b, j: (0, j)),
                  pl.BlockSpec((seg, ST_T), lambda b, j: (0, j)), pl.BlockSpec((seg, ST_T), lambda b, j: (0, j)),
                  pl.BlockSpec((1, CH_T), lambda b, j: (0, j))],
        out_specs=(pl.BlockSpec((1, rows, CH_T), lambda b, j: (b, 0, j)), state(rows, 2 * ST_T), state(16, ST_T)),
        out_shape=(jax.ShapeDtypeStruct((bl, rows, D_SSM), F32),
                   jax.ShapeDtypeStruct((bl, N_GT, rows, 2 * ST_T), F32),
                   jax.ShapeDtypeStruct((bl, N_GT, 16, ST_T), F32)),
        compiler_params=_tc_params(("arbitrary", "arbitrary")),
    )(u_perm, bcat, ccat, a_re, a_im, pw_re, pw_im, d_row)


def _ssm_backward(u_perm, dy_perm, states, carries, bcat_t, ccat_t, a_re, a_im, pw_re, pw_im, d_row, seg):
    bl, rows, _ = u_perm.shape

    def body(u_ref, dy_ref, xs_ref, cs_ref, bt_ref, ct_ref, ar_ref, ai_ref, pr_ref, pi_ref, d_ref,
             du_ref, db_ref, dc_ref, dar_ref, dai_ref, dd_ref, ls, cl):
        b = pl.program_id(1)
        u = u_ref[0]
        dy = dy_ref[0]
        xs, cs = xs_ref.at[0, 0], cs_ref.at[0, 0]
        ls[...] = _mm(dy, ct_ref[0])
        are = jnp.broadcast_to(ar_ref[...], (N_SEG, ST_T))
        aim = jnp.broadcast_to(ai_ref[...], (N_SEG, ST_T))

        def steps(k, carry):
            lr, li = carry
            for j in range(SCAN_UNROLL):
                r = pl.multiple_of((seg - 1 - (k * SCAN_UNROLL + j)) * N_SEG, N_SEG)
                nr = are * lr + aim * li + ls[pl.ds(r, N_SEG), 0:ST_T]
                ni = are * li - aim * lr + ls[pl.ds(r, N_SEG), ST_T:2 * ST_T]
                ls[pl.ds(r, N_SEG), 0:ST_T] = nr
                ls[pl.ds(r, N_SEG), ST_T:2 * ST_T] = ni
                lr, li = nr, ni
            return lr, li

        zero = jnp.zeros((N_SEG, ST_T), F32)
        fr, fi = lax.fori_loop(0, seg // SCAN_UNROLL, steps, (zero, zero))
        sr, si = pr_ref[seg - 1:seg, :], pi_ref[seg - 1:seg, :]
        cr = jnp.zeros((1, ST_T), F32)
        ci = jnp.zeros((1, ST_T), F32)
        cl[7:8, :] = cr
        cl[15:16, :] = ci
        for s in range(N_SEG - 2, -1, -1):
            ncr = sr * cr + si * ci + fr[s + 1:s + 2, :]
            nci = sr * ci - si * cr + fi[s + 1:s + 2, :]
            cr, ci = ncr, nci
            cl[s:s + 1, :] = cr
            cl[8 + s:9 + s, :] = ci
        clr, cli = cl[0:8, :], cl[8:16, :]

        def fix_rows(rows, t, xpr, xpi, acc):
            dr, di = acc
            pr, pi = pr_ref[pl.ds(seg - 1 - t, 1), :], pi_ref[pl.ds(seg - 1 - t, 1), :]
            lr = ls[rows, 0:ST_T] + (pr * clr + pi * cli)
            li = ls[rows, ST_T:2 * ST_T] + (pr * cli - pi * clr)
            ls[rows, 0:ST_T] = lr
            ls[rows, ST_T:2 * ST_T] = li
            return dr + (lr * xpr + li * xpi), di + (li * xpr - lr * xpi)

        def fix_at(t, acc):
            prev = _seg_rows(t - 1)
            return fix_rows(_seg_rows(t), t, xs[prev, 0:ST_T], xs[prev, ST_T:2 * ST_T], acc)

        def fix(k, acc):
            for j in range(SCAN_UNROLL):
                acc = fix_at(k * SCAN_UNROLL + j, acc)
            return acc

        acc = fix_rows(pl.ds(0, N_SEG), 0, cs[0:8, :], cs[8:16, :], (zero, zero))
        for t in range(1, SCAN_UNROLL):
            acc = fix_at(t, acc)
        dr, di = lax.fori_loop(1, seg // SCAN_UNROLL, fix, acc)
        dar = jnp.sum(dr, axis=0, keepdims=True)
        dai = jnp.sum(di, axis=0, keepdims=True)
        lall = ls[...]
        du_ref[0] = (_mm(lall, bt_ref[0]) + d_ref[...] * dy).astype(BF16)
        dbp = _mm_tn(u, lall)
        dcp = _mm_tn(dy, xs[...])
        ddp = jnp.sum(dy * u, axis=0, keepdims=True)

        @pl.when(b == 0)
        def _():
            db_ref[0] = dbp
            dc_ref[0] = dcp
            dar_ref[...] = dar
            dai_ref[...] = dai
            dd_ref[...] = ddp

        @pl.when(b != 0)
        def _():
            db_ref[0] += dbp
            dc_ref[0] += dcp
            dar_ref[...] += dar
            dai_ref[...] += dai
            dd_ref[...] += ddp

    tile3 = lambda r, c: pl.BlockSpec((1, r, c), lambda j, b: (j, 0, 0))
    lane = lambda r, c: pl.BlockSpec((r, c), lambda j, b: (0, j))
    act = pl.BlockSpec((1, rows, CH_T), lambda j, b: (b, 0, j))
    state = lambda r, c: pl.BlockSpec((1, 1, r, c), lambda j, b: (b, j, 0, 0))
    return pl.pallas_call(
        body, name="ssm_backward", grid=(N_GT, bl),
        in_specs=[act, act, state(rows, 2 * ST_T), state(16, ST_T), tile3(2 * ST_T, CH_T), tile3(CH_T, 2 * ST_T),
                  lane(1, ST_T), lane(1, ST_T), lane(seg, ST_T), lane(seg, ST_T), lane(1, CH_T)],
        out_specs=(act, tile3(CH_T, 2 * ST_T), tile3(CH_T, 2 * ST_T), lane(1, ST_T), lane(1, ST_T), lane(1, CH_T)),
        out_shape=(jax.ShapeDtypeStruct((bl, rows, D_SSM), BF16),
                   jax.ShapeDtypeStruct((N_GT, CH_T, 2 * ST_T), F32), jax.ShapeDtypeStruct((N_GT, CH_T, 2 * ST_T), F32),
                   jax.ShapeDtypeStruct((1, N_STATE), F32), jax.ShapeDtypeStruct((1, N_STATE), F32),
                   jax.ShapeDtypeStruct((1, D_SSM), F32)),
        scratch_shapes=[pltpu.VMEM((rows, 2 * ST_T), F32), pltpu.VMEM((16, ST_T), F32)],
        compiler_params=_tc_params(("arbitrary", "arbitrary")),
    )(u_perm, dy_perm, states, carries, bcat_t, ccat_t, a_re, a_im, pw_re, pw_im, d_row)


def _ssm_param_grads(lam_re, lam_im, log_step, b_re, b_im, da_re, da_im, dbb_re, dbb_im):
    def body(lr_ref, li_ref, ls_ref, br_ref, bi_ref, gar_ref, gai_ref, gbr_ref, gbi_ref,
             dlr_ref, dli_ref, dls_ref, dbr_ref, dbi_ref):
        lr, li = lr_ref[...], li_ref[...]
        step, ar, ai, den, cr, ci = _discretise(lr, li, ls_ref[...])
        crb, cib = _per_channel(cr), _per_channel(ci)
        br, bi = br_ref[...], bi_ref[...]
        gbr, gbi = gbr_ref[...], gbi_ref[...]
        dbr_ref[...] = crb * gbr + cib * gbi
        dbi_ref[...] = crb * gbi - cib * gbr
        over_channels = lambda t: jnp.sum(t.reshape(SSM_G, SSM_P, SSM_N), axis=1)
        gcr = over_channels(br * gbr + bi * gbi)
        gci = over_channels(br * gbi - bi * gbr)
        ilr, ili = lr / den, -li / den
        gar = gar_ref[...] + (ilr * gcr + ili * gci)
        gai = gai_ref[...] + (ilr * gci - ili * gcr)
        qr, qi = cr * ilr - ci * ili, cr * ili + ci * ilr
        glr = -(qr * gcr + qi * gci)
        gli = -(qr * gci - qi * gcr)
        gwr = ar * gar + ai * gai
        gwi = ar * gai - ai * gar
        dlr_ref[...] = glr + step * gwr
        dli_ref[...] = gli + step * gwi
        dls_ref[...] = jnp.sum(lr * gwr + li * gwi, axis=-1, keepdims=True) * step

    lam = jax.ShapeDtypeStruct((SSM_G, SSM_N), F32)
    mat = jax.ShapeDtypeStruct((SSM_G * SSM_P, SSM_N), F32)
    vm = pl.BlockSpec(memory_space=pltpu.VMEM)
    return pl.pallas_call(
        body, name="ssm_param_grads", out_shape=(lam, lam, jax.ShapeDtypeStruct((SSM_G, 1), F32), mat, mat),
        in_specs=[vm] * 9, out_specs=(vm,) * 5,
    )(lam_re, lam_im, log_step, b_re, b_im, da_re, da_im, dbb_re, dbb_im)


ROWS4 = Q_PER_KV * ATT_BLOCK


def _att_dist_mask(first_block):
    qi = lax.broadcasted_iota(jnp.int32, (ROWS4, 2 * ATT_BLOCK), 0) & (ATT_BLOCK - 1)
    si = lax.broadcasted_iota(jnp.int32, (ROWS4, 2 * ATT_BLOCK), 1)
    dist = qi + ATT_BLOCK - si
    valid = (dist >= 0) & (dist < ATT_BLOCK) & ((si >= ATT_BLOCK) | jnp.logical_not(first_block))
    return dist.astype(F32), valid


def _stack_heads(x, kv):
    return jnp.concatenate([x[:, (kv * Q_PER_KV + g) * HEAD_DIM:(kv * Q_PER_KV + g + 1) * HEAD_DIM]
                            for g in range(Q_PER_KV)], axis=0)


def _stack_cols(x, kv):
    return jnp.concatenate([x[:, kv * Q_PER_KV + g:kv * Q_PER_KV + g + 1] for g in range(Q_PER_KV)], axis=0)


def _per_head_col(vals):
    return jnp.concatenate([jnp.full((ATT_BLOCK, 1), v, F32) for v in vals], axis=0)


def _attn_forward(q, k, v, sinks, late, bl, nb):
    t = q.shape[0]
    n = len(late)
    steps = bl * nb

    def body(*refs):
        sink_ref, q_ref, kp_ref, kc_ref, vp_ref, vc_ref = refs[:6]
        late_r = refs[6:6 + n]
        o_ref, lse_ref = refs[6 + n:8 + n]
        gath = refs[8 + n:8 + 2 * n]
        cast = refs[8 + 2 * n:8 + 3 * n]
        send_sems, recv_sems, local_sems = refs[8 + 3 * n:]
        i = pl.program_id(1)
        step = pl.program_id(0) * nb + i
        start, relay, finish = _gather_phases(late_r, gath, cast, send_sems, recv_sems, local_sems)
        pl.when(step == 0)(start)
        pl.when(step == (steps * 5) // 8)(relay)
        pl.when(step == steps - 1)(finish)
        dist4, valid4 = _att_dist_mask(i == 0)
        dist, valid = dist4[0:ATT_BLOCK, :], valid4[0:ATT_BLOCK, :]
        kk = jnp.concatenate([kp_ref[...], kc_ref[...]], axis=0)
        vv = jnp.concatenate([vp_ref[...], vc_ref[...]], axis=0)
        qv = q_ref[...]
        for h in range(N_HEADS):
            kv = h // Q_PER_KV
            slope = 2.0 ** (-(h + 1))
            qh = qv[:, h * HEAD_DIM:(h + 1) * HEAD_DIM]
            kh = kk[:, kv * HEAD_DIM:(kv + 1) * HEAD_DIM]
            vh = vv[:, kv * HEAD_DIM:(kv + 1) * HEAD_DIM]
            s = _mm_nt(qh, kh) * ATT_SCALE - slope * dist
            s = jnp.where(valid, s, NEG_BIG)
            sink = sink_ref[h]
            m = jnp.maximum(jnp.max(s, axis=-1, keepdims=True), sink)
            e = jnp.exp(s - m)
            den = jnp.sum(e, axis=-1, keepdims=True) + jnp.exp(sink - m)
            o_ref[:, h * HEAD_DIM:(h + 1) * HEAD_DIM] = _mm(e, vh) * (1.0 / den)
            lse_ref[:, h:h + 1] = m + jnp.log(den)

    cur = lambda w: pl.BlockSpec((ATT_BLOCK, w), lambda b, i: (b * nb + i, 0))
    prev = lambda w: pl.BlockSpec((ATT_BLOCK, w), lambda b, i: (b * nb + jnp.maximum(i - 1, 0), 0))
    g_specs, g_shapes, g_scratch = _gather_operands(late)
    res = pl.pallas_call(
        body, name="attn_forward", grid=(bl, nb),
        in_specs=[pl.BlockSpec(memory_space=pltpu.SMEM), cur(512), prev(128), cur(128), prev(128), cur(128)]
        + [pl.BlockSpec(s.shape, lambda b, i: (0, 0)) for s in late],
        out_specs=(cur(512), cur(N_HEADS)) + g_specs,
        out_shape=(jax.ShapeDtypeStruct((t, D_ATTN), F32), jax.ShapeDtypeStruct((t, N_HEADS), F32)) + g_shapes,
        scratch_shapes=g_scratch,
        compiler_params=_tc_params(("arbitrary", "arbitrary")),
    )(sinks, q, k, k, v, v, *late)
    return res[:2], list(res[2:])


def _attn_backward(q, k, v, o, do, lse, sinks, late16, late_own, bl, nb):
    t = q.shape[0]
    n = len(late16)
    steps = bl * nb
    mid1, mid2, last = steps // 4, (steps * 3) // 4, steps - 1

    def body(*refs):
        (sink_ref, qc_ref, qn_ref, kp_ref, kc_ref, vp_ref, vc_ref, oc_ref, on_ref, doc_ref, don_ref,
         lc_ref, ln_ref) = refs[:13]
        g16_r, go_r = refs[13:13 + n], refs[13 + n:13 + 2 * n]
        dq_ref, dk_ref, dv_ref, ds_ref = refs[13 + 2 * n:17 + 2 * n]
        red = refs[17 + 2 * n:17 + 3 * n]
        own16, recv1, send2, recv2 = (refs[17 + 3 * n + k * n:17 + 3 * n + (k + 1) * n] for k in range(4))
        s_send, s_recv, s_local = refs[17 + 7 * n:]
        b, i = pl.program_id(0), pl.program_id(1)
        step = b * nb + i
        x, y, c = _mesh_pos()
        sibling = (x, y, 1 - c)
        chips = [(1 - x, y), (x, 1 - y), (1 - x, 1 - y)]
        all_chips = [(x, y)] + chips

        def lvl1(a, j):
            return _remote(g16_r[a].at[_slot(*all_chips[j], 1 - c)], recv1[a].at[j], s_send, s_recv, 7 * a + j, sibling)

        def lvl2(a, j):
            return _remote(send2[a].at[j], recv2[a].at[j], s_send, s_recv, 7 * a + 4 + j, (*chips[j], c))

        def mine(a, j):
            return pltpu.make_async_copy(g16_r[a].at[_slot(*chips[j], c)], own16[a].at[j], s_local.at[3 * a + j])

        @pl.when(step == 0)
        def _():
            for a in range(n):
                for j in range(3):
                    mine(a, j).start()
                for j in range(4):
                    lvl1(a, j).start()

        @pl.when(step == mid1)
        def _():
            for a in range(n):
                for j in range(3):
                    mine(a, j).wait()
                for j in range(4):
                    lvl1(a, j).wait_recv()

                def partials(r, a=a):
                    red[a][r, :] = go_r[a][r, :] + recv1[a][0, r, :].astype(F32)
                    for j in range(3):
                        send2[a][j, r, :] = (own16[a][j, r, :].astype(F32)
                                             + recv1[a][1 + j, r, :].astype(F32)).astype(BF16)

                _row_chunks(go_r[a].shape[0], partials)
                for j in range(3):
                    lvl2(a, j).start()

        @pl.when(step == mid2)
        def _():
            for a in range(n):
                for j in range(3):
                    lvl2(a, j).wait_recv()

                def total(r, a=a):
                    g = red[a][r, :]
                    for j in range(3):
                        g = g + recv2[a][j, r, :].astype(F32)
                    red[a][r, :] = g

                _row_chunks(go_r[a].shape[0], total)

        @pl.when(step == last)
        def _():
            for a in range(n):
                for j in range(4):
                    lvl1(a, j).wait_send()
                for j in range(3):
                    lvl2(a, j).wait_send()

        dist, valid = _att_dist_mask(i == 0)
        has_next = i + 1 < nb
        dist_n = dist[:, 0:ATT_BLOCK]
        valid_n = (dist_n < ATT_BLOCK) & has_next
        kk = jnp.concatenate([kp_ref[...], kc_ref[...]], axis=0)
        vv = jnp.concatenate([vp_ref[...], vc_ref[...]], axis=0)
        qc, qn = qc_ref[...], qn_ref[...]
        oc, on = oc_ref[...], on_ref[...]
        doc, don = doc_ref[...], don_ref[...]
        lc, ln = lc_ref[...], ln_ref[...]
        dsink_cols = []
        for kv in range(KV_HEADS):
            heads = range(kv * Q_PER_KV, (kv + 1) * Q_PER_KV)
            kh = kk[:, kv * HEAD_DIM:(kv + 1) * HEAD_DIM]
            vh = vv[:, kv * HEAD_DIM:(kv + 1) * HEAD_DIM]
            khc, vhc = kh[ATT_BLOCK:, :], vh[ATT_BLOCK:, :]
            slope = _per_head_col([2.0 ** (-(h + 1)) for h in heads])
            sink = _per_head_col([sink_ref[h] for h in heads])
            q4, do4 = _stack_heads(qc, kv), _stack_heads(doc, kv)
            delta = jnp.sum(do4 * _stack_heads(oc, kv), axis=-1, keepdims=True)
            lse4 = _stack_cols(lc, kv)
            s = _mm_nt(q4, kh) * ATT_SCALE - slope * dist
            p = jnp.where(valid, jnp.exp(s - lse4), 0.0)
            dsc = p * (_mm_nt(do4, vh) - delta)
            dq4 = _mm(dsc, kh) * ATT_SCALE
            dk_acc = _mm_tn(dsc[:, ATT_BLOCK:], q4)
            dv_acc = _mm_tn(p[:, ATT_BLOCK:], do4)
            dsink4 = jnp.exp(sink - lse4) * delta
            q4n, do4n = _stack_heads(qn, kv), _stack_heads(don, kv)
            delta_n = jnp.sum(do4n * _stack_heads(on, kv), axis=-1, keepdims=True)
            s2 = _mm_nt(q4n, khc) * ATT_SCALE - slope * dist_n
            p2 = jnp.where(valid_n, jnp.exp(s2 - _stack_cols(ln, kv)), 0.0)
            ds2 = p2 * (_mm_nt(do4n, vhc) - delta_n)
            dk_acc += _mm_tn(ds2, q4n)
            dv_acc += _mm_tn(p2, do4n)
            dk_ref[:, kv * HEAD_DIM:(kv + 1) * HEAD_DIM] = dk_acc * ATT_SCALE
            dv_ref[:, kv * HEAD_DIM:(kv + 1) * HEAD_DIM] = dv_acc
            for g, h in enumerate(heads):
                rows = slice(g * ATT_BLOCK, (g + 1) * ATT_BLOCK)
                dq_ref[:, h * HEAD_DIM:(h + 1) * HEAD_DIM] = dq4[rows, :]
                dsink_cols.append(-jnp.sum(dsink4[rows, :], axis=0, keepdims=True))
        dsink = jnp.concatenate(dsink_cols, axis=1)

        @pl.when((b == 0) & (i == 0))
        def _():
            ds_ref[...] = dsink

        @pl.when((b != 0) | (i != 0))
        def _():
            ds_ref[...] += dsink

    cur = lambda w: pl.BlockSpec((ATT_BLOCK, w), lambda b, i: (b * nb + i, 0))
    prev = lambda w: pl.BlockSpec((ATT_BLOCK, w), lambda b, i: (b * nb + jnp.maximum(i - 1, 0), 0))
    nxt = lambda w: pl.BlockSpec((ATT_BLOCK, w), lambda b, i: (b * nb + jnp.minimum(i + 1, nb - 1), 0))
    const2 = lambda s: pl.BlockSpec(s, lambda b, i: (0, 0))
    shard = [s.shape for s in late_own]
    res = pl.pallas_call(
        body, name="attn_backward", grid=(bl, nb),
        in_specs=[pl.BlockSpec(memory_space=pltpu.SMEM), cur(512), nxt(512), prev(128), cur(128), prev(128), cur(128),
                  cur(512), nxt(512), cur(512), nxt(512), cur(N_HEADS), nxt(N_HEADS)]
        + [pl.BlockSpec(memory_space=pl.ANY)] * n + [const2(s) for s in shard],
        out_specs=(cur(512), cur(128), cur(128), const2((1, N_HEADS))) + tuple(const2(s) for s in shard),
        out_shape=(jax.ShapeDtypeStruct((t, D_ATTN), F32), jax.ShapeDtypeStruct((t, 128), F32),
                   jax.ShapeDtypeStruct((t, 128), F32), jax.ShapeDtypeStruct((1, N_HEADS), F32))
        + tuple(jax.ShapeDtypeStruct(s, F32) for s in shard),
        scratch_shapes=[pltpu.VMEM((3,) + s, BF16) for s in shard] + [pltpu.VMEM((4,) + s, BF16) for s in shard]
        + [pltpu.VMEM((3,) + s, BF16) for s in shard] + [pltpu.VMEM((3,) + s, BF16) for s in shard]
        + [pltpu.SemaphoreType.DMA((7 * n,)), pltpu.SemaphoreType.DMA((7 * n,)), pltpu.SemaphoreType.DMA((3 * n,))],
        compiler_params=_tc_params(("arbitrary", "arbitrary")),
    )(sinks, q, q, k, k, v, v, o, o, do, do, lse, lse, *late16, *late_own)
    return res[:4], list(res[4:])


def _mix_forward_backward(x2, y_perm, z_ssm, attn, z_attn, p2, target2, w_glu, b_glu, w_out, g_post, w_gate, b_gate,
                          w_proj, bl, seg):
    t = x2.shape[0]
    tm = seg

    def body(x_ref, y_ref, zs_ref, at_ref, za_ref, p_ref, tg_ref,
             wglu_ref, bglu_ref, wout_ref, gpost_ref, wgate_ref, bgate_ref, wproj_ref,
             loss_ref, dh1_ref, dy_ref, dzs_ref, dat_ref, dza_ref,
             dwglu_ref, dbglu_ref, dwout_ref, dgpost_ref, dwgate_ref, dbgate_ref, dwproj_ref,
             dwout16_ref, dwgate16_ref, dwproj16_ref, dwglu16_ref):
        i = pl.program_id(0)
        y = y_ref[0]
        u3 = GELU_C * (y + GELU_K * y * y * y)
        th = jnp.tanh(u3)
        gl = 0.5 * y * (1.0 + th)
        a = _mm(gl, wglu_ref[...]) + bglu_ref[...]
        sa = _sigmoid(a)
        glu = gl * sa
        zs = zs_ref[...]
        sgs = _sigmoid(zs)
        ssm_out = glu * (zs * sgs)
        za = za_ref[...]
        sga = _sigmoid(za)
        at = at_ref[...]
        attn_out = at * (za * sga)
        cat = jnp.concatenate([ssm_out, attn_out], axis=-1).astype(BF16)
        mixed = _mm(cat, wout_ref[...])
        r2 = lax.rsqrt(jnp.mean(mixed * mixed, axis=-1, keepdims=True) + EPS)
        nhat = mixed * r2
        gpost = gpost_ref[...]
        h1 = x_ref[...] + nhat * gpost
        gate = _sigmoid(_mm(h1, wgate_ref[...]) + bgate_ref[...])
        pv = p_ref[...]
        pp = _mm(pv, wproj_ref[...])
        h2 = h1 + gate * pp
        err = h2 - tg_ref[...]
        loss_part = jnp.sum(jnp.sum(err * err, axis=-1, keepdims=True), axis=0, keepdims=True) * (0.5 / D_MODEL)
        dh2 = err * (1.0 / D_MODEL)
        dgp = dh2 * pp * gate * (1.0 - gate)
        dpp = dh2 * gate
        dh1 = dh2 + _mm_nt(dgp, wgate_ref[...])
        dh1_ref[...] = dh1
        dnhat = dh1 * gpost
        dmixed = r2 * (dnhat - nhat * jnp.mean(dnhat * nhat, axis=-1, keepdims=True))
        dcat = _mm_nt(dmixed, wout_ref[...])
        dso, dao = dcat[:, 0:D_SSM], dcat[:, D_SSM:]
        dat_ref[...] = dao * (za * sga)
        dza_ref[...] = (dao * at * (sga * (1.0 + za * (1.0 - sga)))).astype(BF16)
        dzs_ref[...] = (dso * glu * (sgs * (1.0 + zs * (1.0 - sgs)))).astype(BF16)
        dglu = dso * (zs * sgs)
        da = dglu * gl * sa * (1.0 - sa)
        dgl = dglu * sa + _mm_nt(da, wglu_ref[...])
        dgelu = 0.5 * (1.0 + th) + 0.5 * y * (1.0 - th * th) * (GELU_C * (1.0 + 3.0 * GELU_K * y * y))
        dy_ref[0] = dgl * dgelu
        parts = (
            (dwglu_ref, _mm_tn(gl, da)), (dbglu_ref, jnp.sum(da, axis=0, keepdims=True)),
            (dwout_ref, _mm_tn(cat, dmixed)), (dgpost_ref, jnp.sum(dh1 * nhat, axis=0, keepdims=True)),
            (dwgate_ref, _mm_tn(h1, dgp)), (dbgate_ref, jnp.sum(dgp, axis=0, keepdims=True)),
            (dwproj_ref, _mm_tn(pv, dpp)), (loss_ref, loss_part),
        )

        @pl.when(i == 0)
        def _():
            for ref, val in parts:
                ref[...] = val

        @pl.when(i != 0)
        def _():
            for ref, val in parts:
                ref[...] += val

        @pl.when(i == t // tm - 1)
        def _():
            for ref16, ref in ((dwout16_ref, dwout_ref), (dwgate16_ref, dwgate_ref), (dwproj16_ref, dwproj_ref),
                               (dwglu16_ref, dwglu_ref)):
                def to16(r, ref16=ref16, ref=ref):
                    ref16[r, :] = ref[r, :].astype(BF16)

                _row_chunks(ref.shape[0], to16)

    row = lambda w: pl.BlockSpec((tm, w), lambda i: (i, 0))
    perm = pl.BlockSpec((1, tm, D_SSM), lambda i: (i // N_SEG, 0, i % N_SEG))
    perm_shape = jax.ShapeDtypeStruct((bl, seg, N_SEG * D_SSM), F32)
    acc = lambda r, c, dt=F32: (_const_spec((r, c)), jax.ShapeDtypeStruct((r, c), dt))
    accs = [acc(D_SSM, D_SSM), acc(1, D_SSM), acc(D_MODEL, D_MODEL), acc(1, D_MODEL), acc(D_MODEL, D_MODEL),
            acc(1, D_MODEL), acc(D_PLE, D_MODEL),
            acc(D_MODEL, D_MODEL, BF16), acc(D_MODEL, D_MODEL, BF16), acc(D_PLE, D_MODEL, BF16), acc(D_SSM, D_SSM, BF16)]
    return pl.pallas_call(
        body, name="mix_forward_backward", grid=(t // tm,),
        in_specs=[row(D_MODEL), perm, row(512), row(512), row(512), row(D_PLE), row(D_MODEL),
                  _const_spec((D_SSM, D_SSM)), _const_spec((1, D_SSM)), _const_spec((D_MODEL, D_MODEL)),
                  _const_spec((1, D_MODEL)), _const_spec((D_MODEL, D_MODEL)), _const_spec((1, D_MODEL)),
                  _const_spec((D_PLE, D_MODEL))],
        out_specs=(_const_spec((1, 1)), row(D_MODEL), perm, row(512), row(512), row(512)) + tuple(a[0] for a in accs),
        out_shape=(jax.ShapeDtypeStruct((1, 1), F32), jax.ShapeDtypeStruct((t, D_MODEL), F32), perm_shape,
                   jax.ShapeDtypeStruct((t, 512), BF16), jax.ShapeDtypeStruct((t, 512), F32),
                   jax.ShapeDtypeStruct((t, 512), BF16)) + tuple(a[1] for a in accs),
        compiler_params=_tc_params(("arbitrary",)),
    )(x2, y_perm, z_ssm, attn, z_attn, p2, target2, w_glu, b_glu, w_out, g_post, w_gate, b_gate, w_proj)


def _in_backward(x2, dh1, du_perm, dz_ssm, dq, dk, dv, dz_attn, g_pre, w_in, bl, seg):
    t = x2.shape[0]
    tm = seg

    def body(x_ref, dh1_ref, du_ref, dzs_ref, dq_ref, dk_ref, dv_ref, dza_ref, g_ref, w_ref,
             gx_ref, dw_ref, dg_ref, dw16_ref):
        i = pl.program_id(0)
        xv = x_ref[...]
        r = lax.rsqrt(jnp.mean(xv * xv, axis=-1, keepdims=True) + EPS)
        xhat = xv * r
        g = g_ref[...]
        hn = (xhat * g).astype(BF16)
        dproj = jnp.concatenate([du_ref[0].astype(BF16), dzs_ref[...].astype(BF16), dq_ref[...].astype(BF16),
                                 dk_ref[...].astype(BF16), dv_ref[...].astype(BF16), dza_ref[...].astype(BF16)],
                                axis=-1)
        dhn = _mm(dproj, w_ref[...])
        dxhat = dhn * g
        gx_ref[...] = dh1_ref[...] + r * (dxhat - xhat * jnp.mean(dxhat * xhat, axis=-1, keepdims=True))
        dwp = _mm_tn(dproj, hn)
        dgp = jnp.sum(dhn * xhat, axis=0, keepdims=True)

        @pl.when(i == 0)
        def _():
            dw_ref[...] = dwp
            dg_ref[...] = dgp

        @pl.when(i != 0)
        def _():
            dw_ref[...] += dwp
            dg_ref[...] += dgp

        @pl.when(i == t // tm - 1)
        def _():
            def to16(r):
                dw16_ref[r, :] = dw_ref[r, :].astype(BF16)

            _row_chunks(D_IN, to16)

    row = lambda w: pl.BlockSpec((tm, w), lambda i: (i, 0))
    perm = pl.BlockSpec((1, tm, D_SSM), lambda i: (i // N_SEG, 0, i % N_SEG))
    return pl.pallas_call(
        body, name="in_backward", grid=(t // tm,),
        in_specs=[row(D_MODEL), row(D_MODEL), perm, row(512), row(512), row(128), row(128), row(512),
                  _const_spec((1, D_MODEL)), _const_spec((D_IN, D_MODEL))],
        out_specs=(row(D_MODEL), _const_spec((D_IN, D_MODEL)), _const_spec((1, D_MODEL)),
                   _const_spec((D_IN, D_MODEL))),
        out_shape=(jax.ShapeDtypeStruct((t, D_MODEL), F32), jax.ShapeDtypeStruct((D_IN, D_MODEL), F32),
                   jax.ShapeDtypeStruct((1, D_MODEL), F32), jax.ShapeDtypeStruct((D_IN, D_MODEL), BF16)),
        compiler_params=_tc_params(("arbitrary",)),
    )(x2, dh1, du_perm, dz_ssm, dq, dk, dv, dz_attn, g_pre, w_in)


def _block_diag(t):
    a, b = t.shape[1], t.shape[2]
    eye = jnp.eye(G_TILE, dtype=t.dtype)
    t = t.reshape(N_GT, G_TILE, a, 1, b) * eye[None, :, None, :, None]
    return t.reshape(N_GT, G_TILE * a, G_TILE * b)


def _diag_blocks(m, a, b):
    m = m.reshape(N_GT, G_TILE, a, G_TILE, b)
    return jnp.einsum("tgagb->tgab", m).reshape(SSM_G, a, b)


def _local_step(x, p, target, pre_norm_g, w_in, ssm_lam_re, ssm_lam_im, ssm_log_step, ssm_b_re, ssm_b_im, ssm_c_re,
                ssm_c_im, ssm_d, ssm_b_glu, attn_sinks, post_norm_g, pl_b_gate, late, me):
    bl, seq, _ = x.shape
    seg = seq // N_SEG
    nb = seq // ATT_BLOCK
    t = bl * seq
    x2 = x.reshape(t, D_MODEL)
    p2 = p.reshape(t, D_PLE)
    tg2 = target.reshape(t, D_MODEL)

    lam_re, lam_im = ssm_lam_re, ssm_lam_im
    log_step = ssm_log_step.reshape(SSM_G, 1)
    a_re_row, a_im_row, bb_re, bb_im, pw_re, pw_im = _ssm_prep(lam_re, lam_im, log_step, ssm_b_re, ssm_b_im, seg)
    by_group = lambda t: t.reshape(SSM_G, SSM_P, SSM_N)
    bcat = jnp.concatenate([_block_diag(by_group(bb_re)), _block_diag(by_group(bb_im))], axis=-1).astype(BF16)
    ccat_t = jnp.concatenate([_block_diag(by_group(ssm_c_re)), -_block_diag(by_group(ssm_c_im))],
                             axis=-1).astype(BF16)
    bcat_t = jnp.swapaxes(bcat, 1, 2)
    ccat = jnp.swapaxes(ccat_t, 1, 2)
    d_row = ssm_d.reshape(1, D_SSM)

    (u_perm, z_ssm, q, k, v, z_attn), (g_out, g_glu) = _in_proj(
        x2, pre_norm_g.reshape(1, D_MODEL), w_in, [late[0], late[3]], bl, seg)
    u_perm = u_perm.reshape(bl, seq, D_SSM)
    y_perm, states, carries = _ssm_forward(u_perm, bcat, ccat, a_re_row, a_im_row, pw_re, pw_im, d_row, seg)
    sinks = attn_sinks.reshape(N_HEADS)
    (attn, lse), (g_gate, g_proj) = _attn_forward(q, k, v, sinks, [late[1], late[2]], bl, nb)
    w_out, w_gate, w_proj, w_glu = (_gathered_to_full(n, g) for n, g in zip(LATE_NAMES, (g_out, g_gate, g_proj, g_glu)))
    (loss, dh1, dy_perm, dz_ssm, dattn, dz_attn, d_w_glu, d_b_glu, d_w_out, d_g_post, d_w_gate, d_b_gate,
     d_w_proj, *late16) = _mix_forward_backward(
        x2, y_perm.reshape(bl, seg, N_SEG * D_SSM), z_ssm, attn, z_attn, p2, tg2, w_glu,
        ssm_b_glu.reshape(1, D_SSM), w_out, post_norm_g.reshape(1, D_MODEL), w_gate, pl_b_gate.reshape(1, D_MODEL),
        w_proj, bl, seg)
    owned = [_full_to_owned(n, d) for n, d in zip(LATE_NAMES, (d_w_out, d_w_gate, d_w_proj, d_w_glu))]
    (dq, dk, dv, d_sinks), late_grads = _attn_backward(
        q, k, v, attn, dattn, lse, sinks, [_full_to_owned(n, d) for n, d in zip(LATE_NAMES, late16)],
        [lax.dynamic_index_in_dim(o, me, axis=0, keepdims=False) for o in owned], bl, nb)
    du_perm, d_bcat, d_ccat_t, da_re, da_im, d_d = _ssm_backward(
        u_perm, dy_perm.reshape(bl, seq, D_SSM), states, carries, bcat_t, ccat_t, a_re_row, a_im_row, pw_re, pw_im,
        d_row, seg)
    grad_x, d_w_in, d_g_pre, d_w_in16 = _in_backward(
        x2, dh1, du_perm.reshape(bl, seg, N_SEG * D_SSM), dz_ssm, dq, dk, dv, dz_attn,
        pre_norm_g.reshape(1, D_MODEL), w_in, bl, seg)
    flat = lambda t: t.reshape(SSM_G * SSM_P, SSM_N)
    d_lam_re, d_lam_im, d_ls, d_b_re, d_b_im = _ssm_param_grads(
        lam_re, lam_im, log_step, ssm_b_re, ssm_b_im, da_re.reshape(SSM_G, SSM_N), da_im.reshape(SSM_G, SSM_N),
        flat(_diag_blocks(d_bcat[:, :, 0:ST_T], SSM_P, SSM_N)), flat(_diag_blocks(d_bcat[:, :, ST_T:], SSM_P, SSM_N)))
    grads = {
        "pre_norm_g": d_g_pre, "w_in": d_w_in, "w_in16": d_w_in16, "ssm_lam_re": d_lam_re, "ssm_lam_im": d_lam_im,
        "ssm_log_step": d_ls, "ssm_b_re": d_b_re, "ssm_b_im": d_b_im,
        "ssm_c_re": _diag_blocks(d_ccat_t[:, :, 0:ST_T], SSM_P, SSM_N),
        "ssm_c_im": -_diag_blocks(d_ccat_t[:, :, ST_T:], SSM_P, SSM_N),
        "ssm_d": d_d, "ssm_b_glu": d_b_glu, "attn_sinks": d_sinks, "post_norm_g": d_g_post, "pl_b_gate": d_b_gate,
    }
    return loss, grad_x.reshape(bl, seq, D_MODEL), grads, late_grads


LATE_NAMES = ("w_out", "pl_w_gate", "pl_w_proj", "ssm_w_glu")
BIG_NAMES = ("w_in",) + LATE_NAMES
COL_SHARDED = {"w_in": D_IN // N_DEV, "pl_w_proj": D_MODEL // N_DEV}
WEIGHT_NAMES = ("pre_norm_g", "w_in", "ssm_lam_re", "ssm_lam_im", "ssm_log_step", "ssm_b_re", "ssm_b_im", "ssm_c_re",
                "ssm_c_im", "ssm_d", "ssm_w_glu", "ssm_b_glu", "attn_sinks", "w_out", "post_norm_g", "pl_w_proj",
                "pl_w_gate", "pl_b_gate")


TRANSPOSED = {"w_in": (0, 1), "ssm_b_re": (1, 2), "ssm_b_im": (1, 2)}


def _kernel_form(name, a):
    a = a[0]
    if name in TRANSPOSED:
        a = jnp.swapaxes(a, *TRANSPOSED[name])
    if name in ("ssm_b_re", "ssm_b_im", "ssm_c_re", "ssm_c_im"):
        a = a.reshape(SSM_G * SSM_P, SSM_N)
    return a


def _given_form(name, a, shape):
    if name in TRANSPOSED:
        i, j = TRANSPOSED[name]
        swapped = list(shape[1:])
        swapped[i], swapped[j] = swapped[j], swapped[i]
        return jnp.swapaxes(a.reshape(swapped), i, j).reshape(shape)
    return a.reshape(shape)


def _gathered_to_full(name, g):
    _, rows, cols = g.shape
    if name in COL_SHARDED:
        return jnp.swapaxes(g, 0, 1).reshape(rows, N_DEV * cols)
    return g.reshape(N_DEV * rows, cols)


def _full_to_owned(name, full):
    if name in COL_SHARDED:
        return jnp.swapaxes(full.reshape(full.shape[0], N_DEV, COL_SHARDED[name]), 0, 1)
    return full.reshape(N_DEV, full.shape[0] // N_DEV, full.shape[1])


def kernel(x, p, pre_norm_g, w_in, ssm_lam_re, ssm_lam_im, ssm_log_step, ssm_b_re, ssm_b_im, ssm_c_re, ssm_c_im, ssm_d, ssm_w_glu, ssm_b_glu, attn_sinks, w_out, post_norm_g, pl_w_proj, pl_w_gate, pl_b_gate, loss_target, m_pre_norm_g, m_w_in, m_ssm_lam_re, m_ssm_lam_im, m_ssm_log_step, m_ssm_b_re, m_ssm_b_im, m_ssm_c_re, m_ssm_c_im, m_ssm_d, m_ssm_w_glu, m_ssm_b_glu, m_attn_sinks, m_w_out, m_post_norm_g, m_pl_w_proj, m_pl_w_gate, m_pl_b_gate, v_pre_norm_g, v_w_in, v_ssm_lam_re, v_ssm_lam_im, v_ssm_log_step, v_ssm_b_re, v_ssm_b_im, v_ssm_c_re, v_ssm_c_im, v_ssm_d, v_ssm_w_glu, v_ssm_b_glu, v_attn_sinks, v_w_out, v_post_norm_g, v_pl_w_proj, v_pl_w_gate, v_pl_b_gate):
    w = dict(pre_norm_g=pre_norm_g, w_in=w_in, ssm_lam_re=ssm_lam_re, ssm_lam_im=ssm_lam_im, ssm_log_step=ssm_log_step,
             ssm_b_re=ssm_b_re, ssm_b_im=ssm_b_im, ssm_c_re=ssm_c_re, ssm_c_im=ssm_c_im, ssm_d=ssm_d, ssm_w_glu=ssm_w_glu,
             ssm_b_glu=ssm_b_glu, attn_sinks=attn_sinks, w_out=w_out, post_norm_g=post_norm_g, pl_w_proj=pl_w_proj,
             pl_w_gate=pl_w_gate, pl_b_gate=pl_b_gate)
    m = dict(pre_norm_g=m_pre_norm_g, w_in=m_w_in, ssm_lam_re=m_ssm_lam_re, ssm_lam_im=m_ssm_lam_im,
             ssm_log_step=m_ssm_log_step, ssm_b_re=m_ssm_b_re, ssm_b_im=m_ssm_b_im, ssm_c_re=m_ssm_c_re,
             ssm_c_im=m_ssm_c_im, ssm_d=m_ssm_d, ssm_w_glu=m_ssm_w_glu, ssm_b_glu=m_ssm_b_glu, attn_sinks=m_attn_sinks,
             w_out=m_w_out, post_norm_g=m_post_norm_g, pl_w_proj=m_pl_w_proj, pl_w_gate=m_pl_w_gate,
             pl_b_gate=m_pl_b_gate)
    v = dict(pre_norm_g=v_pre_norm_g, w_in=v_w_in, ssm_lam_re=v_ssm_lam_re, ssm_lam_im=v_ssm_lam_im,
             ssm_log_step=v_ssm_log_step, ssm_b_re=v_ssm_b_re, ssm_b_im=v_ssm_b_im, ssm_c_re=v_ssm_c_re,
             ssm_c_im=v_ssm_c_im, ssm_d=v_ssm_d, ssm_w_glu=v_ssm_w_glu, ssm_b_glu=v_ssm_b_glu, attn_sinks=v_attn_sinks,
             w_out=v_w_out, post_norm_g=v_post_norm_g, pl_w_proj=v_pl_w_proj, pl_w_gate=v_pl_w_gate,
             pl_b_gate=v_pl_b_gate)
    me = _slot(lax.axis_index("x"), lax.axis_index("y"), lax.axis_index("c"))
    kf = lambda d: {n: _kernel_form(n, a) for n, a in d.items()}
    wk, mk, vk = kf(w), kf(m), kf(v)

    (gathered,) = _allgather_weights([wk["w_in"]])
    loss, grad_x, grads, g_late = _local_step(
        x, p[0], loss_target, wk["pre_norm_g"], gathered.reshape(D_IN, D_MODEL), wk["ssm_lam_re"], wk["ssm_lam_im"],
        wk["ssm_log_step"], wk["ssm_b_re"], wk["ssm_b_im"], wk["ssm_c_re"], wk["ssm_c_im"], wk["ssm_d"],
        wk["ssm_b_glu"], wk["attn_sinks"], wk["post_norm_g"], wk["pl_b_gate"], [wk[n] for n in LATE_NAMES], me)

    owned = lambda g: g.reshape(N_DEV, D_IN // N_DEV, D_MODEL)
    tiny_form = lambda d: [d[n].reshape(rows, cols) for n, rows, cols in TINY]
    med_form = lambda d: [d[n].reshape(N_DEV, rows // N_DEV, cols) for n, rows, cols in MEDIUM]
    g_big, loss, g_tiny, g_med = _reduce_final(
        [owned(grads["w_in16"])], [lax.dynamic_index_in_dim(owned(grads["w_in"]), me, axis=0, keepdims=False)],
        loss, tiny_form(grads), med_form(grads))
    names = BIG_NAMES + tuple(n for n, _, _ in TINY + MEDIUM)
    form = lambda d: [d[n] for n in BIG_NAMES] + tiny_form(d) + med_form(d)
    updated = _adamw_update(g_big + g_late + g_tiny + g_med, form(wk), form(mk), form(vk))
    vals = dict(zip(names, updated))
    results = [[_given_form(n, vals[n][kind], w[n].shape) for n in WEIGHT_NAMES] for kind in range(4)]
    return (loss.reshape(()), grad_x, *results[0], *results[1], *results[2], *results[3])
```

```python
import functools
import math

import jax
import jax.numpy as jnp
from jax import lax
from jax.experimental import pallas as pl
from jax.experimental.pallas import tpu as pltpu

F32 = jnp.float32
BF16 = jnp.bfloat16

D_MODEL = 1024
D_SSM = 512
D_ATTN = 512
SSM_P = 16
SSM_G = 32
SSM_N = 64
N_HEADS = 8
KV_HEADS = 2
Q_PER_KV = 4
HEAD_DIM = 64
ATT_BLOCK = 128
D_PLE = 256
D_IN = 2304
EPS = 1e-6
N_DEV = 8
N_SEG = 8
G_TILE = 8
N_GT = SSM_G // G_TILE
CH_T = G_TILE * SSM_P
ST_T = G_TILE * SSM_N
N_STATE = SSM_G * SSM_N
SCAN_UNROLL = 4
LANES = 128
VMEM_LIMIT = 60 * 1024 * 1024

ADAM_LR = 0.001
ADAM_B1 = 0.9
ADAM_B2 = 0.999
ADAM_EPS = 1e-08
ADAM_WD = 0.01
ADAM_STEP = 10

GELU_C = math.sqrt(2.0 / math.pi)
GELU_K = 0.044715
ATT_SCALE = 1.0 / math.sqrt(HEAD_DIM)
NEG_BIG = -1e30


def _mm(a, b):
    return jnp.dot(a.astype(BF16), b.astype(BF16), preferred_element_type=F32)


def _mm_nt(a, b):
    return lax.dot_general(a.astype(BF16), b.astype(BF16), (((1,), (1,)), ((), ())), preferred_element_type=F32)


def _mm_tn(a, b):
    return lax.dot_general(a.astype(BF16), b.astype(BF16), (((0,), (0,)), ((), ())), preferred_element_type=F32)


def _sigmoid(x):
    return 1.0 / (1.0 + jnp.exp(-x))


def _tc_params(sem):
    return pltpu.CompilerParams(dimension_semantics=sem, vmem_limit_bytes=VMEM_LIMIT)


def _const_spec(shape):
    nd = len(shape)
    return pl.BlockSpec(shape, lambda *_: (0,) * nd)


def _mesh_pos():
    return lax.axis_index("x"), lax.axis_index("y"), lax.axis_index("c")


ROW_CHUNKS = (64, 32, 16)


def _row_chunk(nrows):
    return next((c for c in ROW_CHUNKS if nrows % c == 0), None)


def _row_chunks(nrows, fn, chunk=None, init=None):
    chunk = chunk or _row_chunk(nrows)

    def step(i, carry):
        rows = pl.ds(pl.multiple_of(i * chunk, chunk), chunk)
        if init is None:
            fn(rows)
            return carry
        return fn(rows, carry)

    return lax.fori_loop(0, nrows // chunk, step, 0 if init is None else init)


def _slot(px, py, pc):
    return 4 * px + 2 * py + pc


def _allgather_weights(shards):
    n = len(shards)

    def body(*refs):
        srcs, outs, (send_sems, recv_sems) = refs[:n], refs[n:2 * n], refs[2 * n:]
        x, y, c = _mesh_pos()
        me, sibling = (x, y, c), (x, y, 1 - c)
        chips = [(1 - x, y), (x, 1 - y), (1 - x, 1 - y)]

        def copy(a, k, block, to):
            blk = outs[a].at[_slot(*block)]
            return pltpu.make_async_remote_copy(
                src_ref=blk, dst_ref=blk, send_sem=send_sems.at[7 * a + k], recv_sem=recv_sems.at[7 * a + k],
                device_id=to, device_id_type=pl.DeviceIdType.MESH)

        sends = []
        for a in range(n):
            mine = outs[a].at[_slot(*me)]

            def cast(r, mine=mine, src=srcs[a]):
                mine[r, :] = src[r, :].astype(BF16)

            _row_chunks(srcs[a].shape[0], cast)
            first = [copy(a, 0, me, sibling)] + [copy(a, 1 + j, me, (*chip, c)) for j, chip in enumerate(chips)]
            for cp in first:
                cp.start()
            sends += first
        for a in range(n):
            for j, chip in enumerate(chips):
                copy(a, 1 + j, (*chip, c), me).wait_recv()
                fwd = copy(a, 4 + j, (*chip, c), sibling)
                fwd.start()
                sends.append(fwd)
        for a in range(n):
            copy(a, 0, sibling, me).wait_recv()
            for j, chip in enumerate(chips):
                copy(a, 4 + j, (*chip, 1 - c), me).wait_recv()
        for cp in sends:
            cp.wait_send()

    vm = pl.BlockSpec(memory_space=pltpu.VMEM)
    return pl.pallas_call(
        body, name="allgather_weights",
        out_shape=tuple(jax.ShapeDtypeStruct((N_DEV,) + s.shape, BF16) for s in shards),
        in_specs=[vm] * n, out_specs=(vm,) * n,
        scratch_shapes=[pltpu.SemaphoreType.DMA((7 * n,)), pltpu.SemaphoreType.DMA((7 * n,))],
        compiler_params=pltpu.CompilerParams(vmem_limit_bytes=VMEM_LIMIT),
    )(*shards)


def _adamw(w, g, m, v):
    m = ADAM_B1 * m + (1.0 - ADAM_B1) * g
    v = ADAM_B2 * v + (1.0 - ADAM_B2) * (g * g)
    m_hat = m / (1.0 - ADAM_B1 ** ADAM_STEP)
    v_hat = v / (1.0 - ADAM_B2 ** ADAM_STEP)
    delta = -ADAM_LR * (m_hat / (jnp.sqrt(v_hat) + ADAM_EPS) + ADAM_WD * w)
    return delta, m, v


def _remote(src, dst, send_sems, recv_sems, k, to):
    return pltpu.make_async_remote_copy(src_ref=src, dst_ref=dst, send_sem=send_sems.at[k], recv_sem=recv_sems.at[k],
                                        device_id=to, device_id_type=pl.DeviceIdType.MESH)


def _big_reduce_phases(g16_r, go_r, outs, send2, recv1, recv2, s_send, s_recv):
    n = len(g16_r)
    x, y, c = _mesh_pos()
    sibling = (x, y, 1 - c)
    chips = [(1 - x, y), (x, 1 - y), (1 - x, 1 - y)]
    all_chips = [(x, y)] + chips
    lvl1 = []
    for a in range(n):
        cps = [_remote(g16_r[a].at[_slot(*chip, 1 - c)], recv1[a].at[j], s_send, s_recv, 7 * a + j, sibling)
               for j, chip in enumerate(all_chips)]
        for cp in cps:
            cp.start()
        lvl1.append(cps)
    yield
    lvl2 = []
    for a in range(n):
        for cp in lvl1[a]:
            cp.wait_recv()
        og = outs[a]

        def partials(r, a=a, og=og):
            og[r, :] = go_r[a][r, :] + recv1[a][0, r, :].astype(F32)
            for j, chip in enumerate(chips):
                mine16 = g16_r[a][_slot(*chip, c), r, :].astype(F32)
                send2[a][j, r, :] = (mine16 + recv1[a][1 + j, r, :].astype(F32)).astype(BF16)

        _row_chunks(go_r[a].shape[0], partials)
        cps = [_remote(send2[a].at[j], recv2[a].at[j], s_send, s_recv, 7 * a + 4 + j, (*chip, c))
               for j, chip in enumerate(chips)]
        for cp in cps:
            cp.start()
        lvl2.append(cps)
    yield
    for a in range(n):
        for cp in lvl2[a]:
            cp.wait_recv()
        og = outs[a]

        def total(r, a=a, og=og):
            g = og[r, :]
            for j in range(3):
                g = g + recv2[a][j, r, :].astype(F32)
            og[r, :] = g

        _row_chunks(go_r[a].shape[0], total)
    yield
    for cps in lvl1 + lvl2:
        for cp in cps:
            cp.wait_send()


def _adamw_update(g, w, m, v):
    n = len(g)

    def body(*refs):
        g_r, w_r, m_r, v_r = (refs[i * n:(i + 1) * n] for i in range(4))
        outs = refs[4 * n:]
        for a in range(n):
            og, od, om, ov = outs[4 * a:4 * a + 4]

            def update(idx, a=a, og=og, od=od, om=om, ov=ov):
                gv = g_r[a][idx]
                d, nm, nv = _adamw(w_r[a][idx], gv, m_r[a][idx], v_r[a][idx])
                og[idx] = gv
                od[idx] = d
                om[idx] = nm
                ov[idx] = nv

            shape = g_r[a].shape
            if len(shape) == 3:
                for b in range(shape[0]):
                    update(b)
            elif _row_chunk(shape[0]) is not None:
                _row_chunks(shape[0], update)
            else:
                update(Ellipsis)

    vm = pl.BlockSpec(memory_space=pltpu.VMEM)
    res = pl.pallas_call(
        body, name="adamw_update",
        out_shape=tuple(jax.ShapeDtypeStruct(t.shape, F32) for t in g for _ in range(4)),
        in_specs=[vm] * (4 * n), out_specs=(vm,) * (4 * n),
        compiler_params=pltpu.CompilerParams(vmem_limit_bytes=VMEM_LIMIT),
    )(*g, *w, *m, *v)
    return [res[4 * a:4 * a + 4] for a in range(n)]


TINY = (("pre_norm_g", 1, 1024), ("post_norm_g", 1, 1024), ("pl_b_gate", 1, 1024), ("ssm_d", 1, 512),
        ("ssm_b_glu", 1, 512), ("ssm_log_step", 1, 32), ("attn_sinks", 1, 8), ("ssm_lam_re", 32, 64),
        ("ssm_lam_im", 32, 64))
MEDIUM = (("ssm_b_re", SSM_G * SSM_P, SSM_N), ("ssm_b_im", SSM_G * SSM_P, SSM_N), ("ssm_c_re", SSM_G * SSM_P, SSM_N),
          ("ssm_c_im", SSM_G * SSM_P, SSM_N))


def _stage_rows():
    offs, r = {}, 0
    for name, rows, cols in TINY + (("loss", 1, 1),):
        if rows > 1:
            r = -(-r // 8) * 8
        offs[name] = r
        r += rows if rows > 1 else max(cols // LANES, 1)
    return offs, -(-r // 8) * 8


def _reduce_final(g16, gown, loss, g_tiny, g_med):
    nb_, nt, nm_ = len(g16), len(TINY), len(MEDIUM)
    offs, stage_rows = _stage_rows()

    def body(*refs):
        g16_r, go_r = refs[:nb_], refs[nb_:2 * nb_]
        base = 2 * nb_
        loss_r, gt, gm = refs[base], refs[base + 1:base + 1 + nt], refs[base + 1 + nt:base + 1 + nt + nm_]
        base += 1 + nt + nm_
        out_b = refs[base:base + nb_]
        base += nb_
        loss_o, out_t, out_m = refs[base], refs[base + 1:base + 1 + nt], refs[base + 1 + nt:base + 1 + nt + nm_]
        base += 1 + nt + nm_
        send2_b, recv1_b, recv2_b = (refs[base + i * nb_:base + (i + 1) * nb_] for i in range(3))
        base += 3 * nb_
        stage = refs[base]
        recv1, part, recv2 = (refs[base + 1 + i * nm_:base + 1 + (i + 1) * nm_] for i in range(3))
        bs_send, bs_recv, s_send, s_recv = refs[base + 1 + 3 * nm_:]
        big = _big_reduce_phases(g16_r, go_r, out_b, send2_b, recv1_b, recv2_b, bs_send, bs_recv)
        small = small_phases(loss_r, gt, gm, loss_o, out_t, out_m, stage, recv1, part, recv2, s_send, s_recv)
        next(big)
        next(small)
        next(big)
        for _ in small:
            pass
        for _ in big:
            pass

    def small_phases(loss_r, gt, gm, loss_o, out_t, out_m, stage, recv1, part, recv2, s_send, s_recv):
        x, y, c = _mesh_pos()
        me = _slot(x, y, c)
        sibling = (x, y, 1 - c)
        chips = [(1 - x, y), (x, 1 - y), (1 - x, 1 - y)]
        all_chips = [(x, y)] + chips
        peers = [sibling] + [(*chip, c) for chip in chips] + [(*chip, 1 - c) for chip in chips]
        sem = iter(range(7 + 14 * nm_))
        lvl1 = []
        for a in range(nm_):
            cps = [_remote(gm[a].at[_slot(*chip, 1 - c)], recv1[a].at[j], s_send, s_recv, next(sem), sibling)
                   for j, chip in enumerate(all_chips)]
            for cp in cps:
                cp.start()
            lvl1.append(cps)
        mine = stage.at[me]
        mine[...] = jnp.zeros((stage_rows, LANES), F32)
        for (name, rows, cols), ref in zip(TINY + (("loss", 1, 1),), gt + (loss_r,)):
            r0 = offs[name]
            if rows > 1:
                mine[r0:r0 + rows, 0:cols] = ref[...]
            elif cols >= LANES:
                for i in range(cols // LANES):
                    mine[r0 + i:r0 + i + 1, :] = ref[:, i * LANES:(i + 1) * LANES]
            else:
                mine[r0:r0 + 1, 0:cols] = ref[...]
        tiny_cps = [_remote(mine, mine, s_send, s_recv, next(sem), peer) for peer in peers]
        for cp in tiny_cps:
            cp.start()
        yield
        lvl2 = []
        for a in range(nm_):
            for cp in lvl1[a]:
                cp.wait_recv()
            for j, chip in enumerate(all_chips):
                part[a][j] = gm[a][_slot(*chip, c)] + recv1[a][j]
            cps = [_remote(part[a].at[1 + j], recv2[a].at[j], s_send, s_recv, next(sem), (*chip, c))
                   for j, chip in enumerate(chips)]
            for cp in cps:
                cp.start()
            lvl2.append(cps)
        yield
        lvl3 = []
        for a in range(nm_):
            for cp in lvl2[a]:
                cp.wait_recv()
            blk = out_m[a].at[me]
            blk[...] = ((part[a][0] + recv2[a][0]) + recv2[a][1]) + recv2[a][2]
            cps = [_remote(blk, blk, s_send, s_recv, next(sem), peer) for peer in peers]
            for cp in cps:
                cp.start()
            lvl3.append(cps)
        yield
        for cp in tiny_cps:
            cp.wait_recv()
        tot = stage[0]
        for d in range(1, N_DEV):
            tot = tot + stage[d]
        loss_o[...] = tot[offs["loss"]:offs["loss"] + 1, 0:1]
        for k, (name, rows, cols) in enumerate(TINY):
            r0 = offs[name]
            if rows > 1:
                out_t[k][...] = tot[r0:r0 + rows, 0:cols]
            elif cols >= LANES:
                for i in range(cols // LANES):
                    out_t[k][:, i * LANES:(i + 1) * LANES] = tot[r0 + i:r0 + i + 1, :]
            else:
                out_t[k][...] = tot[r0:r0 + 1, 0:cols]
        for cps in lvl3:
            for cp in cps:
                cp.wait_recv()
        for cps in lvl1 + lvl2 + lvl3 + [tiny_cps]:
            for cp in cps:
                cp.wait_send()

    vmem = pl.BlockSpec(memory_space=pltpu.VMEM)
    t_shapes = [jax.ShapeDtypeStruct((rows, cols), F32) for _, rows, cols in TINY]
    m_shapes = [jax.ShapeDtypeStruct((N_DEV, rows // N_DEV, cols), F32) for _, rows, cols in MEDIUM]
    blk = [(rows // N_DEV, cols) for _, rows, cols in MEDIUM]
    scratch = ([pltpu.VMEM((3,) + t.shape, BF16) for t in gown] + [pltpu.VMEM((4,) + t.shape, BF16) for t in gown]
               + [pltpu.VMEM((3,) + t.shape, BF16) for t in gown]
               + [pltpu.VMEM((N_DEV, stage_rows, LANES), F32)]
               + [pltpu.VMEM((4,) + b, F32) for b in blk] + [pltpu.VMEM((4,) + b, F32) for b in blk]
               + [pltpu.VMEM((3,) + b, F32) for b in blk]
               + [pltpu.SemaphoreType.DMA((7 * nb_,)), pltpu.SemaphoreType.DMA((7 * nb_,)),
                  pltpu.SemaphoreType.DMA((7 + 14 * nm_,)), pltpu.SemaphoreType.DMA((7 + 14 * nm_,))])
    n_in, n_out = 2 * nb_ + 1 + nt + nm_, nb_ + 1 + nt + nm_
    res = pl.pallas_call(
        body, name="reduce_final",
        out_shape=tuple(jax.ShapeDtypeStruct(t.shape, F32) for t in gown) + (jax.ShapeDtypeStruct((1, 1), F32),)
        + tuple(t_shapes) + tuple(m_shapes),
        in_specs=[vmem] * n_in, out_specs=(vmem,) * n_out, scratch_shapes=scratch,
        compiler_params=pltpu.CompilerParams(vmem_limit_bytes=VMEM_LIMIT),
    )(*g16, *gown, loss, *g_tiny, *g_med)
    return list(res[:nb_]), res[nb_], list(res[nb_ + 1:nb_ + 1 + nt]), list(res[nb_ + 1 + nt:])


def _gather_phases(shard_r, gath, cast, send_sems, recv_sems, local_sems):
    n = len(shard_r)
    x, y, c = _mesh_pos()
    me, sibling = (x, y, c), (x, y, 1 - c)
    chips = [(1 - x, y), (x, 1 - y), (1 - x, 1 - y)]

    def own(a, k, to):
        return _remote(cast[a], gath[a].at[_slot(*me)], send_sems, recv_sems, 7 * a + k, to)

    def passed(a, k, block, to):
        blk = gath[a].at[_slot(*block)]
        return _remote(blk, blk, send_sems, recv_sems, 7 * a + k, to)

    def keep(a):
        return pltpu.make_async_copy(cast[a], gath[a].at[_slot(*me)], local_sems.at[a])

    def start():
        for a in range(n):
            def to16(r, a=a):
                cast[a][r, :] = shard_r[a][r, :].astype(BF16)

            _row_chunks(shard_r[a].shape[0], to16)
            keep(a).start()
            own(a, 0, sibling).start()
            for j, chip in enumerate(chips):
                own(a, 1 + j, (*chip, c)).start()

    def relay():
        for a in range(n):
            for j, chip in enumerate(chips):
                passed(a, 1 + j, (*chip, c), me).wait_recv()
                passed(a, 4 + j, (*chip, c), sibling).start()

    def finish():
        for a in range(n):
            passed(a, 0, sibling, me).wait_recv()
            for j, chip in enumerate(chips):
                passed(a, 4 + j, (*chip, 1 - c), me).wait_recv()
            own(a, 0, sibling).wait_send()
            for j, chip in enumerate(chips):
                own(a, 1 + j, (*chip, c)).wait_send()
                passed(a, 4 + j, (*chip, c), sibling).wait_send()
            keep(a).wait()

    return start, relay, finish


def _gather_operands(shards):
    n = len(shards)
    return ((pl.BlockSpec(memory_space=pl.ANY),) * n,
            tuple(jax.ShapeDtypeStruct((N_DEV,) + s.shape, BF16) for s in shards),
            [pltpu.VMEM(s.shape, BF16) for s in shards]
            + [pltpu.SemaphoreType.DMA((7 * n,)), pltpu.SemaphoreType.DMA((7 * n,)), pltpu.SemaphoreType.DMA((n,))])


def _in_proj(x2, g_pre, w_in, late, bl, seg):
    t = x2.shape[0]
    tm = seg
    steps = t // tm
    n = len(late)
    forward_step, last_step = (steps * 5) // 8, steps - 1

    def body(*refs):
        x_ref, g_ref, w_ref = refs[:3]
        late_r = refs[3:3 + n]
        u_ref, zs_ref, q_ref, k_ref, v_ref, za_ref = refs[3 + n:9 + n]
        gath = refs[9 + n:9 + 2 * n]
        cast = refs[9 + 2 * n:9 + 3 * n]
        send_sems, recv_sems, local_sems = refs[9 + 3 * n:]
        i = pl.program_id(0)
        start, relay, finish = _gather_phases(late_r, gath, cast, send_sems, recv_sems, local_sems)
        pl.when(i == 0)(start)
        xv = x_ref[...]
        r = lax.rsqrt(jnp.mean(xv * xv, axis=-1, keepdims=True) + EPS)
        hn = xv * r * g_ref[...]
        proj = _mm_nt(hn, w_ref[...])
        u_ref[0] = proj[:, 0:512]
        zs_ref[...] = proj[:, 512:1024]
        q_ref[...] = proj[:, 1024:1536].astype(BF16)
        k_ref[...] = proj[:, 1536:1664].astype(BF16)
        v_ref[...] = proj[:, 1664:1792].astype(BF16)
        za_ref[...] = proj[:, 1792:2304]
        pl.when(i == forward_step)(relay)
        pl.when(i == last_step)(finish)

    row = lambda w: pl.BlockSpec((tm, w), lambda i: (i, 0))
    g_specs, g_shapes, g_scratch = _gather_operands(late)
    res = pl.pallas_call(
        body, name="in_proj", grid=(steps,),
        in_specs=[row(D_MODEL), _const_spec((1, D_MODEL)), _const_spec((D_IN, D_MODEL))]
        + [_const_spec(s.shape) for s in late],
        out_specs=(pl.BlockSpec((1, tm, D_SSM), lambda i: (i // N_SEG, 0, i % N_SEG)),
                   row(512), row(512), row(128), row(128), row(512)) + g_specs,
        out_shape=(jax.ShapeDtypeStruct((bl, seg, N_SEG * D_SSM), F32),
                   jax.ShapeDtypeStruct((t, 512), F32), jax.ShapeDtypeStruct((t, 512), BF16),
                   jax.ShapeDtypeStruct((t, 128), BF16), jax.ShapeDtypeStruct((t, 128), BF16),
                   jax.ShapeDtypeStruct((t, 512), F32)) + g_shapes,
        scratch_shapes=g_scratch,
        compiler_params=_tc_params(("arbitrary",)),
    )(x2, g_pre, w_in, *late)
    return res[:6], list(res[6:])


def _discretise(lr, li, ls):
    step = jnp.exp(ls)
    mag = jnp.exp(lr * step)
    ar = mag * jnp.cos(li * step)
    ai = mag * jnp.sin(li * step)
    den = lr * lr + li * li
    cr = ((ar - 1.0) * lr + ai * li) / den
    ci = (ai * lr - (ar - 1.0) * li) / den
    return step, ar, ai, den, cr, ci


def _per_channel(v):
    return jnp.broadcast_to(v[:, None, :], (SSM_G, SSM_P, SSM_N)).reshape(SSM_G * SSM_P, SSM_N)


def _ssm_prep(lam_re, lam_im, log_step, b_re, b_im, seg):
    def body(lr_ref, li_ref, ls_ref, br_ref, bi_ref, lrr_ref, lir_ref, lsr_ref,
             ar_ref, ai_ref, bbr_ref, bbi_ref, pr_ref, pi_ref):
        _, _, _, _, cr, ci = _discretise(lr_ref[...], li_ref[...], ls_ref[...])
        cr, ci = _per_channel(cr), _per_channel(ci)
        br, bi = br_ref[...], bi_ref[...]
        bbr_ref[...] = cr * br - ci * bi
        bbi_ref[...] = cr * bi + ci * br
        stepr = jnp.exp(lsr_ref[...])
        k = (lax.broadcasted_iota(jnp.int32, (8, N_STATE), 0) + 1).astype(F32)
        magk = jnp.exp(k * (lrr_ref[...] * stepr))
        ang = k * (lir_ref[...] * stepr)
        pr_ref[0:8, :] = magk * jnp.cos(ang)
        pi_ref[0:8, :] = magk * jnp.sin(ang)
        n = 8
        while n < seg:
            tr, ti = pr_ref[n - 1:n, :], pi_ref[n - 1:n, :]
            xr, xi = pr_ref[0:n, :], pi_ref[0:n, :]
            pr_ref[n:2 * n, :] = xr * tr - xi * ti
            pi_ref[n:2 * n, :] = xr * ti + xi * tr
            n *= 2
        ar_ref[...] = pr_ref[0:1, :]
        ai_ref[...] = pi_ref[0:1, :]

    row = jax.ShapeDtypeStruct((1, N_STATE), F32)
    mat = jax.ShapeDtypeStruct((SSM_G * SSM_P, SSM_N), F32)
    pw = jax.ShapeDtypeStruct((seg, N_STATE), F32)
    vm = pl.BlockSpec(memory_space=pltpu.VMEM)
    step_row = jnp.broadcast_to(log_step, (SSM_G, SSM_N)).reshape(1, N_STATE)
    return pl.pallas_call(
        body, name="ssm_prep", out_shape=(row, row, mat, mat, pw, pw),
        in_specs=[vm] * 8, out_specs=(vm,) * 6,
    )(lam_re, lam_im, log_step, b_re, b_im, lam_re.reshape(1, N_STATE), lam_im.reshape(1, N_STATE), step_row)


def _seg_rows(t):
    if isinstance(t, int):
        return pl.ds(t * N_SEG, N_SEG)
    return pl.ds(pl.multiple_of(t * N_SEG, N_SEG), N_SEG)


def _scan_forward(xs, a_re, a_im, pw_re, pw_im, cs, seg):
    are = jnp.broadcast_to(a_re, (N_SEG, ST_T))
    aim = jnp.broadcast_to(a_im, (N_SEG, ST_T))

    def steps(k, carry):
        xr, xi = carry
        for j in range(SCAN_UNROLL):
            r = pl.multiple_of((k * SCAN_UNROLL + j) * N_SEG, N_SEG)
            nr = are * xr - aim * xi + xs[pl.ds(r, N_SEG), 0:ST_T]
            ni = are * xi + aim * xr + xs[pl.ds(r, N_SEG), ST_T:2 * ST_T]
            xs[pl.ds(r, N_SEG), 0:ST_T] = nr
            xs[pl.ds(r, N_SEG), ST_T:2 * ST_T] = ni
            xr, xi = nr, ni
        return xr, xi

    zero = jnp.zeros((N_SEG, ST_T), F32)
    fr, fi = lax.fori_loop(0, seg // SCAN_UNROLL, steps, (zero, zero))
    sr, si = pw_re[seg - 1:seg, :], pw_im[seg - 1:seg, :]
    cr = jnp.zeros((1, ST_T), F32)
    ci = jnp.zeros((1, ST_T), F32)
    cs[0:1, :] = cr
    cs[8:9, :] = ci
    for s in range(1, N_SEG):
        ncr = sr * cr - si * ci + fr[s - 1:s, :]
        nci = sr * ci + si * cr + fi[s - 1:s, :]
        cr, ci = ncr, nci
        cs[s:s + 1, :] = cr
        cs[8 + s:9 + s, :] = ci
    car, cai = cs[0:8, :], cs[8:16, :]

    def fix(t, _):
        r = pl.multiple_of(t * N_SEG, N_SEG)
        pr, pi = pw_re[pl.ds(t, 1), :], pw_im[pl.ds(t, 1), :]
        xs[pl.ds(r, N_SEG), 0:ST_T] = xs[pl.ds(r, N_SEG), 0:ST_T] + (pr * car - pi * cai)
        xs[pl.ds(r, N_SEG), ST_T:2 * ST_T] = xs[pl.ds(r, N_SEG), ST_T:2 * ST_T] + (pr * cai + pi * car)
        return 0

    lax.fori_loop(0, seg, fix, 0, unroll=SCAN_UNROLL)


def _ssm_forward(u_perm, bcat, ccat, a_re, a_im, pw_re, pw_im, d_row, seg):
    bl, rows, _ = u_perm.shape

    def body(u_ref, b_ref, c_ref, ar_ref, ai_ref, pr_ref, pi_ref, d_ref, y_ref, xs_ref, cs_ref):
        u = u_ref[0]
        xs, cs = xs_ref.at[0, 0], cs_ref.at[0, 0]
        xs[...] = _mm(u, b_ref[0])
        _scan_forward(xs, ar_ref[...], ai_ref[...], pr_ref, pi_ref, cs, seg)
        y_ref[0] = _mm(xs[...], c_ref[0]) + d_ref[...] * u

    state = lambda r, c: pl.BlockSpec((1, 1, r, c), lambda b, j: (b, j, 0, 0))
    return pl.pallas_call(
        body, name="ssm_forward", grid=(bl, N_GT),
        in_specs=[pl.BlockSpec((1, rows, CH_T), lambda b, j: (b, 0, j)),
                  pl.BlockSpec((1, CH_T, 2 * ST_T), lambda b, j: (j, 0, 0)),
                  pl.BlockSpec((1, 2 * ST_T, CH_T), lambda b, j: (j, 0, 0)),
                  pl.BlockSpec((1, ST_T), lambda b, j: (0, j)), pl.BlockSpec((1, ST_T), lambda b, j: (0, j)),
                  pl.BlockSpec((seg, ST_T), lambda b, j: (0, j)), pl.BlockSpec((seg, ST_T), lambda b, j: (0, j)),
                  pl.BlockSpec((1, CH_T), lambda b, j: (0, j))],
        out_specs=(pl.BlockSpec((1, rows, CH_T), lambda b, j: (b, 0, j)), state(rows, 2 * ST_T), state(16, ST_T)),
        out_shape=(jax.ShapeDtypeStruct((bl, rows, D_SSM), F32),
                   jax.ShapeDtypeStruct((bl, N_GT, rows, 2 * ST_T), F32),
                   jax.ShapeDtypeStruct((bl, N_GT, 16, ST_T), F32)),
        compiler_params=_tc_params(("arbitrary", "arbitrary")),
    )(u_perm, bcat, ccat, a_re, a_im, pw_re, pw_im, d_row)


def _ssm_backward(u_perm, dy_perm, states, carries, bcat_t, ccat_t, a_re, a_im, pw_re, pw_im, d_row, seg):
    bl, rows, _ = u_perm.shape

    def body(u_ref, dy_ref, xs_ref, cs_ref, bt_ref, ct_ref, ar_ref, ai_ref, pr_ref, pi_ref, d_ref,
             du_ref, db_ref, dc_ref, dar_ref, dai_ref, dd_ref, ls, cl):
        b = pl.program_id(1)
        u = u_ref[0]
        dy = dy_ref[0]
        xs, cs = xs_ref.at[0, 0], cs_ref.at[0, 0]
        ls[...] = _mm(dy, ct_ref[0])
        are = jnp.broadcast_to(ar_ref[...], (N_SEG, ST_T))
        aim = jnp.broadcast_to(ai_ref[...], (N_SEG, ST_T))

        def steps(k, carry):
            lr, li = carry
            for j in range(SCAN_UNROLL):
                r = pl.multiple_of((seg - 1 - (k * SCAN_UNROLL + j)) * N_SEG, N_SEG)
                nr = are * lr + aim * li + ls[pl.ds(r, N_SEG), 0:ST_T]
                ni = are * li - aim * lr + ls[pl.ds(r, N_SEG), ST_T:2 * ST_T]
                ls[pl.ds(r, N_SEG), 0:ST_T] = nr
                ls[pl.ds(r, N_SEG), ST_T:2 * ST_T] = ni
                lr, li = nr, ni
            return lr, li

        zero = jnp.zeros((N_SEG, ST_T), F32)
        fr, fi = lax.fori_loop(0, seg // SCAN_UNROLL, steps, (zero, zero))
        sr, si = pr_ref[seg - 1:seg, :], pi_ref[seg - 1:seg, :]
        cr = jnp.zeros((1, ST_T), F32)
        ci = jnp.zeros((1, ST_T), F32)
        cl[7:8, :] = cr
        cl[15:16, :] = ci
        for s in range(N_SEG - 2, -1, -1):
            ncr = sr * cr + si * ci + fr[s + 1:s + 2, :]
            nci = sr * ci - si * cr + fi[s + 1:s + 2, :]
            cr, ci = ncr, nci
            cl[s:s + 1, :] = cr
            cl[8 + s:9 + s, :] = ci
        clr, cli = cl[0:8, :], cl[8:16, :]

        def fix_rows(rows, t, xpr, xpi, acc):
            dr, di = acc
            pr, pi = pr_ref[pl.ds(seg - 1 - t, 1), :], pi_ref[pl.ds(seg - 1 - t, 1), :]
            lr = ls[rows, 0:ST_T] + (pr * clr + pi * cli)
            li = ls[rows, ST_T:2 * ST_T] + (pr * cli - pi * clr)
            ls[rows, 0:ST_T] = lr
            ls[rows, ST_T:2 * ST_T] = li
            return dr + (lr * xpr + li * xpi), di + (li * xpr - lr * xpi)

        def fix_at(t, acc):
            prev = _seg_rows(t - 1)
            return fix_rows(_seg_rows(t), t, xs[prev, 0:ST_T], xs[prev, ST_T:2 * ST_T], acc)

        def fix(k, acc):
            for j in range(SCAN_UNROLL):
                acc = fix_at(k * SCAN_UNROLL + j, acc)
            return acc

        acc = fix_rows(pl.ds(0, N_SEG), 0, cs[0:8, :], cs[8:16, :], (zero, zero))
        for t in range(1, SCAN_UNROLL):
            acc = fix_at(t, acc)
        dr, di = lax.fori_loop(1, seg // SCAN_UNROLL, fix, acc)
        dar = jnp.sum(dr, axis=0, keepdims=True)
        dai = jnp.sum(di, axis=0, keepdims=True)
        lall = ls[...]
        du_ref[0] = (_mm(lall, bt_ref[0]) + d_ref[...] * dy).astype(BF16)
        dbp = _mm_tn(u, lall)
        dcp = _mm_tn(dy, xs[...])
        ddp = jnp.sum(dy * u, axis=0, keepdims=True)

        @pl.when(b == 0)
        def _():
            db_ref[0] = dbp
            dc_ref[0] = dcp
            dar_ref[...] = dar
            dai_ref[...] = dai
            dd_ref[...] = ddp

        @pl.when(b != 0)
        def _():
            db_ref[0] += dbp
            dc_ref[0] += dcp
            dar_ref[...] += dar
            dai_ref[...] += dai
            dd_ref[...] += ddp

    tile3 = lambda r, c: pl.BlockSpec((1, r, c), lambda j, b: (j, 0, 0))
    lane = lambda r, c: pl.BlockSpec((r, c), lambda j, b: (0, j))
    act = pl.BlockSpec((1, rows, CH_T), lambda j, b: (b, 0, j))
    state = lambda r, c: pl.BlockSpec((1, 1, r, c), lambda j, b: (b, j, 0, 0))
    return pl.pallas_call(
        body, name="ssm_backward", grid=(N_GT, bl),
        in_specs=[act, act, state(rows, 2 * ST_T), state(16, ST_T), tile3(2 * ST_T, CH_T), tile3(CH_T, 2 * ST_T),
                  lane(1, ST_T), lane(1, ST_T), lane(seg, ST_T), lane(seg, ST_T), lane(1, CH_T)],
        out_specs=(act, tile3(CH_T, 2 * ST_T), tile3(CH_T, 2 * ST_T), lane(1, ST_T), lane(1, ST_T), lane(1, CH_T)),
        out_shape=(jax.ShapeDtypeStruct((bl, rows, D_SSM), BF16),
                   jax.ShapeDtypeStruct((N_GT, CH_T, 2 * ST_T), F32), jax.ShapeDtypeStruct((N_GT, CH_T, 2 * ST_T), F32),
                   jax.ShapeDtypeStruct((1, N_STATE), F32), jax.ShapeDtypeStruct((1, N_STATE), F32),
                   jax.ShapeDtypeStruct((1, D_SSM), F32)),
        scratch_shapes=[pltpu.VMEM((rows, 2 * ST_T), F32), pltpu.VMEM((16, ST_T), F32)],
        compiler_params=_tc_params(("arbitrary", "arbitrary")),
    )(u_perm, dy_perm, states, carries, bcat_t, ccat_t, a_re, a_im, pw_re, pw_im, d_row)


def _ssm_param_grads(lam_re, lam_im, log_step, b_re, b_im, da_re, da_im, dbb_re, dbb_im):
    def body(lr_ref, li_ref, ls_ref, br_ref, bi_ref, gar_ref, gai_ref, gbr_ref, gbi_ref,
             dlr_ref, dli_ref, dls_ref, dbr_ref, dbi_ref):
        lr, li = lr_ref[...], li_ref[...]
        step, ar, ai, den, cr, ci = _discretise(lr, li, ls_ref[...])
        crb, cib = _per_channel(cr), _per_channel(ci)
        br, bi = br_ref[...], bi_ref[...]
        gbr, gbi = gbr_ref[...], gbi_ref[...]
        dbr_ref[...] = crb * gbr + cib * gbi
        dbi_ref[...] = crb * gbi - cib * gbr
        over_channels = lambda t: jnp.sum(t.reshape(SSM_G, SSM_P, SSM_N), axis=1)
        gcr = over_channels(br * gbr + bi * gbi)
        gci = over_channels(br * gbi - bi * gbr)
        ilr, ili = lr / den, -li / den
        gar = gar_ref[...] + (ilr * gcr + ili * gci)
        gai = gai_ref[...] + (ilr * gci - ili * gcr)
        qr, qi = cr * ilr - ci * ili, cr * ili + ci * ilr
        glr = -(qr * gcr + qi * gci)
        gli = -(qr * gci - qi * gcr)
        gwr = ar * gar + ai * gai
        gwi = ar * gai - ai * gar
        dlr_ref[...] = glr + step * gwr
        dli_ref[...] = gli + step * gwi
        dls_ref[...] = jnp.sum(lr * gwr + li * gwi, axis=-1, keepdims=True) * step

    lam = jax.ShapeDtypeStruct((SSM_G, SSM_N), F32)
    mat = jax.ShapeDtypeStruct((SSM_G * SSM_P, SSM_N), F32)
    vm = pl.BlockSpec(memory_space=pltpu.VMEM)
    return pl.pallas_call(
        body, name="ssm_param_grads", out_shape=(lam, lam, jax.ShapeDtypeStruct((SSM_G, 1), F32), mat, mat),
        in_specs=[vm] * 9, out_specs=(vm,) * 5,
    )(lam_re, lam_im, log_step, b_re, b_im, da_re, da_im, dbb_re, dbb_im)


ROWS4 = Q_PER_KV * ATT_BLOCK


def _att_dist_mask(first_block):
    qi = lax.broadcasted_iota(jnp.int32, (ROWS4, 2 * ATT_BLOCK), 0) & (ATT_BLOCK - 1)
    si = lax.broadcasted_iota(jnp.int32, (ROWS4, 2 * ATT_BLOCK), 1)
    dist = qi + ATT_BLOCK - si
    valid = (dist >= 0) & (dist < ATT_BLOCK) & ((si >= ATT_BLOCK) | jnp.logical_not(first_block))
    return dist.astype(F32), valid


def _stack_heads(x, kv):
    return jnp.concatenate([x[:, (kv * Q_PER_KV + g) * HEAD_DIM:(kv * Q_PER_KV + g + 1) * HEAD_DIM]
                            for g in range(Q_PER_KV)], axis=0)


def _stack_cols(x, kv):
    return jnp.concatenate([x[:, kv * Q_PER_KV + g:kv * Q_PER_KV + g + 1] for g in range(Q_PER_KV)], axis=0)


def _per_head_col(vals):
    return jnp.concatenate([jnp.full((ATT_BLOCK, 1), v, F32) for v in vals], axis=0)


def _attn_forward(q, k, v, sinks, late, bl, nb):
    t = q.shape[0]
    n = len(late)
    steps = bl * nb

    def body(*refs):
        sink_ref, q_ref, kp_ref, kc_ref, vp_ref, vc_ref = refs[:6]
        late_r = refs[6:6 + n]
        o_ref, lse_ref = refs[6 + n:8 + n]
        gath = refs[8 + n:8 + 2 * n]
        cast = refs[8 + 2 * n:8 + 3 * n]
        send_sems, recv_sems, local_sems = refs[8 + 3 * n:]
        i = pl.program_id(1)
        step = pl.program_id(0) * nb + i
        start, relay, finish = _gather_phases(late_r, gath, cast, send_sems, recv_sems, local_sems)
        pl.when(step == 0)(start)
        pl.when(step == (steps * 5) // 8)(relay)
        pl.when(step == steps - 1)(finish)
        dist4, valid4 = _att_dist_mask(i == 0)
        dist, valid = dist4[0:ATT_BLOCK, :], valid4[0:ATT_BLOCK, :]
        kk = jnp.concatenate([kp_ref[...], kc_ref[...]], axis=0)
        vv = jnp.concatenate([vp_ref[...], vc_ref[...]], axis=0)
        qv = q_ref[...]
        for h in range(N_HEADS):
            kv = h // Q_PER_KV
            slope = 2.0 ** (-(h + 1))
            qh = qv[:, h * HEAD_DIM:(h + 1) * HEAD_DIM]
            kh = kk[:, kv * HEAD_DIM:(kv + 1) * HEAD_DIM]
            vh = vv[:, kv * HEAD_DIM:(kv + 1) * HEAD_DIM]
            s = _mm_nt(qh, kh) * ATT_SCALE - slope * dist
            s = jnp.where(valid, s, NEG_BIG)
            sink = sink_ref[h]
            m = jnp.maximum(jnp.max(s, axis=-1, keepdims=True), sink)
            e = jnp.exp(s - m)
            den = jnp.sum(e, axis=-1, keepdims=True) + jnp.exp(sink - m)
            o_ref[:, h * HEAD_DIM:(h + 1) * HEAD_DIM] = _mm(e, vh) * (1.0 / den)
            lse_ref[:, h:h + 1] = m + jnp.log(den)

    cur = lambda w: pl.BlockSpec((ATT_BLOCK, w), lambda b, i: (b * nb + i, 0))
    prev = lambda w: pl.BlockSpec((ATT_BLOCK, w), lambda b, i: (b * nb + jnp.maximum(i - 1, 0), 0))
    g_specs, g_shapes, g_scratch = _gather_operands(late)
    res = pl.pallas_call(
        body, name="attn_forward", grid=(bl, nb),
        in_specs=[pl.BlockSpec(memory_space=pltpu.SMEM), cur(512), prev(128), cur(128), prev(128), cur(128)]
        + [pl.BlockSpec(s.shape, lambda b, i: (0, 0)) for s in late],
        out_specs=(cur(512), cur(N_HEADS)) + g_specs,
        out_shape=(jax.ShapeDtypeStruct((t, D_ATTN), F32), jax.ShapeDtypeStruct((t, N_HEADS), F32)) + g_shapes,
        scratch_shapes=g_scratch,
        compiler_params=_tc_params(("arbitrary", "arbitrary")),
    )(sinks, q, k, k, v, v, *late)
    return res[:2], list(res[2:])


def _attn_backward(q, k, v, o, do, lse, sinks, late16, late_own, bl, nb):
    t = q.shape[0]
    n = len(late16)
    steps = bl * nb
    mid1, mid2, last = steps // 4, (steps * 3) // 4, steps - 1

    def body(*refs):
        (sink_ref, qc_ref, qn_ref, kp_ref, kc_ref, vp_ref, vc_ref, oc_ref, on_ref, doc_ref, don_ref,
         lc_ref, ln_ref) = refs[:13]
        g16_r, go_r = refs[13:13 + n], refs[13 + n:13 + 2 * n]
        dq_ref, dk_ref, dv_ref, ds_ref = refs[13 + 2 * n:17 + 2 * n]
        red = refs[17 + 2 * n:17 + 3 * n]
        own16, recv1, send2, recv2 = (refs[17 + 3 * n + k * n:17 + 3 * n + (k + 1) * n] for k in range(4))
        s_send, s_recv, s_local = refs[17 + 7 * n:]
        b, i = pl.program_id(0), pl.program_id(1)
        step = b * nb + i
        x, y, c = _mesh_pos()
        sibling = (x, y, 1 - c)
        chips = [(1 - x, y), (x, 1 - y), (1 - x, 1 - y)]
        all_chips = [(x, y)] + chips

        def lvl1(a, j):
            return _remote(g16_r[a].at[_slot(*all_chips[j], 1 - c)], recv1[a].at[j], s_send, s_recv, 7 * a + j, sibling)

        def lvl2(a, j):
            return _remote(send2[a].at[j], recv2[a].at[j], s_send, s_recv, 7 * a + 4 + j, (*chips[j], c))

        def mine(a, j):
            return pltpu.make_async_copy(g16_r[a].at[_slot(*chips[j], c)], own16[a].at[j], s_local.at[3 * a + j])

        @pl.when(step == 0)
        def _():
            for a in range(n):
                for j in range(3):
                    mine(a, j).start()
                for j in range(4):
                    lvl1(a, j).start()

        @pl.when(step == mid1)
        def _():
            for a in range(n):
                for j in range(3):
                    mine(a, j).wait()
                for j in range(4):
                    lvl1(a, j).wait_recv()

                def partials(r, a=a):
                    red[a][r, :] = go_r[a][r, :] + recv1[a][0, r, :].astype(F32)
                    for j in range(3):
                        send2[a][j, r, :] = (own16[a][j, r, :].astype(F32)
                                             + recv1[a][1 + j, r, :].astype(F32)).astype(BF16)

                _row_chunks(go_r[a].shape[0], partials)
                for j in range(3):
                    lvl2(a, j).start()

        @pl.when(step == mid2)
        def _():
            for a in range(n):
                for j in range(3):
                    lvl2(a, j).wait_recv()

                def total(r, a=a):
                    g = red[a][r, :]
                    for j in range(3):
                        g = g + recv2[a][j, r, :].astype(F32)
                    red[a][r, :] = g

                _row_chunks(go_r[a].shape[0], total)

        @pl.when(step == last)
        def _():
            for a in range(n):
                for j in range(4):
                    lvl1(a, j).wait_send()
                for j in range(3):
                    lvl2(a, j).wait_send()

        dist, valid = _att_dist_mask(i == 0)
        has_next = i + 1 < nb
        dist_n = dist[:, 0:ATT_BLOCK]
        valid_n = (dist_n < ATT_BLOCK) & has_next
        kk = jnp.concatenate([kp_ref[...], kc_ref[...]], axis=0)
        vv = jnp.concatenate([vp_ref[...], vc_ref[...]], axis=0)
        qc, qn = qc_ref[...], qn_ref[...]
        oc, on = oc_ref[...], on_ref[...]
        doc, don = doc_ref[...], don_ref[...]
        lc, ln = lc_ref[...], ln_ref[...]
        dsink_cols = []
        for kv in range(KV_HEADS):
            heads = range(kv * Q_PER_KV, (kv + 1) * Q_PER_KV)
            kh = kk[:, kv * HEAD_DIM:(kv + 1) * HEAD_DIM]
            vh = vv[:, kv * HEAD_DIM:(kv + 1) * HEAD_DIM]
            khc, vhc = kh[ATT_BLOCK:, :], vh[ATT_BLOCK:, :]
            slope = _per_head_col([2.0 ** (-(h + 1)) for h in heads])
            sink = _per_head_col([sink_ref[h] for h in heads])
            q4, do4 = _stack_heads(qc, kv), _stack_heads(doc, kv)
            delta = jnp.sum(do4 * _stack_heads(oc, kv), axis=-1, keepdims=True)
            lse4 = _stack_cols(lc, kv)
            s = _mm_nt(q4, kh) * ATT_SCALE - slope * dist
            p = jnp.where(valid, jnp.exp(s - lse4), 0.0)
            dsc = p * (_mm_nt(do4, vh) - delta)
            dq4 = _mm(dsc, kh) * ATT_SCALE
            dk_acc = _mm_tn(dsc[:, ATT_BLOCK:], q4)
            dv_acc = _mm_tn(p[:, ATT_BLOCK:], do4)
            dsink4 = jnp.exp(sink - lse4) * delta
            q4n, do4n = _stack_heads(qn, kv), _stack_heads(don, kv)
            delta_n = jnp.sum(do4n * _stack_heads(on, kv), axis=-1, keepdims=True)
            s2 = _mm_nt(q4n, khc) * ATT_SCALE - slope * dist_n
            p2 = jnp.where(valid_n, jnp.exp(s2 - _stack_cols(ln, kv)), 0.0)
            ds2 = p2 * (_mm_nt(do4n, vhc) - delta_n)
            dk_acc += _mm_tn(ds2, q4n)
            dv_acc += _mm_tn(p2, do4n)
            dk_ref[:, kv * HEAD_DIM:(kv + 1) * HEAD_DIM] = dk_acc * ATT_SCALE
            dv_ref[:, kv * HEAD_DIM:(kv + 1) * HEAD_DIM] = dv_acc
            for g, h in enumerate(heads):
                rows = slice(g * ATT_BLOCK, (g + 1) * ATT_BLOCK)
                dq_ref[:, h * HEAD_DIM:(h + 1) * HEAD_DIM] = dq4[rows, :]
                dsink_cols.append(-jnp.sum(dsink4[rows, :], axis=0, keepdims=True))
        dsink = jnp.concatenate(dsink_cols, axis=1)

        @pl.when((b == 0) & (i == 0))
        def _():
            ds_ref[...] = dsink

        @pl.when((b != 0) | (i != 0))
        def _():
            ds_ref[...] += dsink

    cur = lambda w: pl.BlockSpec((ATT_BLOCK, w), lambda b, i: (b * nb + i, 0))
    prev = lambda w: pl.BlockSpec((ATT_BLOCK, w), lambda b, i: (b * nb + jnp.maximum(i - 1, 0), 0))
    nxt = lambda w: pl.BlockSpec((ATT_BLOCK, w), lambda b, i: (b * nb + jnp.minimum(i + 1, nb - 1), 0))
    const2 = lambda s: pl.BlockSpec(s, lambda b, i: (0, 0))
    shard = [s.shape for s in late_own]
    res = pl.pallas_call(
        body, name="attn_backward", grid=(bl, nb),
        in_specs=[pl.BlockSpec(memory_space=pltpu.SMEM), cur(512), nxt(512), prev(128), cur(128), prev(128), cur(128),
                  cur(512), nxt(512), cur(512), nxt(512), cur(N_HEADS), nxt(N_HEADS)]
        + [pl.BlockSpec(memory_space=pl.ANY)] * n + [const2(s) for s in shard],
        out_specs=(cur(512), cur(128), cur(128), const2((1, N_HEADS))) + tuple(const2(s) for s in shard),
        out_shape=(jax.ShapeDtypeStruct((t, D_ATTN), F32), jax.ShapeDtypeStruct((t, 128), F32),
                   jax.ShapeDtypeStruct((t, 128), F32), jax.ShapeDtypeStruct((1, N_HEADS), F32))
        + tuple(jax.ShapeDtypeStruct(s, F32) for s in shard),
        scratch_shapes=[pltpu.VMEM((3,) + s, BF16) for s in shard] + [pltpu.VMEM((4,) + s, BF16) for s in shard]
        + [pltpu.VMEM((3,) + s, BF16) for s in shard] + [pltpu.VMEM((3,) + s, BF16) for s in shard]
        + [pltpu.SemaphoreType.DMA((7 * n,)), pltpu.SemaphoreType.DMA((7 * n,)), pltpu.SemaphoreType.DMA((3 * n,))],
        compiler_params=_tc_params(("arbitrary", "arbitrary")),
    )(sinks, q, q, k, k, v, v, o, o, do, do, lse, lse, *late16, *late_own)
    return res[:4], list(res[4:])


def _mix_forward_backward(x2, y_perm, z_ssm, attn, z_attn, p2, target2, w_glu, b_glu, w_out, g_post, w_gate, b_gate,
                          w_proj, bl, seg):
    t = x2.shape[0]
    tm = seg

    def body(x_ref, y_ref, zs_ref, at_ref, za_ref, p_ref, tg_ref,
             wglu_ref, bglu_ref, wout_ref, gpost_ref, wgate_ref, bgate_ref, wproj_ref,
             loss_ref, dh1_ref, dy_ref, dzs_ref, dat_ref, dza_ref,
             dwglu_ref, dbglu_ref, dwout_ref, dgpost_ref, dwgate_ref, dbgate_ref, dwproj_ref,
             dwout16_ref, dwgate16_ref, dwproj16_ref, dwglu16_ref):
        i = pl.program_id(0)
        y = y_ref[0]
        u3 = GELU_C * (y + GELU_K * y * y * y)
        th = jnp.tanh(u3)
        gl = 0.5 * y * (1.0 + th)
        a = _mm(gl, wglu_ref[...]) + bglu_ref[...]
        sa = _sigmoid(a)
        glu = gl * sa
        zs = zs_ref[...]
        sgs = _sigmoid(zs)
        ssm_out = glu * (zs * sgs)
        za = za_ref[...]
        sga = _sigmoid(za)
        at = at_ref[...]
        attn_out = at * (za * sga)
        cat = jnp.concatenate([ssm_out, attn_out], axis=-1).astype(BF16)
        mixed = _mm(cat, wout_ref[...])
        r2 = lax.rsqrt(jnp.mean(mixed * mixed, axis=-1, keepdims=True) + EPS)
        nhat = mixed * r2
        gpost = gpost_ref[...]
        h1 = x_ref[...] + nhat * gpost
        gate = _sigmoid(_mm(h1, wgate_ref[...]) + bgate_ref[...])
        pv = p_ref[...]
        pp = _mm(pv, wproj_ref[...])
        h2 = h1 + gate * pp
        err = h2 - tg_ref[...]
        loss_part = jnp.sum(jnp.sum(err * err, axis=-1, keepdims=True), axis=0, keepdims=True) * (0.5 / D_MODEL)
        dh2 = err * (1.0 / D_MODEL)
        dgp = dh2 * pp * gate * (1.0 - gate)
        dpp = dh2 * gate
        dh1 = dh2 + _mm_nt(dgp, wgate_ref[...])
        dh1_ref[...] = dh1
        dnhat = dh1 * gpost
        dmixed = r2 * (dnhat - nhat * jnp.mean(dnhat * nhat, axis=-1, keepdims=True))
        dcat = _mm_nt(dmixed, wout_ref[...])
        dso, dao = dcat[:, 0:D_SSM], dcat[:, D_SSM:]
        dat_ref[...] = dao * (za * sga)
        dza_ref[...] = (dao * at * (sga * (1.0 + za * (1.0 - sga)))).astype(BF16)
        dzs_ref[...] = (dso * glu * (sgs * (1.0 + zs * (1.0 - sgs)))).astype(BF16)
        dglu = dso * (zs * sgs)
        da = dglu * gl * sa * (1.0 - sa)
        dgl = dglu * sa + _mm_nt(da, wglu_ref[...])
        dgelu = 0.5 * (1.0 + th) + 0.5 * y * (1.0 - th * th) * (GELU_C * (1.0 + 3.0 * GELU_K * y * y))
        dy_ref[0] = dgl * dgelu
        parts = (
            (dwglu_ref, _mm_tn(gl, da)), (dbglu_ref, jnp.sum(da, axis=0, keepdims=True)),
            (dwout_ref, _mm_tn(cat, dmixed)), (dgpost_ref, jnp.sum(dh1 * nhat, axis=0, keepdims=True)),
            (dwgate_ref, _mm_tn(h1, dgp)), (dbgate_ref, jnp.sum(dgp, axis=0, keepdims=True)),
            (dwproj_ref, _mm_tn(pv, dpp)), (loss_ref, loss_part),
        )

        @pl.when(i == 0)
        def _():
            for ref, val in parts:
                ref[...] = val

        @pl.when(i != 0)
        def _():
            for ref, val in parts:
                ref[...] += val

        @pl.when(i == t // tm - 1)
        def _():
            for ref16, ref in ((dwout16_ref, dwout_ref), (dwgate16_ref, dwgate_ref), (dwproj16_ref, dwproj_ref),
                               (dwglu16_ref, dwglu_ref)):
                def to16(r, ref16=ref16, ref=ref):
                    ref16[r, :] = ref[r, :].astype(BF16)

                _row_chunks(ref.shape[0], to16)

    row = lambda w: pl.BlockSpec((tm, w), lambda i: (i, 0))
    perm = pl.BlockSpec((1, tm, D_SSM), lambda i: (i // N_SEG, 0, i % N_SEG))
    perm_shape = jax.ShapeDtypeStruct((bl, seg, N_SEG * D_SSM), F32)
    acc = lambda r, c, dt=F32: (_const_spec((r, c)), jax.ShapeDtypeStruct((r, c), dt))
    accs = [acc(D_SSM, D_SSM), acc(1, D_SSM), acc(D_MODEL, D_MODEL), acc(1, D_MODEL), acc(D_MODEL, D_MODEL),
            acc(1, D_MODEL), acc(D_PLE, D_MODEL),
            acc(D_MODEL, D_MODEL, BF16), acc(D_MODEL, D_MODEL, BF16), acc(D_PLE, D_MODEL, BF16), acc(D_SSM, D_SSM, BF16)]
    return pl.pallas_call(
        body, name="mix_forward_backward", grid=(t // tm,),
        in_specs=[row(D_MODEL), perm, row(512), row(512), row(512), row(D_PLE), row(D_MODEL),
                  _const_spec((D_SSM, D_SSM)), _const_spec((1, D_SSM)), _const_spec((D_MODEL, D_MODEL)),
                  _const_spec((1, D_MODEL)), _const_spec((D_MODEL, D_MODEL)), _const_spec((1, D_MODEL)),
                  _const_spec((D_PLE, D_MODEL))],
        out_specs=(_const_spec((1, 1)), row(D_MODEL), perm, row(512), row(512), row(512)) + tuple(a[0] for a in accs),
        out_shape=(jax.ShapeDtypeStruct((1, 1), F32), jax.ShapeDtypeStruct((t, D_MODEL), F32), perm_shape,
                   jax.ShapeDtypeStruct((t, 512), BF16), jax.ShapeDtypeStruct((t, 512), F32),
                   jax.ShapeDtypeStruct((t, 512), BF16)) + tuple(a[1] for a in accs),
        compiler_params=_tc_params(("arbitrary",)),
    )(x2, y_perm, z_ssm, attn, z_attn, p2, target2, w_glu, b_glu, w_out, g_post, w_gate, b_gate, w_proj)


def _in_backward(x2, dh1, du_perm, dz_ssm, dq, dk, dv, dz_attn, g_pre, w_in, bl, seg):
    t = x2.shape[0]
    tm = seg

    def body(x_ref, dh1_ref, du_ref, dzs_ref, dq_ref, dk_ref, dv_ref, dza_ref, g_ref, w_ref,
             gx_ref, dw_ref, dg_ref, dw16_ref):
        i = pl.program_id(0)
        xv = x_ref[...]
        r = lax.rsqrt(jnp.mean(xv * xv, axis=-1, keepdims=True) + EPS)
        xhat = xv * r
        g = g_ref[...]
        hn = (xhat * g).astype(BF16)
        dproj = jnp.concatenate([du_ref[0].astype(BF16), dzs_ref[...].astype(BF16), dq_ref[...].astype(BF16),
                                 dk_ref[...].astype(BF16), dv_ref[...].astype(BF16), dza_ref[...].astype(BF16)],
                                axis=-1)
        dhn = _mm(dproj, w_ref[...])
        dxhat = dhn * g
        gx_ref[...] = dh1_ref[...] + r * (dxhat - xhat * jnp.mean(dxhat * xhat, axis=-1, keepdims=True))
        dwp = _mm_tn(dproj, hn)
        dgp = jnp.sum(dhn * xhat, axis=0, keepdims=True)

        @pl.when(i == 0)
        def _():
            dw_ref[...] = dwp
            dg_ref[...] = dgp

        @pl.when(i != 0)
        def _():
            dw_ref[...] += dwp
            dg_ref[...] += dgp

        @pl.when(i == t // tm - 1)
        def _():
            def to16(r):
                dw16_ref[r, :] = dw_ref[r, :].astype(BF16)

            _row_chunks(D_IN, to16)

    row = lambda w: pl.BlockSpec((tm, w), lambda i: (i, 0))
    perm = pl.BlockSpec((1, tm, D_SSM), lambda i: (i // N_SEG, 0, i % N_SEG))
    return pl.pallas_call(
        body, name="in_backward", grid=(t // tm,),
        in_specs=[row(D_MODEL), row(D_MODEL), perm, row(512), row(512), row(128), row(128), row(512),
                  _const_spec((1, D_MODEL)), _const_spec((D_IN, D_MODEL))],
        out_specs=(row(D_MODEL), _const_spec((D_IN, D_MODEL)), _const_spec((1, D_MODEL)),
                   _const_spec((D_IN, D_MODEL))),
        out_shape=(jax.ShapeDtypeStruct((t, D_MODEL), F32), jax.ShapeDtypeStruct((D_IN, D_MODEL), F32),
                   jax.ShapeDtypeStruct((1, D_MODEL), F32), jax.ShapeDtypeStruct((D_IN, D_MODEL), BF16)),
        compiler_params=_tc_params(("arbitrary",)),
    )(x2, dh1, du_perm, dz_ssm, dq, dk, dv, dz_attn, g_pre, w_in)


def _block_diag(t):
    a, b = t.shape[1], t.shape[2]
    eye = jnp.eye(G_TILE, dtype=t.dtype)
    t = t.reshape(N_GT, G_TILE, a, 1, b) * eye[None, :, None, :, None]
    return t.reshape(N_GT, G_TILE * a, G_TILE * b)


def _diag_blocks(m, a, b):
    m = m.reshape(N_GT, G_TILE, a, G_TILE, b)
    return jnp.einsum("tgagb->tgab", m).reshape(SSM_G, a, b)


def _local_step(x, p, target, pre_norm_g, w_in, ssm_lam_re, ssm_lam_im, ssm_log_step, ssm_b_re, ssm_b_im, ssm_c_re,
                ssm_c_im, ssm_d, ssm_b_glu, attn_sinks, post_norm_g, pl_b_gate, late, me):
    bl, seq, _ = x.shape
    seg = seq // N_SEG
    nb = seq // ATT_BLOCK
    t = bl * seq
    x2 = x.reshape(t, D_MODEL)
    p2 = p.reshape(t, D_PLE)
    tg2 = target.reshape(t, D_MODEL)

    lam_re, lam_im = ssm_lam_re, ssm_lam_im
    log_step = ssm_log_step.reshape(SSM_G, 1)
    a_re_row, a_im_row, bb_re, bb_im, pw_re, pw_im = _ssm_prep(lam_re, lam_im, log_step, ssm_b_re, ssm_b_im, seg)
    by_group = lambda t: t.reshape(SSM_G, SSM_P, SSM_N)
    bcat = jnp.concatenate([_block_diag(by_group(bb_re)), _block_diag(by_group(bb_im))], axis=-1).astype(BF16)
    ccat_t = jnp.concatenate([_block_diag(by_group(ssm_c_re)), -_block_diag(by_group(ssm_c_im))],
                             axis=-1).astype(BF16)
    bcat_t = jnp.swapaxes(bcat, 1, 2)
    ccat = jnp.swapaxes(ccat_t, 1, 2)
    d_row = ssm_d.reshape(1, D_SSM)

    (u_perm, z_ssm, q, k, v, z_attn), (g_out, g_glu) = _in_proj(
        x2, pre_norm_g.reshape(1, D_MODEL), w_in, [late[0], late[3]], bl, seg)
    u_perm = u_perm.reshape(bl, seq, D_SSM)
    y_perm, states, carries = _ssm_forward(u_perm, bcat, ccat, a_re_row, a_im_row, pw_re, pw_im, d_row, seg)
    sinks = attn_sinks.reshape(N_HEADS)
    (attn, lse), (g_gate, g_proj) = _attn_forward(q, k, v, sinks, [late[1], late[2]], bl, nb)
    w_out, w_gate, w_proj, w_glu = (_gathered_to_full(n, g) for n, g in zip(LATE_NAMES, (g_out, g_gate, g_proj, g_glu)))
    (loss, dh1, dy_perm, dz_ssm, dattn, dz_attn, d_w_glu, d_b_glu, d_w_out, d_g_post, d_w_gate, d_b_gate,
     d_w_proj, *late16) = _mix_forward_backward(
        x2, y_perm.reshape(bl, seg, N_SEG * D_SSM), z_ssm, attn, z_attn, p2, tg2, w_glu,
        ssm_b_glu.reshape(1, D_SSM), w_out, post_norm_g.reshape(1, D_MODEL), w_gate, pl_b_gate.reshape(1, D_MODEL),
        w_proj, bl, seg)
    owned = [_full_to_owned(n, d) for n, d in zip(LATE_NAMES, (d_w_out, d_w_gate, d_w_proj, d_w_glu))]
    (dq, dk, dv, d_sinks), late_grads = _attn_backward(
        q, k, v, attn, dattn, lse, sinks, [_full_to_owned(n, d) for n, d in zip(LATE_NAMES, late16)],
        [lax.dynamic_index_in_dim(o, me, axis=0, keepdims=False) for o in owned], bl, nb)
    du_perm, d_bcat, d_ccat_t, da_re, da_im, d_d = _ssm_backward(
        u_perm, dy_perm.reshape(bl, seq, D_SSM), states, carries, bcat_t, ccat_t, a_re_row, a_im_row, pw_re, pw_im,
        d_row, seg)
    grad_x, d_w_in, d_g_pre, d_w_in16 = _in_backward(
        x2, dh1, du_perm.reshape(bl, seg, N_SEG * D_SSM), dz_ssm, dq, dk, dv, dz_attn,
        pre_norm_g.reshape(1, D_MODEL), w_in, bl, seg)
    flat = lambda t: t.reshape(SSM_G * SSM_P, SSM_N)
    d_lam_re, d_lam_im, d_ls, d_b_re, d_b_im = _ssm_param_grads(
        lam_re, lam_im, log_step, ssm_b_re, ssm_b_im, da_re.reshape(SSM_G, SSM_N), da_im.reshape(SSM_G, SSM_N),
        flat(_diag_blocks(d_bcat[:, :, 0:ST_T], SSM_P, SSM_N)), flat(_diag_blocks(d_bcat[:, :, ST_T:], SSM_P, SSM_N)))
    grads = {
        "pre_norm_g": d_g_pre, "w_in": d_w_in, "w_in16": d_w_in16, "ssm_lam_re": d_lam_re, "ssm_lam_im": d_lam_im,
        "ssm_log_step": d_ls, "ssm_b_re": d_b_re, "ssm_b_im": d_b_im,
        "ssm_c_re": _diag_blocks(d_ccat_t[:, :, 0:ST_T], SSM_P, SSM_N),
        "ssm_c_im": -_diag_blocks(d_ccat_t[:, :, ST_T:], SSM_P, SSM_N),
        "ssm_d": d_d, "ssm_b_glu": d_b_glu, "attn_sinks": d_sinks, "post_norm_g": d_g_post, "pl_b_gate": d_b_gate,
    }
    return loss, grad_x.reshape(bl, seq, D_MODEL), grads, late_grads


LATE_NAMES = ("w_out", "pl_w_gate", "pl_w_proj", "ssm_w_glu")
BIG_NAMES = ("w_in",) + LATE_NAMES
COL_SHARDED = {"w_in": D_IN // N_DEV, "pl_w_proj": D_MODEL // N_DEV}
WEIGHT_NAMES = ("pre_norm_g", "w_in", "ssm_lam_re", "ssm_lam_im", "ssm_log_step", "ssm_b_re", "ssm_b_im", "ssm_c_re",
                "ssm_c_im", "ssm_d", "ssm_w_glu", "ssm_b_glu", "attn_sinks", "w_out", "post_norm_g", "pl_w_proj",
                "pl_w_gate", "pl_b_gate")


TRANSPOSED = {"w_in": (0, 1), "ssm_b_re": (1, 2), "ssm_b_im": (1, 2)}


def _kernel_form(name, a):
    a = a[0]
    if name in TRANSPOSED:
        a = jnp.swapaxes(a, *TRANSPOSED[name])
    if name in ("ssm_b_re", "ssm_b_im", "ssm_c_re", "ssm_c_im"):
        a = a.reshape(SSM_G * SSM_P, SSM_N)
    return a


def _given_form(name, a, shape):
    if name in TRANSPOSED:
        i, j = TRANSPOSED[name]
        swapped = list(shape[1:])
        swapped[i], swapped[j] = swapped[j], swapped[i]
        return jnp.swapaxes(a.reshape(swapped), i, j).reshape(shape)
    return a.reshape(shape)


def _gathered_to_full(name, g):
    _, rows, cols = g.shape
    if name in COL_SHARDED:
        return jnp.swapaxes(g, 0, 1).reshape(rows, N_DEV * cols)
    return g.reshape(N_DEV * rows, cols)


def _full_to_owned(name, full):
    if name in COL_SHARDED:
        return jnp.swapaxes(full.reshape(full.shape[0], N_DEV, COL_SHARDED[name]), 0, 1)
    return full.reshape(N_DEV, full.shape[0] // N_DEV, full.shape[1])


def kernel(x, p, pre_norm_g, w_in, ssm_lam_re, ssm_lam_im, ssm_log_step, ssm_b_re, ssm_b_im, ssm_c_re, ssm_c_im, ssm_d, ssm_w_glu, ssm_b_glu, attn_sinks, w_out, post_norm_g, pl_w_proj, pl_w_gate, pl_b_gate, loss_target, m_pre_norm_g, m_w_in, m_ssm_lam_re, m_ssm_lam_im, m_ssm_log_step, m_ssm_b_re, m_ssm_b_im, m_ssm_c_re, m_ssm_c_im, m_ssm_d, m_ssm_w_glu, m_ssm_b_glu, m_attn_sinks, m_w_out, m_post_norm_g, m_pl_w_proj, m_pl_w_gate, m_pl_b_gate, v_pre_norm_g, v_w_in, v_ssm_lam_re, v_ssm_lam_im, v_ssm_log_step, v_ssm_b_re, v_ssm_b_im, v_ssm_c_re, v_ssm_c_im, v_ssm_d, v_ssm_w_glu, v_ssm_b_glu, v_attn_sinks, v_w_out, v_post_norm_g, v_pl_w_proj, v_pl_w_gate, v_pl_b_gate):
    w = dict(pre_norm_g=pre_norm_g, w_in=w_in, ssm_lam_re=ssm_lam_re, ssm_lam_im=ssm_lam_im, ssm_log_step=ssm_log_step,
             ssm_b_re=ssm_b_re, ssm_b_im=ssm_b_im, ssm_c_re=ssm_c_re, ssm_c_im=ssm_c_im, ssm_d=ssm_d, ssm_w_glu=ssm_w_glu,
             ssm_b_glu=ssm_b_glu, attn_sinks=attn_sinks, w_out=w_out, post_norm_g=post_norm_g, pl_w_proj=pl_w_proj,
             pl_w_gate=pl_w_gate, pl_b_gate=pl_b_gate)
    m = dict(pre_norm_g=m_pre_norm_g, w_in=m_w_in, ssm_lam_re=m_ssm_lam_re, ssm_lam_im=m_ssm_lam_im,
             ssm_log_step=m_ssm_log_step, ssm_b_re=m_ssm_b_re, ssm_b_im=m_ssm_b_im, ssm_c_re=m_ssm_c_re,
             ssm_c_im=m_ssm_c_im, ssm_d=m_ssm_d, ssm_w_glu=m_ssm_w_glu, ssm_b_glu=m_ssm_b_glu, attn_sinks=m_attn_sinks,
             w_out=m_w_out, post_norm_g=m_post_norm_g, pl_w_proj=m_pl_w_proj, pl_w_gate=m_pl_w_gate,
             pl_b_gate=m_pl_b_gate)
    v = dict(pre_norm_g=v_pre_norm_g, w_in=v_w_in, ssm_lam_re=v_ssm_lam_re, ssm_lam_im=v_ssm_lam_im,
             ssm_log_step=v_ssm_log_step, ssm_b_re=v_ssm_b_re, ssm_b_im=v_ssm_b_im, ssm_c_re=v_ssm_c_re,
             ssm_c_im=v_ssm_c_im, ssm_d=v_ssm_d, ssm_w_glu=v_ssm_w_glu, ssm_b_glu=v_ssm_b_glu, attn_sinks=v_attn_sinks,
             w_out=v_w_out, post_norm_g=v_post_norm_g, pl_w_proj=v_pl_w_proj, pl_w_gate=v_pl_w_gate,
             pl_b_gate=v_pl_b_gate)
    me = _slot(lax.axis_index("x"), lax.axis_index("y"), lax.axis_index("c"))
    kf = lambda d: {n: _kernel_form(n, a) for n, a in d.items()}
    wk, mk, vk = kf(w), kf(m), kf(v)

    (gathered,) = _allgather_weights([wk["w_in"]])
    loss, grad_x, grads, g_late = _local_step(
        x, p[0], loss_target, wk["pre_norm_g"], gathered.reshape(D_IN, D_MODEL), wk["ssm_lam_re"], wk["ssm_lam_im"],
        wk["ssm_log_step"], wk["ssm_b_re"], wk["ssm_b_im"], wk["ssm_c_re"], wk["ssm_c_im"], wk["ssm_d"],
        wk["ssm_b_glu"], wk["attn_sinks"], wk["post_norm_g"], wk["pl_b_gate"], [wk[n] for n in LATE_NAMES], me)

    owned = lambda g: g.reshape(N_DEV, D_IN // N_DEV, D_MODEL)
    tiny_form = lambda d: [d[n].reshape(rows, cols) for n, rows, cols in TINY]
    med_form = lambda d: [d[n].reshape(N_DEV, rows // N_DEV, cols) for n, rows, cols in MEDIUM]
    g_big, loss, g_tiny, g_med = _reduce_final(
        [owned(grads["w_in16"])], [lax.dynamic_index_in_dim(owned(grads["w_in"]), me, axis=0, keepdims=False)],
        loss, tiny_form(grads), med_form(grads))
    names = BIG_NAMES + tuple(n for n, _, _ in TINY + MEDIUM)
    form = lambda d: [d[n] for n in BIG_NAMES] + tiny_form(d) + med_form(d)
    updated = _adamw_update(g_big + g_late + g_tiny + g_med, form(wk), form(mk), form(vk))
    vals = dict(zip(names, updated))
    results = [[_given_form(n, vals[n][kind], w[n].shape) for n in WEIGHT_NAMES] for kind in range(4)]
    return (loss.reshape(()), grad_x, *results[0], *results[1], *results[2], *results[3])
```

```python
import functools
import math

import jax
import jax.numpy as jnp
from jax import lax
from jax.experimental import pallas as pl
from jax.experimental.pallas import tpu as pltpu

F32 = jnp.float32
BF16 = jnp.bfloat16

D_MODEL = 1024
D_SSM = 512
D_ATTN = 512
SSM_P = 16
SSM_G = 32
SSM_N = 64
N_HEADS = 8
KV_HEADS = 2
Q_PER_KV = 4
HEAD_DIM = 64
ATT_BLOCK = 128
D_PLE = 256
D_IN = 2304
EPS = 1e-6
N_DEV = 8
N_SEG = 8
G_TILE = 8
N_GT = SSM_G // G_TILE
CH_T = G_TILE * SSM_P
ST_T = G_TILE * SSM_N
N_STATE = SSM_G * SSM_N
SCAN_UNROLL = 4
MIX_GROUPS = 1
LANES = 128
VMEM_LIMIT = 60 * 1024 * 1024

ADAM_LR = 0.001
ADAM_B1 = 0.9
ADAM_B2 = 0.999
ADAM_EPS = 1e-08
ADAM_WD = 0.01
ADAM_STEP = 10

GELU_C = math.sqrt(2.0 / math.pi)
GELU_K = 0.044715
ATT_SCALE = 1.0 / math.sqrt(HEAD_DIM)
NEG_BIG = -1e30


def _mm(a, b):
    return jnp.dot(a.astype(BF16), b.astype(BF16), preferred_element_type=F32)


def _mm_nt(a, b):
    return lax.dot_general(a.astype(BF16), b.astype(BF16), (((1,), (1,)), ((), ())), preferred_element_type=F32)


def _mm_tn(a, b):
    return lax.dot_general(a.astype(BF16), b.astype(BF16), (((0,), (0,)), ((), ())), preferred_element_type=F32)


def _sigmoid(x):
    return 1.0 / (1.0 + jnp.exp(-x))


def _tc_params(sem):
    return pltpu.CompilerParams(dimension_semantics=sem, vmem_limit_bytes=VMEM_LIMIT)


def _const_spec(shape):
    nd = len(shape)
    return pl.BlockSpec(shape, lambda *_: (0,) * nd)


def _mesh_pos():
    return lax.axis_index("x"), lax.axis_index("y"), lax.axis_index("c")


ROW_CHUNKS = (64, 32, 16)


def _row_chunk(nrows):
    return next((c for c in ROW_CHUNKS if nrows % c == 0), None)


def _row_chunks(nrows, fn, chunk=None, init=None):
    chunk = chunk or _row_chunk(nrows)

    def step(i, carry):
        rows = pl.ds(pl.multiple_of(i * chunk, chunk), chunk)
        if init is None:
            fn(rows)
            return carry
        return fn(rows, carry)

    return lax.fori_loop(0, nrows // chunk, step, 0 if init is None else init)


def _slot(px, py, pc):
    return 4 * px + 2 * py + pc


def _allgather_weights(shards):
    n = len(shards)

    def body(*refs):
        srcs, outs, (send_sems, recv_sems) = refs[:n], refs[n:2 * n], refs[2 * n:]
        x, y, c = _mesh_pos()
        me, sibling = (x, y, c), (x, y, 1 - c)
        chips = [(1 - x, y), (x, 1 - y), (1 - x, 1 - y)]

        def copy(a, k, block, to):
            blk = outs[a].at[_slot(*block)]
            return pltpu.make_async_remote_copy(
                src_ref=blk, dst_ref=blk, send_sem=send_sems.at[7 * a + k], recv_sem=recv_sems.at[7 * a + k],
                device_id=to, device_id_type=pl.DeviceIdType.MESH)

        sends = []
        for a in range(n):
            mine = outs[a].at[_slot(*me)]

            def cast(r, mine=mine, src=srcs[a]):
                mine[r, :] = src[r, :].astype(BF16)

            _row_chunks(srcs[a].shape[0], cast)
            first = [copy(a, 0, me, sibling)] + [copy(a, 1 + j, me, (*chip, c)) for j, chip in enumerate(chips)]
            for cp in first:
                cp.start()
            sends += first
        for a in range(n):
            for j, chip in enumerate(chips):
                copy(a, 1 + j, (*chip, c), me).wait_recv()
                fwd = copy(a, 4 + j, (*chip, c), sibling)
                fwd.start()
                sends.append(fwd)
        for a in range(n):
            copy(a, 0, sibling, me).wait_recv()
            for j, chip in enumerate(chips):
                copy(a, 4 + j, (*chip, 1 - c), me).wait_recv()
        for cp in sends:
            cp.wait_send()

    vm = pl.BlockSpec(memory_space=pltpu.VMEM)
    return pl.pallas_call(
        body, name="allgather_weights",
        out_shape=tuple(jax.ShapeDtypeStruct((N_DEV,) + s.shape, BF16) for s in shards),
        in_specs=[vm] * n, out_specs=(vm,) * n,
        scratch_shapes=[pltpu.SemaphoreType.DMA((7 * n,)), pltpu.SemaphoreType.DMA((7 * n,))],
        compiler_params=pltpu.CompilerParams(vmem_limit_bytes=VMEM_LIMIT),
    )(*shards)


def _adamw(w, g, m, v):
    m = ADAM_B1 * m + (1.0 - ADAM_B1) * g
    v = ADAM_B2 * v + (1.0 - ADAM_B2) * (g * g)
    m_hat = m / (1.0 - ADAM_B1 ** ADAM_STEP)
    v_hat = v / (1.0 - ADAM_B2 ** ADAM_STEP)
    delta = -ADAM_LR * (m_hat / (jnp.sqrt(v_hat) + ADAM_EPS) + ADAM_WD * w)
    return delta, m, v


def _remote(src, dst, send_sems, recv_sems, k, to):
    return pltpu.make_async_remote_copy(src_ref=src, dst_ref=dst, send_sem=send_sems.at[k], recv_sem=recv_sems.at[k],
                                        device_id=to, device_id_type=pl.DeviceIdType.MESH)


def _big_reduce_phases(g16_r, go_r, outs, send2, recv1, recv2, s_send, s_recv):
    n = len(g16_r)
    x, y, c = _mesh_pos()
    sibling = (x, y, 1 - c)
    chips = [(1 - x, y), (x, 1 - y), (1 - x, 1 - y)]
    all_chips = [(x, y)] + chips
    lvl1 = []
    for a in range(n):
        cps = [_remote(g16_r[a].at[_slot(*chip, 1 - c)], recv1[a].at[j], s_send, s_recv, 7 * a + j, sibling)
               for j, chip in enumerate(all_chips)]
        for cp in cps:
            cp.start()
        lvl1.append(cps)
    yield
    lvl2 = []
    for a in range(n):
        for cp in lvl1[a]:
            cp.wait_recv()
        og = outs[a]

        def partials(r, a=a, og=og):
            og[r, :] = go_r[a][r, :] + recv1[a][0, r, :].astype(F32)
            for j, chip in enumerate(chips):
                mine16 = g16_r[a][_slot(*chip, c), r, :].astype(F32)
                send2[a][j, r, :] = (mine16 + recv1[a][1 + j, r, :].astype(F32)).astype(BF16)

        _row_chunks(go_r[a].shape[0], partials)
        cps = [_remote(send2[a].at[j], recv2[a].at[j], s_send, s_recv, 7 * a + 4 + j, (*chip, c))
               for j, chip in enumerate(chips)]
        for cp in cps:
            cp.start()
        lvl2.append(cps)
    yield
    for a in range(n):
        for cp in lvl2[a]:
            cp.wait_recv()
        og = outs[a]

        def total(r, a=a, og=og):
            g = og[r, :]
            for j in range(3):
                g = g + recv2[a][j, r, :].astype(F32)
            og[r, :] = g

        _row_chunks(go_r[a].shape[0], total)
    yield
    for cps in lvl1 + lvl2:
        for cp in cps:
            cp.wait_send()


def _adamw_update(g, w, m, v):
    n = len(g)

    def body(*refs):
        g_r, w_r, m_r, v_r = (refs[i * n:(i + 1) * n] for i in range(4))
        outs = refs[4 * n:]
        for a in range(n):
            og, od, om, ov = outs[4 * a:4 * a + 4]

            def update(idx, a=a, og=og, od=od, om=om, ov=ov):
                gv = g_r[a][idx]
                d, nm, nv = _adamw(w_r[a][idx], gv, m_r[a][idx], v_r[a][idx])
                og[idx] = gv
                od[idx] = d
                om[idx] = nm
                ov[idx] = nv

            shape = g_r[a].shape
            if len(shape) == 3:
                for b in range(shape[0]):
                    update(b)
            elif _row_chunk(shape[0]) is not None:
                _row_chunks(shape[0], update)
            else:
                update(Ellipsis)

    vm = pl.BlockSpec(memory_space=pltpu.VMEM)
    res = pl.pallas_call(
        body, name="adamw_update",
        out_shape=tuple(jax.ShapeDtypeStruct(t.shape, F32) for t in g for _ in range(4)),
        in_specs=[vm] * (4 * n), out_specs=(vm,) * (4 * n),
        compiler_params=pltpu.CompilerParams(vmem_limit_bytes=VMEM_LIMIT),
    )(*g, *w, *m, *v)
    return [res[4 * a:4 * a + 4] for a in range(n)]


TINY = (("pre_norm_g", 1, 1024), ("post_norm_g", 1, 1024), ("pl_b_gate", 1, 1024), ("ssm_d", 1, 512),
        ("ssm_b_glu", 1, 512), ("ssm_log_step", 1, 32), ("attn_sinks", 1, 8), ("ssm_lam_re", 32, 64),
        ("ssm_lam_im", 32, 64))
MEDIUM = (("ssm_b_re", SSM_G * SSM_P, SSM_N), ("ssm_b_im", SSM_G * SSM_P, SSM_N), ("ssm_c_re", SSM_G * SSM_P, SSM_N),
          ("ssm_c_im", SSM_G * SSM_P, SSM_N))


def _stage_rows():
    offs, r = {}, 0
    for name, rows, cols in TINY + (("loss", 1, 1),):
        if rows > 1:
            r = -(-r // 8) * 8
        offs[name] = r
        r += rows if rows > 1 else max(cols // LANES, 1)
    return offs, -(-r // 8) * 8


def _reduce_final(g16, gown, loss, g_tiny, g_med):
    nb_, nt, nm_ = len(g16), len(TINY), len(MEDIUM)
    offs, stage_rows = _stage_rows()

    def body(*refs):
        g16_r, go_r = refs[:nb_], refs[nb_:2 * nb_]
        base = 2 * nb_
        loss_r, gt, gm = refs[base], refs[base + 1:base + 1 + nt], refs[base + 1 + nt:base + 1 + nt + nm_]
        base += 1 + nt + nm_
        out_b = refs[base:base + nb_]
        base += nb_
        loss_o, out_t, out_m = refs[base], refs[base + 1:base + 1 + nt], refs[base + 1 + nt:base + 1 + nt + nm_]
        base += 1 + nt + nm_
        send2_b, recv1_b, recv2_b = (refs[base + i * nb_:base + (i + 1) * nb_] for i in range(3))
        base += 3 * nb_
        stage = refs[base]
        recv1, part, recv2 = (refs[base + 1 + i * nm_:base + 1 + (i + 1) * nm_] for i in range(3))
        bs_send, bs_recv, s_send, s_recv = refs[base + 1 + 3 * nm_:]
        big = _big_reduce_phases(g16_r, go_r, out_b, send2_b, recv1_b, recv2_b, bs_send, bs_recv)
        small = small_phases(loss_r, gt, gm, loss_o, out_t, out_m, stage, recv1, part, recv2, s_send, s_recv)
        next(big)
        next(small)
        next(big)
        for _ in small:
            pass
        for _ in big:
            pass

    def small_phases(loss_r, gt, gm, loss_o, out_t, out_m, stage, recv1, part, recv2, s_send, s_recv):
        x, y, c = _mesh_pos()
        me = _slot(x, y, c)
        sibling = (x, y, 1 - c)
        chips = [(1 - x, y), (x, 1 - y), (1 - x, 1 - y)]
        all_chips = [(x, y)] + chips
        peers = [sibling] + [(*chip, c) for chip in chips] + [(*chip, 1 - c) for chip in chips]
        sem = iter(range(7 + 14 * nm_))
        lvl1 = []
        for a in range(nm_):
            cps = [_remote(gm[a].at[_slot(*chip, 1 - c)], recv1[a].at[j], s_send, s_recv, next(sem), sibling)
                   for j, chip in enumerate(all_chips)]
            for cp in cps:
                cp.start()
            lvl1.append(cps)
        mine = stage.at[me]
        mine[...] = jnp.zeros((stage_rows, LANES), F32)
        for (name, rows, cols), ref in zip(TINY + (("loss", 1, 1),), gt + (loss_r,)):
            r0 = offs[name]
            if rows > 1:
                mine[r0:r0 + rows, 0:cols] = ref[...]
            elif cols >= LANES:
                for i in range(cols // LANES):
                    mine[r0 + i:r0 + i + 1, :] = ref[:, i * LANES:(i + 1) * LANES]
            else:
                mine[r0:r0 + 1, 0:cols] = ref[...]
        tiny_cps = [_remote(mine, mine, s_send, s_recv, next(sem), peer) for peer in peers]
        for cp in tiny_cps:
            cp.start()
        yield
        lvl2 = []
        for a in range(nm_):
            for cp in lvl1[a]:
                cp.wait_recv()
            for j, chip in enumerate(all_chips):
                part[a][j] = gm[a][_slot(*chip, c)] + recv1[a][j]
            cps = [_remote(part[a].at[1 + j], recv2[a].at[j], s_send, s_recv, next(sem), (*chip, c))
                   for j, chip in enumerate(chips)]
            for cp in cps:
                cp.start()
            lvl2.append(cps)
        yield
        lvl3 = []
        for a in range(nm_):
            for cp in lvl2[a]:
                cp.wait_recv()
            blk = out_m[a].at[me]
            blk[...] = ((part[a][0] + recv2[a][0]) + recv2[a][1]) + recv2[a][2]
            cps = [_remote(blk, blk, s_send, s_recv, next(sem), peer) for peer in peers]
            for cp in cps:
                cp.start()
            lvl3.append(cps)
        yield
        for cp in tiny_cps:
            cp.wait_recv()
        tot = stage[0]
        for d in range(1, N_DEV):
            tot = tot + stage[d]
        loss_o[...] = tot[offs["loss"]:offs["loss"] + 1, 0:1]
        for k, (name, rows, cols) in enumerate(TINY):
            r0 = offs[name]
            if rows > 1:
                out_t[k][...] = tot[r0:r0 + rows, 0:cols]
            elif cols >= LANES:
                for i in range(cols // LANES):
                    out_t[k][:, i * LANES:(i + 1) * LANES] = tot[r0 + i:r0 + i + 1, :]
            else:
                out_t[k][...] = tot[r0:r0 + 1, 0:cols]
        for cps in lvl3:
            for cp in cps:
                cp.wait_recv()
        for cps in lvl1 + lvl2 + lvl3 + [tiny_cps]:
            for cp in cps:
                cp.wait_send()

    vmem = pl.BlockSpec(memory_space=pltpu.VMEM)
    t_shapes = [jax.ShapeDtypeStruct((rows, cols), F32) for _, rows, cols in TINY]
    m_shapes = [jax.ShapeDtypeStruct((N_DEV, rows // N_DEV, cols), F32) for _, rows, cols in MEDIUM]
    blk = [(rows // N_DEV, cols) for _, rows, cols in MEDIUM]
    scratch = ([pltpu.VMEM((3,) + t.shape, BF16) for t in gown] + [pltpu.VMEM((4,) + t.shape, BF16) for t in gown]
               + [pltpu.VMEM((3,) + t.shape, BF16) for t in gown]
               + [pltpu.VMEM((N_DEV, stage_rows, LANES), F32)]
               + [pltpu.VMEM((4,) + b, F32) for b in blk] + [pltpu.VMEM((4,) + b, F32) for b in blk]
               + [pltpu.VMEM((3,) + b, F32) for b in blk]
               + [pltpu.SemaphoreType.DMA((7 * nb_,)), pltpu.SemaphoreType.DMA((7 * nb_,)),
                  pltpu.SemaphoreType.DMA((7 + 14 * nm_,)), pltpu.SemaphoreType.DMA((7 + 14 * nm_,))])
    n_in, n_out = 2 * nb_ + 1 + nt + nm_, nb_ + 1 + nt + nm_
    res = pl.pallas_call(
        body, name="reduce_final",
        out_shape=tuple(jax.ShapeDtypeStruct(t.shape, F32) for t in gown) + (jax.ShapeDtypeStruct((1, 1), F32),)
        + tuple(t_shapes) + tuple(m_shapes),
        in_specs=[vmem] * n_in, out_specs=(vmem,) * n_out, scratch_shapes=scratch,
        compiler_params=pltpu.CompilerParams(vmem_limit_bytes=VMEM_LIMIT),
    )(*g16, *gown, loss, *g_tiny, *g_med)
    return list(res[:nb_]), res[nb_], list(res[nb_ + 1:nb_ + 1 + nt]), list(res[nb_ + 1 + nt:])


def _gather_phases(shard_r, gath, cast, send_sems, recv_sems, local_sems):
    n = len(shard_r)
    x, y, c = _mesh_pos()
    me, sibling = (x, y, c), (x, y, 1 - c)
    chips = [(1 - x, y), (x, 1 - y), (1 - x, 1 - y)]

    def own(a, k, to):
        return _remote(cast[a], gath[a].at[_slot(*me)], send_sems, recv_sems, 7 * a + k, to)

    def passed(a, k, block, to):
        blk = gath[a].at[_slot(*block)]
        return _remote(blk, blk, send_sems, recv_sems, 7 * a + k, to)

    def keep(a):
        return pltpu.make_async_copy(cast[a], gath[a].at[_slot(*me)], local_sems.at[a])

    def start():
        for a in range(n):
            def to16(r, a=a):
                cast[a][r, :] = shard_r[a][r, :].astype(BF16)

            _row_chunks(shard_r[a].shape[0], to16)
            keep(a).start()
            own(a, 0, sibling).start()
            for j, chip in enumerate(chips):
                own(a, 1 + j, (*chip, c)).start()

    def relay():
        for a in range(n):
            for j, chip in enumerate(chips):
                passed(a, 1 + j, (*chip, c), me).wait_recv()
                passed(a, 4 + j, (*chip, c), sibling).start()

    def finish():
        for a in range(n):
            passed(a, 0, sibling, me).wait_recv()
            for j, chip in enumerate(chips):
                passed(a, 4 + j, (*chip, 1 - c), me).wait_recv()
            own(a, 0, sibling).wait_send()
            for j, chip in enumerate(chips):
                own(a, 1 + j, (*chip, c)).wait_send()
                passed(a, 4 + j, (*chip, c), sibling).wait_send()
            keep(a).wait()

    return start, relay, finish


def _gather_operands(shards):
    n = len(shards)
    return ((pl.BlockSpec(memory_space=pl.ANY),) * n,
            tuple(jax.ShapeDtypeStruct((N_DEV,) + s.shape, BF16) for s in shards),
            [pltpu.VMEM(s.shape, BF16) for s in shards]
            + [pltpu.SemaphoreType.DMA((7 * n,)), pltpu.SemaphoreType.DMA((7 * n,)), pltpu.SemaphoreType.DMA((n,))])


def _in_proj(x2, g_pre, w_in, late, bl, seg):
    t = x2.shape[0]
    tm = seg
    steps = t // tm
    n = len(late)
    forward_step, last_step = (steps * 5) // 8, steps - 1

    def body(*refs):
        x_ref, g_ref, w_ref = refs[:3]
        late_r = refs[3:3 + n]
        u_ref, zs_ref, q_ref, k_ref, v_ref, za_ref = refs[3 + n:9 + n]
        gath = refs[9 + n:9 + 2 * n]
        cast = refs[9 + 2 * n:9 + 3 * n]
        send_sems, recv_sems, local_sems = refs[9 + 3 * n:]
        i = pl.program_id(0)
        start, relay, finish = _gather_phases(late_r, gath, cast, send_sems, recv_sems, local_sems)
        pl.when(i == 0)(start)
        xv = x_ref[...]
        r = lax.rsqrt(jnp.mean(xv * xv, axis=-1, keepdims=True) + EPS)
        hn = xv * r * g_ref[...]
        proj = _mm_nt(hn, w_ref[...])
        u_ref[0] = proj[:, 0:512]
        zs_ref[...] = proj[:, 512:1024]
        q_ref[...] = proj[:, 1024:1536].astype(BF16)
        k_ref[...] = proj[:, 1536:1664].astype(BF16)
        v_ref[...] = proj[:, 1664:1792].astype(BF16)
        za_ref[...] = proj[:, 1792:2304]
        pl.when(i == forward_step)(relay)
        pl.when(i == last_step)(finish)

    row = lambda w: pl.BlockSpec((tm, w), lambda i: (i, 0))
    g_specs, g_shapes, g_scratch = _gather_operands(late)
    res = pl.pallas_call(
        body, name="in_proj", grid=(steps,),
        in_specs=[row(D_MODEL), _const_spec((1, D_MODEL)), _const_spec((D_IN, D_MODEL))]
        + [_const_spec(s.shape) for s in late],
        out_specs=(pl.BlockSpec((1, tm, D_SSM), lambda i: (i // N_SEG, 0, i % N_SEG)),
                   row(512), row(512), row(128), row(128), row(512)) + g_specs,
        out_shape=(jax.ShapeDtypeStruct((bl, seg, N_SEG * D_SSM), F32),
                   jax.ShapeDtypeStruct((t, 512), F32), jax.ShapeDtypeStruct((t, 512), BF16),
                   jax.ShapeDtypeStruct((t, 128), BF16), jax.ShapeDtypeStruct((t, 128), BF16),
                   jax.ShapeDtypeStruct((t, 512), F32)) + g_shapes,
        scratch_shapes=g_scratch,
        compiler_params=_tc_params(("arbitrary",)),
    )(x2, g_pre, w_in, *late)
    return res[:6], list(res[6:])


def _discretise(lr, li, ls):
    step = jnp.exp(ls)
    mag = jnp.exp(lr * step)
    ar = mag * jnp.cos(li * step)
    ai = mag * jnp.sin(li * step)
    den = lr * lr + li * li
    cr = ((ar - 1.0) * lr + ai * li) / den
    ci = (ai * lr - (ar - 1.0) * li) / den
    return step, ar, ai, den, cr, ci


def _per_channel(v):
    return jnp.broadcast_to(v[:, None, :], (SSM_G, SSM_P, SSM_N)).reshape(SSM_G * SSM_P, SSM_N)


def _ssm_prep(lam_re, lam_im, log_step, b_re, b_im, seg):
    def body(lr_ref, li_ref, ls_ref, br_ref, bi_ref, lrr_ref, lir_ref, lsr_ref,
             ar_ref, ai_ref, bbr_ref, bbi_ref, pr_ref, pi_ref):
        _, _, _, _, cr, ci = _discretise(lr_ref[...], li_ref[...], ls_ref[...])
        cr, ci = _per_channel(cr), _per_channel(ci)
        br, bi = br_ref[...], bi_ref[...]
        bbr_ref[...] = cr * br - ci * bi
        bbi_ref[...] = cr * bi + ci * br
        stepr = jnp.exp(lsr_ref[...])
        k = (lax.broadcasted_iota(jnp.int32, (8, N_STATE), 0) + 1).astype(F32)
        magk = jnp.exp(k * (lrr_ref[...] * stepr))
        ang = k * (lir_ref[...] * stepr)
        pr_ref[0:8, :] = magk * jnp.cos(ang)
        pi_ref[0:8, :] = magk * jnp.sin(ang)
        n = 8
        while n < seg:
            tr, ti = pr_ref[n - 1:n, :], pi_ref[n - 1:n, :]
            xr, xi = pr_ref[0:n, :], pi_ref[0:n, :]
            pr_ref[n:2 * n, :] = xr * tr - xi * ti
            pi_ref[n:2 * n, :] = xr * ti + xi * tr
            n *= 2
        ar_ref[...] = pr_ref[0:1, :]
        ai_ref[...] = pi_ref[0:1, :]

    row = jax.ShapeDtypeStruct((1, N_STATE), F32)
    mat = jax.ShapeDtypeStruct((SSM_G * SSM_P, SSM_N), F32)
    pw = jax.ShapeDtypeStruct((seg, N_STATE), F32)
    vm = pl.BlockSpec(memory_space=pltpu.VMEM)
    step_row = jnp.broadcast_to(log_step, (SSM_G, SSM_N)).reshape(1, N_STATE)
    return pl.pallas_call(
        body, name="ssm_prep", out_shape=(row, row, mat, mat, pw, pw),
        in_specs=[vm] * 8, out_specs=(vm,) * 6,
    )(lam_re, lam_im, log_step, b_re, b_im, lam_re.reshape(1, N_STATE), lam_im.reshape(1, N_STATE), step_row)


def _seg_rows(t):
    if isinstance(t, int):
        return pl.ds(t * N_SEG, N_SEG)
    return pl.ds(pl.multiple_of(t * N_SEG, N_SEG), N_SEG)


def _scan_forward(xs, a_re, a_im, pw_re, pw_im, cs, seg):
    are = jnp.broadcast_to(a_re, (N_SEG, ST_T))
    aim = jnp.broadcast_to(a_im, (N_SEG, ST_T))

    def steps(k, carry):
        xr, xi = carry
        for j in range(SCAN_UNROLL):
            r = pl.multiple_of((k * SCAN_UNROLL + j) * N_SEG, N_SEG)
            nr = are * xr - aim * xi + xs[pl.ds(r, N_SEG), 0:ST_T]
            ni = are * xi + aim * xr + xs[pl.ds(r, N_SEG), ST_T:2 * ST_T]
            xs[pl.ds(r, N_SEG), 0:ST_T] = nr
            xs[pl.ds(r, N_SEG), ST_T:2 * ST_T] = ni
            xr, xi = nr, ni
        return xr, xi

    zero = jnp.zeros((N_SEG, ST_T), F32)
    fr, fi = lax.fori_loop(0, seg // SCAN_UNROLL, steps, (zero, zero))
    sr, si = pw_re[seg - 1:seg, :], pw_im[seg - 1:seg, :]
    cr = jnp.zeros((1, ST_T), F32)
    ci = jnp.zeros((1, ST_T), F32)
    cs[0:1, :] = cr
    cs[8:9, :] = ci
    for s in range(1, N_SEG):
        ncr = sr * cr - si * ci + fr[s - 1:s, :]
        nci = sr * ci + si * cr + fi[s - 1:s, :]
        cr, ci = ncr, nci
        cs[s:s + 1, :] = cr
        cs[8 + s:9 + s, :] = ci
    car, cai = cs[0:8, :], cs[8:16, :]

    def fix(t, _):
        r = pl.multiple_of(t * N_SEG, N_SEG)
        pr, pi = pw_re[pl.ds(t, 1), :], pw_im[pl.ds(t, 1), :]
        xs[pl.ds(r, N_SEG), 0:ST_T] = xs[pl.ds(r, N_SEG), 0:ST_T] + (pr * car - pi * cai)
        xs[pl.ds(r, N_SEG), ST_T:2 * ST_T] = xs[pl.ds(r, N_SEG), ST_T:2 * ST_T] + (pr * cai + pi * car)
        return 0

    lax.fori_loop(0, seg, fix, 0, unroll=SCAN_UNROLL)


def _ssm_forward(u_perm, bcat, ccat, a_re, a_im, pw_re, pw_im, d_row, seg):
    bl, rows, _ = u_perm.shape

    def body(u_ref, b_ref, c_ref, ar_ref, ai_ref, pr_ref, pi_ref, d_ref, y_ref, xs_ref, cs_ref):
        u = u_ref[0]
        xs, cs = xs_ref.at[0, 0], cs_ref.at[0, 0]
        xs[...] = _mm(u, b_ref[0])
        _scan_forward(xs, ar_ref[...], ai_ref[...], pr_ref, pi_ref, cs, seg)
        y_ref[0] = _mm(xs[...], c_ref[0]) + d_ref[...] * u

    state = lambda r, c: pl.BlockSpec((1, 1, r, c), lambda b, j: (b, j, 0, 0))
    return pl.pallas_call(
        body, name="ssm_forward", grid=(bl, N_GT),
        in_specs=[pl.BlockSpec((1, rows, CH_T), lambda b, j: (b, 0, j)),
                  pl.BlockSpec((1, CH_T, 2 * ST_T), lambda b, j: (j, 0, 0)),
                  pl.BlockSpec((1, 2 * ST_T, CH_T), lambda b, j: (j, 0, 0)),
                  pl.BlockSpec((1, ST_T), lambda b, j: (0, j)), pl.BlockSpec((1, ST_T), lambda b, j: (0, j)),
                  pl.BlockSpec((seg, ST_T), lambda b, j: (0, j)), pl.BlockSpec((seg, ST_T), lambda b, j: (0, j)),
                  pl.BlockSpec((1, CH_T), lambda b, j: (0, j))],
        out_specs=(pl.BlockSpec((1, rows, CH_T), lambda b, j: (b, 0, j)), state(rows, 2 * ST_T), state(16, ST_T)),
        out_shape=(jax.ShapeDtypeStruct((bl, rows, D_SSM), F32),
                   jax.ShapeDtypeStruct((bl, N_GT, rows, 2 * ST_T), F32),
                   jax.ShapeDtypeStruct((bl, N_GT, 16, ST_T), F32)),
        compiler_params=_tc_params(("arbitrary", "arbitrary")),
    )(u_perm, bcat, ccat, a_re, a_im, pw_re, pw_im, d_row)


def _ssm_backward(u_perm, dy_perm, states, carries, bcat_t, ccat_t, a_re, a_im, pw_re, pw_im, d_row, seg):
    bl, rows, _ = u_perm.shape

    def body(u_ref, dy_ref, xs_ref, cs_ref, bt_ref, ct_ref, ar_ref, ai_ref, pr_ref, pi_ref, d_ref,
             du_ref, db_ref, dc_ref, dar_ref, dai_ref, dd_ref, ls, cl):
        b = pl.program_id(1)
        u = u_ref[0]
        dy = dy_ref[0]
        xs, cs = xs_ref.at[0, 0], cs_ref.at[0, 0]
        ls[...] = _mm(dy, ct_ref[0])
        are = jnp.broadcast_to(ar_ref[...], (N_SEG, ST_T))
        aim = jnp.broadcast_to(ai_ref[...], (N_SEG, ST_T))

        def steps(k, carry):
            lr, li = carry
            for j in range(SCAN_UNROLL):
                r = pl.multiple_of((seg - 1 - (k * SCAN_UNROLL + j)) * N_SEG, N_SEG)
                nr = are * lr + aim * li + ls[pl.ds(r, N_SEG), 0:ST_T]
                ni = are * li - aim * lr + ls[pl.ds(r, N_SEG), ST_T:2 * ST_T]
                ls[pl.ds(r, N_SEG), 0:ST_T] = nr
                ls[pl.ds(r, N_SEG), ST_T:2 * ST_T] = ni
                lr, li = nr, ni
            return lr, li

        zero = jnp.zeros((N_SEG, ST_T), F32)
        fr, fi = lax.fori_loop(0, seg // SCAN_UNROLL, steps, (zero, zero))
        sr, si = pr_ref[seg - 1:seg, :], pi_ref[seg - 1:seg, :]
        cr = jnp.zeros((1, ST_T), F32)
        ci = jnp.zeros((1, ST_T), F32)
        cl[7:8, :] = cr
        cl[15:16, :] = ci
        for s in range(N_SEG - 2, -1, -1):
            ncr = sr * cr + si * ci + fr[s + 1:s + 2, :]
            nci = sr * ci - si * cr + fi[s + 1:s + 2, :]
            cr, ci = ncr, nci
            cl[s:s + 1, :] = cr
            cl[8 + s:9 + s, :] = ci
        clr, cli = cl[0:8, :], cl[8:16, :]

        def fix_rows(rows, t, xpr, xpi, acc):
            dr, di = acc
            pr, pi = pr_ref[pl.ds(seg - 1 - t, 1), :], pi_ref[pl.ds(seg - 1 - t, 1), :]
            lr = ls[rows, 0:ST_T] + (pr * clr + pi * cli)
            li = ls[rows, ST_T:2 * ST_T] + (pr * cli - pi * clr)
            ls[rows, 0:ST_T] = lr
            ls[rows, ST_T:2 * ST_T] = li
            return dr + (lr * xpr + li * xpi), di + (li * xpr - lr * xpi)

        def fix_at(t, acc):
            prev = _seg_rows(t - 1)
            return fix_rows(_seg_rows(t), t, xs[prev, 0:ST_T], xs[prev, ST_T:2 * ST_T], acc)

        def fix(k, acc):
            for j in range(SCAN_UNROLL):
                acc = fix_at(k * SCAN_UNROLL + j, acc)
            return acc

        acc = fix_rows(pl.ds(0, N_SEG), 0, cs[0:8, :], cs[8:16, :], (zero, zero))
        for t in range(1, SCAN_UNROLL):
            acc = fix_at(t, acc)
        dr, di = lax.fori_loop(1, seg // SCAN_UNROLL, fix, acc)
        dar = jnp.sum(dr, axis=0, keepdims=True)
        dai = jnp.sum(di, axis=0, keepdims=True)
        lall = ls[...]
        du_ref[0] = (_mm(lall, bt_ref[0]) + d_ref[...] * dy).astype(BF16)
        dbp = _mm_tn(u, lall)
        dcp = _mm_tn(dy, xs[...])
        ddp = jnp.sum(dy * u, axis=0, keepdims=True)

        @pl.when(b == 0)
        def _():
            db_ref[0] = dbp
            dc_ref[0] = dcp
            dar_ref[...] = dar
            dai_ref[...] = dai
            dd_ref[...] = ddp

        @pl.when(b != 0)
        def _():
            db_ref[0] += dbp
            dc_ref[0] += dcp
            dar_ref[...] += dar
            dai_ref[...] += dai
            dd_ref[...] += ddp

    tile3 = lambda r, c: pl.BlockSpec((1, r, c), lambda j, b: (j, 0, 0))
    lane = lambda r, c: pl.BlockSpec((r, c), lambda j, b: (0, j))
    act = pl.BlockSpec((1, rows, CH_T), lambda j, b: (b, 0, j))
    state = lambda r, c: pl.BlockSpec((1, 1, r, c), lambda j, b: (b, j, 0, 0))
    return pl.pallas_call(
        body, name="ssm_backward", grid=(N_GT, bl),
        in_specs=[act, act, state(rows, 2 * ST_T), state(16, ST_T), tile3(2 * ST_T, CH_T), tile3(CH_T, 2 * ST_T),
                  lane(1, ST_T), lane(1, ST_T), lane(seg, ST_T), lane(seg, ST_T), lane(1, CH_T)],
        out_specs=(act, tile3(CH_T, 2 * ST_T), tile3(CH_T, 2 * ST_T), lane(1, ST_T), lane(1, ST_T), lane(1, CH_T)),
        out_shape=(jax.ShapeDtypeStruct((bl, rows, D_SSM), BF16),
                   jax.ShapeDtypeStruct((N_GT, CH_T, 2 * ST_T), F32), jax.ShapeDtypeStruct((N_GT, CH_T, 2 * ST_T), F32),
                   jax.ShapeDtypeStruct((1, N_STATE), F32), jax.ShapeDtypeStruct((1, N_STATE), F32),
                   jax.ShapeDtypeStruct((1, D_SSM), F32)),
        scratch_shapes=[pltpu.VMEM((rows, 2 * ST_T), F32), pltpu.VMEM((16, ST_T), F32)],
        compiler_params=_tc_params(("arbitrary", "arbitrary")),
    )(u_perm, dy_perm, states, carries, bcat_t, ccat_t, a_re, a_im, pw_re, pw_im, d_row)


def _ssm_param_grads(lam_re, lam_im, log_step, b_re, b_im, da_re, da_im, dbb_re, dbb_im):
    def body(lr_ref, li_ref, ls_ref, br_ref, bi_ref, gar_ref, gai_ref, gbr_ref, gbi_ref,
             dlr_ref, dli_ref, dls_ref, dbr_ref, dbi_ref):
        lr, li = lr_ref[...], li_ref[...]
        step, ar, ai, den, cr, ci = _discretise(lr, li, ls_ref[...])
        crb, cib = _per_channel(cr), _per_channel(ci)
        br, bi = br_ref[...], bi_ref[...]
        gbr, gbi = gbr_ref[...], gbi_ref[...]
        dbr_ref[...] = crb * gbr + cib * gbi
        dbi_ref[...] = crb * gbi - cib * gbr
        over_channels = lambda t: jnp.sum(t.reshape(SSM_G, SSM_P, SSM_N), axis=1)
        gcr = over_channels(br * gbr + bi * gbi)
        gci = over_channels(br * gbi - bi * gbr)
        ilr, ili = lr / den, -li / den
        gar = gar_ref[...] + (ilr * gcr + ili * gci)
        gai = gai_ref[...] + (ilr * gci - ili * gcr)
        qr, qi = cr * ilr - ci * ili, cr * ili + ci * ilr
        glr = -(qr * gcr + qi * gci)
        gli = -(qr * gci - qi * gcr)
        gwr = ar * gar + ai * gai
        gwi = ar * gai - ai * gar
        dlr_ref[...] = glr + step * gwr
        dli_ref[...] = gli + step * gwi
        dls_ref[...] = jnp.sum(lr * gwr + li * gwi, axis=-1, keepdims=True) * step

    lam = jax.ShapeDtypeStruct((SSM_G, SSM_N), F32)
    mat = jax.ShapeDtypeStruct((SSM_G * SSM_P, SSM_N), F32)
    vm = pl.BlockSpec(memory_space=pltpu.VMEM)
    return pl.pallas_call(
        body, name="ssm_param_grads", out_shape=(lam, lam, jax.ShapeDtypeStruct((SSM_G, 1), F32), mat, mat),
        in_specs=[vm] * 9, out_specs=(vm,) * 5,
    )(lam_re, lam_im, log_step, b_re, b_im, da_re, da_im, dbb_re, dbb_im)


ROWS4 = Q_PER_KV * ATT_BLOCK


def _att_dist_mask(first_block):
    qi = lax.broadcasted_iota(jnp.int32, (ROWS4, 2 * ATT_BLOCK), 0) & (ATT_BLOCK - 1)
    si = lax.broadcasted_iota(jnp.int32, (ROWS4, 2 * ATT_BLOCK), 1)
    dist = qi + ATT_BLOCK - si
    valid = (dist >= 0) & (dist < ATT_BLOCK) & ((si >= ATT_BLOCK) | jnp.logical_not(first_block))
    return dist.astype(F32), valid


def _stack_heads(x, kv):
    return jnp.concatenate([x[:, (kv * Q_PER_KV + g) * HEAD_DIM:(kv * Q_PER_KV + g + 1) * HEAD_DIM]
                            for g in range(Q_PER_KV)], axis=0)


def _stack_cols(x, kv):
    return jnp.concatenate([x[:, kv * Q_PER_KV + g:kv * Q_PER_KV + g + 1] for g in range(Q_PER_KV)], axis=0)


def _per_head_col(vals):
    return jnp.concatenate([jnp.full((ATT_BLOCK, 1), v, F32) for v in vals], axis=0)


def _attn_forward(q, k, v, sinks, late, bl, nb):
    t = q.shape[0]
    n = len(late)
    steps = bl * nb

    def body(*refs):
        sink_ref, q_ref, kp_ref, kc_ref, vp_ref, vc_ref = refs[:6]
        late_r = refs[6:6 + n]
        o_ref, lse_ref = refs[6 + n:8 + n]
        gath = refs[8 + n:8 + 2 * n]
        cast = refs[8 + 2 * n:8 + 3 * n]
        send_sems, recv_sems, local_sems = refs[8 + 3 * n:]
        i = pl.program_id(1)
        step = pl.program_id(0) * nb + i
        start, relay, finish = _gather_phases(late_r, gath, cast, send_sems, recv_sems, local_sems)
        pl.when(step == 0)(start)
        pl.when(step == (steps * 5) // 8)(relay)
        pl.when(step == steps - 1)(finish)
        dist4, valid4 = _att_dist_mask(i == 0)
        dist, valid = dist4[0:ATT_BLOCK, :], valid4[0:ATT_BLOCK, :]
        kk = jnp.concatenate([kp_ref[...], kc_ref[...]], axis=0)
        vv = jnp.concatenate([vp_ref[...], vc_ref[...]], axis=0)
        qv = q_ref[...]
        for h in range(N_HEADS):
            kv = h // Q_PER_KV
            slope = 2.0 ** (-(h + 1))
            qh = qv[:, h * HEAD_DIM:(h + 1) * HEAD_DIM]
            kh = kk[:, kv * HEAD_DIM:(kv + 1) * HEAD_DIM]
            vh = vv[:, kv * HEAD_DIM:(kv + 1) * HEAD_DIM]
            s = _mm_nt(qh, kh) * ATT_SCALE - slope * dist
            s = jnp.where(valid, s, NEG_BIG)
            sink = sink_ref[h]
            m = jnp.maximum(jnp.max(s, axis=-1, keepdims=True), sink)
            e = jnp.exp(s - m)
            den = jnp.sum(e, axis=-1, keepdims=True) + jnp.exp(sink - m)
            o_ref[:, h * HEAD_DIM:(h + 1) * HEAD_DIM] = _mm(e, vh) * (1.0 / den)
            lse_ref[:, h:h + 1] = m + jnp.log(den)

    cur = lambda w: pl.BlockSpec((ATT_BLOCK, w), lambda b, i: (b * nb + i, 0))
    prev = lambda w: pl.BlockSpec((ATT_BLOCK, w), lambda b, i: (b * nb + jnp.maximum(i - 1, 0), 0))
    g_specs, g_shapes, g_scratch = _gather_operands(late)
    res = pl.pallas_call(
        body, name="attn_forward", grid=(bl, nb),
        in_specs=[pl.BlockSpec(memory_space=pltpu.SMEM), cur(512), prev(128), cur(128), prev(128), cur(128)]
        + [pl.BlockSpec(s.shape, lambda b, i: (0, 0)) for s in late],
        out_specs=(cur(512), cur(N_HEADS)) + g_specs,
        out_shape=(jax.ShapeDtypeStruct((t, D_ATTN), F32), jax.ShapeDtypeStruct((t, N_HEADS), F32)) + g_shapes,
        scratch_shapes=g_scratch,
        compiler_params=_tc_params(("arbitrary", "arbitrary")),
    )(sinks, q, k, k, v, v, *late)
    return res[:2], list(res[2:])


def _attn_backward(q, k, v, o, do, lse, sinks, late16, late_own, bl, nb):
    t = q.shape[0]
    n = len(late16)
    steps = bl * nb
    mid1, mid2, last = steps // 4, (steps * 3) // 4, steps - 1

    def body(*refs):
        (sink_ref, qc_ref, qn_ref, kp_ref, kc_ref, vp_ref, vc_ref, oc_ref, on_ref, doc_ref, don_ref,
         lc_ref, ln_ref) = refs[:13]
        g16_r, go_r = refs[13:13 + n], refs[13 + n:13 + 2 * n]
        dq_ref, dk_ref, dv_ref, ds_ref = refs[13 + 2 * n:17 + 2 * n]
        red = refs[17 + 2 * n:17 + 3 * n]
        own16, recv1, send2, recv2 = (refs[17 + 3 * n + k * n:17 + 3 * n + (k + 1) * n] for k in range(4))
        s_send, s_recv, s_local = refs[17 + 7 * n:]
        b, i = pl.program_id(0), pl.program_id(1)
        step = b * nb + i
        x, y, c = _mesh_pos()
        sibling = (x, y, 1 - c)
        chips = [(1 - x, y), (x, 1 - y), (1 - x, 1 - y)]
        all_chips = [(x, y)] + chips

        def lvl1(a, j):
            return _remote(g16_r[a].at[_slot(*all_chips[j], 1 - c)], recv1[a].at[j], s_send, s_recv, 7 * a + j, sibling)

        def lvl2(a, j):
            return _remote(send2[a].at[j], recv2[a].at[j], s_send, s_recv, 7 * a + 4 + j, (*chips[j], c))

        def mine(a, j):
            return pltpu.make_async_copy(g16_r[a].at[_slot(*chips[j], c)], own16[a].at[j], s_local.at[3 * a + j])

        @pl.when(step == 0)
        def _():
            for a in range(n):
                for j in range(3):
                    mine(a, j).start()
                for j in range(4):
                    lvl1(a, j).start()

        @pl.when(step == mid1)
        def _():
            for a in range(n):
                for j in range(3):
                    mine(a, j).wait()
                for j in range(4):
                    lvl1(a, j).wait_recv()

                def partials(r, a=a):
                    red[a][r, :] = go_r[a][r, :] + recv1[a][0, r, :].astype(F32)
                    for j in range(3):
                        send2[a][j, r, :] = (own16[a][j, r, :].astype(F32)
                                             + recv1[a][1 + j, r, :].astype(F32)).astype(BF16)

                _row_chunks(go_r[a].shape[0], partials)
                for j in range(3):
                    lvl2(a, j).start()

        @pl.when(step == mid2)
        def _():
            for a in range(n):
                for j in range(3):
                    lvl2(a, j).wait_recv()

                def total(r, a=a):
                    g = red[a][r, :]
                    for j in range(3):
                        g = g + recv2[a][j, r, :].astype(F32)
                    red[a][r, :] = g

                _row_chunks(go_r[a].shape[0], total)

        @pl.when(step == last)
        def _():
            for a in range(n):
                for j in range(4):
                    lvl1(a, j).wait_send()
                for j in range(3):
                    lvl2(a, j).wait_send()

        dist, valid = _att_dist_mask(i == 0)
        has_next = i + 1 < nb
        dist_n = dist[:, 0:ATT_BLOCK]
        valid_n = (dist_n < ATT_BLOCK) & has_next
        kk = jnp.concatenate([kp_ref[...], kc_ref[...]], axis=0)
        vv = jnp.concatenate([vp_ref[...], vc_ref[...]], axis=0)
        qc, qn = qc_ref[...], qn_ref[...]
        oc, on = oc_ref[...], on_ref[...]
        doc, don = doc_ref[...], don_ref[...]
        lc, ln = lc_ref[...], ln_ref[...]
        dsink_cols = []
        for kv in range(KV_HEADS):
            heads = range(kv * Q_PER_KV, (kv + 1) * Q_PER_KV)
            kh = kk[:, kv * HEAD_DIM:(kv + 1) * HEAD_DIM]
            vh = vv[:, kv * HEAD_DIM:(kv + 1) * HEAD_DIM]
            khc, vhc = kh[ATT_BLOCK:, :], vh[ATT_BLOCK:, :]
            slope = _per_head_col([2.0 ** (-(h + 1)) for h in heads])
            sink = _per_head_col([sink_ref[h] for h in heads])
            q4, do4 = _stack_heads(qc, kv), _stack_heads(doc, kv)
            delta = jnp.sum(do4 * _stack_heads(oc, kv), axis=-1, keepdims=True)
            lse4 = _stack_cols(lc, kv)
            s = _mm_nt(q4, kh) * ATT_SCALE - slope * dist
            p = jnp.where(valid, jnp.exp(s - lse4), 0.0)
            dsc = p * (_mm_nt(do4, vh) - delta)
            dq4 = _mm(dsc, kh) * ATT_SCALE
            dk_acc = _mm_tn(dsc[:, ATT_BLOCK:], q4)
            dv_acc = _mm_tn(p[:, ATT_BLOCK:], do4)
            dsink4 = jnp.exp(sink - lse4) * delta
            q4n, do4n = _stack_heads(qn, kv), _stack_heads(don, kv)
            delta_n = jnp.sum(do4n * _stack_heads(on, kv), axis=-1, keepdims=True)
            s2 = _mm_nt(q4n, khc) * ATT_SCALE - slope * dist_n
            p2 = jnp.where(valid_n, jnp.exp(s2 - _stack_cols(ln, kv)), 0.0)
            ds2 = p2 * (_mm_nt(do4n, vhc) - delta_n)
            dk_acc += _mm_tn(ds2, q4n)
            dv_acc += _mm_tn(p2, do4n)
            dk_ref[:, kv * HEAD_DIM:(kv + 1) * HEAD_DIM] = dk_acc * ATT_SCALE
            dv_ref[:, kv * HEAD_DIM:(kv + 1) * HEAD_DIM] = dv_acc
            for g, h in enumerate(heads):
                rows = slice(g * ATT_BLOCK, (g + 1) * ATT_BLOCK)
                dq_ref[:, h * HEAD_DIM:(h + 1) * HEAD_DIM] = dq4[rows, :]
                dsink_cols.append(-jnp.sum(dsink4[rows, :], axis=0, keepdims=True))
        dsink = jnp.concatenate(dsink_cols, axis=1)

        @pl.when((b == 0) & (i == 0))
        def _():
            ds_ref[...] = dsink

        @pl.when((b != 0) | (i != 0))
        def _():
            ds_ref[...] += dsink

    cur = lambda w: pl.BlockSpec((ATT_BLOCK, w), lambda b, i: (b * nb + i, 0))
    prev = lambda w: pl.BlockSpec((ATT_BLOCK, w), lambda b, i: (b * nb + jnp.maximum(i - 1, 0), 0))
    nxt = lambda w: pl.BlockSpec((ATT_BLOCK, w), lambda b, i: (b * nb + jnp.minimum(i + 1, nb - 1), 0))
    const2 = lambda s: pl.BlockSpec(s, lambda b, i: (0, 0))
    shard = [s.shape for s in late_own]
    res = pl.pallas_call(
        body, name="attn_backward", grid=(bl, nb),
        in_specs=[pl.BlockSpec(memory_space=pltpu.SMEM), cur(512), nxt(512), prev(128), cur(128), prev(128), cur(128),
                  cur(512), nxt(512), cur(512), nxt(512), cur(N_HEADS), nxt(N_HEADS)]
        + [pl.BlockSpec(memory_space=pl.ANY)] * n + [const2(s) for s in shard],
        out_specs=(cur(512), cur(128), cur(128), const2((1, N_HEADS))) + tuple(const2(s) for s in shard),
        out_shape=(jax.ShapeDtypeStruct((t, D_ATTN), F32), jax.ShapeDtypeStruct((t, 128), F32),
                   jax.ShapeDtypeStruct((t, 128), F32), jax.ShapeDtypeStruct((1, N_HEADS), F32))
        + tuple(jax.ShapeDtypeStruct(s, F32) for s in shard),
        scratch_shapes=[pltpu.VMEM((3,) + s, BF16) for s in shard] + [pltpu.VMEM((4,) + s, BF16) for s in shard]
        + [pltpu.VMEM((3,) + s, BF16) for s in shard] + [pltpu.VMEM((3,) + s, BF16) for s in shard]
        + [pltpu.SemaphoreType.DMA((7 * n,)), pltpu.SemaphoreType.DMA((7 * n,)), pltpu.SemaphoreType.DMA((3 * n,))],
        compiler_params=_tc_params(("arbitrary", "arbitrary")),
    )(sinks, q, q, k, k, v, v, o, o, do, do, lse, lse, *late16, *late_own)
    return res[:4], list(res[4:])


def _mix_forward_backward(x2, y_perm, z_ssm, attn, z_attn, p2, target2, w_glu, b_glu, w_out, g_post, w_gate, b_gate,
                          w_proj, bl, seg):
    t = x2.shape[0]
    tm = seg

    def body(x_ref, y_ref, zs_ref, at_ref, za_ref, p_ref, tg_ref,
             wglu_ref, bglu_ref, wout_ref, gpost_ref, wgate_ref, bgate_ref, wproj_ref,
             loss_ref, dh1_ref, dy_ref, dzs_ref, dat_ref, dza_ref,
             dwglu_ref, dbglu_ref, dwout_ref, dgpost_ref, dwgate_ref, dbgate_ref, dwproj_ref,
             dwout16_ref, dwgate16_ref, dwproj16_ref, dwglu16_ref):
        i = pl.program_id(0)
        gpost = gpost_ref[...]

        @pl.when(i == 0)
        def _():
            for ref in (dwglu_ref, dbglu_ref, dwout_ref, dgpost_ref, dwgate_ref, dbgate_ref, dwproj_ref, loss_ref):
                ref[...] = jnp.zeros(ref.shape, F32)

        def chain(rows):
            y = y_ref[0, rows, :]
            u3 = GELU_C * (y + GELU_K * y * y * y)
            th = jnp.tanh(u3)
            gl = 0.5 * y * (1.0 + th)
            a = _mm(gl, wglu_ref[...]) + bglu_ref[...]
            sa = _sigmoid(a)
            glu = gl * sa
            zs = zs_ref[rows, :]
            sgs = _sigmoid(zs)
            ssm_out = glu * (zs * sgs)
            za = za_ref[rows, :]
            sga = _sigmoid(za)
            at = at_ref[rows, :]
            attn_out = at * (za * sga)
            cat = jnp.concatenate([ssm_out, attn_out], axis=-1).astype(BF16)
            mixed = _mm(cat, wout_ref[...])
            r2 = lax.rsqrt(jnp.mean(mixed * mixed, axis=-1, keepdims=True) + EPS)
            nhat = mixed * r2
            h1 = x_ref[rows, :] + nhat * gpost
            gate = _sigmoid(_mm(h1, wgate_ref[...]) + bgate_ref[...])
            pv = p_ref[rows, :]
            pp = _mm(pv, wproj_ref[...])
            h2 = h1 + gate * pp
            err = h2 - tg_ref[rows, :]
            loss_part = jnp.sum(jnp.sum(err * err, axis=-1, keepdims=True), axis=0, keepdims=True) * (0.5 / D_MODEL)
            dh2 = err * (1.0 / D_MODEL)
            dgp = dh2 * pp * gate * (1.0 - gate)
            dpp = dh2 * gate
            dh1 = dh2 + _mm_nt(dgp, wgate_ref[...])
            dh1_ref[rows, :] = dh1
            dnhat = dh1 * gpost
            dmixed = r2 * (dnhat - nhat * jnp.mean(dnhat * nhat, axis=-1, keepdims=True))
            dcat = _mm_nt(dmixed, wout_ref[...])
            dso, dao = dcat[:, 0:D_SSM], dcat[:, D_SSM:]
            dat_ref[rows, :] = dao * (za * sga)
            dza_ref[rows, :] = (dao * at * (sga * (1.0 + za * (1.0 - sga)))).astype(BF16)
            dzs_ref[rows, :] = (dso * glu * (sgs * (1.0 + zs * (1.0 - sgs)))).astype(BF16)
            dglu = dso * (zs * sgs)
            da = dglu * gl * sa * (1.0 - sa)
            dgl = dglu * sa + _mm_nt(da, wglu_ref[...])
            dgelu = 0.5 * (1.0 + th) + 0.5 * y * (1.0 - th * th) * (GELU_C * (1.0 + 3.0 * GELU_K * y * y))
            dy_ref[0, rows, :] = dgl * dgelu
            return dict(gl=gl.astype(BF16), da=da.astype(BF16), cat=cat, dmixed=dmixed.astype(BF16),
                        h1=h1.astype(BF16), dgp=dgp.astype(BF16), pv=pv.astype(BF16), dpp=dpp.astype(BF16),
                        dbglu=jnp.sum(da, axis=0, keepdims=True), dgpost=jnp.sum(dh1 * nhat, axis=0, keepdims=True),
                        dbgate=jnp.sum(dgp, axis=0, keepdims=True), loss=loss_part)

        groups = [chain(slice(k * (tm // MIX_GROUPS), (k + 1) * (tm // MIX_GROUPS))) for k in range(MIX_GROUPS)]
        rows_of = lambda name: jnp.concatenate([g[name] for g in groups], axis=0)
        total = lambda name: sum(g[name] for g in groups)
        parts = (
            (dwglu_ref, _mm_tn(rows_of("gl"), rows_of("da"))), (dbglu_ref, total("dbglu")),
            (dwout_ref, _mm_tn(rows_of("cat"), rows_of("dmixed"))), (dgpost_ref, total("dgpost")),
            (dwgate_ref, _mm_tn(rows_of("h1"), rows_of("dgp"))), (dbgate_ref, total("dbgate")),
            (dwproj_ref, _mm_tn(rows_of("pv"), rows_of("dpp"))), (loss_ref, total("loss")),
        )

        for ref, val in parts:
            ref[...] += val

        @pl.when(i == t // tm - 1)
        def _():
            for ref16, ref in ((dwout16_ref, dwout_ref), (dwgate16_ref, dwgate_ref), (dwproj16_ref, dwproj_ref),
                               (dwglu16_ref, dwglu_ref)):
                def to16(r, ref16=ref16, ref=ref):
                    ref16[r, :] = ref[r, :].astype(BF16)

                _row_chunks(ref.shape[0], to16)

    row = lambda w: pl.BlockSpec((tm, w), lambda i: (i, 0))
    perm = pl.BlockSpec((1, tm, D_SSM), lambda i: (i // N_SEG, 0, i % N_SEG))
    perm_shape = jax.ShapeDtypeStruct((bl, seg, N_SEG * D_SSM), F32)
    acc = lambda r, c, dt=F32: (_const_spec((r, c)), jax.ShapeDtypeStruct((r, c), dt))
    accs = [acc(D_SSM, D_SSM), acc(1, D_SSM), acc(D_MODEL, D_MODEL), acc(1, D_MODEL), acc(D_MODEL, D_MODEL),
            acc(1, D_MODEL), acc(D_PLE, D_MODEL),
            acc(D_MODEL, D_MODEL, BF16), acc(D_MODEL, D_MODEL, BF16), acc(D_PLE, D_MODEL, BF16), acc(D_SSM, D_SSM, BF16)]
    return pl.pallas_call(
        body, name="mix_forward_backward", grid=(t // tm,),
        in_specs=[row(D_MODEL), perm, row(512), row(512), row(512), row(D_PLE), row(D_MODEL),
                  _const_spec((D_SSM, D_SSM)), _const_spec((1, D_SSM)), _const_spec((D_MODEL, D_MODEL)),
                  _const_spec((1, D_MODEL)), _const_spec((D_MODEL, D_MODEL)), _const_spec((1, D_MODEL)),
                  _const_spec((D_PLE, D_MODEL))],
        out_specs=(_const_spec((1, 1)), row(D_MODEL), perm, row(512), row(512), row(512)) + tuple(a[0] for a in accs),
        out_shape=(jax.ShapeDtypeStruct((1, 1), F32), jax.ShapeDtypeStruct((t, D_MODEL), F32), perm_shape,
                   jax.ShapeDtypeStruct((t, 512), BF16), jax.ShapeDtypeStruct((t, 512), F32),
                   jax.ShapeDtypeStruct((t, 512), BF16)) + tuple(a[1] for a in accs),
        compiler_params=_tc_params(("arbitrary",)),
    )(x2, y_perm, z_ssm, attn, z_attn, p2, target2, w_glu, b_glu, w_out, g_post, w_gate, b_gate, w_proj)


def _in_backward(x2, dh1, du_perm, dz_ssm, dq, dk, dv, dz_attn, g_pre, w_in, bl, seg):
    t = x2.shape[0]
    tm = seg

    def body(x_ref, dh1_ref, du_ref, dzs_ref, dq_ref, dk_ref, dv_ref, dza_ref, g_ref, w_ref,
             gx_ref, dw_ref, dg_ref, dw16_ref):
        i = pl.program_id(0)
        xv = x_ref[...]
        r = lax.rsqrt(jnp.mean(xv * xv, axis=-1, keepdims=True) + EPS)
        xhat = xv * r
        g = g_ref[...]
        hn = (xhat * g).astype(BF16)
        dproj = jnp.concatenate([du_ref[0].astype(BF16), dzs_ref[...].astype(BF16), dq_ref[...].astype(BF16),
                                 dk_ref[...].astype(BF16), dv_ref[...].astype(BF16), dza_ref[...].astype(BF16)],
                                axis=-1)
        dhn = _mm(dproj, w_ref[...])
        dxhat = dhn * g
        gx_ref[...] = dh1_ref[...] + r * (dxhat - xhat * jnp.mean(dxhat * xhat, axis=-1, keepdims=True))
        @pl.when(i == 0)
        def _():
            dw_ref[...] = jnp.zeros((D_IN, D_MODEL), F32)
            dg_ref[...] = jnp.zeros((1, D_MODEL), F32)

        dw_ref[...] += _mm_tn(dproj, hn)
        dg_ref[...] += jnp.sum(dhn * xhat, axis=0, keepdims=True)

        @pl.when(i == t // tm - 1)
        def _():
            def to16(r):
                dw16_ref[r, :] = dw_ref[r, :].astype(BF16)

            _row_chunks(D_IN, to16)

    row = lambda w: pl.BlockSpec((tm, w), lambda i: (i, 0))
    perm = pl.BlockSpec((1, tm, D_SSM), lambda i: (i // N_SEG, 0, i % N_SEG))
    return pl.pallas_call(
        body, name="in_backward", grid=(t // tm,),
        in_specs=[row(D_MODEL), row(D_MODEL), perm, row(512), row(512), row(128), row(128), row(512),
                  _const_spec((1, D_MODEL)), _const_spec((D_IN, D_MODEL))],
        out_specs=(row(D_MODEL), _const_spec((D_IN, D_MODEL)), _const_spec((1, D_MODEL)),
                   _const_spec((D_IN, D_MODEL))),
        out_shape=(jax.ShapeDtypeStruct((t, D_MODEL), F32), jax.ShapeDtypeStruct((D_IN, D_MODEL), F32),
                   jax.ShapeDtypeStruct((1, D_MODEL), F32), jax.ShapeDtypeStruct((D_IN, D_MODEL), BF16)),
        compiler_params=_tc_params(("arbitrary",)),
    )(x2, dh1, du_perm, dz_ssm, dq, dk, dv, dz_attn, g_pre, w_in)


def _block_diag(t):
    a, b = t.shape[1], t.shape[2]
    eye = jnp.eye(G_TILE, dtype=t.dtype)
    t = t.reshape(N_GT, G_TILE, a, 1, b) * eye[None, :, None, :, None]
    return t.reshape(N_GT, G_TILE * a, G_TILE * b)


def _diag_blocks(m, a, b):
    m = m.reshape(N_GT, G_TILE, a, G_TILE, b)
    return jnp.einsum("tgagb->tgab", m).reshape(SSM_G, a, b)


def _local_step(x, p, target, pre_norm_g, w_in, ssm_lam_re, ssm_lam_im, ssm_log_step, ssm_b_re, ssm_b_im, ssm_c_re,
                ssm_c_im, ssm_d, ssm_b_glu, attn_sinks, post_norm_g, pl_b_gate, late, me):
    bl, seq, _ = x.shape
    seg = seq // N_SEG
    nb = seq // ATT_BLOCK
    t = bl * seq
    x2 = x.reshape(t, D_MODEL)
    p2 = p.reshape(t, D_PLE)
    tg2 = target.reshape(t, D_MODEL)

    lam_re, lam_im = ssm_lam_re, ssm_lam_im
    log_step = ssm_log_step.reshape(SSM_G, 1)
    a_re_row, a_im_row, bb_re, bb_im, pw_re, pw_im = _ssm_prep(lam_re, lam_im, log_step, ssm_b_re, ssm_b_im, seg)
    by_group = lambda t: t.reshape(SSM_G, SSM_P, SSM_N)
    bcat = jnp.concatenate([_block_diag(by_group(bb_re)), _block_diag(by_group(bb_im))], axis=-1).astype(BF16)
    ccat_t = jnp.concatenate([_block_diag(by_group(ssm_c_re)), -_block_diag(by_group(ssm_c_im))],
                             axis=-1).astype(BF16)
    bcat_t = jnp.swapaxes(bcat, 1, 2)
    ccat = jnp.swapaxes(ccat_t, 1, 2)
    d_row = ssm_d.reshape(1, D_SSM)

    (u_perm, z_ssm, q, k, v, z_attn), (g_out, g_glu) = _in_proj(
        x2, pre_norm_g.reshape(1, D_MODEL), w_in, [late[0], late[3]], bl, seg)
    u_perm = u_perm.reshape(bl, seq, D_SSM)
    y_perm, states, carries = _ssm_forward(u_perm, bcat, ccat, a_re_row, a_im_row, pw_re, pw_im, d_row, seg)
    sinks = attn_sinks.reshape(N_HEADS)
    (attn, lse), (g_gate, g_proj) = _attn_forward(q, k, v, sinks, [late[1], late[2]], bl, nb)
    w_out, w_gate, w_proj, w_glu = (_gathered_to_full(n, g) for n, g in zip(LATE_NAMES, (g_out, g_gate, g_proj, g_glu)))
    (loss, dh1, dy_perm, dz_ssm, dattn, dz_attn, d_w_glu, d_b_glu, d_w_out, d_g_post, d_w_gate, d_b_gate,
     d_w_proj, *late16) = _mix_forward_backward(
        x2, y_perm.reshape(bl, seg, N_SEG * D_SSM), z_ssm, attn, z_attn, p2, tg2, w_glu,
        ssm_b_glu.reshape(1, D_SSM), w_out, post_norm_g.reshape(1, D_MODEL), w_gate, pl_b_gate.reshape(1, D_MODEL),
        w_proj, bl, seg)
    owned = [_full_to_owned(n, d) for n, d in zip(LATE_NAMES, (d_w_out, d_w_gate, d_w_proj, d_w_glu))]
    (dq, dk, dv, d_sinks), late_grads = _attn_backward(
        q, k, v, attn, dattn, lse, sinks, [_full_to_owned(n, d) for n, d in zip(LATE_NAMES, late16)],
        [lax.dynamic_index_in_dim(o, me, axis=0, keepdims=False) for o in owned], bl, nb)
    du_perm, d_bcat, d_ccat_t, da_re, da_im, d_d = _ssm_backward(
        u_perm, dy_perm.reshape(bl, seq, D_SSM), states, carries, bcat_t, ccat_t, a_re_row, a_im_row, pw_re, pw_im,
        d_row, seg)
    grad_x, d_w_in, d_g_pre, d_w_in16 = _in_backward(
        x2, dh1, du_perm.reshape(bl, seg, N_SEG * D_SSM), dz_ssm, dq, dk, dv, dz_attn,
        pre_norm_g.reshape(1, D_MODEL), w_in, bl, seg)
    flat = lambda t: t.reshape(SSM_G * SSM_P, SSM_N)
    d_lam_re, d_lam_im, d_ls, d_b_re, d_b_im = _ssm_param_grads(
        lam_re, lam_im, log_step, ssm_b_re, ssm_b_im, da_re.reshape(SSM_G, SSM_N), da_im.reshape(SSM_G, SSM_N),
        flat(_diag_blocks(d_bcat[:, :, 0:ST_T], SSM_P, SSM_N)), flat(_diag_blocks(d_bcat[:, :, ST_T:], SSM_P, SSM_N)))
    grads = {
        "pre_norm_g": d_g_pre, "w_in": d_w_in, "w_in16": d_w_in16, "ssm_lam_re": d_lam_re, "ssm_lam_im": d_lam_im,
        "ssm_log_step": d_ls, "ssm_b_re": d_b_re, "ssm_b_im": d_b_im,
        "ssm_c_re": _diag_blocks(d_ccat_t[:, :, 0:ST_T], SSM_P, SSM_N),
        "ssm_c_im": -_diag_blocks(d_ccat_t[:, :, ST_T:], SSM_P, SSM_N),
        "ssm_d": d_d, "ssm_b_glu": d_b_glu, "attn_sinks": d_sinks, "post_norm_g": d_g_post, "pl_b_gate": d_b_gate,
    }
    return loss, grad_x.reshape(bl, seq, D_MODEL), grads, late_grads


LATE_NAMES = ("w_out", "pl_w_gate", "pl_w_proj", "ssm_w_glu")
BIG_NAMES = ("w_in",) + LATE_NAMES
COL_SHARDED = {"w_in": D_IN // N_DEV, "pl_w_proj": D_MODEL // N_DEV}
WEIGHT_NAMES = ("pre_norm_g", "w_in", "ssm_lam_re", "ssm_lam_im", "ssm_log_step", "ssm_b_re", "ssm_b_im", "ssm_c_re",
                "ssm_c_im", "ssm_d", "ssm_w_glu", "ssm_b_glu", "attn_sinks", "w_out", "post_norm_g", "pl_w_proj",
                "pl_w_gate", "pl_b_gate")


TRANSPOSED = {"w_in": (0, 1), "ssm_b_re": (1, 2), "ssm_b_im": (1, 2)}


def _kernel_form(name, a):
    a = a[0]
    if name in TRANSPOSED:
        a = jnp.swapaxes(a, *TRANSPOSED[name])
    if name in ("ssm_b_re", "ssm_b_im", "ssm_c_re", "ssm_c_im"):
        a = a.reshape(SSM_G * SSM_P, SSM_N)
    return a


def _given_form(name, a, shape):
    if name in TRANSPOSED:
        i, j = TRANSPOSED[name]
        swapped = list(shape[1:])
        swapped[i], swapped[j] = swapped[j], swapped[i]
        return jnp.swapaxes(a.reshape(swapped), i, j).reshape(shape)
    return a.reshape(shape)


def _gathered_to_full(name, g):
    _, rows, cols = g.shape
    if name in COL_SHARDED:
        return jnp.swapaxes(g, 0, 1).reshape(rows, N_DEV * cols)
    return g.reshape(N_DEV * rows, cols)


def _full_to_owned(name, full):
    if name in COL_SHARDED:
        return jnp.swapaxes(full.reshape(full.shape[0], N_DEV, COL_SHARDED[name]), 0, 1)
    return full.reshape(N_DEV, full.shape[0] // N_DEV, full.shape[1])


def kernel(x, p, pre_norm_g, w_in, ssm_lam_re, ssm_lam_im, ssm_log_step, ssm_b_re, ssm_b_im, ssm_c_re, ssm_c_im, ssm_d, ssm_w_glu, ssm_b_glu, attn_sinks, w_out, post_norm_g, pl_w_proj, pl_w_gate, pl_b_gate, loss_target, m_pre_norm_g, m_w_in, m_ssm_lam_re, m_ssm_lam_im, m_ssm_log_step, m_ssm_b_re, m_ssm_b_im, m_ssm_c_re, m_ssm_c_im, m_ssm_d, m_ssm_w_glu, m_ssm_b_glu, m_attn_sinks, m_w_out, m_post_norm_g, m_pl_w_proj, m_pl_w_gate, m_pl_b_gate, v_pre_norm_g, v_w_in, v_ssm_lam_re, v_ssm_lam_im, v_ssm_log_step, v_ssm_b_re, v_ssm_b_im, v_ssm_c_re, v_ssm_c_im, v_ssm_d, v_ssm_w_glu, v_ssm_b_glu, v_attn_sinks, v_w_out, v_post_norm_g, v_pl_w_proj, v_pl_w_gate, v_pl_b_gate):
    w = dict(pre_norm_g=pre_norm_g, w_in=w_in, ssm_lam_re=ssm_lam_re, ssm_lam_im=ssm_lam_im, ssm_log_step=ssm_log_step,
             ssm_b_re=ssm_b_re, ssm_b_im=ssm_b_im, ssm_c_re=ssm_c_re, ssm_c_im=ssm_c_im, ssm_d=ssm_d, ssm_w_glu=ssm_w_glu,
             ssm_b_glu=ssm_b_glu, attn_sinks=attn_sinks, w_out=w_out, post_norm_g=post_norm_g, pl_w_proj=pl_w_proj,
             pl_w_gate=pl_w_gate, pl_b_gate=pl_b_gate)
    m = dict(pre_norm_g=m_pre_norm_g, w_in=m_w_in, ssm_lam_re=m_ssm_lam_re, ssm_lam_im=m_ssm_lam_im,
             ssm_log_step=m_ssm_log_step, ssm_b_re=m_ssm_b_re, ssm_b_im=m_ssm_b_im, ssm_c_re=m_ssm_c_re,
             ssm_c_im=m_ssm_c_im, ssm_d=m_ssm_d, ssm_w_glu=m_ssm_w_glu, ssm_b_glu=m_ssm_b_glu, attn_sinks=m_attn_sinks,
             w_out=m_w_out, post_norm_g=m_post_norm_g, pl_w_proj=m_pl_w_proj, pl_w_gate=m_pl_w_gate,
             pl_b_gate=m_pl_b_gate)
    v = dict(pre_norm_g=v_pre_norm_g, w_in=v_w_in, ssm_lam_re=v_ssm_lam_re, ssm_lam_im=v_ssm_lam_im,
             ssm_log_step=v_ssm_log_step, ssm_b_re=v_ssm_b_re, ssm_b_im=v_ssm_b_im, ssm_c_re=v_ssm_c_re,
             ssm_c_im=v_ssm_c_im, ssm_d=v_ssm_d, ssm_w_glu=v_ssm_w_glu, ssm_b_glu=v_ssm_b_glu, attn_sinks=v_attn_sinks,
             w_out=v_w_out, post_norm_g=v_post_norm_g, pl_w_proj=v_pl_w_proj, pl_w_gate=v_pl_w_gate,
             pl_b_gate=v_pl_b_gate)
    me = _slot(lax.axis_index("x"), lax.axis_index("y"), lax.axis_index("c"))
    kf = lambda d: {n: _kernel_form(n, a) for n, a in d.items()}
    wk, mk, vk = kf(w), kf(m), kf(v)

    (gathered,) = _allgather_weights([wk["w_in"]])
    loss, grad_x, grads, g_late = _local_step(
        x, p[0], loss_target, wk["pre_norm_g"], gathered.reshape(D_IN, D_MODEL), wk["ssm_lam_re"], wk["ssm_lam_im"],
        wk["ssm_log_step"], wk["ssm_b_re"], wk["ssm_b_im"], wk["ssm_c_re"], wk["ssm_c_im"], wk["ssm_d"],
        wk["ssm_b_glu"], wk["attn_sinks"], wk["post_norm_g"], wk["pl_b_gate"], [wk[n] for n in LATE_NAMES], me)

    owned = lambda g: g.reshape(N_DEV, D_IN // N_DEV, D_MODEL)
    tiny_form = lambda d: [d[n].reshape(rows, cols) for n, rows, cols in TINY]
    med_form = lambda d: [d[n].reshape(N_DEV, rows // N_DEV, cols) for n, rows, cols in MEDIUM]
    g_big, loss, g_tiny, g_med = _reduce_final(
        [owned(grads["w_in16"])], [lax.dynamic_index_in_dim(owned(grads["w_in"]), me, axis=0, keepdims=False)],
        loss, tiny_form(grads), med_form(grads))
    names = BIG_NAMES + tuple(n for n, _, _ in TINY + MEDIUM)
    form = lambda d: [d[n] for n in BIG_NAMES] + tiny_form(d) + med_form(d)
    updated = _adamw_update(g_big + g_late + g_tiny + g_med, form(wk), form(mk), form(vk))
    vals = dict(zip(names, updated))
    results = [[_given_form(n, vals[n][kind], w[n].shape) for n in WEIGHT_NAMES] for kind in range(4)]
    return (loss.reshape(()), grad_x, *results[0], *results[1], *results[2], *results[3])
```

```python
import functools
import math

import jax
import jax.numpy as jnp
from jax import lax
from jax.experimental import pallas as pl
from jax.experimental.pallas import tpu as pltpu

F32 = jnp.float32
BF16 = jnp.bfloat16

D_MODEL = 1024
D_SSM = 512
D_ATTN = 512
SSM_P = 16
SSM_G = 32
SSM_N = 64
N_HEADS = 8
KV_HEADS = 2
Q_PER_KV = 4
HEAD_DIM = 64
ATT_BLOCK = 128
D_PLE = 256
D_IN = 2304
EPS = 1e-6
N_DEV = 8
N_SEG = 8
G_TILE = 8
N_GT = SSM_G // G_TILE
CH_T = G_TILE * SSM_P
ST_T = G_TILE * SSM_N
N_STATE = SSM_G * SSM_N
SCAN_UNROLL = 4
MIX_GROUPS = 1
LANES = 128
VMEM_LIMIT = 60 * 1024 * 1024

ADAM_LR = 0.001
ADAM_B1 = 0.9
ADAM_B2 = 0.999
ADAM_EPS = 1e-08
ADAM_WD = 0.01
ADAM_STEP = 10

GELU_C = math.sqrt(2.0 / math.pi)
GELU_K = 0.044715
ATT_SCALE = 1.0 / math.sqrt(HEAD_DIM)
NEG_BIG = -1e30


def _mm(a, b):
    return jnp.dot(a.astype(BF16), b.astype(BF16), preferred_element_type=F32)


def _mm_nt(a, b):
    return lax.dot_general(a.astype(BF16), b.astype(BF16), (((1,), (1,)), ((), ())), preferred_element_type=F32)


def _mm_tn(a, b):
    return lax.dot_general(a.astype(BF16), b.astype(BF16), (((0,), (0,)), ((), ())), preferred_element_type=F32)


def _sigmoid(x):
    return 1.0 / (1.0 + jnp.exp(-x))


def _tc_params(sem):
    return pltpu.CompilerParams(dimension_semantics=sem, vmem_limit_bytes=VMEM_LIMIT)


def _const_spec(shape):
    nd = len(shape)
    return pl.BlockSpec(shape, lambda *_: (0,) * nd)


def _mesh_pos():
    return lax.axis_index("x"), lax.axis_index("y"), lax.axis_index("c")


ROW_CHUNKS = (64, 32, 16)


def _row_chunk(nrows):
    return next((c for c in ROW_CHUNKS if nrows % c == 0), None)


def _row_chunks(nrows, fn, chunk=None, init=None):
    chunk = chunk or _row_chunk(nrows)

    def step(i, carry):
        rows = pl.ds(pl.multiple_of(i * chunk, chunk), chunk)
        if init is None:
            fn(rows)
            return carry
        return fn(rows, carry)

    return lax.fori_loop(0, nrows // chunk, step, 0 if init is None else init)


def _slot(px, py, pc):
    return 4 * px + 2 * py + pc


def _allgather_weights(shards):
    n = len(shards)

    def body(*refs):
        srcs, outs, (send_sems, recv_sems) = refs[:n], refs[n:2 * n], refs[2 * n:]
        x, y, c = _mesh_pos()
        me, sibling = (x, y, c), (x, y, 1 - c)
        chips = [(1 - x, y), (x, 1 - y), (1 - x, 1 - y)]

        def copy(a, k, block, to):
            blk = outs[a].at[_slot(*block)]
            return pltpu.make_async_remote_copy(
                src_ref=blk, dst_ref=blk, send_sem=send_sems.at[7 * a + k], recv_sem=recv_sems.at[7 * a + k],
                device_id=to, device_id_type=pl.DeviceIdType.MESH)

        sends = []
        for a in range(n):
            mine = outs[a].at[_slot(*me)]

            def cast(r, mine=mine, src=srcs[a]):
                mine[r, :] = src[r, :].astype(BF16)

            _row_chunks(srcs[a].shape[0], cast)
            first = [copy(a, 0, me, sibling)] + [copy(a, 1 + j, me, (*chip, c)) for j, chip in enumerate(chips)]
            for cp in first:
                cp.start()
            sends += first
        for a in range(n):
            for j, chip in enumerate(chips):
                copy(a, 1 + j, (*chip, c), me).wait_recv()
                fwd = copy(a, 4 + j, (*chip, c), sibling)
                fwd.start()
                sends.append(fwd)
        for a in range(n):
            copy(a, 0, sibling, me).wait_recv()
            for j, chip in enumerate(chips):
                copy(a, 4 + j, (*chip, 1 - c), me).wait_recv()
        for cp in sends:
            cp.wait_send()

    vm = pl.BlockSpec(memory_space=pltpu.VMEM)
    return pl.pallas_call(
        body, name="allgather_weights",
        out_shape=tuple(jax.ShapeDtypeStruct((N_DEV,) + s.shape, BF16) for s in shards),
        in_specs=[vm] * n, out_specs=(vm,) * n,
        scratch_shapes=[pltpu.SemaphoreType.DMA((7 * n,)), pltpu.SemaphoreType.DMA((7 * n,))],
        compiler_params=pltpu.CompilerParams(vmem_limit_bytes=VMEM_LIMIT),
    )(*shards)


def _adamw(w, g, m, v):
    m = ADAM_B1 * m + (1.0 - ADAM_B1) * g
    v = ADAM_B2 * v + (1.0 - ADAM_B2) * (g * g)
    m_hat = m / (1.0 - ADAM_B1 ** ADAM_STEP)
    v_hat = v / (1.0 - ADAM_B2 ** ADAM_STEP)
    delta = -ADAM_LR * (m_hat / (jnp.sqrt(v_hat) + ADAM_EPS) + ADAM_WD * w)
    return delta, m, v


def _remote(src, dst, send_sems, recv_sems, k, to):
    return pltpu.make_async_remote_copy(src_ref=src, dst_ref=dst, send_sem=send_sems.at[k], recv_sem=recv_sems.at[k],
                                        device_id=to, device_id_type=pl.DeviceIdType.MESH)


def _big_reduce_phases(g16_r, go_r, outs, send2, recv1, recv2, s_send, s_recv):
    n = len(g16_r)
    x, y, c = _mesh_pos()
    sibling = (x, y, 1 - c)
    chips = [(1 - x, y), (x, 1 - y), (1 - x, 1 - y)]
    all_chips = [(x, y)] + chips
    lvl1 = []
    for a in range(n):
        cps = [_remote(g16_r[a].at[_slot(*chip, 1 - c)], recv1[a].at[j], s_send, s_recv, 7 * a + j, sibling)
               for j, chip in enumerate(all_chips)]
        for cp in cps:
            cp.start()
        lvl1.append(cps)
    yield
    lvl2 = []
    for a in range(n):
        for cp in lvl1[a]:
            cp.wait_recv()
        og = outs[a]

        def partials(r, a=a, og=og):
            og[r, :] = go_r[a][r, :] + recv1[a][0, r, :].astype(F32)
            for j, chip in enumerate(chips):
                mine16 = g16_r[a][_slot(*chip, c), r, :].astype(F32)
                send2[a][j, r, :] = (mine16 + recv1[a][1 + j, r, :].astype(F32)).astype(BF16)

        _row_chunks(go_r[a].shape[0], partials)
        cps = [_remote(send2[a].at[j], recv2[a].at[j], s_send, s_recv, 7 * a + 4 + j, (*chip, c))
               for j, chip in enumerate(chips)]
        for cp in cps:
            cp.start()
        lvl2.append(cps)
    yield
    for a in range(n):
        for cp in lvl2[a]:
            cp.wait_recv()
        og = outs[a]

        def total(r, a=a, og=og):
            g = og[r, :]
            for j in range(3):
                g = g + recv2[a][j, r, :].astype(F32)
            og[r, :] = g

        _row_chunks(go_r[a].shape[0], total)
    yield
    for cps in lvl1 + lvl2:
        for cp in cps:
            cp.wait_send()


def _adamw_update(g, w, m, v):
    n = len(g)

    def body(*refs):
        g_r, w_r, m_r, v_r = (refs[i * n:(i + 1) * n] for i in range(4))
        outs = refs[4 * n:]
        for a in range(n):
            og, od, om, ov = outs[4 * a:4 * a + 4]

            def update(idx, a=a, og=og, od=od, om=om, ov=ov):
                gv = g_r[a][idx]
                d, nm, nv = _adamw(w_r[a][idx], gv, m_r[a][idx], v_r[a][idx])
                og[idx] = gv
                od[idx] = d
                om[idx] = nm
                ov[idx] = nv

            shape = g_r[a].shape
            if len(shape) == 3:
                for b in range(shape[0]):
                    update(b)
            elif _row_chunk(shape[0]) is not None:
                _row_chunks(shape[0], update)
            else:
                update(Ellipsis)

    vm = pl.BlockSpec(memory_space=pltpu.VMEM)
    res = pl.pallas_call(
        body, name="adamw_update",
        out_shape=tuple(jax.ShapeDtypeStruct(t.shape, F32) for t in g for _ in range(4)),
        in_specs=[vm] * (4 * n), out_specs=(vm,) * (4 * n),
        compiler_params=pltpu.CompilerParams(vmem_limit_bytes=VMEM_LIMIT),
    )(*g, *w, *m, *v)
    return [res[4 * a:4 * a + 4] for a in range(n)]


TINY = (("pre_norm_g", 1, 1024), ("post_norm_g", 1, 1024), ("pl_b_gate", 1, 1024), ("ssm_d", 1, 512),
        ("ssm_b_glu", 1, 512), ("ssm_log_step", 1, 32), ("attn_sinks", 1, 8), ("ssm_lam_re", 32, 64),
        ("ssm_lam_im", 32, 64))
MEDIUM = (("ssm_b_re", SSM_G * SSM_P, SSM_N), ("ssm_b_im", SSM_G * SSM_P, SSM_N), ("ssm_c_re", SSM_G * SSM_P, SSM_N),
          ("ssm_c_im", SSM_G * SSM_P, SSM_N))


def _stage_rows():
    offs, r = {}, 0
    for name, rows, cols in TINY + (("loss", 1, 1),):
        if rows > 1:
            r = -(-r // 8) * 8
        offs[name] = r
        r += rows if rows > 1 else max(cols // LANES, 1)
    return offs, -(-r // 8) * 8


def _reduce_final(g16, gown, loss, g_tiny, g_med):
    nb_, nt, nm_ = len(g16), len(TINY), len(MEDIUM)
    offs, stage_rows = _stage_rows()

    def body(*refs):
        g16_r, go_r = refs[:nb_], refs[nb_:2 * nb_]
        base = 2 * nb_
        loss_r, gt, gm = refs[base], refs[base + 1:base + 1 + nt], refs[base + 1 + nt:base + 1 + nt + nm_]
        base += 1 + nt + nm_
        out_b = refs[base:base + nb_]
        base += nb_
        loss_o, out_t, out_m = refs[base], refs[base + 1:base + 1 + nt], refs[base + 1 + nt:base + 1 + nt + nm_]
        base += 1 + nt + nm_
        send2_b, recv1_b, recv2_b = (refs[base + i * nb_:base + (i + 1) * nb_] for i in range(3))
        base += 3 * nb_
        stage = refs[base]
        recv1, part, recv2 = (refs[base + 1 + i * nm_:base + 1 + (i + 1) * nm_] for i in range(3))
        bs_send, bs_recv, s_send, s_recv = refs[base + 1 + 3 * nm_:]
        big = _big_reduce_phases(g16_r, go_r, out_b, send2_b, recv1_b, recv2_b, bs_send, bs_recv)
        small = small_phases(loss_r, gt, gm, loss_o, out_t, out_m, stage, recv1, part, recv2, s_send, s_recv)
        next(big)
        next(small)
        next(big)
        for _ in small:
            pass
        for _ in big:
            pass

    def small_phases(loss_r, gt, gm, loss_o, out_t, out_m, stage, recv1, part, recv2, s_send, s_recv):
        x, y, c = _mesh_pos()
        me = _slot(x, y, c)
        sibling = (x, y, 1 - c)
        chips = [(1 - x, y), (x, 1 - y), (1 - x, 1 - y)]
        all_chips = [(x, y)] + chips
        peers = [sibling] + [(*chip, c) for chip in chips] + [(*chip, 1 - c) for chip in chips]
        sem = iter(range(7 + 14 * nm_))
        lvl1 = []
        for a in range(nm_):
            cps = [_remote(gm[a].at[_slot(*chip, 1 - c)], recv1[a].at[j], s_send, s_recv, next(sem), sibling)
                   for j, chip in enumerate(all_chips)]
            for cp in cps:
                cp.start()
            lvl1.append(cps)
        mine = stage.at[me]
        mine[...] = jnp.zeros((stage_rows, LANES), F32)
        for (name, rows, cols), ref in zip(TINY + (("loss", 1, 1),), gt + (loss_r,)):
            r0 = offs[name]
            if rows > 1:
                mine[r0:r0 + rows, 0:cols] = ref[...]
            elif cols >= LANES:
                for i in range(cols // LANES):
                    mine[r0 + i:r0 + i + 1, :] = ref[:, i * LANES:(i + 1) * LANES]
            else:
                mine[r0:r0 + 1, 0:cols] = ref[...]
        tiny_cps = [_remote(mine, mine, s_send, s_recv, next(sem), peer) for peer in peers]
        for cp in tiny_cps:
            cp.start()
        yield
        lvl2 = []
        for a in range(nm_):
            for cp in lvl1[a]:
                cp.wait_recv()
            for j, chip in enumerate(all_chips):
                part[a][j] = gm[a][_slot(*chip, c)] + recv1[a][j]
            cps = [_remote(part[a].at[1 + j], recv2[a].at[j], s_send, s_recv, next(sem), (*chip, c))
                   for j, chip in enumerate(chips)]
            for cp in cps:
                cp.start()
            lvl2.append(cps)
        yield
        lvl3 = []
        for a in range(nm_):
            for cp in lvl2[a]:
                cp.wait_recv()
            blk = out_m[a].at[me]
            blk[...] = ((part[a][0] + recv2[a][0]) + recv2[a][1]) + recv2[a][2]
            cps = [_remote(blk, blk, s_send, s_recv, next(sem), peer) for peer in peers]
            for cp in cps:
                cp.start()
            lvl3.append(cps)
        yield
        for cp in tiny_cps:
            cp.wait_recv()
        tot = stage[0]
        for d in range(1, N_DEV):
            tot = tot + stage[d]
        loss_o[...] = tot[offs["loss"]:offs["loss"] + 1, 0:1]
        for k, (name, rows, cols) in enumerate(TINY):
            r0 = offs[name]
            if rows > 1:
                out_t[k][...] = tot[r0:r0 + rows, 0:cols]
            elif cols >= LANES:
                for i in range(cols // LANES):
                    out_t[k][:, i * LANES:(i + 1) * LANES] = tot[r0 + i:r0 + i + 1, :]
            else:
                out_t[k][...] = tot[r0:r0 + 1, 0:cols]
        for cps in lvl3:
            for cp in cps:
                cp.wait_recv()
        for cps in lvl1 + lvl2 + lvl3 + [tiny_cps]:
            for cp in cps:
                cp.wait_send()

    vmem = pl.BlockSpec(memory_space=pltpu.VMEM)
    t_shapes = [jax.ShapeDtypeStruct((rows, cols), F32) for _, rows, cols in TINY]
    m_shapes = [jax.ShapeDtypeStruct((N_DEV, rows // N_DEV, cols), F32) for _, rows, cols in MEDIUM]
    blk = [(rows // N_DEV, cols) for _, rows, cols in MEDIUM]
    scratch = ([pltpu.VMEM((3,) + t.shape, BF16) for t in gown] + [pltpu.VMEM((4,) + t.shape, BF16) for t in gown]
               + [pltpu.VMEM((3,) + t.shape, BF16) for t in gown]
               + [pltpu.VMEM((N_DEV, stage_rows, LANES), F32)]
               + [pltpu.VMEM((4,) + b, F32) for b in blk] + [pltpu.VMEM((4,) + b, F32) for b in blk]
               + [pltpu.VMEM((3,) + b, F32) for b in blk]
               + [pltpu.SemaphoreType.DMA((7 * nb_,)), pltpu.SemaphoreType.DMA((7 * nb_,)),
                  pltpu.SemaphoreType.DMA((7 + 14 * nm_,)), pltpu.SemaphoreType.DMA((7 + 14 * nm_,))])
    n_in, n_out = 2 * nb_ + 1 + nt + nm_, nb_ + 1 + nt + nm_
    res = pl.pallas_call(
        body, name="reduce_final",
        out_shape=tuple(jax.ShapeDtypeStruct(t.shape, F32) for t in gown) + (jax.ShapeDtypeStruct((1, 1), F32),)
        + tuple(t_shapes) + tuple(m_shapes),
        in_specs=[vmem] * n_in, out_specs=(vmem,) * n_out, scratch_shapes=scratch,
        compiler_params=pltpu.CompilerParams(vmem_limit_bytes=VMEM_LIMIT),
    )(*g16, *gown, loss, *g_tiny, *g_med)
    return list(res[:nb_]), res[nb_], list(res[nb_ + 1:nb_ + 1 + nt]), list(res[nb_ + 1 + nt:])


def _gather_phases(shard_r, gath, cast, send_sems, recv_sems, local_sems):
    n = len(shard_r)
    x, y, c = _mesh_pos()
    me, sibling = (x, y, c), (x, y, 1 - c)
    chips = [(1 - x, y), (x, 1 - y), (1 - x, 1 - y)]

    def own(a, k, to):
        return _remote(cast[a], gath[a].at[_slot(*me)], send_sems, recv_sems, 7 * a + k, to)

    def passed(a, k, block, to):
        blk = gath[a].at[_slot(*block)]
        return _remote(blk, blk, send_sems, recv_sems, 7 * a + k, to)

    def keep(a):
        return pltpu.make_async_copy(cast[a], gath[a].at[_slot(*me)], local_sems.at[a])

    def start():
        for a in range(n):
            def to16(r, a=a):
                cast[a][r, :] = shard_r[a][r, :].astype(BF16)

            _row_chunks(shard_r[a].shape[0], to16)
            keep(a).start()
            own(a, 0, sibling).start()
            for j, chip in enumerate(chips):
                own(a, 1 + j, (*chip, c)).start()

    def relay():
        for a in range(n):
            for j, chip in enumerate(chips):
                passed(a, 1 + j, (*chip, c), me).wait_recv()
                passed(a, 4 + j, (*chip, c), sibling).start()

    def finish():
        for a in range(n):
            passed(a, 0, sibling, me).wait_recv()
            for j, chip in enumerate(chips):
                passed(a, 4 + j, (*chip, 1 - c), me).wait_recv()
            own(a, 0, sibling).wait_send()
            for j, chip in enumerate(chips):
                own(a, 1 + j, (*chip, c)).wait_send()
                passed(a, 4 + j, (*chip, c), sibling).wait_send()
            keep(a).wait()

    return start, relay, finish


def _gather_operands(shards):
    n = len(shards)
    return ((pl.BlockSpec(memory_space=pl.ANY),) * n,
            tuple(jax.ShapeDtypeStruct((N_DEV,) + s.shape, BF16) for s in shards),
            [pltpu.VMEM(s.shape, BF16) for s in shards]
            + [pltpu.SemaphoreType.DMA((7 * n,)), pltpu.SemaphoreType.DMA((7 * n,)), pltpu.SemaphoreType.DMA((n,))])


def _in_proj(x2, g_pre, w_in, late, bl, seg):
    t = x2.shape[0]
    tm = seg
    steps = t // tm
    n = len(late)
    forward_step, last_step = (steps * 5) // 8, steps - 1

    def body(*refs):
        x_ref, g_ref, w_ref = refs[:3]
        late_r = refs[3:3 + n]
        u_ref, zs_ref, q_ref, k_ref, v_ref, za_ref = refs[3 + n:9 + n]
        gath = refs[9 + n:9 + 2 * n]
        cast = refs[9 + 2 * n:9 + 3 * n]
        send_sems, recv_sems, local_sems = refs[9 + 3 * n:]
        i = pl.program_id(0)
        start, relay, finish = _gather_phases(late_r, gath, cast, send_sems, recv_sems, local_sems)
        pl.when(i == 0)(start)
        xv = x_ref[...]
        r = lax.rsqrt(jnp.mean(xv * xv, axis=-1, keepdims=True) + EPS)
        hn = xv * r * g_ref[...]
        proj = _mm_nt(hn, w_ref[...])
        u_ref[0] = proj[:, 0:512]
        zs_ref[...] = proj[:, 512:1024]
        q_ref[...] = proj[:, 1024:1536].astype(BF16)
        k_ref[...] = proj[:, 1536:1664].astype(BF16)
        v_ref[...] = proj[:, 1664:1792].astype(BF16)
        za_ref[...] = proj[:, 1792:2304]
        pl.when(i == forward_step)(relay)
        pl.when(i == last_step)(finish)

    row = lambda w: pl.BlockSpec((tm, w), lambda i: (i, 0))
    g_specs, g_shapes, g_scratch = _gather_operands(late)
    res = pl.pallas_call(
        body, name="in_proj", grid=(steps,),
        in_specs=[row(D_MODEL), _const_spec((1, D_MODEL)), _const_spec((D_IN, D_MODEL))]
        + [_const_spec(s.shape) for s in late],
        out_specs=(pl.BlockSpec((1, tm, D_SSM), lambda i: (i // N_SEG, 0, i % N_SEG)),
                   row(512), row(512), row(128), row(128), row(512)) + g_specs,
        out_shape=(jax.ShapeDtypeStruct((bl, seg, N_SEG * D_SSM), F32),
                   jax.ShapeDtypeStruct((t, 512), F32), jax.ShapeDtypeStruct((t, 512), BF16),
                   jax.ShapeDtypeStruct((t, 128), BF16), jax.ShapeDtypeStruct((t, 128), BF16),
                   jax.ShapeDtypeStruct((t, 512), F32)) + g_shapes,
        scratch_shapes=g_scratch,
        compiler_params=_tc_params(("arbitrary",)),
    )(x2, g_pre, w_in, *late)
    return res[:6], list(res[6:])


def _discretise(lr, li, ls):
    step = jnp.exp(ls)
    mag = jnp.exp(lr * step)
    ar = mag * jnp.cos(li * step)
    ai = mag * jnp.sin(li * step)
    den = lr * lr + li * li
    cr = ((ar - 1.0) * lr + ai * li) / den
    ci = (ai * lr - (ar - 1.0) * li) / den
    return step, ar, ai, den, cr, ci


def _per_channel(v):
    return jnp.broadcast_to(v[:, None, :], (SSM_G, SSM_P, SSM_N)).reshape(SSM_G * SSM_P, SSM_N)


def _ssm_prep(lam_re, lam_im, log_step, b_re, b_im, seg):
    def body(lr_ref, li_ref, ls_ref, br_ref, bi_ref, lrr_ref, lir_ref, lsr_ref,
             ar_ref, ai_ref, bbr_ref, bbi_ref, pr_ref, pi_ref):
        _, _, _, _, cr, ci = _discretise(lr_ref[...], li_ref[...], ls_ref[...])
        cr, ci = _per_channel(cr), _per_channel(ci)
        br, bi = br_ref[...], bi_ref[...]
        bbr_ref[...] = cr * br - ci * bi
        bbi_ref[...] = cr * bi + ci * br
        stepr = jnp.exp(lsr_ref[...])
        k = (lax.broadcasted_iota(jnp.int32, (8, N_STATE), 0) + 1).astype(F32)
        magk = jnp.exp(k * (lrr_ref[...] * stepr))
        ang = k * (lir_ref[...] * stepr)
        pr_ref[0:8, :] = magk * jnp.cos(ang)
        pi_ref[0:8, :] = magk * jnp.sin(ang)
        n = 8
        while n < seg:
            tr, ti = pr_ref[n - 1:n, :], pi_ref[n - 1:n, :]
            xr, xi = pr_ref[0:n, :], pi_ref[0:n, :]
            pr_ref[n:2 * n, :] = xr * tr - xi * ti
            pi_ref[n:2 * n, :] = xr * ti + xi * tr
            n *= 2
        ar_ref[...] = pr_ref[0:1, :]
        ai_ref[...] = pi_ref[0:1, :]

    row = jax.ShapeDtypeStruct((1, N_STATE), F32)
    mat = jax.ShapeDtypeStruct((SSM_G * SSM_P, SSM_N), F32)
    pw = jax.ShapeDtypeStruct((seg, N_STATE), F32)
    vm = pl.BlockSpec(memory_space=pltpu.VMEM)
    step_row = jnp.broadcast_to(log_step, (SSM_G, SSM_N)).reshape(1, N_STATE)
    return pl.pallas_call(
        body, name="ssm_prep", out_shape=(row, row, mat, mat, pw, pw),
        in_specs=[vm] * 8, out_specs=(vm,) * 6,
    )(lam_re, lam_im, log_step, b_re, b_im, lam_re.reshape(1, N_STATE), lam_im.reshape(1, N_STATE), step_row)


def _seg_rows(t):
    if isinstance(t, int):
        return pl.ds(t * N_SEG, N_SEG)
    return pl.ds(pl.multiple_of(t * N_SEG, N_SEG), N_SEG)


def _scan_forward(xs, a_re, a_im, pw_re, pw_im, cs, seg):
    are = jnp.broadcast_to(a_re, (N_SEG, ST_T))
    aim = jnp.broadcast_to(a_im, (N_SEG, ST_T))

    def steps(k, carry):
        xr, xi = carry
        for j in range(SCAN_UNROLL):
            r = pl.multiple_of((k * SCAN_UNROLL + j) * N_SEG, N_SEG)
            nr = are * xr - aim * xi + xs[pl.ds(r, N_SEG), 0:ST_T]
            ni = are * xi + aim * xr + xs[pl.ds(r, N_SEG), ST_T:2 * ST_T]
            xs[pl.ds(r, N_SEG), 0:ST_T] = nr
            xs[pl.ds(r, N_SEG), ST_T:2 * ST_T] = ni
            xr, xi = nr, ni
        return xr, xi

    zero = jnp.zeros((N_SEG, ST_T), F32)
    fr, fi = lax.fori_loop(0, seg // SCAN_UNROLL, steps, (zero, zero))
    sr, si = pw_re[seg - 1:seg, :], pw_im[seg - 1:seg, :]
    cr = jnp.zeros((1, ST_T), F32)
    ci = jnp.zeros((1, ST_T), F32)
    cs[0:1, :] = cr
    cs[8:9, :] = ci
    for s in range(1, N_SEG):
        ncr = sr * cr - si * ci + fr[s - 1:s, :]
        nci = sr * ci + si * cr + fi[s - 1:s, :]
        cr, ci = ncr, nci
        cs[s:s + 1, :] = cr
        cs[8 + s:9 + s, :] = ci
    car, cai = cs[0:8, :], cs[8:16, :]

    def fix(t, _):
        r = pl.multiple_of(t * N_SEG, N_SEG)
        pr, pi = pw_re[pl.ds(t, 1), :], pw_im[pl.ds(t, 1), :]
        xs[pl.ds(r, N_SEG), 0:ST_T] = xs[pl.ds(r, N_SEG), 0:ST_T] + (pr * car - pi * cai)
        xs[pl.ds(r, N_SEG), ST_T:2 * ST_T] = xs[pl.ds(r, N_SEG), ST_T:2 * ST_T] + (pr * cai + pi * car)
        return 0

    lax.fori_loop(0, seg, fix, 0, unroll=SCAN_UNROLL)


def _ssm_forward(u_perm, bcat, ccat, a_re, a_im, pw_re, pw_im, d_row, seg):
    bl, rows, _ = u_perm.shape

    def body(u_ref, b_ref, c_ref, ar_ref, ai_ref, pr_ref, pi_ref, d_ref, y_ref, xs_ref, cs_ref):
        u = u_ref[0]
        xs, cs = xs_ref.at[0, 0], cs_ref.at[0, 0]
        xs[...] = _mm(u, b_ref[0])
        _scan_forward(xs, ar_ref[...], ai_ref[...], pr_ref, pi_ref, cs, seg)
        y_ref[0] = _mm(xs[...], c_ref[0]) + d_ref[...] * u

    state = lambda r, c: pl.BlockSpec((1, 1, r, c), lambda b, j: (b, j, 0, 0))
    return pl.pallas_call(
        body, name="ssm_forward", grid=(bl, N_GT),
        in_specs=[pl.BlockSpec((1, rows, CH_T), lambda b, j: (b, 0, j)),
                  pl.BlockSpec((1, CH_T, 2 * ST_T), lambda b, j: (j, 0, 0)),
                  pl.BlockSpec((1, 2 * ST_T, CH_T), lambda b, j: (j, 0, 0)),
                  pl.BlockSpec((1, ST_T), lambda b, j: (0, j)), pl.BlockSpec((1, ST_T), lambda b, j: (0, j)),
                  pl.BlockSpec((seg, ST_T), lambda b, j: (0, j)), pl.BlockSpec((seg, ST_T), lambda b, j: (0, j)),
                  pl.BlockSpec((1, CH_T), lambda b, j: (0, j))],
        out_specs=(pl.BlockSpec((1, rows, CH_T), lambda b, j: (b, 0, j)), state(rows, 2 * ST_T), state(16, ST_T)),
        out_shape=(jax.ShapeDtypeStruct((bl, rows, D_SSM), F32),
                   jax.ShapeDtypeStruct((bl, N_GT, rows, 2 * ST_T), F32),
                   jax.ShapeDtypeStruct((bl, N_GT, 16, ST_T), F32)),
        compiler_params=_tc_params(("arbitrary", "arbitrary")),
    )(u_perm, bcat, ccat, a_re, a_im, pw_re, pw_im, d_row)


def _ssm_backward(u_perm, dy_perm, states, carries, bcat_t, ccat_t, a_re, a_im, pw_re, pw_im, d_row, seg):
    bl, rows, _ = u_perm.shape

    def body(u_ref, dy_ref, xs_ref, cs_ref, bt_ref, ct_ref, ar_ref, ai_ref, pr_ref, pi_ref, d_ref,
             du_ref, db_ref, dc_ref, dar_ref, dai_ref, dd_ref, ls, cl):
        b = pl.program_id(1)
        u = u_ref[0]
        dy = dy_ref[0]
        xs, cs = xs_ref.at[0, 0], cs_ref.at[0, 0]
        ls[...] = _mm(dy, ct_ref[0])
        are = jnp.broadcast_to(ar_ref[...], (N_SEG, ST_T))
        aim = jnp.broadcast_to(ai_ref[...], (N_SEG, ST_T))

        def steps(k, carry):
            lr, li = carry
            for j in range(SCAN_UNROLL):
                r = pl.multiple_of((seg - 1 - (k * SCAN_UNROLL + j)) * N_SEG, N_SEG)
                nr = are * lr + aim * li + ls[pl.ds(r, N_SEG), 0:ST_T]
                ni = are * li - aim * lr + ls[pl.ds(r, N_SEG), ST_T:2 * ST_T]
                ls[pl.ds(r, N_SEG), 0:ST_T] = nr
                ls[pl.ds(r, N_SEG), ST_T:2 * ST_T] = ni
                lr, li = nr, ni
            return lr, li

        zero = jnp.zeros((N_SEG, ST_T), F32)
        fr, fi = lax.fori_loop(0, seg // SCAN_UNROLL, steps, (zero, zero))
        sr, si = pr_ref[seg - 1:seg, :], pi_ref[seg - 1:seg, :]
        cr = jnp.zeros((1, ST_T), F32)
        ci = jnp.zeros((1, ST_T), F32)
        cl[7:8, :] = cr
        cl[15:16, :] = ci
        for s in range(N_SEG - 2, -1, -1):
            ncr = sr * cr + si * ci + fr[s + 1:s + 2, :]
            nci = sr * ci - si * cr + fi[s + 1:s + 2, :]
            cr, ci = ncr, nci
            cl[s:s + 1, :] = cr
            cl[8 + s:9 + s, :] = ci
        clr, cli = cl[0:8, :], cl[8:16, :]

        def fix_rows(rows, t, xpr, xpi, acc):
            dr, di = acc
            pr, pi = pr_ref[pl.ds(seg - 1 - t, 1), :], pi_ref[pl.ds(seg - 1 - t, 1), :]
            lr = ls[rows, 0:ST_T] + (pr * clr + pi * cli)
            li = ls[rows, ST_T:2 * ST_T] + (pr * cli - pi * clr)
            ls[rows, 0:ST_T] = lr
            ls[rows, ST_T:2 * ST_T] = li
            return dr + (lr * xpr + li * xpi), di + (li * xpr - lr * xpi)

        def fix_at(t, acc):
            prev = _seg_rows(t - 1)
            return fix_rows(_seg_rows(t), t, xs[prev, 0:ST_T], xs[prev, ST_T:2 * ST_T], acc)

        def fix(k, acc):
            for j in range(SCAN_UNROLL):
                acc = fix_at(k * SCAN_UNROLL + j, acc)
            return acc

        acc = fix_rows(pl.ds(0, N_SEG), 0, cs[0:8, :], cs[8:16, :], (zero, zero))
        for t in range(1, SCAN_UNROLL):
            acc = fix_at(t, acc)
        dr, di = lax.fori_loop(1, seg // SCAN_UNROLL, fix, acc)
        dar = jnp.sum(dr, axis=0, keepdims=True)
        dai = jnp.sum(di, axis=0, keepdims=True)
        lall = ls[...]
        du_ref[0] = (_mm(lall, bt_ref[0]) + d_ref[...] * dy).astype(BF16)
        dbp = _mm_tn(u, lall)
        dcp = _mm_tn(dy, xs[...])
        ddp = jnp.sum(dy * u, axis=0, keepdims=True)

        @pl.when(b == 0)
        def _():
            db_ref[0] = dbp
            dc_ref[0] = dcp
            dar_ref[...] = dar
            dai_ref[...] = dai
            dd_ref[...] = ddp

        @pl.when(b != 0)
        def _():
            db_ref[0] += dbp
            dc_ref[0] += dcp
            dar_ref[...] += dar
            dai_ref[...] += dai
            dd_ref[...] += ddp

    tile3 = lambda r, c: pl.BlockSpec((1, r, c), lambda j, b: (j, 0, 0))
    lane = lambda r, c: pl.BlockSpec((r, c), lambda j, b: (0, j))
    act = pl.BlockSpec((1, rows, CH_T), lambda j, b: (b, 0, j))
    state = lambda r, c: pl.BlockSpec((1, 1, r, c), lambda j, b: (b, j, 0, 0))
    return pl.pallas_call(
        body, name="ssm_backward", grid=(N_GT, bl),
        in_specs=[act, act, state(rows, 2 * ST_T), state(16, ST_T), tile3(2 * ST_T, CH_T), tile3(CH_T, 2 * ST_T),
                  lane(1, ST_T), lane(1, ST_T), lane(seg, ST_T), lane(seg, ST_T), lane(1, CH_T)],
        out_specs=(act, tile3(CH_T, 2 * ST_T), tile3(CH_T, 2 * ST_T), lane(1, ST_T), lane(1, ST_T), lane(1, CH_T)),
        out_shape=(jax.ShapeDtypeStruct((bl, rows, D_SSM), BF16),
                   jax.ShapeDtypeStruct((N_GT, CH_T, 2 * ST_T), F32), jax.ShapeDtypeStruct((N_GT, CH_T, 2 * ST_T), F32),
                   jax.ShapeDtypeStruct((1, N_STATE), F32), jax.ShapeDtypeStruct((1, N_STATE), F32),
                   jax.ShapeDtypeStruct((1, D_SSM), F32)),
        scratch_shapes=[pltpu.VMEM((rows, 2 * ST_T), F32), pltpu.VMEM((16, ST_T), F32)],
        compiler_params=_tc_params(("arbitrary", "arbitrary")),
    )(u_perm, dy_perm, states, carries, bcat_t, ccat_t, a_re, a_im, pw_re, pw_im, d_row)


def _ssm_param_grads(lam_re, lam_im, log_step, b_re, b_im, da_re, da_im, dbb_re, dbb_im):
    def body(lr_ref, li_ref, ls_ref, br_ref, bi_ref, gar_ref, gai_ref, gbr_ref, gbi_ref,
             dlr_ref, dli_ref, dls_ref, dbr_ref, dbi_ref):
        lr, li = lr_ref[...], li_ref[...]
        step, ar, ai, den, cr, ci = _discretise(lr, li, ls_ref[...])
        crb, cib = _per_channel(cr), _per_channel(ci)
        br, bi = br_ref[...], bi_ref[...]
        gbr, gbi = gbr_ref[...], gbi_ref[...]
        dbr_ref[...] = crb * gbr + cib * gbi
        dbi_ref[...] = crb * gbi - cib * gbr
        over_channels = lambda t: jnp.sum(t.reshape(SSM_G, SSM_P, SSM_N), axis=1)
        gcr = over_channels(br * gbr + bi * gbi)
        gci = over_channels(br * gbi - bi * gbr)
        ilr, ili = lr / den, -li / den
        gar = gar_ref[...] + (ilr * gcr + ili * gci)
        gai = gai_ref[...] + (ilr * gci - ili * gcr)
        qr, qi = cr * ilr - ci * ili, cr * ili + ci * ilr
        glr = -(qr * gcr + qi * gci)
        gli = -(qr * gci - qi * gcr)
        gwr = ar * gar + ai * gai
        gwi = ar * gai - ai * gar
        dlr_ref[...] = glr + step * gwr
        dli_ref[...] = gli + step * gwi
        dls_ref[...] = jnp.sum(lr * gwr + li * gwi, axis=-1, keepdims=True) * step

    lam = jax.ShapeDtypeStruct((SSM_G, SSM_N), F32)
    mat = jax.ShapeDtypeStruct((SSM_G * SSM_P, SSM_N), F32)
    vm = pl.BlockSpec(memory_space=pltpu.VMEM)
    return pl.pallas_call(
        body, name="ssm_param_grads", out_shape=(lam, lam, jax.ShapeDtypeStruct((SSM_G, 1), F32), mat, mat),
        in_specs=[vm] * 9, out_specs=(vm,) * 5,
    )(lam_re, lam_im, log_step, b_re, b_im, da_re, da_im, dbb_re, dbb_im)


ROWS4 = Q_PER_KV * ATT_BLOCK


def _att_dist_mask(first_block):
    qi = lax.broadcasted_iota(jnp.int32, (ROWS4, 2 * ATT_BLOCK), 0) & (ATT_BLOCK - 1)
    si = lax.broadcasted_iota(jnp.int32, (ROWS4, 2 * ATT_BLOCK), 1)
    dist = qi + ATT_BLOCK - si
    valid = (dist >= 0) & (dist < ATT_BLOCK) & ((si >= ATT_BLOCK) | jnp.logical_not(first_block))
    return dist.astype(F32), valid


def _stack_heads(x, kv):
    return jnp.concatenate([x[:, (kv * Q_PER_KV + g) * HEAD_DIM:(kv * Q_PER_KV + g + 1) * HEAD_DIM]
                            for g in range(Q_PER_KV)], axis=0)


def _stack_cols(x, kv):
    return jnp.concatenate([x[:, kv * Q_PER_KV + g:kv * Q_PER_KV + g + 1] for g in range(Q_PER_KV)], axis=0)


def _per_head_col(vals):
    return jnp.concatenate([jnp.full((ATT_BLOCK, 1), v, F32) for v in vals], axis=0)


def _attn_forward(q, k, v, sinks, late, bl, nb):
    t = q.shape[0]
    n = len(late)
    steps = bl * nb

    def body(*refs):
        sink_ref, q_ref, kp_ref, kc_ref, vp_ref, vc_ref = refs[:6]
        late_r = refs[6:6 + n]
        o_ref, lse_ref = refs[6 + n:8 + n]
        gath = refs[8 + n:8 + 2 * n]
        cast = refs[8 + 2 * n:8 + 3 * n]
        send_sems, recv_sems, local_sems = refs[8 + 3 * n:]
        i = pl.program_id(1)
        step = pl.program_id(0) * nb + i
        start, relay, finish = _gather_phases(late_r, gath, cast, send_sems, recv_sems, local_sems)
        pl.when(step == 0)(start)
        pl.when(step == (steps * 5) // 8)(relay)
        pl.when(step == steps - 1)(finish)
        dist4, valid4 = _att_dist_mask(i == 0)
        dist, valid = dist4[0:ATT_BLOCK, :], valid4[0:ATT_BLOCK, :]
        kk = jnp.concatenate([kp_ref[...], kc_ref[...]], axis=0)
        vv = jnp.concatenate([vp_ref[...], vc_ref[...]], axis=0)
        qv = q_ref[...]
        for h in range(N_HEADS):
            kv = h // Q_PER_KV
            slope = 2.0 ** (-(h + 1))
            qh = qv[:, h * HEAD_DIM:(h + 1) * HEAD_DIM]
            kh = kk[:, kv * HEAD_DIM:(kv + 1) * HEAD_DIM]
            vh = vv[:, kv * HEAD_DIM:(kv + 1) * HEAD_DIM]
            s = _mm_nt(qh, kh) * ATT_SCALE - slope * dist
            s = jnp.where(valid, s, NEG_BIG)
            sink = sink_ref[h]
            m = jnp.maximum(jnp.max(s, axis=-1, keepdims=True), sink)
            e = jnp.exp(s - m)
            den = jnp.sum(e, axis=-1, keepdims=True) + jnp.exp(sink - m)
            o_ref[:, h * HEAD_DIM:(h + 1) * HEAD_DIM] = _mm(e, vh) * (1.0 / den)
            lse_ref[:, h:h + 1] = m + jnp.log(den)

    cur = lambda w: pl.BlockSpec((ATT_BLOCK, w), lambda b, i: (b * nb + i, 0))
    prev = lambda w: pl.BlockSpec((ATT_BLOCK, w), lambda b, i: (b * nb + jnp.maximum(i - 1, 0), 0))
    g_specs, g_shapes, g_scratch = _gather_operands(late)
    res = pl.pallas_call(
        body, name="attn_forward", grid=(bl, nb),
        in_specs=[pl.BlockSpec(memory_space=pltpu.SMEM), cur(512), prev(128), cur(128), prev(128), cur(128)]
        + [pl.BlockSpec(s.shape, lambda b, i: (0, 0)) for s in late],
        out_specs=(cur(512), cur(N_HEADS)) + g_specs,
        out_shape=(jax.ShapeDtypeStruct((t, D_ATTN), F32), jax.ShapeDtypeStruct((t, N_HEADS), F32)) + g_shapes,
        scratch_shapes=g_scratch,
        compiler_params=_tc_params(("arbitrary", "arbitrary")),
    )(sinks, q, k, k, v, v, *late)
    return res[:2], list(res[2:])


def _attn_backward(q, k, v, o, do, lse, sinks, late16, late_own, bl, nb):
    t = q.shape[0]
    n = len(late16)
    steps = bl * (nb + 1)
    mid1, mid2, last = steps // 4, (steps * 3) // 4, steps - 1

    def body(*refs):
        sink_ref, qc_ref, kp_ref, kc_ref, vp_ref, vc_ref, oc_ref, doc_ref, lc_ref = refs[:9]
        g16_r, go_r = refs[9:9 + n], refs[9 + n:9 + 2 * n]
        dq_ref, dk_ref, dv_ref, ds_ref = refs[9 + 2 * n:13 + 2 * n]
        red = refs[13 + 2 * n:13 + 3 * n]
        own16, recv1, send2, recv2 = (refs[13 + 3 * n + k * n:13 + 3 * n + (k + 1) * n] for k in range(4))
        s_send, s_recv, s_local, dk_carry, dv_carry = refs[13 + 7 * n:]
        b, i = pl.program_id(0), pl.program_id(1)
        step = b * (nb + 1) + i
        x, y, c = _mesh_pos()
        sibling = (x, y, 1 - c)
        chips = [(1 - x, y), (x, 1 - y), (1 - x, 1 - y)]
        all_chips = [(x, y)] + chips

        def lvl1(a, j):
            return _remote(g16_r[a].at[_slot(*all_chips[j], 1 - c)], recv1[a].at[j], s_send, s_recv, 7 * a + j, sibling)

        def lvl2(a, j):
            return _remote(send2[a].at[j], recv2[a].at[j], s_send, s_recv, 7 * a + 4 + j, (*chips[j], c))

        def mine(a, j):
            return pltpu.make_async_copy(g16_r[a].at[_slot(*chips[j], c)], own16[a].at[j], s_local.at[3 * a + j])

        @pl.when(step == 0)
        def _():
            for a in range(n):
                for j in range(3):
                    mine(a, j).start()
                for j in range(4):
                    lvl1(a, j).start()

        @pl.when(step == mid1)
        def _():
            for a in range(n):
                for j in range(3):
                    mine(a, j).wait()
                for j in range(4):
                    lvl1(a, j).wait_recv()

                def partials(r, a=a):
                    red[a][r, :] = go_r[a][r, :] + recv1[a][0, r, :].astype(F32)
                    for j in range(3):
                        send2[a][j, r, :] = (own16[a][j, r, :].astype(F32)
                                             + recv1[a][1 + j, r, :].astype(F32)).astype(BF16)

                _row_chunks(go_r[a].shape[0], partials)
                for j in range(3):
                    lvl2(a, j).start()

        @pl.when(step == mid2)
        def _():
            for a in range(n):
                for j in range(3):
                    lvl2(a, j).wait_recv()

                def total(r, a=a):
                    g = red[a][r, :]
                    for j in range(3):
                        g = g + recv2[a][j, r, :].astype(F32)
                    red[a][r, :] = g

                _row_chunks(go_r[a].shape[0], total)

        @pl.when(step == last)
        def _():
            for a in range(n):
                for j in range(4):
                    lvl1(a, j).wait_send()
                for j in range(3):
                    lvl2(a, j).wait_send()

        live = i < nb

        @pl.when(i == 0)
        def _():
            dk_carry[...] = jnp.zeros((ATT_BLOCK, KV_HEADS * HEAD_DIM), F32)
            dv_carry[...] = jnp.zeros((ATT_BLOCK, KV_HEADS * HEAD_DIM), F32)

        dist, valid = _att_dist_mask(i == 0)
        valid = valid & live
        kk = jnp.concatenate([kp_ref[...], kc_ref[...]], axis=0)
        vv = jnp.concatenate([vp_ref[...], vc_ref[...]], axis=0)
        qc, oc, doc, lc = qc_ref[...], oc_ref[...], doc_ref[...], lc_ref[...]
        dsink_cols, dq_parts = [], []
        for kv in range(KV_HEADS):
            heads = range(kv * Q_PER_KV, (kv + 1) * Q_PER_KV)
            cols = slice(kv * HEAD_DIM, (kv + 1) * HEAD_DIM)
            kh, vh = kk[:, cols], vv[:, cols]
            slope = _per_head_col([2.0 ** (-(h + 1)) for h in heads])
            sink = _per_head_col([sink_ref[h] for h in heads])
            q4, do4 = _stack_heads(qc, kv), _stack_heads(doc, kv)
            delta = jnp.sum(do4 * _stack_heads(oc, kv), axis=-1, keepdims=True)
            lse4 = _stack_cols(lc, kv)
            s = _mm_nt(q4, kh) * ATT_SCALE - slope * dist
            p = jnp.where(valid, jnp.exp(s - lse4), 0.0)
            dsc = p * (_mm_nt(do4, vh) - delta)
            dq4 = _mm(dsc, kh) * ATT_SCALE
            dk2 = _mm_tn(dsc, q4) * ATT_SCALE
            dv2 = _mm_tn(p, do4)
            dsink4 = jnp.where(live, jnp.exp(sink - lse4) * delta, 0.0)
            dk_ref[:, cols] = dk_carry[:, cols] + dk2[0:ATT_BLOCK, :]
            dv_ref[:, cols] = dv_carry[:, cols] + dv2[0:ATT_BLOCK, :]
            dk_carry[:, cols] = dk2[ATT_BLOCK:, :]
            dv_carry[:, cols] = dv2[ATT_BLOCK:, :]
            for g, h in enumerate(heads):
                rows = slice(g * ATT_BLOCK, (g + 1) * ATT_BLOCK)
                dq_parts.append((h, dq4[rows, :]))
                dsink_cols.append(-jnp.sum(dsink4[rows, :], axis=0, keepdims=True))
        dsink = jnp.concatenate(dsink_cols, axis=1)

        @pl.when(live)
        def _():
            for h, part in dq_parts:
                dq_ref[:, h * HEAD_DIM:(h + 1) * HEAD_DIM] = part

        @pl.when((b == 0) & (i == 0))
        def _():
            ds_ref[...] = dsink

        @pl.when((b != 0) | (i != 0))
        def _():
            ds_ref[...] += dsink

    cur_i = lambda i: jnp.minimum(i, nb - 1)
    cur = lambda w: pl.BlockSpec((ATT_BLOCK, w), lambda b, i: (b * nb + cur_i(i), 0))
    prev = lambda w: pl.BlockSpec((ATT_BLOCK, w), lambda b, i: (b * nb + jnp.maximum(cur_i(i) - 1, 0), 0))
    behind = lambda w: pl.BlockSpec((ATT_BLOCK, w), lambda b, i: (b * nb + jnp.maximum(i - 1, 0), 0))
    const2 = lambda s: pl.BlockSpec(s, lambda b, i: (0, 0))
    shard = [s.shape for s in late_own]
    res = pl.pallas_call(
        body, name="attn_backward", grid=(bl, nb + 1),
        in_specs=[pl.BlockSpec(memory_space=pltpu.SMEM), cur(512), prev(128), cur(128), prev(128), cur(128),
                  cur(512), cur(512), cur(N_HEADS)]
        + [pl.BlockSpec(memory_space=pl.ANY)] * n + [const2(s) for s in shard],
        out_specs=(cur(512), behind(128), behind(128), const2((1, N_HEADS))) + tuple(const2(s) for s in shard),
        out_shape=(jax.ShapeDtypeStruct((t, D_ATTN), F32), jax.ShapeDtypeStruct((t, 128), F32),
                   jax.ShapeDtypeStruct((t, 128), F32), jax.ShapeDtypeStruct((1, N_HEADS), F32))
        + tuple(jax.ShapeDtypeStruct(s, F32) for s in shard),
        scratch_shapes=[pltpu.VMEM((3,) + s, BF16) for s in shard] + [pltpu.VMEM((4,) + s, BF16) for s in shard]
        + [pltpu.VMEM((3,) + s, BF16) for s in shard] + [pltpu.VMEM((3,) + s, BF16) for s in shard]
        + [pltpu.SemaphoreType.DMA((7 * n,)), pltpu.SemaphoreType.DMA((7 * n,)), pltpu.SemaphoreType.DMA((3 * n,)),
           pltpu.VMEM((ATT_BLOCK, KV_HEADS * HEAD_DIM), F32), pltpu.VMEM((ATT_BLOCK, KV_HEADS * HEAD_DIM), F32)],
        compiler_params=_tc_params(("arbitrary", "arbitrary")),
    )(sinks, q, k, k, v, v, o, do, lse, *late16, *late_own)
    return res[:4], list(res[4:])


def _mix_forward_backward(x2, y_perm, z_ssm, attn, z_attn, p2, target2, w_glu, b_glu, w_out, g_post, w_gate, b_gate,
                          w_proj, bl, seg):
    t = x2.shape[0]
    tm = seg

    def body(x_ref, y_ref, zs_ref, at_ref, za_ref, p_ref, tg_ref,
             wglu_ref, bglu_ref, wout_ref, gpost_ref, wgate_ref, bgate_ref, wproj_ref,
             loss_ref, dh1_ref, dy_ref, dzs_ref, dat_ref, dza_ref,
             dwglu_ref, dbglu_ref, dwout_ref, dgpost_ref, dwgate_ref, dbgate_ref, dwproj_ref,
             dwout16_ref, dwgate16_ref, dwproj16_ref, dwglu16_ref):
        i = pl.program_id(0)
        gpost = gpost_ref[...]

        @pl.when(i == 0)
        def _():
            for ref in (dwglu_ref, dbglu_ref, dwout_ref, dgpost_ref, dwgate_ref, dbgate_ref, dwproj_ref, loss_ref):
                ref[...] = jnp.zeros(ref.shape, F32)

        def chain(rows):
            y = y_ref[0, rows, :]
            u3 = GELU_C * (y + GELU_K * y * y * y)
            th = jnp.tanh(u3)
            gl = 0.5 * y * (1.0 + th)
            a = _mm(gl, wglu_ref[...]) + bglu_ref[...]
            sa = _sigmoid(a)
            glu = gl * sa
            zs = zs_ref[rows, :]
            sgs = _sigmoid(zs)
            ssm_out = glu * (zs * sgs)
            za = za_ref[rows, :]
            sga = _sigmoid(za)
            at = at_ref[rows, :]
            attn_out = at * (za * sga)
            cat = jnp.concatenate([ssm_out, attn_out], axis=-1).astype(BF16)
            mixed = _mm(cat, wout_ref[...])
            r2 = lax.rsqrt(jnp.mean(mixed * mixed, axis=-1, keepdims=True) + EPS)
            nhat = mixed * r2
            h1 = x_ref[rows, :] + nhat * gpost
            gate = _sigmoid(_mm(h1, wgate_ref[...]) + bgate_ref[...])
            pv = p_ref[rows, :]
            pp = _mm(pv, wproj_ref[...])
            h2 = h1 + gate * pp
            err = h2 - tg_ref[rows, :]
            loss_part = jnp.sum(jnp.sum(err * err, axis=-1, keepdims=True), axis=0, keepdims=True) * (0.5 / D_MODEL)
            dh2 = err * (1.0 / D_MODEL)
            dgp = dh2 * pp * gate * (1.0 - gate)
            dpp = dh2 * gate
            dh1 = dh2 + _mm_nt(dgp, wgate_ref[...])
            dh1_ref[rows, :] = dh1
            dnhat = dh1 * gpost
            dmixed = r2 * (dnhat - nhat * jnp.mean(dnhat * nhat, axis=-1, keepdims=True))
            dcat = _mm_nt(dmixed, wout_ref[...])
            dso, dao = dcat[:, 0:D_SSM], dcat[:, D_SSM:]
            dat_ref[rows, :] = dao * (za * sga)
            dza_ref[rows, :] = (dao * at * (sga * (1.0 + za * (1.0 - sga)))).astype(BF16)
            dzs_ref[rows, :] = (dso * glu * (sgs * (1.0 + zs * (1.0 - sgs)))).astype(BF16)
            dglu = dso * (zs * sgs)
            da = dglu * gl * sa * (1.0 - sa)
            dgl = dglu * sa + _mm_nt(da, wglu_ref[...])
            dgelu = 0.5 * (1.0 + th) + 0.5 * y * (1.0 - th * th) * (GELU_C * (1.0 + 3.0 * GELU_K * y * y))
            dy_ref[0, rows, :] = dgl * dgelu
            return dict(gl=gl.astype(BF16), da=da.astype(BF16), cat=cat, dmixed=dmixed.astype(BF16),
                        h1=h1.astype(BF16), dgp=dgp.astype(BF16), pv=pv.astype(BF16), dpp=dpp.astype(BF16),
                        dbglu=jnp.sum(da, axis=0, keepdims=True), dgpost=jnp.sum(dh1 * nhat, axis=0, keepdims=True),
                        dbgate=jnp.sum(dgp, axis=0, keepdims=True), loss=loss_part)

        groups = [chain(slice(k * (tm // MIX_GROUPS), (k + 1) * (tm // MIX_GROUPS))) for k in range(MIX_GROUPS)]
        rows_of = lambda name: jnp.concatenate([g[name] for g in groups], axis=0)
        total = lambda name: sum(g[name] for g in groups)
        parts = (
            (dwglu_ref, _mm_tn(rows_of("gl"), rows_of("da"))), (dbglu_ref, total("dbglu")),
            (dwout_ref, _mm_tn(rows_of("cat"), rows_of("dmixed"))), (dgpost_ref, total("dgpost")),
            (dwgate_ref, _mm_tn(rows_of("h1"), rows_of("dgp"))), (dbgate_ref, total("dbgate")),
            (dwproj_ref, _mm_tn(rows_of("pv"), rows_of("dpp"))), (loss_ref, total("loss")),
        )

        for ref, val in parts:
            ref[...] += val

        @pl.when(i == t // tm - 1)
        def _():
            for ref16, ref in ((dwout16_ref, dwout_ref), (dwgate16_ref, dwgate_ref), (dwproj16_ref, dwproj_ref),
                               (dwglu16_ref, dwglu_ref)):
                def to16(r, ref16=ref16, ref=ref):
                    ref16[r, :] = ref[r, :].astype(BF16)

                _row_chunks(ref.shape[0], to16)

    row = lambda w: pl.BlockSpec((tm, w), lambda i: (i, 0))
    perm = pl.BlockSpec((1, tm, D_SSM), lambda i: (i // N_SEG, 0, i % N_SEG))
    perm_shape = jax.ShapeDtypeStruct((bl, seg, N_SEG * D_SSM), F32)
    acc = lambda r, c, dt=F32: (_const_spec((r, c)), jax.ShapeDtypeStruct((r, c), dt))
    accs = [acc(D_SSM, D_SSM), acc(1, D_SSM), acc(D_MODEL, D_MODEL), acc(1, D_MODEL), acc(D_MODEL, D_MODEL),
            acc(1, D_MODEL), acc(D_PLE, D_MODEL),
            acc(D_MODEL, D_MODEL, BF16), acc(D_MODEL, D_MODEL, BF16), acc(D_PLE, D_MODEL, BF16), acc(D_SSM, D_SSM, BF16)]
    return pl.pallas_call(
        body, name="mix_forward_backward", grid=(t // tm,),
        in_specs=[row(D_MODEL), perm, row(512), row(512), row(512), row(D_PLE), row(D_MODEL),
                  _const_spec((D_SSM, D_SSM)), _const_spec((1, D_SSM)), _const_spec((D_MODEL, D_MODEL)),
                  _const_spec((1, D_MODEL)), _const_spec((D_MODEL, D_MODEL)), _const_spec((1, D_MODEL)),
                  _const_spec((D_PLE, D_MODEL))],
        out_specs=(_const_spec((1, 1)), row(D_MODEL), perm, row(512), row(512), row(512)) + tuple(a[0] for a in accs),
        out_shape=(jax.ShapeDtypeStruct((1, 1), F32), jax.ShapeDtypeStruct((t, D_MODEL), F32), perm_shape,
                   jax.ShapeDtypeStruct((t, 512), BF16), jax.ShapeDtypeStruct((t, 512), F32),
                   jax.ShapeDtypeStruct((t, 512), BF16)) + tuple(a[1] for a in accs),
        compiler_params=_tc_params(("arbitrary",)),
    )(x2, y_perm, z_ssm, attn, z_attn, p2, target2, w_glu, b_glu, w_out, g_post, w_gate, b_gate, w_proj)


def _in_backward(x2, dh1, du_perm, dz_ssm, dq, dk, dv, dz_attn, g_pre, w_in, bl, seg):
    t = x2.shape[0]
    tm = seg

    def body(x_ref, dh1_ref, du_ref, dzs_ref, dq_ref, dk_ref, dv_ref, dza_ref, g_ref, w_ref,
             gx_ref, dw_ref, dg_ref, dw16_ref):
        i = pl.program_id(0)
        xv = x_ref[...]
        r = lax.rsqrt(jnp.mean(xv * xv, axis=-1, keepdims=True) + EPS)
        xhat = xv * r
        g = g_ref[...]
        hn = (xhat * g).astype(BF16)
        dproj = jnp.concatenate([du_ref[0].astype(BF16), dzs_ref[...].astype(BF16), dq_ref[...].astype(BF16),
                                 dk_ref[...].astype(BF16), dv_ref[...].astype(BF16), dza_ref[...].astype(BF16)],
                                axis=-1)
        dhn = _mm(dproj, w_ref[...])
        dxhat = dhn * g
        gx_ref[...] = dh1_ref[...] + r * (dxhat - xhat * jnp.mean(dxhat * xhat, axis=-1, keepdims=True))
        @pl.when(i == 0)
        def _():
            dw_ref[...] = jnp.zeros((D_IN, D_MODEL), F32)
            dg_ref[...] = jnp.zeros((1, D_MODEL), F32)

        dw_ref[...] += _mm_tn(dproj, hn)
        dg_ref[...] += jnp.sum(dhn * xhat, axis=0, keepdims=True)

        @pl.when(i == t // tm - 1)
        def _():
            def to16(r):
                dw16_ref[r, :] = dw_ref[r, :].astype(BF16)

            _row_chunks(D_IN, to16)

    row = lambda w: pl.BlockSpec((tm, w), lambda i: (i, 0))
    perm = pl.BlockSpec((1, tm, D_SSM), lambda i: (i // N_SEG, 0, i % N_SEG))
    return pl.pallas_call(
        body, name="in_backward", grid=(t // tm,),
        in_specs=[row(D_MODEL), row(D_MODEL), perm, row(512), row(512), row(128), row(128), row(512),
                  _const_spec((1, D_MODEL)), _const_spec((D_IN, D_MODEL))],
        out_specs=(row(D_MODEL), _const_spec((D_IN, D_MODEL)), _const_spec((1, D_MODEL)),
                   _const_spec((D_IN, D_MODEL))),
        out_shape=(jax.ShapeDtypeStruct((t, D_MODEL), F32), jax.ShapeDtypeStruct((D_IN, D_MODEL), F32),
                   jax.ShapeDtypeStruct((1, D_MODEL), F32), jax.ShapeDtypeStruct((D_IN, D_MODEL), BF16)),
        compiler_params=_tc_params(("arbitrary",)),
    )(x2, dh1, du_perm, dz_ssm, dq, dk, dv, dz_attn, g_pre, w_in)


def _block_diag(t):
    a, b = t.shape[1], t.shape[2]
    eye = jnp.eye(G_TILE, dtype=t.dtype)
    t = t.reshape(N_GT, G_TILE, a, 1, b) * eye[None, :, None, :, None]
    return t.reshape(N_GT, G_TILE * a, G_TILE * b)


def _diag_blocks(m, a, b):
    m = m.reshape(N_GT, G_TILE, a, G_TILE, b)
    return jnp.einsum("tgagb->tgab", m).reshape(SSM_G, a, b)


def _local_step(x, p, target, pre_norm_g, w_in, ssm_lam_re, ssm_lam_im, ssm_log_step, ssm_b_re, ssm_b_im, ssm_c_re,
                ssm_c_im, ssm_d, ssm_b_glu, attn_sinks, post_norm_g, pl_b_gate, late, me):
    bl, seq, _ = x.shape
    seg = seq // N_SEG
    nb = seq // ATT_BLOCK
    t = bl * seq
    x2 = x.reshape(t, D_MODEL)
    p2 = p.reshape(t, D_PLE)
    tg2 = target.reshape(t, D_MODEL)

    lam_re, lam_im = ssm_lam_re, ssm_lam_im
    log_step = ssm_log_step.reshape(SSM_G, 1)
    a_re_row, a_im_row, bb_re, bb_im, pw_re, pw_im = _ssm_prep(lam_re, lam_im, log_step, ssm_b_re, ssm_b_im, seg)
    by_group = lambda t: t.reshape(SSM_G, SSM_P, SSM_N)
    bcat = jnp.concatenate([_block_diag(by_group(bb_re)), _block_diag(by_group(bb_im))], axis=-1).astype(BF16)
    ccat_t = jnp.concatenate([_block_diag(by_group(ssm_c_re)), -_block_diag(by_group(ssm_c_im))],
                             axis=-1).astype(BF16)
    bcat_t = jnp.swapaxes(bcat, 1, 2)
    ccat = jnp.swapaxes(ccat_t, 1, 2)
    d_row = ssm_d.reshape(1, D_SSM)

    (u_perm, z_ssm, q, k, v, z_attn), (g_out, g_glu) = _in_proj(
        x2, pre_norm_g.reshape(1, D_MODEL), w_in, [late[0], late[3]], bl, seg)
    u_perm = u_perm.reshape(bl, seq, D_SSM)
    y_perm, states, carries = _ssm_forward(u_perm, bcat, ccat, a_re_row, a_im_row, pw_re, pw_im, d_row, seg)
    sinks = attn_sinks.reshape(N_HEADS)
    (attn, lse), (g_gate, g_proj) = _attn_forward(q, k, v, sinks, [late[1], late[2]], bl, nb)
    w_out, w_gate, w_proj, w_glu = (_gathered_to_full(n, g) for n, g in zip(LATE_NAMES, (g_out, g_gate, g_proj, g_glu)))
    (loss, dh1, dy_perm, dz_ssm, dattn, dz_attn, d_w_glu, d_b_glu, d_w_out, d_g_post, d_w_gate, d_b_gate,
     d_w_proj, *late16) = _mix_forward_backward(
        x2, y_perm.reshape(bl, seg, N_SEG * D_SSM), z_ssm, attn, z_attn, p2, tg2, w_glu,
        ssm_b_glu.reshape(1, D_SSM), w_out, post_norm_g.reshape(1, D_MODEL), w_gate, pl_b_gate.reshape(1, D_MODEL),
        w_proj, bl, seg)
    owned = [_full_to_owned(n, d) for n, d in zip(LATE_NAMES, (d_w_out, d_w_gate, d_w_proj, d_w_glu))]
    (dq, dk, dv, d_sinks), late_grads = _attn_backward(
        q, k, v, attn, dattn, lse, sinks, [_full_to_owned(n, d) for n, d in zip(LATE_NAMES, late16)],
        [lax.dynamic_index_in_dim(o, me, axis=0, keepdims=False) for o in owned], bl, nb)
    du_perm, d_bcat, d_ccat_t, da_re, da_im, d_d = _ssm_backward(
        u_perm, dy_perm.reshape(bl, seq, D_SSM), states, carries, bcat_t, ccat_t, a_re_row, a_im_row, pw_re, pw_im,
        d_row, seg)
    grad_x, d_w_in, d_g_pre, d_w_in16 = _in_backward(
        x2, dh1, du_perm.reshape(bl, seg, N_SEG * D_SSM), dz_ssm, dq, dk, dv, dz_attn,
        pre_norm_g.reshape(1, D_MODEL), w_in, bl, seg)
    flat = lambda t: t.reshape(SSM_G * SSM_P, SSM_N)
    d_lam_re, d_lam_im, d_ls, d_b_re, d_b_im = _ssm_param_grads(
        lam_re, lam_im, log_step, ssm_b_re, ssm_b_im, da_re.reshape(SSM_G, SSM_N), da_im.reshape(SSM_G, SSM_N),
        flat(_diag_blocks(d_bcat[:, :, 0:ST_T], SSM_P, SSM_N)), flat(_diag_blocks(d_bcat[:, :, ST_T:], SSM_P, SSM_N)))
    grads = {
        "pre_norm_g": d_g_pre, "w_in": d_w_in, "w_in16": d_w_in16, "ssm_lam_re": d_lam_re, "ssm_lam_im": d_lam_im,
        "ssm_log_step": d_ls, "ssm_b_re": d_b_re, "ssm_b_im": d_b_im,
        "ssm_c_re": _diag_blocks(d_ccat_t[:, :, 0:ST_T], SSM_P, SSM_N),
        "ssm_c_im": -_diag_blocks(d_ccat_t[:, :, ST_T:], SSM_P, SSM_N),
        "ssm_d": d_d, "ssm_b_glu": d_b_glu, "attn_sinks": d_sinks, "post_norm_g": d_g_post, "pl_b_gate": d_b_gate,
    }
    return loss, grad_x.reshape(bl, seq, D_MODEL), grads, late_grads


LATE_NAMES = ("w_out", "pl_w_gate", "pl_w_proj", "ssm_w_glu")
BIG_NAMES = ("w_in",) + LATE_NAMES
COL_SHARDED = {"w_in": D_IN // N_DEV, "pl_w_proj": D_MODEL // N_DEV}
WEIGHT_NAMES = ("pre_norm_g", "w_in", "ssm_lam_re", "ssm_lam_im", "ssm_log_step", "ssm_b_re", "ssm_b_im", "ssm_c_re",
                "ssm_c_im", "ssm_d", "ssm_w_glu", "ssm_b_glu", "attn_sinks", "w_out", "post_norm_g", "pl_w_proj",
                "pl_w_gate", "pl_b_gate")


TRANSPOSED = {"w_in": (0, 1), "ssm_b_re": (1, 2), "ssm_b_im": (1, 2)}


def _kernel_form(name, a):
    a = a[0]
    if name in TRANSPOSED:
        a = jnp.swapaxes(a, *TRANSPOSED[name])
    if name in ("ssm_b_re", "ssm_b_im", "ssm_c_re", "ssm_c_im"):
        a = a.reshape(SSM_G * SSM_P, SSM_N)
    return a


def _given_form(name, a, shape):
    if name in TRANSPOSED:
        i, j = TRANSPOSED[name]
        swapped = list(shape[1:])
        swapped[i], swapped[j] = swapped[j], swapped[i]
        return jnp.swapaxes(a.reshape(swapped), i, j).reshape(shape)
    return a.reshape(shape)


def _gathered_to_full(name, g):
    _, rows, cols = g.shape
    if name in COL_SHARDED:
        return jnp.swapaxes(g, 0, 1).reshape(rows, N_DEV * cols)
    return g.reshape(N_DEV * rows, cols)


def _full_to_owned(name, full):
    if name in COL_SHARDED:
        return jnp.swapaxes(full.reshape(full.shape[0], N_DEV, COL_SHARDED[name]), 0, 1)
    return full.reshape(N_DEV, full.shape[0] // N_DEV, full.shape[1])


def kernel(x, p, pre_norm_g, w_in, ssm_lam_re, ssm_lam_im, ssm_log_step, ssm_b_re, ssm_b_im, ssm_c_re, ssm_c_im, ssm_d, ssm_w_glu, ssm_b_glu, attn_sinks, w_out, post_norm_g, pl_w_proj, pl_w_gate, pl_b_gate, loss_target, m_pre_norm_g, m_w_in, m_ssm_lam_re, m_ssm_lam_im, m_ssm_log_step, m_ssm_b_re, m_ssm_b_im, m_ssm_c_re, m_ssm_c_im, m_ssm_d, m_ssm_w_glu, m_ssm_b_glu, m_attn_sinks, m_w_out, m_post_norm_g, m_pl_w_proj, m_pl_w_gate, m_pl_b_gate, v_pre_norm_g, v_w_in, v_ssm_lam_re, v_ssm_lam_im, v_ssm_log_step, v_ssm_b_re, v_ssm_b_im, v_ssm_c_re, v_ssm_c_im, v_ssm_d, v_ssm_w_glu, v_ssm_b_glu, v_attn_sinks, v_w_out, v_post_norm_g, v_pl_w_proj, v_pl_w_gate, v_pl_b_gate):
    w = dict(pre_norm_g=pre_norm_g, w_in=w_in, ssm_lam_re=ssm_lam_re, ssm_lam_im=ssm_lam_im, ssm_log_step=ssm_log_step,
             ssm_b_re=ssm_b_re, ssm_b_im=ssm_b_im, ssm_c_re=ssm_c_re, ssm_c_im=ssm_c_im, ssm_d=ssm_d, ssm_w_glu=ssm_w_glu,
             ssm_b_glu=ssm_b_glu, attn_sinks=attn_sinks, w_out=w_out, post_norm_g=post_norm_g, pl_w_proj=pl_w_proj,
             pl_w_gate=pl_w_gate, pl_b_gate=pl_b_gate)
    m = dict(pre_norm_g=m_pre_norm_g, w_in=m_w_in, ssm_lam_re=m_ssm_lam_re, ssm_lam_im=m_ssm_lam_im,
             ssm_log_step=m_ssm_log_step, ssm_b_re=m_ssm_b_re, ssm_b_im=m_ssm_b_im, ssm_c_re=m_ssm_c_re,
             ssm_c_im=m_ssm_c_im, ssm_d=m_ssm_d, ssm_w_glu=m_ssm_w_glu, ssm_b_glu=m_ssm_b_glu, attn_sinks=m_attn_sinks,
             w_out=m_w_out, post_norm_g=m_post_norm_g, pl_w_proj=m_pl_w_proj, pl_w_gate=m_pl_w_gate,
             pl_b_gate=m_pl_b_gate)
    v = dict(pre_norm_g=v_pre_norm_g, w_in=v_w_in, ssm_lam_re=v_ssm_lam_re, ssm_lam_im=v_ssm_lam_im,
             ssm_log_step=v_ssm_log_step, ssm_b_re=v_ssm_b_re, ssm_b_im=v_ssm_b_im, ssm_c_re=v_ssm_c_re,
             ssm_c_im=v_ssm_c_im, ssm_d=v_ssm_d, ssm_w_glu=v_ssm_w_glu, ssm_b_glu=v_ssm_b_glu, attn_sinks=v_attn_sinks,
             w_out=v_w_out, post_norm_g=v_post_norm_g, pl_w_proj=v_pl_w_proj, pl_w_gate=v_pl_w_gate,
             pl_b_gate=v_pl_b_gate)
    me = _slot(lax.axis_index("x"), lax.axis_index("y"), lax.axis_index("c"))
    kf = lambda d: {n: _kernel_form(n, a) for n, a in d.items()}
    wk, mk, vk = kf(w), kf(m), kf(v)

    (gathered,) = _allgather_weights([wk["w_in"]])
    loss, grad_x, grads, g_late = _local_step(
        x, p[0], loss_target, wk["pre_norm_g"], gathered.reshape(D_IN, D_MODEL), wk["ssm_lam_re"], wk["ssm_lam_im"],
        wk["ssm_log_step"], wk["ssm_b_re"], wk["ssm_b_im"], wk["ssm_c_re"], wk["ssm_c_im"], wk["ssm_d"],
        wk["ssm_b_glu"], wk["attn_sinks"], wk["post_norm_g"], wk["pl_b_gate"], [wk[n] for n in LATE_NAMES], me)

    owned = lambda g: g.reshape(N_DEV, D_IN // N_DEV, D_MODEL)
    tiny_form = lambda d: [d[n].reshape(rows, cols) for n, rows, cols in TINY]
    med_form = lambda d: [d[n].reshape(N_DEV, rows // N_DEV, cols) for n, rows, cols in MEDIUM]
    g_big, loss, g_tiny, g_med = _reduce_final(
        [owned(grads["w_in16"])], [lax.dynamic_index_in_dim(owned(grads["w_in"]), me, axis=0, keepdims=False)],
        loss, tiny_form(grads), med_form(grads))
    names = BIG_NAMES + tuple(n for n, _, _ in TINY + MEDIUM)
    form = lambda d: [d[n] for n in BIG_NAMES] + tiny_form(d) + med_form(d)
    updated = _adamw_update(g_big + g_late + g_tiny + g_med, form(wk), form(mk), form(vk))
    vals = dict(zip(names, updated))
    results = [[_given_form(n, vals[n][kind], w[n].shape) for n in WEIGHT_NAMES] for kind in range(4)]
    return (loss.reshape(()), grad_x, *results[0], *results[1], *results[2], *results[3])
```

```python
import functools
import math

import jax
import jax.numpy as jnp
from jax import lax
from jax.experimental import pallas as pl
from jax.experimental.pallas import tpu as pltpu

F32 = jnp.float32
BF16 = jnp.bfloat16

D_MODEL = 1024
D_SSM = 512
D_ATTN = 512
SSM_P = 16
SSM_G = 32
SSM_N = 64
N_HEADS = 8
KV_HEADS = 2
Q_PER_KV = 4
HEAD_DIM = 64
ATT_BLOCK = 128
D_PLE = 256
D_IN = 2304
EPS = 1e-6
N_DEV = 8
N_SEG = 8
G_TILE = 8
N_GT = SSM_G // G_TILE
CH_T = G_TILE * SSM_P
ST_T = G_TILE * SSM_N
N_STATE = SSM_G * SSM_N
SCAN_UNROLL = 4
MIX_GROUPS = 1
LANES = 128
VMEM_LIMIT = 60 * 1024 * 1024

ADAM_LR = 0.001
ADAM_B1 = 0.9
ADAM_B2 = 0.999
ADAM_EPS = 1e-08
ADAM_WD = 0.01
ADAM_STEP = 10

GELU_C = math.sqrt(2.0 / math.pi)
GELU_K = 0.044715
ATT_SCALE = 1.0 / math.sqrt(HEAD_DIM)
NEG_BIG = -1e30


def _mm(a, b):
    return jnp.dot(a.astype(BF16), b.astype(BF16), preferred_element_type=F32)


def _mm_nt(a, b):
    return lax.dot_general(a.astype(BF16), b.astype(BF16), (((1,), (1,)), ((), ())), preferred_element_type=F32)


def _mm_tn(a, b):
    return lax.dot_general(a.astype(BF16), b.astype(BF16), (((0,), (0,)), ((), ())), preferred_element_type=F32)


def _sigmoid(x):
    return 1.0 / (1.0 + jnp.exp(-x))


def _tc_params(sem):
    return pltpu.CompilerParams(dimension_semantics=sem, vmem_limit_bytes=VMEM_LIMIT)


def _const_spec(shape):
    nd = len(shape)
    return pl.BlockSpec(shape, lambda *_: (0,) * nd)


def _mesh_pos():
    return lax.axis_index("x"), lax.axis_index("y"), lax.axis_index("c")


ROW_CHUNKS = (64, 32, 16)


def _row_chunk(nrows):
    return next((c for c in ROW_CHUNKS if nrows % c == 0), None)


def _row_chunks(nrows, fn, chunk=None, init=None):
    chunk = chunk or _row_chunk(nrows)

    def step(i, carry):
        rows = pl.ds(pl.multiple_of(i * chunk, chunk), chunk)
        if init is None:
            fn(rows)
            return carry
        return fn(rows, carry)

    return lax.fori_loop(0, nrows // chunk, step, 0 if init is None else init)


def _slot(px, py, pc):
    return 4 * px + 2 * py + pc


def _allgather_weights(shards):
    n = len(shards)

    def body(*refs):
        srcs, outs, (send_sems, recv_sems) = refs[:n], refs[n:2 * n], refs[2 * n:]
        x, y, c = _mesh_pos()
        me, sibling = (x, y, c), (x, y, 1 - c)
        chips = [(1 - x, y), (x, 1 - y), (1 - x, 1 - y)]

        def copy(a, k, block, to):
            blk = outs[a].at[_slot(*block)]
            return pltpu.make_async_remote_copy(
                src_ref=blk, dst_ref=blk, send_sem=send_sems.at[7 * a + k], recv_sem=recv_sems.at[7 * a + k],
                device_id=to, device_id_type=pl.DeviceIdType.MESH)

        sends = []
        for a in range(n):
            mine = outs[a].at[_slot(*me)]

            def cast(r, mine=mine, src=srcs[a]):
                mine[r, :] = src[r, :].astype(BF16)

            _row_chunks(srcs[a].shape[0], cast)
            first = [copy(a, 0, me, sibling)] + [copy(a, 1 + j, me, (*chip, c)) for j, chip in enumerate(chips)]
            for cp in first:
                cp.start()
            sends += first
        for a in range(n):
            for j, chip in enumerate(chips):
                copy(a, 1 + j, (*chip, c), me).wait_recv()
                fwd = copy(a, 4 + j, (*chip, c), sibling)
                fwd.start()
                sends.append(fwd)
        for a in range(n):
            copy(a, 0, sibling, me).wait_recv()
            for j, chip in enumerate(chips):
                copy(a, 4 + j, (*chip, 1 - c), me).wait_recv()
        for cp in sends:
            cp.wait_send()

    vm = pl.BlockSpec(memory_space=pltpu.VMEM)
    return pl.pallas_call(
        body, name="allgather_weights",
        out_shape=tuple(jax.ShapeDtypeStruct((N_DEV,) + s.shape, BF16) for s in shards),
        in_specs=[vm] * n, out_specs=(vm,) * n,
        scratch_shapes=[pltpu.SemaphoreType.DMA((7 * n,)), pltpu.SemaphoreType.DMA((7 * n,))],
        compiler_params=pltpu.CompilerParams(vmem_limit_bytes=VMEM_LIMIT),
    )(*shards)


def _adamw(w, g, m, v):
    m = ADAM_B1 * m + (1.0 - ADAM_B1) * g
    v = ADAM_B2 * v + (1.0 - ADAM_B2) * (g * g)
    m_hat = m / (1.0 - ADAM_B1 ** ADAM_STEP)
    v_hat = v / (1.0 - ADAM_B2 ** ADAM_STEP)
    delta = -ADAM_LR * (m_hat / (jnp.sqrt(v_hat) + ADAM_EPS) + ADAM_WD * w)
    return delta, m, v


def _remote(src, dst, send_sems, recv_sems, k, to):
    return pltpu.make_async_remote_copy(src_ref=src, dst_ref=dst, send_sem=send_sems.at[k], recv_sem=recv_sems.at[k],
                                        device_id=to, device_id_type=pl.DeviceIdType.MESH)


def _big_reduce_phases(g16_r, go_r, outs, send2, recv1, recv2, s_send, s_recv):
    n = len(g16_r)
    x, y, c = _mesh_pos()
    sibling = (x, y, 1 - c)
    chips = [(1 - x, y), (x, 1 - y), (1 - x, 1 - y)]
    all_chips = [(x, y)] + chips
    lvl1 = []
    for a in range(n):
        cps = [_remote(g16_r[a].at[_slot(*chip, 1 - c)], recv1[a].at[j], s_send, s_recv, 7 * a + j, sibling)
               for j, chip in enumerate(all_chips)]
        for cp in cps:
            cp.start()
        lvl1.append(cps)
    yield
    lvl2 = []
    for a in range(n):
        for cp in lvl1[a]:
            cp.wait_recv()
        og = outs[a]

        def partials(r, a=a, og=og):
            og[r, :] = go_r[a][r, :] + recv1[a][0, r, :].astype(F32)
            for j, chip in enumerate(chips):
                mine16 = g16_r[a][_slot(*chip, c), r, :].astype(F32)
                send2[a][j, r, :] = (mine16 + recv1[a][1 + j, r, :].astype(F32)).astype(BF16)

        _row_chunks(go_r[a].shape[0], partials)
        cps = [_remote(send2[a].at[j], recv2[a].at[j], s_send, s_recv, 7 * a + 4 + j, (*chip, c))
               for j, chip in enumerate(chips)]
        for cp in cps:
            cp.start()
        lvl2.append(cps)
    yield
    for a in range(n):
        for cp in lvl2[a]:
            cp.wait_recv()
        og = outs[a]

        def total(r, a=a, og=og):
            g = og[r, :]
            for j in range(3):
                g = g + recv2[a][j, r, :].astype(F32)
            og[r, :] = g

        _row_chunks(go_r[a].shape[0], total)
    yield
    for cps in lvl1 + lvl2:
        for cp in cps:
            cp.wait_send()


def _adamw_update(g, w, m, v):
    n = len(g)

    def body(*refs):
        g_r, w_r, m_r, v_r = (refs[i * n:(i + 1) * n] for i in range(4))
        outs = refs[4 * n:]
        for a in range(n):
            og, od, om, ov = outs[4 * a:4 * a + 4]

            def update(idx, a=a, og=og, od=od, om=om, ov=ov):
                gv = g_r[a][idx]
                d, nm, nv = _adamw(w_r[a][idx], gv, m_r[a][idx], v_r[a][idx])
                og[idx] = gv
                od[idx] = d
                om[idx] = nm
                ov[idx] = nv

            shape = g_r[a].shape
            if len(shape) == 3:
                for b in range(shape[0]):
                    update(b)
            elif _row_chunk(shape[0]) is not None:
                _row_chunks(shape[0], update)
            else:
                update(Ellipsis)

    vm = pl.BlockSpec(memory_space=pltpu.VMEM)
    res = pl.pallas_call(
        body, name="adamw_update",
        out_shape=tuple(jax.ShapeDtypeStruct(t.shape, F32) for t in g for _ in range(4)),
        in_specs=[vm] * (4 * n), out_specs=(vm,) * (4 * n),
        compiler_params=pltpu.CompilerParams(vmem_limit_bytes=VMEM_LIMIT),
    )(*g, *w, *m, *v)
    return [res[4 * a:4 * a + 4] for a in range(n)]


TINY = (("pre_norm_g", 1, 1024), ("post_norm_g", 1, 1024), ("pl_b_gate", 1, 1024), ("ssm_d", 1, 512),
        ("ssm_b_glu", 1, 512), ("ssm_log_step", 1, 32), ("attn_sinks", 1, 8), ("ssm_lam_re", 32, 64),
        ("ssm_lam_im", 32, 64))
MEDIUM = (("ssm_b_re", SSM_G * SSM_P, SSM_N), ("ssm_b_im", SSM_G * SSM_P, SSM_N), ("ssm_c_re", SSM_G * SSM_P, SSM_N),
          ("ssm_c_im", SSM_G * SSM_P, SSM_N))


def _stage_rows():
    offs, r = {}, 0
    for name, rows, cols in TINY + (("loss", 1, 1),):
        if rows > 1:
            r = -(-r // 8) * 8
        offs[name] = r
        r += rows if rows > 1 else max(cols // LANES, 1)
    return offs, -(-r // 8) * 8


def _reduce_final(g16, g32, loss, g_tiny, g_med):
    nb_, nt, nm_ = len(g16), len(TINY), len(MEDIUM)
    offs, stage_rows = _stage_rows()

    def body(*refs):
        g16_r, go_r = refs[:nb_], refs[nb_:2 * nb_]
        base = 2 * nb_
        loss_r, gt, gm = refs[base], refs[base + 1:base + 1 + nt], refs[base + 1 + nt:base + 1 + nt + nm_]
        base += 1 + nt + nm_
        out_b = refs[base:base + nb_]
        base += nb_
        loss_o, out_t, out_m = refs[base], refs[base + 1:base + 1 + nt], refs[base + 1 + nt:base + 1 + nt + nm_]
        base += 1 + nt + nm_
        send2_b, recv1_b, recv2_b = (refs[base + i * nb_:base + (i + 1) * nb_] for i in range(3))
        base += 3 * nb_
        stage = refs[base]
        recv1, part, recv2 = (refs[base + 1 + i * nm_:base + 1 + (i + 1) * nm_] for i in range(3))
        bs_send, bs_recv, s_send, s_recv, own_sems = refs[base + 1 + 3 * nm_:base + 6 + 3 * nm_]
        own32 = refs[base + 6 + 3 * nm_:]
        me = _slot(*_mesh_pos())
        fetch = [pltpu.make_async_copy(go_r[a].at[me], own32[a], own_sems.at[a]) for a in range(nb_)]
        for cp in fetch:
            cp.start()
        big = _big_reduce_phases(g16_r, own32, out_b, send2_b, recv1_b, recv2_b, bs_send, bs_recv)
        small = small_phases(loss_r, gt, gm, loss_o, out_t, out_m, stage, recv1, part, recv2, s_send, s_recv)
        next(big)
        next(small)
        for cp in fetch:
            cp.wait()
        next(big)
        for _ in small:
            pass
        for _ in big:
            pass

    def small_phases(loss_r, gt, gm, loss_o, out_t, out_m, stage, recv1, part, recv2, s_send, s_recv):
        x, y, c = _mesh_pos()
        me = _slot(x, y, c)
        sibling = (x, y, 1 - c)
        chips = [(1 - x, y), (x, 1 - y), (1 - x, 1 - y)]
        all_chips = [(x, y)] + chips
        peers = [sibling] + [(*chip, c) for chip in chips] + [(*chip, 1 - c) for chip in chips]
        sem = iter(range(7 + 14 * nm_))
        lvl1 = []
        for a in range(nm_):
            cps = [_remote(gm[a].at[_slot(*chip, 1 - c)], recv1[a].at[j], s_send, s_recv, next(sem), sibling)
                   for j, chip in enumerate(all_chips)]
            for cp in cps:
                cp.start()
            lvl1.append(cps)
        mine = stage.at[me]
        mine[...] = jnp.zeros((stage_rows, LANES), F32)
        for (name, rows, cols), ref in zip(TINY + (("loss", 1, 1),), gt + (loss_r,)):
            r0 = offs[name]
            if rows > 1:
                mine[r0:r0 + rows, 0:cols] = ref[...]
            elif cols >= LANES:
                for i in range(cols // LANES):
                    mine[r0 + i:r0 + i + 1, :] = ref[:, i * LANES:(i + 1) * LANES]
            else:
                mine[r0:r0 + 1, 0:cols] = ref[...]
        tiny_cps = [_remote(mine, mine, s_send, s_recv, next(sem), peer) for peer in peers]
        for cp in tiny_cps:
            cp.start()
        yield
        lvl2 = []
        for a in range(nm_):
            for cp in lvl1[a]:
                cp.wait_recv()
            for j, chip in enumerate(all_chips):
                part[a][j] = gm[a][_slot(*chip, c)] + recv1[a][j]
            cps = [_remote(part[a].at[1 + j], recv2[a].at[j], s_send, s_recv, next(sem), (*chip, c))
                   for j, chip in enumerate(chips)]
            for cp in cps:
                cp.start()
            lvl2.append(cps)
        yield
        lvl3 = []
        for a in range(nm_):
            for cp in lvl2[a]:
                cp.wait_recv()
            blk = out_m[a].at[me]
            blk[...] = ((part[a][0] + recv2[a][0]) + recv2[a][1]) + recv2[a][2]
            cps = [_remote(blk, blk, s_send, s_recv, next(sem), peer) for peer in peers]
            for cp in cps:
                cp.start()
            lvl3.append(cps)
        yield
        for cp in tiny_cps:
            cp.wait_recv()
        tot = stage[0]
        for d in range(1, N_DEV):
            tot = tot + stage[d]
        loss_o[...] = tot[offs["loss"]:offs["loss"] + 1, 0:1]
        for k, (name, rows, cols) in enumerate(TINY):
            r0 = offs[name]
            if rows > 1:
                out_t[k][...] = tot[r0:r0 + rows, 0:cols]
            elif cols >= LANES:
                for i in range(cols // LANES):
                    out_t[k][:, i * LANES:(i + 1) * LANES] = tot[r0 + i:r0 + i + 1, :]
            else:
                out_t[k][...] = tot[r0:r0 + 1, 0:cols]
        for cps in lvl3:
            for cp in cps:
                cp.wait_recv()
        for cps in lvl1 + lvl2 + lvl3 + [tiny_cps]:
            for cp in cps:
                cp.wait_send()

    vmem = pl.BlockSpec(memory_space=pltpu.VMEM)
    t_shapes = [jax.ShapeDtypeStruct((rows, cols), F32) for _, rows, cols in TINY]
    m_shapes = [jax.ShapeDtypeStruct((N_DEV, rows // N_DEV, cols), F32) for _, rows, cols in MEDIUM]
    blk = [(rows // N_DEV, cols) for _, rows, cols in MEDIUM]
    shard = [g.shape[1:] for g in g16]
    scratch = ([pltpu.VMEM((3,) + s, BF16) for s in shard] + [pltpu.VMEM((4,) + s, BF16) for s in shard]
               + [pltpu.VMEM((3,) + s, BF16) for s in shard]
               + [pltpu.VMEM((N_DEV, stage_rows, LANES), F32)]
               + [pltpu.VMEM((4,) + b, F32) for b in blk] + [pltpu.VMEM((4,) + b, F32) for b in blk]
               + [pltpu.VMEM((3,) + b, F32) for b in blk]
               + [pltpu.SemaphoreType.DMA((7 * nb_,)), pltpu.SemaphoreType.DMA((7 * nb_,)),
                  pltpu.SemaphoreType.DMA((7 + 14 * nm_,)), pltpu.SemaphoreType.DMA((7 + 14 * nm_,)),
                  pltpu.SemaphoreType.DMA((nb_,))]
               + [pltpu.VMEM(s, F32) for s in shard])
    n_out = nb_ + 1 + nt + nm_
    res = pl.pallas_call(
        body, name="reduce_final",
        out_shape=tuple(jax.ShapeDtypeStruct(s, F32) for s in shard) + (jax.ShapeDtypeStruct((1, 1), F32),)
        + tuple(t_shapes) + tuple(m_shapes),
        in_specs=[vmem] * nb_ + [pl.BlockSpec(memory_space=pl.ANY)] * nb_ + [vmem] * (1 + nt + nm_),
        out_specs=(vmem,) * n_out, scratch_shapes=scratch,
        compiler_params=pltpu.CompilerParams(vmem_limit_bytes=VMEM_LIMIT),
    )(*g16, *g32, loss, *g_tiny, *g_med)
    return list(res[:nb_]), res[nb_], list(res[nb_ + 1:nb_ + 1 + nt]), list(res[nb_ + 1 + nt:])


def _gather_phases(shard_r, gath, cast, send_sems, recv_sems, local_sems):
    n = len(shard_r)
    x, y, c = _mesh_pos()
    me, sibling = (x, y, c), (x, y, 1 - c)
    chips = [(1 - x, y), (x, 1 - y), (1 - x, 1 - y)]

    def own(a, k, to):
        return _remote(cast[a], gath[a].at[_slot(*me)], send_sems, recv_sems, 7 * a + k, to)

    def passed(a, k, block, to):
        blk = gath[a].at[_slot(*block)]
        return _remote(blk, blk, send_sems, recv_sems, 7 * a + k, to)

    def keep(a):
        return pltpu.make_async_copy(cast[a], gath[a].at[_slot(*me)], local_sems.at[a])

    def start():
        for a in range(n):
            def to16(r, a=a):
                cast[a][r, :] = shard_r[a][r, :].astype(BF16)

            _row_chunks(shard_r[a].shape[0], to16)
            keep(a).start()
            own(a, 0, sibling).start()
            for j, chip in enumerate(chips):
                own(a, 1 + j, (*chip, c)).start()

    def relay():
        for a in range(n):
            for j, chip in enumerate(chips):
                passed(a, 1 + j, (*chip, c), me).wait_recv()
                passed(a, 4 + j, (*chip, c), sibling).start()

    def finish():
        for a in range(n):
            passed(a, 0, sibling, me).wait_recv()
            for j, chip in enumerate(chips):
                passed(a, 4 + j, (*chip, 1 - c), me).wait_recv()
            own(a, 0, sibling).wait_send()
            for j, chip in enumerate(chips):
                own(a, 1 + j, (*chip, c)).wait_send()
                passed(a, 4 + j, (*chip, c), sibling).wait_send()
            keep(a).wait()

    return start, relay, finish


def _gather_operands(shards):
    n = len(shards)
    return ((pl.BlockSpec(memory_space=pl.ANY),) * n,
            tuple(jax.ShapeDtypeStruct((N_DEV,) + s.shape, BF16) for s in shards),
            [pltpu.VMEM(s.shape, BF16) for s in shards]
            + [pltpu.SemaphoreType.DMA((7 * n,)), pltpu.SemaphoreType.DMA((7 * n,)), pltpu.SemaphoreType.DMA((n,))])


def _in_proj(x2, g_pre, w_in, late, bl, seg):
    t = x2.shape[0]
    tm = seg
    steps = t // tm
    n = len(late)
    forward_step, last_step = (steps * 5) // 8, steps - 1

    def body(*refs):
        x_ref, g_ref, w_ref = refs[:3]
        late_r = refs[3:3 + n]
        u_ref, zs_ref, q_ref, k_ref, v_ref, za_ref = refs[3 + n:9 + n]
        gath = refs[9 + n:9 + 2 * n]
        cast = refs[9 + 2 * n:9 + 3 * n]
        send_sems, recv_sems, local_sems = refs[9 + 3 * n:]
        i = pl.program_id(0)
        start, relay, finish = _gather_phases(late_r, gath, cast, send_sems, recv_sems, local_sems)
        pl.when(i == 0)(start)
        xv = x_ref[...]
        r = lax.rsqrt(jnp.mean(xv * xv, axis=-1, keepdims=True) + EPS)
        hn = xv * r * g_ref[...]
        proj = _mm_nt(hn, w_ref[...])
        u_ref[0] = proj[:, 0:512]
        zs_ref[...] = proj[:, 512:1024]
        q_ref[...] = proj[:, 1024:1536].astype(BF16)
        k_ref[...] = proj[:, 1536:1664].astype(BF16)
        v_ref[...] = proj[:, 1664:1792].astype(BF16)
        za_ref[...] = proj[:, 1792:2304]
        pl.when(i == forward_step)(relay)
        pl.when(i == last_step)(finish)

    row = lambda w: pl.BlockSpec((tm, w), lambda i: (i, 0))
    g_specs, g_shapes, g_scratch = _gather_operands(late)
    res = pl.pallas_call(
        body, name="in_proj", grid=(steps,),
        in_specs=[row(D_MODEL), _const_spec((1, D_MODEL)), _const_spec((D_IN, D_MODEL))]
        + [_const_spec(s.shape) for s in late],
        out_specs=(pl.BlockSpec((1, tm, D_SSM), lambda i: (i // N_SEG, 0, i % N_SEG)),
                   row(512), row(512), row(128), row(128), row(512)) + g_specs,
        out_shape=(jax.ShapeDtypeStruct((bl, seg, N_SEG * D_SSM), F32),
                   jax.ShapeDtypeStruct((t, 512), F32), jax.ShapeDtypeStruct((t, 512), BF16),
                   jax.ShapeDtypeStruct((t, 128), BF16), jax.ShapeDtypeStruct((t, 128), BF16),
                   jax.ShapeDtypeStruct((t, 512), F32)) + g_shapes,
        scratch_shapes=g_scratch,
        compiler_params=_tc_params(("arbitrary",)),
    )(x2, g_pre, w_in, *late)
    return res[:6], list(res[6:])


def _discretise(lr, li, ls):
    step = jnp.exp(ls)
    mag = jnp.exp(lr * step)
    ar = mag * jnp.cos(li * step)
    ai = mag * jnp.sin(li * step)
    den = lr * lr + li * li
    cr = ((ar - 1.0) * lr + ai * li) / den
    ci = (ai * lr - (ar - 1.0) * li) / den
    return step, ar, ai, den, cr, ci


def _per_channel(v):
    return jnp.broadcast_to(v[:, None, :], (SSM_G, SSM_P, SSM_N)).reshape(SSM_G * SSM_P, SSM_N)


def _ssm_prep(lam_re, lam_im, log_step, b_re, b_im, seg):
    def body(lr_ref, li_ref, ls_ref, br_ref, bi_ref, lrr_ref, lir_ref, lsr_ref,
             ar_ref, ai_ref, bbr_ref, bbi_ref, pr_ref, pi_ref):
        _, _, _, _, cr, ci = _discretise(lr_ref[...], li_ref[...], ls_ref[...])
        cr, ci = _per_channel(cr), _per_channel(ci)
        br, bi = br_ref[...], bi_ref[...]
        bbr_ref[...] = cr * br - ci * bi
        bbi_ref[...] = cr * bi + ci * br
        stepr = jnp.exp(lsr_ref[...])
        k = (lax.broadcasted_iota(jnp.int32, (8, N_STATE), 0) + 1).astype(F32)
        magk = jnp.exp(k * (lrr_ref[...] * stepr))
        ang = k * (lir_ref[...] * stepr)
        pr_ref[0:8, :] = magk * jnp.cos(ang)
        pi_ref[0:8, :] = magk * jnp.sin(ang)
        n = 8
        while n < seg:
            tr, ti = pr_ref[n - 1:n, :], pi_ref[n - 1:n, :]
            xr, xi = pr_ref[0:n, :], pi_ref[0:n, :]
            pr_ref[n:2 * n, :] = xr * tr - xi * ti
            pi_ref[n:2 * n, :] = xr * ti + xi * tr
            n *= 2
        ar_ref[...] = pr_ref[0:1, :]
        ai_ref[...] = pi_ref[0:1, :]

    row = jax.ShapeDtypeStruct((1, N_STATE), F32)
    mat = jax.ShapeDtypeStruct((SSM_G * SSM_P, SSM_N), F32)
    pw = jax.ShapeDtypeStruct((seg, N_STATE), F32)
    vm = pl.BlockSpec(memory_space=pltpu.VMEM)
    step_row = jnp.broadcast_to(log_step, (SSM_G, SSM_N)).reshape(1, N_STATE)
    return pl.pallas_call(
        body, name="ssm_prep", out_shape=(row, row, mat, mat, pw, pw),
        in_specs=[vm] * 8, out_specs=(vm,) * 6,
    )(lam_re, lam_im, log_step, b_re, b_im, lam_re.reshape(1, N_STATE), lam_im.reshape(1, N_STATE), step_row)


def _seg_rows(t):
    if isinstance(t, int):
        return pl.ds(t * N_SEG, N_SEG)
    return pl.ds(pl.multiple_of(t * N_SEG, N_SEG), N_SEG)


def _scan_forward(xs, a_re, a_im, pw_re, pw_im, cs, seg):
    are = jnp.broadcast_to(a_re, (N_SEG, ST_T))
    aim = jnp.broadcast_to(a_im, (N_SEG, ST_T))

    def steps(k, carry):
        xr, xi = carry
        for j in range(SCAN_UNROLL):
            r = pl.multiple_of((k * SCAN_UNROLL + j) * N_SEG, N_SEG)
            nr = are * xr - aim * xi + xs[pl.ds(r, N_SEG), 0:ST_T]
            ni = are * xi + aim * xr + xs[pl.ds(r, N_SEG), ST_T:2 * ST_T]
            xs[pl.ds(r, N_SEG), 0:ST_T] = nr
            xs[pl.ds(r, N_SEG), ST_T:2 * ST_T] = ni
            xr, xi = nr, ni
        return xr, xi

    zero = jnp.zeros((N_SEG, ST_T), F32)
    fr, fi = lax.fori_loop(0, seg // SCAN_UNROLL, steps, (zero, zero))
    sr, si = pw_re[seg - 1:seg, :], pw_im[seg - 1:seg, :]
    cr = jnp.zeros((1, ST_T), F32)
    ci = jnp.zeros((1, ST_T), F32)
    cs[0:1, :] = cr
    cs[8:9, :] = ci
    for s in range(1, N_SEG):
        ncr = sr * cr - si * ci + fr[s - 1:s, :]
        nci = sr * ci + si * cr + fi[s - 1:s, :]
        cr, ci = ncr, nci
        cs[s:s + 1, :] = cr
        cs[8 + s:9 + s, :] = ci
    car, cai = cs[0:8, :], cs[8:16, :]

    def fix(t, _):
        r = pl.multiple_of(t * N_SEG, N_SEG)
        pr, pi = pw_re[pl.ds(t, 1), :], pw_im[pl.ds(t, 1), :]
        xs[pl.ds(r, N_SEG), 0:ST_T] = xs[pl.ds(r, N_SEG), 0:ST_T] + (pr * car - pi * cai)
        xs[pl.ds(r, N_SEG), ST_T:2 * ST_T] = xs[pl.ds(r, N_SEG), ST_T:2 * ST_T] + (pr * cai + pi * car)
        return 0

    lax.fori_loop(0, seg, fix, 0, unroll=SCAN_UNROLL)


def _ssm_forward(u_perm, bcat, ccat, a_re, a_im, pw_re, pw_im, d_row, seg):
    bl, rows, _ = u_perm.shape

    def body(u_ref, b_ref, c_ref, ar_ref, ai_ref, pr_ref, pi_ref, d_ref, y_ref, xs_ref, cs_ref):
        u = u_ref[0]
        xs, cs = xs_ref.at[0, 0], cs_ref.at[0, 0]
        xs[...] = _mm(u, b_ref[0])
        _scan_forward(xs, ar_ref[...], ai_ref[...], pr_ref, pi_ref, cs, seg)
        y_ref[0] = _mm(xs[...], c_ref[0]) + d_ref[...] * u

    state = lambda r, c: pl.BlockSpec((1, 1, r, c), lambda b, j: (b, j, 0, 0))
    return pl.pallas_call(
        body, name="ssm_forward", grid=(bl, N_GT),
        in_specs=[pl.BlockSpec((1, rows, CH_T), lambda b, j: (b, 0, j)),
                  pl.BlockSpec((1, CH_T, 2 * ST_T), lambda b, j: (j, 0, 0)),
                  pl.BlockSpec((1, 2 * ST_T, CH_T), lambda b, j: (j, 0, 0)),
                  pl.BlockSpec((1, ST_T), lambda b, j: (0, j)), pl.BlockSpec((1, ST_T), lambda b, j: (0, j)),
                  pl.BlockSpec((seg, ST_T), lambda b, j: (0, j)), pl.BlockSpec((seg, ST_T), lambda b, j: (0, j)),
                  pl.BlockSpec((1, CH_T), lambda b, j: (0, j))],
        out_specs=(pl.BlockSpec((1, rows, CH_T), lambda b, j: (b, 0, j)), state(rows, 2 * ST_T), state(16, ST_T)),
        out_shape=(jax.ShapeDtypeStruct((bl, rows, D_SSM), F32),
                   jax.ShapeDtypeStruct((bl, N_GT, rows, 2 * ST_T), F32),
                   jax.ShapeDtypeStruct((bl, N_GT, 16, ST_T), F32)),
        compiler_params=_tc_params(("arbitrary", "arbitrary")),
    )(u_perm, bcat, ccat, a_re, a_im, pw_re, pw_im, d_row)


def _ssm_backward(u_perm, dy_perm, states, carries, bcat_t, ccat_t, a_re, a_im, pw_re, pw_im, d_row, seg):
    bl, rows, _ = u_perm.shape

    def body(u_ref, dy_ref, xs_ref, cs_ref, bt_ref, ct_ref, ar_ref, ai_ref, pr_ref, pi_ref, d_ref,
             du_ref, db_ref, dc_ref, dar_ref, dai_ref, dd_ref, ls, cl):
        b = pl.program_id(1)
        u = u_ref[0]
        dy = dy_ref[0]
        xs, cs = xs_ref.at[0, 0], cs_ref.at[0, 0]
        ls[...] = _mm(dy, ct_ref[0])
        are = jnp.broadcast_to(ar_ref[...], (N_SEG, ST_T))
        aim = jnp.broadcast_to(ai_ref[...], (N_SEG, ST_T))

        def steps(k, carry):
            lr, li = carry
            for j in range(SCAN_UNROLL):
                r = pl.multiple_of((seg - 1 - (k * SCAN_UNROLL + j)) * N_SEG, N_SEG)
                nr = are * lr + aim * li + ls[pl.ds(r, N_SEG), 0:ST_T]
                ni = are * li - aim * lr + ls[pl.ds(r, N_SEG), ST_T:2 * ST_T]
                ls[pl.ds(r, N_SEG), 0:ST_T] = nr
                ls[pl.ds(r, N_SEG), ST_T:2 * ST_T] = ni
                lr, li = nr, ni
            return lr, li

        zero = jnp.zeros((N_SEG, ST_T), F32)
        fr, fi = lax.fori_loop(0, seg // SCAN_UNROLL, steps, (zero, zero))
        sr, si = pr_ref[seg - 1:seg, :], pi_ref[seg - 1:seg, :]
        cr = jnp.zeros((1, ST_T), F32)
        ci = jnp.zeros((1, ST_T), F32)
        cl[7:8, :] = cr
        cl[15:16, :] = ci
        for s in range(N_SEG - 2, -1, -1):
            ncr = sr * cr + si * ci + fr[s + 1:s + 2, :]
            nci = sr * ci - si * cr + fi[s + 1:s + 2, :]
            cr, ci = ncr, nci
            cl[s:s + 1, :] = cr
            cl[8 + s:9 + s, :] = ci
        clr, cli = cl[0:8, :], cl[8:16, :]

        def fix_rows(rows, t, xpr, xpi, acc):
            dr, di = acc
            pr, pi = pr_ref[pl.ds(seg - 1 - t, 1), :], pi_ref[pl.ds(seg - 1 - t, 1), :]
            lr = ls[rows, 0:ST_T] + (pr * clr + pi * cli)
            li = ls[rows, ST_T:2 * ST_T] + (pr * cli - pi * clr)
            ls[rows, 0:ST_T] = lr
            ls[rows, ST_T:2 * ST_T] = li
            return dr + (lr * xpr + li * xpi), di + (li * xpr - lr * xpi)

        def fix_at(t, acc):
            prev = _seg_rows(t - 1)
            return fix_rows(_seg_rows(t), t, xs[prev, 0:ST_T], xs[prev, ST_T:2 * ST_T], acc)

        def fix(k, acc):
            for j in range(SCAN_UNROLL):
                acc = fix_at(k * SCAN_UNROLL + j, acc)
            return acc

        acc = fix_rows(pl.ds(0, N_SEG), 0, cs[0:8, :], cs[8:16, :], (zero, zero))
        for t in range(1, SCAN_UNROLL):
            acc = fix_at(t, acc)
        dr, di = lax.fori_loop(1, seg // SCAN_UNROLL, fix, acc)
        dar = jnp.sum(dr, axis=0, keepdims=True)
        dai = jnp.sum(di, axis=0, keepdims=True)
        lall = ls[...]
        du_ref[0] = (_mm(lall, bt_ref[0]) + d_ref[...] * dy).astype(BF16)
        dbp = _mm_tn(u, lall)
        dcp = _mm_tn(dy, xs[...])
        ddp = jnp.sum(dy * u, axis=0, keepdims=True)

        @pl.when(b == 0)
        def _():
            db_ref[0] = dbp
            dc_ref[0] = dcp
            dar_ref[...] = dar
            dai_ref[...] = dai
            dd_ref[...] = ddp

        @pl.when(b != 0)
        def _():
            db_ref[0] += dbp
            dc_ref[0] += dcp
            dar_ref[...] += dar
            dai_ref[...] += dai
            dd_ref[...] += ddp

    tile3 = lambda r, c: pl.BlockSpec((1, r, c), lambda j, b: (j, 0, 0))
    lane = lambda r, c: pl.BlockSpec((r, c), lambda j, b: (0, j))
    act = pl.BlockSpec((1, rows, CH_T), lambda j, b: (b, 0, j))
    state = lambda r, c: pl.BlockSpec((1, 1, r, c), lambda j, b: (b, j, 0, 0))
    return pl.pallas_call(
        body, name="ssm_backward", grid=(N_GT, bl),
        in_specs=[act, act, state(rows, 2 * ST_T), state(16, ST_T), tile3(2 * ST_T, CH_T), tile3(CH_T, 2 * ST_T),
                  lane(1, ST_T), lane(1, ST_T), lane(seg, ST_T), lane(seg, ST_T), lane(1, CH_T)],
        out_specs=(act, tile3(CH_T, 2 * ST_T), tile3(CH_T, 2 * ST_T), lane(1, ST_T), lane(1, ST_T), lane(1, CH_T)),
        out_shape=(jax.ShapeDtypeStruct((bl, rows, D_SSM), BF16),
                   jax.ShapeDtypeStruct((N_GT, CH_T, 2 * ST_T), F32), jax.ShapeDtypeStruct((N_GT, CH_T, 2 * ST_T), F32),
                   jax.ShapeDtypeStruct((1, N_STATE), F32), jax.ShapeDtypeStruct((1, N_STATE), F32),
                   jax.ShapeDtypeStruct((1, D_SSM), F32)),
        scratch_shapes=[pltpu.VMEM((rows, 2 * ST_T), F32), pltpu.VMEM((16, ST_T), F32)],
        compiler_params=_tc_params(("arbitrary", "arbitrary")),
    )(u_perm, dy_perm, states, carries, bcat_t, ccat_t, a_re, a_im, pw_re, pw_im, d_row)


def _ssm_param_grads(lam_re, lam_im, log_step, b_re, b_im, da_re, da_im, dbb_re, dbb_im):
    def body(lr_ref, li_ref, ls_ref, br_ref, bi_ref, gar_ref, gai_ref, gbr_ref, gbi_ref,
             dlr_ref, dli_ref, dls_ref, dbr_ref, dbi_ref):
        lr, li = lr_ref[...], li_ref[...]
        step, ar, ai, den, cr, ci = _discretise(lr, li, ls_ref[...])
        crb, cib = _per_channel(cr), _per_channel(ci)
        br, bi = br_ref[...], bi_ref[...]
        gbr, gbi = gbr_ref[...], gbi_ref[...]
        dbr_ref[...] = crb * gbr + cib * gbi
        dbi_ref[...] = crb * gbi - cib * gbr
        over_channels = lambda t: jnp.sum(t.reshape(SSM_G, SSM_P, SSM_N), axis=1)
        gcr = over_channels(br * gbr + bi * gbi)
        gci = over_channels(br * gbi - bi * gbr)
        ilr, ili = lr / den, -li / den
        gar = gar_ref[...] + (ilr * gcr + ili * gci)
        gai = gai_ref[...] + (ilr * gci - ili * gcr)
        qr, qi = cr * ilr - ci * ili, cr * ili + ci * ilr
        glr = -(qr * gcr + qi * gci)
        gli = -(qr * gci - qi * gcr)
        gwr = ar * gar + ai * gai
        gwi = ar * gai - ai * gar
        dlr_ref[...] = glr + step * gwr
        dli_ref[...] = gli + step * gwi
        dls_ref[...] = jnp.sum(lr * gwr + li * gwi, axis=-1, keepdims=True) * step

    lam = jax.ShapeDtypeStruct((SSM_G, SSM_N), F32)
    mat = jax.ShapeDtypeStruct((SSM_G * SSM_P, SSM_N), F32)
    vm = pl.BlockSpec(memory_space=pltpu.VMEM)
    return pl.pallas_call(
        body, name="ssm_param_grads", out_shape=(lam, lam, jax.ShapeDtypeStruct((SSM_G, 1), F32), mat, mat),
        in_specs=[vm] * 9, out_specs=(vm,) * 5,
    )(lam_re, lam_im, log_step, b_re, b_im, da_re, da_im, dbb_re, dbb_im)


ROWS4 = Q_PER_KV * ATT_BLOCK


def _att_dist_mask(first_block):
    qi = lax.broadcasted_iota(jnp.int32, (ROWS4, 2 * ATT_BLOCK), 0) & (ATT_BLOCK - 1)
    si = lax.broadcasted_iota(jnp.int32, (ROWS4, 2 * ATT_BLOCK), 1)
    dist = qi + ATT_BLOCK - si
    valid = (dist >= 0) & (dist < ATT_BLOCK) & ((si >= ATT_BLOCK) | jnp.logical_not(first_block))
    return dist.astype(F32), valid


def _stack_heads(x, kv):
    return jnp.concatenate([x[:, (kv * Q_PER_KV + g) * HEAD_DIM:(kv * Q_PER_KV + g + 1) * HEAD_DIM]
                            for g in range(Q_PER_KV)], axis=0)


def _stack_cols(x, kv):
    return jnp.concatenate([x[:, kv * Q_PER_KV + g:kv * Q_PER_KV + g + 1] for g in range(Q_PER_KV)], axis=0)


def _per_head_col(vals):
    return jnp.concatenate([jnp.full((ATT_BLOCK, 1), v, F32) for v in vals], axis=0)


def _attn_forward(q, k, v, sinks, late, bl, nb):
    t = q.shape[0]
    n = len(late)
    steps = bl * nb

    def body(*refs):
        sink_ref, q_ref, kp_ref, kc_ref, vp_ref, vc_ref = refs[:6]
        late_r = refs[6:6 + n]
        o_ref, lse_ref = refs[6 + n:8 + n]
        gath = refs[8 + n:8 + 2 * n]
        cast = refs[8 + 2 * n:8 + 3 * n]
        send_sems, recv_sems, local_sems = refs[8 + 3 * n:]
        i = pl.program_id(1)
        step = pl.program_id(0) * nb + i
        start, relay, finish = _gather_phases(late_r, gath, cast, send_sems, recv_sems, local_sems)
        pl.when(step == 0)(start)
        pl.when(step == (steps * 5) // 8)(relay)
        pl.when(step == steps - 1)(finish)
        dist4, valid4 = _att_dist_mask(i == 0)
        dist, valid = dist4[0:ATT_BLOCK, :], valid4[0:ATT_BLOCK, :]
        kk = jnp.concatenate([kp_ref[...], kc_ref[...]], axis=0)
        vv = jnp.concatenate([vp_ref[...], vc_ref[...]], axis=0)
        qv = q_ref[...]
        for h in range(N_HEADS):
            kv = h // Q_PER_KV
            slope = 2.0 ** (-(h + 1))
            qh = qv[:, h * HEAD_DIM:(h + 1) * HEAD_DIM]
            kh = kk[:, kv * HEAD_DIM:(kv + 1) * HEAD_DIM]
            vh = vv[:, kv * HEAD_DIM:(kv + 1) * HEAD_DIM]
            s = _mm_nt(qh, kh) * ATT_SCALE - slope * dist
            s = jnp.where(valid, s, NEG_BIG)
            sink = sink_ref[h]
            m = jnp.maximum(jnp.max(s, axis=-1, keepdims=True), sink)
            e = jnp.exp(s - m)
            den = jnp.sum(e, axis=-1, keepdims=True) + jnp.exp(sink - m)
            o_ref[:, h * HEAD_DIM:(h + 1) * HEAD_DIM] = _mm(e, vh) * (1.0 / den)
            lse_ref[:, h:h + 1] = m + jnp.log(den)

    cur = lambda w: pl.BlockSpec((ATT_BLOCK, w), lambda b, i: (b * nb + i, 0))
    prev = lambda w: pl.BlockSpec((ATT_BLOCK, w), lambda b, i: (b * nb + jnp.maximum(i - 1, 0), 0))
    g_specs, g_shapes, g_scratch = _gather_operands(late)
    res = pl.pallas_call(
        body, name="attn_forward", grid=(bl, nb),
        in_specs=[pl.BlockSpec(memory_space=pltpu.SMEM), cur(512), prev(128), cur(128), prev(128), cur(128)]
        + [pl.BlockSpec(s.shape, lambda b, i: (0, 0)) for s in late],
        out_specs=(cur(512), cur(N_HEADS)) + g_specs,
        out_shape=(jax.ShapeDtypeStruct((t, D_ATTN), F32), jax.ShapeDtypeStruct((t, N_HEADS), F32)) + g_shapes,
        scratch_shapes=g_scratch,
        compiler_params=_tc_params(("arbitrary", "arbitrary")),
    )(sinks, q, k, k, v, v, *late)
    return res[:2], list(res[2:])


def _attn_backward(q, k, v, o, do, lse, sinks, late16, late32, bl, nb):
    t = q.shape[0]
    n = len(late16)
    steps = bl * (nb + 1)
    mid1, mid2, last = steps // 4, (steps * 3) // 4, steps - 1

    def body(*refs):
        sink_ref, qc_ref, kp_ref, kc_ref, vp_ref, vc_ref, oc_ref, doc_ref, lc_ref = refs[:9]
        g16_r, g32_r = refs[9:9 + n], refs[9 + n:9 + 2 * n]
        dq_ref, dk_ref, dv_ref, ds_ref = refs[9 + 2 * n:13 + 2 * n]
        red = refs[13 + 2 * n:13 + 3 * n]
        own16, recv1, send2, recv2, go_r = (refs[13 + 3 * n + k * n:13 + 3 * n + (k + 1) * n] for k in range(5))
        s_send, s_recv, s_local, dk_carry, dv_carry = refs[13 + 8 * n:]
        b, i = pl.program_id(0), pl.program_id(1)
        step = b * (nb + 1) + i
        x, y, c = _mesh_pos()
        sibling = (x, y, 1 - c)
        chips = [(1 - x, y), (x, 1 - y), (1 - x, 1 - y)]
        all_chips = [(x, y)] + chips

        def lvl1(a, j):
            return _remote(g16_r[a].at[_slot(*all_chips[j], 1 - c)], recv1[a].at[j], s_send, s_recv, 7 * a + j, sibling)

        def lvl2(a, j):
            return _remote(send2[a].at[j], recv2[a].at[j], s_send, s_recv, 7 * a + 4 + j, (*chips[j], c))

        def mine(a, j):
            if j == 3:
                return pltpu.make_async_copy(g32_r[a].at[_slot(x, y, c)], go_r[a], s_local.at[4 * a + j])
            return pltpu.make_async_copy(g16_r[a].at[_slot(*chips[j], c)], own16[a].at[j], s_local.at[4 * a + j])

        @pl.when(step == 0)
        def _():
            for a in range(n):
                for j in range(4):
                    mine(a, j).start()
                for j in range(4):
                    lvl1(a, j).start()

        @pl.when(step == mid1)
        def _():
            for a in range(n):
                for j in range(4):
                    mine(a, j).wait()
                for j in range(4):
                    lvl1(a, j).wait_recv()

                def partials(r, a=a):
                    red[a][r, :] = go_r[a][r, :] + recv1[a][0, r, :].astype(F32)
                    for j in range(3):
                        send2[a][j, r, :] = (own16[a][j, r, :].astype(F32)
                                             + recv1[a][1 + j, r, :].astype(F32)).astype(BF16)

                _row_chunks(go_r[a].shape[0], partials)
                for j in range(3):
                    lvl2(a, j).start()

        @pl.when(step == mid2)
        def _():
            for a in range(n):
                for j in range(3):
                    lvl2(a, j).wait_recv()

                def total(r, a=a):
                    g = red[a][r, :]
                    for j in range(3):
                        g = g + recv2[a][j, r, :].astype(F32)
                    red[a][r, :] = g

                _row_chunks(go_r[a].shape[0], total)

        @pl.when(step == last)
        def _():
            for a in range(n):
                for j in range(4):
                    lvl1(a, j).wait_send()
                for j in range(3):
                    lvl2(a, j).wait_send()

        live = i < nb

        @pl.when(i == 0)
        def _():
            dk_carry[...] = jnp.zeros((ATT_BLOCK, KV_HEADS * HEAD_DIM), F32)
            dv_carry[...] = jnp.zeros((ATT_BLOCK, KV_HEADS * HEAD_DIM), F32)

        dist, valid = _att_dist_mask(i == 0)
        valid = valid & live
        kk = jnp.concatenate([kp_ref[...], kc_ref[...]], axis=0)
        vv = jnp.concatenate([vp_ref[...], vc_ref[...]], axis=0)
        qc, oc, doc, lc = qc_ref[...], oc_ref[...], doc_ref[...], lc_ref[...]
        dsink_cols, dq_parts = [], []
        for kv in range(KV_HEADS):
            heads = range(kv * Q_PER_KV, (kv + 1) * Q_PER_KV)
            cols = slice(kv * HEAD_DIM, (kv + 1) * HEAD_DIM)
            kh, vh = kk[:, cols], vv[:, cols]
            slope = _per_head_col([2.0 ** (-(h + 1)) for h in heads])
            sink = _per_head_col([sink_ref[h] for h in heads])
            q4, do4 = _stack_heads(qc, kv), _stack_heads(doc, kv)
            delta = jnp.sum(do4 * _stack_heads(oc, kv), axis=-1, keepdims=True)
            lse4 = _stack_cols(lc, kv)
            s = _mm_nt(q4, kh) * ATT_SCALE - slope * dist
            p = jnp.where(valid, jnp.exp(s - lse4), 0.0)
            dsc = p * (_mm_nt(do4, vh) - delta)
            dq4 = _mm(dsc, kh) * ATT_SCALE
            dk2 = _mm_tn(dsc, q4) * ATT_SCALE
            dv2 = _mm_tn(p, do4)
            dsink4 = jnp.where(live, jnp.exp(sink - lse4) * delta, 0.0)
            dk_ref[:, cols] = dk_carry[:, cols] + dk2[0:ATT_BLOCK, :]
            dv_ref[:, cols] = dv_carry[:, cols] + dv2[0:ATT_BLOCK, :]
            dk_carry[:, cols] = dk2[ATT_BLOCK:, :]
            dv_carry[:, cols] = dv2[ATT_BLOCK:, :]
            for g, h in enumerate(heads):
                rows = slice(g * ATT_BLOCK, (g + 1) * ATT_BLOCK)
                dq_parts.append((h, dq4[rows, :]))
                dsink_cols.append(-jnp.sum(dsink4[rows, :], axis=0, keepdims=True))
        dsink = jnp.concatenate(dsink_cols, axis=1)

        @pl.when(live)
        def _():
            for h, part in dq_parts:
                dq_ref[:, h * HEAD_DIM:(h + 1) * HEAD_DIM] = part

        @pl.when((b == 0) & (i == 0))
        def _():
            ds_ref[...] = dsink

        @pl.when((b != 0) | (i != 0))
        def _():
            ds_ref[...] += dsink

    cur_i = lambda i: jnp.minimum(i, nb - 1)
    cur = lambda w: pl.BlockSpec((ATT_BLOCK, w), lambda b, i: (b * nb + cur_i(i), 0))
    prev = lambda w: pl.BlockSpec((ATT_BLOCK, w), lambda b, i: (b * nb + jnp.maximum(cur_i(i) - 1, 0), 0))
    behind = lambda w: pl.BlockSpec((ATT_BLOCK, w), lambda b, i: (b * nb + jnp.maximum(i - 1, 0), 0))
    const2 = lambda s: pl.BlockSpec(s, lambda b, i: (0, 0))
    shard = [s.shape[1:] for s in late16]
    res = pl.pallas_call(
        body, name="attn_backward", grid=(bl, nb + 1),
        in_specs=[pl.BlockSpec(memory_space=pltpu.SMEM), cur(512), prev(128), cur(128), prev(128), cur(128),
                  cur(512), cur(512), cur(N_HEADS)]
        + [pl.BlockSpec(memory_space=pl.ANY)] * (2 * n),
        out_specs=(cur(512), behind(128), behind(128), const2((1, N_HEADS))) + tuple(const2(s) for s in shard),
        out_shape=(jax.ShapeDtypeStruct((t, D_ATTN), F32), jax.ShapeDtypeStruct((t, 128), F32),
                   jax.ShapeDtypeStruct((t, 128), F32), jax.ShapeDtypeStruct((1, N_HEADS), F32))
        + tuple(jax.ShapeDtypeStruct(s, F32) for s in shard),
        scratch_shapes=[pltpu.VMEM((3,) + s, BF16) for s in shard] + [pltpu.VMEM((4,) + s, BF16) for s in shard]
        + [pltpu.VMEM((3,) + s, BF16) for s in shard] + [pltpu.VMEM((3,) + s, BF16) for s in shard]
        + [pltpu.VMEM(s, F32) for s in shard]
        + [pltpu.SemaphoreType.DMA((7 * n,)), pltpu.SemaphoreType.DMA((7 * n,)), pltpu.SemaphoreType.DMA((4 * n,)),
           pltpu.VMEM((ATT_BLOCK, KV_HEADS * HEAD_DIM), F32), pltpu.VMEM((ATT_BLOCK, KV_HEADS * HEAD_DIM), F32)],
        compiler_params=_tc_params(("arbitrary", "arbitrary")),
    )(sinks, q, k, k, v, v, o, do, lse, *late16, *late32)
    return res[:4], list(res[4:])


def _mix_forward_backward(x2, y_perm, z_ssm, attn, z_attn, p2, target2, w_glu, b_glu, w_out, g_post, w_gate, b_gate,
                          w_proj, bl, seg):
    t = x2.shape[0]
    tm = seg

    def body(x_ref, y_ref, zs_ref, at_ref, za_ref, p_ref, tg_ref,
             wglu_ref, bglu_ref, wout_ref, gpost_ref, wgate_ref, bgate_ref, wproj_ref,
             loss_ref, dh1_ref, dy_ref, dzs_ref, dat_ref, dza_ref,
             dwglu_ref, dbglu_ref, dwout_ref, dgpost_ref, dwgate_ref, dbgate_ref, dwproj_ref,
             dwout16_ref, dwgate16_ref, dwproj16_ref, dwglu16_ref):
        i = pl.program_id(0)
        gpost = gpost_ref[...]

        @pl.when(i == 0)
        def _():
            for ref in (dwglu_ref, dbglu_ref, dwout_ref, dgpost_ref, dwgate_ref, dbgate_ref, dwproj_ref, loss_ref):
                ref[...] = jnp.zeros(ref.shape, F32)

        def chain(rows):
            y = y_ref[0, rows, :]
            u3 = GELU_C * (y + GELU_K * y * y * y)
            th = jnp.tanh(u3)
            gl = 0.5 * y * (1.0 + th)
            a = _mm(gl, wglu_ref[...]) + bglu_ref[...]
            sa = _sigmoid(a)
            glu = gl * sa
            zs = zs_ref[rows, :]
            sgs = _sigmoid(zs)
            ssm_out = glu * (zs * sgs)
            za = za_ref[rows, :]
            sga = _sigmoid(za)
            at = at_ref[rows, :]
            attn_out = at * (za * sga)
            cat = jnp.concatenate([ssm_out, attn_out], axis=-1).astype(BF16)
            mixed = _mm(cat, wout_ref[...])
            r2 = lax.rsqrt(jnp.mean(mixed * mixed, axis=-1, keepdims=True) + EPS)
            nhat = mixed * r2
            h1 = x_ref[rows, :] + nhat * gpost
            gate = _sigmoid(_mm(h1, wgate_ref[...]) + bgate_ref[...])
            pv = p_ref[rows, :]
            pp = _mm(pv, wproj_ref[...])
            h2 = h1 + gate * pp
            err = h2 - tg_ref[rows, :]
            loss_part = jnp.sum(jnp.sum(err * err, axis=-1, keepdims=True), axis=0, keepdims=True) * (0.5 / D_MODEL)
            dh2 = err * (1.0 / D_MODEL)
            dgp = dh2 * pp * gate * (1.0 - gate)
            dpp = dh2 * gate
            dh1 = dh2 + _mm_nt(dgp, wgate_ref[...])
            dh1_ref[rows, :] = dh1
            dnhat = dh1 * gpost
            dmixed = r2 * (dnhat - nhat * jnp.mean(dnhat * nhat, axis=-1, keepdims=True))
            dcat = _mm_nt(dmixed, wout_ref[...])
            dso, dao = dcat[:, 0:D_SSM], dcat[:, D_SSM:]
            dat_ref[rows, :] = dao * (za * sga)
            dza_ref[rows, :] = (dao * at * (sga * (1.0 + za * (1.0 - sga)))).astype(BF16)
            dzs_ref[rows, :] = (dso * glu * (sgs * (1.0 + zs * (1.0 - sgs)))).astype(BF16)
            dglu = dso * (zs * sgs)
            da = dglu * gl * sa * (1.0 - sa)
            dgl = dglu * sa + _mm_nt(da, wglu_ref[...])
            dgelu = 0.5 * (1.0 + th) + 0.5 * y * (1.0 - th * th) * (GELU_C * (1.0 + 3.0 * GELU_K * y * y))
            dy_ref[0, rows, :] = dgl * dgelu
            return dict(gl=gl.astype(BF16), da=da.astype(BF16), cat=cat, dmixed=dmixed.astype(BF16),
                        h1=h1.astype(BF16), dgp=dgp.astype(BF16), pv=pv.astype(BF16), dpp=dpp.astype(BF16),
                        dbglu=jnp.sum(da, axis=0, keepdims=True), dgpost=jnp.sum(dh1 * nhat, axis=0, keepdims=True),
                        dbgate=jnp.sum(dgp, axis=0, keepdims=True), loss=loss_part)

        groups = [chain(slice(k * (tm // MIX_GROUPS), (k + 1) * (tm // MIX_GROUPS))) for k in range(MIX_GROUPS)]
        rows_of = lambda name: jnp.concatenate([g[name] for g in groups], axis=0)
        total = lambda name: sum(g[name] for g in groups)
        parts = (
            (dwglu_ref, _mm_tn(rows_of("gl"), rows_of("da"))), (dbglu_ref, total("dbglu")),
            (dwout_ref, _mm_tn(rows_of("cat"), rows_of("dmixed"))), (dgpost_ref, total("dgpost")),
            (dwgate_ref, _mm_tn(rows_of("h1"), rows_of("dgp"))), (dbgate_ref, total("dbgate")),
            (dwproj_ref, _mm_tn(rows_of("pv"), rows_of("dpp"))), (loss_ref, total("loss")),
        )

        for ref, val in parts:
            ref[...] += val

        @pl.when(i == t // tm - 1)
        def _():
            for ref16, ref in ((dwout16_ref, dwout_ref), (dwgate16_ref, dwgate_ref), (dwproj16_ref, dwproj_ref),
                               (dwglu16_ref, dwglu_ref)):
                def to16(r, ref16=ref16, ref=ref):
                    ref16[r, :] = ref[r, :].astype(BF16)

                _row_chunks(ref.shape[0], to16)

    row = lambda w: pl.BlockSpec((tm, w), lambda i: (i, 0))
    perm = pl.BlockSpec((1, tm, D_SSM), lambda i: (i // N_SEG, 0, i % N_SEG))
    perm_shape = jax.ShapeDtypeStruct((bl, seg, N_SEG * D_SSM), F32)
    acc = lambda r, c, dt=F32: (_const_spec((r, c)), jax.ShapeDtypeStruct((r, c), dt))
    accs = [acc(D_SSM, D_SSM), acc(1, D_SSM), acc(D_MODEL, D_MODEL), acc(1, D_MODEL), acc(D_MODEL, D_MODEL),
            acc(1, D_MODEL), acc(D_PLE, D_MODEL),
            acc(D_MODEL, D_MODEL, BF16), acc(D_MODEL, D_MODEL, BF16), acc(D_PLE, D_MODEL, BF16), acc(D_SSM, D_SSM, BF16)]
    return pl.pallas_call(
        body, name="mix_forward_backward", grid=(t // tm,),
        in_specs=[row(D_MODEL), perm, row(512), row(512), row(512), row(D_PLE), row(D_MODEL),
                  _const_spec((D_SSM, D_SSM)), _const_spec((1, D_SSM)), _const_spec((D_MODEL, D_MODEL)),
                  _const_spec((1, D_MODEL)), _const_spec((D_MODEL, D_MODEL)), _const_spec((1, D_MODEL)),
                  _const_spec((D_PLE, D_MODEL))],
        out_specs=(_const_spec((1, 1)), row(D_MODEL), perm, row(512), row(512), row(512)) + tuple(a[0] for a in accs),
        out_shape=(jax.ShapeDtypeStruct((1, 1), F32), jax.ShapeDtypeStruct((t, D_MODEL), F32), perm_shape,
                   jax.ShapeDtypeStruct((t, 512), BF16), jax.ShapeDtypeStruct((t, 512), F32),
                   jax.ShapeDtypeStruct((t, 512), BF16)) + tuple(a[1] for a in accs),
        compiler_params=_tc_params(("arbitrary",)),
    )(x2, y_perm, z_ssm, attn, z_attn, p2, target2, w_glu, b_glu, w_out, g_post, w_gate, b_gate, w_proj)


def _in_backward(x2, dh1, du_perm, dz_ssm, dq, dk, dv, dz_attn, g_pre, w_in, bl, seg):
    t = x2.shape[0]
    tm = seg

    def body(x_ref, dh1_ref, du_ref, dzs_ref, dq_ref, dk_ref, dv_ref, dza_ref, g_ref, w_ref,
             gx_ref, dw_ref, dg_ref, dw16_ref):
        i = pl.program_id(0)
        xv = x_ref[...]
        r = lax.rsqrt(jnp.mean(xv * xv, axis=-1, keepdims=True) + EPS)
        xhat = xv * r
        g = g_ref[...]
        hn = (xhat * g).astype(BF16)
        dproj = jnp.concatenate([du_ref[0].astype(BF16), dzs_ref[...].astype(BF16), dq_ref[...].astype(BF16),
                                 dk_ref[...].astype(BF16), dv_ref[...].astype(BF16), dza_ref[...].astype(BF16)],
                                axis=-1)
        dhn = _mm(dproj, w_ref[...])
        dxhat = dhn * g
        gx_ref[...] = dh1_ref[...] + r * (dxhat - xhat * jnp.mean(dxhat * xhat, axis=-1, keepdims=True))
        @pl.when(i == 0)
        def _():
            dw_ref[...] = jnp.zeros((D_IN, D_MODEL), F32)
            dg_ref[...] = jnp.zeros((1, D_MODEL), F32)

        dw_ref[...] += _mm_tn(dproj, hn)
        dg_ref[...] += jnp.sum(dhn * xhat, axis=0, keepdims=True)

        @pl.when(i == t // tm - 1)
        def _():
            def to16(r):
                dw16_ref[r, :] = dw_ref[r, :].astype(BF16)

            _row_chunks(D_IN, to16)

    row = lambda w: pl.BlockSpec((tm, w), lambda i: (i, 0))
    perm = pl.BlockSpec((1, tm, D_SSM), lambda i: (i // N_SEG, 0, i % N_SEG))
    return pl.pallas_call(
        body, name="in_backward", grid=(t // tm,),
        in_specs=[row(D_MODEL), row(D_MODEL), perm, row(512), row(512), row(128), row(128), row(512),
                  _const_spec((1, D_MODEL)), _const_spec((D_IN, D_MODEL))],
        out_specs=(row(D_MODEL), _const_spec((D_IN, D_MODEL)), _const_spec((1, D_MODEL)),
                   _const_spec((D_IN, D_MODEL))),
        out_shape=(jax.ShapeDtypeStruct((t, D_MODEL), F32), jax.ShapeDtypeStruct((D_IN, D_MODEL), F32),
                   jax.ShapeDtypeStruct((1, D_MODEL), F32), jax.ShapeDtypeStruct((D_IN, D_MODEL), BF16)),
        compiler_params=_tc_params(("arbitrary",)),
    )(x2, dh1, du_perm, dz_ssm, dq, dk, dv, dz_attn, g_pre, w_in)


def _block_diag(t):
    a, b = t.shape[1], t.shape[2]
    eye = jnp.eye(G_TILE, dtype=t.dtype)
    t = t.reshape(N_GT, G_TILE, a, 1, b) * eye[None, :, None, :, None]
    return t.reshape(N_GT, G_TILE * a, G_TILE * b)


def _diag_blocks(m, a, b):
    m = m.reshape(N_GT, G_TILE, a, G_TILE, b)
    return jnp.einsum("tgagb->tgab", m).reshape(SSM_G, a, b)


def _local_step(x, p, target, pre_norm_g, w_in, ssm_lam_re, ssm_lam_im, ssm_log_step, ssm_b_re, ssm_b_im, ssm_c_re,
                ssm_c_im, ssm_d, ssm_b_glu, attn_sinks, post_norm_g, pl_b_gate, late):
    bl, seq, _ = x.shape
    seg = seq // N_SEG
    nb = seq // ATT_BLOCK
    t = bl * seq
    x2 = x.reshape(t, D_MODEL)
    p2 = p.reshape(t, D_PLE)
    tg2 = target.reshape(t, D_MODEL)

    lam_re, lam_im = ssm_lam_re, ssm_lam_im
    log_step = ssm_log_step.reshape(SSM_G, 1)
    a_re_row, a_im_row, bb_re, bb_im, pw_re, pw_im = _ssm_prep(lam_re, lam_im, log_step, ssm_b_re, ssm_b_im, seg)
    by_group = lambda t: t.reshape(SSM_G, SSM_P, SSM_N)
    bcat = jnp.concatenate([_block_diag(by_group(bb_re)), _block_diag(by_group(bb_im))], axis=-1).astype(BF16)
    ccat_t = jnp.concatenate([_block_diag(by_group(ssm_c_re)), -_block_diag(by_group(ssm_c_im))],
                             axis=-1).astype(BF16)
    bcat_t = jnp.swapaxes(bcat, 1, 2)
    ccat = jnp.swapaxes(ccat_t, 1, 2)
    d_row = ssm_d.reshape(1, D_SSM)

    (u_perm, z_ssm, q, k, v, z_attn), (g_out, g_glu) = _in_proj(
        x2, pre_norm_g.reshape(1, D_MODEL), w_in, [late[0], late[3]], bl, seg)
    u_perm = u_perm.reshape(bl, seq, D_SSM)
    y_perm, states, carries = _ssm_forward(u_perm, bcat, ccat, a_re_row, a_im_row, pw_re, pw_im, d_row, seg)
    sinks = attn_sinks.reshape(N_HEADS)
    (attn, lse), (g_gate, g_proj) = _attn_forward(q, k, v, sinks, [late[1], late[2]], bl, nb)
    w_out, w_gate, w_proj, w_glu = (_gathered_to_full(n, g) for n, g in zip(LATE_NAMES, (g_out, g_gate, g_proj, g_glu)))
    (loss, dh1, dy_perm, dz_ssm, dattn, dz_attn, d_w_glu, d_b_glu, d_w_out, d_g_post, d_w_gate, d_b_gate,
     d_w_proj, *late16) = _mix_forward_backward(
        x2, y_perm.reshape(bl, seg, N_SEG * D_SSM), z_ssm, attn, z_attn, p2, tg2, w_glu,
        ssm_b_glu.reshape(1, D_SSM), w_out, post_norm_g.reshape(1, D_MODEL), w_gate, pl_b_gate.reshape(1, D_MODEL),
        w_proj, bl, seg)
    owned = lambda ds: [_full_to_owned(n, d) for n, d in zip(LATE_NAMES, ds)]
    (dq, dk, dv, d_sinks), late_grads = _attn_backward(
        q, k, v, attn, dattn, lse, sinks, owned(late16), owned((d_w_out, d_w_gate, d_w_proj, d_w_glu)), bl, nb)
    du_perm, d_bcat, d_ccat_t, da_re, da_im, d_d = _ssm_backward(
        u_perm, dy_perm.reshape(bl, seq, D_SSM), states, carries, bcat_t, ccat_t, a_re_row, a_im_row, pw_re, pw_im,
        d_row, seg)
    grad_x, d_w_in, d_g_pre, d_w_in16 = _in_backward(
        x2, dh1, du_perm.reshape(bl, seg, N_SEG * D_SSM), dz_ssm, dq, dk, dv, dz_attn,
        pre_norm_g.reshape(1, D_MODEL), w_in, bl, seg)
    flat = lambda t: t.reshape(SSM_G * SSM_P, SSM_N)
    d_lam_re, d_lam_im, d_ls, d_b_re, d_b_im = _ssm_param_grads(
        lam_re, lam_im, log_step, ssm_b_re, ssm_b_im, da_re.reshape(SSM_G, SSM_N), da_im.reshape(SSM_G, SSM_N),
        flat(_diag_blocks(d_bcat[:, :, 0:ST_T], SSM_P, SSM_N)), flat(_diag_blocks(d_bcat[:, :, ST_T:], SSM_P, SSM_N)))
    grads = {
        "pre_norm_g": d_g_pre, "w_in": d_w_in, "w_in16": d_w_in16, "ssm_lam_re": d_lam_re, "ssm_lam_im": d_lam_im,
        "ssm_log_step": d_ls, "ssm_b_re": d_b_re, "ssm_b_im": d_b_im,
        "ssm_c_re": _diag_blocks(d_ccat_t[:, :, 0:ST_T], SSM_P, SSM_N),
        "ssm_c_im": -_diag_blocks(d_ccat_t[:, :, ST_T:], SSM_P, SSM_N),
        "ssm_d": d_d, "ssm_b_glu": d_b_glu, "attn_sinks": d_sinks, "post_norm_g": d_g_post, "pl_b_gate": d_b_gate,
    }
    return loss, grad_x.reshape(bl, seq, D_MODEL), grads, late_grads


LATE_NAMES = ("w_out", "pl_w_gate", "pl_w_proj", "ssm_w_glu")
BIG_NAMES = ("w_in",) + LATE_NAMES
COL_SHARDED = {"w_in": D_IN // N_DEV, "pl_w_proj": D_MODEL // N_DEV}
WEIGHT_NAMES = ("pre_norm_g", "w_in", "ssm_lam_re", "ssm_lam_im", "ssm_log_step", "ssm_b_re", "ssm_b_im", "ssm_c_re",
                "ssm_c_im", "ssm_d", "ssm_w_glu", "ssm_b_glu", "attn_sinks", "w_out", "post_norm_g", "pl_w_proj",
                "pl_w_gate", "pl_b_gate")


TRANSPOSED = {"w_in": (0, 1), "ssm_b_re": (1, 2), "ssm_b_im": (1, 2)}


def _kernel_form(name, a):
    a = a[0]
    if name in TRANSPOSED:
        a = jnp.swapaxes(a, *TRANSPOSED[name])
    if name in ("ssm_b_re", "ssm_b_im", "ssm_c_re", "ssm_c_im"):
        a = a.reshape(SSM_G * SSM_P, SSM_N)
    return a


def _given_form(name, a, shape):
    if name in TRANSPOSED:
        i, j = TRANSPOSED[name]
        swapped = list(shape[1:])
        swapped[i], swapped[j] = swapped[j], swapped[i]
        return jnp.swapaxes(a.reshape(swapped), i, j).reshape(shape)
    return a.reshape(shape)


def _gathered_to_full(name, g):
    _, rows, cols = g.shape
    if name in COL_SHARDED:
        return jnp.swapaxes(g, 0, 1).reshape(rows, N_DEV * cols)
    return g.reshape(N_DEV * rows, cols)


def _full_to_owned(name, full):
    if name in COL_SHARDED:
        return jnp.swapaxes(full.reshape(full.shape[0], N_DEV, COL_SHARDED[name]), 0, 1)
    return full.reshape(N_DEV, full.shape[0] // N_DEV, full.shape[1])


def kernel(x, p, pre_norm_g, w_in, ssm_lam_re, ssm_lam_im, ssm_log_step, ssm_b_re, ssm_b_im, ssm_c_re, ssm_c_im, ssm_d, ssm_w_glu, ssm_b_glu, attn_sinks, w_out, post_norm_g, pl_w_proj, pl_w_gate, pl_b_gate, loss_target, m_pre_norm_g, m_w_in, m_ssm_lam_re, m_ssm_lam_im, m_ssm_log_step, m_ssm_b_re, m_ssm_b_im, m_ssm_c_re, m_ssm_c_im, m_ssm_d, m_ssm_w_glu, m_ssm_b_glu, m_attn_sinks, m_w_out, m_post_norm_g, m_pl_w_proj, m_pl_w_gate, m_pl_b_gate, v_pre_norm_g, v_w_in, v_ssm_lam_re, v_ssm_lam_im, v_ssm_log_step, v_ssm_b_re, v_ssm_b_im, v_ssm_c_re, v_ssm_c_im, v_ssm_d, v_ssm_w_glu, v_ssm_b_glu, v_attn_sinks, v_w_out, v_post_norm_g, v_pl_w_proj, v_pl_w_gate, v_pl_b_gate):
    w = dict(pre_norm_g=pre_norm_g, w_in=w_in, ssm_lam_re=ssm_lam_re, ssm_lam_im=ssm_lam_im, ssm_log_step=ssm_log_step,
             ssm_b_re=ssm_b_re, ssm_b_im=ssm_b_im, ssm_c_re=ssm_c_re, ssm_c_im=ssm_c_im, ssm_d=ssm_d, ssm_w_glu=ssm_w_glu,
             ssm_b_glu=ssm_b_glu, attn_sinks=attn_sinks, w_out=w_out, post_norm_g=post_norm_g, pl_w_proj=pl_w_proj,
             pl_w_gate=pl_w_gate, pl_b_gate=pl_b_gate)
    m = dict(pre_norm_g=m_pre_norm_g, w_in=m_w_in, ssm_lam_re=m_ssm_lam_re, ssm_lam_im=m_ssm_lam_im,
             ssm_log_step=m_ssm_log_step, ssm_b_re=m_ssm_b_re, ssm_b_im=m_ssm_b_im, ssm_c_re=m_ssm_c_re,
             ssm_c_im=m_ssm_c_im, ssm_d=m_ssm_d, ssm_w_glu=m_ssm_w_glu, ssm_b_glu=m_ssm_b_glu, attn_sinks=m_attn_sinks,
             w_out=m_w_out, post_norm_g=m_post_norm_g, pl_w_proj=m_pl_w_proj, pl_w_gate=m_pl_w_gate,
             pl_b_gate=m_pl_b_gate)
    v = dict(pre_norm_g=v_pre_norm_g, w_in=v_w_in, ssm_lam_re=v_ssm_lam_re, ssm_lam_im=v_ssm_lam_im,
             ssm_log_step=v_ssm_log_step, ssm_b_re=v_ssm_b_re, ssm_b_im=v_ssm_b_im, ssm_c_re=v_ssm_c_re,
             ssm_c_im=v_ssm_c_im, ssm_d=v_ssm_d, ssm_w_glu=v_ssm_w_glu, ssm_b_glu=v_ssm_b_glu, attn_sinks=v_attn_sinks,
             w_out=v_w_out, post_norm_g=v_post_norm_g, pl_w_proj=v_pl_w_proj, pl_w_gate=v_pl_w_gate,
             pl_b_gate=v_pl_b_gate)
    kf = lambda d: {n: _kernel_form(n, a) for n, a in d.items()}
    wk, mk, vk = kf(w), kf(m), kf(v)

    (gathered,) = _allgather_weights([wk["w_in"]])
    loss, grad_x, grads, g_late = _local_step(
        x, p[0], loss_target, wk["pre_norm_g"], gathered.reshape(D_IN, D_MODEL), wk["ssm_lam_re"], wk["ssm_lam_im"],
        wk["ssm_log_step"], wk["ssm_b_re"], wk["ssm_b_im"], wk["ssm_c_re"], wk["ssm_c_im"], wk["ssm_d"],
        wk["ssm_b_glu"], wk["attn_sinks"], wk["post_norm_g"], wk["pl_b_gate"], [wk[n] for n in LATE_NAMES])

    owned = lambda g: g.reshape(N_DEV, D_IN // N_DEV, D_MODEL)
    tiny_form = lambda d: [d[n].reshape(rows, cols) for n, rows, cols in TINY]
    med_form = lambda d: [d[n].reshape(N_DEV, rows // N_DEV, cols) for n, rows, cols in MEDIUM]
    g_big, loss, g_tiny, g_med = _reduce_final(
        [owned(grads["w_in16"])], [owned(grads["w_in"])], loss, tiny_form(grads), med_form(grads))
    names = BIG_NAMES + tuple(n for n, _, _ in TINY + MEDIUM)
    form = lambda d: [d[n] for n in BIG_NAMES] + tiny_form(d) + med_form(d)
    updated = _adamw_update(g_big + g_late + g_tiny + g_med, form(wk), form(mk), form(vk))
    vals = dict(zip(names, updated))
    results = [[_given_form(n, vals[n][kind], w[n].shape) for n in WEIGHT_NAMES] for kind in range(4)]
    return (loss.reshape(()), grad_x, *results[0], *results[1], *results[2], *results[3])
```

```python
import functools
import math

import jax
import jax.numpy as jnp
from jax import lax
from jax.experimental import pallas as pl
from jax.experimental.pallas import tpu as pltpu

F32 = jnp.float32
BF16 = jnp.bfloat16

D_MODEL = 1024
D_SSM = 512
D_ATTN = 512
SSM_P = 16
SSM_G = 32
SSM_N = 64
N_HEADS = 8
KV_HEADS = 2
Q_PER_KV = 4
HEAD_DIM = 64
ATT_BLOCK = 128
D_PLE = 256
D_IN = 2304
EPS = 1e-6
N_DEV = 8
N_SEG = 8
G_TILE = 8
N_GT = SSM_G // G_TILE
CH_T = G_TILE * SSM_P
ST_T = G_TILE * SSM_N
N_STATE = SSM_G * SSM_N
SCAN_UNROLL = 4
MIX_GROUPS = 1
LANES = 128
VMEM_LIMIT = 60 * 1024 * 1024

ADAM_LR = 0.001
ADAM_B1 = 0.9
ADAM_B2 = 0.999
ADAM_EPS = 1e-08
ADAM_WD = 0.01
ADAM_STEP = 10

GELU_C = math.sqrt(2.0 / math.pi)
GELU_K = 0.044715
ATT_SCALE = 1.0 / math.sqrt(HEAD_DIM)
NEG_BIG = -1e30


def _mm(a, b):
    return jnp.dot(a.astype(BF16), b.astype(BF16), preferred_element_type=F32)


def _mm_nt(a, b):
    return lax.dot_general(a.astype(BF16), b.astype(BF16), (((1,), (1,)), ((), ())), preferred_element_type=F32)


def _mm_tn(a, b):
    return lax.dot_general(a.astype(BF16), b.astype(BF16), (((0,), (0,)), ((), ())), preferred_element_type=F32)


def _sigmoid(x):
    return 1.0 / (1.0 + jnp.exp(-x))


def _tc_params(sem):
    return pltpu.CompilerParams(dimension_semantics=sem, vmem_limit_bytes=VMEM_LIMIT)


def _const_spec(shape):
    nd = len(shape)
    return pl.BlockSpec(shape, lambda *_: (0,) * nd)


def _mesh_pos():
    return lax.axis_index("x"), lax.axis_index("y"), lax.axis_index("c")


ROW_CHUNKS = (64, 32, 16)


def _row_chunk(nrows):
    return next((c for c in ROW_CHUNKS if nrows % c == 0), None)


def _row_chunks(nrows, fn, chunk=None, init=None):
    chunk = chunk or _row_chunk(nrows)

    def step(i, carry):
        rows = pl.ds(pl.multiple_of(i * chunk, chunk), chunk)
        if init is None:
            fn(rows)
            return carry
        return fn(rows, carry)

    return lax.fori_loop(0, nrows // chunk, step, 0 if init is None else init)


def _slot(px, py, pc):
    return 4 * px + 2 * py + pc


def _allgather_weights(shards):
    n = len(shards)

    def body(*refs):
        srcs, outs, (send_sems, recv_sems) = refs[:n], refs[n:2 * n], refs[2 * n:]
        x, y, c = _mesh_pos()
        me, sibling = (x, y, c), (x, y, 1 - c)
        chips = [(1 - x, y), (x, 1 - y), (1 - x, 1 - y)]

        def copy(a, k, block, to):
            blk = outs[a].at[_slot(*block)]
            return pltpu.make_async_remote_copy(
                src_ref=blk, dst_ref=blk, send_sem=send_sems.at[7 * a + k], recv_sem=recv_sems.at[7 * a + k],
                device_id=to, device_id_type=pl.DeviceIdType.MESH)

        sends = []
        for a in range(n):
            mine = outs[a].at[_slot(*me)]

            def cast(r, mine=mine, src=srcs[a]):
                mine[r, :] = src[r, :].astype(BF16)

            _row_chunks(srcs[a].shape[0], cast)
            first = [copy(a, 0, me, sibling)] + [copy(a, 1 + j, me, (*chip, c)) for j, chip in enumerate(chips)]
            for cp in first:
                cp.start()
            sends += first
        for a in range(n):
            for j, chip in enumerate(chips):
                copy(a, 1 + j, (*chip, c), me).wait_recv()
                fwd = copy(a, 4 + j, (*chip, c), sibling)
                fwd.start()
                sends.append(fwd)
        for a in range(n):
            copy(a, 0, sibling, me).wait_recv()
            for j, chip in enumerate(chips):
                copy(a, 4 + j, (*chip, 1 - c), me).wait_recv()
        for cp in sends:
            cp.wait_send()

    vm = pl.BlockSpec(memory_space=pltpu.VMEM)
    return pl.pallas_call(
        body, name="allgather_weights",
        out_shape=tuple(jax.ShapeDtypeStruct((N_DEV,) + s.shape, BF16) for s in shards),
        in_specs=[vm] * n, out_specs=(vm,) * n,
        scratch_shapes=[pltpu.SemaphoreType.DMA((7 * n,)), pltpu.SemaphoreType.DMA((7 * n,))],
        compiler_params=pltpu.CompilerParams(vmem_limit_bytes=VMEM_LIMIT),
    )(*shards)


def _adamw(w, g, m, v):
    m = ADAM_B1 * m + (1.0 - ADAM_B1) * g
    v = ADAM_B2 * v + (1.0 - ADAM_B2) * (g * g)
    m_hat = m / (1.0 - ADAM_B1 ** ADAM_STEP)
    v_hat = v / (1.0 - ADAM_B2 ** ADAM_STEP)
    delta = -ADAM_LR * (m_hat / (jnp.sqrt(v_hat) + ADAM_EPS) + ADAM_WD * w)
    return delta, m, v


def _remote(src, dst, send_sems, recv_sems, k, to):
    return pltpu.make_async_remote_copy(src_ref=src, dst_ref=dst, send_sem=send_sems.at[k], recv_sem=recv_sems.at[k],
                                        device_id=to, device_id_type=pl.DeviceIdType.MESH)


def _big_reduce_phases(g16_r, go_r, outs, send2, recv1, recv2, s_send, s_recv):
    n = len(g16_r)
    x, y, c = _mesh_pos()
    sibling = (x, y, 1 - c)
    chips = [(1 - x, y), (x, 1 - y), (1 - x, 1 - y)]
    all_chips = [(x, y)] + chips
    lvl1 = []
    for a in range(n):
        cps = [_remote(g16_r[a].at[_slot(*chip, 1 - c)], recv1[a].at[j], s_send, s_recv, 7 * a + j, sibling)
               for j, chip in enumerate(all_chips)]
        for cp in cps:
            cp.start()
        lvl1.append(cps)
    yield
    lvl2 = []
    for a in range(n):
        for cp in lvl1[a]:
            cp.wait_recv()
        og = outs[a]

        def partials(r, a=a, og=og):
            og[r, :] = go_r[a][r, :] + recv1[a][0, r, :].astype(F32)
            for j, chip in enumerate(chips):
                mine16 = g16_r[a][_slot(*chip, c), r, :].astype(F32)
                send2[a][j, r, :] = (mine16 + recv1[a][1 + j, r, :].astype(F32)).astype(BF16)

        _row_chunks(go_r[a].shape[0], partials)
        cps = [_remote(send2[a].at[j], recv2[a].at[j], s_send, s_recv, 7 * a + 4 + j, (*chip, c))
               for j, chip in enumerate(chips)]
        for cp in cps:
            cp.start()
        lvl2.append(cps)
    yield
    for a in range(n):
        for cp in lvl2[a]:
            cp.wait_recv()
        og = outs[a]

        def total(r, a=a, og=og):
            g = og[r, :]
            for j in range(3):
                g = g + recv2[a][j, r, :].astype(F32)
            og[r, :] = g

        _row_chunks(go_r[a].shape[0], total)
    yield
    for cps in lvl1 + lvl2:
        for cp in cps:
            cp.wait_send()


def _adamw_update(g, w, m, v):
    n = len(g)

    def body(*refs):
        g_r, w_r, m_r, v_r = (refs[i * n:(i + 1) * n] for i in range(4))
        outs = refs[4 * n:]
        for a in range(n):
            og, od, om, ov = outs[4 * a:4 * a + 4]

            def update(idx, a=a, og=og, od=od, om=om, ov=ov):
                gv = g_r[a][idx]
                d, nm, nv = _adamw(w_r[a][idx], gv, m_r[a][idx], v_r[a][idx])
                og[idx] = gv
                od[idx] = d
                om[idx] = nm
                ov[idx] = nv

            shape = g_r[a].shape
            if len(shape) == 3:
                for b in range(shape[0]):
                    update(b)
            elif _row_chunk(shape[0]) is not None:
                _row_chunks(shape[0], update)
            else:
                update(Ellipsis)

    vm = pl.BlockSpec(memory_space=pltpu.VMEM)
    res = pl.pallas_call(
        body, name="adamw_update",
        out_shape=tuple(jax.ShapeDtypeStruct(t.shape, F32) for t in g for _ in range(4)),
        in_specs=[vm] * (4 * n), out_specs=(vm,) * (4 * n),
        compiler_params=pltpu.CompilerParams(vmem_limit_bytes=VMEM_LIMIT),
    )(*g, *w, *m, *v)
    return [res[4 * a:4 * a + 4] for a in range(n)]


TINY = (("pre_norm_g", 1, 1024), ("post_norm_g", 1, 1024), ("pl_b_gate", 1, 1024), ("ssm_d", 1, 512),
        ("ssm_b_glu", 1, 512), ("ssm_log_step", 1, 32), ("attn_sinks", 1, 8), ("ssm_lam_re", 32, 64),
        ("ssm_lam_im", 32, 64))
MEDIUM = (("ssm_b_re", SSM_G * SSM_P, SSM_N), ("ssm_b_im", SSM_G * SSM_P, SSM_N), ("ssm_c_re", SSM_G * SSM_P, SSM_N),
          ("ssm_c_im", SSM_G * SSM_P, SSM_N))


def _stage_rows():
    offs, r = {}, 0
    for name, rows, cols in TINY + (("loss", 1, 1),):
        if rows > 1:
            r = -(-r // 8) * 8
        offs[name] = r
        r += rows if rows > 1 else max(cols // LANES, 1)
    return offs, -(-r // 8) * 8


def _reduce_final(g16, g32, loss, g_tiny, g_med):
    nb_, nt, nm_ = len(g16), len(TINY), len(MEDIUM)
    offs, stage_rows = _stage_rows()

    def body(*refs):
        g16_r, go_r = refs[:nb_], refs[nb_:2 * nb_]
        base = 2 * nb_
        loss_r, gt, gm = refs[base], refs[base + 1:base + 1 + nt], refs[base + 1 + nt:base + 1 + nt + nm_]
        base += 1 + nt + nm_
        out_b = refs[base:base + nb_]
        base += nb_
        loss_o, out_t, out_m = refs[base], refs[base + 1:base + 1 + nt], refs[base + 1 + nt:base + 1 + nt + nm_]
        base += 1 + nt + nm_
        send2_b, recv1_b, recv2_b = (refs[base + i * nb_:base + (i + 1) * nb_] for i in range(3))
        base += 3 * nb_
        stage = refs[base]
        recv1, part, recv2 = (refs[base + 1 + i * nm_:base + 1 + (i + 1) * nm_] for i in range(3))
        bs_send, bs_recv, s_send, s_recv, own_sems = refs[base + 1 + 3 * nm_:base + 6 + 3 * nm_]
        own32 = refs[base + 6 + 3 * nm_:]
        me = _slot(*_mesh_pos())
        fetch = [pltpu.make_async_copy(go_r[a].at[me], own32[a], own_sems.at[a]) for a in range(nb_)]
        for cp in fetch:
            cp.start()
        big = _big_reduce_phases(g16_r, own32, out_b, send2_b, recv1_b, recv2_b, bs_send, bs_recv)
        small = small_phases(loss_r, gt, gm, loss_o, out_t, out_m, stage, recv1, part, recv2, s_send, s_recv)
        next(big)
        next(small)
        for cp in fetch:
            cp.wait()
        next(big)
        for _ in small:
            pass
        for _ in big:
            pass

    def small_phases(loss_r, gt, gm, loss_o, out_t, out_m, stage, recv1, part, recv2, s_send, s_recv):
        x, y, c = _mesh_pos()
        me = _slot(x, y, c)
        sibling = (x, y, 1 - c)
        chips = [(1 - x, y), (x, 1 - y), (1 - x, 1 - y)]
        all_chips = [(x, y)] + chips
        peers = [sibling] + [(*chip, c) for chip in chips] + [(*chip, 1 - c) for chip in chips]
        sem = iter(range(7 + 14 * nm_))
        lvl1 = []
        for a in range(nm_):
            cps = [_remote(gm[a].at[_slot(*chip, 1 - c)], recv1[a].at[j], s_send, s_recv, next(sem), sibling)
                   for j, chip in enumerate(all_chips)]
            for cp in cps:
                cp.start()
            lvl1.append(cps)
        mine = stage.at[me]
        mine[...] = jnp.zeros((stage_rows, LANES), F32)
        for (name, rows, cols), ref in zip(TINY + (("loss", 1, 1),), gt + (loss_r,)):
            r0 = offs[name]
            if rows > 1:
                mine[r0:r0 + rows, 0:cols] = ref[...]
            elif cols >= LANES:
                for i in range(cols // LANES):
                    mine[r0 + i:r0 + i + 1, :] = ref[:, i * LANES:(i + 1) * LANES]
            else:
                mine[r0:r0 + 1, 0:cols] = ref[...]
        tiny_cps = [_remote(mine, mine, s_send, s_recv, next(sem), peer) for peer in peers]
        for cp in tiny_cps:
            cp.start()
        yield
        lvl2 = []
        for a in range(nm_):
            for cp in lvl1[a]:
                cp.wait_recv()
            for j, chip in enumerate(all_chips):
                part[a][j] = gm[a][_slot(*chip, c)] + recv1[a][j]
            cps = [_remote(part[a].at[1 + j], recv2[a].at[j], s_send, s_recv, next(sem), (*chip, c))
                   for j, chip in enumerate(chips)]
            for cp in cps:
                cp.start()
            lvl2.append(cps)
        yield
        lvl3 = []
        for a in range(nm_):
            for cp in lvl2[a]:
                cp.wait_recv()
            blk = out_m[a].at[me]
            blk[...] = ((part[a][0] + recv2[a][0]) + recv2[a][1]) + recv2[a][2]
            cps = [_remote(blk, blk, s_send, s_recv, next(sem), peer) for peer in peers]
            for cp in cps:
                cp.start()
            lvl3.append(cps)
        yield
        for cp in tiny_cps:
            cp.wait_recv()
        tot = stage[0]
        for d in range(1, N_DEV):
            tot = tot + stage[d]
        loss_o[...] = tot[offs["loss"]:offs["loss"] + 1, 0:1]
        for k, (name, rows, cols) in enumerate(TINY):
            r0 = offs[name]
            if rows > 1:
                out_t[k][...] = tot[r0:r0 + rows, 0:cols]
            elif cols >= LANES:
                for i in range(cols // LANES):
                    out_t[k][:, i * LANES:(i + 1) * LANES] = tot[r0 + i:r0 + i + 1, :]
            else:
                out_t[k][...] = tot[r0:r0 + 1, 0:cols]
        for cps in lvl3:
            for cp in cps:
                cp.wait_recv()
        for cps in lvl1 + lvl2 + lvl3 + [tiny_cps]:
            for cp in cps:
                cp.wait_send()

    vmem = pl.BlockSpec(memory_space=pltpu.VMEM)
    t_shapes = [jax.ShapeDtypeStruct((rows, cols), F32) for _, rows, cols in TINY]
    m_shapes = [jax.ShapeDtypeStruct((N_DEV, rows // N_DEV, cols), F32) for _, rows, cols in MEDIUM]
    blk = [(rows // N_DEV, cols) for _, rows, cols in MEDIUM]
    shard = [g.shape[1:] for g in g16]
    scratch = ([pltpu.VMEM((3,) + s, BF16) for s in shard] + [pltpu.VMEM((4,) + s, BF16) for s in shard]
               + [pltpu.VMEM((3,) + s, BF16) for s in shard]
               + [pltpu.VMEM((N_DEV, stage_rows, LANES), F32)]
               + [pltpu.VMEM((4,) + b, F32) for b in blk] + [pltpu.VMEM((4,) + b, F32) for b in blk]
               + [pltpu.VMEM((3,) + b, F32) for b in blk]
               + [pltpu.SemaphoreType.DMA((7 * nb_,)), pltpu.SemaphoreType.DMA((7 * nb_,)),
                  pltpu.SemaphoreType.DMA((7 + 14 * nm_,)), pltpu.SemaphoreType.DMA((7 + 14 * nm_,)),
                  pltpu.SemaphoreType.DMA((nb_,))]
               + [pltpu.VMEM(s, F32) for s in shard])
    n_out = nb_ + 1 + nt + nm_
    res = pl.pallas_call(
        body, name="reduce_final",
        out_shape=tuple(jax.ShapeDtypeStruct(s, F32) for s in shard) + (jax.ShapeDtypeStruct((1, 1), F32),)
        + tuple(t_shapes) + tuple(m_shapes),
        in_specs=[vmem] * nb_ + [pl.BlockSpec(memory_space=pl.ANY)] * nb_ + [vmem] * (1 + nt + nm_),
        out_specs=(vmem,) * n_out, scratch_shapes=scratch,
        compiler_params=pltpu.CompilerParams(vmem_limit_bytes=VMEM_LIMIT),
    )(*g16, *g32, loss, *g_tiny, *g_med)
    return list(res[:nb_]), res[nb_], list(res[nb_ + 1:nb_ + 1 + nt]), list(res[nb_ + 1 + nt:])


def _gather_phases(shard_r, gath, cast, send_sems, recv_sems, local_sems):
    n = len(shard_r)
    x, y, c = _mesh_pos()
    me, sibling = (x, y, c), (x, y, 1 - c)
    chips = [(1 - x, y), (x, 1 - y), (1 - x, 1 - y)]

    def own(a, k, to):
        return _remote(cast[a], gath[a].at[_slot(*me)], send_sems, recv_sems, 7 * a + k, to)

    def passed(a, k, block, to):
        blk = gath[a].at[_slot(*block)]
        return _remote(blk, blk, send_sems, recv_sems, 7 * a + k, to)

    def keep(a):
        return pltpu.make_async_copy(cast[a], gath[a].at[_slot(*me)], local_sems.at[a])

    def start():
        for a in range(n):
            def to16(r, a=a):
                cast[a][r, :] = shard_r[a][r, :].astype(BF16)

            _row_chunks(shard_r[a].shape[0], to16)
            keep(a).start()
            own(a, 0, sibling).start()
            for j, chip in enumerate(chips):
                own(a, 1 + j, (*chip, c)).start()

    def relay():
        for a in range(n):
            for j, chip in enumerate(chips):
                passed(a, 1 + j, (*chip, c), me).wait_recv()
                passed(a, 4 + j, (*chip, c), sibling).start()

    def finish():
        for a in range(n):
            passed(a, 0, sibling, me).wait_recv()
            for j, chip in enumerate(chips):
                passed(a, 4 + j, (*chip, 1 - c), me).wait_recv()
            own(a, 0, sibling).wait_send()
            for j, chip in enumerate(chips):
                own(a, 1 + j, (*chip, c)).wait_send()
                passed(a, 4 + j, (*chip, c), sibling).wait_send()
            keep(a).wait()

    return start, relay, finish


def _gather_operands(shards):
    n = len(shards)
    return ((pl.BlockSpec(memory_space=pl.ANY),) * n,
            tuple(jax.ShapeDtypeStruct((N_DEV,) + s.shape, BF16) for s in shards),
            [pltpu.VMEM(s.shape, BF16) for s in shards]
            + [pltpu.SemaphoreType.DMA((7 * n,)), pltpu.SemaphoreType.DMA((7 * n,)), pltpu.SemaphoreType.DMA((n,))])


def _hosted_reduce_phases(g16_r, g32_r, red, own16, recv1, send2, recv2, own32, s_send, s_recv, s_local):
    n = len(g16_r)
    x, y, c = _mesh_pos()
    sibling = (x, y, 1 - c)
    chips = [(1 - x, y), (x, 1 - y), (1 - x, 1 - y)]
    all_chips = [(x, y)] + chips

    def lvl1(a, j):
        return _remote(g16_r[a].at[_slot(*all_chips[j], 1 - c)], recv1[a].at[j], s_send, s_recv, 7 * a + j, sibling)

    def lvl2(a, j):
        return _remote(send2[a].at[j], recv2[a].at[j], s_send, s_recv, 7 * a + 4 + j, (*chips[j], c))

    def mine(a, j):
        if j == 3:
            return pltpu.make_async_copy(g32_r[a].at[_slot(x, y, c)], own32[a], s_local.at[4 * a + j])
        return pltpu.make_async_copy(g16_r[a].at[_slot(*chips[j], c)], own16[a].at[j], s_local.at[4 * a + j])

    def start():
        for a in range(n):
            for j in range(4):
                mine(a, j).start()
            for j in range(4):
                lvl1(a, j).start()

    def middle():
        for a in range(n):
            for j in range(4):
                mine(a, j).wait()
            for j in range(4):
                lvl1(a, j).wait_recv()

            def partials(r, a=a):
                red[a][r, :] = own32[a][r, :] + recv1[a][0, r, :].astype(F32)
                for j in range(3):
                    send2[a][j, r, :] = (own16[a][j, r, :].astype(F32) + recv1[a][1 + j, r, :].astype(F32)).astype(BF16)

            _row_chunks(own32[a].shape[0], partials)
            for j in range(3):
                lvl2(a, j).start()

    def total():
        for a in range(n):
            for j in range(3):
                lvl2(a, j).wait_recv()

            def add(r, a=a):
                g = red[a][r, :]
                for j in range(3):
                    g = g + recv2[a][j, r, :].astype(F32)
                red[a][r, :] = g

            _row_chunks(own32[a].shape[0], add)

    def finish():
        for a in range(n):
            for j in range(4):
                lvl1(a, j).wait_send()
            for j in range(3):
                lvl2(a, j).wait_send()

    return start, middle, total, finish


def _hosted_reduce_operands(g16, const_spec):
    n = len(g16)
    shard = [g.shape[1:] for g in g16]
    return ([pl.BlockSpec(memory_space=pl.ANY)] * (2 * n),
            tuple(const_spec(s) for s in shard),
            tuple(jax.ShapeDtypeStruct(s, F32) for s in shard),
            [pltpu.VMEM((3,) + s, BF16) for s in shard] + [pltpu.VMEM((4,) + s, BF16) for s in shard]
            + [pltpu.VMEM((3,) + s, BF16) for s in shard] + [pltpu.VMEM((3,) + s, BF16) for s in shard]
            + [pltpu.VMEM(s, F32) for s in shard]
            + [pltpu.SemaphoreType.DMA((7 * n,)), pltpu.SemaphoreType.DMA((7 * n,)), pltpu.SemaphoreType.DMA((4 * n,))])


def _in_proj(x2, g_pre, w_in, bl, seg):
    t = x2.shape[0]
    tm = seg

    def body(x_ref, g_ref, w_ref, u_ref, zs_ref, q_ref, k_ref, v_ref, za_ref):
        xv = x_ref[...]
        r = lax.rsqrt(jnp.mean(xv * xv, axis=-1, keepdims=True) + EPS)
        hn = xv * r * g_ref[...]
        proj = _mm_nt(hn, w_ref[...])
        u_ref[0] = proj[:, 0:512]
        zs_ref[...] = proj[:, 512:1024]
        q_ref[...] = proj[:, 1024:1536].astype(BF16)
        k_ref[...] = proj[:, 1536:1664].astype(BF16)
        v_ref[...] = proj[:, 1664:1792].astype(BF16)
        za_ref[...] = proj[:, 1792:2304]

    row = lambda w: pl.BlockSpec((tm, w), lambda i: (i, 0))
    return pl.pallas_call(
        body, name="in_proj", grid=(t // tm,),
        in_specs=[row(D_MODEL), _const_spec((1, D_MODEL)), _const_spec((D_IN, D_MODEL))],
        out_specs=(pl.BlockSpec((1, tm, D_SSM), lambda i: (i // N_SEG, 0, i % N_SEG)),
                   row(512), row(512), row(128), row(128), row(512)),
        out_shape=(jax.ShapeDtypeStruct((bl, seg, N_SEG * D_SSM), F32),
                   jax.ShapeDtypeStruct((t, 512), F32), jax.ShapeDtypeStruct((t, 512), BF16),
                   jax.ShapeDtypeStruct((t, 128), BF16), jax.ShapeDtypeStruct((t, 128), BF16),
                   jax.ShapeDtypeStruct((t, 512), F32)),
        compiler_params=_tc_params(("arbitrary",)),
    )(x2, g_pre, w_in)


def _discretise(lr, li, ls):
    step = jnp.exp(ls)
    mag = jnp.exp(lr * step)
    ar = mag * jnp.cos(li * step)
    ai = mag * jnp.sin(li * step)
    den = lr * lr + li * li
    cr = ((ar - 1.0) * lr + ai * li) / den
    ci = (ai * lr - (ar - 1.0) * li) / den
    return step, ar, ai, den, cr, ci


def _per_channel(v):
    return jnp.broadcast_to(v[:, None, :], (SSM_G, SSM_P, SSM_N)).reshape(SSM_G * SSM_P, SSM_N)


def _ssm_prep(lam_re, lam_im, log_step, b_re, b_im, seg):
    def body(lr_ref, li_ref, ls_ref, br_ref, bi_ref, lrr_ref, lir_ref, lsr_ref,
             ar_ref, ai_ref, bbr_ref, bbi_ref, pr_ref, pi_ref):
        _, _, _, _, cr, ci = _discretise(lr_ref[...], li_ref[...], ls_ref[...])
        cr, ci = _per_channel(cr), _per_channel(ci)
        br, bi = br_ref[...], bi_ref[...]
        bbr_ref[...] = cr * br - ci * bi
        bbi_ref[...] = cr * bi + ci * br
        stepr = jnp.exp(lsr_ref[...])
        k = (lax.broadcasted_iota(jnp.int32, (8, N_STATE), 0) + 1).astype(F32)
        magk = jnp.exp(k * (lrr_ref[...] * stepr))
        ang = k * (lir_ref[...] * stepr)
        pr_ref[0:8, :] = magk * jnp.cos(ang)
        pi_ref[0:8, :] = magk * jnp.sin(ang)
        n = 8
        while n < seg:
            tr, ti = pr_ref[n - 1:n, :], pi_ref[n - 1:n, :]
            xr, xi = pr_ref[0:n, :], pi_ref[0:n, :]
            pr_ref[n:2 * n, :] = xr * tr - xi * ti
            pi_ref[n:2 * n, :] = xr * ti + xi * tr
            n *= 2
        ar_ref[...] = pr_ref[0:1, :]
        ai_ref[...] = pi_ref[0:1, :]

    row = jax.ShapeDtypeStruct((1, N_STATE), F32)
    mat = jax.ShapeDtypeStruct((SSM_G * SSM_P, SSM_N), F32)
    pw = jax.ShapeDtypeStruct((seg, N_STATE), F32)
    vm = pl.BlockSpec(memory_space=pltpu.VMEM)
    step_row = jnp.broadcast_to(log_step, (SSM_G, SSM_N)).reshape(1, N_STATE)
    return pl.pallas_call(
        body, name="ssm_prep", out_shape=(row, row, mat, mat, pw, pw),
        in_specs=[vm] * 8, out_specs=(vm,) * 6,
    )(lam_re, lam_im, log_step, b_re, b_im, lam_re.reshape(1, N_STATE), lam_im.reshape(1, N_STATE), step_row)


def _seg_rows(t):
    if isinstance(t, int):
        return pl.ds(t * N_SEG, N_SEG)
    return pl.ds(pl.multiple_of(t * N_SEG, N_SEG), N_SEG)


def _scan_forward(xs, a_re, a_im, pw_re, pw_im, cs, seg):
    are = jnp.broadcast_to(a_re, (N_SEG, ST_T))
    aim = jnp.broadcast_to(a_im, (N_SEG, ST_T))

    def steps(k, carry):
        xr, xi = carry
        for j in range(SCAN_UNROLL):
            r = pl.multiple_of((k * SCAN_UNROLL + j) * N_SEG, N_SEG)
            nr = are * xr - aim * xi + xs[pl.ds(r, N_SEG), 0:ST_T]
            ni = are * xi + aim * xr + xs[pl.ds(r, N_SEG), ST_T:2 * ST_T]
            xs[pl.ds(r, N_SEG), 0:ST_T] = nr
            xs[pl.ds(r, N_SEG), ST_T:2 * ST_T] = ni
            xr, xi = nr, ni
        return xr, xi

    zero = jnp.zeros((N_SEG, ST_T), F32)
    fr, fi = lax.fori_loop(0, seg // SCAN_UNROLL, steps, (zero, zero))
    sr, si = pw_re[seg - 1:seg, :], pw_im[seg - 1:seg, :]
    cr = jnp.zeros((1, ST_T), F32)
    ci = jnp.zeros((1, ST_T), F32)
    cs[0:1, :] = cr
    cs[8:9, :] = ci
    for s in range(1, N_SEG):
        ncr = sr * cr - si * ci + fr[s - 1:s, :]
        nci = sr * ci + si * cr + fi[s - 1:s, :]
        cr, ci = ncr, nci
        cs[s:s + 1, :] = cr
        cs[8 + s:9 + s, :] = ci
    car, cai = cs[0:8, :], cs[8:16, :]

    def fix(t, _):
        r = pl.multiple_of(t * N_SEG, N_SEG)
        pr, pi = pw_re[pl.ds(t, 1), :], pw_im[pl.ds(t, 1), :]
        xs[pl.ds(r, N_SEG), 0:ST_T] = xs[pl.ds(r, N_SEG), 0:ST_T] + (pr * car - pi * cai)
        xs[pl.ds(r, N_SEG), ST_T:2 * ST_T] = xs[pl.ds(r, N_SEG), ST_T:2 * ST_T] + (pr * cai + pi * car)
        return 0

    lax.fori_loop(0, seg, fix, 0, unroll=SCAN_UNROLL)


def _ssm_forward(u_perm, bcat, ccat, a_re, a_im, pw_re, pw_im, d_row, late, seg):
    bl, rows, _ = u_perm.shape
    n = len(late)
    steps = bl * N_GT

    def body(*refs):
        u_ref, b_ref, c_ref, ar_ref, ai_ref, pr_ref, pi_ref, d_ref = refs[:8]
        late_r = refs[8:8 + n]
        y_ref, xs_ref, cs_ref = refs[8 + n:11 + n]
        gath, cast = refs[11 + n:11 + 2 * n], refs[11 + 2 * n:11 + 3 * n]
        send_sems, recv_sems, local_sems = refs[11 + 3 * n:]
        step = pl.program_id(0) * N_GT + pl.program_id(1)
        start, relay, finish = _gather_phases(late_r, gath, cast, send_sems, recv_sems, local_sems)
        pl.when(step == 0)(start)
        u = u_ref[0]
        xs, cs = xs_ref.at[0, 0], cs_ref.at[0, 0]
        xs[...] = _mm(u, b_ref[0])
        _scan_forward(xs, ar_ref[...], ai_ref[...], pr_ref, pi_ref, cs, seg)
        y_ref[0] = _mm(xs[...], c_ref[0]) + d_ref[...] * u
        pl.when(step == steps // 2)(relay)
        pl.when(step == steps - 1)(finish)

    state = lambda r, c: pl.BlockSpec((1, 1, r, c), lambda b, j: (b, j, 0, 0))
    g_specs, g_shapes, g_scratch = _gather_operands(late)
    res = pl.pallas_call(
        body, name="ssm_forward", grid=(bl, N_GT),
        in_specs=[pl.BlockSpec((1, rows, CH_T), lambda b, j: (b, 0, j)),
                  pl.BlockSpec((1, CH_T, 2 * ST_T), lambda b, j: (j, 0, 0)),
                  pl.BlockSpec((1, 2 * ST_T, CH_T), lambda b, j: (j, 0, 0)),
                  pl.BlockSpec((1, ST_T), lambda b, j: (0, j)), pl.BlockSpec((1, ST_T), lambda b, j: (0, j)),
                  pl.BlockSpec((seg, ST_T), lambda b, j: (0, j)), pl.BlockSpec((seg, ST_T), lambda b, j: (0, j)),
                  pl.BlockSpec((1, CH_T), lambda b, j: (0, j))]
        + [pl.BlockSpec(s.shape, lambda b, j: (0, 0)) for s in late],
        out_specs=(pl.BlockSpec((1, rows, CH_T), lambda b, j: (b, 0, j)), state(rows, 2 * ST_T), state(16, ST_T))
        + g_specs,
        out_shape=(jax.ShapeDtypeStruct((bl, rows, D_SSM), F32),
                   jax.ShapeDtypeStruct((bl, N_GT, rows, 2 * ST_T), F32),
                   jax.ShapeDtypeStruct((bl, N_GT, 16, ST_T), F32)) + g_shapes,
        scratch_shapes=g_scratch,
        compiler_params=_tc_params(("arbitrary", "arbitrary")),
    )(u_perm, bcat, ccat, a_re, a_im, pw_re, pw_im, d_row, *late)
    return res[:3], list(res[3:])


def _ssm_backward(u_perm, dy_perm, states, carries, bcat_t, ccat_t, a_re, a_im, pw_re, pw_im, d_row, late16, late32, seg):
    bl, rows, _ = u_perm.shape
    n = len(late16)
    grid_steps = N_GT * bl

    def body(*refs):
        u_ref, dy_ref, xs_ref, cs_ref, bt_ref, ct_ref, ar_ref, ai_ref, pr_ref, pi_ref, d_ref = refs[:11]
        g16_r, g32_r = refs[11:11 + n], refs[11 + n:11 + 2 * n]
        du_ref, db_ref, dc_ref, dar_ref, dai_ref, dd_ref = refs[11 + 2 * n:17 + 2 * n]
        red = refs[17 + 2 * n:17 + 3 * n]
        own16, recv1, send2, recv2, own32 = (refs[17 + 3 * n + k * n:17 + 3 * n + (k + 1) * n] for k in range(5))
        s_send, s_recv, s_local, ls, cl = refs[17 + 8 * n:]
        b = pl.program_id(1)
        step = pl.program_id(0) * bl + b
        start, middle, total, finish = _hosted_reduce_phases(g16_r, g32_r, red, own16, recv1, send2, recv2, own32,
                                                             s_send, s_recv, s_local)
        pl.when(step == 0)(start)
        pl.when(step == grid_steps // 4)(middle)
        pl.when(step == (grid_steps * 3) // 4)(total)
        pl.when(step == grid_steps - 1)(finish)
        u = u_ref[0]
        dy = dy_ref[0]
        xs, cs = xs_ref.at[0, 0], cs_ref.at[0, 0]
        ls[...] = _mm(dy, ct_ref[0])
        are = jnp.broadcast_to(ar_ref[...], (N_SEG, ST_T))
        aim = jnp.broadcast_to(ai_ref[...], (N_SEG, ST_T))

        def steps(k, carry):
            lr, li = carry
            for j in range(SCAN_UNROLL):
                r = pl.multiple_of((seg - 1 - (k * SCAN_UNROLL + j)) * N_SEG, N_SEG)
                nr = are * lr + aim * li + ls[pl.ds(r, N_SEG), 0:ST_T]
                ni = are * li - aim * lr + ls[pl.ds(r, N_SEG), ST_T:2 * ST_T]
                ls[pl.ds(r, N_SEG), 0:ST_T] = nr
                ls[pl.ds(r, N_SEG), ST_T:2 * ST_T] = ni
                lr, li = nr, ni
            return lr, li

        zero = jnp.zeros((N_SEG, ST_T), F32)
        fr, fi = lax.fori_loop(0, seg // SCAN_UNROLL, steps, (zero, zero))
        sr, si = pr_ref[seg - 1:seg, :], pi_ref[seg - 1:seg, :]
        cr = jnp.zeros((1, ST_T), F32)
        ci = jnp.zeros((1, ST_T), F32)
        cl[7:8, :] = cr
        cl[15:16, :] = ci
        for s in range(N_SEG - 2, -1, -1):
            ncr = sr * cr + si * ci + fr[s + 1:s + 2, :]
            nci = sr * ci - si * cr + fi[s + 1:s + 2, :]
            cr, ci = ncr, nci
            cl[s:s + 1, :] = cr
            cl[8 + s:9 + s, :] = ci
        clr, cli = cl[0:8, :], cl[8:16, :]

        def fix_rows(rows, t, xpr, xpi, acc):
            dr, di = acc
            pr, pi = pr_ref[pl.ds(seg - 1 - t, 1), :], pi_ref[pl.ds(seg - 1 - t, 1), :]
            lr = ls[rows, 0:ST_T] + (pr * clr + pi * cli)
            li = ls[rows, ST_T:2 * ST_T] + (pr * cli - pi * clr)
            ls[rows, 0:ST_T] = lr
            ls[rows, ST_T:2 * ST_T] = li
            return dr + (lr * xpr + li * xpi), di + (li * xpr - lr * xpi)

        def fix_at(t, acc):
            prev = _seg_rows(t - 1)
            return fix_rows(_seg_rows(t), t, xs[prev, 0:ST_T], xs[prev, ST_T:2 * ST_T], acc)

        def fix(k, acc):
            for j in range(SCAN_UNROLL):
                acc = fix_at(k * SCAN_UNROLL + j, acc)
            return acc

        acc = fix_rows(pl.ds(0, N_SEG), 0, cs[0:8, :], cs[8:16, :], (zero, zero))
        for t in range(1, SCAN_UNROLL):
            acc = fix_at(t, acc)
        dr, di = lax.fori_loop(1, seg // SCAN_UNROLL, fix, acc)
        dar = jnp.sum(dr, axis=0, keepdims=True)
        dai = jnp.sum(di, axis=0, keepdims=True)
        lall = ls[...]
        du_ref[0] = (_mm(lall, bt_ref[0]) + d_ref[...] * dy).astype(BF16)
        dbp = _mm_tn(u, lall)
        dcp = _mm_tn(dy, xs[...])
        ddp = jnp.sum(dy * u, axis=0, keepdims=True)

        @pl.when(b == 0)
        def _():
            db_ref[0] = dbp
            dc_ref[0] = dcp
            dar_ref[...] = dar
            dai_ref[...] = dai
            dd_ref[...] = ddp

        @pl.when(b != 0)
        def _():
            db_ref[0] += dbp
            dc_ref[0] += dcp
            dar_ref[...] += dar
            dai_ref[...] += dai
            dd_ref[...] += ddp

    tile3 = lambda r, c: pl.BlockSpec((1, r, c), lambda j, b: (j, 0, 0))
    lane = lambda r, c: pl.BlockSpec((r, c), lambda j, b: (0, j))
    act = pl.BlockSpec((1, rows, CH_T), lambda j, b: (b, 0, j))
    state = lambda r, c: pl.BlockSpec((1, 1, r, c), lambda j, b: (b, j, 0, 0))
    r_in, r_out, r_shapes, r_scratch = _hosted_reduce_operands(late16, lambda s: pl.BlockSpec(s, lambda j, b: (0, 0)))
    res = pl.pallas_call(
        body, name="ssm_backward", grid=(N_GT, bl),
        in_specs=[act, act, state(rows, 2 * ST_T), state(16, ST_T), tile3(2 * ST_T, CH_T), tile3(CH_T, 2 * ST_T),
                  lane(1, ST_T), lane(1, ST_T), lane(seg, ST_T), lane(seg, ST_T), lane(1, CH_T)] + r_in,
        out_specs=(act, tile3(CH_T, 2 * ST_T), tile3(CH_T, 2 * ST_T), lane(1, ST_T), lane(1, ST_T), lane(1, CH_T))
        + r_out,
        out_shape=(jax.ShapeDtypeStruct((bl, rows, D_SSM), BF16),
                   jax.ShapeDtypeStruct((N_GT, CH_T, 2 * ST_T), F32), jax.ShapeDtypeStruct((N_GT, CH_T, 2 * ST_T), F32),
                   jax.ShapeDtypeStruct((1, N_STATE), F32), jax.ShapeDtypeStruct((1, N_STATE), F32),
                   jax.ShapeDtypeStruct((1, D_SSM), F32)) + r_shapes,
        scratch_shapes=r_scratch + [pltpu.VMEM((rows, 2 * ST_T), F32), pltpu.VMEM((16, ST_T), F32)],
        compiler_params=_tc_params(("arbitrary", "arbitrary")),
    )(u_perm, dy_perm, states, carries, bcat_t, ccat_t, a_re, a_im, pw_re, pw_im, d_row, *late16, *late32)
    return res[:6], list(res[6:])


def _ssm_param_grads(lam_re, lam_im, log_step, b_re, b_im, da_re, da_im, dbb_re, dbb_im):
    def body(lr_ref, li_ref, ls_ref, br_ref, bi_ref, gar_ref, gai_ref, gbr_ref, gbi_ref,
             dlr_ref, dli_ref, dls_ref, dbr_ref, dbi_ref):
        lr, li = lr_ref[...], li_ref[...]
        step, ar, ai, den, cr, ci = _discretise(lr, li, ls_ref[...])
        crb, cib = _per_channel(cr), _per_channel(ci)
        br, bi = br_ref[...], bi_ref[...]
        gbr, gbi = gbr_ref[...], gbi_ref[...]
        dbr_ref[...] = crb * gbr + cib * gbi
        dbi_ref[...] = crb * gbi - cib * gbr
        over_channels = lambda t: jnp.sum(t.reshape(SSM_G, SSM_P, SSM_N), axis=1)
        gcr = over_channels(br * gbr + bi * gbi)
        gci = over_channels(br * gbi - bi * gbr)
        ilr, ili = lr / den, -li / den
        gar = gar_ref[...] + (ilr * gcr + ili * gci)
        gai = gai_ref[...] + (ilr * gci - ili * gcr)
        qr, qi = cr * ilr - ci * ili, cr * ili + ci * ilr
        glr = -(qr * gcr + qi * gci)
        gli = -(qr * gci - qi * gcr)
        gwr = ar * gar + ai * gai
        gwi = ar * gai - ai * gar
        dlr_ref[...] = glr + step * gwr
        dli_ref[...] = gli + step * gwi
        dls_ref[...] = jnp.sum(lr * gwr + li * gwi, axis=-1, keepdims=True) * step

    lam = jax.ShapeDtypeStruct((SSM_G, SSM_N), F32)
    mat = jax.ShapeDtypeStruct((SSM_G * SSM_P, SSM_N), F32)
    vm = pl.BlockSpec(memory_space=pltpu.VMEM)
    return pl.pallas_call(
        body, name="ssm_param_grads", out_shape=(lam, lam, jax.ShapeDtypeStruct((SSM_G, 1), F32), mat, mat),
        in_specs=[vm] * 9, out_specs=(vm,) * 5,
    )(lam_re, lam_im, log_step, b_re, b_im, da_re, da_im, dbb_re, dbb_im)


ROWS4 = Q_PER_KV * ATT_BLOCK


def _att_dist_mask(first_block):
    qi = lax.broadcasted_iota(jnp.int32, (ROWS4, 2 * ATT_BLOCK), 0) & (ATT_BLOCK - 1)
    si = lax.broadcasted_iota(jnp.int32, (ROWS4, 2 * ATT_BLOCK), 1)
    dist = qi + ATT_BLOCK - si
    valid = (dist >= 0) & (dist < ATT_BLOCK) & ((si >= ATT_BLOCK) | jnp.logical_not(first_block))
    return dist.astype(F32), valid


def _stack_heads(x, kv):
    return jnp.concatenate([x[:, (kv * Q_PER_KV + g) * HEAD_DIM:(kv * Q_PER_KV + g + 1) * HEAD_DIM]
                            for g in range(Q_PER_KV)], axis=0)


def _stack_cols(x, kv):
    return jnp.concatenate([x[:, kv * Q_PER_KV + g:kv * Q_PER_KV + g + 1] for g in range(Q_PER_KV)], axis=0)


def _per_head_col(vals):
    return jnp.concatenate([jnp.full((ATT_BLOCK, 1), v, F32) for v in vals], axis=0)


def _attn_forward(q, k, v, sinks, bl, nb):
    t = q.shape[0]

    def body(sink_ref, q_ref, kp_ref, kc_ref, vp_ref, vc_ref, o_ref, lse_ref):
        i = pl.program_id(1)
        dist4, valid4 = _att_dist_mask(i == 0)
        dist, valid = dist4[0:ATT_BLOCK, :], valid4[0:ATT_BLOCK, :]
        kk = jnp.concatenate([kp_ref[...], kc_ref[...]], axis=0)
        vv = jnp.concatenate([vp_ref[...], vc_ref[...]], axis=0)
        qv = q_ref[...]
        for h in range(N_HEADS):
            kv = h // Q_PER_KV
            slope = 2.0 ** (-(h + 1))
            qh = qv[:, h * HEAD_DIM:(h + 1) * HEAD_DIM]
            kh = kk[:, kv * HEAD_DIM:(kv + 1) * HEAD_DIM]
            vh = vv[:, kv * HEAD_DIM:(kv + 1) * HEAD_DIM]
            s = _mm_nt(qh, kh) * ATT_SCALE - slope * dist
            s = jnp.where(valid, s, NEG_BIG)
            sink = sink_ref[h]
            m = jnp.maximum(jnp.max(s, axis=-1, keepdims=True), sink)
            e = jnp.exp(s - m)
            den = jnp.sum(e, axis=-1, keepdims=True) + jnp.exp(sink - m)
            o_ref[:, h * HEAD_DIM:(h + 1) * HEAD_DIM] = _mm(e, vh) * (1.0 / den)
            lse_ref[:, h:h + 1] = m + jnp.log(den)

    cur = lambda w: pl.BlockSpec((ATT_BLOCK, w), lambda b, i: (b * nb + i, 0))
    prev = lambda w: pl.BlockSpec((ATT_BLOCK, w), lambda b, i: (b * nb + jnp.maximum(i - 1, 0), 0))
    return pl.pallas_call(
        body, name="attn_forward", grid=(bl, nb),
        in_specs=[pl.BlockSpec(memory_space=pltpu.SMEM), cur(512), prev(128), cur(128), prev(128), cur(128)],
        out_specs=(cur(512), cur(N_HEADS)),
        out_shape=(jax.ShapeDtypeStruct((t, D_ATTN), F32), jax.ShapeDtypeStruct((t, N_HEADS), F32)),
        compiler_params=_tc_params(("arbitrary", "arbitrary")),
    )(sinks, q, k, k, v, v)


def _attn_backward(q, k, v, o, do, lse, sinks, bl, nb):
    t = q.shape[0]

    def body(sink_ref, qc_ref, kp_ref, kc_ref, vp_ref, vc_ref, oc_ref, doc_ref, lc_ref,
             dq_ref, dk_ref, dv_ref, ds_ref, dk_carry, dv_carry):
        b, i = pl.program_id(0), pl.program_id(1)
        live = i < nb

        @pl.when(i == 0)
        def _():
            dk_carry[...] = jnp.zeros((ATT_BLOCK, KV_HEADS * HEAD_DIM), F32)
            dv_carry[...] = jnp.zeros((ATT_BLOCK, KV_HEADS * HEAD_DIM), F32)

        dist, valid = _att_dist_mask(i == 0)
        valid = valid & live
        kk = jnp.concatenate([kp_ref[...], kc_ref[...]], axis=0)
        vv = jnp.concatenate([vp_ref[...], vc_ref[...]], axis=0)
        qc, oc, doc, lc = qc_ref[...], oc_ref[...], doc_ref[...], lc_ref[...]
        dsink_cols, dq_parts = [], []
        for kv in range(KV_HEADS):
            heads = range(kv * Q_PER_KV, (kv + 1) * Q_PER_KV)
            cols = slice(kv * HEAD_DIM, (kv + 1) * HEAD_DIM)
            kh, vh = kk[:, cols], vv[:, cols]
            slope = _per_head_col([2.0 ** (-(h + 1)) for h in heads])
            sink = _per_head_col([sink_ref[h] for h in heads])
            q4, do4 = _stack_heads(qc, kv), _stack_heads(doc, kv)
            delta = jnp.sum(do4 * _stack_heads(oc, kv), axis=-1, keepdims=True)
            lse4 = _stack_cols(lc, kv)
            s = _mm_nt(q4, kh) * ATT_SCALE - slope * dist
            p = jnp.where(valid, jnp.exp(s - lse4), 0.0)
            dsc = p * (_mm_nt(do4, vh) - delta)
            dq4 = _mm(dsc, kh) * ATT_SCALE
            dk2 = _mm_tn(dsc, q4) * ATT_SCALE
            dv2 = _mm_tn(p, do4)
            dsink4 = jnp.where(live, jnp.exp(sink - lse4) * delta, 0.0)
            dk_ref[:, cols] = dk_carry[:, cols] + dk2[0:ATT_BLOCK, :]
            dv_ref[:, cols] = dv_carry[:, cols] + dv2[0:ATT_BLOCK, :]
            dk_carry[:, cols] = dk2[ATT_BLOCK:, :]
            dv_carry[:, cols] = dv2[ATT_BLOCK:, :]
            for g, h in enumerate(heads):
                rows = slice(g * ATT_BLOCK, (g + 1) * ATT_BLOCK)
                dq_parts.append((h, dq4[rows, :]))
                dsink_cols.append(-jnp.sum(dsink4[rows, :], axis=0, keepdims=True))
        dsink = jnp.concatenate(dsink_cols, axis=1)

        @pl.when(live)
        def _():
            for h, part in dq_parts:
                dq_ref[:, h * HEAD_DIM:(h + 1) * HEAD_DIM] = part

        @pl.when((b == 0) & (i == 0))
        def _():
            ds_ref[...] = dsink

        @pl.when((b != 0) | (i != 0))
        def _():
            ds_ref[...] += dsink

    cur_i = lambda i: jnp.minimum(i, nb - 1)
    cur = lambda w: pl.BlockSpec((ATT_BLOCK, w), lambda b, i: (b * nb + cur_i(i), 0))
    prev = lambda w: pl.BlockSpec((ATT_BLOCK, w), lambda b, i: (b * nb + jnp.maximum(cur_i(i) - 1, 0), 0))
    behind = lambda w: pl.BlockSpec((ATT_BLOCK, w), lambda b, i: (b * nb + jnp.maximum(i - 1, 0), 0))
    return pl.pallas_call(
        body, name="attn_backward", grid=(bl, nb + 1),
        in_specs=[pl.BlockSpec(memory_space=pltpu.SMEM), cur(512), prev(128), cur(128), prev(128), cur(128),
                  cur(512), cur(512), cur(N_HEADS)],
        out_specs=(cur(512), behind(128), behind(128), pl.BlockSpec((1, N_HEADS), lambda b, i: (0, 0))),
        out_shape=(jax.ShapeDtypeStruct((t, D_ATTN), F32), jax.ShapeDtypeStruct((t, 128), F32),
                   jax.ShapeDtypeStruct((t, 128), F32), jax.ShapeDtypeStruct((1, N_HEADS), F32)),
        scratch_shapes=[pltpu.VMEM((ATT_BLOCK, KV_HEADS * HEAD_DIM), F32), pltpu.VMEM((ATT_BLOCK, KV_HEADS * HEAD_DIM), F32)],
        compiler_params=_tc_params(("arbitrary", "arbitrary")),
    )(sinks, q, k, k, v, v, o, do, lse)


def _mix_forward_backward(x2, y_perm, z_ssm, attn, z_attn, p2, target2, w_glu, b_glu, w_out, g_post, w_gate, b_gate,
                          w_proj, bl, seg):
    t = x2.shape[0]
    tm = seg

    def body(x_ref, y_ref, zs_ref, at_ref, za_ref, p_ref, tg_ref,
             wglu_ref, bglu_ref, wout_ref, gpost_ref, wgate_ref, bgate_ref, wproj_ref,
             loss_ref, dh1_ref, dy_ref, dzs_ref, dat_ref, dza_ref,
             dwglu_ref, dbglu_ref, dwout_ref, dgpost_ref, dwgate_ref, dbgate_ref, dwproj_ref,
             dwout16_ref, dwgate16_ref, dwproj16_ref, dwglu16_ref):
        i = pl.program_id(0)
        gpost = gpost_ref[...]

        @pl.when(i == 0)
        def _():
            for ref in (dwglu_ref, dbglu_ref, dwout_ref, dgpost_ref, dwgate_ref, dbgate_ref, dwproj_ref, loss_ref):
                ref[...] = jnp.zeros(ref.shape, F32)

        def chain(rows):
            y = y_ref[0, rows, :]
            u3 = GELU_C * (y + GELU_K * y * y * y)
            th = jnp.tanh(u3)
            gl = 0.5 * y * (1.0 + th)
            a = _mm(gl, wglu_ref[...]) + bglu_ref[...]
            sa = _sigmoid(a)
            glu = gl * sa
            zs = zs_ref[rows, :]
            sgs = _sigmoid(zs)
            ssm_out = glu * (zs * sgs)
            za = za_ref[rows, :]
            sga = _sigmoid(za)
            at = at_ref[rows, :]
            attn_out = at * (za * sga)
            cat = jnp.concatenate([ssm_out, attn_out], axis=-1).astype(BF16)
            mixed = _mm(cat, wout_ref[...])
            r2 = lax.rsqrt(jnp.mean(mixed * mixed, axis=-1, keepdims=True) + EPS)
            nhat = mixed * r2
            h1 = x_ref[rows, :] + nhat * gpost
            gate = _sigmoid(_mm(h1, wgate_ref[...]) + bgate_ref[...])
            pv = p_ref[rows, :]
            pp = _mm(pv, wproj_ref[...])
            h2 = h1 + gate * pp
            err = h2 - tg_ref[rows, :]
            loss_part = jnp.sum(jnp.sum(err * err, axis=-1, keepdims=True), axis=0, keepdims=True) * (0.5 / D_MODEL)
            dh2 = err * (1.0 / D_MODEL)
            dgp = dh2 * pp * gate * (1.0 - gate)
            dpp = dh2 * gate
            dh1 = dh2 + _mm_nt(dgp, wgate_ref[...])
            dh1_ref[rows, :] = dh1
            dnhat = dh1 * gpost
            dmixed = r2 * (dnhat - nhat * jnp.mean(dnhat * nhat, axis=-1, keepdims=True))
            dcat = _mm_nt(dmixed, wout_ref[...])
            dso, dao = dcat[:, 0:D_SSM], dcat[:, D_SSM:]
            dat_ref[rows, :] = dao * (za * sga)
            dza_ref[rows, :] = (dao * at * (sga * (1.0 + za * (1.0 - sga)))).astype(BF16)
            dzs_ref[rows, :] = (dso * glu * (sgs * (1.0 + zs * (1.0 - sgs)))).astype(BF16)
            dglu = dso * (zs * sgs)
            da = dglu * gl * sa * (1.0 - sa)
            dgl = dglu * sa + _mm_nt(da, wglu_ref[...])
            dgelu = 0.5 * (1.0 + th) + 0.5 * y * (1.0 - th * th) * (GELU_C * (1.0 + 3.0 * GELU_K * y * y))
            dy_ref[0, rows, :] = dgl * dgelu
            return dict(gl=gl.astype(BF16), da=da.astype(BF16), cat=cat, dmixed=dmixed.astype(BF16),
                        h1=h1.astype(BF16), dgp=dgp.astype(BF16), pv=pv.astype(BF16), dpp=dpp.astype(BF16),
                        dbglu=jnp.sum(da, axis=0, keepdims=True), dgpost=jnp.sum(dh1 * nhat, axis=0, keepdims=True),
                        dbgate=jnp.sum(dgp, axis=0, keepdims=True), loss=loss_part)

        groups = [chain(slice(k * (tm // MIX_GROUPS), (k + 1) * (tm // MIX_GROUPS))) for k in range(MIX_GROUPS)]
        rows_of = lambda name: jnp.concatenate([g[name] for g in groups], axis=0)
        total = lambda name: sum(g[name] for g in groups)
        parts = (
            (dwglu_ref, _mm_tn(rows_of("gl"), rows_of("da"))), (dbglu_ref, total("dbglu")),
            (dwout_ref, _mm_tn(rows_of("cat"), rows_of("dmixed"))), (dgpost_ref, total("dgpost")),
            (dwgate_ref, _mm_tn(rows_of("h1"), rows_of("dgp"))), (dbgate_ref, total("dbgate")),
            (dwproj_ref, _mm_tn(rows_of("pv"), rows_of("dpp"))), (loss_ref, total("loss")),
        )

        for ref, val in parts:
            ref[...] += val

        @pl.when(i == t // tm - 1)
        def _():
            for ref16, ref in ((dwout16_ref, dwout_ref), (dwgate16_ref, dwgate_ref), (dwproj16_ref, dwproj_ref),
                               (dwglu16_ref, dwglu_ref)):
                def to16(r, ref16=ref16, ref=ref):
                    ref16[r, :] = ref[r, :].astype(BF16)

                _row_chunks(ref.shape[0], to16)

    row = lambda w: pl.BlockSpec((tm, w), lambda i: (i, 0))
    perm = pl.BlockSpec((1, tm, D_SSM), lambda i: (i // N_SEG, 0, i % N_SEG))
    perm_shape = jax.ShapeDtypeStruct((bl, seg, N_SEG * D_SSM), F32)
    acc = lambda r, c, dt=F32: (_const_spec((r, c)), jax.ShapeDtypeStruct((r, c), dt))
    accs = [acc(D_SSM, D_SSM), acc(1, D_SSM), acc(D_MODEL, D_MODEL), acc(1, D_MODEL), acc(D_MODEL, D_MODEL),
            acc(1, D_MODEL), acc(D_PLE, D_MODEL),
            acc(D_MODEL, D_MODEL, BF16), acc(D_MODEL, D_MODEL, BF16), acc(D_PLE, D_MODEL, BF16), acc(D_SSM, D_SSM, BF16)]
    return pl.pallas_call(
        body, name="mix_forward_backward", grid=(t // tm,),
        in_specs=[row(D_MODEL), perm, row(512), row(512), row(512), row(D_PLE), row(D_MODEL),
                  _const_spec((D_SSM, D_SSM)), _const_spec((1, D_SSM)), _const_spec((D_MODEL, D_MODEL)),
                  _const_spec((1, D_MODEL)), _const_spec((D_MODEL, D_MODEL)), _const_spec((1, D_MODEL)),
                  _const_spec((D_PLE, D_MODEL))],
        out_specs=(_const_spec((1, 1)), row(D_MODEL), perm, row(512), row(512), row(512)) + tuple(a[0] for a in accs),
        out_shape=(jax.ShapeDtypeStruct((1, 1), F32), jax.ShapeDtypeStruct((t, D_MODEL), F32), perm_shape,
                   jax.ShapeDtypeStruct((t, 512), BF16), jax.ShapeDtypeStruct((t, 512), F32),
                   jax.ShapeDtypeStruct((t, 512), BF16)) + tuple(a[1] for a in accs),
        compiler_params=_tc_params(("arbitrary",)),
    )(x2, y_perm, z_ssm, attn, z_attn, p2, target2, w_glu, b_glu, w_out, g_post, w_gate, b_gate, w_proj)


def _in_backward(x2, dh1, du_perm, dz_ssm, dq, dk, dv, dz_attn, g_pre, w_in, bl, seg):
    t = x2.shape[0]
    tm = seg

    def body(x_ref, dh1_ref, du_ref, dzs_ref, dq_ref, dk_ref, dv_ref, dza_ref, g_ref, w_ref,
             gx_ref, dw_ref, dg_ref, dw16_ref):
        i = pl.program_id(0)
        xv = x_ref[...]
        r = lax.rsqrt(jnp.mean(xv * xv, axis=-1, keepdims=True) + EPS)
        xhat = xv * r
        g = g_ref[...]
        hn = (xhat * g).astype(BF16)
        dproj = jnp.concatenate([du_ref[0].astype(BF16), dzs_ref[...].astype(BF16), dq_ref[...].astype(BF16),
                                 dk_ref[...].astype(BF16), dv_ref[...].astype(BF16), dza_ref[...].astype(BF16)],
                                axis=-1)
        dhn = _mm(dproj, w_ref[...])
        dxhat = dhn * g
        gx_ref[...] = dh1_ref[...] + r * (dxhat - xhat * jnp.mean(dxhat * xhat, axis=-1, keepdims=True))
        @pl.when(i == 0)
        def _():
            dw_ref[...] = jnp.zeros((D_IN, D_MODEL), F32)
            dg_ref[...] = jnp.zeros((1, D_MODEL), F32)

        dw_ref[...] += _mm_tn(dproj, hn)
        dg_ref[...] += jnp.sum(dhn * xhat, axis=0, keepdims=True)

        @pl.when(i == t // tm - 1)
        def _():
            def to16(r):
                dw16_ref[r, :] = dw_ref[r, :].astype(BF16)

            _row_chunks(D_IN, to16)

    row = lambda w: pl.BlockSpec((tm, w), lambda i: (i, 0))
    perm = pl.BlockSpec((1, tm, D_SSM), lambda i: (i // N_SEG, 0, i % N_SEG))
    return pl.pallas_call(
        body, name="in_backward", grid=(t // tm,),
        in_specs=[row(D_MODEL), row(D_MODEL), perm, row(512), row(512), row(128), row(128), row(512),
                  _const_spec((1, D_MODEL)), _const_spec((D_IN, D_MODEL))],
        out_specs=(row(D_MODEL), _const_spec((D_IN, D_MODEL)), _const_spec((1, D_MODEL)),
                   _const_spec((D_IN, D_MODEL))),
        out_shape=(jax.ShapeDtypeStruct((t, D_MODEL), F32), jax.ShapeDtypeStruct((D_IN, D_MODEL), F32),
                   jax.ShapeDtypeStruct((1, D_MODEL), F32), jax.ShapeDtypeStruct((D_IN, D_MODEL), BF16)),
        compiler_params=_tc_params(("arbitrary",)),
    )(x2, dh1, du_perm, dz_ssm, dq, dk, dv, dz_attn, g_pre, w_in)


def _block_diag(t):
    a, b = t.shape[1], t.shape[2]
    eye = jnp.eye(G_TILE, dtype=t.dtype)
    t = t.reshape(N_GT, G_TILE, a, 1, b) * eye[None, :, None, :, None]
    return t.reshape(N_GT, G_TILE * a, G_TILE * b)


def _diag_blocks(m, a, b):
    m = m.reshape(N_GT, G_TILE, a, G_TILE, b)
    return jnp.einsum("tgagb->tgab", m).reshape(SSM_G, a, b)


def _local_step(x, p, target, pre_norm_g, w_in, ssm_lam_re, ssm_lam_im, ssm_log_step, ssm_b_re, ssm_b_im, ssm_c_re,
                ssm_c_im, ssm_d, ssm_b_glu, attn_sinks, post_norm_g, pl_b_gate, late):
    bl, seq, _ = x.shape
    seg = seq // N_SEG
    nb = seq // ATT_BLOCK
    t = bl * seq
    x2 = x.reshape(t, D_MODEL)
    p2 = p.reshape(t, D_PLE)
    tg2 = target.reshape(t, D_MODEL)

    lam_re, lam_im = ssm_lam_re, ssm_lam_im
    log_step = ssm_log_step.reshape(SSM_G, 1)
    a_re_row, a_im_row, bb_re, bb_im, pw_re, pw_im = _ssm_prep(lam_re, lam_im, log_step, ssm_b_re, ssm_b_im, seg)
    by_group = lambda t: t.reshape(SSM_G, SSM_P, SSM_N)
    bcat = jnp.concatenate([_block_diag(by_group(bb_re)), _block_diag(by_group(bb_im))], axis=-1).astype(BF16)
    ccat_t = jnp.concatenate([_block_diag(by_group(ssm_c_re)), -_block_diag(by_group(ssm_c_im))],
                             axis=-1).astype(BF16)
    bcat_t = jnp.swapaxes(bcat, 1, 2)
    ccat = jnp.swapaxes(ccat_t, 1, 2)
    d_row = ssm_d.reshape(1, D_SSM)

    u_perm, z_ssm, q, k, v, z_attn = _in_proj(x2, pre_norm_g.reshape(1, D_MODEL), w_in, bl, seg)
    u_perm = u_perm.reshape(bl, seq, D_SSM)
    (y_perm, states, carries), gathered = _ssm_forward(
        u_perm, bcat, ccat, a_re_row, a_im_row, pw_re, pw_im, d_row, late, seg)
    w_out, w_gate, w_proj, w_glu = (_gathered_to_full(n, g) for n, g in zip(LATE_NAMES, gathered))
    sinks = attn_sinks.reshape(N_HEADS)
    attn, lse = _attn_forward(q, k, v, sinks, bl, nb)
    (loss, dh1, dy_perm, dz_ssm, dattn, dz_attn, d_w_glu, d_b_glu, d_w_out, d_g_post, d_w_gate, d_b_gate,
     d_w_proj, *late16) = _mix_forward_backward(
        x2, y_perm.reshape(bl, seg, N_SEG * D_SSM), z_ssm, attn, z_attn, p2, tg2, w_glu,
        ssm_b_glu.reshape(1, D_SSM), w_out, post_norm_g.reshape(1, D_MODEL), w_gate, pl_b_gate.reshape(1, D_MODEL),
        w_proj, bl, seg)
    owned = lambda ds: [_full_to_owned(n, d) for n, d in zip(LATE_NAMES, ds)]
    dq, dk, dv, d_sinks = _attn_backward(q, k, v, attn, dattn, lse, sinks, bl, nb)
    (du_perm, d_bcat, d_ccat_t, da_re, da_im, d_d), late_grads = _ssm_backward(
        u_perm, dy_perm.reshape(bl, seq, D_SSM), states, carries, bcat_t, ccat_t, a_re_row, a_im_row, pw_re, pw_im,
        d_row, owned(late16), owned((d_w_out, d_w_gate, d_w_proj, d_w_glu)), seg)
    grad_x, d_w_in, d_g_pre, d_w_in16 = _in_backward(
        x2, dh1, du_perm.reshape(bl, seg, N_SEG * D_SSM), dz_ssm, dq, dk, dv, dz_attn,
        pre_norm_g.reshape(1, D_MODEL), w_in, bl, seg)
    flat = lambda t: t.reshape(SSM_G * SSM_P, SSM_N)
    d_lam_re, d_lam_im, d_ls, d_b_re, d_b_im = _ssm_param_grads(
        lam_re, lam_im, log_step, ssm_b_re, ssm_b_im, da_re.reshape(SSM_G, SSM_N), da_im.reshape(SSM_G, SSM_N),
        flat(_diag_blocks(d_bcat[:, :, 0:ST_T], SSM_P, SSM_N)), flat(_diag_blocks(d_bcat[:, :, ST_T:], SSM_P, SSM_N)))
    grads = {
        "pre_norm_g": d_g_pre, "w_in": d_w_in, "w_in16": d_w_in16, "ssm_lam_re": d_lam_re, "ssm_lam_im": d_lam_im,
        "ssm_log_step": d_ls, "ssm_b_re": d_b_re, "ssm_b_im": d_b_im,
        "ssm_c_re": _diag_blocks(d_ccat_t[:, :, 0:ST_T], SSM_P, SSM_N),
        "ssm_c_im": -_diag_blocks(d_ccat_t[:, :, ST_T:], SSM_P, SSM_N),
        "ssm_d": d_d, "ssm_b_glu": d_b_glu, "attn_sinks": d_sinks, "post_norm_g": d_g_post, "pl_b_gate": d_b_gate,
    }
    return loss, grad_x.reshape(bl, seq, D_MODEL), grads, late_grads


LATE_NAMES = ("w_out", "pl_w_gate", "pl_w_proj", "ssm_w_glu")
BIG_NAMES = ("w_in",) + LATE_NAMES
COL_SHARDED = {"w_in": D_IN // N_DEV, "pl_w_proj": D_MODEL // N_DEV}
WEIGHT_NAMES = ("pre_norm_g", "w_in", "ssm_lam_re", "ssm_lam_im", "ssm_log_step", "ssm_b_re", "ssm_b_im", "ssm_c_re",
                "ssm_c_im", "ssm_d", "ssm_w_glu", "ssm_b_glu", "attn_sinks", "w_out", "post_norm_g", "pl_w_proj",
                "pl_w_gate", "pl_b_gate")


TRANSPOSED = {"w_in": (0, 1), "ssm_b_re": (1, 2), "ssm_b_im": (1, 2)}


def _kernel_form(name, a):
    a = a[0]
    if name in TRANSPOSED:
        a = jnp.swapaxes(a, *TRANSPOSED[name])
    if name in ("ssm_b_re", "ssm_b_im", "ssm_c_re", "ssm_c_im"):
        a = a.reshape(SSM_G * SSM_P, SSM_N)
    return a


def _given_form(name, a, shape):
    if name in TRANSPOSED:
        i, j = TRANSPOSED[name]
        swapped = list(shape[1:])
        swapped[i], swapped[j] = swapped[j], swapped[i]
        return jnp.swapaxes(a.reshape(swapped), i, j).reshape(shape)
    return a.reshape(shape)


def _gathered_to_full(name, g):
    _, rows, cols = g.shape
    if name in COL_SHARDED:
        return jnp.swapaxes(g, 0, 1).reshape(rows, N_DEV * cols)
    return g.reshape(N_DEV * rows, cols)


def _full_to_owned(name, full):
    if name in COL_SHARDED:
        return jnp.swapaxes(full.reshape(full.shape[0], N_DEV, COL_SHARDED[name]), 0, 1)
    return full.reshape(N_DEV, full.shape[0] // N_DEV, full.shape[1])


def kernel(x, p, pre_norm_g, w_in, ssm_lam_re, ssm_lam_im, ssm_log_step, ssm_b_re, ssm_b_im, ssm_c_re, ssm_c_im, ssm_d, ssm_w_glu, ssm_b_glu, attn_sinks, w_out, post_norm_g, pl_w_proj, pl_w_gate, pl_b_gate, loss_target, m_pre_norm_g, m_w_in, m_ssm_lam_re, m_ssm_lam_im, m_ssm_log_step, m_ssm_b_re, m_ssm_b_im, m_ssm_c_re, m_ssm_c_im, m_ssm_d, m_ssm_w_glu, m_ssm_b_glu, m_attn_sinks, m_w_out, m_post_norm_g, m_pl_w_proj, m_pl_w_gate, m_pl_b_gate, v_pre_norm_g, v_w_in, v_ssm_lam_re, v_ssm_lam_im, v_ssm_log_step, v_ssm_b_re, v_ssm_b_im, v_ssm_c_re, v_ssm_c_im, v_ssm_d, v_ssm_w_glu, v_ssm_b_glu, v_attn_sinks, v_w_out, v_post_norm_g, v_pl_w_proj, v_pl_w_gate, v_pl_b_gate):
    w = dict(pre_norm_g=pre_norm_g, w_in=w_in, ssm_lam_re=ssm_lam_re, ssm_lam_im=ssm_lam_im, ssm_log_step=ssm_log_step,
             ssm_b_re=ssm_b_re, ssm_b_im=ssm_b_im, ssm_c_re=ssm_c_re, ssm_c_im=ssm_c_im, ssm_d=ssm_d, ssm_w_glu=ssm_w_glu,
             ssm_b_glu=ssm_b_glu, attn_sinks=attn_sinks, w_out=w_out, post_norm_g=post_norm_g, pl_w_proj=pl_w_proj,
             pl_w_gate=pl_w_gate, pl_b_gate=pl_b_gate)
    m = dict(pre_norm_g=m_pre_norm_g, w_in=m_w_in, ssm_lam_re=m_ssm_lam_re, ssm_lam_im=m_ssm_lam_im,
             ssm_log_step=m_ssm_log_step, ssm_b_re=m_ssm_b_re, ssm_b_im=m_ssm_b_im, ssm_c_re=m_ssm_c_re,
             ssm_c_im=m_ssm_c_im, ssm_d=m_ssm_d, ssm_w_glu=m_ssm_w_glu, ssm_b_glu=m_ssm_b_glu, attn_sinks=m_attn_sinks,
             w_out=m_w_out, post_norm_g=m_post_norm_g, pl_w_proj=m_pl_w_proj, pl_w_gate=m_pl_w_gate,
             pl_b_gate=m_pl_b_gate)
    v = dict(pre_norm_g=v_pre_norm_g, w_in=v_w_in, ssm_lam_re=v_ssm_lam_re, ssm_lam_im=v_ssm_lam_im,
             ssm_log_step=v_ssm_log_step, ssm_b_re=v_ssm_b_re, ssm_b_im=v_ssm_b_im, ssm_c_re=v_ssm_c_re,
             ssm_c_im=v_ssm_c_im, ssm_d=v_ssm_d, ssm_w_glu=v_ssm_w_glu, ssm_b_glu=v_ssm_b_glu, attn_sinks=v_attn_sinks,
             w_out=v_w_out, post_norm_g=v_post_norm_g, pl_w_proj=v_pl_w_proj, pl_w_gate=v_pl_w_gate,
             pl_b_gate=v_pl_b_gate)
    kf = lambda d: {n: _kernel_form(n, a) for n, a in d.items()}
    wk, mk, vk = kf(w), kf(m), kf(v)

    (gathered,) = _allgather_weights([wk["w_in"]])
    loss, grad_x, grads, g_late = _local_step(
        x, p[0], loss_target, wk["pre_norm_g"], gathered.reshape(D_IN, D_MODEL), wk["ssm_lam_re"], wk["ssm_lam_im"],
        wk["ssm_log_step"], wk["ssm_b_re"], wk["ssm_b_im"], wk["ssm_c_re"], wk["ssm_c_im"], wk["ssm_d"],
        wk["ssm_b_glu"], wk["attn_sinks"], wk["post_norm_g"], wk["pl_b_gate"], [wk[n] for n in LATE_NAMES])

    owned = lambda g: g.reshape(N_DEV, D_IN // N_DEV, D_MODEL)
    tiny_form = lambda d: [d[n].reshape(rows, cols) for n, rows, cols in TINY]
    med_form = lambda d: [d[n].reshape(N_DEV, rows // N_DEV, cols) for n, rows, cols in MEDIUM]
    g_big, loss, g_tiny, g_med = _reduce_final(
        [owned(grads["w_in16"])], [owned(grads["w_in"])], loss, tiny_form(grads), med_form(grads))
    names = BIG_NAMES + tuple(n for n, _, _ in TINY + MEDIUM)
    form = lambda d: [d[n] for n in BIG_NAMES] + tiny_form(d) + med_form(d)
    updated = _adamw_update(g_big + g_late + g_tiny + g_med, form(wk), form(mk), form(vk))
    vals = dict(zip(names, updated))
    results = [[_given_form(n, vals[n][kind], w[n].shape) for n in WEIGHT_NAMES] for kind in range(4)]
    return (loss.reshape(()), grad_x, *results[0], *results[1], *results[2], *results[3])
```

```python
import functools
import math

import jax
import jax.numpy as jnp
from jax import lax
from jax.experimental import pallas as pl
from jax.experimental.pallas import tpu as pltpu

F32 = jnp.float32
BF16 = jnp.bfloat16

D_MODEL = 1024
D_SSM = 512
D_ATTN = 512
SSM_P = 16
SSM_G = 32
SSM_N = 64
N_HEADS = 8
KV_HEADS = 2
Q_PER_KV = 4
HEAD_DIM = 64
ATT_BLOCK = 128
D_PLE = 256
D_IN = 2304
EPS = 1e-6
N_DEV = 8
N_SEG = 8
G_TILE = 8
N_GT = SSM_G // G_TILE
CH_T = G_TILE * SSM_P
ST_T = G_TILE * SSM_N
N_STATE = SSM_G * SSM_N
SCAN_UNROLL = 4
MIX_GROUPS = 1
LANES = 128
VMEM_LIMIT = 60 * 1024 * 1024

ADAM_LR = 0.001
ADAM_B1 = 0.9
ADAM_B2 = 0.999
ADAM_EPS = 1e-08
ADAM_WD = 0.01
ADAM_STEP = 10

GELU_C = math.sqrt(2.0 / math.pi)
GELU_K = 0.044715
ATT_SCALE = 1.0 / math.sqrt(HEAD_DIM)
NEG_BIG = -1e30


def _mm(a, b):
    return jnp.dot(a.astype(BF16), b.astype(BF16), preferred_element_type=F32)


def _mm_nt(a, b):
    return lax.dot_general(a.astype(BF16), b.astype(BF16), (((1,), (1,)), ((), ())), preferred_element_type=F32)


def _mm_tn(a, b):
    return lax.dot_general(a.astype(BF16), b.astype(BF16), (((0,), (0,)), ((), ())), preferred_element_type=F32)


def _sigmoid(x):
    return 1.0 / (1.0 + jnp.exp(-x))


def _tc_params(sem):
    return pltpu.CompilerParams(dimension_semantics=sem, vmem_limit_bytes=VMEM_LIMIT)


def _const_spec(shape):
    nd = len(shape)
    return pl.BlockSpec(shape, lambda *_: (0,) * nd)


def _mesh_pos():
    return lax.axis_index("x"), lax.axis_index("y"), lax.axis_index("c")


ROW_CHUNKS = (64, 32, 16)


def _row_chunk(nrows):
    return next((c for c in ROW_CHUNKS if nrows % c == 0), None)


def _row_chunks(nrows, fn, chunk=None, init=None):
    chunk = chunk or _row_chunk(nrows)

    def step(i, carry):
        rows = pl.ds(pl.multiple_of(i * chunk, chunk), chunk)
        if init is None:
            fn(rows)
            return carry
        return fn(rows, carry)

    return lax.fori_loop(0, nrows // chunk, step, 0 if init is None else init)


def _slot(px, py, pc):
    return 4 * px + 2 * py + pc


def _allgather_weights(shards):
    n = len(shards)

    def body(*refs):
        srcs, outs, (send_sems, recv_sems) = refs[:n], refs[n:2 * n], refs[2 * n:]
        x, y, c = _mesh_pos()
        me, sibling = (x, y, c), (x, y, 1 - c)
        chips = [(1 - x, y), (x, 1 - y), (1 - x, 1 - y)]

        def copy(a, k, block, to):
            blk = outs[a].at[_slot(*block)]
            return pltpu.make_async_remote_copy(
                src_ref=blk, dst_ref=blk, send_sem=send_sems.at[7 * a + k], recv_sem=recv_sems.at[7 * a + k],
                device_id=to, device_id_type=pl.DeviceIdType.MESH)

        sends = []
        for a in range(n):
            mine = outs[a].at[_slot(*me)]

            def cast(r, mine=mine, src=srcs[a]):
                mine[r, :] = src[r, :].astype(BF16)

            _row_chunks(srcs[a].shape[0], cast)
            first = [copy(a, 0, me, sibling)] + [copy(a, 1 + j, me, (*chip, c)) for j, chip in enumerate(chips)]
            for cp in first:
                cp.start()
            sends += first
        for a in range(n):
            for j, chip in enumerate(chips):
                copy(a, 1 + j, (*chip, c), me).wait_recv()
                fwd = copy(a, 4 + j, (*chip, c), sibling)
                fwd.start()
                sends.append(fwd)
        for a in range(n):
            copy(a, 0, sibling, me).wait_recv()
            for j, chip in enumerate(chips):
                copy(a, 4 + j, (*chip, 1 - c), me).wait_recv()
        for cp in sends:
            cp.wait_send()

    vm = pl.BlockSpec(memory_space=pltpu.VMEM)
    return pl.pallas_call(
        body, name="allgather_weights",
        out_shape=tuple(jax.ShapeDtypeStruct((N_DEV,) + s.shape, BF16) for s in shards),
        in_specs=[vm] * n, out_specs=(vm,) * n,
        scratch_shapes=[pltpu.SemaphoreType.DMA((7 * n,)), pltpu.SemaphoreType.DMA((7 * n,))],
        compiler_params=pltpu.CompilerParams(vmem_limit_bytes=VMEM_LIMIT),
    )(*shards)


def _adamw(w, g, m, v):
    m = ADAM_B1 * m + (1.0 - ADAM_B1) * g
    v = ADAM_B2 * v + (1.0 - ADAM_B2) * (g * g)
    m_hat = m / (1.0 - ADAM_B1 ** ADAM_STEP)
    v_hat = v / (1.0 - ADAM_B2 ** ADAM_STEP)
    delta = -ADAM_LR * (m_hat / (jnp.sqrt(v_hat) + ADAM_EPS) + ADAM_WD * w)
    return delta, m, v


def _remote(src, dst, send_sems, recv_sems, k, to):
    return pltpu.make_async_remote_copy(src_ref=src, dst_ref=dst, send_sem=send_sems.at[k], recv_sem=recv_sems.at[k],
                                        device_id=to, device_id_type=pl.DeviceIdType.MESH)


def _big_reduce_phases(g16_r, go_r, outs, send2, recv1, recv2, s_send, s_recv):
    n = len(g16_r)
    x, y, c = _mesh_pos()
    sibling = (x, y, 1 - c)
    chips = [(1 - x, y), (x, 1 - y), (1 - x, 1 - y)]
    all_chips = [(x, y)] + chips
    lvl1 = []
    for a in range(n):
        cps = [_remote(g16_r[a].at[_slot(*chip, 1 - c)], recv1[a].at[j], s_send, s_recv, 7 * a + j, sibling)
               for j, chip in enumerate(all_chips)]
        for cp in cps:
            cp.start()
        lvl1.append(cps)
    yield
    lvl2 = []
    for a in range(n):
        for cp in lvl1[a]:
            cp.wait_recv()
        og = outs[a]

        def partials(r, a=a, og=og):
            og[r, :] = go_r[a][r, :] + recv1[a][0, r, :].astype(F32)
            for j, chip in enumerate(chips):
                mine16 = g16_r[a][_slot(*chip, c), r, :].astype(F32)
                send2[a][j, r, :] = (mine16 + recv1[a][1 + j, r, :].astype(F32)).astype(BF16)

        _row_chunks(go_r[a].shape[0], partials)
        cps = [_remote(send2[a].at[j], recv2[a].at[j], s_send, s_recv, 7 * a + 4 + j, (*chip, c))
               for j, chip in enumerate(chips)]
        for cp in cps:
            cp.start()
        lvl2.append(cps)
    yield
    for a in range(n):
        for cp in lvl2[a]:
            cp.wait_recv()
        og = outs[a]

        def total(r, a=a, og=og):
            g = og[r, :]
            for j in range(3):
                g = g + recv2[a][j, r, :].astype(F32)
            og[r, :] = g

        _row_chunks(go_r[a].shape[0], total)
    yield
    for cps in lvl1 + lvl2:
        for cp in cps:
            cp.wait_send()


def _adamw_update(g, w, m, v):
    n = len(g)

    def body(*refs):
        g_r, w_r, m_r, v_r = (refs[i * n:(i + 1) * n] for i in range(4))
        outs = refs[4 * n:]
        for a in range(n):
            og, od, om, ov = outs[4 * a:4 * a + 4]

            def update(idx, a=a, og=og, od=od, om=om, ov=ov):
                gv = g_r[a][idx]
                d, nm, nv = _adamw(w_r[a][idx], gv, m_r[a][idx], v_r[a][idx])
                og[idx] = gv
                od[idx] = d
                om[idx] = nm
                ov[idx] = nv

            shape = g_r[a].shape
            if len(shape) == 3:
                for b in range(shape[0]):
                    update(b)
            elif _row_chunk(shape[0]) is not None:
                _row_chunks(shape[0], update)
            else:
                update(Ellipsis)

    vm = pl.BlockSpec(memory_space=pltpu.VMEM)
    res = pl.pallas_call(
        body, name="adamw_update",
        out_shape=tuple(jax.ShapeDtypeStruct(t.shape, F32) for t in g for _ in range(4)),
        in_specs=[vm] * (4 * n), out_specs=(vm,) * (4 * n),
        compiler_params=pltpu.CompilerParams(vmem_limit_bytes=VMEM_LIMIT),
    )(*g, *w, *m, *v)
    return [res[4 * a:4 * a + 4] for a in range(n)]


TINY = (("pre_norm_g", 1, 1024), ("post_norm_g", 1, 1024), ("pl_b_gate", 1, 1024), ("ssm_d", 1, 512),
        ("ssm_b_glu", 1, 512), ("ssm_log_step", 1, 32), ("attn_sinks", 1, 8), ("ssm_lam_re", 32, 64),
        ("ssm_lam_im", 32, 64))
MEDIUM = (("ssm_b_re", SSM_G * SSM_P, SSM_N), ("ssm_b_im", SSM_G * SSM_P, SSM_N), ("ssm_c_re", SSM_G * SSM_P, SSM_N),
          ("ssm_c_im", SSM_G * SSM_P, SSM_N))


def _stage_rows():
    offs, r = {}, 0
    for name, rows, cols in TINY + (("loss", 1, 1),):
        if rows > 1:
            r = -(-r // 8) * 8
        offs[name] = r
        r += rows if rows > 1 else max(cols // LANES, 1)
    return offs, -(-r // 8) * 8


def _reduce_final(g16, g32, loss, g_tiny, g_med):
    nb_, nt, nm_ = len(g16), len(TINY), len(MEDIUM)
    offs, stage_rows = _stage_rows()

    def body(*refs):
        g16_r, go_r = refs[:nb_], refs[nb_:2 * nb_]
        base = 2 * nb_
        loss_r, gt, gm = refs[base], refs[base + 1:base + 1 + nt], refs[base + 1 + nt:base + 1 + nt + nm_]
        base += 1 + nt + nm_
        out_b = refs[base:base + nb_]
        base += nb_
        loss_o, out_t, out_m = refs[base], refs[base + 1:base + 1 + nt], refs[base + 1 + nt:base + 1 + nt + nm_]
        base += 1 + nt + nm_
        send2_b, recv1_b, recv2_b = (refs[base + i * nb_:base + (i + 1) * nb_] for i in range(3))
        base += 3 * nb_
        stage = refs[base]
        recv1, part, recv2 = (refs[base + 1 + i * nm_:base + 1 + (i + 1) * nm_] for i in range(3))
        bs_send, bs_recv, s_send, s_recv, own_sems = refs[base + 1 + 3 * nm_:base + 6 + 3 * nm_]
        own32 = refs[base + 6 + 3 * nm_:]
        me = _slot(*_mesh_pos())
        fetch = [pltpu.make_async_copy(go_r[a].at[me], own32[a], own_sems.at[a]) for a in range(nb_)]
        for cp in fetch:
            cp.start()
        big = _big_reduce_phases(g16_r, own32, out_b, send2_b, recv1_b, recv2_b, bs_send, bs_recv)
        small = small_phases(loss_r, gt, gm, loss_o, out_t, out_m, stage, recv1, part, recv2, s_send, s_recv)
        next(big)
        next(small)
        for cp in fetch:
            cp.wait()
        next(big)
        for _ in small:
            pass
        for _ in big:
            pass

    def small_phases(loss_r, gt, gm, loss_o, out_t, out_m, stage, recv1, part, recv2, s_send, s_recv):
        x, y, c = _mesh_pos()
        me = _slot(x, y, c)
        sibling = (x, y, 1 - c)
        chips = [(1 - x, y), (x, 1 - y), (1 - x, 1 - y)]
        all_chips = [(x, y)] + chips
        peers = [sibling] + [(*chip, c) for chip in chips] + [(*chip, 1 - c) for chip in chips]
        sem = iter(range(7 + 14 * nm_))
        lvl1 = []
        for a in range(nm_):
            cps = [_remote(gm[a].at[_slot(*chip, 1 - c)], recv1[a].at[j], s_send, s_recv, next(sem), sibling)
                   for j, chip in enumerate(all_chips)]
            for cp in cps:
                cp.start()
            lvl1.append(cps)
        mine = stage.at[me]
        mine[...] = jnp.zeros((stage_rows, LANES), F32)
        for (name, rows, cols), ref in zip(TINY + (("loss", 1, 1),), gt + (loss_r,)):
            r0 = offs[name]
            if rows > 1:
                mine[r0:r0 + rows, 0:cols] = ref[...]
            elif cols >= LANES:
                for i in range(cols // LANES):
                    mine[r0 + i:r0 + i + 1, :] = ref[:, i * LANES:(i + 1) * LANES]
            else:
                mine[r0:r0 + 1, 0:cols] = ref[...]
        tiny_cps = [_remote(mine, mine, s_send, s_recv, next(sem), peer) for peer in peers]
        for cp in tiny_cps:
            cp.start()
        yield
        lvl2 = []
        for a in range(nm_):
            for cp in lvl1[a]:
                cp.wait_recv()
            for j, chip in enumerate(all_chips):
                part[a][j] = gm[a][_slot(*chip, c)] + recv1[a][j]
            cps = [_remote(part[a].at[1 + j], recv2[a].at[j], s_send, s_recv, next(sem), (*chip, c))
                   for j, chip in enumerate(chips)]
            for cp in cps:
                cp.start()
            lvl2.append(cps)
        yield
        lvl3 = []
        for a in range(nm_):
            for cp in lvl2[a]:
                cp.wait_recv()
            blk = out_m[a].at[me]
            blk[...] = ((part[a][0] + recv2[a][0]) + recv2[a][1]) + recv2[a][2]
            cps = [_remote(blk, blk, s_send, s_recv, next(sem), peer) for peer in peers]
            for cp in cps:
                cp.start()
            lvl3.append(cps)
        yield
        for cp in tiny_cps:
            cp.wait_recv()
        tot = stage[0]
        for d in range(1, N_DEV):
            tot = tot + stage[d]
        loss_o[...] = tot[offs["loss"]:offs["loss"] + 1, 0:1]
        for k, (name, rows, cols) in enumerate(TINY):
            r0 = offs[name]
            if rows > 1:
                out_t[k][...] = tot[r0:r0 + rows, 0:cols]
            elif cols >= LANES:
                for i in range(cols // LANES):
                    out_t[k][:, i * LANES:(i + 1) * LANES] = tot[r0 + i:r0 + i + 1, :]
            else:
                out_t[k][...] = tot[r0:r0 + 1, 0:cols]
        for cps in lvl3:
            for cp in cps:
                cp.wait_recv()
        for cps in lvl1 + lvl2 + lvl3 + [tiny_cps]:
            for cp in cps:
                cp.wait_send()

    vmem = pl.BlockSpec(memory_space=pltpu.VMEM)
    t_shapes = [jax.ShapeDtypeStruct((rows, cols), F32) for _, rows, cols in TINY]
    m_shapes = [jax.ShapeDtypeStruct((N_DEV, rows // N_DEV, cols), F32) for _, rows, cols in MEDIUM]
    blk = [(rows // N_DEV, cols) for _, rows, cols in MEDIUM]
    shard = [g.shape[1:] for g in g16]
    scratch = ([pltpu.VMEM((3,) + s, BF16) for s in shard] + [pltpu.VMEM((4,) + s, BF16) for s in shard]
               + [pltpu.VMEM((3,) + s, BF16) for s in shard]
               + [pltpu.VMEM((N_DEV, stage_rows, LANES), F32)]
               + [pltpu.VMEM((4,) + b, F32) for b in blk] + [pltpu.VMEM((4,) + b, F32) for b in blk]
               + [pltpu.VMEM((3,) + b, F32) for b in blk]
               + [pltpu.SemaphoreType.DMA((7 * nb_,)), pltpu.SemaphoreType.DMA((7 * nb_,)),
                  pltpu.SemaphoreType.DMA((7 + 14 * nm_,)), pltpu.SemaphoreType.DMA((7 + 14 * nm_,)),
                  pltpu.SemaphoreType.DMA((nb_,))]
               + [pltpu.VMEM(s, F32) for s in shard])
    n_out = nb_ + 1 + nt + nm_
    res = pl.pallas_call(
        body, name="reduce_final",
        out_shape=tuple(jax.ShapeDtypeStruct(s, F32) for s in shard) + (jax.ShapeDtypeStruct((1, 1), F32),)
        + tuple(t_shapes) + tuple(m_shapes),
        in_specs=[vmem] * nb_ + [pl.BlockSpec(memory_space=pl.ANY)] * nb_ + [vmem] * (1 + nt + nm_),
        out_specs=(vmem,) * n_out, scratch_shapes=scratch,
        compiler_params=pltpu.CompilerParams(vmem_limit_bytes=VMEM_LIMIT),
    )(*g16, *g32, loss, *g_tiny, *g_med)
    return list(res[:nb_]), res[nb_], list(res[nb_ + 1:nb_ + 1 + nt]), list(res[nb_ + 1 + nt:])


def _gather_phases(shard_r, gath, cast, send_sems, recv_sems, local_sems):
    n = len(shard_r)
    x, y, c = _mesh_pos()
    me, sibling = (x, y, c), (x, y, 1 - c)
    chips = [(1 - x, y), (x, 1 - y), (1 - x, 1 - y)]

    def own(a, k, to):
        return _remote(cast[a], gath[a].at[_slot(*me)], send_sems, recv_sems, 7 * a + k, to)

    def passed(a, k, block, to):
        blk = gath[a].at[_slot(*block)]
        return _remote(blk, blk, send_sems, recv_sems, 7 * a + k, to)

    def keep(a):
        return pltpu.make_async_copy(cast[a], gath[a].at[_slot(*me)], local_sems.at[a])

    def start():
        for a in range(n):
            def to16(r, a=a):
                cast[a][r, :] = shard_r[a][r, :].astype(BF16)

            _row_chunks(shard_r[a].shape[0], to16)
            keep(a).start()
            own(a, 0, sibling).start()
            for j, chip in enumerate(chips):
                own(a, 1 + j, (*chip, c)).start()

    def relay():
        for a in range(n):
            for j, chip in enumerate(chips):
                passed(a, 1 + j, (*chip, c), me).wait_recv()
                passed(a, 4 + j, (*chip, c), sibling).start()

    def finish():
        for a in range(n):
            passed(a, 0, sibling, me).wait_recv()
            for j, chip in enumerate(chips):
                passed(a, 4 + j, (*chip, 1 - c), me).wait_recv()
            own(a, 0, sibling).wait_send()
            for j, chip in enumerate(chips):
                own(a, 1 + j, (*chip, c)).wait_send()
                passed(a, 4 + j, (*chip, c), sibling).wait_send()
            keep(a).wait()

    return start, relay, finish


def _gather_operands(shards):
    n = len(shards)
    return ((pl.BlockSpec(memory_space=pl.ANY),) * n,
            tuple(jax.ShapeDtypeStruct((N_DEV,) + s.shape, BF16) for s in shards),
            [pltpu.VMEM(s.shape, BF16) for s in shards]
            + [pltpu.SemaphoreType.DMA((7 * n,)), pltpu.SemaphoreType.DMA((7 * n,)), pltpu.SemaphoreType.DMA((n,))])


def _hosted_reduce_phases(g16_r, g32_r, red, own16, recv1, send2, recv2, own32, s_send, s_recv, s_local):
    n = len(g16_r)
    x, y, c = _mesh_pos()
    sibling = (x, y, 1 - c)
    chips = [(1 - x, y), (x, 1 - y), (1 - x, 1 - y)]
    all_chips = [(x, y)] + chips

    def lvl1(a, j):
        return _remote(g16_r[a].at[_slot(*all_chips[j], 1 - c)], recv1[a].at[j], s_send, s_recv, 7 * a + j, sibling)

    def lvl2(a, j):
        return _remote(send2[a].at[j], recv2[a].at[j], s_send, s_recv, 7 * a + 4 + j, (*chips[j], c))

    def mine(a, j):
        if j == 3:
            return pltpu.make_async_copy(g32_r[a].at[_slot(x, y, c)], own32[a], s_local.at[4 * a + j])
        return pltpu.make_async_copy(g16_r[a].at[_slot(*chips[j], c)], own16[a].at[j], s_local.at[4 * a + j])

    def start():
        for a in range(n):
            for j in range(4):
                mine(a, j).start()
            for j in range(4):
                lvl1(a, j).start()

    def middle():
        for a in range(n):
            for j in range(4):
                mine(a, j).wait()
            for j in range(4):
                lvl1(a, j).wait_recv()

            def partials(r, a=a):
                red[a][r, :] = own32[a][r, :] + recv1[a][0, r, :].astype(F32)
                for j in range(3):
                    send2[a][j, r, :] = (own16[a][j, r, :].astype(F32) + recv1[a][1 + j, r, :].astype(F32)).astype(BF16)

            _row_chunks(own32[a].shape[0], partials)
            for j in range(3):
                lvl2(a, j).start()

    def total():
        for a in range(n):
            for j in range(3):
                lvl2(a, j).wait_recv()

            def add(r, a=a):
                g = red[a][r, :]
                for j in range(3):
                    g = g + recv2[a][j, r, :].astype(F32)
                red[a][r, :] = g

            _row_chunks(own32[a].shape[0], add)

    def finish():
        for a in range(n):
            for j in range(4):
                lvl1(a, j).wait_send()
            for j in range(3):
                lvl2(a, j).wait_send()

    return start, middle, total, finish


def _hosted_reduce_operands(g16, const_spec):
    n = len(g16)
    shard = [g.shape[1:] for g in g16]
    return ([pl.BlockSpec(memory_space=pl.ANY)] * (2 * n),
            tuple(const_spec(s) for s in shard),
            tuple(jax.ShapeDtypeStruct(s, F32) for s in shard),
            [pltpu.VMEM((3,) + s, BF16) for s in shard] + [pltpu.VMEM((4,) + s, BF16) for s in shard]
            + [pltpu.VMEM((3,) + s, BF16) for s in shard] + [pltpu.VMEM((3,) + s, BF16) for s in shard]
            + [pltpu.VMEM(s, F32) for s in shard]
            + [pltpu.SemaphoreType.DMA((7 * n,)), pltpu.SemaphoreType.DMA((7 * n,)), pltpu.SemaphoreType.DMA((4 * n,))])


def _in_proj(x2, g_pre, w_in, bl, seg):
    t = x2.shape[0]
    tm = seg

    def body(x_ref, g_ref, w_ref, u_ref, zs_ref, q_ref, k_ref, v_ref, za_ref):
        xv = x_ref[...]
        r = lax.rsqrt(jnp.mean(xv * xv, axis=-1, keepdims=True) + EPS)
        hn = xv * r * g_ref[...]
        proj = _mm_nt(hn, w_ref[...])
        u_ref[0] = proj[:, 0:512]
        zs_ref[...] = proj[:, 512:1024]
        q_ref[...] = proj[:, 1024:1536].astype(BF16)
        k_ref[...] = proj[:, 1536:1664].astype(BF16)
        v_ref[...] = proj[:, 1664:1792].astype(BF16)
        za_ref[...] = proj[:, 1792:2304]

    row = lambda w: pl.BlockSpec((tm, w), lambda i: (i, 0))
    return pl.pallas_call(
        body, name="in_proj", grid=(t // tm,),
        in_specs=[row(D_MODEL), _const_spec((1, D_MODEL)), _const_spec((D_IN, D_MODEL))],
        out_specs=(pl.BlockSpec((1, tm, D_SSM), lambda i: (i // N_SEG, 0, i % N_SEG)),
                   row(512), row(512), row(128), row(128), row(512)),
        out_shape=(jax.ShapeDtypeStruct((bl, seg, N_SEG * D_SSM), F32),
                   jax.ShapeDtypeStruct((t, 512), F32), jax.ShapeDtypeStruct((t, 512), BF16),
                   jax.ShapeDtypeStruct((t, 128), BF16), jax.ShapeDtypeStruct((t, 128), BF16),
                   jax.ShapeDtypeStruct((t, 512), F32)),
        compiler_params=_tc_params(("arbitrary",)),
    )(x2, g_pre, w_in)


def _discretise(lr, li, ls):
    step = jnp.exp(ls)
    mag = jnp.exp(lr * step)
    ar = mag * jnp.cos(li * step)
    ai = mag * jnp.sin(li * step)
    den = lr * lr + li * li
    cr = ((ar - 1.0) * lr + ai * li) / den
    ci = (ai * lr - (ar - 1.0) * li) / den
    return step, ar, ai, den, cr, ci


def _per_channel(v):
    return jnp.broadcast_to(v[:, None, :], (SSM_G, SSM_P, SSM_N)).reshape(SSM_G * SSM_P, SSM_N)


def _tile_masks():
    r = lax.broadcasted_iota(jnp.int32, (CH_T, ST_T), 0) // SSM_P
    l = lax.broadcasted_iota(jnp.int32, (CH_T, ST_T), 1) // SSM_N
    lt = lax.broadcasted_iota(jnp.int32, (ST_T, CH_T), 0) // SSM_N
    rt = lax.broadcasted_iota(jnp.int32, (ST_T, CH_T), 1) // SSM_P
    rep = lax.broadcasted_iota(jnp.int32, (SSM_N, ST_T), 0) == lax.broadcasted_iota(jnp.int32, (SSM_N, ST_T), 1) % SSM_N
    rep_t = lax.broadcasted_iota(jnp.int32, (ST_T, SSM_N), 0) % SSM_N == lax.broadcasted_iota(jnp.int32, (ST_T, SSM_N), 1)
    return r == l, lt == rt, rep, rep_t


def _ssm_prep(lam_re, lam_im, log_step, b_re, b_im, c_re, c_im, seg):
    def body(lr_ref, li_ref, ls_ref, br_ref, bi_ref, cre_ref, cim_ref, lrr_ref, lir_ref, lsr_ref,
             ar_ref, ai_ref, pr_ref, pi_ref, bcat_ref, bcat_t_ref, ccat_ref, ccat_t_ref):
        _, _, _, _, cr, ci = _discretise(lr_ref[...], li_ref[...], ls_ref[...])
        cr, ci = _per_channel(cr), _per_channel(ci)
        br, bi = br_ref[...], bi_ref[...]
        bb_re = cr * br - ci * bi
        bb_im = cr * bi + ci * br
        same, same_t, rep, rep_t = _tile_masks()
        rep, rep_t = rep.astype(BF16), rep_t.astype(BF16)
        for j in range(N_GT):
            rows = slice(j * CH_T, (j + 1) * CH_T)
            for wide, tall, parts in ((bcat_ref, bcat_t_ref, (bb_re[rows], bb_im[rows])),
                                      (ccat_t_ref, ccat_ref, (cre_ref[rows, :], -cim_ref[rows, :]))):
                for k, part in enumerate(parts):
                    p16 = part.astype(BF16)
                    wide[j, :, k * ST_T:(k + 1) * ST_T] = jnp.where(same, _mm(p16, rep), 0.0).astype(BF16)
                    tall[j, k * ST_T:(k + 1) * ST_T, :] = jnp.where(same_t, _mm_nt(rep_t, p16), 0.0).astype(BF16)
        stepr = jnp.exp(lsr_ref[...])
        k = (lax.broadcasted_iota(jnp.int32, (8, N_STATE), 0) + 1).astype(F32)
        magk = jnp.exp(k * (lrr_ref[...] * stepr))
        ang = k * (lir_ref[...] * stepr)
        pr_ref[0:8, :] = magk * jnp.cos(ang)
        pi_ref[0:8, :] = magk * jnp.sin(ang)
        n = 8
        while n < seg:
            tr, ti = pr_ref[n - 1:n, :], pi_ref[n - 1:n, :]
            xr, xi = pr_ref[0:n, :], pi_ref[0:n, :]
            pr_ref[n:2 * n, :] = xr * tr - xi * ti
            pi_ref[n:2 * n, :] = xr * ti + xi * tr
            n *= 2
        ar_ref[...] = pr_ref[0:1, :]
        ai_ref[...] = pi_ref[0:1, :]

    row = jax.ShapeDtypeStruct((1, N_STATE), F32)
    pw = jax.ShapeDtypeStruct((seg, N_STATE), F32)
    wide = jax.ShapeDtypeStruct((N_GT, CH_T, 2 * ST_T), BF16)
    tall = jax.ShapeDtypeStruct((N_GT, 2 * ST_T, CH_T), BF16)
    vm = pl.BlockSpec(memory_space=pltpu.VMEM)
    step_row = jnp.broadcast_to(log_step, (SSM_G, SSM_N)).reshape(1, N_STATE)
    return pl.pallas_call(
        body, name="ssm_prep", out_shape=(row, row, pw, pw, wide, tall, tall, wide),
        in_specs=[vm] * 10, out_specs=(vm,) * 8,
    )(lam_re, lam_im, log_step, b_re, b_im, c_re, c_im, lam_re.reshape(1, N_STATE), lam_im.reshape(1, N_STATE),
      step_row)


def _seg_rows(t):
    if isinstance(t, int):
        return pl.ds(t * N_SEG, N_SEG)
    return pl.ds(pl.multiple_of(t * N_SEG, N_SEG), N_SEG)


def _scan_forward(xs, a_re, a_im, pw_re, pw_im, cs, seg):
    are = jnp.broadcast_to(a_re, (N_SEG, ST_T))
    aim = jnp.broadcast_to(a_im, (N_SEG, ST_T))

    def steps(k, carry):
        xr, xi = carry
        for j in range(SCAN_UNROLL):
            r = pl.multiple_of((k * SCAN_UNROLL + j) * N_SEG, N_SEG)
            nr = are * xr - aim * xi + xs[pl.ds(r, N_SEG), 0:ST_T]
            ni = are * xi + aim * xr + xs[pl.ds(r, N_SEG), ST_T:2 * ST_T]
            xs[pl.ds(r, N_SEG), 0:ST_T] = nr
            xs[pl.ds(r, N_SEG), ST_T:2 * ST_T] = ni
            xr, xi = nr, ni
        return xr, xi

    zero = jnp.zeros((N_SEG, ST_T), F32)
    fr, fi = lax.fori_loop(0, seg // SCAN_UNROLL, steps, (zero, zero))
    sr, si = pw_re[seg - 1:seg, :], pw_im[seg - 1:seg, :]
    cr = jnp.zeros((1, ST_T), F32)
    ci = jnp.zeros((1, ST_T), F32)
    cs[0:1, :] = cr
    cs[8:9, :] = ci
    for s in range(1, N_SEG):
        ncr = sr * cr - si * ci + fr[s - 1:s, :]
        nci = sr * ci + si * cr + fi[s - 1:s, :]
        cr, ci = ncr, nci
        cs[s:s + 1, :] = cr
        cs[8 + s:9 + s, :] = ci
    car, cai = cs[0:8, :], cs[8:16, :]

    def fix(t, _):
        r = pl.multiple_of(t * N_SEG, N_SEG)
        pr, pi = pw_re[pl.ds(t, 1), :], pw_im[pl.ds(t, 1), :]
        xs[pl.ds(r, N_SEG), 0:ST_T] = xs[pl.ds(r, N_SEG), 0:ST_T] + (pr * car - pi * cai)
        xs[pl.ds(r, N_SEG), ST_T:2 * ST_T] = xs[pl.ds(r, N_SEG), ST_T:2 * ST_T] + (pr * cai + pi * car)
        return 0

    lax.fori_loop(0, seg, fix, 0, unroll=SCAN_UNROLL)


def _ssm_forward(u_perm, bcat, ccat, a_re, a_im, pw_re, pw_im, d_row, late, seg):
    bl, rows, _ = u_perm.shape
    n = len(late)
    steps = bl * N_GT

    def body(*refs):
        u_ref, b_ref, c_ref, ar_ref, ai_ref, pr_ref, pi_ref, d_ref = refs[:8]
        late_r = refs[8:8 + n]
        y_ref, xs_ref, cs_ref = refs[8 + n:11 + n]
        gath, cast = refs[11 + n:11 + 2 * n], refs[11 + 2 * n:11 + 3 * n]
        send_sems, recv_sems, local_sems = refs[11 + 3 * n:]
        step = pl.program_id(0) * N_GT + pl.program_id(1)
        start, relay, finish = _gather_phases(late_r, gath, cast, send_sems, recv_sems, local_sems)
        pl.when(step == 0)(start)
        u = u_ref[0]
        xs, cs = xs_ref.at[0, 0], cs_ref.at[0, 0]
        xs[...] = _mm(u, b_ref[0])
        _scan_forward(xs, ar_ref[...], ai_ref[...], pr_ref, pi_ref, cs, seg)
        y_ref[0] = _mm(xs[...], c_ref[0]) + d_ref[...] * u
        pl.when(step == steps // 2)(relay)
        pl.when(step == steps - 1)(finish)

    state = lambda r, c: pl.BlockSpec((1, 1, r, c), lambda b, j: (b, j, 0, 0))
    g_specs, g_shapes, g_scratch = _gather_operands(late)
    res = pl.pallas_call(
        body, name="ssm_forward", grid=(bl, N_GT),
        in_specs=[pl.BlockSpec((1, rows, CH_T), lambda b, j: (b, 0, j)),
                  pl.BlockSpec((1, CH_T, 2 * ST_T), lambda b, j: (j, 0, 0)),
                  pl.BlockSpec((1, 2 * ST_T, CH_T), lambda b, j: (j, 0, 0)),
                  pl.BlockSpec((1, ST_T), lambda b, j: (0, j)), pl.BlockSpec((1, ST_T), lambda b, j: (0, j)),
                  pl.BlockSpec((seg, ST_T), lambda b, j: (0, j)), pl.BlockSpec((seg, ST_T), lambda b, j: (0, j)),
                  pl.BlockSpec((1, CH_T), lambda b, j: (0, j))]
        + [pl.BlockSpec(s.shape, lambda b, j: (0, 0)) for s in late],
        out_specs=(pl.BlockSpec((1, rows, CH_T), lambda b, j: (b, 0, j)), state(rows, 2 * ST_T), state(16, ST_T))
        + g_specs,
        out_shape=(jax.ShapeDtypeStruct((bl, rows, D_SSM), F32),
                   jax.ShapeDtypeStruct((bl, N_GT, rows, 2 * ST_T), F32),
                   jax.ShapeDtypeStruct((bl, N_GT, 16, ST_T), F32)) + g_shapes,
        scratch_shapes=g_scratch,
        compiler_params=_tc_params(("arbitrary", "arbitrary")),
    )(u_perm, bcat, ccat, a_re, a_im, pw_re, pw_im, d_row, *late)
    return res[:3], list(res[3:])


def _ssm_backward(u_perm, dy_perm, states, carries, bcat_t, ccat_t, a_re, a_im, pw_re, pw_im, d_row, late16, late32, seg):
    bl, rows, _ = u_perm.shape
    n = len(late16)
    grid_steps = N_GT * bl

    def body(*refs):
        u_ref, dy_ref, xs_ref, cs_ref, bt_ref, ct_ref, ar_ref, ai_ref, pr_ref, pi_ref, d_ref = refs[:11]
        g16_r, g32_r = refs[11:11 + n], refs[11 + n:11 + 2 * n]
        du_ref, db_ref, dc_ref, dar_ref, dai_ref, dd_ref = refs[11 + 2 * n:17 + 2 * n]
        red = refs[17 + 2 * n:17 + 3 * n]
        own16, recv1, send2, recv2, own32 = (refs[17 + 3 * n + k * n:17 + 3 * n + (k + 1) * n] for k in range(5))
        s_send, s_recv, s_local, ls, cl = refs[17 + 8 * n:]
        b = pl.program_id(1)
        step = pl.program_id(0) * bl + b
        start, middle, total, finish = _hosted_reduce_phases(g16_r, g32_r, red, own16, recv1, send2, recv2, own32,
                                                             s_send, s_recv, s_local)
        pl.when(step == 0)(start)
        pl.when(step == grid_steps // 4)(middle)
        pl.when(step == (grid_steps * 3) // 4)(total)
        pl.when(step == grid_steps - 1)(finish)
        u = u_ref[0]
        dy = dy_ref[0]
        xs, cs = xs_ref.at[0, 0], cs_ref.at[0, 0]
        ls[...] = _mm(dy, ct_ref[0])
        are = jnp.broadcast_to(ar_ref[...], (N_SEG, ST_T))
        aim = jnp.broadcast_to(ai_ref[...], (N_SEG, ST_T))

        def steps(k, carry):
            lr, li = carry
            for j in range(SCAN_UNROLL):
                r = pl.multiple_of((seg - 1 - (k * SCAN_UNROLL + j)) * N_SEG, N_SEG)
                nr = are * lr + aim * li + ls[pl.ds(r, N_SEG), 0:ST_T]
                ni = are * li - aim * lr + ls[pl.ds(r, N_SEG), ST_T:2 * ST_T]
                ls[pl.ds(r, N_SEG), 0:ST_T] = nr
                ls[pl.ds(r, N_SEG), ST_T:2 * ST_T] = ni
                lr, li = nr, ni
            return lr, li

        zero = jnp.zeros((N_SEG, ST_T), F32)
        fr, fi = lax.fori_loop(0, seg // SCAN_UNROLL, steps, (zero, zero))
        sr, si = pr_ref[seg - 1:seg, :], pi_ref[seg - 1:seg, :]
        cr = jnp.zeros((1, ST_T), F32)
        ci = jnp.zeros((1, ST_T), F32)
        cl[7:8, :] = cr
        cl[15:16, :] = ci
        for s in range(N_SEG - 2, -1, -1):
            ncr = sr * cr + si * ci + fr[s + 1:s + 2, :]
            nci = sr * ci - si * cr + fi[s + 1:s + 2, :]
            cr, ci = ncr, nci
            cl[s:s + 1, :] = cr
            cl[8 + s:9 + s, :] = ci
        clr, cli = cl[0:8, :], cl[8:16, :]

        def fix_rows(rows, t, xpr, xpi, acc):
            dr, di = acc
            pr, pi = pr_ref[pl.ds(seg - 1 - t, 1), :], pi_ref[pl.ds(seg - 1 - t, 1), :]
            lr = ls[rows, 0:ST_T] + (pr * clr + pi * cli)
            li = ls[rows, ST_T:2 * ST_T] + (pr * cli - pi * clr)
            ls[rows, 0:ST_T] = lr
            ls[rows, ST_T:2 * ST_T] = li
            return dr + (lr * xpr + li * xpi), di + (li * xpr - lr * xpi)

        def fix_at(t, acc):
            prev = _seg_rows(t - 1)
            return fix_rows(_seg_rows(t), t, xs[prev, 0:ST_T], xs[prev, ST_T:2 * ST_T], acc)

        def fix(k, acc):
            for j in range(SCAN_UNROLL):
                acc = fix_at(k * SCAN_UNROLL + j, acc)
            return acc

        acc = fix_rows(pl.ds(0, N_SEG), 0, cs[0:8, :], cs[8:16, :], (zero, zero))
        for t in range(1, SCAN_UNROLL):
            acc = fix_at(t, acc)
        dr, di = lax.fori_loop(1, seg // SCAN_UNROLL, fix, acc)
        dar = jnp.sum(dr, axis=0, keepdims=True)
        dai = jnp.sum(di, axis=0, keepdims=True)
        lall = ls[...]
        du_ref[0] = (_mm(lall, bt_ref[0]) + d_ref[...] * dy).astype(BF16)
        dbp = _mm_tn(u, lall)
        dcp = _mm_tn(dy, xs[...])
        ddp = jnp.sum(dy * u, axis=0, keepdims=True)

        @pl.when(b == 0)
        def _():
            db_ref[0] = dbp
            dc_ref[0] = dcp
            dar_ref[...] = dar
            dai_ref[...] = dai
            dd_ref[...] = ddp

        @pl.when(b != 0)
        def _():
            db_ref[0] += dbp
            dc_ref[0] += dcp
            dar_ref[...] += dar
            dai_ref[...] += dai
            dd_ref[...] += ddp

    tile3 = lambda r, c: pl.BlockSpec((1, r, c), lambda j, b: (j, 0, 0))
    lane = lambda r, c: pl.BlockSpec((r, c), lambda j, b: (0, j))
    act = pl.BlockSpec((1, rows, CH_T), lambda j, b: (b, 0, j))
    state = lambda r, c: pl.BlockSpec((1, 1, r, c), lambda j, b: (b, j, 0, 0))
    r_in, r_out, r_shapes, r_scratch = _hosted_reduce_operands(late16, lambda s: pl.BlockSpec(s, lambda j, b: (0, 0)))
    res = pl.pallas_call(
        body, name="ssm_backward", grid=(N_GT, bl),
        in_specs=[act, act, state(rows, 2 * ST_T), state(16, ST_T), tile3(2 * ST_T, CH_T), tile3(CH_T, 2 * ST_T),
                  lane(1, ST_T), lane(1, ST_T), lane(seg, ST_T), lane(seg, ST_T), lane(1, CH_T)] + r_in,
        out_specs=(act, tile3(CH_T, 2 * ST_T), tile3(CH_T, 2 * ST_T), lane(1, ST_T), lane(1, ST_T), lane(1, CH_T))
        + r_out,
        out_shape=(jax.ShapeDtypeStruct((bl, rows, D_SSM), BF16),
                   jax.ShapeDtypeStruct((N_GT, CH_T, 2 * ST_T), F32), jax.ShapeDtypeStruct((N_GT, CH_T, 2 * ST_T), F32),
                   jax.ShapeDtypeStruct((1, N_STATE), F32), jax.ShapeDtypeStruct((1, N_STATE), F32),
                   jax.ShapeDtypeStruct((1, D_SSM), F32)) + r_shapes,
        scratch_shapes=r_scratch + [pltpu.VMEM((rows, 2 * ST_T), F32), pltpu.VMEM((16, ST_T), F32)],
        compiler_params=_tc_params(("arbitrary", "arbitrary")),
    )(u_perm, dy_perm, states, carries, bcat_t, ccat_t, a_re, a_im, pw_re, pw_im, d_row, *late16, *late32)
    return res[:6], list(res[6:])


def _ssm_param_grads(lam_re, lam_im, log_step, b_re, b_im, da_re, da_im, d_bcat, d_ccat_t):
    def body(lr_ref, li_ref, ls_ref, br_ref, bi_ref, gar_ref, gai_ref, gbcat_ref, gccat_ref,
             dlr_ref, dli_ref, dls_ref, dbr_ref, dbi_ref, dcr_ref, dci_ref, gbr_s, gbi_s):
        same, _, _, rep_t = _tile_masks()
        rep_t = rep_t.astype(F32)
        for j in range(N_GT):
            rows = slice(j * CH_T, (j + 1) * CH_T)
            for src, dsts in ((gbcat_ref, (gbr_s, gbi_s)), (gccat_ref, (dcr_ref, dci_ref))):
                for k, dst in enumerate(dsts):
                    blk = jnp.where(same, src[j, :, k * ST_T:(k + 1) * ST_T], 0.0)
                    dst[rows, :] = jnp.dot(blk, rep_t, precision=lax.Precision.HIGHEST, preferred_element_type=F32)
        dci_ref[...] = -dci_ref[...]
        lr, li = lr_ref[...], li_ref[...]
        step, ar, ai, den, cr, ci = _discretise(lr, li, ls_ref[...])
        crb, cib = _per_channel(cr), _per_channel(ci)
        br, bi = br_ref[...], bi_ref[...]
        gbr, gbi = gbr_s[...], gbi_s[...]
        dbr_ref[...] = crb * gbr + cib * gbi
        dbi_ref[...] = crb * gbi - cib * gbr
        over_channels = lambda t: jnp.sum(t.reshape(SSM_G, SSM_P, SSM_N), axis=1)
        gcr = over_channels(br * gbr + bi * gbi)
        gci = over_channels(br * gbi - bi * gbr)
        ilr, ili = lr / den, -li / den
        gar = gar_ref[...] + (ilr * gcr + ili * gci)
        gai = gai_ref[...] + (ilr * gci - ili * gcr)
        qr, qi = cr * ilr - ci * ili, cr * ili + ci * ilr
        glr = -(qr * gcr + qi * gci)
        gli = -(qr * gci - qi * gcr)
        gwr = ar * gar + ai * gai
        gwi = ar * gai - ai * gar
        dlr_ref[...] = glr + step * gwr
        dli_ref[...] = gli + step * gwi
        dls_ref[...] = jnp.sum(lr * gwr + li * gwi, axis=-1, keepdims=True) * step

    lam = jax.ShapeDtypeStruct((SSM_G, SSM_N), F32)
    mat = jax.ShapeDtypeStruct((SSM_G * SSM_P, SSM_N), F32)
    vm = pl.BlockSpec(memory_space=pltpu.VMEM)
    return pl.pallas_call(
        body, name="ssm_param_grads", out_shape=(lam, lam, jax.ShapeDtypeStruct((SSM_G, 1), F32), mat, mat, mat, mat),
        in_specs=[vm] * 9, out_specs=(vm,) * 7,
        scratch_shapes=[pltpu.VMEM((SSM_G * SSM_P, SSM_N), F32), pltpu.VMEM((SSM_G * SSM_P, SSM_N), F32)],
    )(lam_re, lam_im, log_step, b_re, b_im, da_re, da_im, d_bcat, d_ccat_t)


ROWS4 = Q_PER_KV * ATT_BLOCK


def _att_dist_mask(first_block):
    qi = lax.broadcasted_iota(jnp.int32, (ROWS4, 2 * ATT_BLOCK), 0) & (ATT_BLOCK - 1)
    si = lax.broadcasted_iota(jnp.int32, (ROWS4, 2 * ATT_BLOCK), 1)
    dist = qi + ATT_BLOCK - si
    valid = (dist >= 0) & (dist < ATT_BLOCK) & ((si >= ATT_BLOCK) | jnp.logical_not(first_block))
    return dist.astype(F32), valid


def _stack_heads(x, kv):
    return jnp.concatenate([x[:, (kv * Q_PER_KV + g) * HEAD_DIM:(kv * Q_PER_KV + g + 1) * HEAD_DIM]
                            for g in range(Q_PER_KV)], axis=0)


def _stack_cols(x, kv):
    return jnp.concatenate([x[:, kv * Q_PER_KV + g:kv * Q_PER_KV + g + 1] for g in range(Q_PER_KV)], axis=0)


def _per_head_col(vals):
    return jnp.concatenate([jnp.full((ATT_BLOCK, 1), v, F32) for v in vals], axis=0)


def _attn_forward(q, k, v, sinks, bl, nb):
    t = q.shape[0]

    def body(sink_ref, q_ref, kp_ref, kc_ref, vp_ref, vc_ref, o_ref, lse_ref):
        i = pl.program_id(1)
        dist4, valid4 = _att_dist_mask(i == 0)
        dist, valid = dist4[0:ATT_BLOCK, :], valid4[0:ATT_BLOCK, :]
        kk = jnp.concatenate([kp_ref[...], kc_ref[...]], axis=0)
        vv = jnp.concatenate([vp_ref[...], vc_ref[...]], axis=0)
        qv = q_ref[...]
        for h in range(N_HEADS):
            kv = h // Q_PER_KV
            slope = 2.0 ** (-(h + 1))
            qh = qv[:, h * HEAD_DIM:(h + 1) * HEAD_DIM]
            kh = kk[:, kv * HEAD_DIM:(kv + 1) * HEAD_DIM]
            vh = vv[:, kv * HEAD_DIM:(kv + 1) * HEAD_DIM]
            s = _mm_nt(qh, kh) * ATT_SCALE - slope * dist
            s = jnp.where(valid, s, NEG_BIG)
            sink = sink_ref[h]
            m = jnp.maximum(jnp.max(s, axis=-1, keepdims=True), sink)
            e = jnp.exp(s - m)
            den = jnp.sum(e, axis=-1, keepdims=True) + jnp.exp(sink - m)
            o_ref[:, h * HEAD_DIM:(h + 1) * HEAD_DIM] = _mm(e, vh) * (1.0 / den)
            lse_ref[:, h:h + 1] = m + jnp.log(den)

    cur = lambda w: pl.BlockSpec((ATT_BLOCK, w), lambda b, i: (b * nb + i, 0))
    prev = lambda w: pl.BlockSpec((ATT_BLOCK, w), lambda b, i: (b * nb + jnp.maximum(i - 1, 0), 0))
    return pl.pallas_call(
        body, name="attn_forward", grid=(bl, nb),
        in_specs=[pl.BlockSpec(memory_space=pltpu.SMEM), cur(512), prev(128), cur(128), prev(128), cur(128)],
        out_specs=(cur(512), cur(N_HEADS)),
        out_shape=(jax.ShapeDtypeStruct((t, D_ATTN), F32), jax.ShapeDtypeStruct((t, N_HEADS), F32)),
        compiler_params=_tc_params(("arbitrary", "arbitrary")),
    )(sinks, q, k, k, v, v)


def _attn_backward(q, k, v, o, do, lse, sinks, bl, nb):
    t = q.shape[0]

    def body(sink_ref, qc_ref, kp_ref, kc_ref, vp_ref, vc_ref, oc_ref, doc_ref, lc_ref,
             dq_ref, dk_ref, dv_ref, ds_ref, dk_carry, dv_carry):
        b, i = pl.program_id(0), pl.program_id(1)
        live = i < nb

        @pl.when(i == 0)
        def _():
            dk_carry[...] = jnp.zeros((ATT_BLOCK, KV_HEADS * HEAD_DIM), F32)
            dv_carry[...] = jnp.zeros((ATT_BLOCK, KV_HEADS * HEAD_DIM), F32)

        dist, valid = _att_dist_mask(i == 0)
        valid = valid & live
        kk = jnp.concatenate([kp_ref[...], kc_ref[...]], axis=0)
        vv = jnp.concatenate([vp_ref[...], vc_ref[...]], axis=0)
        qc, oc, doc, lc = qc_ref[...], oc_ref[...], doc_ref[...], lc_ref[...]
        dsink_cols, dq_parts = [], []
        for kv in range(KV_HEADS):
            heads = range(kv * Q_PER_KV, (kv + 1) * Q_PER_KV)
            cols = slice(kv * HEAD_DIM, (kv + 1) * HEAD_DIM)
            kh, vh = kk[:, cols], vv[:, cols]
            slope = _per_head_col([2.0 ** (-(h + 1)) for h in heads])
            sink = _per_head_col([sink_ref[h] for h in heads])
            q4, do4 = _stack_heads(qc, kv), _stack_heads(doc, kv)
            delta = jnp.sum(do4 * _stack_heads(oc, kv), axis=-1, keepdims=True)
            lse4 = _stack_cols(lc, kv)
            s = _mm_nt(q4, kh) * ATT_SCALE - slope * dist
            p = jnp.where(valid, jnp.exp(s - lse4), 0.0)
            dsc = p * (_mm_nt(do4, vh) - delta)
            dq4 = _mm(dsc, kh) * ATT_SCALE
            dk2 = _mm_tn(dsc, q4) * ATT_SCALE
            dv2 = _mm_tn(p, do4)
            dsink4 = jnp.where(live, jnp.exp(sink - lse4) * delta, 0.0)
            dk_ref[:, cols] = dk_carry[:, cols] + dk2[0:ATT_BLOCK, :]
            dv_ref[:, cols] = dv_carry[:, cols] + dv2[0:ATT_BLOCK, :]
            dk_carry[:, cols] = dk2[ATT_BLOCK:, :]
            dv_carry[:, cols] = dv2[ATT_BLOCK:, :]
            for g, h in enumerate(heads):
                rows = slice(g * ATT_BLOCK, (g + 1) * ATT_BLOCK)
                dq_parts.append((h, dq4[rows, :]))
                dsink_cols.append(-jnp.sum(dsink4[rows, :], axis=0, keepdims=True))
        dsink = jnp.concatenate(dsink_cols, axis=1)

        @pl.when(live)
        def _():
            for h, part in dq_parts:
                dq_ref[:, h * HEAD_DIM:(h + 1) * HEAD_DIM] = part

        @pl.when((b == 0) & (i == 0))
        def _():
            ds_ref[...] = dsink

        @pl.when((b != 0) | (i != 0))
        def _():
            ds_ref[...] += dsink

    cur_i = lambda i: jnp.minimum(i, nb - 1)
    cur = lambda w: pl.BlockSpec((ATT_BLOCK, w), lambda b, i: (b * nb + cur_i(i), 0))
    prev = lambda w: pl.BlockSpec((ATT_BLOCK, w), lambda b, i: (b * nb + jnp.maximum(cur_i(i) - 1, 0), 0))
    behind = lambda w: pl.BlockSpec((ATT_BLOCK, w), lambda b, i: (b * nb + jnp.maximum(i - 1, 0), 0))
    return pl.pallas_call(
        body, name="attn_backward", grid=(bl, nb + 1),
        in_specs=[pl.BlockSpec(memory_space=pltpu.SMEM), cur(512), prev(128), cur(128), prev(128), cur(128),
                  cur(512), cur(512), cur(N_HEADS)],
        out_specs=(cur(512), behind(128), behind(128), pl.BlockSpec((1, N_HEADS), lambda b, i: (0, 0))),
        out_shape=(jax.ShapeDtypeStruct((t, D_ATTN), F32), jax.ShapeDtypeStruct((t, 128), F32),
                   jax.ShapeDtypeStruct((t, 128), F32), jax.ShapeDtypeStruct((1, N_HEADS), F32)),
        scratch_shapes=[pltpu.VMEM((ATT_BLOCK, KV_HEADS * HEAD_DIM), F32), pltpu.VMEM((ATT_BLOCK, KV_HEADS * HEAD_DIM), F32)],
        compiler_params=_tc_params(("arbitrary", "arbitrary")),
    )(sinks, q, k, k, v, v, o, do, lse)


def _mix_forward_backward(x2, y_perm, z_ssm, attn, z_attn, p2, target2, w_glu, b_glu, w_out, g_post, w_gate, b_gate,
                          w_proj, bl, seg):
    t = x2.shape[0]
    tm = seg

    def body(x_ref, y_ref, zs_ref, at_ref, za_ref, p_ref, tg_ref,
             wglu_ref, bglu_ref, wout_ref, gpost_ref, wgate_ref, bgate_ref, wproj_ref,
             loss_ref, dh1_ref, dy_ref, dzs_ref, dat_ref, dza_ref,
             dwglu_ref, dbglu_ref, dwout_ref, dgpost_ref, dwgate_ref, dbgate_ref, dwproj_ref,
             dwout16_ref, dwgate16_ref, dwproj16_ref, dwglu16_ref):
        i = pl.program_id(0)
        gpost = gpost_ref[...]

        @pl.when(i == 0)
        def _():
            for ref in (dwglu_ref, dbglu_ref, dwout_ref, dgpost_ref, dwgate_ref, dbgate_ref, dwproj_ref, loss_ref):
                ref[...] = jnp.zeros(ref.shape, F32)

        def chain(rows):
            y = y_ref[0, rows, :]
            u3 = GELU_C * (y + GELU_K * y * y * y)
            th = jnp.tanh(u3)
            gl = 0.5 * y * (1.0 + th)
            a = _mm(gl, wglu_ref[...]) + bglu_ref[...]
            sa = _sigmoid(a)
            glu = gl * sa
            zs = zs_ref[rows, :]
            sgs = _sigmoid(zs)
            ssm_out = glu * (zs * sgs)
            za = za_ref[rows, :]
            sga = _sigmoid(za)
            at = at_ref[rows, :]
            attn_out = at * (za * sga)
            cat = jnp.concatenate([ssm_out, attn_out], axis=-1).astype(BF16)
            mixed = _mm(cat, wout_ref[...])
            r2 = lax.rsqrt(jnp.mean(mixed * mixed, axis=-1, keepdims=True) + EPS)
            nhat = mixed * r2
            h1 = x_ref[rows, :] + nhat * gpost
            gate = _sigmoid(_mm(h1, wgate_ref[...]) + bgate_ref[...])
            pv = p_ref[rows, :]
            pp = _mm(pv, wproj_ref[...])
            h2 = h1 + gate * pp
            err = h2 - tg_ref[rows, :]
            loss_part = jnp.sum(jnp.sum(err * err, axis=-1, keepdims=True), axis=0, keepdims=True) * (0.5 / D_MODEL)
            dh2 = err * (1.0 / D_MODEL)
            dgp = dh2 * pp * gate * (1.0 - gate)
            dpp = dh2 * gate
            dh1 = dh2 + _mm_nt(dgp, wgate_ref[...])
            dh1_ref[rows, :] = dh1
            dnhat = dh1 * gpost
            dmixed = r2 * (dnhat - nhat * jnp.mean(dnhat * nhat, axis=-1, keepdims=True))
            dcat = _mm_nt(dmixed, wout_ref[...])
            dso, dao = dcat[:, 0:D_SSM], dcat[:, D_SSM:]
            dat_ref[rows, :] = dao * (za * sga)
            dza_ref[rows, :] = (dao * at * (sga * (1.0 + za * (1.0 - sga)))).astype(BF16)
            dzs_ref[rows, :] = (dso * glu * (sgs * (1.0 + zs * (1.0 - sgs)))).astype(BF16)
            dglu = dso * (zs * sgs)
            da = dglu * gl * sa * (1.0 - sa)
            dgl = dglu * sa + _mm_nt(da, wglu_ref[...])
            dgelu = 0.5 * (1.0 + th) + 0.5 * y * (1.0 - th * th) * (GELU_C * (1.0 + 3.0 * GELU_K * y * y))
            dy_ref[0, rows, :] = dgl * dgelu
            return dict(gl=gl.astype(BF16), da=da.astype(BF16), cat=cat, dmixed=dmixed.astype(BF16),
                        h1=h1.astype(BF16), dgp=dgp.astype(BF16), pv=pv.astype(BF16), dpp=dpp.astype(BF16),
                        dbglu=jnp.sum(da, axis=0, keepdims=True), dgpost=jnp.sum(dh1 * nhat, axis=0, keepdims=True),
                        dbgate=jnp.sum(dgp, axis=0, keepdims=True), loss=loss_part)

        groups = [chain(slice(k * (tm // MIX_GROUPS), (k + 1) * (tm // MIX_GROUPS))) for k in range(MIX_GROUPS)]
        rows_of = lambda name: jnp.concatenate([g[name] for g in groups], axis=0)
        total = lambda name: sum(g[name] for g in groups)
        parts = (
            (dwglu_ref, _mm_tn(rows_of("gl"), rows_of("da"))), (dbglu_ref, total("dbglu")),
            (dwout_ref, _mm_tn(rows_of("cat"), rows_of("dmixed"))), (dgpost_ref, total("dgpost")),
            (dwgate_ref, _mm_tn(rows_of("h1"), rows_of("dgp"))), (dbgate_ref, total("dbgate")),
            (dwproj_ref, _mm_tn(rows_of("pv"), rows_of("dpp"))), (loss_ref, total("loss")),
        )

        for ref, val in parts:
            ref[...] += val

        @pl.when(i == t // tm - 1)
        def _():
            for ref16, ref in ((dwout16_ref, dwout_ref), (dwgate16_ref, dwgate_ref), (dwproj16_ref, dwproj_ref),
                               (dwglu16_ref, dwglu_ref)):
                def to16(r, ref16=ref16, ref=ref):
                    ref16[r, :] = ref[r, :].astype(BF16)

                _row_chunks(ref.shape[0], to16)

    row = lambda w: pl.BlockSpec((tm, w), lambda i: (i, 0))
    perm = pl.BlockSpec((1, tm, D_SSM), lambda i: (i // N_SEG, 0, i % N_SEG))
    perm_shape = jax.ShapeDtypeStruct((bl, seg, N_SEG * D_SSM), F32)
    acc = lambda r, c, dt=F32: (_const_spec((r, c)), jax.ShapeDtypeStruct((r, c), dt))
    accs = [acc(D_SSM, D_SSM), acc(1, D_SSM), acc(D_MODEL, D_MODEL), acc(1, D_MODEL), acc(D_MODEL, D_MODEL),
            acc(1, D_MODEL), acc(D_PLE, D_MODEL),
            acc(D_MODEL, D_MODEL, BF16), acc(D_MODEL, D_MODEL, BF16), acc(D_PLE, D_MODEL, BF16), acc(D_SSM, D_SSM, BF16)]
    return pl.pallas_call(
        body, name="mix_forward_backward", grid=(t // tm,),
        in_specs=[row(D_MODEL), perm, row(512), row(512), row(512), row(D_PLE), row(D_MODEL),
                  _const_spec((D_SSM, D_SSM)), _const_spec((1, D_SSM)), _const_spec((D_MODEL, D_MODEL)),
                  _const_spec((1, D_MODEL)), _const_spec((D_MODEL, D_MODEL)), _const_spec((1, D_MODEL)),
                  _const_spec((D_PLE, D_MODEL))],
        out_specs=(_const_spec((1, 1)), row(D_MODEL), perm, row(512), row(512), row(512)) + tuple(a[0] for a in accs),
        out_shape=(jax.ShapeDtypeStruct((1, 1), F32), jax.ShapeDtypeStruct((t, D_MODEL), F32), perm_shape,
                   jax.ShapeDtypeStruct((t, 512), BF16), jax.ShapeDtypeStruct((t, 512), F32),
                   jax.ShapeDtypeStruct((t, 512), BF16)) + tuple(a[1] for a in accs),
        compiler_params=_tc_params(("arbitrary",)),
    )(x2, y_perm, z_ssm, attn, z_attn, p2, target2, w_glu, b_glu, w_out, g_post, w_gate, b_gate, w_proj)


def _in_backward(x2, dh1, du_perm, dz_ssm, dq, dk, dv, dz_attn, g_pre, w_in, bl, seg):
    t = x2.shape[0]
    tm = seg

    def body(x_ref, dh1_ref, du_ref, dzs_ref, dq_ref, dk_ref, dv_ref, dza_ref, g_ref, w_ref,
             gx_ref, dw_ref, dg_ref, dw16_ref):
        i = pl.program_id(0)
        xv = x_ref[...]
        r = lax.rsqrt(jnp.mean(xv * xv, axis=-1, keepdims=True) + EPS)
        xhat = xv * r
        g = g_ref[...]
        hn = (xhat * g).astype(BF16)
        dproj = jnp.concatenate([du_ref[0].astype(BF16), dzs_ref[...].astype(BF16), dq_ref[...].astype(BF16),
                                 dk_ref[...].astype(BF16), dv_ref[...].astype(BF16), dza_ref[...].astype(BF16)],
                                axis=-1)
        dhn = _mm(dproj, w_ref[...])
        dxhat = dhn * g
        gx_ref[...] = dh1_ref[...] + r * (dxhat - xhat * jnp.mean(dxhat * xhat, axis=-1, keepdims=True))
        @pl.when(i == 0)
        def _():
            dw_ref[...] = jnp.zeros((D_IN, D_MODEL), F32)
            dg_ref[...] = jnp.zeros((1, D_MODEL), F32)

        dw_ref[...] += _mm_tn(dproj, hn)
        dg_ref[...] += jnp.sum(dhn * xhat, axis=0, keepdims=True)

        @pl.when(i == t // tm - 1)
        def _():
            def to16(r):
                dw16_ref[r, :] = dw_ref[r, :].astype(BF16)

            _row_chunks(D_IN, to16)

    row = lambda w: pl.BlockSpec((tm, w), lambda i: (i, 0))
    perm = pl.BlockSpec((1, tm, D_SSM), lambda i: (i // N_SEG, 0, i % N_SEG))
    return pl.pallas_call(
        body, name="in_backward", grid=(t // tm,),
        in_specs=[row(D_MODEL), row(D_MODEL), perm, row(512), row(512), row(128), row(128), row(512),
                  _const_spec((1, D_MODEL)), _const_spec((D_IN, D_MODEL))],
        out_specs=(row(D_MODEL), _const_spec((D_IN, D_MODEL)), _const_spec((1, D_MODEL)),
                   _const_spec((D_IN, D_MODEL))),
        out_shape=(jax.ShapeDtypeStruct((t, D_MODEL), F32), jax.ShapeDtypeStruct((D_IN, D_MODEL), F32),
                   jax.ShapeDtypeStruct((1, D_MODEL), F32), jax.ShapeDtypeStruct((D_IN, D_MODEL), BF16)),
        compiler_params=_tc_params(("arbitrary",)),
    )(x2, dh1, du_perm, dz_ssm, dq, dk, dv, dz_attn, g_pre, w_in)


def _local_step(x, p, target, pre_norm_g, w_in, ssm_lam_re, ssm_lam_im, ssm_log_step, ssm_b_re, ssm_b_im, ssm_c_re,
                ssm_c_im, ssm_d, ssm_b_glu, attn_sinks, post_norm_g, pl_b_gate, late):
    bl, seq, _ = x.shape
    seg = seq // N_SEG
    nb = seq // ATT_BLOCK
    t = bl * seq
    x2 = x.reshape(t, D_MODEL)
    p2 = p.reshape(t, D_PLE)
    tg2 = target.reshape(t, D_MODEL)

    lam_re, lam_im = ssm_lam_re, ssm_lam_im
    log_step = ssm_log_step.reshape(SSM_G, 1)
    a_re_row, a_im_row, pw_re, pw_im, bcat, bcat_t, ccat, ccat_t = _ssm_prep(
        lam_re, lam_im, log_step, ssm_b_re, ssm_b_im, ssm_c_re, ssm_c_im, seg)
    d_row = ssm_d.reshape(1, D_SSM)

    u_perm, z_ssm, q, k, v, z_attn = _in_proj(x2, pre_norm_g.reshape(1, D_MODEL), w_in, bl, seg)
    u_perm = u_perm.reshape(bl, seq, D_SSM)
    (y_perm, states, carries), gathered = _ssm_forward(
        u_perm, bcat, ccat, a_re_row, a_im_row, pw_re, pw_im, d_row, late, seg)
    w_out, w_gate, w_proj, w_glu = (_gathered_to_full(n, g) for n, g in zip(LATE_NAMES, gathered))
    sinks = attn_sinks.reshape(N_HEADS)
    attn, lse = _attn_forward(q, k, v, sinks, bl, nb)
    (loss, dh1, dy_perm, dz_ssm, dattn, dz_attn, d_w_glu, d_b_glu, d_w_out, d_g_post, d_w_gate, d_b_gate,
     d_w_proj, *late16) = _mix_forward_backward(
        x2, y_perm.reshape(bl, seg, N_SEG * D_SSM), z_ssm, attn, z_attn, p2, tg2, w_glu,
        ssm_b_glu.reshape(1, D_SSM), w_out, post_norm_g.reshape(1, D_MODEL), w_gate, pl_b_gate.reshape(1, D_MODEL),
        w_proj, bl, seg)
    owned = lambda ds: [_full_to_owned(n, d) for n, d in zip(LATE_NAMES, ds)]
    dq, dk, dv, d_sinks = _attn_backward(q, k, v, attn, dattn, lse, sinks, bl, nb)
    (du_perm, d_bcat, d_ccat_t, da_re, da_im, d_d), late_grads = _ssm_backward(
        u_perm, dy_perm.reshape(bl, seq, D_SSM), states, carries, bcat_t, ccat_t, a_re_row, a_im_row, pw_re, pw_im,
        d_row, owned(late16), owned((d_w_out, d_w_gate, d_w_proj, d_w_glu)), seg)
    grad_x, d_w_in, d_g_pre, d_w_in16 = _in_backward(
        x2, dh1, du_perm.reshape(bl, seg, N_SEG * D_SSM), dz_ssm, dq, dk, dv, dz_attn,
        pre_norm_g.reshape(1, D_MODEL), w_in, bl, seg)
    d_lam_re, d_lam_im, d_ls, d_b_re, d_b_im, d_c_re, d_c_im = _ssm_param_grads(
        lam_re, lam_im, log_step, ssm_b_re, ssm_b_im, da_re.reshape(SSM_G, SSM_N), da_im.reshape(SSM_G, SSM_N),
        d_bcat, d_ccat_t)
    grads = {
        "pre_norm_g": d_g_pre, "w_in": d_w_in, "w_in16": d_w_in16, "ssm_lam_re": d_lam_re, "ssm_lam_im": d_lam_im,
        "ssm_log_step": d_ls, "ssm_b_re": d_b_re, "ssm_b_im": d_b_im, "ssm_c_re": d_c_re, "ssm_c_im": d_c_im,
        "ssm_d": d_d, "ssm_b_glu": d_b_glu, "attn_sinks": d_sinks, "post_norm_g": d_g_post, "pl_b_gate": d_b_gate,
    }
    return loss, grad_x.reshape(bl, seq, D_MODEL), grads, late_grads


LATE_NAMES = ("w_out", "pl_w_gate", "pl_w_proj", "ssm_w_glu")
BIG_NAMES = ("w_in",) + LATE_NAMES
COL_SHARDED = {"w_in": D_IN // N_DEV, "pl_w_proj": D_MODEL // N_DEV}
WEIGHT_NAMES = ("pre_norm_g", "w_in", "ssm_lam_re", "ssm_lam_im", "ssm_log_step", "ssm_b_re", "ssm_b_im", "ssm_c_re",
                "ssm_c_im", "ssm_d", "ssm_w_glu", "ssm_b_glu", "attn_sinks", "w_out", "post_norm_g", "pl_w_proj",
                "pl_w_gate", "pl_b_gate")


TRANSPOSED = {"w_in": (0, 1), "ssm_b_re": (1, 2), "ssm_b_im": (1, 2)}


def _kernel_form(name, a):
    a = a[0]
    if name in TRANSPOSED:
        a = jnp.swapaxes(a, *TRANSPOSED[name])
    if name in ("ssm_b_re", "ssm_b_im", "ssm_c_re", "ssm_c_im"):
        a = a.reshape(SSM_G * SSM_P, SSM_N)
    return a


def _given_form(name, a, shape):
    if name in TRANSPOSED:
        i, j = TRANSPOSED[name]
        swapped = list(shape[1:])
        swapped[i], swapped[j] = swapped[j], swapped[i]
        return jnp.swapaxes(a.reshape(swapped), i, j).reshape(shape)
    return a.reshape(shape)


def _gathered_to_full(name, g):
    _, rows, cols = g.shape
    if name in COL_SHARDED:
        return jnp.swapaxes(g, 0, 1).reshape(rows, N_DEV * cols)
    return g.reshape(N_DEV * rows, cols)


def _full_to_owned(name, full):
    if name in COL_SHARDED:
        return jnp.swapaxes(full.reshape(full.shape[0], N_DEV, COL_SHARDED[name]), 0, 1)
    return full.reshape(N_DEV, full.shape[0] // N_DEV, full.shape[1])


def kernel(x, p, pre_norm_g, w_in, ssm_lam_re, ssm_lam_im, ssm_log_step, ssm_b_re, ssm_b_im, ssm_c_re, ssm_c_im, ssm_d, ssm_w_glu, ssm_b_glu, attn_sinks, w_out, post_norm_g, pl_w_proj, pl_w_gate, pl_b_gate, loss_target, m_pre_norm_g, m_w_in, m_ssm_lam_re, m_ssm_lam_im, m_ssm_log_step, m_ssm_b_re, m_ssm_b_im, m_ssm_c_re, m_ssm_c_im, m_ssm_d, m_ssm_w_glu, m_ssm_b_glu, m_attn_sinks, m_w_out, m_post_norm_g, m_pl_w_proj, m_pl_w_gate, m_pl_b_gate, v_pre_norm_g, v_w_in, v_ssm_lam_re, v_ssm_lam_im, v_ssm_log_step, v_ssm_b_re, v_ssm_b_im, v_ssm_c_re, v_ssm_c_im, v_ssm_d, v_ssm_w_glu, v_ssm_b_glu, v_attn_sinks, v_w_out, v_post_norm_g, v_pl_w_proj, v_pl_w_gate, v_pl_b_gate):
    w = dict(pre_norm_g=pre_norm_g, w_in=w_in, ssm_lam_re=ssm_lam_re, ssm_lam_im=ssm_lam_im, ssm_log_step=ssm_log_step,
             ssm_b_re=ssm_b_re, ssm_b_im=ssm_b_im, ssm_c_re=ssm_c_re, ssm_c_im=ssm_c_im, ssm_d=ssm_d, ssm_w_glu=ssm_w_glu,
             ssm_b_glu=ssm_b_glu, attn_sinks=attn_sinks, w_out=w_out, post_norm_g=post_norm_g, pl_w_proj=pl_w_proj,
             pl_w_gate=pl_w_gate, pl_b_gate=pl_b_gate)
    m = dict(pre_norm_g=m_pre_norm_g, w_in=m_w_in, ssm_lam_re=m_ssm_lam_re, ssm_lam_im=m_ssm_lam_im,
             ssm_log_step=m_ssm_log_step, ssm_b_re=m_ssm_b_re, ssm_b_im=m_ssm_b_im, ssm_c_re=m_ssm_c_re,
             ssm_c_im=m_ssm_c_im, ssm_d=m_ssm_d, ssm_w_glu=m_ssm_w_glu, ssm_b_glu=m_ssm_b_glu, attn_sinks=m_attn_sinks,
             w_out=m_w_out, post_norm_g=m_post_norm_g, pl_w_proj=m_pl_w_proj, pl_w_gate=m_pl_w_gate,
             pl_b_gate=m_pl_b_gate)
    v = dict(pre_norm_g=v_pre_norm_g, w_in=v_w_in, ssm_lam_re=v_ssm_lam_re, ssm_lam_im=v_ssm_lam_im,
             ssm_log_step=v_ssm_log_step, ssm_b_re=v_ssm_b_re, ssm_b_im=v_ssm_b_im, ssm_c_re=v_ssm_c_re,
             ssm_c_im=v_ssm_c_im, ssm_d=v_ssm_d, ssm_w_glu=v_ssm_w_glu, ssm_b_glu=v_ssm_b_glu, attn_sinks=v_attn_sinks,
             w_out=v_w_out, post_norm_g=v_post_norm_g, pl_w_proj=v_pl_w_proj, pl_w_gate=v_pl_w_gate,
             pl_b_gate=v_pl_b_gate)
    kf = lambda d: {n: _kernel_form(n, a) for n, a in d.items()}
    wk, mk, vk = kf(w), kf(m), kf(v)

    (gathered,) = _allgather_weights([wk["w_in"]])
    loss, grad_x, grads, g_late = _local_step(
        x, p[0], loss_target, wk["pre_norm_g"], gathered.reshape(D_IN, D_MODEL), wk["ssm_lam_re"], wk["ssm_lam_im"],
        wk["ssm_log_step"], wk["ssm_b_re"], wk["ssm_b_im"], wk["ssm_c_re"], wk["ssm_c_im"], wk["ssm_d"],
        wk["ssm_b_glu"], wk["attn_sinks"], wk["post_norm_g"], wk["pl_b_gate"], [wk[n] for n in LATE_NAMES])

    owned = lambda g: g.reshape(N_DEV, D_IN // N_DEV, D_MODEL)
    tiny_form = lambda d: [d[n].reshape(rows, cols) for n, rows, cols in TINY]
    med_form = lambda d: [d[n].reshape(N_DEV, rows // N_DEV, cols) for n, rows, cols in MEDIUM]
    g_big, loss, g_tiny, g_med = _reduce_final(
        [owned(grads["w_in16"])], [owned(grads["w_in"])], loss, tiny_form(grads), med_form(grads))
    names = BIG_NAMES + tuple(n for n, _, _ in TINY + MEDIUM)
    form = lambda d: [d[n] for n in BIG_NAMES] + tiny_form(d) + med_form(d)
    updated = _adamw_update(g_big + g_late + g_tiny + g_med, form(wk), form(mk), form(vk))
    vals = dict(zip(names, updated))
    results = [[_given_form(n, vals[n][kind], w[n].shape) for n in WEIGHT_NAMES] for kind in range(4)]
    return (loss.reshape(()), grad_x, *results[0], *results[1], *results[2], *results[3])
```

```python
import functools
import math

import jax
import jax.numpy as jnp
from jax import lax
from jax.experimental import pallas as pl
from jax.experimental.pallas import tpu as pltpu

F32 = jnp.float32
BF16 = jnp.bfloat16

D_MODEL = 1024
D_SSM = 512
D_ATTN = 512
SSM_P = 16
SSM_G = 32
SSM_N = 64
N_HEADS = 8
KV_HEADS = 2
Q_PER_KV = 4
HEAD_DIM = 64
ATT_BLOCK = 128
D_PLE = 256
D_IN = 2304
EPS = 1e-6
N_DEV = 8
N_SEG = 8
G_TILE = 8
N_GT = SSM_G // G_TILE
CH_T = G_TILE * SSM_P
ST_T = G_TILE * SSM_N
N_STATE = SSM_G * SSM_N
SCAN_UNROLL = 4
MIX_GROUPS = 1
LANES = 128
VMEM_LIMIT = 60 * 1024 * 1024

ADAM_LR = 0.001
ADAM_B1 = 0.9
ADAM_B2 = 0.999
ADAM_EPS = 1e-08
ADAM_WD = 0.01
ADAM_STEP = 10

GELU_C = math.sqrt(2.0 / math.pi)
GELU_K = 0.044715
ATT_SCALE = 1.0 / math.sqrt(HEAD_DIM)
NEG_BIG = -1e30


def _mm(a, b):
    return jnp.dot(a.astype(BF16), b.astype(BF16), preferred_element_type=F32)


def _mm_nt(a, b):
    return lax.dot_general(a.astype(BF16), b.astype(BF16), (((1,), (1,)), ((), ())), preferred_element_type=F32)


def _mm_tn(a, b):
    return lax.dot_general(a.astype(BF16), b.astype(BF16), (((0,), (0,)), ((), ())), preferred_element_type=F32)


def _sigmoid(x):
    return 1.0 / (1.0 + jnp.exp(-x))


def _tc_params(sem):
    return pltpu.CompilerParams(dimension_semantics=sem, vmem_limit_bytes=VMEM_LIMIT)


def _const_spec(shape):
    nd = len(shape)
    return pl.BlockSpec(shape, lambda *_: (0,) * nd)


def _mesh_pos():
    return lax.axis_index("x"), lax.axis_index("y"), lax.axis_index("c")


ROW_CHUNKS = (64, 32, 16)


def _row_chunk(nrows):
    return next((c for c in ROW_CHUNKS if nrows % c == 0), None)


def _row_chunks(nrows, fn, chunk=None, init=None):
    chunk = chunk or _row_chunk(nrows)

    def step(i, carry):
        rows = pl.ds(pl.multiple_of(i * chunk, chunk), chunk)
        if init is None:
            fn(rows)
            return carry
        return fn(rows, carry)

    return lax.fori_loop(0, nrows // chunk, step, 0 if init is None else init)


def _slot(px, py, pc):
    return 4 * px + 2 * py + pc


def _allgather_weights(shards):
    n = len(shards)

    def body(*refs):
        srcs, outs, (send_sems, recv_sems) = refs[:n], refs[n:2 * n], refs[2 * n:]
        x, y, c = _mesh_pos()
        me, sibling = (x, y, c), (x, y, 1 - c)
        chips = [(1 - x, y), (x, 1 - y), (1 - x, 1 - y)]

        def copy(a, k, block, to):
            blk = outs[a].at[_slot(*block)]
            return pltpu.make_async_remote_copy(
                src_ref=blk, dst_ref=blk, send_sem=send_sems.at[7 * a + k], recv_sem=recv_sems.at[7 * a + k],
                device_id=to, device_id_type=pl.DeviceIdType.MESH)

        sends = []
        for a in range(n):
            mine = outs[a].at[_slot(*me)]

            def cast(r, mine=mine, src=srcs[a]):
                mine[r, :] = src[r, :].astype(BF16)

            _row_chunks(srcs[a].shape[0], cast)
            first = [copy(a, 0, me, sibling)] + [copy(a, 1 + j, me, (*chip, c)) for j, chip in enumerate(chips)]
            for cp in first:
                cp.start()
            sends += first
        for a in range(n):
            for j, chip in enumerate(chips):
                copy(a, 1 + j, (*chip, c), me).wait_recv()
                fwd = copy(a, 4 + j, (*chip, c), sibling)
                fwd.start()
                sends.append(fwd)
        for a in range(n):
            copy(a, 0, sibling, me).wait_recv()
            for j, chip in enumerate(chips):
                copy(a, 4 + j, (*chip, 1 - c), me).wait_recv()
        for cp in sends:
            cp.wait_send()

    vm = pl.BlockSpec(memory_space=pltpu.VMEM)
    return pl.pallas_call(
        body, name="allgather_weights",
        out_shape=tuple(jax.ShapeDtypeStruct((N_DEV,) + s.shape, BF16) for s in shards),
        in_specs=[vm] * n, out_specs=(vm,) * n,
        scratch_shapes=[pltpu.SemaphoreType.DMA((7 * n,)), pltpu.SemaphoreType.DMA((7 * n,))],
        compiler_params=pltpu.CompilerParams(vmem_limit_bytes=VMEM_LIMIT),
    )(*shards)


def _adamw(w, g, m, v):
    m = ADAM_B1 * m + (1.0 - ADAM_B1) * g
    v = ADAM_B2 * v + (1.0 - ADAM_B2) * (g * g)
    m_hat = m / (1.0 - ADAM_B1 ** ADAM_STEP)
    v_hat = v / (1.0 - ADAM_B2 ** ADAM_STEP)
    delta = -ADAM_LR * (m_hat / (jnp.sqrt(v_hat) + ADAM_EPS) + ADAM_WD * w)
    return delta, m, v


def _remote(src, dst, send_sems, recv_sems, k, to):
    return pltpu.make_async_remote_copy(src_ref=src, dst_ref=dst, send_sem=send_sems.at[k], recv_sem=recv_sems.at[k],
                                        device_id=to, device_id_type=pl.DeviceIdType.MESH)


def _big_reduce_phases(g16_r, go_r, outs, send2, recv1, recv2, s_send, s_recv):
    n = len(g16_r)
    x, y, c = _mesh_pos()
    sibling = (x, y, 1 - c)
    chips = [(1 - x, y), (x, 1 - y), (1 - x, 1 - y)]
    all_chips = [(x, y)] + chips
    lvl1 = []
    for a in range(n):
        cps = [_remote(g16_r[a].at[_slot(*chip, 1 - c)], recv1[a].at[j], s_send, s_recv, 7 * a + j, sibling)
               for j, chip in enumerate(all_chips)]
        for cp in cps:
            cp.start()
        lvl1.append(cps)
    yield
    lvl2 = []
    for a in range(n):
        for cp in lvl1[a]:
            cp.wait_recv()
        og = outs[a]

        def partials(r, a=a, og=og):
            og[r, :] = go_r[a][r, :] + recv1[a][0, r, :].astype(F32)
            for j, chip in enumerate(chips):
                mine16 = g16_r[a][_slot(*chip, c), r, :].astype(F32)
                send2[a][j, r, :] = (mine16 + recv1[a][1 + j, r, :].astype(F32)).astype(BF16)

        _row_chunks(go_r[a].shape[0], partials)
        cps = [_remote(send2[a].at[j], recv2[a].at[j], s_send, s_recv, 7 * a + 4 + j, (*chip, c))
               for j, chip in enumerate(chips)]
        for cp in cps:
            cp.start()
        lvl2.append(cps)
    yield
    for a in range(n):
        for cp in lvl2[a]:
            cp.wait_recv()
        og = outs[a]

        def total(r, a=a, og=og):
            g = og[r, :]
            for j in range(3):
                g = g + recv2[a][j, r, :].astype(F32)
            og[r, :] = g

        _row_chunks(go_r[a].shape[0], total)
    yield
    for cps in lvl1 + lvl2:
        for cp in cps:
            cp.wait_send()


def _adamw_update(g, w, m, v):
    n = len(g)

    def body(*refs):
        g_r, w_r, m_r, v_r = (refs[i * n:(i + 1) * n] for i in range(4))
        outs = refs[4 * n:]
        for a in range(n):
            og, od, om, ov = outs[4 * a:4 * a + 4]

            def update(idx, a=a, og=og, od=od, om=om, ov=ov):
                gv = g_r[a][idx]
                d, nm, nv = _adamw(w_r[a][idx], gv, m_r[a][idx], v_r[a][idx])
                og[idx] = gv
                od[idx] = d
                om[idx] = nm
                ov[idx] = nv

            shape = g_r[a].shape
            if len(shape) == 3:
                for b in range(shape[0]):
                    update(b)
            elif _row_chunk(shape[0]) is not None:
                _row_chunks(shape[0], update)
            else:
                update(Ellipsis)

    vm = pl.BlockSpec(memory_space=pltpu.VMEM)
    res = pl.pallas_call(
        body, name="adamw_update",
        out_shape=tuple(jax.ShapeDtypeStruct(t.shape, F32) for t in g for _ in range(4)),
        in_specs=[vm] * (4 * n), out_specs=(vm,) * (4 * n),
        compiler_params=pltpu.CompilerParams(vmem_limit_bytes=VMEM_LIMIT),
    )(*g, *w, *m, *v)
    return [res[4 * a:4 * a + 4] for a in range(n)]


TINY = (("pre_norm_g", 1, 1024), ("post_norm_g", 1, 1024), ("pl_b_gate", 1, 1024), ("ssm_d", 1, 512),
        ("ssm_b_glu", 1, 512), ("ssm_log_step", 1, 32), ("attn_sinks", 1, 8), ("ssm_lam_re", 32, 64),
        ("ssm_lam_im", 32, 64))
MEDIUM = (("ssm_b_re", SSM_G * SSM_P, SSM_N), ("ssm_b_im", SSM_G * SSM_P, SSM_N), ("ssm_c_re", SSM_G * SSM_P, SSM_N),
          ("ssm_c_im", SSM_G * SSM_P, SSM_N))


def _stage_rows():
    offs, r = {}, 0
    for name, rows, cols in TINY + (("loss", 1, 1),):
        if rows > 1:
            r = -(-r // 8) * 8
        offs[name] = r
        r += rows if rows > 1 else max(cols // LANES, 1)
    return offs, -(-r // 8) * 8


def _reduce_final(g16, g32, loss, g_tiny, g_med):
    nb_, nt, nm_ = len(g16), len(TINY), len(MEDIUM)
    offs, stage_rows = _stage_rows()

    def body(*refs):
        g16_r, go_r = refs[:nb_], refs[nb_:2 * nb_]
        base = 2 * nb_
        loss_r, gt, gm = refs[base], refs[base + 1:base + 1 + nt], refs[base + 1 + nt:base + 1 + nt + nm_]
        base += 1 + nt + nm_
        out_b = refs[base:base + nb_]
        base += nb_
        loss_o, out_t, out_m = refs[base], refs[base + 1:base + 1 + nt], refs[base + 1 + nt:base + 1 + nt + nm_]
        base += 1 + nt + nm_
        send2_b, recv1_b, recv2_b = (refs[base + i * nb_:base + (i + 1) * nb_] for i in range(3))
        base += 3 * nb_
        stage = refs[base]
        recv1, part, recv2 = (refs[base + 1 + i * nm_:base + 1 + (i + 1) * nm_] for i in range(3))
        bs_send, bs_recv, s_send, s_recv, own_sems = refs[base + 1 + 3 * nm_:base + 6 + 3 * nm_]
        own32 = refs[base + 6 + 3 * nm_:]
        me = _slot(*_mesh_pos())
        fetch = [pltpu.make_async_copy(go_r[a].at[me], own32[a], own_sems.at[a]) for a in range(nb_)]
        for cp in fetch:
            cp.start()
        big = _big_reduce_phases(g16_r, own32, out_b, send2_b, recv1_b, recv2_b, bs_send, bs_recv)
        small = small_phases(loss_r, gt, gm, loss_o, out_t, out_m, stage, recv1, part, recv2, s_send, s_recv)
        next(big)
        next(small)
        for cp in fetch:
            cp.wait()
        next(big)
        for _ in small:
            pass
        for _ in big:
            pass

    def small_phases(loss_r, gt, gm, loss_o, out_t, out_m, stage, recv1, part, recv2, s_send, s_recv):
        x, y, c = _mesh_pos()
        me = _slot(x, y, c)
        sibling = (x, y, 1 - c)
        chips = [(1 - x, y), (x, 1 - y), (1 - x, 1 - y)]
        all_chips = [(x, y)] + chips
        peers = [sibling] + [(*chip, c) for chip in chips] + [(*chip, 1 - c) for chip in chips]
        sem = iter(range(7 + 14 * nm_))
        lvl1 = []
        for a in range(nm_):
            cps = [_remote(gm[a].at[_slot(*chip, 1 - c)], recv1[a].at[j], s_send, s_recv, next(sem), sibling)
                   for j, chip in enumerate(all_chips)]
            for cp in cps:
                cp.start()
            lvl1.append(cps)
        mine = stage.at[me]
        mine[...] = jnp.zeros((stage_rows, LANES), F32)
        for (name, rows, cols), ref in zip(TINY + (("loss", 1, 1),), gt + (loss_r,)):
            r0 = offs[name]
            if rows > 1:
                mine[r0:r0 + rows, 0:cols] = ref[...]
            elif cols >= LANES:
                for i in range(cols // LANES):
                    mine[r0 + i:r0 + i + 1, :] = ref[:, i * LANES:(i + 1) * LANES]
            else:
                mine[r0:r0 + 1, 0:cols] = ref[...]
        tiny_cps = [_remote(mine, mine, s_send, s_recv, next(sem), peer) for peer in peers]
        for cp in tiny_cps:
            cp.start()
        yield
        lvl2 = []
        for a in range(nm_):
            for cp in lvl1[a]:
                cp.wait_recv()
            for j, chip in enumerate(all_chips):
                part[a][j] = gm[a][_slot(*chip, c)] + recv1[a][j]
            cps = [_remote(part[a].at[1 + j], recv2[a].at[j], s_send, s_recv, next(sem), (*chip, c))
                   for j, chip in enumerate(chips)]
            for cp in cps:
                cp.start()
            lvl2.append(cps)
        yield
        lvl3 = []
        for a in range(nm_):
            for cp in lvl2[a]:
                cp.wait_recv()
            blk = out_m[a].at[me]
            blk[...] = ((part[a][0] + recv2[a][0]) + recv2[a][1]) + recv2[a][2]
            cps = [_remote(blk, blk, s_send, s_recv, next(sem), peer) for peer in peers]
            for cp in cps:
                cp.start()
            lvl3.append(cps)
        yield
        for cp in tiny_cps:
            cp.wait_recv()
        tot = stage[0]
        for d in range(1, N_DEV):
            tot = tot + stage[d]
        loss_o[...] = tot[offs["loss"]:offs["loss"] + 1, 0:1]
        for k, (name, rows, cols) in enumerate(TINY):
            r0 = offs[name]
            if rows > 1:
                out_t[k][...] = tot[r0:r0 + rows, 0:cols]
            elif cols >= LANES:
                for i in range(cols // LANES):
                    out_t[k][:, i * LANES:(i + 1) * LANES] = tot[r0 + i:r0 + i + 1, :]
            else:
                out_t[k][...] = tot[r0:r0 + 1, 0:cols]
        for cps in lvl3:
            for cp in cps:
                cp.wait_recv()
        for cps in lvl1 + lvl2 + lvl3 + [tiny_cps]:
            for cp in cps:
                cp.wait_send()

    vmem = pl.BlockSpec(memory_space=pltpu.VMEM)
    t_shapes = [jax.ShapeDtypeStruct((rows, cols), F32) for _, rows, cols in TINY]
    m_shapes = [jax.ShapeDtypeStruct((N_DEV, rows // N_DEV, cols), F32) for _, rows, cols in MEDIUM]
    blk = [(rows // N_DEV, cols) for _, rows, cols in MEDIUM]
    shard = [g.shape[1:] for g in g16]
    scratch = ([pltpu.VMEM((3,) + s, BF16) for s in shard] + [pltpu.VMEM((4,) + s, BF16) for s in shard]
               + [pltpu.VMEM((3,) + s, BF16) for s in shard]
               + [pltpu.VMEM((N_DEV, stage_rows, LANES), F32)]
               + [pltpu.VMEM((4,) + b, F32) for b in blk] + [pltpu.VMEM((4,) + b, F32) for b in blk]
               + [pltpu.VMEM((3,) + b, F32) for b in blk]
               + [pltpu.SemaphoreType.DMA((7 * nb_,)), pltpu.SemaphoreType.DMA((7 * nb_,)),
                  pltpu.SemaphoreType.DMA((7 + 14 * nm_,)), pltpu.SemaphoreType.DMA((7 + 14 * nm_,)),
                  pltpu.SemaphoreType.DMA((nb_,))]
               + [pltpu.VMEM(s, F32) for s in shard])
    n_out = nb_ + 1 + nt + nm_
    res = pl.pallas_call(
        body, name="reduce_final",
        out_shape=tuple(jax.ShapeDtypeStruct(s, F32) for s in shard) + (jax.ShapeDtypeStruct((1, 1), F32),)
        + tuple(t_shapes) + tuple(m_shapes),
        in_specs=[vmem] * nb_ + [pl.BlockSpec(memory_space=pl.ANY)] * nb_ + [vmem] * (1 + nt + nm_),
        out_specs=(vmem,) * n_out, scratch_shapes=scratch,
        compiler_params=pltpu.CompilerParams(vmem_limit_bytes=VMEM_LIMIT),
    )(*g16, *g32, loss, *g_tiny, *g_med)
    return list(res[:nb_]), res[nb_], list(res[nb_ + 1:nb_ + 1 + nt]), list(res[nb_ + 1 + nt:])


def _gather_phases(shard_r, gath, cast, send_sems, recv_sems, local_sems):
    n = len(shard_r)
    x, y, c = _mesh_pos()
    me, sibling = (x, y, c), (x, y, 1 - c)
    chips = [(1 - x, y), (x, 1 - y), (1 - x, 1 - y)]

    def own(a, k, to):
        return _remote(cast[a], gath[a].at[_slot(*me)], send_sems, recv_sems, 7 * a + k, to)

    def passed(a, k, block, to):
        blk = gath[a].at[_slot(*block)]
        return _remote(blk, blk, send_sems, recv_sems, 7 * a + k, to)

    def keep(a):
        return pltpu.make_async_copy(cast[a], gath[a].at[_slot(*me)], local_sems.at[a])

    def start():
        for a in range(n):
            def to16(r, a=a):
                cast[a][r, :] = shard_r[a][r, :].astype(BF16)

            _row_chunks(shard_r[a].shape[0], to16)
            keep(a).start()
            own(a, 0, sibling).start()
            for j, chip in enumerate(chips):
                own(a, 1 + j, (*chip, c)).start()

    def relay():
        for a in range(n):
            for j, chip in enumerate(chips):
                passed(a, 1 + j, (*chip, c), me).wait_recv()
                passed(a, 4 + j, (*chip, c), sibling).start()

    def finish():
        for a in range(n):
            passed(a, 0, sibling, me).wait_recv()
            for j, chip in enumerate(chips):
                passed(a, 4 + j, (*chip, 1 - c), me).wait_recv()
            own(a, 0, sibling).wait_send()
            for j, chip in enumerate(chips):
                own(a, 1 + j, (*chip, c)).wait_send()
                passed(a, 4 + j, (*chip, c), sibling).wait_send()
            keep(a).wait()

    return start, relay, finish


def _gather_operands(shards):
    n = len(shards)
    return ((pl.BlockSpec(memory_space=pl.ANY),) * n,
            tuple(jax.ShapeDtypeStruct((N_DEV,) + s.shape, BF16) for s in shards),
            [pltpu.VMEM(s.shape, BF16) for s in shards]
            + [pltpu.SemaphoreType.DMA((7 * n,)), pltpu.SemaphoreType.DMA((7 * n,)), pltpu.SemaphoreType.DMA((n,))])


def _hosted_reduce_phases(g16_r, g32_r, red, own16, recv1, send2, recv2, own32, s_send, s_recv, s_local):
    n = len(g16_r)
    x, y, c = _mesh_pos()
    sibling = (x, y, 1 - c)
    chips = [(1 - x, y), (x, 1 - y), (1 - x, 1 - y)]
    all_chips = [(x, y)] + chips

    def lvl1(a, j):
        return _remote(g16_r[a].at[_slot(*all_chips[j], 1 - c)], recv1[a].at[j], s_send, s_recv, 7 * a + j, sibling)

    def lvl2(a, j):
        return _remote(send2[a].at[j], recv2[a].at[j], s_send, s_recv, 7 * a + 4 + j, (*chips[j], c))

    def mine(a, j):
        if j == 3:
            return pltpu.make_async_copy(g32_r[a].at[_slot(x, y, c)], own32[a], s_local.at[4 * a + j])
        return pltpu.make_async_copy(g16_r[a].at[_slot(*chips[j], c)], own16[a].at[j], s_local.at[4 * a + j])

    def start():
        for a in range(n):
            for j in range(4):
                mine(a, j).start()
            for j in range(4):
                lvl1(a, j).start()

    def middle():
        for a in range(n):
            for j in range(4):
                mine(a, j).wait()
            for j in range(4):
                lvl1(a, j).wait_recv()

            def partials(r, a=a):
                red[a][r, :] = own32[a][r, :] + recv1[a][0, r, :].astype(F32)
                for j in range(3):
                    send2[a][j, r, :] = (own16[a][j, r, :].astype(F32) + recv1[a][1 + j, r, :].astype(F32)).astype(BF16)

            _row_chunks(own32[a].shape[0], partials)
            for j in range(3):
                lvl2(a, j).start()

    def total():
        for a in range(n):
            for j in range(3):
                lvl2(a, j).wait_recv()

            def add(r, a=a):
                g = red[a][r, :]
                for j in range(3):
                    g = g + recv2[a][j, r, :].astype(F32)
                red[a][r, :] = g

            _row_chunks(own32[a].shape[0], add)

    def finish():
        for a in range(n):
            for j in range(4):
                lvl1(a, j).wait_send()
            for j in range(3):
                lvl2(a, j).wait_send()

    return start, middle, total, finish


def _hosted_reduce_operands(g16, const_spec):
    n = len(g16)
    shard = [g.shape[1:] for g in g16]
    return ([pl.BlockSpec(memory_space=pl.ANY)] * (2 * n),
            tuple(const_spec(s) for s in shard),
            tuple(jax.ShapeDtypeStruct(s, F32) for s in shard),
            [pltpu.VMEM((3,) + s, BF16) for s in shard] + [pltpu.VMEM((4,) + s, BF16) for s in shard]
            + [pltpu.VMEM((3,) + s, BF16) for s in shard] + [pltpu.VMEM((3,) + s, BF16) for s in shard]
            + [pltpu.VMEM(s, F32) for s in shard]
            + [pltpu.SemaphoreType.DMA((7 * n,)), pltpu.SemaphoreType.DMA((7 * n,)), pltpu.SemaphoreType.DMA((4 * n,))])


def _in_proj(x2, g_pre, w_in, tm):
    t = x2.shape[0]

    def body(x_ref, g_ref, w_ref, u_ref, zs_ref, q_ref, k_ref, v_ref, za_ref):
        xv = x_ref[...]
        r = lax.rsqrt(jnp.mean(xv * xv, axis=-1, keepdims=True) + EPS)
        hn = xv * r * g_ref[...]
        proj = _mm_nt(hn, w_ref[...])
        u_ref[...] = proj[:, 0:512]
        zs_ref[...] = proj[:, 512:1024]
        q_ref[...] = proj[:, 1024:1536].astype(BF16)
        k_ref[...] = proj[:, 1536:1664].astype(BF16)
        v_ref[...] = proj[:, 1664:1792].astype(BF16)
        za_ref[...] = proj[:, 1792:2304]

    row = lambda w: pl.BlockSpec((tm, w), lambda i: (i, 0))
    return pl.pallas_call(
        body, name="in_proj", grid=(t // tm,),
        in_specs=[row(D_MODEL), _const_spec((1, D_MODEL)), _const_spec((D_IN, D_MODEL))],
        out_specs=(row(512), row(512), row(512), row(128), row(128), row(512)),
        out_shape=(jax.ShapeDtypeStruct((t, 512), F32),
                   jax.ShapeDtypeStruct((t, 512), F32), jax.ShapeDtypeStruct((t, 512), BF16),
                   jax.ShapeDtypeStruct((t, 128), BF16), jax.ShapeDtypeStruct((t, 128), BF16),
                   jax.ShapeDtypeStruct((t, 512), F32)),
        compiler_params=_tc_params(("arbitrary",)),
    )(x2, g_pre, w_in)


def _discretise(lr, li, ls):
    step = jnp.exp(ls)
    mag = jnp.exp(lr * step)
    ar = mag * jnp.cos(li * step)
    ai = mag * jnp.sin(li * step)
    den = lr * lr + li * li
    cr = ((ar - 1.0) * lr + ai * li) / den
    ci = (ai * lr - (ar - 1.0) * li) / den
    return step, ar, ai, den, cr, ci


def _per_channel(v):
    return jnp.broadcast_to(v[:, None, :], (SSM_G, SSM_P, SSM_N)).reshape(SSM_G * SSM_P, SSM_N)


def _tile_masks():
    r = lax.broadcasted_iota(jnp.int32, (CH_T, ST_T), 0) // SSM_P
    l = lax.broadcasted_iota(jnp.int32, (CH_T, ST_T), 1) // SSM_N
    lt = lax.broadcasted_iota(jnp.int32, (ST_T, CH_T), 0) // SSM_N
    rt = lax.broadcasted_iota(jnp.int32, (ST_T, CH_T), 1) // SSM_P
    rep = lax.broadcasted_iota(jnp.int32, (SSM_N, ST_T), 0) == lax.broadcasted_iota(jnp.int32, (SSM_N, ST_T), 1) % SSM_N
    rep_t = lax.broadcasted_iota(jnp.int32, (ST_T, SSM_N), 0) % SSM_N == lax.broadcasted_iota(jnp.int32, (ST_T, SSM_N), 1)
    return r == l, lt == rt, rep, rep_t


def _ssm_prep(lam_re, lam_im, log_step, b_re, b_im, c_re, c_im, seg):
    def body(lr_ref, li_ref, ls_ref, br_ref, bi_ref, cre_ref, cim_ref, lrr_ref, lir_ref, lsr_ref,
             ar_ref, ai_ref, pr_ref, pi_ref, bcat_ref, bcat_t_ref, ccat_ref, ccat_t_ref):
        _, _, _, _, cr, ci = _discretise(lr_ref[...], li_ref[...], ls_ref[...])
        cr, ci = _per_channel(cr), _per_channel(ci)
        br, bi = br_ref[...], bi_ref[...]
        bb_re = cr * br - ci * bi
        bb_im = cr * bi + ci * br
        same, same_t, rep, rep_t = _tile_masks()
        rep, rep_t = rep.astype(BF16), rep_t.astype(BF16)
        for j in range(N_GT):
            rows = slice(j * CH_T, (j + 1) * CH_T)
            for wide, tall, parts in ((bcat_ref, bcat_t_ref, (bb_re[rows], bb_im[rows])),
                                      (ccat_t_ref, ccat_ref, (cre_ref[rows, :], -cim_ref[rows, :]))):
                for k, part in enumerate(parts):
                    p16 = part.astype(BF16)
                    wide[j, :, k * ST_T:(k + 1) * ST_T] = jnp.where(same, _mm(p16, rep), 0.0).astype(BF16)
                    tall[j, k * ST_T:(k + 1) * ST_T, :] = jnp.where(same_t, _mm_nt(rep_t, p16), 0.0).astype(BF16)
        stepr = jnp.exp(lsr_ref[...])
        k = (lax.broadcasted_iota(jnp.int32, (8, N_STATE), 0) + 1).astype(F32)
        magk = jnp.exp(k * (lrr_ref[...] * stepr))
        ang = k * (lir_ref[...] * stepr)
        pr_ref[0:8, :] = magk * jnp.cos(ang)
        pi_ref[0:8, :] = magk * jnp.sin(ang)
        n = 8
        while n < seg:
            tr, ti = pr_ref[n - 1:n, :], pi_ref[n - 1:n, :]
            xr, xi = pr_ref[0:n, :], pi_ref[0:n, :]
            pr_ref[n:2 * n, :] = xr * tr - xi * ti
            pi_ref[n:2 * n, :] = xr * ti + xi * tr
            n *= 2
        ar_ref[...] = pr_ref[0:1, :]
        ai_ref[...] = pi_ref[0:1, :]

    row = jax.ShapeDtypeStruct((1, N_STATE), F32)
    pw = jax.ShapeDtypeStruct((seg, N_STATE), F32)
    wide = jax.ShapeDtypeStruct((N_GT, CH_T, 2 * ST_T), BF16)
    tall = jax.ShapeDtypeStruct((N_GT, 2 * ST_T, CH_T), BF16)
    vm = pl.BlockSpec(memory_space=pltpu.VMEM)
    step_row = jnp.broadcast_to(log_step, (SSM_G, SSM_N)).reshape(1, N_STATE)
    return pl.pallas_call(
        body, name="ssm_prep", out_shape=(row, row, pw, pw, wide, tall, tall, wide),
        in_specs=[vm] * 10, out_specs=(vm,) * 8,
    )(lam_re, lam_im, log_step, b_re, b_im, c_re, c_im, lam_re.reshape(1, N_STATE), lam_im.reshape(1, N_STATE),
      step_row)


def _seg_rows(t):
    if isinstance(t, int):
        return pl.ds(t * N_SEG, N_SEG)
    return pl.ds(pl.multiple_of(t * N_SEG, N_SEG), N_SEG)


def _scan_forward(xs, a_re, a_im, pw_re, pw_im, cs, seg):
    are = jnp.broadcast_to(a_re, (N_SEG, ST_T))
    aim = jnp.broadcast_to(a_im, (N_SEG, ST_T))

    def steps(k, carry):
        xr, xi = carry
        for j in range(SCAN_UNROLL):
            r = pl.multiple_of((k * SCAN_UNROLL + j) * N_SEG, N_SEG)
            nr = are * xr - aim * xi + xs[pl.ds(r, N_SEG), 0:ST_T]
            ni = are * xi + aim * xr + xs[pl.ds(r, N_SEG), ST_T:2 * ST_T]
            xs[pl.ds(r, N_SEG), 0:ST_T] = nr
            xs[pl.ds(r, N_SEG), ST_T:2 * ST_T] = ni
            xr, xi = nr, ni
        return xr, xi

    zero = jnp.zeros((N_SEG, ST_T), F32)
    fr, fi = lax.fori_loop(0, seg // SCAN_UNROLL, steps, (zero, zero))
    sr, si = pw_re[seg - 1:seg, :], pw_im[seg - 1:seg, :]
    cr = jnp.zeros((1, ST_T), F32)
    ci = jnp.zeros((1, ST_T), F32)
    cs[0:1, :] = cr
    cs[8:9, :] = ci
    for s in range(1, N_SEG):
        ncr = sr * cr - si * ci + fr[s - 1:s, :]
        nci = sr * ci + si * cr + fi[s - 1:s, :]
        cr, ci = ncr, nci
        cs[s:s + 1, :] = cr
        cs[8 + s:9 + s, :] = ci
    car, cai = cs[0:8, :], cs[8:16, :]

    def fix(t, _):
        r = pl.multiple_of(t * N_SEG, N_SEG)
        pr, pi = pw_re[pl.ds(t, 1), :], pw_im[pl.ds(t, 1), :]
        xs[pl.ds(r, N_SEG), 0:ST_T] = xs[pl.ds(r, N_SEG), 0:ST_T] + (pr * car - pi * cai)
        xs[pl.ds(r, N_SEG), ST_T:2 * ST_T] = xs[pl.ds(r, N_SEG), ST_T:2 * ST_T] + (pr * cai + pi * car)
        return 0

    lax.fori_loop(0, seg, fix, 0, unroll=SCAN_UNROLL)


def _interleave(src, dst, seg):
    for s in range(N_SEG):
        dst[pl.ds(s, seg, stride=N_SEG), :] = src[s]


def _deinterleave(src, seg, s):
    return src[pl.ds(s, seg, stride=N_SEG), :]


def _ssm_forward(u, bcat, ccat, a_re, a_im, pw_re, pw_im, d_row, late, seg):
    bl = u.shape[0]
    rows = N_SEG * seg
    n = len(late)
    steps = bl * N_GT

    def body(*refs):
        u_ref, b_ref, c_ref, ar_ref, ai_ref, pr_ref, pi_ref, d_ref = refs[:8]
        late_r = refs[8:8 + n]
        y_ref, xs_ref, cs_ref = refs[8 + n:11 + n]
        gath, cast = refs[11 + n:11 + 2 * n], refs[11 + 2 * n:11 + 3 * n]
        send_sems, recv_sems, local_sems, ui, yi = refs[11 + 3 * n:]
        step = pl.program_id(0) * N_GT + pl.program_id(1)
        start, relay, finish = _gather_phases(late_r, gath, cast, send_sems, recv_sems, local_sems)
        pl.when(step == 0)(start)
        _interleave(u_ref.at[0], ui, seg)
        u = ui[...]
        xs, cs = xs_ref.at[0, 0], cs_ref.at[0, 0]
        xs[...] = _mm(u, b_ref[0])
        _scan_forward(xs, ar_ref[...], ai_ref[...], pr_ref, pi_ref, cs, seg)
        yi[...] = _mm(xs[...], c_ref[0]) + d_ref[...] * u
        for s in range(N_SEG):
            y_ref[0, s] = _deinterleave(yi, seg, s)
        pl.when(step == steps // 2)(relay)
        pl.when(step == steps - 1)(finish)

    state = lambda r, c: pl.BlockSpec((1, 1, r, c), lambda b, j: (b, j, 0, 0))
    act = pl.BlockSpec((1, N_SEG, seg, CH_T), lambda b, j: (b, 0, 0, j))
    g_specs, g_shapes, g_scratch = _gather_operands(late)
    res = pl.pallas_call(
        body, name="ssm_forward", grid=(bl, N_GT),
        in_specs=[act,
                  pl.BlockSpec((1, CH_T, 2 * ST_T), lambda b, j: (j, 0, 0)),
                  pl.BlockSpec((1, 2 * ST_T, CH_T), lambda b, j: (j, 0, 0)),
                  pl.BlockSpec((1, ST_T), lambda b, j: (0, j)), pl.BlockSpec((1, ST_T), lambda b, j: (0, j)),
                  pl.BlockSpec((seg, ST_T), lambda b, j: (0, j)), pl.BlockSpec((seg, ST_T), lambda b, j: (0, j)),
                  pl.BlockSpec((1, CH_T), lambda b, j: (0, j))]
        + [pl.BlockSpec(s.shape, lambda b, j: (0, 0)) for s in late],
        out_specs=(act, state(rows, 2 * ST_T), state(16, ST_T)) + g_specs,
        out_shape=(jax.ShapeDtypeStruct((bl, N_SEG, seg, D_SSM), F32),
                   jax.ShapeDtypeStruct((bl, N_GT, rows, 2 * ST_T), F32),
                   jax.ShapeDtypeStruct((bl, N_GT, 16, ST_T), F32)) + g_shapes,
        scratch_shapes=g_scratch + [pltpu.VMEM((rows, CH_T), F32), pltpu.VMEM((rows, CH_T), F32)],
        compiler_params=_tc_params(("arbitrary", "arbitrary")),
    )(u, bcat, ccat, a_re, a_im, pw_re, pw_im, d_row, *late)
    return res[:3], list(res[3:])


def _ssm_backward(u, dy, states, carries, bcat_t, ccat_t, a_re, a_im, pw_re, pw_im, d_row, late16, late32, seg):
    bl = u.shape[0]
    rows = N_SEG * seg
    n = len(late16)
    grid_steps = N_GT * bl

    def body(*refs):
        u_ref, dy_ref, xs_ref, cs_ref, bt_ref, ct_ref, ar_ref, ai_ref, pr_ref, pi_ref, d_ref = refs[:11]
        g16_r, g32_r = refs[11:11 + n], refs[11 + n:11 + 2 * n]
        du_ref, db_ref, dc_ref, dar_ref, dai_ref, dd_ref = refs[11 + 2 * n:17 + 2 * n]
        red = refs[17 + 2 * n:17 + 3 * n]
        own16, recv1, send2, recv2, own32 = (refs[17 + 3 * n + k * n:17 + 3 * n + (k + 1) * n] for k in range(5))
        s_send, s_recv, s_local, ls, cl, ui, dyi, dui = refs[17 + 8 * n:]
        b = pl.program_id(1)
        step = pl.program_id(0) * bl + b
        start, middle, total, finish = _hosted_reduce_phases(g16_r, g32_r, red, own16, recv1, send2, recv2, own32,
                                                             s_send, s_recv, s_local)
        pl.when(step == 0)(start)
        pl.when(step == grid_steps // 4)(middle)
        pl.when(step == (grid_steps * 3) // 4)(total)
        pl.when(step == grid_steps - 1)(finish)
        _interleave(u_ref.at[0], ui, seg)
        _interleave(dy_ref.at[0], dyi, seg)
        u = ui[...]
        dy = dyi[...]
        xs, cs = xs_ref.at[0, 0], cs_ref.at[0, 0]
        ls[...] = _mm(dy, ct_ref[0])
        are = jnp.broadcast_to(ar_ref[...], (N_SEG, ST_T))
        aim = jnp.broadcast_to(ai_ref[...], (N_SEG, ST_T))

        def steps(k, carry):
            lr, li = carry
            for j in range(SCAN_UNROLL):
                r = pl.multiple_of((seg - 1 - (k * SCAN_UNROLL + j)) * N_SEG, N_SEG)
                nr = are * lr + aim * li + ls[pl.ds(r, N_SEG), 0:ST_T]
                ni = are * li - aim * lr + ls[pl.ds(r, N_SEG), ST_T:2 * ST_T]
                ls[pl.ds(r, N_SEG), 0:ST_T] = nr
                ls[pl.ds(r, N_SEG), ST_T:2 * ST_T] = ni
                lr, li = nr, ni
            return lr, li

        zero = jnp.zeros((N_SEG, ST_T), F32)
        fr, fi = lax.fori_loop(0, seg // SCAN_UNROLL, steps, (zero, zero))
        sr, si = pr_ref[seg - 1:seg, :], pi_ref[seg - 1:seg, :]
        cr = jnp.zeros((1, ST_T), F32)
        ci = jnp.zeros((1, ST_T), F32)
        cl[7:8, :] = cr
        cl[15:16, :] = ci
        for s in range(N_SEG - 2, -1, -1):
            ncr = sr * cr + si * ci + fr[s + 1:s + 2, :]
            nci = sr * ci - si * cr + fi[s + 1:s + 2, :]
            cr, ci = ncr, nci
            cl[s:s + 1, :] = cr
            cl[8 + s:9 + s, :] = ci
        clr, cli = cl[0:8, :], cl[8:16, :]

        def fix_rows(rows, t, xpr, xpi, acc):
            dr, di = acc
            pr, pi = pr_ref[pl.ds(seg - 1 - t, 1), :], pi_ref[pl.ds(seg - 1 - t, 1), :]
            lr = ls[rows, 0:ST_T] + (pr * clr + pi * cli)
            li = ls[rows, ST_T:2 * ST_T] + (pr * cli - pi * clr)
            ls[rows, 0:ST_T] = lr
            ls[rows, ST_T:2 * ST_T] = li
            return dr + (lr * xpr + li * xpi), di + (li * xpr - lr * xpi)

        def fix_at(t, acc):
            prev = _seg_rows(t - 1)
            return fix_rows(_seg_rows(t), t, xs[prev, 0:ST_T], xs[prev, ST_T:2 * ST_T], acc)

        def fix(k, acc):
            for j in range(SCAN_UNROLL):
                acc = fix_at(k * SCAN_UNROLL + j, acc)
            return acc

        acc = fix_rows(pl.ds(0, N_SEG), 0, cs[0:8, :], cs[8:16, :], (zero, zero))
        for t in range(1, SCAN_UNROLL):
            acc = fix_at(t, acc)
        dr, di = lax.fori_loop(1, seg // SCAN_UNROLL, fix, acc)
        dar = jnp.sum(dr, axis=0, keepdims=True)
        dai = jnp.sum(di, axis=0, keepdims=True)
        lall = ls[...]
        dui[...] = _mm(lall, bt_ref[0]) + d_ref[...] * dy
        for s in range(N_SEG):
            du_ref[0, s] = _deinterleave(dui, seg, s).astype(BF16)
        dbp = _mm_tn(u, lall)
        dcp = _mm_tn(dy, xs[...])
        ddp = jnp.sum(dy * u, axis=0, keepdims=True)

        @pl.when(b == 0)
        def _():
            db_ref[0] = dbp
            dc_ref[0] = dcp
            dar_ref[...] = dar
            dai_ref[...] = dai
            dd_ref[...] = ddp

        @pl.when(b != 0)
        def _():
            db_ref[0] += dbp
            dc_ref[0] += dcp
            dar_ref[...] += dar
            dai_ref[...] += dai
            dd_ref[...] += ddp

    tile3 = lambda r, c: pl.BlockSpec((1, r, c), lambda j, b: (j, 0, 0))
    lane = lambda r, c: pl.BlockSpec((r, c), lambda j, b: (0, j))
    act = pl.BlockSpec((1, N_SEG, seg, CH_T), lambda j, b: (b, 0, 0, j))
    state = lambda r, c: pl.BlockSpec((1, 1, r, c), lambda j, b: (b, j, 0, 0))
    r_in, r_out, r_shapes, r_scratch = _hosted_reduce_operands(late16, lambda s: pl.BlockSpec(s, lambda j, b: (0, 0)))
    res = pl.pallas_call(
        body, name="ssm_backward", grid=(N_GT, bl),
        in_specs=[act, act, state(rows, 2 * ST_T), state(16, ST_T), tile3(2 * ST_T, CH_T), tile3(CH_T, 2 * ST_T),
                  lane(1, ST_T), lane(1, ST_T), lane(seg, ST_T), lane(seg, ST_T), lane(1, CH_T)] + r_in,
        out_specs=(act, tile3(CH_T, 2 * ST_T), tile3(CH_T, 2 * ST_T), lane(1, ST_T), lane(1, ST_T), lane(1, CH_T))
        + r_out,
        out_shape=(jax.ShapeDtypeStruct((bl, N_SEG, seg, D_SSM), BF16),
                   jax.ShapeDtypeStruct((N_GT, CH_T, 2 * ST_T), F32), jax.ShapeDtypeStruct((N_GT, CH_T, 2 * ST_T), F32),
                   jax.ShapeDtypeStruct((1, N_STATE), F32), jax.ShapeDtypeStruct((1, N_STATE), F32),
                   jax.ShapeDtypeStruct((1, D_SSM), F32)) + r_shapes,
        scratch_shapes=r_scratch + [pltpu.VMEM((rows, 2 * ST_T), F32), pltpu.VMEM((16, ST_T), F32)]
        + [pltpu.VMEM((rows, CH_T), F32)] * 3,
        compiler_params=_tc_params(("arbitrary", "arbitrary")),
    )(u, dy, states, carries, bcat_t, ccat_t, a_re, a_im, pw_re, pw_im, d_row, *late16, *late32)
    return res[:6], list(res[6:])


def _ssm_param_grads(lam_re, lam_im, log_step, b_re, b_im, da_re, da_im, d_bcat, d_ccat_t):
    def body(lr_ref, li_ref, ls_ref, br_ref, bi_ref, gar_ref, gai_ref, gbcat_ref, gccat_ref,
             dlr_ref, dli_ref, dls_ref, dbr_ref, dbi_ref, dcr_ref, dci_ref, gbr_s, gbi_s):
        same, _, _, rep_t = _tile_masks()
        rep_t = rep_t.astype(F32)
        for j in range(N_GT):
            rows = slice(j * CH_T, (j + 1) * CH_T)
            for src, dsts in ((gbcat_ref, (gbr_s, gbi_s)), (gccat_ref, (dcr_ref, dci_ref))):
                for k, dst in enumerate(dsts):
                    blk = jnp.where(same, src[j, :, k * ST_T:(k + 1) * ST_T], 0.0)
                    dst[rows, :] = jnp.dot(blk, rep_t, precision=lax.Precision.HIGHEST, preferred_element_type=F32)
        dci_ref[...] = -dci_ref[...]
        lr, li = lr_ref[...], li_ref[...]
        step, ar, ai, den, cr, ci = _discretise(lr, li, ls_ref[...])
        crb, cib = _per_channel(cr), _per_channel(ci)
        br, bi = br_ref[...], bi_ref[...]
        gbr, gbi = gbr_s[...], gbi_s[...]
        dbr_ref[...] = crb * gbr + cib * gbi
        dbi_ref[...] = crb * gbi - cib * gbr
        over_channels = lambda t: jnp.sum(t.reshape(SSM_G, SSM_P, SSM_N), axis=1)
        gcr = over_channels(br * gbr + bi * gbi)
        gci = over_channels(br * gbi - bi * gbr)
        ilr, ili = lr / den, -li / den
        gar = gar_ref[...] + (ilr * gcr + ili * gci)
        gai = gai_ref[...] + (ilr * gci - ili * gcr)
        qr, qi = cr * ilr - ci * ili, cr * ili + ci * ilr
        glr = -(qr * gcr + qi * gci)
        gli = -(qr * gci - qi * gcr)
        gwr = ar * gar + ai * gai
        gwi = ar * gai - ai * gar
        dlr_ref[...] = glr + step * gwr
        dli_ref[...] = gli + step * gwi
        dls_ref[...] = jnp.sum(lr * gwr + li * gwi, axis=-1, keepdims=True) * step

    lam = jax.ShapeDtypeStruct((SSM_G, SSM_N), F32)
    mat = jax.ShapeDtypeStruct((SSM_G * SSM_P, SSM_N), F32)
    vm = pl.BlockSpec(memory_space=pltpu.VMEM)
    return pl.pallas_call(
        body, name="ssm_param_grads", out_shape=(lam, lam, jax.ShapeDtypeStruct((SSM_G, 1), F32), mat, mat, mat, mat),
        in_specs=[vm] * 9, out_specs=(vm,) * 7,
        scratch_shapes=[pltpu.VMEM((SSM_G * SSM_P, SSM_N), F32), pltpu.VMEM((SSM_G * SSM_P, SSM_N), F32)],
    )(lam_re, lam_im, log_step, b_re, b_im, da_re, da_im, d_bcat, d_ccat_t)


ROWS4 = Q_PER_KV * ATT_BLOCK


def _att_dist_mask(first_block):
    qi = lax.broadcasted_iota(jnp.int32, (ROWS4, 2 * ATT_BLOCK), 0) & (ATT_BLOCK - 1)
    si = lax.broadcasted_iota(jnp.int32, (ROWS4, 2 * ATT_BLOCK), 1)
    dist = qi + ATT_BLOCK - si
    valid = (dist >= 0) & (dist < ATT_BLOCK) & ((si >= ATT_BLOCK) | jnp.logical_not(first_block))
    return dist.astype(F32), valid


def _stack_heads(x, kv):
    return jnp.concatenate([x[:, (kv * Q_PER_KV + g) * HEAD_DIM:(kv * Q_PER_KV + g + 1) * HEAD_DIM]
                            for g in range(Q_PER_KV)], axis=0)


def _stack_cols(x, kv):
    return jnp.concatenate([x[:, kv * Q_PER_KV + g:kv * Q_PER_KV + g + 1] for g in range(Q_PER_KV)], axis=0)


def _per_head_col(vals):
    return jnp.concatenate([jnp.full((ATT_BLOCK, 1), v, F32) for v in vals], axis=0)


def _attn_forward(q, k, v, sinks, bl, nb):
    t = q.shape[0]

    def body(sink_ref, q_ref, kp_ref, kc_ref, vp_ref, vc_ref, o_ref, lse_ref):
        i = pl.program_id(1)
        dist4, valid4 = _att_dist_mask(i == 0)
        dist, valid = dist4[0:ATT_BLOCK, :], valid4[0:ATT_BLOCK, :]
        kk = jnp.concatenate([kp_ref[...], kc_ref[...]], axis=0)
        vv = jnp.concatenate([vp_ref[...], vc_ref[...]], axis=0)
        qv = q_ref[...]
        for h in range(N_HEADS):
            kv = h // Q_PER_KV
            slope = 2.0 ** (-(h + 1))
            qh = qv[:, h * HEAD_DIM:(h + 1) * HEAD_DIM]
            kh = kk[:, kv * HEAD_DIM:(kv + 1) * HEAD_DIM]
            vh = vv[:, kv * HEAD_DIM:(kv + 1) * HEAD_DIM]
            s = _mm_nt(qh, kh) * ATT_SCALE - slope * dist
            s = jnp.where(valid, s, NEG_BIG)
            sink = sink_ref[h]
            m = jnp.maximum(jnp.max(s, axis=-1, keepdims=True), sink)
            e = jnp.exp(s - m)
            den = jnp.sum(e, axis=-1, keepdims=True) + jnp.exp(sink - m)
            o_ref[:, h * HEAD_DIM:(h + 1) * HEAD_DIM] = _mm(e, vh) * (1.0 / den)
            lse_ref[:, h:h + 1] = m + jnp.log(den)

    cur = lambda w: pl.BlockSpec((ATT_BLOCK, w), lambda b, i: (b * nb + i, 0))
    prev = lambda w: pl.BlockSpec((ATT_BLOCK, w), lambda b, i: (b * nb + jnp.maximum(i - 1, 0), 0))
    return pl.pallas_call(
        body, name="attn_forward", grid=(bl, nb),
        in_specs=[pl.BlockSpec(memory_space=pltpu.SMEM), cur(512), prev(128), cur(128), prev(128), cur(128)],
        out_specs=(cur(512), cur(N_HEADS)),
        out_shape=(jax.ShapeDtypeStruct((t, D_ATTN), F32), jax.ShapeDtypeStruct((t, N_HEADS), F32)),
        compiler_params=_tc_params(("arbitrary", "arbitrary")),
    )(sinks, q, k, k, v, v)


def _attn_backward(q, k, v, o, do, lse, sinks, bl, nb):
    t = q.shape[0]

    def body(sink_ref, qc_ref, kp_ref, kc_ref, vp_ref, vc_ref, oc_ref, doc_ref, lc_ref,
             dq_ref, dk_ref, dv_ref, ds_ref, dk_carry, dv_carry):
        b, i = pl.program_id(0), pl.program_id(1)
        live = i < nb

        @pl.when(i == 0)
        def _():
            dk_carry[...] = jnp.zeros((ATT_BLOCK, KV_HEADS * HEAD_DIM), F32)
            dv_carry[...] = jnp.zeros((ATT_BLOCK, KV_HEADS * HEAD_DIM), F32)

        dist, valid = _att_dist_mask(i == 0)
        valid = valid & live
        kk = jnp.concatenate([kp_ref[...], kc_ref[...]], axis=0)
        vv = jnp.concatenate([vp_ref[...], vc_ref[...]], axis=0)
        qc, oc, doc, lc = qc_ref[...], oc_ref[...], doc_ref[...], lc_ref[...]
        dsink_cols, dq_parts = [], []
        for kv in range(KV_HEADS):
            heads = range(kv * Q_PER_KV, (kv + 1) * Q_PER_KV)
            cols = slice(kv * HEAD_DIM, (kv + 1) * HEAD_DIM)
            kh, vh = kk[:, cols], vv[:, cols]
            slope = _per_head_col([2.0 ** (-(h + 1)) for h in heads])
            sink = _per_head_col([sink_ref[h] for h in heads])
            q4, do4 = _stack_heads(qc, kv), _stack_heads(doc, kv)
            delta = jnp.sum(do4 * _stack_heads(oc, kv), axis=-1, keepdims=True)
            lse4 = _stack_cols(lc, kv)
            s = _mm_nt(q4, kh) * ATT_SCALE - slope * dist
            p = jnp.where(valid, jnp.exp(s - lse4), 0.0)
            dsc = p * (_mm_nt(do4, vh) - delta)
            dq4 = _mm(dsc, kh) * ATT_SCALE
            dk2 = _mm_tn(dsc, q4) * ATT_SCALE
            dv2 = _mm_tn(p, do4)
            dsink4 = jnp.where(live, jnp.exp(sink - lse4) * delta, 0.0)
            dk_ref[:, cols] = dk_carry[:, cols] + dk2[0:ATT_BLOCK, :]
            dv_ref[:, cols] = dv_carry[:, cols] + dv2[0:ATT_BLOCK, :]
            dk_carry[:, cols] = dk2[ATT_BLOCK:, :]
            dv_carry[:, cols] = dv2[ATT_BLOCK:, :]
            for g, h in enumerate(heads):
                rows = slice(g * ATT_BLOCK, (g + 1) * ATT_BLOCK)
                dq_parts.append((h, dq4[rows, :]))
                dsink_cols.append(-jnp.sum(dsink4[rows, :], axis=0, keepdims=True))
        dsink = jnp.concatenate(dsink_cols, axis=1)

        @pl.when(live)
        def _():
            for h, part in dq_parts:
                dq_ref[:, h * HEAD_DIM:(h + 1) * HEAD_DIM] = part

        @pl.when((b == 0) & (i == 0))
        def _():
            ds_ref[...] = dsink

        @pl.when((b != 0) | (i != 0))
        def _():
            ds_ref[...] += dsink

    cur_i = lambda i: jnp.minimum(i, nb - 1)
    cur = lambda w: pl.BlockSpec((ATT_BLOCK, w), lambda b, i: (b * nb + cur_i(i), 0))
    prev = lambda w: pl.BlockSpec((ATT_BLOCK, w), lambda b, i: (b * nb + jnp.maximum(cur_i(i) - 1, 0), 0))
    behind = lambda w: pl.BlockSpec((ATT_BLOCK, w), lambda b, i: (b * nb + jnp.maximum(i - 1, 0), 0))
    return pl.pallas_call(
        body, name="attn_backward", grid=(bl, nb + 1),
        in_specs=[pl.BlockSpec(memory_space=pltpu.SMEM), cur(512), prev(128), cur(128), prev(128), cur(128),
                  cur(512), cur(512), cur(N_HEADS)],
        out_specs=(cur(512), behind(128), behind(128), pl.BlockSpec((1, N_HEADS), lambda b, i: (0, 0))),
        out_shape=(jax.ShapeDtypeStruct((t, D_ATTN), F32), jax.ShapeDtypeStruct((t, 128), F32),
                   jax.ShapeDtypeStruct((t, 128), F32), jax.ShapeDtypeStruct((1, N_HEADS), F32)),
        scratch_shapes=[pltpu.VMEM((ATT_BLOCK, KV_HEADS * HEAD_DIM), F32), pltpu.VMEM((ATT_BLOCK, KV_HEADS * HEAD_DIM), F32)],
        compiler_params=_tc_params(("arbitrary", "arbitrary")),
    )(sinks, q, k, k, v, v, o, do, lse)


def _mix_forward_backward(x2, y2, z_ssm, attn, z_attn, p2, target2, w_glu, b_glu, w_out, g_post, w_gate, b_gate,
                          w_proj, tm):
    t = x2.shape[0]

    def body(x_ref, y_ref, zs_ref, at_ref, za_ref, p_ref, tg_ref,
             wglu_ref, bglu_ref, wout_ref, gpost_ref, wgate_ref, bgate_ref, wproj_ref,
             loss_ref, dh1_ref, dy_ref, dzs_ref, dat_ref, dza_ref,
             dwglu_ref, dbglu_ref, dwout_ref, dgpost_ref, dwgate_ref, dbgate_ref, dwproj_ref,
             dwout16_ref, dwgate16_ref, dwproj16_ref, dwglu16_ref):
        i = pl.program_id(0)
        gpost = gpost_ref[...]

        @pl.when(i == 0)
        def _():
            for ref in (dwglu_ref, dbglu_ref, dwout_ref, dgpost_ref, dwgate_ref, dbgate_ref, dwproj_ref, loss_ref):
                ref[...] = jnp.zeros(ref.shape, F32)

        def chain(rows):
            y = y_ref[rows, :]
            u3 = GELU_C * (y + GELU_K * y * y * y)
            th = jnp.tanh(u3)
            gl = 0.5 * y * (1.0 + th)
            a = _mm(gl, wglu_ref[...]) + bglu_ref[...]
            sa = _sigmoid(a)
            glu = gl * sa
            zs = zs_ref[rows, :]
            sgs = _sigmoid(zs)
            ssm_out = glu * (zs * sgs)
            za = za_ref[rows, :]
            sga = _sigmoid(za)
            at = at_ref[rows, :]
            attn_out = at * (za * sga)
            cat = jnp.concatenate([ssm_out, attn_out], axis=-1).astype(BF16)
            mixed = _mm(cat, wout_ref[...])
            r2 = lax.rsqrt(jnp.mean(mixed * mixed, axis=-1, keepdims=True) + EPS)
            nhat = mixed * r2
            h1 = x_ref[rows, :] + nhat * gpost
            gate = _sigmoid(_mm(h1, wgate_ref[...]) + bgate_ref[...])
            pv = p_ref[rows, :]
            pp = _mm(pv, wproj_ref[...])
            h2 = h1 + gate * pp
            err = h2 - tg_ref[rows, :]
            loss_part = jnp.sum(jnp.sum(err * err, axis=-1, keepdims=True), axis=0, keepdims=True) * (0.5 / D_MODEL)
            dh2 = err * (1.0 / D_MODEL)
            dgp = dh2 * pp * gate * (1.0 - gate)
            dpp = dh2 * gate
            dh1 = dh2 + _mm_nt(dgp, wgate_ref[...])
            dh1_ref[rows, :] = dh1
            dnhat = dh1 * gpost
            dmixed = r2 * (dnhat - nhat * jnp.mean(dnhat * nhat, axis=-1, keepdims=True))
            dcat = _mm_nt(dmixed, wout_ref[...])
            dso, dao = dcat[:, 0:D_SSM], dcat[:, D_SSM:]
            dat_ref[rows, :] = dao * (za * sga)
            dza_ref[rows, :] = (dao * at * (sga * (1.0 + za * (1.0 - sga)))).astype(BF16)
            dzs_ref[rows, :] = (dso * glu * (sgs * (1.0 + zs * (1.0 - sgs)))).astype(BF16)
            dglu = dso * (zs * sgs)
            da = dglu * gl * sa * (1.0 - sa)
            dgl = dglu * sa + _mm_nt(da, wglu_ref[...])
            dgelu = 0.5 * (1.0 + th) + 0.5 * y * (1.0 - th * th) * (GELU_C * (1.0 + 3.0 * GELU_K * y * y))
            dy_ref[rows, :] = dgl * dgelu
            return dict(gl=gl.astype(BF16), da=da.astype(BF16), cat=cat, dmixed=dmixed.astype(BF16),
                        h1=h1.astype(BF16), dgp=dgp.astype(BF16), pv=pv.astype(BF16), dpp=dpp.astype(BF16),
                        dbglu=jnp.sum(da, axis=0, keepdims=True), dgpost=jnp.sum(dh1 * nhat, axis=0, keepdims=True),
                        dbgate=jnp.sum(dgp, axis=0, keepdims=True), loss=loss_part)

        groups = [chain(slice(k * (tm // MIX_GROUPS), (k + 1) * (tm // MIX_GROUPS))) for k in range(MIX_GROUPS)]
        rows_of = lambda name: jnp.concatenate([g[name] for g in groups], axis=0)
        total = lambda name: sum(g[name] for g in groups)
        parts = (
            (dwglu_ref, _mm_tn(rows_of("gl"), rows_of("da"))), (dbglu_ref, total("dbglu")),
            (dwout_ref, _mm_tn(rows_of("cat"), rows_of("dmixed"))), (dgpost_ref, total("dgpost")),
            (dwgate_ref, _mm_tn(rows_of("h1"), rows_of("dgp"))), (dbgate_ref, total("dbgate")),
            (dwproj_ref, _mm_tn(rows_of("pv"), rows_of("dpp"))), (loss_ref, total("loss")),
        )

        for ref, val in parts:
            ref[...] += val

        @pl.when(i == t // tm - 1)
        def _():
            for ref16, ref in ((dwout16_ref, dwout_ref), (dwgate16_ref, dwgate_ref), (dwproj16_ref, dwproj_ref),
                               (dwglu16_ref, dwglu_ref)):
                def to16(r, ref16=ref16, ref=ref):
                    ref16[r, :] = ref[r, :].astype(BF16)

                _row_chunks(ref.shape[0], to16)

    row = lambda w: pl.BlockSpec((tm, w), lambda i: (i, 0))
    acc = lambda r, c, dt=F32: (_const_spec((r, c)), jax.ShapeDtypeStruct((r, c), dt))
    accs = [acc(D_SSM, D_SSM), acc(1, D_SSM), acc(D_MODEL, D_MODEL), acc(1, D_MODEL), acc(D_MODEL, D_MODEL),
            acc(1, D_MODEL), acc(D_PLE, D_MODEL),
            acc(D_MODEL, D_MODEL, BF16), acc(D_MODEL, D_MODEL, BF16), acc(D_PLE, D_MODEL, BF16), acc(D_SSM, D_SSM, BF16)]
    return pl.pallas_call(
        body, name="mix_forward_backward", grid=(t // tm,),
        in_specs=[row(D_MODEL), row(512), row(512), row(512), row(512), row(D_PLE), row(D_MODEL),
                  _const_spec((D_SSM, D_SSM)), _const_spec((1, D_SSM)), _const_spec((D_MODEL, D_MODEL)),
                  _const_spec((1, D_MODEL)), _const_spec((D_MODEL, D_MODEL)), _const_spec((1, D_MODEL)),
                  _const_spec((D_PLE, D_MODEL))],
        out_specs=(_const_spec((1, 1)), row(D_MODEL), row(512), row(512), row(512), row(512))
        + tuple(a[0] for a in accs),
        out_shape=(jax.ShapeDtypeStruct((1, 1), F32), jax.ShapeDtypeStruct((t, D_MODEL), F32),
                   jax.ShapeDtypeStruct((t, 512), F32),
                   jax.ShapeDtypeStruct((t, 512), BF16), jax.ShapeDtypeStruct((t, 512), F32),
                   jax.ShapeDtypeStruct((t, 512), BF16)) + tuple(a[1] for a in accs),
        compiler_params=_tc_params(("arbitrary",)),
    )(x2, y2, z_ssm, attn, z_attn, p2, target2, w_glu, b_glu, w_out, g_post, w_gate, b_gate, w_proj)


def _in_backward(x2, dh1, du, dz_ssm, dq, dk, dv, dz_attn, g_pre, w_in, tm):
    t = x2.shape[0]

    def body(x_ref, dh1_ref, du_ref, dzs_ref, dq_ref, dk_ref, dv_ref, dza_ref, g_ref, w_ref,
             gx_ref, dw_ref, dg_ref, dw16_ref):
        i = pl.program_id(0)
        xv = x_ref[...]
        r = lax.rsqrt(jnp.mean(xv * xv, axis=-1, keepdims=True) + EPS)
        xhat = xv * r
        g = g_ref[...]
        hn = (xhat * g).astype(BF16)
        dproj = jnp.concatenate([du_ref[...].astype(BF16), dzs_ref[...].astype(BF16), dq_ref[...].astype(BF16),
                                 dk_ref[...].astype(BF16), dv_ref[...].astype(BF16), dza_ref[...].astype(BF16)],
                                axis=-1)
        dhn = _mm(dproj, w_ref[...])
        dxhat = dhn * g
        gx_ref[...] = dh1_ref[...] + r * (dxhat - xhat * jnp.mean(dxhat * xhat, axis=-1, keepdims=True))
        @pl.when(i == 0)
        def _():
            dw_ref[...] = jnp.zeros((D_IN, D_MODEL), F32)
            dg_ref[...] = jnp.zeros((1, D_MODEL), F32)

        dw_ref[...] += _mm_tn(dproj, hn)
        dg_ref[...] += jnp.sum(dhn * xhat, axis=0, keepdims=True)

        @pl.when(i == t // tm - 1)
        def _():
            def to16(r):
                dw16_ref[r, :] = dw_ref[r, :].astype(BF16)

            _row_chunks(D_IN, to16)

    row = lambda w: pl.BlockSpec((tm, w), lambda i: (i, 0))
    return pl.pallas_call(
        body, name="in_backward", grid=(t // tm,),
        in_specs=[row(D_MODEL), row(D_MODEL), row(512), row(512), row(512), row(128), row(128), row(512),
                  _const_spec((1, D_MODEL)), _const_spec((D_IN, D_MODEL))],
        out_specs=(row(D_MODEL), _const_spec((D_IN, D_MODEL)), _const_spec((1, D_MODEL)),
                   _const_spec((D_IN, D_MODEL))),
        out_shape=(jax.ShapeDtypeStruct((t, D_MODEL), F32), jax.ShapeDtypeStruct((D_IN, D_MODEL), F32),
                   jax.ShapeDtypeStruct((1, D_MODEL), F32), jax.ShapeDtypeStruct((D_IN, D_MODEL), BF16)),
        compiler_params=_tc_params(("arbitrary",)),
    )(x2, dh1, du, dz_ssm, dq, dk, dv, dz_attn, g_pre, w_in)


def _local_step(x, p, target, pre_norm_g, w_in, ssm_lam_re, ssm_lam_im, ssm_log_step, ssm_b_re, ssm_b_im, ssm_c_re,
                ssm_c_im, ssm_d, ssm_b_glu, attn_sinks, post_norm_g, pl_b_gate, late):
    bl, seq, _ = x.shape
    seg = seq // N_SEG
    nb = seq // ATT_BLOCK
    t = bl * seq
    x2 = x.reshape(t, D_MODEL)
    p2 = p.reshape(t, D_PLE)
    tg2 = target.reshape(t, D_MODEL)

    lam_re, lam_im = ssm_lam_re, ssm_lam_im
    log_step = ssm_log_step.reshape(SSM_G, 1)
    a_re_row, a_im_row, pw_re, pw_im, bcat, bcat_t, ccat, ccat_t = _ssm_prep(
        lam_re, lam_im, log_step, ssm_b_re, ssm_b_im, ssm_c_re, ssm_c_im, seg)
    d_row = ssm_d.reshape(1, D_SSM)

    segments = lambda a: a.reshape(bl, N_SEG, seg, D_SSM)
    u, z_ssm, q, k, v, z_attn = _in_proj(x2, pre_norm_g.reshape(1, D_MODEL), w_in, seg)
    (y, states, carries), gathered = _ssm_forward(
        segments(u), bcat, ccat, a_re_row, a_im_row, pw_re, pw_im, d_row, late, seg)
    w_out, w_gate, w_proj, w_glu = (_gathered_to_full(n, g) for n, g in zip(LATE_NAMES, gathered))
    sinks = attn_sinks.reshape(N_HEADS)
    attn, lse = _attn_forward(q, k, v, sinks, bl, nb)
    (loss, dh1, dy, dz_ssm, dattn, dz_attn, d_w_glu, d_b_glu, d_w_out, d_g_post, d_w_gate, d_b_gate,
     d_w_proj, *late16) = _mix_forward_backward(
        x2, y.reshape(t, D_SSM), z_ssm, attn, z_attn, p2, tg2, w_glu,
        ssm_b_glu.reshape(1, D_SSM), w_out, post_norm_g.reshape(1, D_MODEL), w_gate, pl_b_gate.reshape(1, D_MODEL),
        w_proj, seg)
    owned = lambda ds: [_full_to_owned(n, d) for n, d in zip(LATE_NAMES, ds)]
    dq, dk, dv, d_sinks = _attn_backward(q, k, v, attn, dattn, lse, sinks, bl, nb)
    (du, d_bcat, d_ccat_t, da_re, da_im, d_d), late_grads = _ssm_backward(
        segments(u), segments(dy), states, carries, bcat_t, ccat_t, a_re_row, a_im_row, pw_re, pw_im,
        d_row, owned(late16), owned((d_w_out, d_w_gate, d_w_proj, d_w_glu)), seg)
    grad_x, d_w_in, d_g_pre, d_w_in16 = _in_backward(
        x2, dh1, du.reshape(t, D_SSM), dz_ssm, dq, dk, dv, dz_attn, pre_norm_g.reshape(1, D_MODEL), w_in, seg)
    d_lam_re, d_lam_im, d_ls, d_b_re, d_b_im, d_c_re, d_c_im = _ssm_param_grads(
        lam_re, lam_im, log_step, ssm_b_re, ssm_b_im, da_re.reshape(SSM_G, SSM_N), da_im.reshape(SSM_G, SSM_N),
        d_bcat, d_ccat_t)
    grads = {
        "pre_norm_g": d_g_pre, "w_in": d_w_in, "w_in16": d_w_in16, "ssm_lam_re": d_lam_re, "ssm_lam_im": d_lam_im,
        "ssm_log_step": d_ls, "ssm_b_re": d_b_re, "ssm_b_im": d_b_im, "ssm_c_re": d_c_re, "ssm_c_im": d_c_im,
        "ssm_d": d_d, "ssm_b_glu": d_b_glu, "attn_sinks": d_sinks, "post_norm_g": d_g_post, "pl_b_gate": d_b_gate,
    }
    return loss, grad_x.reshape(bl, seq, D_MODEL), grads, late_grads


LATE_NAMES = ("w_out", "pl_w_gate", "pl_w_proj", "ssm_w_glu")
BIG_NAMES = ("w_in",) + LATE_NAMES
COL_SHARDED = {"w_in": D_IN // N_DEV, "pl_w_proj": D_MODEL // N_DEV}
WEIGHT_NAMES = ("pre_norm_g", "w_in", "ssm_lam_re", "ssm_lam_im", "ssm_log_step", "ssm_b_re", "ssm_b_im", "ssm_c_re",
                "ssm_c_im", "ssm_d", "ssm_w_glu", "ssm_b_glu", "attn_sinks", "w_out", "post_norm_g", "pl_w_proj",
                "pl_w_gate", "pl_b_gate")


TRANSPOSED = {"w_in": (0, 1), "ssm_b_re": (1, 2), "ssm_b_im": (1, 2)}


def _kernel_form(name, a):
    a = a[0]
    if name in TRANSPOSED:
        a = jnp.swapaxes(a, *TRANSPOSED[name])
    if name in ("ssm_b_re", "ssm_b_im", "ssm_c_re", "ssm_c_im"):
        a = a.reshape(SSM_G * SSM_P, SSM_N)
    return a


def _given_form(name, a, shape):
    if name in TRANSPOSED:
        i, j = TRANSPOSED[name]
        swapped = list(shape[1:])
        swapped[i], swapped[j] = swapped[j], swapped[i]
        return jnp.swapaxes(a.reshape(swapped), i, j).reshape(shape)
    return a.reshape(shape)


def _gathered_to_full(name, g):
    _, rows, cols = g.shape
    if name in COL_SHARDED:
        return jnp.swapaxes(g, 0, 1).reshape(rows, N_DEV * cols)
    return g.reshape(N_DEV * rows, cols)


def _full_to_owned(name, full):
    if name in COL_SHARDED:
        return jnp.swapaxes(full.reshape(full.shape[0], N_DEV, COL_SHARDED[name]), 0, 1)
    return full.reshape(N_DEV, full.shape[0] // N_DEV, full.shape[1])


def kernel(x, p, pre_norm_g, w_in, ssm_lam_re, ssm_lam_im, ssm_log_step, ssm_b_re, ssm_b_im, ssm_c_re, ssm_c_im, ssm_d, ssm_w_glu, ssm_b_glu, attn_sinks, w_out, post_norm_g, pl_w_proj, pl_w_gate, pl_b_gate, loss_target, m_pre_norm_g, m_w_in, m_ssm_lam_re, m_ssm_lam_im, m_ssm_log_step, m_ssm_b_re, m_ssm_b_im, m_ssm_c_re, m_ssm_c_im, m_ssm_d, m_ssm_w_glu, m_ssm_b_glu, m_attn_sinks, m_w_out, m_post_norm_g, m_pl_w_proj, m_pl_w_gate, m_pl_b_gate, v_pre_norm_g, v_w_in, v_ssm_lam_re, v_ssm_lam_im, v_ssm_log_step, v_ssm_b_re, v_ssm_b_im, v_ssm_c_re, v_ssm_c_im, v_ssm_d, v_ssm_w_glu, v_ssm_b_glu, v_attn_sinks, v_w_out, v_post_norm_g, v_pl_w_proj, v_pl_w_gate, v_pl_b_gate):
    w = dict(pre_norm_g=pre_norm_g, w_in=w_in, ssm_lam_re=ssm_lam_re, ssm_lam_im=ssm_lam_im, ssm_log_step=ssm_log_step,
             ssm_b_re=ssm_b_re, ssm_b_im=ssm_b_im, ssm_c_re=ssm_c_re, ssm_c_im=ssm_c_im, ssm_d=ssm_d, ssm_w_glu=ssm_w_glu,
             ssm_b_glu=ssm_b_glu, attn_sinks=attn_sinks, w_out=w_out, post_norm_g=post_norm_g, pl_w_proj=pl_w_proj,
             pl_w_gate=pl_w_gate, pl_b_gate=pl_b_gate)
    m = dict(pre_norm_g=m_pre_norm_g, w_in=m_w_in, ssm_lam_re=m_ssm_lam_re, ssm_lam_im=m_ssm_lam_im,
             ssm_log_step=m_ssm_log_step, ssm_b_re=m_ssm_b_re, ssm_b_im=m_ssm_b_im, ssm_c_re=m_ssm_c_re,
             ssm_c_im=m_ssm_c_im, ssm_d=m_ssm_d, ssm_w_glu=m_ssm_w_glu, ssm_b_glu=m_ssm_b_glu, attn_sinks=m_attn_sinks,
             w_out=m_w_out, post_norm_g=m_post_norm_g, pl_w_proj=m_pl_w_proj, pl_w_gate=m_pl_w_gate,
             pl_b_gate=m_pl_b_gate)
    v = dict(pre_norm_g=v_pre_norm_g, w_in=v_w_in, ssm_lam_re=v_ssm_lam_re, ssm_lam_im=v_ssm_lam_im,
             ssm_log_step=v_ssm_log_step, ssm_b_re=v_ssm_b_re, ssm_b_im=v_ssm_b_im, ssm_c_re=v_ssm_c_re,
             ssm_c_im=v_ssm_c_im, ssm_d=v_ssm_d, ssm_w_glu=v_ssm_w_glu, ssm_b_glu=v_ssm_b_glu, attn_sinks=v_attn_sinks,
             w_out=v_w_out, post_norm_g=v_post_norm_g, pl_w_proj=v_pl_w_proj, pl_w_gate=v_pl_w_gate,
             pl_b_gate=v_pl_b_gate)
    kf = lambda d: {n: _kernel_form(n, a) for n, a in d.items()}
    wk, mk, vk = kf(w), kf(m), kf(v)

    (gathered,) = _allgather_weights([wk["w_in"]])
    loss, grad_x, grads, g_late = _local_step(
        x, p[0], loss_target, wk["pre_norm_g"], gathered.reshape(D_IN, D_MODEL), wk["ssm_lam_re"], wk["ssm_lam_im"],
        wk["ssm_log_step"], wk["ssm_b_re"], wk["ssm_b_im"], wk["ssm_c_re"], wk["ssm_c_im"], wk["ssm_d"],
        wk["ssm_b_glu"], wk["attn_sinks"], wk["post_norm_g"], wk["pl_b_gate"], [wk[n] for n in LATE_NAMES])

    owned = lambda g: g.reshape(N_DEV, D_IN // N_DEV, D_MODEL)
    tiny_form = lambda d: [d[n].reshape(rows, cols) for n, rows, cols in TINY]
    med_form = lambda d: [d[n].reshape(N_DEV, rows // N_DEV, cols) for n, rows, cols in MEDIUM]
    g_big, loss, g_tiny, g_med = _reduce_final(
        [owned(grads["w_in16"])], [owned(grads["w_in"])], loss, tiny_form(grads), med_form(grads))
    names = BIG_NAMES + tuple(n for n, _, _ in TINY + MEDIUM)
    form = lambda d: [d[n] for n in BIG_NAMES] + tiny_form(d) + med_form(d)
    updated = _adamw_update(g_big + g_late + g_tiny + g_med, form(wk), form(mk), form(vk))
    vals = dict(zip(names, updated))
    results = [[_given_form(n, vals[n][kind], w[n].shape) for n in WEIGHT_NAMES] for kind in range(4)]
    return (loss.reshape(()), grad_x, *results[0], *results[1], *results[2], *results[3])
```

```python
import functools
import math

import jax
import jax.numpy as jnp
from jax import lax
from jax.experimental import pallas as pl
from jax.experimental.pallas import tpu as pltpu

F32 = jnp.float32
BF16 = jnp.bfloat16

D_MODEL = 1024
D_SSM = 512
D_ATTN = 512
SSM_P = 16
SSM_G = 32
SSM_N = 64
N_HEADS = 8
KV_HEADS = 2
Q_PER_KV = 4
HEAD_DIM = 64
ATT_BLOCK = 128
D_PLE = 256
D_IN = 2304
EPS = 1e-6
N_DEV = 8
N_SEG = 8
G_TILE = 8
N_GT = SSM_G // G_TILE
CH_T = G_TILE * SSM_P
ST_T = G_TILE * SSM_N
N_STATE = SSM_G * SSM_N
SCAN_UNROLL = 4
MIX_GROUPS = 1
LANES = 128
VMEM_LIMIT = 60 * 1024 * 1024

ADAM_LR = 0.001
ADAM_B1 = 0.9
ADAM_B2 = 0.999
ADAM_EPS = 1e-08
ADAM_WD = 0.01
ADAM_STEP = 10

GELU_C = math.sqrt(2.0 / math.pi)
GELU_K = 0.044715
ATT_SCALE = 1.0 / math.sqrt(HEAD_DIM)
NEG_BIG = -1e30


def _mm(a, b):
    return jnp.dot(a.astype(BF16), b.astype(BF16), preferred_element_type=F32)


def _mm_nt(a, b):
    return lax.dot_general(a.astype(BF16), b.astype(BF16), (((1,), (1,)), ((), ())), preferred_element_type=F32)


def _mm_tn(a, b):
    return lax.dot_general(a.astype(BF16), b.astype(BF16), (((0,), (0,)), ((), ())), preferred_element_type=F32)


def _sigmoid(x):
    return 1.0 / (1.0 + jnp.exp(-x))


def _tc_params(sem):
    return pltpu.CompilerParams(dimension_semantics=sem, vmem_limit_bytes=VMEM_LIMIT)


def _const_spec(shape):
    nd = len(shape)
    return pl.BlockSpec(shape, lambda *_: (0,) * nd)


def _mesh_pos():
    return lax.axis_index("x"), lax.axis_index("y"), lax.axis_index("c")


ROW_CHUNKS = (64, 32, 16)


def _row_chunk(nrows):
    return next((c for c in ROW_CHUNKS if nrows % c == 0), None)


def _row_chunks(nrows, fn, chunk=None, init=None):
    chunk = chunk or _row_chunk(nrows)

    def step(i, carry):
        rows = pl.ds(pl.multiple_of(i * chunk, chunk), chunk)
        if init is None:
            fn(rows)
            return carry
        return fn(rows, carry)

    return lax.fori_loop(0, nrows // chunk, step, 0 if init is None else init)


def _slot(px, py, pc):
    return 4 * px + 2 * py + pc


def _allgather_weights(shards):
    n = len(shards)

    def body(*refs):
        srcs, outs, (send_sems, recv_sems) = refs[:n], refs[n:2 * n], refs[2 * n:]
        x, y, c = _mesh_pos()
        me, sibling = (x, y, c), (x, y, 1 - c)
        chips = [(1 - x, y), (x, 1 - y), (1 - x, 1 - y)]

        def copy(a, k, block, to):
            blk = outs[a].at[_slot(*block)]
            return pltpu.make_async_remote_copy(
                src_ref=blk, dst_ref=blk, send_sem=send_sems.at[7 * a + k], recv_sem=recv_sems.at[7 * a + k],
                device_id=to, device_id_type=pl.DeviceIdType.MESH)

        sends = []
        for a in range(n):
            mine = outs[a].at[_slot(*me)]

            def cast(r, mine=mine, src=srcs[a]):
                mine[r, :] = src[r, :].astype(BF16)

            _row_chunks(srcs[a].shape[0], cast)
            first = [copy(a, 0, me, sibling)] + [copy(a, 1 + j, me, (*chip, c)) for j, chip in enumerate(chips)]
            for cp in first:
                cp.start()
            sends += first
        for a in range(n):
            for j, chip in enumerate(chips):
                copy(a, 1 + j, (*chip, c), me).wait_recv()
                fwd = copy(a, 4 + j, (*chip, c), sibling)
                fwd.start()
                sends.append(fwd)
        for a in range(n):
            copy(a, 0, sibling, me).wait_recv()
            for j, chip in enumerate(chips):
                copy(a, 4 + j, (*chip, 1 - c), me).wait_recv()
        for cp in sends:
            cp.wait_send()

    vm = pl.BlockSpec(memory_space=pltpu.VMEM)
    return pl.pallas_call(
        body, name="allgather_weights",
        out_shape=tuple(jax.ShapeDtypeStruct((N_DEV,) + s.shape, BF16) for s in shards),
        in_specs=[vm] * n, out_specs=(vm,) * n,
        scratch_shapes=[pltpu.SemaphoreType.DMA((7 * n,)), pltpu.SemaphoreType.DMA((7 * n,))],
        compiler_params=pltpu.CompilerParams(vmem_limit_bytes=VMEM_LIMIT),
    )(*shards)


def _adamw(w, g, m, v):
    m = ADAM_B1 * m + (1.0 - ADAM_B1) * g
    v = ADAM_B2 * v + (1.0 - ADAM_B2) * (g * g)
    m_hat = m / (1.0 - ADAM_B1 ** ADAM_STEP)
    v_hat = v / (1.0 - ADAM_B2 ** ADAM_STEP)
    delta = -ADAM_LR * (m_hat / (jnp.sqrt(v_hat) + ADAM_EPS) + ADAM_WD * w)
    return delta, m, v


def _remote(src, dst, send_sems, recv_sems, k, to):
    return pltpu.make_async_remote_copy(src_ref=src, dst_ref=dst, send_sem=send_sems.at[k], recv_sem=recv_sems.at[k],
                                        device_id=to, device_id_type=pl.DeviceIdType.MESH)


def _big_reduce_phases(g16_r, go_r, outs, send2, recv1, recv2, s_send, s_recv):
    n = len(g16_r)
    x, y, c = _mesh_pos()
    sibling = (x, y, 1 - c)
    chips = [(1 - x, y), (x, 1 - y), (1 - x, 1 - y)]
    all_chips = [(x, y)] + chips
    lvl1 = []
    for a in range(n):
        cps = [_remote(g16_r[a].at[_slot(*chip, 1 - c)], recv1[a].at[j], s_send, s_recv, 7 * a + j, sibling)
               for j, chip in enumerate(all_chips)]
        for cp in cps:
            cp.start()
        lvl1.append(cps)
    yield
    lvl2 = []
    for a in range(n):
        for cp in lvl1[a]:
            cp.wait_recv()
        og = outs[a]

        def partials(r, a=a, og=og):
            og[r, :] = go_r[a][r, :] + recv1[a][0, r, :].astype(F32)
            for j, chip in enumerate(chips):
                mine16 = g16_r[a][_slot(*chip, c), r, :].astype(F32)
                send2[a][j, r, :] = (mine16 + recv1[a][1 + j, r, :].astype(F32)).astype(BF16)

        _row_chunks(go_r[a].shape[0], partials)
        cps = [_remote(send2[a].at[j], recv2[a].at[j], s_send, s_recv, 7 * a + 4 + j, (*chip, c))
               for j, chip in enumerate(chips)]
        for cp in cps:
            cp.start()
        lvl2.append(cps)
    yield
    for a in range(n):
        for cp in lvl2[a]:
            cp.wait_recv()
        og = outs[a]

        def total(r, a=a, og=og):
            g = og[r, :]
            for j in range(3):
                g = g + recv2[a][j, r, :].astype(F32)
            og[r, :] = g

        _row_chunks(go_r[a].shape[0], total)
    yield
    for cps in lvl1 + lvl2:
        for cp in cps:
            cp.wait_send()


def _adamw_update(g, w, m, v):
    n = len(g)

    def body(*refs):
        g_r, w_r, m_r, v_r = (refs[i * n:(i + 1) * n] for i in range(4))
        outs = refs[4 * n:]
        for a in range(n):
            og, od, om, ov = outs[4 * a:4 * a + 4]

            def update(idx, a=a, og=og, od=od, om=om, ov=ov):
                gv = g_r[a][idx]
                d, nm, nv = _adamw(w_r[a][idx], gv, m_r[a][idx], v_r[a][idx])
                og[idx] = gv
                od[idx] = d
                om[idx] = nm
                ov[idx] = nv

            shape = g_r[a].shape
            if len(shape) == 3:
                for b in range(shape[0]):
                    update(b)
            elif _row_chunk(shape[0]) is not None:
                _row_chunks(shape[0], update)
            else:
                update(Ellipsis)

    vm = pl.BlockSpec(memory_space=pltpu.VMEM)
    res = pl.pallas_call(
        body, name="adamw_update",
        out_shape=tuple(jax.ShapeDtypeStruct(t.shape, F32) for t in g for _ in range(4)),
        in_specs=[vm] * (4 * n), out_specs=(vm,) * (4 * n),
        compiler_params=pltpu.CompilerParams(vmem_limit_bytes=VMEM_LIMIT),
    )(*g, *w, *m, *v)
    return [res[4 * a:4 * a + 4] for a in range(n)]


TINY = (("pre_norm_g", 1, 1024), ("post_norm_g", 1, 1024), ("pl_b_gate", 1, 1024), ("ssm_d", 1, 512),
        ("ssm_b_glu", 1, 512), ("ssm_log_step", 1, 32), ("attn_sinks", 1, 8), ("ssm_lam_re", 32, 64),
        ("ssm_lam_im", 32, 64))
MEDIUM = (("ssm_b_re", SSM_G * SSM_P, SSM_N), ("ssm_b_im", SSM_G * SSM_P, SSM_N), ("ssm_c_re", SSM_G * SSM_P, SSM_N),
          ("ssm_c_im", SSM_G * SSM_P, SSM_N))


def _stage_rows():
    offs, r = {}, 0
    for name, rows, cols in TINY + (("loss", 1, 1),):
        if rows > 1:
            r = -(-r // 8) * 8
        offs[name] = r
        r += rows if rows > 1 else max(cols // LANES, 1)
    return offs, -(-r // 8) * 8


def _reduce_final(g16, g32, loss, g_tiny, g_med):
    nb_, nt, nm_ = len(g16), len(TINY), len(MEDIUM)
    offs, stage_rows = _stage_rows()

    def body(*refs):
        g16_r, go_r = refs[:nb_], refs[nb_:2 * nb_]
        base = 2 * nb_
        loss_r, gt, gm = refs[base], refs[base + 1:base + 1 + nt], refs[base + 1 + nt:base + 1 + nt + nm_]
        base += 1 + nt + nm_
        out_b = refs[base:base + nb_]
        base += nb_
        loss_o, out_t, out_m = refs[base], refs[base + 1:base + 1 + nt], refs[base + 1 + nt:base + 1 + nt + nm_]
        base += 1 + nt + nm_
        send2_b, recv1_b, recv2_b = (refs[base + i * nb_:base + (i + 1) * nb_] for i in range(3))
        base += 3 * nb_
        stage = refs[base]
        recv1, part, recv2 = (refs[base + 1 + i * nm_:base + 1 + (i + 1) * nm_] for i in range(3))
        bs_send, bs_recv, s_send, s_recv, own_sems = refs[base + 1 + 3 * nm_:base + 6 + 3 * nm_]
        own32 = refs[base + 6 + 3 * nm_:]
        me = _slot(*_mesh_pos())
        fetch = [pltpu.make_async_copy(go_r[a].at[me], own32[a], own_sems.at[a]) for a in range(nb_)]
        for cp in fetch:
            cp.start()
        big = _big_reduce_phases(g16_r, own32, out_b, send2_b, recv1_b, recv2_b, bs_send, bs_recv)
        small = small_phases(loss_r, gt, gm, loss_o, out_t, out_m, stage, recv1, part, recv2, s_send, s_recv)
        next(big)
        next(small)
        for cp in fetch:
            cp.wait()
        next(big)
        for _ in small:
            pass
        for _ in big:
            pass

    def small_phases(loss_r, gt, gm, loss_o, out_t, out_m, stage, recv1, part, recv2, s_send, s_recv):
        x, y, c = _mesh_pos()
        me = _slot(x, y, c)
        sibling = (x, y, 1 - c)
        chips = [(1 - x, y), (x, 1 - y), (1 - x, 1 - y)]
        all_chips = [(x, y)] + chips
        peers = [sibling] + [(*chip, c) for chip in chips] + [(*chip, 1 - c) for chip in chips]
        sem = iter(range(7 + 14 * nm_))
        lvl1 = []
        for a in range(nm_):
            cps = [_remote(gm[a].at[_slot(*chip, 1 - c)], recv1[a].at[j], s_send, s_recv, next(sem), sibling)
                   for j, chip in enumerate(all_chips)]
            for cp in cps:
                cp.start()
            lvl1.append(cps)
        mine = stage.at[me]
        mine[...] = jnp.zeros((stage_rows, LANES), F32)
        for (name, rows, cols), ref in zip(TINY + (("loss", 1, 1),), gt + (loss_r,)):
            r0 = offs[name]
            if rows > 1:
                mine[r0:r0 + rows, 0:cols] = ref[...]
            elif cols >= LANES:
                for i in range(cols // LANES):
                    mine[r0 + i:r0 + i + 1, :] = ref[:, i * LANES:(i + 1) * LANES]
            else:
                mine[r0:r0 + 1, 0:cols] = ref[...]
        tiny_cps = [_remote(mine, mine, s_send, s_recv, next(sem), peer) for peer in peers]
        for cp in tiny_cps:
            cp.start()
        yield
        lvl2 = []
        for a in range(nm_):
            for cp in lvl1[a]:
                cp.wait_recv()
            for j, chip in enumerate(all_chips):
                part[a][j] = gm[a][_slot(*chip, c)] + recv1[a][j]
            cps = [_remote(part[a].at[1 + j], recv2[a].at[j], s_send, s_recv, next(sem), (*chip, c))
                   for j, chip in enumerate(chips)]
            for cp in cps:
                cp.start()
            lvl2.append(cps)
        yield
        lvl3 = []
        for a in range(nm_):
            for cp in lvl2[a]:
                cp.wait_recv()
            blk = out_m[a].at[me]
            blk[...] = ((part[a][0] + recv2[a][0]) + recv2[a][1]) + recv2[a][2]
            cps = [_remote(blk, blk, s_send, s_recv, next(sem), peer) for peer in peers]
            for cp in cps:
                cp.start()
            lvl3.append(cps)
        yield
        for cp in tiny_cps:
            cp.wait_recv()
        tot = stage[0]
        for d in range(1, N_DEV):
            tot = tot + stage[d]
        loss_o[...] = tot[offs["loss"]:offs["loss"] + 1, 0:1]
        for k, (name, rows, cols) in enumerate(TINY):
            r0 = offs[name]
            if rows > 1:
                out_t[k][...] = tot[r0:r0 + rows, 0:cols]
            elif cols >= LANES:
                for i in range(cols // LANES):
                    out_t[k][:, i * LANES:(i + 1) * LANES] = tot[r0 + i:r0 + i + 1, :]
            else:
                out_t[k][...] = tot[r0:r0 + 1, 0:cols]
        for cps in lvl3:
            for cp in cps:
                cp.wait_recv()
        for cps in lvl1 + lvl2 + lvl3 + [tiny_cps]:
            for cp in cps:
                cp.wait_send()

    vmem = pl.BlockSpec(memory_space=pltpu.VMEM)
    t_shapes = [jax.ShapeDtypeStruct((rows, cols), F32) for _, rows, cols in TINY]
    m_shapes = [jax.ShapeDtypeStruct((N_DEV, rows // N_DEV, cols), F32) for _, rows, cols in MEDIUM]
    blk = [(rows // N_DEV, cols) for _, rows, cols in MEDIUM]
    shard = [g.shape[1:] for g in g16]
    scratch = ([pltpu.VMEM((3,) + s, BF16) for s in shard] + [pltpu.VMEM((4,) + s, BF16) for s in shard]
               + [pltpu.VMEM((3,) + s, BF16) for s in shard]
               + [pltpu.VMEM((N_DEV, stage_rows, LANES), F32)]
               + [pltpu.VMEM((4,) + b, F32) for b in blk] + [pltpu.VMEM((4,) + b, F32) for b in blk]
               + [pltpu.VMEM((3,) + b, F32) for b in blk]
               + [pltpu.SemaphoreType.DMA((7 * nb_,)), pltpu.SemaphoreType.DMA((7 * nb_,)),
                  pltpu.SemaphoreType.DMA((7 + 14 * nm_,)), pltpu.SemaphoreType.DMA((7 + 14 * nm_,)),
                  pltpu.SemaphoreType.DMA((nb_,))]
               + [pltpu.VMEM(s, F32) for s in shard])
    n_out = nb_ + 1 + nt + nm_
    res = pl.pallas_call(
        body, name="reduce_final",
        out_shape=tuple(jax.ShapeDtypeStruct(s, F32) for s in shard) + (jax.ShapeDtypeStruct((1, 1), F32),)
        + tuple(t_shapes) + tuple(m_shapes),
        in_specs=[vmem] * nb_ + [pl.BlockSpec(memory_space=pl.ANY)] * nb_ + [vmem] * (1 + nt + nm_),
        out_specs=(vmem,) * n_out, scratch_shapes=scratch,
        compiler_params=pltpu.CompilerParams(vmem_limit_bytes=VMEM_LIMIT),
    )(*g16, *g32, loss, *g_tiny, *g_med)
    return list(res[:nb_]), res[nb_], list(res[nb_ + 1:nb_ + 1 + nt]), list(res[nb_ + 1 + nt:])


def _gather_phases(shard_r, gath, cast, send_sems, recv_sems, local_sems):
    n = len(shard_r)
    x, y, c = _mesh_pos()
    me, sibling = (x, y, c), (x, y, 1 - c)
    chips = [(1 - x, y), (x, 1 - y), (1 - x, 1 - y)]

    def own(a, k, to):
        return _remote(cast[a], gath[a].at[_slot(*me)], send_sems, recv_sems, 7 * a + k, to)

    def passed(a, k, block, to):
        blk = gath[a].at[_slot(*block)]
        return _remote(blk, blk, send_sems, recv_sems, 7 * a + k, to)

    def keep(a):
        return pltpu.make_async_copy(cast[a], gath[a].at[_slot(*me)], local_sems.at[a])

    def start():
        for a in range(n):
            def to16(r, a=a):
                cast[a][r, :] = shard_r[a][r, :].astype(BF16)

            _row_chunks(shard_r[a].shape[0], to16)
            keep(a).start()
            own(a, 0, sibling).start()
            for j, chip in enumerate(chips):
                own(a, 1 + j, (*chip, c)).start()

    def relay():
        for a in range(n):
            for j, chip in enumerate(chips):
                passed(a, 1 + j, (*chip, c), me).wait_recv()
                passed(a, 4 + j, (*chip, c), sibling).start()

    def finish():
        for a in range(n):
            passed(a, 0, sibling, me).wait_recv()
            for j, chip in enumerate(chips):
                passed(a, 4 + j, (*chip, 1 - c), me).wait_recv()
            own(a, 0, sibling).wait_send()
            for j, chip in enumerate(chips):
                own(a, 1 + j, (*chip, c)).wait_send()
                passed(a, 4 + j, (*chip, c), sibling).wait_send()
            keep(a).wait()

    return start, relay, finish


def _gather_operands(shards):
    n = len(shards)
    return ((pl.BlockSpec(memory_space=pl.ANY),) * n,
            tuple(jax.ShapeDtypeStruct((N_DEV,) + s.shape, BF16) for s in shards),
            [pltpu.VMEM(s.shape, BF16) for s in shards]
            + [pltpu.SemaphoreType.DMA((7 * n,)), pltpu.SemaphoreType.DMA((7 * n,)), pltpu.SemaphoreType.DMA((n,))])


def _hosted_reduce_phases(g16_r, g32_r, red, own16, recv1, send2, recv2, own32, s_send, s_recv, s_local):
    n = len(g16_r)
    x, y, c = _mesh_pos()
    sibling = (x, y, 1 - c)
    chips = [(1 - x, y), (x, 1 - y), (1 - x, 1 - y)]
    all_chips = [(x, y)] + chips

    def lvl1(a, j):
        return _remote(g16_r[a].at[_slot(*all_chips[j], 1 - c)], recv1[a].at[j], s_send, s_recv, 7 * a + j, sibling)

    def lvl2(a, j):
        return _remote(send2[a].at[j], recv2[a].at[j], s_send, s_recv, 7 * a + 4 + j, (*chips[j], c))

    def mine(a, j):
        if j == 3:
            return pltpu.make_async_copy(g32_r[a].at[_slot(x, y, c)], own32[a], s_local.at[4 * a + j])
        return pltpu.make_async_copy(g16_r[a].at[_slot(*chips[j], c)], own16[a].at[j], s_local.at[4 * a + j])

    def start():
        for a in range(n):
            for j in range(4):
                mine(a, j).start()
            for j in range(4):
                lvl1(a, j).start()

    def middle():
        for a in range(n):
            for j in range(4):
                mine(a, j).wait()
            for j in range(4):
                lvl1(a, j).wait_recv()

            def partials(r, a=a):
                red[a][r, :] = own32[a][r, :] + recv1[a][0, r, :].astype(F32)
                for j in range(3):
                    send2[a][j, r, :] = (own16[a][j, r, :].astype(F32) + recv1[a][1 + j, r, :].astype(F32)).astype(BF16)

            _row_chunks(own32[a].shape[0], partials)
            for j in range(3):
                lvl2(a, j).start()

    def total():
        for a in range(n):
            for j in range(3):
                lvl2(a, j).wait_recv()

            def add(r, a=a):
                g = red[a][r, :]
                for j in range(3):
                    g = g + recv2[a][j, r, :].astype(F32)
                red[a][r, :] = g

            _row_chunks(own32[a].shape[0], add)

    def finish():
        for a in range(n):
            for j in range(4):
                lvl1(a, j).wait_send()
            for j in range(3):
                lvl2(a, j).wait_send()

    return start, middle, total, finish


def _hosted_reduce_operands(g16, const_spec):
    n = len(g16)
    shard = [g.shape[1:] for g in g16]
    return ([pl.BlockSpec(memory_space=pl.ANY)] * (2 * n),
            tuple(const_spec(s) for s in shard),
            tuple(jax.ShapeDtypeStruct(s, F32) for s in shard),
            [pltpu.VMEM((3,) + s, BF16) for s in shard] + [pltpu.VMEM((4,) + s, BF16) for s in shard]
            + [pltpu.VMEM((3,) + s, BF16) for s in shard] + [pltpu.VMEM((3,) + s, BF16) for s in shard]
            + [pltpu.VMEM(s, F32) for s in shard]
            + [pltpu.SemaphoreType.DMA((7 * n,)), pltpu.SemaphoreType.DMA((7 * n,)), pltpu.SemaphoreType.DMA((4 * n,))])


def _in_proj(x2, g_pre, w_in, tm):
    t = x2.shape[0]

    def body(x_ref, g_ref, w_ref, u_ref, zs_ref, q_ref, k_ref, v_ref, za_ref):
        xv = x_ref[...]
        r = lax.rsqrt(jnp.mean(xv * xv, axis=-1, keepdims=True) + EPS)
        hn = xv * r * g_ref[...]
        proj = _mm_nt(hn, w_ref[...])
        u_ref[...] = proj[:, 0:512]
        zs_ref[...] = proj[:, 512:1024]
        q_ref[...] = proj[:, 1024:1536].astype(BF16)
        k_ref[...] = proj[:, 1536:1664].astype(BF16)
        v_ref[...] = proj[:, 1664:1792].astype(BF16)
        za_ref[...] = proj[:, 1792:2304]

    row = lambda w: pl.BlockSpec((tm, w), lambda i: (i, 0))
    return pl.pallas_call(
        body, name="in_proj", grid=(t // tm,),
        in_specs=[row(D_MODEL), _const_spec((1, D_MODEL)), _const_spec((D_IN, D_MODEL))],
        out_specs=(row(512), row(512), row(512), row(128), row(128), row(512)),
        out_shape=(jax.ShapeDtypeStruct((t, 512), F32),
                   jax.ShapeDtypeStruct((t, 512), F32), jax.ShapeDtypeStruct((t, 512), BF16),
                   jax.ShapeDtypeStruct((t, 128), BF16), jax.ShapeDtypeStruct((t, 128), BF16),
                   jax.ShapeDtypeStruct((t, 512), F32)),
        compiler_params=_tc_params(("arbitrary",)),
    )(x2, g_pre, w_in)


def _discretise(lr, li, ls):
    step = jnp.exp(ls)
    mag = jnp.exp(lr * step)
    ar = mag * jnp.cos(li * step)
    ai = mag * jnp.sin(li * step)
    den = lr * lr + li * li
    cr = ((ar - 1.0) * lr + ai * li) / den
    ci = (ai * lr - (ar - 1.0) * li) / den
    return step, ar, ai, den, cr, ci


def _per_channel(v):
    return jnp.broadcast_to(v[:, None, :], (SSM_G, SSM_P, SSM_N)).reshape(SSM_G * SSM_P, SSM_N)


def _tile_masks():
    r = lax.broadcasted_iota(jnp.int32, (CH_T, ST_T), 0) // SSM_P
    l = lax.broadcasted_iota(jnp.int32, (CH_T, ST_T), 1) // SSM_N
    lt = lax.broadcasted_iota(jnp.int32, (ST_T, CH_T), 0) // SSM_N
    rt = lax.broadcasted_iota(jnp.int32, (ST_T, CH_T), 1) // SSM_P
    rep = lax.broadcasted_iota(jnp.int32, (SSM_N, ST_T), 0) == lax.broadcasted_iota(jnp.int32, (SSM_N, ST_T), 1) % SSM_N
    rep_t = lax.broadcasted_iota(jnp.int32, (ST_T, SSM_N), 0) % SSM_N == lax.broadcasted_iota(jnp.int32, (ST_T, SSM_N), 1)
    return r == l, lt == rt, rep, rep_t


def _ssm_prep(lam_re, lam_im, log_step, b_re, b_im, c_re, c_im, seg):
    def body(lr_ref, li_ref, ls_ref, br_ref, bi_ref, cre_ref, cim_ref, lrr_ref, lir_ref, lsr_ref,
             ar_ref, ai_ref, pr_ref, pi_ref, bcat_ref, bcat_t_ref, ccat_ref, ccat_t_ref):
        _, _, _, _, cr, ci = _discretise(lr_ref[...], li_ref[...], ls_ref[...])
        cr, ci = _per_channel(cr), _per_channel(ci)
        br, bi = br_ref[...], bi_ref[...]
        bb_re = cr * br - ci * bi
        bb_im = cr * bi + ci * br
        same, same_t, rep, rep_t = _tile_masks()
        rep, rep_t = rep.astype(BF16), rep_t.astype(BF16)
        for j in range(N_GT):
            rows = slice(j * CH_T, (j + 1) * CH_T)
            for wide, tall, parts in ((bcat_ref, bcat_t_ref, (bb_re[rows], bb_im[rows])),
                                      (ccat_t_ref, ccat_ref, (cre_ref[rows, :], -cim_ref[rows, :]))):
                for k, part in enumerate(parts):
                    p16 = part.astype(BF16)
                    wide[j, :, k * ST_T:(k + 1) * ST_T] = jnp.where(same, _mm(p16, rep), 0.0).astype(BF16)
                    tall[j, k * ST_T:(k + 1) * ST_T, :] = jnp.where(same_t, _mm_nt(rep_t, p16), 0.0).astype(BF16)
        stepr = jnp.exp(lsr_ref[...])
        k = (lax.broadcasted_iota(jnp.int32, (8, N_STATE), 0) + 1).astype(F32)
        magk = jnp.exp(k * (lrr_ref[...] * stepr))
        ang = k * (lir_ref[...] * stepr)
        pr_ref[0:8, :] = magk * jnp.cos(ang)
        pi_ref[0:8, :] = magk * jnp.sin(ang)
        n = 8
        while n < seg:
            tr, ti = pr_ref[n - 1:n, :], pi_ref[n - 1:n, :]
            xr, xi = pr_ref[0:n, :], pi_ref[0:n, :]
            pr_ref[n:2 * n, :] = xr * tr - xi * ti
            pi_ref[n:2 * n, :] = xr * ti + xi * tr
            n *= 2
        ar_ref[...] = pr_ref[0:1, :]
        ai_ref[...] = pi_ref[0:1, :]

    row = jax.ShapeDtypeStruct((1, N_STATE), F32)
    pw = jax.ShapeDtypeStruct((seg, N_STATE), F32)
    wide = jax.ShapeDtypeStruct((N_GT, CH_T, 2 * ST_T), BF16)
    tall = jax.ShapeDtypeStruct((N_GT, 2 * ST_T, CH_T), BF16)
    vm = pl.BlockSpec(memory_space=pltpu.VMEM)
    step_row = jnp.broadcast_to(log_step, (SSM_G, SSM_N)).reshape(1, N_STATE)
    return pl.pallas_call(
        body, name="ssm_prep", out_shape=(row, row, pw, pw, wide, tall, tall, wide),
        in_specs=[vm] * 10, out_specs=(vm,) * 8,
    )(lam_re, lam_im, log_step, b_re, b_im, c_re, c_im, lam_re.reshape(1, N_STATE), lam_im.reshape(1, N_STATE),
      step_row)


def _seg_rows(t):
    if isinstance(t, int):
        return pl.ds(t * N_SEG, N_SEG)
    return pl.ds(pl.multiple_of(t * N_SEG, N_SEG), N_SEG)


def _scan_forward(xs, a_re, a_im, pw_re, pw_im, cs, seg):
    are = jnp.broadcast_to(a_re, (N_SEG, ST_T))
    aim = jnp.broadcast_to(a_im, (N_SEG, ST_T))

    def steps(k, carry):
        xr, xi = carry
        for j in range(SCAN_UNROLL):
            r = pl.multiple_of((k * SCAN_UNROLL + j) * N_SEG, N_SEG)
            nr = are * xr - aim * xi + xs[pl.ds(r, N_SEG), 0:ST_T]
            ni = are * xi + aim * xr + xs[pl.ds(r, N_SEG), ST_T:2 * ST_T]
            xs[pl.ds(r, N_SEG), 0:ST_T] = nr
            xs[pl.ds(r, N_SEG), ST_T:2 * ST_T] = ni
            xr, xi = nr, ni
        return xr, xi

    zero = jnp.zeros((N_SEG, ST_T), F32)
    fr, fi = lax.fori_loop(0, seg // SCAN_UNROLL, steps, (zero, zero))
    sr, si = pw_re[seg - 1:seg, :], pw_im[seg - 1:seg, :]
    cr = jnp.zeros((1, ST_T), F32)
    ci = jnp.zeros((1, ST_T), F32)
    cs[0:1, :] = cr
    cs[8:9, :] = ci
    for s in range(1, N_SEG):
        ncr = sr * cr - si * ci + fr[s - 1:s, :]
        nci = sr * ci + si * cr + fi[s - 1:s, :]
        cr, ci = ncr, nci
        cs[s:s + 1, :] = cr
        cs[8 + s:9 + s, :] = ci
    car, cai = cs[0:8, :], cs[8:16, :]

    def fix(t, _):
        r = pl.multiple_of(t * N_SEG, N_SEG)
        pr, pi = pw_re[pl.ds(t, 1), :], pw_im[pl.ds(t, 1), :]
        xs[pl.ds(r, N_SEG), 0:ST_T] = xs[pl.ds(r, N_SEG), 0:ST_T] + (pr * car - pi * cai)
        xs[pl.ds(r, N_SEG), ST_T:2 * ST_T] = xs[pl.ds(r, N_SEG), ST_T:2 * ST_T] + (pr * cai + pi * car)
        return 0

    lax.fori_loop(0, seg, fix, 0, unroll=SCAN_UNROLL)


def _interleave(src, dst, seg):
    for s in range(N_SEG):
        dst[pl.ds(s, seg, stride=N_SEG), :] = src[s]


def _deinterleave(src, seg, s):
    return src[pl.ds(s, seg, stride=N_SEG), :]


def _ssm_forward(u, bcat, ccat, a_re, a_im, pw_re, pw_im, d_row, late, seg):
    bl = u.shape[0]
    rows = N_SEG * seg
    n = len(late)
    steps = bl * N_GT

    def body(*refs):
        u_ref, b_ref, c_ref, ar_ref, ai_ref, pr_ref, pi_ref, d_ref = refs[:8]
        late_r = refs[8:8 + n]
        y_ref, xs_ref, cs_ref = refs[8 + n:11 + n]
        gath, cast = refs[11 + n:11 + 2 * n], refs[11 + 2 * n:11 + 3 * n]
        send_sems, recv_sems, local_sems, ui, yi = refs[11 + 3 * n:]
        step = pl.program_id(0) * N_GT + pl.program_id(1)
        start, relay, finish = _gather_phases(late_r, gath, cast, send_sems, recv_sems, local_sems)
        pl.when(step == 0)(start)
        _interleave(u_ref.at[0], ui, seg)
        u = ui[...]
        xs, cs = xs_ref.at[0, 0], cs_ref.at[0, 0]
        xs[...] = _mm(u, b_ref[0])
        _scan_forward(xs, ar_ref[...], ai_ref[...], pr_ref, pi_ref, cs, seg)
        yi[...] = _mm(xs[...], c_ref[0]) + d_ref[...] * u
        for s in range(N_SEG):
            y_ref[0, s] = _deinterleave(yi, seg, s)
        pl.when(step == steps // 2)(relay)
        pl.when(step == steps - 1)(finish)

    state = lambda r, c: pl.BlockSpec((1, 1, r, c), lambda b, j: (b, j, 0, 0))
    act = pl.BlockSpec((1, N_SEG, seg, CH_T), lambda b, j: (b, 0, 0, j))
    g_specs, g_shapes, g_scratch = _gather_operands(late)
    res = pl.pallas_call(
        body, name="ssm_forward", grid=(bl, N_GT),
        in_specs=[act,
                  pl.BlockSpec((1, CH_T, 2 * ST_T), lambda b, j: (j, 0, 0)),
                  pl.BlockSpec((1, 2 * ST_T, CH_T), lambda b, j: (j, 0, 0)),
                  pl.BlockSpec((1, ST_T), lambda b, j: (0, j)), pl.BlockSpec((1, ST_T), lambda b, j: (0, j)),
                  pl.BlockSpec((seg, ST_T), lambda b, j: (0, j)), pl.BlockSpec((seg, ST_T), lambda b, j: (0, j)),
                  pl.BlockSpec((1, CH_T), lambda b, j: (0, j))]
        + [pl.BlockSpec(s.shape, lambda b, j: (0, 0)) for s in late],
        out_specs=(act, state(rows, 2 * ST_T), state(16, ST_T)) + g_specs,
        out_shape=(jax.ShapeDtypeStruct((bl, N_SEG, seg, D_SSM), F32),
                   jax.ShapeDtypeStruct((bl, N_GT, rows, 2 * ST_T), F32),
                   jax.ShapeDtypeStruct((bl, N_GT, 16, ST_T), F32)) + g_shapes,
        scratch_shapes=g_scratch + [pltpu.VMEM((rows, CH_T), F32), pltpu.VMEM((rows, CH_T), F32)],
        compiler_params=_tc_params(("arbitrary", "arbitrary")),
    )(u, bcat, ccat, a_re, a_im, pw_re, pw_im, d_row, *late)
    return res[:3], list(res[3:])


def _ssm_backward(u, dy, states, carries, bcat_t, ccat_t, a_re, a_im, pw_re, pw_im, d_row, late16, late32, seg):
    bl = u.shape[0]
    rows = N_SEG * seg
    n = len(late16)
    grid_steps = N_GT * bl

    def body(*refs):
        u_ref, dy_ref, xs_ref, cs_ref, bt_ref, ct_ref, ar_ref, ai_ref, pr_ref, pi_ref, d_ref = refs[:11]
        g16_r, g32_r = refs[11:11 + n], refs[11 + n:11 + 2 * n]
        du_ref, db_ref, dc_ref, dar_ref, dai_ref, dd_ref = refs[11 + 2 * n:17 + 2 * n]
        red = refs[17 + 2 * n:17 + 3 * n]
        own16, recv1, send2, recv2, own32 = (refs[17 + 3 * n + k * n:17 + 3 * n + (k + 1) * n] for k in range(5))
        s_send, s_recv, s_local, ls, cl, ui, dyi, dui = refs[17 + 8 * n:]
        b = pl.program_id(1)
        step = pl.program_id(0) * bl + b
        start, middle, total, finish = _hosted_reduce_phases(g16_r, g32_r, red, own16, recv1, send2, recv2, own32,
                                                             s_send, s_recv, s_local)
        pl.when(step == 0)(start)
        pl.when(step == grid_steps // 4)(middle)
        pl.when(step == (grid_steps * 3) // 4)(total)
        pl.when(step == grid_steps - 1)(finish)
        _interleave(u_ref.at[0], ui, seg)
        _interleave(dy_ref.at[0], dyi, seg)
        u = ui[...]
        dy = dyi[...]
        xs, cs = xs_ref.at[0, 0], cs_ref.at[0, 0]
        ls[...] = _mm(dy, ct_ref[0])
        are = jnp.broadcast_to(ar_ref[...], (N_SEG, ST_T))
        aim = jnp.broadcast_to(ai_ref[...], (N_SEG, ST_T))

        def steps(k, carry):
            lr, li = carry
            for j in range(SCAN_UNROLL):
                r = pl.multiple_of((seg - 1 - (k * SCAN_UNROLL + j)) * N_SEG, N_SEG)
                nr = are * lr + aim * li + ls[pl.ds(r, N_SEG), 0:ST_T]
                ni = are * li - aim * lr + ls[pl.ds(r, N_SEG), ST_T:2 * ST_T]
                ls[pl.ds(r, N_SEG), 0:ST_T] = nr
                ls[pl.ds(r, N_SEG), ST_T:2 * ST_T] = ni
                lr, li = nr, ni
            return lr, li

        zero = jnp.zeros((N_SEG, ST_T), F32)
        fr, fi = lax.fori_loop(0, seg // SCAN_UNROLL, steps, (zero, zero))
        sr, si = pr_ref[seg - 1:seg, :], pi_ref[seg - 1:seg, :]
        cr = jnp.zeros((1, ST_T), F32)
        ci = jnp.zeros((1, ST_T), F32)
        cl[7:8, :] = cr
        cl[15:16, :] = ci
        for s in range(N_SEG - 2, -1, -1):
            ncr = sr * cr + si * ci + fr[s + 1:s + 2, :]
            nci = sr * ci - si * cr + fi[s + 1:s + 2, :]
            cr, ci = ncr, nci
            cl[s:s + 1, :] = cr
            cl[8 + s:9 + s, :] = ci
        clr, cli = cl[0:8, :], cl[8:16, :]

        def fix_rows(rows, t, xpr, xpi, acc):
            dr, di = acc
            pr, pi = pr_ref[pl.ds(seg - 1 - t, 1), :], pi_ref[pl.ds(seg - 1 - t, 1), :]
            lr = ls[rows, 0:ST_T] + (pr * clr + pi * cli)
            li = ls[rows, ST_T:2 * ST_T] + (pr * cli - pi * clr)
            ls[rows, 0:ST_T] = lr
            ls[rows, ST_T:2 * ST_T] = li
            return dr + (lr * xpr + li * xpi), di + (li * xpr - lr * xpi)

        def fix_at(t, acc):
            prev = _seg_rows(t - 1)
            return fix_rows(_seg_rows(t), t, xs[prev, 0:ST_T], xs[prev, ST_T:2 * ST_T], acc)

        def fix(k, acc):
            for j in range(SCAN_UNROLL):
                acc = fix_at(k * SCAN_UNROLL + j, acc)
            return acc

        acc = fix_rows(pl.ds(0, N_SEG), 0, cs[0:8, :], cs[8:16, :], (zero, zero))
        for t in range(1, SCAN_UNROLL):
            acc = fix_at(t, acc)
        dr, di = lax.fori_loop(1, seg // SCAN_UNROLL, fix, acc)
        dar = jnp.sum(dr, axis=0, keepdims=True)
        dai = jnp.sum(di, axis=0, keepdims=True)
        lall = ls[...]
        dui[...] = _mm(lall, bt_ref[0]) + d_ref[...] * dy
        for s in range(N_SEG):
            du_ref[0, s] = _deinterleave(dui, seg, s).astype(BF16)
        dbp = _mm_tn(u, lall)
        dcp = _mm_tn(dy, xs[...])
        ddp = jnp.sum(dy * u, axis=0, keepdims=True)

        @pl.when(b == 0)
        def _():
            db_ref[0] = dbp
            dc_ref[0] = dcp
            dar_ref[...] = dar
            dai_ref[...] = dai
            dd_ref[...] = ddp

        @pl.when(b != 0)
        def _():
            db_ref[0] += dbp
            dc_ref[0] += dcp
            dar_ref[...] += dar
            dai_ref[...] += dai
            dd_ref[...] += ddp

    tile3 = lambda r, c: pl.BlockSpec((1, r, c), lambda j, b: (j, 0, 0))
    lane = lambda r, c: pl.BlockSpec((r, c), lambda j, b: (0, j))
    act = pl.BlockSpec((1, N_SEG, seg, CH_T), lambda j, b: (b, 0, 0, j))
    state = lambda r, c: pl.BlockSpec((1, 1, r, c), lambda j, b: (b, j, 0, 0))
    r_in, r_out, r_shapes, r_scratch = _hosted_reduce_operands(late16, lambda s: pl.BlockSpec(s, lambda j, b: (0, 0)))
    res = pl.pallas_call(
        body, name="ssm_backward", grid=(N_GT, bl),
        in_specs=[act, act, state(rows, 2 * ST_T), state(16, ST_T), tile3(2 * ST_T, CH_T), tile3(CH_T, 2 * ST_T),
                  lane(1, ST_T), lane(1, ST_T), lane(seg, ST_T), lane(seg, ST_T), lane(1, CH_T)] + r_in,
        out_specs=(act, tile3(CH_T, 2 * ST_T), tile3(CH_T, 2 * ST_T), lane(1, ST_T), lane(1, ST_T), lane(1, CH_T))
        + r_out,
        out_shape=(jax.ShapeDtypeStruct((bl, N_SEG, seg, D_SSM), BF16),
                   jax.ShapeDtypeStruct((N_GT, CH_T, 2 * ST_T), F32), jax.ShapeDtypeStruct((N_GT, CH_T, 2 * ST_T), F32),
                   jax.ShapeDtypeStruct((1, N_STATE), F32), jax.ShapeDtypeStruct((1, N_STATE), F32),
                   jax.ShapeDtypeStruct((1, D_SSM), F32)) + r_shapes,
        scratch_shapes=r_scratch + [pltpu.VMEM((rows, 2 * ST_T), F32), pltpu.VMEM((16, ST_T), F32)]
        + [pltpu.VMEM((rows, CH_T), F32)] * 3,
        compiler_params=_tc_params(("arbitrary", "arbitrary")),
    )(u, dy, states, carries, bcat_t, ccat_t, a_re, a_im, pw_re, pw_im, d_row, *late16, *late32)
    return res[:6], list(res[6:])


def _ssm_param_grads(lam_re, lam_im, log_step, b_re, b_im, da_re, da_im, d_bcat, d_ccat_t):
    def body(lr_ref, li_ref, ls_ref, br_ref, bi_ref, gar_ref, gai_ref, gbcat_ref, gccat_ref,
             dlr_ref, dli_ref, dls_ref, dbr_ref, dbi_ref, dcr_ref, dci_ref, gbr_s, gbi_s):
        same, _, _, rep_t = _tile_masks()
        rep_t = rep_t.astype(F32)
        for j in range(N_GT):
            rows = slice(j * CH_T, (j + 1) * CH_T)
            for src, dsts in ((gbcat_ref, (gbr_s, gbi_s)), (gccat_ref, (dcr_ref, dci_ref))):
                for k, dst in enumerate(dsts):
                    blk = jnp.where(same, src[j, :, k * ST_T:(k + 1) * ST_T], 0.0)
                    dst[rows, :] = jnp.dot(blk, rep_t, precision=lax.Precision.HIGHEST, preferred_element_type=F32)
        dci_ref[...] = -dci_ref[...]
        lr, li = lr_ref[...], li_ref[...]
        step, ar, ai, den, cr, ci = _discretise(lr, li, ls_ref[...])
        crb, cib = _per_channel(cr), _per_channel(ci)
        br, bi = br_ref[...], bi_ref[...]
        gbr, gbi = gbr_s[...], gbi_s[...]
        dbr_ref[...] = crb * gbr + cib * gbi
        dbi_ref[...] = crb * gbi - cib * gbr
        over_channels = lambda t: jnp.sum(t.reshape(SSM_G, SSM_P, SSM_N), axis=1)
        gcr = over_channels(br * gbr + bi * gbi)
        gci = over_channels(br * gbi - bi * gbr)
        ilr, ili = lr / den, -li / den
        gar = gar_ref[...] + (ilr * gcr + ili * gci)
        gai = gai_ref[...] + (ilr * gci - ili * gcr)
        qr, qi = cr * ilr - ci * ili, cr * ili + ci * ilr
        glr = -(qr * gcr + qi * gci)
        gli = -(qr * gci - qi * gcr)
        gwr = ar * gar + ai * gai
        gwi = ar * gai - ai * gar
        dlr_ref[...] = glr + step * gwr
        dli_ref[...] = gli + step * gwi
        dls_ref[...] = jnp.sum(lr * gwr + li * gwi, axis=-1, keepdims=True) * step

    lam = jax.ShapeDtypeStruct((SSM_G, SSM_N), F32)
    mat = jax.ShapeDtypeStruct((SSM_G * SSM_P, SSM_N), F32)
    vm = pl.BlockSpec(memory_space=pltpu.VMEM)
    return pl.pallas_call(
        body, name="ssm_param_grads", out_shape=(lam, lam, jax.ShapeDtypeStruct((SSM_G, 1), F32), mat, mat, mat, mat),
        in_specs=[vm] * 9, out_specs=(vm,) * 7,
        scratch_shapes=[pltpu.VMEM((SSM_G * SSM_P, SSM_N), F32), pltpu.VMEM((SSM_G * SSM_P, SSM_N), F32)],
    )(lam_re, lam_im, log_step, b_re, b_im, da_re, da_im, d_bcat, d_ccat_t)


ROWS4 = Q_PER_KV * ATT_BLOCK


def _att_dist_mask(first_block):
    qi = lax.broadcasted_iota(jnp.int32, (ROWS4, 2 * ATT_BLOCK), 0) & (ATT_BLOCK - 1)
    si = lax.broadcasted_iota(jnp.int32, (ROWS4, 2 * ATT_BLOCK), 1)
    dist = qi + ATT_BLOCK - si
    valid = (dist >= 0) & (dist < ATT_BLOCK) & ((si >= ATT_BLOCK) | jnp.logical_not(first_block))
    return dist.astype(F32), valid


def _stack_heads(x, kv):
    return jnp.concatenate([x[:, (kv * Q_PER_KV + g) * HEAD_DIM:(kv * Q_PER_KV + g + 1) * HEAD_DIM]
                            for g in range(Q_PER_KV)], axis=0)


def _stack_cols(x, kv):
    return jnp.concatenate([x[:, kv * Q_PER_KV + g:kv * Q_PER_KV + g + 1] for g in range(Q_PER_KV)], axis=0)


def _per_head_col(vals):
    return jnp.concatenate([jnp.full((ATT_BLOCK, 1), v, F32) for v in vals], axis=0)


def _attn_forward(q, k, v, sinks, bl, nb):
    t = q.shape[0]

    def body(sink_ref, q_ref, kp_ref, kc_ref, vp_ref, vc_ref, o_ref, lse_ref):
        i = pl.program_id(1)
        dist4, valid4 = _att_dist_mask(i == 0)
        dist, valid = dist4[0:ATT_BLOCK, :], valid4[0:ATT_BLOCK, :]
        kk = jnp.concatenate([kp_ref[...], kc_ref[...]], axis=0)
        vv = jnp.concatenate([vp_ref[...], vc_ref[...]], axis=0)
        qv = q_ref[...]
        for h in range(N_HEADS):
            kv = h // Q_PER_KV
            slope = 2.0 ** (-(h + 1))
            qh = qv[:, h * HEAD_DIM:(h + 1) * HEAD_DIM]
            kh = kk[:, kv * HEAD_DIM:(kv + 1) * HEAD_DIM]
            vh = vv[:, kv * HEAD_DIM:(kv + 1) * HEAD_DIM]
            s = _mm_nt(qh, kh) * ATT_SCALE - slope * dist
            s = jnp.where(valid, s, NEG_BIG)
            sink = sink_ref[h]
            m = jnp.maximum(jnp.max(s, axis=-1, keepdims=True), sink)
            e = jnp.exp(s - m)
            den = jnp.sum(e, axis=-1, keepdims=True) + jnp.exp(sink - m)
            o_ref[:, h * HEAD_DIM:(h + 1) * HEAD_DIM] = _mm(e, vh) * (1.0 / den)
            lse_ref[:, h:h + 1] = m + jnp.log(den)

    cur = lambda w: pl.BlockSpec((ATT_BLOCK, w), lambda b, i: (b * nb + i, 0))
    prev = lambda w: pl.BlockSpec((ATT_BLOCK, w), lambda b, i: (b * nb + jnp.maximum(i - 1, 0), 0))
    return pl.pallas_call(
        body, name="attn_forward", grid=(bl, nb),
        in_specs=[pl.BlockSpec(memory_space=pltpu.SMEM), cur(512), prev(128), cur(128), prev(128), cur(128)],
        out_specs=(cur(512), cur(N_HEADS)),
        out_shape=(jax.ShapeDtypeStruct((t, D_ATTN), F32), jax.ShapeDtypeStruct((t, N_HEADS), F32)),
        compiler_params=_tc_params(("arbitrary", "arbitrary")),
    )(sinks, q, k, k, v, v)


def _attn_backward(q, k, v, o, do, lse, sinks, bl, nb):
    t = q.shape[0]

    def body(sink_ref, qc_ref, kp_ref, kc_ref, vp_ref, vc_ref, oc_ref, doc_ref, lc_ref,
             dq_ref, dk_ref, dv_ref, ds_ref, dk_carry, dv_carry):
        b, i = pl.program_id(0), pl.program_id(1)
        live = i < nb

        @pl.when(i == 0)
        def _():
            dk_carry[...] = jnp.zeros((ATT_BLOCK, KV_HEADS * HEAD_DIM), F32)
            dv_carry[...] = jnp.zeros((ATT_BLOCK, KV_HEADS * HEAD_DIM), F32)

        dist, valid = _att_dist_mask(i == 0)
        valid = valid & live
        kk = jnp.concatenate([kp_ref[...], kc_ref[...]], axis=0)
        vv = jnp.concatenate([vp_ref[...], vc_ref[...]], axis=0)
        qc, oc, doc, lc = qc_ref[...], oc_ref[...], doc_ref[...], lc_ref[...]
        dsink_cols, dq_parts, dk_t, dv_t = [], [], [], []
        for kv in range(KV_HEADS):
            heads = range(kv * Q_PER_KV, (kv + 1) * Q_PER_KV)
            cols = slice(kv * HEAD_DIM, (kv + 1) * HEAD_DIM)
            kh, vh = kk[:, cols], vv[:, cols]
            slope = _per_head_col([2.0 ** (-(h + 1)) for h in heads])
            sink = _per_head_col([sink_ref[h] for h in heads])
            q4, do4 = _stack_heads(qc, kv), _stack_heads(doc, kv)
            delta = jnp.sum(do4 * _stack_heads(oc, kv), axis=-1, keepdims=True)
            lse4 = _stack_cols(lc, kv)
            s = _mm_nt(q4, kh) * ATT_SCALE - slope * dist
            p = jnp.where(valid, jnp.exp(s - lse4), 0.0)
            dsc = p * (_mm_nt(do4, vh) - delta)
            dq4 = _mm(dsc, kh) * ATT_SCALE
            dk_t.append(_mm_tn(q4, dsc) * ATT_SCALE)
            dv_t.append(_mm_tn(do4, p))
            dsink4 = jnp.where(live, jnp.exp(sink - lse4) * delta, 0.0)
            for g, h in enumerate(heads):
                rows = slice(g * ATT_BLOCK, (g + 1) * ATT_BLOCK)
                dq_parts.append((h, dq4[rows, :]))
                dsink_cols.append(-jnp.sum(dsink4[rows, :], axis=0, keepdims=True))
        dsink = jnp.concatenate(dsink_cols, axis=1)
        for out_ref, carry, parts in ((dk_ref, dk_carry, dk_t), (dv_ref, dv_carry, dv_t)):
            both = jnp.concatenate(parts, axis=0)
            out_ref[...] = (carry[...] + both[:, 0:ATT_BLOCK]).T
            carry[...] = both[:, ATT_BLOCK:]

        @pl.when(live)
        def _():
            for h, part in dq_parts:
                dq_ref[:, h * HEAD_DIM:(h + 1) * HEAD_DIM] = part

        @pl.when((b == 0) & (i == 0))
        def _():
            ds_ref[...] = dsink

        @pl.when((b != 0) | (i != 0))
        def _():
            ds_ref[...] += dsink

    cur_i = lambda i: jnp.minimum(i, nb - 1)
    cur = lambda w: pl.BlockSpec((ATT_BLOCK, w), lambda b, i: (b * nb + cur_i(i), 0))
    prev = lambda w: pl.BlockSpec((ATT_BLOCK, w), lambda b, i: (b * nb + jnp.maximum(cur_i(i) - 1, 0), 0))
    behind = lambda w: pl.BlockSpec((ATT_BLOCK, w), lambda b, i: (b * nb + jnp.maximum(i - 1, 0), 0))
    return pl.pallas_call(
        body, name="attn_backward", grid=(bl, nb + 1),
        in_specs=[pl.BlockSpec(memory_space=pltpu.SMEM), cur(512), prev(128), cur(128), prev(128), cur(128),
                  cur(512), cur(512), cur(N_HEADS)],
        out_specs=(cur(512), behind(128), behind(128), pl.BlockSpec((1, N_HEADS), lambda b, i: (0, 0))),
        out_shape=(jax.ShapeDtypeStruct((t, D_ATTN), F32), jax.ShapeDtypeStruct((t, 128), F32),
                   jax.ShapeDtypeStruct((t, 128), F32), jax.ShapeDtypeStruct((1, N_HEADS), F32)),
        scratch_shapes=[pltpu.VMEM((ATT_BLOCK, KV_HEADS * HEAD_DIM), F32), pltpu.VMEM((ATT_BLOCK, KV_HEADS * HEAD_DIM), F32)],
        compiler_params=_tc_params(("arbitrary", "arbitrary")),
    )(sinks, q, k, k, v, v, o, do, lse)


def _mix_forward_backward(x2, y2, z_ssm, attn, z_attn, p2, target2, w_glu, b_glu, w_out, g_post, w_gate, b_gate,
                          w_proj, tm):
    t = x2.shape[0]

    def body(x_ref, y_ref, zs_ref, at_ref, za_ref, p_ref, tg_ref,
             wglu_ref, bglu_ref, wout_ref, gpost_ref, wgate_ref, bgate_ref, wproj_ref,
             loss_ref, dh1_ref, dy_ref, dzs_ref, dat_ref, dza_ref,
             dwglu_ref, dbglu_ref, dwout_ref, dgpost_ref, dwgate_ref, dbgate_ref, dwproj_ref,
             dwout16_ref, dwgate16_ref, dwproj16_ref, dwglu16_ref):
        i = pl.program_id(0)
        gpost = gpost_ref[...]

        @pl.when(i == 0)
        def _():
            for ref in (dwglu_ref, dbglu_ref, dwout_ref, dgpost_ref, dwgate_ref, dbgate_ref, dwproj_ref, loss_ref):
                ref[...] = jnp.zeros(ref.shape, F32)

        def chain(rows):
            y = y_ref[rows, :]
            u3 = GELU_C * (y + GELU_K * y * y * y)
            th = jnp.tanh(u3)
            gl = 0.5 * y * (1.0 + th)
            a = _mm(gl, wglu_ref[...]) + bglu_ref[...]
            sa = _sigmoid(a)
            glu = gl * sa
            zs = zs_ref[rows, :]
            sgs = _sigmoid(zs)
            ssm_out = glu * (zs * sgs)
            za = za_ref[rows, :]
            sga = _sigmoid(za)
            at = at_ref[rows, :]
            attn_out = at * (za * sga)
            cat = jnp.concatenate([ssm_out, attn_out], axis=-1).astype(BF16)
            mixed = _mm(cat, wout_ref[...])
            r2 = lax.rsqrt(jnp.mean(mixed * mixed, axis=-1, keepdims=True) + EPS)
            nhat = mixed * r2
            h1 = x_ref[rows, :] + nhat * gpost
            gate = _sigmoid(_mm(h1, wgate_ref[...]) + bgate_ref[...])
            pv = p_ref[rows, :]
            pp = _mm(pv, wproj_ref[...])
            h2 = h1 + gate * pp
            err = h2 - tg_ref[rows, :]
            loss_part = jnp.sum(jnp.sum(err * err, axis=-1, keepdims=True), axis=0, keepdims=True) * (0.5 / D_MODEL)
            dh2 = err * (1.0 / D_MODEL)
            dgp = dh2 * pp * gate * (1.0 - gate)
            dpp = dh2 * gate
            dh1 = dh2 + _mm_nt(dgp, wgate_ref[...])
            dh1_ref[rows, :] = dh1
            dnhat = dh1 * gpost
            dmixed = r2 * (dnhat - nhat * jnp.mean(dnhat * nhat, axis=-1, keepdims=True))
            dcat = _mm_nt(dmixed, wout_ref[...])
            dso, dao = dcat[:, 0:D_SSM], dcat[:, D_SSM:]
            dat_ref[rows, :] = dao * (za * sga)
            dza_ref[rows, :] = (dao * at * (sga * (1.0 + za * (1.0 - sga)))).astype(BF16)
            dzs_ref[rows, :] = (dso * glu * (sgs * (1.0 + zs * (1.0 - sgs)))).astype(BF16)
            dglu = dso * (zs * sgs)
            da = dglu * gl * sa * (1.0 - sa)
            dgl = dglu * sa + _mm_nt(da, wglu_ref[...])
            dgelu = 0.5 * (1.0 + th) + 0.5 * y * (1.0 - th * th) * (GELU_C * (1.0 + 3.0 * GELU_K * y * y))
            dy_ref[rows, :] = dgl * dgelu
            return dict(gl=gl.astype(BF16), da=da.astype(BF16), cat=cat, dmixed=dmixed.astype(BF16),
                        h1=h1.astype(BF16), dgp=dgp.astype(BF16), pv=pv.astype(BF16), dpp=dpp.astype(BF16),
                        dbglu=jnp.sum(da, axis=0, keepdims=True), dgpost=jnp.sum(dh1 * nhat, axis=0, keepdims=True),
                        dbgate=jnp.sum(dgp, axis=0, keepdims=True), loss=loss_part)

        groups = [chain(slice(k * (tm // MIX_GROUPS), (k + 1) * (tm // MIX_GROUPS))) for k in range(MIX_GROUPS)]
        rows_of = lambda name: jnp.concatenate([g[name] for g in groups], axis=0)
        total = lambda name: sum(g[name] for g in groups)
        parts = (
            (dwglu_ref, _mm_tn(rows_of("gl"), rows_of("da"))), (dbglu_ref, total("dbglu")),
            (dwout_ref, _mm_tn(rows_of("cat"), rows_of("dmixed"))), (dgpost_ref, total("dgpost")),
            (dwgate_ref, _mm_tn(rows_of("h1"), rows_of("dgp"))), (dbgate_ref, total("dbgate")),
            (dwproj_ref, _mm_tn(rows_of("pv"), rows_of("dpp"))), (loss_ref, total("loss")),
        )

        for ref, val in parts:
            ref[...] += val

        @pl.when(i == t // tm - 1)
        def _():
            for ref16, ref in ((dwout16_ref, dwout_ref), (dwgate16_ref, dwgate_ref), (dwproj16_ref, dwproj_ref),
                               (dwglu16_ref, dwglu_ref)):
                def to16(r, ref16=ref16, ref=ref):
                    ref16[r, :] = ref[r, :].astype(BF16)

                _row_chunks(ref.shape[0], to16)

    row = lambda w: pl.BlockSpec((tm, w), lambda i: (i, 0))
    acc = lambda r, c, dt=F32: (_const_spec((r, c)), jax.ShapeDtypeStruct((r, c), dt))
    accs = [acc(D_SSM, D_SSM), acc(1, D_SSM), acc(D_MODEL, D_MODEL), acc(1, D_MODEL), acc(D_MODEL, D_MODEL),
            acc(1, D_MODEL), acc(D_PLE, D_MODEL),
            acc(D_MODEL, D_MODEL, BF16), acc(D_MODEL, D_MODEL, BF16), acc(D_PLE, D_MODEL, BF16), acc(D_SSM, D_SSM, BF16)]
    return pl.pallas_call(
        body, name="mix_forward_backward", grid=(t // tm,),
        in_specs=[row(D_MODEL), row(512), row(512), row(512), row(512), row(D_PLE), row(D_MODEL),
                  _const_spec((D_SSM, D_SSM)), _const_spec((1, D_SSM)), _const_spec((D_MODEL, D_MODEL)),
                  _const_spec((1, D_MODEL)), _const_spec((D_MODEL, D_MODEL)), _const_spec((1, D_MODEL)),
                  _const_spec((D_PLE, D_MODEL))],
        out_specs=(_const_spec((1, 1)), row(D_MODEL), row(512), row(512), row(512), row(512))
        + tuple(a[0] for a in accs),
        out_shape=(jax.ShapeDtypeStruct((1, 1), F32), jax.ShapeDtypeStruct((t, D_MODEL), F32),
                   jax.ShapeDtypeStruct((t, 512), F32),
                   jax.ShapeDtypeStruct((t, 512), BF16), jax.ShapeDtypeStruct((t, 512), F32),
                   jax.ShapeDtypeStruct((t, 512), BF16)) + tuple(a[1] for a in accs),
        compiler_params=_tc_params(("arbitrary",)),
    )(x2, y2, z_ssm, attn, z_attn, p2, target2, w_glu, b_glu, w_out, g_post, w_gate, b_gate, w_proj)


def _in_backward(x2, dh1, du, dz_ssm, dq, dk, dv, dz_attn, g_pre, w_in, tm):
    t = x2.shape[0]

    def body(x_ref, dh1_ref, du_ref, dzs_ref, dq_ref, dk_ref, dv_ref, dza_ref, g_ref, w_ref,
             gx_ref, dw_ref, dg_ref, dw16_ref):
        i = pl.program_id(0)
        xv = x_ref[...]
        r = lax.rsqrt(jnp.mean(xv * xv, axis=-1, keepdims=True) + EPS)
        xhat = xv * r
        g = g_ref[...]
        hn = (xhat * g).astype(BF16)
        dproj = jnp.concatenate([du_ref[...].astype(BF16), dzs_ref[...].astype(BF16), dq_ref[...].astype(BF16),
                                 dk_ref[...].astype(BF16), dv_ref[...].astype(BF16), dza_ref[...].astype(BF16)],
                                axis=-1)
        dhn = _mm(dproj, w_ref[...])
        dxhat = dhn * g
        gx_ref[...] = dh1_ref[...] + r * (dxhat - xhat * jnp.mean(dxhat * xhat, axis=-1, keepdims=True))
        @pl.when(i == 0)
        def _():
            dw_ref[...] = jnp.zeros((D_IN, D_MODEL), F32)
            dg_ref[...] = jnp.zeros((1, D_MODEL), F32)

        dw_ref[...] += _mm_tn(dproj, hn)
        dg_ref[...] += jnp.sum(dhn * xhat, axis=0, keepdims=True)

        @pl.when(i == t // tm - 1)
        def _():
            def to16(r):
                dw16_ref[r, :] = dw_ref[r, :].astype(BF16)

            _row_chunks(D_IN, to16)

    row = lambda w: pl.BlockSpec((tm, w), lambda i: (i, 0))
    return pl.pallas_call(
        body, name="in_backward", grid=(t // tm,),
        in_specs=[row(D_MODEL), row(D_MODEL), row(512), row(512), row(512), row(128), row(128), row(512),
                  _const_spec((1, D_MODEL)), _const_spec((D_IN, D_MODEL))],
        out_specs=(row(D_MODEL), _const_spec((D_IN, D_MODEL)), _const_spec((1, D_MODEL)),
                   _const_spec((D_IN, D_MODEL))),
        out_shape=(jax.ShapeDtypeStruct((t, D_MODEL), F32), jax.ShapeDtypeStruct((D_IN, D_MODEL), F32),
                   jax.ShapeDtypeStruct((1, D_MODEL), F32), jax.ShapeDtypeStruct((D_IN, D_MODEL), BF16)),
        compiler_params=_tc_params(("arbitrary",)),
    )(x2, dh1, du, dz_ssm, dq, dk, dv, dz_attn, g_pre, w_in)


def _local_step(x, p, target, pre_norm_g, w_in, ssm_lam_re, ssm_lam_im, ssm_log_step, ssm_b_re, ssm_b_im, ssm_c_re,
                ssm_c_im, ssm_d, ssm_b_glu, attn_sinks, post_norm_g, pl_b_gate, late):
    bl, seq, _ = x.shape
    seg = seq // N_SEG
    nb = seq // ATT_BLOCK
    t = bl * seq
    x2 = x.reshape(t, D_MODEL)
    p2 = p.reshape(t, D_PLE)
    tg2 = target.reshape(t, D_MODEL)

    lam_re, lam_im = ssm_lam_re, ssm_lam_im
    log_step = ssm_log_step.reshape(SSM_G, 1)
    a_re_row, a_im_row, pw_re, pw_im, bcat, bcat_t, ccat, ccat_t = _ssm_prep(
        lam_re, lam_im, log_step, ssm_b_re, ssm_b_im, ssm_c_re, ssm_c_im, seg)
    d_row = ssm_d.reshape(1, D_SSM)

    segments = lambda a: a.reshape(bl, N_SEG, seg, D_SSM)
    u, z_ssm, q, k, v, z_attn = _in_proj(x2, pre_norm_g.reshape(1, D_MODEL), w_in, seg)
    (y, states, carries), gathered = _ssm_forward(
        segments(u), bcat, ccat, a_re_row, a_im_row, pw_re, pw_im, d_row, late, seg)
    w_out, w_gate, w_proj, w_glu = (_gathered_to_full(n, g) for n, g in zip(LATE_NAMES, gathered))
    sinks = attn_sinks.reshape(N_HEADS)
    attn, lse = _attn_forward(q, k, v, sinks, bl, nb)
    (loss, dh1, dy, dz_ssm, dattn, dz_attn, d_w_glu, d_b_glu, d_w_out, d_g_post, d_w_gate, d_b_gate,
     d_w_proj, *late16) = _mix_forward_backward(
        x2, y.reshape(t, D_SSM), z_ssm, attn, z_attn, p2, tg2, w_glu,
        ssm_b_glu.reshape(1, D_SSM), w_out, post_norm_g.reshape(1, D_MODEL), w_gate, pl_b_gate.reshape(1, D_MODEL),
        w_proj, seg)
    owned = lambda ds: [_full_to_owned(n, d) for n, d in zip(LATE_NAMES, ds)]
    dq, dk, dv, d_sinks = _attn_backward(q, k, v, attn, dattn, lse, sinks, bl, nb)
    (du, d_bcat, d_ccat_t, da_re, da_im, d_d), late_grads = _ssm_backward(
        segments(u), segments(dy), states, carries, bcat_t, ccat_t, a_re_row, a_im_row, pw_re, pw_im,
        d_row, owned(late16), owned((d_w_out, d_w_gate, d_w_proj, d_w_glu)), seg)
    grad_x, d_w_in, d_g_pre, d_w_in16 = _in_backward(
        x2, dh1, du.reshape(t, D_SSM), dz_ssm, dq, dk, dv, dz_attn, pre_norm_g.reshape(1, D_MODEL), w_in, seg)
    d_lam_re, d_lam_im, d_ls, d_b_re, d_b_im, d_c_re, d_c_im = _ssm_param_grads(
        lam_re, lam_im, log_step, ssm_b_re, ssm_b_im, da_re.reshape(SSM_G, SSM_N), da_im.reshape(SSM_G, SSM_N),
        d_bcat, d_ccat_t)
    grads = {
        "pre_norm_g": d_g_pre, "w_in": d_w_in, "w_in16": d_w_in16, "ssm_lam_re": d_lam_re, "ssm_lam_im": d_lam_im,
        "ssm_log_step": d_ls, "ssm_b_re": d_b_re, "ssm_b_im": d_b_im, "ssm_c_re": d_c_re, "ssm_c_im": d_c_im,
        "ssm_d": d_d, "ssm_b_glu": d_b_glu, "attn_sinks": d_sinks, "post_norm_g": d_g_post, "pl_b_gate": d_b_gate,
    }
    return loss, grad_x.reshape(bl, seq, D_MODEL), grads, late_grads


LATE_NAMES = ("w_out", "pl_w_gate", "pl_w_proj", "ssm_w_glu")
BIG_NAMES = ("w_in",) + LATE_NAMES
COL_SHARDED = {"w_in": D_IN // N_DEV, "pl_w_proj": D_MODEL // N_DEV}
WEIGHT_NAMES = ("pre_norm_g", "w_in", "ssm_lam_re", "ssm_lam_im", "ssm_log_step", "ssm_b_re", "ssm_b_im", "ssm_c_re",
                "ssm_c_im", "ssm_d", "ssm_w_glu", "ssm_b_glu", "attn_sinks", "w_out", "post_norm_g", "pl_w_proj",
                "pl_w_gate", "pl_b_gate")


TRANSPOSED = {"w_in": (0, 1), "ssm_b_re": (1, 2), "ssm_b_im": (1, 2)}


def _kernel_form(name, a):
    a = a[0]
    if name in TRANSPOSED:
        a = jnp.swapaxes(a, *TRANSPOSED[name])
    if name in ("ssm_b_re", "ssm_b_im", "ssm_c_re", "ssm_c_im"):
        a = a.reshape(SSM_G * SSM_P, SSM_N)
    return a


def _given_form(name, a, shape):
    if name in TRANSPOSED:
        i, j = TRANSPOSED[name]
        swapped = list(shape[1:])
        swapped[i], swapped[j] = swapped[j], swapped[i]
        return jnp.swapaxes(a.reshape(swapped), i, j).reshape(shape)
    return a.reshape(shape)


def _gathered_to_full(name, g):
    _, rows, cols = g.shape
    if name in COL_SHARDED:
        return jnp.swapaxes(g, 0, 1).reshape(rows, N_DEV * cols)
    return g.reshape(N_DEV * rows, cols)


def _full_to_owned(name, full):
    if name in COL_SHARDED:
        return jnp.swapaxes(full.reshape(full.shape[0], N_DEV, COL_SHARDED[name]), 0, 1)
    return full.reshape(N_DEV, full.shape[0] // N_DEV, full.shape[1])


def kernel(x, p, pre_norm_g, w_in, ssm_lam_re, ssm_lam_im, ssm_log_step, ssm_b_re, ssm_b_im, ssm_c_re, ssm_c_im, ssm_d, ssm_w_glu, ssm_b_glu, attn_sinks, w_out, post_norm_g, pl_w_proj, pl_w_gate, pl_b_gate, loss_target, m_pre_norm_g, m_w_in, m_ssm_lam_re, m_ssm_lam_im, m_ssm_log_step, m_ssm_b_re, m_ssm_b_im, m_ssm_c_re, m_ssm_c_im, m_ssm_d, m_ssm_w_glu, m_ssm_b_glu, m_attn_sinks, m_w_out, m_post_norm_g, m_pl_w_proj, m_pl_w_gate, m_pl_b_gate, v_pre_norm_g, v_w_in, v_ssm_lam_re, v_ssm_lam_im, v_ssm_log_step, v_ssm_b_re, v_ssm_b_im, v_ssm_c_re, v_ssm_c_im, v_ssm_d, v_ssm_w_glu, v_ssm_b_glu, v_attn_sinks, v_w_out, v_post_norm_g, v_pl_w_proj, v_pl_w_gate, v_pl_b_gate):
    w = dict(pre_norm_g=pre_norm_g, w_in=w_in, ssm_lam_re=ssm_lam_re, ssm_lam_im=ssm_lam_im, ssm_log_step=ssm_log_step,
             ssm_b_re=ssm_b_re, ssm_b_im=ssm_b_im, ssm_c_re=ssm_c_re, ssm_c_im=ssm_c_im, ssm_d=ssm_d, ssm_w_glu=ssm_w_glu,
             ssm_b_glu=ssm_b_glu, attn_sinks=attn_sinks, w_out=w_out, post_norm_g=post_norm_g, pl_w_proj=pl_w_proj,
             pl_w_gate=pl_w_gate, pl_b_gate=pl_b_gate)
    m = dict(pre_norm_g=m_pre_norm_g, w_in=m_w_in, ssm_lam_re=m_ssm_lam_re, ssm_lam_im=m_ssm_lam_im,
             ssm_log_step=m_ssm_log_step, ssm_b_re=m_ssm_b_re, ssm_b_im=m_ssm_b_im, ssm_c_re=m_ssm_c_re,
             ssm_c_im=m_ssm_c_im, ssm_d=m_ssm_d, ssm_w_glu=m_ssm_w_glu, ssm_b_glu=m_ssm_b_glu, attn_sinks=m_attn_sinks,
             w_out=m_w_out, post_norm_g=m_post_norm_g, pl_w_proj=m_pl_w_proj, pl_w_gate=m_pl_w_gate,
             pl_b_gate=m_pl_b_gate)
    v = dict(pre_norm_g=v_pre_norm_g, w_in=v_w_in, ssm_lam_re=v_ssm_lam_re, ssm_lam_im=v_ssm_lam_im,
             ssm_log_step=v_ssm_log_step, ssm_b_re=v_ssm_b_re, ssm_b_im=v_ssm_b_im, ssm_c_re=v_ssm_c_re,
             ssm_c_im=v_ssm_c_im, ssm_d=v_ssm_d, ssm_w_glu=v_ssm_w_glu, ssm_b_glu=v_ssm_b_glu, attn_sinks=v_attn_sinks,
             w_out=v_w_out, post_norm_g=v_post_norm_g, pl_w_proj=v_pl_w_proj, pl_w_gate=v_pl_w_gate,
             pl_b_gate=v_pl_b_gate)
    kf = lambda d: {n: _kernel_form(n, a) for n, a in d.items()}
    wk, mk, vk = kf(w), kf(m), kf(v)

    (gathered,) = _allgather_weights([wk["w_in"]])
    loss, grad_x, grads, g_late = _local_step(
        x, p[0], loss_target, wk["pre_norm_g"], gathered.reshape(D_IN, D_MODEL), wk["ssm_lam_re"], wk["ssm_lam_im"],
        wk["ssm_log_step"], wk["ssm_b_re"], wk["ssm_b_im"], wk["ssm_c_re"], wk["ssm_c_im"], wk["ssm_d"],
        wk["ssm_b_glu"], wk["attn_sinks"], wk["post_norm_g"], wk["pl_b_gate"], [wk[n] for n in LATE_NAMES])

    owned = lambda g: g.reshape(N_DEV, D_IN // N_DEV, D_MODEL)
    tiny_form = lambda d: [d[n].reshape(rows, cols) for n, rows, cols in TINY]
    med_form = lambda d: [d[n].reshape(N_DEV, rows // N_DEV, cols) for n, rows, cols in MEDIUM]
    g_big, loss, g_tiny, g_med = _reduce_final(
        [owned(grads["w_in16"])], [owned(grads["w_in"])], loss, tiny_form(grads), med_form(grads))
    names = BIG_NAMES + tuple(n for n, _, _ in TINY + MEDIUM)
    form = lambda d: [d[n] for n in BIG_NAMES] + tiny_form(d) + med_form(d)
    updated = _adamw_update(g_big + g_late + g_tiny + g_med, form(wk), form(mk), form(vk))
    vals = dict(zip(names, updated))
    results = [[_given_form(n, vals[n][kind], w[n].shape) for n in WEIGHT_NAMES] for kind in range(4)]
    return (loss.reshape(()), grad_x, *results[0], *results[1], *results[2], *results[3])
```

```python
import functools
import math

import jax
import jax.numpy as jnp
from jax import lax
from jax.experimental import pallas as pl
from jax.experimental.pallas import tpu as pltpu

F32 = jnp.float32
BF16 = jnp.bfloat16

D_MODEL = 1024
D_SSM = 512
D_ATTN = 512
SSM_P = 16
SSM_G = 32
SSM_N = 64
N_HEADS = 8
KV_HEADS = 2
Q_PER_KV = 4
HEAD_DIM = 64
ATT_BLOCK = 128
D_PLE = 256
D_IN = 2304
EPS = 1e-6
N_DEV = 8
N_SEG = 8
G_TILE = 8
N_GT = SSM_G // G_TILE
CH_T = G_TILE * SSM_P
ST_T = G_TILE * SSM_N
N_STATE = SSM_G * SSM_N
SCAN_UNROLL = 4
MIX_GROUPS = 1
TOKEN_TILE = 256
TOKEN_TILE_WIDE = 512
LANES = 128
VMEM_LIMIT = 60 * 1024 * 1024

ADAM_LR = 0.001
ADAM_B1 = 0.9
ADAM_B2 = 0.999
ADAM_EPS = 1e-08
ADAM_WD = 0.01
ADAM_STEP = 10

GELU_C = math.sqrt(2.0 / math.pi)
GELU_K = 0.044715
ATT_SCALE = 1.0 / math.sqrt(HEAD_DIM)
NEG_BIG = -1e30


def _mm(a, b):
    return jnp.dot(a.astype(BF16), b.astype(BF16), preferred_element_type=F32)


def _mm_nt(a, b):
    return lax.dot_general(a.astype(BF16), b.astype(BF16), (((1,), (1,)), ((), ())), preferred_element_type=F32)


def _mm_tn(a, b):
    return lax.dot_general(a.astype(BF16), b.astype(BF16), (((0,), (0,)), ((), ())), preferred_element_type=F32)


def _sigmoid(x):
    return 1.0 / (1.0 + jnp.exp(-x))


def _tc_params(sem):
    return pltpu.CompilerParams(dimension_semantics=sem, vmem_limit_bytes=VMEM_LIMIT)


def _const_spec(shape):
    nd = len(shape)
    return pl.BlockSpec(shape, lambda *_: (0,) * nd)


def _mesh_pos():
    return lax.axis_index("x"), lax.axis_index("y"), lax.axis_index("c")


ROW_CHUNKS = (64, 32, 16)


def _row_chunk(nrows):
    return next((c for c in ROW_CHUNKS if nrows % c == 0), None)


def _row_chunks(nrows, fn, chunk=None, init=None):
    chunk = chunk or _row_chunk(nrows)

    def step(i, carry):
        rows = pl.ds(pl.multiple_of(i * chunk, chunk), chunk)
        if init is None:
            fn(rows)
            return carry
        return fn(rows, carry)

    return lax.fori_loop(0, nrows // chunk, step, 0 if init is None else init)


def _slot(px, py, pc):
    return 4 * px + 2 * py + pc


def _allgather_weights(shards):
    n = len(shards)

    def body(*refs):
        srcs, outs, (send_sems, recv_sems) = refs[:n], refs[n:2 * n], refs[2 * n:]
        x, y, c = _mesh_pos()
        me, sibling = (x, y, c), (x, y, 1 - c)
        chips = [(1 - x, y), (x, 1 - y), (1 - x, 1 - y)]

        def copy(a, k, block, to):
            blk = outs[a].at[_slot(*block)]
            return pltpu.make_async_remote_copy(
                src_ref=blk, dst_ref=blk, send_sem=send_sems.at[7 * a + k], recv_sem=recv_sems.at[7 * a + k],
                device_id=to, device_id_type=pl.DeviceIdType.MESH)

        sends = []
        for a in range(n):
            mine = outs[a].at[_slot(*me)]

            def cast(r, mine=mine, src=srcs[a]):
                mine[r, :] = src[r, :].astype(BF16)

            _row_chunks(srcs[a].shape[0], cast)
            first = [copy(a, 0, me, sibling)] + [copy(a, 1 + j, me, (*chip, c)) for j, chip in enumerate(chips)]
            for cp in first:
                cp.start()
            sends += first
        for a in range(n):
            for j, chip in enumerate(chips):
                copy(a, 1 + j, (*chip, c), me).wait_recv()
                fwd = copy(a, 4 + j, (*chip, c), sibling)
                fwd.start()
                sends.append(fwd)
        for a in range(n):
            copy(a, 0, sibling, me).wait_recv()
            for j, chip in enumerate(chips):
                copy(a, 4 + j, (*chip, 1 - c), me).wait_recv()
        for cp in sends:
            cp.wait_send()

    vm = pl.BlockSpec(memory_space=pltpu.VMEM)
    return pl.pallas_call(
        body, name="allgather_weights",
        out_shape=tuple(jax.ShapeDtypeStruct((N_DEV,) + s.shape, BF16) for s in shards),
        in_specs=[vm] * n, out_specs=(vm,) * n,
        scratch_shapes=[pltpu.SemaphoreType.DMA((7 * n,)), pltpu.SemaphoreType.DMA((7 * n,))],
        compiler_params=pltpu.CompilerParams(vmem_limit_bytes=VMEM_LIMIT),
    )(*shards)


def _adamw(w, g, m, v):
    m = ADAM_B1 * m + (1.0 - ADAM_B1) * g
    v = ADAM_B2 * v + (1.0 - ADAM_B2) * (g * g)
    m_hat = m / (1.0 - ADAM_B1 ** ADAM_STEP)
    v_hat = v / (1.0 - ADAM_B2 ** ADAM_STEP)
    delta = -ADAM_LR * (m_hat / (jnp.sqrt(v_hat) + ADAM_EPS) + ADAM_WD * w)
    return delta, m, v


def _remote(src, dst, send_sems, recv_sems, k, to):
    return pltpu.make_async_remote_copy(src_ref=src, dst_ref=dst, send_sem=send_sems.at[k], recv_sem=recv_sems.at[k],
                                        device_id=to, device_id_type=pl.DeviceIdType.MESH)


def _big_reduce_phases(g16_r, go_r, outs, send2, recv1, recv2, s_send, s_recv):
    n = len(g16_r)
    x, y, c = _mesh_pos()
    sibling = (x, y, 1 - c)
    chips = [(1 - x, y), (x, 1 - y), (1 - x, 1 - y)]
    all_chips = [(x, y)] + chips
    lvl1 = []
    for a in range(n):
        cps = [_remote(g16_r[a].at[_slot(*chip, 1 - c)], recv1[a].at[j], s_send, s_recv, 7 * a + j, sibling)
               for j, chip in enumerate(all_chips)]
        for cp in cps:
            cp.start()
        lvl1.append(cps)
    yield
    lvl2 = []
    for a in range(n):
        for cp in lvl1[a]:
            cp.wait_recv()
        og = outs[a]

        def partials(r, a=a, og=og):
            og[r, :] = go_r[a][r, :] + recv1[a][0, r, :].astype(F32)
            for j, chip in enumerate(chips):
                mine16 = g16_r[a][_slot(*chip, c), r, :].astype(F32)
                send2[a][j, r, :] = (mine16 + recv1[a][1 + j, r, :].astype(F32)).astype(BF16)

        _row_chunks(go_r[a].shape[0], partials)
        cps = [_remote(send2[a].at[j], recv2[a].at[j], s_send, s_recv, 7 * a + 4 + j, (*chip, c))
               for j, chip in enumerate(chips)]
        for cp in cps:
            cp.start()
        lvl2.append(cps)
    yield
    for a in range(n):
        for cp in lvl2[a]:
            cp.wait_recv()
        og = outs[a]

        def total(r, a=a, og=og):
            g = og[r, :]
            for j in range(3):
                g = g + recv2[a][j, r, :].astype(F32)
            og[r, :] = g

        _row_chunks(go_r[a].shape[0], total)
    yield
    for cps in lvl1 + lvl2:
        for cp in cps:
            cp.wait_send()


def _adamw_update(g, w, m, v):
    n = len(g)

    def body(*refs):
        g_r, w_r, m_r, v_r = (refs[i * n:(i + 1) * n] for i in range(4))
        outs = refs[4 * n:]
        for a in range(n):
            og, od, om, ov = outs[4 * a:4 * a + 4]

            def update(idx, a=a, og=og, od=od, om=om, ov=ov):
                gv = g_r[a][idx]
                d, nm, nv = _adamw(w_r[a][idx], gv, m_r[a][idx], v_r[a][idx])
                og[idx] = gv
                od[idx] = d
                om[idx] = nm
                ov[idx] = nv

            shape = g_r[a].shape
            if len(shape) == 3:
                for b in range(shape[0]):
                    update(b)
            elif _row_chunk(shape[0]) is not None:
                _row_chunks(shape[0], update)
            else:
                update(Ellipsis)

    vm = pl.BlockSpec(memory_space=pltpu.VMEM)
    res = pl.pallas_call(
        body, name="adamw_update",
        out_shape=tuple(jax.ShapeDtypeStruct(t.shape, F32) for t in g for _ in range(4)),
        in_specs=[vm] * (4 * n), out_specs=(vm,) * (4 * n),
        compiler_params=pltpu.CompilerParams(vmem_limit_bytes=VMEM_LIMIT),
    )(*g, *w, *m, *v)
    return [res[4 * a:4 * a + 4] for a in range(n)]


TINY = (("pre_norm_g", 1, 1024), ("post_norm_g", 1, 1024), ("pl_b_gate", 1, 1024), ("ssm_d", 1, 512),
        ("ssm_b_glu", 1, 512), ("ssm_log_step", 1, 32), ("attn_sinks", 1, 8), ("ssm_lam_re", 32, 64),
        ("ssm_lam_im", 32, 64))
MEDIUM = (("ssm_b_re", SSM_G * SSM_P, SSM_N), ("ssm_b_im", SSM_G * SSM_P, SSM_N), ("ssm_c_re", SSM_G * SSM_P, SSM_N),
          ("ssm_c_im", SSM_G * SSM_P, SSM_N))


def _stage_rows():
    offs, r = {}, 0
    for name, rows, cols in TINY + (("loss", 1, 1),):
        if rows > 1:
            r = -(-r // 8) * 8
        offs[name] = r
        r += rows if rows > 1 else max(cols // LANES, 1)
    return offs, -(-r // 8) * 8


def _reduce_final(g16, g32, loss, g_tiny, g_med):
    nb_, nt, nm_ = len(g16), len(TINY), len(MEDIUM)
    offs, stage_rows = _stage_rows()

    def body(*refs):
        g16_r, go_r = refs[:nb_], refs[nb_:2 * nb_]
        base = 2 * nb_
        loss_r, gt, gm = refs[base], refs[base + 1:base + 1 + nt], refs[base + 1 + nt:base + 1 + nt + nm_]
        base += 1 + nt + nm_
        out_b = refs[base:base + nb_]
        base += nb_
        loss_o, out_t, out_m = refs[base], refs[base + 1:base + 1 + nt], refs[base + 1 + nt:base + 1 + nt + nm_]
        base += 1 + nt + nm_
        send2_b, recv1_b, recv2_b = (refs[base + i * nb_:base + (i + 1) * nb_] for i in range(3))
        base += 3 * nb_
        stage = refs[base]
        recv1, part, recv2 = (refs[base + 1 + i * nm_:base + 1 + (i + 1) * nm_] for i in range(3))
        bs_send, bs_recv, s_send, s_recv, own_sems = refs[base + 1 + 3 * nm_:base + 6 + 3 * nm_]
        own32 = refs[base + 6 + 3 * nm_:]
        me = _slot(*_mesh_pos())
        fetch = [pltpu.make_async_copy(go_r[a].at[me], own32[a], own_sems.at[a]) for a in range(nb_)]
        for cp in fetch:
            cp.start()
        big = _big_reduce_phases(g16_r, own32, out_b, send2_b, recv1_b, recv2_b, bs_send, bs_recv)
        small = small_phases(loss_r, gt, gm, loss_o, out_t, out_m, stage, recv1, part, recv2, s_send, s_recv)
        next(big)
        next(small)
        for cp in fetch:
            cp.wait()
        next(big)
        for _ in small:
            pass
        for _ in big:
            pass

    def small_phases(loss_r, gt, gm, loss_o, out_t, out_m, stage, recv1, part, recv2, s_send, s_recv):
        x, y, c = _mesh_pos()
        me = _slot(x, y, c)
        sibling = (x, y, 1 - c)
        chips = [(1 - x, y), (x, 1 - y), (1 - x, 1 - y)]
        all_chips = [(x, y)] + chips
        peers = [sibling] + [(*chip, c) for chip in chips] + [(*chip, 1 - c) for chip in chips]
        sem = iter(range(7 + 14 * nm_))
        lvl1 = []
        for a in range(nm_):
            cps = [_remote(gm[a].at[_slot(*chip, 1 - c)], recv1[a].at[j], s_send, s_recv, next(sem), sibling)
                   for j, chip in enumerate(all_chips)]
            for cp in cps:
                cp.start()
            lvl1.append(cps)
        mine = stage.at[me]
        mine[...] = jnp.zeros((stage_rows, LANES), F32)
        for (name, rows, cols), ref in zip(TINY + (("loss", 1, 1),), gt + (loss_r,)):
            r0 = offs[name]
            if rows > 1:
                mine[r0:r0 + rows, 0:cols] = ref[...]
            elif cols >= LANES:
                for i in range(cols // LANES):
                    mine[r0 + i:r0 + i + 1, :] = ref[:, i * LANES:(i + 1) * LANES]
            else:
                mine[r0:r0 + 1, 0:cols] = ref[...]
        tiny_cps = [_remote(mine, mine, s_send, s_recv, next(sem), peer) for peer in peers]
        for cp in tiny_cps:
            cp.start()
        yield
        lvl2 = []
        for a in range(nm_):
            for cp in lvl1[a]:
                cp.wait_recv()
            for j, chip in enumerate(all_chips):
                part[a][j] = gm[a][_slot(*chip, c)] + recv1[a][j]
            cps = [_remote(part[a].at[1 + j], recv2[a].at[j], s_send, s_recv, next(sem), (*chip, c))
                   for j, chip in enumerate(chips)]
            for cp in cps:
                cp.start()
            lvl2.append(cps)
        yield
        lvl3 = []
        for a in range(nm_):
            for cp in lvl2[a]:
                cp.wait_recv()
            blk = out_m[a].at[me]
            blk[...] = ((part[a][0] + recv2[a][0]) + recv2[a][1]) + recv2[a][2]
            cps = [_remote(blk, blk, s_send, s_recv, next(sem), peer) for peer in peers]
            for cp in cps:
                cp.start()
            lvl3.append(cps)
        yield
        for cp in tiny_cps:
            cp.wait_recv()
        tot = stage[0]
        for d in range(1, N_DEV):
            tot = tot + stage[d]
        loss_o[...] = tot[offs["loss"]:offs["loss"] + 1, 0:1]
        for k, (name, rows, cols) in enumerate(TINY):
            r0 = offs[name]
            if rows > 1:
                out_t[k][...] = tot[r0:r0 + rows, 0:cols]
            elif cols >= LANES:
                for i in range(cols // LANES):
                    out_t[k][:, i * LANES:(i + 1) * LANES] = tot[r0 + i:r0 + i + 1, :]
            else:
                out_t[k][...] = tot[r0:r0 + 1, 0:cols]
        for cps in lvl3:
            for cp in cps:
                cp.wait_recv()
        for cps in lvl1 + lvl2 + lvl3 + [tiny_cps]:
            for cp in cps:
                cp.wait_send()

    vmem = pl.BlockSpec(memory_space=pltpu.VMEM)
    t_shapes = [jax.ShapeDtypeStruct((rows, cols), F32) for _, rows, cols in TINY]
    m_shapes = [jax.ShapeDtypeStruct((N_DEV, rows // N_DEV, cols), F32) for _, rows, cols in MEDIUM]
    blk = [(rows // N_DEV, cols) for _, rows, cols in MEDIUM]
    shard = [g.shape[1:] for g in g16]
    scratch = ([pltpu.VMEM((3,) + s, BF16) for s in shard] + [pltpu.VMEM((4,) + s, BF16) for s in shard]
               + [pltpu.VMEM((3,) + s, BF16) for s in shard]
               + [pltpu.VMEM((N_DEV, stage_rows, LANES), F32)]
               + [pltpu.VMEM((4,) + b, F32) for b in blk] + [pltpu.VMEM((4,) + b, F32) for b in blk]
               + [pltpu.VMEM((3,) + b, F32) for b in blk]
               + [pltpu.SemaphoreType.DMA((7 * nb_,)), pltpu.SemaphoreType.DMA((7 * nb_,)),
                  pltpu.SemaphoreType.DMA((7 + 14 * nm_,)), pltpu.SemaphoreType.DMA((7 + 14 * nm_,)),
                  pltpu.SemaphoreType.DMA((nb_,))]
               + [pltpu.VMEM(s, F32) for s in shard])
    n_out = nb_ + 1 + nt + nm_
    res = pl.pallas_call(
        body, name="reduce_final",
        out_shape=tuple(jax.ShapeDtypeStruct(s, F32) for s in shard) + (jax.ShapeDtypeStruct((1, 1), F32),)
        + tuple(t_shapes) + tuple(m_shapes),
        in_specs=[vmem] * nb_ + [pl.BlockSpec(memory_space=pl.ANY)] * nb_ + [vmem] * (1 + nt + nm_),
        out_specs=(vmem,) * n_out, scratch_shapes=scratch,
        compiler_params=pltpu.CompilerParams(vmem_limit_bytes=VMEM_LIMIT),
    )(*g16, *g32, loss, *g_tiny, *g_med)
    return list(res[:nb_]), res[nb_], list(res[nb_ + 1:nb_ + 1 + nt]), list(res[nb_ + 1 + nt:])


def _gather_phases(shard_r, gath, cast, send_sems, recv_sems, local_sems):
    n = len(shard_r)
    x, y, c = _mesh_pos()
    me, sibling = (x, y, c), (x, y, 1 - c)
    chips = [(1 - x, y), (x, 1 - y), (1 - x, 1 - y)]

    def own(a, k, to):
        return _remote(cast[a], gath[a].at[_slot(*me)], send_sems, recv_sems, 7 * a + k, to)

    def passed(a, k, block, to):
        blk = gath[a].at[_slot(*block)]
        return _remote(blk, blk, send_sems, recv_sems, 7 * a + k, to)

    def keep(a):
        return pltpu.make_async_copy(cast[a], gath[a].at[_slot(*me)], local_sems.at[a])

    def start():
        for a in range(n):
            def to16(r, a=a):
                cast[a][r, :] = shard_r[a][r, :].astype(BF16)

            _row_chunks(shard_r[a].shape[0], to16)
            keep(a).start()
            own(a, 0, sibling).start()
            for j, chip in enumerate(chips):
                own(a, 1 + j, (*chip, c)).start()

    def relay():
        for a in range(n):
            for j, chip in enumerate(chips):
                passed(a, 1 + j, (*chip, c), me).wait_recv()
                passed(a, 4 + j, (*chip, c), sibling).start()

    def finish():
        for a in range(n):
            passed(a, 0, sibling, me).wait_recv()
            for j, chip in enumerate(chips):
                passed(a, 4 + j, (*chip, 1 - c), me).wait_recv()
            own(a, 0, sibling).wait_send()
            for j, chip in enumerate(chips):
                own(a, 1 + j, (*chip, c)).wait_send()
                passed(a, 4 + j, (*chip, c), sibling).wait_send()
            keep(a).wait()

    return start, relay, finish


def _gather_operands(shards):
    n = len(shards)
    return ((pl.BlockSpec(memory_space=pl.ANY),) * n,
            tuple(jax.ShapeDtypeStruct((N_DEV,) + s.shape, BF16) for s in shards),
            [pltpu.VMEM(s.shape, BF16) for s in shards]
            + [pltpu.SemaphoreType.DMA((7 * n,)), pltpu.SemaphoreType.DMA((7 * n,)), pltpu.SemaphoreType.DMA((n,))])


def _hosted_reduce_phases(g16_r, g32_r, red, own16, recv1, send2, recv2, own32, s_send, s_recv, s_local):
    n = len(g16_r)
    x, y, c = _mesh_pos()
    sibling = (x, y, 1 - c)
    chips = [(1 - x, y), (x, 1 - y), (1 - x, 1 - y)]
    all_chips = [(x, y)] + chips

    def lvl1(a, j):
        return _remote(g16_r[a].at[_slot(*all_chips[j], 1 - c)], recv1[a].at[j], s_send, s_recv, 7 * a + j, sibling)

    def lvl2(a, j):
        return _remote(send2[a].at[j], recv2[a].at[j], s_send, s_recv, 7 * a + 4 + j, (*chips[j], c))

    def mine(a, j):
        if j == 3:
            return pltpu.make_async_copy(g32_r[a].at[_slot(x, y, c)], own32[a], s_local.at[4 * a + j])
        return pltpu.make_async_copy(g16_r[a].at[_slot(*chips[j], c)], own16[a].at[j], s_local.at[4 * a + j])

    def start():
        for a in range(n):
            for j in range(4):
                mine(a, j).start()
            for j in range(4):
                lvl1(a, j).start()

    def middle():
        for a in range(n):
            for j in range(4):
                mine(a, j).wait()
            for j in range(4):
                lvl1(a, j).wait_recv()

            def partials(r, a=a):
                red[a][r, :] = own32[a][r, :] + recv1[a][0, r, :].astype(F32)
                for j in range(3):
                    send2[a][j, r, :] = (own16[a][j, r, :].astype(F32) + recv1[a][1 + j, r, :].astype(F32)).astype(BF16)

            _row_chunks(own32[a].shape[0], partials)
            for j in range(3):
                lvl2(a, j).start()

    def total():
        for a in range(n):
            for j in range(3):
                lvl2(a, j).wait_recv()

            def add(r, a=a):
                g = red[a][r, :]
                for j in range(3):
                    g = g + recv2[a][j, r, :].astype(F32)
                red[a][r, :] = g

            _row_chunks(own32[a].shape[0], add)

    def finish():
        for a in range(n):
            for j in range(4):
                lvl1(a, j).wait_send()
            for j in range(3):
                lvl2(a, j).wait_send()

    return start, middle, total, finish


def _hosted_reduce_operands(g16, const_spec):
    n = len(g16)
    shard = [g.shape[1:] for g in g16]
    return ([pl.BlockSpec(memory_space=pl.ANY)] * (2 * n),
            tuple(const_spec(s) for s in shard),
            tuple(jax.ShapeDtypeStruct(s, F32) for s in shard),
            [pltpu.VMEM((3,) + s, BF16) for s in shard] + [pltpu.VMEM((4,) + s, BF16) for s in shard]
            + [pltpu.VMEM((3,) + s, BF16) for s in shard] + [pltpu.VMEM((3,) + s, BF16) for s in shard]
            + [pltpu.VMEM(s, F32) for s in shard]
            + [pltpu.SemaphoreType.DMA((7 * n,)), pltpu.SemaphoreType.DMA((7 * n,)), pltpu.SemaphoreType.DMA((4 * n,))])


def _in_proj(x2, g_pre, w_in, tm):
    t = x2.shape[0]

    def body(x_ref, g_ref, w_ref, u_ref, zs_ref, q_ref, k_ref, v_ref, za_ref):
        xv = x_ref[...]
        r = lax.rsqrt(jnp.mean(xv * xv, axis=-1, keepdims=True) + EPS)
        hn = xv * r * g_ref[...]
        proj = _mm_nt(hn, w_ref[...])
        u_ref[...] = proj[:, 0:512]
        zs_ref[...] = proj[:, 512:1024]
        q_ref[...] = proj[:, 1024:1536].astype(BF16)
        k_ref[...] = proj[:, 1536:1664].astype(BF16)
        v_ref[...] = proj[:, 1664:1792].astype(BF16)
        za_ref[...] = proj[:, 1792:2304]

    row = lambda w: pl.BlockSpec((tm, w), lambda i: (i, 0))
    return pl.pallas_call(
        body, name="in_proj", grid=(t // tm,),
        in_specs=[row(D_MODEL), _const_spec((1, D_MODEL)), _const_spec((D_IN, D_MODEL))],
        out_specs=(row(512), row(512), row(512), row(128), row(128), row(512)),
        out_shape=(jax.ShapeDtypeStruct((t, 512), F32),
                   jax.ShapeDtypeStruct((t, 512), F32), jax.ShapeDtypeStruct((t, 512), BF16),
                   jax.ShapeDtypeStruct((t, 128), BF16), jax.ShapeDtypeStruct((t, 128), BF16),
                   jax.ShapeDtypeStruct((t, 512), F32)),
        compiler_params=_tc_params(("arbitrary",)),
    )(x2, g_pre, w_in)


def _discretise(lr, li, ls):
    step = jnp.exp(ls)
    mag = jnp.exp(lr * step)
    ar = mag * jnp.cos(li * step)
    ai = mag * jnp.sin(li * step)
    den = lr * lr + li * li
    cr = ((ar - 1.0) * lr + ai * li) / den
    ci = (ai * lr - (ar - 1.0) * li) / den
    return step, ar, ai, den, cr, ci


def _per_channel(v):
    return jnp.broadcast_to(v[:, None, :], (SSM_G, SSM_P, SSM_N)).reshape(SSM_G * SSM_P, SSM_N)


def _tile_masks():
    r = lax.broadcasted_iota(jnp.int32, (CH_T, ST_T), 0) // SSM_P
    l = lax.broadcasted_iota(jnp.int32, (CH_T, ST_T), 1) // SSM_N
    lt = lax.broadcasted_iota(jnp.int32, (ST_T, CH_T), 0) // SSM_N
    rt = lax.broadcasted_iota(jnp.int32, (ST_T, CH_T), 1) // SSM_P
    rep = lax.broadcasted_iota(jnp.int32, (SSM_N, ST_T), 0) == lax.broadcasted_iota(jnp.int32, (SSM_N, ST_T), 1) % SSM_N
    rep_t = lax.broadcasted_iota(jnp.int32, (ST_T, SSM_N), 0) % SSM_N == lax.broadcasted_iota(jnp.int32, (ST_T, SSM_N), 1)
    return r == l, lt == rt, rep, rep_t


def _ssm_prep(lam_re, lam_im, log_step, b_re, b_im, c_re, c_im, seg):
    def body(lr_ref, li_ref, ls_ref, br_ref, bi_ref, cre_ref, cim_ref, lrr_ref, lir_ref, lsr_ref,
             ar_ref, ai_ref, pr_ref, pi_ref, bcat_ref, bcat_t_ref, ccat_ref, ccat_t_ref):
        _, _, _, _, cr, ci = _discretise(lr_ref[...], li_ref[...], ls_ref[...])
        cr, ci = _per_channel(cr), _per_channel(ci)
        br, bi = br_ref[...], bi_ref[...]
        bb_re = cr * br - ci * bi
        bb_im = cr * bi + ci * br
        same, same_t, rep, rep_t = _tile_masks()
        rep, rep_t = rep.astype(BF16), rep_t.astype(BF16)
        for j in range(N_GT):
            rows = slice(j * CH_T, (j + 1) * CH_T)
            for wide, tall, parts in ((bcat_ref, bcat_t_ref, (bb_re[rows], bb_im[rows])),
                                      (ccat_t_ref, ccat_ref, (cre_ref[rows, :], -cim_ref[rows, :]))):
                for k, part in enumerate(parts):
                    p16 = part.astype(BF16)
                    wide[j, :, k * ST_T:(k + 1) * ST_T] = jnp.where(same, _mm(p16, rep), 0.0).astype(BF16)
                    tall[j, k * ST_T:(k + 1) * ST_T, :] = jnp.where(same_t, _mm_nt(rep_t, p16), 0.0).astype(BF16)
        stepr = jnp.exp(lsr_ref[...])
        k = (lax.broadcasted_iota(jnp.int32, (8, N_STATE), 0) + 1).astype(F32)
        magk = jnp.exp(k * (lrr_ref[...] * stepr))
        ang = k * (lir_ref[...] * stepr)
        pr_ref[0:8, :] = magk * jnp.cos(ang)
        pi_ref[0:8, :] = magk * jnp.sin(ang)
        n = 8
        while n < seg:
            tr, ti = pr_ref[n - 1:n, :], pi_ref[n - 1:n, :]
            xr, xi = pr_ref[0:n, :], pi_ref[0:n, :]
            pr_ref[n:2 * n, :] = xr * tr - xi * ti
            pi_ref[n:2 * n, :] = xr * ti + xi * tr
            n *= 2
        ar_ref[...] = pr_ref[0:1, :]
        ai_ref[...] = pi_ref[0:1, :]

    row = jax.ShapeDtypeStruct((1, N_STATE), F32)
    pw = jax.ShapeDtypeStruct((seg, N_STATE), F32)
    wide = jax.ShapeDtypeStruct((N_GT, CH_T, 2 * ST_T), BF16)
    tall = jax.ShapeDtypeStruct((N_GT, 2 * ST_T, CH_T), BF16)
    vm = pl.BlockSpec(memory_space=pltpu.VMEM)
    step_row = jnp.broadcast_to(log_step, (SSM_G, SSM_N)).reshape(1, N_STATE)
    return pl.pallas_call(
        body, name="ssm_prep", out_shape=(row, row, pw, pw, wide, tall, tall, wide),
        in_specs=[vm] * 10, out_specs=(vm,) * 8,
    )(lam_re, lam_im, log_step, b_re, b_im, c_re, c_im, lam_re.reshape(1, N_STATE), lam_im.reshape(1, N_STATE),
      step_row)


def _seg_rows(t):
    if isinstance(t, int):
        return pl.ds(t * N_SEG, N_SEG)
    return pl.ds(pl.multiple_of(t * N_SEG, N_SEG), N_SEG)


def _scan_forward(xs, a_re, a_im, pw_re, pw_im, cs, seg):
    are = jnp.broadcast_to(a_re, (N_SEG, ST_T))
    aim = jnp.broadcast_to(a_im, (N_SEG, ST_T))

    def steps(k, carry):
        xr, xi = carry
        for j in range(SCAN_UNROLL):
            r = pl.multiple_of((k * SCAN_UNROLL + j) * N_SEG, N_SEG)
            nr = are * xr - aim * xi + xs[pl.ds(r, N_SEG), 0:ST_T]
            ni = are * xi + aim * xr + xs[pl.ds(r, N_SEG), ST_T:2 * ST_T]
            xs[pl.ds(r, N_SEG), 0:ST_T] = nr
            xs[pl.ds(r, N_SEG), ST_T:2 * ST_T] = ni
            xr, xi = nr, ni
        return xr, xi

    zero = jnp.zeros((N_SEG, ST_T), F32)
    fr, fi = lax.fori_loop(0, seg // SCAN_UNROLL, steps, (zero, zero))
    sr, si = pw_re[seg - 1:seg, :], pw_im[seg - 1:seg, :]
    cr = jnp.zeros((1, ST_T), F32)
    ci = jnp.zeros((1, ST_T), F32)
    cs[0:1, :] = cr
    cs[8:9, :] = ci
    for s in range(1, N_SEG):
        ncr = sr * cr - si * ci + fr[s - 1:s, :]
        nci = sr * ci + si * cr + fi[s - 1:s, :]
        cr, ci = ncr, nci
        cs[s:s + 1, :] = cr
        cs[8 + s:9 + s, :] = ci
    car, cai = cs[0:8, :], cs[8:16, :]

    def fix(t, _):
        r = pl.multiple_of(t * N_SEG, N_SEG)
        pr, pi = pw_re[pl.ds(t, 1), :], pw_im[pl.ds(t, 1), :]
        xs[pl.ds(r, N_SEG), 0:ST_T] = xs[pl.ds(r, N_SEG), 0:ST_T] + (pr * car - pi * cai)
        xs[pl.ds(r, N_SEG), ST_T:2 * ST_T] = xs[pl.ds(r, N_SEG), ST_T:2 * ST_T] + (pr * cai + pi * car)
        return 0

    lax.fori_loop(0, seg, fix, 0, unroll=SCAN_UNROLL)


def _interleave(src, dst, seg):
    for s in range(N_SEG):
        dst[pl.ds(s, seg, stride=N_SEG), :] = src[s]


def _deinterleave(src, seg, s):
    return src[pl.ds(s, seg, stride=N_SEG), :]


def _ssm_forward(u, bcat, ccat, a_re, a_im, pw_re, pw_im, d_row, late, seg):
    bl = u.shape[0]
    rows = N_SEG * seg
    n = len(late)
    steps = bl * N_GT

    def body(*refs):
        u_ref, b_ref, c_ref, ar_ref, ai_ref, pr_ref, pi_ref, d_ref = refs[:8]
        late_r = refs[8:8 + n]
        y_ref, xs_ref, cs_ref = refs[8 + n:11 + n]
        gath, cast = refs[11 + n:11 + 2 * n], refs[11 + 2 * n:11 + 3 * n]
        send_sems, recv_sems, local_sems, ui, yi = refs[11 + 3 * n:]
        step = pl.program_id(0) * N_GT + pl.program_id(1)
        start, relay, finish = _gather_phases(late_r, gath, cast, send_sems, recv_sems, local_sems)
        pl.when(step == 0)(start)
        _interleave(u_ref.at[0], ui, seg)
        u = ui[...]
        xs, cs = xs_ref.at[0, 0], cs_ref.at[0, 0]
        xs[...] = _mm(u, b_ref[0])
        _scan_forward(xs, ar_ref[...], ai_ref[...], pr_ref, pi_ref, cs, seg)
        yi[...] = _mm(xs[...], c_ref[0]) + d_ref[...] * u
        for s in range(N_SEG):
            y_ref[0, s] = _deinterleave(yi, seg, s)
        pl.when(step == steps // 2)(relay)
        pl.when(step == steps - 1)(finish)

    state = lambda r, c: pl.BlockSpec((1, 1, r, c), lambda b, j: (b, j, 0, 0))
    act = pl.BlockSpec((1, N_SEG, seg, CH_T), lambda b, j: (b, 0, 0, j))
    g_specs, g_shapes, g_scratch = _gather_operands(late)
    res = pl.pallas_call(
        body, name="ssm_forward", grid=(bl, N_GT),
        in_specs=[act,
                  pl.BlockSpec((1, CH_T, 2 * ST_T), lambda b, j: (j, 0, 0)),
                  pl.BlockSpec((1, 2 * ST_T, CH_T), lambda b, j: (j, 0, 0)),
                  pl.BlockSpec((1, ST_T), lambda b, j: (0, j)), pl.BlockSpec((1, ST_T), lambda b, j: (0, j)),
                  pl.BlockSpec((seg, ST_T), lambda b, j: (0, j)), pl.BlockSpec((seg, ST_T), lambda b, j: (0, j)),
                  pl.BlockSpec((1, CH_T), lambda b, j: (0, j))]
        + [pl.BlockSpec(s.shape, lambda b, j: (0, 0)) for s in late],
        out_specs=(act, state(rows, 2 * ST_T), state(16, ST_T)) + g_specs,
        out_shape=(jax.ShapeDtypeStruct((bl, N_SEG, seg, D_SSM), F32),
                   jax.ShapeDtypeStruct((bl, N_GT, rows, 2 * ST_T), F32),
                   jax.ShapeDtypeStruct((bl, N_GT, 16, ST_T), F32)) + g_shapes,
        scratch_shapes=g_scratch + [pltpu.VMEM((rows, CH_T), F32), pltpu.VMEM((rows, CH_T), F32)],
        compiler_params=_tc_params(("arbitrary", "arbitrary")),
    )(u, bcat, ccat, a_re, a_im, pw_re, pw_im, d_row, *late)
    return res[:3], list(res[3:])


def _ssm_backward(u, dy, states, carries, bcat_t, ccat_t, a_re, a_im, pw_re, pw_im, d_row, late16, late32, seg):
    bl = u.shape[0]
    rows = N_SEG * seg
    n = len(late16)
    grid_steps = N_GT * bl

    def body(*refs):
        u_ref, dy_ref, xs_ref, cs_ref, bt_ref, ct_ref, ar_ref, ai_ref, pr_ref, pi_ref, d_ref = refs[:11]
        g16_r, g32_r = refs[11:11 + n], refs[11 + n:11 + 2 * n]
        du_ref, db_ref, dc_ref, dar_ref, dai_ref, dd_ref = refs[11 + 2 * n:17 + 2 * n]
        red = refs[17 + 2 * n:17 + 3 * n]
        own16, recv1, send2, recv2, own32 = (refs[17 + 3 * n + k * n:17 + 3 * n + (k + 1) * n] for k in range(5))
        s_send, s_recv, s_local, ls, cl, ui, dyi, dui = refs[17 + 8 * n:]
        b = pl.program_id(1)
        step = pl.program_id(0) * bl + b
        start, middle, total, finish = _hosted_reduce_phases(g16_r, g32_r, red, own16, recv1, send2, recv2, own32,
                                                             s_send, s_recv, s_local)
        pl.when(step == 0)(start)
        pl.when(step == grid_steps // 4)(middle)
        pl.when(step == (grid_steps * 3) // 4)(total)
        pl.when(step == grid_steps - 1)(finish)
        _interleave(u_ref.at[0], ui, seg)
        _interleave(dy_ref.at[0], dyi, seg)
        u = ui[...]
        dy = dyi[...]
        xs, cs = xs_ref.at[0, 0], cs_ref.at[0, 0]
        ls[...] = _mm(dy, ct_ref[0])
        are = jnp.broadcast_to(ar_ref[...], (N_SEG, ST_T))
        aim = jnp.broadcast_to(ai_ref[...], (N_SEG, ST_T))

        def steps(k, carry):
            lr, li = carry
            for j in range(SCAN_UNROLL):
                r = pl.multiple_of((seg - 1 - (k * SCAN_UNROLL + j)) * N_SEG, N_SEG)
                nr = are * lr + aim * li + ls[pl.ds(r, N_SEG), 0:ST_T]
                ni = are * li - aim * lr + ls[pl.ds(r, N_SEG), ST_T:2 * ST_T]
                ls[pl.ds(r, N_SEG), 0:ST_T] = nr
                ls[pl.ds(r, N_SEG), ST_T:2 * ST_T] = ni
                lr, li = nr, ni
            return lr, li

        zero = jnp.zeros((N_SEG, ST_T), F32)
        fr, fi = lax.fori_loop(0, seg // SCAN_UNROLL, steps, (zero, zero))
        sr, si = pr_ref[seg - 1:seg, :], pi_ref[seg - 1:seg, :]
        cr = jnp.zeros((1, ST_T), F32)
        ci = jnp.zeros((1, ST_T), F32)
        cl[7:8, :] = cr
        cl[15:16, :] = ci
        for s in range(N_SEG - 2, -1, -1):
            ncr = sr * cr + si * ci + fr[s + 1:s + 2, :]
            nci = sr * ci - si * cr + fi[s + 1:s + 2, :]
            cr, ci = ncr, nci
            cl[s:s + 1, :] = cr
            cl[8 + s:9 + s, :] = ci
        clr, cli = cl[0:8, :], cl[8:16, :]

        def fix_rows(rows, t, xpr, xpi, acc):
            dr, di = acc
            pr, pi = pr_ref[pl.ds(seg - 1 - t, 1), :], pi_ref[pl.ds(seg - 1 - t, 1), :]
            lr = ls[rows, 0:ST_T] + (pr * clr + pi * cli)
            li = ls[rows, ST_T:2 * ST_T] + (pr * cli - pi * clr)
            ls[rows, 0:ST_T] = lr
            ls[rows, ST_T:2 * ST_T] = li
            return dr + (lr * xpr + li * xpi), di + (li * xpr - lr * xpi)

        def fix_at(t, acc):
            prev = _seg_rows(t - 1)
            return fix_rows(_seg_rows(t), t, xs[prev, 0:ST_T], xs[prev, ST_T:2 * ST_T], acc)

        def fix(k, acc):
            for j in range(SCAN_UNROLL):
                acc = fix_at(k * SCAN_UNROLL + j, acc)
            return acc

        acc = fix_rows(pl.ds(0, N_SEG), 0, cs[0:8, :], cs[8:16, :], (zero, zero))
        for t in range(1, SCAN_UNROLL):
            acc = fix_at(t, acc)
        dr, di = lax.fori_loop(1, seg // SCAN_UNROLL, fix, acc)
        dar = jnp.sum(dr, axis=0, keepdims=True)
        dai = jnp.sum(di, axis=0, keepdims=True)
        lall = ls[...]
        dui[...] = _mm(lall, bt_ref[0]) + d_ref[...] * dy
        for s in range(N_SEG):
            du_ref[0, s] = _deinterleave(dui, seg, s).astype(BF16)
        dbp = _mm_tn(u, lall)
        dcp = _mm_tn(dy, xs[...])
        ddp = jnp.sum(dy * u, axis=0, keepdims=True)

        @pl.when(b == 0)
        def _():
            db_ref[0] = dbp
            dc_ref[0] = dcp
            dar_ref[...] = dar
            dai_ref[...] = dai
            dd_ref[...] = ddp

        @pl.when(b != 0)
        def _():
            db_ref[0] += dbp
            dc_ref[0] += dcp
            dar_ref[...] += dar
            dai_ref[...] += dai
            dd_ref[...] += ddp

    tile3 = lambda r, c: pl.BlockSpec((1, r, c), lambda j, b: (j, 0, 0))
    lane = lambda r, c: pl.BlockSpec((r, c), lambda j, b: (0, j))
    act = pl.BlockSpec((1, N_SEG, seg, CH_T), lambda j, b: (b, 0, 0, j))
    state = lambda r, c: pl.BlockSpec((1, 1, r, c), lambda j, b: (b, j, 0, 0))
    r_in, r_out, r_shapes, r_scratch = _hosted_reduce_operands(late16, lambda s: pl.BlockSpec(s, lambda j, b: (0, 0)))
    res = pl.pallas_call(
        body, name="ssm_backward", grid=(N_GT, bl),
        in_specs=[act, act, state(rows, 2 * ST_T), state(16, ST_T), tile3(2 * ST_T, CH_T), tile3(CH_T, 2 * ST_T),
                  lane(1, ST_T), lane(1, ST_T), lane(seg, ST_T), lane(seg, ST_T), lane(1, CH_T)] + r_in,
        out_specs=(act, tile3(CH_T, 2 * ST_T), tile3(CH_T, 2 * ST_T), lane(1, ST_T), lane(1, ST_T), lane(1, CH_T))
        + r_out,
        out_shape=(jax.ShapeDtypeStruct((bl, N_SEG, seg, D_SSM), BF16),
                   jax.ShapeDtypeStruct((N_GT, CH_T, 2 * ST_T), F32), jax.ShapeDtypeStruct((N_GT, CH_T, 2 * ST_T), F32),
                   jax.ShapeDtypeStruct((1, N_STATE), F32), jax.ShapeDtypeStruct((1, N_STATE), F32),
                   jax.ShapeDtypeStruct((1, D_SSM), F32)) + r_shapes,
        scratch_shapes=r_scratch + [pltpu.VMEM((rows, 2 * ST_T), F32), pltpu.VMEM((16, ST_T), F32)]
        + [pltpu.VMEM((rows, CH_T), F32)] * 3,
        compiler_params=_tc_params(("arbitrary", "arbitrary")),
    )(u, dy, states, carries, bcat_t, ccat_t, a_re, a_im, pw_re, pw_im, d_row, *late16, *late32)
    return res[:6], list(res[6:])


def _ssm_param_grads(lam_re, lam_im, log_step, b_re, b_im, da_re, da_im, d_bcat, d_ccat_t):
    def body(lr_ref, li_ref, ls_ref, br_ref, bi_ref, gar_ref, gai_ref, gbcat_ref, gccat_ref,
             dlr_ref, dli_ref, dls_ref, dbr_ref, dbi_ref, dcr_ref, dci_ref, gbr_s, gbi_s):
        same, _, _, rep_t = _tile_masks()
        rep_t = rep_t.astype(F32)
        for j in range(N_GT):
            rows = slice(j * CH_T, (j + 1) * CH_T)
            for src, dsts in ((gbcat_ref, (gbr_s, gbi_s)), (gccat_ref, (dcr_ref, dci_ref))):
                for k, dst in enumerate(dsts):
                    blk = jnp.where(same, src[j, :, k * ST_T:(k + 1) * ST_T], 0.0)
                    dst[rows, :] = jnp.dot(blk, rep_t, precision=lax.Precision.HIGHEST, preferred_element_type=F32)
        dci_ref[...] = -dci_ref[...]
        lr, li = lr_ref[...], li_ref[...]
        step, ar, ai, den, cr, ci = _discretise(lr, li, ls_ref[...])
        crb, cib = _per_channel(cr), _per_channel(ci)
        br, bi = br_ref[...], bi_ref[...]
        gbr, gbi = gbr_s[...], gbi_s[...]
        dbr_ref[...] = crb * gbr + cib * gbi
        dbi_ref[...] = crb * gbi - cib * gbr
        over_channels = lambda t: jnp.sum(t.reshape(SSM_G, SSM_P, SSM_N), axis=1)
        gcr = over_channels(br * gbr + bi * gbi)
        gci = over_channels(br * gbi - bi * gbr)
        ilr, ili = lr / den, -li / den
        gar = gar_ref[...] + (ilr * gcr + ili * gci)
        gai = gai_ref[...] + (ilr * gci - ili * gcr)
        qr, qi = cr * ilr - ci * ili, cr * ili + ci * ilr
        glr = -(qr * gcr + qi * gci)
        gli = -(qr * gci - qi * gcr)
        gwr = ar * gar + ai * gai
        gwi = ar * gai - ai * gar
        dlr_ref[...] = glr + step * gwr
        dli_ref[...] = gli + step * gwi
        dls_ref[...] = jnp.sum(lr * gwr + li * gwi, axis=-1, keepdims=True) * step

    lam = jax.ShapeDtypeStruct((SSM_G, SSM_N), F32)
    mat = jax.ShapeDtypeStruct((SSM_G * SSM_P, SSM_N), F32)
    vm = pl.BlockSpec(memory_space=pltpu.VMEM)
    return pl.pallas_call(
        body, name="ssm_param_grads", out_shape=(lam, lam, jax.ShapeDtypeStruct((SSM_G, 1), F32), mat, mat, mat, mat),
        in_specs=[vm] * 9, out_specs=(vm,) * 7,
        scratch_shapes=[pltpu.VMEM((SSM_G * SSM_P, SSM_N), F32), pltpu.VMEM((SSM_G * SSM_P, SSM_N), F32)],
    )(lam_re, lam_im, log_step, b_re, b_im, da_re, da_im, d_bcat, d_ccat_t)


ROWS4 = Q_PER_KV * ATT_BLOCK


def _att_dist_mask(first_block):
    qi = lax.broadcasted_iota(jnp.int32, (ROWS4, 2 * ATT_BLOCK), 0) & (ATT_BLOCK - 1)
    si = lax.broadcasted_iota(jnp.int32, (ROWS4, 2 * ATT_BLOCK), 1)
    dist = qi + ATT_BLOCK - si
    valid = (dist >= 0) & (dist < ATT_BLOCK) & ((si >= ATT_BLOCK) | jnp.logical_not(first_block))
    return dist.astype(F32), valid


def _stack_heads(x, kv):
    return jnp.concatenate([x[:, (kv * Q_PER_KV + g) * HEAD_DIM:(kv * Q_PER_KV + g + 1) * HEAD_DIM]
                            for g in range(Q_PER_KV)], axis=0)


def _stack_cols(x, kv):
    return jnp.concatenate([x[:, kv * Q_PER_KV + g:kv * Q_PER_KV + g + 1] for g in range(Q_PER_KV)], axis=0)


def _per_head_col(vals):
    return jnp.concatenate([jnp.full((ATT_BLOCK, 1), v, F32) for v in vals], axis=0)


def _attn_forward(q, k, v, sinks, bl, nb):
    t = q.shape[0]

    def body(sink_ref, q_ref, kp_ref, kc_ref, vp_ref, vc_ref, o_ref, lse_ref):
        i = pl.program_id(1)
        dist4, valid4 = _att_dist_mask(i == 0)
        dist, valid = dist4[0:ATT_BLOCK, :], valid4[0:ATT_BLOCK, :]
        kk = jnp.concatenate([kp_ref[...], kc_ref[...]], axis=0)
        vv = jnp.concatenate([vp_ref[...], vc_ref[...]], axis=0)
        qv = q_ref[...]
        for h in range(N_HEADS):
            kv = h // Q_PER_KV
            slope = 2.0 ** (-(h + 1))
            qh = qv[:, h * HEAD_DIM:(h + 1) * HEAD_DIM]
            kh = kk[:, kv * HEAD_DIM:(kv + 1) * HEAD_DIM]
            vh = vv[:, kv * HEAD_DIM:(kv + 1) * HEAD_DIM]
            s = _mm_nt(qh, kh) * ATT_SCALE - slope * dist
            s = jnp.where(valid, s, NEG_BIG)
            sink = sink_ref[h]
            m = jnp.maximum(jnp.max(s, axis=-1, keepdims=True), sink)
            e = jnp.exp(s - m)
            den = jnp.sum(e, axis=-1, keepdims=True) + jnp.exp(sink - m)
            o_ref[:, h * HEAD_DIM:(h + 1) * HEAD_DIM] = _mm(e, vh) * (1.0 / den)
            lse_ref[:, h:h + 1] = m + jnp.log(den)

    cur = lambda w: pl.BlockSpec((ATT_BLOCK, w), lambda b, i: (b * nb + i, 0))
    prev = lambda w: pl.BlockSpec((ATT_BLOCK, w), lambda b, i: (b * nb + jnp.maximum(i - 1, 0), 0))
    return pl.pallas_call(
        body, name="attn_forward", grid=(bl, nb),
        in_specs=[pl.BlockSpec(memory_space=pltpu.SMEM), cur(512), prev(128), cur(128), prev(128), cur(128)],
        out_specs=(cur(512), cur(N_HEADS)),
        out_shape=(jax.ShapeDtypeStruct((t, D_ATTN), F32), jax.ShapeDtypeStruct((t, N_HEADS), F32)),
        compiler_params=_tc_params(("arbitrary", "arbitrary")),
    )(sinks, q, k, k, v, v)


def _attn_backward(q, k, v, o, do, lse, sinks, bl, nb):
    t = q.shape[0]

    def body(sink_ref, qc_ref, kp_ref, kc_ref, vp_ref, vc_ref, oc_ref, doc_ref, lc_ref,
             dq_ref, dk_ref, dv_ref, ds_ref, dk_carry, dv_carry):
        b, i = pl.program_id(0), pl.program_id(1)
        live = i < nb

        @pl.when(i == 0)
        def _():
            dk_carry[...] = jnp.zeros((ATT_BLOCK, KV_HEADS * HEAD_DIM), F32)
            dv_carry[...] = jnp.zeros((ATT_BLOCK, KV_HEADS * HEAD_DIM), F32)

        dist, valid = _att_dist_mask(i == 0)
        valid = valid & live
        kk = jnp.concatenate([kp_ref[...], kc_ref[...]], axis=0)
        vv = jnp.concatenate([vp_ref[...], vc_ref[...]], axis=0)
        qc, oc, doc, lc = qc_ref[...], oc_ref[...], doc_ref[...], lc_ref[...]
        dsink_cols, dq_parts, dk_t, dv_t = [], [], [], []
        for kv in range(KV_HEADS):
            heads = range(kv * Q_PER_KV, (kv + 1) * Q_PER_KV)
            cols = slice(kv * HEAD_DIM, (kv + 1) * HEAD_DIM)
            kh, vh = kk[:, cols], vv[:, cols]
            slope = _per_head_col([2.0 ** (-(h + 1)) for h in heads])
            sink = _per_head_col([sink_ref[h] for h in heads])
            q4, do4 = _stack_heads(qc, kv), _stack_heads(doc, kv)
            delta = jnp.sum(do4 * _stack_heads(oc, kv), axis=-1, keepdims=True)
            lse4 = _stack_cols(lc, kv)
            s = _mm_nt(q4, kh) * ATT_SCALE - slope * dist
            p = jnp.where(valid, jnp.exp(s - lse4), 0.0)
            dsc = p * (_mm_nt(do4, vh) - delta)
            dq4 = _mm(dsc, kh) * ATT_SCALE
            dk_t.append(_mm_tn(q4, dsc) * ATT_SCALE)
            dv_t.append(_mm_tn(do4, p))
            dsink4 = jnp.where(live, jnp.exp(sink - lse4) * delta, 0.0)
            for g, h in enumerate(heads):
                rows = slice(g * ATT_BLOCK, (g + 1) * ATT_BLOCK)
                dq_parts.append((h, dq4[rows, :]))
                dsink_cols.append(-jnp.sum(dsink4[rows, :], axis=0, keepdims=True))
        dsink = jnp.concatenate(dsink_cols, axis=1)
        for out_ref, carry, parts in ((dk_ref, dk_carry, dk_t), (dv_ref, dv_carry, dv_t)):
            both = jnp.concatenate(parts, axis=0)
            out_ref[...] = (carry[...] + both[:, 0:ATT_BLOCK]).T
            carry[...] = both[:, ATT_BLOCK:]

        @pl.when(live)
        def _():
            for h, part in dq_parts:
                dq_ref[:, h * HEAD_DIM:(h + 1) * HEAD_DIM] = part

        @pl.when((b == 0) & (i == 0))
        def _():
            ds_ref[...] = dsink

        @pl.when((b != 0) | (i != 0))
        def _():
            ds_ref[...] += dsink

    cur_i = lambda i: jnp.minimum(i, nb - 1)
    cur = lambda w: pl.BlockSpec((ATT_BLOCK, w), lambda b, i: (b * nb + cur_i(i), 0))
    prev = lambda w: pl.BlockSpec((ATT_BLOCK, w), lambda b, i: (b * nb + jnp.maximum(cur_i(i) - 1, 0), 0))
    behind = lambda w: pl.BlockSpec((ATT_BLOCK, w), lambda b, i: (b * nb + jnp.maximum(i - 1, 0), 0))
    return pl.pallas_call(
        body, name="attn_backward", grid=(bl, nb + 1),
        in_specs=[pl.BlockSpec(memory_space=pltpu.SMEM), cur(512), prev(128), cur(128), prev(128), cur(128),
                  cur(512), cur(512), cur(N_HEADS)],
        out_specs=(cur(512), behind(128), behind(128), pl.BlockSpec((1, N_HEADS), lambda b, i: (0, 0))),
        out_shape=(jax.ShapeDtypeStruct((t, D_ATTN), F32), jax.ShapeDtypeStruct((t, 128), F32),
                   jax.ShapeDtypeStruct((t, 128), F32), jax.ShapeDtypeStruct((1, N_HEADS), F32)),
        scratch_shapes=[pltpu.VMEM((ATT_BLOCK, KV_HEADS * HEAD_DIM), F32), pltpu.VMEM((ATT_BLOCK, KV_HEADS * HEAD_DIM), F32)],
        compiler_params=_tc_params(("arbitrary", "arbitrary")),
    )(sinks, q, k, k, v, v, o, do, lse)


def _mix_forward_backward(x2, y2, z_ssm, attn, z_attn, p2, target2, w_glu, b_glu, w_out, g_post, w_gate, b_gate,
                          w_proj, tm):
    t = x2.shape[0]

    def body(x_ref, y_ref, zs_ref, at_ref, za_ref, p_ref, tg_ref,
             wglu_ref, bglu_ref, wout_ref, gpost_ref, wgate_ref, bgate_ref, wproj_ref,
             loss_ref, dh1_ref, dy_ref, dzs_ref, dat_ref, dza_ref,
             dwglu_ref, dbglu_ref, dwout_ref, dgpost_ref, dwgate_ref, dbgate_ref, dwproj_ref,
             dwout16_ref, dwgate16_ref, dwproj16_ref, dwglu16_ref):
        i = pl.program_id(0)
        gpost = gpost_ref[...]

        @pl.when(i == 0)
        def _():
            for ref in (dwglu_ref, dbglu_ref, dwout_ref, dgpost_ref, dwgate_ref, dbgate_ref, dwproj_ref, loss_ref):
                ref[...] = jnp.zeros(ref.shape, F32)

        def chain(rows):
            y = y_ref[rows, :]
            u3 = GELU_C * (y + GELU_K * y * y * y)
            th = jnp.tanh(u3)
            gl = 0.5 * y * (1.0 + th)
            a = _mm(gl, wglu_ref[...]) + bglu_ref[...]
            sa = _sigmoid(a)
            glu = gl * sa
            zs = zs_ref[rows, :]
            sgs = _sigmoid(zs)
            ssm_out = glu * (zs * sgs)
            za = za_ref[rows, :]
            sga = _sigmoid(za)
            at = at_ref[rows, :]
            attn_out = at * (za * sga)
            cat = jnp.concatenate([ssm_out, attn_out], axis=-1).astype(BF16)
            mixed = _mm(cat, wout_ref[...])
            r2 = lax.rsqrt(jnp.mean(mixed * mixed, axis=-1, keepdims=True) + EPS)
            nhat = mixed * r2
            h1 = x_ref[rows, :] + nhat * gpost
            gate = _sigmoid(_mm(h1, wgate_ref[...]) + bgate_ref[...])
            pv = p_ref[rows, :]
            pp = _mm(pv, wproj_ref[...])
            h2 = h1 + gate * pp
            err = h2 - tg_ref[rows, :]
            loss_part = jnp.sum(jnp.sum(err * err, axis=-1, keepdims=True), axis=0, keepdims=True) * (0.5 / D_MODEL)
            dh2 = err * (1.0 / D_MODEL)
            dgp = dh2 * pp * gate * (1.0 - gate)
            dpp = dh2 * gate
            dh1 = dh2 + _mm_nt(dgp, wgate_ref[...])
            dh1_ref[rows, :] = dh1
            dnhat = dh1 * gpost
            dmixed = r2 * (dnhat - nhat * jnp.mean(dnhat * nhat, axis=-1, keepdims=True))
            dcat = _mm_nt(dmixed, wout_ref[...])
            dso, dao = dcat[:, 0:D_SSM], dcat[:, D_SSM:]
            dat_ref[rows, :] = dao * (za * sga)
            dza_ref[rows, :] = (dao * at * (sga * (1.0 + za * (1.0 - sga)))).astype(BF16)
            dzs_ref[rows, :] = (dso * glu * (sgs * (1.0 + zs * (1.0 - sgs)))).astype(BF16)
            dglu = dso * (zs * sgs)
            da = dglu * gl * sa * (1.0 - sa)
            dgl = dglu * sa + _mm_nt(da, wglu_ref[...])
            dgelu = 0.5 * (1.0 + th) + 0.5 * y * (1.0 - th * th) * (GELU_C * (1.0 + 3.0 * GELU_K * y * y))
            dy_ref[rows, :] = dgl * dgelu
            return dict(gl=gl.astype(BF16), da=da.astype(BF16), cat=cat, dmixed=dmixed.astype(BF16),
                        h1=h1.astype(BF16), dgp=dgp.astype(BF16), pv=pv.astype(BF16), dpp=dpp.astype(BF16),
                        dbglu=jnp.sum(da, axis=0, keepdims=True), dgpost=jnp.sum(dh1 * nhat, axis=0, keepdims=True),
                        dbgate=jnp.sum(dgp, axis=0, keepdims=True), loss=loss_part)

        groups = [chain(slice(k * (tm // MIX_GROUPS), (k + 1) * (tm // MIX_GROUPS))) for k in range(MIX_GROUPS)]
        rows_of = lambda name: jnp.concatenate([g[name] for g in groups], axis=0)
        total = lambda name: sum(g[name] for g in groups)
        parts = (
            (dwglu_ref, _mm_tn(rows_of("gl"), rows_of("da"))), (dbglu_ref, total("dbglu")),
            (dwout_ref, _mm_tn(rows_of("cat"), rows_of("dmixed"))), (dgpost_ref, total("dgpost")),
            (dwgate_ref, _mm_tn(rows_of("h1"), rows_of("dgp"))), (dbgate_ref, total("dbgate")),
            (dwproj_ref, _mm_tn(rows_of("pv"), rows_of("dpp"))), (loss_ref, total("loss")),
        )

        for ref, val in parts:
            ref[...] += val

        @pl.when(i == t // tm - 1)
        def _():
            for ref16, ref in ((dwout16_ref, dwout_ref), (dwgate16_ref, dwgate_ref), (dwproj16_ref, dwproj_ref),
                               (dwglu16_ref, dwglu_ref)):
                def to16(r, ref16=ref16, ref=ref):
                    ref16[r, :] = ref[r, :].astype(BF16)

                _row_chunks(ref.shape[0], to16)

    row = lambda w: pl.BlockSpec((tm, w), lambda i: (i, 0))
    acc = lambda r, c, dt=F32: (_const_spec((r, c)), jax.ShapeDtypeStruct((r, c), dt))
    accs = [acc(D_SSM, D_SSM), acc(1, D_SSM), acc(D_MODEL, D_MODEL), acc(1, D_MODEL), acc(D_MODEL, D_MODEL),
            acc(1, D_MODEL), acc(D_PLE, D_MODEL),
            acc(D_MODEL, D_MODEL, BF16), acc(D_MODEL, D_MODEL, BF16), acc(D_PLE, D_MODEL, BF16), acc(D_SSM, D_SSM, BF16)]
    return pl.pallas_call(
        body, name="mix_forward_backward", grid=(t // tm,),
        in_specs=[row(D_MODEL), row(512), row(512), row(512), row(512), row(D_PLE), row(D_MODEL),
                  _const_spec((D_SSM, D_SSM)), _const_spec((1, D_SSM)), _const_spec((D_MODEL, D_MODEL)),
                  _const_spec((1, D_MODEL)), _const_spec((D_MODEL, D_MODEL)), _const_spec((1, D_MODEL)),
                  _const_spec((D_PLE, D_MODEL))],
        out_specs=(_const_spec((1, 1)), row(D_MODEL), row(512), row(512), row(512), row(512))
        + tuple(a[0] for a in accs),
        out_shape=(jax.ShapeDtypeStruct((1, 1), F32), jax.ShapeDtypeStruct((t, D_MODEL), F32),
                   jax.ShapeDtypeStruct((t, 512), F32),
                   jax.ShapeDtypeStruct((t, 512), BF16), jax.ShapeDtypeStruct((t, 512), F32),
                   jax.ShapeDtypeStruct((t, 512), BF16)) + tuple(a[1] for a in accs),
        compiler_params=_tc_params(("arbitrary",)),
    )(x2, y2, z_ssm, attn, z_attn, p2, target2, w_glu, b_glu, w_out, g_post, w_gate, b_gate, w_proj)


def _in_backward(x2, dh1, du, dz_ssm, dq, dk, dv, dz_attn, g_pre, w_in, tm):
    t = x2.shape[0]

    def body(x_ref, dh1_ref, du_ref, dzs_ref, dq_ref, dk_ref, dv_ref, dza_ref, g_ref, w_ref,
             gx_ref, dw_ref, dg_ref, dw16_ref):
        i = pl.program_id(0)
        xv = x_ref[...]
        r = lax.rsqrt(jnp.mean(xv * xv, axis=-1, keepdims=True) + EPS)
        xhat = xv * r
        g = g_ref[...]
        hn = (xhat * g).astype(BF16)
        dproj = jnp.concatenate([du_ref[...].astype(BF16), dzs_ref[...].astype(BF16), dq_ref[...].astype(BF16),
                                 dk_ref[...].astype(BF16), dv_ref[...].astype(BF16), dza_ref[...].astype(BF16)],
                                axis=-1)
        dhn = _mm(dproj, w_ref[...])
        dxhat = dhn * g
        gx_ref[...] = dh1_ref[...] + r * (dxhat - xhat * jnp.mean(dxhat * xhat, axis=-1, keepdims=True))
        @pl.when(i == 0)
        def _():
            dw_ref[...] = jnp.zeros((D_IN, D_MODEL), F32)
            dg_ref[...] = jnp.zeros((1, D_MODEL), F32)

        dw_ref[...] += _mm_tn(dproj, hn)
        dg_ref[...] += jnp.sum(dhn * xhat, axis=0, keepdims=True)

        @pl.when(i == t // tm - 1)
        def _():
            def to16(r):
                dw16_ref[r, :] = dw_ref[r, :].astype(BF16)

            _row_chunks(D_IN, to16)

    row = lambda w: pl.BlockSpec((tm, w), lambda i: (i, 0))
    return pl.pallas_call(
        body, name="in_backward", grid=(t // tm,),
        in_specs=[row(D_MODEL), row(D_MODEL), row(512), row(512), row(512), row(128), row(128), row(512),
                  _const_spec((1, D_MODEL)), _const_spec((D_IN, D_MODEL))],
        out_specs=(row(D_MODEL), _const_spec((D_IN, D_MODEL)), _const_spec((1, D_MODEL)),
                   _const_spec((D_IN, D_MODEL))),
        out_shape=(jax.ShapeDtypeStruct((t, D_MODEL), F32), jax.ShapeDtypeStruct((D_IN, D_MODEL), F32),
                   jax.ShapeDtypeStruct((1, D_MODEL), F32), jax.ShapeDtypeStruct((D_IN, D_MODEL), BF16)),
        compiler_params=_tc_params(("arbitrary",)),
    )(x2, dh1, du, dz_ssm, dq, dk, dv, dz_attn, g_pre, w_in)


def _local_step(x, p, target, pre_norm_g, w_in, ssm_lam_re, ssm_lam_im, ssm_log_step, ssm_b_re, ssm_b_im, ssm_c_re,
                ssm_c_im, ssm_d, ssm_b_glu, attn_sinks, post_norm_g, pl_b_gate, late):
    bl, seq, _ = x.shape
    seg = seq // N_SEG
    nb = seq // ATT_BLOCK
    t = bl * seq
    x2 = x.reshape(t, D_MODEL)
    p2 = p.reshape(t, D_PLE)
    tg2 = target.reshape(t, D_MODEL)

    lam_re, lam_im = ssm_lam_re, ssm_lam_im
    log_step = ssm_log_step.reshape(SSM_G, 1)
    a_re_row, a_im_row, pw_re, pw_im, bcat, bcat_t, ccat, ccat_t = _ssm_prep(
        lam_re, lam_im, log_step, ssm_b_re, ssm_b_im, ssm_c_re, ssm_c_im, seg)
    d_row = ssm_d.reshape(1, D_SSM)

    segments = lambda a: a.reshape(bl, N_SEG, seg, D_SSM)
    u, z_ssm, q, k, v, z_attn = _in_proj(x2, pre_norm_g.reshape(1, D_MODEL), w_in, min(TOKEN_TILE_WIDE, t))
    (y, states, carries), gathered = _ssm_forward(
        segments(u), bcat, ccat, a_re_row, a_im_row, pw_re, pw_im, d_row, late, seg)
    w_out, w_gate, w_proj, w_glu = (_gathered_to_full(n, g) for n, g in zip(LATE_NAMES, gathered))
    sinks = attn_sinks.reshape(N_HEADS)
    attn, lse = _attn_forward(q, k, v, sinks, bl, nb)
    (loss, dh1, dy, dz_ssm, dattn, dz_attn, d_w_glu, d_b_glu, d_w_out, d_g_post, d_w_gate, d_b_gate,
     d_w_proj, *late16) = _mix_forward_backward(
        x2, y.reshape(t, D_SSM), z_ssm, attn, z_attn, p2, tg2, w_glu,
        ssm_b_glu.reshape(1, D_SSM), w_out, post_norm_g.reshape(1, D_MODEL), w_gate, pl_b_gate.reshape(1, D_MODEL),
        w_proj, min(TOKEN_TILE, t))
    owned = lambda ds: [_full_to_owned(n, d) for n, d in zip(LATE_NAMES, ds)]
    dq, dk, dv, d_sinks = _attn_backward(q, k, v, attn, dattn, lse, sinks, bl, nb)
    (du, d_bcat, d_ccat_t, da_re, da_im, d_d), late_grads = _ssm_backward(
        segments(u), segments(dy), states, carries, bcat_t, ccat_t, a_re_row, a_im_row, pw_re, pw_im,
        d_row, owned(late16), owned((d_w_out, d_w_gate, d_w_proj, d_w_glu)), seg)
    grad_x, d_w_in, d_g_pre, d_w_in16 = _in_backward(
        x2, dh1, du.reshape(t, D_SSM), dz_ssm, dq, dk, dv, dz_attn, pre_norm_g.reshape(1, D_MODEL), w_in,
        min(TOKEN_TILE_WIDE, t))
    d_lam_re, d_lam_im, d_ls, d_b_re, d_b_im, d_c_re, d_c_im = _ssm_param_grads(
        lam_re, lam_im, log_step, ssm_b_re, ssm_b_im, da_re.reshape(SSM_G, SSM_N), da_im.reshape(SSM_G, SSM_N),
        d_bcat, d_ccat_t)
    grads = {
        "pre_norm_g": d_g_pre, "w_in": d_w_in, "w_in16": d_w_in16, "ssm_lam_re": d_lam_re, "ssm_lam_im": d_lam_im,
        "ssm_log_step": d_ls, "ssm_b_re": d_b_re, "ssm_b_im": d_b_im, "ssm_c_re": d_c_re, "ssm_c_im": d_c_im,
        "ssm_d": d_d, "ssm_b_glu": d_b_glu, "attn_sinks": d_sinks, "post_norm_g": d_g_post, "pl_b_gate": d_b_gate,
    }
    return loss, grad_x.reshape(bl, seq, D_MODEL), grads, late_grads


LATE_NAMES = ("w_out", "pl_w_gate", "pl_w_proj", "ssm_w_glu")
BIG_NAMES = ("w_in",) + LATE_NAMES
COL_SHARDED = {"w_in": D_IN // N_DEV, "pl_w_proj": D_MODEL // N_DEV}
WEIGHT_NAMES = ("pre_norm_g", "w_in", "ssm_lam_re", "ssm_lam_im", "ssm_log_step", "ssm_b_re", "ssm_b_im", "ssm_c_re",
                "ssm_c_im", "ssm_d", "ssm_w_glu", "ssm_b_glu", "attn_sinks", "w_out", "post_norm_g", "pl_w_proj",
                "pl_w_gate", "pl_b_gate")


TRANSPOSED = {"w_in": (0, 1), "ssm_b_re": (1, 2), "ssm_b_im": (1, 2)}


def _kernel_form(name, a):
    a = a[0]
    if name in TRANSPOSED:
        a = jnp.swapaxes(a, *TRANSPOSED[name])
    if name in ("ssm_b_re", "ssm_b_im", "ssm_c_re", "ssm_c_im"):
        a = a.reshape(SSM_G * SSM_P, SSM_N)
    return a


def _given_form(name, a, shape):
    if name in TRANSPOSED:
        i, j = TRANSPOSED[name]
        swapped = list(shape[1:])
        swapped[i], swapped[j] = swapped[j], swapped[i]
        return jnp.swapaxes(a.reshape(swapped), i, j).reshape(shape)
    return a.reshape(shape)


def _gathered_to_full(name, g):
    _, rows, cols = g.shape
    if name in COL_SHARDED:
        return jnp.swapaxes(g, 0, 1).reshape(rows, N_DEV * cols)
    return g.reshape(N_DEV * rows, cols)


def _full_to_owned(name, full):
    if name in COL_SHARDED:
        return jnp.swapaxes(full.reshape(full.shape[0], N_DEV, COL_SHARDED[name]), 0, 1)
    return full.reshape(N_DEV, full.shape[0] // N_DEV, full.shape[1])


def kernel(x, p, pre_norm_g, w_in, ssm_lam_re, ssm_lam_im, ssm_log_step, ssm_b_re, ssm_b_im, ssm_c_re, ssm_c_im, ssm_d, ssm_w_glu, ssm_b_glu, attn_sinks, w_out, post_norm_g, pl_w_proj, pl_w_gate, pl_b_gate, loss_target, m_pre_norm_g, m_w_in, m_ssm_lam_re, m_ssm_lam_im, m_ssm_log_step, m_ssm_b_re, m_ssm_b_im, m_ssm_c_re, m_ssm_c_im, m_ssm_d, m_ssm_w_glu, m_ssm_b_glu, m_attn_sinks, m_w_out, m_post_norm_g, m_pl_w_proj, m_pl_w_gate, m_pl_b_gate, v_pre_norm_g, v_w_in, v_ssm_lam_re, v_ssm_lam_im, v_ssm_log_step, v_ssm_b_re, v_ssm_b_im, v_ssm_c_re, v_ssm_c_im, v_ssm_d, v_ssm_w_glu, v_ssm_b_glu, v_attn_sinks, v_w_out, v_post_norm_g, v_pl_w_proj, v_pl_w_gate, v_pl_b_gate):
    w = dict(pre_norm_g=pre_norm_g, w_in=w_in, ssm_lam_re=ssm_lam_re, ssm_lam_im=ssm_lam_im, ssm_log_step=ssm_log_step,
             ssm_b_re=ssm_b_re, ssm_b_im=ssm_b_im, ssm_c_re=ssm_c_re, ssm_c_im=ssm_c_im, ssm_d=ssm_d, ssm_w_glu=ssm_w_glu,
             ssm_b_glu=ssm_b_glu, attn_sinks=attn_sinks, w_out=w_out, post_norm_g=post_norm_g, pl_w_proj=pl_w_proj,
             pl_w_gate=pl_w_gate, pl_b_gate=pl_b_gate)
    m = dict(pre_norm_g=m_pre_norm_g, w_in=m_w_in, ssm_lam_re=m_ssm_lam_re, ssm_lam_im=m_ssm_lam_im,
             ssm_log_step=m_ssm_log_step, ssm_b_re=m_ssm_b_re, ssm_b_im=m_ssm_b_im, ssm_c_re=m_ssm_c_re,
             ssm_c_im=m_ssm_c_im, ssm_d=m_ssm_d, ssm_w_glu=m_ssm_w_glu, ssm_b_glu=m_ssm_b_glu, attn_sinks=m_attn_sinks,
             w_out=m_w_out, post_norm_g=m_post_norm_g, pl_w_proj=m_pl_w_proj, pl_w_gate=m_pl_w_gate,
             pl_b_gate=m_pl_b_gate)
    v = dict(pre_norm_g=v_pre_norm_g, w_in=v_w_in, ssm_lam_re=v_ssm_lam_re, ssm_lam_im=v_ssm_lam_im,
             ssm_log_step=v_ssm_log_step, ssm_b_re=v_ssm_b_re, ssm_b_im=v_ssm_b_im, ssm_c_re=v_ssm_c_re,
             ssm_c_im=v_ssm_c_im, ssm_d=v_ssm_d, ssm_w_glu=v_ssm_w_glu, ssm_b_glu=v_ssm_b_glu, attn_sinks=v_attn_sinks,
             w_out=v_w_out, post_norm_g=v_post_norm_g, pl_w_proj=v_pl_w_proj, pl_w_gate=v_pl_w_gate,
             pl_b_gate=v_pl_b_gate)
    kf = lambda d: {n: _kernel_form(n, a) for n, a in d.items()}
    wk, mk, vk = kf(w), kf(m), kf(v)

    (gathered,) = _allgather_weights([wk["w_in"]])
    loss, grad_x, grads, g_late = _local_step(
        x, p[0], loss_target, wk["pre_norm_g"], gathered.reshape(D_IN, D_MODEL), wk["ssm_lam_re"], wk["ssm_lam_im"],
        wk["ssm_log_step"], wk["ssm_b_re"], wk["ssm_b_im"], wk["ssm_c_re"], wk["ssm_c_im"], wk["ssm_d"],
        wk["ssm_b_glu"], wk["attn_sinks"], wk["post_norm_g"], wk["pl_b_gate"], [wk[n] for n in LATE_NAMES])

    owned = lambda g: g.reshape(N_DEV, D_IN // N_DEV, D_MODEL)
    tiny_form = lambda d: [d[n].reshape(rows, cols) for n, rows, cols in TINY]
    med_form = lambda d: [d[n].reshape(N_DEV, rows // N_DEV, cols) for n, rows, cols in MEDIUM]
    g_big, loss, g_tiny, g_med = _reduce_final(
        [owned(grads["w_in16"])], [owned(grads["w_in"])], loss, tiny_form(grads), med_form(grads))
    names = BIG_NAMES + tuple(n for n, _, _ in TINY + MEDIUM)
    form = lambda d: [d[n] for n in BIG_NAMES] + tiny_form(d) + med_form(d)
    updated = _adamw_update(g_big + g_late + g_tiny + g_med, form(wk), form(mk), form(vk))
    vals = dict(zip(names, updated))
    results = [[_given_form(n, vals[n][kind], w[n].shape) for n in WEIGHT_NAMES] for kind in range(4)]
    return (loss.reshape(()), grad_x, *results[0], *results[1], *results[2], *results[3])
```

```python
import functools
import math

import jax
import jax.numpy as jnp
from jax import lax
from jax.experimental import pallas as pl
from jax.experimental.pallas import tpu as pltpu

F32 = jnp.float32
BF16 = jnp.bfloat16

D_MODEL = 1024
D_SSM = 512
D_ATTN = 512
SSM_P = 16
SSM_G = 32
SSM_N = 64
N_HEADS = 8
KV_HEADS = 2
Q_PER_KV = 4
HEAD_DIM = 64
ATT_BLOCK = 128
D_PLE = 256
D_IN = 2304
EPS = 1e-6
N_DEV = 8
N_SEG = 8
G_TILE = 8
N_GT = SSM_G // G_TILE
CH_T = G_TILE * SSM_P
ST_T = G_TILE * SSM_N
N_STATE = SSM_G * SSM_N
SCAN_UNROLL = 4
MIX_GROUPS = 1
TOKEN_TILE = 256
TOKEN_TILE_WIDE = 512
LANES = 128
VMEM_LIMIT = 60 * 1024 * 1024

ADAM_LR = 0.001
ADAM_B1 = 0.9
ADAM_B2 = 0.999
ADAM_EPS = 1e-08
ADAM_WD = 0.01
ADAM_STEP = 10

GELU_C = math.sqrt(2.0 / math.pi)
GELU_K = 0.044715
ATT_SCALE = 1.0 / math.sqrt(HEAD_DIM)
NEG_INF = float("-inf")


def _mm(a, b):
    return jnp.dot(a.astype(BF16), b.astype(BF16), preferred_element_type=F32)


def _mm_nt(a, b):
    return lax.dot_general(a.astype(BF16), b.astype(BF16), (((1,), (1,)), ((), ())), preferred_element_type=F32)


def _mm_tn(a, b):
    return lax.dot_general(a.astype(BF16), b.astype(BF16), (((0,), (0,)), ((), ())), preferred_element_type=F32)


def _sigmoid(x):
    return 1.0 / (1.0 + jnp.exp(-x))


def _tc_params(sem):
    return pltpu.CompilerParams(dimension_semantics=sem, vmem_limit_bytes=VMEM_LIMIT)


def _const_spec(shape):
    nd = len(shape)
    return pl.BlockSpec(shape, lambda *_: (0,) * nd)


def _mesh_pos():
    return lax.axis_index("x"), lax.axis_index("y"), lax.axis_index("c")


ROW_CHUNKS = (64, 32, 16)


def _row_chunk(nrows):
    return next((c for c in ROW_CHUNKS if nrows % c == 0), None)


def _row_chunks(nrows, fn, chunk=None, init=None):
    chunk = chunk or _row_chunk(nrows)

    def step(i, carry):
        rows = pl.ds(pl.multiple_of(i * chunk, chunk), chunk)
        if init is None:
            fn(rows)
            return carry
        return fn(rows, carry)

    return lax.fori_loop(0, nrows // chunk, step, 0 if init is None else init)


def _slot(px, py, pc):
    return 4 * px + 2 * py + pc


def _allgather_weights(shards):
    n = len(shards)

    def body(*refs):
        srcs, outs, (send_sems, recv_sems) = refs[:n], refs[n:2 * n], refs[2 * n:]
        x, y, c = _mesh_pos()
        me, sibling = (x, y, c), (x, y, 1 - c)
        chips = [(1 - x, y), (x, 1 - y), (1 - x, 1 - y)]

        def copy(a, k, block, to):
            blk = outs[a].at[_slot(*block)]
            return pltpu.make_async_remote_copy(
                src_ref=blk, dst_ref=blk, send_sem=send_sems.at[7 * a + k], recv_sem=recv_sems.at[7 * a + k],
                device_id=to, device_id_type=pl.DeviceIdType.MESH)

        sends = []
        for a in range(n):
            mine = outs[a].at[_slot(*me)]

            def cast(r, mine=mine, src=srcs[a]):
                mine[r, :] = src[r, :].astype(BF16)

            _row_chunks(srcs[a].shape[0], cast)
            first = [copy(a, 0, me, sibling)] + [copy(a, 1 + j, me, (*chip, c)) for j, chip in enumerate(chips)]
            for cp in first:
                cp.start()
            sends += first
        for a in range(n):
            for j, chip in enumerate(chips):
                copy(a, 1 + j, (*chip, c), me).wait_recv()
                fwd = copy(a, 4 + j, (*chip, c), sibling)
                fwd.start()
                sends.append(fwd)
        for a in range(n):
            copy(a, 0, sibling, me).wait_recv()
            for j, chip in enumerate(chips):
                copy(a, 4 + j, (*chip, 1 - c), me).wait_recv()
        for cp in sends:
            cp.wait_send()

    vm = pl.BlockSpec(memory_space=pltpu.VMEM)
    return pl.pallas_call(
        body, name="allgather_weights",
        out_shape=tuple(jax.ShapeDtypeStruct((N_DEV,) + s.shape, BF16) for s in shards),
        in_specs=[vm] * n, out_specs=(vm,) * n,
        scratch_shapes=[pltpu.SemaphoreType.DMA((7 * n,)), pltpu.SemaphoreType.DMA((7 * n,))],
        compiler_params=pltpu.CompilerParams(vmem_limit_bytes=VMEM_LIMIT),
    )(*shards)


def _adamw(w, g, m, v):
    m = ADAM_B1 * m + (1.0 - ADAM_B1) * g
    v = ADAM_B2 * v + (1.0 - ADAM_B2) * (g * g)
    m_hat = m / (1.0 - ADAM_B1 ** ADAM_STEP)
    v_hat = v / (1.0 - ADAM_B2 ** ADAM_STEP)
    delta = -ADAM_LR * (m_hat / (jnp.sqrt(v_hat) + ADAM_EPS) + ADAM_WD * w)
    return delta, m, v


def _remote(src, dst, send_sems, recv_sems, k, to):
    return pltpu.make_async_remote_copy(src_ref=src, dst_ref=dst, send_sem=send_sems.at[k], recv_sem=recv_sems.at[k],
                                        device_id=to, device_id_type=pl.DeviceIdType.MESH)


def _big_reduce_phases(g16_r, go_r, outs, send2, recv1, recv2, s_send, s_recv):
    n = len(g16_r)
    x, y, c = _mesh_pos()
    sibling = (x, y, 1 - c)
    chips = [(1 - x, y), (x, 1 - y), (1 - x, 1 - y)]
    all_chips = [(x, y)] + chips
    lvl1 = []
    for a in range(n):
        cps = [_remote(g16_r[a].at[_slot(*chip, 1 - c)], recv1[a].at[j], s_send, s_recv, 7 * a + j, sibling)
               for j, chip in enumerate(all_chips)]
        for cp in cps:
            cp.start()
        lvl1.append(cps)
    yield
    lvl2 = []
    for a in range(n):
        for cp in lvl1[a]:
            cp.wait_recv()
        og = outs[a]

        def partials(r, a=a, og=og):
            og[r, :] = go_r[a][r, :] + recv1[a][0, r, :].astype(F32)
            for j, chip in enumerate(chips):
                mine16 = g16_r[a][_slot(*chip, c), r, :].astype(F32)
                send2[a][j, r, :] = (mine16 + recv1[a][1 + j, r, :].astype(F32)).astype(BF16)

        _row_chunks(go_r[a].shape[0], partials)
        cps = [_remote(send2[a].at[j], recv2[a].at[j], s_send, s_recv, 7 * a + 4 + j, (*chip, c))
               for j, chip in enumerate(chips)]
        for cp in cps:
            cp.start()
        lvl2.append(cps)
    yield
    for a in range(n):
        for cp in lvl2[a]:
            cp.wait_recv()
        og = outs[a]

        def total(r, a=a, og=og):
            g = og[r, :]
            for j in range(3):
                g = g + recv2[a][j, r, :].astype(F32)
            og[r, :] = g

        _row_chunks(go_r[a].shape[0], total)
    yield
    for cps in lvl1 + lvl2:
        for cp in cps:
            cp.wait_send()


def _adamw_update(g, w, m, v):
    n = len(g)

    def body(*refs):
        g_r, w_r, m_r, v_r = (refs[i * n:(i + 1) * n] for i in range(4))
        outs = refs[4 * n:]
        for a in range(n):
            og, od, om, ov = outs[4 * a:4 * a + 4]

            def update(idx, a=a, og=og, od=od, om=om, ov=ov):
                gv = g_r[a][idx]
                d, nm, nv = _adamw(w_r[a][idx], gv, m_r[a][idx], v_r[a][idx])
                og[idx] = gv
                od[idx] = d
                om[idx] = nm
                ov[idx] = nv

            shape = g_r[a].shape
            if len(shape) == 3:
                for b in range(shape[0]):
                    update(b)
            elif _row_chunk(shape[0]) is not None:
                _row_chunks(shape[0], update)
            else:
                update(Ellipsis)

    vm = pl.BlockSpec(memory_space=pltpu.VMEM)
    res = pl.pallas_call(
        body, name="adamw_update",
        out_shape=tuple(jax.ShapeDtypeStruct(t.shape, F32) for t in g for _ in range(4)),
        in_specs=[vm] * (4 * n), out_specs=(vm,) * (4 * n),
        compiler_params=pltpu.CompilerParams(vmem_limit_bytes=VMEM_LIMIT),
    )(*g, *w, *m, *v)
    return [res[4 * a:4 * a + 4] for a in range(n)]


TINY = (("pre_norm_g", 1, 1024), ("post_norm_g", 1, 1024), ("pl_b_gate", 1, 1024), ("ssm_d", 1, 512),
        ("ssm_b_glu", 1, 512), ("ssm_log_step", 1, 32), ("attn_sinks", 1, 8), ("ssm_lam_re", 32, 64),
        ("ssm_lam_im", 32, 64))
MEDIUM = (("ssm_b_re", SSM_G * SSM_P, SSM_N), ("ssm_b_im", SSM_G * SSM_P, SSM_N), ("ssm_c_re", SSM_G * SSM_P, SSM_N),
          ("ssm_c_im", SSM_G * SSM_P, SSM_N))


def _stage_rows():
    offs, r = {}, 0
    for name, rows, cols in TINY + (("loss", 1, 1),):
        if rows > 1:
            r = -(-r // 8) * 8
        offs[name] = r
        r += rows if rows > 1 else max(cols // LANES, 1)
    return offs, -(-r // 8) * 8


def _reduce_final(g16, g32, loss, g_tiny, g_med):
    nb_, nt, nm_ = len(g16), len(TINY), len(MEDIUM)
    offs, stage_rows = _stage_rows()

    def body(*refs):
        g16_r, go_r = refs[:nb_], refs[nb_:2 * nb_]
        base = 2 * nb_
        loss_r, gt, gm = refs[base], refs[base + 1:base + 1 + nt], refs[base + 1 + nt:base + 1 + nt + nm_]
        base += 1 + nt + nm_
        out_b = refs[base:base + nb_]
        base += nb_
        loss_o, out_t, out_m = refs[base], refs[base + 1:base + 1 + nt], refs[base + 1 + nt:base + 1 + nt + nm_]
        base += 1 + nt + nm_
        send2_b, recv1_b, recv2_b = (refs[base + i * nb_:base + (i + 1) * nb_] for i in range(3))
        base += 3 * nb_
        stage = refs[base]
        recv1, part, recv2 = (refs[base + 1 + i * nm_:base + 1 + (i + 1) * nm_] for i in range(3))
        bs_send, bs_recv, s_send, s_recv, own_sems = refs[base + 1 + 3 * nm_:base + 6 + 3 * nm_]
        own32 = refs[base + 6 + 3 * nm_:]
        me = _slot(*_mesh_pos())
        fetch = [pltpu.make_async_copy(go_r[a].at[me], own32[a], own_sems.at[a]) for a in range(nb_)]
        for cp in fetch:
            cp.start()
        big = _big_reduce_phases(g16_r, own32, out_b, send2_b, recv1_b, recv2_b, bs_send, bs_recv)
        small = small_phases(loss_r, gt, gm, loss_o, out_t, out_m, stage, recv1, part, recv2, s_send, s_recv)
        next(big)
        next(small)
        for cp in fetch:
            cp.wait()
        next(big)
        for _ in small:
            pass
        for _ in big:
            pass

    def small_phases(loss_r, gt, gm, loss_o, out_t, out_m, stage, recv1, part, recv2, s_send, s_recv):
        x, y, c = _mesh_pos()
        me = _slot(x, y, c)
        sibling = (x, y, 1 - c)
        chips = [(1 - x, y), (x, 1 - y), (1 - x, 1 - y)]
        all_chips = [(x, y)] + chips
        peers = [sibling] + [(*chip, c) for chip in chips] + [(*chip, 1 - c) for chip in chips]
        sem = iter(range(7 + 14 * nm_))
        lvl1 = []
        for a in range(nm_):
            cps = [_remote(gm[a].at[_slot(*chip, 1 - c)], recv1[a].at[j], s_send, s_recv, next(sem), sibling)
                   for j, chip in enumerate(all_chips)]
            for cp in cps:
                cp.start()
            lvl1.append(cps)
        mine = stage.at[me]
        mine[...] = jnp.zeros((stage_rows, LANES), F32)
        for (name, rows, cols), ref in zip(TINY + (("loss", 1, 1),), gt + (loss_r,)):
            r0 = offs[name]
            if rows > 1:
                mine[r0:r0 + rows, 0:cols] = ref[...]
            elif cols >= LANES:
                for i in range(cols // LANES):
                    mine[r0 + i:r0 + i + 1, :] = ref[:, i * LANES:(i + 1) * LANES]
            else:
                mine[r0:r0 + 1, 0:cols] = ref[...]
        tiny_cps = [_remote(mine, mine, s_send, s_recv, next(sem), peer) for peer in peers]
        for cp in tiny_cps:
            cp.start()
        yield
        lvl2 = []
        for a in range(nm_):
            for cp in lvl1[a]:
                cp.wait_recv()
            for j, chip in enumerate(all_chips):
                part[a][j] = gm[a][_slot(*chip, c)] + recv1[a][j]
            cps = [_remote(part[a].at[1 + j], recv2[a].at[j], s_send, s_recv, next(sem), (*chip, c))
                   for j, chip in enumerate(chips)]
            for cp in cps:
                cp.start()
            lvl2.append(cps)
        yield
        lvl3 = []
        for a in range(nm_):
            for cp in lvl2[a]:
                cp.wait_recv()
            blk = out_m[a].at[me]
            blk[...] = ((part[a][0] + recv2[a][0]) + recv2[a][1]) + recv2[a][2]
            cps = [_remote(blk, blk, s_send, s_recv, next(sem), peer) for peer in peers]
            for cp in cps:
                cp.start()
            lvl3.append(cps)
        yield
        for cp in tiny_cps:
            cp.wait_recv()
        tot = stage[0]
        for d in range(1, N_DEV):
            tot = tot + stage[d]
        loss_o[...] = tot[offs["loss"]:offs["loss"] + 1, 0:1]
        for k, (name, rows, cols) in enumerate(TINY):
            r0 = offs[name]
            if rows > 1:
                out_t[k][...] = tot[r0:r0 + rows, 0:cols]
            elif cols >= LANES:
                for i in range(cols // LANES):
                    out_t[k][:, i * LANES:(i + 1) * LANES] = tot[r0 + i:r0 + i + 1, :]
            else:
                out_t[k][...] = tot[r0:r0 + 1, 0:cols]
        for cps in lvl3:
            for cp in cps:
                cp.wait_recv()
        for cps in lvl1 + lvl2 + lvl3 + [tiny_cps]:
            for cp in cps:
                cp.wait_send()

    vmem = pl.BlockSpec(memory_space=pltpu.VMEM)
    t_shapes = [jax.ShapeDtypeStruct((rows, cols), F32) for _, rows, cols in TINY]
    m_shapes = [jax.ShapeDtypeStruct((N_DEV, rows // N_DEV, cols), F32) for _, rows, cols in MEDIUM]
    blk = [(rows // N_DEV, cols) for _, rows, cols in MEDIUM]
    shard = [g.shape[1:] for g in g16]
    scratch = ([pltpu.VMEM((3,) + s, BF16) for s in shard] + [pltpu.VMEM((4,) + s, BF16) for s in shard]
               + [pltpu.VMEM((3,) + s, BF16) for s in shard]
               + [pltpu.VMEM((N_DEV, stage_rows, LANES), F32)]
               + [pltpu.VMEM((4,) + b, F32) for b in blk] + [pltpu.VMEM((4,) + b, F32) for b in blk]
               + [pltpu.VMEM((3,) + b, F32) for b in blk]
               + [pltpu.SemaphoreType.DMA((7 * nb_,)), pltpu.SemaphoreType.DMA((7 * nb_,)),
                  pltpu.SemaphoreType.DMA((7 + 14 * nm_,)), pltpu.SemaphoreType.DMA((7 + 14 * nm_,)),
                  pltpu.SemaphoreType.DMA((nb_,))]
               + [pltpu.VMEM(s, F32) for s in shard])
    n_out = nb_ + 1 + nt + nm_
    res = pl.pallas_call(
        body, name="reduce_final",
        out_shape=tuple(jax.ShapeDtypeStruct(s, F32) for s in shard) + (jax.ShapeDtypeStruct((1, 1), F32),)
        + tuple(t_shapes) + tuple(m_shapes),
        in_specs=[vmem] * nb_ + [pl.BlockSpec(memory_space=pl.ANY)] * nb_ + [vmem] * (1 + nt + nm_),
        out_specs=(vmem,) * n_out, scratch_shapes=scratch,
        compiler_params=pltpu.CompilerParams(vmem_limit_bytes=VMEM_LIMIT),
    )(*g16, *g32, loss, *g_tiny, *g_med)
    return list(res[:nb_]), res[nb_], list(res[nb_ + 1:nb_ + 1 + nt]), list(res[nb_ + 1 + nt:])


def _gather_phases(shard_r, gath, cast, send_sems, recv_sems, local_sems):
    n = len(shard_r)
    x, y, c = _mesh_pos()
    me, sibling = (x, y, c), (x, y, 1 - c)
    chips = [(1 - x, y), (x, 1 - y), (1 - x, 1 - y)]

    def own(a, k, to):
        return _remote(cast[a], gath[a].at[_slot(*me)], send_sems, recv_sems, 7 * a + k, to)

    def passed(a, k, block, to):
        blk = gath[a].at[_slot(*block)]
        return _remote(blk, blk, send_sems, recv_sems, 7 * a + k, to)

    def keep(a):
        return pltpu.make_async_copy(cast[a], gath[a].at[_slot(*me)], local_sems.at[a])

    def start():
        for a in range(n):
            def to16(r, a=a):
                cast[a][r, :] = shard_r[a][r, :].astype(BF16)

            _row_chunks(shard_r[a].shape[0], to16)
            keep(a).start()
            own(a, 0, sibling).start()
            for j, chip in enumerate(chips):
                own(a, 1 + j, (*chip, c)).start()

    def relay():
        for a in range(n):
            for j, chip in enumerate(chips):
                passed(a, 1 + j, (*chip, c), me).wait_recv()
                passed(a, 4 + j, (*chip, c), sibling).start()

    def finish():
        for a in range(n):
            passed(a, 0, sibling, me).wait_recv()
            for j, chip in enumerate(chips):
                passed(a, 4 + j, (*chip, 1 - c), me).wait_recv()
            own(a, 0, sibling).wait_send()
            for j, chip in enumerate(chips):
                own(a, 1 + j, (*chip, c)).wait_send()
                passed(a, 4 + j, (*chip, c), sibling).wait_send()
            keep(a).wait()

    return start, relay, finish


def _gather_operands(shards):
    n = len(shards)
    return ((pl.BlockSpec(memory_space=pl.ANY),) * n,
            tuple(jax.ShapeDtypeStruct((N_DEV,) + s.shape, BF16) for s in shards),
            [pltpu.VMEM(s.shape, BF16) for s in shards]
            + [pltpu.SemaphoreType.DMA((7 * n,)), pltpu.SemaphoreType.DMA((7 * n,)), pltpu.SemaphoreType.DMA((n,))])


def _hosted_reduce_phases(g16_r, g32_r, red, own16, recv1, send2, recv2, own32, s_send, s_recv, s_local):
    n = len(g16_r)
    x, y, c = _mesh_pos()
    sibling = (x, y, 1 - c)
    chips = [(1 - x, y), (x, 1 - y), (1 - x, 1 - y)]
    all_chips = [(x, y)] + chips

    def lvl1(a, j):
        return _remote(g16_r[a].at[_slot(*all_chips[j], 1 - c)], recv1[a].at[j], s_send, s_recv, 7 * a + j, sibling)

    def lvl2(a, j):
        return _remote(send2[a].at[j], recv2[a].at[j], s_send, s_recv, 7 * a + 4 + j, (*chips[j], c))

    def mine(a, j):
        if j == 3:
            return pltpu.make_async_copy(g32_r[a].at[_slot(x, y, c)], own32[a], s_local.at[4 * a + j])
        return pltpu.make_async_copy(g16_r[a].at[_slot(*chips[j], c)], own16[a].at[j], s_local.at[4 * a + j])

    def start():
        for a in range(n):
            for j in range(4):
                mine(a, j).start()
            for j in range(4):
                lvl1(a, j).start()

    def middle():
        for a in range(n):
            for j in range(4):
                mine(a, j).wait()
            for j in range(4):
                lvl1(a, j).wait_recv()

            def partials(r, a=a):
                red[a][r, :] = own32[a][r, :] + recv1[a][0, r, :].astype(F32)
                for j in range(3):
                    send2[a][j, r, :] = (own16[a][j, r, :].astype(F32) + recv1[a][1 + j, r, :].astype(F32)).astype(BF16)

            _row_chunks(own32[a].shape[0], partials)
            for j in range(3):
                lvl2(a, j).start()

    def total():
        for a in range(n):
            for j in range(3):
                lvl2(a, j).wait_recv()

            def add(r, a=a):
                g = red[a][r, :]
                for j in range(3):
                    g = g + recv2[a][j, r, :].astype(F32)
                red[a][r, :] = g

            _row_chunks(own32[a].shape[0], add)

    def finish():
        for a in range(n):
            for j in range(4):
                lvl1(a, j).wait_send()
            for j in range(3):
                lvl2(a, j).wait_send()

    return start, middle, total, finish


def _hosted_reduce_operands(g16, const_spec):
    n = len(g16)
    shard = [g.shape[1:] for g in g16]
    return ([pl.BlockSpec(memory_space=pl.ANY)] * (2 * n),
            tuple(const_spec(s) for s in shard),
            tuple(jax.ShapeDtypeStruct(s, F32) for s in shard),
            [pltpu.VMEM((3,) + s, BF16) for s in shard] + [pltpu.VMEM((4,) + s, BF16) for s in shard]
            + [pltpu.VMEM((3,) + s, BF16) for s in shard] + [pltpu.VMEM((3,) + s, BF16) for s in shard]
            + [pltpu.VMEM(s, F32) for s in shard]
            + [pltpu.SemaphoreType.DMA((7 * n,)), pltpu.SemaphoreType.DMA((7 * n,)), pltpu.SemaphoreType.DMA((4 * n,))])


def _in_proj(x2, g_pre, w_in, tm):
    t = x2.shape[0]

    def body(x_ref, g_ref, w_ref, u_ref, zs_ref, q_ref, k_ref, v_ref, za_ref):
        xv = x_ref[...]
        r = lax.rsqrt(jnp.mean(xv * xv, axis=-1, keepdims=True) + EPS)
        hn = xv * r * g_ref[...]
        proj = _mm_nt(hn, w_ref[...])
        u_ref[...] = proj[:, 0:512]
        zs_ref[...] = proj[:, 512:1024]
        q_ref[...] = proj[:, 1024:1536].astype(BF16)
        k_ref[...] = proj[:, 1536:1664].astype(BF16)
        v_ref[...] = proj[:, 1664:1792].astype(BF16)
        za_ref[...] = proj[:, 1792:2304]

    row = lambda w: pl.BlockSpec((tm, w), lambda i: (i, 0))
    return pl.pallas_call(
        body, name="in_proj", grid=(t // tm,),
        in_specs=[row(D_MODEL), _const_spec((1, D_MODEL)), _const_spec((D_IN, D_MODEL))],
        out_specs=(row(512), row(512), row(512), row(128), row(128), row(512)),
        out_shape=(jax.ShapeDtypeStruct((t, 512), F32),
                   jax.ShapeDtypeStruct((t, 512), F32), jax.ShapeDtypeStruct((t, 512), BF16),
                   jax.ShapeDtypeStruct((t, 128), BF16), jax.ShapeDtypeStruct((t, 128), BF16),
                   jax.ShapeDtypeStruct((t, 512), F32)),
        compiler_params=_tc_params(("arbitrary",)),
    )(x2, g_pre, w_in)


def _discretise(lr, li, ls):
    step = jnp.exp(ls)
    mag = jnp.exp(lr * step)
    ar = mag * jnp.cos(li * step)
    ai = mag * jnp.sin(li * step)
    den = lr * lr + li * li
    cr = ((ar - 1.0) * lr + ai * li) / den
    ci = (ai * lr - (ar - 1.0) * li) / den
    return step, ar, ai, den, cr, ci


def _per_channel(v):
    return jnp.broadcast_to(v[:, None, :], (SSM_G, SSM_P, SSM_N)).reshape(SSM_G * SSM_P, SSM_N)


def _tile_masks():
    r = lax.broadcasted_iota(jnp.int32, (CH_T, ST_T), 0) // SSM_P
    l = lax.broadcasted_iota(jnp.int32, (CH_T, ST_T), 1) // SSM_N
    lt = lax.broadcasted_iota(jnp.int32, (ST_T, CH_T), 0) // SSM_N
    rt = lax.broadcasted_iota(jnp.int32, (ST_T, CH_T), 1) // SSM_P
    rep = lax.broadcasted_iota(jnp.int32, (SSM_N, ST_T), 0) == lax.broadcasted_iota(jnp.int32, (SSM_N, ST_T), 1) % SSM_N
    rep_t = lax.broadcasted_iota(jnp.int32, (ST_T, SSM_N), 0) % SSM_N == lax.broadcasted_iota(jnp.int32, (ST_T, SSM_N), 1)
    return r == l, lt == rt, rep, rep_t


def _ssm_prep(lam_re, lam_im, log_step, b_re, b_im, c_re, c_im, seg):
    def body(lr_ref, li_ref, ls_ref, br_ref, bi_ref, cre_ref, cim_ref, lrr_ref, lir_ref, lsr_ref,
             ar_ref, ai_ref, pr_ref, pi_ref, bcat_ref, bcat_t_ref, ccat_ref, ccat_t_ref):
        _, _, _, _, cr, ci = _discretise(lr_ref[...], li_ref[...], ls_ref[...])
        cr, ci = _per_channel(cr), _per_channel(ci)
        br, bi = br_ref[...], bi_ref[...]
        bb_re = cr * br - ci * bi
        bb_im = cr * bi + ci * br
        same, same_t, rep, rep_t = _tile_masks()
        rep, rep_t = rep.astype(BF16), rep_t.astype(BF16)
        for j in range(N_GT):
            rows = slice(j * CH_T, (j + 1) * CH_T)
            for wide, tall, parts in ((bcat_ref, bcat_t_ref, (bb_re[rows], bb_im[rows])),
                                      (ccat_t_ref, ccat_ref, (cre_ref[rows, :], -cim_ref[rows, :]))):
                for k, part in enumerate(parts):
                    p16 = part.astype(BF16)
                    wide[j, :, k * ST_T:(k + 1) * ST_T] = jnp.where(same, _mm(p16, rep), 0.0).astype(BF16)
                    tall[j, k * ST_T:(k + 1) * ST_T, :] = jnp.where(same_t, _mm_nt(rep_t, p16), 0.0).astype(BF16)
        stepr = jnp.exp(lsr_ref[...])
        mag = jnp.exp(lrr_ref[...] * stepr)
        a_r, a_i = mag * jnp.cos(lir_ref[...] * stepr), mag * jnp.sin(lir_ref[...] * stepr)
        p_r, p_i = a_r, a_i
        for k in range(8):
            pr_ref[k:k + 1, :] = p_r
            pi_ref[k:k + 1, :] = p_i
            p_r, p_i = p_r * a_r - p_i * a_i, p_r * a_i + p_i * a_r
        n = 8
        while n < seg:
            tr, ti = pr_ref[n - 1:n, :], pi_ref[n - 1:n, :]
            xr, xi = pr_ref[0:n, :], pi_ref[0:n, :]
            pr_ref[n:2 * n, :] = xr * tr - xi * ti
            pi_ref[n:2 * n, :] = xr * ti + xi * tr
            n *= 2
        ar_ref[...] = pr_ref[0:1, :]
        ai_ref[...] = pi_ref[0:1, :]

    row = jax.ShapeDtypeStruct((1, N_STATE), F32)
    pw = jax.ShapeDtypeStruct((seg, N_STATE), F32)
    wide = jax.ShapeDtypeStruct((N_GT, CH_T, 2 * ST_T), BF16)
    tall = jax.ShapeDtypeStruct((N_GT, 2 * ST_T, CH_T), BF16)
    vm = pl.BlockSpec(memory_space=pltpu.VMEM)
    step_row = jnp.broadcast_to(log_step, (SSM_G, SSM_N)).reshape(1, N_STATE)
    return pl.pallas_call(
        body, name="ssm_prep", out_shape=(row, row, pw, pw, wide, tall, tall, wide),
        in_specs=[vm] * 10, out_specs=(vm,) * 8,
    )(lam_re, lam_im, log_step, b_re, b_im, c_re, c_im, lam_re.reshape(1, N_STATE), lam_im.reshape(1, N_STATE),
      step_row)


def _seg_rows(t):
    if isinstance(t, int):
        return pl.ds(t * N_SEG, N_SEG)
    return pl.ds(pl.multiple_of(t * N_SEG, N_SEG), N_SEG)


def _scan_forward(xs, a_re, a_im, pw_re, pw_im, cs, seg):
    are = jnp.broadcast_to(a_re, (N_SEG, ST_T))
    aim = jnp.broadcast_to(a_im, (N_SEG, ST_T))

    def steps(k, carry):
        xr, xi = carry
        for j in range(SCAN_UNROLL):
            r = pl.multiple_of((k * SCAN_UNROLL + j) * N_SEG, N_SEG)
            nr = are * xr - aim * xi + xs[pl.ds(r, N_SEG), 0:ST_T]
            ni = are * xi + aim * xr + xs[pl.ds(r, N_SEG), ST_T:2 * ST_T]
            xs[pl.ds(r, N_SEG), 0:ST_T] = nr
            xs[pl.ds(r, N_SEG), ST_T:2 * ST_T] = ni
            xr, xi = nr, ni
        return xr, xi

    zero = jnp.zeros((N_SEG, ST_T), F32)
    fr, fi = lax.fori_loop(0, seg // SCAN_UNROLL, steps, (zero, zero))
    sr, si = pw_re[seg - 1:seg, :], pw_im[seg - 1:seg, :]
    cr = jnp.zeros((1, ST_T), F32)
    ci = jnp.zeros((1, ST_T), F32)
    cs[0:1, :] = cr
    cs[8:9, :] = ci
    for s in range(1, N_SEG):
        ncr = sr * cr - si * ci + fr[s - 1:s, :]
        nci = sr * ci + si * cr + fi[s - 1:s, :]
        cr, ci = ncr, nci
        cs[s:s + 1, :] = cr
        cs[8 + s:9 + s, :] = ci
    car, cai = cs[0:8, :], cs[8:16, :]

    def fix(t, _):
        r = pl.multiple_of(t * N_SEG, N_SEG)
        pr, pi = pw_re[pl.ds(t, 1), :], pw_im[pl.ds(t, 1), :]
        xs[pl.ds(r, N_SEG), 0:ST_T] = xs[pl.ds(r, N_SEG), 0:ST_T] + (pr * car - pi * cai)
        xs[pl.ds(r, N_SEG), ST_T:2 * ST_T] = xs[pl.ds(r, N_SEG), ST_T:2 * ST_T] + (pr * cai + pi * car)
        return 0

    lax.fori_loop(0, seg, fix, 0, unroll=SCAN_UNROLL)


def _interleave(src, dst, seg):
    for s in range(N_SEG):
        dst[pl.ds(s, seg, stride=N_SEG), :] = src[s]


def _deinterleave(src, seg, s):
    return src[pl.ds(s, seg, stride=N_SEG), :]


def _ssm_forward(u, bcat, ccat, a_re, a_im, pw_re, pw_im, d_row, late, seg):
    bl = u.shape[0]
    rows = N_SEG * seg
    n = len(late)
    steps = bl * N_GT

    def body(*refs):
        u_ref, b_ref, c_ref, ar_ref, ai_ref, pr_ref, pi_ref, d_ref = refs[:8]
        late_r = refs[8:8 + n]
        y_ref, xs_ref, cs_ref = refs[8 + n:11 + n]
        gath, cast = refs[11 + n:11 + 2 * n], refs[11 + 2 * n:11 + 3 * n]
        send_sems, recv_sems, local_sems, ui, yi = refs[11 + 3 * n:]
        step = pl.program_id(0) * N_GT + pl.program_id(1)
        start, relay, finish = _gather_phases(late_r, gath, cast, send_sems, recv_sems, local_sems)
        pl.when(step == 0)(start)
        _interleave(u_ref.at[0], ui, seg)
        u = ui[...]
        xs, cs = xs_ref.at[0, 0], cs_ref.at[0, 0]
        xs[...] = _mm(u, b_ref[0])
        _scan_forward(xs, ar_ref[...], ai_ref[...], pr_ref, pi_ref, cs, seg)
        yi[...] = _mm(xs[...], c_ref[0]) + d_ref[...] * u
        for s in range(N_SEG):
            y_ref[0, s] = _deinterleave(yi, seg, s)
        pl.when(step == steps // 2)(relay)
        pl.when(step == steps - 1)(finish)

    state = lambda r, c: pl.BlockSpec((1, 1, r, c), lambda b, j: (b, j, 0, 0))
    act = pl.BlockSpec((1, N_SEG, seg, CH_T), lambda b, j: (b, 0, 0, j))
    g_specs, g_shapes, g_scratch = _gather_operands(late)
    res = pl.pallas_call(
        body, name="ssm_forward", grid=(bl, N_GT),
        in_specs=[act,
                  pl.BlockSpec((1, CH_T, 2 * ST_T), lambda b, j: (j, 0, 0)),
                  pl.BlockSpec((1, 2 * ST_T, CH_T), lambda b, j: (j, 0, 0)),
                  pl.BlockSpec((1, ST_T), lambda b, j: (0, j)), pl.BlockSpec((1, ST_T), lambda b, j: (0, j)),
                  pl.BlockSpec((seg, ST_T), lambda b, j: (0, j)), pl.BlockSpec((seg, ST_T), lambda b, j: (0, j)),
                  pl.BlockSpec((1, CH_T), lambda b, j: (0, j))]
        + [pl.BlockSpec(s.shape, lambda b, j: (0, 0)) for s in late],
        out_specs=(act, state(rows, 2 * ST_T), state(16, ST_T)) + g_specs,
        out_shape=(jax.ShapeDtypeStruct((bl, N_SEG, seg, D_SSM), F32),
                   jax.ShapeDtypeStruct((bl, N_GT, rows, 2 * ST_T), F32),
                   jax.ShapeDtypeStruct((bl, N_GT, 16, ST_T), F32)) + g_shapes,
        scratch_shapes=g_scratch + [pltpu.VMEM((rows, CH_T), F32), pltpu.VMEM((rows, CH_T), F32)],
        compiler_params=_tc_params(("arbitrary", "arbitrary")),
    )(u, bcat, ccat, a_re, a_im, pw_re, pw_im, d_row, *late)
    return res[:3], list(res[3:])


def _ssm_backward(u, dy, states, carries, bcat_t, ccat_t, a_re, a_im, pw_re, pw_im, d_row, late16, late32, seg):
    bl = u.shape[0]
    rows = N_SEG * seg
    n = len(late16)
    grid_steps = N_GT * bl

    def body(*refs):
        u_ref, dy_ref, xs_ref, cs_ref, bt_ref, ct_ref, ar_ref, ai_ref, pr_ref, pi_ref, d_ref = refs[:11]
        g16_r, g32_r = refs[11:11 + n], refs[11 + n:11 + 2 * n]
        du_ref, db_ref, dc_ref, dar_ref, dai_ref, dd_ref = refs[11 + 2 * n:17 + 2 * n]
        red = refs[17 + 2 * n:17 + 3 * n]
        own16, recv1, send2, recv2, own32 = (refs[17 + 3 * n + k * n:17 + 3 * n + (k + 1) * n] for k in range(5))
        s_send, s_recv, s_local, ls, cl, ui, dyi, dui = refs[17 + 8 * n:]
        b = pl.program_id(1)
        step = pl.program_id(0) * bl + b
        start, middle, total, finish = _hosted_reduce_phases(g16_r, g32_r, red, own16, recv1, send2, recv2, own32,
                                                             s_send, s_recv, s_local)
        pl.when(step == 0)(start)
        pl.when(step == grid_steps // 4)(middle)
        pl.when(step == (grid_steps * 3) // 4)(total)
        pl.when(step == grid_steps - 1)(finish)
        _interleave(u_ref.at[0], ui, seg)
        _interleave(dy_ref.at[0], dyi, seg)
        u = ui[...]
        dy = dyi[...]
        xs, cs = xs_ref.at[0, 0], cs_ref.at[0, 0]
        ls[...] = _mm(dy, ct_ref[0])
        are = jnp.broadcast_to(ar_ref[...], (N_SEG, ST_T))
        aim = jnp.broadcast_to(ai_ref[...], (N_SEG, ST_T))

        def steps(k, carry):
            lr, li = carry
            for j in range(SCAN_UNROLL):
                r = pl.multiple_of((seg - 1 - (k * SCAN_UNROLL + j)) * N_SEG, N_SEG)
                nr = are * lr + aim * li + ls[pl.ds(r, N_SEG), 0:ST_T]
                ni = are * li - aim * lr + ls[pl.ds(r, N_SEG), ST_T:2 * ST_T]
                ls[pl.ds(r, N_SEG), 0:ST_T] = nr
                ls[pl.ds(r, N_SEG), ST_T:2 * ST_T] = ni
                lr, li = nr, ni
            return lr, li

        zero = jnp.zeros((N_SEG, ST_T), F32)
        fr, fi = lax.fori_loop(0, seg // SCAN_UNROLL, steps, (zero, zero))
        sr, si = pr_ref[seg - 1:seg, :], pi_ref[seg - 1:seg, :]
        cr = jnp.zeros((1, ST_T), F32)
        ci = jnp.zeros((1, ST_T), F32)
        cl[7:8, :] = cr
        cl[15:16, :] = ci
        for s in range(N_SEG - 2, -1, -1):
            ncr = sr * cr + si * ci + fr[s + 1:s + 2, :]
            nci = sr * ci - si * cr + fi[s + 1:s + 2, :]
            cr, ci = ncr, nci
            cl[s:s + 1, :] = cr
            cl[8 + s:9 + s, :] = ci
        clr, cli = cl[0:8, :], cl[8:16, :]

        def fix_rows(rows, t, xpr, xpi, acc):
            dr, di = acc
            pr, pi = pr_ref[pl.ds(seg - 1 - t, 1), :], pi_ref[pl.ds(seg - 1 - t, 1), :]
            lr = ls[rows, 0:ST_T] + (pr * clr + pi * cli)
            li = ls[rows, ST_T:2 * ST_T] + (pr * cli - pi * clr)
            ls[rows, 0:ST_T] = lr
            ls[rows, ST_T:2 * ST_T] = li
            return dr + (lr * xpr + li * xpi), di + (li * xpr - lr * xpi)

        def fix_at(t, acc):
            prev = _seg_rows(t - 1)
            return fix_rows(_seg_rows(t), t, xs[prev, 0:ST_T], xs[prev, ST_T:2 * ST_T], acc)

        def fix(k, acc):
            for j in range(SCAN_UNROLL):
                acc = fix_at(k * SCAN_UNROLL + j, acc)
            return acc

        acc = fix_rows(pl.ds(0, N_SEG), 0, cs[0:8, :], cs[8:16, :], (zero, zero))
        for t in range(1, SCAN_UNROLL):
            acc = fix_at(t, acc)
        dr, di = lax.fori_loop(1, seg // SCAN_UNROLL, fix, acc)
        dar = jnp.sum(dr, axis=0, keepdims=True)
        dai = jnp.sum(di, axis=0, keepdims=True)
        lall = ls[...]
        dui[...] = _mm(lall, bt_ref[0]) + d_ref[...] * dy
        for s in range(N_SEG):
            du_ref[0, s] = _deinterleave(dui, seg, s).astype(BF16)
        dbp = _mm_tn(u, lall)
        dcp = _mm_tn(dy, xs[...])
        ddp = jnp.sum(dy * u, axis=0, keepdims=True)

        @pl.when(b == 0)
        def _():
            db_ref[0] = dbp
            dc_ref[0] = dcp
            dar_ref[...] = dar
            dai_ref[...] = dai
            dd_ref[...] = ddp

        @pl.when(b != 0)
        def _():
            db_ref[0] += dbp
            dc_ref[0] += dcp
            dar_ref[...] += dar
            dai_ref[...] += dai
            dd_ref[...] += ddp

    tile3 = lambda r, c: pl.BlockSpec((1, r, c), lambda j, b: (j, 0, 0))
    lane = lambda r, c: pl.BlockSpec((r, c), lambda j, b: (0, j))
    act = pl.BlockSpec((1, N_SEG, seg, CH_T), lambda j, b: (b, 0, 0, j))
    state = lambda r, c: pl.BlockSpec((1, 1, r, c), lambda j, b: (b, j, 0, 0))
    r_in, r_out, r_shapes, r_scratch = _hosted_reduce_operands(late16, lambda s: pl.BlockSpec(s, lambda j, b: (0, 0)))
    res = pl.pallas_call(
        body, name="ssm_backward", grid=(N_GT, bl),
        in_specs=[act, act, state(rows, 2 * ST_T), state(16, ST_T), tile3(2 * ST_T, CH_T), tile3(CH_T, 2 * ST_T),
                  lane(1, ST_T), lane(1, ST_T), lane(seg, ST_T), lane(seg, ST_T), lane(1, CH_T)] + r_in,
        out_specs=(act, tile3(CH_T, 2 * ST_T), tile3(CH_T, 2 * ST_T), lane(1, ST_T), lane(1, ST_T), lane(1, CH_T))
        + r_out,
        out_shape=(jax.ShapeDtypeStruct((bl, N_SEG, seg, D_SSM), BF16),
                   jax.ShapeDtypeStruct((N_GT, CH_T, 2 * ST_T), F32), jax.ShapeDtypeStruct((N_GT, CH_T, 2 * ST_T), F32),
                   jax.ShapeDtypeStruct((1, N_STATE), F32), jax.ShapeDtypeStruct((1, N_STATE), F32),
                   jax.ShapeDtypeStruct((1, D_SSM), F32)) + r_shapes,
        scratch_shapes=r_scratch + [pltpu.VMEM((rows, 2 * ST_T), F32), pltpu.VMEM((16, ST_T), F32)]
        + [pltpu.VMEM((rows, CH_T), F32)] * 3,
        compiler_params=_tc_params(("arbitrary", "arbitrary")),
    )(u, dy, states, carries, bcat_t, ccat_t, a_re, a_im, pw_re, pw_im, d_row, *late16, *late32)
    return res[:6], list(res[6:])


def _ssm_param_grads(lam_re, lam_im, log_step, b_re, b_im, da_re, da_im, d_bcat, d_ccat_t):
    def body(lr_ref, li_ref, ls_ref, br_ref, bi_ref, gar_ref, gai_ref, gbcat_ref, gccat_ref,
             dlr_ref, dli_ref, dls_ref, dbr_ref, dbi_ref, dcr_ref, dci_ref, gbr_s, gbi_s):
        same, _, _, rep_t = _tile_masks()
        rep_t = rep_t.astype(F32)
        for j in range(N_GT):
            rows = slice(j * CH_T, (j + 1) * CH_T)
            for src, dsts in ((gbcat_ref, (gbr_s, gbi_s)), (gccat_ref, (dcr_ref, dci_ref))):
                for k, dst in enumerate(dsts):
                    blk = jnp.where(same, src[j, :, k * ST_T:(k + 1) * ST_T], 0.0)
                    dst[rows, :] = jnp.dot(blk, rep_t, precision=lax.Precision.HIGHEST, preferred_element_type=F32)
        dci_ref[...] = -dci_ref[...]
        lr, li = lr_ref[...], li_ref[...]
        step, ar, ai, den, cr, ci = _discretise(lr, li, ls_ref[...])
        crb, cib = _per_channel(cr), _per_channel(ci)
        br, bi = br_ref[...], bi_ref[...]
        gbr, gbi = gbr_s[...], gbi_s[...]
        dbr_ref[...] = crb * gbr + cib * gbi
        dbi_ref[...] = crb * gbi - cib * gbr
        over_channels = lambda t: jnp.sum(t.reshape(SSM_G, SSM_P, SSM_N), axis=1)
        gcr = over_channels(br * gbr + bi * gbi)
        gci = over_channels(br * gbi - bi * gbr)
        ilr, ili = lr / den, -li / den
        gar = gar_ref[...] + (ilr * gcr + ili * gci)
        gai = gai_ref[...] + (ilr * gci - ili * gcr)
        qr, qi = cr * ilr - ci * ili, cr * ili + ci * ilr
        glr = -(qr * gcr + qi * gci)
        gli = -(qr * gci - qi * gcr)
        gwr = ar * gar + ai * gai
        gwi = ar * gai - ai * gar
        dlr_ref[...] = glr + step * gwr
        dli_ref[...] = gli + step * gwi
        dls_ref[...] = jnp.sum(lr * gwr + li * gwi, axis=-1, keepdims=True) * step

    lam = jax.ShapeDtypeStruct((SSM_G, SSM_N), F32)
    mat = jax.ShapeDtypeStruct((SSM_G * SSM_P, SSM_N), F32)
    vm = pl.BlockSpec(memory_space=pltpu.VMEM)
    return pl.pallas_call(
        body, name="ssm_param_grads", out_shape=(lam, lam, jax.ShapeDtypeStruct((SSM_G, 1), F32), mat, mat, mat, mat),
        in_specs=[vm] * 9, out_specs=(vm,) * 7,
        scratch_shapes=[pltpu.VMEM((SSM_G * SSM_P, SSM_N), F32), pltpu.VMEM((SSM_G * SSM_P, SSM_N), F32)],
    )(lam_re, lam_im, log_step, b_re, b_im, da_re, da_im, d_bcat, d_ccat_t)


ROWS4 = Q_PER_KV * ATT_BLOCK
ATT_FWD_STACK = 1


def _att_dist_mask(first_block):
    qi = lax.broadcasted_iota(jnp.int32, (ROWS4, 2 * ATT_BLOCK), 0) & (ATT_BLOCK - 1)
    si = lax.broadcasted_iota(jnp.int32, (ROWS4, 2 * ATT_BLOCK), 1)
    dist = qi + ATT_BLOCK - si
    valid = (dist >= 0) & (dist < ATT_BLOCK) & ((si >= ATT_BLOCK) | jnp.logical_not(first_block))
    return dist.astype(F32), valid


def _stack_heads(x, kv):
    return jnp.concatenate([x[:, (kv * Q_PER_KV + g) * HEAD_DIM:(kv * Q_PER_KV + g + 1) * HEAD_DIM]
                            for g in range(Q_PER_KV)], axis=0)


def _stack_cols(x, kv):
    return jnp.concatenate([x[:, kv * Q_PER_KV + g:kv * Q_PER_KV + g + 1] for g in range(Q_PER_KV)], axis=0)


def _per_head_col(vals):
    return jnp.concatenate([jnp.full((ATT_BLOCK, 1), v, F32) for v in vals], axis=0)


def _attn_forward(q, k, v, sinks, bl, nb):
    t = q.shape[0]

    def body(sink_ref, q_ref, kp_ref, kc_ref, vp_ref, vc_ref, o_ref, lse_ref):
        i = pl.program_id(1)
        dist4, valid4 = _att_dist_mask(i == 0)
        rows2 = ATT_FWD_STACK * ATT_BLOCK
        dist, valid = dist4[0:rows2, :], valid4[0:rows2, :]
        kk = jnp.concatenate([kp_ref[...], kc_ref[...]], axis=0)
        vv = jnp.concatenate([vp_ref[...], vc_ref[...]], axis=0)
        qv = q_ref[...]
        col = lambda vals: jnp.concatenate([jnp.full((ATT_BLOCK, 1), v, F32) for v in vals], axis=0)
        for h0 in range(0, N_HEADS, ATT_FWD_STACK):
            heads = range(h0, h0 + ATT_FWD_STACK)
            kv = h0 // Q_PER_KV
            slope = col([2.0 ** (-(h + 1)) for h in heads])
            sink = col([sink_ref[h] for h in heads])
            qh = jnp.concatenate([qv[:, h * HEAD_DIM:(h + 1) * HEAD_DIM] for h in heads], axis=0)
            kh = kk[:, kv * HEAD_DIM:(kv + 1) * HEAD_DIM]
            vh = vv[:, kv * HEAD_DIM:(kv + 1) * HEAD_DIM]
            s = _mm_nt(qh, kh) * ATT_SCALE - slope * dist
            s = jnp.where(valid, s, NEG_INF)
            m = jnp.maximum(jnp.max(s, axis=-1, keepdims=True), sink)
            e = jnp.exp(s - m)
            den = jnp.sum(e, axis=-1, keepdims=True) + jnp.exp(sink - m)
            o = _mm(e, vh) * (1.0 / den)
            lse = m + jnp.log(den)
            for g, h in enumerate(heads):
                rows = slice(g * ATT_BLOCK, (g + 1) * ATT_BLOCK)
                o_ref[:, h * HEAD_DIM:(h + 1) * HEAD_DIM] = o[rows, :]
                lse_ref[:, h:h + 1] = lse[rows, :]

    cur = lambda w: pl.BlockSpec((ATT_BLOCK, w), lambda b, i: (b * nb + i, 0))
    prev = lambda w: pl.BlockSpec((ATT_BLOCK, w), lambda b, i: (b * nb + jnp.maximum(i - 1, 0), 0))
    return pl.pallas_call(
        body, name="attn_forward", grid=(bl, nb),
        in_specs=[pl.BlockSpec(memory_space=pltpu.SMEM), cur(512), prev(128), cur(128), prev(128), cur(128)],
        out_specs=(cur(512), cur(N_HEADS)),
        out_shape=(jax.ShapeDtypeStruct((t, D_ATTN), F32), jax.ShapeDtypeStruct((t, N_HEADS), F32)),
        compiler_params=_tc_params(("arbitrary", "arbitrary")),
    )(sinks, q, k, k, v, v)


def _attn_backward(q, k, v, o, do, lse, sinks, bl, nb):
    t = q.shape[0]

    def body(sink_ref, qc_ref, kp_ref, kc_ref, vp_ref, vc_ref, oc_ref, doc_ref, lc_ref,
             dq_ref, dk_ref, dv_ref, ds_ref, dk_carry, dv_carry):
        b, i = pl.program_id(0), pl.program_id(1)
        live = i < nb

        @pl.when(i == 0)
        def _():
            dk_carry[...] = jnp.zeros((ATT_BLOCK, KV_HEADS * HEAD_DIM), F32)
            dv_carry[...] = jnp.zeros((ATT_BLOCK, KV_HEADS * HEAD_DIM), F32)

        dist, valid = _att_dist_mask(i == 0)
        valid = valid & live
        kk = jnp.concatenate([kp_ref[...], kc_ref[...]], axis=0)
        vv = jnp.concatenate([vp_ref[...], vc_ref[...]], axis=0)
        qc, oc, doc, lc = qc_ref[...], oc_ref[...], doc_ref[...], lc_ref[...]
        dsink_cols, dq_parts, dk_t, dv_t = [], [], [], []
        for kv in range(KV_HEADS):
            heads = range(kv * Q_PER_KV, (kv + 1) * Q_PER_KV)
            cols = slice(kv * HEAD_DIM, (kv + 1) * HEAD_DIM)
            kh, vh = kk[:, cols], vv[:, cols]
            slope = _per_head_col([2.0 ** (-(h + 1)) for h in heads])
            sink = _per_head_col([sink_ref[h] for h in heads])
            q4, do4 = _stack_heads(qc, kv), _stack_heads(doc, kv)
            delta = jnp.sum(do4 * _stack_heads(oc, kv), axis=-1, keepdims=True)
            lse4 = _stack_cols(lc, kv)
            s = _mm_nt(q4, kh) * ATT_SCALE - slope * dist
            p = jnp.where(valid, jnp.exp(s - lse4), 0.0)
            dsc = p * (_mm_nt(do4, vh) - delta)
            dq4 = _mm(dsc, kh) * ATT_SCALE
            dk_t.append(_mm_tn(q4, dsc) * ATT_SCALE)
            dv_t.append(_mm_tn(do4, p))
            dsink4 = jnp.where(live, jnp.exp(sink - lse4) * delta, 0.0)
            for g, h in enumerate(heads):
                rows = slice(g * ATT_BLOCK, (g + 1) * ATT_BLOCK)
                dq_parts.append((h, dq4[rows, :]))
                dsink_cols.append(-jnp.sum(dsink4[rows, :], axis=0, keepdims=True))
        dsink = jnp.concatenate(dsink_cols, axis=1)
        for out_ref, carry, parts in ((dk_ref, dk_carry, dk_t), (dv_ref, dv_carry, dv_t)):
            both = jnp.concatenate(parts, axis=0)
            out_ref[...] = (carry[...] + both[:, 0:ATT_BLOCK]).T
            carry[...] = both[:, ATT_BLOCK:]

        @pl.when(live)
        def _():
            for h, part in dq_parts:
                dq_ref[:, h * HEAD_DIM:(h + 1) * HEAD_DIM] = part

        @pl.when((b == 0) & (i == 0))
        def _():
            ds_ref[...] = dsink

        @pl.when((b != 0) | (i != 0))
        def _():
            ds_ref[...] += dsink

    cur_i = lambda i: jnp.minimum(i, nb - 1)
    cur = lambda w: pl.BlockSpec((ATT_BLOCK, w), lambda b, i: (b * nb + cur_i(i), 0))
    prev = lambda w: pl.BlockSpec((ATT_BLOCK, w), lambda b, i: (b * nb + jnp.maximum(cur_i(i) - 1, 0), 0))
    behind = lambda w: pl.BlockSpec((ATT_BLOCK, w), lambda b, i: (b * nb + jnp.maximum(i - 1, 0), 0))
    return pl.pallas_call(
        body, name="attn_backward", grid=(bl, nb + 1),
        in_specs=[pl.BlockSpec(memory_space=pltpu.SMEM), cur(512), prev(128), cur(128), prev(128), cur(128),
                  cur(512), cur(512), cur(N_HEADS)],
        out_specs=(cur(512), behind(128), behind(128), pl.BlockSpec((1, N_HEADS), lambda b, i: (0, 0))),
        out_shape=(jax.ShapeDtypeStruct((t, D_ATTN), F32), jax.ShapeDtypeStruct((t, 128), F32),
                   jax.ShapeDtypeStruct((t, 128), F32), jax.ShapeDtypeStruct((1, N_HEADS), F32)),
        scratch_shapes=[pltpu.VMEM((ATT_BLOCK, KV_HEADS * HEAD_DIM), F32), pltpu.VMEM((ATT_BLOCK, KV_HEADS * HEAD_DIM), F32)],
        compiler_params=_tc_params(("arbitrary", "arbitrary")),
    )(sinks, q, k, k, v, v, o, do, lse)


def _mix_forward_backward(x2, y2, z_ssm, attn, z_attn, p2, target2, w_glu, b_glu, w_out, g_post, w_gate, b_gate,
                          w_proj, tm):
    t = x2.shape[0]

    def body(x_ref, y_ref, zs_ref, at_ref, za_ref, p_ref, tg_ref,
             wglu_ref, bglu_ref, wout_ref, gpost_ref, wgate_ref, bgate_ref, wproj_ref,
             loss_ref, dh1_ref, dy_ref, dzs_ref, dat_ref, dza_ref,
             dwglu_ref, dbglu_ref, dwout_ref, dgpost_ref, dwgate_ref, dbgate_ref, dwproj_ref,
             dwout16_ref, dwgate16_ref, dwproj16_ref, dwglu16_ref):
        i = pl.program_id(0)
        gpost = gpost_ref[...]

        @pl.when(i == 0)
        def _():
            for ref in (dwglu_ref, dbglu_ref, dwout_ref, dgpost_ref, dwgate_ref, dbgate_ref, dwproj_ref, loss_ref):
                ref[...] = jnp.zeros(ref.shape, F32)

        def chain(rows):
            y = y_ref[rows, :]
            u3 = GELU_C * (y + GELU_K * y * y * y)
            th = jnp.tanh(u3)
            gl = 0.5 * y * (1.0 + th)
            a = _mm(gl, wglu_ref[...]) + bglu_ref[...]
            sa = _sigmoid(a)
            glu = gl * sa
            zs = zs_ref[rows, :]
            sgs = _sigmoid(zs)
            ssm_out = glu * (zs * sgs)
            za = za_ref[rows, :]
            sga = _sigmoid(za)
            at = at_ref[rows, :]
            attn_out = at * (za * sga)
            cat = jnp.concatenate([ssm_out, attn_out], axis=-1).astype(BF16)
            mixed = _mm(cat, wout_ref[...])
            r2 = lax.rsqrt(jnp.mean(mixed * mixed, axis=-1, keepdims=True) + EPS)
            nhat = mixed * r2
            h1 = x_ref[rows, :] + nhat * gpost
            gate = _sigmoid(_mm(h1, wgate_ref[...]) + bgate_ref[...])
            pv = p_ref[rows, :]
            pp = _mm(pv, wproj_ref[...])
            h2 = h1 + gate * pp
            err = h2 - tg_ref[rows, :]
            loss_part = jnp.sum(jnp.sum(err * err, axis=-1, keepdims=True), axis=0, keepdims=True) * (0.5 / D_MODEL)
            dh2 = err * (1.0 / D_MODEL)
            dgp = dh2 * pp * gate * (1.0 - gate)
            dpp = dh2 * gate
            dh1 = dh2 + _mm_nt(dgp, wgate_ref[...])
            dh1_ref[rows, :] = dh1
            dnhat = dh1 * gpost
            dmixed = r2 * (dnhat - nhat * jnp.mean(dnhat * nhat, axis=-1, keepdims=True))
            dcat = _mm_nt(dmixed, wout_ref[...])
            dso, dao = dcat[:, 0:D_SSM], dcat[:, D_SSM:]
            dat_ref[rows, :] = dao * (za * sga)
            dza_ref[rows, :] = (dao * at * (sga * (1.0 + za * (1.0 - sga)))).astype(BF16)
            dzs_ref[rows, :] = (dso * glu * (sgs * (1.0 + zs * (1.0 - sgs)))).astype(BF16)
            dglu = dso * (zs * sgs)
            da = dglu * gl * sa * (1.0 - sa)
            dgl = dglu * sa + _mm_nt(da, wglu_ref[...])
            dgelu = 0.5 * (1.0 + th) + 0.5 * y * (1.0 - th * th) * (GELU_C * (1.0 + 3.0 * GELU_K * y * y))
            dy_ref[rows, :] = dgl * dgelu
            return dict(gl=gl.astype(BF16), da=da.astype(BF16), cat=cat, dmixed=dmixed.astype(BF16),
                        h1=h1.astype(BF16), dgp=dgp.astype(BF16), pv=pv.astype(BF16), dpp=dpp.astype(BF16),
                        dbglu=jnp.sum(da, axis=0, keepdims=True), dgpost=jnp.sum(dh1 * nhat, axis=0, keepdims=True),
                        dbgate=jnp.sum(dgp, axis=0, keepdims=True), loss=loss_part)

        groups = [chain(slice(k * (tm // MIX_GROUPS), (k + 1) * (tm // MIX_GROUPS))) for k in range(MIX_GROUPS)]
        rows_of = lambda name: jnp.concatenate([g[name] for g in groups], axis=0)
        total = lambda name: sum(g[name] for g in groups)
        parts = (
            (dwglu_ref, _mm_tn(rows_of("gl"), rows_of("da"))), (dbglu_ref, total("dbglu")),
            (dwout_ref, _mm_tn(rows_of("cat"), rows_of("dmixed"))), (dgpost_ref, total("dgpost")),
            (dwgate_ref, _mm_tn(rows_of("h1"), rows_of("dgp"))), (dbgate_ref, total("dbgate")),
            (dwproj_ref, _mm_tn(rows_of("pv"), rows_of("dpp"))), (loss_ref, total("loss")),
        )

        for ref, val in parts:
            ref[...] += val

        @pl.when(i == t // tm - 1)
        def _():
            for ref16, ref in ((dwout16_ref, dwout_ref), (dwgate16_ref, dwgate_ref), (dwproj16_ref, dwproj_ref),
                               (dwglu16_ref, dwglu_ref)):
                def to16(r, ref16=ref16, ref=ref):
                    ref16[r, :] = ref[r, :].astype(BF16)

                _row_chunks(ref.shape[0], to16)

    row = lambda w: pl.BlockSpec((tm, w), lambda i: (i, 0))
    acc = lambda r, c, dt=F32: (_const_spec((r, c)), jax.ShapeDtypeStruct((r, c), dt))
    accs = [acc(D_SSM, D_SSM), acc(1, D_SSM), acc(D_MODEL, D_MODEL), acc(1, D_MODEL), acc(D_MODEL, D_MODEL),
            acc(1, D_MODEL), acc(D_PLE, D_MODEL),
            acc(D_MODEL, D_MODEL, BF16), acc(D_MODEL, D_MODEL, BF16), acc(D_PLE, D_MODEL, BF16), acc(D_SSM, D_SSM, BF16)]
    return pl.pallas_call(
        body, name="mix_forward_backward", grid=(t // tm,),
        in_specs=[row(D_MODEL), row(512), row(512), row(512), row(512), row(D_PLE), row(D_MODEL),
                  _const_spec((D_SSM, D_SSM)), _const_spec((1, D_SSM)), _const_spec((D_MODEL, D_MODEL)),
                  _const_spec((1, D_MODEL)), _const_spec((D_MODEL, D_MODEL)), _const_spec((1, D_MODEL)),
                  _const_spec((D_PLE, D_MODEL))],
        out_specs=(_const_spec((1, 1)), row(D_MODEL), row(512), row(512), row(512), row(512))
        + tuple(a[0] for a in accs),
        out_shape=(jax.ShapeDtypeStruct((1, 1), F32), jax.ShapeDtypeStruct((t, D_MODEL), F32),
                   jax.ShapeDtypeStruct((t, 512), F32),
                   jax.ShapeDtypeStruct((t, 512), BF16), jax.ShapeDtypeStruct((t, 512), F32),
                   jax.ShapeDtypeStruct((t, 512), BF16)) + tuple(a[1] for a in accs),
        compiler_params=_tc_params(("arbitrary",)),
    )(x2, y2, z_ssm, attn, z_attn, p2, target2, w_glu, b_glu, w_out, g_post, w_gate, b_gate, w_proj)


def _in_backward(x2, dh1, du, dz_ssm, dq, dk, dv, dz_attn, g_pre, w_in, tm):
    t = x2.shape[0]

    def body(x_ref, dh1_ref, du_ref, dzs_ref, dq_ref, dk_ref, dv_ref, dza_ref, g_ref, w_ref,
             gx_ref, dw_ref, dg_ref, dw16_ref):
        i = pl.program_id(0)
        xv = x_ref[...]
        r = lax.rsqrt(jnp.mean(xv * xv, axis=-1, keepdims=True) + EPS)
        xhat = xv * r
        g = g_ref[...]
        hn = (xhat * g).astype(BF16)
        dproj = jnp.concatenate([du_ref[...].astype(BF16), dzs_ref[...].astype(BF16), dq_ref[...].astype(BF16),
                                 dk_ref[...].astype(BF16), dv_ref[...].astype(BF16), dza_ref[...].astype(BF16)],
                                axis=-1)
        dhn = _mm(dproj, w_ref[...])
        dxhat = dhn * g
        gx_ref[...] = dh1_ref[...] + r * (dxhat - xhat * jnp.mean(dxhat * xhat, axis=-1, keepdims=True))
        @pl.when(i == 0)
        def _():
            dw_ref[...] = jnp.zeros((D_IN, D_MODEL), F32)
            dg_ref[...] = jnp.zeros((1, D_MODEL), F32)

        dw_ref[...] += _mm_tn(dproj, hn)
        dg_ref[...] += jnp.sum(dhn * xhat, axis=0, keepdims=True)

        @pl.when(i == t // tm - 1)
        def _():
            def to16(r):
                dw16_ref[r, :] = dw_ref[r, :].astype(BF16)

            _row_chunks(D_IN, to16)

    row = lambda w: pl.BlockSpec((tm, w), lambda i: (i, 0))
    return pl.pallas_call(
        body, name="in_backward", grid=(t // tm,),
        in_specs=[row(D_MODEL), row(D_MODEL), row(512), row(512), row(512), row(128), row(128), row(512),
                  _const_spec((1, D_MODEL)), _const_spec((D_IN, D_MODEL))],
        out_specs=(row(D_MODEL), _const_spec((D_IN, D_MODEL)), _const_spec((1, D_MODEL)),
                   _const_spec((D_IN, D_MODEL))),
        out_shape=(jax.ShapeDtypeStruct((t, D_MODEL), F32), jax.ShapeDtypeStruct((D_IN, D_MODEL), F32),
                   jax.ShapeDtypeStruct((1, D_MODEL), F32), jax.ShapeDtypeStruct((D_IN, D_MODEL), BF16)),
        compiler_params=_tc_params(("arbitrary",)),
    )(x2, dh1, du, dz_ssm, dq, dk, dv, dz_attn, g_pre, w_in)


def _local_step(x, p, target, pre_norm_g, w_in, ssm_lam_re, ssm_lam_im, ssm_log_step, ssm_b_re, ssm_b_im, ssm_c_re,
                ssm_c_im, ssm_d, ssm_b_glu, attn_sinks, post_norm_g, pl_b_gate, late):
    bl, seq, _ = x.shape
    seg = seq // N_SEG
    nb = seq // ATT_BLOCK
    t = bl * seq
    x2 = x.reshape(t, D_MODEL)
    p2 = p.reshape(t, D_PLE)
    tg2 = target.reshape(t, D_MODEL)

    lam_re, lam_im = ssm_lam_re, ssm_lam_im
    log_step = ssm_log_step.reshape(SSM_G, 1)
    a_re_row, a_im_row, pw_re, pw_im, bcat, bcat_t, ccat, ccat_t = _ssm_prep(
        lam_re, lam_im, log_step, ssm_b_re, ssm_b_im, ssm_c_re, ssm_c_im, seg)
    d_row = ssm_d.reshape(1, D_SSM)

    segments = lambda a: a.reshape(bl, N_SEG, seg, D_SSM)
    u, z_ssm, q, k, v, z_attn = _in_proj(x2, pre_norm_g.reshape(1, D_MODEL), w_in, min(TOKEN_TILE_WIDE, t))
    (y, states, carries), gathered = _ssm_forward(
        segments(u), bcat, ccat, a_re_row, a_im_row, pw_re, pw_im, d_row, late, seg)
    w_out, w_gate, w_proj, w_glu = (_gathered_to_full(n, g) for n, g in zip(LATE_NAMES, gathered))
    sinks = attn_sinks.reshape(N_HEADS)
    attn, lse = _attn_forward(q, k, v, sinks, bl, nb)
    (loss, dh1, dy, dz_ssm, dattn, dz_attn, d_w_glu, d_b_glu, d_w_out, d_g_post, d_w_gate, d_b_gate,
     d_w_proj, *late16) = _mix_forward_backward(
        x2, y.reshape(t, D_SSM), z_ssm, attn, z_attn, p2, tg2, w_glu,
        ssm_b_glu.reshape(1, D_SSM), w_out, post_norm_g.reshape(1, D_MODEL), w_gate, pl_b_gate.reshape(1, D_MODEL),
        w_proj, min(TOKEN_TILE, t))
    owned = lambda ds: [_full_to_owned(n, d) for n, d in zip(LATE_NAMES, ds)]
    dq, dk, dv, d_sinks = _attn_backward(q, k, v, attn, dattn, lse, sinks, bl, nb)
    (du, d_bcat, d_ccat_t, da_re, da_im, d_d), late_grads = _ssm_backward(
        segments(u), segments(dy), states, carries, bcat_t, ccat_t, a_re_row, a_im_row, pw_re, pw_im,
        d_row, owned(late16), owned((d_w_out, d_w_gate, d_w_proj, d_w_glu)), seg)
    grad_x, d_w_in, d_g_pre, d_w_in16 = _in_backward(
        x2, dh1, du.reshape(t, D_SSM), dz_ssm, dq, dk, dv, dz_attn, pre_norm_g.reshape(1, D_MODEL), w_in,
        min(TOKEN_TILE_WIDE, t))
    d_lam_re, d_lam_im, d_ls, d_b_re, d_b_im, d_c_re, d_c_im = _ssm_param_grads(
        lam_re, lam_im, log_step, ssm_b_re, ssm_b_im, da_re.reshape(SSM_G, SSM_N), da_im.reshape(SSM_G, SSM_N),
        d_bcat, d_ccat_t)
    grads = {
        "pre_norm_g": d_g_pre, "w_in": d_w_in, "w_in16": d_w_in16, "ssm_lam_re": d_lam_re, "ssm_lam_im": d_lam_im,
        "ssm_log_step": d_ls, "ssm_b_re": d_b_re, "ssm_b_im": d_b_im, "ssm_c_re": d_c_re, "ssm_c_im": d_c_im,
        "ssm_d": d_d, "ssm_b_glu": d_b_glu, "attn_sinks": d_sinks, "post_norm_g": d_g_post, "pl_b_gate": d_b_gate,
    }
    return loss, grad_x.reshape(bl, seq, D_MODEL), grads, late_grads


LATE_NAMES = ("w_out", "pl_w_gate", "pl_w_proj", "ssm_w_glu")
BIG_NAMES = ("w_in",) + LATE_NAMES
COL_SHARDED = {"w_in": D_IN // N_DEV, "pl_w_proj": D_MODEL // N_DEV}
WEIGHT_NAMES = ("pre_norm_g", "w_in", "ssm_lam_re", "ssm_lam_im", "ssm_log_step", "ssm_b_re", "ssm_b_im", "ssm_c_re",
                "ssm_c_im", "ssm_d", "ssm_w_glu", "ssm_b_glu", "attn_sinks", "w_out", "post_norm_g", "pl_w_proj",
                "pl_w_gate", "pl_b_gate")


TRANSPOSED = {"w_in": (0, 1), "ssm_b_re": (1, 2), "ssm_b_im": (1, 2)}


def _kernel_form(name, a):
    a = a[0]
    if name in TRANSPOSED:
        a = jnp.swapaxes(a, *TRANSPOSED[name])
    if name in ("ssm_b_re", "ssm_b_im", "ssm_c_re", "ssm_c_im"):
        a = a.reshape(SSM_G * SSM_P, SSM_N)
    return a


def _given_form(name, a, shape):
    if name in TRANSPOSED:
        i, j = TRANSPOSED[name]
        swapped = list(shape[1:])
        swapped[i], swapped[j] = swapped[j], swapped[i]
        return jnp.swapaxes(a.reshape(swapped), i, j).reshape(shape)
    return a.reshape(shape)


def _gathered_to_full(name, g):
    _, rows, cols = g.shape
    if name in COL_SHARDED:
        return jnp.swapaxes(g, 0, 1).reshape(rows, N_DEV * cols)
    return g.reshape(N_DEV * rows, cols)


def _full_to_owned(name, full):
    if name in COL_SHARDED:
        return jnp.swapaxes(full.reshape(full.shape[0], N_DEV, COL_SHARDED[name]), 0, 1)
    return full.reshape(N_DEV, full.shape[0] // N_DEV, full.shape[1])


def kernel(x, p, pre_norm_g, w_in, ssm_lam_re, ssm_lam_im, ssm_log_step, ssm_b_re, ssm_b_im, ssm_c_re, ssm_c_im, ssm_d, ssm_w_glu, ssm_b_glu, attn_sinks, w_out, post_norm_g, pl_w_proj, pl_w_gate, pl_b_gate, loss_target, m_pre_norm_g, m_w_in, m_ssm_lam_re, m_ssm_lam_im, m_ssm_log_step, m_ssm_b_re, m_ssm_b_im, m_ssm_c_re, m_ssm_c_im, m_ssm_d, m_ssm_w_glu, m_ssm_b_glu, m_attn_sinks, m_w_out, m_post_norm_g, m_pl_w_proj, m_pl_w_gate, m_pl_b_gate, v_pre_norm_g, v_w_in, v_ssm_lam_re, v_ssm_lam_im, v_ssm_log_step, v_ssm_b_re, v_ssm_b_im, v_ssm_c_re, v_ssm_c_im, v_ssm_d, v_ssm_w_glu, v_ssm_b_glu, v_attn_sinks, v_w_out, v_post_norm_g, v_pl_w_proj, v_pl_w_gate, v_pl_b_gate):
    w = dict(pre_norm_g=pre_norm_g, w_in=w_in, ssm_lam_re=ssm_lam_re, ssm_lam_im=ssm_lam_im, ssm_log_step=ssm_log_step,
             ssm_b_re=ssm_b_re, ssm_b_im=ssm_b_im, ssm_c_re=ssm_c_re, ssm_c_im=ssm_c_im, ssm_d=ssm_d, ssm_w_glu=ssm_w_glu,
             ssm_b_glu=ssm_b_glu, attn_sinks=attn_sinks, w_out=w_out, post_norm_g=post_norm_g, pl_w_proj=pl_w_proj,
             pl_w_gate=pl_w_gate, pl_b_gate=pl_b_gate)
    m = dict(pre_norm_g=m_pre_norm_g, w_in=m_w_in, ssm_lam_re=m_ssm_lam_re, ssm_lam_im=m_ssm_lam_im,
             ssm_log_step=m_ssm_log_step, ssm_b_re=m_ssm_b_re, ssm_b_im=m_ssm_b_im, ssm_c_re=m_ssm_c_re,
             ssm_c_im=m_ssm_c_im, ssm_d=m_ssm_d, ssm_w_glu=m_ssm_w_glu, ssm_b_glu=m_ssm_b_glu, attn_sinks=m_attn_sinks,
             w_out=m_w_out, post_norm_g=m_post_norm_g, pl_w_proj=m_pl_w_proj, pl_w_gate=m_pl_w_gate,
             pl_b_gate=m_pl_b_gate)
    v = dict(pre_norm_g=v_pre_norm_g, w_in=v_w_in, ssm_lam_re=v_ssm_lam_re, ssm_lam_im=v_ssm_lam_im,
             ssm_log_step=v_ssm_log_step, ssm_b_re=v_ssm_b_re, ssm_b_im=v_ssm_b_im, ssm_c_re=v_ssm_c_re,
             ssm_c_im=v_ssm_c_im, ssm_d=v_ssm_d, ssm_w_glu=v_ssm_w_glu, ssm_b_glu=v_ssm_b_glu, attn_sinks=v_attn_sinks,
             w_out=v_w_out, post_norm_g=v_post_norm_g, pl_w_proj=v_pl_w_proj, pl_w_gate=v_pl_w_gate,
             pl_b_gate=v_pl_b_gate)
    kf = lambda d: {n: _kernel_form(n, a) for n, a in d.items()}
    wk, mk, vk = kf(w), kf(m), kf(v)

    (gathered,) = _allgather_weights([wk["w_in"]])
    loss, grad_x, grads, g_late = _local_step(
        x, p[0], loss_target, wk["pre_norm_g"], gathered.reshape(D_IN, D_MODEL), wk["ssm_lam_re"], wk["ssm_lam_im"],
        wk["ssm_log_step"], wk["ssm_b_re"], wk["ssm_b_im"], wk["ssm_c_re"], wk["ssm_c_im"], wk["ssm_d"],
        wk["ssm_b_glu"], wk["attn_sinks"], wk["post_norm_g"], wk["pl_b_gate"], [wk[n] for n in LATE_NAMES])

    owned = lambda g: g.reshape(N_DEV, D_IN // N_DEV, D_MODEL)
    tiny_form = lambda d: [d[n].reshape(rows, cols) for n, rows, cols in TINY]
    med_form = lambda d: [d[n].reshape(N_DEV, rows // N_DEV, cols) for n, rows, cols in MEDIUM]
    g_big, loss, g_tiny, g_med = _reduce_final(
        [owned(grads["w_in16"])], [owned(grads["w_in"])], loss, tiny_form(grads), med_form(grads))
    names = BIG_NAMES + tuple(n for n, _, _ in TINY + MEDIUM)
    form = lambda d: [d[n] for n in BIG_NAMES] + tiny_form(d) + med_form(d)
    updated = _adamw_update(g_big + g_late + g_tiny + g_med, form(wk), form(mk), form(vk))
    vals = dict(zip(names, updated))
    results = [[_given_form(n, vals[n][kind], w[n].shape) for n in WEIGHT_NAMES] for kind in range(4)]
    return (loss.reshape(()), grad_x, *results[0], *results[1], *results[2], *results[3])
```

```python
import functools
import math

import jax
import jax.numpy as jnp
from jax import lax
from jax.experimental import pallas as pl
from jax.experimental.pallas import tpu as pltpu

F32 = jnp.float32
BF16 = jnp.bfloat16

D_MODEL = 1024
D_SSM = 512
D_ATTN = 512
SSM_P = 16
SSM_G = 32
SSM_N = 64
N_HEADS = 8
KV_HEADS = 2
Q_PER_KV = 4
HEAD_DIM = 64
ATT_BLOCK = 128
D_PLE = 256
D_IN = 2304
EPS = 1e-6
N_DEV = 8
N_SEG = 8
G_TILE = 8
N_GT = SSM_G // G_TILE
CH_T = G_TILE * SSM_P
ST_T = G_TILE * SSM_N
N_STATE = SSM_G * SSM_N
SCAN_UNROLL = 4
MIX_GROUPS = 1
TOKEN_TILE = 256
TOKEN_TILE_WIDE = 512
LANES = 128
VMEM_LIMIT = 60 * 1024 * 1024

ADAM_LR = 0.001
ADAM_B1 = 0.9
ADAM_B2 = 0.999
ADAM_EPS = 1e-08
ADAM_WD = 0.01
ADAM_STEP = 10

GELU_C = math.sqrt(2.0 / math.pi)
GELU_K = 0.044715
ATT_SCALE = 1.0 / math.sqrt(HEAD_DIM)
NEG_INF = float("-inf")


def _mm(a, b):
    return jnp.dot(a.astype(BF16), b.astype(BF16), preferred_element_type=F32)


def _mm_nt(a, b):
    return lax.dot_general(a.astype(BF16), b.astype(BF16), (((1,), (1,)), ((), ())), preferred_element_type=F32)


def _mm_tn(a, b):
    return lax.dot_general(a.astype(BF16), b.astype(BF16), (((0,), (0,)), ((), ())), preferred_element_type=F32)


def _sigmoid(x):
    return 1.0 / (1.0 + jnp.exp(-x))


def _tc_params(sem):
    return pltpu.CompilerParams(dimension_semantics=sem, vmem_limit_bytes=VMEM_LIMIT)


def _const_spec(shape):
    nd = len(shape)
    return pl.BlockSpec(shape, lambda *_: (0,) * nd)


def _mesh_pos():
    return lax.axis_index("x"), lax.axis_index("y"), lax.axis_index("c")


ROW_CHUNKS = (64, 32, 16)


def _row_chunk(nrows):
    return next((c for c in ROW_CHUNKS if nrows % c == 0), None)


def _row_chunks(nrows, fn, chunk=None, init=None):
    chunk = chunk or _row_chunk(nrows)

    def step(i, carry):
        rows = pl.ds(pl.multiple_of(i * chunk, chunk), chunk)
        if init is None:
            fn(rows)
            return carry
        return fn(rows, carry)

    return lax.fori_loop(0, nrows // chunk, step, 0 if init is None else init)


def _slot(px, py, pc):
    return 4 * px + 2 * py + pc


def _allgather_weights(shards, work=None, work_inputs=(), work_out_shapes=()):
    n, n_wi, n_wo = len(shards), len(work_inputs), len(work_out_shapes)

    def body(*refs):
        srcs, w_in_refs = refs[:n], refs[n:n + n_wi]
        outs, w_out_refs = refs[n + n_wi:2 * n + n_wi], refs[2 * n + n_wi:2 * n + n_wi + n_wo]
        send_sems, recv_sems = refs[2 * n + n_wi + n_wo:]
        x, y, c = _mesh_pos()
        me, sibling = (x, y, c), (x, y, 1 - c)
        chips = [(1 - x, y), (x, 1 - y), (1 - x, 1 - y)]

        def copy(a, k, block, to):
            blk = outs[a].at[_slot(*block)]
            return pltpu.make_async_remote_copy(
                src_ref=blk, dst_ref=blk, send_sem=send_sems.at[7 * a + k], recv_sem=recv_sems.at[7 * a + k],
                device_id=to, device_id_type=pl.DeviceIdType.MESH)

        sends = []
        for a in range(n):
            mine = outs[a].at[_slot(*me)]

            def cast(r, mine=mine, src=srcs[a]):
                mine[r, :] = src[r, :].astype(BF16)

            _row_chunks(srcs[a].shape[0], cast)
            first = [copy(a, 0, me, sibling)] + [copy(a, 1 + j, me, (*chip, c)) for j, chip in enumerate(chips)]
            for cp in first:
                cp.start()
            sends += first
        if work is not None:
            work(w_in_refs, w_out_refs)
        for a in range(n):
            for j, chip in enumerate(chips):
                copy(a, 1 + j, (*chip, c), me).wait_recv()
                fwd = copy(a, 4 + j, (*chip, c), sibling)
                fwd.start()
                sends.append(fwd)
        for a in range(n):
            copy(a, 0, sibling, me).wait_recv()
            for j, chip in enumerate(chips):
                copy(a, 4 + j, (*chip, 1 - c), me).wait_recv()
        for cp in sends:
            cp.wait_send()

    vm = pl.BlockSpec(memory_space=pltpu.VMEM)
    res = pl.pallas_call(
        body, name="allgather_weights",
        out_shape=tuple(jax.ShapeDtypeStruct((N_DEV,) + s.shape, BF16) for s in shards) + tuple(work_out_shapes),
        in_specs=[vm] * (n + n_wi), out_specs=(vm,) * (n + n_wo),
        scratch_shapes=[pltpu.SemaphoreType.DMA((7 * n,)), pltpu.SemaphoreType.DMA((7 * n,))],
        compiler_params=pltpu.CompilerParams(vmem_limit_bytes=VMEM_LIMIT),
    )(*shards, *work_inputs)
    return list(res[:n]), list(res[n:])


def _adamw(w, g, m, v):
    m = ADAM_B1 * m + (1.0 - ADAM_B1) * g
    v = ADAM_B2 * v + (1.0 - ADAM_B2) * (g * g)
    m_hat = m / (1.0 - ADAM_B1 ** ADAM_STEP)
    v_hat = v / (1.0 - ADAM_B2 ** ADAM_STEP)
    delta = -ADAM_LR * (m_hat / (jnp.sqrt(v_hat) + ADAM_EPS) + ADAM_WD * w)
    return delta, m, v


def _remote(src, dst, send_sems, recv_sems, k, to):
    return pltpu.make_async_remote_copy(src_ref=src, dst_ref=dst, send_sem=send_sems.at[k], recv_sem=recv_sems.at[k],
                                        device_id=to, device_id_type=pl.DeviceIdType.MESH)


def _big_reduce_phases(g16_r, go_r, outs, send2, recv1, recv2, s_send, s_recv):
    n = len(g16_r)
    x, y, c = _mesh_pos()
    sibling = (x, y, 1 - c)
    chips = [(1 - x, y), (x, 1 - y), (1 - x, 1 - y)]
    all_chips = [(x, y)] + chips
    lvl1 = []
    for a in range(n):
        cps = [_remote(g16_r[a].at[_slot(*chip, 1 - c)], recv1[a].at[j], s_send, s_recv, 7 * a + j, sibling)
               for j, chip in enumerate(all_chips)]
        for cp in cps:
            cp.start()
        lvl1.append(cps)
    yield
    lvl2 = []
    for a in range(n):
        for cp in lvl1[a]:
            cp.wait_recv()
        og = outs[a]

        def partials(r, a=a, og=og):
            og[r, :] = go_r[a][r, :] + recv1[a][0, r, :].astype(F32)
            for j, chip in enumerate(chips):
                mine16 = g16_r[a][_slot(*chip, c), r, :].astype(F32)
                send2[a][j, r, :] = (mine16 + recv1[a][1 + j, r, :].astype(F32)).astype(BF16)

        _row_chunks(go_r[a].shape[0], partials)
        cps = [_remote(send2[a].at[j], recv2[a].at[j], s_send, s_recv, 7 * a + 4 + j, (*chip, c))
               for j, chip in enumerate(chips)]
        for cp in cps:
            cp.start()
        lvl2.append(cps)
    yield
    for a in range(n):
        for cp in lvl2[a]:
            cp.wait_recv()
        og = outs[a]

        def total(r, a=a, og=og):
            g = og[r, :]
            for j in range(3):
                g = g + recv2[a][j, r, :].astype(F32)
            og[r, :] = g

        _row_chunks(go_r[a].shape[0], total)
    yield
    for cps in lvl1 + lvl2:
        for cp in cps:
            cp.wait_send()


def _adamw_update(g, w, m, v):
    n = len(g)

    def body(*refs):
        g_r, w_r, m_r, v_r = (refs[i * n:(i + 1) * n] for i in range(4))
        outs = refs[4 * n:]
        for a in range(n):
            og, od, om, ov = outs[4 * a:4 * a + 4]

            def update(idx, a=a, og=og, od=od, om=om, ov=ov):
                gv = g_r[a][idx]
                d, nm, nv = _adamw(w_r[a][idx], gv, m_r[a][idx], v_r[a][idx])
                og[idx] = gv
                od[idx] = d
                om[idx] = nm
                ov[idx] = nv

            shape = g_r[a].shape
            if len(shape) == 3:
                for b in range(shape[0]):
                    update(b)
            elif _row_chunk(shape[0]) is not None:
                _row_chunks(shape[0], update)
            else:
                update(Ellipsis)

    vm = pl.BlockSpec(memory_space=pltpu.VMEM)
    res = pl.pallas_call(
        body, name="adamw_update",
        out_shape=tuple(jax.ShapeDtypeStruct(t.shape, F32) for t in g for _ in range(4)),
        in_specs=[vm] * (4 * n), out_specs=(vm,) * (4 * n),
        compiler_params=pltpu.CompilerParams(vmem_limit_bytes=VMEM_LIMIT),
    )(*g, *w, *m, *v)
    return [res[4 * a:4 * a + 4] for a in range(n)]


TINY = (("pre_norm_g", 1, 1024), ("post_norm_g", 1, 1024), ("pl_b_gate", 1, 1024), ("ssm_d", 1, 512),
        ("ssm_b_glu", 1, 512), ("ssm_log_step", 1, 32), ("attn_sinks", 1, 8), ("ssm_lam_re", 32, 64),
        ("ssm_lam_im", 32, 64))
MEDIUM = (("ssm_b_re", SSM_G * SSM_P, SSM_N), ("ssm_b_im", SSM_G * SSM_P, SSM_N), ("ssm_c_re", SSM_G * SSM_P, SSM_N),
          ("ssm_c_im", SSM_G * SSM_P, SSM_N))


def _stage_rows():
    offs, r = {}, 0
    for name, rows, cols in TINY + (("loss", 1, 1),):
        if rows > 1:
            r = -(-r // 8) * 8
        offs[name] = r
        r += rows if rows > 1 else max(cols // LANES, 1)
    return offs, -(-r // 8) * 8


def _reduce_final(g16, g32, loss, g_tiny, g_med):
    nb_, nt, nm_ = len(g16), len(TINY), len(MEDIUM)
    offs, stage_rows = _stage_rows()

    def body(*refs):
        g16_r, go_r = refs[:nb_], refs[nb_:2 * nb_]
        base = 2 * nb_
        loss_r, gt, gm = refs[base], refs[base + 1:base + 1 + nt], refs[base + 1 + nt:base + 1 + nt + nm_]
        base += 1 + nt + nm_
        out_b = refs[base:base + nb_]
        base += nb_
        loss_o, out_t, out_m = refs[base], refs[base + 1:base + 1 + nt], refs[base + 1 + nt:base + 1 + nt + nm_]
        base += 1 + nt + nm_
        send2_b, recv1_b, recv2_b = (refs[base + i * nb_:base + (i + 1) * nb_] for i in range(3))
        base += 3 * nb_
        stage = refs[base]
        recv1, part, recv2 = (refs[base + 1 + i * nm_:base + 1 + (i + 1) * nm_] for i in range(3))
        bs_send, bs_recv, s_send, s_recv, own_sems = refs[base + 1 + 3 * nm_:base + 6 + 3 * nm_]
        own32 = refs[base + 6 + 3 * nm_:]
        me = _slot(*_mesh_pos())
        fetch = [pltpu.make_async_copy(go_r[a].at[me], own32[a], own_sems.at[a]) for a in range(nb_)]
        for cp in fetch:
            cp.start()
        big = _big_reduce_phases(g16_r, own32, out_b, send2_b, recv1_b, recv2_b, bs_send, bs_recv)
        small = small_phases(loss_r, gt, gm, loss_o, out_t, out_m, stage, recv1, part, recv2, s_send, s_recv)
        next(big)
        next(small)
        for cp in fetch:
            cp.wait()
        next(big)
        for _ in small:
            pass
        for _ in big:
            pass

    def small_phases(loss_r, gt, gm, loss_o, out_t, out_m, stage, recv1, part, recv2, s_send, s_recv):
        x, y, c = _mesh_pos()
        me = _slot(x, y, c)
        sibling = (x, y, 1 - c)
        chips = [(1 - x, y), (x, 1 - y), (1 - x, 1 - y)]
        all_chips = [(x, y)] + chips
        peers = [sibling] + [(*chip, c) for chip in chips] + [(*chip, 1 - c) for chip in chips]
        sem = iter(range(7 + 14 * nm_))
        lvl1 = []
        for a in range(nm_):
            cps = [_remote(gm[a].at[_slot(*chip, 1 - c)], recv1[a].at[j], s_send, s_recv, next(sem), sibling)
                   for j, chip in enumerate(all_chips)]
            for cp in cps:
                cp.start()
            lvl1.append(cps)
        mine = stage.at[me]
        mine[...] = jnp.zeros((stage_rows, LANES), F32)
        for (name, rows, cols), ref in zip(TINY + (("loss", 1, 1),), gt + (loss_r,)):
            r0 = offs[name]
            if rows > 1:
                mine[r0:r0 + rows, 0:cols] = ref[...]
            elif cols >= LANES:
                for i in range(cols // LANES):
                    mine[r0 + i:r0 + i + 1, :] = ref[:, i * LANES:(i + 1) * LANES]
            else:
                mine[r0:r0 + 1, 0:cols] = ref[...]
        tiny_cps = [_remote(mine, mine, s_send, s_recv, next(sem), peer) for peer in peers]
        for cp in tiny_cps:
            cp.start()
        yield
        lvl2 = []
        for a in range(nm_):
            for cp in lvl1[a]:
                cp.wait_recv()
            for j, chip in enumerate(all_chips):
                part[a][j] = gm[a][_slot(*chip, c)] + recv1[a][j]
            cps = [_remote(part[a].at[1 + j], recv2[a].at[j], s_send, s_recv, next(sem), (*chip, c))
                   for j, chip in enumerate(chips)]
            for cp in cps:
                cp.start()
            lvl2.append(cps)
        yield
        lvl3 = []
        for a in range(nm_):
            for cp in lvl2[a]:
                cp.wait_recv()
            blk = out_m[a].at[me]
            blk[...] = ((part[a][0] + recv2[a][0]) + recv2[a][1]) + recv2[a][2]
            cps = [_remote(blk, blk, s_send, s_recv, next(sem), peer) for peer in peers]
            for cp in cps:
                cp.start()
            lvl3.append(cps)
        yield
        for cp in tiny_cps:
            cp.wait_recv()
        tot = stage[0]
        for d in range(1, N_DEV):
            tot = tot + stage[d]
        loss_o[...] = tot[offs["loss"]:offs["loss"] + 1, 0:1]
        for k, (name, rows, cols) in enumerate(TINY):
            r0 = offs[name]
            if rows > 1:
                out_t[k][...] = tot[r0:r0 + rows, 0:cols]
            elif cols >= LANES:
                for i in range(cols // LANES):
                    out_t[k][:, i * LANES:(i + 1) * LANES] = tot[r0 + i:r0 + i + 1, :]
            else:
                out_t[k][...] = tot[r0:r0 + 1, 0:cols]
        for cps in lvl3:
            for cp in cps:
                cp.wait_recv()
        for cps in lvl1 + lvl2 + lvl3 + [tiny_cps]:
            for cp in cps:
                cp.wait_send()

    vmem = pl.BlockSpec(memory_space=pltpu.VMEM)
    t_shapes = [jax.ShapeDtypeStruct((rows, cols), F32) for _, rows, cols in TINY]
    m_shapes = [jax.ShapeDtypeStruct((N_DEV, rows // N_DEV, cols), F32) for _, rows, cols in MEDIUM]
    blk = [(rows // N_DEV, cols) for _, rows, cols in MEDIUM]
    shard = [g.shape[1:] for g in g16]
    scratch = ([pltpu.VMEM((3,) + s, BF16) for s in shard] + [pltpu.VMEM((4,) + s, BF16) for s in shard]
               + [pltpu.VMEM((3,) + s, BF16) for s in shard]
               + [pltpu.VMEM((N_DEV, stage_rows, LANES), F32)]
               + [pltpu.VMEM((4,) + b, F32) for b in blk] + [pltpu.VMEM((4,) + b, F32) for b in blk]
               + [pltpu.VMEM((3,) + b, F32) for b in blk]
               + [pltpu.SemaphoreType.DMA((7 * nb_,)), pltpu.SemaphoreType.DMA((7 * nb_,)),
                  pltpu.SemaphoreType.DMA((7 + 14 * nm_,)), pltpu.SemaphoreType.DMA((7 + 14 * nm_,)),
                  pltpu.SemaphoreType.DMA((nb_,))]
               + [pltpu.VMEM(s, F32) for s in shard])
    n_out = nb_ + 1 + nt + nm_
    res = pl.pallas_call(
        body, name="reduce_final",
        out_shape=tuple(jax.ShapeDtypeStruct(s, F32) for s in shard) + (jax.ShapeDtypeStruct((1, 1), F32),)
        + tuple(t_shapes) + tuple(m_shapes),
        in_specs=[vmem] * nb_ + [pl.BlockSpec(memory_space=pl.ANY)] * nb_ + [vmem] * (1 + nt + nm_),
        out_specs=(vmem,) * n_out, scratch_shapes=scratch,
        compiler_params=pltpu.CompilerParams(vmem_limit_bytes=VMEM_LIMIT),
    )(*g16, *g32, loss, *g_tiny, *g_med)
    return list(res[:nb_]), res[nb_], list(res[nb_ + 1:nb_ + 1 + nt]), list(res[nb_ + 1 + nt:])


def _gather_phases(shard_r, gath, cast, send_sems, recv_sems, local_sems):
    n = len(shard_r)
    x, y, c = _mesh_pos()
    me, sibling = (x, y, c), (x, y, 1 - c)
    chips = [(1 - x, y), (x, 1 - y), (1 - x, 1 - y)]

    def own(a, k, to):
        return _remote(cast[a], gath[a].at[_slot(*me)], send_sems, recv_sems, 7 * a + k, to)

    def passed(a, k, block, to):
        blk = gath[a].at[_slot(*block)]
        return _remote(blk, blk, send_sems, recv_sems, 7 * a + k, to)

    def keep(a):
        return pltpu.make_async_copy(cast[a], gath[a].at[_slot(*me)], local_sems.at[a])

    def start():
        for a in range(n):
            def to16(r, a=a):
                cast[a][r, :] = shard_r[a][r, :].astype(BF16)

            _row_chunks(shard_r[a].shape[0], to16)
            keep(a).start()
            own(a, 0, sibling).start()
            for j, chip in enumerate(chips):
                own(a, 1 + j, (*chip, c)).start()

    def relay():
        for a in range(n):
            for j, chip in enumerate(chips):
                passed(a, 1 + j, (*chip, c), me).wait_recv()
                passed(a, 4 + j, (*chip, c), sibling).start()

    def finish():
        for a in range(n):
            passed(a, 0, sibling, me).wait_recv()
            for j, chip in enumerate(chips):
                passed(a, 4 + j, (*chip, 1 - c), me).wait_recv()
            own(a, 0, sibling).wait_send()
            for j, chip in enumerate(chips):
                own(a, 1 + j, (*chip, c)).wait_send()
                passed(a, 4 + j, (*chip, c), sibling).wait_send()
            keep(a).wait()

    return start, relay, finish


def _gather_operands(shards):
    n = len(shards)
    return ((pl.BlockSpec(memory_space=pl.ANY),) * n,
            tuple(jax.ShapeDtypeStruct((N_DEV,) + s.shape, BF16) for s in shards),
            [pltpu.VMEM(s.shape, BF16) for s in shards]
            + [pltpu.SemaphoreType.DMA((7 * n,)), pltpu.SemaphoreType.DMA((7 * n,)), pltpu.SemaphoreType.DMA((n,))])


def _hosted_reduce_phases(g16_r, g32_r, red, own16, recv1, send2, recv2, own32, s_send, s_recv, s_local):
    n = len(g16_r)
    x, y, c = _mesh_pos()
    sibling = (x, y, 1 - c)
    chips = [(1 - x, y), (x, 1 - y), (1 - x, 1 - y)]
    all_chips = [(x, y)] + chips

    def lvl1(a, j):
        return _remote(g16_r[a].at[_slot(*all_chips[j], 1 - c)], recv1[a].at[j], s_send, s_recv, 7 * a + j, sibling)

    def lvl2(a, j):
        return _remote(send2[a].at[j], recv2[a].at[j], s_send, s_recv, 7 * a + 4 + j, (*chips[j], c))

    def mine(a, j):
        if j == 3:
            return pltpu.make_async_copy(g32_r[a].at[_slot(x, y, c)], own32[a], s_local.at[4 * a + j])
        return pltpu.make_async_copy(g16_r[a].at[_slot(*chips[j], c)], own16[a].at[j], s_local.at[4 * a + j])

    def start():
        for a in range(n):
            for j in range(4):
                mine(a, j).start()
            for j in range(4):
                lvl1(a, j).start()

    def middle():
        for a in range(n):
            for j in range(4):
                mine(a, j).wait()
            for j in range(4):
                lvl1(a, j).wait_recv()

            def partials(r, a=a):
                red[a][r, :] = own32[a][r, :] + recv1[a][0, r, :].astype(F32)
                for j in range(3):
                    send2[a][j, r, :] = (own16[a][j, r, :].astype(F32) + recv1[a][1 + j, r, :].astype(F32)).astype(BF16)

            _row_chunks(own32[a].shape[0], partials)
            for j in range(3):
                lvl2(a, j).start()

    def total():
        for a in range(n):
            for j in range(3):
                lvl2(a, j).wait_recv()

            def add(r, a=a):
                g = red[a][r, :]
                for j in range(3):
                    g = g + recv2[a][j, r, :].astype(F32)
                red[a][r, :] = g

            _row_chunks(own32[a].shape[0], add)

    def finish():
        for a in range(n):
            for j in range(4):
                lvl1(a, j).wait_send()
            for j in range(3):
                lvl2(a, j).wait_send()

    return start, middle, total, finish


def _hosted_reduce_operands(g16, const_spec):
    n = len(g16)
    shard = [g.shape[1:] for g in g16]
    return ([pl.BlockSpec(memory_space=pl.ANY)] * (2 * n),
            tuple(const_spec(s) for s in shard),
            tuple(jax.ShapeDtypeStruct(s, F32) for s in shard),
            [pltpu.VMEM((3,) + s, BF16) for s in shard] + [pltpu.VMEM((4,) + s, BF16) for s in shard]
            + [pltpu.VMEM((3,) + s, BF16) for s in shard] + [pltpu.VMEM((3,) + s, BF16) for s in shard]
            + [pltpu.VMEM(s, F32) for s in shard]
            + [pltpu.SemaphoreType.DMA((7 * n,)), pltpu.SemaphoreType.DMA((7 * n,)), pltpu.SemaphoreType.DMA((4 * n,))])


def _in_proj(x2, g_pre, w_in, tm):
    t = x2.shape[0]

    def body(x_ref, g_ref, w_ref, u_ref, zs_ref, q_ref, k_ref, v_ref, za_ref):
        xv = x_ref[...]
        r = lax.rsqrt(jnp.mean(xv * xv, axis=-1, keepdims=True) + EPS)
        hn = xv * r * g_ref[...]
        proj = _mm_nt(hn, w_ref[...])
        u_ref[...] = proj[:, 0:512]
        zs_ref[...] = proj[:, 512:1024]
        q_ref[...] = proj[:, 1024:1536].astype(BF16)
        k_ref[...] = proj[:, 1536:1664].astype(BF16)
        v_ref[...] = proj[:, 1664:1792].astype(BF16)
        za_ref[...] = proj[:, 1792:2304]

    row = lambda w: pl.BlockSpec((tm, w), lambda i: (i, 0))
    return pl.pallas_call(
        body, name="in_proj", grid=(t // tm,),
        in_specs=[row(D_MODEL), _const_spec((1, D_MODEL)), _const_spec((D_IN, D_MODEL))],
        out_specs=(row(512), row(512), row(512), row(128), row(128), row(512)),
        out_shape=(jax.ShapeDtypeStruct((t, 512), F32),
                   jax.ShapeDtypeStruct((t, 512), F32), jax.ShapeDtypeStruct((t, 512), BF16),
                   jax.ShapeDtypeStruct((t, 128), BF16), jax.ShapeDtypeStruct((t, 128), BF16),
                   jax.ShapeDtypeStruct((t, 512), F32)),
        compiler_params=_tc_params(("arbitrary",)),
    )(x2, g_pre, w_in)


def _discretise(lr, li, ls):
    step = jnp.exp(ls)
    mag = jnp.exp(lr * step)
    ar = mag * jnp.cos(li * step)
    ai = mag * jnp.sin(li * step)
    den = lr * lr + li * li
    cr = ((ar - 1.0) * lr + ai * li) / den
    ci = (ai * lr - (ar - 1.0) * li) / den
    return step, ar, ai, den, cr, ci


def _per_channel(v):
    return jnp.broadcast_to(v[:, None, :], (SSM_G, SSM_P, SSM_N)).reshape(SSM_G * SSM_P, SSM_N)


def _tile_masks():
    r = lax.broadcasted_iota(jnp.int32, (CH_T, ST_T), 0) // SSM_P
    l = lax.broadcasted_iota(jnp.int32, (CH_T, ST_T), 1) // SSM_N
    lt = lax.broadcasted_iota(jnp.int32, (ST_T, CH_T), 0) // SSM_N
    rt = lax.broadcasted_iota(jnp.int32, (ST_T, CH_T), 1) // SSM_P
    rep = lax.broadcasted_iota(jnp.int32, (SSM_N, ST_T), 0) == lax.broadcasted_iota(jnp.int32, (SSM_N, ST_T), 1) % SSM_N
    rep_t = lax.broadcasted_iota(jnp.int32, (ST_T, SSM_N), 0) % SSM_N == lax.broadcasted_iota(jnp.int32, (ST_T, SSM_N), 1)
    return r == l, lt == rt, rep, rep_t


def _ssm_prep(lam_re, lam_im, log_step, b_re, b_im, c_re, c_im, seg):
    def work(in_refs, out_refs):
        lr_ref, li_ref, ls_ref, br_ref, bi_ref, cre_ref, cim_ref, lrr_ref, lir_ref, lsr_ref = in_refs
        ar_ref, ai_ref, pr_ref, pi_ref, bcat_ref, bcat_t_ref, ccat_ref, ccat_t_ref = out_refs
        _, _, _, _, cr, ci = _discretise(lr_ref[...], li_ref[...], ls_ref[...])
        cr, ci = _per_channel(cr), _per_channel(ci)
        br, bi = br_ref[...], bi_ref[...]
        bb_re = cr * br - ci * bi
        bb_im = cr * bi + ci * br
        same, same_t, rep, rep_t = _tile_masks()
        rep, rep_t = rep.astype(BF16), rep_t.astype(BF16)
        for j in range(N_GT):
            rows = slice(j * CH_T, (j + 1) * CH_T)
            for wide, tall, parts in ((bcat_ref, bcat_t_ref, (bb_re[rows], bb_im[rows])),
                                      (ccat_t_ref, ccat_ref, (cre_ref[rows, :], -cim_ref[rows, :]))):
                for k, part in enumerate(parts):
                    p16 = part.astype(BF16)
                    wide[j, :, k * ST_T:(k + 1) * ST_T] = jnp.where(same, _mm(p16, rep), 0.0).astype(BF16)
                    tall[j, k * ST_T:(k + 1) * ST_T, :] = jnp.where(same_t, _mm_nt(rep_t, p16), 0.0).astype(BF16)
        stepr = jnp.exp(lsr_ref[...])
        mag = jnp.exp(lrr_ref[...] * stepr)
        a_r, a_i = mag * jnp.cos(lir_ref[...] * stepr), mag * jnp.sin(lir_ref[...] * stepr)
        p_r, p_i = a_r, a_i
        for k in range(8):
            pr_ref[k:k + 1, :] = p_r
            pi_ref[k:k + 1, :] = p_i
            p_r, p_i = p_r * a_r - p_i * a_i, p_r * a_i + p_i * a_r
        n = 8
        while n < seg:
            tr, ti = pr_ref[n - 1:n, :], pi_ref[n - 1:n, :]
            xr, xi = pr_ref[0:n, :], pi_ref[0:n, :]
            pr_ref[n:2 * n, :] = xr * tr - xi * ti
            pi_ref[n:2 * n, :] = xr * ti + xi * tr
            n *= 2
        ar_ref[...] = pr_ref[0:1, :]
        ai_ref[...] = pi_ref[0:1, :]

    row = jax.ShapeDtypeStruct((1, N_STATE), F32)
    pw = jax.ShapeDtypeStruct((seg, N_STATE), F32)
    wide = jax.ShapeDtypeStruct((N_GT, CH_T, 2 * ST_T), BF16)
    tall = jax.ShapeDtypeStruct((N_GT, 2 * ST_T, CH_T), BF16)
    step_row = jnp.broadcast_to(log_step, (SSM_G, SSM_N)).reshape(1, N_STATE)
    inputs = (lam_re, lam_im, log_step, b_re, b_im, c_re, c_im, lam_re.reshape(1, N_STATE),
              lam_im.reshape(1, N_STATE), step_row)
    return work, inputs, (row, row, pw, pw, wide, tall, tall, wide)


def _seg_rows(t):
    if isinstance(t, int):
        return pl.ds(t * N_SEG, N_SEG)
    return pl.ds(pl.multiple_of(t * N_SEG, N_SEG), N_SEG)


def _scan_forward(xs, a_re, a_im, pw_re, pw_im, cs, seg):
    are = jnp.broadcast_to(a_re, (N_SEG, ST_T))
    aim = jnp.broadcast_to(a_im, (N_SEG, ST_T))

    def steps(k, carry):
        xr, xi = carry
        for j in range(SCAN_UNROLL):
            r = pl.multiple_of((k * SCAN_UNROLL + j) * N_SEG, N_SEG)
            nr = are * xr - aim * xi + xs[pl.ds(r, N_SEG), 0:ST_T]
            ni = are * xi + aim * xr + xs[pl.ds(r, N_SEG), ST_T:2 * ST_T]
            xs[pl.ds(r, N_SEG), 0:ST_T] = nr
            xs[pl.ds(r, N_SEG), ST_T:2 * ST_T] = ni
            xr, xi = nr, ni
        return xr, xi

    zero = jnp.zeros((N_SEG, ST_T), F32)
    fr, fi = lax.fori_loop(0, seg // SCAN_UNROLL, steps, (zero, zero))
    sr, si = pw_re[seg - 1:seg, :], pw_im[seg - 1:seg, :]
    cr = jnp.zeros((1, ST_T), F32)
    ci = jnp.zeros((1, ST_T), F32)
    cs[0:1, :] = cr
    cs[8:9, :] = ci
    for s in range(1, N_SEG):
        ncr = sr * cr - si * ci + fr[s - 1:s, :]
        nci = sr * ci + si * cr + fi[s - 1:s, :]
        cr, ci = ncr, nci
        cs[s:s + 1, :] = cr
        cs[8 + s:9 + s, :] = ci
    car, cai = cs[0:8, :], cs[8:16, :]

    def fix(t, _):
        r = pl.multiple_of(t * N_SEG, N_SEG)
        pr, pi = pw_re[pl.ds(t, 1), :], pw_im[pl.ds(t, 1), :]
        xs[pl.ds(r, N_SEG), 0:ST_T] = xs[pl.ds(r, N_SEG), 0:ST_T] + (pr * car - pi * cai)
        xs[pl.ds(r, N_SEG), ST_T:2 * ST_T] = xs[pl.ds(r, N_SEG), ST_T:2 * ST_T] + (pr * cai + pi * car)
        return 0

    lax.fori_loop(0, seg, fix, 0, unroll=SCAN_UNROLL)


def _interleave(src, dst, seg):
    for s in range(N_SEG):
        dst[pl.ds(s, seg, stride=N_SEG), :] = src[s]


def _deinterleave(src, seg, s):
    return src[pl.ds(s, seg, stride=N_SEG), :]


def _ssm_forward(u, bcat, ccat, a_re, a_im, pw_re, pw_im, d_row, late, seg):
    bl = u.shape[0]
    rows = N_SEG * seg
    n = len(late)
    steps = bl * N_GT

    def body(*refs):
        u_ref, b_ref, c_ref, ar_ref, ai_ref, pr_ref, pi_ref, d_ref = refs[:8]
        late_r = refs[8:8 + n]
        y_ref, xs_ref, cs_ref = refs[8 + n:11 + n]
        gath, cast = refs[11 + n:11 + 2 * n], refs[11 + 2 * n:11 + 3 * n]
        send_sems, recv_sems, local_sems, ui, yi = refs[11 + 3 * n:]
        step = pl.program_id(0) * N_GT + pl.program_id(1)
        start, relay, finish = _gather_phases(late_r, gath, cast, send_sems, recv_sems, local_sems)
        pl.when(step == 0)(start)
        _interleave(u_ref.at[0], ui, seg)
        u = ui[...]
        xs, cs = xs_ref.at[0, 0], cs_ref.at[0, 0]
        xs[...] = _mm(u, b_ref[0])
        _scan_forward(xs, ar_ref[...], ai_ref[...], pr_ref, pi_ref, cs, seg)
        yi[...] = _mm(xs[...], c_ref[0]) + d_ref[...] * u
        for s in range(N_SEG):
            y_ref[0, s] = _deinterleave(yi, seg, s)
        pl.when(step == steps // 2)(relay)
        pl.when(step == steps - 1)(finish)

    state = lambda r, c: pl.BlockSpec((1, 1, r, c), lambda b, j: (b, j, 0, 0))
    act = pl.BlockSpec((1, N_SEG, seg, CH_T), lambda b, j: (b, 0, 0, j))
    g_specs, g_shapes, g_scratch = _gather_operands(late)
    res = pl.pallas_call(
        body, name="ssm_forward", grid=(bl, N_GT),
        in_specs=[act,
                  pl.BlockSpec((1, CH_T, 2 * ST_T), lambda b, j: (j, 0, 0)),
                  pl.BlockSpec((1, 2 * ST_T, CH_T), lambda b, j: (j, 0, 0)),
                  pl.BlockSpec((1, ST_T), lambda b, j: (0, j)), pl.BlockSpec((1, ST_T), lambda b, j: (0, j)),
                  pl.BlockSpec((seg, ST_T), lambda b, j: (0, j)), pl.BlockSpec((seg, ST_T), lambda b, j: (0, j)),
                  pl.BlockSpec((1, CH_T), lambda b, j: (0, j))]
        + [pl.BlockSpec(s.shape, lambda b, j: (0, 0)) for s in late],
        out_specs=(act, state(rows, 2 * ST_T), state(16, ST_T)) + g_specs,
        out_shape=(jax.ShapeDtypeStruct((bl, N_SEG, seg, D_SSM), F32),
                   jax.ShapeDtypeStruct((bl, N_GT, rows, 2 * ST_T), F32),
                   jax.ShapeDtypeStruct((bl, N_GT, 16, ST_T), F32)) + g_shapes,
        scratch_shapes=g_scratch + [pltpu.VMEM((rows, CH_T), F32), pltpu.VMEM((rows, CH_T), F32)],
        compiler_params=_tc_params(("arbitrary", "arbitrary")),
    )(u, bcat, ccat, a_re, a_im, pw_re, pw_im, d_row, *late)
    return res[:3], list(res[3:])


def _ssm_backward(u, dy, states, carries, bcat_t, ccat_t, a_re, a_im, pw_re, pw_im, d_row, late16, late32, seg):
    bl = u.shape[0]
    rows = N_SEG * seg
    n = len(late16)
    grid_steps = N_GT * bl

    def body(*refs):
        u_ref, dy_ref, xs_ref, cs_ref, bt_ref, ct_ref, ar_ref, ai_ref, pr_ref, pi_ref, d_ref = refs[:11]
        g16_r, g32_r = refs[11:11 + n], refs[11 + n:11 + 2 * n]
        du_ref, db_ref, dc_ref, dar_ref, dai_ref, dd_ref = refs[11 + 2 * n:17 + 2 * n]
        red = refs[17 + 2 * n:17 + 3 * n]
        own16, recv1, send2, recv2, own32 = (refs[17 + 3 * n + k * n:17 + 3 * n + (k + 1) * n] for k in range(5))
        s_send, s_recv, s_local, ls, cl, ui, dyi, dui = refs[17 + 8 * n:]
        b = pl.program_id(1)
        step = pl.program_id(0) * bl + b
        start, middle, total, finish = _hosted_reduce_phases(g16_r, g32_r, red, own16, recv1, send2, recv2, own32,
                                                             s_send, s_recv, s_local)
        pl.when(step == 0)(start)
        pl.when(step == grid_steps // 4)(middle)
        pl.when(step == (grid_steps * 3) // 4)(total)
        pl.when(step == grid_steps - 1)(finish)
        _interleave(u_ref.at[0], ui, seg)
        _interleave(dy_ref.at[0], dyi, seg)
        u = ui[...]
        dy = dyi[...]
        xs, cs = xs_ref.at[0, 0], cs_ref.at[0, 0]
        ls[...] = _mm(dy, ct_ref[0])
        are = jnp.broadcast_to(ar_ref[...], (N_SEG, ST_T))
        aim = jnp.broadcast_to(ai_ref[...], (N_SEG, ST_T))

        def steps(k, carry):
            lr, li = carry
            for j in range(SCAN_UNROLL):
                r = pl.multiple_of((seg - 1 - (k * SCAN_UNROLL + j)) * N_SEG, N_SEG)
                nr = are * lr + aim * li + ls[pl.ds(r, N_SEG), 0:ST_T]
                ni = are * li - aim * lr + ls[pl.ds(r, N_SEG), ST_T:2 * ST_T]
                ls[pl.ds(r, N_SEG), 0:ST_T] = nr
                ls[pl.ds(r, N_SEG), ST_T:2 * ST_T] = ni
                lr, li = nr, ni
            return lr, li

        zero = jnp.zeros((N_SEG, ST_T), F32)
        fr, fi = lax.fori_loop(0, seg // SCAN_UNROLL, steps, (zero, zero))
        sr, si = pr_ref[seg - 1:seg, :], pi_ref[seg - 1:seg, :]
        cr = jnp.zeros((1, ST_T), F32)
        ci = jnp.zeros((1, ST_T), F32)
        cl[7:8, :] = cr
        cl[15:16, :] = ci
        for s in range(N_SEG - 2, -1, -1):
            ncr = sr * cr + si * ci + fr[s + 1:s + 2, :]
            nci = sr * ci - si * cr + fi[s + 1:s + 2, :]
            cr, ci = ncr, nci
            cl[s:s + 1, :] = cr
            cl[8 + s:9 + s, :] = ci
        clr, cli = cl[0:8, :], cl[8:16, :]

        def fix_rows(rows, t, xpr, xpi, acc):
            dr, di = acc
            pr, pi = pr_ref[pl.ds(seg - 1 - t, 1), :], pi_ref[pl.ds(seg - 1 - t, 1), :]
            lr = ls[rows, 0:ST_T] + (pr * clr + pi * cli)
            li = ls[rows, ST_T:2 * ST_T] + (pr * cli - pi * clr)
            ls[rows, 0:ST_T] = lr
            ls[rows, ST_T:2 * ST_T] = li
            return dr + (lr * xpr + li * xpi), di + (li * xpr - lr * xpi)

        def fix_at(t, acc):
            prev = _seg_rows(t - 1)
            return fix_rows(_seg_rows(t), t, xs[prev, 0:ST_T], xs[prev, ST_T:2 * ST_T], acc)

        def fix(k, acc):
            for j in range(SCAN_UNROLL):
                acc = fix_at(k * SCAN_UNROLL + j, acc)
            return acc

        acc = fix_rows(pl.ds(0, N_SEG), 0, cs[0:8, :], cs[8:16, :], (zero, zero))
        for t in range(1, SCAN_UNROLL):
            acc = fix_at(t, acc)
        dr, di = lax.fori_loop(1, seg // SCAN_UNROLL, fix, acc)
        dar = jnp.sum(dr, axis=0, keepdims=True)
        dai = jnp.sum(di, axis=0, keepdims=True)
        lall = ls[...]
        dui[...] = _mm(lall, bt_ref[0]) + d_ref[...] * dy
        for s in range(N_SEG):
            du_ref[0, s] = _deinterleave(dui, seg, s).astype(BF16)
        dbp = _mm_tn(u, lall)
        dcp = _mm_tn(dy, xs[...])
        ddp = jnp.sum(dy * u, axis=0, keepdims=True)

        @pl.when(b == 0)
        def _():
            db_ref[0] = dbp
            dc_ref[0] = dcp
            dar_ref[...] = dar
            dai_ref[...] = dai
            dd_ref[...] = ddp

        @pl.when(b != 0)
        def _():
            db_ref[0] += dbp
            dc_ref[0] += dcp
            dar_ref[...] += dar
            dai_ref[...] += dai
            dd_ref[...] += ddp

    tile3 = lambda r, c: pl.BlockSpec((1, r, c), lambda j, b: (j, 0, 0))
    lane = lambda r, c: pl.BlockSpec((r, c), lambda j, b: (0, j))
    act = pl.BlockSpec((1, N_SEG, seg, CH_T), lambda j, b: (b, 0, 0, j))
    state = lambda r, c: pl.BlockSpec((1, 1, r, c), lambda j, b: (b, j, 0, 0))
    r_in, r_out, r_shapes, r_scratch = _hosted_reduce_operands(late16, lambda s: pl.BlockSpec(s, lambda j, b: (0, 0)))
    res = pl.pallas_call(
        body, name="ssm_backward", grid=(N_GT, bl),
        in_specs=[act, act, state(rows, 2 * ST_T), state(16, ST_T), tile3(2 * ST_T, CH_T), tile3(CH_T, 2 * ST_T),
                  lane(1, ST_T), lane(1, ST_T), lane(seg, ST_T), lane(seg, ST_T), lane(1, CH_T)] + r_in,
        out_specs=(act, tile3(CH_T, 2 * ST_T), tile3(CH_T, 2 * ST_T), lane(1, ST_T), lane(1, ST_T), lane(1, CH_T))
        + r_out,
        out_shape=(jax.ShapeDtypeStruct((bl, N_SEG, seg, D_SSM), BF16),
                   jax.ShapeDtypeStruct((N_GT, CH_T, 2 * ST_T), F32), jax.ShapeDtypeStruct((N_GT, CH_T, 2 * ST_T), F32),
                   jax.ShapeDtypeStruct((1, N_STATE), F32), jax.ShapeDtypeStruct((1, N_STATE), F32),
                   jax.ShapeDtypeStruct((1, D_SSM), F32)) + r_shapes,
        scratch_shapes=r_scratch + [pltpu.VMEM((rows, 2 * ST_T), F32), pltpu.VMEM((16, ST_T), F32)]
        + [pltpu.VMEM((rows, CH_T), F32)] * 3,
        compiler_params=_tc_params(("arbitrary", "arbitrary")),
    )(u, dy, states, carries, bcat_t, ccat_t, a_re, a_im, pw_re, pw_im, d_row, *late16, *late32)
    return res[:6], list(res[6:])


def _ssm_param_grads(lam_re, lam_im, log_step, b_re, b_im, da_re, da_im, d_bcat, d_ccat_t):
    def body(lr_ref, li_ref, ls_ref, br_ref, bi_ref, gar_ref, gai_ref, gbcat_ref, gccat_ref,
             dlr_ref, dli_ref, dls_ref, dbr_ref, dbi_ref, dcr_ref, dci_ref, gbr_s, gbi_s):
        same, _, _, rep_t = _tile_masks()
        rep_t = rep_t.astype(F32)
        for j in range(N_GT):
            rows = slice(j * CH_T, (j + 1) * CH_T)
            for src, dsts in ((gbcat_ref, (gbr_s, gbi_s)), (gccat_ref, (dcr_ref, dci_ref))):
                for k, dst in enumerate(dsts):
                    blk = jnp.where(same, src[j, :, k * ST_T:(k + 1) * ST_T], 0.0)
                    dst[rows, :] = jnp.dot(blk, rep_t, precision=lax.Precision.HIGHEST, preferred_element_type=F32)
        dci_ref[...] = -dci_ref[...]
        lr, li = lr_ref[...], li_ref[...]
        step, ar, ai, den, cr, ci = _discretise(lr, li, ls_ref[...])
        crb, cib = _per_channel(cr), _per_channel(ci)
        br, bi = br_ref[...], bi_ref[...]
        gbr, gbi = gbr_s[...], gbi_s[...]
        dbr_ref[...] = crb * gbr + cib * gbi
        dbi_ref[...] = crb * gbi - cib * gbr
        over_channels = lambda t: jnp.sum(t.reshape(SSM_G, SSM_P, SSM_N), axis=1)
        gcr = over_channels(br * gbr + bi * gbi)
        gci = over_channels(br * gbi - bi * gbr)
        ilr, ili = lr / den, -li / den
        gar = gar_ref[...] + (ilr * gcr + ili * gci)
        gai = gai_ref[...] + (ilr * gci - ili * gcr)
        qr, qi = cr * ilr - ci * ili, cr * ili + ci * ilr
        glr = -(qr * gcr + qi * gci)
        gli = -(qr * gci - qi * gcr)
        gwr = ar * gar + ai * gai
        gwi = ar * gai - ai * gar
        dlr_ref[...] = glr + step * gwr
        dli_ref[...] = gli + step * gwi
        dls_ref[...] = jnp.sum(lr * gwr + li * gwi, axis=-1, keepdims=True) * step

    lam = jax.ShapeDtypeStruct((SSM_G, SSM_N), F32)
    mat = jax.ShapeDtypeStruct((SSM_G * SSM_P, SSM_N), F32)
    vm = pl.BlockSpec(memory_space=pltpu.VMEM)
    return pl.pallas_call(
        body, name="ssm_param_grads", out_shape=(lam, lam, jax.ShapeDtypeStruct((SSM_G, 1), F32), mat, mat, mat, mat),
        in_specs=[vm] * 9, out_specs=(vm,) * 7,
        scratch_shapes=[pltpu.VMEM((SSM_G * SSM_P, SSM_N), F32), pltpu.VMEM((SSM_G * SSM_P, SSM_N), F32)],
    )(lam_re, lam_im, log_step, b_re, b_im, da_re, da_im, d_bcat, d_ccat_t)


ROWS4 = Q_PER_KV * ATT_BLOCK
ATT_FWD_STACK = 1


def _att_dist_mask(first_block):
    qi = lax.broadcasted_iota(jnp.int32, (ROWS4, 2 * ATT_BLOCK), 0) & (ATT_BLOCK - 1)
    si = lax.broadcasted_iota(jnp.int32, (ROWS4, 2 * ATT_BLOCK), 1)
    dist = qi + ATT_BLOCK - si
    valid = (dist >= 0) & (dist < ATT_BLOCK) & ((si >= ATT_BLOCK) | jnp.logical_not(first_block))
    return dist.astype(F32), valid


def _stack_heads(x, kv):
    return jnp.concatenate([x[:, (kv * Q_PER_KV + g) * HEAD_DIM:(kv * Q_PER_KV + g + 1) * HEAD_DIM]
                            for g in range(Q_PER_KV)], axis=0)


def _stack_cols(x, kv):
    return jnp.concatenate([x[:, kv * Q_PER_KV + g:kv * Q_PER_KV + g + 1] for g in range(Q_PER_KV)], axis=0)


def _per_head_col(vals):
    return jnp.concatenate([jnp.full((ATT_BLOCK, 1), v, F32) for v in vals], axis=0)


def _attn_forward(q, k, v, sinks, bl, nb):
    t = q.shape[0]

    def body(sink_ref, q_ref, kp_ref, kc_ref, vp_ref, vc_ref, o_ref, lse_ref):
        i = pl.program_id(1)
        dist4, valid4 = _att_dist_mask(i == 0)
        rows2 = ATT_FWD_STACK * ATT_BLOCK
        dist, valid = dist4[0:rows2, :], valid4[0:rows2, :]
        kk = jnp.concatenate([kp_ref[...], kc_ref[...]], axis=0)
        vv = jnp.concatenate([vp_ref[...], vc_ref[...]], axis=0)
        qv = q_ref[...]
        col = lambda vals: jnp.concatenate([jnp.full((ATT_BLOCK, 1), v, F32) for v in vals], axis=0)
        for h0 in range(0, N_HEADS, ATT_FWD_STACK):
            heads = range(h0, h0 + ATT_FWD_STACK)
            kv = h0 // Q_PER_KV
            slope = col([2.0 ** (-(h + 1)) for h in heads])
            sink = col([sink_ref[h] for h in heads])
            qh = jnp.concatenate([qv[:, h * HEAD_DIM:(h + 1) * HEAD_DIM] for h in heads], axis=0)
            kh = kk[:, kv * HEAD_DIM:(kv + 1) * HEAD_DIM]
            vh = vv[:, kv * HEAD_DIM:(kv + 1) * HEAD_DIM]
            s = _mm_nt(qh, kh) * ATT_SCALE - slope * dist
            s = jnp.where(valid, s, NEG_INF)
            m = jnp.maximum(jnp.max(s, axis=-1, keepdims=True), sink)
            e = jnp.exp(s - m)
            den = jnp.sum(e, axis=-1, keepdims=True) + jnp.exp(sink - m)
            o = _mm(e, vh) * (1.0 / den)
            lse = m + jnp.log(den)
            for g, h in enumerate(heads):
                rows = slice(g * ATT_BLOCK, (g + 1) * ATT_BLOCK)
                o_ref[:, h * HEAD_DIM:(h + 1) * HEAD_DIM] = o[rows, :]
                lse_ref[:, h:h + 1] = lse[rows, :]

    cur = lambda w: pl.BlockSpec((ATT_BLOCK, w), lambda b, i: (b * nb + i, 0))
    prev = lambda w: pl.BlockSpec((ATT_BLOCK, w), lambda b, i: (b * nb + jnp.maximum(i - 1, 0), 0))
    return pl.pallas_call(
        body, name="attn_forward", grid=(bl, nb),
        in_specs=[pl.BlockSpec(memory_space=pltpu.SMEM), cur(512), prev(128), cur(128), prev(128), cur(128)],
        out_specs=(cur(512), cur(N_HEADS)),
        out_shape=(jax.ShapeDtypeStruct((t, D_ATTN), F32), jax.ShapeDtypeStruct((t, N_HEADS), F32)),
        compiler_params=_tc_params(("arbitrary", "arbitrary")),
    )(sinks, q, k, k, v, v)


def _attn_backward(q, k, v, o, do, lse, sinks, bl, nb):
    t = q.shape[0]

    def body(sink_ref, qc_ref, kp_ref, kc_ref, vp_ref, vc_ref, oc_ref, doc_ref, lc_ref,
             dq_ref, dk_ref, dv_ref, ds_ref, dk_carry, dv_carry):
        b, i = pl.program_id(0), pl.program_id(1)
        live = i < nb

        @pl.when(i == 0)
        def _():
            dk_carry[...] = jnp.zeros((ATT_BLOCK, KV_HEADS * HEAD_DIM), F32)
            dv_carry[...] = jnp.zeros((ATT_BLOCK, KV_HEADS * HEAD_DIM), F32)

        dist, valid = _att_dist_mask(i == 0)
        valid = valid & live
        kk = jnp.concatenate([kp_ref[...], kc_ref[...]], axis=0)
        vv = jnp.concatenate([vp_ref[...], vc_ref[...]], axis=0)
        qc, oc, doc, lc = qc_ref[...], oc_ref[...], doc_ref[...], lc_ref[...]
        dsink_cols, dq_parts, dk_t, dv_t = [], [], [], []
        for kv in range(KV_HEADS):
            heads = range(kv * Q_PER_KV, (kv + 1) * Q_PER_KV)
            cols = slice(kv * HEAD_DIM, (kv + 1) * HEAD_DIM)
            kh, vh = kk[:, cols], vv[:, cols]
            slope = _per_head_col([2.0 ** (-(h + 1)) for h in heads])
            sink = _per_head_col([sink_ref[h] for h in heads])
            q4, do4 = _stack_heads(qc, kv), _stack_heads(doc, kv)
            delta = jnp.sum(do4 * _stack_heads(oc, kv), axis=-1, keepdims=True)
            lse4 = _stack_cols(lc, kv)
            s = _mm_nt(q4, kh) * ATT_SCALE - slope * dist
            p = jnp.where(valid, jnp.exp(s - lse4), 0.0)
            dsc = p * (_mm_nt(do4, vh) - delta)
            dq4 = _mm(dsc, kh) * ATT_SCALE
            dk_t.append(_mm_tn(q4, dsc) * ATT_SCALE)
            dv_t.append(_mm_tn(do4, p))
            dsink4 = jnp.where(live, jnp.exp(sink - lse4) * delta, 0.0)
            for g, h in enumerate(heads):
                rows = slice(g * ATT_BLOCK, (g + 1) * ATT_BLOCK)
                dq_parts.append((h, dq4[rows, :]))
                dsink_cols.append(-jnp.sum(dsink4[rows, :], axis=0, keepdims=True))
        dsink = jnp.concatenate(dsink_cols, axis=1)
        for out_ref, carry, parts in ((dk_ref, dk_carry, dk_t), (dv_ref, dv_carry, dv_t)):
            both = jnp.concatenate(parts, axis=0)
            out_ref[...] = (carry[...] + both[:, 0:ATT_BLOCK]).T
            carry[...] = both[:, ATT_BLOCK:]

        @pl.when(live)
        def _():
            for h, part in dq_parts:
                dq_ref[:, h * HEAD_DIM:(h + 1) * HEAD_DIM] = part

        @pl.when((b == 0) & (i == 0))
        def _():
            ds_ref[...] = dsink

        @pl.when((b != 0) | (i != 0))
        def _():
            ds_ref[...] += dsink

    cur_i = lambda i: jnp.minimum(i, nb - 1)
    cur = lambda w: pl.BlockSpec((ATT_BLOCK, w), lambda b, i: (b * nb + cur_i(i), 0))
    prev = lambda w: pl.BlockSpec((ATT_BLOCK, w), lambda b, i: (b * nb + jnp.maximum(cur_i(i) - 1, 0), 0))
    behind = lambda w: pl.BlockSpec((ATT_BLOCK, w), lambda b, i: (b * nb + jnp.maximum(i - 1, 0), 0))
    return pl.pallas_call(
        body, name="attn_backward", grid=(bl, nb + 1),
        in_specs=[pl.BlockSpec(memory_space=pltpu.SMEM), cur(512), prev(128), cur(128), prev(128), cur(128),
                  cur(512), cur(512), cur(N_HEADS)],
        out_specs=(cur(512), behind(128), behind(128), pl.BlockSpec((1, N_HEADS), lambda b, i: (0, 0))),
        out_shape=(jax.ShapeDtypeStruct((t, D_ATTN), F32), jax.ShapeDtypeStruct((t, 128), F32),
                   jax.ShapeDtypeStruct((t, 128), F32), jax.ShapeDtypeStruct((1, N_HEADS), F32)),
        scratch_shapes=[pltpu.VMEM((ATT_BLOCK, KV_HEADS * HEAD_DIM), F32), pltpu.VMEM((ATT_BLOCK, KV_HEADS * HEAD_DIM), F32)],
        compiler_params=_tc_params(("arbitrary", "arbitrary")),
    )(sinks, q, k, k, v, v, o, do, lse)


def _mix_forward_backward(x2, y2, z_ssm, attn, z_attn, p2, target2, w_glu, b_glu, w_out, g_post, w_gate, b_gate,
                          w_proj, tm):
    t = x2.shape[0]

    def body(x_ref, y_ref, zs_ref, at_ref, za_ref, p_ref, tg_ref,
             wglu_ref, bglu_ref, wout_ref, gpost_ref, wgate_ref, bgate_ref, wproj_ref,
             loss_ref, dh1_ref, dy_ref, dzs_ref, dat_ref, dza_ref,
             dwglu_ref, dbglu_ref, dwout_ref, dgpost_ref, dwgate_ref, dbgate_ref, dwproj_ref,
             dwout16_ref, dwgate16_ref, dwproj16_ref, dwglu16_ref):
        i = pl.program_id(0)
        gpost = gpost_ref[...]

        @pl.when(i == 0)
        def _():
            for ref in (dwglu_ref, dbglu_ref, dwout_ref, dgpost_ref, dwgate_ref, dbgate_ref, dwproj_ref, loss_ref):
                ref[...] = jnp.zeros(ref.shape, F32)

        def chain(rows):
            y = y_ref[rows, :]
            u3 = GELU_C * (y + GELU_K * y * y * y)
            th = jnp.tanh(u3)
            gl = 0.5 * y * (1.0 + th)
            a = _mm(gl, wglu_ref[...]) + bglu_ref[...]
            sa = _sigmoid(a)
            glu = gl * sa
            zs = zs_ref[rows, :]
            sgs = _sigmoid(zs)
            ssm_out = glu * (zs * sgs)
            za = za_ref[rows, :]
            sga = _sigmoid(za)
            at = at_ref[rows, :]
            attn_out = at * (za * sga)
            cat = jnp.concatenate([ssm_out, attn_out], axis=-1).astype(BF16)
            mixed = _mm(cat, wout_ref[...])
            r2 = lax.rsqrt(jnp.mean(mixed * mixed, axis=-1, keepdims=True) + EPS)
            nhat = mixed * r2
            h1 = x_ref[rows, :] + nhat * gpost
            gate = _sigmoid(_mm(h1, wgate_ref[...]) + bgate_ref[...])
            pv = p_ref[rows, :]
            pp = _mm(pv, wproj_ref[...])
            h2 = h1 + gate * pp
            err = h2 - tg_ref[rows, :]
            loss_part = jnp.sum(jnp.sum(err * err, axis=-1, keepdims=True), axis=0, keepdims=True) * (0.5 / D_MODEL)
            dh2 = err * (1.0 / D_MODEL)
            dgp = dh2 * pp * gate * (1.0 - gate)
            dpp = dh2 * gate
            dh1 = dh2 + _mm_nt(dgp, wgate_ref[...])
            dh1_ref[rows, :] = dh1
            dnhat = dh1 * gpost
            dmixed = r2 * (dnhat - nhat * jnp.mean(dnhat * nhat, axis=-1, keepdims=True))
            dcat = _mm_nt(dmixed, wout_ref[...])
            dso, dao = dcat[:, 0:D_SSM], dcat[:, D_SSM:]
            dat_ref[rows, :] = dao * (za * sga)
            dza_ref[rows, :] = (dao * at * (sga * (1.0 + za * (1.0 - sga)))).astype(BF16)
            dzs_ref[rows, :] = (dso * glu * (sgs * (1.0 + zs * (1.0 - sgs)))).astype(BF16)
            dglu = dso * (zs * sgs)
            da = dglu * gl * sa * (1.0 - sa)
            dgl = dglu * sa + _mm_nt(da, wglu_ref[...])
            dgelu = 0.5 * (1.0 + th) + 0.5 * y * (1.0 - th * th) * (GELU_C * (1.0 + 3.0 * GELU_K * y * y))
            dy_ref[rows, :] = dgl * dgelu
            return dict(gl=gl.astype(BF16), da=da.astype(BF16), cat=cat, dmixed=dmixed.astype(BF16),
                        h1=h1.astype(BF16), dgp=dgp.astype(BF16), pv=pv.astype(BF16), dpp=dpp.astype(BF16),
                        dbglu=jnp.sum(da, axis=0, keepdims=True), dgpost=jnp.sum(dh1 * nhat, axis=0, keepdims=True),
                        dbgate=jnp.sum(dgp, axis=0, keepdims=True), loss=loss_part)

        groups = [chain(slice(k * (tm // MIX_GROUPS), (k + 1) * (tm // MIX_GROUPS))) for k in range(MIX_GROUPS)]
        rows_of = lambda name: jnp.concatenate([g[name] for g in groups], axis=0)
        total = lambda name: sum(g[name] for g in groups)
        parts = (
            (dwglu_ref, _mm_tn(rows_of("gl"), rows_of("da"))), (dbglu_ref, total("dbglu")),
            (dwout_ref, _mm_tn(rows_of("cat"), rows_of("dmixed"))), (dgpost_ref, total("dgpost")),
            (dwgate_ref, _mm_tn(rows_of("h1"), rows_of("dgp"))), (dbgate_ref, total("dbgate")),
            (dwproj_ref, _mm_tn(rows_of("pv"), rows_of("dpp"))), (loss_ref, total("loss")),
        )

        for ref, val in parts:
            ref[...] += val

        @pl.when(i == t // tm - 1)
        def _():
            for ref16, ref in ((dwout16_ref, dwout_ref), (dwgate16_ref, dwgate_ref), (dwproj16_ref, dwproj_ref),
                               (dwglu16_ref, dwglu_ref)):
                def to16(r, ref16=ref16, ref=ref):
                    ref16[r, :] = ref[r, :].astype(BF16)

                _row_chunks(ref.shape[0], to16)

    row = lambda w: pl.BlockSpec((tm, w), lambda i: (i, 0))
    acc = lambda r, c, dt=F32: (_const_spec((r, c)), jax.ShapeDtypeStruct((r, c), dt))
    accs = [acc(D_SSM, D_SSM), acc(1, D_SSM), acc(D_MODEL, D_MODEL), acc(1, D_MODEL), acc(D_MODEL, D_MODEL),
            acc(1, D_MODEL), acc(D_PLE, D_MODEL),
            acc(D_MODEL, D_MODEL, BF16), acc(D_MODEL, D_MODEL, BF16), acc(D_PLE, D_MODEL, BF16), acc(D_SSM, D_SSM, BF16)]
    return pl.pallas_call(
        body, name="mix_forward_backward", grid=(t // tm,),
        in_specs=[row(D_MODEL), row(512), row(512), row(512), row(512), row(D_PLE), row(D_MODEL),
                  _const_spec((D_SSM, D_SSM)), _const_spec((1, D_SSM)), _const_spec((D_MODEL, D_MODEL)),
                  _const_spec((1, D_MODEL)), _const_spec((D_MODEL, D_MODEL)), _const_spec((1, D_MODEL)),
                  _const_spec((D_PLE, D_MODEL))],
        out_specs=(_const_spec((1, 1)), row(D_MODEL), row(512), row(512), row(512), row(512))
        + tuple(a[0] for a in accs),
        out_shape=(jax.ShapeDtypeStruct((1, 1), F32), jax.ShapeDtypeStruct((t, D_MODEL), F32),
                   jax.ShapeDtypeStruct((t, 512), F32),
                   jax.ShapeDtypeStruct((t, 512), BF16), jax.ShapeDtypeStruct((t, 512), F32),
                   jax.ShapeDtypeStruct((t, 512), BF16)) + tuple(a[1] for a in accs),
        compiler_params=_tc_params(("arbitrary",)),
    )(x2, y2, z_ssm, attn, z_attn, p2, target2, w_glu, b_glu, w_out, g_post, w_gate, b_gate, w_proj)


def _in_backward(x2, dh1, du, dz_ssm, dq, dk, dv, dz_attn, g_pre, w_in, tm):
    t = x2.shape[0]

    def body(x_ref, dh1_ref, du_ref, dzs_ref, dq_ref, dk_ref, dv_ref, dza_ref, g_ref, w_ref,
             gx_ref, dw_ref, dg_ref, dw16_ref):
        i = pl.program_id(0)
        xv = x_ref[...]
        r = lax.rsqrt(jnp.mean(xv * xv, axis=-1, keepdims=True) + EPS)
        xhat = xv * r
        g = g_ref[...]
        hn = (xhat * g).astype(BF16)
        dproj = jnp.concatenate([du_ref[...].astype(BF16), dzs_ref[...].astype(BF16), dq_ref[...].astype(BF16),
                                 dk_ref[...].astype(BF16), dv_ref[...].astype(BF16), dza_ref[...].astype(BF16)],
                                axis=-1)
        dhn = _mm(dproj, w_ref[...])
        dxhat = dhn * g
        gx_ref[...] = dh1_ref[...] + r * (dxhat - xhat * jnp.mean(dxhat * xhat, axis=-1, keepdims=True))
        @pl.when(i == 0)
        def _():
            dw_ref[...] = jnp.zeros((D_IN, D_MODEL), F32)
            dg_ref[...] = jnp.zeros((1, D_MODEL), F32)

        dw_ref[...] += _mm_tn(dproj, hn)
        dg_ref[...] += jnp.sum(dhn * xhat, axis=0, keepdims=True)

        @pl.when(i == t // tm - 1)
        def _():
            def to16(r):
                dw16_ref[r, :] = dw_ref[r, :].astype(BF16)

            _row_chunks(D_IN, to16)

    row = lambda w: pl.BlockSpec((tm, w), lambda i: (i, 0))
    return pl.pallas_call(
        body, name="in_backward", grid=(t // tm,),
        in_specs=[row(D_MODEL), row(D_MODEL), row(512), row(512), row(512), row(128), row(128), row(512),
                  _const_spec((1, D_MODEL)), _const_spec((D_IN, D_MODEL))],
        out_specs=(row(D_MODEL), _const_spec((D_IN, D_MODEL)), _const_spec((1, D_MODEL)),
                   _const_spec((D_IN, D_MODEL))),
        out_shape=(jax.ShapeDtypeStruct((t, D_MODEL), F32), jax.ShapeDtypeStruct((D_IN, D_MODEL), F32),
                   jax.ShapeDtypeStruct((1, D_MODEL), F32), jax.ShapeDtypeStruct((D_IN, D_MODEL), BF16)),
        compiler_params=_tc_params(("arbitrary",)),
    )(x2, dh1, du, dz_ssm, dq, dk, dv, dz_attn, g_pre, w_in)


def _local_step(x, p, target, pre_norm_g, w_in, prep, ssm_lam_re, ssm_lam_im, ssm_log_step, ssm_b_re, ssm_b_im, ssm_d,
                ssm_b_glu, attn_sinks, post_norm_g, pl_b_gate, late):
    bl, seq, _ = x.shape
    seg = seq // N_SEG
    nb = seq // ATT_BLOCK
    t = bl * seq
    x2 = x.reshape(t, D_MODEL)
    p2 = p.reshape(t, D_PLE)
    tg2 = target.reshape(t, D_MODEL)

    lam_re, lam_im = ssm_lam_re, ssm_lam_im
    log_step = ssm_log_step.reshape(SSM_G, 1)
    a_re_row, a_im_row, pw_re, pw_im, bcat, bcat_t, ccat, ccat_t = prep
    d_row = ssm_d.reshape(1, D_SSM)

    segments = lambda a: a.reshape(bl, N_SEG, seg, D_SSM)
    u, z_ssm, q, k, v, z_attn = _in_proj(x2, pre_norm_g.reshape(1, D_MODEL), w_in, min(TOKEN_TILE_WIDE, t))
    (y, states, carries), gathered = _ssm_forward(
        segments(u), bcat, ccat, a_re_row, a_im_row, pw_re, pw_im, d_row, late, seg)
    w_out, w_gate, w_proj, w_glu = (_gathered_to_full(n, g) for n, g in zip(LATE_NAMES, gathered))
    sinks = attn_sinks.reshape(N_HEADS)
    attn, lse = _attn_forward(q, k, v, sinks, bl, nb)
    (loss, dh1, dy, dz_ssm, dattn, dz_attn, d_w_glu, d_b_glu, d_w_out, d_g_post, d_w_gate, d_b_gate,
     d_w_proj, *late16) = _mix_forward_backward(
        x2, y.reshape(t, D_SSM), z_ssm, attn, z_attn, p2, tg2, w_glu,
        ssm_b_glu.reshape(1, D_SSM), w_out, post_norm_g.reshape(1, D_MODEL), w_gate, pl_b_gate.reshape(1, D_MODEL),
        w_proj, min(TOKEN_TILE, t))
    owned = lambda ds: [_full_to_owned(n, d) for n, d in zip(LATE_NAMES, ds)]
    dq, dk, dv, d_sinks = _attn_backward(q, k, v, attn, dattn, lse, sinks, bl, nb)
    (du, d_bcat, d_ccat_t, da_re, da_im, d_d), late_grads = _ssm_backward(
        segments(u), segments(dy), states, carries, bcat_t, ccat_t, a_re_row, a_im_row, pw_re, pw_im,
        d_row, owned(late16), owned((d_w_out, d_w_gate, d_w_proj, d_w_glu)), seg)
    grad_x, d_w_in, d_g_pre, d_w_in16 = _in_backward(
        x2, dh1, du.reshape(t, D_SSM), dz_ssm, dq, dk, dv, dz_attn, pre_norm_g.reshape(1, D_MODEL), w_in,
        min(TOKEN_TILE_WIDE, t))
    d_lam_re, d_lam_im, d_ls, d_b_re, d_b_im, d_c_re, d_c_im = _ssm_param_grads(
        lam_re, lam_im, log_step, ssm_b_re, ssm_b_im, da_re.reshape(SSM_G, SSM_N), da_im.reshape(SSM_G, SSM_N),
        d_bcat, d_ccat_t)
    grads = {
        "pre_norm_g": d_g_pre, "w_in": d_w_in, "w_in16": d_w_in16, "ssm_lam_re": d_lam_re, "ssm_lam_im": d_lam_im,
        "ssm_log_step": d_ls, "ssm_b_re": d_b_re, "ssm_b_im": d_b_im, "ssm_c_re": d_c_re, "ssm_c_im": d_c_im,
        "ssm_d": d_d, "ssm_b_glu": d_b_glu, "attn_sinks": d_sinks, "post_norm_g": d_g_post, "pl_b_gate": d_b_gate,
    }
    return loss, grad_x.reshape(bl, seq, D_MODEL), grads, late_grads


LATE_NAMES = ("w_out", "pl_w_gate", "pl_w_proj", "ssm_w_glu")
BIG_NAMES = ("w_in",) + LATE_NAMES
COL_SHARDED = {"w_in": D_IN // N_DEV, "pl_w_proj": D_MODEL // N_DEV}
WEIGHT_NAMES = ("pre_norm_g", "w_in", "ssm_lam_re", "ssm_lam_im", "ssm_log_step", "ssm_b_re", "ssm_b_im", "ssm_c_re",
                "ssm_c_im", "ssm_d", "ssm_w_glu", "ssm_b_glu", "attn_sinks", "w_out", "post_norm_g", "pl_w_proj",
                "pl_w_gate", "pl_b_gate")


TRANSPOSED = {"w_in": (0, 1), "ssm_b_re": (1, 2), "ssm_b_im": (1, 2)}


def _kernel_form(name, a):
    a = a[0]
    if name in TRANSPOSED:
        a = jnp.swapaxes(a, *TRANSPOSED[name])
    if name in ("ssm_b_re", "ssm_b_im", "ssm_c_re", "ssm_c_im"):
        a = a.reshape(SSM_G * SSM_P, SSM_N)
    return a


def _given_form(name, a, shape):
    if name in TRANSPOSED:
        i, j = TRANSPOSED[name]
        swapped = list(shape[1:])
        swapped[i], swapped[j] = swapped[j], swapped[i]
        return jnp.swapaxes(a.reshape(swapped), i, j).reshape(shape)
    return a.reshape(shape)


def _gathered_to_full(name, g):
    _, rows, cols = g.shape
    if name in COL_SHARDED:
        return jnp.swapaxes(g, 0, 1).reshape(rows, N_DEV * cols)
    return g.reshape(N_DEV * rows, cols)


def _full_to_owned(name, full):
    if name in COL_SHARDED:
        return jnp.swapaxes(full.reshape(full.shape[0], N_DEV, COL_SHARDED[name]), 0, 1)
    return full.reshape(N_DEV, full.shape[0] // N_DEV, full.shape[1])


def kernel(x, p, pre_norm_g, w_in, ssm_lam_re, ssm_lam_im, ssm_log_step, ssm_b_re, ssm_b_im, ssm_c_re, ssm_c_im, ssm_d, ssm_w_glu, ssm_b_glu, attn_sinks, w_out, post_norm_g, pl_w_proj, pl_w_gate, pl_b_gate, loss_target, m_pre_norm_g, m_w_in, m_ssm_lam_re, m_ssm_lam_im, m_ssm_log_step, m_ssm_b_re, m_ssm_b_im, m_ssm_c_re, m_ssm_c_im, m_ssm_d, m_ssm_w_glu, m_ssm_b_glu, m_attn_sinks, m_w_out, m_post_norm_g, m_pl_w_proj, m_pl_w_gate, m_pl_b_gate, v_pre_norm_g, v_w_in, v_ssm_lam_re, v_ssm_lam_im, v_ssm_log_step, v_ssm_b_re, v_ssm_b_im, v_ssm_c_re, v_ssm_c_im, v_ssm_d, v_ssm_w_glu, v_ssm_b_glu, v_attn_sinks, v_w_out, v_post_norm_g, v_pl_w_proj, v_pl_w_gate, v_pl_b_gate):
    w = dict(pre_norm_g=pre_norm_g, w_in=w_in, ssm_lam_re=ssm_lam_re, ssm_lam_im=ssm_lam_im, ssm_log_step=ssm_log_step,
             ssm_b_re=ssm_b_re, ssm_b_im=ssm_b_im, ssm_c_re=ssm_c_re, ssm_c_im=ssm_c_im, ssm_d=ssm_d, ssm_w_glu=ssm_w_glu,
             ssm_b_glu=ssm_b_glu, attn_sinks=attn_sinks, w_out=w_out, post_norm_g=post_norm_g, pl_w_proj=pl_w_proj,
             pl_w_gate=pl_w_gate, pl_b_gate=pl_b_gate)
    m = dict(pre_norm_g=m_pre_norm_g, w_in=m_w_in, ssm_lam_re=m_ssm_lam_re, ssm_lam_im=m_ssm_lam_im,
             ssm_log_step=m_ssm_log_step, ssm_b_re=m_ssm_b_re, ssm_b_im=m_ssm_b_im, ssm_c_re=m_ssm_c_re,
             ssm_c_im=m_ssm_c_im, ssm_d=m_ssm_d, ssm_w_glu=m_ssm_w_glu, ssm_b_glu=m_ssm_b_glu, attn_sinks=m_attn_sinks,
             w_out=m_w_out, post_norm_g=m_post_norm_g, pl_w_proj=m_pl_w_proj, pl_w_gate=m_pl_w_gate,
             pl_b_gate=m_pl_b_gate)
    v = dict(pre_norm_g=v_pre_norm_g, w_in=v_w_in, ssm_lam_re=v_ssm_lam_re, ssm_lam_im=v_ssm_lam_im,
             ssm_log_step=v_ssm_log_step, ssm_b_re=v_ssm_b_re, ssm_b_im=v_ssm_b_im, ssm_c_re=v_ssm_c_re,
             ssm_c_im=v_ssm_c_im, ssm_d=v_ssm_d, ssm_w_glu=v_ssm_w_glu, ssm_b_glu=v_ssm_b_glu, attn_sinks=v_attn_sinks,
             w_out=v_w_out, post_norm_g=v_post_norm_g, pl_w_proj=v_pl_w_proj, pl_w_gate=v_pl_w_gate,
             pl_b_gate=v_pl_b_gate)
    kf = lambda d: {n: _kernel_form(n, a) for n, a in d.items()}
    wk, mk, vk = kf(w), kf(m), kf(v)

    (gathered,), prep = _allgather_weights([wk["w_in"]], *_ssm_prep(
        wk["ssm_lam_re"], wk["ssm_lam_im"], wk["ssm_log_step"].reshape(SSM_G, 1), wk["ssm_b_re"], wk["ssm_b_im"],
        wk["ssm_c_re"], wk["ssm_c_im"], x.shape[1] // N_SEG))
    loss, grad_x, grads, g_late = _local_step(
        x, p[0], loss_target, wk["pre_norm_g"], gathered.reshape(D_IN, D_MODEL), prep, wk["ssm_lam_re"],
        wk["ssm_lam_im"], wk["ssm_log_step"], wk["ssm_b_re"], wk["ssm_b_im"], wk["ssm_d"],
        wk["ssm_b_glu"], wk["attn_sinks"], wk["post_norm_g"], wk["pl_b_gate"], [wk[n] for n in LATE_NAMES])

    owned = lambda g: g.reshape(N_DEV, D_IN // N_DEV, D_MODEL)
    tiny_form = lambda d: [d[n].reshape(rows, cols) for n, rows, cols in TINY]
    med_form = lambda d: [d[n].reshape(N_DEV, rows // N_DEV, cols) for n, rows, cols in MEDIUM]
    g_big, loss, g_tiny, g_med = _reduce_final(
        [owned(grads["w_in16"])], [owned(grads["w_in"])], loss, tiny_form(grads), med_form(grads))
    names = BIG_NAMES + tuple(n for n, _, _ in TINY + MEDIUM)
    form = lambda d: [d[n] for n in BIG_NAMES] + tiny_form(d) + med_form(d)
    updated = _adamw_update(g_big + g_late + g_tiny + g_med, form(wk), form(mk), form(vk))
    vals = dict(zip(names, updated))
    results = [[_given_form(n, vals[n][kind], w[n].shape) for n in WEIGHT_NAMES] for kind in range(4)]
    return (loss.reshape(()), grad_x, *results[0], *results[1], *results[2], *results[3])
```

```python
import functools
import math

import jax
import jax.numpy as jnp
from jax import lax
from jax.experimental import pallas as pl
from jax.experimental.pallas import tpu as pltpu

F32 = jnp.float32
BF16 = jnp.bfloat16

D_MODEL = 1024
D_SSM = 512
D_ATTN = 512
SSM_P = 16
SSM_G = 32
SSM_N = 64
N_HEADS = 8
KV_HEADS = 2
Q_PER_KV = 4
HEAD_DIM = 64
ATT_BLOCK = 128
D_PLE = 256
D_IN = 2304
EPS = 1e-6
N_DEV = 8
N_SEG = 8
G_TILE = 8
N_GT = SSM_G // G_TILE
CH_T = G_TILE * SSM_P
ST_T = G_TILE * SSM_N
N_STATE = SSM_G * SSM_N
SCAN_UNROLL = 4
MIX_GROUPS = 1
TOKEN_TILE = 256
TOKEN_TILE_WIDE = 512
LANES = 128
VMEM_LIMIT = 60 * 1024 * 1024

ADAM_LR = 0.001
ADAM_B1 = 0.9
ADAM_B2 = 0.999
ADAM_EPS = 1e-08
ADAM_WD = 0.01
ADAM_STEP = 10

GELU_C = math.sqrt(2.0 / math.pi)
GELU_K = 0.044715
ATT_SCALE = 1.0 / math.sqrt(HEAD_DIM)
NEG_INF = float("-inf")


def _mm(a, b):
    return jnp.dot(a.astype(BF16), b.astype(BF16), preferred_element_type=F32)


def _mm_nt(a, b):
    return lax.dot_general(a.astype(BF16), b.astype(BF16), (((1,), (1,)), ((), ())), preferred_element_type=F32)


def _mm_tn(a, b):
    return lax.dot_general(a.astype(BF16), b.astype(BF16), (((0,), (0,)), ((), ())), preferred_element_type=F32)


def _sigmoid(x):
    return 1.0 / (1.0 + jnp.exp(-x))


def _tc_params(sem):
    return pltpu.CompilerParams(dimension_semantics=sem, vmem_limit_bytes=VMEM_LIMIT)


def _const_spec(shape):
    nd = len(shape)
    return pl.BlockSpec(shape, lambda *_: (0,) * nd)


def _mesh_pos():
    return lax.axis_index("x"), lax.axis_index("y"), lax.axis_index("c")


ROW_CHUNKS = (64, 32, 16)


def _row_chunk(nrows):
    return next((c for c in ROW_CHUNKS if nrows % c == 0), None)


def _row_chunks(nrows, fn, chunk=None, init=None):
    chunk = chunk or _row_chunk(nrows)

    def step(i, carry):
        rows = pl.ds(pl.multiple_of(i * chunk, chunk), chunk)
        if init is None:
            fn(rows)
            return carry
        return fn(rows, carry)

    return lax.fori_loop(0, nrows // chunk, step, 0 if init is None else init)


def _slot(px, py, pc):
    return 4 * px + 2 * py + pc


def _allgather_weights(shards, work=None, work_inputs=(), work_out_shapes=()):
    n, n_wi, n_wo = len(shards), len(work_inputs), len(work_out_shapes)

    def body(*refs):
        srcs, w_in_refs = refs[:n], refs[n:n + n_wi]
        outs, w_out_refs = refs[n + n_wi:2 * n + n_wi], refs[2 * n + n_wi:2 * n + n_wi + n_wo]
        send_sems, recv_sems = refs[2 * n + n_wi + n_wo:]
        x, y, c = _mesh_pos()
        me, sibling = (x, y, c), (x, y, 1 - c)
        chips = [(1 - x, y), (x, 1 - y), (1 - x, 1 - y)]

        def copy(a, k, block, to):
            blk = outs[a].at[_slot(*block)]
            return pltpu.make_async_remote_copy(
                src_ref=blk, dst_ref=blk, send_sem=send_sems.at[7 * a + k], recv_sem=recv_sems.at[7 * a + k],
                device_id=to, device_id_type=pl.DeviceIdType.MESH)

        sends = []
        for a in range(n):
            mine = outs[a].at[_slot(*me)]

            def cast(r, mine=mine, src=srcs[a]):
                mine[r, :] = src[r, :].astype(BF16)

            _row_chunks(srcs[a].shape[0], cast)
            first = [copy(a, 0, me, sibling)] + [copy(a, 1 + j, me, (*chip, c)) for j, chip in enumerate(chips)]
            for cp in first:
                cp.start()
            sends += first
        if work is not None:
            work(w_in_refs, w_out_refs)
        for a in range(n):
            for j, chip in enumerate(chips):
                copy(a, 1 + j, (*chip, c), me).wait_recv()
                fwd = copy(a, 4 + j, (*chip, c), sibling)
                fwd.start()
                sends.append(fwd)
        for a in range(n):
            copy(a, 0, sibling, me).wait_recv()
            for j, chip in enumerate(chips):
                copy(a, 4 + j, (*chip, 1 - c), me).wait_recv()
        for cp in sends:
            cp.wait_send()

    vm = pl.BlockSpec(memory_space=pltpu.VMEM)
    res = pl.pallas_call(
        body, name="allgather_weights",
        out_shape=tuple(jax.ShapeDtypeStruct((N_DEV,) + s.shape, BF16) for s in shards) + tuple(work_out_shapes),
        in_specs=[vm] * (n + n_wi), out_specs=(vm,) * (n + n_wo),
        scratch_shapes=[pltpu.SemaphoreType.DMA((7 * n,)), pltpu.SemaphoreType.DMA((7 * n,))],
        compiler_params=pltpu.CompilerParams(vmem_limit_bytes=VMEM_LIMIT),
    )(*shards, *work_inputs)
    return list(res[:n]), list(res[n:])


def _adamw(w, g, m, v):
    m = ADAM_B1 * m + (1.0 - ADAM_B1) * g
    v = ADAM_B2 * v + (1.0 - ADAM_B2) * (g * g)
    m_hat = m / (1.0 - ADAM_B1 ** ADAM_STEP)
    v_hat = v / (1.0 - ADAM_B2 ** ADAM_STEP)
    delta = -ADAM_LR * (m_hat / (jnp.sqrt(v_hat) + ADAM_EPS) + ADAM_WD * w)
    return delta, m, v


def _remote(src, dst, send_sems, recv_sems, k, to):
    return pltpu.make_async_remote_copy(src_ref=src, dst_ref=dst, send_sem=send_sems.at[k], recv_sem=recv_sems.at[k],
                                        device_id=to, device_id_type=pl.DeviceIdType.MESH)


def _big_reduce_phases(g16_r, go_r, outs, send2, recv1, recv2, s_send, s_recv):
    n = len(g16_r)
    x, y, c = _mesh_pos()
    sibling = (x, y, 1 - c)
    chips = [(1 - x, y), (x, 1 - y), (1 - x, 1 - y)]
    all_chips = [(x, y)] + chips
    lvl1 = []
    for a in range(n):
        cps = [_remote(g16_r[a].at[_slot(*chip, 1 - c)], recv1[a].at[j], s_send, s_recv, 7 * a + j, sibling)
               for j, chip in enumerate(all_chips)]
        for cp in cps:
            cp.start()
        lvl1.append(cps)
    yield
    lvl2 = []
    for a in range(n):
        for cp in lvl1[a]:
            cp.wait_recv()
        og = outs[a]

        def partials(r, a=a, og=og):
            og[r, :] = go_r[a][r, :] + recv1[a][0, r, :].astype(F32)
            for j, chip in enumerate(chips):
                mine16 = g16_r[a][_slot(*chip, c), r, :].astype(F32)
                send2[a][j, r, :] = (mine16 + recv1[a][1 + j, r, :].astype(F32)).astype(BF16)

        _row_chunks(go_r[a].shape[0], partials)
        cps = [_remote(send2[a].at[j], recv2[a].at[j], s_send, s_recv, 7 * a + 4 + j, (*chip, c))
               for j, chip in enumerate(chips)]
        for cp in cps:
            cp.start()
        lvl2.append(cps)
    yield
    for a in range(n):
        for cp in lvl2[a]:
            cp.wait_recv()
        og = outs[a]

        def total(r, a=a, og=og):
            g = og[r, :]
            for j in range(3):
                g = g + recv2[a][j, r, :].astype(F32)
            og[r, :] = g

        _row_chunks(go_r[a].shape[0], total)
    yield
    for cps in lvl1 + lvl2:
        for cp in cps:
            cp.wait_send()


def _adamw_update(g, w, m, v, n_streamed):
    n = len(g)
    ns = n_streamed

    def body(*refs):
        g_r, w_r, m_r, v_r = (refs[i * n:(i + 1) * n] for i in range(4))
        outs = refs[4 * n:8 * n]
        in_buf, out_buf = refs[8 * n:8 * n + 4 * ns], refs[8 * n + 4 * ns:8 * n + 8 * ns]
        in_sems, out_sems = refs[8 * n + 8 * ns:]
        loads = [[pltpu.make_async_copy(src[a], in_buf[4 * a + k], in_sems.at[4 * a + k])
                  for k, src in enumerate((g_r, w_r, m_r, v_r))] for a in range(ns)]
        stores = [[pltpu.make_async_copy(out_buf[4 * a + k], outs[4 * a + k], out_sems.at[4 * a + k]) for k in range(4)]
                  for a in range(ns)]
        for cps in loads:
            for cp in cps:
                cp.start()
        for a in range(n):
            if a < ns:
                for cp in loads[a]:
                    cp.wait()
                gs, ws, ms, vs = in_buf[4 * a:4 * a + 4]
                og, od, om, ov = out_buf[4 * a:4 * a + 4]
            else:
                gs, ws, ms, vs = g_r[a], w_r[a], m_r[a], v_r[a]
                og, od, om, ov = outs[4 * a:4 * a + 4]

            def update(idx, gs=gs, ws=ws, ms=ms, vs=vs, og=og, od=od, om=om, ov=ov):
                gv = gs[idx]
                d, nm, nv = _adamw(ws[idx], gv, ms[idx], vs[idx])
                og[idx] = gv
                od[idx] = d
                om[idx] = nm
                ov[idx] = nv

            shape = gs.shape
            if len(shape) == 3:
                for b in range(shape[0]):
                    update(b)
            elif _row_chunk(shape[0]) is not None:
                _row_chunks(shape[0], update)
            else:
                update(Ellipsis)
            if a < ns:
                for cp in stores[a]:
                    cp.start()
        for cps in stores:
            for cp in cps:
                cp.wait()

    vm, hbm = pl.BlockSpec(memory_space=pltpu.VMEM), pl.BlockSpec(memory_space=pl.ANY)
    place = lambda: [hbm] * ns + [vm] * (n - ns)
    buf = [pltpu.VMEM(t.shape, F32) for t in g[:ns] for _ in range(4)]
    res = pl.pallas_call(
        body, name="adamw_update",
        out_shape=tuple(jax.ShapeDtypeStruct(t.shape, F32) for t in g for _ in range(4)),
        in_specs=place() * 4, out_specs=tuple(s for a in range(n) for s in [hbm if a < ns else vm] * 4),
        scratch_shapes=buf + buf + [pltpu.SemaphoreType.DMA((4 * ns,)), pltpu.SemaphoreType.DMA((4 * ns,))],
        compiler_params=pltpu.CompilerParams(vmem_limit_bytes=VMEM_LIMIT),
    )(*g, *w, *m, *v)
    return [res[4 * a:4 * a + 4] for a in range(n)]


TINY = (("pre_norm_g", 1, 1024), ("post_norm_g", 1, 1024), ("pl_b_gate", 1, 1024), ("ssm_d", 1, 512),
        ("ssm_b_glu", 1, 512), ("ssm_log_step", 1, 32), ("attn_sinks", 1, 8), ("ssm_lam_re", 32, 64),
        ("ssm_lam_im", 32, 64))
MEDIUM = (("ssm_b_re", SSM_G * SSM_P, SSM_N), ("ssm_b_im", SSM_G * SSM_P, SSM_N), ("ssm_c_re", SSM_G * SSM_P, SSM_N),
          ("ssm_c_im", SSM_G * SSM_P, SSM_N))


def _stage_rows():
    offs, r = {}, 0
    for name, rows, cols in TINY + (("loss", 1, 1),):
        if rows > 1:
            r = -(-r // 8) * 8
        offs[name] = r
        r += rows if rows > 1 else max(cols // LANES, 1)
    return offs, -(-r // 8) * 8


def _reduce_final(g16, g32, loss, g_tiny, g_med):
    nb_, nt, nm_ = len(g16), len(TINY), len(MEDIUM)
    offs, stage_rows = _stage_rows()

    def body(*refs):
        g16_r, go_r = refs[:nb_], refs[nb_:2 * nb_]
        base = 2 * nb_
        loss_r, gt, gm = refs[base], refs[base + 1:base + 1 + nt], refs[base + 1 + nt:base + 1 + nt + nm_]
        base += 1 + nt + nm_
        out_b = refs[base:base + nb_]
        base += nb_
        loss_o, out_t, out_m = refs[base], refs[base + 1:base + 1 + nt], refs[base + 1 + nt:base + 1 + nt + nm_]
        base += 1 + nt + nm_
        send2_b, recv1_b, recv2_b = (refs[base + i * nb_:base + (i + 1) * nb_] for i in range(3))
        base += 3 * nb_
        stage = refs[base]
        recv1, part, recv2 = (refs[base + 1 + i * nm_:base + 1 + (i + 1) * nm_] for i in range(3))
        bs_send, bs_recv, s_send, s_recv, own_sems = refs[base + 1 + 3 * nm_:base + 6 + 3 * nm_]
        own32 = refs[base + 6 + 3 * nm_:]
        me = _slot(*_mesh_pos())
        fetch = [pltpu.make_async_copy(go_r[a].at[me], own32[a], own_sems.at[a]) for a in range(nb_)]
        for cp in fetch:
            cp.start()
        big = _big_reduce_phases(g16_r, own32, out_b, send2_b, recv1_b, recv2_b, bs_send, bs_recv)
        small = small_phases(loss_r, gt, gm, loss_o, out_t, out_m, stage, recv1, part, recv2, s_send, s_recv)
        next(big)
        next(small)
        for cp in fetch:
            cp.wait()
        next(big)
        for _ in small:
            pass
        for _ in big:
            pass

    def small_phases(loss_r, gt, gm, loss_o, out_t, out_m, stage, recv1, part, recv2, s_send, s_recv):
        x, y, c = _mesh_pos()
        me = _slot(x, y, c)
        sibling = (x, y, 1 - c)
        chips = [(1 - x, y), (x, 1 - y), (1 - x, 1 - y)]
        all_chips = [(x, y)] + chips
        peers = [sibling] + [(*chip, c) for chip in chips] + [(*chip, 1 - c) for chip in chips]
        sem = iter(range(7 + 14 * nm_))
        lvl1 = []
        for a in range(nm_):
            cps = [_remote(gm[a].at[_slot(*chip, 1 - c)], recv1[a].at[j], s_send, s_recv, next(sem), sibling)
                   for j, chip in enumerate(all_chips)]
            for cp in cps:
                cp.start()
            lvl1.append(cps)
        mine = stage.at[me]
        mine[...] = jnp.zeros((stage_rows, LANES), F32)
        for (name, rows, cols), ref in zip(TINY + (("loss", 1, 1),), gt + (loss_r,)):
            r0 = offs[name]
            if rows > 1:
                mine[r0:r0 + rows, 0:cols] = ref[...]
            elif cols >= LANES:
                for i in range(cols // LANES):
                    mine[r0 + i:r0 + i + 1, :] = ref[:, i * LANES:(i + 1) * LANES]
            else:
                mine[r0:r0 + 1, 0:cols] = ref[...]
        tiny_cps = [_remote(mine, mine, s_send, s_recv, next(sem), peer) for peer in peers]
        for cp in tiny_cps:
            cp.start()
        yield
        lvl2 = []
        for a in range(nm_):
            for cp in lvl1[a]:
                cp.wait_recv()
            for j, chip in enumerate(all_chips):
                part[a][j] = gm[a][_slot(*chip, c)] + recv1[a][j]
            cps = [_remote(part[a].at[1 + j], recv2[a].at[j], s_send, s_recv, next(sem), (*chip, c))
                   for j, chip in enumerate(chips)]
            for cp in cps:
                cp.start()
            lvl2.append(cps)
        yield
        lvl3 = []
        for a in range(nm_):
            for cp in lvl2[a]:
                cp.wait_recv()
            blk = out_m[a].at[me]
            blk[...] = ((part[a][0] + recv2[a][0]) + recv2[a][1]) + recv2[a][2]
            cps = [_remote(blk, blk, s_send, s_recv, next(sem), peer) for peer in peers]
            for cp in cps:
                cp.start()
            lvl3.append(cps)
        yield
        for cp in tiny_cps:
            cp.wait_recv()
        tot = stage[0]
        for d in range(1, N_DEV):
            tot = tot + stage[d]
        loss_o[...] = tot[offs["loss"]:offs["loss"] + 1, 0:1]
        for k, (name, rows, cols) in enumerate(TINY):
            r0 = offs[name]
            if rows > 1:
                out_t[k][...] = tot[r0:r0 + rows, 0:cols]
            elif cols >= LANES:
                for i in range(cols // LANES):
                    out_t[k][:, i * LANES:(i + 1) * LANES] = tot[r0 + i:r0 + i + 1, :]
            else:
                out_t[k][...] = tot[r0:r0 + 1, 0:cols]
        for cps in lvl3:
            for cp in cps:
                cp.wait_recv()
        for cps in lvl1 + lvl2 + lvl3 + [tiny_cps]:
            for cp in cps:
                cp.wait_send()

    vmem = pl.BlockSpec(memory_space=pltpu.VMEM)
    t_shapes = [jax.ShapeDtypeStruct((rows, cols), F32) for _, rows, cols in TINY]
    m_shapes = [jax.ShapeDtypeStruct((N_DEV, rows // N_DEV, cols), F32) for _, rows, cols in MEDIUM]
    blk = [(rows // N_DEV, cols) for _, rows, cols in MEDIUM]
    shard = [g.shape[1:] for g in g16]
    scratch = ([pltpu.VMEM((3,) + s, BF16) for s in shard] + [pltpu.VMEM((4,) + s, BF16) for s in shard]
               + [pltpu.VMEM((3,) + s, BF16) for s in shard]
               + [pltpu.VMEM((N_DEV, stage_rows, LANES), F32)]
               + [pltpu.VMEM((4,) + b, F32) for b in blk] + [pltpu.VMEM((4,) + b, F32) for b in blk]
               + [pltpu.VMEM((3,) + b, F32) for b in blk]
               + [pltpu.SemaphoreType.DMA((7 * nb_,)), pltpu.SemaphoreType.DMA((7 * nb_,)),
                  pltpu.SemaphoreType.DMA((7 + 14 * nm_,)), pltpu.SemaphoreType.DMA((7 + 14 * nm_,)),
                  pltpu.SemaphoreType.DMA((nb_,))]
               + [pltpu.VMEM(s, F32) for s in shard])
    n_out = nb_ + 1 + nt + nm_
    res = pl.pallas_call(
        body, name="reduce_final",
        out_shape=tuple(jax.ShapeDtypeStruct(s, F32) for s in shard) + (jax.ShapeDtypeStruct((1, 1), F32),)
        + tuple(t_shapes) + tuple(m_shapes),
        in_specs=[vmem] * nb_ + [pl.BlockSpec(memory_space=pl.ANY)] * nb_ + [vmem] * (1 + nt + nm_),
        out_specs=(vmem,) * n_out, scratch_shapes=scratch,
        compiler_params=pltpu.CompilerParams(vmem_limit_bytes=VMEM_LIMIT),
    )(*g16, *g32, loss, *g_tiny, *g_med)
    return list(res[:nb_]), res[nb_], list(res[nb_ + 1:nb_ + 1 + nt]), list(res[nb_ + 1 + nt:])


def _gather_phases(shard_r, gath, cast, send_sems, recv_sems, local_sems):
    n = len(shard_r)
    x, y, c = _mesh_pos()
    me, sibling = (x, y, c), (x, y, 1 - c)
    chips = [(1 - x, y), (x, 1 - y), (1 - x, 1 - y)]

    def own(a, k, to):
        return _remote(cast[a], gath[a].at[_slot(*me)], send_sems, recv_sems, 7 * a + k, to)

    def passed(a, k, block, to):
        blk = gath[a].at[_slot(*block)]
        return _remote(blk, blk, send_sems, recv_sems, 7 * a + k, to)

    def keep(a):
        return pltpu.make_async_copy(cast[a], gath[a].at[_slot(*me)], local_sems.at[a])

    def start():
        for a in range(n):
            def to16(r, a=a):
                cast[a][r, :] = shard_r[a][r, :].astype(BF16)

            _row_chunks(shard_r[a].shape[0], to16)
            keep(a).start()
            own(a, 0, sibling).start()
            for j, chip in enumerate(chips):
                own(a, 1 + j, (*chip, c)).start()

    def relay():
        for a in range(n):
            for j, chip in enumerate(chips):
                passed(a, 1 + j, (*chip, c), me).wait_recv()
                passed(a, 4 + j, (*chip, c), sibling).start()

    def finish():
        for a in range(n):
            passed(a, 0, sibling, me).wait_recv()
            for j, chip in enumerate(chips):
                passed(a, 4 + j, (*chip, 1 - c), me).wait_recv()
            own(a, 0, sibling).wait_send()
            for j, chip in enumerate(chips):
                own(a, 1 + j, (*chip, c)).wait_send()
                passed(a, 4 + j, (*chip, c), sibling).wait_send()
            keep(a).wait()

    return start, relay, finish


def _gather_operands(shards):
    n = len(shards)
    return ((pl.BlockSpec(memory_space=pl.ANY),) * n,
            tuple(jax.ShapeDtypeStruct((N_DEV,) + s.shape, BF16) for s in shards),
            [pltpu.VMEM(s.shape, BF16) for s in shards]
            + [pltpu.SemaphoreType.DMA((7 * n,)), pltpu.SemaphoreType.DMA((7 * n,)), pltpu.SemaphoreType.DMA((n,))])


def _hosted_reduce_phases(g16_r, g32_r, red, own16, recv1, send2, recv2, own32, s_send, s_recv, s_local):
    n = len(g16_r)
    x, y, c = _mesh_pos()
    sibling = (x, y, 1 - c)
    chips = [(1 - x, y), (x, 1 - y), (1 - x, 1 - y)]
    all_chips = [(x, y)] + chips

    def lvl1(a, j):
        return _remote(g16_r[a].at[_slot(*all_chips[j], 1 - c)], recv1[a].at[j], s_send, s_recv, 7 * a + j, sibling)

    def lvl2(a, j):
        return _remote(send2[a].at[j], recv2[a].at[j], s_send, s_recv, 7 * a + 4 + j, (*chips[j], c))

    def mine(a, j):
        if j == 3:
            return pltpu.make_async_copy(g32_r[a].at[_slot(x, y, c)], own32[a], s_local.at[4 * a + j])
        return pltpu.make_async_copy(g16_r[a].at[_slot(*chips[j], c)], own16[a].at[j], s_local.at[4 * a + j])

    def start():
        for a in range(n):
            for j in range(4):
                mine(a, j).start()
            for j in range(4):
                lvl1(a, j).start()

    def middle():
        for a in range(n):
            for j in range(4):
                mine(a, j).wait()
            for j in range(4):
                lvl1(a, j).wait_recv()

            def partials(r, a=a):
                red[a][r, :] = own32[a][r, :] + recv1[a][0, r, :].astype(F32)
                for j in range(3):
                    send2[a][j, r, :] = (own16[a][j, r, :].astype(F32) + recv1[a][1 + j, r, :].astype(F32)).astype(BF16)

            _row_chunks(own32[a].shape[0], partials)
            for j in range(3):
                lvl2(a, j).start()

    def total():
        for a in range(n):
            for j in range(3):
                lvl2(a, j).wait_recv()

            def add(r, a=a):
                g = red[a][r, :]
                for j in range(3):
                    g = g + recv2[a][j, r, :].astype(F32)
                red[a][r, :] = g

            _row_chunks(own32[a].shape[0], add)

    def finish():
        for a in range(n):
            for j in range(4):
                lvl1(a, j).wait_send()
            for j in range(3):
                lvl2(a, j).wait_send()

    return start, middle, total, finish


def _hosted_reduce_operands(g16, const_spec):
    n = len(g16)
    shard = [g.shape[1:] for g in g16]
    return ([pl.BlockSpec(memory_space=pl.ANY)] * (2 * n),
            tuple(const_spec(s) for s in shard),
            tuple(jax.ShapeDtypeStruct(s, F32) for s in shard),
            [pltpu.VMEM((3,) + s, BF16) for s in shard] + [pltpu.VMEM((4,) + s, BF16) for s in shard]
            + [pltpu.VMEM((3,) + s, BF16) for s in shard] + [pltpu.VMEM((3,) + s, BF16) for s in shard]
            + [pltpu.VMEM(s, F32) for s in shard]
            + [pltpu.SemaphoreType.DMA((7 * n,)), pltpu.SemaphoreType.DMA((7 * n,)), pltpu.SemaphoreType.DMA((4 * n,))])


def _in_proj(x2, g_pre, w_in, tm):
    t = x2.shape[0]

    def body(x_ref, g_ref, w_ref, u_ref, zs_ref, q_ref, k_ref, v_ref, za_ref):
        xv = x_ref[...]
        r = lax.rsqrt(jnp.mean(xv * xv, axis=-1, keepdims=True) + EPS)
        hn = xv * r * g_ref[...]
        proj = _mm_nt(hn, w_ref[...])
        u_ref[...] = proj[:, 0:512]
        zs_ref[...] = proj[:, 512:1024]
        q_ref[...] = proj[:, 1024:1536].astype(BF16)
        k_ref[...] = proj[:, 1536:1664].astype(BF16)
        v_ref[...] = proj[:, 1664:1792].astype(BF16)
        za_ref[...] = proj[:, 1792:2304]

    row = lambda w: pl.BlockSpec((tm, w), lambda i: (i, 0))
    return pl.pallas_call(
        body, name="in_proj", grid=(t // tm,),
        in_specs=[row(D_MODEL), _const_spec((1, D_MODEL)), _const_spec((D_IN, D_MODEL))],
        out_specs=(row(512), row(512), row(512), row(128), row(128), row(512)),
        out_shape=(jax.ShapeDtypeStruct((t, 512), F32),
                   jax.ShapeDtypeStruct((t, 512), F32), jax.ShapeDtypeStruct((t, 512), BF16),
                   jax.ShapeDtypeStruct((t, 128), BF16), jax.ShapeDtypeStruct((t, 128), BF16),
                   jax.ShapeDtypeStruct((t, 512), F32)),
        compiler_params=_tc_params(("arbitrary",)),
    )(x2, g_pre, w_in)


def _discretise(lr, li, ls):
    step = jnp.exp(ls)
    mag = jnp.exp(lr * step)
    ar = mag * jnp.cos(li * step)
    ai = mag * jnp.sin(li * step)
    den = lr * lr + li * li
    cr = ((ar - 1.0) * lr + ai * li) / den
    ci = (ai * lr - (ar - 1.0) * li) / den
    return step, ar, ai, den, cr, ci


def _per_channel(v):
    return jnp.broadcast_to(v[:, None, :], (SSM_G, SSM_P, SSM_N)).reshape(SSM_G * SSM_P, SSM_N)


def _tile_masks():
    r = lax.broadcasted_iota(jnp.int32, (CH_T, ST_T), 0) // SSM_P
    l = lax.broadcasted_iota(jnp.int32, (CH_T, ST_T), 1) // SSM_N
    lt = lax.broadcasted_iota(jnp.int32, (ST_T, CH_T), 0) // SSM_N
    rt = lax.broadcasted_iota(jnp.int32, (ST_T, CH_T), 1) // SSM_P
    rep = lax.broadcasted_iota(jnp.int32, (SSM_N, ST_T), 0) == lax.broadcasted_iota(jnp.int32, (SSM_N, ST_T), 1) % SSM_N
    rep_t = lax.broadcasted_iota(jnp.int32, (ST_T, SSM_N), 0) % SSM_N == lax.broadcasted_iota(jnp.int32, (ST_T, SSM_N), 1)
    return r == l, lt == rt, rep, rep_t


def _ssm_prep(lam_re, lam_im, log_step, b_re, b_im, c_re, c_im, seg):
    def work(in_refs, out_refs):
        lr_ref, li_ref, ls_ref, br_ref, bi_ref, cre_ref, cim_ref, lrr_ref, lir_ref, lsr_ref = in_refs
        ar_ref, ai_ref, pr_ref, pi_ref, bcat_ref, bcat_t_ref, ccat_ref, ccat_t_ref = out_refs
        _, _, _, _, cr, ci = _discretise(lr_ref[...], li_ref[...], ls_ref[...])
        cr, ci = _per_channel(cr), _per_channel(ci)
        br, bi = br_ref[...], bi_ref[...]
        bb_re = cr * br - ci * bi
        bb_im = cr * bi + ci * br
        same, same_t, rep, rep_t = _tile_masks()
        rep, rep_t = rep.astype(BF16), rep_t.astype(BF16)
        for j in range(N_GT):
            rows = slice(j * CH_T, (j + 1) * CH_T)
            for wide, tall, parts in ((bcat_ref, bcat_t_ref, (bb_re[rows], bb_im[rows])),
                                      (ccat_t_ref, ccat_ref, (cre_ref[rows, :], -cim_ref[rows, :]))):
                for k, part in enumerate(parts):
                    p16 = part.astype(BF16)
                    wide[j, :, k * ST_T:(k + 1) * ST_T] = jnp.where(same, _mm(p16, rep), 0.0).astype(BF16)
                    tall[j, k * ST_T:(k + 1) * ST_T, :] = jnp.where(same_t, _mm_nt(rep_t, p16), 0.0).astype(BF16)
        stepr = jnp.exp(lsr_ref[...])
        mag = jnp.exp(lrr_ref[...] * stepr)
        a_r, a_i = mag * jnp.cos(lir_ref[...] * stepr), mag * jnp.sin(lir_ref[...] * stepr)
        p_r, p_i = a_r, a_i
        for k in range(8):
            pr_ref[k:k + 1, :] = p_r
            pi_ref[k:k + 1, :] = p_i
            p_r, p_i = p_r * a_r - p_i * a_i, p_r * a_i + p_i * a_r
        n = 8
        while n < seg:
            tr, ti = pr_ref[n - 1:n, :], pi_ref[n - 1:n, :]
            xr, xi = pr_ref[0:n, :], pi_ref[0:n, :]
            pr_ref[n:2 * n, :] = xr * tr - xi * ti
            pi_ref[n:2 * n, :] = xr * ti + xi * tr
            n *= 2
        ar_ref[...] = pr_ref[0:1, :]
        ai_ref[...] = pi_ref[0:1, :]

    row = jax.ShapeDtypeStruct((1, N_STATE), F32)
    pw = jax.ShapeDtypeStruct((seg, N_STATE), F32)
    wide = jax.ShapeDtypeStruct((N_GT, CH_T, 2 * ST_T), BF16)
    tall = jax.ShapeDtypeStruct((N_GT, 2 * ST_T, CH_T), BF16)
    step_row = jnp.broadcast_to(log_step, (SSM_G, SSM_N)).reshape(1, N_STATE)
    inputs = (lam_re, lam_im, log_step, b_re, b_im, c_re, c_im, lam_re.reshape(1, N_STATE),
              lam_im.reshape(1, N_STATE), step_row)
    return work, inputs, (row, row, pw, pw, wide, tall, tall, wide)


def _seg_rows(t):
    if isinstance(t, int):
        return pl.ds(t * N_SEG, N_SEG)
    return pl.ds(pl.multiple_of(t * N_SEG, N_SEG), N_SEG)


def _scan_forward(xs, a_re, a_im, pw_re, pw_im, cs, seg):
    are = jnp.broadcast_to(a_re, (N_SEG, ST_T))
    aim = jnp.broadcast_to(a_im, (N_SEG, ST_T))

    def steps(k, carry):
        xr, xi = carry
        for j in range(SCAN_UNROLL):
            r = pl.multiple_of((k * SCAN_UNROLL + j) * N_SEG, N_SEG)
            nr = are * xr - aim * xi + xs[pl.ds(r, N_SEG), 0:ST_T]
            ni = are * xi + aim * xr + xs[pl.ds(r, N_SEG), ST_T:2 * ST_T]
            xs[pl.ds(r, N_SEG), 0:ST_T] = nr
            xs[pl.ds(r, N_SEG), ST_T:2 * ST_T] = ni
            xr, xi = nr, ni
        return xr, xi

    zero = jnp.zeros((N_SEG, ST_T), F32)
    fr, fi = lax.fori_loop(0, seg // SCAN_UNROLL, steps, (zero, zero))
    sr, si = pw_re[seg - 1:seg, :], pw_im[seg - 1:seg, :]
    cr = jnp.zeros((1, ST_T), F32)
    ci = jnp.zeros((1, ST_T), F32)
    cs[0:1, :] = cr
    cs[8:9, :] = ci
    for s in range(1, N_SEG):
        ncr = sr * cr - si * ci + fr[s - 1:s, :]
        nci = sr * ci + si * cr + fi[s - 1:s, :]
        cr, ci = ncr, nci
        cs[s:s + 1, :] = cr
        cs[8 + s:9 + s, :] = ci
    car, cai = cs[0:8, :], cs[8:16, :]

    def fix(t, _):
        r = pl.multiple_of(t * N_SEG, N_SEG)
        pr, pi = pw_re[pl.ds(t, 1), :], pw_im[pl.ds(t, 1), :]
        xs[pl.ds(r, N_SEG), 0:ST_T] = xs[pl.ds(r, N_SEG), 0:ST_T] + (pr * car - pi * cai)
        xs[pl.ds(r, N_SEG), ST_T:2 * ST_T] = xs[pl.ds(r, N_SEG), ST_T:2 * ST_T] + (pr * cai + pi * car)
        return 0

    lax.fori_loop(0, seg, fix, 0, unroll=SCAN_UNROLL)


def _interleave(src, dst, seg):
    for s in range(N_SEG):
        dst[pl.ds(s, seg, stride=N_SEG), :] = src[s]


def _deinterleave(src, seg, s):
    return src[pl.ds(s, seg, stride=N_SEG), :]


def _ssm_forward(u, bcat, ccat, a_re, a_im, pw_re, pw_im, d_row, late, seg):
    bl = u.shape[0]
    rows = N_SEG * seg
    n = len(late)
    steps = bl * N_GT

    def body(*refs):
        u_ref, b_ref, c_ref, ar_ref, ai_ref, pr_ref, pi_ref, d_ref = refs[:8]
        late_r = refs[8:8 + n]
        y_ref, xs_ref, cs_ref = refs[8 + n:11 + n]
        gath, cast = refs[11 + n:11 + 2 * n], refs[11 + 2 * n:11 + 3 * n]
        send_sems, recv_sems, local_sems, ui, yi = refs[11 + 3 * n:]
        step = pl.program_id(0) * N_GT + pl.program_id(1)
        start, relay, finish = _gather_phases(late_r, gath, cast, send_sems, recv_sems, local_sems)
        pl.when(step == 0)(start)
        _interleave(u_ref.at[0], ui, seg)
        u = ui[...]
        xs, cs = xs_ref.at[0, 0], cs_ref.at[0, 0]
        xs[...] = _mm(u, b_ref[0])
        _scan_forward(xs, ar_ref[...], ai_ref[...], pr_ref, pi_ref, cs, seg)
        yi[...] = _mm(xs[...], c_ref[0]) + d_ref[...] * u
        for s in range(N_SEG):
            y_ref[0, s] = _deinterleave(yi, seg, s)
        pl.when(step == steps // 2)(relay)
        pl.when(step == steps - 1)(finish)

    state = lambda r, c: pl.BlockSpec((1, 1, r, c), lambda b, j: (b, j, 0, 0))
    act = pl.BlockSpec((1, N_SEG, seg, CH_T), lambda b, j: (b, 0, 0, j))
    g_specs, g_shapes, g_scratch = _gather_operands(late)
    res = pl.pallas_call(
        body, name="ssm_forward", grid=(bl, N_GT),
        in_specs=[act,
                  pl.BlockSpec((1, CH_T, 2 * ST_T), lambda b, j: (j, 0, 0)),
                  pl.BlockSpec((1, 2 * ST_T, CH_T), lambda b, j: (j, 0, 0)),
                  pl.BlockSpec((1, ST_T), lambda b, j: (0, j)), pl.BlockSpec((1, ST_T), lambda b, j: (0, j)),
                  pl.BlockSpec((seg, ST_T), lambda b, j: (0, j)), pl.BlockSpec((seg, ST_T), lambda b, j: (0, j)),
                  pl.BlockSpec((1, CH_T), lambda b, j: (0, j))]
        + [pl.BlockSpec(s.shape, lambda b, j: (0, 0)) for s in late],
        out_specs=(act, state(rows, 2 * ST_T), state(16, ST_T)) + g_specs,
        out_shape=(jax.ShapeDtypeStruct((bl, N_SEG, seg, D_SSM), F32),
                   jax.ShapeDtypeStruct((bl, N_GT, rows, 2 * ST_T), F32),
                   jax.ShapeDtypeStruct((bl, N_GT, 16, ST_T), F32)) + g_shapes,
        scratch_shapes=g_scratch + [pltpu.VMEM((rows, CH_T), F32), pltpu.VMEM((rows, CH_T), F32)],
        compiler_params=_tc_params(("arbitrary", "arbitrary")),
    )(u, bcat, ccat, a_re, a_im, pw_re, pw_im, d_row, *late)
    return res[:3], list(res[3:])


def _ssm_backward(u, dy, states, carries, bcat_t, ccat_t, a_re, a_im, pw_re, pw_im, d_row, late16, late32, seg):
    bl = u.shape[0]
    rows = N_SEG * seg
    n = len(late16)
    grid_steps = N_GT * bl

    def body(*refs):
        u_ref, dy_ref, xs_ref, cs_ref, bt_ref, ct_ref, ar_ref, ai_ref, pr_ref, pi_ref, d_ref = refs[:11]
        g16_r, g32_r = refs[11:11 + n], refs[11 + n:11 + 2 * n]
        du_ref, db_ref, dc_ref, dar_ref, dai_ref, dd_ref = refs[11 + 2 * n:17 + 2 * n]
        red = refs[17 + 2 * n:17 + 3 * n]
        own16, recv1, send2, recv2, own32 = (refs[17 + 3 * n + k * n:17 + 3 * n + (k + 1) * n] for k in range(5))
        s_send, s_recv, s_local, ls, cl, ui, dyi, dui = refs[17 + 8 * n:]
        b = pl.program_id(1)
        step = pl.program_id(0) * bl + b
        start, middle, total, finish = _hosted_reduce_phases(g16_r, g32_r, red, own16, recv1, send2, recv2, own32,
                                                             s_send, s_recv, s_local)
        pl.when(step == 0)(start)
        pl.when(step == grid_steps // 4)(middle)
        pl.when(step == (grid_steps * 3) // 4)(total)
        pl.when(step == grid_steps - 1)(finish)
        _interleave(u_ref.at[0], ui, seg)
        _interleave(dy_ref.at[0], dyi, seg)
        u = ui[...]
        dy = dyi[...]
        xs, cs = xs_ref.at[0, 0], cs_ref.at[0, 0]
        ls[...] = _mm(dy, ct_ref[0])
        are = jnp.broadcast_to(ar_ref[...], (N_SEG, ST_T))
        aim = jnp.broadcast_to(ai_ref[...], (N_SEG, ST_T))

        def steps(k, carry):
            lr, li = carry
            for j in range(SCAN_UNROLL):
                r = pl.multiple_of((seg - 1 - (k * SCAN_UNROLL + j)) * N_SEG, N_SEG)
                nr = are * lr + aim * li + ls[pl.ds(r, N_SEG), 0:ST_T]
                ni = are * li - aim * lr + ls[pl.ds(r, N_SEG), ST_T:2 * ST_T]
                ls[pl.ds(r, N_SEG), 0:ST_T] = nr
                ls[pl.ds(r, N_SEG), ST_T:2 * ST_T] = ni
                lr, li = nr, ni
            return lr, li

        zero = jnp.zeros((N_SEG, ST_T), F32)
        fr, fi = lax.fori_loop(0, seg // SCAN_UNROLL, steps, (zero, zero))
        sr, si = pr_ref[seg - 1:seg, :], pi_ref[seg - 1:seg, :]
        cr = jnp.zeros((1, ST_T), F32)
        ci = jnp.zeros((1, ST_T), F32)
        cl[7:8, :] = cr
        cl[15:16, :] = ci
        for s in range(N_SEG - 2, -1, -1):
            ncr = sr * cr + si * ci + fr[s + 1:s + 2, :]
            nci = sr * ci - si * cr + fi[s + 1:s + 2, :]
            cr, ci = ncr, nci
            cl[s:s + 1, :] = cr
            cl[8 + s:9 + s, :] = ci
        clr, cli = cl[0:8, :], cl[8:16, :]

        def fix_rows(rows, t, xpr, xpi, acc):
            dr, di = acc
            pr, pi = pr_ref[pl.ds(seg - 1 - t, 1), :], pi_ref[pl.ds(seg - 1 - t, 1), :]
            lr = ls[rows, 0:ST_T] + (pr * clr + pi * cli)
            li = ls[rows, ST_T:2 * ST_T] + (pr * cli - pi * clr)
            ls[rows, 0:ST_T] = lr
            ls[rows, ST_T:2 * ST_T] = li
            return dr + (lr * xpr + li * xpi), di + (li * xpr - lr * xpi)

        def fix_at(t, acc):
            prev = _seg_rows(t - 1)
            return fix_rows(_seg_rows(t), t, xs[prev, 0:ST_T], xs[prev, ST_T:2 * ST_T], acc)

        def fix(k, acc):
            for j in range(SCAN_UNROLL):
                acc = fix_at(k * SCAN_UNROLL + j, acc)
            return acc

        acc = fix_rows(pl.ds(0, N_SEG), 0, cs[0:8, :], cs[8:16, :], (zero, zero))
        for t in range(1, SCAN_UNROLL):
            acc = fix_at(t, acc)
        dr, di = lax.fori_loop(1, seg // SCAN_UNROLL, fix, acc)
        dar = jnp.sum(dr, axis=0, keepdims=True)
        dai = jnp.sum(di, axis=0, keepdims=True)
        lall = ls[...]
        dui[...] = _mm(lall, bt_ref[0]) + d_ref[...] * dy
        for s in range(N_SEG):
            du_ref[0, s] = _deinterleave(dui, seg, s).astype(BF16)
        dbp = _mm_tn(u, lall)
        dcp = _mm_tn(dy, xs[...])
        ddp = jnp.sum(dy * u, axis=0, keepdims=True)

        @pl.when(b == 0)
        def _():
            db_ref[0] = dbp
            dc_ref[0] = dcp
            dar_ref[...] = dar
            dai_ref[...] = dai
            dd_ref[...] = ddp

        @pl.when(b != 0)
        def _():
            db_ref[0] += dbp
            dc_ref[0] += dcp
            dar_ref[...] += dar
            dai_ref[...] += dai
            dd_ref[...] += ddp

    tile3 = lambda r, c: pl.BlockSpec((1, r, c), lambda j, b: (j, 0, 0))
    lane = lambda r, c: pl.BlockSpec((r, c), lambda j, b: (0, j))
    act = pl.BlockSpec((1, N_SEG, seg, CH_T), lambda j, b: (b, 0, 0, j))
    state = lambda r, c: pl.BlockSpec((1, 1, r, c), lambda j, b: (b, j, 0, 0))
    r_in, r_out, r_shapes, r_scratch = _hosted_reduce_operands(late16, lambda s: pl.BlockSpec(s, lambda j, b: (0, 0)))
    res = pl.pallas_call(
        body, name="ssm_backward", grid=(N_GT, bl),
        in_specs=[act, act, state(rows, 2 * ST_T), state(16, ST_T), tile3(2 * ST_T, CH_T), tile3(CH_T, 2 * ST_T),
                  lane(1, ST_T), lane(1, ST_T), lane(seg, ST_T), lane(seg, ST_T), lane(1, CH_T)] + r_in,
        out_specs=(act, tile3(CH_T, 2 * ST_T), tile3(CH_T, 2 * ST_T), lane(1, ST_T), lane(1, ST_T), lane(1, CH_T))
        + r_out,
        out_shape=(jax.ShapeDtypeStruct((bl, N_SEG, seg, D_SSM), BF16),
                   jax.ShapeDtypeStruct((N_GT, CH_T, 2 * ST_T), F32), jax.ShapeDtypeStruct((N_GT, CH_T, 2 * ST_T), F32),
                   jax.ShapeDtypeStruct((1, N_STATE), F32), jax.ShapeDtypeStruct((1, N_STATE), F32),
                   jax.ShapeDtypeStruct((1, D_SSM), F32)) + r_shapes,
        scratch_shapes=r_scratch + [pltpu.VMEM((rows, 2 * ST_T), F32), pltpu.VMEM((16, ST_T), F32)]
        + [pltpu.VMEM((rows, CH_T), F32)] * 3,
        compiler_params=_tc_params(("arbitrary", "arbitrary")),
    )(u, dy, states, carries, bcat_t, ccat_t, a_re, a_im, pw_re, pw_im, d_row, *late16, *late32)
    return res[:6], list(res[6:])


def _ssm_param_grads(lam_re, lam_im, log_step, b_re, b_im, da_re, da_im, d_bcat, d_ccat_t):
    def body(lr_ref, li_ref, ls_ref, br_ref, bi_ref, gar_ref, gai_ref, gbcat_ref, gccat_ref,
             dlr_ref, dli_ref, dls_ref, dbr_ref, dbi_ref, dcr_ref, dci_ref, gbr_s, gbi_s):
        same, _, _, rep_t = _tile_masks()
        rep_t = rep_t.astype(F32)
        for j in range(N_GT):
            rows = slice(j * CH_T, (j + 1) * CH_T)
            for src, dsts in ((gbcat_ref, (gbr_s, gbi_s)), (gccat_ref, (dcr_ref, dci_ref))):
                for k, dst in enumerate(dsts):
                    blk = jnp.where(same, src[j, :, k * ST_T:(k + 1) * ST_T], 0.0)
                    dst[rows, :] = jnp.dot(blk, rep_t, precision=lax.Precision.HIGHEST, preferred_element_type=F32)
        dci_ref[...] = -dci_ref[...]
        lr, li = lr_ref[...], li_ref[...]
        step, ar, ai, den, cr, ci = _discretise(lr, li, ls_ref[...])
        crb, cib = _per_channel(cr), _per_channel(ci)
        br, bi = br_ref[...], bi_ref[...]
        gbr, gbi = gbr_s[...], gbi_s[...]
        dbr_ref[...] = crb * gbr + cib * gbi
        dbi_ref[...] = crb * gbi - cib * gbr
        over_channels = lambda t: jnp.sum(t.reshape(SSM_G, SSM_P, SSM_N), axis=1)
        gcr = over_channels(br * gbr + bi * gbi)
        gci = over_channels(br * gbi - bi * gbr)
        ilr, ili = lr / den, -li / den
        gar = gar_ref[...] + (ilr * gcr + ili * gci)
        gai = gai_ref[...] + (ilr * gci - ili * gcr)
        qr, qi = cr * ilr - ci * ili, cr * ili + ci * ilr
        glr = -(qr * gcr + qi * gci)
        gli = -(qr * gci - qi * gcr)
        gwr = ar * gar + ai * gai
        gwi = ar * gai - ai * gar
        dlr_ref[...] = glr + step * gwr
        dli_ref[...] = gli + step * gwi
        dls_ref[...] = jnp.sum(lr * gwr + li * gwi, axis=-1, keepdims=True) * step

    lam = jax.ShapeDtypeStruct((SSM_G, SSM_N), F32)
    mat = jax.ShapeDtypeStruct((SSM_G * SSM_P, SSM_N), F32)
    vm = pl.BlockSpec(memory_space=pltpu.VMEM)
    return pl.pallas_call(
        body, name="ssm_param_grads", out_shape=(lam, lam, jax.ShapeDtypeStruct((SSM_G, 1), F32), mat, mat, mat, mat),
        in_specs=[vm] * 9, out_specs=(vm,) * 7,
        scratch_shapes=[pltpu.VMEM((SSM_G * SSM_P, SSM_N), F32), pltpu.VMEM((SSM_G * SSM_P, SSM_N), F32)],
    )(lam_re, lam_im, log_step, b_re, b_im, da_re, da_im, d_bcat, d_ccat_t)


ROWS4 = Q_PER_KV * ATT_BLOCK
ATT_FWD_STACK = 1


def _att_dist_mask(first_block):
    qi = lax.broadcasted_iota(jnp.int32, (ROWS4, 2 * ATT_BLOCK), 0) & (ATT_BLOCK - 1)
    si = lax.broadcasted_iota(jnp.int32, (ROWS4, 2 * ATT_BLOCK), 1)
    dist = qi + ATT_BLOCK - si
    valid = (dist >= 0) & (dist < ATT_BLOCK) & ((si >= ATT_BLOCK) | jnp.logical_not(first_block))
    return dist.astype(F32), valid


def _stack_heads(x, kv):
    return jnp.concatenate([x[:, (kv * Q_PER_KV + g) * HEAD_DIM:(kv * Q_PER_KV + g + 1) * HEAD_DIM]
                            for g in range(Q_PER_KV)], axis=0)


def _stack_cols(x, kv):
    return jnp.concatenate([x[:, kv * Q_PER_KV + g:kv * Q_PER_KV + g + 1] for g in range(Q_PER_KV)], axis=0)


def _per_head_col(vals):
    return jnp.concatenate([jnp.full((ATT_BLOCK, 1), v, F32) for v in vals], axis=0)


def _attn_forward(q, k, v, sinks, bl, nb):
    t = q.shape[0]

    def body(sink_ref, q_ref, kp_ref, kc_ref, vp_ref, vc_ref, o_ref, lse_ref):
        i = pl.program_id(1)
        dist4, valid4 = _att_dist_mask(i == 0)
        rows2 = ATT_FWD_STACK * ATT_BLOCK
        dist, valid = dist4[0:rows2, :], valid4[0:rows2, :]
        kk = jnp.concatenate([kp_ref[...], kc_ref[...]], axis=0)
        vv = jnp.concatenate([vp_ref[...], vc_ref[...]], axis=0)
        qv = q_ref[...]
        col = lambda vals: jnp.concatenate([jnp.full((ATT_BLOCK, 1), v, F32) for v in vals], axis=0)
        for h0 in range(0, N_HEADS, ATT_FWD_STACK):
            heads = range(h0, h0 + ATT_FWD_STACK)
            kv = h0 // Q_PER_KV
            slope = col([2.0 ** (-(h + 1)) for h in heads])
            sink = col([sink_ref[h] for h in heads])
            qh = jnp.concatenate([qv[:, h * HEAD_DIM:(h + 1) * HEAD_DIM] for h in heads], axis=0)
            kh = kk[:, kv * HEAD_DIM:(kv + 1) * HEAD_DIM]
            vh = vv[:, kv * HEAD_DIM:(kv + 1) * HEAD_DIM]
            s = _mm_nt(qh, kh) * ATT_SCALE - slope * dist
            s = jnp.where(valid, s, NEG_INF)
            m = jnp.maximum(jnp.max(s, axis=-1, keepdims=True), sink)
            e = jnp.exp(s - m)
            den = jnp.sum(e, axis=-1, keepdims=True) + jnp.exp(sink - m)
            o = _mm(e, vh) * (1.0 / den)
            lse = m + jnp.log(den)
            for g, h in enumerate(heads):
                rows = slice(g * ATT_BLOCK, (g + 1) * ATT_BLOCK)
                o_ref[:, h * HEAD_DIM:(h + 1) * HEAD_DIM] = o[rows, :]
                lse_ref[:, h:h + 1] = lse[rows, :]

    cur = lambda w: pl.BlockSpec((ATT_BLOCK, w), lambda b, i: (b * nb + i, 0))
    prev = lambda w: pl.BlockSpec((ATT_BLOCK, w), lambda b, i: (b * nb + jnp.maximum(i - 1, 0), 0))
    return pl.pallas_call(
        body, name="attn_forward", grid=(bl, nb),
        in_specs=[pl.BlockSpec(memory_space=pltpu.SMEM), cur(512), prev(128), cur(128), prev(128), cur(128)],
        out_specs=(cur(512), cur(N_HEADS)),
        out_shape=(jax.ShapeDtypeStruct((t, D_ATTN), F32), jax.ShapeDtypeStruct((t, N_HEADS), F32)),
        compiler_params=_tc_params(("arbitrary", "arbitrary")),
    )(sinks, q, k, k, v, v)


def _attn_backward(q, k, v, o, do, lse, sinks, bl, nb):
    t = q.shape[0]

    def body(sink_ref, qc_ref, kp_ref, kc_ref, vp_ref, vc_ref, oc_ref, doc_ref, lc_ref,
             dq_ref, dk_ref, dv_ref, ds_ref, dk_carry, dv_carry):
        b, i = pl.program_id(0), pl.program_id(1)
        live = i < nb

        @pl.when(i == 0)
        def _():
            dk_carry[...] = jnp.zeros((ATT_BLOCK, KV_HEADS * HEAD_DIM), F32)
            dv_carry[...] = jnp.zeros((ATT_BLOCK, KV_HEADS * HEAD_DIM), F32)

        dist, valid = _att_dist_mask(i == 0)
        valid = valid & live
        kk = jnp.concatenate([kp_ref[...], kc_ref[...]], axis=0)
        vv = jnp.concatenate([vp_ref[...], vc_ref[...]], axis=0)
        qc, oc, doc, lc = qc_ref[...], oc_ref[...], doc_ref[...], lc_ref[...]
        dsink_cols, dq_parts, dk_t, dv_t = [], [], [], []
        for kv in range(KV_HEADS):
            heads = range(kv * Q_PER_KV, (kv + 1) * Q_PER_KV)
            cols = slice(kv * HEAD_DIM, (kv + 1) * HEAD_DIM)
            kh, vh = kk[:, cols], vv[:, cols]
            slope = _per_head_col([2.0 ** (-(h + 1)) for h in heads])
            sink = _per_head_col([sink_ref[h] for h in heads])
            q4, do4 = _stack_heads(qc, kv), _stack_heads(doc, kv)
            delta = jnp.sum(do4 * _stack_heads(oc, kv), axis=-1, keepdims=True)
            lse4 = _stack_cols(lc, kv)
            s = _mm_nt(q4, kh) * ATT_SCALE - slope * dist
            p = jnp.where(valid, jnp.exp(s - lse4), 0.0)
            dsc = p * (_mm_nt(do4, vh) - delta)
            dq4 = _mm(dsc, kh) * ATT_SCALE
            dk_t.append(_mm_tn(q4, dsc) * ATT_SCALE)
            dv_t.append(_mm_tn(do4, p))
            dsink4 = jnp.where(live, jnp.exp(sink - lse4) * delta, 0.0)
            for g, h in enumerate(heads):
                rows = slice(g * ATT_BLOCK, (g + 1) * ATT_BLOCK)
                dq_parts.append((h, dq4[rows, :]))
                dsink_cols.append(-jnp.sum(dsink4[rows, :], axis=0, keepdims=True))
        dsink = jnp.concatenate(dsink_cols, axis=1)
        for out_ref, carry, parts in ((dk_ref, dk_carry, dk_t), (dv_ref, dv_carry, dv_t)):
            both = jnp.concatenate(parts, axis=0)
            out_ref[...] = (carry[...] + both[:, 0:ATT_BLOCK]).T
            carry[...] = both[:, ATT_BLOCK:]

        @pl.when(live)
        def _():
            for h, part in dq_parts:
                dq_ref[:, h * HEAD_DIM:(h + 1) * HEAD_DIM] = part

        @pl.when((b == 0) & (i == 0))
        def _():
            ds_ref[...] = dsink

        @pl.when((b != 0) | (i != 0))
        def _():
            ds_ref[...] += dsink

    cur_i = lambda i: jnp.minimum(i, nb - 1)
    cur = lambda w: pl.BlockSpec((ATT_BLOCK, w), lambda b, i: (b * nb + cur_i(i), 0))
    prev = lambda w: pl.BlockSpec((ATT_BLOCK, w), lambda b, i: (b * nb + jnp.maximum(cur_i(i) - 1, 0), 0))
    behind = lambda w: pl.BlockSpec((ATT_BLOCK, w), lambda b, i: (b * nb + jnp.maximum(i - 1, 0), 0))
    return pl.pallas_call(
        body, name="attn_backward", grid=(bl, nb + 1),
        in_specs=[pl.BlockSpec(memory_space=pltpu.SMEM), cur(512), prev(128), cur(128), prev(128), cur(128),
                  cur(512), cur(512), cur(N_HEADS)],
        out_specs=(cur(512), behind(128), behind(128), pl.BlockSpec((1, N_HEADS), lambda b, i: (0, 0))),
        out_shape=(jax.ShapeDtypeStruct((t, D_ATTN), F32), jax.ShapeDtypeStruct((t, 128), F32),
                   jax.ShapeDtypeStruct((t, 128), F32), jax.ShapeDtypeStruct((1, N_HEADS), F32)),
        scratch_shapes=[pltpu.VMEM((ATT_BLOCK, KV_HEADS * HEAD_DIM), F32), pltpu.VMEM((ATT_BLOCK, KV_HEADS * HEAD_DIM), F32)],
        compiler_params=_tc_params(("arbitrary", "arbitrary")),
    )(sinks, q, k, k, v, v, o, do, lse)


def _mix_forward_backward(x2, y2, z_ssm, attn, z_attn, p2, target2, w_glu, b_glu, w_out, g_post, w_gate, b_gate,
                          w_proj, tm):
    t = x2.shape[0]

    def body(x_ref, y_ref, zs_ref, at_ref, za_ref, p_ref, tg_ref,
             wglu_ref, bglu_ref, wout_ref, gpost_ref, wgate_ref, bgate_ref, wproj_ref,
             loss_ref, dh1_ref, dy_ref, dzs_ref, dat_ref, dza_ref,
             dwglu_ref, dbglu_ref, dwout_ref, dgpost_ref, dwgate_ref, dbgate_ref, dwproj_ref,
             dwout16_ref, dwgate16_ref, dwproj16_ref, dwglu16_ref):
        i = pl.program_id(0)
        gpost = gpost_ref[...]

        @pl.when(i == 0)
        def _():
            for ref in (dwglu_ref, dbglu_ref, dwout_ref, dgpost_ref, dwgate_ref, dbgate_ref, dwproj_ref, loss_ref):
                ref[...] = jnp.zeros(ref.shape, F32)

        def chain(rows):
            y = y_ref[rows, :]
            u3 = GELU_C * (y + GELU_K * y * y * y)
            th = jnp.tanh(u3)
            gl = 0.5 * y * (1.0 + th)
            a = _mm(gl, wglu_ref[...]) + bglu_ref[...]
            sa = _sigmoid(a)
            glu = gl * sa
            zs = zs_ref[rows, :]
            sgs = _sigmoid(zs)
            ssm_out = glu * (zs * sgs)
            za = za_ref[rows, :]
            sga = _sigmoid(za)
            at = at_ref[rows, :]
            attn_out = at * (za * sga)
            cat = jnp.concatenate([ssm_out, attn_out], axis=-1).astype(BF16)
            mixed = _mm(cat, wout_ref[...])
            r2 = lax.rsqrt(jnp.mean(mixed * mixed, axis=-1, keepdims=True) + EPS)
            nhat = mixed * r2
            h1 = x_ref[rows, :] + nhat * gpost
            gate = _sigmoid(_mm(h1, wgate_ref[...]) + bgate_ref[...])
            pv = p_ref[rows, :]
            pp = _mm(pv, wproj_ref[...])
            h2 = h1 + gate * pp
            err = h2 - tg_ref[rows, :]
            loss_part = jnp.sum(jnp.sum(err * err, axis=-1, keepdims=True), axis=0, keepdims=True) * (0.5 / D_MODEL)
            dh2 = err * (1.0 / D_MODEL)
            dgp = dh2 * pp * gate * (1.0 - gate)
            dpp = dh2 * gate
            dh1 = dh2 + _mm_nt(dgp, wgate_ref[...])
            dh1_ref[rows, :] = dh1
            dnhat = dh1 * gpost
            dmixed = r2 * (dnhat - nhat * jnp.mean(dnhat * nhat, axis=-1, keepdims=True))
            dcat = _mm_nt(dmixed, wout_ref[...])
            dso, dao = dcat[:, 0:D_SSM], dcat[:, D_SSM:]
            dat_ref[rows, :] = dao * (za * sga)
            dza_ref[rows, :] = (dao * at * (sga * (1.0 + za * (1.0 - sga)))).astype(BF16)
            dzs_ref[rows, :] = (dso * glu * (sgs * (1.0 + zs * (1.0 - sgs)))).astype(BF16)
            dglu = dso * (zs * sgs)
            da = dglu * gl * sa * (1.0 - sa)
            dgl = dglu * sa + _mm_nt(da, wglu_ref[...])
            dgelu = 0.5 * (1.0 + th) + 0.5 * y * (1.0 - th * th) * (GELU_C * (1.0 + 3.0 * GELU_K * y * y))
            dy_ref[rows, :] = dgl * dgelu
            return dict(gl=gl.astype(BF16), da=da.astype(BF16), cat=cat, dmixed=dmixed.astype(BF16),
                        h1=h1.astype(BF16), dgp=dgp.astype(BF16), pv=pv.astype(BF16), dpp=dpp.astype(BF16),
                        dbglu=jnp.sum(da, axis=0, keepdims=True), dgpost=jnp.sum(dh1 * nhat, axis=0, keepdims=True),
                        dbgate=jnp.sum(dgp, axis=0, keepdims=True), loss=loss_part)

        groups = [chain(slice(k * (tm // MIX_GROUPS), (k + 1) * (tm // MIX_GROUPS))) for k in range(MIX_GROUPS)]
        rows_of = lambda name: jnp.concatenate([g[name] for g in groups], axis=0)
        total = lambda name: sum(g[name] for g in groups)
        parts = (
            (dwglu_ref, _mm_tn(rows_of("gl"), rows_of("da"))), (dbglu_ref, total("dbglu")),
            (dwout_ref, _mm_tn(rows_of("cat"), rows_of("dmixed"))), (dgpost_ref, total("dgpost")),
            (dwgate_ref, _mm_tn(rows_of("h1"), rows_of("dgp"))), (dbgate_ref, total("dbgate")),
            (dwproj_ref, _mm_tn(rows_of("pv"), rows_of("dpp"))), (loss_ref, total("loss")),
        )

        for ref, val in parts:
            ref[...] += val

        @pl.when(i == t // tm - 1)
        def _():
            for ref16, ref in ((dwout16_ref, dwout_ref), (dwgate16_ref, dwgate_ref), (dwproj16_ref, dwproj_ref),
                               (dwglu16_ref, dwglu_ref)):
                def to16(r, ref16=ref16, ref=ref):
                    ref16[r, :] = ref[r, :].astype(BF16)

                _row_chunks(ref.shape[0], to16)

    row = lambda w: pl.BlockSpec((tm, w), lambda i: (i, 0))
    acc = lambda r, c, dt=F32: (_const_spec((r, c)), jax.ShapeDtypeStruct((r, c), dt))
    accs = [acc(D_SSM, D_SSM), acc(1, D_SSM), acc(D_MODEL, D_MODEL), acc(1, D_MODEL), acc(D_MODEL, D_MODEL),
            acc(1, D_MODEL), acc(D_PLE, D_MODEL),
            acc(D_MODEL, D_MODEL, BF16), acc(D_MODEL, D_MODEL, BF16), acc(D_PLE, D_MODEL, BF16), acc(D_SSM, D_SSM, BF16)]
    return pl.pallas_call(
        body, name="mix_forward_backward", grid=(t // tm,),
        in_specs=[row(D_MODEL), row(512), row(512), row(512), row(512), row(D_PLE), row(D_MODEL),
                  _const_spec((D_SSM, D_SSM)), _const_spec((1, D_SSM)), _const_spec((D_MODEL, D_MODEL)),
                  _const_spec((1, D_MODEL)), _const_spec((D_MODEL, D_MODEL)), _const_spec((1, D_MODEL)),
                  _const_spec((D_PLE, D_MODEL))],
        out_specs=(_const_spec((1, 1)), row(D_MODEL), row(512), row(512), row(512), row(512))
        + tuple(a[0] for a in accs),
        out_shape=(jax.ShapeDtypeStruct((1, 1), F32), jax.ShapeDtypeStruct((t, D_MODEL), F32),
                   jax.ShapeDtypeStruct((t, 512), F32),
                   jax.ShapeDtypeStruct((t, 512), BF16), jax.ShapeDtypeStruct((t, 512), F32),
                   jax.ShapeDtypeStruct((t, 512), BF16)) + tuple(a[1] for a in accs),
        compiler_params=_tc_params(("arbitrary",)),
    )(x2, y2, z_ssm, attn, z_attn, p2, target2, w_glu, b_glu, w_out, g_post, w_gate, b_gate, w_proj)


def _in_backward(x2, dh1, du, dz_ssm, dq, dk, dv, dz_attn, g_pre, w_in, tm):
    t = x2.shape[0]

    def body(x_ref, dh1_ref, du_ref, dzs_ref, dq_ref, dk_ref, dv_ref, dza_ref, g_ref, w_ref,
             gx_ref, dw_ref, dg_ref, dw16_ref):
        i = pl.program_id(0)
        xv = x_ref[...]
        r = lax.rsqrt(jnp.mean(xv * xv, axis=-1, keepdims=True) + EPS)
        xhat = xv * r
        g = g_ref[...]
        hn = (xhat * g).astype(BF16)
        dproj = jnp.concatenate([du_ref[...].astype(BF16), dzs_ref[...].astype(BF16), dq_ref[...].astype(BF16),
                                 dk_ref[...].astype(BF16), dv_ref[...].astype(BF16), dza_ref[...].astype(BF16)],
                                axis=-1)
        dhn = _mm(dproj, w_ref[...])
        dxhat = dhn * g
        gx_ref[...] = dh1_ref[...] + r * (dxhat - xhat * jnp.mean(dxhat * xhat, axis=-1, keepdims=True))
        @pl.when(i == 0)
        def _():
            dw_ref[...] = jnp.zeros((D_IN, D_MODEL), F32)
            dg_ref[...] = jnp.zeros((1, D_MODEL), F32)

        dw_ref[...] += _mm_tn(dproj, hn)
        dg_ref[...] += jnp.sum(dhn * xhat, axis=0, keepdims=True)

        @pl.when(i == t // tm - 1)
        def _():
            def to16(r):
                dw16_ref[r, :] = dw_ref[r, :].astype(BF16)

            _row_chunks(D_IN, to16)

    row = lambda w: pl.BlockSpec((tm, w), lambda i: (i, 0))
    return pl.pallas_call(
        body, name="in_backward", grid=(t // tm,),
        in_specs=[row(D_MODEL), row(D_MODEL), row(512), row(512), row(512), row(128), row(128), row(512),
                  _const_spec((1, D_MODEL)), _const_spec((D_IN, D_MODEL))],
        out_specs=(row(D_MODEL), _const_spec((D_IN, D_MODEL)), _const_spec((1, D_MODEL)),
                   _const_spec((D_IN, D_MODEL))),
        out_shape=(jax.ShapeDtypeStruct((t, D_MODEL), F32), jax.ShapeDtypeStruct((D_IN, D_MODEL), F32),
                   jax.ShapeDtypeStruct((1, D_MODEL), F32), jax.ShapeDtypeStruct((D_IN, D_MODEL), BF16)),
        compiler_params=_tc_params(("arbitrary",)),
    )(x2, dh1, du, dz_ssm, dq, dk, dv, dz_attn, g_pre, w_in)


def _local_step(x, p, target, pre_norm_g, w_in, prep, ssm_lam_re, ssm_lam_im, ssm_log_step, ssm_b_re, ssm_b_im, ssm_d,
                ssm_b_glu, attn_sinks, post_norm_g, pl_b_gate, late):
    bl, seq, _ = x.shape
    seg = seq // N_SEG
    nb = seq // ATT_BLOCK
    t = bl * seq
    x2 = x.reshape(t, D_MODEL)
    p2 = p.reshape(t, D_PLE)
    tg2 = target.reshape(t, D_MODEL)

    lam_re, lam_im = ssm_lam_re, ssm_lam_im
    log_step = ssm_log_step.reshape(SSM_G, 1)
    a_re_row, a_im_row, pw_re, pw_im, bcat, bcat_t, ccat, ccat_t = prep
    d_row = ssm_d.reshape(1, D_SSM)

    segments = lambda a: a.reshape(bl, N_SEG, seg, D_SSM)
    u, z_ssm, q, k, v, z_attn = _in_proj(x2, pre_norm_g.reshape(1, D_MODEL), w_in, min(TOKEN_TILE_WIDE, t))
    (y, states, carries), gathered = _ssm_forward(
        segments(u), bcat, ccat, a_re_row, a_im_row, pw_re, pw_im, d_row, late, seg)
    w_out, w_gate, w_proj, w_glu = (_gathered_to_full(n, g) for n, g in zip(LATE_NAMES, gathered))
    sinks = attn_sinks.reshape(N_HEADS)
    attn, lse = _attn_forward(q, k, v, sinks, bl, nb)
    (loss, dh1, dy, dz_ssm, dattn, dz_attn, d_w_glu, d_b_glu, d_w_out, d_g_post, d_w_gate, d_b_gate,
     d_w_proj, *late16) = _mix_forward_backward(
        x2, y.reshape(t, D_SSM), z_ssm, attn, z_attn, p2, tg2, w_glu,
        ssm_b_glu.reshape(1, D_SSM), w_out, post_norm_g.reshape(1, D_MODEL), w_gate, pl_b_gate.reshape(1, D_MODEL),
        w_proj, min(TOKEN_TILE, t))
    owned = lambda ds: [_full_to_owned(n, d) for n, d in zip(LATE_NAMES, ds)]
    dq, dk, dv, d_sinks = _attn_backward(q, k, v, attn, dattn, lse, sinks, bl, nb)
    (du, d_bcat, d_ccat_t, da_re, da_im, d_d), late_grads = _ssm_backward(
        segments(u), segments(dy), states, carries, bcat_t, ccat_t, a_re_row, a_im_row, pw_re, pw_im,
        d_row, owned(late16), owned((d_w_out, d_w_gate, d_w_proj, d_w_glu)), seg)
    grad_x, d_w_in, d_g_pre, d_w_in16 = _in_backward(
        x2, dh1, du.reshape(t, D_SSM), dz_ssm, dq, dk, dv, dz_attn, pre_norm_g.reshape(1, D_MODEL), w_in,
        min(TOKEN_TILE_WIDE, t))
    d_lam_re, d_lam_im, d_ls, d_b_re, d_b_im, d_c_re, d_c_im = _ssm_param_grads(
        lam_re, lam_im, log_step, ssm_b_re, ssm_b_im, da_re.reshape(SSM_G, SSM_N), da_im.reshape(SSM_G, SSM_N),
        d_bcat, d_ccat_t)
    grads = {
        "pre_norm_g": d_g_pre, "w_in": d_w_in, "w_in16": d_w_in16, "ssm_lam_re": d_lam_re, "ssm_lam_im": d_lam_im,
        "ssm_log_step": d_ls, "ssm_b_re": d_b_re, "ssm_b_im": d_b_im, "ssm_c_re": d_c_re, "ssm_c_im": d_c_im,
        "ssm_d": d_d, "ssm_b_glu": d_b_glu, "attn_sinks": d_sinks, "post_norm_g": d_g_post, "pl_b_gate": d_b_gate,
    }
    return loss, grad_x.reshape(bl, seq, D_MODEL), grads, late_grads


LATE_NAMES = ("w_out", "pl_w_gate", "pl_w_proj", "ssm_w_glu")
BIG_NAMES = ("w_in",) + LATE_NAMES
COL_SHARDED = {"w_in": D_IN // N_DEV, "pl_w_proj": D_MODEL // N_DEV}
WEIGHT_NAMES = ("pre_norm_g", "w_in", "ssm_lam_re", "ssm_lam_im", "ssm_log_step", "ssm_b_re", "ssm_b_im", "ssm_c_re",
                "ssm_c_im", "ssm_d", "ssm_w_glu", "ssm_b_glu", "attn_sinks", "w_out", "post_norm_g", "pl_w_proj",
                "pl_w_gate", "pl_b_gate")


TRANSPOSED = {"w_in": (0, 1), "ssm_b_re": (1, 2), "ssm_b_im": (1, 2)}


def _kernel_form(name, a):
    a = a[0]
    if name in TRANSPOSED:
        a = jnp.swapaxes(a, *TRANSPOSED[name])
    if name in ("ssm_b_re", "ssm_b_im", "ssm_c_re", "ssm_c_im"):
        a = a.reshape(SSM_G * SSM_P, SSM_N)
    return a


def _given_form(name, a, shape):
    if name in TRANSPOSED:
        i, j = TRANSPOSED[name]
        swapped = list(shape[1:])
        swapped[i], swapped[j] = swapped[j], swapped[i]
        return jnp.swapaxes(a.reshape(swapped), i, j).reshape(shape)
    return a.reshape(shape)


def _gathered_to_full(name, g):
    _, rows, cols = g.shape
    if name in COL_SHARDED:
        return jnp.swapaxes(g, 0, 1).reshape(rows, N_DEV * cols)
    return g.reshape(N_DEV * rows, cols)


def _full_to_owned(name, full):
    if name in COL_SHARDED:
        return jnp.swapaxes(full.reshape(full.shape[0], N_DEV, COL_SHARDED[name]), 0, 1)
    return full.reshape(N_DEV, full.shape[0] // N_DEV, full.shape[1])


def kernel(x, p, pre_norm_g, w_in, ssm_lam_re, ssm_lam_im, ssm_log_step, ssm_b_re, ssm_b_im, ssm_c_re, ssm_c_im, ssm_d, ssm_w_glu, ssm_b_glu, attn_sinks, w_out, post_norm_g, pl_w_proj, pl_w_gate, pl_b_gate, loss_target, m_pre_norm_g, m_w_in, m_ssm_lam_re, m_ssm_lam_im, m_ssm_log_step, m_ssm_b_re, m_ssm_b_im, m_ssm_c_re, m_ssm_c_im, m_ssm_d, m_ssm_w_glu, m_ssm_b_glu, m_attn_sinks, m_w_out, m_post_norm_g, m_pl_w_proj, m_pl_w_gate, m_pl_b_gate, v_pre_norm_g, v_w_in, v_ssm_lam_re, v_ssm_lam_im, v_ssm_log_step, v_ssm_b_re, v_ssm_b_im, v_ssm_c_re, v_ssm_c_im, v_ssm_d, v_ssm_w_glu, v_ssm_b_glu, v_attn_sinks, v_w_out, v_post_norm_g, v_pl_w_proj, v_pl_w_gate, v_pl_b_gate):
    w = dict(pre_norm_g=pre_norm_g, w_in=w_in, ssm_lam_re=ssm_lam_re, ssm_lam_im=ssm_lam_im, ssm_log_step=ssm_log_step,
             ssm_b_re=ssm_b_re, ssm_b_im=ssm_b_im, ssm_c_re=ssm_c_re, ssm_c_im=ssm_c_im, ssm_d=ssm_d, ssm_w_glu=ssm_w_glu,
             ssm_b_glu=ssm_b_glu, attn_sinks=attn_sinks, w_out=w_out, post_norm_g=post_norm_g, pl_w_proj=pl_w_proj,
             pl_w_gate=pl_w_gate, pl_b_gate=pl_b_gate)
    m = dict(pre_norm_g=m_pre_norm_g, w_in=m_w_in, ssm_lam_re=m_ssm_lam_re, ssm_lam_im=m_ssm_lam_im,
             ssm_log_step=m_ssm_log_step, ssm_b_re=m_ssm_b_re, ssm_b_im=m_ssm_b_im, ssm_c_re=m_ssm_c_re,
             ssm_c_im=m_ssm_c_im, ssm_d=m_ssm_d, ssm_w_glu=m_ssm_w_glu, ssm_b_glu=m_ssm_b_glu, attn_sinks=m_attn_sinks,
             w_out=m_w_out, post_norm_g=m_post_norm_g, pl_w_proj=m_pl_w_proj, pl_w_gate=m_pl_w_gate,
             pl_b_gate=m_pl_b_gate)
    v = dict(pre_norm_g=v_pre_norm_g, w_in=v_w_in, ssm_lam_re=v_ssm_lam_re, ssm_lam_im=v_ssm_lam_im,
             ssm_log_step=v_ssm_log_step, ssm_b_re=v_ssm_b_re, ssm_b_im=v_ssm_b_im, ssm_c_re=v_ssm_c_re,
             ssm_c_im=v_ssm_c_im, ssm_d=v_ssm_d, ssm_w_glu=v_ssm_w_glu, ssm_b_glu=v_ssm_b_glu, attn_sinks=v_attn_sinks,
             w_out=v_w_out, post_norm_g=v_post_norm_g, pl_w_proj=v_pl_w_proj, pl_w_gate=v_pl_w_gate,
             pl_b_gate=v_pl_b_gate)
    kf = lambda d: {n: _kernel_form(n, a) for n, a in d.items()}
    wk, mk, vk = kf(w), kf(m), kf(v)

    (gathered,), prep = _allgather_weights([wk["w_in"]], *_ssm_prep(
        wk["ssm_lam_re"], wk["ssm_lam_im"], wk["ssm_log_step"].reshape(SSM_G, 1), wk["ssm_b_re"], wk["ssm_b_im"],
        wk["ssm_c_re"], wk["ssm_c_im"], x.shape[1] // N_SEG))
    loss, grad_x, grads, g_late = _local_step(
        x, p[0], loss_target, wk["pre_norm_g"], gathered.reshape(D_IN, D_MODEL), prep, wk["ssm_lam_re"],
        wk["ssm_lam_im"], wk["ssm_log_step"], wk["ssm_b_re"], wk["ssm_b_im"], wk["ssm_d"],
        wk["ssm_b_glu"], wk["attn_sinks"], wk["post_norm_g"], wk["pl_b_gate"], [wk[n] for n in LATE_NAMES])

    owned = lambda g: g.reshape(N_DEV, D_IN // N_DEV, D_MODEL)
    tiny_form = lambda d: [d[n].reshape(rows, cols) for n, rows, cols in TINY]
    med_form = lambda d: [d[n].reshape(N_DEV, rows // N_DEV, cols) for n, rows, cols in MEDIUM]
    g_big, loss, g_tiny, g_med = _reduce_final(
        [owned(grads["w_in16"])], [owned(grads["w_in"])], loss, tiny_form(grads), med_form(grads))
    names = BIG_NAMES + tuple(n for n, _, _ in TINY + MEDIUM)
    form = lambda d: [d[n] for n in BIG_NAMES] + tiny_form(d) + med_form(d)
    updated = _adamw_update(g_big + g_late + g_tiny + g_med, form(wk), form(mk), form(vk), len(BIG_NAMES))
    vals = dict(zip(names, updated))
    results = [[_given_form(n, vals[n][kind], w[n].shape) for n in WEIGHT_NAMES] for kind in range(4)]
    return (loss.reshape(()), grad_x, *results[0], *results[1], *results[2], *results[3])
```

```python
import functools
import math

import jax
import jax.numpy as jnp
from jax import lax
from jax.experimental import pallas as pl
from jax.experimental.pallas import tpu as pltpu

F32 = jnp.float32
BF16 = jnp.bfloat16

D_MODEL = 1024
D_SSM = 512
D_ATTN = 512
SSM_P = 16
SSM_G = 32
SSM_N = 64
N_HEADS = 8
KV_HEADS = 2
Q_PER_KV = 4
HEAD_DIM = 64
ATT_BLOCK = 128
D_PLE = 256
D_IN = 2304
EPS = 1e-6
N_DEV = 8
N_SEG = 8
G_TILE = 8
N_GT = SSM_G // G_TILE
CH_T = G_TILE * SSM_P
ST_T = G_TILE * SSM_N
N_STATE = SSM_G * SSM_N
SCAN_UNROLL = 4
MIX_GROUPS = 1
TOKEN_TILE = 256
TOKEN_TILE_WIDE = 512
LANES = 128
VMEM_LIMIT = 60 * 1024 * 1024

ADAM_LR = 0.001
ADAM_B1 = 0.9
ADAM_B2 = 0.999
ADAM_EPS = 1e-08
ADAM_WD = 0.01
ADAM_STEP = 10

GELU_C = math.sqrt(2.0 / math.pi)
GELU_K = 0.044715
ATT_SCALE = 1.0 / math.sqrt(HEAD_DIM)
NEG_INF = float("-inf")


def _mm(a, b):
    return jnp.dot(a.astype(BF16), b.astype(BF16), preferred_element_type=F32)


def _mm_nt(a, b):
    return lax.dot_general(a.astype(BF16), b.astype(BF16), (((1,), (1,)), ((), ())), preferred_element_type=F32)


def _mm_tn(a, b):
    return lax.dot_general(a.astype(BF16), b.astype(BF16), (((0,), (0,)), ((), ())), preferred_element_type=F32)


def _sigmoid(x):
    return 1.0 / (1.0 + jnp.exp(-x))


def _tc_params(sem):
    return pltpu.CompilerParams(dimension_semantics=sem, vmem_limit_bytes=VMEM_LIMIT)


def _const_spec(shape):
    nd = len(shape)
    return pl.BlockSpec(shape, lambda *_: (0,) * nd)


def _mesh_pos():
    return lax.axis_index("x"), lax.axis_index("y"), lax.axis_index("c")


ROW_CHUNKS = (64, 32, 16)


def _row_chunk(nrows):
    return next((c for c in ROW_CHUNKS if nrows % c == 0), None)


def _row_chunks(nrows, fn, chunk=None, init=None):
    chunk = chunk or _row_chunk(nrows)

    def step(i, carry):
        rows = pl.ds(pl.multiple_of(i * chunk, chunk), chunk)
        if init is None:
            fn(rows)
            return carry
        return fn(rows, carry)

    return lax.fori_loop(0, nrows // chunk, step, 0 if init is None else init)


def _slot(px, py, pc):
    return 4 * px + 2 * py + pc


def _allgather_weights(shards, work=None, work_inputs=(), work_out_shapes=()):
    n, n_wi, n_wo = len(shards), len(work_inputs), len(work_out_shapes)

    def body(*refs):
        srcs, w_in_refs = refs[:n], refs[n:n + n_wi]
        outs, w_out_refs = refs[n + n_wi:2 * n + n_wi], refs[2 * n + n_wi:2 * n + n_wi + n_wo]
        send_sems, recv_sems = refs[2 * n + n_wi + n_wo:]
        x, y, c = _mesh_pos()
        me, sibling = (x, y, c), (x, y, 1 - c)
        chips = [(1 - x, y), (x, 1 - y), (1 - x, 1 - y)]

        def copy(a, k, block, to):
            blk = outs[a].at[_slot(*block)]
            return pltpu.make_async_remote_copy(
                src_ref=blk, dst_ref=blk, send_sem=send_sems.at[7 * a + k], recv_sem=recv_sems.at[7 * a + k],
                device_id=to, device_id_type=pl.DeviceIdType.MESH)

        sends = []
        for a in range(n):
            mine = outs[a].at[_slot(*me)]

            def cast(r, mine=mine, src=srcs[a]):
                mine[r, :] = src[r, :].astype(BF16)

            _row_chunks(srcs[a].shape[0], cast)
            first = [copy(a, 0, me, sibling)] + [copy(a, 1 + j, me, (*chip, c)) for j, chip in enumerate(chips)]
            for cp in first:
                cp.start()
            sends += first
        if work is not None:
            work(w_in_refs, w_out_refs)
        for a in range(n):
            for j, chip in enumerate(chips):
                copy(a, 1 + j, (*chip, c), me).wait_recv()
                fwd = copy(a, 4 + j, (*chip, c), sibling)
                fwd.start()
                sends.append(fwd)
        for a in range(n):
            copy(a, 0, sibling, me).wait_recv()
            for j, chip in enumerate(chips):
                copy(a, 4 + j, (*chip, 1 - c), me).wait_recv()
        for cp in sends:
            cp.wait_send()

    vm = pl.BlockSpec(memory_space=pltpu.VMEM)
    res = pl.pallas_call(
        body, name="allgather_weights",
        out_shape=tuple(jax.ShapeDtypeStruct((N_DEV,) + s.shape, BF16) for s in shards) + tuple(work_out_shapes),
        in_specs=[vm] * (n + n_wi), out_specs=(vm,) * (n + n_wo),
        scratch_shapes=[pltpu.SemaphoreType.DMA((7 * n,)), pltpu.SemaphoreType.DMA((7 * n,))],
        compiler_params=pltpu.CompilerParams(vmem_limit_bytes=VMEM_LIMIT),
    )(*shards, *work_inputs)
    return list(res[:n]), list(res[n:])


def _adamw(w, g, m, v):
    m = ADAM_B1 * m + (1.0 - ADAM_B1) * g
    v = ADAM_B2 * v + (1.0 - ADAM_B2) * (g * g)
    m_hat = m / (1.0 - ADAM_B1 ** ADAM_STEP)
    v_hat = v / (1.0 - ADAM_B2 ** ADAM_STEP)
    delta = -ADAM_LR * (m_hat / (jnp.sqrt(v_hat) + ADAM_EPS) + ADAM_WD * w)
    return delta, m, v


def _remote(src, dst, send_sems, recv_sems, k, to):
    return pltpu.make_async_remote_copy(src_ref=src, dst_ref=dst, send_sem=send_sems.at[k], recv_sem=recv_sems.at[k],
                                        device_id=to, device_id_type=pl.DeviceIdType.MESH)


def _big_reduce_phases(g16_r, go_r, outs, send2, recv1, recv2, s_send, s_recv):
    n = len(g16_r)
    x, y, c = _mesh_pos()
    sibling = (x, y, 1 - c)
    chips = [(1 - x, y), (x, 1 - y), (1 - x, 1 - y)]
    all_chips = [(x, y)] + chips
    lvl1 = []
    for a in range(n):
        cps = [_remote(g16_r[a].at[_slot(*chip, 1 - c)], recv1[a].at[j], s_send, s_recv, 7 * a + j, sibling)
               for j, chip in enumerate(all_chips)]
        for cp in cps:
            cp.start()
        lvl1.append(cps)
    yield
    lvl2 = []
    for a in range(n):
        for cp in lvl1[a]:
            cp.wait_recv()
        og = outs[a]

        def partials(r, a=a, og=og):
            og[r, :] = go_r[a][r, :] + recv1[a][0, r, :].astype(F32)
            for j, chip in enumerate(chips):
                mine16 = g16_r[a][_slot(*chip, c), r, :].astype(F32)
                send2[a][j, r, :] = (mine16 + recv1[a][1 + j, r, :].astype(F32)).astype(BF16)

        _row_chunks(go_r[a].shape[0], partials)
        cps = [_remote(send2[a].at[j], recv2[a].at[j], s_send, s_recv, 7 * a + 4 + j, (*chip, c))
               for j, chip in enumerate(chips)]
        for cp in cps:
            cp.start()
        lvl2.append(cps)
    yield
    for a in range(n):
        for cp in lvl2[a]:
            cp.wait_recv()
        og = outs[a]

        def total(r, a=a, og=og):
            g = og[r, :]
            for j in range(3):
                g = g + recv2[a][j, r, :].astype(F32)
            og[r, :] = g

        _row_chunks(go_r[a].shape[0], total)
    yield
    for cps in lvl1 + lvl2:
        for cp in cps:
            cp.wait_send()


def _adamw_update(g, w, m, v, n_streamed):
    n = len(g)
    ns = n_streamed

    def body(*refs):
        g_r, w_r, m_r, v_r = (refs[i * n:(i + 1) * n] for i in range(4))
        outs = refs[4 * n:8 * n]
        in_buf, out_buf = refs[8 * n:8 * n + 4 * ns], refs[8 * n + 4 * ns:8 * n + 8 * ns]
        in_sems, out_sems = refs[8 * n + 8 * ns:]
        loads = [[pltpu.make_async_copy(src[a], in_buf[4 * a + k], in_sems.at[4 * a + k])
                  for k, src in enumerate((g_r, w_r, m_r, v_r))] for a in range(ns)]
        stores = [[pltpu.make_async_copy(out_buf[4 * a + k], outs[4 * a + k], out_sems.at[4 * a + k]) for k in range(4)]
                  for a in range(ns)]
        for cps in loads:
            for cp in cps:
                cp.start()
        for a in range(n):
            if a < ns:
                for cp in loads[a]:
                    cp.wait()
                gs, ws, ms, vs = in_buf[4 * a:4 * a + 4]
                og, od, om, ov = out_buf[4 * a:4 * a + 4]
            else:
                gs, ws, ms, vs = g_r[a], w_r[a], m_r[a], v_r[a]
                og, od, om, ov = outs[4 * a:4 * a + 4]

            def update(idx, gs=gs, ws=ws, ms=ms, vs=vs, og=og, od=od, om=om, ov=ov):
                gv = gs[idx]
                d, nm, nv = _adamw(ws[idx], gv, ms[idx], vs[idx])
                og[idx] = gv
                od[idx] = d
                om[idx] = nm
                ov[idx] = nv

            shape = gs.shape
            if len(shape) == 3:
                for b in range(shape[0]):
                    update(b)
            elif _row_chunk(shape[0]) is not None:
                _row_chunks(shape[0], update)
            else:
                update(Ellipsis)
            if a < ns:
                for cp in stores[a]:
                    cp.start()
        for cps in stores:
            for cp in cps:
                cp.wait()

    vm, hbm = pl.BlockSpec(memory_space=pltpu.VMEM), pl.BlockSpec(memory_space=pl.ANY)
    place = lambda: [hbm] * ns + [vm] * (n - ns)
    buf = [pltpu.VMEM(t.shape, F32) for t in g[:ns] for _ in range(4)]
    res = pl.pallas_call(
        body, name="adamw_update",
        out_shape=tuple(jax.ShapeDtypeStruct(t.shape, F32) for t in g for _ in range(4)),
        in_specs=place() * 4, out_specs=tuple(s for a in range(n) for s in [hbm if a < ns else vm] * 4),
        scratch_shapes=buf + buf + [pltpu.SemaphoreType.DMA((4 * ns,)), pltpu.SemaphoreType.DMA((4 * ns,))],
        compiler_params=pltpu.CompilerParams(vmem_limit_bytes=VMEM_LIMIT),
    )(*g, *w, *m, *v)
    return [res[4 * a:4 * a + 4] for a in range(n)]


TINY = (("pre_norm_g", 1, 1024), ("post_norm_g", 1, 1024), ("pl_b_gate", 1, 1024), ("ssm_d", 1, 512),
        ("ssm_b_glu", 1, 512), ("ssm_log_step", 1, 32), ("attn_sinks", 1, 8), ("ssm_lam_re", 32, 64),
        ("ssm_lam_im", 32, 64))
MEDIUM = (("ssm_b_re", SSM_G * SSM_P, SSM_N), ("ssm_b_im", SSM_G * SSM_P, SSM_N), ("ssm_c_re", SSM_G * SSM_P, SSM_N),
          ("ssm_c_im", SSM_G * SSM_P, SSM_N))


def _stage_rows():
    offs, r = {}, 0
    for name, rows, cols in TINY + (("loss", 1, 1),):
        if rows > 1:
            r = -(-r // 8) * 8
        offs[name] = r
        r += rows if rows > 1 else max(cols // LANES, 1)
    return offs, -(-r // 8) * 8


def _reduce_final(g16, g32, loss, g_tiny, g_med):
    nb_, nt, nm_ = len(g16), len(TINY), len(MEDIUM)
    offs, stage_rows = _stage_rows()

    def body(*refs):
        g16_r, go_r = refs[:nb_], refs[nb_:2 * nb_]
        base = 2 * nb_
        loss_r, gt, gm = refs[base], refs[base + 1:base + 1 + nt], refs[base + 1 + nt:base + 1 + nt + nm_]
        base += 1 + nt + nm_
        out_b = refs[base:base + nb_]
        base += nb_
        loss_o, out_t, out_m = refs[base], refs[base + 1:base + 1 + nt], refs[base + 1 + nt:base + 1 + nt + nm_]
        base += 1 + nt + nm_
        send2_b, recv1_b, recv2_b = (refs[base + i * nb_:base + (i + 1) * nb_] for i in range(3))
        base += 3 * nb_
        stage = refs[base]
        recv1, part, recv2 = (refs[base + 1 + i * nm_:base + 1 + (i + 1) * nm_] for i in range(3))
        bs_send, bs_recv, s_send, s_recv, own_sems = refs[base + 1 + 3 * nm_:base + 6 + 3 * nm_]
        own32 = refs[base + 6 + 3 * nm_:]
        me = _slot(*_mesh_pos())
        fetch = [pltpu.make_async_copy(go_r[a].at[me], own32[a], own_sems.at[a]) for a in range(nb_)]
        for cp in fetch:
            cp.start()
        big = _big_reduce_phases(g16_r, own32, out_b, send2_b, recv1_b, recv2_b, bs_send, bs_recv)
        small = small_phases(loss_r, gt, gm, loss_o, out_t, out_m, stage, recv1, part, recv2, s_send, s_recv)
        next(big)
        next(small)
        for cp in fetch:
            cp.wait()
        next(big)
        for _ in small:
            pass
        for _ in big:
            pass

    def small_phases(loss_r, gt, gm, loss_o, out_t, out_m, stage, recv1, part, recv2, s_send, s_recv):
        x, y, c = _mesh_pos()
        me = _slot(x, y, c)
        sibling = (x, y, 1 - c)
        chips = [(1 - x, y), (x, 1 - y), (1 - x, 1 - y)]
        all_chips = [(x, y)] + chips
        peers = [sibling] + [(*chip, c) for chip in chips] + [(*chip, 1 - c) for chip in chips]
        sem = iter(range(7 + 14 * nm_))
        lvl1 = []
        for a in range(nm_):
            cps = [_remote(gm[a].at[_slot(*chip, 1 - c)], recv1[a].at[j], s_send, s_recv, next(sem), sibling)
                   for j, chip in enumerate(all_chips)]
            for cp in cps:
                cp.start()
            lvl1.append(cps)
        mine = stage.at[me]
        mine[...] = jnp.zeros((stage_rows, LANES), F32)
        for (name, rows, cols), ref in zip(TINY + (("loss", 1, 1),), gt + (loss_r,)):
            r0 = offs[name]
            if rows > 1:
                mine[r0:r0 + rows, 0:cols] = ref[...]
            elif cols >= LANES:
                for i in range(cols // LANES):
                    mine[r0 + i:r0 + i + 1, :] = ref[:, i * LANES:(i + 1) * LANES]
            else:
                mine[r0:r0 + 1, 0:cols] = ref[...]
        tiny_cps = [_remote(mine, mine, s_send, s_recv, next(sem), peer) for peer in peers]
        for cp in tiny_cps:
            cp.start()
        yield
        lvl2 = []
        for a in range(nm_):
            for cp in lvl1[a]:
                cp.wait_recv()
            for j, chip in enumerate(all_chips):
                part[a][j] = gm[a][_slot(*chip, c)] + recv1[a][j]
            cps = [_remote(part[a].at[1 + j], recv2[a].at[j], s_send, s_recv, next(sem), (*chip, c))
                   for j, chip in enumerate(chips)]
            for cp in cps:
                cp.start()
            lvl2.append(cps)
        yield
        lvl3 = []
        for a in range(nm_):
            for cp in lvl2[a]:
                cp.wait_recv()
            blk = out_m[a].at[me]
            blk[...] = ((part[a][0] + recv2[a][0]) + recv2[a][1]) + recv2[a][2]
            cps = [_remote(blk, blk, s_send, s_recv, next(sem), peer) for peer in peers]
            for cp in cps:
                cp.start()
            lvl3.append(cps)
        yield
        for cp in tiny_cps:
            cp.wait_recv()
        tot = stage[0]
        for d in range(1, N_DEV):
            tot = tot + stage[d]
        loss_o[...] = tot[offs["loss"]:offs["loss"] + 1, 0:1]
        for k, (name, rows, cols) in enumerate(TINY):
            r0 = offs[name]
            if rows > 1:
                out_t[k][...] = tot[r0:r0 + rows, 0:cols]
            elif cols >= LANES:
                for i in range(cols // LANES):
                    out_t[k][:, i * LANES:(i + 1) * LANES] = tot[r0 + i:r0 + i + 1, :]
            else:
                out_t[k][...] = tot[r0:r0 + 1, 0:cols]
        for cps in lvl3:
            for cp in cps:
                cp.wait_recv()
        for cps in lvl1 + lvl2 + lvl3 + [tiny_cps]:
            for cp in cps:
                cp.wait_send()

    vmem = pl.BlockSpec(memory_space=pltpu.VMEM)
    t_shapes = [jax.ShapeDtypeStruct((rows, cols), F32) for _, rows, cols in TINY]
    m_shapes = [jax.ShapeDtypeStruct((N_DEV, rows // N_DEV, cols), F32) for _, rows, cols in MEDIUM]
    blk = [(rows // N_DEV, cols) for _, rows, cols in MEDIUM]
    shard = [g.shape[1:] for g in g16]
    scratch = ([pltpu.VMEM((3,) + s, BF16) for s in shard] + [pltpu.VMEM((4,) + s, BF16) for s in shard]
               + [pltpu.VMEM((3,) + s, BF16) for s in shard]
               + [pltpu.VMEM((N_DEV, stage_rows, LANES), F32)]
               + [pltpu.VMEM((4,) + b, F32) for b in blk] + [pltpu.VMEM((4,) + b, F32) for b in blk]
               + [pltpu.VMEM((3,) + b, F32) for b in blk]
               + [pltpu.SemaphoreType.DMA((7 * nb_,)), pltpu.SemaphoreType.DMA((7 * nb_,)),
                  pltpu.SemaphoreType.DMA((7 + 14 * nm_,)), pltpu.SemaphoreType.DMA((7 + 14 * nm_,)),
                  pltpu.SemaphoreType.DMA((nb_,))]
               + [pltpu.VMEM(s, F32) for s in shard])
    n_out = nb_ + 1 + nt + nm_
    res = pl.pallas_call(
        body, name="reduce_final",
        out_shape=tuple(jax.ShapeDtypeStruct(s, F32) for s in shard) + (jax.ShapeDtypeStruct((1, 1), F32),)
        + tuple(t_shapes) + tuple(m_shapes),
        in_specs=[vmem] * nb_ + [pl.BlockSpec(memory_space=pl.ANY)] * nb_ + [vmem] * (1 + nt + nm_),
        out_specs=(vmem,) * n_out, scratch_shapes=scratch,
        compiler_params=pltpu.CompilerParams(vmem_limit_bytes=VMEM_LIMIT),
    )(*g16, *g32, loss, *g_tiny, *g_med)
    return list(res[:nb_]), res[nb_], list(res[nb_ + 1:nb_ + 1 + nt]), list(res[nb_ + 1 + nt:])


def _gather_phases(shard_r, gath, cast, send_sems, recv_sems, local_sems):
    n = len(shard_r)
    x, y, c = _mesh_pos()
    me, sibling = (x, y, c), (x, y, 1 - c)
    chips = [(1 - x, y), (x, 1 - y), (1 - x, 1 - y)]

    def own(a, k, to):
        return _remote(cast[a], gath[a].at[_slot(*me)], send_sems, recv_sems, 7 * a + k, to)

    def passed(a, k, block, to):
        blk = gath[a].at[_slot(*block)]
        return _remote(blk, blk, send_sems, recv_sems, 7 * a + k, to)

    def keep(a):
        return pltpu.make_async_copy(cast[a], gath[a].at[_slot(*me)], local_sems.at[a])

    def start():
        for a in range(n):
            def to16(r, a=a):
                cast[a][r, :] = shard_r[a][r, :].astype(BF16)

            _row_chunks(shard_r[a].shape[0], to16)
            keep(a).start()
            own(a, 0, sibling).start()
            for j, chip in enumerate(chips):
                own(a, 1 + j, (*chip, c)).start()

    def relay():
        for a in range(n):
            for j, chip in enumerate(chips):
                passed(a, 1 + j, (*chip, c), me).wait_recv()
                passed(a, 4 + j, (*chip, c), sibling).start()

    def finish():
        for a in range(n):
            passed(a, 0, sibling, me).wait_recv()
            for j, chip in enumerate(chips):
                passed(a, 4 + j, (*chip, 1 - c), me).wait_recv()
            own(a, 0, sibling).wait_send()
            for j, chip in enumerate(chips):
                own(a, 1 + j, (*chip, c)).wait_send()
                passed(a, 4 + j, (*chip, c), sibling).wait_send()
            keep(a).wait()

    return start, relay, finish


def _gather_operands(shards):
    n = len(shards)
    return ((pl.BlockSpec(memory_space=pl.ANY),) * n,
            tuple(jax.ShapeDtypeStruct((N_DEV,) + s.shape, BF16) for s in shards),
            [pltpu.VMEM(s.shape, BF16) for s in shards]
            + [pltpu.SemaphoreType.DMA((7 * n,)), pltpu.SemaphoreType.DMA((7 * n,)), pltpu.SemaphoreType.DMA((n,))])


def _hosted_reduce_phases(g16_r, g32_r, red, own16, recv1, send2, recv2, own32, s_send, s_recv, s_local):
    n = len(g16_r)
    x, y, c = _mesh_pos()
    sibling = (x, y, 1 - c)
    chips = [(1 - x, y), (x, 1 - y), (1 - x, 1 - y)]
    all_chips = [(x, y)] + chips

    def lvl1(a, j):
        return _remote(g16_r[a].at[_slot(*all_chips[j], 1 - c)], recv1[a].at[j], s_send, s_recv, 7 * a + j, sibling)

    def lvl2(a, j):
        return _remote(send2[a].at[j], recv2[a].at[j], s_send, s_recv, 7 * a + 4 + j, (*chips[j], c))

    def mine(a, j):
        if j == 3:
            return pltpu.make_async_copy(g32_r[a].at[_slot(x, y, c)], own32[a], s_local.at[4 * a + j])
        return pltpu.make_async_copy(g16_r[a].at[_slot(*chips[j], c)], own16[a].at[j], s_local.at[4 * a + j])

    def start():
        for a in range(n):
            for j in range(4):
                mine(a, j).start()
            for j in range(4):
                lvl1(a, j).start()

    def middle():
        for a in range(n):
            for j in range(4):
                mine(a, j).wait()
            for j in range(4):
                lvl1(a, j).wait_recv()

            def partials(r, a=a):
                red[a][r, :] = own32[a][r, :] + recv1[a][0, r, :].astype(F32)
                for j in range(3):
                    send2[a][j, r, :] = (own16[a][j, r, :].astype(F32) + recv1[a][1 + j, r, :].astype(F32)).astype(BF16)

            _row_chunks(own32[a].shape[0], partials)
            for j in range(3):
                lvl2(a, j).start()

    def total():
        for a in range(n):
            for j in range(3):
                lvl2(a, j).wait_recv()

            def add(r, a=a):
                g = red[a][r, :]
                for j in range(3):
                    g = g + recv2[a][j, r, :].astype(F32)
                red[a][r, :] = g

            _row_chunks(own32[a].shape[0], add)

    def finish():
        for a in range(n):
            for j in range(4):
                lvl1(a, j).wait_send()
            for j in range(3):
                lvl2(a, j).wait_send()

    return start, middle, total, finish


def _hosted_reduce_operands(g16, const_spec):
    n = len(g16)
    shard = [g.shape[1:] for g in g16]
    return ([pl.BlockSpec(memory_space=pl.ANY)] * (2 * n),
            tuple(const_spec(s) for s in shard),
            tuple(jax.ShapeDtypeStruct(s, F32) for s in shard),
            [pltpu.VMEM((3,) + s, BF16) for s in shard] + [pltpu.VMEM((4,) + s, BF16) for s in shard]
            + [pltpu.VMEM((3,) + s, BF16) for s in shard] + [pltpu.VMEM((3,) + s, BF16) for s in shard]
            + [pltpu.VMEM(s, F32) for s in shard]
            + [pltpu.SemaphoreType.DMA((7 * n,)), pltpu.SemaphoreType.DMA((7 * n,)), pltpu.SemaphoreType.DMA((4 * n,))])


def _in_proj(x2, g_pre, w_in, tm):
    t = x2.shape[0]

    def body(x_ref, g_ref, w_ref, u_ref, zs_ref, q_ref, k_ref, v_ref, za_ref):
        xv = x_ref[...]
        r = lax.rsqrt(jnp.mean(xv * xv, axis=-1, keepdims=True) + EPS)
        hn = xv * r * g_ref[...]
        proj = _mm_nt(hn, w_ref[...])
        u_ref[...] = proj[:, 0:512]
        zs_ref[...] = proj[:, 512:1024]
        q_ref[...] = proj[:, 1024:1536].astype(BF16)
        k_ref[...] = proj[:, 1536:1664].astype(BF16)
        v_ref[...] = proj[:, 1664:1792].astype(BF16)
        za_ref[...] = proj[:, 1792:2304]

    row = lambda w: pl.BlockSpec((tm, w), lambda i: (i, 0))
    return pl.pallas_call(
        body, name="in_proj", grid=(t // tm,),
        in_specs=[row(D_MODEL), _const_spec((1, D_MODEL)), _const_spec((D_IN, D_MODEL))],
        out_specs=(row(512), row(512), row(512), row(128), row(128), row(512)),
        out_shape=(jax.ShapeDtypeStruct((t, 512), F32),
                   jax.ShapeDtypeStruct((t, 512), F32), jax.ShapeDtypeStruct((t, 512), BF16),
                   jax.ShapeDtypeStruct((t, 128), BF16), jax.ShapeDtypeStruct((t, 128), BF16),
                   jax.ShapeDtypeStruct((t, 512), F32)),
        compiler_params=_tc_params(("arbitrary",)),
    )(x2, g_pre, w_in)


def _discretise(lr, li, ls):
    step = jnp.exp(ls)
    mag = jnp.exp(lr * step)
    ar = mag * jnp.cos(li * step)
    ai = mag * jnp.sin(li * step)
    den = lr * lr + li * li
    cr = ((ar - 1.0) * lr + ai * li) / den
    ci = (ai * lr - (ar - 1.0) * li) / den
    return step, ar, ai, den, cr, ci


def _per_channel(v):
    return jnp.broadcast_to(v[:, None, :], (SSM_G, SSM_P, SSM_N)).reshape(SSM_G * SSM_P, SSM_N)


def _tile_masks():
    r = lax.broadcasted_iota(jnp.int32, (CH_T, ST_T), 0) // SSM_P
    l = lax.broadcasted_iota(jnp.int32, (CH_T, ST_T), 1) // SSM_N
    lt = lax.broadcasted_iota(jnp.int32, (ST_T, CH_T), 0) // SSM_N
    rt = lax.broadcasted_iota(jnp.int32, (ST_T, CH_T), 1) // SSM_P
    rep = lax.broadcasted_iota(jnp.int32, (SSM_N, ST_T), 0) == lax.broadcasted_iota(jnp.int32, (SSM_N, ST_T), 1) % SSM_N
    rep_t = lax.broadcasted_iota(jnp.int32, (ST_T, SSM_N), 0) % SSM_N == lax.broadcasted_iota(jnp.int32, (ST_T, SSM_N), 1)
    return r == l, lt == rt, rep, rep_t


def _ssm_prep(lam_re, lam_im, log_step, b_re, b_im, c_re, c_im, seg):
    def work(in_refs, out_refs):
        lr_ref, li_ref, ls_ref, br_ref, bi_ref, cre_ref, cim_ref, lrr_ref, lir_ref, lsr_ref = in_refs
        ar_ref, ai_ref, pr_ref, pi_ref, bcat_ref, bcat_t_ref, ccat_ref, ccat_t_ref = out_refs
        _, _, _, _, cr, ci = _discretise(lr_ref[...], li_ref[...], ls_ref[...])
        cr, ci = _per_channel(cr), _per_channel(ci)
        br, bi = br_ref[...], bi_ref[...]
        bb_re = cr * br - ci * bi
        bb_im = cr * bi + ci * br
        same, same_t, rep, rep_t = _tile_masks()
        rep, rep_t = rep.astype(BF16), rep_t.astype(BF16)
        for j in range(N_GT):
            rows = slice(j * CH_T, (j + 1) * CH_T)
            for wide, tall, parts in ((bcat_ref, bcat_t_ref, (bb_re[rows], bb_im[rows])),
                                      (ccat_t_ref, ccat_ref, (cre_ref[rows, :], -cim_ref[rows, :]))):
                for k, part in enumerate(parts):
                    p16 = part.astype(BF16)
                    wide[j, :, k * ST_T:(k + 1) * ST_T] = jnp.where(same, _mm(p16, rep), 0.0).astype(BF16)
                    tall[j, k * ST_T:(k + 1) * ST_T, :] = jnp.where(same_t, _mm_nt(rep_t, p16), 0.0).astype(BF16)
        stepr = jnp.exp(lsr_ref[...])
        mag = jnp.exp(lrr_ref[...] * stepr)
        a_r, a_i = mag * jnp.cos(lir_ref[...] * stepr), mag * jnp.sin(lir_ref[...] * stepr)
        p_r, p_i = a_r, a_i
        for k in range(8):
            pr_ref[k:k + 1, :] = p_r
            pi_ref[k:k + 1, :] = p_i
            p_r, p_i = p_r * a_r - p_i * a_i, p_r * a_i + p_i * a_r
        n = 8
        while n < seg:
            tr, ti = pr_ref[n - 1:n, :], pi_ref[n - 1:n, :]
            xr, xi = pr_ref[0:n, :], pi_ref[0:n, :]
            pr_ref[n:2 * n, :] = xr * tr - xi * ti
            pi_ref[n:2 * n, :] = xr * ti + xi * tr
            n *= 2
        ar_ref[...] = pr_ref[0:1, :]
        ai_ref[...] = pi_ref[0:1, :]

    row = jax.ShapeDtypeStruct((1, N_STATE), F32)
    pw = jax.ShapeDtypeStruct((seg, N_STATE), F32)
    wide = jax.ShapeDtypeStruct((N_GT, CH_T, 2 * ST_T), BF16)
    tall = jax.ShapeDtypeStruct((N_GT, 2 * ST_T, CH_T), BF16)
    step_row = jnp.broadcast_to(log_step, (SSM_G, SSM_N)).reshape(1, N_STATE)
    inputs = (lam_re, lam_im, log_step, b_re, b_im, c_re, c_im, lam_re.reshape(1, N_STATE),
              lam_im.reshape(1, N_STATE), step_row)
    return work, inputs, (row, row, pw, pw, wide, tall, tall, wide)


def _seg_rows(t):
    if isinstance(t, int):
        return pl.ds(t * N_SEG, N_SEG)
    return pl.ds(pl.multiple_of(t * N_SEG, N_SEG), N_SEG)


def _scan_forward(xs, a_re, a_im, pw_re, pw_im, cs, seg):
    are = jnp.broadcast_to(a_re, (N_SEG, ST_T))
    aim = jnp.broadcast_to(a_im, (N_SEG, ST_T))

    def steps(k, carry):
        xr, xi = carry
        for j in range(SCAN_UNROLL):
            r = pl.multiple_of((k * SCAN_UNROLL + j) * N_SEG, N_SEG)
            nr = are * xr - aim * xi + xs[pl.ds(r, N_SEG), 0:ST_T]
            ni = are * xi + aim * xr + xs[pl.ds(r, N_SEG), ST_T:2 * ST_T]
            xs[pl.ds(r, N_SEG), 0:ST_T] = nr
            xs[pl.ds(r, N_SEG), ST_T:2 * ST_T] = ni
            xr, xi = nr, ni
        return xr, xi

    zero = jnp.zeros((N_SEG, ST_T), F32)
    fr, fi = lax.fori_loop(0, seg // SCAN_UNROLL, steps, (zero, zero))
    sr, si = pw_re[seg - 1:seg, :], pw_im[seg - 1:seg, :]
    cr = jnp.zeros((1, ST_T), F32)
    ci = jnp.zeros((1, ST_T), F32)
    cs[0:1, :] = cr
    cs[8:9, :] = ci
    for s in range(1, N_SEG):
        ncr = sr * cr - si * ci + fr[s - 1:s, :]
        nci = sr * ci + si * cr + fi[s - 1:s, :]
        cr, ci = ncr, nci
        cs[s:s + 1, :] = cr
        cs[8 + s:9 + s, :] = ci
    car, cai = cs[0:8, :], cs[8:16, :]

    def fix(t, _):
        r = pl.multiple_of(t * N_SEG, N_SEG)
        pr, pi = pw_re[pl.ds(t, 1), :], pw_im[pl.ds(t, 1), :]
        xs[pl.ds(r, N_SEG), 0:ST_T] = xs[pl.ds(r, N_SEG), 0:ST_T] + (pr * car - pi * cai)
        xs[pl.ds(r, N_SEG), ST_T:2 * ST_T] = xs[pl.ds(r, N_SEG), ST_T:2 * ST_T] + (pr * cai + pi * car)
        return 0

    lax.fori_loop(0, seg, fix, 0, unroll=SCAN_UNROLL)


def _interleave(src, dst, seg):
    for s in range(N_SEG):
        dst[pl.ds(s, seg, stride=N_SEG), :] = src[s]


def _deinterleave(src, seg, s):
    return src[pl.ds(s, seg, stride=N_SEG), :]


def _ssm_forward(u, bcat, ccat, a_re, a_im, pw_re, pw_im, d_row, late, seg):
    bl = u.shape[0]
    rows = N_SEG * seg
    n = len(late)
    steps = bl * N_GT

    def body(*refs):
        u_ref, b_ref, c_ref, ar_ref, ai_ref, pr_ref, pi_ref, d_ref = refs[:8]
        late_r = refs[8:8 + n]
        y_ref, xs_ref, cs_ref = refs[8 + n:11 + n]
        gath, cast = refs[11 + n:11 + 2 * n], refs[11 + 2 * n:11 + 3 * n]
        send_sems, recv_sems, local_sems, ui, yi = refs[11 + 3 * n:]
        step = pl.program_id(0) * N_GT + pl.program_id(1)
        start, relay, finish = _gather_phases(late_r, gath, cast, send_sems, recv_sems, local_sems)
        pl.when(step == 0)(start)
        _interleave(u_ref.at[0], ui, seg)
        u = ui[...]
        xs, cs = xs_ref.at[0, 0], cs_ref.at[0, 0]
        xs[...] = _mm(u, b_ref[0])
        _scan_forward(xs, ar_ref[...], ai_ref[...], pr_ref, pi_ref, cs, seg)
        yi[...] = _mm(xs[...], c_ref[0]) + d_ref[...] * u
        for s in range(N_SEG):
            y_ref[0, s] = _deinterleave(yi, seg, s)
        pl.when(step == steps // 2)(relay)
        pl.when(step == steps - 1)(finish)

    state = lambda r, c: pl.BlockSpec((1, 1, r, c), lambda b, j: (b, j, 0, 0))
    act = pl.BlockSpec((1, N_SEG, seg, CH_T), lambda b, j: (b, 0, 0, j))
    g_specs, g_shapes, g_scratch = _gather_operands(late)
    res = pl.pallas_call(
        body, name="ssm_forward", grid=(bl, N_GT),
        in_specs=[act,
                  pl.BlockSpec((1, CH_T, 2 * ST_T), lambda b, j: (j, 0, 0)),
                  pl.BlockSpec((1, 2 * ST_T, CH_T), lambda b, j: (j, 0, 0)),
                  pl.BlockSpec((1, ST_T), lambda b, j: (0, j)), pl.BlockSpec((1, ST_T), lambda b, j: (0, j)),
                  pl.BlockSpec((seg, ST_T), lambda b, j: (0, j)), pl.BlockSpec((seg, ST_T), lambda b, j: (0, j)),
                  pl.BlockSpec((1, CH_T), lambda b, j: (0, j))]
        + [pl.BlockSpec(s.shape, lambda b, j: (0, 0)) for s in late],
        out_specs=(act, state(rows, 2 * ST_T), state(16, ST_T)) + g_specs,
        out_shape=(jax.ShapeDtypeStruct((bl, N_SEG, seg, D_SSM), F32),
                   jax.ShapeDtypeStruct((bl, N_GT, rows, 2 * ST_T), F32),
                   jax.ShapeDtypeStruct((bl, N_GT, 16, ST_T), F32)) + g_shapes,
        scratch_shapes=g_scratch + [pltpu.VMEM((rows, CH_T), F32), pltpu.VMEM((rows, CH_T), F32)],
        compiler_params=_tc_params(("arbitrary", "arbitrary")),
    )(u, bcat, ccat, a_re, a_im, pw_re, pw_im, d_row, *late)
    return res[:3], list(res[3:])


def _ssm_backward(u, dy, states, carries, bcat_t, ccat_t, a_re, a_im, pw_re, pw_im, d_row, late16, late32, seg):
    bl = u.shape[0]
    rows = N_SEG * seg
    n = len(late16)
    grid_steps = N_GT * bl

    def body(*refs):
        u_ref, dy_ref, xs_ref, cs_ref, bt_ref, ct_ref, ar_ref, ai_ref, pr_ref, pi_ref, d_ref = refs[:11]
        g16_r, g32_r = refs[11:11 + n], refs[11 + n:11 + 2 * n]
        du_ref, db_ref, dc_ref, dar_ref, dai_ref, dd_ref = refs[11 + 2 * n:17 + 2 * n]
        red = refs[17 + 2 * n:17 + 3 * n]
        own16, recv1, send2, recv2, own32 = (refs[17 + 3 * n + k * n:17 + 3 * n + (k + 1) * n] for k in range(5))
        s_send, s_recv, s_local, ls, cl, ui, dyi, dui = refs[17 + 8 * n:]
        b = pl.program_id(1)
        step = pl.program_id(0) * bl + b
        start, middle, total, finish = _hosted_reduce_phases(g16_r, g32_r, red, own16, recv1, send2, recv2, own32,
                                                             s_send, s_recv, s_local)
        pl.when(step == 0)(start)
        pl.when(step == grid_steps // 4)(middle)
        pl.when(step == (grid_steps * 3) // 4)(total)
        pl.when(step == grid_steps - 1)(finish)
        _interleave(u_ref.at[0], ui, seg)
        _interleave(dy_ref.at[0], dyi, seg)
        u = ui[...]
        dy = dyi[...]
        xs, cs = xs_ref.at[0, 0], cs_ref.at[0, 0]
        ls[...] = _mm(dy, ct_ref[0])
        are = jnp.broadcast_to(ar_ref[...], (N_SEG, ST_T))
        aim = jnp.broadcast_to(ai_ref[...], (N_SEG, ST_T))

        def steps(k, carry):
            lr, li = carry
            for j in range(SCAN_UNROLL):
                r = pl.multiple_of((seg - 1 - (k * SCAN_UNROLL + j)) * N_SEG, N_SEG)
                nr = are * lr + aim * li + ls[pl.ds(r, N_SEG), 0:ST_T]
                ni = are * li - aim * lr + ls[pl.ds(r, N_SEG), ST_T:2 * ST_T]
                ls[pl.ds(r, N_SEG), 0:ST_T] = nr
                ls[pl.ds(r, N_SEG), ST_T:2 * ST_T] = ni
                lr, li = nr, ni
            return lr, li

        zero = jnp.zeros((N_SEG, ST_T), F32)
        fr, fi = lax.fori_loop(0, seg // SCAN_UNROLL, steps, (zero, zero))
        sr, si = pr_ref[seg - 1:seg, :], pi_ref[seg - 1:seg, :]
        cr = jnp.zeros((1, ST_T), F32)
        ci = jnp.zeros((1, ST_T), F32)
        cl[7:8, :] = cr
        cl[15:16, :] = ci
        for s in range(N_SEG - 2, -1, -1):
            ncr = sr * cr + si * ci + fr[s + 1:s + 2, :]
            nci = sr * ci - si * cr + fi[s + 1:s + 2, :]
            cr, ci = ncr, nci
            cl[s:s + 1, :] = cr
            cl[8 + s:9 + s, :] = ci
        clr, cli = cl[0:8, :], cl[8:16, :]

        def fix_rows(rows, t, xpr, xpi, acc):
            dr, di = acc
            pr, pi = pr_ref[pl.ds(seg - 1 - t, 1), :], pi_ref[pl.ds(seg - 1 - t, 1), :]
            lr = ls[rows, 0:ST_T] + (pr * clr + pi * cli)
            li = ls[rows, ST_T:2 * ST_T] + (pr * cli - pi * clr)
            ls[rows, 0:ST_T] = lr
            ls[rows, ST_T:2 * ST_T] = li
            return dr + (lr * xpr + li * xpi), di + (li * xpr - lr * xpi)

        def fix_at(t, acc):
            prev = _seg_rows(t - 1)
            return fix_rows(_seg_rows(t), t, xs[prev, 0:ST_T], xs[prev, ST_T:2 * ST_T], acc)

        def fix(k, acc):
            for j in range(SCAN_UNROLL):
                acc = fix_at(k * SCAN_UNROLL + j, acc)
            return acc

        acc = fix_rows(pl.ds(0, N_SEG), 0, cs[0:8, :], cs[8:16, :], (zero, zero))
        for t in range(1, SCAN_UNROLL):
            acc = fix_at(t, acc)
        dr, di = lax.fori_loop(1, seg // SCAN_UNROLL, fix, acc)
        dar = jnp.sum(dr, axis=0, keepdims=True)
        dai = jnp.sum(di, axis=0, keepdims=True)
        lall = ls[...]
        dui[...] = _mm(lall, bt_ref[0]) + d_ref[...] * dy
        for s in range(N_SEG):
            du_ref[0, s] = _deinterleave(dui, seg, s).astype(BF16)
        dbp = _mm_tn(u, lall)
        dcp = _mm_tn(dy, xs[...])
        ddp = jnp.sum(dy * u, axis=0, keepdims=True)

        @pl.when(b == 0)
        def _():
            db_ref[0] = dbp
            dc_ref[0] = dcp
            dar_ref[...] = dar
            dai_ref[...] = dai
            dd_ref[...] = ddp

        @pl.when(b != 0)
        def _():
            db_ref[0] += dbp
            dc_ref[0] += dcp
            dar_ref[...] += dar
            dai_ref[...] += dai
            dd_ref[...] += ddp

    tile3 = lambda r, c: pl.BlockSpec((1, r, c), lambda j, b: (j, 0, 0))
    lane = lambda r, c: pl.BlockSpec((r, c), lambda j, b: (0, j))
    act = pl.BlockSpec((1, N_SEG, seg, CH_T), lambda j, b: (b, 0, 0, j))
    state = lambda r, c: pl.BlockSpec((1, 1, r, c), lambda j, b: (b, j, 0, 0))
    r_in, r_out, r_shapes, r_scratch = _hosted_reduce_operands(late16, lambda s: pl.BlockSpec(s, lambda j, b: (0, 0)))
    res = pl.pallas_call(
        body, name="ssm_backward", grid=(N_GT, bl),
        in_specs=[act, act, state(rows, 2 * ST_T), state(16, ST_T), tile3(2 * ST_T, CH_T), tile3(CH_T, 2 * ST_T),
                  lane(1, ST_T), lane(1, ST_T), lane(seg, ST_T), lane(seg, ST_T), lane(1, CH_T)] + r_in,
        out_specs=(act, tile3(CH_T, 2 * ST_T), tile3(CH_T, 2 * ST_T), lane(1, ST_T), lane(1, ST_T), lane(1, CH_T))
        + r_out,
        out_shape=(jax.ShapeDtypeStruct((bl, N_SEG, seg, D_SSM), BF16),
                   jax.ShapeDtypeStruct((N_GT, CH_T, 2 * ST_T), F32), jax.ShapeDtypeStruct((N_GT, CH_T, 2 * ST_T), F32),
                   jax.ShapeDtypeStruct((1, N_STATE), F32), jax.ShapeDtypeStruct((1, N_STATE), F32),
                   jax.ShapeDtypeStruct((1, D_SSM), F32)) + r_shapes,
        scratch_shapes=r_scratch + [pltpu.VMEM((rows, 2 * ST_T), F32), pltpu.VMEM((16, ST_T), F32)]
        + [pltpu.VMEM((rows, CH_T), F32)] * 3,
        compiler_params=_tc_params(("arbitrary", "arbitrary")),
    )(u, dy, states, carries, bcat_t, ccat_t, a_re, a_im, pw_re, pw_im, d_row, *late16, *late32)
    return res[:6], list(res[6:])


def _ssm_param_grads(lam_re, lam_im, log_step, b_re, b_im, da_re, da_im, d_bcat, d_ccat_t):
    def body(lr_ref, li_ref, ls_ref, br_ref, bi_ref, gar_ref, gai_ref, gbcat_ref, gccat_ref,
             dlr_ref, dli_ref, dls_ref, dbr_ref, dbi_ref, dcr_ref, dci_ref, gbr_s, gbi_s):
        same, _, _, rep_t = _tile_masks()
        rep_t = rep_t.astype(F32)
        for j in range(N_GT):
            rows = slice(j * CH_T, (j + 1) * CH_T)
            for src, dsts in ((gbcat_ref, (gbr_s, gbi_s)), (gccat_ref, (dcr_ref, dci_ref))):
                for k, dst in enumerate(dsts):
                    blk = jnp.where(same, src[j, :, k * ST_T:(k + 1) * ST_T], 0.0)
                    dst[rows, :] = jnp.dot(blk, rep_t, precision=lax.Precision.HIGHEST, preferred_element_type=F32)
        dci_ref[...] = -dci_ref[...]
        lr, li = lr_ref[...], li_ref[...]
        step, ar, ai, den, cr, ci = _discretise(lr, li, ls_ref[...])
        crb, cib = _per_channel(cr), _per_channel(ci)
        br, bi = br_ref[...], bi_ref[...]
        gbr, gbi = gbr_s[...], gbi_s[...]
        dbr_ref[...] = crb * gbr + cib * gbi
        dbi_ref[...] = crb * gbi - cib * gbr
        over_channels = lambda t: jnp.sum(t.reshape(SSM_G, SSM_P, SSM_N), axis=1)
        gcr = over_channels(br * gbr + bi * gbi)
        gci = over_channels(br * gbi - bi * gbr)
        ilr, ili = lr / den, -li / den
        gar = gar_ref[...] + (ilr * gcr + ili * gci)
        gai = gai_ref[...] + (ilr * gci - ili * gcr)
        qr, qi = cr * ilr - ci * ili, cr * ili + ci * ilr
        glr = -(qr * gcr + qi * gci)
        gli = -(qr * gci - qi * gcr)
        gwr = ar * gar + ai * gai
        gwi = ar * gai - ai * gar
        dlr_ref[...] = glr + step * gwr
        dli_ref[...] = gli + step * gwi
        dls_ref[...] = jnp.sum(lr * gwr + li * gwi, axis=-1, keepdims=True) * step

    lam = jax.ShapeDtypeStruct((SSM_G, SSM_N), F32)
    mat = jax.ShapeDtypeStruct((SSM_G * SSM_P, SSM_N), F32)
    vm = pl.BlockSpec(memory_space=pltpu.VMEM)
    return pl.pallas_call(
        body, name="ssm_param_grads", out_shape=(lam, lam, jax.ShapeDtypeStruct((SSM_G, 1), F32), mat, mat, mat, mat),
        in_specs=[vm] * 9, out_specs=(vm,) * 7,
        scratch_shapes=[pltpu.VMEM((SSM_G * SSM_P, SSM_N), F32), pltpu.VMEM((SSM_G * SSM_P, SSM_N), F32)],
    )(lam_re, lam_im, log_step, b_re, b_im, da_re, da_im, d_bcat, d_ccat_t)


ROWS4 = Q_PER_KV * ATT_BLOCK
ATT_FWD_STACK = 1


def _att_dist_mask(first_block):
    qi = lax.broadcasted_iota(jnp.int32, (ROWS4, 2 * ATT_BLOCK), 0) & (ATT_BLOCK - 1)
    si = lax.broadcasted_iota(jnp.int32, (ROWS4, 2 * ATT_BLOCK), 1)
    dist = qi + ATT_BLOCK - si
    valid = (dist >= 0) & (dist < ATT_BLOCK) & ((si >= ATT_BLOCK) | jnp.logical_not(first_block))
    return dist.astype(F32), valid


def _stack_heads(x, kv):
    return jnp.concatenate([x[:, (kv * Q_PER_KV + g) * HEAD_DIM:(kv * Q_PER_KV + g + 1) * HEAD_DIM]
                            for g in range(Q_PER_KV)], axis=0)


def _stack_cols(x, kv):
    return jnp.concatenate([x[:, kv * Q_PER_KV + g:kv * Q_PER_KV + g + 1] for g in range(Q_PER_KV)], axis=0)


def _per_head_col(vals):
    return jnp.concatenate([jnp.full((ATT_BLOCK, 1), v, F32) for v in vals], axis=0)


def _attn_forward(q, k, v, sinks, bl, nb):
    t = q.shape[0]

    def body(sink_ref, q_ref, kp_ref, kc_ref, vp_ref, vc_ref, o_ref, lse_ref):
        i = pl.program_id(1)
        dist4, valid4 = _att_dist_mask(i == 0)
        rows2 = ATT_FWD_STACK * ATT_BLOCK
        dist, valid = dist4[0:rows2, :], valid4[0:rows2, :]
        kk = jnp.concatenate([kp_ref[...], kc_ref[...]], axis=0)
        vv = jnp.concatenate([vp_ref[...], vc_ref[...]], axis=0)
        qv = q_ref[...]
        col = lambda vals: jnp.concatenate([jnp.full((ATT_BLOCK, 1), v, F32) for v in vals], axis=0)
        stacks = [range(h0, h0 + ATT_FWD_STACK) for h0 in range(0, N_HEADS, ATT_FWD_STACK)]
        kv_cols = lambda heads: slice(heads[0] // Q_PER_KV * HEAD_DIM, (heads[0] // Q_PER_KV + 1) * HEAD_DIM)
        scores = [_mm_nt(jnp.concatenate([qv[:, h * HEAD_DIM:(h + 1) * HEAD_DIM] for h in heads], axis=0),
                         kk[:, kv_cols(heads)]) for heads in stacks]
        softmaxes = []
        for heads, qk in zip(stacks, scores):
            slope = col([2.0 ** (-(h + 1)) for h in heads])
            sink = col([sink_ref[h] for h in heads])
            s = jnp.where(valid, qk * ATT_SCALE - slope * dist, NEG_INF)
            m = jnp.maximum(jnp.max(s, axis=-1, keepdims=True), sink)
            e = jnp.exp(s - m)
            den = jnp.sum(e, axis=-1, keepdims=True) + jnp.exp(sink - m)
            softmaxes.append((e.astype(BF16), 1.0 / den, m + jnp.log(den)))
        for heads, (e, inv_den, lse) in zip(stacks, softmaxes):
            o = _mm(e, vv[:, kv_cols(heads)]) * inv_den
            for g, h in enumerate(heads):
                rows = slice(g * ATT_BLOCK, (g + 1) * ATT_BLOCK)
                o_ref[:, h * HEAD_DIM:(h + 1) * HEAD_DIM] = o[rows, :]
                lse_ref[:, h:h + 1] = lse[rows, :]

    cur = lambda w: pl.BlockSpec((ATT_BLOCK, w), lambda b, i: (b * nb + i, 0))
    prev = lambda w: pl.BlockSpec((ATT_BLOCK, w), lambda b, i: (b * nb + jnp.maximum(i - 1, 0), 0))
    return pl.pallas_call(
        body, name="attn_forward", grid=(bl, nb),
        in_specs=[pl.BlockSpec(memory_space=pltpu.SMEM), cur(512), prev(128), cur(128), prev(128), cur(128)],
        out_specs=(cur(512), cur(N_HEADS)),
        out_shape=(jax.ShapeDtypeStruct((t, D_ATTN), F32), jax.ShapeDtypeStruct((t, N_HEADS), F32)),
        compiler_params=_tc_params(("arbitrary", "arbitrary")),
    )(sinks, q, k, k, v, v)


def _attn_backward(q, k, v, o, do, lse, sinks, bl, nb):
    t = q.shape[0]

    def body(sink_ref, qc_ref, kp_ref, kc_ref, vp_ref, vc_ref, oc_ref, doc_ref, lc_ref,
             dq_ref, dk_ref, dv_ref, ds_ref, dk_carry, dv_carry):
        b, i = pl.program_id(0), pl.program_id(1)
        live = i < nb

        @pl.when(i == 0)
        def _():
            dk_carry[...] = jnp.zeros((ATT_BLOCK, KV_HEADS * HEAD_DIM), F32)
            dv_carry[...] = jnp.zeros((ATT_BLOCK, KV_HEADS * HEAD_DIM), F32)

        dist, valid = _att_dist_mask(i == 0)
        valid = valid & live
        kk = jnp.concatenate([kp_ref[...], kc_ref[...]], axis=0)
        vv = jnp.concatenate([vp_ref[...], vc_ref[...]], axis=0)
        qc, oc, doc, lc = qc_ref[...], oc_ref[...], doc_ref[...], lc_ref[...]
        dsink_cols, dq_parts, dk_t, dv_t = [], [], [], []
        for kv in range(KV_HEADS):
            heads = range(kv * Q_PER_KV, (kv + 1) * Q_PER_KV)
            cols = slice(kv * HEAD_DIM, (kv + 1) * HEAD_DIM)
            kh, vh = kk[:, cols], vv[:, cols]
            slope = _per_head_col([2.0 ** (-(h + 1)) for h in heads])
            sink = _per_head_col([sink_ref[h] for h in heads])
            q4, do4 = _stack_heads(qc, kv), _stack_heads(doc, kv)
            delta = jnp.sum(do4 * _stack_heads(oc, kv), axis=-1, keepdims=True)
            lse4 = _stack_cols(lc, kv)
            s = _mm_nt(q4, kh) * ATT_SCALE - slope * dist
            p = jnp.where(valid, jnp.exp(s - lse4), 0.0)
            dsc = p * (_mm_nt(do4, vh) - delta)
            dq4 = _mm(dsc, kh) * ATT_SCALE
            dk_t.append(_mm_tn(q4, dsc) * ATT_SCALE)
            dv_t.append(_mm_tn(do4, p))
            dsink4 = jnp.where(live, jnp.exp(sink - lse4) * delta, 0.0)
            for g, h in enumerate(heads):
                rows = slice(g * ATT_BLOCK, (g + 1) * ATT_BLOCK)
                dq_parts.append((h, dq4[rows, :]))
                dsink_cols.append(-jnp.sum(dsink4[rows, :], axis=0, keepdims=True))
        dsink = jnp.concatenate(dsink_cols, axis=1)
        for out_ref, carry, parts in ((dk_ref, dk_carry, dk_t), (dv_ref, dv_carry, dv_t)):
            both = jnp.concatenate(parts, axis=0)
            out_ref[...] = (carry[...] + both[:, 0:ATT_BLOCK]).T
            carry[...] = both[:, ATT_BLOCK:]

        @pl.when(live)
        def _():
            for h, part in dq_parts:
                dq_ref[:, h * HEAD_DIM:(h + 1) * HEAD_DIM] = part

        @pl.when((b == 0) & (i == 0))
        def _():
            ds_ref[...] = dsink

        @pl.when((b != 0) | (i != 0))
        def _():
            ds_ref[...] += dsink

    cur_i = lambda i: jnp.minimum(i, nb - 1)
    cur = lambda w: pl.BlockSpec((ATT_BLOCK, w), lambda b, i: (b * nb + cur_i(i), 0))
    prev = lambda w: pl.BlockSpec((ATT_BLOCK, w), lambda b, i: (b * nb + jnp.maximum(cur_i(i) - 1, 0), 0))
    behind = lambda w: pl.BlockSpec((ATT_BLOCK, w), lambda b, i: (b * nb + jnp.maximum(i - 1, 0), 0))
    return pl.pallas_call(
        body, name="attn_backward", grid=(bl, nb + 1),
        in_specs=[pl.BlockSpec(memory_space=pltpu.SMEM), cur(512), prev(128), cur(128), prev(128), cur(128),
                  cur(512), cur(512), cur(N_HEADS)],
        out_specs=(cur(512), behind(128), behind(128), pl.BlockSpec((1, N_HEADS), lambda b, i: (0, 0))),
        out_shape=(jax.ShapeDtypeStruct((t, D_ATTN), F32), jax.ShapeDtypeStruct((t, 128), F32),
                   jax.ShapeDtypeStruct((t, 128), F32), jax.ShapeDtypeStruct((1, N_HEADS), F32)),
        scratch_shapes=[pltpu.VMEM((ATT_BLOCK, KV_HEADS * HEAD_DIM), F32), pltpu.VMEM((ATT_BLOCK, KV_HEADS * HEAD_DIM), F32)],
        compiler_params=_tc_params(("arbitrary", "arbitrary")),
    )(sinks, q, k, k, v, v, o, do, lse)


def _mix_forward_backward(x2, y2, z_ssm, attn, z_attn, p2, target2, w_glu, b_glu, w_out, g_post, w_gate, b_gate,
                          w_proj, tm):
    t = x2.shape[0]

    def body(x_ref, y_ref, zs_ref, at_ref, za_ref, p_ref, tg_ref,
             wglu_ref, bglu_ref, wout_ref, gpost_ref, wgate_ref, bgate_ref, wproj_ref,
             loss_ref, dh1_ref, dy_ref, dzs_ref, dat_ref, dza_ref,
             dwglu_ref, dbglu_ref, dwout_ref, dgpost_ref, dwgate_ref, dbgate_ref, dwproj_ref,
             dwout16_ref, dwgate16_ref, dwproj16_ref, dwglu16_ref):
        i = pl.program_id(0)
        gpost = gpost_ref[...]

        @pl.when(i == 0)
        def _():
            for ref in (dwglu_ref, dbglu_ref, dwout_ref, dgpost_ref, dwgate_ref, dbgate_ref, dwproj_ref, loss_ref):
                ref[...] = jnp.zeros(ref.shape, F32)

        def chain(rows):
            y = y_ref[rows, :]
            u3 = GELU_C * (y + GELU_K * y * y * y)
            th = jnp.tanh(u3)
            gl = 0.5 * y * (1.0 + th)
            a = _mm(gl, wglu_ref[...]) + bglu_ref[...]
            sa = _sigmoid(a)
            glu = gl * sa
            zs = zs_ref[rows, :]
            sgs = _sigmoid(zs)
            ssm_out = glu * (zs * sgs)
            za = za_ref[rows, :]
            sga = _sigmoid(za)
            at = at_ref[rows, :]
            attn_out = at * (za * sga)
            cat = jnp.concatenate([ssm_out, attn_out], axis=-1).astype(BF16)
            mixed = _mm(cat, wout_ref[...])
            r2 = lax.rsqrt(jnp.mean(mixed * mixed, axis=-1, keepdims=True) + EPS)
            nhat = mixed * r2
            h1 = x_ref[rows, :] + nhat * gpost
            gate = _sigmoid(_mm(h1, wgate_ref[...]) + bgate_ref[...])
            pv = p_ref[rows, :]
            pp = _mm(pv, wproj_ref[...])
            h2 = h1 + gate * pp
            err = h2 - tg_ref[rows, :]
            loss_part = jnp.sum(jnp.sum(err * err, axis=-1, keepdims=True), axis=0, keepdims=True) * (0.5 / D_MODEL)
            dh2 = err * (1.0 / D_MODEL)
            dgp = dh2 * pp * gate * (1.0 - gate)
            dpp = dh2 * gate
            dh1 = dh2 + _mm_nt(dgp, wgate_ref[...])
            dh1_ref[rows, :] = dh1
            dnhat = dh1 * gpost
            dmixed = r2 * (dnhat - nhat * jnp.mean(dnhat * nhat, axis=-1, keepdims=True))
            dcat = _mm_nt(dmixed, wout_ref[...])
            dso, dao = dcat[:, 0:D_SSM], dcat[:, D_SSM:]
            dat_ref[rows, :] = dao * (za * sga)
            dza_ref[rows, :] = (dao * at * (sga * (1.0 + za * (1.0 - sga)))).astype(BF16)
            dzs_ref[rows, :] = (dso * glu * (sgs * (1.0 + zs * (1.0 - sgs)))).astype(BF16)
            dglu = dso * (zs * sgs)
            da = dglu * gl * sa * (1.0 - sa)
            dgl = dglu * sa + _mm_nt(da, wglu_ref[...])
            dgelu = 0.5 * (1.0 + th) + 0.5 * y * (1.0 - th * th) * (GELU_C * (1.0 + 3.0 * GELU_K * y * y))
            dy_ref[rows, :] = dgl * dgelu
            return dict(gl=gl.astype(BF16), da=da.astype(BF16), cat=cat, dmixed=dmixed.astype(BF16),
                        h1=h1.astype(BF16), dgp=dgp.astype(BF16), pv=pv.astype(BF16), dpp=dpp.astype(BF16),
                        dbglu=jnp.sum(da, axis=0, keepdims=True), dgpost=jnp.sum(dh1 * nhat, axis=0, keepdims=True),
                        dbgate=jnp.sum(dgp, axis=0, keepdims=True), loss=loss_part)

        groups = [chain(slice(k * (tm // MIX_GROUPS), (k + 1) * (tm // MIX_GROUPS))) for k in range(MIX_GROUPS)]
        rows_of = lambda name: jnp.concatenate([g[name] for g in groups], axis=0)
        total = lambda name: sum(g[name] for g in groups)
        parts = (
            (dwglu_ref, _mm_tn(rows_of("gl"), rows_of("da"))), (dbglu_ref, total("dbglu")),
            (dwout_ref, _mm_tn(rows_of("cat"), rows_of("dmixed"))), (dgpost_ref, total("dgpost")),
            (dwgate_ref, _mm_tn(rows_of("h1"), rows_of("dgp"))), (dbgate_ref, total("dbgate")),
            (dwproj_ref, _mm_tn(rows_of("pv"), rows_of("dpp"))), (loss_ref, total("loss")),
        )

        for ref, val in parts:
            ref[...] += val

        @pl.when(i == t // tm - 1)
        def _():
            for ref16, ref in ((dwout16_ref, dwout_ref), (dwgate16_ref, dwgate_ref), (dwproj16_ref, dwproj_ref),
                               (dwglu16_ref, dwglu_ref)):
                def to16(r, ref16=ref16, ref=ref):
                    ref16[r, :] = ref[r, :].astype(BF16)

                _row_chunks(ref.shape[0], to16)

    row = lambda w: pl.BlockSpec((tm, w), lambda i: (i, 0))
    acc = lambda r, c, dt=F32: (_const_spec((r, c)), jax.ShapeDtypeStruct((r, c), dt))
    accs = [acc(D_SSM, D_SSM), acc(1, D_SSM), acc(D_MODEL, D_MODEL), acc(1, D_MODEL), acc(D_MODEL, D_MODEL),
            acc(1, D_MODEL), acc(D_PLE, D_MODEL),
            acc(D_MODEL, D_MODEL, BF16), acc(D_MODEL, D_MODEL, BF16), acc(D_PLE, D_MODEL, BF16), acc(D_SSM, D_SSM, BF16)]
    return pl.pallas_call(
        body, name="mix_forward_backward", grid=(t // tm,),
        in_specs=[row(D_MODEL), row(512), row(512), row(512), row(512), row(D_PLE), row(D_MODEL),
                  _const_spec((D_SSM, D_SSM)), _const_spec((1, D_SSM)), _const_spec((D_MODEL, D_MODEL)),
                  _const_spec((1, D_MODEL)), _const_spec((D_MODEL, D_MODEL)), _const_spec((1, D_MODEL)),
                  _const_spec((D_PLE, D_MODEL))],
        out_specs=(_const_spec((1, 1)), row(D_MODEL), row(512), row(512), row(512), row(512))
        + tuple(a[0] for a in accs),
        out_shape=(jax.ShapeDtypeStruct((1, 1), F32), jax.ShapeDtypeStruct((t, D_MODEL), F32),
                   jax.ShapeDtypeStruct((t, 512), F32),
                   jax.ShapeDtypeStruct((t, 512), BF16), jax.ShapeDtypeStruct((t, 512), F32),
                   jax.ShapeDtypeStruct((t, 512), BF16)) + tuple(a[1] for a in accs),
        compiler_params=_tc_params(("arbitrary",)),
    )(x2, y2, z_ssm, attn, z_attn, p2, target2, w_glu, b_glu, w_out, g_post, w_gate, b_gate, w_proj)


def _in_backward(x2, dh1, du, dz_ssm, dq, dk, dv, dz_attn, g_pre, w_in, tm):
    t = x2.shape[0]

    def body(x_ref, dh1_ref, du_ref, dzs_ref, dq_ref, dk_ref, dv_ref, dza_ref, g_ref, w_ref,
             gx_ref, dw_ref, dg_ref, dw16_ref):
        i = pl.program_id(0)
        xv = x_ref[...]
        r = lax.rsqrt(jnp.mean(xv * xv, axis=-1, keepdims=True) + EPS)
        xhat = xv * r
        g = g_ref[...]
        hn = (xhat * g).astype(BF16)
        dproj = jnp.concatenate([du_ref[...].astype(BF16), dzs_ref[...].astype(BF16), dq_ref[...].astype(BF16),
                                 dk_ref[...].astype(BF16), dv_ref[...].astype(BF16), dza_ref[...].astype(BF16)],
                                axis=-1)
        dhn = _mm(dproj, w_ref[...])
        dxhat = dhn * g
        gx_ref[...] = dh1_ref[...] + r * (dxhat - xhat * jnp.mean(dxhat * xhat, axis=-1, keepdims=True))
        @pl.when(i == 0)
        def _():
            dw_ref[...] = jnp.zeros((D_IN, D_MODEL), F32)
            dg_ref[...] = jnp.zeros((1, D_MODEL), F32)

        dw_ref[...] += _mm_tn(dproj, hn)
        dg_ref[...] += jnp.sum(dhn * xhat, axis=0, keepdims=True)

        @pl.when(i == t // tm - 1)
        def _():
            def to16(r):
                dw16_ref[r, :] = dw_ref[r, :].astype(BF16)

            _row_chunks(D_IN, to16)

    row = lambda w: pl.BlockSpec((tm, w), lambda i: (i, 0))
    return pl.pallas_call(
        body, name="in_backward", grid=(t // tm,),
        in_specs=[row(D_MODEL), row(D_MODEL), row(512), row(512), row(512), row(128), row(128), row(512),
                  _const_spec((1, D_MODEL)), _const_spec((D_IN, D_MODEL))],
        out_specs=(row(D_MODEL), _const_spec((D_IN, D_MODEL)), _const_spec((1, D_MODEL)),
                   _const_spec((D_IN, D_MODEL))),
        out_shape=(jax.ShapeDtypeStruct((t, D_MODEL), F32), jax.ShapeDtypeStruct((D_IN, D_MODEL), F32),
                   jax.ShapeDtypeStruct((1, D_MODEL), F32), jax.ShapeDtypeStruct((D_IN, D_MODEL), BF16)),
        compiler_params=_tc_params(("arbitrary",)),
    )(x2, dh1, du, dz_ssm, dq, dk, dv, dz_attn, g_pre, w_in)


def _local_step(x, p, target, pre_norm_g, w_in, prep, ssm_lam_re, ssm_lam_im, ssm_log_step, ssm_b_re, ssm_b_im, ssm_d,
                ssm_b_glu, attn_sinks, post_norm_g, pl_b_gate, late):
    bl, seq, _ = x.shape
    seg = seq // N_SEG
    nb = seq // ATT_BLOCK
    t = bl * seq
    x2 = x.reshape(t, D_MODEL)
    p2 = p.reshape(t, D_PLE)
    tg2 = target.reshape(t, D_MODEL)

    lam_re, lam_im = ssm_lam_re, ssm_lam_im
    log_step = ssm_log_step.reshape(SSM_G, 1)
    a_re_row, a_im_row, pw_re, pw_im, bcat, bcat_t, ccat, ccat_t = prep
    d_row = ssm_d.reshape(1, D_SSM)

    segments = lambda a: a.reshape(bl, N_SEG, seg, D_SSM)
    u, z_ssm, q, k, v, z_attn = _in_proj(x2, pre_norm_g.reshape(1, D_MODEL), w_in, min(TOKEN_TILE_WIDE, t))
    (y, states, carries), gathered = _ssm_forward(
        segments(u), bcat, ccat, a_re_row, a_im_row, pw_re, pw_im, d_row, late, seg)
    w_out, w_gate, w_proj, w_glu = (_gathered_to_full(n, g) for n, g in zip(LATE_NAMES, gathered))
    sinks = attn_sinks.reshape(N_HEADS)
    attn, lse = _attn_forward(q, k, v, sinks, bl, nb)
    (loss, dh1, dy, dz_ssm, dattn, dz_attn, d_w_glu, d_b_glu, d_w_out, d_g_post, d_w_gate, d_b_gate,
     d_w_proj, *late16) = _mix_forward_backward(
        x2, y.reshape(t, D_SSM), z_ssm, attn, z_attn, p2, tg2, w_glu,
        ssm_b_glu.reshape(1, D_SSM), w_out, post_norm_g.reshape(1, D_MODEL), w_gate, pl_b_gate.reshape(1, D_MODEL),
        w_proj, min(TOKEN_TILE, t))
    owned = lambda ds: [_full_to_owned(n, d) for n, d in zip(LATE_NAMES, ds)]
    dq, dk, dv, d_sinks = _attn_backward(q, k, v, attn, dattn, lse, sinks, bl, nb)
    (du, d_bcat, d_ccat_t, da_re, da_im, d_d), late_grads = _ssm_backward(
        segments(u), segments(dy), states, carries, bcat_t, ccat_t, a_re_row, a_im_row, pw_re, pw_im,
        d_row, owned(late16), owned((d_w_out, d_w_gate, d_w_proj, d_w_glu)), seg)
    grad_x, d_w_in, d_g_pre, d_w_in16 = _in_backward(
        x2, dh1, du.reshape(t, D_SSM), dz_ssm, dq, dk, dv, dz_attn, pre_norm_g.reshape(1, D_MODEL), w_in,
        min(TOKEN_TILE_WIDE, t))
    d_lam_re, d_lam_im, d_ls, d_b_re, d_b_im, d_c_re, d_c_im = _ssm_param_grads(
        lam_re, lam_im, log_step, ssm_b_re, ssm_b_im, da_re.reshape(SSM_G, SSM_N), da_im.reshape(SSM_G, SSM_N),
        d_bcat, d_ccat_t)
    grads = {
        "pre_norm_g": d_g_pre, "w_in": d_w_in, "w_in16": d_w_in16, "ssm_lam_re": d_lam_re, "ssm_lam_im": d_lam_im,
        "ssm_log_step": d_ls, "ssm_b_re": d_b_re, "ssm_b_im": d_b_im, "ssm_c_re": d_c_re, "ssm_c_im": d_c_im,
        "ssm_d": d_d, "ssm_b_glu": d_b_glu, "attn_sinks": d_sinks, "post_norm_g": d_g_post, "pl_b_gate": d_b_gate,
    }
    return loss, grad_x.reshape(bl, seq, D_MODEL), grads, late_grads


LATE_NAMES = ("w_out", "pl_w_gate", "pl_w_proj", "ssm_w_glu")
BIG_NAMES = ("w_in",) + LATE_NAMES
COL_SHARDED = {"w_in": D_IN // N_DEV, "pl_w_proj": D_MODEL // N_DEV}
WEIGHT_NAMES = ("pre_norm_g", "w_in", "ssm_lam_re", "ssm_lam_im", "ssm_log_step", "ssm_b_re", "ssm_b_im", "ssm_c_re",
                "ssm_c_im", "ssm_d", "ssm_w_glu", "ssm_b_glu", "attn_sinks", "w_out", "post_norm_g", "pl_w_proj",
                "pl_w_gate", "pl_b_gate")


TRANSPOSED = {"w_in": (0, 1), "ssm_b_re": (1, 2), "ssm_b_im": (1, 2)}


def _kernel_form(name, a):
    a = a[0]
    if name in TRANSPOSED:
        a = jnp.swapaxes(a, *TRANSPOSED[name])
    if name in ("ssm_b_re", "ssm_b_im", "ssm_c_re", "ssm_c_im"):
        a = a.reshape(SSM_G * SSM_P, SSM_N)
    return a


def _given_form(name, a, shape):
    if name in TRANSPOSED:
        i, j = TRANSPOSED[name]
        swapped = list(shape[1:])
        swapped[i], swapped[j] = swapped[j], swapped[i]
        return jnp.swapaxes(a.reshape(swapped), i, j).reshape(shape)
    return a.reshape(shape)


def _gathered_to_full(name, g):
    _, rows, cols = g.shape
    if name in COL_SHARDED:
        return jnp.swapaxes(g, 0, 1).reshape(rows, N_DEV * cols)
    return g.reshape(N_DEV * rows, cols)


def _full_to_owned(name, full):
    if name in COL_SHARDED:
        return jnp.swapaxes(full.reshape(full.shape[0], N_DEV, COL_SHARDED[name]), 0, 1)
    return full.reshape(N_DEV, full.shape[0] // N_DEV, full.shape[1])


def kernel(x, p, pre_norm_g, w_in, ssm_lam_re, ssm_lam_im, ssm_log_step, ssm_b_re, ssm_b_im, ssm_c_re, ssm_c_im, ssm_d, ssm_w_glu, ssm_b_glu, attn_sinks, w_out, post_norm_g, pl_w_proj, pl_w_gate, pl_b_gate, loss_target, m_pre_norm_g, m_w_in, m_ssm_lam_re, m_ssm_lam_im, m_ssm_log_step, m_ssm_b_re, m_ssm_b_im, m_ssm_c_re, m_ssm_c_im, m_ssm_d, m_ssm_w_glu, m_ssm_b_glu, m_attn_sinks, m_w_out, m_post_norm_g, m_pl_w_proj, m_pl_w_gate, m_pl_b_gate, v_pre_norm_g, v_w_in, v_ssm_lam_re, v_ssm_lam_im, v_ssm_log_step, v_ssm_b_re, v_ssm_b_im, v_ssm_c_re, v_ssm_c_im, v_ssm_d, v_ssm_w_glu, v_ssm_b_glu, v_attn_sinks, v_w_out, v_post_norm_g, v_pl_w_proj, v_pl_w_gate, v_pl_b_gate):
    w = dict(pre_norm_g=pre_norm_g, w_in=w_in, ssm_lam_re=ssm_lam_re, ssm_lam_im=ssm_lam_im, ssm_log_step=ssm_log_step,
             ssm_b_re=ssm_b_re, ssm_b_im=ssm_b_im, ssm_c_re=ssm_c_re, ssm_c_im=ssm_c_im, ssm_d=ssm_d, ssm_w_glu=ssm_w_glu,
             ssm_b_glu=ssm_b_glu, attn_sinks=attn_sinks, w_out=w_out, post_norm_g=post_norm_g, pl_w_proj=pl_w_proj,
             pl_w_gate=pl_w_gate, pl_b_gate=pl_b_gate)
    m = dict(pre_norm_g=m_pre_norm_g, w_in=m_w_in, ssm_lam_re=m_ssm_lam_re, ssm_lam_im=m_ssm_lam_im,
             ssm_log_step=m_ssm_log_step, ssm_b_re=m_ssm_b_re, ssm_b_im=m_ssm_b_im, ssm_c_re=m_ssm_c_re,
             ssm_c_im=m_ssm_c_im, ssm_d=m_ssm_d, ssm_w_glu=m_ssm_w_glu, ssm_b_glu=m_ssm_b_glu, attn_sinks=m_attn_sinks,
             w_out=m_w_out, post_norm_g=m_post_norm_g, pl_w_proj=m_pl_w_proj, pl_w_gate=m_pl_w_gate,
             pl_b_gate=m_pl_b_gate)
    v = dict(pre_norm_g=v_pre_norm_g, w_in=v_w_in, ssm_lam_re=v_ssm_lam_re, ssm_lam_im=v_ssm_lam_im,
             ssm_log_step=v_ssm_log_step, ssm_b_re=v_ssm_b_re, ssm_b_im=v_ssm_b_im, ssm_c_re=v_ssm_c_re,
             ssm_c_im=v_ssm_c_im, ssm_d=v_ssm_d, ssm_w_glu=v_ssm_w_glu, ssm_b_glu=v_ssm_b_glu, attn_sinks=v_attn_sinks,
             w_out=v_w_out, post_norm_g=v_post_norm_g, pl_w_proj=v_pl_w_proj, pl_w_gate=v_pl_w_gate,
             pl_b_gate=v_pl_b_gate)
    kf = lambda d: {n: _kernel_form(n, a) for n, a in d.items()}
    wk, mk, vk = kf(w), kf(m), kf(v)

    (gathered,), prep = _allgather_weights([wk["w_in"]], *_ssm_prep(
        wk["ssm_lam_re"], wk["ssm_lam_im"], wk["ssm_log_step"].reshape(SSM_G, 1), wk["ssm_b_re"], wk["ssm_b_im"],
        wk["ssm_c_re"], wk["ssm_c_im"], x.shape[1] // N_SEG))
    loss, grad_x, grads, g_late = _local_step(
        x, p[0], loss_target, wk["pre_norm_g"], gathered.reshape(D_IN, D_MODEL), prep, wk["ssm_lam_re"],
        wk["ssm_lam_im"], wk["ssm_log_step"], wk["ssm_b_re"], wk["ssm_b_im"], wk["ssm_d"],
        wk["ssm_b_glu"], wk["attn_sinks"], wk["post_norm_g"], wk["pl_b_gate"], [wk[n] for n in LATE_NAMES])

    owned = lambda g: g.reshape(N_DEV, D_IN // N_DEV, D_MODEL)
    tiny_form = lambda d: [d[n].reshape(rows, cols) for n, rows, cols in TINY]
    med_form = lambda d: [d[n].reshape(N_DEV, rows // N_DEV, cols) for n, rows, cols in MEDIUM]
    g_big, loss, g_tiny, g_med = _reduce_final(
        [owned(grads["w_in16"])], [owned(grads["w_in"])], loss, tiny_form(grads), med_form(grads))
    names = BIG_NAMES + tuple(n for n, _, _ in TINY + MEDIUM)
    form = lambda d: [d[n] for n in BIG_NAMES] + tiny_form(d) + med_form(d)
    updated = _adamw_update(g_big + g_late + g_tiny + g_med, form(wk), form(mk), form(vk), len(BIG_NAMES))
    vals = dict(zip(names, updated))
    results = [[_given_form(n, vals[n][kind], w[n].shape) for n in WEIGHT_NAMES] for kind in range(4)]
    return (loss.reshape(()), grad_x, *results[0], *results[1], *results[2], *results[3])
```

```python
import functools
import math

import jax
import jax.numpy as jnp
from jax import lax
from jax.experimental import pallas as pl
from jax.experimental.pallas import tpu as pltpu

F32 = jnp.float32
BF16 = jnp.bfloat16

D_MODEL = 1024
D_SSM = 512
D_ATTN = 512
SSM_P = 16
SSM_G = 32
SSM_N = 64
N_HEADS = 8
KV_HEADS = 2
Q_PER_KV = 4
HEAD_DIM = 64
ATT_BLOCK = 128
D_PLE = 256
D_IN = 2304
EPS = 1e-6
N_DEV = 8
N_SEG = 8
G_TILE = 8
N_GT = SSM_G // G_TILE
CH_T = G_TILE * SSM_P
ST_T = G_TILE * SSM_N
N_STATE = SSM_G * SSM_N
SCAN_UNROLL = 4
TOKEN_TILE = 256
TOKEN_TILE_WIDE = 512
LANES = 128
VMEM_LIMIT = 60 * 1024 * 1024

ADAM_LR = 0.001
ADAM_B1 = 0.9
ADAM_B2 = 0.999
ADAM_EPS = 1e-08
ADAM_WD = 0.01
ADAM_STEP = 10

GELU_C = math.sqrt(2.0 / math.pi)
GELU_K = 0.044715
ATT_SCALE = 1.0 / math.sqrt(HEAD_DIM)
NEG_INF = float("-inf")


def _mm(a, b):
    return jnp.dot(a.astype(BF16), b.astype(BF16), preferred_element_type=F32)


def _mm_nt(a, b):
    return lax.dot_general(a.astype(BF16), b.astype(BF16), (((1,), (1,)), ((), ())), preferred_element_type=F32)


def _mm_tn(a, b):
    return lax.dot_general(a.astype(BF16), b.astype(BF16), (((0,), (0,)), ((), ())), preferred_element_type=F32)


def _sigmoid(x):
    return 1.0 / (1.0 + jnp.exp(-x))


def _tc_params(sem):
    return pltpu.CompilerParams(dimension_semantics=sem, vmem_limit_bytes=VMEM_LIMIT)


def _const_spec(shape):
    nd = len(shape)
    return pl.BlockSpec(shape, lambda *_: (0,) * nd)


def _mesh_pos():
    return lax.axis_index("x"), lax.axis_index("y"), lax.axis_index("c")


ROW_CHUNKS = (64, 32, 16)


def _row_chunk(nrows):
    return next((c for c in ROW_CHUNKS if nrows % c == 0), None)


def _row_chunks(nrows, fn, chunk=None, init=None):
    chunk = chunk or _row_chunk(nrows)

    def step(i, carry):
        rows = pl.ds(pl.multiple_of(i * chunk, chunk), chunk)
        if init is None:
            fn(rows)
            return carry
        return fn(rows, carry)

    return lax.fori_loop(0, nrows // chunk, step, 0 if init is None else init)


def _slot(px, py, pc):
    return 4 * px + 2 * py + pc


def _allgather_weights(shards, work=None, work_inputs=(), work_out_shapes=()):
    n, n_wi, n_wo = len(shards), len(work_inputs), len(work_out_shapes)

    def body(*refs):
        srcs, w_in_refs = refs[:n], refs[n:n + n_wi]
        outs, w_out_refs = refs[n + n_wi:2 * n + n_wi], refs[2 * n + n_wi:2 * n + n_wi + n_wo]
        send_sems, recv_sems = refs[2 * n + n_wi + n_wo:]
        x, y, c = _mesh_pos()
        me, sibling = (x, y, c), (x, y, 1 - c)
        chips = [(1 - x, y), (x, 1 - y), (1 - x, 1 - y)]

        def copy(a, k, block, to):
            blk = outs[a].at[_slot(*block)]
            return pltpu.make_async_remote_copy(
                src_ref=blk, dst_ref=blk, send_sem=send_sems.at[7 * a + k], recv_sem=recv_sems.at[7 * a + k],
                device_id=to, device_id_type=pl.DeviceIdType.MESH)

        sends = []
        for a in range(n):
            mine = outs[a].at[_slot(*me)]

            def cast(r, mine=mine, src=srcs[a]):
                mine[r, :] = src[r, :].astype(BF16)

            _row_chunks(srcs[a].shape[0], cast)
            first = [copy(a, 0, me, sibling)] + [copy(a, 1 + j, me, (*chip, c)) for j, chip in enumerate(chips)]
            for cp in first:
                cp.start()
            sends += first
        if work is not None:
            work(w_in_refs, w_out_refs)
        for a in range(n):
            for j, chip in enumerate(chips):
                copy(a, 1 + j, (*chip, c), me).wait_recv()
                fwd = copy(a, 4 + j, (*chip, c), sibling)
                fwd.start()
                sends.append(fwd)
        for a in range(n):
            copy(a, 0, sibling, me).wait_recv()
            for j, chip in enumerate(chips):
                copy(a, 4 + j, (*chip, 1 - c), me).wait_recv()
        for cp in sends:
            cp.wait_send()

    vm = pl.BlockSpec(memory_space=pltpu.VMEM)
    res = pl.pallas_call(
        body, name="allgather_weights",
        out_shape=tuple(jax.ShapeDtypeStruct((N_DEV,) + s.shape, BF16) for s in shards) + tuple(work_out_shapes),
        in_specs=[vm] * (n + n_wi), out_specs=(vm,) * (n + n_wo),
        scratch_shapes=[pltpu.SemaphoreType.DMA((7 * n,)), pltpu.SemaphoreType.DMA((7 * n,))],
        compiler_params=pltpu.CompilerParams(vmem_limit_bytes=VMEM_LIMIT),
    )(*shards, *work_inputs)
    return list(res[:n]), list(res[n:])


def _adamw(w, g, m, v):
    m = ADAM_B1 * m + (1.0 - ADAM_B1) * g
    v = ADAM_B2 * v + (1.0 - ADAM_B2) * (g * g)
    m_hat = m / (1.0 - ADAM_B1 ** ADAM_STEP)
    v_hat = v / (1.0 - ADAM_B2 ** ADAM_STEP)
    delta = -ADAM_LR * (m_hat / (jnp.sqrt(v_hat) + ADAM_EPS) + ADAM_WD * w)
    return delta, m, v


def _remote(src, dst, send_sems, recv_sems, k, to):
    return pltpu.make_async_remote_copy(src_ref=src, dst_ref=dst, send_sem=send_sems.at[k], recv_sem=recv_sems.at[k],
                                        device_id=to, device_id_type=pl.DeviceIdType.MESH)


def _big_reduce_phases(g16_r, go_r, outs, send2, recv1, recv2, s_send, s_recv):
    n = len(g16_r)
    x, y, c = _mesh_pos()
    sibling = (x, y, 1 - c)
    chips = [(1 - x, y), (x, 1 - y), (1 - x, 1 - y)]
    all_chips = [(x, y)] + chips
    lvl1 = []
    for a in range(n):
        cps = [_remote(g16_r[a].at[_slot(*chip, 1 - c)], recv1[a].at[j], s_send, s_recv, 7 * a + j, sibling)
               for j, chip in enumerate(all_chips)]
        for cp in cps:
            cp.start()
        lvl1.append(cps)
    yield
    lvl2 = []
    for a in range(n):
        for cp in lvl1[a]:
            cp.wait_recv()
        og = outs[a]

        def partials(r, a=a, og=og):
            og[r, :] = go_r[a][r, :] + recv1[a][0, r, :].astype(F32)
            for j, chip in enumerate(chips):
                mine16 = g16_r[a][_slot(*chip, c), r, :].astype(F32)
                send2[a][j, r, :] = (mine16 + recv1[a][1 + j, r, :].astype(F32)).astype(BF16)

        _row_chunks(go_r[a].shape[0], partials)
        cps = [_remote(send2[a].at[j], recv2[a].at[j], s_send, s_recv, 7 * a + 4 + j, (*chip, c))
               for j, chip in enumerate(chips)]
        for cp in cps:
            cp.start()
        lvl2.append(cps)
    yield
    for a in range(n):
        for cp in lvl2[a]:
            cp.wait_recv()
        og = outs[a]

        def total(r, a=a, og=og):
            g = og[r, :]
            for j in range(3):
                g = g + recv2[a][j, r, :].astype(F32)
            og[r, :] = g

        _row_chunks(go_r[a].shape[0], total)
    yield
    for cps in lvl1 + lvl2:
        for cp in cps:
            cp.wait_send()


def _adamw_update(g, w, m, v, n_streamed):
    n = len(g)
    ns = n_streamed

    def body(*refs):
        g_r, w_r, m_r, v_r = (refs[i * n:(i + 1) * n] for i in range(4))
        outs = refs[4 * n:8 * n]
        in_buf, out_buf = refs[8 * n:8 * n + 4 * ns], refs[8 * n + 4 * ns:8 * n + 8 * ns]
        in_sems, out_sems = refs[8 * n + 8 * ns:]
        loads = [[pltpu.make_async_copy(src[a], in_buf[4 * a + k], in_sems.at[4 * a + k])
                  for k, src in enumerate((g_r, w_r, m_r, v_r))] for a in range(ns)]
        stores = [[pltpu.make_async_copy(out_buf[4 * a + k], outs[4 * a + k], out_sems.at[4 * a + k]) for k in range(4)]
                  for a in range(ns)]
        for cps in loads:
            for cp in cps:
                cp.start()
        for a in range(n):
            if a < ns:
                for cp in loads[a]:
                    cp.wait()
                gs, ws, ms, vs = in_buf[4 * a:4 * a + 4]
                og, od, om, ov = out_buf[4 * a:4 * a + 4]
            else:
                gs, ws, ms, vs = g_r[a], w_r[a], m_r[a], v_r[a]
                og, od, om, ov = outs[4 * a:4 * a + 4]

            def update(idx, gs=gs, ws=ws, ms=ms, vs=vs, og=og, od=od, om=om, ov=ov):
                gv = gs[idx]
                d, nm, nv = _adamw(ws[idx], gv, ms[idx], vs[idx])
                og[idx] = gv
                od[idx] = d
                om[idx] = nm
                ov[idx] = nv

            shape = gs.shape
            if len(shape) == 3:
                for b in range(shape[0]):
                    update(b)
            elif _row_chunk(shape[0]) is not None:
                _row_chunks(shape[0], update)
            else:
                update(Ellipsis)
            if a < ns:
                for cp in stores[a]:
                    cp.start()
        for cps in stores:
            for cp in cps:
                cp.wait()

    vm, hbm = pl.BlockSpec(memory_space=pltpu.VMEM), pl.BlockSpec(memory_space=pl.ANY)
    place = lambda: [hbm] * ns + [vm] * (n - ns)
    buf = [pltpu.VMEM(t.shape, F32) for t in g[:ns] for _ in range(4)]
    res = pl.pallas_call(
        body, name="adamw_update",
        out_shape=tuple(jax.ShapeDtypeStruct(t.shape, F32) for t in g for _ in range(4)),
        in_specs=place() * 4, out_specs=tuple(s for a in range(n) for s in [hbm if a < ns else vm] * 4),
        scratch_shapes=buf + buf + [pltpu.SemaphoreType.DMA((4 * ns,)), pltpu.SemaphoreType.DMA((4 * ns,))],
        compiler_params=pltpu.CompilerParams(vmem_limit_bytes=VMEM_LIMIT),
    )(*g, *w, *m, *v)
    return [res[4 * a:4 * a + 4] for a in range(n)]


TINY = (("pre_norm_g", 1, 1024), ("post_norm_g", 1, 1024), ("pl_b_gate", 1, 1024), ("ssm_d", 1, 512),
        ("ssm_b_glu", 1, 512), ("ssm_log_step", 1, 32), ("attn_sinks", 1, 8), ("ssm_lam_re", 32, 64),
        ("ssm_lam_im", 32, 64))
MEDIUM = (("ssm_b_re", SSM_G * SSM_P, SSM_N), ("ssm_b_im", SSM_G * SSM_P, SSM_N), ("ssm_c_re", SSM_G * SSM_P, SSM_N),
          ("ssm_c_im", SSM_G * SSM_P, SSM_N))


def _stage_rows():
    offs, r = {}, 0
    for name, rows, cols in TINY + (("loss", 1, 1),):
        if rows > 1:
            r = -(-r // 8) * 8
        offs[name] = r
        r += rows if rows > 1 else max(cols // LANES, 1)
    return offs, -(-r // 8) * 8


def _reduce_final(g16, g32, loss, g_tiny, g_med):
    nb_, nt, nm_ = len(g16), len(TINY), len(MEDIUM)
    offs, stage_rows = _stage_rows()

    def body(*refs):
        g16_r, go_r = refs[:nb_], refs[nb_:2 * nb_]
        base = 2 * nb_
        loss_r, gt, gm = refs[base], refs[base + 1:base + 1 + nt], refs[base + 1 + nt:base + 1 + nt + nm_]
        base += 1 + nt + nm_
        out_b = refs[base:base + nb_]
        base += nb_
        loss_o, out_t, out_m = refs[base], refs[base + 1:base + 1 + nt], refs[base + 1 + nt:base + 1 + nt + nm_]
        base += 1 + nt + nm_
        send2_b, recv1_b, recv2_b = (refs[base + i * nb_:base + (i + 1) * nb_] for i in range(3))
        base += 3 * nb_
        stage = refs[base]
        recv1, part, recv2 = (refs[base + 1 + i * nm_:base + 1 + (i + 1) * nm_] for i in range(3))
        bs_send, bs_recv, s_send, s_recv, own_sems = refs[base + 1 + 3 * nm_:base + 6 + 3 * nm_]
        own32 = refs[base + 6 + 3 * nm_:]
        me = _slot(*_mesh_pos())
        fetch = [pltpu.make_async_copy(go_r[a].at[me], own32[a], own_sems.at[a]) for a in range(nb_)]
        for cp in fetch:
            cp.start()
        big = _big_reduce_phases(g16_r, own32, out_b, send2_b, recv1_b, recv2_b, bs_send, bs_recv)
        small = small_phases(loss_r, gt, gm, loss_o, out_t, out_m, stage, recv1, part, recv2, s_send, s_recv)
        next(big)
        next(small)
        for cp in fetch:
            cp.wait()
        next(big)
        for _ in small:
            pass
        for _ in big:
            pass

    def small_phases(loss_r, gt, gm, loss_o, out_t, out_m, stage, recv1, part, recv2, s_send, s_recv):
        x, y, c = _mesh_pos()
        me = _slot(x, y, c)
        sibling = (x, y, 1 - c)
        chips = [(1 - x, y), (x, 1 - y), (1 - x, 1 - y)]
        all_chips = [(x, y)] + chips
        peers = [sibling] + [(*chip, c) for chip in chips] + [(*chip, 1 - c) for chip in chips]
        sem = iter(range(7 + 14 * nm_))
        lvl1 = []
        for a in range(nm_):
            cps = [_remote(gm[a].at[_slot(*chip, 1 - c)], recv1[a].at[j], s_send, s_recv, next(sem), sibling)
                   for j, chip in enumerate(all_chips)]
            for cp in cps:
                cp.start()
            lvl1.append(cps)
        mine = stage.at[me]
        mine[...] = jnp.zeros((stage_rows, LANES), F32)
        for (name, rows, cols), ref in zip(TINY + (("loss", 1, 1),), gt + (loss_r,)):
            r0 = offs[name]
            if rows > 1:
                mine[r0:r0 + rows, 0:cols] = ref[...]
            elif cols >= LANES:
                for i in range(cols // LANES):
                    mine[r0 + i:r0 + i + 1, :] = ref[:, i * LANES:(i + 1) * LANES]
            else:
                mine[r0:r0 + 1, 0:cols] = ref[...]
        tiny_cps = [_remote(mine, mine, s_send, s_recv, next(sem), peer) for peer in peers]
        for cp in tiny_cps:
            cp.start()
        yield
        lvl2 = []
        for a in range(nm_):
            for cp in lvl1[a]:
                cp.wait_recv()
            for j, chip in enumerate(all_chips):
                part[a][j] = gm[a][_slot(*chip, c)] + recv1[a][j]
            cps = [_remote(part[a].at[1 + j], recv2[a].at[j], s_send, s_recv, next(sem), (*chip, c))
                   for j, chip in enumerate(chips)]
            for cp in cps:
                cp.start()
            lvl2.append(cps)
        yield
        lvl3 = []
        for a in range(nm_):
            for cp in lvl2[a]:
                cp.wait_recv()
            blk = out_m[a].at[me]
            blk[...] = ((part[a][0] + recv2[a][0]) + recv2[a][1]) + recv2[a][2]
            cps = [_remote(blk, blk, s_send, s_recv, next(sem), peer) for peer in peers]
            for cp in cps:
                cp.start()
            lvl3.append(cps)
        yield
        for cp in tiny_cps:
            cp.wait_recv()
        tot = stage[0]
        for d in range(1, N_DEV):
            tot = tot + stage[d]
        loss_o[...] = tot[offs["loss"]:offs["loss"] + 1, 0:1]
        for k, (name, rows, cols) in enumerate(TINY):
            r0 = offs[name]
            if rows > 1:
                out_t[k][...] = tot[r0:r0 + rows, 0:cols]
            elif cols >= LANES:
                for i in range(cols // LANES):
                    out_t[k][:, i * LANES:(i + 1) * LANES] = tot[r0 + i:r0 + i + 1, :]
            else:
                out_t[k][...] = tot[r0:r0 + 1, 0:cols]
        for cps in lvl3:
            for cp in cps:
                cp.wait_recv()
        for cps in lvl1 + lvl2 + lvl3 + [tiny_cps]:
            for cp in cps:
                cp.wait_send()

    vmem = pl.BlockSpec(memory_space=pltpu.VMEM)
    t_shapes = [jax.ShapeDtypeStruct((rows, cols), F32) for _, rows, cols in TINY]
    m_shapes = [jax.ShapeDtypeStruct((N_DEV, rows // N_DEV, cols), F32) for _, rows, cols in MEDIUM]
    blk = [(rows // N_DEV, cols) for _, rows, cols in MEDIUM]
    shard = [g.shape[1:] for g in g16]
    scratch = ([pltpu.VMEM((3,) + s, BF16) for s in shard] + [pltpu.VMEM((4,) + s, BF16) for s in shard]
               + [pltpu.VMEM((3,) + s, BF16) for s in shard]
               + [pltpu.VMEM((N_DEV, stage_rows, LANES), F32)]
               + [pltpu.VMEM((4,) + b, F32) for b in blk] + [pltpu.VMEM((4,) + b, F32) for b in blk]
               + [pltpu.VMEM((3,) + b, F32) for b in blk]
               + [pltpu.SemaphoreType.DMA((7 * nb_,)), pltpu.SemaphoreType.DMA((7 * nb_,)),
                  pltpu.SemaphoreType.DMA((7 + 14 * nm_,)), pltpu.SemaphoreType.DMA((7 + 14 * nm_,)),
                  pltpu.SemaphoreType.DMA((nb_,))]
               + [pltpu.VMEM(s, F32) for s in shard])
    n_out = nb_ + 1 + nt + nm_
    res = pl.pallas_call(
        body, name="reduce_final",
        out_shape=tuple(jax.ShapeDtypeStruct(s, F32) for s in shard) + (jax.ShapeDtypeStruct((1, 1), F32),)
        + tuple(t_shapes) + tuple(m_shapes),
        in_specs=[vmem] * nb_ + [pl.BlockSpec(memory_space=pl.ANY)] * nb_ + [vmem] * (1 + nt + nm_),
        out_specs=(vmem,) * n_out, scratch_shapes=scratch,
        compiler_params=pltpu.CompilerParams(vmem_limit_bytes=VMEM_LIMIT),
    )(*g16, *g32, loss, *g_tiny, *g_med)
    return list(res[:nb_]), res[nb_], list(res[nb_ + 1:nb_ + 1 + nt]), list(res[nb_ + 1 + nt:])


def _gather_phases(shard_r, gath, cast, send_sems, recv_sems, local_sems):
    n = len(shard_r)
    x, y, c = _mesh_pos()
    me, sibling = (x, y, c), (x, y, 1 - c)
    chips = [(1 - x, y), (x, 1 - y), (1 - x, 1 - y)]

    def own(a, k, to):
        return _remote(cast[a], gath[a].at[_slot(*me)], send_sems, recv_sems, 7 * a + k, to)

    def passed(a, k, block, to):
        blk = gath[a].at[_slot(*block)]
        return _remote(blk, blk, send_sems, recv_sems, 7 * a + k, to)

    def keep(a):
        return pltpu.make_async_copy(cast[a], gath[a].at[_slot(*me)], local_sems.at[a])

    def start():
        for a in range(n):
            def to16(r, a=a):
                cast[a][r, :] = shard_r[a][r, :].astype(BF16)

            _row_chunks(shard_r[a].shape[0], to16)
            keep(a).start()
            own(a, 0, sibling).start()
            for j, chip in enumerate(chips):
                own(a, 1 + j, (*chip, c)).start()

    def relay():
        for a in range(n):
            for j, chip in enumerate(chips):
                passed(a, 1 + j, (*chip, c), me).wait_recv()
                passed(a, 4 + j, (*chip, c), sibling).start()

    def finish():
        for a in range(n):
            passed(a, 0, sibling, me).wait_recv()
            for j, chip in enumerate(chips):
                passed(a, 4 + j, (*chip, 1 - c), me).wait_recv()
            own(a, 0, sibling).wait_send()
            for j, chip in enumerate(chips):
                own(a, 1 + j, (*chip, c)).wait_send()
                passed(a, 4 + j, (*chip, c), sibling).wait_send()
            keep(a).wait()

    return start, relay, finish


def _gather_operands(shards):
    n = len(shards)
    return ((pl.BlockSpec(memory_space=pl.ANY),) * n,
            tuple(jax.ShapeDtypeStruct((N_DEV,) + s.shape, BF16) for s in shards),
            [pltpu.VMEM(s.shape, BF16) for s in shards]
            + [pltpu.SemaphoreType.DMA((7 * n,)), pltpu.SemaphoreType.DMA((7 * n,)), pltpu.SemaphoreType.DMA((n,))])


def _hosted_reduce_phases(g16_r, g32_r, red, own16, recv1, send2, recv2, own32, s_send, s_recv, s_local):
    n = len(g16_r)
    x, y, c = _mesh_pos()
    sibling = (x, y, 1 - c)
    chips = [(1 - x, y), (x, 1 - y), (1 - x, 1 - y)]
    all_chips = [(x, y)] + chips

    def lvl1(a, j):
        return _remote(g16_r[a].at[_slot(*all_chips[j], 1 - c)], recv1[a].at[j], s_send, s_recv, 7 * a + j, sibling)

    def lvl2(a, j):
        return _remote(send2[a].at[j], recv2[a].at[j], s_send, s_recv, 7 * a + 4 + j, (*chips[j], c))

    def mine(a, j):
        if j == 3:
            return pltpu.make_async_copy(g32_r[a].at[_slot(x, y, c)], own32[a], s_local.at[4 * a + j])
        return pltpu.make_async_copy(g16_r[a].at[_slot(*chips[j], c)], own16[a].at[j], s_local.at[4 * a + j])

    def start():
        for a in range(n):
            for j in range(4):
                mine(a, j).start()
            for j in range(4):
                lvl1(a, j).start()

    def middle():
        for a in range(n):
            for j in range(4):
                mine(a, j).wait()
            for j in range(4):
                lvl1(a, j).wait_recv()

            def partials(r, a=a):
                red[a][r, :] = own32[a][r, :] + recv1[a][0, r, :].astype(F32)
                for j in range(3):
                    send2[a][j, r, :] = (own16[a][j, r, :].astype(F32) + recv1[a][1 + j, r, :].astype(F32)).astype(BF16)

            _row_chunks(own32[a].shape[0], partials)
            for j in range(3):
                lvl2(a, j).start()

    def total():
        for a in range(n):
            for j in range(3):
                lvl2(a, j).wait_recv()

            def add(r, a=a):
                g = red[a][r, :]
                for j in range(3):
                    g = g + recv2[a][j, r, :].astype(F32)
                red[a][r, :] = g

            _row_chunks(own32[a].shape[0], add)

    def finish():
        for a in range(n):
            for j in range(4):
                lvl1(a, j).wait_send()
            for j in range(3):
                lvl2(a, j).wait_send()

    return start, middle, total, finish


def _hosted_reduce_operands(g16, const_spec):
    n = len(g16)
    shard = [g.shape[1:] for g in g16]
    return ([pl.BlockSpec(memory_space=pl.ANY)] * (2 * n),
            tuple(const_spec(s) for s in shard),
            tuple(jax.ShapeDtypeStruct(s, F32) for s in shard),
            [pltpu.VMEM((3,) + s, BF16) for s in shard] + [pltpu.VMEM((4,) + s, BF16) for s in shard]
            + [pltpu.VMEM((3,) + s, BF16) for s in shard] + [pltpu.VMEM((3,) + s, BF16) for s in shard]
            + [pltpu.VMEM(s, F32) for s in shard]
            + [pltpu.SemaphoreType.DMA((7 * n,)), pltpu.SemaphoreType.DMA((7 * n,)), pltpu.SemaphoreType.DMA((4 * n,))])


def _in_proj(x2, g_pre, w_in, tm):
    t = x2.shape[0]

    def body(x_ref, g_ref, w_ref, u_ref, zs_ref, q_ref, k_ref, v_ref, za_ref):
        xv = x_ref[...]
        r = lax.rsqrt(jnp.mean(xv * xv, axis=-1, keepdims=True) + EPS)
        hn = xv * r * g_ref[...]
        proj = _mm_nt(hn, w_ref[...])
        u_ref[...] = proj[:, 0:512]
        zs_ref[...] = proj[:, 512:1024]
        q_ref[...] = proj[:, 1024:1536].astype(BF16)
        k_ref[...] = proj[:, 1536:1664].astype(BF16)
        v_ref[...] = proj[:, 1664:1792].astype(BF16)
        za_ref[...] = proj[:, 1792:2304]

    row = lambda w: pl.BlockSpec((tm, w), lambda i: (i, 0))
    return pl.pallas_call(
        body, name="in_proj", grid=(t // tm,),
        in_specs=[row(D_MODEL), _const_spec((1, D_MODEL)), _const_spec((D_IN, D_MODEL))],
        out_specs=(row(512), row(512), row(512), row(128), row(128), row(512)),
        out_shape=(jax.ShapeDtypeStruct((t, 512), F32),
                   jax.ShapeDtypeStruct((t, 512), F32), jax.ShapeDtypeStruct((t, 512), BF16),
                   jax.ShapeDtypeStruct((t, 128), BF16), jax.ShapeDtypeStruct((t, 128), BF16),
                   jax.ShapeDtypeStruct((t, 512), F32)),
        compiler_params=_tc_params(("arbitrary",)),
    )(x2, g_pre, w_in)


def _discretise(lr, li, ls):
    step = jnp.exp(ls)
    mag = jnp.exp(lr * step)
    ar = mag * jnp.cos(li * step)
    ai = mag * jnp.sin(li * step)
    den = lr * lr + li * li
    cr = ((ar - 1.0) * lr + ai * li) / den
    ci = (ai * lr - (ar - 1.0) * li) / den
    return step, ar, ai, den, cr, ci


def _per_channel(v):
    return jnp.broadcast_to(v[:, None, :], (SSM_G, SSM_P, SSM_N)).reshape(SSM_G * SSM_P, SSM_N)


def _tile_masks():
    r = lax.broadcasted_iota(jnp.int32, (CH_T, ST_T), 0) // SSM_P
    l = lax.broadcasted_iota(jnp.int32, (CH_T, ST_T), 1) // SSM_N
    lt = lax.broadcasted_iota(jnp.int32, (ST_T, CH_T), 0) // SSM_N
    rt = lax.broadcasted_iota(jnp.int32, (ST_T, CH_T), 1) // SSM_P
    rep = lax.broadcasted_iota(jnp.int32, (SSM_N, ST_T), 0) == lax.broadcasted_iota(jnp.int32, (SSM_N, ST_T), 1) % SSM_N
    rep_t = lax.broadcasted_iota(jnp.int32, (ST_T, SSM_N), 0) % SSM_N == lax.broadcasted_iota(jnp.int32, (ST_T, SSM_N), 1)
    return r == l, lt == rt, rep, rep_t


def _ssm_prep(lam_re, lam_im, log_step, b_re, b_im, c_re, c_im, seg):
    def work(in_refs, out_refs):
        lr_ref, li_ref, ls_ref, br_ref, bi_ref, cre_ref, cim_ref, lrr_ref, lir_ref, lsr_ref = in_refs
        ar_ref, ai_ref, pr_ref, pi_ref, bcat_ref, bcat_t_ref, ccat_ref, ccat_t_ref = out_refs
        _, _, _, _, cr, ci = _discretise(lr_ref[...], li_ref[...], ls_ref[...])
        cr, ci = _per_channel(cr), _per_channel(ci)
        br, bi = br_ref[...], bi_ref[...]
        bb_re = cr * br - ci * bi
        bb_im = cr * bi + ci * br
        same, same_t, rep, rep_t = _tile_masks()
        rep, rep_t = rep.astype(BF16), rep_t.astype(BF16)
        for j in range(N_GT):
            rows = slice(j * CH_T, (j + 1) * CH_T)
            for wide, tall, parts in ((bcat_ref, bcat_t_ref, (bb_re[rows], bb_im[rows])),
                                      (ccat_t_ref, ccat_ref, (cre_ref[rows, :], -cim_ref[rows, :]))):
                for k, part in enumerate(parts):
                    p16 = part.astype(BF16)
                    wide[j, :, k * ST_T:(k + 1) * ST_T] = jnp.where(same, _mm(p16, rep), 0.0).astype(BF16)
                    tall[j, k * ST_T:(k + 1) * ST_T, :] = jnp.where(same_t, _mm_nt(rep_t, p16), 0.0).astype(BF16)
        stepr = jnp.exp(lsr_ref[...])
        mag = jnp.exp(lrr_ref[...] * stepr)
        a_r, a_i = mag * jnp.cos(lir_ref[...] * stepr), mag * jnp.sin(lir_ref[...] * stepr)
        p_r, p_i = a_r, a_i
        for k in range(8):
            pr_ref[k:k + 1, :] = p_r
            pi_ref[k:k + 1, :] = p_i
            p_r, p_i = p_r * a_r - p_i * a_i, p_r * a_i + p_i * a_r
        n = 8
        while n < seg:
            tr, ti = pr_ref[n - 1:n, :], pi_ref[n - 1:n, :]
            xr, xi = pr_ref[0:n, :], pi_ref[0:n, :]
            pr_ref[n:2 * n, :] = xr * tr - xi * ti
            pi_ref[n:2 * n, :] = xr * ti + xi * tr
            n *= 2
        ar_ref[...] = pr_ref[0:1, :]
        ai_ref[...] = pi_ref[0:1, :]

    row = jax.ShapeDtypeStruct((1, N_STATE), F32)
    pw = jax.ShapeDtypeStruct((seg, N_STATE), F32)
    wide = jax.ShapeDtypeStruct((N_GT, CH_T, 2 * ST_T), BF16)
    tall = jax.ShapeDtypeStruct((N_GT, 2 * ST_T, CH_T), BF16)
    step_row = jnp.broadcast_to(log_step, (SSM_G, SSM_N)).reshape(1, N_STATE)
    inputs = (lam_re, lam_im, log_step, b_re, b_im, c_re, c_im, lam_re.reshape(1, N_STATE),
              lam_im.reshape(1, N_STATE), step_row)
    return work, inputs, (row, row, pw, pw, wide, tall, tall, wide)


def _seg_rows(t):
    if isinstance(t, int):
        return pl.ds(t * N_SEG, N_SEG)
    return pl.ds(pl.multiple_of(t * N_SEG, N_SEG), N_SEG)


def _scan_forward(xs, a_re, a_im, pw_re, pw_im, cs, seg):
    are = jnp.broadcast_to(a_re, (N_SEG, ST_T))
    aim = jnp.broadcast_to(a_im, (N_SEG, ST_T))

    def steps(k, carry):
        xr, xi = carry
        for j in range(SCAN_UNROLL):
            r = pl.multiple_of((k * SCAN_UNROLL + j) * N_SEG, N_SEG)
            nr = are * xr - aim * xi + xs[pl.ds(r, N_SEG), 0:ST_T]
            ni = are * xi + aim * xr + xs[pl.ds(r, N_SEG), ST_T:2 * ST_T]
            xs[pl.ds(r, N_SEG), 0:ST_T] = nr
            xs[pl.ds(r, N_SEG), ST_T:2 * ST_T] = ni
            xr, xi = nr, ni
        return xr, xi

    zero = jnp.zeros((N_SEG, ST_T), F32)
    fr, fi = lax.fori_loop(0, seg // SCAN_UNROLL, steps, (zero, zero))
    sr, si = pw_re[seg - 1:seg, :], pw_im[seg - 1:seg, :]
    cr = jnp.zeros((1, ST_T), F32)
    ci = jnp.zeros((1, ST_T), F32)
    cs[0:1, :] = cr
    cs[8:9, :] = ci
    for s in range(1, N_SEG):
        ncr = sr * cr - si * ci + fr[s - 1:s, :]
        nci = sr * ci + si * cr + fi[s - 1:s, :]
        cr, ci = ncr, nci
        cs[s:s + 1, :] = cr
        cs[8 + s:9 + s, :] = ci
    car, cai = cs[0:8, :], cs[8:16, :]

    def fix(t, _):
        r = pl.multiple_of(t * N_SEG, N_SEG)
        pr, pi = pw_re[pl.ds(t, 1), :], pw_im[pl.ds(t, 1), :]
        xs[pl.ds(r, N_SEG), 0:ST_T] = xs[pl.ds(r, N_SEG), 0:ST_T] + (pr * car - pi * cai)
        xs[pl.ds(r, N_SEG), ST_T:2 * ST_T] = xs[pl.ds(r, N_SEG), ST_T:2 * ST_T] + (pr * cai + pi * car)
        return 0

    lax.fori_loop(0, seg, fix, 0, unroll=SCAN_UNROLL)


def _interleave(src, dst, seg):
    for s in range(N_SEG):
        dst[pl.ds(s, seg, stride=N_SEG), :] = src[s]


def _deinterleave(src, seg, s):
    return src[pl.ds(s, seg, stride=N_SEG), :]


def _ssm_forward(u, bcat, ccat, a_re, a_im, pw_re, pw_im, d_row, late, seg):
    bl = u.shape[0]
    rows = N_SEG * seg
    n = len(late)
    steps = bl * N_GT

    def body(*refs):
        u_ref, b_ref, c_ref, ar_ref, ai_ref, pr_ref, pi_ref, d_ref = refs[:8]
        late_r = refs[8:8 + n]
        y_ref, xs_ref, cs_ref = refs[8 + n:11 + n]
        gath, cast = refs[11 + n:11 + 2 * n], refs[11 + 2 * n:11 + 3 * n]
        send_sems, recv_sems, local_sems, ui, yi = refs[11 + 3 * n:]
        step = pl.program_id(0) * N_GT + pl.program_id(1)
        start, relay, finish = _gather_phases(late_r, gath, cast, send_sems, recv_sems, local_sems)
        pl.when(step == 0)(start)
        _interleave(u_ref.at[0], ui, seg)
        u = ui[...]
        xs, cs = xs_ref.at[0, 0], cs_ref.at[0, 0]
        xs[...] = _mm(u, b_ref[0])
        _scan_forward(xs, ar_ref[...], ai_ref[...], pr_ref, pi_ref, cs, seg)
        yi[...] = _mm(xs[...], c_ref[0]) + d_ref[...] * u
        for s in range(N_SEG):
            y_ref[0, s] = _deinterleave(yi, seg, s)
        pl.when(step == steps // 2)(relay)
        pl.when(step == steps - 1)(finish)

    state = lambda r, c: pl.BlockSpec((1, 1, r, c), lambda b, j: (b, j, 0, 0))
    act = pl.BlockSpec((1, N_SEG, seg, CH_T), lambda b, j: (b, 0, 0, j))
    g_specs, g_shapes, g_scratch = _gather_operands(late)
    res = pl.pallas_call(
        body, name="ssm_forward", grid=(bl, N_GT),
        in_specs=[act,
                  pl.BlockSpec((1, CH_T, 2 * ST_T), lambda b, j: (j, 0, 0)),
                  pl.BlockSpec((1, 2 * ST_T, CH_T), lambda b, j: (j, 0, 0)),
                  pl.BlockSpec((1, ST_T), lambda b, j: (0, j)), pl.BlockSpec((1, ST_T), lambda b, j: (0, j)),
                  pl.BlockSpec((seg, ST_T), lambda b, j: (0, j)), pl.BlockSpec((seg, ST_T), lambda b, j: (0, j)),
                  pl.BlockSpec((1, CH_T), lambda b, j: (0, j))]
        + [pl.BlockSpec(s.shape, lambda b, j: (0, 0)) for s in late],
        out_specs=(act, state(rows, 2 * ST_T), state(16, ST_T)) + g_specs,
        out_shape=(jax.ShapeDtypeStruct((bl, N_SEG, seg, D_SSM), F32),
                   jax.ShapeDtypeStruct((bl, N_GT, rows, 2 * ST_T), F32),
                   jax.ShapeDtypeStruct((bl, N_GT, 16, ST_T), F32)) + g_shapes,
        scratch_shapes=g_scratch + [pltpu.VMEM((rows, CH_T), F32), pltpu.VMEM((rows, CH_T), F32)],
        compiler_params=_tc_params(("arbitrary", "arbitrary")),
    )(u, bcat, ccat, a_re, a_im, pw_re, pw_im, d_row, *late)
    return res[:3], list(res[3:])


def _ssm_backward(u, dy, states, carries, bcat_t, ccat_t, a_re, a_im, pw_re, pw_im, d_row, late16, late32, seg):
    bl = u.shape[0]
    rows = N_SEG * seg
    n = len(late16)
    grid_steps = N_GT * bl

    def body(*refs):
        u_ref, dy_ref, xs_ref, cs_ref, bt_ref, ct_ref, ar_ref, ai_ref, pr_ref, pi_ref, d_ref = refs[:11]
        g16_r, g32_r = refs[11:11 + n], refs[11 + n:11 + 2 * n]
        du_ref, db_ref, dc_ref, dar_ref, dai_ref, dd_ref = refs[11 + 2 * n:17 + 2 * n]
        red = refs[17 + 2 * n:17 + 3 * n]
        own16, recv1, send2, recv2, own32 = (refs[17 + 3 * n + k * n:17 + 3 * n + (k + 1) * n] for k in range(5))
        s_send, s_recv, s_local, ls, cl, ui, dyi, dui = refs[17 + 8 * n:]
        b = pl.program_id(1)
        step = pl.program_id(0) * bl + b
        start, middle, total, finish = _hosted_reduce_phases(g16_r, g32_r, red, own16, recv1, send2, recv2, own32,
                                                             s_send, s_recv, s_local)
        pl.when(step == 0)(start)
        pl.when(step == grid_steps // 4)(middle)
        pl.when(step == (grid_steps * 3) // 4)(total)
        pl.when(step == grid_steps - 1)(finish)
        _interleave(u_ref.at[0], ui, seg)
        _interleave(dy_ref.at[0], dyi, seg)
        u = ui[...]
        dy = dyi[...]
        xs, cs = xs_ref.at[0, 0], cs_ref.at[0, 0]
        ls[...] = _mm(dy, ct_ref[0])
        are = jnp.broadcast_to(ar_ref[...], (N_SEG, ST_T))
        aim = jnp.broadcast_to(ai_ref[...], (N_SEG, ST_T))

        def steps(k, carry):
            lr, li = carry
            for j in range(SCAN_UNROLL):
                r = pl.multiple_of((seg - 1 - (k * SCAN_UNROLL + j)) * N_SEG, N_SEG)
                nr = are * lr + aim * li + ls[pl.ds(r, N_SEG), 0:ST_T]
                ni = are * li - aim * lr + ls[pl.ds(r, N_SEG), ST_T:2 * ST_T]
                ls[pl.ds(r, N_SEG), 0:ST_T] = nr
                ls[pl.ds(r, N_SEG), ST_T:2 * ST_T] = ni
                lr, li = nr, ni
            return lr, li

        zero = jnp.zeros((N_SEG, ST_T), F32)
        fr, fi = lax.fori_loop(0, seg // SCAN_UNROLL, steps, (zero, zero))
        sr, si = pr_ref[seg - 1:seg, :], pi_ref[seg - 1:seg, :]
        cr = jnp.zeros((1, ST_T), F32)
        ci = jnp.zeros((1, ST_T), F32)
        cl[7:8, :] = cr
        cl[15:16, :] = ci
        for s in range(N_SEG - 2, -1, -1):
            ncr = sr * cr + si * ci + fr[s + 1:s + 2, :]
            nci = sr * ci - si * cr + fi[s + 1:s + 2, :]
            cr, ci = ncr, nci
            cl[s:s + 1, :] = cr
            cl[8 + s:9 + s, :] = ci
        clr, cli = cl[0:8, :], cl[8:16, :]

        def fix_rows(rows, t, xpr, xpi, acc):
            dr, di = acc
            pr, pi = pr_ref[pl.ds(seg - 1 - t, 1), :], pi_ref[pl.ds(seg - 1 - t, 1), :]
            lr = ls[rows, 0:ST_T] + (pr * clr + pi * cli)
            li = ls[rows, ST_T:2 * ST_T] + (pr * cli - pi * clr)
            ls[rows, 0:ST_T] = lr
            ls[rows, ST_T:2 * ST_T] = li
            return dr + (lr * xpr + li * xpi), di + (li * xpr - lr * xpi)

        def fix_at(t, acc):
            prev = _seg_rows(t - 1)
            return fix_rows(_seg_rows(t), t, xs[prev, 0:ST_T], xs[prev, ST_T:2 * ST_T], acc)

        def fix(k, acc):
            for j in range(SCAN_UNROLL):
                acc = fix_at(k * SCAN_UNROLL + j, acc)
            return acc

        acc = fix_rows(pl.ds(0, N_SEG), 0, cs[0:8, :], cs[8:16, :], (zero, zero))
        for t in range(1, SCAN_UNROLL):
            acc = fix_at(t, acc)
        dr, di = lax.fori_loop(1, seg // SCAN_UNROLL, fix, acc)
        dar = jnp.sum(dr, axis=0, keepdims=True)
        dai = jnp.sum(di, axis=0, keepdims=True)
        lall = ls[...]
        dui[...] = _mm(lall, bt_ref[0]) + d_ref[...] * dy
        for s in range(N_SEG):
            du_ref[0, s] = _deinterleave(dui, seg, s).astype(BF16)
        dbp = _mm_tn(u, lall)
        dcp = _mm_tn(dy, xs[...])
        ddp = jnp.sum(dy * u, axis=0, keepdims=True)

        @pl.when(b == 0)
        def _():
            db_ref[0] = dbp
            dc_ref[0] = dcp
            dar_ref[...] = dar
            dai_ref[...] = dai
            dd_ref[...] = ddp

        @pl.when(b != 0)
        def _():
            db_ref[0] += dbp
            dc_ref[0] += dcp
            dar_ref[...] += dar
            dai_ref[...] += dai
            dd_ref[...] += ddp

    tile3 = lambda r, c: pl.BlockSpec((1, r, c), lambda j, b: (j, 0, 0))
    lane = lambda r, c: pl.BlockSpec((r, c), lambda j, b: (0, j))
    act = pl.BlockSpec((1, N_SEG, seg, CH_T), lambda j, b: (b, 0, 0, j))
    state = lambda r, c: pl.BlockSpec((1, 1, r, c), lambda j, b: (b, j, 0, 0))
    r_in, r_out, r_shapes, r_scratch = _hosted_reduce_operands(late16, lambda s: pl.BlockSpec(s, lambda j, b: (0, 0)))
    res = pl.pallas_call(
        body, name="ssm_backward", grid=(N_GT, bl),
        in_specs=[act, act, state(rows, 2 * ST_T), state(16, ST_T), tile3(2 * ST_T, CH_T), tile3(CH_T, 2 * ST_T),
                  lane(1, ST_T), lane(1, ST_T), lane(seg, ST_T), lane(seg, ST_T), lane(1, CH_T)] + r_in,
        out_specs=(act, tile3(CH_T, 2 * ST_T), tile3(CH_T, 2 * ST_T), lane(1, ST_T), lane(1, ST_T), lane(1, CH_T))
        + r_out,
        out_shape=(jax.ShapeDtypeStruct((bl, N_SEG, seg, D_SSM), BF16),
                   jax.ShapeDtypeStruct((N_GT, CH_T, 2 * ST_T), F32), jax.ShapeDtypeStruct((N_GT, CH_T, 2 * ST_T), F32),
                   jax.ShapeDtypeStruct((1, N_STATE), F32), jax.ShapeDtypeStruct((1, N_STATE), F32),
                   jax.ShapeDtypeStruct((1, D_SSM), F32)) + r_shapes,
        scratch_shapes=r_scratch + [pltpu.VMEM((rows, 2 * ST_T), F32), pltpu.VMEM((16, ST_T), F32)]
        + [pltpu.VMEM((rows, CH_T), F32)] * 3,
        compiler_params=_tc_params(("arbitrary", "arbitrary")),
    )(u, dy, states, carries, bcat_t, ccat_t, a_re, a_im, pw_re, pw_im, d_row, *late16, *late32)
    return res[:6], list(res[6:])


def _ssm_param_grads(lam_re, lam_im, log_step, b_re, b_im, da_re, da_im, d_bcat, d_ccat_t):
    def body(lr_ref, li_ref, ls_ref, br_ref, bi_ref, gar_ref, gai_ref, gbcat_ref, gccat_ref,
             dlr_ref, dli_ref, dls_ref, dbr_ref, dbi_ref, dcr_ref, dci_ref, gbr_s, gbi_s):
        same, _, _, rep_t = _tile_masks()
        rep_t = rep_t.astype(F32)
        for j in range(N_GT):
            rows = slice(j * CH_T, (j + 1) * CH_T)
            for src, dsts in ((gbcat_ref, (gbr_s, gbi_s)), (gccat_ref, (dcr_ref, dci_ref))):
                for k, dst in enumerate(dsts):
                    blk = jnp.where(same, src[j, :, k * ST_T:(k + 1) * ST_T], 0.0)
                    dst[rows, :] = jnp.dot(blk, rep_t, precision=lax.Precision.HIGHEST, preferred_element_type=F32)
        dci_ref[...] = -dci_ref[...]
        lr, li = lr_ref[...], li_ref[...]
        step, ar, ai, den, cr, ci = _discretise(lr, li, ls_ref[...])
        crb, cib = _per_channel(cr), _per_channel(ci)
        br, bi = br_ref[...], bi_ref[...]
        gbr, gbi = gbr_s[...], gbi_s[...]
        dbr_ref[...] = crb * gbr + cib * gbi
        dbi_ref[...] = crb * gbi - cib * gbr
        over_channels = lambda t: jnp.sum(t.reshape(SSM_G, SSM_P, SSM_N), axis=1)
        gcr = over_channels(br * gbr + bi * gbi)
        gci = over_channels(br * gbi - bi * gbr)
        ilr, ili = lr / den, -li / den
        gar = gar_ref[...] + (ilr * gcr + ili * gci)
        gai = gai_ref[...] + (ilr * gci - ili * gcr)
        qr, qi = cr * ilr - ci * ili, cr * ili + ci * ilr
        glr = -(qr * gcr + qi * gci)
        gli = -(qr * gci - qi * gcr)
        gwr = ar * gar + ai * gai
        gwi = ar * gai - ai * gar
        dlr_ref[...] = glr + step * gwr
        dli_ref[...] = gli + step * gwi
        dls_ref[...] = jnp.sum(lr * gwr + li * gwi, axis=-1, keepdims=True) * step

    lam = jax.ShapeDtypeStruct((SSM_G, SSM_N), F32)
    mat = jax.ShapeDtypeStruct((SSM_G * SSM_P, SSM_N), F32)
    vm = pl.BlockSpec(memory_space=pltpu.VMEM)
    return pl.pallas_call(
        body, name="ssm_param_grads", out_shape=(lam, lam, jax.ShapeDtypeStruct((SSM_G, 1), F32), mat, mat, mat, mat),
        in_specs=[vm] * 9, out_specs=(vm,) * 7,
        scratch_shapes=[pltpu.VMEM((SSM_G * SSM_P, SSM_N), F32), pltpu.VMEM((SSM_G * SSM_P, SSM_N), F32)],
    )(lam_re, lam_im, log_step, b_re, b_im, da_re, da_im, d_bcat, d_ccat_t)


ROWS4 = Q_PER_KV * ATT_BLOCK
ATT_FWD_STACK = 1


def _att_dist_mask(first_block):
    qi = lax.broadcasted_iota(jnp.int32, (ROWS4, 2 * ATT_BLOCK), 0) & (ATT_BLOCK - 1)
    si = lax.broadcasted_iota(jnp.int32, (ROWS4, 2 * ATT_BLOCK), 1)
    dist = qi + ATT_BLOCK - si
    valid = (dist >= 0) & (dist < ATT_BLOCK) & ((si >= ATT_BLOCK) | jnp.logical_not(first_block))
    return dist.astype(F32), valid


def _stack_heads(x, kv):
    return jnp.concatenate([x[:, (kv * Q_PER_KV + g) * HEAD_DIM:(kv * Q_PER_KV + g + 1) * HEAD_DIM]
                            for g in range(Q_PER_KV)], axis=0)


def _stack_cols(x, kv):
    return jnp.concatenate([x[:, kv * Q_PER_KV + g:kv * Q_PER_KV + g + 1] for g in range(Q_PER_KV)], axis=0)


def _per_head_col(vals):
    return jnp.concatenate([jnp.full((ATT_BLOCK, 1), v, F32) for v in vals], axis=0)


def _attn_forward(q, k, v, sinks, bl, nb):
    t = q.shape[0]

    def body(sink_ref, q_ref, kp_ref, kc_ref, vp_ref, vc_ref, o_ref, lse_ref):
        i = pl.program_id(1)
        dist4, valid4 = _att_dist_mask(i == 0)
        rows2 = ATT_FWD_STACK * ATT_BLOCK
        dist, valid = dist4[0:rows2, :], valid4[0:rows2, :]
        kk = jnp.concatenate([kp_ref[...], kc_ref[...]], axis=0)
        vv = jnp.concatenate([vp_ref[...], vc_ref[...]], axis=0)
        qv = q_ref[...]
        col = lambda vals: jnp.concatenate([jnp.full((ATT_BLOCK, 1), v, F32) for v in vals], axis=0)
        stacks = [range(h0, h0 + ATT_FWD_STACK) for h0 in range(0, N_HEADS, ATT_FWD_STACK)]
        kv_cols = lambda heads: slice(heads[0] // Q_PER_KV * HEAD_DIM, (heads[0] // Q_PER_KV + 1) * HEAD_DIM)
        scores = [_mm_nt(jnp.concatenate([qv[:, h * HEAD_DIM:(h + 1) * HEAD_DIM] for h in heads], axis=0),
                         kk[:, kv_cols(heads)]) for heads in stacks]
        softmaxes = []
        for heads, qk in zip(stacks, scores):
            slope = col([2.0 ** (-(h + 1)) for h in heads])
            sink = col([sink_ref[h] for h in heads])
            s = jnp.where(valid, qk * ATT_SCALE - slope * dist, NEG_INF)
            m = jnp.maximum(jnp.max(s, axis=-1, keepdims=True), sink)
            e = jnp.exp(s - m)
            den = jnp.sum(e, axis=-1, keepdims=True) + jnp.exp(sink - m)
            softmaxes.append((e.astype(BF16), 1.0 / den, m + jnp.log(den)))
        for heads, (e, inv_den, lse) in zip(stacks, softmaxes):
            o = _mm(e, vv[:, kv_cols(heads)]) * inv_den
            for g, h in enumerate(heads):
                rows = slice(g * ATT_BLOCK, (g + 1) * ATT_BLOCK)
                o_ref[:, h * HEAD_DIM:(h + 1) * HEAD_DIM] = o[rows, :]
                lse_ref[:, h:h + 1] = lse[rows, :]

    cur = lambda w: pl.BlockSpec((ATT_BLOCK, w), lambda b, i: (b * nb + i, 0))
    prev = lambda w: pl.BlockSpec((ATT_BLOCK, w), lambda b, i: (b * nb + jnp.maximum(i - 1, 0), 0))
    return pl.pallas_call(
        body, name="attn_forward", grid=(bl, nb),
        in_specs=[pl.BlockSpec(memory_space=pltpu.SMEM), cur(512), prev(128), cur(128), prev(128), cur(128)],
        out_specs=(cur(512), cur(N_HEADS)),
        out_shape=(jax.ShapeDtypeStruct((t, D_ATTN), F32), jax.ShapeDtypeStruct((t, N_HEADS), F32)),
        compiler_params=_tc_params(("arbitrary", "arbitrary")),
    )(sinks, q, k, k, v, v)


def _attn_backward(q, k, v, o, do, lse, sinks, bl, nb):
    t = q.shape[0]

    def body(sink_ref, qc_ref, kp_ref, kc_ref, vp_ref, vc_ref, oc_ref, doc_ref, lc_ref,
             dq_ref, dk_ref, dv_ref, ds_ref, dk_carry, dv_carry):
        b, i = pl.program_id(0), pl.program_id(1)
        live = i < nb

        @pl.when(i == 0)
        def _():
            dk_carry[...] = jnp.zeros((ATT_BLOCK, KV_HEADS * HEAD_DIM), F32)
            dv_carry[...] = jnp.zeros((ATT_BLOCK, KV_HEADS * HEAD_DIM), F32)

        dist, valid = _att_dist_mask(i == 0)
        valid = valid & live
        kk = jnp.concatenate([kp_ref[...], kc_ref[...]], axis=0)
        vv = jnp.concatenate([vp_ref[...], vc_ref[...]], axis=0)
        qc, oc, doc, lc = qc_ref[...], oc_ref[...], doc_ref[...], lc_ref[...]
        dsink_cols, dq_parts, dk_t, dv_t = [], [], [], []
        for kv in range(KV_HEADS):
            heads = range(kv * Q_PER_KV, (kv + 1) * Q_PER_KV)
            cols = slice(kv * HEAD_DIM, (kv + 1) * HEAD_DIM)
            kh, vh = kk[:, cols], vv[:, cols]
            slope = _per_head_col([2.0 ** (-(h + 1)) for h in heads])
            sink = _per_head_col([sink_ref[h] for h in heads])
            q4, do4 = _stack_heads(qc, kv), _stack_heads(doc, kv)
            delta = jnp.sum(do4 * _stack_heads(oc, kv), axis=-1, keepdims=True)
            lse4 = _stack_cols(lc, kv)
            s = _mm_nt(q4, kh) * ATT_SCALE - slope * dist
            p = jnp.where(valid, jnp.exp(s - lse4), 0.0)
            dsc = p * (_mm_nt(do4, vh) - delta)
            dq4 = _mm(dsc, kh) * ATT_SCALE
            dk_t.append(_mm_tn(q4, dsc) * ATT_SCALE)
            dv_t.append(_mm_tn(do4, p))
            dsink4 = jnp.where(live, jnp.exp(sink - lse4) * delta, 0.0)
            for g, h in enumerate(heads):
                rows = slice(g * ATT_BLOCK, (g + 1) * ATT_BLOCK)
                dq_parts.append((h, dq4[rows, :]))
                dsink_cols.append(-jnp.sum(dsink4[rows, :], axis=0, keepdims=True))
        dsink = jnp.concatenate(dsink_cols, axis=1)
        for out_ref, carry, parts in ((dk_ref, dk_carry, dk_t), (dv_ref, dv_carry, dv_t)):
            both = jnp.concatenate(parts, axis=0)
            out_ref[...] = (carry[...] + both[:, 0:ATT_BLOCK]).T
            carry[...] = both[:, ATT_BLOCK:]

        @pl.when(live)
        def _():
            for h, part in dq_parts:
                dq_ref[:, h * HEAD_DIM:(h + 1) * HEAD_DIM] = part

        @pl.when((b == 0) & (i == 0))
        def _():
            ds_ref[...] = dsink

        @pl.when((b != 0) | (i != 0))
        def _():
            ds_ref[...] += dsink

    cur_i = lambda i: jnp.minimum(i, nb - 1)
    cur = lambda w: pl.BlockSpec((ATT_BLOCK, w), lambda b, i: (b * nb + cur_i(i), 0))
    prev = lambda w: pl.BlockSpec((ATT_BLOCK, w), lambda b, i: (b * nb + jnp.maximum(cur_i(i) - 1, 0), 0))
    behind = lambda w: pl.BlockSpec((ATT_BLOCK, w), lambda b, i: (b * nb + jnp.maximum(i - 1, 0), 0))
    return pl.pallas_call(
        body, name="attn_backward", grid=(bl, nb + 1),
        in_specs=[pl.BlockSpec(memory_space=pltpu.SMEM), cur(512), prev(128), cur(128), prev(128), cur(128),
                  cur(512), cur(512), cur(N_HEADS)],
        out_specs=(cur(512), behind(128), behind(128), pl.BlockSpec((1, N_HEADS), lambda b, i: (0, 0))),
        out_shape=(jax.ShapeDtypeStruct((t, D_ATTN), F32), jax.ShapeDtypeStruct((t, 128), F32),
                   jax.ShapeDtypeStruct((t, 128), F32), jax.ShapeDtypeStruct((1, N_HEADS), F32)),
        scratch_shapes=[pltpu.VMEM((ATT_BLOCK, KV_HEADS * HEAD_DIM), F32), pltpu.VMEM((ATT_BLOCK, KV_HEADS * HEAD_DIM), F32)],
        compiler_params=_tc_params(("arbitrary", "arbitrary")),
    )(sinks, q, k, k, v, v, o, do, lse)


def _mix_forward_backward(x2, y2, z_ssm, attn, z_attn, p2, target2, w_glu, b_glu, w_out, g_post, w_gate, b_gate,
                          w_proj, tm):
    t = x2.shape[0]

    def body(x_ref, y_ref, zs_ref, at_ref, za_ref, p_ref, tg_ref,
             wglu_ref, bglu_ref, wout_ref, gpost_ref, wgate_ref, bgate_ref, wproj_ref,
             loss_ref, dh1_ref, dy_ref, dzs_ref, dat_ref, dza_ref,
             dwglu_ref, dbglu_ref, dwout_ref, dgpost_ref, dwgate_ref, dbgate_ref, dwproj_ref,
             dwout16_ref, dwgate16_ref, dwproj16_ref, dwglu16_ref):
        i = pl.program_id(0)
        gpost = gpost_ref[...]

        @pl.when(i == 0)
        def _():
            for ref in (dwglu_ref, dbglu_ref, dwout_ref, dgpost_ref, dwgate_ref, dbgate_ref, dwproj_ref, loss_ref):
                ref[...] = jnp.zeros(ref.shape, F32)

        def chain(rows):
            y = y_ref[rows, :]
            u3 = GELU_C * (y + GELU_K * y * y * y)
            th = jnp.tanh(u3)
            gl = 0.5 * y * (1.0 + th)
            a = _mm(gl, wglu_ref[...]) + bglu_ref[...]
            sa = _sigmoid(a)
            glu = gl * sa
            zs = zs_ref[rows, :]
            sgs = _sigmoid(zs)
            ssm_out = glu * (zs * sgs)
            za = za_ref[rows, :]
            sga = _sigmoid(za)
            at = at_ref[rows, :]
            attn_out = at * (za * sga)
            cat = jnp.concatenate([ssm_out, attn_out], axis=-1).astype(BF16)
            mixed = _mm(cat, wout_ref[...])
            r2 = lax.rsqrt(jnp.mean(mixed * mixed, axis=-1, keepdims=True) + EPS)
            nhat = mixed * r2
            h1 = x_ref[rows, :] + nhat * gpost
            gate = _sigmoid(_mm(h1, wgate_ref[...]) + bgate_ref[...])
            pv = p_ref[rows, :]
            pp = _mm(pv, wproj_ref[...])
            h2 = h1 + gate * pp
            err = h2 - tg_ref[rows, :]
            loss_part = jnp.sum(jnp.sum(err * err, axis=-1, keepdims=True), axis=0, keepdims=True) * (0.5 / D_MODEL)
            dh2 = err * (1.0 / D_MODEL)
            dgp = dh2 * pp * gate * (1.0 - gate)
            dpp = dh2 * gate
            dh1 = dh2 + _mm_nt(dgp, wgate_ref[...])
            dwproj_ref[...] += _mm_tn(pv, dpp)
            dwgate_ref[...] += _mm_tn(h1, dgp)
            dh1_ref[rows, :] = dh1
            dnhat = dh1 * gpost
            dmixed = r2 * (dnhat - nhat * jnp.mean(dnhat * nhat, axis=-1, keepdims=True))
            dcat = _mm_nt(dmixed, wout_ref[...])
            dwout_ref[...] += _mm_tn(cat, dmixed)
            dso, dao = dcat[:, 0:D_SSM], dcat[:, D_SSM:]
            dat_ref[rows, :] = dao * (za * sga)
            dza_ref[rows, :] = (dao * at * (sga * (1.0 + za * (1.0 - sga)))).astype(BF16)
            dzs_ref[rows, :] = (dso * glu * (sgs * (1.0 + zs * (1.0 - sgs)))).astype(BF16)
            dglu = dso * (zs * sgs)
            da = dglu * gl * sa * (1.0 - sa)
            dgl = dglu * sa + _mm_nt(da, wglu_ref[...])
            dwglu_ref[...] += _mm_tn(gl, da)
            dgelu = 0.5 * (1.0 + th) + 0.5 * y * (1.0 - th * th) * (GELU_C * (1.0 + 3.0 * GELU_K * y * y))
            dy_ref[rows, :] = dgl * dgelu
            dbglu_ref[...] += jnp.sum(da, axis=0, keepdims=True)
            dgpost_ref[...] += jnp.sum(dh1 * nhat, axis=0, keepdims=True)
            dbgate_ref[...] += jnp.sum(dgp, axis=0, keepdims=True)
            loss_ref[...] += loss_part

        chain(slice(None))

        @pl.when(i == t // tm - 1)
        def _():
            for ref16, ref in ((dwout16_ref, dwout_ref), (dwgate16_ref, dwgate_ref), (dwproj16_ref, dwproj_ref),
                               (dwglu16_ref, dwglu_ref)):
                def to16(r, ref16=ref16, ref=ref):
                    ref16[r, :] = ref[r, :].astype(BF16)

                _row_chunks(ref.shape[0], to16)

    row = lambda w: pl.BlockSpec((tm, w), lambda i: (i, 0))
    acc = lambda r, c, dt=F32: (_const_spec((r, c)), jax.ShapeDtypeStruct((r, c), dt))
    accs = [acc(D_SSM, D_SSM), acc(1, D_SSM), acc(D_MODEL, D_MODEL), acc(1, D_MODEL), acc(D_MODEL, D_MODEL),
            acc(1, D_MODEL), acc(D_PLE, D_MODEL),
            acc(D_MODEL, D_MODEL, BF16), acc(D_MODEL, D_MODEL, BF16), acc(D_PLE, D_MODEL, BF16), acc(D_SSM, D_SSM, BF16)]
    return pl.pallas_call(
        body, name="mix_forward_backward", grid=(t // tm,),
        in_specs=[row(D_MODEL), row(512), row(512), row(512), row(512), row(D_PLE), row(D_MODEL),
                  _const_spec((D_SSM, D_SSM)), _const_spec((1, D_SSM)), _const_spec((D_MODEL, D_MODEL)),
                  _const_spec((1, D_MODEL)), _const_spec((D_MODEL, D_MODEL)), _const_spec((1, D_MODEL)),
                  _const_spec((D_PLE, D_MODEL))],
        out_specs=(_const_spec((1, 1)), row(D_MODEL), row(512), row(512), row(512), row(512))
        + tuple(a[0] for a in accs),
        out_shape=(jax.ShapeDtypeStruct((1, 1), F32), jax.ShapeDtypeStruct((t, D_MODEL), F32),
                   jax.ShapeDtypeStruct((t, 512), F32),
                   jax.ShapeDtypeStruct((t, 512), BF16), jax.ShapeDtypeStruct((t, 512), F32),
                   jax.ShapeDtypeStruct((t, 512), BF16)) + tuple(a[1] for a in accs),
        compiler_params=_tc_params(("arbitrary",)),
    )(x2, y2, z_ssm, attn, z_attn, p2, target2, w_glu, b_glu, w_out, g_post, w_gate, b_gate, w_proj)


def _in_backward(x2, dh1, du, dz_ssm, dq, dk, dv, dz_attn, g_pre, w_in, tm):
    t = x2.shape[0]

    def body(x_ref, dh1_ref, du_ref, dzs_ref, dq_ref, dk_ref, dv_ref, dza_ref, g_ref, w_ref,
             gx_ref, dw_ref, dg_ref, dw16_ref):
        i = pl.program_id(0)

        @pl.when(i == 0)
        def _():
            dw_ref[...] = jnp.zeros((D_IN, D_MODEL), F32)
            dg_ref[...] = jnp.zeros((1, D_MODEL), F32)

        xv = x_ref[...]
        r = lax.rsqrt(jnp.mean(xv * xv, axis=-1, keepdims=True) + EPS)
        xhat = xv * r
        g = g_ref[...]
        hn = (xhat * g).astype(BF16)
        dproj = jnp.concatenate([du_ref[...].astype(BF16), dzs_ref[...].astype(BF16), dq_ref[...].astype(BF16),
                                 dk_ref[...].astype(BF16), dv_ref[...].astype(BF16), dza_ref[...].astype(BF16)],
                                axis=-1)
        dhn = _mm(dproj, w_ref[...])
        dxhat = dhn * g
        gx_ref[...] = dh1_ref[...] + r * (dxhat - xhat * jnp.mean(dxhat * xhat, axis=-1, keepdims=True))
        dw_ref[...] += _mm_tn(dproj, hn)
        dg_ref[...] += jnp.sum(dhn * xhat, axis=0, keepdims=True)

        @pl.when(i == t // tm - 1)
        def _():
            def to16(r):
                dw16_ref[r, :] = dw_ref[r, :].astype(BF16)

            _row_chunks(D_IN, to16)

    row = lambda w: pl.BlockSpec((tm, w), lambda i: (i, 0))
    return pl.pallas_call(
        body, name="in_backward", grid=(t // tm,),
        in_specs=[row(D_MODEL), row(D_MODEL), row(512), row(512), row(512), row(128), row(128), row(512),
                  _const_spec((1, D_MODEL)), _const_spec((D_IN, D_MODEL))],
        out_specs=(row(D_MODEL), _const_spec((D_IN, D_MODEL)), _const_spec((1, D_MODEL)),
                   _const_spec((D_IN, D_MODEL))),
        out_shape=(jax.ShapeDtypeStruct((t, D_MODEL), F32), jax.ShapeDtypeStruct((D_IN, D_MODEL), F32),
                   jax.ShapeDtypeStruct((1, D_MODEL), F32), jax.ShapeDtypeStruct((D_IN, D_MODEL), BF16)),
        compiler_params=_tc_params(("arbitrary",)),
    )(x2, dh1, du, dz_ssm, dq, dk, dv, dz_attn, g_pre, w_in)


def _local_step(x, p, target, pre_norm_g, w_in, prep, ssm_lam_re, ssm_lam_im, ssm_log_step, ssm_b_re, ssm_b_im, ssm_d,
                ssm_b_glu, attn_sinks, post_norm_g, pl_b_gate, late):
    bl, seq, _ = x.shape
    seg = seq // N_SEG
    nb = seq // ATT_BLOCK
    t = bl * seq
    x2 = x.reshape(t, D_MODEL)
    p2 = p.reshape(t, D_PLE)
    tg2 = target.reshape(t, D_MODEL)

    lam_re, lam_im = ssm_lam_re, ssm_lam_im
    log_step = ssm_log_step.reshape(SSM_G, 1)
    a_re_row, a_im_row, pw_re, pw_im, bcat, bcat_t, ccat, ccat_t = prep
    d_row = ssm_d.reshape(1, D_SSM)

    segments = lambda a: a.reshape(bl, N_SEG, seg, D_SSM)
    u, z_ssm, q, k, v, z_attn = _in_proj(x2, pre_norm_g.reshape(1, D_MODEL), w_in, min(TOKEN_TILE_WIDE, t))
    (y, states, carries), gathered = _ssm_forward(
        segments(u), bcat, ccat, a_re_row, a_im_row, pw_re, pw_im, d_row, late, seg)
    w_out, w_gate, w_proj, w_glu = (_gathered_to_full(n, g) for n, g in zip(LATE_NAMES, gathered))
    sinks = attn_sinks.reshape(N_HEADS)
    attn, lse = _attn_forward(q, k, v, sinks, bl, nb)
    (loss, dh1, dy, dz_ssm, dattn, dz_attn, d_w_glu, d_b_glu, d_w_out, d_g_post, d_w_gate, d_b_gate,
     d_w_proj, *late16) = _mix_forward_backward(
        x2, y.reshape(t, D_SSM), z_ssm, attn, z_attn, p2, tg2, w_glu,
        ssm_b_glu.reshape(1, D_SSM), w_out, post_norm_g.reshape(1, D_MODEL), w_gate, pl_b_gate.reshape(1, D_MODEL),
        w_proj, min(TOKEN_TILE, t))
    owned = lambda ds: [_full_to_owned(n, d) for n, d in zip(LATE_NAMES, ds)]
    dq, dk, dv, d_sinks = _attn_backward(q, k, v, attn, dattn, lse, sinks, bl, nb)
    (du, d_bcat, d_ccat_t, da_re, da_im, d_d), late_grads = _ssm_backward(
        segments(u), segments(dy), states, carries, bcat_t, ccat_t, a_re_row, a_im_row, pw_re, pw_im,
        d_row, owned(late16), owned((d_w_out, d_w_gate, d_w_proj, d_w_glu)), seg)
    grad_x, d_w_in, d_g_pre, d_w_in16 = _in_backward(
        x2, dh1, du.reshape(t, D_SSM), dz_ssm, dq, dk, dv, dz_attn, pre_norm_g.reshape(1, D_MODEL), w_in,
        min(TOKEN_TILE_WIDE, t))
    d_lam_re, d_lam_im, d_ls, d_b_re, d_b_im, d_c_re, d_c_im = _ssm_param_grads(
        lam_re, lam_im, log_step, ssm_b_re, ssm_b_im, da_re.reshape(SSM_G, SSM_N), da_im.reshape(SSM_G, SSM_N),
        d_bcat, d_ccat_t)
    grads = {
        "pre_norm_g": d_g_pre, "w_in": d_w_in, "w_in16": d_w_in16, "ssm_lam_re": d_lam_re, "ssm_lam_im": d_lam_im,
        "ssm_log_step": d_ls, "ssm_b_re": d_b_re, "ssm_b_im": d_b_im, "ssm_c_re": d_c_re, "ssm_c_im": d_c_im,
        "ssm_d": d_d, "ssm_b_glu": d_b_glu, "attn_sinks": d_sinks, "post_norm_g": d_g_post, "pl_b_gate": d_b_gate,
    }
    return loss, grad_x.reshape(bl, seq, D_MODEL), grads, late_grads


LATE_NAMES = ("w_out", "pl_w_gate", "pl_w_proj", "ssm_w_glu")
BIG_NAMES = ("w_in",) + LATE_NAMES
COL_SHARDED = {"w_in": D_IN // N_DEV, "pl_w_proj": D_MODEL // N_DEV}
WEIGHT_NAMES = ("pre_norm_g", "w_in", "ssm_lam_re", "ssm_lam_im", "ssm_log_step", "ssm_b_re", "ssm_b_im", "ssm_c_re",
                "ssm_c_im", "ssm_d", "ssm_w_glu", "ssm_b_glu", "attn_sinks", "w_out", "post_norm_g", "pl_w_proj",
                "pl_w_gate", "pl_b_gate")


TRANSPOSED = {"w_in": (0, 1), "ssm_b_re": (1, 2), "ssm_b_im": (1, 2)}


def _kernel_form(name, a):
    a = a[0]
    if name in TRANSPOSED:
        a = jnp.swapaxes(a, *TRANSPOSED[name])
    if name in ("ssm_b_re", "ssm_b_im", "ssm_c_re", "ssm_c_im"):
        a = a.reshape(SSM_G * SSM_P, SSM_N)
    return a


def _given_form(name, a, shape):
    if name in TRANSPOSED:
        i, j = TRANSPOSED[name]
        swapped = list(shape[1:])
        swapped[i], swapped[j] = swapped[j], swapped[i]
        return jnp.swapaxes(a.reshape(swapped), i, j).reshape(shape)
    return a.reshape(shape)


def _gathered_to_full(name, g):
    _, rows, cols = g.shape
    if name in COL_SHARDED:
        return jnp.swapaxes(g, 0, 1).reshape(rows, N_DEV * cols)
    return g.reshape(N_DEV * rows, cols)


def _full_to_owned(name, full):
    if name in COL_SHARDED:
        return jnp.swapaxes(full.reshape(full.shape[0], N_DEV, COL_SHARDED[name]), 0, 1)
    return full.reshape(N_DEV, full.shape[0] // N_DEV, full.shape[1])


def kernel(x, p, pre_norm_g, w_in, ssm_lam_re, ssm_lam_im, ssm_log_step, ssm_b_re, ssm_b_im, ssm_c_re, ssm_c_im, ssm_d, ssm_w_glu, ssm_b_glu, attn_sinks, w_out, post_norm_g, pl_w_proj, pl_w_gate, pl_b_gate, loss_target, m_pre_norm_g, m_w_in, m_ssm_lam_re, m_ssm_lam_im, m_ssm_log_step, m_ssm_b_re, m_ssm_b_im, m_ssm_c_re, m_ssm_c_im, m_ssm_d, m_ssm_w_glu, m_ssm_b_glu, m_attn_sinks, m_w_out, m_post_norm_g, m_pl_w_proj, m_pl_w_gate, m_pl_b_gate, v_pre_norm_g, v_w_in, v_ssm_lam_re, v_ssm_lam_im, v_ssm_log_step, v_ssm_b_re, v_ssm_b_im, v_ssm_c_re, v_ssm_c_im, v_ssm_d, v_ssm_w_glu, v_ssm_b_glu, v_attn_sinks, v_w_out, v_post_norm_g, v_pl_w_proj, v_pl_w_gate, v_pl_b_gate):
    w = dict(pre_norm_g=pre_norm_g, w_in=w_in, ssm_lam_re=ssm_lam_re, ssm_lam_im=ssm_lam_im, ssm_log_step=ssm_log_step,
             ssm_b_re=ssm_b_re, ssm_b_im=ssm_b_im, ssm_c_re=ssm_c_re, ssm_c_im=ssm_c_im, ssm_d=ssm_d, ssm_w_glu=ssm_w_glu,
             ssm_b_glu=ssm_b_glu, attn_sinks=attn_sinks, w_out=w_out, post_norm_g=post_norm_g, pl_w_proj=pl_w_proj,
             pl_w_gate=pl_w_gate, pl_b_gate=pl_b_gate)
    m = dict(pre_norm_g=m_pre_norm_g, w_in=m_w_in, ssm_lam_re=m_ssm_lam_re, ssm_lam_im=m_ssm_lam_im,
             ssm_log_step=m_ssm_log_step, ssm_b_re=m_ssm_b_re, ssm_b_im=m_ssm_b_im, ssm_c_re=m_ssm_c_re,
             ssm_c_im=m_ssm_c_im, ssm_d=m_ssm_d, ssm_w_glu=m_ssm_w_glu, ssm_b_glu=m_ssm_b_glu, attn_sinks=m_attn_sinks,
             w_out=m_w_out, post_norm_g=m_post_norm_g, pl_w_proj=m_pl_w_proj, pl_w_gate=m_pl_w_gate,
             pl_b_gate=m_pl_b_gate)
    v = dict(pre_norm_g=v_pre_norm_g, w_in=v_w_in, ssm_lam_re=v_ssm_lam_re, ssm_lam_im=v_ssm_lam_im,
             ssm_log_step=v_ssm_log_step, ssm_b_re=v_ssm_b_re, ssm_b_im=v_ssm_b_im, ssm_c_re=v_ssm_c_re,
             ssm_c_im=v_ssm_c_im, ssm_d=v_ssm_d, ssm_w_glu=v_ssm_w_glu, ssm_b_glu=v_ssm_b_glu, attn_sinks=v_attn_sinks,
             w_out=v_w_out, post_norm_g=v_post_norm_g, pl_w_proj=v_pl_w_proj, pl_w_gate=v_pl_w_gate,
             pl_b_gate=v_pl_b_gate)
    kf = lambda d: {n: _kernel_form(n, a) for n, a in d.items()}
    wk, mk, vk = kf(w), kf(m), kf(v)

    (gathered,), prep = _allgather_weights([wk["w_in"]], *_ssm_prep(
        wk["ssm_lam_re"], wk["ssm_lam_im"], wk["ssm_log_step"].reshape(SSM_G, 1), wk["ssm_b_re"], wk["ssm_b_im"],
        wk["ssm_c_re"], wk["ssm_c_im"], x.shape[1] // N_SEG))
    loss, grad_x, grads, g_late = _local_step(
        x, p[0], loss_target, wk["pre_norm_g"], gathered.reshape(D_IN, D_MODEL), prep, wk["ssm_lam_re"],
        wk["ssm_lam_im"], wk["ssm_log_step"], wk["ssm_b_re"], wk["ssm_b_im"], wk["ssm_d"],
        wk["ssm_b_glu"], wk["attn_sinks"], wk["post_norm_g"], wk["pl_b_gate"], [wk[n] for n in LATE_NAMES])

    owned = lambda g: g.reshape(N_DEV, D_IN // N_DEV, D_MODEL)
    tiny_form = lambda d: [d[n].reshape(rows, cols) for n, rows, cols in TINY]
    med_form = lambda d: [d[n].reshape(N_DEV, rows // N_DEV, cols) for n, rows, cols in MEDIUM]
    g_big, loss, g_tiny, g_med = _reduce_final(
        [owned(grads["w_in16"])], [owned(grads["w_in"])], loss, tiny_form(grads), med_form(grads))
    names = BIG_NAMES + tuple(n for n, _, _ in TINY + MEDIUM)
    form = lambda d: [d[n] for n in BIG_NAMES] + tiny_form(d) + med_form(d)
    updated = _adamw_update(g_big + g_late + g_tiny + g_med, form(wk), form(mk), form(vk), len(BIG_NAMES))
    vals = dict(zip(names, updated))
    results = [[_given_form(n, vals[n][kind], w[n].shape) for n in WEIGHT_NAMES] for kind in range(4)]
    return (loss.reshape(()), grad_x, *results[0], *results[1], *results[2], *results[3])
```

```python
import functools
import math

import jax
import jax.numpy as jnp
from jax import lax
from jax.experimental import pallas as pl
from jax.experimental.pallas import tpu as pltpu

F32 = jnp.float32
BF16 = jnp.bfloat16

D_MODEL = 1024
D_SSM = 512
D_ATTN = 512
SSM_P = 16
SSM_G = 32
SSM_N = 64
N_HEADS = 8
KV_HEADS = 2
Q_PER_KV = 4
HEAD_DIM = 64
ATT_BLOCK = 128
D_PLE = 256
D_IN = 2304
EPS = 1e-6
N_DEV = 8
N_SEG = 8
G_TILE = 8
N_GT = SSM_G // G_TILE
CH_T = G_TILE * SSM_P
ST_T = G_TILE * SSM_N
N_STATE = SSM_G * SSM_N
SCAN_UNROLL = 4
TOKEN_TILE = 256
TOKEN_TILE_WIDE = 512
LANES = 128
VMEM_LIMIT = 60 * 1024 * 1024

ADAM_LR = 0.001
ADAM_B1 = 0.9
ADAM_B2 = 0.999
ADAM_EPS = 1e-08
ADAM_WD = 0.01
ADAM_STEP = 10

GELU_C = math.sqrt(2.0 / math.pi)
GELU_K = 0.044715
ATT_SCALE = 1.0 / math.sqrt(HEAD_DIM)
NEG_INF = float("-inf")


def _mm(a, b):
    return jnp.dot(a.astype(BF16), b.astype(BF16), preferred_element_type=F32)


def _mm_nt(a, b):
    return lax.dot_general(a.astype(BF16), b.astype(BF16), (((1,), (1,)), ((), ())), preferred_element_type=F32)


def _mm_tn(a, b):
    return lax.dot_general(a.astype(BF16), b.astype(BF16), (((0,), (0,)), ((), ())), preferred_element_type=F32)


def _sigmoid(x):
    return 1.0 / (1.0 + jnp.exp(-x))


def _tc_params(sem):
    return pltpu.CompilerParams(dimension_semantics=sem, vmem_limit_bytes=VMEM_LIMIT)


def _const_spec(shape):
    nd = len(shape)
    return pl.BlockSpec(shape, lambda *_: (0,) * nd)


def _mesh_pos():
    return lax.axis_index("x"), lax.axis_index("y"), lax.axis_index("c")


ROW_CHUNKS = (64, 32, 16)


def _row_chunk(nrows):
    return next((c for c in ROW_CHUNKS if nrows % c == 0), None)


def _row_chunks(nrows, fn, chunk=None, init=None):
    chunk = chunk or _row_chunk(nrows)

    def step(i, carry):
        rows = pl.ds(pl.multiple_of(i * chunk, chunk), chunk)
        if init is None:
            fn(rows)
            return carry
        return fn(rows, carry)

    return lax.fori_loop(0, nrows // chunk, step, 0 if init is None else init)


def _slot(px, py, pc):
    return 4 * px + 2 * py + pc


def _copy_owned_rows(acc_ref, own_ref):
    rows = own_ref.shape[1]
    own_ref[0] = acc_ref[pl.ds(pl.multiple_of(_slot(*_mesh_pos()) * rows, 8), rows), :]


def _allgather_weights(shards, work=None, work_inputs=(), work_out_shapes=()):
    n, n_wi, n_wo = len(shards), len(work_inputs), len(work_out_shapes)

    def body(*refs):
        srcs, w_in_refs = refs[:n], refs[n:n + n_wi]
        outs, w_out_refs = refs[n + n_wi:2 * n + n_wi], refs[2 * n + n_wi:2 * n + n_wi + n_wo]
        send_sems, recv_sems = refs[2 * n + n_wi + n_wo:]
        x, y, c = _mesh_pos()
        me, sibling = (x, y, c), (x, y, 1 - c)
        chips = [(1 - x, y), (x, 1 - y), (1 - x, 1 - y)]

        def copy(a, k, block, to):
            blk = outs[a].at[_slot(*block)]
            return pltpu.make_async_remote_copy(
                src_ref=blk, dst_ref=blk, send_sem=send_sems.at[7 * a + k], recv_sem=recv_sems.at[7 * a + k],
                device_id=to, device_id_type=pl.DeviceIdType.MESH)

        sends = []
        for a in range(n):
            mine = outs[a].at[_slot(*me)]

            def cast(r, mine=mine, src=srcs[a]):
                mine[r, :] = src[r, :].astype(BF16)

            _row_chunks(srcs[a].shape[0], cast)
            first = [copy(a, 0, me, sibling)] + [copy(a, 1 + j, me, (*chip, c)) for j, chip in enumerate(chips)]
            for cp in first:
                cp.start()
            sends += first
        if work is not None:
            work(w_in_refs, w_out_refs)
        for a in range(n):
            for j, chip in enumerate(chips):
                copy(a, 1 + j, (*chip, c), me).wait_recv()
                fwd = copy(a, 4 + j, (*chip, c), sibling)
                fwd.start()
                sends.append(fwd)
        for a in range(n):
            copy(a, 0, sibling, me).wait_recv()
            for j, chip in enumerate(chips):
                copy(a, 4 + j, (*chip, 1 - c), me).wait_recv()
        for cp in sends:
            cp.wait_send()

    vm = pl.BlockSpec(memory_space=pltpu.VMEM)
    res = pl.pallas_call(
        body, name="allgather_weights",
        out_shape=tuple(jax.ShapeDtypeStruct((N_DEV,) + s.shape, BF16) for s in shards) + tuple(work_out_shapes),
        in_specs=[vm] * (n + n_wi), out_specs=(vm,) * (n + n_wo),
        scratch_shapes=[pltpu.SemaphoreType.DMA((7 * n,)), pltpu.SemaphoreType.DMA((7 * n,))],
        compiler_params=pltpu.CompilerParams(vmem_limit_bytes=VMEM_LIMIT),
    )(*shards, *work_inputs)
    return list(res[:n]), list(res[n:])


def _adamw(w, g, m, v):
    m = ADAM_B1 * m + (1.0 - ADAM_B1) * g
    v = ADAM_B2 * v + (1.0 - ADAM_B2) * (g * g)
    m_hat = m / (1.0 - ADAM_B1 ** ADAM_STEP)
    v_hat = v / (1.0 - ADAM_B2 ** ADAM_STEP)
    delta = -ADAM_LR * (m_hat / (jnp.sqrt(v_hat) + ADAM_EPS) + ADAM_WD * w)
    return delta, m, v


def _remote(src, dst, send_sems, recv_sems, k, to):
    return pltpu.make_async_remote_copy(src_ref=src, dst_ref=dst, send_sem=send_sems.at[k], recv_sem=recv_sems.at[k],
                                        device_id=to, device_id_type=pl.DeviceIdType.MESH)


def _big_reduce_phases(g16_r, go_r, outs, send2, recv1, recv2, s_send, s_recv):
    n = len(g16_r)
    x, y, c = _mesh_pos()
    sibling = (x, y, 1 - c)
    chips = [(1 - x, y), (x, 1 - y), (1 - x, 1 - y)]
    all_chips = [(x, y)] + chips
    lvl1 = []
    for a in range(n):
        cps = [_remote(g16_r[a].at[_slot(*chip, 1 - c)], recv1[a].at[j], s_send, s_recv, 7 * a + j, sibling)
               for j, chip in enumerate(all_chips)]
        for cp in cps:
            cp.start()
        lvl1.append(cps)
    yield
    lvl2 = []
    for a in range(n):
        for cp in lvl1[a]:
            cp.wait_recv()
        og = outs[a]

        def partials(r, a=a, og=og):
            og[r, :] = go_r[a][r, :] + recv1[a][0, r, :].astype(F32)
            for j, chip in enumerate(chips):
                mine16 = g16_r[a][_slot(*chip, c), r, :].astype(F32)
                send2[a][j, r, :] = (mine16 + recv1[a][1 + j, r, :].astype(F32)).astype(BF16)

        _row_chunks(go_r[a].shape[0], partials)
        cps = [_remote(send2[a].at[j], recv2[a].at[j], s_send, s_recv, 7 * a + 4 + j, (*chip, c))
               for j, chip in enumerate(chips)]
        for cp in cps:
            cp.start()
        lvl2.append(cps)
    yield
    for a in range(n):
        for cp in lvl2[a]:
            cp.wait_recv()
        og = outs[a]

        def total(r, a=a, og=og):
            g = og[r, :]
            for j in range(3):
                g = g + recv2[a][j, r, :].astype(F32)
            og[r, :] = g

        _row_chunks(go_r[a].shape[0], total)
    yield
    for cps in lvl1 + lvl2:
        for cp in cps:
            cp.wait_send()


def _adamw_update(g, w, m, v, n_streamed):
    n = len(g)
    ns = n_streamed

    def body(*refs):
        g_r, w_r, m_r, v_r = (refs[i * n:(i + 1) * n] for i in range(4))
        outs = refs[4 * n:8 * n]
        in_buf, out_buf = refs[8 * n:8 * n + 4 * ns], refs[8 * n + 4 * ns:8 * n + 8 * ns]
        in_sems, out_sems = refs[8 * n + 8 * ns:]
        loads = [[pltpu.make_async_copy(src[a], in_buf[4 * a + k], in_sems.at[4 * a + k])
                  for k, src in enumerate((g_r, w_r, m_r, v_r))] for a in range(ns)]
        stores = [[pltpu.make_async_copy(out_buf[4 * a + k], outs[4 * a + k], out_sems.at[4 * a + k]) for k in range(4)]
                  for a in range(ns)]
        for cps in loads:
            for cp in cps:
                cp.start()
        for a in range(n):
            if a < ns:
                for cp in loads[a]:
                    cp.wait()
                gs, ws, ms, vs = in_buf[4 * a:4 * a + 4]
                og, od, om, ov = out_buf[4 * a:4 * a + 4]
            else:
                gs, ws, ms, vs = g_r[a], w_r[a], m_r[a], v_r[a]
                og, od, om, ov = outs[4 * a:4 * a + 4]

            def update(idx, gs=gs, ws=ws, ms=ms, vs=vs, og=og, od=od, om=om, ov=ov):
                gv = gs[idx]
                d, nm, nv = _adamw(ws[idx], gv, ms[idx], vs[idx])
                og[idx] = gv
                od[idx] = d
                om[idx] = nm
                ov[idx] = nv

            shape = gs.shape
            if len(shape) == 3:
                for b in range(shape[0]):
                    update(b)
            elif _row_chunk(shape[0]) is not None:
                _row_chunks(shape[0], update)
            else:
                update(Ellipsis)
            if a < ns:
                for cp in stores[a]:
                    cp.start()
        for cps in stores:
            for cp in cps:
                cp.wait()

    vm, hbm = pl.BlockSpec(memory_space=pltpu.VMEM), pl.BlockSpec(memory_space=pl.ANY)
    place = lambda: [hbm] * ns + [vm] * (n - ns)
    buf = [pltpu.VMEM(t.shape, F32) for t in g[:ns] for _ in range(4)]
    res = pl.pallas_call(
        body, name="adamw_update",
        out_shape=tuple(jax.ShapeDtypeStruct(t.shape, F32) for t in g for _ in range(4)),
        in_specs=place() * 4, out_specs=tuple(s for a in range(n) for s in [hbm if a < ns else vm] * 4),
        scratch_shapes=buf + buf + [pltpu.SemaphoreType.DMA((4 * ns,)), pltpu.SemaphoreType.DMA((4 * ns,))],
        compiler_params=pltpu.CompilerParams(vmem_limit_bytes=VMEM_LIMIT),
    )(*g, *w, *m, *v)
    return [res[4 * a:4 * a + 4] for a in range(n)]


TINY = (("pre_norm_g", 1, 1024), ("post_norm_g", 1, 1024), ("pl_b_gate", 1, 1024), ("ssm_d", 1, 512),
        ("ssm_b_glu", 1, 512), ("ssm_log_step", 1, 32), ("attn_sinks", 1, 8), ("ssm_lam_re", 32, 64),
        ("ssm_lam_im", 32, 64))
MEDIUM = (("ssm_b_re", SSM_G * SSM_P, SSM_N), ("ssm_b_im", SSM_G * SSM_P, SSM_N), ("ssm_c_re", SSM_G * SSM_P, SSM_N),
          ("ssm_c_im", SSM_G * SSM_P, SSM_N))


def _stage_rows():
    offs, r = {}, 0
    for name, rows, cols in TINY + (("loss", 1, 1),):
        if rows > 1:
            r = -(-r // 8) * 8
        offs[name] = r
        r += rows if rows > 1 else max(cols // LANES, 1)
    return offs, -(-r // 8) * 8


def _reduce_final(g16, g32, loss, g_tiny, g_med):
    nb_, nt, nm_ = len(g16), len(TINY), len(MEDIUM)
    offs, stage_rows = _stage_rows()

    def body(*refs):
        g16_r, go_r = refs[:nb_], refs[nb_:2 * nb_]
        base = 2 * nb_
        loss_r, gt, gm = refs[base], refs[base + 1:base + 1 + nt], refs[base + 1 + nt:base + 1 + nt + nm_]
        base += 1 + nt + nm_
        out_b = refs[base:base + nb_]
        base += nb_
        loss_o, out_t, out_m = refs[base], refs[base + 1:base + 1 + nt], refs[base + 1 + nt:base + 1 + nt + nm_]
        base += 1 + nt + nm_
        send2_b, recv1_b, recv2_b = (refs[base + i * nb_:base + (i + 1) * nb_] for i in range(3))
        base += 3 * nb_
        stage = refs[base]
        recv1, part, recv2 = (refs[base + 1 + i * nm_:base + 1 + (i + 1) * nm_] for i in range(3))
        bs_send, bs_recv, s_send, s_recv, own_sems = refs[base + 1 + 3 * nm_:base + 6 + 3 * nm_]
        own32 = refs[base + 6 + 3 * nm_:]
        fetch = [pltpu.make_async_copy(go_r[a].at[0], own32[a], own_sems.at[a]) for a in range(nb_)]
        for cp in fetch:
            cp.start()
        big = _big_reduce_phases(g16_r, own32, out_b, send2_b, recv1_b, recv2_b, bs_send, bs_recv)
        small = small_phases(loss_r, gt, gm, loss_o, out_t, out_m, stage, recv1, part, recv2, s_send, s_recv)
        next(big)
        next(small)
        for cp in fetch:
            cp.wait()
        next(big)
        for _ in small:
            pass
        for _ in big:
            pass

    def small_phases(loss_r, gt, gm, loss_o, out_t, out_m, stage, recv1, part, recv2, s_send, s_recv):
        x, y, c = _mesh_pos()
        me = _slot(x, y, c)
        sibling = (x, y, 1 - c)
        chips = [(1 - x, y), (x, 1 - y), (1 - x, 1 - y)]
        all_chips = [(x, y)] + chips
        peers = [sibling] + [(*chip, c) for chip in chips] + [(*chip, 1 - c) for chip in chips]
        sem = iter(range(7 + 14 * nm_))
        lvl1 = []
        for a in range(nm_):
            cps = [_remote(gm[a].at[_slot(*chip, 1 - c)], recv1[a].at[j], s_send, s_recv, next(sem), sibling)
                   for j, chip in enumerate(all_chips)]
            for cp in cps:
                cp.start()
            lvl1.append(cps)
        mine = stage.at[me]
        mine[...] = jnp.zeros((stage_rows, LANES), F32)
        for (name, rows, cols), ref in zip(TINY + (("loss", 1, 1),), gt + (loss_r,)):
            r0 = offs[name]
            if rows > 1:
                mine[r0:r0 + rows, 0:cols] = ref[...]
            elif cols >= LANES:
                for i in range(cols // LANES):
                    mine[r0 + i:r0 + i + 1, :] = ref[:, i * LANES:(i + 1) * LANES]
            else:
                mine[r0:r0 + 1, 0:cols] = ref[...]
        tiny_cps = [_remote(mine, mine, s_send, s_recv, next(sem), peer) for peer in peers]
        for cp in tiny_cps:
            cp.start()
        yield
        lvl2 = []
        for a in range(nm_):
            for cp in lvl1[a]:
                cp.wait_recv()
            for j, chip in enumerate(all_chips):
                part[a][j] = gm[a][_slot(*chip, c)] + recv1[a][j]
            cps = [_remote(part[a].at[1 + j], recv2[a].at[j], s_send, s_recv, next(sem), (*chip, c))
                   for j, chip in enumerate(chips)]
            for cp in cps:
                cp.start()
            lvl2.append(cps)
        yield
        lvl3 = []
        for a in range(nm_):
            for cp in lvl2[a]:
                cp.wait_recv()
            blk = out_m[a].at[me]
            blk[...] = ((part[a][0] + recv2[a][0]) + recv2[a][1]) + recv2[a][2]
            cps = [_remote(blk, blk, s_send, s_recv, next(sem), peer) for peer in peers]
            for cp in cps:
                cp.start()
            lvl3.append(cps)
        yield
        for cp in tiny_cps:
            cp.wait_recv()
        tot = stage[0]
        for d in range(1, N_DEV):
            tot = tot + stage[d]
        loss_o[...] = tot[offs["loss"]:offs["loss"] + 1, 0:1]
        for k, (name, rows, cols) in enumerate(TINY):
            r0 = offs[name]
            if rows > 1:
                out_t[k][...] = tot[r0:r0 + rows, 0:cols]
            elif cols >= LANES:
                for i in range(cols // LANES):
                    out_t[k][:, i * LANES:(i + 1) * LANES] = tot[r0 + i:r0 + i + 1, :]
            else:
                out_t[k][...] = tot[r0:r0 + 1, 0:cols]
        for cps in lvl3:
            for cp in cps:
                cp.wait_recv()
        for cps in lvl1 + lvl2 + lvl3 + [tiny_cps]:
            for cp in cps:
                cp.wait_send()

    vmem = pl.BlockSpec(memory_space=pltpu.VMEM)
    t_shapes = [jax.ShapeDtypeStruct((rows, cols), F32) for _, rows, cols in TINY]
    m_shapes = [jax.ShapeDtypeStruct((N_DEV, rows // N_DEV, cols), F32) for _, rows, cols in MEDIUM]
    blk = [(rows // N_DEV, cols) for _, rows, cols in MEDIUM]
    shard = [g.shape[1:] for g in g16]
    scratch = ([pltpu.VMEM((3,) + s, BF16) for s in shard] + [pltpu.VMEM((4,) + s, BF16) for s in shard]
               + [pltpu.VMEM((3,) + s, BF16) for s in shard]
               + [pltpu.VMEM((N_DEV, stage_rows, LANES), F32)]
               + [pltpu.VMEM((4,) + b, F32) for b in blk] + [pltpu.VMEM((4,) + b, F32) for b in blk]
               + [pltpu.VMEM((3,) + b, F32) for b in blk]
               + [pltpu.SemaphoreType.DMA((7 * nb_,)), pltpu.SemaphoreType.DMA((7 * nb_,)),
                  pltpu.SemaphoreType.DMA((7 + 14 * nm_,)), pltpu.SemaphoreType.DMA((7 + 14 * nm_,)),
                  pltpu.SemaphoreType.DMA((nb_,))]
               + [pltpu.VMEM(s, F32) for s in shard])
    n_out = nb_ + 1 + nt + nm_
    res = pl.pallas_call(
        body, name="reduce_final",
        out_shape=tuple(jax.ShapeDtypeStruct(s, F32) for s in shard) + (jax.ShapeDtypeStruct((1, 1), F32),)
        + tuple(t_shapes) + tuple(m_shapes),
        in_specs=[vmem] * nb_ + [pl.BlockSpec(memory_space=pl.ANY)] * nb_ + [vmem] * (1 + nt + nm_),
        out_specs=(vmem,) * n_out, scratch_shapes=scratch,
        compiler_params=pltpu.CompilerParams(vmem_limit_bytes=VMEM_LIMIT),
    )(*g16, *g32, loss, *g_tiny, *g_med)
    return list(res[:nb_]), res[nb_], list(res[nb_ + 1:nb_ + 1 + nt]), list(res[nb_ + 1 + nt:])


def _gather_phases(shard_r, gath, cast, send_sems, recv_sems, local_sems):
    n = len(shard_r)
    x, y, c = _mesh_pos()
    me, sibling = (x, y, c), (x, y, 1 - c)
    chips = [(1 - x, y), (x, 1 - y), (1 - x, 1 - y)]

    def own(a, k, to):
        return _remote(cast[a], gath[a].at[_slot(*me)], send_sems, recv_sems, 7 * a + k, to)

    def passed(a, k, block, to):
        blk = gath[a].at[_slot(*block)]
        return _remote(blk, blk, send_sems, recv_sems, 7 * a + k, to)

    def keep(a):
        return pltpu.make_async_copy(cast[a], gath[a].at[_slot(*me)], local_sems.at[a])

    def start():
        for a in range(n):
            def to16(r, a=a):
                cast[a][r, :] = shard_r[a][r, :].astype(BF16)

            _row_chunks(shard_r[a].shape[0], to16)
            keep(a).start()
            own(a, 0, sibling).start()
            for j, chip in enumerate(chips):
                own(a, 1 + j, (*chip, c)).start()

    def relay():
        for a in range(n):
            for j, chip in enumerate(chips):
                passed(a, 1 + j, (*chip, c), me).wait_recv()
                passed(a, 4 + j, (*chip, c), sibling).start()

    def finish():
        for a in range(n):
            passed(a, 0, sibling, me).wait_recv()
            for j, chip in enumerate(chips):
                passed(a, 4 + j, (*chip, 1 - c), me).wait_recv()
            own(a, 0, sibling).wait_send()
            for j, chip in enumerate(chips):
                own(a, 1 + j, (*chip, c)).wait_send()
                passed(a, 4 + j, (*chip, c), sibling).wait_send()
            keep(a).wait()

    return start, relay, finish


def _gather_operands(shards):
    n = len(shards)
    return ((pl.BlockSpec(memory_space=pl.ANY),) * n,
            tuple(jax.ShapeDtypeStruct((N_DEV,) + s.shape, BF16) for s in shards),
            [pltpu.VMEM(s.shape, BF16) for s in shards]
            + [pltpu.SemaphoreType.DMA((7 * n,)), pltpu.SemaphoreType.DMA((7 * n,)), pltpu.SemaphoreType.DMA((n,))])


def _hosted_reduce_phases(g16_r, g32_r, red, own16, recv1, send2, recv2, own32, s_send, s_recv, s_local):
    n = len(g16_r)
    x, y, c = _mesh_pos()
    sibling = (x, y, 1 - c)
    chips = [(1 - x, y), (x, 1 - y), (1 - x, 1 - y)]
    all_chips = [(x, y)] + chips

    def lvl1(a, j):
        return _remote(g16_r[a].at[_slot(*all_chips[j], 1 - c)], recv1[a].at[j], s_send, s_recv, 7 * a + j, sibling)

    def lvl2(a, j):
        return _remote(send2[a].at[j], recv2[a].at[j], s_send, s_recv, 7 * a + 4 + j, (*chips[j], c))

    def mine(a, j):
        if j == 3:
            only_mine = g32_r[a].shape[0] == 1
            return pltpu.make_async_copy(g32_r[a].at[0 if only_mine else _slot(x, y, c)], own32[a], s_local.at[4 * a + j])
        return pltpu.make_async_copy(g16_r[a].at[_slot(*chips[j], c)], own16[a].at[j], s_local.at[4 * a + j])

    def start():
        for a in range(n):
            for j in range(4):
                mine(a, j).start()
            for j in range(4):
                lvl1(a, j).start()

    def middle():
        for a in range(n):
            for j in range(4):
                mine(a, j).wait()
            for j in range(4):
                lvl1(a, j).wait_recv()

            def partials(r, a=a):
                red[a][r, :] = own32[a][r, :] + recv1[a][0, r, :].astype(F32)
                for j in range(3):
                    send2[a][j, r, :] = (own16[a][j, r, :].astype(F32) + recv1[a][1 + j, r, :].astype(F32)).astype(BF16)

            _row_chunks(own32[a].shape[0], partials)
            for j in range(3):
                lvl2(a, j).start()

    def total():
        for a in range(n):
            for j in range(3):
                lvl2(a, j).wait_recv()

            def add(r, a=a):
                g = red[a][r, :]
                for j in range(3):
                    g = g + recv2[a][j, r, :].astype(F32)
                red[a][r, :] = g

            _row_chunks(own32[a].shape[0], add)

    def finish():
        for a in range(n):
            for j in range(4):
                lvl1(a, j).wait_send()
            for j in range(3):
                lvl2(a, j).wait_send()

    return start, middle, total, finish


def _hosted_reduce_operands(g16, const_spec):
    n = len(g16)
    shard = [g.shape[1:] for g in g16]
    return ([pl.BlockSpec(memory_space=pl.ANY)] * (2 * n),
            tuple(const_spec(s) for s in shard),
            tuple(jax.ShapeDtypeStruct(s, F32) for s in shard),
            [pltpu.VMEM((3,) + s, BF16) for s in shard] + [pltpu.VMEM((4,) + s, BF16) for s in shard]
            + [pltpu.VMEM((3,) + s, BF16) for s in shard] + [pltpu.VMEM((3,) + s, BF16) for s in shard]
            + [pltpu.VMEM(s, F32) for s in shard]
            + [pltpu.SemaphoreType.DMA((7 * n,)), pltpu.SemaphoreType.DMA((7 * n,)), pltpu.SemaphoreType.DMA((4 * n,))])


def _in_proj(x2, g_pre, w_in, tm):
    t = x2.shape[0]

    def body(x_ref, g_ref, w_ref, u_ref, zs_ref, q_ref, k_ref, v_ref, za_ref):
        xv = x_ref[...]
        r = lax.rsqrt(jnp.mean(xv * xv, axis=-1, keepdims=True) + EPS)
        hn = xv * r * g_ref[...]
        proj = _mm_nt(hn, w_ref[...])
        u_ref[...] = proj[:, 0:512]
        zs_ref[...] = proj[:, 512:1024]
        q_ref[...] = proj[:, 1024:1536].astype(BF16)
        k_ref[...] = proj[:, 1536:1664].astype(BF16)
        v_ref[...] = proj[:, 1664:1792].astype(BF16)
        za_ref[...] = proj[:, 1792:2304]

    row = lambda w: pl.BlockSpec((tm, w), lambda i: (i, 0))
    return pl.pallas_call(
        body, name="in_proj", grid=(t // tm,),
        in_specs=[row(D_MODEL), _const_spec((1, D_MODEL)), _const_spec((D_IN, D_MODEL))],
        out_specs=(row(512), row(512), row(512), row(128), row(128), row(512)),
        out_shape=(jax.ShapeDtypeStruct((t, 512), F32),
                   jax.ShapeDtypeStruct((t, 512), F32), jax.ShapeDtypeStruct((t, 512), BF16),
                   jax.ShapeDtypeStruct((t, 128), BF16), jax.ShapeDtypeStruct((t, 128), BF16),
                   jax.ShapeDtypeStruct((t, 512), F32)),
        compiler_params=_tc_params(("arbitrary",)),
    )(x2, g_pre, w_in)


def _discretise(lr, li, ls):
    step = jnp.exp(ls)
    mag = jnp.exp(lr * step)
    ar = mag * jnp.cos(li * step)
    ai = mag * jnp.sin(li * step)
    den = lr * lr + li * li
    cr = ((ar - 1.0) * lr + ai * li) / den
    ci = (ai * lr - (ar - 1.0) * li) / den
    return step, ar, ai, den, cr, ci


def _per_channel(v):
    return jnp.broadcast_to(v[:, None, :], (SSM_G, SSM_P, SSM_N)).reshape(SSM_G * SSM_P, SSM_N)


def _tile_masks():
    r = lax.broadcasted_iota(jnp.int32, (CH_T, ST_T), 0) // SSM_P
    l = lax.broadcasted_iota(jnp.int32, (CH_T, ST_T), 1) // SSM_N
    lt = lax.broadcasted_iota(jnp.int32, (ST_T, CH_T), 0) // SSM_N
    rt = lax.broadcasted_iota(jnp.int32, (ST_T, CH_T), 1) // SSM_P
    rep = lax.broadcasted_iota(jnp.int32, (SSM_N, ST_T), 0) == lax.broadcasted_iota(jnp.int32, (SSM_N, ST_T), 1) % SSM_N
    rep_t = lax.broadcasted_iota(jnp.int32, (ST_T, SSM_N), 0) % SSM_N == lax.broadcasted_iota(jnp.int32, (ST_T, SSM_N), 1)
    return r == l, lt == rt, rep, rep_t


def _ssm_prep(lam_re, lam_im, log_step, b_re, b_im, c_re, c_im, seg):
    def work(in_refs, out_refs):
        lr_ref, li_ref, ls_ref, br_ref, bi_ref, cre_ref, cim_ref, lrr_ref, lir_ref, lsr_ref = in_refs
        ar_ref, ai_ref, pr_ref, pi_ref, bcat_ref, bcat_t_ref, ccat_ref, ccat_t_ref = out_refs
        _, _, _, _, cr, ci = _discretise(lr_ref[...], li_ref[...], ls_ref[...])
        cr, ci = _per_channel(cr), _per_channel(ci)
        br, bi = br_ref[...], bi_ref[...]
        bb_re = cr * br - ci * bi
        bb_im = cr * bi + ci * br
        same, same_t, rep, rep_t = _tile_masks()
        rep, rep_t = rep.astype(BF16), rep_t.astype(BF16)
        for j in range(N_GT):
            rows = slice(j * CH_T, (j + 1) * CH_T)
            for wide, tall, parts in ((bcat_ref, bcat_t_ref, (bb_re[rows], bb_im[rows])),
                                      (ccat_t_ref, ccat_ref, (cre_ref[rows, :], -cim_ref[rows, :]))):
                for k, part in enumerate(parts):
                    p16 = part.astype(BF16)
                    wide[j, :, k * ST_T:(k + 1) * ST_T] = jnp.where(same, _mm(p16, rep), 0.0).astype(BF16)
                    tall[j, k * ST_T:(k + 1) * ST_T, :] = jnp.where(same_t, _mm_nt(rep_t, p16), 0.0).astype(BF16)
        stepr = jnp.exp(lsr_ref[...])
        mag = jnp.exp(lrr_ref[...] * stepr)
        a_r, a_i = mag * jnp.cos(lir_ref[...] * stepr), mag * jnp.sin(lir_ref[...] * stepr)
        p_r, p_i = a_r, a_i
        for k in range(8):
            pr_ref[k:k + 1, :] = p_r
            pi_ref[k:k + 1, :] = p_i
            p_r, p_i = p_r * a_r - p_i * a_i, p_r * a_i + p_i * a_r
        n = 8
        while n < seg:
            tr, ti = pr_ref[n - 1:n, :], pi_ref[n - 1:n, :]
            xr, xi = pr_ref[0:n, :], pi_ref[0:n, :]
            pr_ref[n:2 * n, :] = xr * tr - xi * ti
            pi_ref[n:2 * n, :] = xr * ti + xi * tr
            n *= 2
        ar_ref[...] = pr_ref[0:1, :]
        ai_ref[...] = pi_ref[0:1, :]

    row = jax.ShapeDtypeStruct((1, N_STATE), F32)
    pw = jax.ShapeDtypeStruct((seg, N_STATE), F32)
    wide = jax.ShapeDtypeStruct((N_GT, CH_T, 2 * ST_T), BF16)
    tall = jax.ShapeDtypeStruct((N_GT, 2 * ST_T, CH_T), BF16)
    step_row = jnp.broadcast_to(log_step, (SSM_G, SSM_N)).reshape(1, N_STATE)
    inputs = (lam_re, lam_im, log_step, b_re, b_im, c_re, c_im, lam_re.reshape(1, N_STATE),
              lam_im.reshape(1, N_STATE), step_row)
    return work, inputs, (row, row, pw, pw, wide, tall, tall, wide)


def _seg_rows(t):
    if isinstance(t, int):
        return pl.ds(t * N_SEG, N_SEG)
    return pl.ds(pl.multiple_of(t * N_SEG, N_SEG), N_SEG)


def _scan_forward(xs, a_re, a_im, pw_re, pw_im, cs, seg):
    are = jnp.broadcast_to(a_re, (N_SEG, ST_T))
    aim = jnp.broadcast_to(a_im, (N_SEG, ST_T))

    def steps(k, carry):
        xr, xi = carry
        for j in range(SCAN_UNROLL):
            r = pl.multiple_of((k * SCAN_UNROLL + j) * N_SEG, N_SEG)
            nr = are * xr - aim * xi + xs[pl.ds(r, N_SEG), 0:ST_T]
            ni = are * xi + aim * xr + xs[pl.ds(r, N_SEG), ST_T:2 * ST_T]
            xs[pl.ds(r, N_SEG), 0:ST_T] = nr
            xs[pl.ds(r, N_SEG), ST_T:2 * ST_T] = ni
            xr, xi = nr, ni
        return xr, xi

    zero = jnp.zeros((N_SEG, ST_T), F32)
    fr, fi = lax.fori_loop(0, seg // SCAN_UNROLL, steps, (zero, zero))
    sr, si = pw_re[seg - 1:seg, :], pw_im[seg - 1:seg, :]
    cr = jnp.zeros((1, ST_T), F32)
    ci = jnp.zeros((1, ST_T), F32)
    cs[0:1, :] = cr
    cs[8:9, :] = ci
    for s in range(1, N_SEG):
        ncr = sr * cr - si * ci + fr[s - 1:s, :]
        nci = sr * ci + si * cr + fi[s - 1:s, :]
        cr, ci = ncr, nci
        cs[s:s + 1, :] = cr
        cs[8 + s:9 + s, :] = ci
    car, cai = cs[0:8, :], cs[8:16, :]

    def fix(t, _):
        r = pl.multiple_of(t * N_SEG, N_SEG)
        pr, pi = pw_re[pl.ds(t, 1), :], pw_im[pl.ds(t, 1), :]
        xs[pl.ds(r, N_SEG), 0:ST_T] = xs[pl.ds(r, N_SEG), 0:ST_T] + (pr * car - pi * cai)
        xs[pl.ds(r, N_SEG), ST_T:2 * ST_T] = xs[pl.ds(r, N_SEG), ST_T:2 * ST_T] + (pr * cai + pi * car)
        return 0

    lax.fori_loop(0, seg, fix, 0, unroll=SCAN_UNROLL)


def _interleave(src, dst, seg):
    for s in range(N_SEG):
        dst[pl.ds(s, seg, stride=N_SEG), :] = src[s]


def _deinterleave(src, seg, s):
    return src[pl.ds(s, seg, stride=N_SEG), :]


def _ssm_forward(u, bcat, ccat, a_re, a_im, pw_re, pw_im, d_row, late, seg):
    bl = u.shape[0]
    rows = N_SEG * seg
    n = len(late)
    steps = bl * N_GT

    def body(*refs):
        u_ref, b_ref, c_ref, ar_ref, ai_ref, pr_ref, pi_ref, d_ref = refs[:8]
        late_r = refs[8:8 + n]
        y_ref, xs_ref, cs_ref = refs[8 + n:11 + n]
        gath, cast = refs[11 + n:11 + 2 * n], refs[11 + 2 * n:11 + 3 * n]
        send_sems, recv_sems, local_sems, ui, yi = refs[11 + 3 * n:]
        step = pl.program_id(0) * N_GT + pl.program_id(1)
        start, relay, finish = _gather_phases(late_r, gath, cast, send_sems, recv_sems, local_sems)
        pl.when(step == 0)(start)
        _interleave(u_ref.at[0], ui, seg)
        u = ui[...]
        xs, cs = xs_ref.at[0, 0], cs_ref.at[0, 0]
        xs[...] = _mm(u, b_ref[0])
        _scan_forward(xs, ar_ref[...], ai_ref[...], pr_ref, pi_ref, cs, seg)
        yi[...] = _mm(xs[...], c_ref[0]) + d_ref[...] * u
        for s in range(N_SEG):
            y_ref[0, s] = _deinterleave(yi, seg, s)
        pl.when(step == steps // 2)(relay)
        pl.when(step == steps - 1)(finish)

    state = lambda r, c: pl.BlockSpec((1, 1, r, c), lambda b, j: (b, j, 0, 0))
    act = pl.BlockSpec((1, N_SEG, seg, CH_T), lambda b, j: (b, 0, 0, j))
    g_specs, g_shapes, g_scratch = _gather_operands(late)
    res = pl.pallas_call(
        body, name="ssm_forward", grid=(bl, N_GT),
        in_specs=[act,
                  pl.BlockSpec((1, CH_T, 2 * ST_T), lambda b, j: (j, 0, 0)),
                  pl.BlockSpec((1, 2 * ST_T, CH_T), lambda b, j: (j, 0, 0)),
                  pl.BlockSpec((1, ST_T), lambda b, j: (0, j)), pl.BlockSpec((1, ST_T), lambda b, j: (0, j)),
                  pl.BlockSpec((seg, ST_T), lambda b, j: (0, j)), pl.BlockSpec((seg, ST_T), lambda b, j: (0, j)),
                  pl.BlockSpec((1, CH_T), lambda b, j: (0, j))]
        + [pl.BlockSpec(s.shape, lambda b, j: (0, 0)) for s in late],
        out_specs=(act, state(rows, 2 * ST_T), state(16, ST_T)) + g_specs,
        out_shape=(jax.ShapeDtypeStruct((bl, N_SEG, seg, D_SSM), F32),
                   jax.ShapeDtypeStruct((bl, N_GT, rows, 2 * ST_T), F32),
                   jax.ShapeDtypeStruct((bl, N_GT, 16, ST_T), F32)) + g_shapes,
        scratch_shapes=g_scratch + [pltpu.VMEM((rows, CH_T), F32), pltpu.VMEM((rows, CH_T), F32)],
        compiler_params=_tc_params(("arbitrary", "arbitrary")),
    )(u, bcat, ccat, a_re, a_im, pw_re, pw_im, d_row, *late)
    return res[:3], list(res[3:])


def _ssm_backward(u, dy, states, carries, bcat_t, ccat_t, a_re, a_im, pw_re, pw_im, d_row, late16, late32, seg):
    bl = u.shape[0]
    rows = N_SEG * seg
    n = len(late16)
    grid_steps = N_GT * bl

    def body(*refs):
        u_ref, dy_ref, xs_ref, cs_ref, bt_ref, ct_ref, ar_ref, ai_ref, pr_ref, pi_ref, d_ref = refs[:11]
        g16_r, g32_r = refs[11:11 + n], refs[11 + n:11 + 2 * n]
        du_ref, db_ref, dc_ref, dar_ref, dai_ref, dd_ref = refs[11 + 2 * n:17 + 2 * n]
        red = refs[17 + 2 * n:17 + 3 * n]
        own16, recv1, send2, recv2, own32 = (refs[17 + 3 * n + k * n:17 + 3 * n + (k + 1) * n] for k in range(5))
        s_send, s_recv, s_local, ls, cl, ui, dyi, dui = refs[17 + 8 * n:]
        b = pl.program_id(1)
        step = pl.program_id(0) * bl + b
        start, middle, total, finish = _hosted_reduce_phases(g16_r, g32_r, red, own16, recv1, send2, recv2, own32,
                                                             s_send, s_recv, s_local)
        pl.when(step == 0)(start)
        pl.when(step == grid_steps // 4)(middle)
        pl.when(step == (grid_steps * 3) // 4)(total)
        pl.when(step == grid_steps - 1)(finish)
        _interleave(u_ref.at[0], ui, seg)
        _interleave(dy_ref.at[0], dyi, seg)
        u = ui[...]
        dy = dyi[...]
        xs, cs = xs_ref.at[0, 0], cs_ref.at[0, 0]
        ls[...] = _mm(dy, ct_ref[0])
        are = jnp.broadcast_to(ar_ref[...], (N_SEG, ST_T))
        aim = jnp.broadcast_to(ai_ref[...], (N_SEG, ST_T))

        def steps(k, carry):
            lr, li = carry
            for j in range(SCAN_UNROLL):
                r = pl.multiple_of((seg - 1 - (k * SCAN_UNROLL + j)) * N_SEG, N_SEG)
                nr = are * lr + aim * li + ls[pl.ds(r, N_SEG), 0:ST_T]
                ni = are * li - aim * lr + ls[pl.ds(r, N_SEG), ST_T:2 * ST_T]
                ls[pl.ds(r, N_SEG), 0:ST_T] = nr
                ls[pl.ds(r, N_SEG), ST_T:2 * ST_T] = ni
                lr, li = nr, ni
            return lr, li

        zero = jnp.zeros((N_SEG, ST_T), F32)
        fr, fi = lax.fori_loop(0, seg // SCAN_UNROLL, steps, (zero, zero))
        sr, si = pr_ref[seg - 1:seg, :], pi_ref[seg - 1:seg, :]
        cr = jnp.zeros((1, ST_T), F32)
        ci = jnp.zeros((1, ST_T), F32)
        cl[7:8, :] = cr
        cl[15:16, :] = ci
        for s in range(N_SEG - 2, -1, -1):
            ncr = sr * cr + si * ci + fr[s + 1:s + 2, :]
            nci = sr * ci - si * cr + fi[s + 1:s + 2, :]
            cr, ci = ncr, nci
            cl[s:s + 1, :] = cr
            cl[8 + s:9 + s, :] = ci
        clr, cli = cl[0:8, :], cl[8:16, :]

        def fix_rows(rows, t, xpr, xpi, acc):
            dr, di = acc
            pr, pi = pr_ref[pl.ds(seg - 1 - t, 1), :], pi_ref[pl.ds(seg - 1 - t, 1), :]
            lr = ls[rows, 0:ST_T] + (pr * clr + pi * cli)
            li = ls[rows, ST_T:2 * ST_T] + (pr * cli - pi * clr)
            ls[rows, 0:ST_T] = lr
            ls[rows, ST_T:2 * ST_T] = li
            return dr + (lr * xpr + li * xpi), di + (li * xpr - lr * xpi)

        def fix_at(t, acc):
            prev = _seg_rows(t - 1)
            return fix_rows(_seg_rows(t), t, xs[prev, 0:ST_T], xs[prev, ST_T:2 * ST_T], acc)

        def fix(k, acc):
            for j in range(SCAN_UNROLL):
                acc = fix_at(k * SCAN_UNROLL + j, acc)
            return acc

        acc = fix_rows(pl.ds(0, N_SEG), 0, cs[0:8, :], cs[8:16, :], (zero, zero))
        for t in range(1, SCAN_UNROLL):
            acc = fix_at(t, acc)
        dr, di = lax.fori_loop(1, seg // SCAN_UNROLL, fix, acc)
        dar = jnp.sum(dr, axis=0, keepdims=True)
        dai = jnp.sum(di, axis=0, keepdims=True)
        lall = ls[...]
        dui[...] = _mm(lall, bt_ref[0]) + d_ref[...] * dy
        for s in range(N_SEG):
            du_ref[0, s] = _deinterleave(dui, seg, s).astype(BF16)
        dbp = _mm_tn(u, lall)
        dcp = _mm_tn(dy, xs[...])
        ddp = jnp.sum(dy * u, axis=0, keepdims=True)

        @pl.when(b == 0)
        def _():
            db_ref[0] = dbp
            dc_ref[0] = dcp
            dar_ref[...] = dar
            dai_ref[...] = dai
            dd_ref[...] = ddp

        @pl.when(b != 0)
        def _():
            db_ref[0] += dbp
            dc_ref[0] += dcp
            dar_ref[...] += dar
            dai_ref[...] += dai
            dd_ref[...] += ddp

    tile3 = lambda r, c: pl.BlockSpec((1, r, c), lambda j, b: (j, 0, 0))
    lane = lambda r, c: pl.BlockSpec((r, c), lambda j, b: (0, j))
    act = pl.BlockSpec((1, N_SEG, seg, CH_T), lambda j, b: (b, 0, 0, j))
    state = lambda r, c: pl.BlockSpec((1, 1, r, c), lambda j, b: (b, j, 0, 0))
    r_in, r_out, r_shapes, r_scratch = _hosted_reduce_operands(late16, lambda s: pl.BlockSpec(s, lambda j, b: (0, 0)))
    res = pl.pallas_call(
        body, name="ssm_backward", grid=(N_GT, bl),
        in_specs=[act, act, state(rows, 2 * ST_T), state(16, ST_T), tile3(2 * ST_T, CH_T), tile3(CH_T, 2 * ST_T),
                  lane(1, ST_T), lane(1, ST_T), lane(seg, ST_T), lane(seg, ST_T), lane(1, CH_T)] + r_in,
        out_specs=(act, tile3(CH_T, 2 * ST_T), tile3(CH_T, 2 * ST_T), lane(1, ST_T), lane(1, ST_T), lane(1, CH_T))
        + r_out,
        out_shape=(jax.ShapeDtypeStruct((bl, N_SEG, seg, D_SSM), BF16),
                   jax.ShapeDtypeStruct((N_GT, CH_T, 2 * ST_T), F32), jax.ShapeDtypeStruct((N_GT, CH_T, 2 * ST_T), F32),
                   jax.ShapeDtypeStruct((1, N_STATE), F32), jax.ShapeDtypeStruct((1, N_STATE), F32),
                   jax.ShapeDtypeStruct((1, D_SSM), F32)) + r_shapes,
        scratch_shapes=r_scratch + [pltpu.VMEM((rows, 2 * ST_T), F32), pltpu.VMEM((16, ST_T), F32)]
        + [pltpu.VMEM((rows, CH_T), F32)] * 3,
        compiler_params=_tc_params(("arbitrary", "arbitrary")),
    )(u, dy, states, carries, bcat_t, ccat_t, a_re, a_im, pw_re, pw_im, d_row, *late16, *late32)
    return res[:6], list(res[6:])


def _ssm_param_grads(lam_re, lam_im, log_step, b_re, b_im, da_re, da_im, d_bcat, d_ccat_t):
    def body(lr_ref, li_ref, ls_ref, br_ref, bi_ref, gar_ref, gai_ref, gbcat_ref, gccat_ref,
             dlr_ref, dli_ref, dls_ref, dbr_ref, dbi_ref, dcr_ref, dci_ref, gbr_s, gbi_s):
        same, _, _, rep_t = _tile_masks()
        rep_t = rep_t.astype(F32)
        for j in range(N_GT):
            rows = slice(j * CH_T, (j + 1) * CH_T)
            for src, dsts in ((gbcat_ref, (gbr_s, gbi_s)), (gccat_ref, (dcr_ref, dci_ref))):
                for k, dst in enumerate(dsts):
                    blk = jnp.where(same, src[j, :, k * ST_T:(k + 1) * ST_T], 0.0)
                    dst[rows, :] = jnp.dot(blk, rep_t, precision=lax.Precision.HIGHEST, preferred_element_type=F32)
        dci_ref[...] = -dci_ref[...]
        lr, li = lr_ref[...], li_ref[...]
        step, ar, ai, den, cr, ci = _discretise(lr, li, ls_ref[...])
        crb, cib = _per_channel(cr), _per_channel(ci)
        br, bi = br_ref[...], bi_ref[...]
        gbr, gbi = gbr_s[...], gbi_s[...]
        dbr_ref[...] = crb * gbr + cib * gbi
        dbi_ref[...] = crb * gbi - cib * gbr
        over_channels = lambda t: jnp.sum(t.reshape(SSM_G, SSM_P, SSM_N), axis=1)
        gcr = over_channels(br * gbr + bi * gbi)
        gci = over_channels(br * gbi - bi * gbr)
        ilr, ili = lr / den, -li / den
        gar = gar_ref[...] + (ilr * gcr + ili * gci)
        gai = gai_ref[...] + (ilr * gci - ili * gcr)
        qr, qi = cr * ilr - ci * ili, cr * ili + ci * ilr
        glr = -(qr * gcr + qi * gci)
        gli = -(qr * gci - qi * gcr)
        gwr = ar * gar + ai * gai
        gwi = ar * gai - ai * gar
        dlr_ref[...] = glr + step * gwr
        dli_ref[...] = gli + step * gwi
        dls_ref[...] = jnp.sum(lr * gwr + li * gwi, axis=-1, keepdims=True) * step

    lam = jax.ShapeDtypeStruct((SSM_G, SSM_N), F32)
    mat = jax.ShapeDtypeStruct((SSM_G * SSM_P, SSM_N), F32)
    vm = pl.BlockSpec(memory_space=pltpu.VMEM)
    return pl.pallas_call(
        body, name="ssm_param_grads", out_shape=(lam, lam, jax.ShapeDtypeStruct((SSM_G, 1), F32), mat, mat, mat, mat),
        in_specs=[vm] * 9, out_specs=(vm,) * 7,
        scratch_shapes=[pltpu.VMEM((SSM_G * SSM_P, SSM_N), F32), pltpu.VMEM((SSM_G * SSM_P, SSM_N), F32)],
    )(lam_re, lam_im, log_step, b_re, b_im, da_re, da_im, d_bcat, d_ccat_t)


ROWS4 = Q_PER_KV * ATT_BLOCK
ATT_FWD_STACK = 1


def _att_dist_mask(first_block):
    qi = lax.broadcasted_iota(jnp.int32, (ROWS4, 2 * ATT_BLOCK), 0) & (ATT_BLOCK - 1)
    si = lax.broadcasted_iota(jnp.int32, (ROWS4, 2 * ATT_BLOCK), 1)
    dist = qi + ATT_BLOCK - si
    valid = (dist >= 0) & (dist < ATT_BLOCK) & ((si >= ATT_BLOCK) | jnp.logical_not(first_block))
    return dist.astype(F32), valid


def _stack_heads(x, kv):
    return jnp.concatenate([x[:, (kv * Q_PER_KV + g) * HEAD_DIM:(kv * Q_PER_KV + g + 1) * HEAD_DIM]
                            for g in range(Q_PER_KV)], axis=0)


def _stack_cols(x, kv):
    return jnp.concatenate([x[:, kv * Q_PER_KV + g:kv * Q_PER_KV + g + 1] for g in range(Q_PER_KV)], axis=0)


def _per_head_col(vals):
    return jnp.concatenate([jnp.full((ATT_BLOCK, 1), v, F32) for v in vals], axis=0)


def _attn_forward(q, k, v, sinks, bl, nb):
    t = q.shape[0]

    def body(sink_ref, q_ref, kp_ref, kc_ref, vp_ref, vc_ref, o_ref, lse_ref):
        i = pl.program_id(1)
        dist4, valid4 = _att_dist_mask(i == 0)
        rows2 = ATT_FWD_STACK * ATT_BLOCK
        dist, valid = dist4[0:rows2, :], valid4[0:rows2, :]
        kk = jnp.concatenate([kp_ref[...], kc_ref[...]], axis=0)
        vv = jnp.concatenate([vp_ref[...], vc_ref[...]], axis=0)
        qv = q_ref[...]
        col = lambda vals: jnp.concatenate([jnp.full((ATT_BLOCK, 1), v, F32) for v in vals], axis=0)
        stacks = [range(h0, h0 + ATT_FWD_STACK) for h0 in range(0, N_HEADS, ATT_FWD_STACK)]
        kv_cols = lambda heads: slice(heads[0] // Q_PER_KV * HEAD_DIM, (heads[0] // Q_PER_KV + 1) * HEAD_DIM)
        scores = [_mm_nt(jnp.concatenate([qv[:, h * HEAD_DIM:(h + 1) * HEAD_DIM] for h in heads], axis=0),
                         kk[:, kv_cols(heads)]) for heads in stacks]
        softmaxes = []
        for heads, qk in zip(stacks, scores):
            slope = col([2.0 ** (-(h + 1)) for h in heads])
            sink = col([sink_ref[h] for h in heads])
            s = jnp.where(valid, qk * ATT_SCALE - slope * dist, NEG_INF)
            m = jnp.maximum(jnp.max(s, axis=-1, keepdims=True), sink)
            e = jnp.exp(s - m)
            den = jnp.sum(e, axis=-1, keepdims=True) + jnp.exp(sink - m)
            softmaxes.append((e.astype(BF16), 1.0 / den, m + jnp.log(den)))
        for heads, (e, inv_den, lse) in zip(stacks, softmaxes):
            o = _mm(e, vv[:, kv_cols(heads)]) * inv_den
            for g, h in enumerate(heads):
                rows = slice(g * ATT_BLOCK, (g + 1) * ATT_BLOCK)
                o_ref[:, h * HEAD_DIM:(h + 1) * HEAD_DIM] = o[rows, :]
                lse_ref[:, h:h + 1] = lse[rows, :]

    cur = lambda w: pl.BlockSpec((ATT_BLOCK, w), lambda b, i: (b * nb + i, 0))
    prev = lambda w: pl.BlockSpec((ATT_BLOCK, w), lambda b, i: (b * nb + jnp.maximum(i - 1, 0), 0))
    return pl.pallas_call(
        body, name="attn_forward", grid=(bl, nb),
        in_specs=[pl.BlockSpec(memory_space=pltpu.SMEM), cur(512), prev(128), cur(128), prev(128), cur(128)],
        out_specs=(cur(512), cur(N_HEADS)),
        out_shape=(jax.ShapeDtypeStruct((t, D_ATTN), F32), jax.ShapeDtypeStruct((t, N_HEADS), F32)),
        compiler_params=_tc_params(("arbitrary", "arbitrary")),
    )(sinks, q, k, k, v, v)


def _attn_backward(q, k, v, o, do, lse, sinks, bl, nb):
    t = q.shape[0]

    def body(sink_ref, qc_ref, kp_ref, kc_ref, vp_ref, vc_ref, oc_ref, doc_ref, lc_ref,
             dq_ref, dk_ref, dv_ref, ds_ref, dk_carry, dv_carry):
        b, i = pl.program_id(0), pl.program_id(1)
        live = i < nb

        @pl.when(i == 0)
        def _():
            dk_carry[...] = jnp.zeros((ATT_BLOCK, KV_HEADS * HEAD_DIM), F32)
            dv_carry[...] = jnp.zeros((ATT_BLOCK, KV_HEADS * HEAD_DIM), F32)

        dist, valid = _att_dist_mask(i == 0)
        valid = valid & live
        kk = jnp.concatenate([kp_ref[...], kc_ref[...]], axis=0)
        vv = jnp.concatenate([vp_ref[...], vc_ref[...]], axis=0)
        qc, oc, doc, lc = qc_ref[...], oc_ref[...], doc_ref[...], lc_ref[...]
        dsink_cols, dq_parts, dk_t, dv_t = [], [], [], []
        for kv in range(KV_HEADS):
            heads = range(kv * Q_PER_KV, (kv + 1) * Q_PER_KV)
            cols = slice(kv * HEAD_DIM, (kv + 1) * HEAD_DIM)
            kh, vh = kk[:, cols], vv[:, cols]
            slope = _per_head_col([2.0 ** (-(h + 1)) for h in heads])
            sink = _per_head_col([sink_ref[h] for h in heads])
            q4, do4 = _stack_heads(qc, kv), _stack_heads(doc, kv)
            delta = jnp.sum(do4 * _stack_heads(oc, kv), axis=-1, keepdims=True)
            lse4 = _stack_cols(lc, kv)
            s = _mm_nt(q4, kh) * ATT_SCALE - slope * dist
            p = jnp.where(valid, jnp.exp(s - lse4), 0.0)
            dsc = p * (_mm_nt(do4, vh) - delta)
            dq4 = _mm(dsc, kh) * ATT_SCALE
            dk_t.append(_mm_tn(q4, dsc) * ATT_SCALE)
            dv_t.append(_mm_tn(do4, p))
            dsink4 = jnp.where(live, jnp.exp(sink - lse4) * delta, 0.0)
            for g, h in enumerate(heads):
                rows = slice(g * ATT_BLOCK, (g + 1) * ATT_BLOCK)
                dq_parts.append((h, dq4[rows, :]))
                dsink_cols.append(-jnp.sum(dsink4[rows, :], axis=0, keepdims=True))
        dsink = jnp.concatenate(dsink_cols, axis=1)
        for out_ref, carry, parts in ((dk_ref, dk_carry, dk_t), (dv_ref, dv_carry, dv_t)):
            both = jnp.concatenate(parts, axis=0)
            out_ref[...] = (carry[...] + both[:, 0:ATT_BLOCK]).T
            carry[...] = both[:, ATT_BLOCK:]

        @pl.when(live)
        def _():
            for h, part in dq_parts:
                dq_ref[:, h * HEAD_DIM:(h + 1) * HEAD_DIM] = part

        @pl.when((b == 0) & (i == 0))
        def _():
            ds_ref[...] = dsink

        @pl.when((b != 0) | (i != 0))
        def _():
            ds_ref[...] += dsink

    cur_i = lambda i: jnp.minimum(i, nb - 1)
    cur = lambda w: pl.BlockSpec((ATT_BLOCK, w), lambda b, i: (b * nb + cur_i(i), 0))
    prev = lambda w: pl.BlockSpec((ATT_BLOCK, w), lambda b, i: (b * nb + jnp.maximum(cur_i(i) - 1, 0), 0))
    behind = lambda w: pl.BlockSpec((ATT_BLOCK, w), lambda b, i: (b * nb + jnp.maximum(i - 1, 0), 0))
    return pl.pallas_call(
        body, name="attn_backward", grid=(bl, nb + 1),
        in_specs=[pl.BlockSpec(memory_space=pltpu.SMEM), cur(512), prev(128), cur(128), prev(128), cur(128),
                  cur(512), cur(512), cur(N_HEADS)],
        out_specs=(cur(512), behind(128), behind(128), pl.BlockSpec((1, N_HEADS), lambda b, i: (0, 0))),
        out_shape=(jax.ShapeDtypeStruct((t, D_ATTN), F32), jax.ShapeDtypeStruct((t, 128), F32),
                   jax.ShapeDtypeStruct((t, 128), F32), jax.ShapeDtypeStruct((1, N_HEADS), F32)),
        scratch_shapes=[pltpu.VMEM((ATT_BLOCK, KV_HEADS * HEAD_DIM), F32), pltpu.VMEM((ATT_BLOCK, KV_HEADS * HEAD_DIM), F32)],
        compiler_params=_tc_params(("arbitrary", "arbitrary")),
    )(sinks, q, k, k, v, v, o, do, lse)


def _mix_forward_backward(x2, y2, z_ssm, attn, z_attn, p2, target2, w_glu, b_glu, w_out, g_post, w_gate, b_gate,
                          w_proj, tm):
    t = x2.shape[0]

    def body(x_ref, y_ref, zs_ref, at_ref, za_ref, p_ref, tg_ref,
             wglu_ref, bglu_ref, wout_ref, gpost_ref, wgate_ref, bgate_ref, wproj_ref,
             loss_ref, dh1_ref, dy_ref, dzs_ref, dat_ref, dza_ref,
             dwglu_own_ref, dbglu_ref, dwout_own_ref, dgpost_ref, dwgate_own_ref, dbgate_ref, dwproj_ref,
             dwout16_ref, dwgate16_ref, dwproj16_ref, dwglu16_ref, dwglu_ref, dwout_ref, dwgate_ref):
        i = pl.program_id(0)
        gpost = gpost_ref[...]

        @pl.when(i == 0)
        def _():
            for ref in (dwglu_ref, dbglu_ref, dwout_ref, dgpost_ref, dwgate_ref, dbgate_ref, dwproj_ref, loss_ref):
                ref[...] = jnp.zeros(ref.shape, F32)

        def chain(rows):
            y = y_ref[rows, :]
            u3 = GELU_C * (y + GELU_K * y * y * y)
            th = jnp.tanh(u3)
            gl = 0.5 * y * (1.0 + th)
            a = _mm(gl, wglu_ref[...]) + bglu_ref[...]
            sa = _sigmoid(a)
            glu = gl * sa
            zs = zs_ref[rows, :]
            sgs = _sigmoid(zs)
            ssm_out = glu * (zs * sgs)
            za = za_ref[rows, :]
            sga = _sigmoid(za)
            at = at_ref[rows, :]
            attn_out = at * (za * sga)
            cat = jnp.concatenate([ssm_out, attn_out], axis=-1).astype(BF16)
            mixed = _mm(cat, wout_ref[...])
            r2 = lax.rsqrt(jnp.mean(mixed * mixed, axis=-1, keepdims=True) + EPS)
            nhat = mixed * r2
            h1 = x_ref[rows, :] + nhat * gpost
            gate = _sigmoid(_mm(h1, wgate_ref[...]) + bgate_ref[...])
            pv = p_ref[rows, :]
            pp = _mm(pv, wproj_ref[...])
            h2 = h1 + gate * pp
            err = h2 - tg_ref[rows, :]
            loss_part = jnp.sum(jnp.sum(err * err, axis=-1, keepdims=True), axis=0, keepdims=True) * (0.5 / D_MODEL)
            dh2 = err * (1.0 / D_MODEL)
            dgp = dh2 * pp * gate * (1.0 - gate)
            dpp = dh2 * gate
            dh1 = dh2 + _mm_nt(dgp, wgate_ref[...])
            dwproj_ref[...] += _mm_tn(pv, dpp)
            dwgate_ref[...] += _mm_tn(h1, dgp)
            dh1_ref[rows, :] = dh1
            dnhat = dh1 * gpost
            dmixed = r2 * (dnhat - nhat * jnp.mean(dnhat * nhat, axis=-1, keepdims=True))
            dcat = _mm_nt(dmixed, wout_ref[...])
            dwout_ref[...] += _mm_tn(cat, dmixed)
            dso, dao = dcat[:, 0:D_SSM], dcat[:, D_SSM:]
            dat_ref[rows, :] = dao * (za * sga)
            dza_ref[rows, :] = (dao * at * (sga * (1.0 + za * (1.0 - sga)))).astype(BF16)
            dzs_ref[rows, :] = (dso * glu * (sgs * (1.0 + zs * (1.0 - sgs)))).astype(BF16)
            dglu = dso * (zs * sgs)
            da = dglu * gl * sa * (1.0 - sa)
            dgl = dglu * sa + _mm_nt(da, wglu_ref[...])
            dwglu_ref[...] += _mm_tn(gl, da)
            dgelu = 0.5 * (1.0 + th) + 0.5 * y * (1.0 - th * th) * (GELU_C * (1.0 + 3.0 * GELU_K * y * y))
            dy_ref[rows, :] = dgl * dgelu
            dbglu_ref[...] += jnp.sum(da, axis=0, keepdims=True)
            dgpost_ref[...] += jnp.sum(dh1 * nhat, axis=0, keepdims=True)
            dbgate_ref[...] += jnp.sum(dgp, axis=0, keepdims=True)
            loss_ref[...] += loss_part

        chain(slice(None))

        @pl.when(i == t // tm - 1)
        def _():
            for ref16, ref in ((dwout16_ref, dwout_ref), (dwgate16_ref, dwgate_ref), (dwproj16_ref, dwproj_ref),
                               (dwglu16_ref, dwglu_ref)):
                def to16(r, ref16=ref16, ref=ref):
                    ref16[r, :] = ref[r, :].astype(BF16)

                _row_chunks(ref.shape[0], to16)
            for ref, own_ref in ((dwglu_ref, dwglu_own_ref), (dwout_ref, dwout_own_ref), (dwgate_ref, dwgate_own_ref)):
                _copy_owned_rows(ref, own_ref)

    row = lambda w: pl.BlockSpec((tm, w), lambda i: (i, 0))
    acc = lambda *shape, dt=F32: (_const_spec(shape), jax.ShapeDtypeStruct(shape, dt))
    accs = [acc(1, D_SSM // N_DEV, D_SSM), acc(1, D_SSM), acc(1, D_MODEL // N_DEV, D_MODEL), acc(1, D_MODEL),
            acc(1, D_MODEL // N_DEV, D_MODEL), acc(1, D_MODEL), acc(D_PLE, D_MODEL),
            acc(D_MODEL, D_MODEL, dt=BF16), acc(D_MODEL, D_MODEL, dt=BF16), acc(D_PLE, D_MODEL, dt=BF16),
            acc(D_SSM, D_SSM, dt=BF16)]
    return pl.pallas_call(
        body, name="mix_forward_backward", grid=(t // tm,),
        in_specs=[row(D_MODEL), row(512), row(512), row(512), row(512), row(D_PLE), row(D_MODEL),
                  _const_spec((D_SSM, D_SSM)), _const_spec((1, D_SSM)), _const_spec((D_MODEL, D_MODEL)),
                  _const_spec((1, D_MODEL)), _const_spec((D_MODEL, D_MODEL)), _const_spec((1, D_MODEL)),
                  _const_spec((D_PLE, D_MODEL))],
        out_specs=(_const_spec((1, 1)), row(D_MODEL), row(512), row(512), row(512), row(512))
        + tuple(a[0] for a in accs),
        out_shape=(jax.ShapeDtypeStruct((1, 1), F32), jax.ShapeDtypeStruct((t, D_MODEL), F32),
                   jax.ShapeDtypeStruct((t, 512), F32),
                   jax.ShapeDtypeStruct((t, 512), BF16), jax.ShapeDtypeStruct((t, 512), F32),
                   jax.ShapeDtypeStruct((t, 512), BF16)) + tuple(a[1] for a in accs),
        scratch_shapes=[pltpu.VMEM((D_SSM, D_SSM), F32), pltpu.VMEM((D_MODEL, D_MODEL), F32),
                        pltpu.VMEM((D_MODEL, D_MODEL), F32)],
        compiler_params=_tc_params(("arbitrary",)),
    )(x2, y2, z_ssm, attn, z_attn, p2, target2, w_glu, b_glu, w_out, g_post, w_gate, b_gate, w_proj)


def _in_backward(x2, dh1, du, dz_ssm, dq, dk, dv, dz_attn, g_pre, w_in, tm):
    t = x2.shape[0]

    def body(x_ref, dh1_ref, du_ref, dzs_ref, dq_ref, dk_ref, dv_ref, dza_ref, g_ref, w_ref,
             gx_ref, dw_own_ref, dg_ref, dw16_ref, dw_ref):
        i = pl.program_id(0)

        @pl.when(i == 0)
        def _():
            dw_ref[...] = jnp.zeros((D_IN, D_MODEL), F32)
            dg_ref[...] = jnp.zeros((1, D_MODEL), F32)

        xv = x_ref[...]
        r = lax.rsqrt(jnp.mean(xv * xv, axis=-1, keepdims=True) + EPS)
        xhat = xv * r
        g = g_ref[...]
        hn = (xhat * g).astype(BF16)
        dproj = jnp.concatenate([du_ref[...].astype(BF16), dzs_ref[...].astype(BF16), dq_ref[...].astype(BF16),
                                 dk_ref[...].astype(BF16), dv_ref[...].astype(BF16), dza_ref[...].astype(BF16)],
                                axis=-1)
        dhn = _mm(dproj, w_ref[...])
        dxhat = dhn * g
        gx_ref[...] = dh1_ref[...] + r * (dxhat - xhat * jnp.mean(dxhat * xhat, axis=-1, keepdims=True))
        dw_ref[...] += _mm_tn(dproj, hn)
        dg_ref[...] += jnp.sum(dhn * xhat, axis=0, keepdims=True)

        @pl.when(i == t // tm - 1)
        def _():
            def to16(r):
                dw16_ref[r, :] = dw_ref[r, :].astype(BF16)

            _row_chunks(D_IN, to16)
            _copy_owned_rows(dw_ref, dw_own_ref)

    row = lambda w: pl.BlockSpec((tm, w), lambda i: (i, 0))
    own = (1, D_IN // N_DEV, D_MODEL)
    return pl.pallas_call(
        body, name="in_backward", grid=(t // tm,),
        in_specs=[row(D_MODEL), row(D_MODEL), row(512), row(512), row(512), row(128), row(128), row(512),
                  _const_spec((1, D_MODEL)), _const_spec((D_IN, D_MODEL))],
        out_specs=(row(D_MODEL), _const_spec(own), _const_spec((1, D_MODEL)), _const_spec((D_IN, D_MODEL))),
        out_shape=(jax.ShapeDtypeStruct((t, D_MODEL), F32), jax.ShapeDtypeStruct(own, F32),
                   jax.ShapeDtypeStruct((1, D_MODEL), F32), jax.ShapeDtypeStruct((D_IN, D_MODEL), BF16)),
        scratch_shapes=[pltpu.VMEM((D_IN, D_MODEL), F32)],
        compiler_params=_tc_params(("arbitrary",)),
    )(x2, dh1, du, dz_ssm, dq, dk, dv, dz_attn, g_pre, w_in)


def _local_step(x, p, target, pre_norm_g, w_in, prep, ssm_lam_re, ssm_lam_im, ssm_log_step, ssm_b_re, ssm_b_im, ssm_d,
                ssm_b_glu, attn_sinks, post_norm_g, pl_b_gate, late):
    bl, seq, _ = x.shape
    seg = seq // N_SEG
    nb = seq // ATT_BLOCK
    t = bl * seq
    x2 = x.reshape(t, D_MODEL)
    p2 = p.reshape(t, D_PLE)
    tg2 = target.reshape(t, D_MODEL)

    lam_re, lam_im = ssm_lam_re, ssm_lam_im
    log_step = ssm_log_step.reshape(SSM_G, 1)
    a_re_row, a_im_row, pw_re, pw_im, bcat, bcat_t, ccat, ccat_t = prep
    d_row = ssm_d.reshape(1, D_SSM)

    segments = lambda a: a.reshape(bl, N_SEG, seg, D_SSM)
    u, z_ssm, q, k, v, z_attn = _in_proj(x2, pre_norm_g.reshape(1, D_MODEL), w_in, min(TOKEN_TILE_WIDE, t))
    (y, states, carries), gathered = _ssm_forward(
        segments(u), bcat, ccat, a_re_row, a_im_row, pw_re, pw_im, d_row, late, seg)
    w_out, w_gate, w_proj, w_glu = (_gathered_to_full(n, g) for n, g in zip(LATE_NAMES, gathered))
    sinks = attn_sinks.reshape(N_HEADS)
    attn, lse = _attn_forward(q, k, v, sinks, bl, nb)
    (loss, dh1, dy, dz_ssm, dattn, dz_attn, d_w_glu, d_b_glu, d_w_out, d_g_post, d_w_gate, d_b_gate,
     d_w_proj, *late16) = _mix_forward_backward(
        x2, y.reshape(t, D_SSM), z_ssm, attn, z_attn, p2, tg2, w_glu,
        ssm_b_glu.reshape(1, D_SSM), w_out, post_norm_g.reshape(1, D_MODEL), w_gate, pl_b_gate.reshape(1, D_MODEL),
        w_proj, min(TOKEN_TILE, t))
    owned = lambda ds: [_full_to_owned(n, d) for n, d in zip(LATE_NAMES, ds)]
    dq, dk, dv, d_sinks = _attn_backward(q, k, v, attn, dattn, lse, sinks, bl, nb)
    (du, d_bcat, d_ccat_t, da_re, da_im, d_d), late_grads = _ssm_backward(
        segments(u), segments(dy), states, carries, bcat_t, ccat_t, a_re_row, a_im_row, pw_re, pw_im,
        d_row, owned(late16), [d_w_out, d_w_gate, _full_to_owned("pl_w_proj", d_w_proj), d_w_glu], seg)
    grad_x, d_w_in, d_g_pre, d_w_in16 = _in_backward(
        x2, dh1, du.reshape(t, D_SSM), dz_ssm, dq, dk, dv, dz_attn, pre_norm_g.reshape(1, D_MODEL), w_in,
        min(TOKEN_TILE_WIDE, t))
    d_lam_re, d_lam_im, d_ls, d_b_re, d_b_im, d_c_re, d_c_im = _ssm_param_grads(
        lam_re, lam_im, log_step, ssm_b_re, ssm_b_im, da_re.reshape(SSM_G, SSM_N), da_im.reshape(SSM_G, SSM_N),
        d_bcat, d_ccat_t)
    grads = {
        "pre_norm_g": d_g_pre, "w_in": d_w_in, "w_in16": d_w_in16, "ssm_lam_re": d_lam_re, "ssm_lam_im": d_lam_im,
        "ssm_log_step": d_ls, "ssm_b_re": d_b_re, "ssm_b_im": d_b_im, "ssm_c_re": d_c_re, "ssm_c_im": d_c_im,
        "ssm_d": d_d, "ssm_b_glu": d_b_glu, "attn_sinks": d_sinks, "post_norm_g": d_g_post, "pl_b_gate": d_b_gate,
    }
    return loss, grad_x.reshape(bl, seq, D_MODEL), grads, late_grads


LATE_NAMES = ("w_out", "pl_w_gate", "pl_w_proj", "ssm_w_glu")
BIG_NAMES = ("w_in",) + LATE_NAMES
COL_SHARDED = {"w_in": D_IN // N_DEV, "pl_w_proj": D_MODEL // N_DEV}
WEIGHT_NAMES = ("pre_norm_g", "w_in", "ssm_lam_re", "ssm_lam_im", "ssm_log_step", "ssm_b_re", "ssm_b_im", "ssm_c_re",
                "ssm_c_im", "ssm_d", "ssm_w_glu", "ssm_b_glu", "attn_sinks", "w_out", "post_norm_g", "pl_w_proj",
                "pl_w_gate", "pl_b_gate")


TRANSPOSED = {"w_in": (0, 1), "ssm_b_re": (1, 2), "ssm_b_im": (1, 2)}


def _kernel_form(name, a):
    a = a[0]
    if name in TRANSPOSED:
        a = jnp.swapaxes(a, *TRANSPOSED[name])
    if name in ("ssm_b_re", "ssm_b_im", "ssm_c_re", "ssm_c_im"):
        a = a.reshape(SSM_G * SSM_P, SSM_N)
    return a


def _given_form(name, a, shape):
    if name in TRANSPOSED:
        i, j = TRANSPOSED[name]
        swapped = list(shape[1:])
        swapped[i], swapped[j] = swapped[j], swapped[i]
        return jnp.swapaxes(a.reshape(swapped), i, j).reshape(shape)
    return a.reshape(shape)


def _gathered_to_full(name, g):
    _, rows, cols = g.shape
    if name in COL_SHARDED:
        return jnp.swapaxes(g, 0, 1).reshape(rows, N_DEV * cols)
    return g.reshape(N_DEV * rows, cols)


def _full_to_owned(name, full):
    if name in COL_SHARDED:
        return jnp.swapaxes(full.reshape(full.shape[0], N_DEV, COL_SHARDED[name]), 0, 1)
    return full.reshape(N_DEV, full.shape[0] // N_DEV, full.shape[1])


def kernel(x, p, pre_norm_g, w_in, ssm_lam_re, ssm_lam_im, ssm_log_step, ssm_b_re, ssm_b_im, ssm_c_re, ssm_c_im, ssm_d, ssm_w_glu, ssm_b_glu, attn_sinks, w_out, post_norm_g, pl_w_proj, pl_w_gate, pl_b_gate, loss_target, m_pre_norm_g, m_w_in, m_ssm_lam_re, m_ssm_lam_im, m_ssm_log_step, m_ssm_b_re, m_ssm_b_im, m_ssm_c_re, m_ssm_c_im, m_ssm_d, m_ssm_w_glu, m_ssm_b_glu, m_attn_sinks, m_w_out, m_post_norm_g, m_pl_w_proj, m_pl_w_gate, m_pl_b_gate, v_pre_norm_g, v_w_in, v_ssm_lam_re, v_ssm_lam_im, v_ssm_log_step, v_ssm_b_re, v_ssm_b_im, v_ssm_c_re, v_ssm_c_im, v_ssm_d, v_ssm_w_glu, v_ssm_b_glu, v_attn_sinks, v_w_out, v_post_norm_g, v_pl_w_proj, v_pl_w_gate, v_pl_b_gate):
    w = dict(pre_norm_g=pre_norm_g, w_in=w_in, ssm_lam_re=ssm_lam_re, ssm_lam_im=ssm_lam_im, ssm_log_step=ssm_log_step,
             ssm_b_re=ssm_b_re, ssm_b_im=ssm_b_im, ssm_c_re=ssm_c_re, ssm_c_im=ssm_c_im, ssm_d=ssm_d, ssm_w_glu=ssm_w_glu,
             ssm_b_glu=ssm_b_glu, attn_sinks=attn_sinks, w_out=w_out, post_norm_g=post_norm_g, pl_w_proj=pl_w_proj,
             pl_w_gate=pl_w_gate, pl_b_gate=pl_b_gate)
    m = dict(pre_norm_g=m_pre_norm_g, w_in=m_w_in, ssm_lam_re=m_ssm_lam_re, ssm_lam_im=m_ssm_lam_im,
             ssm_log_step=m_ssm_log_step, ssm_b_re=m_ssm_b_re, ssm_b_im=m_ssm_b_im, ssm_c_re=m_ssm_c_re,
             ssm_c_im=m_ssm_c_im, ssm_d=m_ssm_d, ssm_w_glu=m_ssm_w_glu, ssm_b_glu=m_ssm_b_glu, attn_sinks=m_attn_sinks,
             w_out=m_w_out, post_norm_g=m_post_norm_g, pl_w_proj=m_pl_w_proj, pl_w_gate=m_pl_w_gate,
             pl_b_gate=m_pl_b_gate)
    v = dict(pre_norm_g=v_pre_norm_g, w_in=v_w_in, ssm_lam_re=v_ssm_lam_re, ssm_lam_im=v_ssm_lam_im,
             ssm_log_step=v_ssm_log_step, ssm_b_re=v_ssm_b_re, ssm_b_im=v_ssm_b_im, ssm_c_re=v_ssm_c_re,
             ssm_c_im=v_ssm_c_im, ssm_d=v_ssm_d, ssm_w_glu=v_ssm_w_glu, ssm_b_glu=v_ssm_b_glu, attn_sinks=v_attn_sinks,
             w_out=v_w_out, post_norm_g=v_post_norm_g, pl_w_proj=v_pl_w_proj, pl_w_gate=v_pl_w_gate,
             pl_b_gate=v_pl_b_gate)
    kf = lambda d: {n: _kernel_form(n, a) for n, a in d.items()}
    wk, mk, vk = kf(w), kf(m), kf(v)

    (gathered,), prep = _allgather_weights([wk["w_in"]], *_ssm_prep(
        wk["ssm_lam_re"], wk["ssm_lam_im"], wk["ssm_log_step"].reshape(SSM_G, 1), wk["ssm_b_re"], wk["ssm_b_im"],
        wk["ssm_c_re"], wk["ssm_c_im"], x.shape[1] // N_SEG))
    loss, grad_x, grads, g_late = _local_step(
        x, p[0], loss_target, wk["pre_norm_g"], gathered.reshape(D_IN, D_MODEL), prep, wk["ssm_lam_re"],
        wk["ssm_lam_im"], wk["ssm_log_step"], wk["ssm_b_re"], wk["ssm_b_im"], wk["ssm_d"],
        wk["ssm_b_glu"], wk["attn_sinks"], wk["post_norm_g"], wk["pl_b_gate"], [wk[n] for n in LATE_NAMES])

    owned = lambda g: g.reshape(N_DEV, D_IN // N_DEV, D_MODEL)
    tiny_form = lambda d: [d[n].reshape(rows, cols) for n, rows, cols in TINY]
    med_form = lambda d: [d[n].reshape(N_DEV, rows // N_DEV, cols) for n, rows, cols in MEDIUM]
    g_big, loss, g_tiny, g_med = _reduce_final(
        [owned(grads["w_in16"])], [grads["w_in"]], loss, tiny_form(grads), med_form(grads))
    names = BIG_NAMES + tuple(n for n, _, _ in TINY + MEDIUM)
    form = lambda d: [d[n] for n in BIG_NAMES] + tiny_form(d) + med_form(d)
    updated = _adamw_update(g_big + g_late + g_tiny + g_med, form(wk), form(mk), form(vk), len(BIG_NAMES))
    vals = dict(zip(names, updated))
    results = [[_given_form(n, vals[n][kind], w[n].shape) for n in WEIGHT_NAMES] for kind in range(4)]
    return (loss.reshape(()), grad_x, *results[0], *results[1], *results[2], *results[3])
```

```python
import functools
import math

import jax
import jax.numpy as jnp
from jax import lax
from jax.experimental import pallas as pl
from jax.experimental.pallas import tpu as pltpu

F32 = jnp.float32
BF16 = jnp.bfloat16

D_MODEL = 1024
D_SSM = 512
D_ATTN = 512
SSM_P = 16
SSM_G = 32
SSM_N = 64
N_HEADS = 8
KV_HEADS = 2
Q_PER_KV = 4
HEAD_DIM = 64
ATT_BLOCK = 128
D_PLE = 256
D_IN = 2304
EPS = 1e-6
N_DEV = 8
N_SEG = 8
G_TILE = 8
N_GT = SSM_G // G_TILE
CH_T = G_TILE * SSM_P
ST_T = G_TILE * SSM_N
N_STATE = SSM_G * SSM_N
SCAN_UNROLL = 4
TOKEN_TILE = 256
TOKEN_TILE_WIDE = 512
LANES = 128
VMEM_LIMIT = 60 * 1024 * 1024

ADAM_LR = 0.001
ADAM_B1 = 0.9
ADAM_B2 = 0.999
ADAM_EPS = 1e-08
ADAM_WD = 0.01
ADAM_STEP = 10

GELU_C = math.sqrt(2.0 / math.pi)
GELU_K = 0.044715
ATT_SCALE = 1.0 / math.sqrt(HEAD_DIM)
NEG_INF = float("-inf")


def _mm(a, b):
    return jnp.dot(a.astype(BF16), b.astype(BF16), preferred_element_type=F32)


def _mm_nt(a, b):
    return lax.dot_general(a.astype(BF16), b.astype(BF16), (((1,), (1,)), ((), ())), preferred_element_type=F32)


def _mm_tn(a, b):
    return lax.dot_general(a.astype(BF16), b.astype(BF16), (((0,), (0,)), ((), ())), preferred_element_type=F32)


def _sigmoid(x):
    return 1.0 / (1.0 + jnp.exp(-x))


def _tc_params(sem):
    return pltpu.CompilerParams(dimension_semantics=sem, vmem_limit_bytes=VMEM_LIMIT)


def _const_spec(shape):
    nd = len(shape)
    return pl.BlockSpec(shape, lambda *_: (0,) * nd)


def _mesh_pos():
    return lax.axis_index("x"), lax.axis_index("y"), lax.axis_index("c")


ROW_CHUNKS = (64, 32, 16)


def _row_chunk(nrows):
    return next((c for c in ROW_CHUNKS if nrows % c == 0), None)


def _row_chunks(nrows, fn, chunk=None, init=None):
    chunk = chunk or _row_chunk(nrows)

    def step(i, carry):
        rows = pl.ds(pl.multiple_of(i * chunk, chunk), chunk)
        if init is None:
            fn(rows)
            return carry
        return fn(rows, carry)

    return lax.fori_loop(0, nrows // chunk, step, 0 if init is None else init)


def _slot(px, py, pc):
    return 4 * px + 2 * py + pc


def _copy_owned_rows(acc_ref, own_ref):
    rows = own_ref.shape[1]
    own_ref[0] = acc_ref[pl.ds(pl.multiple_of(_slot(*_mesh_pos()) * rows, 8), rows), :]


def _allgather_weights(shards, work=None, work_inputs=(), work_out_shapes=()):
    n, n_wi, n_wo = len(shards), len(work_inputs), len(work_out_shapes)

    def body(*refs):
        srcs, w_in_refs = refs[:n], refs[n:n + n_wi]
        outs, w_out_refs = refs[n + n_wi:2 * n + n_wi], refs[2 * n + n_wi:2 * n + n_wi + n_wo]
        send_sems, recv_sems = refs[2 * n + n_wi + n_wo:]
        x, y, c = _mesh_pos()
        me, sibling = (x, y, c), (x, y, 1 - c)
        chips = [(1 - x, y), (x, 1 - y), (1 - x, 1 - y)]

        def copy(a, k, block, to):
            blk = outs[a].at[_slot(*block)]
            return pltpu.make_async_remote_copy(
                src_ref=blk, dst_ref=blk, send_sem=send_sems.at[7 * a + k], recv_sem=recv_sems.at[7 * a + k],
                device_id=to, device_id_type=pl.DeviceIdType.MESH)

        sends = []
        for a in range(n):
            mine = outs[a].at[_slot(*me)]

            def cast(r, mine=mine, src=srcs[a]):
                mine[r, :] = src[r, :].astype(BF16)

            _row_chunks(srcs[a].shape[0], cast)
            first = [copy(a, 0, me, sibling)] + [copy(a, 1 + j, me, (*chip, c)) for j, chip in enumerate(chips)]
            for cp in first:
                cp.start()
            sends += first
        if work is not None:
            work(w_in_refs, w_out_refs)
        for a in range(n):
            for j, chip in enumerate(chips):
                copy(a, 1 + j, (*chip, c), me).wait_recv()
                fwd = copy(a, 4 + j, (*chip, c), sibling)
                fwd.start()
                sends.append(fwd)
        for a in range(n):
            copy(a, 0, sibling, me).wait_recv()
            for j, chip in enumerate(chips):
                copy(a, 4 + j, (*chip, 1 - c), me).wait_recv()
        for cp in sends:
            cp.wait_send()

    vm = pl.BlockSpec(memory_space=pltpu.VMEM)
    res = pl.pallas_call(
        body, name="allgather_weights",
        out_shape=tuple(jax.ShapeDtypeStruct((N_DEV,) + s.shape, BF16) for s in shards) + tuple(work_out_shapes),
        in_specs=[vm] * (n + n_wi), out_specs=(vm,) * (n + n_wo),
        scratch_shapes=[pltpu.SemaphoreType.DMA((7 * n,)), pltpu.SemaphoreType.DMA((7 * n,))],
        compiler_params=pltpu.CompilerParams(vmem_limit_bytes=VMEM_LIMIT),
    )(*shards, *work_inputs)
    return list(res[:n]), list(res[n:])


def _adamw(w, g, m, v):
    m = ADAM_B1 * m + (1.0 - ADAM_B1) * g
    v = ADAM_B2 * v + (1.0 - ADAM_B2) * (g * g)
    m_hat = m / (1.0 - ADAM_B1 ** ADAM_STEP)
    v_hat = v / (1.0 - ADAM_B2 ** ADAM_STEP)
    delta = -ADAM_LR * (m_hat / (jnp.sqrt(v_hat) + ADAM_EPS) + ADAM_WD * w)
    return delta, m, v


def _remote(src, dst, send_sems, recv_sems, k, to):
    return pltpu.make_async_remote_copy(src_ref=src, dst_ref=dst, send_sem=send_sems.at[k], recv_sem=recv_sems.at[k],
                                        device_id=to, device_id_type=pl.DeviceIdType.MESH)


def _big_reduce_phases(g16_r, go_r, outs, send2, recv1, recv2, s_send, s_recv):
    n = len(g16_r)
    x, y, c = _mesh_pos()
    sibling = (x, y, 1 - c)
    chips = [(1 - x, y), (x, 1 - y), (1 - x, 1 - y)]
    all_chips = [(x, y)] + chips
    lvl1 = []
    for a in range(n):
        cps = [_remote(g16_r[a].at[_slot(*chip, 1 - c)], recv1[a].at[j], s_send, s_recv, 7 * a + j, sibling)
               for j, chip in enumerate(all_chips)]
        for cp in cps:
            cp.start()
        lvl1.append(cps)
    yield
    lvl2 = []
    for a in range(n):
        for cp in lvl1[a]:
            cp.wait_recv()
        og = outs[a]

        def partials(r, a=a, og=og):
            og[r, :] = go_r[a][r, :] + recv1[a][0, r, :].astype(F32)
            for j, chip in enumerate(chips):
                mine16 = g16_r[a][_slot(*chip, c), r, :].astype(F32)
                send2[a][j, r, :] = (mine16 + recv1[a][1 + j, r, :].astype(F32)).astype(BF16)

        _row_chunks(go_r[a].shape[0], partials)
        cps = [_remote(send2[a].at[j], recv2[a].at[j], s_send, s_recv, 7 * a + 4 + j, (*chip, c))
               for j, chip in enumerate(chips)]
        for cp in cps:
            cp.start()
        lvl2.append(cps)
    yield
    for a in range(n):
        for cp in lvl2[a]:
            cp.wait_recv()
        og = outs[a]

        def total(r, a=a, og=og):
            g = og[r, :]
            for j in range(3):
                g = g + recv2[a][j, r, :].astype(F32)
            og[r, :] = g

        _row_chunks(go_r[a].shape[0], total)
    yield
    for cps in lvl1 + lvl2:
        for cp in cps:
            cp.wait_send()


def _adamw_update(g, w, m, v, n_streamed):
    n = len(g)
    ns = n_streamed

    def body(*refs):
        g_r, w_r, m_r, v_r = (refs[i * n:(i + 1) * n] for i in range(4))
        outs = refs[4 * n:8 * n]
        in_buf, out_buf = refs[8 * n:8 * n + 4 * ns], refs[8 * n + 4 * ns:8 * n + 8 * ns]
        in_sems, out_sems = refs[8 * n + 8 * ns:]
        loads = [[pltpu.make_async_copy(src[a], in_buf[4 * a + k], in_sems.at[4 * a + k])
                  for k, src in enumerate((g_r, w_r, m_r, v_r))] for a in range(ns)]
        stores = [[pltpu.make_async_copy(out_buf[4 * a + k], outs[4 * a + k], out_sems.at[4 * a + k]) for k in range(4)]
                  for a in range(ns)]
        for cps in loads:
            for cp in cps:
                cp.start()
        for a in range(n):
            if a < ns:
                for cp in loads[a]:
                    cp.wait()
                gs, ws, ms, vs = in_buf[4 * a:4 * a + 4]
                og, od, om, ov = out_buf[4 * a:4 * a + 4]
            else:
                gs, ws, ms, vs = g_r[a], w_r[a], m_r[a], v_r[a]
                og, od, om, ov = outs[4 * a:4 * a + 4]

            def update(idx, gs=gs, ws=ws, ms=ms, vs=vs, og=og, od=od, om=om, ov=ov):
                gv = gs[idx]
                d, nm, nv = _adamw(ws[idx], gv, ms[idx], vs[idx])
                og[idx] = gv
                od[idx] = d
                om[idx] = nm
                ov[idx] = nv

            shape = gs.shape
            if len(shape) == 3:
                for b in range(shape[0]):
                    update(b)
            elif _row_chunk(shape[0]) is not None:
                _row_chunks(shape[0], update)
            else:
                update(Ellipsis)
            if a < ns:
                for cp in stores[a]:
                    cp.start()
        for cps in stores:
            for cp in cps:
                cp.wait()

    vm, hbm = pl.BlockSpec(memory_space=pltpu.VMEM), pl.BlockSpec(memory_space=pl.ANY)
    place = lambda: [hbm] * ns + [vm] * (n - ns)
    buf = [pltpu.VMEM(t.shape, F32) for t in g[:ns] for _ in range(4)]
    res = pl.pallas_call(
        body, name="adamw_update",
        out_shape=tuple(jax.ShapeDtypeStruct(t.shape, F32) for t in g for _ in range(4)),
        in_specs=place() * 4, out_specs=tuple(s for a in range(n) for s in [hbm if a < ns else vm] * 4),
        scratch_shapes=buf + buf + [pltpu.SemaphoreType.DMA((4 * ns,)), pltpu.SemaphoreType.DMA((4 * ns,))],
        compiler_params=pltpu.CompilerParams(vmem_limit_bytes=VMEM_LIMIT),
    )(*g, *w, *m, *v)
    return [res[4 * a:4 * a + 4] for a in range(n)]


TINY = (("pre_norm_g", 1, 1024), ("post_norm_g", 1, 1024), ("pl_b_gate", 1, 1024), ("ssm_d", 1, 512),
        ("ssm_b_glu", 1, 512), ("ssm_log_step", 1, 32), ("attn_sinks", 1, 8), ("ssm_lam_re", 32, 64),
        ("ssm_lam_im", 32, 64))
MEDIUM = (("ssm_b_re", SSM_G * SSM_P, SSM_N), ("ssm_b_im", SSM_G * SSM_P, SSM_N), ("ssm_c_re", SSM_G * SSM_P, SSM_N),
          ("ssm_c_im", SSM_G * SSM_P, SSM_N))


def _stage_rows():
    offs, r = {}, 0
    for name, rows, cols in TINY + (("loss", 1, 1),):
        if rows > 1:
            r = -(-r // 8) * 8
        offs[name] = r
        r += rows if rows > 1 else max(cols // LANES, 1)
    return offs, -(-r // 8) * 8


def _reduce_final(g16, g32, loss, g_tiny, g_med):
    nb_, nt, nm_ = len(g16), len(TINY), len(MEDIUM)
    offs, stage_rows = _stage_rows()

    def body(*refs):
        g16_r, go_r = refs[:nb_], refs[nb_:2 * nb_]
        base = 2 * nb_
        loss_r, gt, gm = refs[base], refs[base + 1:base + 1 + nt], refs[base + 1 + nt:base + 1 + nt + nm_]
        base += 1 + nt + nm_
        out_b = refs[base:base + nb_]
        base += nb_
        loss_o, out_t, out_m = refs[base], refs[base + 1:base + 1 + nt], refs[base + 1 + nt:base + 1 + nt + nm_]
        base += 1 + nt + nm_
        send2_b, recv1_b, recv2_b = (refs[base + i * nb_:base + (i + 1) * nb_] for i in range(3))
        base += 3 * nb_
        stage = refs[base]
        recv1, part, recv2 = (refs[base + 1 + i * nm_:base + 1 + (i + 1) * nm_] for i in range(3))
        bs_send, bs_recv, s_send, s_recv, own_sems = refs[base + 1 + 3 * nm_:base + 6 + 3 * nm_]
        own32 = refs[base + 6 + 3 * nm_:]
        fetch = [pltpu.make_async_copy(go_r[a].at[0], own32[a], own_sems.at[a]) for a in range(nb_)]
        for cp in fetch:
            cp.start()
        big = _big_reduce_phases(g16_r, own32, out_b, send2_b, recv1_b, recv2_b, bs_send, bs_recv)
        small = small_phases(loss_r, gt, gm, loss_o, out_t, out_m, stage, recv1, part, recv2, s_send, s_recv)
        next(big)
        next(small)
        for cp in fetch:
            cp.wait()
        next(big)
        for _ in small:
            pass
        for _ in big:
            pass

    def small_phases(loss_r, gt, gm, loss_o, out_t, out_m, stage, recv1, part, recv2, s_send, s_recv):
        x, y, c = _mesh_pos()
        me = _slot(x, y, c)
        sibling = (x, y, 1 - c)
        chips = [(1 - x, y), (x, 1 - y), (1 - x, 1 - y)]
        all_chips = [(x, y)] + chips
        peers = [sibling] + [(*chip, c) for chip in chips] + [(*chip, 1 - c) for chip in chips]
        sem = iter(range(7 + 14 * nm_))
        lvl1 = []
        for a in range(nm_):
            cps = [_remote(gm[a].at[_slot(*chip, 1 - c)], recv1[a].at[j], s_send, s_recv, next(sem), sibling)
                   for j, chip in enumerate(all_chips)]
            for cp in cps:
                cp.start()
            lvl1.append(cps)
        mine = stage.at[me]
        mine[...] = jnp.zeros((stage_rows, LANES), F32)
        for (name, rows, cols), ref in zip(TINY + (("loss", 1, 1),), gt + (loss_r,)):
            r0 = offs[name]
            if rows > 1:
                mine[r0:r0 + rows, 0:cols] = ref[...]
            elif cols >= LANES:
                for i in range(cols // LANES):
                    mine[r0 + i:r0 + i + 1, :] = ref[:, i * LANES:(i + 1) * LANES]
            else:
                mine[r0:r0 + 1, 0:cols] = ref[...]
        tiny_cps = [_remote(mine, mine, s_send, s_recv, next(sem), peer) for peer in peers]
        for cp in tiny_cps:
            cp.start()
        yield
        lvl2 = []
        for a in range(nm_):
            for cp in lvl1[a]:
                cp.wait_recv()
            for j, chip in enumerate(all_chips):
                part[a][j] = gm[a][_slot(*chip, c)] + recv1[a][j]
            cps = [_remote(part[a].at[1 + j], recv2[a].at[j], s_send, s_recv, next(sem), (*chip, c))
                   for j, chip in enumerate(chips)]
            for cp in cps:
                cp.start()
            lvl2.append(cps)
        yield
        lvl3 = []
        for a in range(nm_):
            for cp in lvl2[a]:
                cp.wait_recv()
            blk = out_m[a].at[me]
            blk[...] = ((part[a][0] + recv2[a][0]) + recv2[a][1]) + recv2[a][2]
            cps = [_remote(blk, blk, s_send, s_recv, next(sem), peer) for peer in peers]
            for cp in cps:
                cp.start()
            lvl3.append(cps)
        yield
        for cp in tiny_cps:
            cp.wait_recv()
        tot = stage[0]
        for d in range(1, N_DEV):
            tot = tot + stage[d]
        loss_o[...] = tot[offs["loss"]:offs["loss"] + 1, 0:1]
        for k, (name, rows, cols) in enumerate(TINY):
            r0 = offs[name]
            if rows > 1:
                out_t[k][...] = tot[r0:r0 + rows, 0:cols]
            elif cols >= LANES:
                for i in range(cols // LANES):
                    out_t[k][:, i * LANES:(i + 1) * LANES] = tot[r0 + i:r0 + i + 1, :]
            else:
                out_t[k][...] = tot[r0:r0 + 1, 0:cols]
        for cps in lvl3:
            for cp in cps:
                cp.wait_recv()
        for cps in lvl1 + lvl2 + lvl3 + [tiny_cps]:
            for cp in cps:
                cp.wait_send()

    vmem = pl.BlockSpec(memory_space=pltpu.VMEM)
    t_shapes = [jax.ShapeDtypeStruct((rows, cols), F32) for _, rows, cols in TINY]
    m_shapes = [jax.ShapeDtypeStruct((N_DEV, rows // N_DEV, cols), F32) for _, rows, cols in MEDIUM]
    blk = [(rows // N_DEV, cols) for _, rows, cols in MEDIUM]
    shard = [g.shape[1:] for g in g16]
    scratch = ([pltpu.VMEM((3,) + s, BF16) for s in shard] + [pltpu.VMEM((4,) + s, BF16) for s in shard]
               + [pltpu.VMEM((3,) + s, BF16) for s in shard]
               + [pltpu.VMEM((N_DEV, stage_rows, LANES), F32)]
               + [pltpu.VMEM((4,) + b, F32) for b in blk] + [pltpu.VMEM((4,) + b, F32) for b in blk]
               + [pltpu.VMEM((3,) + b, F32) for b in blk]
               + [pltpu.SemaphoreType.DMA((7 * nb_,)), pltpu.SemaphoreType.DMA((7 * nb_,)),
                  pltpu.SemaphoreType.DMA((7 + 14 * nm_,)), pltpu.SemaphoreType.DMA((7 + 14 * nm_,)),
                  pltpu.SemaphoreType.DMA((nb_,))]
               + [pltpu.VMEM(s, F32) for s in shard])
    n_out = nb_ + 1 + nt + nm_
    res = pl.pallas_call(
        body, name="reduce_final",
        out_shape=tuple(jax.ShapeDtypeStruct(s, F32) for s in shard) + (jax.ShapeDtypeStruct((1, 1), F32),)
        + tuple(t_shapes) + tuple(m_shapes),
        in_specs=[vmem] * nb_ + [pl.BlockSpec(memory_space=pl.ANY)] * nb_ + [vmem] * (1 + nt + nm_),
        out_specs=(vmem,) * n_out, scratch_shapes=scratch,
        compiler_params=pltpu.CompilerParams(vmem_limit_bytes=VMEM_LIMIT),
    )(*g16, *g32, loss, *g_tiny, *g_med)
    return list(res[:nb_]), res[nb_], list(res[nb_ + 1:nb_ + 1 + nt]), list(res[nb_ + 1 + nt:])


def _gather_phases(shard_r, gath, cast, send_sems, recv_sems, local_sems):
    n = len(shard_r)
    x, y, c = _mesh_pos()
    me, sibling = (x, y, c), (x, y, 1 - c)
    chips = [(1 - x, y), (x, 1 - y), (1 - x, 1 - y)]

    def own(a, k, to):
        return _remote(cast[a], gath[a].at[_slot(*me)], send_sems, recv_sems, 7 * a + k, to)

    def passed(a, k, block, to):
        blk = gath[a].at[_slot(*block)]
        return _remote(blk, blk, send_sems, recv_sems, 7 * a + k, to)

    def keep(a):
        return pltpu.make_async_copy(cast[a], gath[a].at[_slot(*me)], local_sems.at[a])

    def start():
        for a in range(n):
            def to16(r, a=a):
                cast[a][r, :] = shard_r[a][r, :].astype(BF16)

            _row_chunks(shard_r[a].shape[0], to16)
            keep(a).start()
            own(a, 0, sibling).start()
            for j, chip in enumerate(chips):
                own(a, 1 + j, (*chip, c)).start()

    def relay():
        for a in range(n):
            for j, chip in enumerate(chips):
                passed(a, 1 + j, (*chip, c), me).wait_recv()
                passed(a, 4 + j, (*chip, c), sibling).start()

    def finish():
        for a in range(n):
            passed(a, 0, sibling, me).wait_recv()
            for j, chip in enumerate(chips):
                passed(a, 4 + j, (*chip, 1 - c), me).wait_recv()
            own(a, 0, sibling).wait_send()
            for j, chip in enumerate(chips):
                own(a, 1 + j, (*chip, c)).wait_send()
                passed(a, 4 + j, (*chip, c), sibling).wait_send()
            keep(a).wait()

    return start, relay, finish


def _gather_operands(shards):
    n = len(shards)
    return ((pl.BlockSpec(memory_space=pl.ANY),) * n,
            tuple(jax.ShapeDtypeStruct((N_DEV,) + s.shape, BF16) for s in shards),
            [pltpu.VMEM(s.shape, BF16) for s in shards]
            + [pltpu.SemaphoreType.DMA((7 * n,)), pltpu.SemaphoreType.DMA((7 * n,)), pltpu.SemaphoreType.DMA((n,))])


def _hosted_reduce_phases(g16_r, g32_r, red, own16, recv1, send2, recv2, own32, s_send, s_recv, s_local):
    n = len(g16_r)
    x, y, c = _mesh_pos()
    sibling = (x, y, 1 - c)
    chips = [(1 - x, y), (x, 1 - y), (1 - x, 1 - y)]
    all_chips = [(x, y)] + chips

    def lvl1(a, j):
        return _remote(g16_r[a].at[_slot(*all_chips[j], 1 - c)], recv1[a].at[j], s_send, s_recv, 7 * a + j, sibling)

    def lvl2(a, j):
        return _remote(send2[a].at[j], recv2[a].at[j], s_send, s_recv, 7 * a + 4 + j, (*chips[j], c))

    def mine(a, j):
        if j == 3:
            only_mine = g32_r[a].shape[0] == 1
            return pltpu.make_async_copy(g32_r[a].at[0 if only_mine else _slot(x, y, c)], own32[a], s_local.at[4 * a + j])
        return pltpu.make_async_copy(g16_r[a].at[_slot(*chips[j], c)], own16[a].at[j], s_local.at[4 * a + j])

    def start():
        for a in range(n):
            for j in range(4):
                mine(a, j).start()
            for j in range(4):
                lvl1(a, j).start()

    def middle():
        for a in range(n):
            for j in range(4):
                mine(a, j).wait()
            for j in range(4):
                lvl1(a, j).wait_recv()

            def partials(r, a=a):
                red[a][r, :] = own32[a][r, :] + recv1[a][0, r, :].astype(F32)
                for j in range(3):
                    send2[a][j, r, :] = (own16[a][j, r, :].astype(F32) + recv1[a][1 + j, r, :].astype(F32)).astype(BF16)

            _row_chunks(own32[a].shape[0], partials)
            for j in range(3):
                lvl2(a, j).start()

    def total():
        for a in range(n):
            for j in range(3):
                lvl2(a, j).wait_recv()

            def add(r, a=a):
                g = red[a][r, :]
                for j in range(3):
                    g = g + recv2[a][j, r, :].astype(F32)
                red[a][r, :] = g

            _row_chunks(own32[a].shape[0], add)

    def finish():
        for a in range(n):
            for j in range(4):
                lvl1(a, j).wait_send()
            for j in range(3):
                lvl2(a, j).wait_send()

    return start, middle, total, finish


def _hosted_reduce_operands(g16, const_spec):
    n = len(g16)
    shard = [g.shape[1:] for g in g16]
    return ([pl.BlockSpec(memory_space=pl.ANY)] * (2 * n),
            tuple(const_spec(s) for s in shard),
            tuple(jax.ShapeDtypeStruct(s, F32) for s in shard),
            [pltpu.VMEM((3,) + s, BF16) for s in shard] + [pltpu.VMEM((4,) + s, BF16) for s in shard]
            + [pltpu.VMEM((3,) + s, BF16) for s in shard] + [pltpu.VMEM((3,) + s, BF16) for s in shard]
            + [pltpu.VMEM(s, F32) for s in shard]
            + [pltpu.SemaphoreType.DMA((7 * n,)), pltpu.SemaphoreType.DMA((7 * n,)), pltpu.SemaphoreType.DMA((4 * n,))])


def _in_proj(x2, g_pre, w_in, tm):
    t = x2.shape[0]

    def body(x_ref, g_ref, w_ref, u_ref, zs_ref, q_ref, k_ref, v_ref, za_ref):
        xv = x_ref[...]
        r = lax.rsqrt(jnp.mean(xv * xv, axis=-1, keepdims=True) + EPS)
        hn = xv * r * g_ref[...]
        proj = _mm_nt(hn, w_ref[...])
        u_ref[...] = proj[:, 0:512]
        zs_ref[...] = proj[:, 512:1024]
        q_ref[...] = proj[:, 1024:1536].astype(BF16)
        k_ref[...] = proj[:, 1536:1664].astype(BF16)
        v_ref[...] = proj[:, 1664:1792].astype(BF16)
        za_ref[...] = proj[:, 1792:2304]

    row = lambda w: pl.BlockSpec((tm, w), lambda i: (i, 0))
    return pl.pallas_call(
        body, name="in_proj", grid=(t // tm,),
        in_specs=[row(D_MODEL), _const_spec((1, D_MODEL)), _const_spec((D_IN, D_MODEL))],
        out_specs=(row(512), row(512), row(512), row(128), row(128), row(512)),
        out_shape=(jax.ShapeDtypeStruct((t, 512), F32),
                   jax.ShapeDtypeStruct((t, 512), F32), jax.ShapeDtypeStruct((t, 512), BF16),
                   jax.ShapeDtypeStruct((t, 128), BF16), jax.ShapeDtypeStruct((t, 128), BF16),
                   jax.ShapeDtypeStruct((t, 512), F32)),
        compiler_params=_tc_params(("arbitrary",)),
    )(x2, g_pre, w_in)


def _discretise(lr, li, ls):
    step = jnp.exp(ls)
    mag = jnp.exp(lr * step)
    ar = mag * jnp.cos(li * step)
    ai = mag * jnp.sin(li * step)
    den = lr * lr + li * li
    cr = ((ar - 1.0) * lr + ai * li) / den
    ci = (ai * lr - (ar - 1.0) * li) / den
    return step, ar, ai, den, cr, ci


def _per_channel(v):
    return jnp.broadcast_to(v[:, None, :], (SSM_G, SSM_P, SSM_N)).reshape(SSM_G * SSM_P, SSM_N)


def _tile_masks():
    r = lax.broadcasted_iota(jnp.int32, (CH_T, ST_T), 0) // SSM_P
    l = lax.broadcasted_iota(jnp.int32, (CH_T, ST_T), 1) // SSM_N
    lt = lax.broadcasted_iota(jnp.int32, (ST_T, CH_T), 0) // SSM_N
    rt = lax.broadcasted_iota(jnp.int32, (ST_T, CH_T), 1) // SSM_P
    rep = lax.broadcasted_iota(jnp.int32, (SSM_N, ST_T), 0) == lax.broadcasted_iota(jnp.int32, (SSM_N, ST_T), 1) % SSM_N
    rep_t = lax.broadcasted_iota(jnp.int32, (ST_T, SSM_N), 0) % SSM_N == lax.broadcasted_iota(jnp.int32, (ST_T, SSM_N), 1)
    return r == l, lt == rt, rep, rep_t


def _ssm_prep(lam_re, lam_im, log_step, b_re, b_im, c_re, c_im, seg):
    def work(in_refs, out_refs):
        lr_ref, li_ref, ls_ref, br_ref, bi_ref, cre_ref, cim_ref, lrr_ref, lir_ref, lsr_ref = in_refs
        ar_ref, ai_ref, pr_ref, pi_ref, bcat_ref, bcat_t_ref, ccat_ref, ccat_t_ref = out_refs
        _, _, _, _, cr, ci = _discretise(lr_ref[...], li_ref[...], ls_ref[...])
        cr, ci = _per_channel(cr), _per_channel(ci)
        br, bi = br_ref[...], bi_ref[...]
        bb_re = cr * br - ci * bi
        bb_im = cr * bi + ci * br
        same, same_t, rep, rep_t = _tile_masks()
        rep, rep_t = rep.astype(BF16), rep_t.astype(BF16)
        for j in range(N_GT):
            rows = slice(j * CH_T, (j + 1) * CH_T)
            for wide, tall, parts in ((bcat_ref, bcat_t_ref, (bb_re[rows], bb_im[rows])),
                                      (ccat_t_ref, ccat_ref, (cre_ref[rows, :], -cim_ref[rows, :]))):
                for k, part in enumerate(parts):
                    p16 = part.astype(BF16)
                    wide[j, :, k * ST_T:(k + 1) * ST_T] = jnp.where(same, _mm(p16, rep), 0.0).astype(BF16)
                    tall[j, k * ST_T:(k + 1) * ST_T, :] = jnp.where(same_t, _mm_nt(rep_t, p16), 0.0).astype(BF16)
        stepr = jnp.exp(lsr_ref[...])
        mag = jnp.exp(lrr_ref[...] * stepr)
        a_r, a_i = mag * jnp.cos(lir_ref[...] * stepr), mag * jnp.sin(lir_ref[...] * stepr)
        p_r, p_i = a_r, a_i
        for k in range(8):
            pr_ref[k:k + 1, :] = p_r
            pi_ref[k:k + 1, :] = p_i
            p_r, p_i = p_r * a_r - p_i * a_i, p_r * a_i + p_i * a_r
        n = 8
        while n < seg:
            tr, ti = pr_ref[n - 1:n, :], pi_ref[n - 1:n, :]
            xr, xi = pr_ref[0:n, :], pi_ref[0:n, :]
            pr_ref[n:2 * n, :] = xr * tr - xi * ti
            pi_ref[n:2 * n, :] = xr * ti + xi * tr
            n *= 2
        ar_ref[...] = pr_ref[0:1, :]
        ai_ref[...] = pi_ref[0:1, :]

    row = jax.ShapeDtypeStruct((1, N_STATE), F32)
    pw = jax.ShapeDtypeStruct((seg, N_STATE), F32)
    wide = jax.ShapeDtypeStruct((N_GT, CH_T, 2 * ST_T), BF16)
    tall = jax.ShapeDtypeStruct((N_GT, 2 * ST_T, CH_T), BF16)
    step_row = jnp.broadcast_to(log_step, (SSM_G, SSM_N)).reshape(1, N_STATE)
    inputs = (lam_re, lam_im, log_step, b_re, b_im, c_re, c_im, lam_re.reshape(1, N_STATE),
              lam_im.reshape(1, N_STATE), step_row)
    return work, inputs, (row, row, pw, pw, wide, tall, tall, wide)


def _seg_rows(t):
    if isinstance(t, int):
        return pl.ds(t * N_SEG, N_SEG)
    return pl.ds(pl.multiple_of(t * N_SEG, N_SEG), N_SEG)


def _scan_forward(xs, a_re, a_im, pw_re, pw_im, cs, seg):
    are = jnp.broadcast_to(a_re, (N_SEG, ST_T))
    aim = jnp.broadcast_to(a_im, (N_SEG, ST_T))

    def steps(k, carry):
        xr, xi = carry
        for j in range(SCAN_UNROLL):
            r = pl.multiple_of((k * SCAN_UNROLL + j) * N_SEG, N_SEG)
            nr = are * xr - aim * xi + xs[pl.ds(r, N_SEG), 0:ST_T]
            ni = are * xi + aim * xr + xs[pl.ds(r, N_SEG), ST_T:2 * ST_T]
            xs[pl.ds(r, N_SEG), 0:ST_T] = nr
            xs[pl.ds(r, N_SEG), ST_T:2 * ST_T] = ni
            xr, xi = nr, ni
        return xr, xi

    zero = jnp.zeros((N_SEG, ST_T), F32)
    fr, fi = lax.fori_loop(0, seg // SCAN_UNROLL, steps, (zero, zero))
    sr, si = pw_re[seg - 1:seg, :], pw_im[seg - 1:seg, :]
    cr = jnp.zeros((1, ST_T), F32)
    ci = jnp.zeros((1, ST_T), F32)
    cs[0:1, :] = cr
    cs[8:9, :] = ci
    for s in range(1, N_SEG):
        ncr = sr * cr - si * ci + fr[s - 1:s, :]
        nci = sr * ci + si * cr + fi[s - 1:s, :]
        cr, ci = ncr, nci
        cs[s:s + 1, :] = cr
        cs[8 + s:9 + s, :] = ci
    car, cai = cs[0:8, :], cs[8:16, :]

    def fix(t, _):
        r = pl.multiple_of(t * N_SEG, N_SEG)
        pr, pi = pw_re[pl.ds(t, 1), :], pw_im[pl.ds(t, 1), :]
        xs[pl.ds(r, N_SEG), 0:ST_T] = xs[pl.ds(r, N_SEG), 0:ST_T] + (pr * car - pi * cai)
        xs[pl.ds(r, N_SEG), ST_T:2 * ST_T] = xs[pl.ds(r, N_SEG), ST_T:2 * ST_T] + (pr * cai + pi * car)
        return 0

    lax.fori_loop(0, seg, fix, 0, unroll=SCAN_UNROLL)


def _interleave(src, dst, seg):
    for s in range(N_SEG):
        dst[pl.ds(s, seg, stride=N_SEG), :] = src[s]


def _deinterleave(src, seg, s):
    return src[pl.ds(s, seg, stride=N_SEG), :]


def _ssm_forward(u, bcat, ccat, a_re, a_im, pw_re, pw_im, d_row, late, seg):
    bl = u.shape[0]
    rows = N_SEG * seg
    n = len(late)
    steps = bl * N_GT

    def body(*refs):
        u_ref, b_ref, c_ref, ar_ref, ai_ref, pr_ref, pi_ref, d_ref = refs[:8]
        late_r = refs[8:8 + n]
        y_ref, xs_ref, cs_ref = refs[8 + n:11 + n]
        gath, cast = refs[11 + n:11 + 2 * n], refs[11 + 2 * n:11 + 3 * n]
        send_sems, recv_sems, local_sems, ui, yi = refs[11 + 3 * n:]
        step = pl.program_id(0) * N_GT + pl.program_id(1)
        start, relay, finish = _gather_phases(late_r, gath, cast, send_sems, recv_sems, local_sems)
        pl.when(step == 0)(start)
        _interleave(u_ref.at[0], ui, seg)
        u = ui[...]
        xs, cs = xs_ref.at[0, 0], cs_ref.at[0, 0]
        xs[:, 0:ST_T] = _mm(u, b_ref[0, :, 0:ST_T])
        xs[:, ST_T:] = _mm(u, b_ref[0, :, ST_T:])
        _scan_forward(xs, ar_ref[...], ai_ref[...], pr_ref, pi_ref, cs, seg)
        yi[...] = _mm(xs[:, 0:ST_T], c_ref[0, 0:ST_T, :]) + _mm(xs[:, ST_T:], c_ref[0, ST_T:, :]) + d_ref[...] * u
        for s in range(N_SEG):
            y_ref[0, s] = _deinterleave(yi, seg, s)
        pl.when(step == steps // 2)(relay)
        pl.when(step == steps - 1)(finish)

    state = lambda r, c: pl.BlockSpec((1, 1, r, c), lambda b, j: (b, j, 0, 0))
    act = pl.BlockSpec((1, N_SEG, seg, CH_T), lambda b, j: (b, 0, 0, j))
    g_specs, g_shapes, g_scratch = _gather_operands(late)
    res = pl.pallas_call(
        body, name="ssm_forward", grid=(bl, N_GT),
        in_specs=[act,
                  pl.BlockSpec((1, CH_T, 2 * ST_T), lambda b, j: (j, 0, 0)),
                  pl.BlockSpec((1, 2 * ST_T, CH_T), lambda b, j: (j, 0, 0)),
                  pl.BlockSpec((1, ST_T), lambda b, j: (0, j)), pl.BlockSpec((1, ST_T), lambda b, j: (0, j)),
                  pl.BlockSpec((seg, ST_T), lambda b, j: (0, j)), pl.BlockSpec((seg, ST_T), lambda b, j: (0, j)),
                  pl.BlockSpec((1, CH_T), lambda b, j: (0, j))]
        + [pl.BlockSpec(s.shape, lambda b, j: (0, 0)) for s in late],
        out_specs=(act, state(rows, 2 * ST_T), state(16, ST_T)) + g_specs,
        out_shape=(jax.ShapeDtypeStruct((bl, N_SEG, seg, D_SSM), F32),
                   jax.ShapeDtypeStruct((bl, N_GT, rows, 2 * ST_T), F32),
                   jax.ShapeDtypeStruct((bl, N_GT, 16, ST_T), F32)) + g_shapes,
        scratch_shapes=g_scratch + [pltpu.VMEM((rows, CH_T), F32), pltpu.VMEM((rows, CH_T), F32)],
        compiler_params=_tc_params(("arbitrary", "arbitrary")),
    )(u, bcat, ccat, a_re, a_im, pw_re, pw_im, d_row, *late)
    return res[:3], list(res[3:])


def _ssm_backward(u, dy, states, carries, bcat_t, ccat_t, a_re, a_im, pw_re, pw_im, d_row, late16, late32, seg):
    bl = u.shape[0]
    rows = N_SEG * seg
    n = len(late16)
    grid_steps = N_GT * bl

    def body(*refs):
        u_ref, dy_ref, xs_ref, cs_ref, bt_ref, ct_ref, ar_ref, ai_ref, pr_ref, pi_ref, d_ref = refs[:11]
        g16_r, g32_r = refs[11:11 + n], refs[11 + n:11 + 2 * n]
        du_ref, db_ref, dc_ref, dar_ref, dai_ref, dd_ref = refs[11 + 2 * n:17 + 2 * n]
        red = refs[17 + 2 * n:17 + 3 * n]
        own16, recv1, send2, recv2, own32 = (refs[17 + 3 * n + k * n:17 + 3 * n + (k + 1) * n] for k in range(5))
        s_send, s_recv, s_local, ls, cl, ui, dyi, dui = refs[17 + 8 * n:]
        b = pl.program_id(1)
        step = pl.program_id(0) * bl + b
        start, middle, total, finish = _hosted_reduce_phases(g16_r, g32_r, red, own16, recv1, send2, recv2, own32,
                                                             s_send, s_recv, s_local)
        pl.when(step == 0)(start)
        pl.when(step == grid_steps // 4)(middle)
        pl.when(step == (grid_steps * 3) // 4)(total)
        pl.when(step == grid_steps - 1)(finish)
        _interleave(u_ref.at[0], ui, seg)
        _interleave(dy_ref.at[0], dyi, seg)
        u = ui[...]
        dy = dyi[...]
        xs, cs = xs_ref.at[0, 0], cs_ref.at[0, 0]
        ls[...] = _mm(dy, ct_ref[0])
        are = jnp.broadcast_to(ar_ref[...], (N_SEG, ST_T))
        aim = jnp.broadcast_to(ai_ref[...], (N_SEG, ST_T))

        def steps(k, carry):
            lr, li = carry
            for j in range(SCAN_UNROLL):
                r = pl.multiple_of((seg - 1 - (k * SCAN_UNROLL + j)) * N_SEG, N_SEG)
                nr = are * lr + aim * li + ls[pl.ds(r, N_SEG), 0:ST_T]
                ni = are * li - aim * lr + ls[pl.ds(r, N_SEG), ST_T:2 * ST_T]
                ls[pl.ds(r, N_SEG), 0:ST_T] = nr
                ls[pl.ds(r, N_SEG), ST_T:2 * ST_T] = ni
                lr, li = nr, ni
            return lr, li

        zero = jnp.zeros((N_SEG, ST_T), F32)
        fr, fi = lax.fori_loop(0, seg // SCAN_UNROLL, steps, (zero, zero))
        sr, si = pr_ref[seg - 1:seg, :], pi_ref[seg - 1:seg, :]
        cr = jnp.zeros((1, ST_T), F32)
        ci = jnp.zeros((1, ST_T), F32)
        cl[7:8, :] = cr
        cl[15:16, :] = ci
        for s in range(N_SEG - 2, -1, -1):
            ncr = sr * cr + si * ci + fr[s + 1:s + 2, :]
            nci = sr * ci - si * cr + fi[s + 1:s + 2, :]
            cr, ci = ncr, nci
            cl[s:s + 1, :] = cr
            cl[8 + s:9 + s, :] = ci
        clr, cli = cl[0:8, :], cl[8:16, :]

        def fix_rows(rows, t, xpr, xpi, acc):
            dr, di = acc
            pr, pi = pr_ref[pl.ds(seg - 1 - t, 1), :], pi_ref[pl.ds(seg - 1 - t, 1), :]
            lr = ls[rows, 0:ST_T] + (pr * clr + pi * cli)
            li = ls[rows, ST_T:2 * ST_T] + (pr * cli - pi * clr)
            ls[rows, 0:ST_T] = lr
            ls[rows, ST_T:2 * ST_T] = li
            return dr + (lr * xpr + li * xpi), di + (li * xpr - lr * xpi)

        def fix_at(t, acc):
            prev = _seg_rows(t - 1)
            return fix_rows(_seg_rows(t), t, xs[prev, 0:ST_T], xs[prev, ST_T:2 * ST_T], acc)

        def fix(k, acc):
            for j in range(SCAN_UNROLL):
                acc = fix_at(k * SCAN_UNROLL + j, acc)
            return acc

        acc = fix_rows(pl.ds(0, N_SEG), 0, cs[0:8, :], cs[8:16, :], (zero, zero))
        for t in range(1, SCAN_UNROLL):
            acc = fix_at(t, acc)
        dr, di = lax.fori_loop(1, seg // SCAN_UNROLL, fix, acc)
        dar = jnp.sum(dr, axis=0, keepdims=True)
        dai = jnp.sum(di, axis=0, keepdims=True)
        lall = ls[...]
        dui[...] = _mm(lall, bt_ref[0]) + d_ref[...] * dy
        for s in range(N_SEG):
            du_ref[0, s] = _deinterleave(dui, seg, s).astype(BF16)
        dbp = _mm_tn(u, lall)
        dcp = _mm_tn(dy, xs[...])
        ddp = jnp.sum(dy * u, axis=0, keepdims=True)

        @pl.when(b == 0)
        def _():
            db_ref[0] = dbp
            dc_ref[0] = dcp
            dar_ref[...] = dar
            dai_ref[...] = dai
            dd_ref[...] = ddp

        @pl.when(b != 0)
        def _():
            db_ref[0] += dbp
            dc_ref[0] += dcp
            dar_ref[...] += dar
            dai_ref[...] += dai
            dd_ref[...] += ddp

    tile3 = lambda r, c: pl.BlockSpec((1, r, c), lambda j, b: (j, 0, 0))
    lane = lambda r, c: pl.BlockSpec((r, c), lambda j, b: (0, j))
    act = pl.BlockSpec((1, N_SEG, seg, CH_T), lambda j, b: (b, 0, 0, j))
    state = lambda r, c: pl.BlockSpec((1, 1, r, c), lambda j, b: (b, j, 0, 0))
    r_in, r_out, r_shapes, r_scratch = _hosted_reduce_operands(late16, lambda s: pl.BlockSpec(s, lambda j, b: (0, 0)))
    res = pl.pallas_call(
        body, name="ssm_backward", grid=(N_GT, bl),
        in_specs=[act, act, state(rows, 2 * ST_T), state(16, ST_T), tile3(2 * ST_T, CH_T), tile3(CH_T, 2 * ST_T),
                  lane(1, ST_T), lane(1, ST_T), lane(seg, ST_T), lane(seg, ST_T), lane(1, CH_T)] + r_in,
        out_specs=(act, tile3(CH_T, 2 * ST_T), tile3(CH_T, 2 * ST_T), lane(1, ST_T), lane(1, ST_T), lane(1, CH_T))
        + r_out,
        out_shape=(jax.ShapeDtypeStruct((bl, N_SEG, seg, D_SSM), BF16),
                   jax.ShapeDtypeStruct((N_GT, CH_T, 2 * ST_T), F32), jax.ShapeDtypeStruct((N_GT, CH_T, 2 * ST_T), F32),
                   jax.ShapeDtypeStruct((1, N_STATE), F32), jax.ShapeDtypeStruct((1, N_STATE), F32),
                   jax.ShapeDtypeStruct((1, D_SSM), F32)) + r_shapes,
        scratch_shapes=r_scratch + [pltpu.VMEM((rows, 2 * ST_T), F32), pltpu.VMEM((16, ST_T), F32)]
        + [pltpu.VMEM((rows, CH_T), F32)] * 3,
        compiler_params=_tc_params(("arbitrary", "arbitrary")),
    )(u, dy, states, carries, bcat_t, ccat_t, a_re, a_im, pw_re, pw_im, d_row, *late16, *late32)
    return res[:6], list(res[6:])


def _ssm_param_grads(lam_re, lam_im, log_step, b_re, b_im, da_re, da_im, d_bcat, d_ccat_t):
    def body(lr_ref, li_ref, ls_ref, br_ref, bi_ref, gar_ref, gai_ref, gbcat_ref, gccat_ref,
             dlr_ref, dli_ref, dls_ref, dbr_ref, dbi_ref, dcr_ref, dci_ref, gbr_s, gbi_s):
        same, _, _, rep_t = _tile_masks()
        rep_t = rep_t.astype(F32)
        for j in range(N_GT):
            rows = slice(j * CH_T, (j + 1) * CH_T)
            for src, dsts in ((gbcat_ref, (gbr_s, gbi_s)), (gccat_ref, (dcr_ref, dci_ref))):
                for k, dst in enumerate(dsts):
                    blk = jnp.where(same, src[j, :, k * ST_T:(k + 1) * ST_T], 0.0)
                    dst[rows, :] = jnp.dot(blk, rep_t, precision=lax.Precision.HIGHEST, preferred_element_type=F32)
        dci_ref[...] = -dci_ref[...]
        lr, li = lr_ref[...], li_ref[...]
        step, ar, ai, den, cr, ci = _discretise(lr, li, ls_ref[...])
        crb, cib = _per_channel(cr), _per_channel(ci)
        br, bi = br_ref[...], bi_ref[...]
        gbr, gbi = gbr_s[...], gbi_s[...]
        dbr_ref[...] = crb * gbr + cib * gbi
        dbi_ref[...] = crb * gbi - cib * gbr
        over_channels = lambda t: jnp.sum(t.reshape(SSM_G, SSM_P, SSM_N), axis=1)
        gcr = over_channels(br * gbr + bi * gbi)
        gci = over_channels(br * gbi - bi * gbr)
        ilr, ili = lr / den, -li / den
        gar = gar_ref[...] + (ilr * gcr + ili * gci)
        gai = gai_ref[...] + (ilr * gci - ili * gcr)
        qr, qi = cr * ilr - ci * ili, cr * ili + ci * ilr
        glr = -(qr * gcr + qi * gci)
        gli = -(qr * gci - qi * gcr)
        gwr = ar * gar + ai * gai
        gwi = ar * gai - ai * gar
        dlr_ref[...] = glr + step * gwr
        dli_ref[...] = gli + step * gwi
        dls_ref[...] = jnp.sum(lr * gwr + li * gwi, axis=-1, keepdims=True) * step

    lam = jax.ShapeDtypeStruct((SSM_G, SSM_N), F32)
    mat = jax.ShapeDtypeStruct((SSM_G * SSM_P, SSM_N), F32)
    vm = pl.BlockSpec(memory_space=pltpu.VMEM)
    return pl.pallas_call(
        body, name="ssm_param_grads", out_shape=(lam, lam, jax.ShapeDtypeStruct((SSM_G, 1), F32), mat, mat, mat, mat),
        in_specs=[vm] * 9, out_specs=(vm,) * 7,
        scratch_shapes=[pltpu.VMEM((SSM_G * SSM_P, SSM_N), F32), pltpu.VMEM((SSM_G * SSM_P, SSM_N), F32)],
    )(lam_re, lam_im, log_step, b_re, b_im, da_re, da_im, d_bcat, d_ccat_t)


ROWS4 = Q_PER_KV * ATT_BLOCK
ATT_FWD_STACK = 1


def _att_dist_mask(first_block):
    qi = lax.broadcasted_iota(jnp.int32, (ROWS4, 2 * ATT_BLOCK), 0) & (ATT_BLOCK - 1)
    si = lax.broadcasted_iota(jnp.int32, (ROWS4, 2 * ATT_BLOCK), 1)
    dist = qi + ATT_BLOCK - si
    valid = (dist >= 0) & (dist < ATT_BLOCK) & ((si >= ATT_BLOCK) | jnp.logical_not(first_block))
    return dist.astype(F32), valid


def _stack_heads(x, kv):
    return jnp.concatenate([x[:, (kv * Q_PER_KV + g) * HEAD_DIM:(kv * Q_PER_KV + g + 1) * HEAD_DIM]
                            for g in range(Q_PER_KV)], axis=0)


def _stack_cols(x, kv):
    return jnp.concatenate([x[:, kv * Q_PER_KV + g:kv * Q_PER_KV + g + 1] for g in range(Q_PER_KV)], axis=0)


def _per_head_col(vals):
    return jnp.concatenate([jnp.full((ATT_BLOCK, 1), v, F32) for v in vals], axis=0)


def _attn_forward(q, k, v, sinks, bl, nb):
    t = q.shape[0]

    def body(sink_ref, q_ref, kp_ref, kc_ref, vp_ref, vc_ref, o_ref, lse_ref):
        i = pl.program_id(1)
        dist4, valid4 = _att_dist_mask(i == 0)
        rows2 = ATT_FWD_STACK * ATT_BLOCK
        dist, valid = dist4[0:rows2, :], valid4[0:rows2, :]
        kk = jnp.concatenate([kp_ref[...], kc_ref[...]], axis=0)
        vv = jnp.concatenate([vp_ref[...], vc_ref[...]], axis=0)
        qv = q_ref[...]
        col = lambda vals: jnp.concatenate([jnp.full((ATT_BLOCK, 1), v, F32) for v in vals], axis=0)
        stacks = [range(h0, h0 + ATT_FWD_STACK) for h0 in range(0, N_HEADS, ATT_FWD_STACK)]
        kv_cols = lambda heads: slice(heads[0] // Q_PER_KV * HEAD_DIM, (heads[0] // Q_PER_KV + 1) * HEAD_DIM)
        scores = [_mm_nt(jnp.concatenate([qv[:, h * HEAD_DIM:(h + 1) * HEAD_DIM] for h in heads], axis=0),
                         kk[:, kv_cols(heads)]) for heads in stacks]
        softmaxes = []
        for heads, qk in zip(stacks, scores):
            slope = col([2.0 ** (-(h + 1)) for h in heads])
            sink = col([sink_ref[h] for h in heads])
            s = jnp.where(valid, qk * ATT_SCALE - slope * dist, NEG_INF)
            m = jnp.maximum(jnp.max(s, axis=-1, keepdims=True), sink)
            e = jnp.exp(s - m)
            den = jnp.sum(e, axis=-1, keepdims=True) + jnp.exp(sink - m)
            softmaxes.append((e.astype(BF16), 1.0 / den, m + jnp.log(den)))
        for heads, (e, inv_den, lse) in zip(stacks, softmaxes):
            o = _mm(e, vv[:, kv_cols(heads)]) * inv_den
            for g, h in enumerate(heads):
                rows = slice(g * ATT_BLOCK, (g + 1) * ATT_BLOCK)
                o_ref[:, h * HEAD_DIM:(h + 1) * HEAD_DIM] = o[rows, :]
                lse_ref[:, h:h + 1] = lse[rows, :]

    cur = lambda w: pl.BlockSpec((ATT_BLOCK, w), lambda b, i: (b * nb + i, 0))
    prev = lambda w: pl.BlockSpec((ATT_BLOCK, w), lambda b, i: (b * nb + jnp.maximum(i - 1, 0), 0))
    return pl.pallas_call(
        body, name="attn_forward", grid=(bl, nb),
        in_specs=[pl.BlockSpec(memory_space=pltpu.SMEM), cur(512), prev(128), cur(128), prev(128), cur(128)],
        out_specs=(cur(512), cur(N_HEADS)),
        out_shape=(jax.ShapeDtypeStruct((t, D_ATTN), F32), jax.ShapeDtypeStruct((t, N_HEADS), F32)),
        compiler_params=_tc_params(("arbitrary", "arbitrary")),
    )(sinks, q, k, k, v, v)


def _attn_backward(q, k, v, o, do, lse, sinks, bl, nb):
    t = q.shape[0]

    def body(sink_ref, qc_ref, kp_ref, kc_ref, vp_ref, vc_ref, oc_ref, doc_ref, lc_ref,
             dq_ref, dk_ref, dv_ref, ds_ref, dk_carry, dv_carry):
        b, i = pl.program_id(0), pl.program_id(1)
        live = i < nb

        @pl.when(i == 0)
        def _():
            dk_carry[...] = jnp.zeros((ATT_BLOCK, KV_HEADS * HEAD_DIM), F32)
            dv_carry[...] = jnp.zeros((ATT_BLOCK, KV_HEADS * HEAD_DIM), F32)

        dist, valid = _att_dist_mask(i == 0)
        valid = valid & live
        kk = jnp.concatenate([kp_ref[...], kc_ref[...]], axis=0)
        vv = jnp.concatenate([vp_ref[...], vc_ref[...]], axis=0)
        qc, oc, doc, lc = qc_ref[...], oc_ref[...], doc_ref[...], lc_ref[...]
        dsink_cols, dq_parts, dk_t, dv_t = [], [], [], []
        for kv in range(KV_HEADS):
            heads = range(kv * Q_PER_KV, (kv + 1) * Q_PER_KV)
            cols = slice(kv * HEAD_DIM, (kv + 1) * HEAD_DIM)
            kh, vh = kk[:, cols], vv[:, cols]
            slope = _per_head_col([2.0 ** (-(h + 1)) for h in heads])
            sink = _per_head_col([sink_ref[h] for h in heads])
            q4, do4 = _stack_heads(qc, kv), _stack_heads(doc, kv)
            delta = jnp.sum(do4 * _stack_heads(oc, kv), axis=-1, keepdims=True)
            lse4 = _stack_cols(lc, kv)
            s = _mm_nt(q4, kh) * ATT_SCALE - slope * dist
            p = jnp.where(valid, jnp.exp(s - lse4), 0.0)
            dsc = p * (_mm_nt(do4, vh) - delta)
            dq4 = _mm(dsc, kh) * ATT_SCALE
            dk_t.append(_mm_tn(q4, dsc) * ATT_SCALE)
            dv_t.append(_mm_tn(do4, p))
            dsink4 = jnp.where(live, jnp.exp(sink - lse4) * delta, 0.0)
            for g, h in enumerate(heads):
                rows = slice(g * ATT_BLOCK, (g + 1) * ATT_BLOCK)
                dq_parts.append((h, dq4[rows, :]))
                dsink_cols.append(-jnp.sum(dsink4[rows, :], axis=0, keepdims=True))
        dsink = jnp.concatenate(dsink_cols, axis=1)
        for out_ref, carry, parts in ((dk_ref, dk_carry, dk_t), (dv_ref, dv_carry, dv_t)):
            both = jnp.concatenate(parts, axis=0)
            out_ref[...] = (carry[...] + both[:, 0:ATT_BLOCK]).T
            carry[...] = both[:, ATT_BLOCK:]

        @pl.when(live)
        def _():
            for h, part in dq_parts:
                dq_ref[:, h * HEAD_DIM:(h + 1) * HEAD_DIM] = part

        @pl.when((b == 0) & (i == 0))
        def _():
            ds_ref[...] = dsink

        @pl.when((b != 0) | (i != 0))
        def _():
            ds_ref[...] += dsink

    cur_i = lambda i: jnp.minimum(i, nb - 1)
    cur = lambda w: pl.BlockSpec((ATT_BLOCK, w), lambda b, i: (b * nb + cur_i(i), 0))
    prev = lambda w: pl.BlockSpec((ATT_BLOCK, w), lambda b, i: (b * nb + jnp.maximum(cur_i(i) - 1, 0), 0))
    behind = lambda w: pl.BlockSpec((ATT_BLOCK, w), lambda b, i: (b * nb + jnp.maximum(i - 1, 0), 0))
    return pl.pallas_call(
        body, name="attn_backward", grid=(bl, nb + 1),
        in_specs=[pl.BlockSpec(memory_space=pltpu.SMEM), cur(512), prev(128), cur(128), prev(128), cur(128),
                  cur(512), cur(512), cur(N_HEADS)],
        out_specs=(cur(512), behind(128), behind(128), pl.BlockSpec((1, N_HEADS), lambda b, i: (0, 0))),
        out_shape=(jax.ShapeDtypeStruct((t, D_ATTN), F32), jax.ShapeDtypeStruct((t, 128), F32),
                   jax.ShapeDtypeStruct((t, 128), F32), jax.ShapeDtypeStruct((1, N_HEADS), F32)),
        scratch_shapes=[pltpu.VMEM((ATT_BLOCK, KV_HEADS * HEAD_DIM), F32), pltpu.VMEM((ATT_BLOCK, KV_HEADS * HEAD_DIM), F32)],
        compiler_params=_tc_params(("arbitrary", "arbitrary")),
    )(sinks, q, k, k, v, v, o, do, lse)


def _mix_forward_backward(x2, y2, z_ssm, attn, z_attn, p2, target2, w_glu, b_glu, w_out, g_post, w_gate, b_gate,
                          w_proj, tm):
    t = x2.shape[0]

    def body(x_ref, y_ref, zs_ref, at_ref, za_ref, p_ref, tg_ref,
             wglu_ref, bglu_ref, wout_ref, gpost_ref, wgate_ref, bgate_ref, wproj_ref,
             loss_ref, dh1_ref, dy_ref, dzs_ref, dat_ref, dza_ref,
             dwglu_own_ref, dbglu_ref, dwout_own_ref, dgpost_ref, dwgate_own_ref, dbgate_ref, dwproj_ref,
             dwout16_ref, dwgate16_ref, dwproj16_ref, dwglu16_ref, dwglu_ref, dwout_ref, dwgate_ref):
        i = pl.program_id(0)
        gpost = gpost_ref[...]

        @pl.when(i == 0)
        def _():
            for ref in (dwglu_ref, dbglu_ref, dwout_ref, dgpost_ref, dwgate_ref, dbgate_ref, dwproj_ref, loss_ref):
                ref[...] = jnp.zeros(ref.shape, F32)

        def chain(rows):
            y = y_ref[rows, :]
            u3 = GELU_C * (y + GELU_K * y * y * y)
            th = jnp.tanh(u3)
            gl = 0.5 * y * (1.0 + th)
            a = _mm(gl, wglu_ref[...]) + bglu_ref[...]
            sa = _sigmoid(a)
            glu = gl * sa
            zs = zs_ref[rows, :]
            sgs = _sigmoid(zs)
            ssm_out = glu * (zs * sgs)
            za = za_ref[rows, :]
            sga = _sigmoid(za)
            at = at_ref[rows, :]
            attn_out = at * (za * sga)
            cat = jnp.concatenate([ssm_out, attn_out], axis=-1).astype(BF16)
            mixed = _mm(cat, wout_ref[...])
            r2 = lax.rsqrt(jnp.mean(mixed * mixed, axis=-1, keepdims=True) + EPS)
            nhat = mixed * r2
            h1 = x_ref[rows, :] + nhat * gpost
            gate = _sigmoid(_mm(h1, wgate_ref[...]) + bgate_ref[...])
            pv = p_ref[rows, :]
            pp = _mm(pv, wproj_ref[...])
            h2 = h1 + gate * pp
            err = h2 - tg_ref[rows, :]
            loss_part = jnp.sum(jnp.sum(err * err, axis=-1, keepdims=True), axis=0, keepdims=True) * (0.5 / D_MODEL)
            dh2 = err * (1.0 / D_MODEL)
            dgp = dh2 * pp * gate * (1.0 - gate)
            dpp = dh2 * gate
            dh1 = dh2 + _mm_nt(dgp, wgate_ref[...])
            dwproj_ref[...] += _mm_tn(pv, dpp)
            dwgate_ref[...] += _mm_tn(h1, dgp)
            dh1_ref[rows, :] = dh1
            dnhat = dh1 * gpost
            dmixed = r2 * (dnhat - nhat * jnp.mean(dnhat * nhat, axis=-1, keepdims=True))
            dcat = _mm_nt(dmixed, wout_ref[...])
            dwout_ref[...] += _mm_tn(cat, dmixed)
            dso, dao = dcat[:, 0:D_SSM], dcat[:, D_SSM:]
            dat_ref[rows, :] = dao * (za * sga)
            dza_ref[rows, :] = (dao * at * (sga * (1.0 + za * (1.0 - sga)))).astype(BF16)
            dzs_ref[rows, :] = (dso * glu * (sgs * (1.0 + zs * (1.0 - sgs)))).astype(BF16)
            dglu = dso * (zs * sgs)
            da = dglu * gl * sa * (1.0 - sa)
            dgl = dglu * sa + _mm_nt(da, wglu_ref[...])
            dwglu_ref[...] += _mm_tn(gl, da)
            dgelu = 0.5 * (1.0 + th) + 0.5 * y * (1.0 - th * th) * (GELU_C * (1.0 + 3.0 * GELU_K * y * y))
            dy_ref[rows, :] = dgl * dgelu
            dbglu_ref[...] += jnp.sum(da, axis=0, keepdims=True)
            dgpost_ref[...] += jnp.sum(dh1 * nhat, axis=0, keepdims=True)
            dbgate_ref[...] += jnp.sum(dgp, axis=0, keepdims=True)
            loss_ref[...] += loss_part

        chain(slice(None))

        @pl.when(i == t // tm - 1)
        def _():
            for ref16, ref in ((dwout16_ref, dwout_ref), (dwgate16_ref, dwgate_ref), (dwproj16_ref, dwproj_ref),
                               (dwglu16_ref, dwglu_ref)):
                def to16(r, ref16=ref16, ref=ref):
                    ref16[r, :] = ref[r, :].astype(BF16)

                _row_chunks(ref.shape[0], to16)
            for ref, own_ref in ((dwglu_ref, dwglu_own_ref), (dwout_ref, dwout_own_ref), (dwgate_ref, dwgate_own_ref)):
                _copy_owned_rows(ref, own_ref)

    row = lambda w: pl.BlockSpec((tm, w), lambda i: (i, 0))
    acc = lambda *shape, dt=F32: (_const_spec(shape), jax.ShapeDtypeStruct(shape, dt))
    accs = [acc(1, D_SSM // N_DEV, D_SSM), acc(1, D_SSM), acc(1, D_MODEL // N_DEV, D_MODEL), acc(1, D_MODEL),
            acc(1, D_MODEL // N_DEV, D_MODEL), acc(1, D_MODEL), acc(D_PLE, D_MODEL),
            acc(D_MODEL, D_MODEL, dt=BF16), acc(D_MODEL, D_MODEL, dt=BF16), acc(D_PLE, D_MODEL, dt=BF16),
            acc(D_SSM, D_SSM, dt=BF16)]
    return pl.pallas_call(
        body, name="mix_forward_backward", grid=(t // tm,),
        in_specs=[row(D_MODEL), row(512), row(512), row(512), row(512), row(D_PLE), row(D_MODEL),
                  _const_spec((D_SSM, D_SSM)), _const_spec((1, D_SSM)), _const_spec((D_MODEL, D_MODEL)),
                  _const_spec((1, D_MODEL)), _const_spec((D_MODEL, D_MODEL)), _const_spec((1, D_MODEL)),
                  _const_spec((D_PLE, D_MODEL))],
        out_specs=(_const_spec((1, 1)), row(D_MODEL), row(512), row(512), row(512), row(512))
        + tuple(a[0] for a in accs),
        out_shape=(jax.ShapeDtypeStruct((1, 1), F32), jax.ShapeDtypeStruct((t, D_MODEL), F32),
                   jax.ShapeDtypeStruct((t, 512), F32),
                   jax.ShapeDtypeStruct((t, 512), BF16), jax.ShapeDtypeStruct((t, 512), F32),
                   jax.ShapeDtypeStruct((t, 512), BF16)) + tuple(a[1] for a in accs),
        scratch_shapes=[pltpu.VMEM((D_SSM, D_SSM), F32), pltpu.VMEM((D_MODEL, D_MODEL), F32),
                        pltpu.VMEM((D_MODEL, D_MODEL), F32)],
        compiler_params=_tc_params(("arbitrary",)),
    )(x2, y2, z_ssm, attn, z_attn, p2, target2, w_glu, b_glu, w_out, g_post, w_gate, b_gate, w_proj)


def _in_backward(x2, dh1, du, dz_ssm, dq, dk, dv, dz_attn, g_pre, w_in, tm):
    t = x2.shape[0]

    def body(x_ref, dh1_ref, du_ref, dzs_ref, dq_ref, dk_ref, dv_ref, dza_ref, g_ref, w_ref,
             gx_ref, dw_own_ref, dg_ref, dw16_ref, dw_ref):
        i = pl.program_id(0)

        @pl.when(i == 0)
        def _():
            dw_ref[...] = jnp.zeros((D_IN, D_MODEL), F32)
            dg_ref[...] = jnp.zeros((1, D_MODEL), F32)

        xv = x_ref[...]
        r = lax.rsqrt(jnp.mean(xv * xv, axis=-1, keepdims=True) + EPS)
        xhat = xv * r
        g = g_ref[...]
        hn = (xhat * g).astype(BF16)
        dproj = jnp.concatenate([du_ref[...].astype(BF16), dzs_ref[...].astype(BF16), dq_ref[...].astype(BF16),
                                 dk_ref[...].astype(BF16), dv_ref[...].astype(BF16), dza_ref[...].astype(BF16)],
                                axis=-1)
        dhn = _mm(dproj, w_ref[...])
        dxhat = dhn * g
        gx_ref[...] = dh1_ref[...] + r * (dxhat - xhat * jnp.mean(dxhat * xhat, axis=-1, keepdims=True))
        dw_ref[...] += _mm_tn(dproj, hn)
        dg_ref[...] += jnp.sum(dhn * xhat, axis=0, keepdims=True)

        @pl.when(i == t // tm - 1)
        def _():
            def to16(r):
                dw16_ref[r, :] = dw_ref[r, :].astype(BF16)

            _row_chunks(D_IN, to16)
            _copy_owned_rows(dw_ref, dw_own_ref)

    row = lambda w: pl.BlockSpec((tm, w), lambda i: (i, 0))
    own = (1, D_IN // N_DEV, D_MODEL)
    return pl.pallas_call(
        body, name="in_backward", grid=(t // tm,),
        in_specs=[row(D_MODEL), row(D_MODEL), row(512), row(512), row(512), row(128), row(128), row(512),
                  _const_spec((1, D_MODEL)), _const_spec((D_IN, D_MODEL))],
        out_specs=(row(D_MODEL), _const_spec(own), _const_spec((1, D_MODEL)), _const_spec((D_IN, D_MODEL))),
        out_shape=(jax.ShapeDtypeStruct((t, D_MODEL), F32), jax.ShapeDtypeStruct(own, F32),
                   jax.ShapeDtypeStruct((1, D_MODEL), F32), jax.ShapeDtypeStruct((D_IN, D_MODEL), BF16)),
        scratch_shapes=[pltpu.VMEM((D_IN, D_MODEL), F32)],
        compiler_params=_tc_params(("arbitrary",)),
    )(x2, dh1, du, dz_ssm, dq, dk, dv, dz_attn, g_pre, w_in)


def _local_step(x, p, target, pre_norm_g, w_in, prep, ssm_lam_re, ssm_lam_im, ssm_log_step, ssm_b_re, ssm_b_im, ssm_d,
                ssm_b_glu, attn_sinks, post_norm_g, pl_b_gate, late):
    bl, seq, _ = x.shape
    seg = seq // N_SEG
    nb = seq // ATT_BLOCK
    t = bl * seq
    x2 = x.reshape(t, D_MODEL)
    p2 = p.reshape(t, D_PLE)
    tg2 = target.reshape(t, D_MODEL)

    lam_re, lam_im = ssm_lam_re, ssm_lam_im
    log_step = ssm_log_step.reshape(SSM_G, 1)
    a_re_row, a_im_row, pw_re, pw_im, bcat, bcat_t, ccat, ccat_t = prep
    d_row = ssm_d.reshape(1, D_SSM)

    segments = lambda a: a.reshape(bl, N_SEG, seg, D_SSM)
    u, z_ssm, q, k, v, z_attn = _in_proj(x2, pre_norm_g.reshape(1, D_MODEL), w_in, min(TOKEN_TILE_WIDE, t))
    (y, states, carries), gathered = _ssm_forward(
        segments(u), bcat, ccat, a_re_row, a_im_row, pw_re, pw_im, d_row, late, seg)
    w_out, w_gate, w_proj, w_glu = (_gathered_to_full(n, g) for n, g in zip(LATE_NAMES, gathered))
    sinks = attn_sinks.reshape(N_HEADS)
    attn, lse = _attn_forward(q, k, v, sinks, bl, nb)
    (loss, dh1, dy, dz_ssm, dattn, dz_attn, d_w_glu, d_b_glu, d_w_out, d_g_post, d_w_gate, d_b_gate,
     d_w_proj, *late16) = _mix_forward_backward(
        x2, y.reshape(t, D_SSM), z_ssm, attn, z_attn, p2, tg2, w_glu,
        ssm_b_glu.reshape(1, D_SSM), w_out, post_norm_g.reshape(1, D_MODEL), w_gate, pl_b_gate.reshape(1, D_MODEL),
        w_proj, min(TOKEN_TILE, t))
    owned = lambda ds: [_full_to_owned(n, d) for n, d in zip(LATE_NAMES, ds)]
    dq, dk, dv, d_sinks = _attn_backward(q, k, v, attn, dattn, lse, sinks, bl, nb)
    (du, d_bcat, d_ccat_t, da_re, da_im, d_d), late_grads = _ssm_backward(
        segments(u), segments(dy), states, carries, bcat_t, ccat_t, a_re_row, a_im_row, pw_re, pw_im,
        d_row, owned(late16), [d_w_out, d_w_gate, _full_to_owned("pl_w_proj", d_w_proj), d_w_glu], seg)
    grad_x, d_w_in, d_g_pre, d_w_in16 = _in_backward(
        x2, dh1, du.reshape(t, D_SSM), dz_ssm, dq, dk, dv, dz_attn, pre_norm_g.reshape(1, D_MODEL), w_in,
        min(TOKEN_TILE_WIDE, t))
    d_lam_re, d_lam_im, d_ls, d_b_re, d_b_im, d_c_re, d_c_im = _ssm_param_grads(
        lam_re, lam_im, log_step, ssm_b_re, ssm_b_im, da_re.reshape(SSM_G, SSM_N), da_im.reshape(SSM_G, SSM_N),
        d_bcat, d_ccat_t)
    grads = {
        "pre_norm_g": d_g_pre, "w_in": d_w_in, "w_in16": d_w_in16, "ssm_lam_re": d_lam_re, "ssm_lam_im": d_lam_im,
        "ssm_log_step": d_ls, "ssm_b_re": d_b_re, "ssm_b_im": d_b_im, "ssm_c_re": d_c_re, "ssm_c_im": d_c_im,
        "ssm_d": d_d, "ssm_b_glu": d_b_glu, "attn_sinks": d_sinks, "post_norm_g": d_g_post, "pl_b_gate": d_b_gate,
    }
    return loss, grad_x.reshape(bl, seq, D_MODEL), grads, late_grads


LATE_NAMES = ("w_out", "pl_w_gate", "pl_w_proj", "ssm_w_glu")
BIG_NAMES = ("w_in",) + LATE_NAMES
COL_SHARDED = {"w_in": D_IN // N_DEV, "pl_w_proj": D_MODEL // N_DEV}
WEIGHT_NAMES = ("pre_norm_g", "w_in", "ssm_lam_re", "ssm_lam_im", "ssm_log_step", "ssm_b_re", "ssm_b_im", "ssm_c_re",
                "ssm_c_im", "ssm_d", "ssm_w_glu", "ssm_b_glu", "attn_sinks", "w_out", "post_norm_g", "pl_w_proj",
                "pl_w_gate", "pl_b_gate")


TRANSPOSED = {"w_in": (0, 1), "ssm_b_re": (1, 2), "ssm_b_im": (1, 2)}


def _kernel_form(name, a):
    a = a[0]
    if name in TRANSPOSED:
        a = jnp.swapaxes(a, *TRANSPOSED[name])
    if name in ("ssm_b_re", "ssm_b_im", "ssm_c_re", "ssm_c_im"):
        a = a.reshape(SSM_G * SSM_P, SSM_N)
    return a


def _given_form(name, a, shape):
    if name in TRANSPOSED:
        i, j = TRANSPOSED[name]
        swapped = list(shape[1:])
        swapped[i], swapped[j] = swapped[j], swapped[i]
        return jnp.swapaxes(a.reshape(swapped), i, j).reshape(shape)
    return a.reshape(shape)


def _gathered_to_full(name, g):
    _, rows, cols = g.shape
    if name in COL_SHARDED:
        return jnp.swapaxes(g, 0, 1).reshape(rows, N_DEV * cols)
    return g.reshape(N_DEV * rows, cols)


def _full_to_owned(name, full):
    if name in COL_SHARDED:
        return jnp.swapaxes(full.reshape(full.shape[0], N_DEV, COL_SHARDED[name]), 0, 1)
    return full.reshape(N_DEV, full.shape[0] // N_DEV, full.shape[1])


def kernel(x, p, pre_norm_g, w_in, ssm_lam_re, ssm_lam_im, ssm_log_step, ssm_b_re, ssm_b_im, ssm_c_re, ssm_c_im, ssm_d, ssm_w_glu, ssm_b_glu, attn_sinks, w_out, post_norm_g, pl_w_proj, pl_w_gate, pl_b_gate, loss_target, m_pre_norm_g, m_w_in, m_ssm_lam_re, m_ssm_lam_im, m_ssm_log_step, m_ssm_b_re, m_ssm_b_im, m_ssm_c_re, m_ssm_c_im, m_ssm_d, m_ssm_w_glu, m_ssm_b_glu, m_attn_sinks, m_w_out, m_post_norm_g, m_pl_w_proj, m_pl_w_gate, m_pl_b_gate, v_pre_norm_g, v_w_in, v_ssm_lam_re, v_ssm_lam_im, v_ssm_log_step, v_ssm_b_re, v_ssm_b_im, v_ssm_c_re, v_ssm_c_im, v_ssm_d, v_ssm_w_glu, v_ssm_b_glu, v_attn_sinks, v_w_out, v_post_norm_g, v_pl_w_proj, v_pl_w_gate, v_pl_b_gate):
    w = dict(pre_norm_g=pre_norm_g, w_in=w_in, ssm_lam_re=ssm_lam_re, ssm_lam_im=ssm_lam_im, ssm_log_step=ssm_log_step,
             ssm_b_re=ssm_b_re, ssm_b_im=ssm_b_im, ssm_c_re=ssm_c_re, ssm_c_im=ssm_c_im, ssm_d=ssm_d, ssm_w_glu=ssm_w_glu,
             ssm_b_glu=ssm_b_glu, attn_sinks=attn_sinks, w_out=w_out, post_norm_g=post_norm_g, pl_w_proj=pl_w_proj,
             pl_w_gate=pl_w_gate, pl_b_gate=pl_b_gate)
    m = dict(pre_norm_g=m_pre_norm_g, w_in=m_w_in, ssm_lam_re=m_ssm_lam_re, ssm_lam_im=m_ssm_lam_im,
             ssm_log_step=m_ssm_log_step, ssm_b_re=m_ssm_b_re, ssm_b_im=m_ssm_b_im, ssm_c_re=m_ssm_c_re,
             ssm_c_im=m_ssm_c_im, ssm_d=m_ssm_d, ssm_w_glu=m_ssm_w_glu, ssm_b_glu=m_ssm_b_glu, attn_sinks=m_attn_sinks,
             w_out=m_w_out, post_norm_g=m_post_norm_g, pl_w_proj=m_pl_w_proj, pl_w_gate=m_pl_w_gate,
             pl_b_gate=m_pl_b_gate)
    v = dict(pre_norm_g=v_pre_norm_g, w_in=v_w_in, ssm_lam_re=v_ssm_lam_re, ssm_lam_im=v_ssm_lam_im,
             ssm_log_step=v_ssm_log_step, ssm_b_re=v_ssm_b_re, ssm_b_im=v_ssm_b_im, ssm_c_re=v_ssm_c_re,
             ssm_c_im=v_ssm_c_im, ssm_d=v_ssm_d, ssm_w_glu=v_ssm_w_glu, ssm_b_glu=v_ssm_b_glu, attn_sinks=v_attn_sinks,
             w_out=v_w_out, post_norm_g=v_post_norm_g, pl_w_proj=v_pl_w_proj, pl_w_gate=v_pl_w_gate,
             pl_b_gate=v_pl_b_gate)
    kf = lambda d: {n: _kernel_form(n, a) for n, a in d.items()}
    wk, mk, vk = kf(w), kf(m), kf(v)

    (gathered,), prep = _allgather_weights([wk["w_in"]], *_ssm_prep(
        wk["ssm_lam_re"], wk["ssm_lam_im"], wk["ssm_log_step"].reshape(SSM_G, 1), wk["ssm_b_re"], wk["ssm_b_im"],
        wk["ssm_c_re"], wk["ssm_c_im"], x.shape[1] // N_SEG))
    loss, grad_x, grads, g_late = _local_step(
        x, p[0], loss_target, wk["pre_norm_g"], gathered.reshape(D_IN, D_MODEL), prep, wk["ssm_lam_re"],
        wk["ssm_lam_im"], wk["ssm_log_step"], wk["ssm_b_re"], wk["ssm_b_im"], wk["ssm_d"],
        wk["ssm_b_glu"], wk["attn_sinks"], wk["post_norm_g"], wk["pl_b_gate"], [wk[n] for n in LATE_NAMES])

    owned = lambda g: g.reshape(N_DEV, D_IN // N_DEV, D_MODEL)
    tiny_form = lambda d: [d[n].reshape(rows, cols) for n, rows, cols in TINY]
    med_form = lambda d: [d[n].reshape(N_DEV, rows // N_DEV, cols) for n, rows, cols in MEDIUM]
    g_big, loss, g_tiny, g_med = _reduce_final(
        [owned(grads["w_in16"])], [grads["w_in"]], loss, tiny_form(grads), med_form(grads))
    names = BIG_NAMES + tuple(n for n, _, _ in TINY + MEDIUM)
    form = lambda d: [d[n] for n in BIG_NAMES] + tiny_form(d) + med_form(d)
    updated = _adamw_update(g_big + g_late + g_tiny + g_med, form(wk), form(mk), form(vk), len(BIG_NAMES))
    vals = dict(zip(names, updated))
    results = [[_given_form(n, vals[n][kind], w[n].shape) for n in WEIGHT_NAMES] for kind in range(4)]
    return (loss.reshape(()), grad_x, *results[0], *results[1], *results[2], *results[3])
```

```python
import functools
import math

import jax
import jax.numpy as jnp
from jax import lax
from jax.experimental import pallas as pl
from jax.experimental.pallas import tpu as pltpu

F32 = jnp.float32
BF16 = jnp.bfloat16

D_MODEL = 1024
D_SSM = 512
D_ATTN = 512
SSM_P = 16
SSM_G = 32
SSM_N = 64
N_HEADS = 8
KV_HEADS = 2
Q_PER_KV = 4
HEAD_DIM = 64
ATT_BLOCK = 128
D_PLE = 256
D_IN = 2304
EPS = 1e-6
N_DEV = 8
N_SEG = 8
G_TILE = 8
N_GT = SSM_G // G_TILE
CH_T = G_TILE * SSM_P
ST_T = G_TILE * SSM_N
N_STATE = SSM_G * SSM_N
SCAN_UNROLL = 4
TOKEN_TILE = 256
TOKEN_TILE_WIDE = 512
LANES = 128
VMEM_LIMIT = 60 * 1024 * 1024

ADAM_LR = 0.001
ADAM_B1 = 0.9
ADAM_B2 = 0.999
ADAM_EPS = 1e-08
ADAM_WD = 0.01
ADAM_STEP = 10

GELU_C = math.sqrt(2.0 / math.pi)
GELU_K = 0.044715
ATT_SCALE = 1.0 / math.sqrt(HEAD_DIM)
NEG_INF = float("-inf")


def _mm(a, b):
    return jnp.dot(a.astype(BF16), b.astype(BF16), preferred_element_type=F32)


def _mm_nt(a, b):
    return lax.dot_general(a.astype(BF16), b.astype(BF16), (((1,), (1,)), ((), ())), preferred_element_type=F32)


def _mm_tn(a, b):
    return lax.dot_general(a.astype(BF16), b.astype(BF16), (((0,), (0,)), ((), ())), preferred_element_type=F32)


def _sigmoid(x):
    return 1.0 / (1.0 + jnp.exp(-x))


def _tc_params(sem):
    return pltpu.CompilerParams(dimension_semantics=sem, vmem_limit_bytes=VMEM_LIMIT)


def _const_spec(shape):
    nd = len(shape)
    return pl.BlockSpec(shape, lambda *_: (0,) * nd)


def _mesh_pos():
    return lax.axis_index("x"), lax.axis_index("y"), lax.axis_index("c")


ROW_CHUNKS = (64, 32, 16)


def _row_chunk(nrows):
    return next((c for c in ROW_CHUNKS if nrows % c == 0), None)


def _row_chunks(nrows, fn, chunk=None, init=None):
    chunk = chunk or _row_chunk(nrows)

    def step(i, carry):
        rows = pl.ds(pl.multiple_of(i * chunk, chunk), chunk)
        if init is None:
            fn(rows)
            return carry
        return fn(rows, carry)

    return lax.fori_loop(0, nrows // chunk, step, 0 if init is None else init)


def _slot(px, py, pc):
    return 4 * px + 2 * py + pc


def _copy_owned_rows(acc_ref, own_ref):
    rows = own_ref.shape[1]
    own_ref[0] = acc_ref[pl.ds(pl.multiple_of(_slot(*_mesh_pos()) * rows, 8), rows), :]


def _allgather_weights(shards, work=None, work_inputs=(), work_out_shapes=()):
    n, n_wi, n_wo = len(shards), len(work_inputs), len(work_out_shapes)

    def body(*refs):
        srcs, w_in_refs = refs[:n], refs[n:n + n_wi]
        outs, w_out_refs = refs[n + n_wi:2 * n + n_wi], refs[2 * n + n_wi:2 * n + n_wi + n_wo]
        send_sems, recv_sems = refs[2 * n + n_wi + n_wo:]
        x, y, c = _mesh_pos()
        me, sibling = (x, y, c), (x, y, 1 - c)
        chips = [(1 - x, y), (x, 1 - y), (1 - x, 1 - y)]

        def copy(a, k, block, to):
            blk = outs[a].at[_slot(*block)]
            return pltpu.make_async_remote_copy(
                src_ref=blk, dst_ref=blk, send_sem=send_sems.at[7 * a + k], recv_sem=recv_sems.at[7 * a + k],
                device_id=to, device_id_type=pl.DeviceIdType.MESH)

        sends = []
        for a in range(n):
            mine = outs[a].at[_slot(*me)]

            def cast(r, mine=mine, src=srcs[a]):
                mine[r, :] = src[r, :].astype(BF16)

            _row_chunks(srcs[a].shape[0], cast)
            first = [copy(a, 0, me, sibling)] + [copy(a, 1 + j, me, (*chip, c)) for j, chip in enumerate(chips)]
            for cp in first:
                cp.start()
            sends += first
        if work is not None:
            work(w_in_refs, w_out_refs)
        for a in range(n):
            for j, chip in enumerate(chips):
                copy(a, 1 + j, (*chip, c), me).wait_recv()
                fwd = copy(a, 4 + j, (*chip, c), sibling)
                fwd.start()
                sends.append(fwd)
        for a in range(n):
            copy(a, 0, sibling, me).wait_recv()
            for j, chip in enumerate(chips):
                copy(a, 4 + j, (*chip, 1 - c), me).wait_recv()
        for cp in sends:
            cp.wait_send()

    vm = pl.BlockSpec(memory_space=pltpu.VMEM)
    res = pl.pallas_call(
        body, name="allgather_weights",
        out_shape=tuple(jax.ShapeDtypeStruct((N_DEV,) + s.shape, BF16) for s in shards) + tuple(work_out_shapes),
        in_specs=[vm] * (n + n_wi), out_specs=(vm,) * (n + n_wo),
        scratch_shapes=[pltpu.SemaphoreType.DMA((7 * n,)), pltpu.SemaphoreType.DMA((7 * n,))],
        compiler_params=pltpu.CompilerParams(vmem_limit_bytes=VMEM_LIMIT),
    )(*shards, *work_inputs)
    return list(res[:n]), list(res[n:])


def _adamw(w, g, m, v):
    m = ADAM_B1 * m + (1.0 - ADAM_B1) * g
    v = ADAM_B2 * v + (1.0 - ADAM_B2) * (g * g)
    m_hat = m / (1.0 - ADAM_B1 ** ADAM_STEP)
    v_hat = v / (1.0 - ADAM_B2 ** ADAM_STEP)
    delta = -ADAM_LR * (m_hat / (jnp.sqrt(v_hat) + ADAM_EPS) + ADAM_WD * w)
    return delta, m, v


def _remote(src, dst, send_sems, recv_sems, k, to):
    return pltpu.make_async_remote_copy(src_ref=src, dst_ref=dst, send_sem=send_sems.at[k], recv_sem=recv_sems.at[k],
                                        device_id=to, device_id_type=pl.DeviceIdType.MESH)


def _big_reduce_phases(g16_r, go_r, outs, send2, recv1, recv2, s_send, s_recv):
    n = len(g16_r)
    x, y, c = _mesh_pos()
    sibling = (x, y, 1 - c)
    chips = [(1 - x, y), (x, 1 - y), (1 - x, 1 - y)]
    all_chips = [(x, y)] + chips
    lvl1 = []
    for a in range(n):
        cps = [_remote(g16_r[a].at[_slot(*chip, 1 - c)], recv1[a].at[j], s_send, s_recv, 7 * a + j, sibling)
               for j, chip in enumerate(all_chips)]
        for cp in cps:
            cp.start()
        lvl1.append(cps)
    yield
    lvl2 = []
    for a in range(n):
        for cp in lvl1[a]:
            cp.wait_recv()
        og = outs[a]

        def partials(r, a=a, og=og):
            og[r, :] = go_r[a][r, :] + recv1[a][0, r, :].astype(F32)
            for j, chip in enumerate(chips):
                mine16 = g16_r[a][_slot(*chip, c), r, :].astype(F32)
                send2[a][j, r, :] = (mine16 + recv1[a][1 + j, r, :].astype(F32)).astype(BF16)

        _row_chunks(go_r[a].shape[0], partials)
        cps = [_remote(send2[a].at[j], recv2[a].at[j], s_send, s_recv, 7 * a + 4 + j, (*chip, c))
               for j, chip in enumerate(chips)]
        for cp in cps:
            cp.start()
        lvl2.append(cps)
    yield
    for a in range(n):
        for cp in lvl2[a]:
            cp.wait_recv()
        og = outs[a]

        def total(r, a=a, og=og):
            g = og[r, :]
            for j in range(3):
                g = g + recv2[a][j, r, :].astype(F32)
            og[r, :] = g

        _row_chunks(go_r[a].shape[0], total)
    yield
    for cps in lvl1 + lvl2:
        for cp in cps:
            cp.wait_send()


def _adamw_update(g, w, m, v, n_streamed):
    n = len(g)
    ns = n_streamed

    def body(*refs):
        g_r, w_r, m_r, v_r = (refs[i * n:(i + 1) * n] for i in range(4))
        outs = refs[4 * n:8 * n]
        in_buf, out_buf = refs[8 * n:8 * n + 4 * ns], refs[8 * n + 4 * ns:8 * n + 8 * ns]
        in_sems, out_sems = refs[8 * n + 8 * ns:]
        loads = [[pltpu.make_async_copy(src[a], in_buf[4 * a + k], in_sems.at[4 * a + k])
                  for k, src in enumerate((g_r, w_r, m_r, v_r))] for a in range(ns)]
        stores = [[pltpu.make_async_copy(out_buf[4 * a + k], outs[4 * a + k], out_sems.at[4 * a + k]) for k in range(4)]
                  for a in range(ns)]
        for cps in loads:
            for cp in cps:
                cp.start()
        for a in range(n):
            if a < ns:
                for cp in loads[a]:
                    cp.wait()
                gs, ws, ms, vs = in_buf[4 * a:4 * a + 4]
                og, od, om, ov = out_buf[4 * a:4 * a + 4]
            else:
                gs, ws, ms, vs = g_r[a], w_r[a], m_r[a], v_r[a]
                og, od, om, ov = outs[4 * a:4 * a + 4]

            def update(idx, gs=gs, ws=ws, ms=ms, vs=vs, og=og, od=od, om=om, ov=ov):
                gv = gs[idx]
                d, nm, nv = _adamw(ws[idx], gv, ms[idx], vs[idx])
                og[idx] = gv
                od[idx] = d
                om[idx] = nm
                ov[idx] = nv

            shape = gs.shape
            if len(shape) == 3:
                for b in range(shape[0]):
                    update(b)
            elif _row_chunk(shape[0]) is not None:
                _row_chunks(shape[0], update)
            else:
                update(Ellipsis)
            if a < ns:
                for cp in stores[a]:
                    cp.start()
        for cps in stores:
            for cp in cps:
                cp.wait()

    vm, hbm = pl.BlockSpec(memory_space=pltpu.VMEM), pl.BlockSpec(memory_space=pl.ANY)
    place = lambda: [hbm] * ns + [vm] * (n - ns)
    buf = [pltpu.VMEM(t.shape, F32) for t in g[:ns] for _ in range(4)]
    res = pl.pallas_call(
        body, name="adamw_update",
        out_shape=tuple(jax.ShapeDtypeStruct(t.shape, F32) for t in g for _ in range(4)),
        in_specs=place() * 4, out_specs=tuple(s for a in range(n) for s in [hbm if a < ns else vm] * 4),
        scratch_shapes=buf + buf + [pltpu.SemaphoreType.DMA((4 * ns,)), pltpu.SemaphoreType.DMA((4 * ns,))],
        compiler_params=pltpu.CompilerParams(vmem_limit_bytes=VMEM_LIMIT),
    )(*g, *w, *m, *v)
    return [res[4 * a:4 * a + 4] for a in range(n)]


TINY = (("pre_norm_g", 1, 1024), ("post_norm_g", 1, 1024), ("pl_b_gate", 1, 1024), ("ssm_d", 1, 512),
        ("ssm_b_glu", 1, 512), ("ssm_log_step", 1, 32), ("attn_sinks", 1, 8), ("ssm_lam_re", 32, 64),
        ("ssm_lam_im", 32, 64))
MEDIUM = (("ssm_b_re", SSM_G * SSM_P, SSM_N), ("ssm_b_im", SSM_G * SSM_P, SSM_N), ("ssm_c_re", SSM_G * SSM_P, SSM_N),
          ("ssm_c_im", SSM_G * SSM_P, SSM_N))


def _stage_rows():
    offs, r = {}, 0
    for name, rows, cols in TINY + (("loss", 1, 1),):
        if rows > 1:
            r = -(-r // 8) * 8
        offs[name] = r
        r += rows if rows > 1 else max(cols // LANES, 1)
    return offs, -(-r // 8) * 8


def _reduce_final(g16, g32, loss, g_tiny, g_med):
    nb_, nt, nm_ = len(g16), len(TINY), len(MEDIUM)
    offs, stage_rows = _stage_rows()

    def body(*refs):
        g16_r, go_r = refs[:nb_], refs[nb_:2 * nb_]
        base = 2 * nb_
        loss_r, gt, gm = refs[base], refs[base + 1:base + 1 + nt], refs[base + 1 + nt:base + 1 + nt + nm_]
        base += 1 + nt + nm_
        out_b = refs[base:base + nb_]
        base += nb_
        loss_o, out_t, out_m = refs[base], refs[base + 1:base + 1 + nt], refs[base + 1 + nt:base + 1 + nt + nm_]
        base += 1 + nt + nm_
        send2_b, recv1_b, recv2_b = (refs[base + i * nb_:base + (i + 1) * nb_] for i in range(3))
        base += 3 * nb_
        stage = refs[base]
        recv1, part, recv2 = (refs[base + 1 + i * nm_:base + 1 + (i + 1) * nm_] for i in range(3))
        bs_send, bs_recv, s_send, s_recv, own_sems = refs[base + 1 + 3 * nm_:base + 6 + 3 * nm_]
        own32 = refs[base + 6 + 3 * nm_:]
        fetch = [pltpu.make_async_copy(go_r[a].at[0], own32[a], own_sems.at[a]) for a in range(nb_)]
        for cp in fetch:
            cp.start()
        big = _big_reduce_phases(g16_r, own32, out_b, send2_b, recv1_b, recv2_b, bs_send, bs_recv)
        small = small_phases(loss_r, gt, gm, loss_o, out_t, out_m, stage, recv1, part, recv2, s_send, s_recv)
        next(big)
        next(small)
        for cp in fetch:
            cp.wait()
        next(big)
        for _ in small:
            pass
        for _ in big:
            pass

    def small_phases(loss_r, gt, gm, loss_o, out_t, out_m, stage, recv1, part, recv2, s_send, s_recv):
        x, y, c = _mesh_pos()
        me = _slot(x, y, c)
        sibling = (x, y, 1 - c)
        chips = [(1 - x, y), (x, 1 - y), (1 - x, 1 - y)]
        all_chips = [(x, y)] + chips
        peers = [sibling] + [(*chip, c) for chip in chips] + [(*chip, 1 - c) for chip in chips]
        sem = iter(range(7 + 14 * nm_))
        lvl1 = []
        for a in range(nm_):
            cps = [_remote(gm[a].at[_slot(*chip, 1 - c)], recv1[a].at[j], s_send, s_recv, next(sem), sibling)
                   for j, chip in enumerate(all_chips)]
            for cp in cps:
                cp.start()
            lvl1.append(cps)
        mine = stage.at[me]
        mine[...] = jnp.zeros((stage_rows, LANES), F32)
        for (name, rows, cols), ref in zip(TINY + (("loss", 1, 1),), gt + (loss_r,)):
            r0 = offs[name]
            if rows > 1:
                mine[r0:r0 + rows, 0:cols] = ref[...]
            elif cols >= LANES:
                for i in range(cols // LANES):
                    mine[r0 + i:r0 + i + 1, :] = ref[:, i * LANES:(i + 1) * LANES]
            else:
                mine[r0:r0 + 1, 0:cols] = ref[...]
        tiny_cps = [_remote(mine, mine, s_send, s_recv, next(sem), peer) for peer in peers]
        for cp in tiny_cps:
            cp.start()
        yield
        lvl2 = []
        for a in range(nm_):
            for cp in lvl1[a]:
                cp.wait_recv()
            for j, chip in enumerate(all_chips):
                part[a][j] = gm[a][_slot(*chip, c)] + recv1[a][j]
            cps = [_remote(part[a].at[1 + j], recv2[a].at[j], s_send, s_recv, next(sem), (*chip, c))
                   for j, chip in enumerate(chips)]
            for cp in cps:
                cp.start()
            lvl2.append(cps)
        yield
        lvl3 = []
        for a in range(nm_):
            for cp in lvl2[a]:
                cp.wait_recv()
            blk = out_m[a].at[me]
            blk[...] = ((part[a][0] + recv2[a][0]) + recv2[a][1]) + recv2[a][2]
            cps = [_remote(blk, blk, s_send, s_recv, next(sem), peer) for peer in peers]
            for cp in cps:
                cp.start()
            lvl3.append(cps)
        yield
        for cp in tiny_cps:
            cp.wait_recv()
        tot = stage[0]
        for d in range(1, N_DEV):
            tot = tot + stage[d]
        loss_o[...] = tot[offs["loss"]:offs["loss"] + 1, 0:1]
        for k, (name, rows, cols) in enumerate(TINY):
            r0 = offs[name]
            if rows > 1:
                out_t[k][...] = tot[r0:r0 + rows, 0:cols]
            elif cols >= LANES:
                for i in range(cols // LANES):
                    out_t[k][:, i * LANES:(i + 1) * LANES] = tot[r0 + i:r0 + i + 1, :]
            else:
                out_t[k][...] = tot[r0:r0 + 1, 0:cols]
        for cps in lvl3:
            for cp in cps:
                cp.wait_recv()
        for cps in lvl1 + lvl2 + lvl3 + [tiny_cps]:
            for cp in cps:
                cp.wait_send()

    vmem = pl.BlockSpec(memory_space=pltpu.VMEM)
    t_shapes = [jax.ShapeDtypeStruct((rows, cols), F32) for _, rows, cols in TINY]
    m_shapes = [jax.ShapeDtypeStruct((N_DEV, rows // N_DEV, cols), F32) for _, rows, cols in MEDIUM]
    blk = [(rows // N_DEV, cols) for _, rows, cols in MEDIUM]
    shard = [g.shape[1:] for g in g16]
    scratch = ([pltpu.VMEM((3,) + s, BF16) for s in shard] + [pltpu.VMEM((4,) + s, BF16) for s in shard]
               + [pltpu.VMEM((3,) + s, BF16) for s in shard]
               + [pltpu.VMEM((N_DEV, stage_rows, LANES), F32)]
               + [pltpu.VMEM((4,) + b, F32) for b in blk] + [pltpu.VMEM((4,) + b, F32) for b in blk]
               + [pltpu.VMEM((3,) + b, F32) for b in blk]
               + [pltpu.SemaphoreType.DMA((7 * nb_,)), pltpu.SemaphoreType.DMA((7 * nb_,)),
                  pltpu.SemaphoreType.DMA((7 + 14 * nm_,)), pltpu.SemaphoreType.DMA((7 + 14 * nm_,)),
                  pltpu.SemaphoreType.DMA((nb_,))]
               + [pltpu.VMEM(s, F32) for s in shard])
    n_out = nb_ + 1 + nt + nm_
    res = pl.pallas_call(
        body, name="reduce_final",
        out_shape=tuple(jax.ShapeDtypeStruct(s, F32) for s in shard) + (jax.ShapeDtypeStruct((1, 1), F32),)
        + tuple(t_shapes) + tuple(m_shapes),
        in_specs=[vmem] * nb_ + [pl.BlockSpec(memory_space=pl.ANY)] * nb_ + [vmem] * (1 + nt + nm_),
        out_specs=(vmem,) * n_out, scratch_shapes=scratch,
        compiler_params=pltpu.CompilerParams(vmem_limit_bytes=VMEM_LIMIT),
    )(*g16, *g32, loss, *g_tiny, *g_med)
    return list(res[:nb_]), res[nb_], list(res[nb_ + 1:nb_ + 1 + nt]), list(res[nb_ + 1 + nt:])


def _gather_phases(shard_r, gath, cast, send_sems, recv_sems, local_sems):
    n = len(shard_r)
    x, y, c = _mesh_pos()
    me, sibling = (x, y, c), (x, y, 1 - c)
    chips = [(1 - x, y), (x, 1 - y), (1 - x, 1 - y)]

    def own(a, k, to):
        return _remote(cast[a], gath[a].at[_slot(*me)], send_sems, recv_sems, 7 * a + k, to)

    def passed(a, k, block, to):
        blk = gath[a].at[_slot(*block)]
        return _remote(blk, blk, send_sems, recv_sems, 7 * a + k, to)

    def keep(a):
        return pltpu.make_async_copy(cast[a], gath[a].at[_slot(*me)], local_sems.at[a])

    def start():
        for a in range(n):
            def to16(r, a=a):
                cast[a][r, :] = shard_r[a][r, :].astype(BF16)

            _row_chunks(shard_r[a].shape[0], to16)
            keep(a).start()
            own(a, 0, sibling).start()
            for j, chip in enumerate(chips):
                own(a, 1 + j, (*chip, c)).start()

    def relay():
        for a in range(n):
            for j, chip in enumerate(chips):
                passed(a, 1 + j, (*chip, c), me).wait_recv()
                passed(a, 4 + j, (*chip, c), sibling).start()

    def finish():
        for a in range(n):
            passed(a, 0, sibling, me).wait_recv()
            for j, chip in enumerate(chips):
                passed(a, 4 + j, (*chip, 1 - c), me).wait_recv()
            own(a, 0, sibling).wait_send()
            for j, chip in enumerate(chips):
                own(a, 1 + j, (*chip, c)).wait_send()
                passed(a, 4 + j, (*chip, c), sibling).wait_send()
            keep(a).wait()

    return start, relay, finish


def _gather_operands(shards):
    n = len(shards)
    return ((pl.BlockSpec(memory_space=pl.ANY),) * n,
            tuple(jax.ShapeDtypeStruct((N_DEV,) + s.shape, BF16) for s in shards),
            [pltpu.VMEM(s.shape, BF16) for s in shards]
            + [pltpu.SemaphoreType.DMA((7 * n,)), pltpu.SemaphoreType.DMA((7 * n,)), pltpu.SemaphoreType.DMA((n,))])


def _hosted_reduce_phases(g16_r, g32_r, red, own16, recv1, send2, recv2, own32, s_send, s_recv, s_local):
    n = len(g16_r)
    x, y, c = _mesh_pos()
    sibling = (x, y, 1 - c)
    chips = [(1 - x, y), (x, 1 - y), (1 - x, 1 - y)]
    all_chips = [(x, y)] + chips

    def lvl1(a, j):
        return _remote(g16_r[a].at[_slot(*all_chips[j], 1 - c)], recv1[a].at[j], s_send, s_recv, 7 * a + j, sibling)

    def lvl2(a, j):
        return _remote(send2[a].at[j], recv2[a].at[j], s_send, s_recv, 7 * a + 4 + j, (*chips[j], c))

    def mine(a, j):
        if j == 3:
            only_mine = g32_r[a].shape[0] == 1
            return pltpu.make_async_copy(g32_r[a].at[0 if only_mine else _slot(x, y, c)], own32[a], s_local.at[4 * a + j])
        return pltpu.make_async_copy(g16_r[a].at[_slot(*chips[j], c)], own16[a].at[j], s_local.at[4 * a + j])

    def start():
        for a in range(n):
            for j in range(4):
                mine(a, j).start()
            for j in range(4):
                lvl1(a, j).start()

    def middle():
        for a in range(n):
            for j in range(4):
                mine(a, j).wait()
            for j in range(4):
                lvl1(a, j).wait_recv()

            def partials(r, a=a):
                red[a][r, :] = own32[a][r, :] + recv1[a][0, r, :].astype(F32)
                for j in range(3):
                    send2[a][j, r, :] = (own16[a][j, r, :].astype(F32) + recv1[a][1 + j, r, :].astype(F32)).astype(BF16)

            _row_chunks(own32[a].shape[0], partials)
            for j in range(3):
                lvl2(a, j).start()

    def total():
        for a in range(n):
            for j in range(3):
                lvl2(a, j).wait_recv()

            def add(r, a=a):
                g = red[a][r, :]
                for j in range(3):
                    g = g + recv2[a][j, r, :].astype(F32)
                red[a][r, :] = g

            _row_chunks(own32[a].shape[0], add)

    def finish():
        for a in range(n):
            for j in range(4):
                lvl1(a, j).wait_send()
            for j in range(3):
                lvl2(a, j).wait_send()

    return start, middle, total, finish


def _hosted_reduce_operands(g16, const_spec):
    n = len(g16)
    shard = [g.shape[1:] for g in g16]
    return ([pl.BlockSpec(memory_space=pl.ANY)] * (2 * n),
            tuple(const_spec(s) for s in shard),
            tuple(jax.ShapeDtypeStruct(s, F32) for s in shard),
            [pltpu.VMEM((3,) + s, BF16) for s in shard] + [pltpu.VMEM((4,) + s, BF16) for s in shard]
            + [pltpu.VMEM((3,) + s, BF16) for s in shard] + [pltpu.VMEM((3,) + s, BF16) for s in shard]
            + [pltpu.VMEM(s, F32) for s in shard]
            + [pltpu.SemaphoreType.DMA((7 * n,)), pltpu.SemaphoreType.DMA((7 * n,)), pltpu.SemaphoreType.DMA((4 * n,))])


def _in_proj(x2, g_pre, w_in, tm):
    t = x2.shape[0]

    def body(x_ref, g_ref, w_ref, u_ref, zs_ref, q_ref, k_ref, v_ref, za_ref):
        xv = x_ref[...]
        r = lax.rsqrt(jnp.mean(xv * xv, axis=-1, keepdims=True) + EPS)
        hn = xv * r * g_ref[...]
        proj = _mm_nt(hn, w_ref[...])
        u_ref[...] = proj[:, 0:512]
        zs_ref[...] = proj[:, 512:1024]
        q_ref[...] = proj[:, 1024:1536].astype(BF16)
        k_ref[...] = proj[:, 1536:1664].astype(BF16)
        v_ref[...] = proj[:, 1664:1792].astype(BF16)
        za_ref[...] = proj[:, 1792:2304]

    row = lambda w: pl.BlockSpec((tm, w), lambda i: (i, 0))
    return pl.pallas_call(
        body, name="in_proj", grid=(t // tm,),
        in_specs=[row(D_MODEL), _const_spec((1, D_MODEL)), _const_spec((D_IN, D_MODEL))],
        out_specs=(row(512), row(512), row(512), row(128), row(128), row(512)),
        out_shape=(jax.ShapeDtypeStruct((t, 512), F32),
                   jax.ShapeDtypeStruct((t, 512), F32), jax.ShapeDtypeStruct((t, 512), BF16),
                   jax.ShapeDtypeStruct((t, 128), BF16), jax.ShapeDtypeStruct((t, 128), BF16),
                   jax.ShapeDtypeStruct((t, 512), F32)),
        compiler_params=_tc_params(("arbitrary",)),
    )(x2, g_pre, w_in)


def _discretise(lr, li, ls):
    step = jnp.exp(ls)
    mag = jnp.exp(lr * step)
    ar = mag * jnp.cos(li * step)
    ai = mag * jnp.sin(li * step)
    den = lr * lr + li * li
    cr = ((ar - 1.0) * lr + ai * li) / den
    ci = (ai * lr - (ar - 1.0) * li) / den
    return step, ar, ai, den, cr, ci


def _per_channel(v):
    return jnp.broadcast_to(v[:, None, :], (SSM_G, SSM_P, SSM_N)).reshape(SSM_G * SSM_P, SSM_N)


def _tile_masks():
    r = lax.broadcasted_iota(jnp.int32, (CH_T, ST_T), 0) // SSM_P
    l = lax.broadcasted_iota(jnp.int32, (CH_T, ST_T), 1) // SSM_N
    lt = lax.broadcasted_iota(jnp.int32, (ST_T, CH_T), 0) // SSM_N
    rt = lax.broadcasted_iota(jnp.int32, (ST_T, CH_T), 1) // SSM_P
    rep = lax.broadcasted_iota(jnp.int32, (SSM_N, ST_T), 0) == lax.broadcasted_iota(jnp.int32, (SSM_N, ST_T), 1) % SSM_N
    rep_t = lax.broadcasted_iota(jnp.int32, (ST_T, SSM_N), 0) % SSM_N == lax.broadcasted_iota(jnp.int32, (ST_T, SSM_N), 1)
    return r == l, lt == rt, rep, rep_t


def _ssm_prep(lam_re, lam_im, log_step, b_re, b_im, c_re, c_im, seg):
    def work(in_refs, out_refs):
        lr_ref, li_ref, ls_ref, br_ref, bi_ref, cre_ref, cim_ref, lrr_ref, lir_ref, lsr_ref = in_refs
        ar_ref, ai_ref, pr_ref, pi_ref, bcat_ref, bcat_t_ref, ccat_ref, ccat_t_ref = out_refs
        _, _, _, _, cr, ci = _discretise(lr_ref[...], li_ref[...], ls_ref[...])
        cr, ci = _per_channel(cr), _per_channel(ci)
        br, bi = br_ref[...], bi_ref[...]
        bb_re = cr * br - ci * bi
        bb_im = cr * bi + ci * br
        same, same_t, rep, rep_t = _tile_masks()
        rep, rep_t = rep.astype(BF16), rep_t.astype(BF16)
        for j in range(N_GT):
            rows = slice(j * CH_T, (j + 1) * CH_T)
            for wide, tall, parts in ((bcat_ref, bcat_t_ref, (bb_re[rows], bb_im[rows])),
                                      (ccat_t_ref, ccat_ref, (cre_ref[rows, :], -cim_ref[rows, :]))):
                for k, part in enumerate(parts):
                    p16 = part.astype(BF16)
                    wide[j, :, k * ST_T:(k + 1) * ST_T] = jnp.where(same, _mm(p16, rep), 0.0).astype(BF16)
                    tall[j, k * ST_T:(k + 1) * ST_T, :] = jnp.where(same_t, _mm_nt(rep_t, p16), 0.0).astype(BF16)
        stepr = jnp.exp(lsr_ref[...])
        mag = jnp.exp(lrr_ref[...] * stepr)
        a_r, a_i = mag * jnp.cos(lir_ref[...] * stepr), mag * jnp.sin(lir_ref[...] * stepr)
        p_r, p_i = a_r, a_i
        for k in range(8):
            pr_ref[k:k + 1, :] = p_r
            pi_ref[k:k + 1, :] = p_i
            p_r, p_i = p_r * a_r - p_i * a_i, p_r * a_i + p_i * a_r
        n = 8
        while n < seg:
            tr, ti = pr_ref[n - 1:n, :], pi_ref[n - 1:n, :]
            xr, xi = pr_ref[0:n, :], pi_ref[0:n, :]
            pr_ref[n:2 * n, :] = xr * tr - xi * ti
            pi_ref[n:2 * n, :] = xr * ti + xi * tr
            n *= 2
        ar_ref[...] = pr_ref[0:1, :]
        ai_ref[...] = pi_ref[0:1, :]

    row = jax.ShapeDtypeStruct((1, N_STATE), F32)
    pw = jax.ShapeDtypeStruct((seg, N_STATE), F32)
    wide = jax.ShapeDtypeStruct((N_GT, CH_T, 2 * ST_T), BF16)
    tall = jax.ShapeDtypeStruct((N_GT, 2 * ST_T, CH_T), BF16)
    step_row = jnp.broadcast_to(log_step, (SSM_G, SSM_N)).reshape(1, N_STATE)
    inputs = (lam_re, lam_im, log_step, b_re, b_im, c_re, c_im, lam_re.reshape(1, N_STATE),
              lam_im.reshape(1, N_STATE), step_row)
    return work, inputs, (row, row, pw, pw, wide, tall, tall, wide)


def _seg_rows(t):
    if isinstance(t, int):
        return pl.ds(t * N_SEG, N_SEG)
    return pl.ds(pl.multiple_of(t * N_SEG, N_SEG), N_SEG)


def _scan_forward(xs, a_re, a_im, pw_re, pw_im, cs, seg):
    are = jnp.broadcast_to(a_re, (N_SEG, ST_T))
    aim = jnp.broadcast_to(a_im, (N_SEG, ST_T))

    def steps(k, carry):
        xr, xi = carry
        for j in range(SCAN_UNROLL):
            r = pl.multiple_of((k * SCAN_UNROLL + j) * N_SEG, N_SEG)
            nr = are * xr - aim * xi + xs[pl.ds(r, N_SEG), 0:ST_T]
            ni = are * xi + aim * xr + xs[pl.ds(r, N_SEG), ST_T:2 * ST_T]
            xs[pl.ds(r, N_SEG), 0:ST_T] = nr
            xs[pl.ds(r, N_SEG), ST_T:2 * ST_T] = ni
            xr, xi = nr, ni
        return xr, xi

    zero = jnp.zeros((N_SEG, ST_T), F32)
    fr, fi = lax.fori_loop(0, seg // SCAN_UNROLL, steps, (zero, zero))
    sr, si = pw_re[seg - 1:seg, :], pw_im[seg - 1:seg, :]
    cr = jnp.zeros((1, ST_T), F32)
    ci = jnp.zeros((1, ST_T), F32)
    cs[0:1, :] = cr
    cs[8:9, :] = ci
    for s in range(1, N_SEG):
        ncr = sr * cr - si * ci + fr[s - 1:s, :]
        nci = sr * ci + si * cr + fi[s - 1:s, :]
        cr, ci = ncr, nci
        cs[s:s + 1, :] = cr
        cs[8 + s:9 + s, :] = ci
    car, cai = cs[0:8, :], cs[8:16, :]

    def fix(t, _):
        r = pl.multiple_of(t * N_SEG, N_SEG)
        pr, pi = pw_re[pl.ds(t, 1), :], pw_im[pl.ds(t, 1), :]
        xs[pl.ds(r, N_SEG), 0:ST_T] = xs[pl.ds(r, N_SEG), 0:ST_T] + (pr * car - pi * cai)
        xs[pl.ds(r, N_SEG), ST_T:2 * ST_T] = xs[pl.ds(r, N_SEG), ST_T:2 * ST_T] + (pr * cai + pi * car)
        return 0

    lax.fori_loop(0, seg, fix, 0, unroll=SCAN_UNROLL)


def _interleave(src, dst, seg):
    for s in range(N_SEG):
        dst[pl.ds(s, seg, stride=N_SEG), :] = src[s]


def _deinterleave(src, seg, s):
    return src[pl.ds(s, seg, stride=N_SEG), :]


def _ssm_forward(u, bcat, ccat, a_re, a_im, pw_re, pw_im, d_row, late, seg):
    bl = u.shape[0]
    rows = N_SEG * seg
    n = len(late)
    steps = bl * N_GT

    def body(*refs):
        u_ref, b_ref, c_ref, ar_ref, ai_ref, pr_ref, pi_ref, d_ref = refs[:8]
        late_r = refs[8:8 + n]
        y_ref, xs_ref, cs_ref = refs[8 + n:11 + n]
        gath, cast = refs[11 + n:11 + 2 * n], refs[11 + 2 * n:11 + 3 * n]
        send_sems, recv_sems, local_sems, ui, yi = refs[11 + 3 * n:]
        step = pl.program_id(0) * N_GT + pl.program_id(1)
        start, relay, finish = _gather_phases(late_r, gath, cast, send_sems, recv_sems, local_sems)
        pl.when(step == 0)(start)
        _interleave(u_ref.at[0], ui, seg)
        u = ui[...]
        xs, cs = xs_ref.at[0, 0], cs_ref.at[0, 0]
        xs[:, 0:ST_T] = _mm(u, b_ref[0, :, 0:ST_T])
        xs[:, ST_T:] = _mm(u, b_ref[0, :, ST_T:])
        _scan_forward(xs, ar_ref[...], ai_ref[...], pr_ref, pi_ref, cs, seg)
        yi[...] = _mm(xs[:, 0:ST_T], c_ref[0, 0:ST_T, :]) + _mm(xs[:, ST_T:], c_ref[0, ST_T:, :]) + d_ref[...] * u
        for s in range(N_SEG):
            y_ref[0, s] = _deinterleave(yi, seg, s)
        pl.when(step == steps // 2)(relay)
        pl.when(step == steps - 1)(finish)

    state = lambda r, c: pl.BlockSpec((1, 1, r, c), lambda b, j: (b, j, 0, 0))
    act = pl.BlockSpec((1, N_SEG, seg, CH_T), lambda b, j: (b, 0, 0, j))
    g_specs, g_shapes, g_scratch = _gather_operands(late)
    res = pl.pallas_call(
        body, name="ssm_forward", grid=(bl, N_GT),
        in_specs=[act,
                  pl.BlockSpec((1, CH_T, 2 * ST_T), lambda b, j: (j, 0, 0)),
                  pl.BlockSpec((1, 2 * ST_T, CH_T), lambda b, j: (j, 0, 0)),
                  pl.BlockSpec((1, ST_T), lambda b, j: (0, j)), pl.BlockSpec((1, ST_T), lambda b, j: (0, j)),
                  pl.BlockSpec((seg, ST_T), lambda b, j: (0, j)), pl.BlockSpec((seg, ST_T), lambda b, j: (0, j)),
                  pl.BlockSpec((1, CH_T), lambda b, j: (0, j))]
        + [pl.BlockSpec(s.shape, lambda b, j: (0, 0)) for s in late],
        out_specs=(act, state(rows, 2 * ST_T), state(16, ST_T)) + g_specs,
        out_shape=(jax.ShapeDtypeStruct((bl, N_SEG, seg, D_SSM), F32),
                   jax.ShapeDtypeStruct((bl, N_GT, rows, 2 * ST_T), F32),
                   jax.ShapeDtypeStruct((bl, N_GT, 16, ST_T), F32)) + g_shapes,
        scratch_shapes=g_scratch + [pltpu.VMEM((rows, CH_T), F32), pltpu.VMEM((rows, CH_T), F32)],
        compiler_params=_tc_params(("arbitrary", "arbitrary")),
    )(u, bcat, ccat, a_re, a_im, pw_re, pw_im, d_row, *late)
    return res[:3], list(res[3:])


def _ssm_backward(u, dy, states, carries, bcat_t, ccat_t, a_re, a_im, pw_re, pw_im, d_row, late16, late32, seg):
    bl = u.shape[0]
    rows = N_SEG * seg
    n = len(late16)
    grid_steps = N_GT * bl

    def body(*refs):
        u_ref, dy_ref, xs_ref, cs_ref, bt_ref, ct_ref, ar_ref, ai_ref, pr_ref, pi_ref, d_ref = refs[:11]
        g16_r, g32_r = refs[11:11 + n], refs[11 + n:11 + 2 * n]
        du_ref, db_ref, dc_ref, dar_ref, dai_ref, dd_ref = refs[11 + 2 * n:17 + 2 * n]
        red = refs[17 + 2 * n:17 + 3 * n]
        own16, recv1, send2, recv2, own32 = (refs[17 + 3 * n + k * n:17 + 3 * n + (k + 1) * n] for k in range(5))
        s_send, s_recv, s_local, ls, cl, ui, dyi, dui = refs[17 + 8 * n:]
        b = pl.program_id(1)
        step = pl.program_id(0) * bl + b
        start, middle, total, finish = _hosted_reduce_phases(g16_r, g32_r, red, own16, recv1, send2, recv2, own32,
                                                             s_send, s_recv, s_local)
        pl.when(step == 0)(start)
        pl.when(step == grid_steps // 4)(middle)
        pl.when(step == (grid_steps * 3) // 4)(total)
        pl.when(step == grid_steps - 1)(finish)

        @pl.when(b == 0)
        def _():
            for ref in (db_ref, dc_ref, dar_ref, dai_ref, dd_ref):
                ref[...] = jnp.zeros(ref.shape, F32)

        _interleave(u_ref.at[0], ui, seg)
        _interleave(dy_ref.at[0], dyi, seg)
        u = ui[...]
        dy = dyi[...]
        xs, cs = xs_ref.at[0, 0], cs_ref.at[0, 0]
        ls[...] = _mm(dy, ct_ref[0])
        are = jnp.broadcast_to(ar_ref[...], (N_SEG, ST_T))
        aim = jnp.broadcast_to(ai_ref[...], (N_SEG, ST_T))

        def steps(k, carry):
            lr, li = carry
            for j in range(SCAN_UNROLL):
                r = pl.multiple_of((seg - 1 - (k * SCAN_UNROLL + j)) * N_SEG, N_SEG)
                nr = are * lr + aim * li + ls[pl.ds(r, N_SEG), 0:ST_T]
                ni = are * li - aim * lr + ls[pl.ds(r, N_SEG), ST_T:2 * ST_T]
                ls[pl.ds(r, N_SEG), 0:ST_T] = nr
                ls[pl.ds(r, N_SEG), ST_T:2 * ST_T] = ni
                lr, li = nr, ni
            return lr, li

        zero = jnp.zeros((N_SEG, ST_T), F32)
        fr, fi = lax.fori_loop(0, seg // SCAN_UNROLL, steps, (zero, zero))
        sr, si = pr_ref[seg - 1:seg, :], pi_ref[seg - 1:seg, :]
        cr = jnp.zeros((1, ST_T), F32)
        ci = jnp.zeros((1, ST_T), F32)
        cl[7:8, :] = cr
        cl[15:16, :] = ci
        for s in range(N_SEG - 2, -1, -1):
            ncr = sr * cr + si * ci + fr[s + 1:s + 2, :]
            nci = sr * ci - si * cr + fi[s + 1:s + 2, :]
            cr, ci = ncr, nci
            cl[s:s + 1, :] = cr
            cl[8 + s:9 + s, :] = ci
        clr, cli = cl[0:8, :], cl[8:16, :]

        def fix_rows(rows, t, xpr, xpi, acc):
            dr, di = acc
            pr, pi = pr_ref[pl.ds(seg - 1 - t, 1), :], pi_ref[pl.ds(seg - 1 - t, 1), :]
            lr = ls[rows, 0:ST_T] + (pr * clr + pi * cli)
            li = ls[rows, ST_T:2 * ST_T] + (pr * cli - pi * clr)
            ls[rows, 0:ST_T] = lr
            ls[rows, ST_T:2 * ST_T] = li
            return dr + (lr * xpr + li * xpi), di + (li * xpr - lr * xpi)

        def fix_at(t, acc):
            prev = _seg_rows(t - 1)
            return fix_rows(_seg_rows(t), t, xs[prev, 0:ST_T], xs[prev, ST_T:2 * ST_T], acc)

        def fix(k, acc):
            for j in range(SCAN_UNROLL):
                acc = fix_at(k * SCAN_UNROLL + j, acc)
            return acc

        acc = fix_rows(pl.ds(0, N_SEG), 0, cs[0:8, :], cs[8:16, :], (zero, zero))
        for t in range(1, SCAN_UNROLL):
            acc = fix_at(t, acc)
        dr, di = lax.fori_loop(1, seg // SCAN_UNROLL, fix, acc)
        dar = jnp.sum(dr, axis=0, keepdims=True)
        dai = jnp.sum(di, axis=0, keepdims=True)
        lall = ls[...]
        dui[...] = _mm(lall, bt_ref[0]) + d_ref[...] * dy
        for s in range(N_SEG):
            du_ref[0, s] = _deinterleave(dui, seg, s).astype(BF16)
        db_ref[0] += _mm_tn(u, lall)
        dc_ref[0] += _mm_tn(dy, xs[...])
        dar_ref[...] += dar
        dai_ref[...] += dai
        dd_ref[...] += jnp.sum(dy * u, axis=0, keepdims=True)

    tile3 = lambda r, c: pl.BlockSpec((1, r, c), lambda j, b: (j, 0, 0))
    lane = lambda r, c: pl.BlockSpec((r, c), lambda j, b: (0, j))
    act = pl.BlockSpec((1, N_SEG, seg, CH_T), lambda j, b: (b, 0, 0, j))
    state = lambda r, c: pl.BlockSpec((1, 1, r, c), lambda j, b: (b, j, 0, 0))
    r_in, r_out, r_shapes, r_scratch = _hosted_reduce_operands(late16, lambda s: pl.BlockSpec(s, lambda j, b: (0, 0)))
    res = pl.pallas_call(
        body, name="ssm_backward", grid=(N_GT, bl),
        in_specs=[act, act, state(rows, 2 * ST_T), state(16, ST_T), tile3(2 * ST_T, CH_T), tile3(CH_T, 2 * ST_T),
                  lane(1, ST_T), lane(1, ST_T), lane(seg, ST_T), lane(seg, ST_T), lane(1, CH_T)] + r_in,
        out_specs=(act, tile3(CH_T, 2 * ST_T), tile3(CH_T, 2 * ST_T), lane(1, ST_T), lane(1, ST_T), lane(1, CH_T))
        + r_out,
        out_shape=(jax.ShapeDtypeStruct((bl, N_SEG, seg, D_SSM), BF16),
                   jax.ShapeDtypeStruct((N_GT, CH_T, 2 * ST_T), F32), jax.ShapeDtypeStruct((N_GT, CH_T, 2 * ST_T), F32),
                   jax.ShapeDtypeStruct((1, N_STATE), F32), jax.ShapeDtypeStruct((1, N_STATE), F32),
                   jax.ShapeDtypeStruct((1, D_SSM), F32)) + r_shapes,
        scratch_shapes=r_scratch + [pltpu.VMEM((rows, 2 * ST_T), F32), pltpu.VMEM((16, ST_T), F32)]
        + [pltpu.VMEM((rows, CH_T), F32)] * 3,
        compiler_params=_tc_params(("arbitrary", "arbitrary")),
    )(u, dy, states, carries, bcat_t, ccat_t, a_re, a_im, pw_re, pw_im, d_row, *late16, *late32)
    return res[:6], list(res[6:])


def _ssm_param_grads(lam_re, lam_im, log_step, b_re, b_im, da_re, da_im, d_bcat, d_ccat_t):
    def body(lr_ref, li_ref, ls_ref, br_ref, bi_ref, gar_ref, gai_ref, gbcat_ref, gccat_ref,
             dlr_ref, dli_ref, dls_ref, dbr_ref, dbi_ref, dcr_ref, dci_ref, gbr_s, gbi_s):
        same, _, _, rep_t = _tile_masks()
        rep_t = rep_t.astype(F32)
        for j in range(N_GT):
            rows = slice(j * CH_T, (j + 1) * CH_T)
            for src, dsts in ((gbcat_ref, (gbr_s, gbi_s)), (gccat_ref, (dcr_ref, dci_ref))):
                for k, dst in enumerate(dsts):
                    blk = jnp.where(same, src[j, :, k * ST_T:(k + 1) * ST_T], 0.0)
                    dst[rows, :] = jnp.dot(blk, rep_t, precision=lax.Precision.HIGHEST, preferred_element_type=F32)
        dci_ref[...] = -dci_ref[...]
        lr, li = lr_ref[...], li_ref[...]
        step, ar, ai, den, cr, ci = _discretise(lr, li, ls_ref[...])
        crb, cib = _per_channel(cr), _per_channel(ci)
        br, bi = br_ref[...], bi_ref[...]
        gbr, gbi = gbr_s[...], gbi_s[...]
        dbr_ref[...] = crb * gbr + cib * gbi
        dbi_ref[...] = crb * gbi - cib * gbr
        over_channels = lambda t: jnp.sum(t.reshape(SSM_G, SSM_P, SSM_N), axis=1)
        gcr = over_channels(br * gbr + bi * gbi)
        gci = over_channels(br * gbi - bi * gbr)
        ilr, ili = lr / den, -li / den
        gar = gar_ref[...] + (ilr * gcr + ili * gci)
        gai = gai_ref[...] + (ilr * gci - ili * gcr)
        qr, qi = cr * ilr - ci * ili, cr * ili + ci * ilr
        glr = -(qr * gcr + qi * gci)
        gli = -(qr * gci - qi * gcr)
        gwr = ar * gar + ai * gai
        gwi = ar * gai - ai * gar
        dlr_ref[...] = glr + step * gwr
        dli_ref[...] = gli + step * gwi
        dls_ref[...] = jnp.sum(lr * gwr + li * gwi, axis=-1, keepdims=True) * step

    lam = jax.ShapeDtypeStruct((SSM_G, SSM_N), F32)
    mat = jax.ShapeDtypeStruct((SSM_G * SSM_P, SSM_N), F32)
    vm = pl.BlockSpec(memory_space=pltpu.VMEM)
    return pl.pallas_call(
        body, name="ssm_param_grads", out_shape=(lam, lam, jax.ShapeDtypeStruct((SSM_G, 1), F32), mat, mat, mat, mat),
        in_specs=[vm] * 9, out_specs=(vm,) * 7,
        scratch_shapes=[pltpu.VMEM((SSM_G * SSM_P, SSM_N), F32), pltpu.VMEM((SSM_G * SSM_P, SSM_N), F32)],
    )(lam_re, lam_im, log_step, b_re, b_im, da_re, da_im, d_bcat, d_ccat_t)


ROWS4 = Q_PER_KV * ATT_BLOCK
ATT_FWD_STACK = 1


def _att_dist_mask(first_block):
    qi = lax.broadcasted_iota(jnp.int32, (ROWS4, 2 * ATT_BLOCK), 0) & (ATT_BLOCK - 1)
    si = lax.broadcasted_iota(jnp.int32, (ROWS4, 2 * ATT_BLOCK), 1)
    dist = qi + ATT_BLOCK - si
    valid = (dist >= 0) & (dist < ATT_BLOCK) & ((si >= ATT_BLOCK) | jnp.logical_not(first_block))
    return dist.astype(F32), valid


def _stack_heads(x, kv):
    return jnp.concatenate([x[:, (kv * Q_PER_KV + g) * HEAD_DIM:(kv * Q_PER_KV + g + 1) * HEAD_DIM]
                            for g in range(Q_PER_KV)], axis=0)


def _stack_cols(x, kv):
    return jnp.concatenate([x[:, kv * Q_PER_KV + g:kv * Q_PER_KV + g + 1] for g in range(Q_PER_KV)], axis=0)


def _per_head_col(vals):
    return jnp.concatenate([jnp.full((ATT_BLOCK, 1), v, F32) for v in vals], axis=0)


def _attn_forward(q, k, v, sinks, bl, nb):
    t = q.shape[0]

    def body(sink_ref, q_ref, kp_ref, kc_ref, vp_ref, vc_ref, o_ref, lse_ref):
        i = pl.program_id(1)
        dist4, valid4 = _att_dist_mask(i == 0)
        rows2 = ATT_FWD_STACK * ATT_BLOCK
        dist, valid = dist4[0:rows2, :], valid4[0:rows2, :]
        kk = jnp.concatenate([kp_ref[...], kc_ref[...]], axis=0)
        vv = jnp.concatenate([vp_ref[...], vc_ref[...]], axis=0)
        qv = q_ref[...]
        col = lambda vals: jnp.concatenate([jnp.full((ATT_BLOCK, 1), v, F32) for v in vals], axis=0)
        stacks = [range(h0, h0 + ATT_FWD_STACK) for h0 in range(0, N_HEADS, ATT_FWD_STACK)]
        kv_cols = lambda heads: slice(heads[0] // Q_PER_KV * HEAD_DIM, (heads[0] // Q_PER_KV + 1) * HEAD_DIM)
        scores = [_mm_nt(jnp.concatenate([qv[:, h * HEAD_DIM:(h + 1) * HEAD_DIM] for h in heads], axis=0),
                         kk[:, kv_cols(heads)]) for heads in stacks]
        softmaxes = []
        for heads, qk in zip(stacks, scores):
            slope = col([2.0 ** (-(h + 1)) for h in heads])
            sink = col([sink_ref[h] for h in heads])
            s = jnp.where(valid, qk * ATT_SCALE - slope * dist, NEG_INF)
            m = jnp.maximum(jnp.max(s, axis=-1, keepdims=True), sink)
            e = jnp.exp(s - m)
            den = jnp.sum(e, axis=-1, keepdims=True) + jnp.exp(sink - m)
            softmaxes.append((e.astype(BF16), 1.0 / den, m + jnp.log(den)))
        for heads, (e, inv_den, lse) in zip(stacks, softmaxes):
            o = _mm(e, vv[:, kv_cols(heads)]) * inv_den
            for g, h in enumerate(heads):
                rows = slice(g * ATT_BLOCK, (g + 1) * ATT_BLOCK)
                o_ref[:, h * HEAD_DIM:(h + 1) * HEAD_DIM] = o[rows, :]
                lse_ref[:, h:h + 1] = lse[rows, :]

    cur = lambda w: pl.BlockSpec((ATT_BLOCK, w), lambda b, i: (b * nb + i, 0))
    prev = lambda w: pl.BlockSpec((ATT_BLOCK, w), lambda b, i: (b * nb + jnp.maximum(i - 1, 0), 0))
    return pl.pallas_call(
        body, name="attn_forward", grid=(bl, nb),
        in_specs=[pl.BlockSpec(memory_space=pltpu.SMEM), cur(512), prev(128), cur(128), prev(128), cur(128)],
        out_specs=(cur(512), cur(N_HEADS)),
        out_shape=(jax.ShapeDtypeStruct((t, D_ATTN), F32), jax.ShapeDtypeStruct((t, N_HEADS), F32)),
        compiler_params=_tc_params(("arbitrary", "arbitrary")),
    )(sinks, q, k, k, v, v)


def _attn_backward(q, k, v, o, do, lse, sinks, bl, nb):
    t = q.shape[0]

    def body(sink_ref, qc_ref, kp_ref, kc_ref, vp_ref, vc_ref, oc_ref, doc_ref, lc_ref,
             dq_ref, dk_ref, dv_ref, ds_ref, dk_carry, dv_carry):
        b, i = pl.program_id(0), pl.program_id(1)
        live = i < nb

        @pl.when(i == 0)
        def _():
            dk_carry[...] = jnp.zeros((ATT_BLOCK, KV_HEADS * HEAD_DIM), F32)
            dv_carry[...] = jnp.zeros((ATT_BLOCK, KV_HEADS * HEAD_DIM), F32)

        dist, valid = _att_dist_mask(i == 0)
        valid = valid & live
        kk = jnp.concatenate([kp_ref[...], kc_ref[...]], axis=0)
        vv = jnp.concatenate([vp_ref[...], vc_ref[...]], axis=0)
        qc, oc, doc, lc = qc_ref[...], oc_ref[...], doc_ref[...], lc_ref[...]
        dsink_cols, dq_parts, dk_t, dv_t = [], [], [], []
        for kv in range(KV_HEADS):
            heads = range(kv * Q_PER_KV, (kv + 1) * Q_PER_KV)
            cols = slice(kv * HEAD_DIM, (kv + 1) * HEAD_DIM)
            kh, vh = kk[:, cols], vv[:, cols]
            slope = _per_head_col([2.0 ** (-(h + 1)) for h in heads])
            sink = _per_head_col([sink_ref[h] for h in heads])
            q4, do4 = _stack_heads(qc, kv), _stack_heads(doc, kv)
            delta = jnp.sum(do4 * _stack_heads(oc, kv), axis=-1, keepdims=True)
            lse4 = _stack_cols(lc, kv)
            s = _mm_nt(q4, kh) * ATT_SCALE - slope * dist
            p = jnp.where(valid, jnp.exp(s - lse4), 0.0)
            dsc = p * (_mm_nt(do4, vh) - delta)
            dq4 = _mm(dsc, kh) * ATT_SCALE
            dk_t.append(_mm_tn(q4, dsc) * ATT_SCALE)
            dv_t.append(_mm_tn(do4, p))
            dsink4 = jnp.where(live, jnp.exp(sink - lse4) * delta, 0.0)
            for g, h in enumerate(heads):
                rows = slice(g * ATT_BLOCK, (g + 1) * ATT_BLOCK)
                dq_parts.append((h, dq4[rows, :]))
                dsink_cols.append(-jnp.sum(dsink4[rows, :], axis=0, keepdims=True))
        dsink = jnp.concatenate(dsink_cols, axis=1)
        for out_ref, carry, parts in ((dk_ref, dk_carry, dk_t), (dv_ref, dv_carry, dv_t)):
            both = jnp.concatenate(parts, axis=0)
            out_ref[...] = (carry[...] + both[:, 0:ATT_BLOCK]).T
            carry[...] = both[:, ATT_BLOCK:]

        @pl.when(live)
        def _():
            for h, part in dq_parts:
                dq_ref[:, h * HEAD_DIM:(h + 1) * HEAD_DIM] = part

        @pl.when((b == 0) & (i == 0))
        def _():
            ds_ref[...] = dsink

        @pl.when((b != 0) | (i != 0))
        def _():
            ds_ref[...] += dsink

    cur_i = lambda i: jnp.minimum(i, nb - 1)
    cur = lambda w: pl.BlockSpec((ATT_BLOCK, w), lambda b, i: (b * nb + cur_i(i), 0))
    prev = lambda w: pl.BlockSpec((ATT_BLOCK, w), lambda b, i: (b * nb + jnp.maximum(cur_i(i) - 1, 0), 0))
    behind = lambda w: pl.BlockSpec((ATT_BLOCK, w), lambda b, i: (b * nb + jnp.maximum(i - 1, 0), 0))
    return pl.pallas_call(
        body, name="attn_backward", grid=(bl, nb + 1),
        in_specs=[pl.BlockSpec(memory_space=pltpu.SMEM), cur(512), prev(128), cur(128), prev(128), cur(128),
                  cur(512), cur(512), cur(N_HEADS)],
        out_specs=(cur(512), behind(128), behind(128), pl.BlockSpec((1, N_HEADS), lambda b, i: (0, 0))),
        out_shape=(jax.ShapeDtypeStruct((t, D_ATTN), F32), jax.ShapeDtypeStruct((t, 128), F32),
                   jax.ShapeDtypeStruct((t, 128), F32), jax.ShapeDtypeStruct((1, N_HEADS), F32)),
        scratch_shapes=[pltpu.VMEM((ATT_BLOCK, KV_HEADS * HEAD_DIM), F32), pltpu.VMEM((ATT_BLOCK, KV_HEADS * HEAD_DIM), F32)],
        compiler_params=_tc_params(("arbitrary", "arbitrary")),
    )(sinks, q, k, k, v, v, o, do, lse)


def _mix_forward_backward(x2, y2, z_ssm, attn, z_attn, p2, target2, w_glu, b_glu, w_out, g_post, w_gate, b_gate,
                          w_proj, tm):
    t = x2.shape[0]

    def body(x_ref, y_ref, zs_ref, at_ref, za_ref, p_ref, tg_ref,
             wglu_ref, bglu_ref, wout_ref, gpost_ref, wgate_ref, bgate_ref, wproj_ref,
             loss_ref, dh1_ref, dy_ref, dzs_ref, dat_ref, dza_ref,
             dwglu_own_ref, dbglu_ref, dwout_own_ref, dgpost_ref, dwgate_own_ref, dbgate_ref, dwproj_ref,
             dwout16_ref, dwgate16_ref, dwproj16_ref, dwglu16_ref, dwglu_ref, dwout_ref, dwgate_ref):
        i = pl.program_id(0)
        gpost = gpost_ref[...]

        @pl.when(i == 0)
        def _():
            for ref in (dwglu_ref, dbglu_ref, dwout_ref, dgpost_ref, dwgate_ref, dbgate_ref, dwproj_ref, loss_ref):
                ref[...] = jnp.zeros(ref.shape, F32)

        def chain(rows):
            y = y_ref[rows, :]
            u3 = GELU_C * (y + GELU_K * y * y * y)
            th = jnp.tanh(u3)
            gl = 0.5 * y * (1.0 + th)
            a = _mm(gl, wglu_ref[...]) + bglu_ref[...]
            sa = _sigmoid(a)
            glu = gl * sa
            zs = zs_ref[rows, :]
            sgs = _sigmoid(zs)
            ssm_out = glu * (zs * sgs)
            za = za_ref[rows, :]
            sga = _sigmoid(za)
            at = at_ref[rows, :]
            attn_out = at * (za * sga)
            cat = jnp.concatenate([ssm_out, attn_out], axis=-1).astype(BF16)
            mixed = _mm(cat, wout_ref[...])
            r2 = lax.rsqrt(jnp.mean(mixed * mixed, axis=-1, keepdims=True) + EPS)
            nhat = mixed * r2
            h1 = x_ref[rows, :] + nhat * gpost
            gate = _sigmoid(_mm(h1, wgate_ref[...]) + bgate_ref[...])
            pv = p_ref[rows, :]
            pp = _mm(pv, wproj_ref[...])
            h2 = h1 + gate * pp
            err = h2 - tg_ref[rows, :]
            loss_part = jnp.sum(jnp.sum(err * err, axis=-1, keepdims=True), axis=0, keepdims=True) * (0.5 / D_MODEL)
            dh2 = err * (1.0 / D_MODEL)
            dgp = dh2 * pp * gate * (1.0 - gate)
            dpp = dh2 * gate
            dh1 = dh2 + _mm_nt(dgp, wgate_ref[...])
            dwproj_ref[...] += _mm_tn(pv, dpp)
            dwgate_ref[...] += _mm_tn(h1, dgp)
            dh1_ref[rows, :] = dh1
            dnhat = dh1 * gpost
            dmixed = r2 * (dnhat - nhat * jnp.mean(dnhat * nhat, axis=-1, keepdims=True))
            dcat = _mm_nt(dmixed, wout_ref[...])
            dwout_ref[...] += _mm_tn(cat, dmixed)
            dso, dao = dcat[:, 0:D_SSM], dcat[:, D_SSM:]
            dat_ref[rows, :] = dao * (za * sga)
            dza_ref[rows, :] = (dao * at * (sga * (1.0 + za * (1.0 - sga)))).astype(BF16)
            dzs_ref[rows, :] = (dso * glu * (sgs * (1.0 + zs * (1.0 - sgs)))).astype(BF16)
            dglu = dso * (zs * sgs)
            da = dglu * gl * sa * (1.0 - sa)
            dgl = dglu * sa + _mm_nt(da, wglu_ref[...])
            dwglu_ref[...] += _mm_tn(gl, da)
            dgelu = 0.5 * (1.0 + th) + 0.5 * y * (1.0 - th * th) * (GELU_C * (1.0 + 3.0 * GELU_K * y * y))
            dy_ref[rows, :] = dgl * dgelu
            dbglu_ref[...] += jnp.sum(da, axis=0, keepdims=True)
            dgpost_ref[...] += jnp.sum(dh1 * nhat, axis=0, keepdims=True)
            dbgate_ref[...] += jnp.sum(dgp, axis=0, keepdims=True)
            loss_ref[...] += loss_part

        chain(slice(None))

        @pl.when(i == t // tm - 1)
        def _():
            for ref16, ref in ((dwout16_ref, dwout_ref), (dwgate16_ref, dwgate_ref), (dwproj16_ref, dwproj_ref),
                               (dwglu16_ref, dwglu_ref)):
                def to16(r, ref16=ref16, ref=ref):
                    ref16[r, :] = ref[r, :].astype(BF16)

                _row_chunks(ref.shape[0], to16)
            for ref, own_ref in ((dwglu_ref, dwglu_own_ref), (dwout_ref, dwout_own_ref), (dwgate_ref, dwgate_own_ref)):
                _copy_owned_rows(ref, own_ref)

    row = lambda w: pl.BlockSpec((tm, w), lambda i: (i, 0))
    acc = lambda *shape, dt=F32: (_const_spec(shape), jax.ShapeDtypeStruct(shape, dt))
    accs = [acc(1, D_SSM // N_DEV, D_SSM), acc(1, D_SSM), acc(1, D_MODEL // N_DEV, D_MODEL), acc(1, D_MODEL),
            acc(1, D_MODEL // N_DEV, D_MODEL), acc(1, D_MODEL), acc(D_PLE, D_MODEL),
            acc(D_MODEL, D_MODEL, dt=BF16), acc(D_MODEL, D_MODEL, dt=BF16), acc(D_PLE, D_MODEL, dt=BF16),
            acc(D_SSM, D_SSM, dt=BF16)]
    return pl.pallas_call(
        body, name="mix_forward_backward", grid=(t // tm,),
        in_specs=[row(D_MODEL), row(512), row(512), row(512), row(512), row(D_PLE), row(D_MODEL),
                  _const_spec((D_SSM, D_SSM)), _const_spec((1, D_SSM)), _const_spec((D_MODEL, D_MODEL)),
                  _const_spec((1, D_MODEL)), _const_spec((D_MODEL, D_MODEL)), _const_spec((1, D_MODEL)),
                  _const_spec((D_PLE, D_MODEL))],
        out_specs=(_const_spec((1, 1)), row(D_MODEL), row(512), row(512), row(512), row(512))
        + tuple(a[0] for a in accs),
        out_shape=(jax.ShapeDtypeStruct((1, 1), F32), jax.ShapeDtypeStruct((t, D_MODEL), F32),
                   jax.ShapeDtypeStruct((t, 512), F32),
                   jax.ShapeDtypeStruct((t, 512), BF16), jax.ShapeDtypeStruct((t, 512), F32),
                   jax.ShapeDtypeStruct((t, 512), BF16)) + tuple(a[1] for a in accs),
        scratch_shapes=[pltpu.VMEM((D_SSM, D_SSM), F32), pltpu.VMEM((D_MODEL, D_MODEL), F32),
                        pltpu.VMEM((D_MODEL, D_MODEL), F32)],
        compiler_params=_tc_params(("arbitrary",)),
    )(x2, y2, z_ssm, attn, z_attn, p2, target2, w_glu, b_glu, w_out, g_post, w_gate, b_gate, w_proj)


def _in_backward(x2, dh1, du, dz_ssm, dq, dk, dv, dz_attn, g_pre, w_in, tm):
    t = x2.shape[0]

    def body(x_ref, dh1_ref, du_ref, dzs_ref, dq_ref, dk_ref, dv_ref, dza_ref, g_ref, w_ref,
             gx_ref, dw_own_ref, dg_ref, dw16_ref, dw_ref):
        i = pl.program_id(0)

        @pl.when(i == 0)
        def _():
            dw_ref[...] = jnp.zeros((D_IN, D_MODEL), F32)
            dg_ref[...] = jnp.zeros((1, D_MODEL), F32)

        xv = x_ref[...]
        r = lax.rsqrt(jnp.mean(xv * xv, axis=-1, keepdims=True) + EPS)
        xhat = xv * r
        g = g_ref[...]
        hn = (xhat * g).astype(BF16)
        dproj = jnp.concatenate([du_ref[...].astype(BF16), dzs_ref[...].astype(BF16), dq_ref[...].astype(BF16),
                                 dk_ref[...].astype(BF16), dv_ref[...].astype(BF16), dza_ref[...].astype(BF16)],
                                axis=-1)
        dhn = _mm(dproj, w_ref[...])
        dxhat = dhn * g
        gx_ref[...] = dh1_ref[...] + r * (dxhat - xhat * jnp.mean(dxhat * xhat, axis=-1, keepdims=True))
        dw_ref[...] += _mm_tn(dproj, hn)
        dg_ref[...] += jnp.sum(dhn * xhat, axis=0, keepdims=True)

        @pl.when(i == t // tm - 1)
        def _():
            def to16(r):
                dw16_ref[r, :] = dw_ref[r, :].astype(BF16)

            _row_chunks(D_IN, to16)
            _copy_owned_rows(dw_ref, dw_own_ref)

    row = lambda w: pl.BlockSpec((tm, w), lambda i: (i, 0))
    own = (1, D_IN // N_DEV, D_MODEL)
    return pl.pallas_call(
        body, name="in_backward", grid=(t // tm,),
        in_specs=[row(D_MODEL), row(D_MODEL), row(512), row(512), row(512), row(128), row(128), row(512),
                  _const_spec((1, D_MODEL)), _const_spec((D_IN, D_MODEL))],
        out_specs=(row(D_MODEL), _const_spec(own), _const_spec((1, D_MODEL)), _const_spec((D_IN, D_MODEL))),
        out_shape=(jax.ShapeDtypeStruct((t, D_MODEL), F32), jax.ShapeDtypeStruct(own, F32),
                   jax.ShapeDtypeStruct((1, D_MODEL), F32), jax.ShapeDtypeStruct((D_IN, D_MODEL), BF16)),
        scratch_shapes=[pltpu.VMEM((D_IN, D_MODEL), F32)],
        compiler_params=_tc_params(("arbitrary",)),
    )(x2, dh1, du, dz_ssm, dq, dk, dv, dz_attn, g_pre, w_in)


def _local_step(x, p, target, pre_norm_g, w_in, prep, ssm_lam_re, ssm_lam_im, ssm_log_step, ssm_b_re, ssm_b_im, ssm_d,
                ssm_b_glu, attn_sinks, post_norm_g, pl_b_gate, late):
    bl, seq, _ = x.shape
    seg = seq // N_SEG
    nb = seq // ATT_BLOCK
    t = bl * seq
    x2 = x.reshape(t, D_MODEL)
    p2 = p.reshape(t, D_PLE)
    tg2 = target.reshape(t, D_MODEL)

    lam_re, lam_im = ssm_lam_re, ssm_lam_im
    log_step = ssm_log_step.reshape(SSM_G, 1)
    a_re_row, a_im_row, pw_re, pw_im, bcat, bcat_t, ccat, ccat_t = prep
    d_row = ssm_d.reshape(1, D_SSM)

    segments = lambda a: a.reshape(bl, N_SEG, seg, D_SSM)
    u, z_ssm, q, k, v, z_attn = _in_proj(x2, pre_norm_g.reshape(1, D_MODEL), w_in, min(TOKEN_TILE_WIDE, t))
    (y, states, carries), gathered = _ssm_forward(
        segments(u), bcat, ccat, a_re_row, a_im_row, pw_re, pw_im, d_row, late, seg)
    w_out, w_gate, w_proj, w_glu = (_gathered_to_full(n, g) for n, g in zip(LATE_NAMES, gathered))
    sinks = attn_sinks.reshape(N_HEADS)
    attn, lse = _attn_forward(q, k, v, sinks, bl, nb)
    (loss, dh1, dy, dz_ssm, dattn, dz_attn, d_w_glu, d_b_glu, d_w_out, d_g_post, d_w_gate, d_b_gate,
     d_w_proj, *late16) = _mix_forward_backward(
        x2, y.reshape(t, D_SSM), z_ssm, attn, z_attn, p2, tg2, w_glu,
        ssm_b_glu.reshape(1, D_SSM), w_out, post_norm_g.reshape(1, D_MODEL), w_gate, pl_b_gate.reshape(1, D_MODEL),
        w_proj, min(TOKEN_TILE, t))
    owned = lambda ds: [_full_to_owned(n, d) for n, d in zip(LATE_NAMES, ds)]
    dq, dk, dv, d_sinks = _attn_backward(q, k, v, attn, dattn, lse, sinks, bl, nb)
    (du, d_bcat, d_ccat_t, da_re, da_im, d_d), late_grads = _ssm_backward(
        segments(u), segments(dy), states, carries, bcat_t, ccat_t, a_re_row, a_im_row, pw_re, pw_im,
        d_row, owned(late16), [d_w_out, d_w_gate, _full_to_owned("pl_w_proj", d_w_proj), d_w_glu], seg)
    grad_x, d_w_in, d_g_pre, d_w_in16 = _in_backward(
        x2, dh1, du.reshape(t, D_SSM), dz_ssm, dq, dk, dv, dz_attn, pre_norm_g.reshape(1, D_MODEL), w_in,
        min(TOKEN_TILE_WIDE, t))
    d_lam_re, d_lam_im, d_ls, d_b_re, d_b_im, d_c_re, d_c_im = _ssm_param_grads(
        lam_re, lam_im, log_step, ssm_b_re, ssm_b_im, da_re.reshape(SSM_G, SSM_N), da_im.reshape(SSM_G, SSM_N),
        d_bcat, d_ccat_t)
    grads = {
        "pre_norm_g": d_g_pre, "w_in": d_w_in, "w_in16": d_w_in16, "ssm_lam_re": d_lam_re, "ssm_lam_im": d_lam_im,
        "ssm_log_step": d_ls, "ssm_b_re": d_b_re, "ssm_b_im": d_b_im, "ssm_c_re": d_c_re, "ssm_c_im": d_c_im,
        "ssm_d": d_d, "ssm_b_glu": d_b_glu, "attn_sinks": d_sinks, "post_norm_g": d_g_post, "pl_b_gate": d_b_gate,
    }
    return loss, grad_x.reshape(bl, seq, D_MODEL), grads, late_grads


LATE_NAMES = ("w_out", "pl_w_gate", "pl_w_proj", "ssm_w_glu")
BIG_NAMES = ("w_in",) + LATE_NAMES
COL_SHARDED = {"w_in": D_IN // N_DEV, "pl_w_proj": D_MODEL // N_DEV}
WEIGHT_NAMES = ("pre_norm_g", "w_in", "ssm_lam_re", "ssm_lam_im", "ssm_log_step", "ssm_b_re", "ssm_b_im", "ssm_c_re",
                "ssm_c_im", "ssm_d", "ssm_w_glu", "ssm_b_glu", "attn_sinks", "w_out", "post_norm_g", "pl_w_proj",
                "pl_w_gate", "pl_b_gate")


TRANSPOSED = {"w_in": (0, 1), "ssm_b_re": (1, 2), "ssm_b_im": (1, 2)}


def _kernel_form(name, a):
    a = a[0]
    if name in TRANSPOSED:
        a = jnp.swapaxes(a, *TRANSPOSED[name])
    if name in ("ssm_b_re", "ssm_b_im", "ssm_c_re", "ssm_c_im"):
        a = a.reshape(SSM_G * SSM_P, SSM_N)
    return a


def _given_form(name, a, shape):
    if name in TRANSPOSED:
        i, j = TRANSPOSED[name]
        swapped = list(shape[1:])
        swapped[i], swapped[j] = swapped[j], swapped[i]
        return jnp.swapaxes(a.reshape(swapped), i, j).reshape(shape)
    return a.reshape(shape)


def _gathered_to_full(name, g):
    _, rows, cols = g.shape
    if name in COL_SHARDED:
        return jnp.swapaxes(g, 0, 1).reshape(rows, N_DEV * cols)
    return g.reshape(N_DEV * rows, cols)


def _full_to_owned(name, full):
    if name in COL_SHARDED:
        return jnp.swapaxes(full.reshape(full.shape[0], N_DEV, COL_SHARDED[name]), 0, 1)
    return full.reshape(N_DEV, full.shape[0] // N_DEV, full.shape[1])


def kernel(x, p, pre_norm_g, w_in, ssm_lam_re, ssm_lam_im, ssm_log_step, ssm_b_re, ssm_b_im, ssm_c_re, ssm_c_im, ssm_d, ssm_w_glu, ssm_b_glu, attn_sinks, w_out, post_norm_g, pl_w_proj, pl_w_gate, pl_b_gate, loss_target, m_pre_norm_g, m_w_in, m_ssm_lam_re, m_ssm_lam_im, m_ssm_log_step, m_ssm_b_re, m_ssm_b_im, m_ssm_c_re, m_ssm_c_im, m_ssm_d, m_ssm_w_glu, m_ssm_b_glu, m_attn_sinks, m_w_out, m_post_norm_g, m_pl_w_proj, m_pl_w_gate, m_pl_b_gate, v_pre_norm_g, v_w_in, v_ssm_lam_re, v_ssm_lam_im, v_ssm_log_step, v_ssm_b_re, v_ssm_b_im, v_ssm_c_re, v_ssm_c_im, v_ssm_d, v_ssm_w_glu, v_ssm_b_glu, v_attn_sinks, v_w_out, v_post_norm_g, v_pl_w_proj, v_pl_w_gate, v_pl_b_gate):
    w = dict(pre_norm_g=pre_norm_g, w_in=w_in, ssm_lam_re=ssm_lam_re, ssm_lam_im=ssm_lam_im, ssm_log_step=ssm_log_step,
             ssm_b_re=ssm_b_re, ssm_b_im=ssm_b_im, ssm_c_re=ssm_c_re, ssm_c_im=ssm_c_im, ssm_d=ssm_d, ssm_w_glu=ssm_w_glu,
             ssm_b_glu=ssm_b_glu, attn_sinks=attn_sinks, w_out=w_out, post_norm_g=post_norm_g, pl_w_proj=pl_w_proj,
             pl_w_gate=pl_w_gate, pl_b_gate=pl_b_gate)
    m = dict(pre_norm_g=m_pre_norm_g, w_in=m_w_in, ssm_lam_re=m_ssm_lam_re, ssm_lam_im=m_ssm_lam_im,
             ssm_log_step=m_ssm_log_step, ssm_b_re=m_ssm_b_re, ssm_b_im=m_ssm_b_im, ssm_c_re=m_ssm_c_re,
             ssm_c_im=m_ssm_c_im, ssm_d=m_ssm_d, ssm_w_glu=m_ssm_w_glu, ssm_b_glu=m_ssm_b_glu, attn_sinks=m_attn_sinks,
             w_out=m_w_out, post_norm_g=m_post_norm_g, pl_w_proj=m_pl_w_proj, pl_w_gate=m_pl_w_gate,
             pl_b_gate=m_pl_b_gate)
    v = dict(pre_norm_g=v_pre_norm_g, w_in=v_w_in, ssm_lam_re=v_ssm_lam_re, ssm_lam_im=v_ssm_lam_im,
             ssm_log_step=v_ssm_log_step, ssm_b_re=v_ssm_b_re, ssm_b_im=v_ssm_b_im, ssm_c_re=v_ssm_c_re,
             ssm_c_im=v_ssm_c_im, ssm_d=v_ssm_d, ssm_w_glu=v_ssm_w_glu, ssm_b_glu=v_ssm_b_glu, attn_sinks=v_attn_sinks,
             w_out=v_w_out, post_norm_g=v_post_norm_g, pl_w_proj=v_pl_w_proj, pl_w_gate=v_pl_w_gate,
             pl_b_gate=v_pl_b_gate)
    kf = lambda d: {n: _kernel_form(n, a) for n, a in d.items()}
    wk, mk, vk = kf(w), kf(m), kf(v)

    (gathered,), prep = _allgather_weights([wk["w_in"]], *_ssm_prep(
        wk["ssm_lam_re"], wk["ssm_lam_im"], wk["ssm_log_step"].reshape(SSM_G, 1), wk["ssm_b_re"], wk["ssm_b_im"],
        wk["ssm_c_re"], wk["ssm_c_im"], x.shape[1] // N_SEG))
    loss, grad_x, grads, g_late = _local_step(
        x, p[0], loss_target, wk["pre_norm_g"], gathered.reshape(D_IN, D_MODEL), prep, wk["ssm_lam_re"],
        wk["ssm_lam_im"], wk["ssm_log_step"], wk["ssm_b_re"], wk["ssm_b_im"], wk["ssm_d"],
        wk["ssm_b_glu"], wk["attn_sinks"], wk["post_norm_g"], wk["pl_b_gate"], [wk[n] for n in LATE_NAMES])

    owned = lambda g: g.reshape(N_DEV, D_IN // N_DEV, D_MODEL)
    tiny_form = lambda d: [d[n].reshape(rows, cols) for n, rows, cols in TINY]
    med_form = lambda d: [d[n].reshape(N_DEV, rows // N_DEV, cols) for n, rows, cols in MEDIUM]
    g_big, loss, g_tiny, g_med = _reduce_final(
        [owned(grads["w_in16"])], [grads["w_in"]], loss, tiny_form(grads), med_form(grads))
    names = BIG_NAMES + tuple(n for n, _, _ in TINY + MEDIUM)
    form = lambda d: [d[n] for n in BIG_NAMES] + tiny_form(d) + med_form(d)
    updated = _adamw_update(g_big + g_late + g_tiny + g_med, form(wk), form(mk), form(vk), len(BIG_NAMES))
    vals = dict(zip(names, updated))
    results = [[_given_form(n, vals[n][kind], w[n].shape) for n in WEIGHT_NAMES] for kind in range(4)]
    return (loss.reshape(()), grad_x, *results[0], *results[1], *results[2], *results[3])
```

```python
import functools
import math

import jax
import jax.numpy as jnp
from jax import lax
from jax.experimental import pallas as pl
from jax.experimental.pallas import tpu as pltpu

F32 = jnp.float32
BF16 = jnp.bfloat16

D_MODEL = 1024
D_SSM = 512
D_ATTN = 512
SSM_P = 16
SSM_G = 32
SSM_N = 64
N_HEADS = 8
KV_HEADS = 2
Q_PER_KV = 4
HEAD_DIM = 64
ATT_BLOCK = 128
D_PLE = 256
D_IN = 2304
EPS = 1e-6
N_DEV = 8
N_SEG = 8
G_TILE = 8
N_GT = SSM_G // G_TILE
CH_T = G_TILE * SSM_P
ST_T = G_TILE * SSM_N
N_STATE = SSM_G * SSM_N
SCAN_UNROLL = 4
TOKEN_TILE = 256
TOKEN_TILE_WIDE = 512
LANES = 128
VMEM_LIMIT = 60 * 1024 * 1024

ADAM_LR = 0.001
ADAM_B1 = 0.9
ADAM_B2 = 0.999
ADAM_EPS = 1e-08
ADAM_WD = 0.01
ADAM_STEP = 10

GELU_C = math.sqrt(2.0 / math.pi)
GELU_K = 0.044715
ATT_SCALE = 1.0 / math.sqrt(HEAD_DIM)
NEG_INF = float("-inf")


def _mm(a, b):
    return jnp.dot(a.astype(BF16), b.astype(BF16), preferred_element_type=F32)


def _mm_nt(a, b):
    return lax.dot_general(a.astype(BF16), b.astype(BF16), (((1,), (1,)), ((), ())), preferred_element_type=F32)


def _mm_tn(a, b):
    return lax.dot_general(a.astype(BF16), b.astype(BF16), (((0,), (0,)), ((), ())), preferred_element_type=F32)


def _sigmoid(x):
    return 1.0 / (1.0 + jnp.exp(-x))


def _tc_params(sem):
    return pltpu.CompilerParams(dimension_semantics=sem, vmem_limit_bytes=VMEM_LIMIT)


def _const_spec(shape):
    nd = len(shape)
    return pl.BlockSpec(shape, lambda *_: (0,) * nd)


def _mesh_pos():
    return lax.axis_index("x"), lax.axis_index("y"), lax.axis_index("c")


ROW_CHUNKS = (64, 32, 16)


def _row_chunk(nrows):
    return next((c for c in ROW_CHUNKS if nrows % c == 0), None)


def _row_chunks(nrows, fn, chunk=None, init=None):
    chunk = chunk or _row_chunk(nrows)

    def step(i, carry):
        rows = pl.ds(pl.multiple_of(i * chunk, chunk), chunk)
        if init is None:
            fn(rows)
            return carry
        return fn(rows, carry)

    return lax.fori_loop(0, nrows // chunk, step, 0 if init is None else init)


def _slot(px, py, pc):
    return 4 * px + 2 * py + pc


def _copy_owned_rows(acc_ref, own_ref):
    rows = own_ref.shape[1]
    own_ref[0] = acc_ref[pl.ds(pl.multiple_of(_slot(*_mesh_pos()) * rows, 8), rows), :]


def _allgather_weights(shards, work=None, work_inputs=(), work_out_shapes=()):
    n, n_wi, n_wo = len(shards), len(work_inputs), len(work_out_shapes)

    def body(*refs):
        srcs, w_in_refs = refs[:n], refs[n:n + n_wi]
        outs, w_out_refs = refs[n + n_wi:2 * n + n_wi], refs[2 * n + n_wi:2 * n + n_wi + n_wo]
        send_sems, recv_sems = refs[2 * n + n_wi + n_wo:]
        x, y, c = _mesh_pos()
        me, sibling = (x, y, c), (x, y, 1 - c)
        chips = [(1 - x, y), (x, 1 - y), (1 - x, 1 - y)]

        def copy(a, k, block, to):
            blk = outs[a].at[_slot(*block)]
            return pltpu.make_async_remote_copy(
                src_ref=blk, dst_ref=blk, send_sem=send_sems.at[7 * a + k], recv_sem=recv_sems.at[7 * a + k],
                device_id=to, device_id_type=pl.DeviceIdType.MESH)

        sends = []
        for a in range(n):
            mine = outs[a].at[_slot(*me)]

            def cast(r, mine=mine, src=srcs[a]):
                mine[r, :] = src[r, :].astype(BF16)

            _row_chunks(srcs[a].shape[0], cast)
            first = [copy(a, 0, me, sibling)] + [copy(a, 1 + j, me, (*chip, c)) for j, chip in enumerate(chips)]
            for cp in first:
                cp.start()
            sends += first
        if work is not None:
            work(w_in_refs, w_out_refs)
        for a in range(n):
            for j, chip in enumerate(chips):
                copy(a, 1 + j, (*chip, c), me).wait_recv()
                fwd = copy(a, 4 + j, (*chip, c), sibling)
                fwd.start()
                sends.append(fwd)
        for a in range(n):
            copy(a, 0, sibling, me).wait_recv()
            for j, chip in enumerate(chips):
                copy(a, 4 + j, (*chip, 1 - c), me).wait_recv()
        for cp in sends:
            cp.wait_send()

    vm = pl.BlockSpec(memory_space=pltpu.VMEM)
    res = pl.pallas_call(
        body, name="allgather_weights",
        out_shape=tuple(jax.ShapeDtypeStruct((N_DEV,) + s.shape, BF16) for s in shards) + tuple(work_out_shapes),
        in_specs=[vm] * (n + n_wi), out_specs=(vm,) * (n + n_wo),
        scratch_shapes=[pltpu.SemaphoreType.DMA((7 * n,)), pltpu.SemaphoreType.DMA((7 * n,))],
        compiler_params=pltpu.CompilerParams(vmem_limit_bytes=VMEM_LIMIT),
    )(*shards, *work_inputs)
    return list(res[:n]), list(res[n:])


def _adamw(w, g, m, v):
    m = ADAM_B1 * m + (1.0 - ADAM_B1) * g
    v = ADAM_B2 * v + (1.0 - ADAM_B2) * (g * g)
    m_hat = m / (1.0 - ADAM_B1 ** ADAM_STEP)
    v_hat = v / (1.0 - ADAM_B2 ** ADAM_STEP)
    delta = -ADAM_LR * (m_hat / (jnp.sqrt(v_hat) + ADAM_EPS) + ADAM_WD * w)
    return delta, m, v


def _remote(src, dst, send_sems, recv_sems, k, to):
    return pltpu.make_async_remote_copy(src_ref=src, dst_ref=dst, send_sem=send_sems.at[k], recv_sem=recv_sems.at[k],
                                        device_id=to, device_id_type=pl.DeviceIdType.MESH)


def _big_reduce_phases(g16_r, go_r, outs, send2, recv1, recv2, s_send, s_recv):
    n = len(g16_r)
    x, y, c = _mesh_pos()
    sibling = (x, y, 1 - c)
    chips = [(1 - x, y), (x, 1 - y), (1 - x, 1 - y)]
    all_chips = [(x, y)] + chips
    lvl1 = []
    for a in range(n):
        cps = [_remote(g16_r[a].at[_slot(*chip, 1 - c)], recv1[a].at[j], s_send, s_recv, 7 * a + j, sibling)
               for j, chip in enumerate(all_chips)]
        for cp in cps:
            cp.start()
        lvl1.append(cps)
    yield
    lvl2 = []
    for a in range(n):
        for cp in lvl1[a]:
            cp.wait_recv()
        og = outs[a]

        def partials(r, a=a, og=og):
            og[r, :] = go_r[a][r, :] + recv1[a][0, r, :].astype(F32)
            for j, chip in enumerate(chips):
                mine16 = g16_r[a][_slot(*chip, c), r, :].astype(F32)
                send2[a][j, r, :] = (mine16 + recv1[a][1 + j, r, :].astype(F32)).astype(BF16)

        _row_chunks(go_r[a].shape[0], partials)
        cps = [_remote(send2[a].at[j], recv2[a].at[j], s_send, s_recv, 7 * a + 4 + j, (*chip, c))
               for j, chip in enumerate(chips)]
        for cp in cps:
            cp.start()
        lvl2.append(cps)
    yield
    for a in range(n):
        for cp in lvl2[a]:
            cp.wait_recv()
        og = outs[a]

        def total(r, a=a, og=og):
            g = og[r, :]
            for j in range(3):
                g = g + recv2[a][j, r, :].astype(F32)
            og[r, :] = g

        _row_chunks(go_r[a].shape[0], total)
    yield
    for cps in lvl1 + lvl2:
        for cp in cps:
            cp.wait_send()


def _adamw_update(g, w, m, v, n_streamed):
    n = len(g)
    ns = n_streamed

    def body(*refs):
        g_r, w_r, m_r, v_r = (refs[i * n:(i + 1) * n] for i in range(4))
        outs = refs[4 * n:8 * n]
        in_buf, out_buf = refs[8 * n:8 * n + 4 * ns], refs[8 * n + 4 * ns:8 * n + 8 * ns]
        in_sems, out_sems = refs[8 * n + 8 * ns:]
        loads = [[pltpu.make_async_copy(src[a], in_buf[4 * a + k], in_sems.at[4 * a + k])
                  for k, src in enumerate((g_r, w_r, m_r, v_r))] for a in range(ns)]
        stores = [[pltpu.make_async_copy(out_buf[4 * a + k], outs[4 * a + k], out_sems.at[4 * a + k]) for k in range(4)]
                  for a in range(ns)]
        for cps in loads:
            for cp in cps:
                cp.start()
        for a in range(n):
            if a < ns:
                for cp in loads[a]:
                    cp.wait()
                gs, ws, ms, vs = in_buf[4 * a:4 * a + 4]
                og, od, om, ov = out_buf[4 * a:4 * a + 4]
            else:
                gs, ws, ms, vs = g_r[a], w_r[a], m_r[a], v_r[a]
                og, od, om, ov = outs[4 * a:4 * a + 4]

            def update(idx, gs=gs, ws=ws, ms=ms, vs=vs, og=og, od=od, om=om, ov=ov):
                gv = gs[idx]
                d, nm, nv = _adamw(ws[idx], gv, ms[idx], vs[idx])
                og[idx] = gv
                od[idx] = d
                om[idx] = nm
                ov[idx] = nv

            shape = gs.shape
            if len(shape) == 3:
                for b in range(shape[0]):
                    update(b)
            elif _row_chunk(shape[0]) is not None:
                _row_chunks(shape[0], update)
            else:
                update(Ellipsis)
            if a < ns:
                for cp in stores[a]:
                    cp.start()
        for cps in stores:
            for cp in cps:
                cp.wait()

    vm, hbm = pl.BlockSpec(memory_space=pltpu.VMEM), pl.BlockSpec(memory_space=pl.ANY)
    place = lambda: [hbm] * ns + [vm] * (n - ns)
    buf = [pltpu.VMEM(t.shape, F32) for t in g[:ns] for _ in range(4)]
    res = pl.pallas_call(
        body, name="adamw_update",
        out_shape=tuple(jax.ShapeDtypeStruct(t.shape, F32) for t in g for _ in range(4)),
        in_specs=place() * 4, out_specs=tuple(s for a in range(n) for s in [hbm if a < ns else vm] * 4),
        scratch_shapes=buf + buf + [pltpu.SemaphoreType.DMA((4 * ns,)), pltpu.SemaphoreType.DMA((4 * ns,))],
        compiler_params=pltpu.CompilerParams(vmem_limit_bytes=VMEM_LIMIT),
    )(*g, *w, *m, *v)
    return [res[4 * a:4 * a + 4] for a in range(n)]


TINY = (("pre_norm_g", 1, 1024), ("post_norm_g", 1, 1024), ("pl_b_gate", 1, 1024), ("ssm_d", 1, 512),
        ("ssm_b_glu", 1, 512), ("ssm_log_step", 1, 32), ("attn_sinks", 1, 8), ("ssm_lam_re", 32, 64),
        ("ssm_lam_im", 32, 64))
MEDIUM = (("ssm_b_re", SSM_G * SSM_P, SSM_N), ("ssm_b_im", SSM_G * SSM_P, SSM_N), ("ssm_c_re", SSM_G * SSM_P, SSM_N),
          ("ssm_c_im", SSM_G * SSM_P, SSM_N))


def _stage_rows():
    offs, r = {}, 0
    for name, rows, cols in TINY + (("loss", 1, 1),):
        if rows > 1:
            r = -(-r // 8) * 8
        offs[name] = r
        r += rows if rows > 1 else max(cols // LANES, 1)
    return offs, -(-r // 8) * 8


def _reduce_final(g16, g32, loss, g_tiny, g_med):
    nb_, nt, nm_ = len(g16), len(TINY), len(MEDIUM)
    offs, stage_rows = _stage_rows()

    def body(*refs):
        g16_r, go_r = refs[:nb_], refs[nb_:2 * nb_]
        base = 2 * nb_
        loss_r, gt, gm = refs[base], refs[base + 1:base + 1 + nt], refs[base + 1 + nt:base + 1 + nt + nm_]
        base += 1 + nt + nm_
        out_b = refs[base:base + nb_]
        base += nb_
        loss_o, out_t, out_m = refs[base], refs[base + 1:base + 1 + nt], refs[base + 1 + nt:base + 1 + nt + nm_]
        base += 1 + nt + nm_
        send2_b, recv1_b, recv2_b = (refs[base + i * nb_:base + (i + 1) * nb_] for i in range(3))
        base += 3 * nb_
        stage = refs[base]
        recv1, part, recv2 = (refs[base + 1 + i * nm_:base + 1 + (i + 1) * nm_] for i in range(3))
        bs_send, bs_recv, s_send, s_recv, own_sems = refs[base + 1 + 3 * nm_:base + 6 + 3 * nm_]
        own32 = refs[base + 6 + 3 * nm_:]
        fetch = [pltpu.make_async_copy(go_r[a].at[0], own32[a], own_sems.at[a]) for a in range(nb_)]
        for cp in fetch:
            cp.start()
        big = _big_reduce_phases(g16_r, own32, out_b, send2_b, recv1_b, recv2_b, bs_send, bs_recv)
        small = small_phases(loss_r, gt, gm, loss_o, out_t, out_m, stage, recv1, part, recv2, s_send, s_recv)
        next(big)
        next(small)
        for cp in fetch:
            cp.wait()
        next(big)
        for _ in small:
            pass
        for _ in big:
            pass

    def small_phases(loss_r, gt, gm, loss_o, out_t, out_m, stage, recv1, part, recv2, s_send, s_recv):
        x, y, c = _mesh_pos()
        me = _slot(x, y, c)
        sibling = (x, y, 1 - c)
        chips = [(1 - x, y), (x, 1 - y), (1 - x, 1 - y)]
        all_chips = [(x, y)] + chips
        peers = [sibling] + [(*chip, c) for chip in chips] + [(*chip, 1 - c) for chip in chips]
        sem = iter(range(7 + 14 * nm_))
        lvl1 = []
        for a in range(nm_):
            cps = [_remote(gm[a].at[_slot(*chip, 1 - c)], recv1[a].at[j], s_send, s_recv, next(sem), sibling)
                   for j, chip in enumerate(all_chips)]
            for cp in cps:
                cp.start()
            lvl1.append(cps)
        mine = stage.at[me]
        mine[...] = jnp.zeros((stage_rows, LANES), F32)
        for (name, rows, cols), ref in zip(TINY + (("loss", 1, 1),), gt + (loss_r,)):
            r0 = offs[name]
            if rows > 1:
                mine[r0:r0 + rows, 0:cols] = ref[...]
            elif cols >= LANES:
                for i in range(cols // LANES):
                    mine[r0 + i:r0 + i + 1, :] = ref[:, i * LANES:(i + 1) * LANES]
            else:
                mine[r0:r0 + 1, 0:cols] = ref[...]
        tiny_cps = [_remote(mine, mine, s_send, s_recv, next(sem), peer) for peer in peers]
        for cp in tiny_cps:
            cp.start()
        yield
        lvl2 = []
        for a in range(nm_):
            for cp in lvl1[a]:
                cp.wait_recv()
            for j, chip in enumerate(all_chips):
                part[a][j] = gm[a][_slot(*chip, c)] + recv1[a][j]
            cps = [_remote(part[a].at[1 + j], recv2[a].at[j], s_send, s_recv, next(sem), (*chip, c))
                   for j, chip in enumerate(chips)]
            for cp in cps:
                cp.start()
            lvl2.append(cps)
        yield
        lvl3 = []
        for a in range(nm_):
            for cp in lvl2[a]:
                cp.wait_recv()
            blk = out_m[a].at[me]
            blk[...] = ((part[a][0] + recv2[a][0]) + recv2[a][1]) + recv2[a][2]
            cps = [_remote(blk, blk, s_send, s_recv, next(sem), peer) for peer in peers]
            for cp in cps:
                cp.start()
            lvl3.append(cps)
        yield
        for cp in tiny_cps:
            cp.wait_recv()
        tot = stage[0]
        for d in range(1, N_DEV):
            tot = tot + stage[d]
        loss_o[...] = tot[offs["loss"]:offs["loss"] + 1, 0:1]
        for k, (name, rows, cols) in enumerate(TINY):
            r0 = offs[name]
            if rows > 1:
                out_t[k][...] = tot[r0:r0 + rows, 0:cols]
            elif cols >= LANES:
                for i in range(cols // LANES):
                    out_t[k][:, i * LANES:(i + 1) * LANES] = tot[r0 + i:r0 + i + 1, :]
            else:
                out_t[k][...] = tot[r0:r0 + 1, 0:cols]
        for cps in lvl3:
            for cp in cps:
                cp.wait_recv()
        for cps in lvl1 + lvl2 + lvl3 + [tiny_cps]:
            for cp in cps:
                cp.wait_send()

    vmem = pl.BlockSpec(memory_space=pltpu.VMEM)
    t_shapes = [jax.ShapeDtypeStruct((rows, cols), F32) for _, rows, cols in TINY]
    m_shapes = [jax.ShapeDtypeStruct((N_DEV, rows // N_DEV, cols), F32) for _, rows, cols in MEDIUM]
    blk = [(rows // N_DEV, cols) for _, rows, cols in MEDIUM]
    shard = [g.shape[1:] for g in g16]
    scratch = ([pltpu.VMEM((3,) + s, BF16) for s in shard] + [pltpu.VMEM((4,) + s, BF16) for s in shard]
               + [pltpu.VMEM((3,) + s, BF16) for s in shard]
               + [pltpu.VMEM((N_DEV, stage_rows, LANES), F32)]
               + [pltpu.VMEM((4,) + b, F32) for b in blk] + [pltpu.VMEM((4,) + b, F32) for b in blk]
               + [pltpu.VMEM((3,) + b, F32) for b in blk]
               + [pltpu.SemaphoreType.DMA((7 * nb_,)), pltpu.SemaphoreType.DMA((7 * nb_,)),
                  pltpu.SemaphoreType.DMA((7 + 14 * nm_,)), pltpu.SemaphoreType.DMA((7 + 14 * nm_,)),
                  pltpu.SemaphoreType.DMA((nb_,))]
               + [pltpu.VMEM(s, F32) for s in shard])
    n_out = nb_ + 1 + nt + nm_
    res = pl.pallas_call(
        body, name="reduce_final",
        out_shape=tuple(jax.ShapeDtypeStruct(s, F32) for s in shard) + (jax.ShapeDtypeStruct((1, 1), F32),)
        + tuple(t_shapes) + tuple(m_shapes),
        in_specs=[vmem] * nb_ + [pl.BlockSpec(memory_space=pl.ANY)] * nb_ + [vmem] * (1 + nt + nm_),
        out_specs=(vmem,) * n_out, scratch_shapes=scratch,
        compiler_params=pltpu.CompilerParams(vmem_limit_bytes=VMEM_LIMIT),
    )(*g16, *g32, loss, *g_tiny, *g_med)
    return list(res[:nb_]), res[nb_], list(res[nb_ + 1:nb_ + 1 + nt]), list(res[nb_ + 1 + nt:])


def _gather_phases(shard_r, gath, cast, send_sems, recv_sems, local_sems):
    n = len(shard_r)
    x, y, c = _mesh_pos()
    me, sibling = (x, y, c), (x, y, 1 - c)
    chips = [(1 - x, y), (x, 1 - y), (1 - x, 1 - y)]

    def own(a, k, to):
        return _remote(cast[a], gath[a].at[_slot(*me)], send_sems, recv_sems, 7 * a + k, to)

    def passed(a, k, block, to):
        blk = gath[a].at[_slot(*block)]
        return _remote(blk, blk, send_sems, recv_sems, 7 * a + k, to)

    def keep(a):
        return pltpu.make_async_copy(cast[a], gath[a].at[_slot(*me)], local_sems.at[a])

    def start():
        for a in range(n):
            def to16(r, a=a):
                cast[a][r, :] = shard_r[a][r, :].astype(BF16)

            _row_chunks(shard_r[a].shape[0], to16)
            keep(a).start()
            own(a, 0, sibling).start()
            for j, chip in enumerate(chips):
                own(a, 1 + j, (*chip, c)).start()

    def relay():
        for a in range(n):
            for j, chip in enumerate(chips):
                passed(a, 1 + j, (*chip, c), me).wait_recv()
                passed(a, 4 + j, (*chip, c), sibling).start()

    def finish():
        for a in range(n):
            passed(a, 0, sibling, me).wait_recv()
            for j, chip in enumerate(chips):
                passed(a, 4 + j, (*chip, 1 - c), me).wait_recv()
            own(a, 0, sibling).wait_send()
            for j, chip in enumerate(chips):
                own(a, 1 + j, (*chip, c)).wait_send()
                passed(a, 4 + j, (*chip, c), sibling).wait_send()
            keep(a).wait()

    return start, relay, finish


def _gather_operands(shards):
    n = len(shards)
    return ((pl.BlockSpec(memory_space=pl.ANY),) * n,
            tuple(jax.ShapeDtypeStruct((N_DEV,) + s.shape, BF16) for s in shards),
            [pltpu.VMEM(s.shape, BF16) for s in shards]
            + [pltpu.SemaphoreType.DMA((7 * n,)), pltpu.SemaphoreType.DMA((7 * n,)), pltpu.SemaphoreType.DMA((n,))])


def _hosted_reduce_phases(g16_r, g32_r, red, own16, recv1, send2, recv2, own32, s_send, s_recv, s_local):
    n = len(g16_r)
    x, y, c = _mesh_pos()
    sibling = (x, y, 1 - c)
    chips = [(1 - x, y), (x, 1 - y), (1 - x, 1 - y)]
    all_chips = [(x, y)] + chips

    def lvl1(a, j):
        return _remote(g16_r[a].at[_slot(*all_chips[j], 1 - c)], recv1[a].at[j], s_send, s_recv, 7 * a + j, sibling)

    def lvl2(a, j):
        return _remote(send2[a].at[j], recv2[a].at[j], s_send, s_recv, 7 * a + 4 + j, (*chips[j], c))

    def mine(a, j):
        if j == 3:
            only_mine = g32_r[a].shape[0] == 1
            return pltpu.make_async_copy(g32_r[a].at[0 if only_mine else _slot(x, y, c)], own32[a], s_local.at[4 * a + j])
        return pltpu.make_async_copy(g16_r[a].at[_slot(*chips[j], c)], own16[a].at[j], s_local.at[4 * a + j])

    def start():
        for a in range(n):
            for j in range(4):
                mine(a, j).start()
            for j in range(4):
                lvl1(a, j).start()

    def middle():
        for a in range(n):
            for j in range(4):
                mine(a, j).wait()
            for j in range(4):
                lvl1(a, j).wait_recv()

            def partials(r, a=a):
                red[a][r, :] = own32[a][r, :] + recv1[a][0, r, :].astype(F32)
                for j in range(3):
                    send2[a][j, r, :] = (own16[a][j, r, :].astype(F32) + recv1[a][1 + j, r, :].astype(F32)).astype(BF16)

            _row_chunks(own32[a].shape[0], partials)
            for j in range(3):
                lvl2(a, j).start()

    def total():
        for a in range(n):
            for j in range(3):
                lvl2(a, j).wait_recv()

            def add(r, a=a):
                g = red[a][r, :]
                for j in range(3):
                    g = g + recv2[a][j, r, :].astype(F32)
                red[a][r, :] = g

            _row_chunks(own32[a].shape[0], add)

    def finish():
        for a in range(n):
            for j in range(4):
                lvl1(a, j).wait_send()
            for j in range(3):
                lvl2(a, j).wait_send()

    return start, middle, total, finish


def _hosted_reduce_operands(g16, const_spec):
    n = len(g16)
    shard = [g.shape[1:] for g in g16]
    return ([pl.BlockSpec(memory_space=pl.ANY)] * (2 * n),
            tuple(const_spec(s) for s in shard),
            tuple(jax.ShapeDtypeStruct(s, F32) for s in shard),
            [pltpu.VMEM((3,) + s, BF16) for s in shard] + [pltpu.VMEM((4,) + s, BF16) for s in shard]
            + [pltpu.VMEM((3,) + s, BF16) for s in shard] + [pltpu.VMEM((3,) + s, BF16) for s in shard]
            + [pltpu.VMEM(s, F32) for s in shard]
            + [pltpu.SemaphoreType.DMA((7 * n,)), pltpu.SemaphoreType.DMA((7 * n,)), pltpu.SemaphoreType.DMA((4 * n,))])


def _in_proj(x2, g_pre, w_in, tm):
    t = x2.shape[0]

    def body(x_ref, g_ref, w_ref, u_ref, zs_ref, q_ref, k_ref, v_ref, za_ref):
        xv = x_ref[...]
        r = lax.rsqrt(jnp.mean(xv * xv, axis=-1, keepdims=True) + EPS)
        hn = xv * r * g_ref[...]
        proj = _mm_nt(hn, w_ref[...])
        u_ref[...] = proj[:, 0:512]
        zs_ref[...] = proj[:, 512:1024]
        q_ref[...] = proj[:, 1024:1536].astype(BF16)
        k_ref[...] = proj[:, 1536:1664].astype(BF16)
        v_ref[...] = proj[:, 1664:1792].astype(BF16)
        za_ref[...] = proj[:, 1792:2304]

    row = lambda w: pl.BlockSpec((tm, w), lambda i: (i, 0))
    return pl.pallas_call(
        body, name="in_proj", grid=(t // tm,),
        in_specs=[row(D_MODEL), _const_spec((1, D_MODEL)), _const_spec((D_IN, D_MODEL))],
        out_specs=(row(512), row(512), row(512), row(128), row(128), row(512)),
        out_shape=(jax.ShapeDtypeStruct((t, 512), F32),
                   jax.ShapeDtypeStruct((t, 512), F32), jax.ShapeDtypeStruct((t, 512), BF16),
                   jax.ShapeDtypeStruct((t, 128), BF16), jax.ShapeDtypeStruct((t, 128), BF16),
                   jax.ShapeDtypeStruct((t, 512), F32)),
        compiler_params=_tc_params(("arbitrary",)),
    )(x2, g_pre, w_in)


def _discretise(lr, li, ls):
    step = jnp.exp(ls)
    mag = jnp.exp(lr * step)
    ar = mag * jnp.cos(li * step)
    ai = mag * jnp.sin(li * step)
    den = lr * lr + li * li
    cr = ((ar - 1.0) * lr + ai * li) / den
    ci = (ai * lr - (ar - 1.0) * li) / den
    return step, ar, ai, den, cr, ci


def _per_channel(v):
    return jnp.broadcast_to(v[:, None, :], (SSM_G, SSM_P, SSM_N)).reshape(SSM_G * SSM_P, SSM_N)


def _tile_masks():
    r = lax.broadcasted_iota(jnp.int32, (CH_T, ST_T), 0) // SSM_P
    l = lax.broadcasted_iota(jnp.int32, (CH_T, ST_T), 1) // SSM_N
    lt = lax.broadcasted_iota(jnp.int32, (ST_T, CH_T), 0) // SSM_N
    rt = lax.broadcasted_iota(jnp.int32, (ST_T, CH_T), 1) // SSM_P
    rep = lax.broadcasted_iota(jnp.int32, (SSM_N, ST_T), 0) == lax.broadcasted_iota(jnp.int32, (SSM_N, ST_T), 1) % SSM_N
    rep_t = lax.broadcasted_iota(jnp.int32, (ST_T, SSM_N), 0) % SSM_N == lax.broadcasted_iota(jnp.int32, (ST_T, SSM_N), 1)
    return r == l, lt == rt, rep, rep_t


def _ssm_prep(lam_re, lam_im, log_step, b_re, b_im, c_re, c_im, seg):
    def work(in_refs, out_refs):
        lr_ref, li_ref, ls_ref, br_ref, bi_ref, cre_ref, cim_ref, lrr_ref, lir_ref, lsr_ref = in_refs
        ar_ref, ai_ref, pr_ref, pi_ref, bcat_ref, bcat_t_ref, ccat_ref, ccat_t_ref = out_refs
        _, _, _, _, cr, ci = _discretise(lr_ref[...], li_ref[...], ls_ref[...])
        cr, ci = _per_channel(cr), _per_channel(ci)
        br, bi = br_ref[...], bi_ref[...]
        bb_re = cr * br - ci * bi
        bb_im = cr * bi + ci * br
        same, same_t, rep, rep_t = _tile_masks()
        rep, rep_t = rep.astype(BF16), rep_t.astype(BF16)
        for j in range(N_GT):
            rows = slice(j * CH_T, (j + 1) * CH_T)
            for wide, tall, parts in ((bcat_ref, bcat_t_ref, (bb_re[rows], bb_im[rows])),
                                      (ccat_t_ref, ccat_ref, (cre_ref[rows, :], -cim_ref[rows, :]))):
                for k, part in enumerate(parts):
                    p16 = part.astype(BF16)
                    wide[j, :, k * ST_T:(k + 1) * ST_T] = jnp.where(same, _mm(p16, rep), 0.0).astype(BF16)
                    tall[j, k * ST_T:(k + 1) * ST_T, :] = jnp.where(same_t, _mm_nt(rep_t, p16), 0.0).astype(BF16)
        stepr = jnp.exp(lsr_ref[...])
        mag = jnp.exp(lrr_ref[...] * stepr)
        a_r, a_i = mag * jnp.cos(lir_ref[...] * stepr), mag * jnp.sin(lir_ref[...] * stepr)
        p_r, p_i = a_r, a_i
        for k in range(8):
            pr_ref[k:k + 1, :] = p_r
            pi_ref[k:k + 1, :] = p_i
            p_r, p_i = p_r * a_r - p_i * a_i, p_r * a_i + p_i * a_r
        n = 8
        while n < seg:
            tr, ti = pr_ref[n - 1:n, :], pi_ref[n - 1:n, :]
            xr, xi = pr_ref[0:n, :], pi_ref[0:n, :]
            pr_ref[n:2 * n, :] = xr * tr - xi * ti
            pi_ref[n:2 * n, :] = xr * ti + xi * tr
            n *= 2
        ar_ref[...] = pr_ref[0:1, :]
        ai_ref[...] = pi_ref[0:1, :]

    row = jax.ShapeDtypeStruct((1, N_STATE), F32)
    pw = jax.ShapeDtypeStruct((seg, N_STATE), F32)
    wide = jax.ShapeDtypeStruct((N_GT, CH_T, 2 * ST_T), BF16)
    tall = jax.ShapeDtypeStruct((N_GT, 2 * ST_T, CH_T), BF16)
    step_row = jnp.broadcast_to(log_step, (SSM_G, SSM_N)).reshape(1, N_STATE)
    inputs = (lam_re, lam_im, log_step, b_re, b_im, c_re, c_im, lam_re.reshape(1, N_STATE),
              lam_im.reshape(1, N_STATE), step_row)
    return work, inputs, (row, row, pw, pw, wide, tall, tall, wide)


def _seg_rows(t):
    if isinstance(t, int):
        return pl.ds(t * N_SEG, N_SEG)
    return pl.ds(pl.multiple_of(t * N_SEG, N_SEG), N_SEG)


def _scan_forward(xs, a_re, a_im, pw_re, pw_im, cs, seg):
    are = jnp.broadcast_to(a_re, (N_SEG, ST_T))
    aim = jnp.broadcast_to(a_im, (N_SEG, ST_T))

    def steps(k, carry):
        xr, xi = carry
        for j in range(SCAN_UNROLL):
            r = pl.multiple_of((k * SCAN_UNROLL + j) * N_SEG, N_SEG)
            nr = are * xr - aim * xi + xs[pl.ds(r, N_SEG), 0:ST_T]
            ni = are * xi + aim * xr + xs[pl.ds(r, N_SEG), ST_T:2 * ST_T]
            xs[pl.ds(r, N_SEG), 0:ST_T] = nr
            xs[pl.ds(r, N_SEG), ST_T:2 * ST_T] = ni
            xr, xi = nr, ni
        return xr, xi

    zero = jnp.zeros((N_SEG, ST_T), F32)
    fr, fi = lax.fori_loop(0, seg // SCAN_UNROLL, steps, (zero, zero))
    sr, si = pw_re[seg - 1:seg, :], pw_im[seg - 1:seg, :]
    cr = jnp.zeros((1, ST_T), F32)
    ci = jnp.zeros((1, ST_T), F32)
    cs[0:1, :] = cr
    cs[8:9, :] = ci
    for s in range(1, N_SEG):
        ncr = sr * cr - si * ci + fr[s - 1:s, :]
        nci = sr * ci + si * cr + fi[s - 1:s, :]
        cr, ci = ncr, nci
        cs[s:s + 1, :] = cr
        cs[8 + s:9 + s, :] = ci
    car, cai = cs[0:8, :], cs[8:16, :]

    def fix(t, _):
        r = pl.multiple_of(t * N_SEG, N_SEG)
        pr, pi = pw_re[pl.ds(t, 1), :], pw_im[pl.ds(t, 1), :]
        xs[pl.ds(r, N_SEG), 0:ST_T] = xs[pl.ds(r, N_SEG), 0:ST_T] + (pr * car - pi * cai)
        xs[pl.ds(r, N_SEG), ST_T:2 * ST_T] = xs[pl.ds(r, N_SEG), ST_T:2 * ST_T] + (pr * cai + pi * car)
        return 0

    lax.fori_loop(0, seg, fix, 0, unroll=SCAN_UNROLL)


def _interleave(src, dst, seg):
    for s in range(N_SEG):
        dst[pl.ds(s, seg, stride=N_SEG), :] = src[s]


def _deinterleave(src, seg, s):
    return src[pl.ds(s, seg, stride=N_SEG), :]


def _ssm_forward(u, bcat, ccat, a_re, a_im, pw_re, pw_im, d_row, late, seg):
    bl = u.shape[0]
    rows = N_SEG * seg
    n = len(late)
    steps = bl * N_GT

    def body(*refs):
        u_ref, b_ref, c_ref, ar_ref, ai_ref, pr_ref, pi_ref, d_ref = refs[:8]
        late_r = refs[8:8 + n]
        y_ref, xs_ref, cs_ref = refs[8 + n:11 + n]
        gath, cast = refs[11 + n:11 + 2 * n], refs[11 + 2 * n:11 + 3 * n]
        send_sems, recv_sems, local_sems, ui, yi = refs[11 + 3 * n:]
        step = pl.program_id(0) * N_GT + pl.program_id(1)
        start, relay, finish = _gather_phases(late_r, gath, cast, send_sems, recv_sems, local_sems)
        pl.when(step == 0)(start)
        _interleave(u_ref.at[0], ui, seg)
        u = ui[...]
        xs, cs = xs_ref.at[0, 0], cs_ref.at[0, 0]
        xs[:, 0:ST_T] = _mm(u, b_ref[0, :, 0:ST_T])
        xs[:, ST_T:] = _mm(u, b_ref[0, :, ST_T:])
        _scan_forward(xs, ar_ref[...], ai_ref[...], pr_ref, pi_ref, cs, seg)
        yi[...] = _mm(xs[:, 0:ST_T], c_ref[0, 0:ST_T, :]) + _mm(xs[:, ST_T:], c_ref[0, ST_T:, :]) + d_ref[...] * u
        for s in range(N_SEG):
            y_ref[0, s] = _deinterleave(yi, seg, s)
        pl.when(step == steps // 2)(relay)
        pl.when(step == steps - 1)(finish)

    state = lambda r, c: pl.BlockSpec((1, 1, r, c), lambda b, j: (b, j, 0, 0))
    act = pl.BlockSpec((1, N_SEG, seg, CH_T), lambda b, j: (b, 0, 0, j))
    g_specs, g_shapes, g_scratch = _gather_operands(late)
    res = pl.pallas_call(
        body, name="ssm_forward", grid=(bl, N_GT),
        in_specs=[act,
                  pl.BlockSpec((1, CH_T, 2 * ST_T), lambda b, j: (j, 0, 0)),
                  pl.BlockSpec((1, 2 * ST_T, CH_T), lambda b, j: (j, 0, 0)),
                  pl.BlockSpec((1, ST_T), lambda b, j: (0, j)), pl.BlockSpec((1, ST_T), lambda b, j: (0, j)),
                  pl.BlockSpec((seg, ST_T), lambda b, j: (0, j)), pl.BlockSpec((seg, ST_T), lambda b, j: (0, j)),
                  pl.BlockSpec((1, CH_T), lambda b, j: (0, j))]
        + [pl.BlockSpec(s.shape, lambda b, j: (0, 0)) for s in late],
        out_specs=(act, state(rows, 2 * ST_T), state(16, ST_T)) + g_specs,
        out_shape=(jax.ShapeDtypeStruct((bl, N_SEG, seg, D_SSM), F32),
                   jax.ShapeDtypeStruct((bl, N_GT, rows, 2 * ST_T), F32),
                   jax.ShapeDtypeStruct((bl, N_GT, 16, ST_T), F32)) + g_shapes,
        scratch_shapes=g_scratch + [pltpu.VMEM((rows, CH_T), F32), pltpu.VMEM((rows, CH_T), F32)],
        compiler_params=_tc_params(("arbitrary", "arbitrary")),
    )(u, bcat, ccat, a_re, a_im, pw_re, pw_im, d_row, *late)
    return res[:3], list(res[3:])


def _ssm_backward(u, dy, states, carries, bcat_t, ccat_t, a_re, a_im, pw_re, pw_im, d_row, late16, late32, seg):
    bl = u.shape[0]
    rows = N_SEG * seg
    n = len(late16)
    grid_steps = N_GT * bl

    def body(*refs):
        u_ref, dy_ref, xs_ref, cs_ref, bt_ref, ct_ref, ar_ref, ai_ref, pr_ref, pi_ref, d_ref = refs[:11]
        g16_r, g32_r = refs[11:11 + n], refs[11 + n:11 + 2 * n]
        du_ref, db_ref, dc_ref, dar_ref, dai_ref, dd_ref = refs[11 + 2 * n:17 + 2 * n]
        red = refs[17 + 2 * n:17 + 3 * n]
        own16, recv1, send2, recv2, own32 = (refs[17 + 3 * n + k * n:17 + 3 * n + (k + 1) * n] for k in range(5))
        s_send, s_recv, s_local, ls, cl, ui, dyi, dui = refs[17 + 8 * n:]
        b = pl.program_id(1)
        step = pl.program_id(0) * bl + b
        start, middle, total, finish = _hosted_reduce_phases(g16_r, g32_r, red, own16, recv1, send2, recv2, own32,
                                                             s_send, s_recv, s_local)
        pl.when(step == 0)(start)
        pl.when(step == grid_steps // 4)(middle)
        pl.when(step == (grid_steps * 3) // 4)(total)
        pl.when(step == grid_steps - 1)(finish)

        @pl.when(b == 0)
        def _():
            for ref in (db_ref, dc_ref, dar_ref, dai_ref, dd_ref):
                ref[...] = jnp.zeros(ref.shape, F32)

        _interleave(u_ref.at[0], ui, seg)
        _interleave(dy_ref.at[0], dyi, seg)
        u = ui[...]
        dy = dyi[...]
        xs, cs = xs_ref.at[0, 0], cs_ref.at[0, 0]
        ls[...] = _mm(dy, ct_ref[0])
        are = jnp.broadcast_to(ar_ref[...], (N_SEG, ST_T))
        aim = jnp.broadcast_to(ai_ref[...], (N_SEG, ST_T))

        def steps(k, carry):
            lr, li = carry
            for j in range(SCAN_UNROLL):
                r = pl.multiple_of((seg - 1 - (k * SCAN_UNROLL + j)) * N_SEG, N_SEG)
                nr = are * lr + aim * li + ls[pl.ds(r, N_SEG), 0:ST_T]
                ni = are * li - aim * lr + ls[pl.ds(r, N_SEG), ST_T:2 * ST_T]
                ls[pl.ds(r, N_SEG), 0:ST_T] = nr
                ls[pl.ds(r, N_SEG), ST_T:2 * ST_T] = ni
                lr, li = nr, ni
            return lr, li

        zero = jnp.zeros((N_SEG, ST_T), F32)
        fr, fi = lax.fori_loop(0, seg // SCAN_UNROLL, steps, (zero, zero))
        sr, si = pr_ref[seg - 1:seg, :], pi_ref[seg - 1:seg, :]
        cr = jnp.zeros((1, ST_T), F32)
        ci = jnp.zeros((1, ST_T), F32)
        cl[7:8, :] = cr
        cl[15:16, :] = ci
        for s in range(N_SEG - 2, -1, -1):
            ncr = sr * cr + si * ci + fr[s + 1:s + 2, :]
            nci = sr * ci - si * cr + fi[s + 1:s + 2, :]
            cr, ci = ncr, nci
            cl[s:s + 1, :] = cr
            cl[8 + s:9 + s, :] = ci
        clr, cli = cl[0:8, :], cl[8:16, :]

        def fix_rows(rows, t, xpr, xpi, acc):
            dr, di = acc
            pr, pi = pr_ref[pl.ds(seg - 1 - t, 1), :], pi_ref[pl.ds(seg - 1 - t, 1), :]
            lr = ls[rows, 0:ST_T] + (pr * clr + pi * cli)
            li = ls[rows, ST_T:2 * ST_T] + (pr * cli - pi * clr)
            ls[rows, 0:ST_T] = lr
            ls[rows, ST_T:2 * ST_T] = li
            return dr + (lr * xpr + li * xpi), di + (li * xpr - lr * xpi)

        def fix_at(t, acc):
            prev = _seg_rows(t - 1)
            return fix_rows(_seg_rows(t), t, xs[prev, 0:ST_T], xs[prev, ST_T:2 * ST_T], acc)

        def fix(k, acc):
            for j in range(SCAN_UNROLL):
                acc = fix_at(k * SCAN_UNROLL + j, acc)
            return acc

        acc = fix_rows(pl.ds(0, N_SEG), 0, cs[0:8, :], cs[8:16, :], (zero, zero))
        for t in range(1, SCAN_UNROLL):
            acc = fix_at(t, acc)
        dr, di = lax.fori_loop(1, seg // SCAN_UNROLL, fix, acc)
        dar = jnp.sum(dr, axis=0, keepdims=True)
        dai = jnp.sum(di, axis=0, keepdims=True)
        lall = ls[...]
        dui[...] = _mm(lall, bt_ref[0]) + d_ref[...] * dy
        for s in range(N_SEG):
            du_ref[0, s] = _deinterleave(dui, seg, s).astype(BF16)
        db_ref[0] += _mm_tn(u, lall)
        dc_ref[0] += _mm_tn(dy, xs[...])
        dar_ref[...] += dar
        dai_ref[...] += dai
        dd_ref[...] += jnp.sum(dy * u, axis=0, keepdims=True)

    tile3 = lambda r, c: pl.BlockSpec((1, r, c), lambda j, b: (j, 0, 0))
    lane = lambda r, c: pl.BlockSpec((r, c), lambda j, b: (0, j))
    act = pl.BlockSpec((1, N_SEG, seg, CH_T), lambda j, b: (b, 0, 0, j))
    state = lambda r, c: pl.BlockSpec((1, 1, r, c), lambda j, b: (b, j, 0, 0))
    r_in, r_out, r_shapes, r_scratch = _hosted_reduce_operands(late16, lambda s: pl.BlockSpec(s, lambda j, b: (0, 0)))
    res = pl.pallas_call(
        body, name="ssm_backward", grid=(N_GT, bl),
        in_specs=[act, act, state(rows, 2 * ST_T), state(16, ST_T), tile3(2 * ST_T, CH_T), tile3(CH_T, 2 * ST_T),
                  lane(1, ST_T), lane(1, ST_T), lane(seg, ST_T), lane(seg, ST_T), lane(1, CH_T)] + r_in,
        out_specs=(act, tile3(CH_T, 2 * ST_T), tile3(CH_T, 2 * ST_T), lane(1, ST_T), lane(1, ST_T), lane(1, CH_T))
        + r_out,
        out_shape=(jax.ShapeDtypeStruct((bl, N_SEG, seg, D_SSM), BF16),
                   jax.ShapeDtypeStruct((N_GT, CH_T, 2 * ST_T), F32), jax.ShapeDtypeStruct((N_GT, CH_T, 2 * ST_T), F32),
                   jax.ShapeDtypeStruct((1, N_STATE), F32), jax.ShapeDtypeStruct((1, N_STATE), F32),
                   jax.ShapeDtypeStruct((1, D_SSM), F32)) + r_shapes,
        scratch_shapes=r_scratch + [pltpu.VMEM((rows, 2 * ST_T), F32), pltpu.VMEM((16, ST_T), F32)]
        + [pltpu.VMEM((rows, CH_T), F32)] * 3,
        compiler_params=_tc_params(("arbitrary", "arbitrary")),
    )(u, dy, states, carries, bcat_t, ccat_t, a_re, a_im, pw_re, pw_im, d_row, *late16, *late32)
    return res[:6], list(res[6:])


def _ssm_param_grads(lam_re, lam_im, log_step, b_re, b_im, da_re, da_im, d_bcat, d_ccat_t):
    def body(lr_ref, li_ref, ls_ref, br_ref, bi_ref, gar_ref, gai_ref, gbcat_ref, gccat_ref,
             dlr_ref, dli_ref, dls_ref, dbr_ref, dbi_ref, dcr_ref, dci_ref, gbr_s, gbi_s):
        same, _, _, rep_t = _tile_masks()
        rep_t = rep_t.astype(F32)
        for j in range(N_GT):
            rows = slice(j * CH_T, (j + 1) * CH_T)
            for src, dsts in ((gbcat_ref, (gbr_s, gbi_s)), (gccat_ref, (dcr_ref, dci_ref))):
                for k, dst in enumerate(dsts):
                    blk = jnp.where(same, src[j, :, k * ST_T:(k + 1) * ST_T], 0.0)
                    dst[rows, :] = jnp.dot(blk, rep_t, precision=lax.Precision.HIGHEST, preferred_element_type=F32)
        dci_ref[...] = -dci_ref[...]
        lr, li = lr_ref[...], li_ref[...]
        step, ar, ai, den, cr, ci = _discretise(lr, li, ls_ref[...])
        crb, cib = _per_channel(cr), _per_channel(ci)
        br, bi = br_ref[...], bi_ref[...]
        gbr, gbi = gbr_s[...], gbi_s[...]
        dbr_ref[...] = crb * gbr + cib * gbi
        dbi_ref[...] = crb * gbi - cib * gbr
        over_channels = lambda t: jnp.sum(t.reshape(SSM_G, SSM_P, SSM_N), axis=1)
        gcr = over_channels(br * gbr + bi * gbi)
        gci = over_channels(br * gbi - bi * gbr)
        ilr, ili = lr / den, -li / den
        gar = gar_ref[...] + (ilr * gcr + ili * gci)
        gai = gai_ref[...] + (ilr * gci - ili * gcr)
        qr, qi = cr * ilr - ci * ili, cr * ili + ci * ilr
        glr = -(qr * gcr + qi * gci)
        gli = -(qr * gci - qi * gcr)
        gwr = ar * gar + ai * gai
        gwi = ar * gai - ai * gar
        dlr_ref[...] = glr + step * gwr
        dli_ref[...] = gli + step * gwi
        dls_ref[...] = jnp.sum(lr * gwr + li * gwi, axis=-1, keepdims=True) * step

    lam = jax.ShapeDtypeStruct((SSM_G, SSM_N), F32)
    mat = jax.ShapeDtypeStruct((SSM_G * SSM_P, SSM_N), F32)
    vm = pl.BlockSpec(memory_space=pltpu.VMEM)
    return pl.pallas_call(
        body, name="ssm_param_grads", out_shape=(lam, lam, jax.ShapeDtypeStruct((SSM_G, 1), F32), mat, mat, mat, mat),
        in_specs=[vm] * 9, out_specs=(vm,) * 7,
        scratch_shapes=[pltpu.VMEM((SSM_G * SSM_P, SSM_N), F32), pltpu.VMEM((SSM_G * SSM_P, SSM_N), F32)],
    )(lam_re, lam_im, log_step, b_re, b_im, da_re, da_im, d_bcat, d_ccat_t)


ROWS4 = Q_PER_KV * ATT_BLOCK
ATT_FWD_STACK = 1


def _att_dist_mask(first_block):
    qi = lax.broadcasted_iota(jnp.int32, (ROWS4, 2 * ATT_BLOCK), 0) & (ATT_BLOCK - 1)
    si = lax.broadcasted_iota(jnp.int32, (ROWS4, 2 * ATT_BLOCK), 1)
    dist = qi + ATT_BLOCK - si
    valid = (dist >= 0) & (dist < ATT_BLOCK) & ((si >= ATT_BLOCK) | jnp.logical_not(first_block))
    return dist.astype(F32), valid


def _stack_heads(x, kv):
    return jnp.concatenate([x[:, (kv * Q_PER_KV + g) * HEAD_DIM:(kv * Q_PER_KV + g + 1) * HEAD_DIM]
                            for g in range(Q_PER_KV)], axis=0)


def _stack_cols(x, kv):
    return jnp.concatenate([x[:, kv * Q_PER_KV + g:kv * Q_PER_KV + g + 1] for g in range(Q_PER_KV)], axis=0)


def _per_head_col(vals):
    return jnp.concatenate([jnp.full((ATT_BLOCK, 1), v, F32) for v in vals], axis=0)


def _attn_forward(q, k, v, sinks, bl, nb):
    t = q.shape[0]

    def body(sink_ref, q_ref, kp_ref, kc_ref, vp_ref, vc_ref, o_ref, lse_ref):
        i = pl.program_id(1)
        dist4, valid4 = _att_dist_mask(i == 0)
        rows2 = ATT_FWD_STACK * ATT_BLOCK
        dist, valid = dist4[0:rows2, :], valid4[0:rows2, :]
        kk = jnp.concatenate([kp_ref[...], kc_ref[...]], axis=0)
        vv = jnp.concatenate([vp_ref[...], vc_ref[...]], axis=0)
        qv = q_ref[...]
        col = lambda vals: jnp.concatenate([jnp.full((ATT_BLOCK, 1), v, F32) for v in vals], axis=0)
        stacks = [range(h0, h0 + ATT_FWD_STACK) for h0 in range(0, N_HEADS, ATT_FWD_STACK)]
        kv_cols = lambda heads: slice(heads[0] // Q_PER_KV * HEAD_DIM, (heads[0] // Q_PER_KV + 1) * HEAD_DIM)
        scores = [_mm_nt(jnp.concatenate([qv[:, h * HEAD_DIM:(h + 1) * HEAD_DIM] for h in heads], axis=0),
                         kk[:, kv_cols(heads)]) for heads in stacks]
        softmaxes = []
        for heads, qk in zip(stacks, scores):
            slope = col([2.0 ** (-(h + 1)) for h in heads])
            sink = col([sink_ref[h] for h in heads])
            s = jnp.where(valid, qk * ATT_SCALE - slope * dist, NEG_INF)
            m = jnp.maximum(jnp.max(s, axis=-1, keepdims=True), sink)
            e = jnp.exp(s - m)
            den = jnp.sum(e, axis=-1, keepdims=True) + jnp.exp(sink - m)
            softmaxes.append((e.astype(BF16), 1.0 / den, m + jnp.log(den)))
        for heads, (e, inv_den, lse) in zip(stacks, softmaxes):
            o = _mm(e, vv[:, kv_cols(heads)]) * inv_den
            for g, h in enumerate(heads):
                rows = slice(g * ATT_BLOCK, (g + 1) * ATT_BLOCK)
                o_ref[:, h * HEAD_DIM:(h + 1) * HEAD_DIM] = o[rows, :]
                lse_ref[:, h:h + 1] = lse[rows, :]

    cur = lambda w: pl.BlockSpec((ATT_BLOCK, w), lambda b, i: (b * nb + i, 0))
    prev = lambda w: pl.BlockSpec((ATT_BLOCK, w), lambda b, i: (b * nb + jnp.maximum(i - 1, 0), 0))
    return pl.pallas_call(
        body, name="attn_forward", grid=(bl, nb),
        in_specs=[pl.BlockSpec(memory_space=pltpu.SMEM), cur(512), prev(128), cur(128), prev(128), cur(128)],
        out_specs=(cur(512), cur(N_HEADS)),
        out_shape=(jax.ShapeDtypeStruct((t, D_ATTN), F32), jax.ShapeDtypeStruct((t, N_HEADS), F32)),
        compiler_params=_tc_params(("arbitrary", "arbitrary")),
    )(sinks, q, k, k, v, v)


def _attn_backward(q, k, v, o, do, lse, sinks, bl, nb):
    t = q.shape[0]

    def body(sink_ref, qc_ref, kp_ref, kc_ref, vp_ref, vc_ref, oc_ref, doc_ref, lc_ref,
             dq_ref, dk_ref, dv_ref, ds_ref, dk_carry, dv_carry):
        b, i = pl.program_id(0), pl.program_id(1)
        live = i < nb

        @pl.when(i == 0)
        def _():
            dk_carry[...] = jnp.zeros((ATT_BLOCK, KV_HEADS * HEAD_DIM), F32)
            dv_carry[...] = jnp.zeros((ATT_BLOCK, KV_HEADS * HEAD_DIM), F32)

        dist, valid = _att_dist_mask(i == 0)
        valid = valid & live
        kk = jnp.concatenate([kp_ref[...], kc_ref[...]], axis=0)
        vv = jnp.concatenate([vp_ref[...], vc_ref[...]], axis=0)
        qc, oc, doc, lc = qc_ref[...], oc_ref[...], doc_ref[...], lc_ref[...]
        dsink_cols, dq_parts, dk_t, dv_t = [], [], [], []
        for kv in range(KV_HEADS):
            heads = range(kv * Q_PER_KV, (kv + 1) * Q_PER_KV)
            cols = slice(kv * HEAD_DIM, (kv + 1) * HEAD_DIM)
            kh, vh = kk[:, cols], vv[:, cols]
            slope = _per_head_col([2.0 ** (-(h + 1)) for h in heads])
            sink = _per_head_col([sink_ref[h] for h in heads])
            q4, do4 = _stack_heads(qc, kv), _stack_heads(doc, kv)
            delta = jnp.sum(do4 * _stack_heads(oc, kv), axis=-1, keepdims=True)
            lse4 = _stack_cols(lc, kv)
            s = _mm_nt(q4, kh) * ATT_SCALE - slope * dist
            p = jnp.where(valid, jnp.exp(s - lse4), 0.0)
            dsc = p * (_mm_nt(do4, vh) - delta)
            dq4 = _mm(dsc, kh) * ATT_SCALE
            dk_t.append(_mm_tn(q4, dsc) * ATT_SCALE)
            dv_t.append(_mm_tn(do4, p))
            dsink4 = jnp.where(live, jnp.exp(sink - lse4) * delta, 0.0)
            for g, h in enumerate(heads):
                rows = slice(g * ATT_BLOCK, (g + 1) * ATT_BLOCK)
                dq_parts.append((h, dq4[rows, :]))
                dsink_cols.append(-jnp.sum(dsink4[rows, :], axis=0, keepdims=True))
        dsink = jnp.concatenate(dsink_cols, axis=1)
        for out_ref, carry, parts in ((dk_ref, dk_carry, dk_t), (dv_ref, dv_carry, dv_t)):
            both = jnp.concatenate(parts, axis=0)
            out_ref[...] = (carry[...] + both[:, 0:ATT_BLOCK]).T
            carry[...] = both[:, ATT_BLOCK:]

        @pl.when(live)
        def _():
            for h, part in dq_parts:
                dq_ref[:, h * HEAD_DIM:(h + 1) * HEAD_DIM] = part

        @pl.when((b == 0) & (i == 0))
        def _():
            ds_ref[...] = dsink

        @pl.when((b != 0) | (i != 0))
        def _():
            ds_ref[...] += dsink

    cur_i = lambda i: jnp.minimum(i, nb - 1)
    cur = lambda w: pl.BlockSpec((ATT_BLOCK, w), lambda b, i: (b * nb + cur_i(i), 0))
    prev = lambda w: pl.BlockSpec((ATT_BLOCK, w), lambda b, i: (b * nb + jnp.maximum(cur_i(i) - 1, 0), 0))
    behind = lambda w: pl.BlockSpec((ATT_BLOCK, w), lambda b, i: (b * nb + jnp.maximum(i - 1, 0), 0))
    return pl.pallas_call(
        body, name="attn_backward", grid=(bl, nb + 1),
        in_specs=[pl.BlockSpec(memory_space=pltpu.SMEM), cur(512), prev(128), cur(128), prev(128), cur(128),
                  cur(512), cur(512), cur(N_HEADS)],
        out_specs=(cur(512), behind(128), behind(128), pl.BlockSpec((1, N_HEADS), lambda b, i: (0, 0))),
        out_shape=(jax.ShapeDtypeStruct((t, D_ATTN), F32), jax.ShapeDtypeStruct((t, 128), F32),
                   jax.ShapeDtypeStruct((t, 128), F32), jax.ShapeDtypeStruct((1, N_HEADS), F32)),
        scratch_shapes=[pltpu.VMEM((ATT_BLOCK, KV_HEADS * HEAD_DIM), F32), pltpu.VMEM((ATT_BLOCK, KV_HEADS * HEAD_DIM), F32)],
        compiler_params=_tc_params(("arbitrary", "arbitrary")),
    )(sinks, q, k, k, v, v, o, do, lse)


def _mix_forward_backward(x2, y2, z_ssm, attn, z_attn, p2, target2, w_glu, b_glu, w_out, g_post, w_gate, b_gate,
                          w_proj, tm):
    t = x2.shape[0]
    proj_cols = D_MODEL // N_DEV

    def body(x_ref, y_ref, zs_ref, at_ref, za_ref, p_ref, tg_ref,
             wglu_ref, bglu_ref, wout_ref, gpost_ref, wgate_ref, bgate_ref, wproj_ref,
             loss_ref, dh1_ref, dy_ref, dzs_ref, dat_ref, dza_ref,
             dwglu_own_ref, dbglu_ref, dwout_own_ref, dgpost_ref, dwgate_own_ref, dbgate_ref, dwproj_own_ref,
             dwout16_ref, dwgate16_ref, dwproj16_ref, dwglu16_ref, dwglu_ref, dwout_ref, dwgate_ref, dwproj_ref):
        i = pl.program_id(0)
        gpost = gpost_ref[...]

        @pl.when(i == 0)
        def _():
            for ref in (dwglu_ref, dbglu_ref, dwout_ref, dgpost_ref, dwgate_ref, dbgate_ref, dwproj_ref, loss_ref):
                ref[...] = jnp.zeros(ref.shape, F32)

        def chain(rows):
            y = y_ref[rows, :]
            u3 = GELU_C * (y + GELU_K * y * y * y)
            th = jnp.tanh(u3)
            gl = 0.5 * y * (1.0 + th)
            a = _mm(gl, wglu_ref[...]) + bglu_ref[...]
            sa = _sigmoid(a)
            glu = gl * sa
            zs = zs_ref[rows, :]
            sgs = _sigmoid(zs)
            ssm_out = glu * (zs * sgs)
            za = za_ref[rows, :]
            sga = _sigmoid(za)
            at = at_ref[rows, :]
            attn_out = at * (za * sga)
            cat = jnp.concatenate([ssm_out, attn_out], axis=-1).astype(BF16)
            mixed = _mm(cat, wout_ref[...])
            r2 = lax.rsqrt(jnp.mean(mixed * mixed, axis=-1, keepdims=True) + EPS)
            nhat = mixed * r2
            h1 = x_ref[rows, :] + nhat * gpost
            gate = _sigmoid(_mm(h1, wgate_ref[...]) + bgate_ref[...])
            pv = p_ref[rows, :]
            pp = _mm(pv, wproj_ref[...])
            h2 = h1 + gate * pp
            err = h2 - tg_ref[rows, :]
            loss_part = jnp.sum(jnp.sum(err * err, axis=-1, keepdims=True), axis=0, keepdims=True) * (0.5 / D_MODEL)
            dh2 = err * (1.0 / D_MODEL)
            dgp = dh2 * pp * gate * (1.0 - gate)
            dpp = dh2 * gate
            dh1 = dh2 + _mm_nt(dgp, wgate_ref[...])
            dwproj_ref[...] += _mm_tn(pv, dpp)
            dwgate_ref[...] += _mm_tn(h1, dgp)
            dh1_ref[rows, :] = dh1
            dnhat = dh1 * gpost
            dmixed = r2 * (dnhat - nhat * jnp.mean(dnhat * nhat, axis=-1, keepdims=True))
            dcat = _mm_nt(dmixed, wout_ref[...])
            dwout_ref[...] += _mm_tn(cat, dmixed)
            dso, dao = dcat[:, 0:D_SSM], dcat[:, D_SSM:]
            dat_ref[rows, :] = dao * (za * sga)
            dza_ref[rows, :] = (dao * at * (sga * (1.0 + za * (1.0 - sga)))).astype(BF16)
            dzs_ref[rows, :] = (dso * glu * (sgs * (1.0 + zs * (1.0 - sgs)))).astype(BF16)
            dglu = dso * (zs * sgs)
            da = dglu * gl * sa * (1.0 - sa)
            dgl = dglu * sa + _mm_nt(da, wglu_ref[...])
            dwglu_ref[...] += _mm_tn(gl, da)
            dgelu = 0.5 * (1.0 + th) + 0.5 * y * (1.0 - th * th) * (GELU_C * (1.0 + 3.0 * GELU_K * y * y))
            dy_ref[rows, :] = dgl * dgelu
            dbglu_ref[...] += jnp.sum(da, axis=0, keepdims=True)
            dgpost_ref[...] += jnp.sum(dh1 * nhat, axis=0, keepdims=True)
            dbgate_ref[...] += jnp.sum(dgp, axis=0, keepdims=True)
            loss_ref[...] += loss_part

        chain(slice(None))

        @pl.when(i == t // tm - 1)
        def _():
            for ref16, ref in ((dwout16_ref, dwout_ref), (dwgate16_ref, dwgate_ref), (dwglu16_ref, dwglu_ref)):
                def to16(r, ref16=ref16, ref=ref):
                    ref16[r, :] = ref[r, :].astype(BF16)

                _row_chunks(ref.shape[0], to16)
            for ref, own_ref in ((dwglu_ref, dwglu_own_ref), (dwout_ref, dwout_own_ref), (dwgate_ref, dwgate_own_ref)):
                _copy_owned_rows(ref, own_ref)
            me = _slot(*_mesh_pos())
            for d in range(N_DEV):
                cols = slice(d * proj_cols, (d + 1) * proj_cols)
                dwproj16_ref[d] = dwproj_ref[:, cols].astype(BF16)

                @pl.when(me == d)
                def _(cols=cols):
                    dwproj_own_ref[0] = dwproj_ref[:, cols]

    row = lambda w: pl.BlockSpec((tm, w), lambda i: (i, 0))
    acc = lambda *shape, dt=F32: (_const_spec(shape), jax.ShapeDtypeStruct(shape, dt))
    accs = [acc(1, D_SSM // N_DEV, D_SSM), acc(1, D_SSM), acc(1, D_MODEL // N_DEV, D_MODEL), acc(1, D_MODEL),
            acc(1, D_MODEL // N_DEV, D_MODEL), acc(1, D_MODEL), acc(1, D_PLE, proj_cols),
            acc(D_MODEL, D_MODEL, dt=BF16), acc(D_MODEL, D_MODEL, dt=BF16), acc(N_DEV, D_PLE, proj_cols, dt=BF16),
            acc(D_SSM, D_SSM, dt=BF16)]
    return pl.pallas_call(
        body, name="mix_forward_backward", grid=(t // tm,),
        in_specs=[row(D_MODEL), row(512), row(512), row(512), row(512), row(D_PLE), row(D_MODEL),
                  _const_spec((D_SSM, D_SSM)), _const_spec((1, D_SSM)), _const_spec((D_MODEL, D_MODEL)),
                  _const_spec((1, D_MODEL)), _const_spec((D_MODEL, D_MODEL)), _const_spec((1, D_MODEL)),
                  _const_spec((D_PLE, D_MODEL))],
        out_specs=(_const_spec((1, 1)), row(D_MODEL), row(512), row(512), row(512), row(512))
        + tuple(a[0] for a in accs),
        out_shape=(jax.ShapeDtypeStruct((1, 1), F32), jax.ShapeDtypeStruct((t, D_MODEL), F32),
                   jax.ShapeDtypeStruct((t, 512), F32),
                   jax.ShapeDtypeStruct((t, 512), BF16), jax.ShapeDtypeStruct((t, 512), F32),
                   jax.ShapeDtypeStruct((t, 512), BF16)) + tuple(a[1] for a in accs),
        scratch_shapes=[pltpu.VMEM((D_SSM, D_SSM), F32), pltpu.VMEM((D_MODEL, D_MODEL), F32),
                        pltpu.VMEM((D_MODEL, D_MODEL), F32), pltpu.VMEM((D_PLE, D_MODEL), F32)],
        compiler_params=_tc_params(("arbitrary",)),
    )(x2, y2, z_ssm, attn, z_attn, p2, target2, w_glu, b_glu, w_out, g_post, w_gate, b_gate, w_proj)


def _in_backward(x2, dh1, du, dz_ssm, dq, dk, dv, dz_attn, g_pre, w_in, tm):
    t = x2.shape[0]

    def body(x_ref, dh1_ref, du_ref, dzs_ref, dq_ref, dk_ref, dv_ref, dza_ref, g_ref, w_ref,
             gx_ref, dw_own_ref, dg_ref, dw16_ref, dw_ref):
        i = pl.program_id(0)

        @pl.when(i == 0)
        def _():
            dw_ref[...] = jnp.zeros((D_IN, D_MODEL), F32)
            dg_ref[...] = jnp.zeros((1, D_MODEL), F32)

        xv = x_ref[...]
        r = lax.rsqrt(jnp.mean(xv * xv, axis=-1, keepdims=True) + EPS)
        xhat = xv * r
        g = g_ref[...]
        hn = (xhat * g).astype(BF16)
        dproj = jnp.concatenate([du_ref[...].astype(BF16), dzs_ref[...].astype(BF16), dq_ref[...].astype(BF16),
                                 dk_ref[...].astype(BF16), dv_ref[...].astype(BF16), dza_ref[...].astype(BF16)],
                                axis=-1)
        dhn = _mm(dproj, w_ref[...])
        dxhat = dhn * g
        gx_ref[...] = dh1_ref[...] + r * (dxhat - xhat * jnp.mean(dxhat * xhat, axis=-1, keepdims=True))
        dw_ref[...] += _mm_tn(dproj, hn)
        dg_ref[...] += jnp.sum(dhn * xhat, axis=0, keepdims=True)

        @pl.when(i == t // tm - 1)
        def _():
            def to16(r):
                dw16_ref[r, :] = dw_ref[r, :].astype(BF16)

            _row_chunks(D_IN, to16)
            _copy_owned_rows(dw_ref, dw_own_ref)

    row = lambda w: pl.BlockSpec((tm, w), lambda i: (i, 0))
    own = (1, D_IN // N_DEV, D_MODEL)
    return pl.pallas_call(
        body, name="in_backward", grid=(t // tm,),
        in_specs=[row(D_MODEL), row(D_MODEL), row(512), row(512), row(512), row(128), row(128), row(512),
                  _const_spec((1, D_MODEL)), _const_spec((D_IN, D_MODEL))],
        out_specs=(row(D_MODEL), _const_spec(own), _const_spec((1, D_MODEL)), _const_spec((D_IN, D_MODEL))),
        out_shape=(jax.ShapeDtypeStruct((t, D_MODEL), F32), jax.ShapeDtypeStruct(own, F32),
                   jax.ShapeDtypeStruct((1, D_MODEL), F32), jax.ShapeDtypeStruct((D_IN, D_MODEL), BF16)),
        scratch_shapes=[pltpu.VMEM((D_IN, D_MODEL), F32)],
        compiler_params=_tc_params(("arbitrary",)),
    )(x2, dh1, du, dz_ssm, dq, dk, dv, dz_attn, g_pre, w_in)


def _local_step(x, p, target, pre_norm_g, w_in, prep, ssm_lam_re, ssm_lam_im, ssm_log_step, ssm_b_re, ssm_b_im, ssm_d,
                ssm_b_glu, attn_sinks, post_norm_g, pl_b_gate, late):
    bl, seq, _ = x.shape
    seg = seq // N_SEG
    nb = seq // ATT_BLOCK
    t = bl * seq
    x2 = x.reshape(t, D_MODEL)
    p2 = p.reshape(t, D_PLE)
    tg2 = target.reshape(t, D_MODEL)

    lam_re, lam_im = ssm_lam_re, ssm_lam_im
    log_step = ssm_log_step.reshape(SSM_G, 1)
    a_re_row, a_im_row, pw_re, pw_im, bcat, bcat_t, ccat, ccat_t = prep
    d_row = ssm_d.reshape(1, D_SSM)

    segments = lambda a: a.reshape(bl, N_SEG, seg, D_SSM)
    u, z_ssm, q, k, v, z_attn = _in_proj(x2, pre_norm_g.reshape(1, D_MODEL), w_in, min(TOKEN_TILE_WIDE, t))
    (y, states, carries), gathered = _ssm_forward(
        segments(u), bcat, ccat, a_re_row, a_im_row, pw_re, pw_im, d_row, late, seg)
    w_out, w_gate, w_proj, w_glu = (_gathered_to_full(n, g) for n, g in zip(LATE_NAMES, gathered))
    sinks = attn_sinks.reshape(N_HEADS)
    attn, lse = _attn_forward(q, k, v, sinks, bl, nb)
    (loss, dh1, dy, dz_ssm, dattn, dz_attn, d_w_glu, d_b_glu, d_w_out, d_g_post, d_w_gate, d_b_gate,
     d_w_proj, *late16) = _mix_forward_backward(
        x2, y.reshape(t, D_SSM), z_ssm, attn, z_attn, p2, tg2, w_glu,
        ssm_b_glu.reshape(1, D_SSM), w_out, post_norm_g.reshape(1, D_MODEL), w_gate, pl_b_gate.reshape(1, D_MODEL),
        w_proj, min(TOKEN_TILE, t))
    owned = lambda ds: [d if n in COL_SHARDED else _full_to_owned(n, d) for n, d in zip(LATE_NAMES, ds)]
    dq, dk, dv, d_sinks = _attn_backward(q, k, v, attn, dattn, lse, sinks, bl, nb)
    (du, d_bcat, d_ccat_t, da_re, da_im, d_d), late_grads = _ssm_backward(
        segments(u), segments(dy), states, carries, bcat_t, ccat_t, a_re_row, a_im_row, pw_re, pw_im,
        d_row, owned(late16), [d_w_out, d_w_gate, d_w_proj, d_w_glu], seg)
    grad_x, d_w_in, d_g_pre, d_w_in16 = _in_backward(
        x2, dh1, du.reshape(t, D_SSM), dz_ssm, dq, dk, dv, dz_attn, pre_norm_g.reshape(1, D_MODEL), w_in,
        min(TOKEN_TILE_WIDE, t))
    d_lam_re, d_lam_im, d_ls, d_b_re, d_b_im, d_c_re, d_c_im = _ssm_param_grads(
        lam_re, lam_im, log_step, ssm_b_re, ssm_b_im, da_re.reshape(SSM_G, SSM_N), da_im.reshape(SSM_G, SSM_N),
        d_bcat, d_ccat_t)
    grads = {
        "pre_norm_g": d_g_pre, "w_in": d_w_in, "w_in16": d_w_in16, "ssm_lam_re": d_lam_re, "ssm_lam_im": d_lam_im,
        "ssm_log_step": d_ls, "ssm_b_re": d_b_re, "ssm_b_im": d_b_im, "ssm_c_re": d_c_re, "ssm_c_im": d_c_im,
        "ssm_d": d_d, "ssm_b_glu": d_b_glu, "attn_sinks": d_sinks, "post_norm_g": d_g_post, "pl_b_gate": d_b_gate,
    }
    return loss, grad_x.reshape(bl, seq, D_MODEL), grads, late_grads


LATE_NAMES = ("w_out", "pl_w_gate", "pl_w_proj", "ssm_w_glu")
BIG_NAMES = ("w_in",) + LATE_NAMES
COL_SHARDED = {"w_in": D_IN // N_DEV, "pl_w_proj": D_MODEL // N_DEV}
WEIGHT_NAMES = ("pre_norm_g", "w_in", "ssm_lam_re", "ssm_lam_im", "ssm_log_step", "ssm_b_re", "ssm_b_im", "ssm_c_re",
                "ssm_c_im", "ssm_d", "ssm_w_glu", "ssm_b_glu", "attn_sinks", "w_out", "post_norm_g", "pl_w_proj",
                "pl_w_gate", "pl_b_gate")


TRANSPOSED = {"w_in": (0, 1), "ssm_b_re": (1, 2), "ssm_b_im": (1, 2)}


def _kernel_form(name, a):
    a = a[0]
    if name in TRANSPOSED:
        a = jnp.swapaxes(a, *TRANSPOSED[name])
    if name in ("ssm_b_re", "ssm_b_im", "ssm_c_re", "ssm_c_im"):
        a = a.reshape(SSM_G * SSM_P, SSM_N)
    return a


def _given_form(name, a, shape):
    if name in TRANSPOSED:
        i, j = TRANSPOSED[name]
        swapped = list(shape[1:])
        swapped[i], swapped[j] = swapped[j], swapped[i]
        return jnp.swapaxes(a.reshape(swapped), i, j).reshape(shape)
    return a.reshape(shape)


def _gathered_to_full(name, g):
    _, rows, cols = g.shape
    if name in COL_SHARDED:
        return jnp.swapaxes(g, 0, 1).reshape(rows, N_DEV * cols)
    return g.reshape(N_DEV * rows, cols)


def _full_to_owned(name, full):
    if name in COL_SHARDED:
        return jnp.swapaxes(full.reshape(full.shape[0], N_DEV, COL_SHARDED[name]), 0, 1)
    return full.reshape(N_DEV, full.shape[0] // N_DEV, full.shape[1])


def kernel(x, p, pre_norm_g, w_in, ssm_lam_re, ssm_lam_im, ssm_log_step, ssm_b_re, ssm_b_im, ssm_c_re, ssm_c_im, ssm_d, ssm_w_glu, ssm_b_glu, attn_sinks, w_out, post_norm_g, pl_w_proj, pl_w_gate, pl_b_gate, loss_target, m_pre_norm_g, m_w_in, m_ssm_lam_re, m_ssm_lam_im, m_ssm_log_step, m_ssm_b_re, m_ssm_b_im, m_ssm_c_re, m_ssm_c_im, m_ssm_d, m_ssm_w_glu, m_ssm_b_glu, m_attn_sinks, m_w_out, m_post_norm_g, m_pl_w_proj, m_pl_w_gate, m_pl_b_gate, v_pre_norm_g, v_w_in, v_ssm_lam_re, v_ssm_lam_im, v_ssm_log_step, v_ssm_b_re, v_ssm_b_im, v_ssm_c_re, v_ssm_c_im, v_ssm_d, v_ssm_w_glu, v_ssm_b_glu, v_attn_sinks, v_w_out, v_post_norm_g, v_pl_w_proj, v_pl_w_gate, v_pl_b_gate):
    w = dict(pre_norm_g=pre_norm_g, w_in=w_in, ssm_lam_re=ssm_lam_re, ssm_lam_im=ssm_lam_im, ssm_log_step=ssm_log_step,
             ssm_b_re=ssm_b_re, ssm_b_im=ssm_b_im, ssm_c_re=ssm_c_re, ssm_c_im=ssm_c_im, ssm_d=ssm_d, ssm_w_glu=ssm_w_glu,
             ssm_b_glu=ssm_b_glu, attn_sinks=attn_sinks, w_out=w_out, post_norm_g=post_norm_g, pl_w_proj=pl_w_proj,
             pl_w_gate=pl_w_gate, pl_b_gate=pl_b_gate)
    m = dict(pre_norm_g=m_pre_norm_g, w_in=m_w_in, ssm_lam_re=m_ssm_lam_re, ssm_lam_im=m_ssm_lam_im,
             ssm_log_step=m_ssm_log_step, ssm_b_re=m_ssm_b_re, ssm_b_im=m_ssm_b_im, ssm_c_re=m_ssm_c_re,
             ssm_c_im=m_ssm_c_im, ssm_d=m_ssm_d, ssm_w_glu=m_ssm_w_glu, ssm_b_glu=m_ssm_b_glu, attn_sinks=m_attn_sinks,
             w_out=m_w_out, post_norm_g=m_post_norm_g, pl_w_proj=m_pl_w_proj, pl_w_gate=m_pl_w_gate,
             pl_b_gate=m_pl_b_gate)
    v = dict(pre_norm_g=v_pre_norm_g, w_in=v_w_in, ssm_lam_re=v_ssm_lam_re, ssm_lam_im=v_ssm_lam_im,
             ssm_log_step=v_ssm_log_step, ssm_b_re=v_ssm_b_re, ssm_b_im=v_ssm_b_im, ssm_c_re=v_ssm_c_re,
             ssm_c_im=v_ssm_c_im, ssm_d=v_ssm_d, ssm_w_glu=v_ssm_w_glu, ssm_b_glu=v_ssm_b_glu, attn_sinks=v_attn_sinks,
             w_out=v_w_out, post_norm_g=v_post_norm_g, pl_w_proj=v_pl_w_proj, pl_w_gate=v_pl_w_gate,
             pl_b_gate=v_pl_b_gate)
    kf = lambda d: {n: _kernel_form(n, a) for n, a in d.items()}
    wk, mk, vk = kf(w), kf(m), kf(v)

    (gathered,), prep = _allgather_weights([wk["w_in"]], *_ssm_prep(
        wk["ssm_lam_re"], wk["ssm_lam_im"], wk["ssm_log_step"].reshape(SSM_G, 1), wk["ssm_b_re"], wk["ssm_b_im"],
        wk["ssm_c_re"], wk["ssm_c_im"], x.shape[1] // N_SEG))
    loss, grad_x, grads, g_late = _local_step(
        x, p[0], loss_target, wk["pre_norm_g"], gathered.reshape(D_IN, D_MODEL), prep, wk["ssm_lam_re"],
        wk["ssm_lam_im"], wk["ssm_log_step"], wk["ssm_b_re"], wk["ssm_b_im"], wk["ssm_d"],
        wk["ssm_b_glu"], wk["attn_sinks"], wk["post_norm_g"], wk["pl_b_gate"], [wk[n] for n in LATE_NAMES])

    owned = lambda g: g.reshape(N_DEV, D_IN // N_DEV, D_MODEL)
    tiny_form = lambda d: [d[n].reshape(rows, cols) for n, rows, cols in TINY]
    med_form = lambda d: [d[n].reshape(N_DEV, rows // N_DEV, cols) for n, rows, cols in MEDIUM]
    g_big, loss, g_tiny, g_med = _reduce_final(
        [owned(grads["w_in16"])], [grads["w_in"]], loss, tiny_form(grads), med_form(grads))
    names = BIG_NAMES + tuple(n for n, _, _ in TINY + MEDIUM)
    form = lambda d: [d[n] for n in BIG_NAMES] + tiny_form(d) + med_form(d)
    updated = _adamw_update(g_big + g_late + g_tiny + g_med, form(wk), form(mk), form(vk), len(BIG_NAMES))
    vals = dict(zip(names, updated))
    results = [[_given_form(n, vals[n][kind], w[n].shape) for n in WEIGHT_NAMES] for kind in range(4)]
    return (loss.reshape(()), grad_x, *results[0], *results[1], *results[2], *results[3])
```

```python
import functools
import math

import jax
import jax.numpy as jnp
from jax import lax
from jax.experimental import pallas as pl
from jax.experimental.pallas import tpu as pltpu

F32 = jnp.float32
BF16 = jnp.bfloat16

D_MODEL = 1024
D_SSM = 512
D_ATTN = 512
SSM_P = 16
SSM_G = 32
SSM_N = 64
N_HEADS = 8
KV_HEADS = 2
Q_PER_KV = 4
HEAD_DIM = 64
ATT_BLOCK = 128
D_PLE = 256
D_IN = 2304
EPS = 1e-6
N_DEV = 8
N_SEG = 8
G_TILE = 8
N_GT = SSM_G // G_TILE
CH_T = G_TILE * SSM_P
ST_T = G_TILE * SSM_N
N_STATE = SSM_G * SSM_N
SCAN_UNROLL = 4
TOKEN_TILE = 256
TOKEN_TILE_WIDE = 512
LANES = 128
VMEM_LIMIT = 60 * 1024 * 1024

ADAM_LR = 0.001
ADAM_B1 = 0.9
ADAM_B2 = 0.999
ADAM_EPS = 1e-08
ADAM_WD = 0.01
ADAM_STEP = 10

GELU_C = math.sqrt(2.0 / math.pi)
GELU_K = 0.044715
ATT_SCALE = 1.0 / math.sqrt(HEAD_DIM)
NEG_INF = float("-inf")


def _mm(a, b):
    return jnp.dot(a.astype(BF16), b.astype(BF16), preferred_element_type=F32)


def _mm_nt(a, b):
    return lax.dot_general(a.astype(BF16), b.astype(BF16), (((1,), (1,)), ((), ())), preferred_element_type=F32)


def _mm_tn(a, b):
    return lax.dot_general(a.astype(BF16), b.astype(BF16), (((0,), (0,)), ((), ())), preferred_element_type=F32)


def _sigmoid(x):
    return 1.0 / (1.0 + jnp.exp(-x))


def _tc_params(sem):
    return pltpu.CompilerParams(dimension_semantics=sem, vmem_limit_bytes=VMEM_LIMIT)


def _const_spec(shape):
    nd = len(shape)
    return pl.BlockSpec(shape, lambda *_: (0,) * nd)


def _mesh_pos():
    return lax.axis_index("x"), lax.axis_index("y"), lax.axis_index("c")


ROW_CHUNKS = (64, 32, 16)


def _row_chunk(nrows):
    return next((c for c in ROW_CHUNKS if nrows % c == 0), None)


def _row_chunks(nrows, fn, chunk=None, init=None):
    chunk = chunk or _row_chunk(nrows)

    def step(i, carry):
        rows = pl.ds(pl.multiple_of(i * chunk, chunk), chunk)
        if init is None:
            fn(rows)
            return carry
        return fn(rows, carry)

    return lax.fori_loop(0, nrows // chunk, step, 0 if init is None else init)


def _slot(px, py, pc):
    return 4 * px + 2 * py + pc


def _copy_owned_rows(acc_ref, own_ref):
    rows = own_ref.shape[1]
    own_ref[0] = acc_ref[pl.ds(pl.multiple_of(_slot(*_mesh_pos()) * rows, 8), rows), :]


def _allgather_weights(shards, work=None, work_inputs=(), work_out_shapes=()):
    n, n_wi, n_wo = len(shards), len(work_inputs), len(work_out_shapes)

    def body(*refs):
        srcs, w_in_refs = refs[:n], refs[n:n + n_wi]
        outs, w_out_refs = refs[n + n_wi:2 * n + n_wi], refs[2 * n + n_wi:2 * n + n_wi + n_wo]
        send_sems, recv_sems = refs[2 * n + n_wi + n_wo:]
        x, y, c = _mesh_pos()
        me, sibling = (x, y, c), (x, y, 1 - c)
        chips = [(1 - x, y), (x, 1 - y), (1 - x, 1 - y)]

        def copy(a, k, block, to):
            blk = outs[a].at[_slot(*block)]
            return pltpu.make_async_remote_copy(
                src_ref=blk, dst_ref=blk, send_sem=send_sems.at[7 * a + k], recv_sem=recv_sems.at[7 * a + k],
                device_id=to, device_id_type=pl.DeviceIdType.MESH)

        sends = []
        for a in range(n):
            mine = outs[a].at[_slot(*me)]

            def cast(r, mine=mine, src=srcs[a]):
                mine[r, :] = src[r, :].astype(BF16)

            _row_chunks(srcs[a].shape[0], cast)
            first = [copy(a, 0, me, sibling)] + [copy(a, 1 + j, me, (*chip, c)) for j, chip in enumerate(chips)]
            for cp in first:
                cp.start()
            sends += first
        if work is not None:
            work(w_in_refs, w_out_refs)
        for a in range(n):
            for j, chip in enumerate(chips):
                copy(a, 1 + j, (*chip, c), me).wait_recv()
                fwd = copy(a, 4 + j, (*chip, c), sibling)
                fwd.start()
                sends.append(fwd)
        for a in range(n):
            copy(a, 0, sibling, me).wait_recv()
            for j, chip in enumerate(chips):
                copy(a, 4 + j, (*chip, 1 - c), me).wait_recv()
        for cp in sends:
            cp.wait_send()

    vm = pl.BlockSpec(memory_space=pltpu.VMEM)
    res = pl.pallas_call(
        body, name="allgather_weights",
        out_shape=tuple(jax.ShapeDtypeStruct((N_DEV,) + s.shape, BF16) for s in shards) + tuple(work_out_shapes),
        in_specs=[vm] * (n + n_wi), out_specs=(vm,) * (n + n_wo),
        scratch_shapes=[pltpu.SemaphoreType.DMA((7 * n,)), pltpu.SemaphoreType.DMA((7 * n,))],
        compiler_params=pltpu.CompilerParams(vmem_limit_bytes=VMEM_LIMIT),
    )(*shards, *work_inputs)
    return list(res[:n]), list(res[n:])


def _adamw(w, g, m, v):
    m = ADAM_B1 * m + (1.0 - ADAM_B1) * g
    v = ADAM_B2 * v + (1.0 - ADAM_B2) * (g * g)
    m_hat = m / (1.0 - ADAM_B1 ** ADAM_STEP)
    v_hat = v / (1.0 - ADAM_B2 ** ADAM_STEP)
    delta = -ADAM_LR * (m_hat / (jnp.sqrt(v_hat) + ADAM_EPS) + ADAM_WD * w)
    return delta, m, v


def _remote(src, dst, send_sems, recv_sems, k, to):
    return pltpu.make_async_remote_copy(src_ref=src, dst_ref=dst, send_sem=send_sems.at[k], recv_sem=recv_sems.at[k],
                                        device_id=to, device_id_type=pl.DeviceIdType.MESH)


def _big_reduce_phases(g16_r, go_r, outs, send2, recv1, recv2, s_send, s_recv):
    n = len(g16_r)
    x, y, c = _mesh_pos()
    sibling = (x, y, 1 - c)
    chips = [(1 - x, y), (x, 1 - y), (1 - x, 1 - y)]
    all_chips = [(x, y)] + chips
    lvl1 = []
    for a in range(n):
        cps = [_remote(g16_r[a].at[_slot(*chip, 1 - c)], recv1[a].at[j], s_send, s_recv, 7 * a + j, sibling)
               for j, chip in enumerate(all_chips)]
        for cp in cps:
            cp.start()
        lvl1.append(cps)
    yield
    lvl2 = []
    for a in range(n):
        for cp in lvl1[a]:
            cp.wait_recv()
        og = outs[a]

        def partials(r, a=a, og=og):
            og[r, :] = go_r[a][r, :] + recv1[a][0, r, :].astype(F32)
            for j, chip in enumerate(chips):
                mine16 = g16_r[a][_slot(*chip, c), r, :].astype(F32)
                send2[a][j, r, :] = (mine16 + recv1[a][1 + j, r, :].astype(F32)).astype(BF16)

        _row_chunks(go_r[a].shape[0], partials)
        cps = [_remote(send2[a].at[j], recv2[a].at[j], s_send, s_recv, 7 * a + 4 + j, (*chip, c))
               for j, chip in enumerate(chips)]
        for cp in cps:
            cp.start()
        lvl2.append(cps)
    yield
    for a in range(n):
        for cp in lvl2[a]:
            cp.wait_recv()
        og = outs[a]

        def total(r, a=a, og=og):
            g = og[r, :]
            for j in range(3):
                g = g + recv2[a][j, r, :].astype(F32)
            og[r, :] = g

        _row_chunks(go_r[a].shape[0], total)
    yield
    for cps in lvl1 + lvl2:
        for cp in cps:
            cp.wait_send()


def _adamw_update(g, w, m, v, n_streamed):
    n = len(g)
    ns = n_streamed

    def body(*refs):
        g_r, w_r, m_r, v_r = (refs[i * n:(i + 1) * n] for i in range(4))
        outs = refs[4 * n:8 * n]
        in_buf, out_buf = refs[8 * n:8 * n + 4 * ns], refs[8 * n + 4 * ns:8 * n + 8 * ns]
        in_sems, out_sems = refs[8 * n + 8 * ns:]
        loads = [[pltpu.make_async_copy(src[a], in_buf[4 * a + k], in_sems.at[4 * a + k])
                  for k, src in enumerate((g_r, w_r, m_r, v_r))] for a in range(ns)]
        stores = [[pltpu.make_async_copy(out_buf[4 * a + k], outs[4 * a + k], out_sems.at[4 * a + k]) for k in range(4)]
                  for a in range(ns)]
        for cps in loads:
            for cp in cps:
                cp.start()
        for a in range(n):
            if a < ns:
                for cp in loads[a]:
                    cp.wait()
                gs, ws, ms, vs = in_buf[4 * a:4 * a + 4]
                og, od, om, ov = out_buf[4 * a:4 * a + 4]
            else:
                gs, ws, ms, vs = g_r[a], w_r[a], m_r[a], v_r[a]
                og, od, om, ov = outs[4 * a:4 * a + 4]

            def update(idx, gs=gs, ws=ws, ms=ms, vs=vs, og=og, od=od, om=om, ov=ov):
                gv = gs[idx]
                d, nm, nv = _adamw(ws[idx], gv, ms[idx], vs[idx])
                og[idx] = gv
                od[idx] = d
                om[idx] = nm
                ov[idx] = nv

            shape = gs.shape
            if len(shape) == 3:
                for b in range(shape[0]):
                    update(b)
            elif _row_chunk(shape[0]) is not None:
                _row_chunks(shape[0], update)
            else:
                update(Ellipsis)
            if a < ns:
                for cp in stores[a]:
                    cp.start()
        for cps in stores:
            for cp in cps:
                cp.wait()

    vm, hbm = pl.BlockSpec(memory_space=pltpu.VMEM), pl.BlockSpec(memory_space=pl.ANY)
    place = lambda: [hbm] * ns + [vm] * (n - ns)
    buf = [pltpu.VMEM(t.shape, F32) for t in g[:ns] for _ in range(4)]
    res = pl.pallas_call(
        body, name="adamw_update",
        out_shape=tuple(jax.ShapeDtypeStruct(t.shape, F32) for t in g for _ in range(4)),
        in_specs=place() * 4, out_specs=tuple(s for a in range(n) for s in [hbm if a < ns else vm] * 4),
        scratch_shapes=buf + buf + [pltpu.SemaphoreType.DMA((4 * ns,)), pltpu.SemaphoreType.DMA((4 * ns,))],
        compiler_params=pltpu.CompilerParams(vmem_limit_bytes=VMEM_LIMIT),
    )(*g, *w, *m, *v)
    return [res[4 * a:4 * a + 4] for a in range(n)]


TINY = (("pre_norm_g", 1, 1024), ("post_norm_g", 1, 1024), ("pl_b_gate", 1, 1024), ("ssm_d", 1, 512),
        ("ssm_b_glu", 1, 512), ("ssm_log_step", 1, 32), ("attn_sinks", 1, 8), ("ssm_lam_re", 32, 64),
        ("ssm_lam_im", 32, 64))
MEDIUM = (("ssm_b_re", SSM_G * SSM_P, SSM_N), ("ssm_b_im", SSM_G * SSM_P, SSM_N), ("ssm_c_re", SSM_G * SSM_P, SSM_N),
          ("ssm_c_im", SSM_G * SSM_P, SSM_N))


def _stage_rows():
    offs, r = {}, 0
    for name, rows, cols in TINY + (("loss", 1, 1),):
        if rows > 1:
            r = -(-r // 8) * 8
        offs[name] = r
        r += rows if rows > 1 else max(cols // LANES, 1)
    return offs, -(-r // 8) * 8


def _reduce_final(g16, g32, loss, g_tiny, g_med):
    nb_, nt, nm_ = len(g16), len(TINY), len(MEDIUM)
    offs, stage_rows = _stage_rows()

    def body(*refs):
        g16_r, go_r = refs[:nb_], refs[nb_:2 * nb_]
        base = 2 * nb_
        loss_r, gt, gm = refs[base], refs[base + 1:base + 1 + nt], refs[base + 1 + nt:base + 1 + nt + nm_]
        base += 1 + nt + nm_
        out_b = refs[base:base + nb_]
        base += nb_
        loss_o, out_t, out_m = refs[base], refs[base + 1:base + 1 + nt], refs[base + 1 + nt:base + 1 + nt + nm_]
        base += 1 + nt + nm_
        send2_b, recv1_b, recv2_b = (refs[base + i * nb_:base + (i + 1) * nb_] for i in range(3))
        base += 3 * nb_
        stage = refs[base]
        recv1, part, recv2 = (refs[base + 1 + i * nm_:base + 1 + (i + 1) * nm_] for i in range(3))
        bs_send, bs_recv, s_send, s_recv, own_sems = refs[base + 1 + 3 * nm_:base + 6 + 3 * nm_]
        own32 = refs[base + 6 + 3 * nm_:]
        fetch = [pltpu.make_async_copy(go_r[a].at[0], own32[a], own_sems.at[a]) for a in range(nb_)]
        for cp in fetch:
            cp.start()
        big = _big_reduce_phases(g16_r, own32, out_b, send2_b, recv1_b, recv2_b, bs_send, bs_recv)
        small = small_phases(loss_r, gt, gm, loss_o, out_t, out_m, stage, recv1, part, recv2, s_send, s_recv)
        next(big)
        next(small)
        for cp in fetch:
            cp.wait()
        next(big)
        for _ in small:
            pass
        for _ in big:
            pass

    def small_phases(loss_r, gt, gm, loss_o, out_t, out_m, stage, recv1, part, recv2, s_send, s_recv):
        x, y, c = _mesh_pos()
        me = _slot(x, y, c)
        sibling = (x, y, 1 - c)
        chips = [(1 - x, y), (x, 1 - y), (1 - x, 1 - y)]
        all_chips = [(x, y)] + chips
        peers = [sibling] + [(*chip, c) for chip in chips] + [(*chip, 1 - c) for chip in chips]
        sem = iter(range(7 + 14 * nm_))
        lvl1 = []
        for a in range(nm_):
            cps = [_remote(gm[a].at[_slot(*chip, 1 - c)], recv1[a].at[j], s_send, s_recv, next(sem), sibling)
                   for j, chip in enumerate(all_chips)]
            for cp in cps:
                cp.start()
            lvl1.append(cps)
        mine = stage.at[me]
        mine[...] = jnp.zeros((stage_rows, LANES), F32)
        for (name, rows, cols), ref in zip(TINY + (("loss", 1, 1),), gt + (loss_r,)):
            r0 = offs[name]
            if rows > 1:
                mine[r0:r0 + rows, 0:cols] = ref[...]
            elif cols >= LANES:
                for i in range(cols // LANES):
                    mine[r0 + i:r0 + i + 1, :] = ref[:, i * LANES:(i + 1) * LANES]
            else:
                mine[r0:r0 + 1, 0:cols] = ref[...]
        tiny_cps = [_remote(mine, mine, s_send, s_recv, next(sem), peer) for peer in peers]
        for cp in tiny_cps:
            cp.start()
        yield
        lvl2 = []
        for a in range(nm_):
            for cp in lvl1[a]:
                cp.wait_recv()
            for j, chip in enumerate(all_chips):
                part[a][j] = gm[a][_slot(*chip, c)] + recv1[a][j]
            cps = [_remote(part[a].at[1 + j], recv2[a].at[j], s_send, s_recv, next(sem), (*chip, c))
                   for j, chip in enumerate(chips)]
            for cp in cps:
                cp.start()
            lvl2.append(cps)
        yield
        lvl3 = []
        for a in range(nm_):
            for cp in lvl2[a]:
                cp.wait_recv()
            blk = out_m[a].at[me]
            blk[...] = ((part[a][0] + recv2[a][0]) + recv2[a][1]) + recv2[a][2]
            cps = [_remote(blk, blk, s_send, s_recv, next(sem), peer) for peer in peers]
            for cp in cps:
                cp.start()
            lvl3.append(cps)
        yield
        for cp in tiny_cps:
            cp.wait_recv()
        tot = stage[0]
        for d in range(1, N_DEV):
            tot = tot + stage[d]
        loss_o[...] = tot[offs["loss"]:offs["loss"] + 1, 0:1]
        for k, (name, rows, cols) in enumerate(TINY):
            r0 = offs[name]
            if rows > 1:
                out_t[k][...] = tot[r0:r0 + rows, 0:cols]
            elif cols >= LANES:
                for i in range(cols // LANES):
                    out_t[k][:, i * LANES:(i + 1) * LANES] = tot[r0 + i:r0 + i + 1, :]
            else:
                out_t[k][...] = tot[r0:r0 + 1, 0:cols]
        for cps in lvl3:
            for cp in cps:
                cp.wait_recv()
        for cps in lvl1 + lvl2 + lvl3 + [tiny_cps]:
            for cp in cps:
                cp.wait_send()

    vmem = pl.BlockSpec(memory_space=pltpu.VMEM)
    t_shapes = [jax.ShapeDtypeStruct((rows, cols), F32) for _, rows, cols in TINY]
    m_shapes = [jax.ShapeDtypeStruct((N_DEV, rows // N_DEV, cols), F32) for _, rows, cols in MEDIUM]
    blk = [(rows // N_DEV, cols) for _, rows, cols in MEDIUM]
    shard = [g.shape[1:] for g in g16]
    scratch = ([pltpu.VMEM((3,) + s, BF16) for s in shard] + [pltpu.VMEM((4,) + s, BF16) for s in shard]
               + [pltpu.VMEM((3,) + s, BF16) for s in shard]
               + [pltpu.VMEM((N_DEV, stage_rows, LANES), F32)]
               + [pltpu.VMEM((4,) + b, F32) for b in blk] + [pltpu.VMEM((4,) + b, F32) for b in blk]
               + [pltpu.VMEM((3,) + b, F32) for b in blk]
               + [pltpu.SemaphoreType.DMA((7 * nb_,)), pltpu.SemaphoreType.DMA((7 * nb_,)),
                  pltpu.SemaphoreType.DMA((7 + 14 * nm_,)), pltpu.SemaphoreType.DMA((7 + 14 * nm_,)),
                  pltpu.SemaphoreType.DMA((nb_,))]
               + [pltpu.VMEM(s, F32) for s in shard])
    n_out = nb_ + 1 + nt + nm_
    res = pl.pallas_call(
        body, name="reduce_final",
        out_shape=tuple(jax.ShapeDtypeStruct(s, F32) for s in shard) + (jax.ShapeDtypeStruct((1, 1), F32),)
        + tuple(t_shapes) + tuple(m_shapes),
        in_specs=[vmem] * nb_ + [pl.BlockSpec(memory_space=pl.ANY)] * nb_ + [vmem] * (1 + nt + nm_),
        out_specs=(vmem,) * n_out, scratch_shapes=scratch,
        compiler_params=pltpu.CompilerParams(vmem_limit_bytes=VMEM_LIMIT),
    )(*g16, *g32, loss, *g_tiny, *g_med)
    return list(res[:nb_]), res[nb_], list(res[nb_ + 1:nb_ + 1 + nt]), list(res[nb_ + 1 + nt:])


def _gather_phases(shard_r, gath, cast, send_sems, recv_sems, local_sems):
    n = len(shard_r)
    x, y, c = _mesh_pos()
    me, sibling = (x, y, c), (x, y, 1 - c)
    chips = [(1 - x, y), (x, 1 - y), (1 - x, 1 - y)]

    def own(a, k, to):
        return _remote(cast[a], gath[a].at[_slot(*me)], send_sems, recv_sems, 7 * a + k, to)

    def passed(a, k, block, to):
        blk = gath[a].at[_slot(*block)]
        return _remote(blk, blk, send_sems, recv_sems, 7 * a + k, to)

    def keep(a):
        return pltpu.make_async_copy(cast[a], gath[a].at[_slot(*me)], local_sems.at[a])

    def start():
        for a in range(n):
            def to16(r, a=a):
                cast[a][r, :] = shard_r[a][r, :].astype(BF16)

            _row_chunks(shard_r[a].shape[0], to16)
            keep(a).start()
            own(a, 0, sibling).start()
            for j, chip in enumerate(chips):
                own(a, 1 + j, (*chip, c)).start()

    def relay():
        for a in range(n):
            for j, chip in enumerate(chips):
                passed(a, 1 + j, (*chip, c), me).wait_recv()
                passed(a, 4 + j, (*chip, c), sibling).start()

    def finish():
        for a in range(n):
            passed(a, 0, sibling, me).wait_recv()
            for j, chip in enumerate(chips):
                passed(a, 4 + j, (*chip, 1 - c), me).wait_recv()
            own(a, 0, sibling).wait_send()
            for j, chip in enumerate(chips):
                own(a, 1 + j, (*chip, c)).wait_send()
                passed(a, 4 + j, (*chip, c), sibling).wait_send()
            keep(a).wait()

    return start, relay, finish


def _gather_operands(shards):
    n = len(shards)
    return ((pl.BlockSpec(memory_space=pl.ANY),) * n,
            tuple(jax.ShapeDtypeStruct((N_DEV,) + s.shape, BF16) for s in shards),
            [pltpu.VMEM(s.shape, BF16) for s in shards]
            + [pltpu.SemaphoreType.DMA((7 * n,)), pltpu.SemaphoreType.DMA((7 * n,)), pltpu.SemaphoreType.DMA((n,))])


def _hosted_reduce_phases(g16_r, g32_r, red, own16, recv1, send2, recv2, own32, s_send, s_recv, s_local):
    n = len(g16_r)
    x, y, c = _mesh_pos()
    sibling = (x, y, 1 - c)
    chips = [(1 - x, y), (x, 1 - y), (1 - x, 1 - y)]
    all_chips = [(x, y)] + chips

    def lvl1(a, j):
        return _remote(g16_r[a].at[_slot(*all_chips[j], 1 - c)], recv1[a].at[j], s_send, s_recv, 7 * a + j, sibling)

    def lvl2(a, j):
        return _remote(send2[a].at[j], recv2[a].at[j], s_send, s_recv, 7 * a + 4 + j, (*chips[j], c))

    def mine(a, j):
        if j == 3:
            only_mine = g32_r[a].shape[0] == 1
            return pltpu.make_async_copy(g32_r[a].at[0 if only_mine else _slot(x, y, c)], own32[a], s_local.at[4 * a + j])
        return pltpu.make_async_copy(g16_r[a].at[_slot(*chips[j], c)], own16[a].at[j], s_local.at[4 * a + j])

    def start():
        for a in range(n):
            for j in range(4):
                mine(a, j).start()
            for j in range(4):
                lvl1(a, j).start()

    def middle():
        for a in range(n):
            for j in range(4):
                mine(a, j).wait()
            for j in range(4):
                lvl1(a, j).wait_recv()

            def partials(r, a=a):
                red[a][r, :] = own32[a][r, :] + recv1[a][0, r, :].astype(F32)
                for j in range(3):
                    send2[a][j, r, :] = (own16[a][j, r, :].astype(F32) + recv1[a][1 + j, r, :].astype(F32)).astype(BF16)

            _row_chunks(own32[a].shape[0], partials)
            for j in range(3):
                lvl2(a, j).start()

    def total():
        for a in range(n):
            for j in range(3):
                lvl2(a, j).wait_recv()

            def add(r, a=a):
                g = red[a][r, :]
                for j in range(3):
                    g = g + recv2[a][j, r, :].astype(F32)
                red[a][r, :] = g

            _row_chunks(own32[a].shape[0], add)

    def finish():
        for a in range(n):
            for j in range(4):
                lvl1(a, j).wait_send()
            for j in range(3):
                lvl2(a, j).wait_send()

    return start, middle, total, finish


def _hosted_reduce_operands(g16, const_spec):
    n = len(g16)
    shard = [g.shape[1:] for g in g16]
    return ([pl.BlockSpec(memory_space=pl.ANY)] * (2 * n),
            tuple(const_spec(s) for s in shard),
            tuple(jax.ShapeDtypeStruct(s, F32) for s in shard),
            [pltpu.VMEM((3,) + s, BF16) for s in shard] + [pltpu.VMEM((4,) + s, BF16) for s in shard]
            + [pltpu.VMEM((3,) + s, BF16) for s in shard] + [pltpu.VMEM((3,) + s, BF16) for s in shard]
            + [pltpu.VMEM(s, F32) for s in shard]
            + [pltpu.SemaphoreType.DMA((7 * n,)), pltpu.SemaphoreType.DMA((7 * n,)), pltpu.SemaphoreType.DMA((4 * n,))])


def _in_proj(x2, g_pre, w_in, tm):
    t = x2.shape[0]

    def body(x_ref, g_ref, w_ref, u_ref, zs_ref, q_ref, k_ref, v_ref, za_ref):
        xv = x_ref[...]
        r = lax.rsqrt(jnp.mean(xv * xv, axis=-1, keepdims=True) + EPS)
        hn = xv * r * g_ref[...]
        proj = _mm_nt(hn, w_ref[...])
        u_ref[...] = proj[:, 0:512]
        zs_ref[...] = proj[:, 512:1024]
        q_ref[...] = proj[:, 1024:1536].astype(BF16)
        k_ref[...] = proj[:, 1536:1664].astype(BF16)
        v_ref[...] = proj[:, 1664:1792].astype(BF16)
        za_ref[...] = proj[:, 1792:2304]

    row = lambda w: pl.BlockSpec((tm, w), lambda i: (i, 0))
    return pl.pallas_call(
        body, name="in_proj", grid=(t // tm,),
        in_specs=[row(D_MODEL), _const_spec((1, D_MODEL)), _const_spec((D_IN, D_MODEL))],
        out_specs=(row(512), row(512), row(512), row(128), row(128), row(512)),
        out_shape=(jax.ShapeDtypeStruct((t, 512), F32),
                   jax.ShapeDtypeStruct((t, 512), F32), jax.ShapeDtypeStruct((t, 512), BF16),
                   jax.ShapeDtypeStruct((t, 128), BF16), jax.ShapeDtypeStruct((t, 128), BF16),
                   jax.ShapeDtypeStruct((t, 512), F32)),
        compiler_params=_tc_params(("arbitrary",)),
    )(x2, g_pre, w_in)


def _discretise(lr, li, ls):
    step = jnp.exp(ls)
    mag = jnp.exp(lr * step)
    ar = mag * jnp.cos(li * step)
    ai = mag * jnp.sin(li * step)
    den = lr * lr + li * li
    cr = ((ar - 1.0) * lr + ai * li) / den
    ci = (ai * lr - (ar - 1.0) * li) / den
    return step, ar, ai, den, cr, ci


def _per_channel(v):
    return jnp.broadcast_to(v[:, None, :], (SSM_G, SSM_P, SSM_N)).reshape(SSM_G * SSM_P, SSM_N)


def _tile_masks():
    r = lax.broadcasted_iota(jnp.int32, (CH_T, ST_T), 0) // SSM_P
    l = lax.broadcasted_iota(jnp.int32, (CH_T, ST_T), 1) // SSM_N
    lt = lax.broadcasted_iota(jnp.int32, (ST_T, CH_T), 0) // SSM_N
    rt = lax.broadcasted_iota(jnp.int32, (ST_T, CH_T), 1) // SSM_P
    rep = lax.broadcasted_iota(jnp.int32, (SSM_N, ST_T), 0) == lax.broadcasted_iota(jnp.int32, (SSM_N, ST_T), 1) % SSM_N
    rep_t = lax.broadcasted_iota(jnp.int32, (ST_T, SSM_N), 0) % SSM_N == lax.broadcasted_iota(jnp.int32, (ST_T, SSM_N), 1)
    return r == l, lt == rt, rep, rep_t


def _ssm_prep(lam_re, lam_im, log_step, b_re, b_im, c_re, c_im, seg):
    def work(in_refs, out_refs):
        lr_ref, li_ref, ls_ref, br_ref, bi_ref, cre_ref, cim_ref, lrr_ref, lir_ref, lsr_ref = in_refs
        ar_ref, ai_ref, pr_ref, pi_ref, bcat_ref, bcat_t_ref, ccat_ref, ccat_t_ref = out_refs
        _, _, _, _, cr, ci = _discretise(lr_ref[...], li_ref[...], ls_ref[...])
        cr, ci = _per_channel(cr), _per_channel(ci)
        br, bi = br_ref[...], bi_ref[...]
        bb_re = cr * br - ci * bi
        bb_im = cr * bi + ci * br
        same, same_t, rep, rep_t = _tile_masks()
        rep, rep_t = rep.astype(BF16), rep_t.astype(BF16)
        for j in range(N_GT):
            rows = slice(j * CH_T, (j + 1) * CH_T)
            for wide, tall, parts in ((bcat_ref, bcat_t_ref, (bb_re[rows], bb_im[rows])),
                                      (ccat_t_ref, ccat_ref, (cre_ref[rows, :], -cim_ref[rows, :]))):
                for k, part in enumerate(parts):
                    p16 = part.astype(BF16)
                    wide[j, :, k * ST_T:(k + 1) * ST_T] = jnp.where(same, _mm(p16, rep), 0.0).astype(BF16)
                    tall[j, k * ST_T:(k + 1) * ST_T, :] = jnp.where(same_t, _mm_nt(rep_t, p16), 0.0).astype(BF16)
        stepr = jnp.exp(lsr_ref[...])
        mag = jnp.exp(lrr_ref[...] * stepr)
        a_r, a_i = mag * jnp.cos(lir_ref[...] * stepr), mag * jnp.sin(lir_ref[...] * stepr)
        p_r, p_i = a_r, a_i
        for k in range(8):
            pr_ref[k:k + 1, :] = p_r
            pi_ref[k:k + 1, :] = p_i
            p_r, p_i = p_r * a_r - p_i * a_i, p_r * a_i + p_i * a_r
        n = 8
        while n < seg:
            tr, ti = pr_ref[n - 1:n, :], pi_ref[n - 1:n, :]
            xr, xi = pr_ref[0:n, :], pi_ref[0:n, :]
            pr_ref[n:2 * n, :] = xr * tr - xi * ti
            pi_ref[n:2 * n, :] = xr * ti + xi * tr
            n *= 2
        ar_ref[...] = pr_ref[0:1, :]
        ai_ref[...] = pi_ref[0:1, :]

    row = jax.ShapeDtypeStruct((1, N_STATE), F32)
    pw = jax.ShapeDtypeStruct((seg, N_STATE), F32)
    wide = jax.ShapeDtypeStruct((N_GT, CH_T, 2 * ST_T), BF16)
    tall = jax.ShapeDtypeStruct((N_GT, 2 * ST_T, CH_T), BF16)
    step_row = jnp.broadcast_to(log_step, (SSM_G, SSM_N)).reshape(1, N_STATE)
    inputs = (lam_re, lam_im, log_step, b_re, b_im, c_re, c_im, lam_re.reshape(1, N_STATE),
              lam_im.reshape(1, N_STATE), step_row)
    return work, inputs, (row, row, pw, pw, wide, tall, tall, wide)


def _seg_rows(t):
    if isinstance(t, int):
        return pl.ds(t * N_SEG, N_SEG)
    return pl.ds(pl.multiple_of(t * N_SEG, N_SEG), N_SEG)


def _scan_forward(xs, a_re, a_im, pw_re, pw_im, cs, seg):
    are = jnp.broadcast_to(a_re, (N_SEG, ST_T))
    aim = jnp.broadcast_to(a_im, (N_SEG, ST_T))

    def steps(k, carry):
        xr, xi = carry
        for j in range(SCAN_UNROLL):
            r = pl.multiple_of((k * SCAN_UNROLL + j) * N_SEG, N_SEG)
            nr = are * xr - aim * xi + xs[pl.ds(r, N_SEG), 0:ST_T]
            ni = are * xi + aim * xr + xs[pl.ds(r, N_SEG), ST_T:2 * ST_T]
            xs[pl.ds(r, N_SEG), 0:ST_T] = nr
            xs[pl.ds(r, N_SEG), ST_T:2 * ST_T] = ni
            xr, xi = nr, ni
        return xr, xi

    zero = jnp.zeros((N_SEG, ST_T), F32)
    fr, fi = lax.fori_loop(0, seg // SCAN_UNROLL, steps, (zero, zero))
    sr, si = pw_re[seg - 1:seg, :], pw_im[seg - 1:seg, :]
    cr = jnp.zeros((1, ST_T), F32)
    ci = jnp.zeros((1, ST_T), F32)
    cs[0:1, :] = cr
    cs[8:9, :] = ci
    for s in range(1, N_SEG):
        ncr = sr * cr - si * ci + fr[s - 1:s, :]
        nci = sr * ci + si * cr + fi[s - 1:s, :]
        cr, ci = ncr, nci
        cs[s:s + 1, :] = cr
        cs[8 + s:9 + s, :] = ci
    car, cai = cs[0:8, :], cs[8:16, :]

    def fix(t, _):
        r = pl.multiple_of(t * N_SEG, N_SEG)
        pr, pi = pw_re[pl.ds(t, 1), :], pw_im[pl.ds(t, 1), :]
        xs[pl.ds(r, N_SEG), 0:ST_T] = xs[pl.ds(r, N_SEG), 0:ST_T] + (pr * car - pi * cai)
        xs[pl.ds(r, N_SEG), ST_T:2 * ST_T] = xs[pl.ds(r, N_SEG), ST_T:2 * ST_T] + (pr * cai + pi * car)
        return 0

    lax.fori_loop(0, seg, fix, 0, unroll=SCAN_UNROLL)


def _interleave(src, dst, seg):
    for s in range(N_SEG):
        dst[pl.ds(s, seg, stride=N_SEG), :] = src[s]


def _deinterleave(src, seg, s):
    return src[pl.ds(s, seg, stride=N_SEG), :]


def _ssm_forward(u, bcat, ccat, a_re, a_im, pw_re, pw_im, d_row, late, seg):
    bl = u.shape[0]
    rows = N_SEG * seg
    n = len(late)
    steps = bl * N_GT

    def body(*refs):
        u_ref, b_ref, c_ref, ar_ref, ai_ref, pr_ref, pi_ref, d_ref = refs[:8]
        late_r = refs[8:8 + n]
        y_ref, xs_ref, cs_ref = refs[8 + n:11 + n]
        gath, cast = refs[11 + n:11 + 2 * n], refs[11 + 2 * n:11 + 3 * n]
        send_sems, recv_sems, local_sems, ui, yi = refs[11 + 3 * n:]
        step = pl.program_id(0) * N_GT + pl.program_id(1)
        start, relay, finish = _gather_phases(late_r, gath, cast, send_sems, recv_sems, local_sems)
        pl.when(step == 0)(start)
        _interleave(u_ref.at[0], ui, seg)
        u = ui[...]
        xs, cs = xs_ref.at[0, 0], cs_ref.at[0, 0]
        xs[:, 0:ST_T] = _mm(u, b_ref[0, :, 0:ST_T])
        xs[:, ST_T:] = _mm(u, b_ref[0, :, ST_T:])
        _scan_forward(xs, ar_ref[...], ai_ref[...], pr_ref, pi_ref, cs, seg)
        yi[...] = _mm(xs[:, 0:ST_T], c_ref[0, 0:ST_T, :]) + _mm(xs[:, ST_T:], c_ref[0, ST_T:, :]) + d_ref[...] * u
        for s in range(N_SEG):
            y_ref[0, s] = _deinterleave(yi, seg, s)
        pl.when(step == steps // 2)(relay)
        pl.when(step == steps - 1)(finish)

    state = lambda r, c: pl.BlockSpec((1, 1, r, c), lambda b, j: (b, j, 0, 0))
    act = pl.BlockSpec((1, N_SEG, seg, CH_T), lambda b, j: (b, 0, 0, j))
    g_specs, g_shapes, g_scratch = _gather_operands(late)
    res = pl.pallas_call(
        body, name="ssm_forward", grid=(bl, N_GT),
        in_specs=[act,
                  pl.BlockSpec((1, CH_T, 2 * ST_T), lambda b, j: (j, 0, 0)),
                  pl.BlockSpec((1, 2 * ST_T, CH_T), lambda b, j: (j, 0, 0)),
                  pl.BlockSpec((1, ST_T), lambda b, j: (0, j)), pl.BlockSpec((1, ST_T), lambda b, j: (0, j)),
                  pl.BlockSpec((seg, ST_T), lambda b, j: (0, j)), pl.BlockSpec((seg, ST_T), lambda b, j: (0, j)),
                  pl.BlockSpec((1, CH_T), lambda b, j: (0, j))]
        + [pl.BlockSpec(s.shape, lambda b, j: (0, 0)) for s in late],
        out_specs=(act, state(rows, 2 * ST_T), state(16, ST_T)) + g_specs,
        out_shape=(jax.ShapeDtypeStruct((bl, N_SEG, seg, D_SSM), F32),
                   jax.ShapeDtypeStruct((bl, N_GT, rows, 2 * ST_T), F32),
                   jax.ShapeDtypeStruct((bl, N_GT, 16, ST_T), F32)) + g_shapes,
        scratch_shapes=g_scratch + [pltpu.VMEM((rows, CH_T), F32), pltpu.VMEM((rows, CH_T), F32)],
        compiler_params=_tc_params(("arbitrary", "arbitrary")),
    )(u, bcat, ccat, a_re, a_im, pw_re, pw_im, d_row, *late)
    return res[:3], list(res[3:])


def _ssm_backward(u, dy, states, carries, bcat_t, ccat_t, a_re, a_im, pw_re, pw_im, d_row, late16, late32, seg):
    bl = u.shape[0]
    rows = N_SEG * seg
    n = len(late16)
    grid_steps = N_GT * bl

    def body(*refs):
        u_ref, dy_ref, xs_ref, cs_ref, bt_ref, ct_ref, ar_ref, ai_ref, pr_ref, pi_ref, d_ref = refs[:11]
        g16_r, g32_r = refs[11:11 + n], refs[11 + n:11 + 2 * n]
        du_ref, db_ref, dc_ref, dar_ref, dai_ref, dd_ref = refs[11 + 2 * n:17 + 2 * n]
        red = refs[17 + 2 * n:17 + 3 * n]
        own16, recv1, send2, recv2, own32 = (refs[17 + 3 * n + k * n:17 + 3 * n + (k + 1) * n] for k in range(5))
        s_send, s_recv, s_local, ls, cl, ui, dyi, dui = refs[17 + 8 * n:]
        b = pl.program_id(1)
        step = pl.program_id(0) * bl + b
        start, middle, total, finish = _hosted_reduce_phases(g16_r, g32_r, red, own16, recv1, send2, recv2, own32,
                                                             s_send, s_recv, s_local)
        pl.when(step == 0)(start)
        pl.when(step == grid_steps // 4)(middle)
        pl.when(step == (grid_steps * 3) // 4)(total)
        pl.when(step == grid_steps - 1)(finish)

        @pl.when(b == 0)
        def _():
            for ref in (db_ref, dc_ref, dar_ref, dai_ref, dd_ref):
                ref[...] = jnp.zeros(ref.shape, F32)

        _interleave(u_ref.at[0], ui, seg)
        _interleave(dy_ref.at[0], dyi, seg)
        u = ui[...]
        dy = dyi[...]
        xs, cs = xs_ref.at[0, 0], cs_ref.at[0, 0]
        ls[...] = _mm(dy, ct_ref[0])
        are = jnp.broadcast_to(ar_ref[...], (N_SEG, ST_T))
        aim = jnp.broadcast_to(ai_ref[...], (N_SEG, ST_T))

        def steps(k, carry):
            lr, li = carry
            for j in range(SCAN_UNROLL):
                r = pl.multiple_of((seg - 1 - (k * SCAN_UNROLL + j)) * N_SEG, N_SEG)
                nr = are * lr + aim * li + ls[pl.ds(r, N_SEG), 0:ST_T]
                ni = are * li - aim * lr + ls[pl.ds(r, N_SEG), ST_T:2 * ST_T]
                ls[pl.ds(r, N_SEG), 0:ST_T] = nr
                ls[pl.ds(r, N_SEG), ST_T:2 * ST_T] = ni
                lr, li = nr, ni
            return lr, li

        zero = jnp.zeros((N_SEG, ST_T), F32)
        fr, fi = lax.fori_loop(0, seg // SCAN_UNROLL, steps, (zero, zero))
        sr, si = pr_ref[seg - 1:seg, :], pi_ref[seg - 1:seg, :]
        cr = jnp.zeros((1, ST_T), F32)
        ci = jnp.zeros((1, ST_T), F32)
        cl[7:8, :] = cr
        cl[15:16, :] = ci
        for s in range(N_SEG - 2, -1, -1):
            ncr = sr * cr + si * ci + fr[s + 1:s + 2, :]
            nci = sr * ci - si * cr + fi[s + 1:s + 2, :]
            cr, ci = ncr, nci
            cl[s:s + 1, :] = cr
            cl[8 + s:9 + s, :] = ci
        clr, cli = cl[0:8, :], cl[8:16, :]

        def fix_rows(rows, t, xpr, xpi, acc):
            dr, di = acc
            pr, pi = pr_ref[pl.ds(seg - 1 - t, 1), :], pi_ref[pl.ds(seg - 1 - t, 1), :]
            lr = ls[rows, 0:ST_T] + (pr * clr + pi * cli)
            li = ls[rows, ST_T:2 * ST_T] + (pr * cli - pi * clr)
            ls[rows, 0:ST_T] = lr
            ls[rows, ST_T:2 * ST_T] = li
            return dr + (lr * xpr + li * xpi), di + (li * xpr - lr * xpi)

        def fix_at(t, acc):
            prev = _seg_rows(t - 1)
            return fix_rows(_seg_rows(t), t, xs[prev, 0:ST_T], xs[prev, ST_T:2 * ST_T], acc)

        def fix(k, acc):
            for j in range(SCAN_UNROLL):
                acc = fix_at(k * SCAN_UNROLL + j, acc)
            return acc

        acc = fix_rows(pl.ds(0, N_SEG), 0, cs[0:8, :], cs[8:16, :], (zero, zero))
        for t in range(1, SCAN_UNROLL):
            acc = fix_at(t, acc)
        dr, di = lax.fori_loop(1, seg // SCAN_UNROLL, fix, acc)
        dar = jnp.sum(dr, axis=0, keepdims=True)
        dai = jnp.sum(di, axis=0, keepdims=True)
        lall = ls[...]
        dui[...] = _mm(lall, bt_ref[0]) + d_ref[...] * dy
        for s in range(N_SEG):
            du_ref[0, s] = _deinterleave(dui, seg, s).astype(BF16)
        db_ref[0] += _mm_tn(u, lall)
        dc_ref[0] += _mm_tn(dy, xs[...])
        dar_ref[...] += dar
        dai_ref[...] += dai
        dd_ref[...] += jnp.sum(dy * u, axis=0, keepdims=True)

    tile3 = lambda r, c: pl.BlockSpec((1, r, c), lambda j, b: (j, 0, 0))
    lane = lambda r, c: pl.BlockSpec((r, c), lambda j, b: (0, j))
    act = pl.BlockSpec((1, N_SEG, seg, CH_T), lambda j, b: (b, 0, 0, j))
    state = lambda r, c: pl.BlockSpec((1, 1, r, c), lambda j, b: (b, j, 0, 0))
    r_in, r_out, r_shapes, r_scratch = _hosted_reduce_operands(late16, lambda s: pl.BlockSpec(s, lambda j, b: (0, 0)))
    res = pl.pallas_call(
        body, name="ssm_backward", grid=(N_GT, bl),
        in_specs=[act, act, state(rows, 2 * ST_T), state(16, ST_T), tile3(2 * ST_T, CH_T), tile3(CH_T, 2 * ST_T),
                  lane(1, ST_T), lane(1, ST_T), lane(seg, ST_T), lane(seg, ST_T), lane(1, CH_T)] + r_in,
        out_specs=(act, tile3(CH_T, 2 * ST_T), tile3(CH_T, 2 * ST_T), lane(1, ST_T), lane(1, ST_T), lane(1, CH_T))
        + r_out,
        out_shape=(jax.ShapeDtypeStruct((bl, N_SEG, seg, D_SSM), BF16),
                   jax.ShapeDtypeStruct((N_GT, CH_T, 2 * ST_T), F32), jax.ShapeDtypeStruct((N_GT, CH_T, 2 * ST_T), F32),
                   jax.ShapeDtypeStruct((1, N_STATE), F32), jax.ShapeDtypeStruct((1, N_STATE), F32),
                   jax.ShapeDtypeStruct((1, D_SSM), F32)) + r_shapes,
        scratch_shapes=r_scratch + [pltpu.VMEM((rows, 2 * ST_T), F32), pltpu.VMEM((16, ST_T), F32)]
        + [pltpu.VMEM((rows, CH_T), F32)] * 3,
        compiler_params=_tc_params(("arbitrary", "arbitrary")),
    )(u, dy, states, carries, bcat_t, ccat_t, a_re, a_im, pw_re, pw_im, d_row, *late16, *late32)
    return res[:6], list(res[6:])


def _ssm_param_grads(lam_re, lam_im, log_step, b_re, b_im, da_re, da_im, d_bcat, d_ccat_t):
    def body(lr_ref, li_ref, ls_ref, br_ref, bi_ref, gar_ref, gai_ref, gbcat_ref, gccat_ref,
             dlr_ref, dli_ref, dls_ref, dbr_ref, dbi_ref, dcr_ref, dci_ref, gbr_s, gbi_s):
        same, _, _, rep_t = _tile_masks()
        rep_t = rep_t.astype(F32)
        for j in range(N_GT):
            rows = slice(j * CH_T, (j + 1) * CH_T)
            for src, dsts in ((gbcat_ref, (gbr_s, gbi_s)), (gccat_ref, (dcr_ref, dci_ref))):
                for k, dst in enumerate(dsts):
                    blk = jnp.where(same, src[j, :, k * ST_T:(k + 1) * ST_T], 0.0)
                    dst[rows, :] = jnp.dot(blk, rep_t, precision=lax.Precision.HIGHEST, preferred_element_type=F32)
        dci_ref[...] = -dci_ref[...]
        lr, li = lr_ref[...], li_ref[...]
        step, ar, ai, den, cr, ci = _discretise(lr, li, ls_ref[...])
        crb, cib = _per_channel(cr), _per_channel(ci)
        br, bi = br_ref[...], bi_ref[...]
        gbr, gbi = gbr_s[...], gbi_s[...]
        dbr_ref[...] = crb * gbr + cib * gbi
        dbi_ref[...] = crb * gbi - cib * gbr
        over_channels = lambda t: jnp.sum(t.reshape(SSM_G, SSM_P, SSM_N), axis=1)
        gcr = over_channels(br * gbr + bi * gbi)
        gci = over_channels(br * gbi - bi * gbr)
        ilr, ili = lr / den, -li / den
        gar = gar_ref[...] + (ilr * gcr + ili * gci)
        gai = gai_ref[...] + (ilr * gci - ili * gcr)
        qr, qi = cr * ilr - ci * ili, cr * ili + ci * ilr
        glr = -(qr * gcr + qi * gci)
        gli = -(qr * gci - qi * gcr)
        gwr = ar * gar + ai * gai
        gwi = ar * gai - ai * gar
        dlr_ref[...] = glr + step * gwr
        dli_ref[...] = gli + step * gwi
        dls_ref[...] = jnp.sum(lr * gwr + li * gwi, axis=-1, keepdims=True) * step

    lam = jax.ShapeDtypeStruct((SSM_G, SSM_N), F32)
    mat = jax.ShapeDtypeStruct((SSM_G * SSM_P, SSM_N), F32)
    vm = pl.BlockSpec(memory_space=pltpu.VMEM)
    return pl.pallas_call(
        body, name="ssm_param_grads", out_shape=(lam, lam, jax.ShapeDtypeStruct((SSM_G, 1), F32), mat, mat, mat, mat),
        in_specs=[vm] * 9, out_specs=(vm,) * 7,
        scratch_shapes=[pltpu.VMEM((SSM_G * SSM_P, SSM_N), F32), pltpu.VMEM((SSM_G * SSM_P, SSM_N), F32)],
    )(lam_re, lam_im, log_step, b_re, b_im, da_re, da_im, d_bcat, d_ccat_t)


ROWS4 = Q_PER_KV * ATT_BLOCK
ATT_FWD_STACK = 1


def _att_dist_mask(first_block):
    qi = lax.broadcasted_iota(jnp.int32, (ROWS4, 2 * ATT_BLOCK), 0) & (ATT_BLOCK - 1)
    si = lax.broadcasted_iota(jnp.int32, (ROWS4, 2 * ATT_BLOCK), 1)
    dist = qi + ATT_BLOCK - si
    valid = (dist >= 0) & (dist < ATT_BLOCK) & ((si >= ATT_BLOCK) | jnp.logical_not(first_block))
    return dist.astype(F32), valid


def _stack_heads(x, kv):
    return jnp.concatenate([x[:, (kv * Q_PER_KV + g) * HEAD_DIM:(kv * Q_PER_KV + g + 1) * HEAD_DIM]
                            for g in range(Q_PER_KV)], axis=0)


def _stack_cols(x, kv):
    return jnp.concatenate([x[:, kv * Q_PER_KV + g:kv * Q_PER_KV + g + 1] for g in range(Q_PER_KV)], axis=0)


def _per_head_col(vals):
    return jnp.concatenate([jnp.full((ATT_BLOCK, 1), v, F32) for v in vals], axis=0)


def _attn_forward(q, k, v, sinks, bl, nb):
    t = q.shape[0]

    def body(sink_ref, q_ref, kp_ref, kc_ref, vp_ref, vc_ref, o_ref, lse_ref):
        i = pl.program_id(1)
        dist4, valid4 = _att_dist_mask(i == 0)
        rows2 = ATT_FWD_STACK * ATT_BLOCK
        dist, valid = dist4[0:rows2, :], valid4[0:rows2, :]
        kk = jnp.concatenate([kp_ref[...], kc_ref[...]], axis=0)
        vv = jnp.concatenate([vp_ref[...], vc_ref[...]], axis=0)
        qv = q_ref[...]
        col = lambda vals: jnp.concatenate([jnp.full((ATT_BLOCK, 1), v, F32) for v in vals], axis=0)
        stacks = [range(h0, h0 + ATT_FWD_STACK) for h0 in range(0, N_HEADS, ATT_FWD_STACK)]
        kv_cols = lambda heads: slice(heads[0] // Q_PER_KV * HEAD_DIM, (heads[0] // Q_PER_KV + 1) * HEAD_DIM)
        scores = [_mm_nt(jnp.concatenate([qv[:, h * HEAD_DIM:(h + 1) * HEAD_DIM] for h in heads], axis=0),
                         kk[:, kv_cols(heads)]) for heads in stacks]
        softmaxes = []
        for heads, qk in zip(stacks, scores):
            slope = col([2.0 ** (-(h + 1)) for h in heads])
            sink = col([sink_ref[h] for h in heads])
            s = jnp.where(valid, qk * ATT_SCALE - slope * dist, NEG_INF)
            m = jnp.maximum(jnp.max(s, axis=-1, keepdims=True), sink)
            e = jnp.exp(s - m)
            den = jnp.sum(e, axis=-1, keepdims=True) + jnp.exp(sink - m)
            softmaxes.append((e.astype(BF16), 1.0 / den, m + jnp.log(den)))
        for heads, (e, inv_den, lse) in zip(stacks, softmaxes):
            o = _mm(e, vv[:, kv_cols(heads)]) * inv_den
            for g, h in enumerate(heads):
                rows = slice(g * ATT_BLOCK, (g + 1) * ATT_BLOCK)
                o_ref[:, h * HEAD_DIM:(h + 1) * HEAD_DIM] = o[rows, :]
                lse_ref[:, h:h + 1] = lse[rows, :]

    cur = lambda w: pl.BlockSpec((ATT_BLOCK, w), lambda b, i: (b * nb + i, 0))
    prev = lambda w: pl.BlockSpec((ATT_BLOCK, w), lambda b, i: (b * nb + jnp.maximum(i - 1, 0), 0))
    return pl.pallas_call(
        body, name="attn_forward", grid=(bl, nb),
        in_specs=[pl.BlockSpec(memory_space=pltpu.SMEM), cur(512), prev(128), cur(128), prev(128), cur(128)],
        out_specs=(cur(512), cur(N_HEADS)),
        out_shape=(jax.ShapeDtypeStruct((t, D_ATTN), F32), jax.ShapeDtypeStruct((t, N_HEADS), F32)),
        compiler_params=_tc_params(("arbitrary", "arbitrary")),
    )(sinks, q, k, k, v, v)


def _attn_backward(q, k, v, o, do, lse, sinks, bl, nb):
    t = q.shape[0]

    def body(sink_ref, qc_ref, kp_ref, kc_ref, vp_ref, vc_ref, oc_ref, doc_ref, lc_ref,
             dq_ref, dk_ref, dv_ref, ds_ref, dk_carry, dv_carry):
        b, i = pl.program_id(0), pl.program_id(1)
        live = i < nb

        @pl.when(i == 0)
        def _():
            dk_carry[...] = jnp.zeros((ATT_BLOCK, KV_HEADS * HEAD_DIM), F32)
            dv_carry[...] = jnp.zeros((ATT_BLOCK, KV_HEADS * HEAD_DIM), F32)

        dist, valid = _att_dist_mask(i == 0)
        valid = valid & live
        kk = jnp.concatenate([kp_ref[...], kc_ref[...]], axis=0)
        vv = jnp.concatenate([vp_ref[...], vc_ref[...]], axis=0)
        qc, oc, doc, lc = qc_ref[...], oc_ref[...], doc_ref[...], lc_ref[...]
        dsink_cols, dq_parts, dk_t, dv_t = [], [], [], []
        for kv in range(KV_HEADS):
            heads = range(kv * Q_PER_KV, (kv + 1) * Q_PER_KV)
            cols = slice(kv * HEAD_DIM, (kv + 1) * HEAD_DIM)
            kh, vh = kk[:, cols], vv[:, cols]
            slope = _per_head_col([2.0 ** (-(h + 1)) for h in heads])
            sink = _per_head_col([sink_ref[h] for h in heads])
            q4, do4 = _stack_heads(qc, kv), _stack_heads(doc, kv)
            delta = jnp.sum(do4 * _stack_heads(oc, kv), axis=-1, keepdims=True)
            lse4 = _stack_cols(lc, kv)
            s = _mm_nt(q4, kh) * ATT_SCALE - slope * dist
            p = jnp.where(valid, jnp.exp(s - lse4), 0.0)
            dsc = p * (_mm_nt(do4, vh) - delta)
            dq4 = _mm(dsc, kh) * ATT_SCALE
            dk_t.append(_mm_tn(q4, dsc) * ATT_SCALE)
            dv_t.append(_mm_tn(do4, p))
            dsink4 = jnp.where(live, jnp.exp(sink - lse4) * delta, 0.0)
            for g, h in enumerate(heads):
                rows = slice(g * ATT_BLOCK, (g + 1) * ATT_BLOCK)
                dq_parts.append((h, dq4[rows, :]))
                dsink_cols.append(-jnp.sum(dsink4[rows, :], axis=0, keepdims=True))
        dsink = jnp.concatenate(dsink_cols, axis=1)
        for out_ref, carry, parts in ((dk_ref, dk_carry, dk_t), (dv_ref, dv_carry, dv_t)):
            both = jnp.concatenate(parts, axis=0)
            out_ref[...] = (carry[...] + both[:, 0:ATT_BLOCK]).T
            carry[...] = both[:, ATT_BLOCK:]

        @pl.when(live)
        def _():
            for h, part in dq_parts:
                dq_ref[:, h * HEAD_DIM:(h + 1) * HEAD_DIM] = part

        @pl.when((b == 0) & (i == 0))
        def _():
            ds_ref[...] = dsink

        @pl.when((b != 0) | (i != 0))
        def _():
            ds_ref[...] += dsink

    cur_i = lambda i: jnp.minimum(i, nb - 1)
    cur = lambda w: pl.BlockSpec((ATT_BLOCK, w), lambda b, i: (b * nb + cur_i(i), 0))
    prev = lambda w: pl.BlockSpec((ATT_BLOCK, w), lambda b, i: (b * nb + jnp.maximum(cur_i(i) - 1, 0), 0))
    behind = lambda w: pl.BlockSpec((ATT_BLOCK, w), lambda b, i: (b * nb + jnp.maximum(i - 1, 0), 0))
    return pl.pallas_call(
        body, name="attn_backward", grid=(bl, nb + 1),
        in_specs=[pl.BlockSpec(memory_space=pltpu.SMEM), cur(512), prev(128), cur(128), prev(128), cur(128),
                  cur(512), cur(512), cur(N_HEADS)],
        out_specs=(cur(512), behind(128), behind(128), pl.BlockSpec((1, N_HEADS), lambda b, i: (0, 0))),
        out_shape=(jax.ShapeDtypeStruct((t, D_ATTN), F32), jax.ShapeDtypeStruct((t, 128), F32),
                   jax.ShapeDtypeStruct((t, 128), F32), jax.ShapeDtypeStruct((1, N_HEADS), F32)),
        scratch_shapes=[pltpu.VMEM((ATT_BLOCK, KV_HEADS * HEAD_DIM), F32), pltpu.VMEM((ATT_BLOCK, KV_HEADS * HEAD_DIM), F32)],
        compiler_params=_tc_params(("arbitrary", "arbitrary")),
    )(sinks, q, k, k, v, v, o, do, lse)


def _mix_forward_backward(x2, y2, z_ssm, attn, z_attn, p2, target2, w_glu, b_glu, w_out, g_post, w_gate, b_gate,
                          w_proj, tm):
    t = x2.shape[0]
    proj_cols = D_MODEL // N_DEV

    def body(x_ref, y_ref, zs_ref, at_ref, za_ref, p_ref, tg_ref,
             wglu_ref, bglu_ref, wout_ref, gpost_ref, wgate_ref, bgate_ref, wproj_ref,
             loss_ref, dh1_ref, dy_ref, dzs_ref, dat_ref, dza_ref,
             dwglu_own_ref, dbglu_ref, dwout_own_ref, dgpost_ref, dwgate_own_ref, dbgate_ref, dwproj_own_ref,
             dwout16_ref, dwgate16_ref, dwproj16_ref, dwglu16_ref, dwglu_ref, dwout_ref, dwgate_ref, dwproj_ref):
        i = pl.program_id(0)
        gpost = gpost_ref[...]

        @pl.when(i == 0)
        def _():
            for ref in (dwglu_ref, dbglu_ref, dwout_ref, dgpost_ref, dwgate_ref, dbgate_ref, dwproj_ref, loss_ref):
                ref[...] = jnp.zeros(ref.shape, F32)

        def chain(rows):
            y = y_ref[rows, :]
            u3 = GELU_C * (y + GELU_K * y * y * y)
            th = jnp.tanh(u3)
            gl = 0.5 * y * (1.0 + th)
            a = _mm(gl, wglu_ref[...]) + bglu_ref[...]
            sa = _sigmoid(a)
            glu = gl * sa
            zs = zs_ref[rows, :]
            sgs = _sigmoid(zs)
            ssm_out = glu * (zs * sgs)
            za = za_ref[rows, :]
            sga = _sigmoid(za)
            at = at_ref[rows, :]
            attn_out = at * (za * sga)
            cat = jnp.concatenate([ssm_out, attn_out], axis=-1).astype(BF16)
            mixed = _mm(cat, wout_ref[...])
            r2 = lax.rsqrt(jnp.mean(mixed * mixed, axis=-1, keepdims=True) + EPS)
            nhat = mixed * r2
            h1 = x_ref[rows, :] + nhat * gpost
            gate = _sigmoid(_mm(h1, wgate_ref[...]) + bgate_ref[...])
            pv = p_ref[rows, :]
            pp = _mm(pv, jnp.concatenate([wproj_ref[d] for d in range(N_DEV)], axis=1))
            h2 = h1 + gate * pp
            err = h2 - tg_ref[rows, :]
            loss_part = jnp.sum(jnp.sum(err * err, axis=-1, keepdims=True), axis=0, keepdims=True) * (0.5 / D_MODEL)
            dh2 = err * (1.0 / D_MODEL)
            dgp = dh2 * pp * gate * (1.0 - gate)
            dpp = dh2 * gate
            dh1 = dh2 + _mm_nt(dgp, wgate_ref[...])
            dwproj_ref[...] += _mm_tn(pv, dpp)
            dwgate_ref[...] += _mm_tn(h1, dgp)
            dh1_ref[rows, :] = dh1
            dnhat = dh1 * gpost
            dmixed = r2 * (dnhat - nhat * jnp.mean(dnhat * nhat, axis=-1, keepdims=True))
            dcat = _mm_nt(dmixed, wout_ref[...])
            dwout_ref[...] += _mm_tn(cat, dmixed)
            dso, dao = dcat[:, 0:D_SSM], dcat[:, D_SSM:]
            dat_ref[rows, :] = dao * (za * sga)
            dza_ref[rows, :] = (dao * at * (sga * (1.0 + za * (1.0 - sga)))).astype(BF16)
            dzs_ref[rows, :] = (dso * glu * (sgs * (1.0 + zs * (1.0 - sgs)))).astype(BF16)
            dglu = dso * (zs * sgs)
            da = dglu * gl * sa * (1.0 - sa)
            dgl = dglu * sa + _mm_nt(da, wglu_ref[...])
            dwglu_ref[...] += _mm_tn(gl, da)
            dgelu = 0.5 * (1.0 + th) + 0.5 * y * (1.0 - th * th) * (GELU_C * (1.0 + 3.0 * GELU_K * y * y))
            dy_ref[rows, :] = dgl * dgelu
            dbglu_ref[...] += jnp.sum(da, axis=0, keepdims=True)
            dgpost_ref[...] += jnp.sum(dh1 * nhat, axis=0, keepdims=True)
            dbgate_ref[...] += jnp.sum(dgp, axis=0, keepdims=True)
            loss_ref[...] += loss_part

        chain(slice(None))

        @pl.when(i == t // tm - 1)
        def _():
            for ref16, ref in ((dwout16_ref, dwout_ref), (dwgate16_ref, dwgate_ref), (dwglu16_ref, dwglu_ref)):
                def to16(r, ref16=ref16, ref=ref):
                    ref16[r, :] = ref[r, :].astype(BF16)

                _row_chunks(ref.shape[0], to16)
            for ref, own_ref in ((dwglu_ref, dwglu_own_ref), (dwout_ref, dwout_own_ref), (dwgate_ref, dwgate_own_ref)):
                _copy_owned_rows(ref, own_ref)
            me = _slot(*_mesh_pos())
            for d in range(N_DEV):
                cols = slice(d * proj_cols, (d + 1) * proj_cols)
                dwproj16_ref[d] = dwproj_ref[:, cols].astype(BF16)

                @pl.when(me == d)
                def _(cols=cols):
                    dwproj_own_ref[0] = dwproj_ref[:, cols]

    row = lambda w: pl.BlockSpec((tm, w), lambda i: (i, 0))
    acc = lambda *shape, dt=F32: (_const_spec(shape), jax.ShapeDtypeStruct(shape, dt))
    accs = [acc(1, D_SSM // N_DEV, D_SSM), acc(1, D_SSM), acc(1, D_MODEL // N_DEV, D_MODEL), acc(1, D_MODEL),
            acc(1, D_MODEL // N_DEV, D_MODEL), acc(1, D_MODEL), acc(1, D_PLE, proj_cols),
            acc(D_MODEL, D_MODEL, dt=BF16), acc(D_MODEL, D_MODEL, dt=BF16), acc(N_DEV, D_PLE, proj_cols, dt=BF16),
            acc(D_SSM, D_SSM, dt=BF16)]
    return pl.pallas_call(
        body, name="mix_forward_backward", grid=(t // tm,),
        in_specs=[row(D_MODEL), row(512), row(512), row(512), row(512), row(D_PLE), row(D_MODEL),
                  _const_spec((D_SSM, D_SSM)), _const_spec((1, D_SSM)), _const_spec((D_MODEL, D_MODEL)),
                  _const_spec((1, D_MODEL)), _const_spec((D_MODEL, D_MODEL)), _const_spec((1, D_MODEL)),
                  _const_spec((N_DEV, D_PLE, proj_cols))],
        out_specs=(_const_spec((1, 1)), row(D_MODEL), row(512), row(512), row(512), row(512))
        + tuple(a[0] for a in accs),
        out_shape=(jax.ShapeDtypeStruct((1, 1), F32), jax.ShapeDtypeStruct((t, D_MODEL), F32),
                   jax.ShapeDtypeStruct((t, 512), F32),
                   jax.ShapeDtypeStruct((t, 512), BF16), jax.ShapeDtypeStruct((t, 512), F32),
                   jax.ShapeDtypeStruct((t, 512), BF16)) + tuple(a[1] for a in accs),
        scratch_shapes=[pltpu.VMEM((D_SSM, D_SSM), F32), pltpu.VMEM((D_MODEL, D_MODEL), F32),
                        pltpu.VMEM((D_MODEL, D_MODEL), F32), pltpu.VMEM((D_PLE, D_MODEL), F32)],
        compiler_params=_tc_params(("arbitrary",)),
    )(x2, y2, z_ssm, attn, z_attn, p2, target2, w_glu, b_glu, w_out, g_post, w_gate, b_gate, w_proj)


def _in_backward(x2, dh1, du, dz_ssm, dq, dk, dv, dz_attn, g_pre, w_in, tm):
    t = x2.shape[0]

    def body(x_ref, dh1_ref, du_ref, dzs_ref, dq_ref, dk_ref, dv_ref, dza_ref, g_ref, w_ref,
             gx_ref, dw_own_ref, dg_ref, dw16_ref, dw_ref):
        i = pl.program_id(0)

        @pl.when(i == 0)
        def _():
            dw_ref[...] = jnp.zeros((D_IN, D_MODEL), F32)
            dg_ref[...] = jnp.zeros((1, D_MODEL), F32)

        xv = x_ref[...]
        r = lax.rsqrt(jnp.mean(xv * xv, axis=-1, keepdims=True) + EPS)
        xhat = xv * r
        g = g_ref[...]
        hn = (xhat * g).astype(BF16)
        dproj = jnp.concatenate([du_ref[...].astype(BF16), dzs_ref[...].astype(BF16), dq_ref[...].astype(BF16),
                                 dk_ref[...].astype(BF16), dv_ref[...].astype(BF16), dza_ref[...].astype(BF16)],
                                axis=-1)
        dhn = _mm(dproj, w_ref[...])
        dxhat = dhn * g
        gx_ref[...] = dh1_ref[...] + r * (dxhat - xhat * jnp.mean(dxhat * xhat, axis=-1, keepdims=True))
        dw_ref[...] += _mm_tn(dproj, hn)
        dg_ref[...] += jnp.sum(dhn * xhat, axis=0, keepdims=True)

        @pl.when(i == t // tm - 1)
        def _():
            def to16(r):
                dw16_ref[r, :] = dw_ref[r, :].astype(BF16)

            _row_chunks(D_IN, to16)
            _copy_owned_rows(dw_ref, dw_own_ref)

    row = lambda w: pl.BlockSpec((tm, w), lambda i: (i, 0))
    own = (1, D_IN // N_DEV, D_MODEL)
    return pl.pallas_call(
        body, name="in_backward", grid=(t // tm,),
        in_specs=[row(D_MODEL), row(D_MODEL), row(512), row(512), row(512), row(128), row(128), row(512),
                  _const_spec((1, D_MODEL)), _const_spec((D_IN, D_MODEL))],
        out_specs=(row(D_MODEL), _const_spec(own), _const_spec((1, D_MODEL)), _const_spec((D_IN, D_MODEL))),
        out_shape=(jax.ShapeDtypeStruct((t, D_MODEL), F32), jax.ShapeDtypeStruct(own, F32),
                   jax.ShapeDtypeStruct((1, D_MODEL), F32), jax.ShapeDtypeStruct((D_IN, D_MODEL), BF16)),
        scratch_shapes=[pltpu.VMEM((D_IN, D_MODEL), F32)],
        compiler_params=_tc_params(("arbitrary",)),
    )(x2, dh1, du, dz_ssm, dq, dk, dv, dz_attn, g_pre, w_in)


def _local_step(x, p, target, pre_norm_g, w_in, prep, ssm_lam_re, ssm_lam_im, ssm_log_step, ssm_b_re, ssm_b_im, ssm_d,
                ssm_b_glu, attn_sinks, post_norm_g, pl_b_gate, late):
    bl, seq, _ = x.shape
    seg = seq // N_SEG
    nb = seq // ATT_BLOCK
    t = bl * seq
    x2 = x.reshape(t, D_MODEL)
    p2 = p.reshape(t, D_PLE)
    tg2 = target.reshape(t, D_MODEL)

    lam_re, lam_im = ssm_lam_re, ssm_lam_im
    log_step = ssm_log_step.reshape(SSM_G, 1)
    a_re_row, a_im_row, pw_re, pw_im, bcat, bcat_t, ccat, ccat_t = prep
    d_row = ssm_d.reshape(1, D_SSM)

    segments = lambda a: a.reshape(bl, N_SEG, seg, D_SSM)
    u, z_ssm, q, k, v, z_attn = _in_proj(x2, pre_norm_g.reshape(1, D_MODEL), w_in, min(TOKEN_TILE_WIDE, t))
    (y, states, carries), gathered = _ssm_forward(
        segments(u), bcat, ccat, a_re_row, a_im_row, pw_re, pw_im, d_row, late, seg)
    w_out, w_gate, w_proj, w_glu = (g if n in COL_SHARDED else _gathered_to_full(n, g)
                                    for n, g in zip(LATE_NAMES, gathered))
    sinks = attn_sinks.reshape(N_HEADS)
    attn, lse = _attn_forward(q, k, v, sinks, bl, nb)
    (loss, dh1, dy, dz_ssm, dattn, dz_attn, d_w_glu, d_b_glu, d_w_out, d_g_post, d_w_gate, d_b_gate,
     d_w_proj, *late16) = _mix_forward_backward(
        x2, y.reshape(t, D_SSM), z_ssm, attn, z_attn, p2, tg2, w_glu,
        ssm_b_glu.reshape(1, D_SSM), w_out, post_norm_g.reshape(1, D_MODEL), w_gate, pl_b_gate.reshape(1, D_MODEL),
        w_proj, min(TOKEN_TILE, t))
    owned = lambda ds: [d if n in COL_SHARDED else _full_to_owned(n, d) for n, d in zip(LATE_NAMES, ds)]
    dq, dk, dv, d_sinks = _attn_backward(q, k, v, attn, dattn, lse, sinks, bl, nb)
    (du, d_bcat, d_ccat_t, da_re, da_im, d_d), late_grads = _ssm_backward(
        segments(u), segments(dy), states, carries, bcat_t, ccat_t, a_re_row, a_im_row, pw_re, pw_im,
        d_row, owned(late16), [d_w_out, d_w_gate, d_w_proj, d_w_glu], seg)
    grad_x, d_w_in, d_g_pre, d_w_in16 = _in_backward(
        x2, dh1, du.reshape(t, D_SSM), dz_ssm, dq, dk, dv, dz_attn, pre_norm_g.reshape(1, D_MODEL), w_in,
        min(TOKEN_TILE_WIDE, t))
    d_lam_re, d_lam_im, d_ls, d_b_re, d_b_im, d_c_re, d_c_im = _ssm_param_grads(
        lam_re, lam_im, log_step, ssm_b_re, ssm_b_im, da_re.reshape(SSM_G, SSM_N), da_im.reshape(SSM_G, SSM_N),
        d_bcat, d_ccat_t)
    grads = {
        "pre_norm_g": d_g_pre, "w_in": d_w_in, "w_in16": d_w_in16, "ssm_lam_re": d_lam_re, "ssm_lam_im": d_lam_im,
        "ssm_log_step": d_ls, "ssm_b_re": d_b_re, "ssm_b_im": d_b_im, "ssm_c_re": d_c_re, "ssm_c_im": d_c_im,
        "ssm_d": d_d, "ssm_b_glu": d_b_glu, "attn_sinks": d_sinks, "post_norm_g": d_g_post, "pl_b_gate": d_b_gate,
    }
    return loss, grad_x.reshape(bl, seq, D_MODEL), grads, late_grads


LATE_NAMES = ("w_out", "pl_w_gate", "pl_w_proj", "ssm_w_glu")
BIG_NAMES = ("w_in",) + LATE_NAMES
COL_SHARDED = {"w_in": D_IN // N_DEV, "pl_w_proj": D_MODEL // N_DEV}
WEIGHT_NAMES = ("pre_norm_g", "w_in", "ssm_lam_re", "ssm_lam_im", "ssm_log_step", "ssm_b_re", "ssm_b_im", "ssm_c_re",
                "ssm_c_im", "ssm_d", "ssm_w_glu", "ssm_b_glu", "attn_sinks", "w_out", "post_norm_g", "pl_w_proj",
                "pl_w_gate", "pl_b_gate")


TRANSPOSED = {"w_in": (0, 1), "ssm_b_re": (1, 2), "ssm_b_im": (1, 2)}


def _kernel_form(name, a):
    a = a[0]
    if name in TRANSPOSED:
        a = jnp.swapaxes(a, *TRANSPOSED[name])
    if name in ("ssm_b_re", "ssm_b_im", "ssm_c_re", "ssm_c_im"):
        a = a.reshape(SSM_G * SSM_P, SSM_N)
    return a


def _given_form(name, a, shape):
    if name in TRANSPOSED:
        i, j = TRANSPOSED[name]
        swapped = list(shape[1:])
        swapped[i], swapped[j] = swapped[j], swapped[i]
        return jnp.swapaxes(a.reshape(swapped), i, j).reshape(shape)
    return a.reshape(shape)


def _gathered_to_full(name, g):
    _, rows, cols = g.shape
    if name in COL_SHARDED:
        return jnp.swapaxes(g, 0, 1).reshape(rows, N_DEV * cols)
    return g.reshape(N_DEV * rows, cols)


def _full_to_owned(name, full):
    if name in COL_SHARDED:
        return jnp.swapaxes(full.reshape(full.shape[0], N_DEV, COL_SHARDED[name]), 0, 1)
    return full.reshape(N_DEV, full.shape[0] // N_DEV, full.shape[1])


def kernel(x, p, pre_norm_g, w_in, ssm_lam_re, ssm_lam_im, ssm_log_step, ssm_b_re, ssm_b_im, ssm_c_re, ssm_c_im, ssm_d, ssm_w_glu, ssm_b_glu, attn_sinks, w_out, post_norm_g, pl_w_proj, pl_w_gate, pl_b_gate, loss_target, m_pre_norm_g, m_w_in, m_ssm_lam_re, m_ssm_lam_im, m_ssm_log_step, m_ssm_b_re, m_ssm_b_im, m_ssm_c_re, m_ssm_c_im, m_ssm_d, m_ssm_w_glu, m_ssm_b_glu, m_attn_sinks, m_w_out, m_post_norm_g, m_pl_w_proj, m_pl_w_gate, m_pl_b_gate, v_pre_norm_g, v_w_in, v_ssm_lam_re, v_ssm_lam_im, v_ssm_log_step, v_ssm_b_re, v_ssm_b_im, v_ssm_c_re, v_ssm_c_im, v_ssm_d, v_ssm_w_glu, v_ssm_b_glu, v_attn_sinks, v_w_out, v_post_norm_g, v_pl_w_proj, v_pl_w_gate, v_pl_b_gate):
    w = dict(pre_norm_g=pre_norm_g, w_in=w_in, ssm_lam_re=ssm_lam_re, ssm_lam_im=ssm_lam_im, ssm_log_step=ssm_log_step,
             ssm_b_re=ssm_b_re, ssm_b_im=ssm_b_im, ssm_c_re=ssm_c_re, ssm_c_im=ssm_c_im, ssm_d=ssm_d, ssm_w_glu=ssm_w_glu,
             ssm_b_glu=ssm_b_glu, attn_sinks=attn_sinks, w_out=w_out, post_norm_g=post_norm_g, pl_w_proj=pl_w_proj,
             pl_w_gate=pl_w_gate, pl_b_gate=pl_b_gate)
    m = dict(pre_norm_g=m_pre_norm_g, w_in=m_w_in, ssm_lam_re=m_ssm_lam_re, ssm_lam_im=m_ssm_lam_im,
             ssm_log_step=m_ssm_log_step, ssm_b_re=m_ssm_b_re, ssm_b_im=m_ssm_b_im, ssm_c_re=m_ssm_c_re,
             ssm_c_im=m_ssm_c_im, ssm_d=m_ssm_d, ssm_w_glu=m_ssm_w_glu, ssm_b_glu=m_ssm_b_glu, attn_sinks=m_attn_sinks,
             w_out=m_w_out, post_norm_g=m_post_norm_g, pl_w_proj=m_pl_w_proj, pl_w_gate=m_pl_w_gate,
             pl_b_gate=m_pl_b_gate)
    v = dict(pre_norm_g=v_pre_norm_g, w_in=v_w_in, ssm_lam_re=v_ssm_lam_re, ssm_lam_im=v_ssm_lam_im,
             ssm_log_step=v_ssm_log_step, ssm_b_re=v_ssm_b_re, ssm_b_im=v_ssm_b_im, ssm_c_re=v_ssm_c_re,
             ssm_c_im=v_ssm_c_im, ssm_d=v_ssm_d, ssm_w_glu=v_ssm_w_glu, ssm_b_glu=v_ssm_b_glu, attn_sinks=v_attn_sinks,
             w_out=v_w_out, post_norm_g=v_post_norm_g, pl_w_proj=v_pl_w_proj, pl_w_gate=v_pl_w_gate,
             pl_b_gate=v_pl_b_gate)
    kf = lambda d: {n: _kernel_form(n, a) for n, a in d.items()}
    wk, mk, vk = kf(w), kf(m), kf(v)

    (gathered,), prep = _allgather_weights([wk["w_in"]], *_ssm_prep(
        wk["ssm_lam_re"], wk["ssm_lam_im"], wk["ssm_log_step"].reshape(SSM_G, 1), wk["ssm_b_re"], wk["ssm_b_im"],
        wk["ssm_c_re"], wk["ssm_c_im"], x.shape[1] // N_SEG))
    loss, grad_x, grads, g_late = _local_step(
        x, p[0], loss_target, wk["pre_norm_g"], gathered.reshape(D_IN, D_MODEL), prep, wk["ssm_lam_re"],
        wk["ssm_lam_im"], wk["ssm_log_step"], wk["ssm_b_re"], wk["ssm_b_im"], wk["ssm_d"],
        wk["ssm_b_glu"], wk["attn_sinks"], wk["post_norm_g"], wk["pl_b_gate"], [wk[n] for n in LATE_NAMES])

    owned = lambda g: g.reshape(N_DEV, D_IN // N_DEV, D_MODEL)
    tiny_form = lambda d: [d[n].reshape(rows, cols) for n, rows, cols in TINY]
    med_form = lambda d: [d[n].reshape(N_DEV, rows // N_DEV, cols) for n, rows, cols in MEDIUM]
    g_big, loss, g_tiny, g_med = _reduce_final(
        [owned(grads["w_in16"])], [grads["w_in"]], loss, tiny_form(grads), med_form(grads))
    names = BIG_NAMES + tuple(n for n, _, _ in TINY + MEDIUM)
    form = lambda d: [d[n] for n in BIG_NAMES] + tiny_form(d) + med_form(d)
    updated = _adamw_update(g_big + g_late + g_tiny + g_med, form(wk), form(mk), form(vk), len(BIG_NAMES))
    vals = dict(zip(names, updated))
    results = [[_given_form(n, vals[n][kind], w[n].shape) for n in WEIGHT_NAMES] for kind in range(4)]
    return (loss.reshape(()), grad_x, *results[0], *results[1], *results[2], *results[3])
```
